```python
import math
import jax, jax.numpy as jnp
from jax import lax
import numpy as np

D_MODEL = 2048
BATCH = 8
SEQ = 2048
DEPTH = 1

MEM_LEN = 256
D_RNN = 1024
RNN_BLOCKS = 8
RNN_BLOCK = D_RNN // RNN_BLOCKS
CONV_W = 4
LRU_C = 8.0
SWA_HEADS = 16
SWA_KV_HEADS = 2
SWA_GROUP = SWA_HEADS // SWA_KV_HEADS
SWA_HD = 64
D_SWA = SWA_HEADS * SWA_HD
D_SWA_KV = SWA_KV_HEADS * SWA_HD
WINDOW = 128
BLOCK = WINDOW
MEM_HEADS = 4
MEM_HD = 256
D_MEM = MEM_HEADS * MEM_HD
REL_BUCKETS = 32
REL_MAX_DIST = 128
N_BRANCH = 3
EPS = 1e-6
NEG_INF = -1e30

IN_SPLITS = (D_RNN, D_RNN, D_SWA, D_SWA_KV, D_SWA_KV, D_SWA, D_MEM, D_MEM, N_BRANCH * D_MODEL)
D_IN = 2 * D_RNN + 2 * D_SWA + 2 * D_SWA_KV + 2 * D_MEM + N_BRANCH * D_MODEL

kernel_name = "hybrid_rglru_swa_sink_memxattn_gated"


def rmsnorm(x, g):
    xf = x.astype(jnp.float32)
    y = xf * lax.rsqrt(jnp.mean(xf * xf, axis=-1, keepdims=True) + EPS)
    return (y * g.astype(jnp.float32)).astype(x.dtype)


def rel_bucket(dist):
    n = jnp.maximum(dist, 0)
    max_exact = REL_BUCKETS // 2
    ratio = jnp.log(jnp.maximum(n, 1).astype(jnp.float32) / max_exact) / math.log(REL_MAX_DIST / max_exact)
    large = max_exact + (ratio * (REL_BUCKETS - max_exact)).astype(jnp.int32)
    large = jnp.minimum(large, REL_BUCKETS - 1)
    return jnp.where(n < max_exact, n, large)


def rglru_branch(xr, conv_w, conv_b, w_a, b_a, w_x, b_x, lam):
    B, S, _ = xr.shape
    xp = jnp.pad(xr, ((0, 0), (CONV_W - 1, 0), (0, 0)))
    conv = conv_b + sum(xp[:, CONV_W - 1 - k: CONV_W - 1 - k + S] * conv_w[k] for k in range(CONV_W))
    cb = conv.reshape(B, S, RNN_BLOCKS, RNN_BLOCK)
    gate_r = jax.nn.sigmoid((jnp.einsum('bsni,nij->bsnj', cb, w_a).reshape(B, S, D_RNN) + b_a).astype(jnp.float32))
    gate_i = jax.nn.sigmoid((jnp.einsum('bsni,nij->bsnj', cb, w_x).reshape(B, S, D_RNN) + b_x).astype(jnp.float32))
    log_a = -LRU_C * gate_r * jax.nn.softplus(-lam.astype(jnp.float32))
    a = jnp.exp(log_a)
    mult = jnp.sqrt(-jnp.expm1(2.0 * log_a))
    is_start = (jnp.arange(S) == 0)[None, :, None]
    mult = jnp.where(is_start, 1.0, mult)
    b = mult * gate_i * conv.astype(jnp.float32)

    def combine(e1, e2):
        a1, b1 = e1
        a2, b2 = e2
        return a1 * a2, a2 * b1 + b2

    _, h = lax.associative_scan(combine, (a, b), axis=1)
    return h.astype(xr.dtype)


def swa_branch(q, k, v, sinks, rel_bias):
    B, S, _ = q.shape
    nb = S // BLOCK
    q = q.reshape(B, nb, BLOCK, SWA_KV_HEADS, SWA_GROUP, SWA_HD)
    k = k.reshape(B, nb, BLOCK, SWA_KV_HEADS, SWA_HD)
    v = v.reshape(B, nb, BLOCK, SWA_KV_HEADS, SWA_HD)

    def with_prev(t):
        prev = jnp.concatenate([jnp.zeros_like(t[:, :1]), t[:, :-1]], axis=1)
        return jnp.concatenate([prev, t], axis=2)

    kk = with_prev(k)
    vv = with_prev(v)
    logits = jnp.einsum('bnqhgd,bnkhd->bnhgqk', q, kk).astype(jnp.float32) * (SWA_HD ** -0.5)

    qi = jnp.arange(BLOCK)[:, None]
    kj = jnp.arange(2 * BLOCK)[None, :]
    dist = qi + BLOCK - kj
    in_window = (dist >= 0) & (dist < WINDOW)
    key_abs = jnp.arange(nb)[:, None, None] * BLOCK + kj[None] - BLOCK
    valid = in_window[None] & (key_abs >= 0)

    bias = rel_bias.astype(jnp.float32)[rel_bucket(dist)]
    bias = jnp.transpose(bias, (2, 0, 1)).reshape(SWA_KV_HEADS, SWA_GROUP, BLOCK, 2 * BLOCK)
    logits = logits + bias[None, None]
    logits = jnp.where(valid[None, :, None, None], logits, NEG_INF)

    sink = sinks.astype(jnp.float32).reshape(SWA_KV_HEADS, SWA_GROUP)[None, None, :, :, None, None]
    m = jnp.maximum(jnp.max(logits, axis=-1, keepdims=True), sink)
    p = jnp.exp(logits - m)
    denom = jnp.sum(p, axis=-1, keepdims=True) + jnp.exp(sink - m)
    probs = (p / denom).astype(v.dtype)
    out = jnp.einsum('bnhgqk,bnkhd->bnqhgd', probs, vv)
    return out.reshape(B, S, D_SWA)


def mem_branch(q, mk, mv):
    B, S, _ = q.shape
    M = mk.shape[1]
    q = q.reshape(B, S, MEM_HEADS, MEM_HD)
    mk = mk.reshape(B, M, MEM_HEADS, MEM_HD)
    mv = mv.reshape(B, M, MEM_HEADS, MEM_HD)
    logits = jnp.einsum('bshd,bmhd->bhsm', q, mk).astype(jnp.float32) * (MEM_HD ** -0.5)
    probs = jax.nn.softmax(logits, axis=-1).astype(mv.dtype)
    out = jnp.einsum('bhsm,bmhd->bshd', probs, mv)
    return out.reshape(B, S, D_MEM)


def _fwd_setup_inputs(seed: int = 0) -> dict:
    key = jax.random.key(seed)
    ks = jax.random.split(key, 24)
    f32 = jnp.float32
    nrm = lambda k, shape, s: jax.random.normal(k, shape, f32) * s
    L = DEPTH
    u = jax.random.uniform(ks[12], (L, D_RNN), f32, 0.9, 0.999)
    a0 = u ** (1.0 / LRU_C)
    lru_lambda = jnp.log(a0) - jnp.log1p(-a0)
    return {
        "x": nrm(ks[0], (BATCH, SEQ, D_MODEL), 1.0),
        "mem": nrm(ks[1], (BATCH, MEM_LEN, D_MODEL), 1.0),
        "pre_norm_g": 1.0 + nrm(ks[2], (L, D_MODEL), 0.02),
        "post_norm_g": 1.0 + nrm(ks[3], (L, D_MODEL), 0.02),
        "mem_norm_g": 1.0 + nrm(ks[4], (L, D_MODEL), 0.02),
        "w_in": nrm(ks[5], (L, D_MODEL, D_IN), D_MODEL ** -0.5),
        "conv_w": nrm(ks[6], (L, CONV_W, D_RNN), CONV_W ** -0.5),
        "conv_b": nrm(ks[7], (L, D_RNN), 0.02),
        "w_rg_a": nrm(ks[8], (L, RNN_BLOCKS, RNN_BLOCK, RNN_BLOCK), RNN_BLOCK ** -0.5),
        "b_rg_a": nrm(ks[9], (L, D_RNN), 0.02),
        "w_rg_x": nrm(ks[10], (L, RNN_BLOCKS, RNN_BLOCK, RNN_BLOCK), RNN_BLOCK ** -0.5),
        "b_rg_x": nrm(ks[11], (L, D_RNN), 0.02),
        "lru_lambda": lru_lambda,
        "swa_sinks": nrm(ks[13], (L, SWA_HEADS), 0.5),
        "rel_bias": nrm(ks[14], (REL_BUCKETS, SWA_HEADS), 0.5),
        "w_mem_kv": nrm(ks[15], (L, D_MODEL, 2 * D_MEM), D_MODEL ** -0.5),
        "w_br_rg": nrm(ks[16], (L, D_RNN, D_MODEL), D_RNN ** -0.5),
        "w_br_swa": nrm(ks[17], (L, D_SWA, D_MODEL), D_SWA ** -0.5),
        "w_br_mem": nrm(ks[18], (L, D_MEM, D_MODEL), D_MEM ** -0.5),
        "w_out": nrm(ks[19], (L, D_MODEL, D_MODEL), D_MODEL ** -0.5),
    }


def _fwd_reference(x, mem, pre_norm_g, post_norm_g, mem_norm_g, w_in, conv_w, conv_b, w_rg_a, b_rg_a,
              w_rg_x, b_rg_x, lru_lambda, swa_sinks, rel_bias, w_mem_kv, w_br_rg, w_br_swa,
              w_br_mem, w_out):
    B, S, D = x.shape
    split_at = np.cumsum(IN_SPLITS)[:-1].tolist()
    for l in range(DEPTH):
        h = rmsnorm(x, pre_norm_g[l])
        proj = jnp.einsum('bsd,de->bse', h, w_in[l])
        (xr, g_rg, q_s, k_s, v_s, g_swa, q_m, g_mem, gate_logits) = jnp.split(proj, split_at, axis=-1)

        y_rg = rglru_branch(xr, conv_w[l], conv_b[l], w_rg_a[l], b_rg_a[l], w_rg_x[l], b_rg_x[l],
                            lru_lambda[l]) * jax.nn.silu(g_rg)
        y_swa = swa_branch(q_s, k_s, v_s, swa_sinks[l], rel_bias) * jax.nn.silu(g_swa)
        memn = rmsnorm(mem, mem_norm_g[l])
        mkv = jnp.einsum('bmd,de->bme', memn, w_mem_kv[l])
        mk, mv = jnp.split(mkv, 2, axis=-1)
        y_mem = mem_branch(q_m, mk, mv) * jax.nn.silu(g_mem)

        gates = jax.nn.sigmoid(gate_logits.astype(jnp.float32)).astype(x.dtype).reshape(B, S, N_BRANCH, D)
        merged = (gates[:, :, 0] * jnp.einsum('bsr,rd->bsd', y_rg, w_br_rg[l])
                  + gates[:, :, 1] * jnp.einsum('bsr,rd->bsd', y_swa, w_br_swa[l])
                  + gates[:, :, 2] * jnp.einsum('bsr,rd->bsd', y_mem, w_br_mem[l]))
        out = jnp.einsum('bsd,de->bse', merged, w_out[l])
        x = x + rmsnorm(out, post_norm_g[l])
    return x


import jax as _jax
import jax.numpy as _jnp

TWIN_FORMAT = 'train_step'
FWD_PARAMS = ['x', 'mem', 'pre_norm_g', 'post_norm_g', 'mem_norm_g', 'w_in', 'conv_w', 'conv_b', 'w_rg_a', 'b_rg_a', 'w_rg_x', 'b_rg_x', 'lru_lambda', 'swa_sinks', 'rel_bias', 'w_mem_kv', 'w_br_rg', 'w_br_swa', 'w_br_mem', 'w_out']
TWIN_WEIGHTS = ['pre_norm_g', 'post_norm_g', 'mem_norm_g', 'w_in', 'conv_w', 'conv_b', 'w_rg_a', 'b_rg_a', 'w_rg_x', 'b_rg_x', 'lru_lambda', 'swa_sinks', 'rel_bias', 'w_mem_kv', 'w_br_rg', 'w_br_swa', 'w_br_mem', 'w_out']
TWIN_DIFF_INPUT = 'x'
TWIN_INPUTS = ['x', 'mem', 'pre_norm_g', 'post_norm_g', 'mem_norm_g', 'w_in', 'conv_w', 'conv_b', 'w_rg_a', 'b_rg_a', 'w_rg_x', 'b_rg_x', 'lru_lambda', 'swa_sinks', 'rel_bias', 'w_mem_kv', 'w_br_rg', 'w_br_swa', 'w_br_mem', 'w_out', 'loss_target', 'm_pre_norm_g', 'm_post_norm_g', 'm_mem_norm_g', 'm_w_in', 'm_conv_w', 'm_conv_b', 'm_w_rg_a', 'm_b_rg_a', 'm_w_rg_x', 'm_b_rg_x', 'm_lru_lambda', 'm_swa_sinks', 'm_rel_bias', 'm_w_mem_kv', 'm_w_br_rg', 'm_w_br_swa', 'm_w_br_mem', 'm_w_out', 'v_pre_norm_g', 'v_post_norm_g', 'v_mem_norm_g', 'v_w_in', 'v_conv_w', 'v_conv_b', 'v_w_rg_a', 'v_b_rg_a', 'v_w_rg_x', 'v_b_rg_x', 'v_lru_lambda', 'v_swa_sinks', 'v_rel_bias', 'v_w_mem_kv', 'v_w_br_rg', 'v_w_br_swa', 'v_w_br_mem', 'v_w_out']
TWIN_OUTPUTS = ['loss', 'grad_x', 'grad_pre_norm_g', 'grad_post_norm_g', 'grad_mem_norm_g', 'grad_w_in', 'grad_conv_w', 'grad_conv_b', 'grad_w_rg_a', 'grad_b_rg_a', 'grad_w_rg_x', 'grad_b_rg_x', 'grad_lru_lambda', 'grad_swa_sinks', 'grad_rel_bias', 'grad_w_mem_kv', 'grad_w_br_rg', 'grad_w_br_swa', 'grad_w_br_mem', 'grad_w_out', 'delta_pre_norm_g', 'delta_post_norm_g', 'delta_mem_norm_g', 'delta_w_in', 'delta_conv_w', 'delta_conv_b', 'delta_w_rg_a', 'delta_b_rg_a', 'delta_w_rg_x', 'delta_b_rg_x', 'delta_lru_lambda', 'delta_swa_sinks', 'delta_rel_bias', 'delta_w_mem_kv', 'delta_w_br_rg', 'delta_w_br_swa', 'delta_w_br_mem', 'delta_w_out', 'new_m_pre_norm_g', 'new_m_post_norm_g', 'new_m_mem_norm_g', 'new_m_w_in', 'new_m_conv_w', 'new_m_conv_b', 'new_m_w_rg_a', 'new_m_b_rg_a', 'new_m_w_rg_x', 'new_m_b_rg_x', 'new_m_lru_lambda', 'new_m_swa_sinks', 'new_m_rel_bias', 'new_m_w_mem_kv', 'new_m_w_br_rg', 'new_m_w_br_swa', 'new_m_w_br_mem', 'new_m_w_out', 'new_v_pre_norm_g', 'new_v_post_norm_g', 'new_v_mem_norm_g', 'new_v_w_in', 'new_v_conv_w', 'new_v_conv_b', 'new_v_w_rg_a', 'new_v_b_rg_a', 'new_v_w_rg_x', 'new_v_b_rg_x', 'new_v_lru_lambda', 'new_v_swa_sinks', 'new_v_rel_bias', 'new_v_w_mem_kv', 'new_v_w_br_rg', 'new_v_w_br_swa', 'new_v_w_br_mem', 'new_v_w_out']
TWIN_LEAF_KINDS = {'loss': 'loss', 'grad_x': 'grad_x', 'grad_pre_norm_g': 'grad_w', 'grad_post_norm_g': 'grad_w', 'grad_mem_norm_g': 'grad_w', 'grad_w_in': 'grad_w', 'grad_conv_w': 'grad_w', 'grad_conv_b': 'grad_w', 'grad_w_rg_a': 'grad_w', 'grad_b_rg_a': 'grad_w', 'grad_w_rg_x': 'grad_w', 'grad_b_rg_x': 'grad_w', 'grad_lru_lambda': 'grad_w', 'grad_swa_sinks': 'grad_w', 'grad_rel_bias': 'grad_w', 'grad_w_mem_kv': 'grad_w', 'grad_w_br_rg': 'grad_w', 'grad_w_br_swa': 'grad_w', 'grad_w_br_mem': 'grad_w', 'grad_w_out': 'grad_w', 'delta_pre_norm_g': 'delta_w', 'delta_post_norm_g': 'delta_w', 'delta_mem_norm_g': 'delta_w', 'delta_w_in': 'delta_w', 'delta_conv_w': 'delta_w', 'delta_conv_b': 'delta_w', 'delta_w_rg_a': 'delta_w', 'delta_b_rg_a': 'delta_w', 'delta_w_rg_x': 'delta_w', 'delta_b_rg_x': 'delta_w', 'delta_lru_lambda': 'delta_w', 'delta_swa_sinks': 'delta_w', 'delta_rel_bias': 'delta_w', 'delta_w_mem_kv': 'delta_w', 'delta_w_br_rg': 'delta_w', 'delta_w_br_swa': 'delta_w', 'delta_w_br_mem': 'delta_w', 'delta_w_out': 'delta_w', 'new_m_pre_norm_g': 'new_m', 'new_m_post_norm_g': 'new_m', 'new_m_mem_norm_g': 'new_m', 'new_m_w_in': 'new_m', 'new_m_conv_w': 'new_m', 'new_m_conv_b': 'new_m', 'new_m_w_rg_a': 'new_m', 'new_m_b_rg_a': 'new_m', 'new_m_w_rg_x': 'new_m', 'new_m_b_rg_x': 'new_m', 'new_m_lru_lambda': 'new_m', 'new_m_swa_sinks': 'new_m', 'new_m_rel_bias': 'new_m', 'new_m_w_mem_kv': 'new_m', 'new_m_w_br_rg': 'new_m', 'new_m_w_br_swa': 'new_m', 'new_m_w_br_mem': 'new_m', 'new_m_w_out': 'new_m', 'new_v_pre_norm_g': 'new_v', 'new_v_post_norm_g': 'new_v', 'new_v_mem_norm_g': 'new_v', 'new_v_w_in': 'new_v', 'new_v_conv_w': 'new_v', 'new_v_conv_b': 'new_v', 'new_v_w_rg_a': 'new_v', 'new_v_b_rg_a': 'new_v', 'new_v_w_rg_x': 'new_v', 'new_v_b_rg_x': 'new_v', 'new_v_lru_lambda': 'new_v', 'new_v_swa_sinks': 'new_v', 'new_v_rel_bias': 'new_v', 'new_v_w_mem_kv': 'new_v', 'new_v_w_br_rg': 'new_v', 'new_v_w_br_swa': 'new_v', 'new_v_w_br_mem': 'new_v', 'new_v_w_out': 'new_v'}


def _forward(args):
    return _fwd_reference(*[args[k] for k in FWD_PARAMS])


def _output_shape():
    out = _jax.eval_shape(lambda: _forward(_fwd_setup_inputs(0)))
    return out.shape, out.dtype

N_MICROBATCH = 1
ADAM_LR = 0.001
ADAM_B1 = 0.9
ADAM_B2 = 0.999
ADAM_EPS = 1e-08
ADAM_WD = 0.01
ADAM_STEP = 10
PER_EXAMPLE_BATCH_AXIS = {'x': 0, 'mem': 0, 'loss_target': 0}
SHARED_INPUTS = []
_WEIGHT_DTYPES = {'pre_norm_g': _jnp.float32, 'post_norm_g': _jnp.float32, 'mem_norm_g': _jnp.float32, 'w_in': _jnp.float32, 'conv_w': _jnp.float32, 'conv_b': _jnp.float32, 'w_rg_a': _jnp.float32, 'b_rg_a': _jnp.float32, 'w_rg_x': _jnp.float32, 'b_rg_x': _jnp.float32, 'lru_lambda': _jnp.float32, 'swa_sinks': _jnp.float32, 'rel_bias': _jnp.float32, 'w_mem_kv': _jnp.float32, 'w_br_rg': _jnp.float32, 'w_br_swa': _jnp.float32, 'w_br_mem': _jnp.float32, 'w_out': _jnp.float32}
MOMENT_SCALE = {'pre_norm_g': 1.448956e-01, 'post_norm_g': 8.003949e+00, 'mem_norm_g': 2.294106e-02, 'w_in': 5.696206e-02, 'conv_w': 1.451390e-01, 'conv_b': 2.170550e+00, 'w_rg_a': 5.099596e-02, 'b_rg_a': 3.714426e-02, 'w_rg_x': 9.229813e-02, 'b_rg_x': 4.550000e-02, 'lru_lambda': 6.852497e-02, 'swa_sinks': 3.860870e-02, 'rel_bias': 5.632723e-02, 'w_mem_kv': 2.234062e-02, 'w_br_rg': 9.765483e-02, 'w_br_swa': 3.302966e-02, 'w_br_mem': 1.596907e-02, 'w_out': 9.917263e-02}


def _to_microbatches(a, axis):
    t = _jnp.moveaxis(a, axis, 0)
    t = t.reshape((N_MICROBATCH, t.shape[0] // N_MICROBATCH) + t.shape[1:])
    return _jnp.moveaxis(t, 1, axis + 1)


def setup_inputs(seed: int = 0) -> dict:
    inp = _fwd_setup_inputs(seed)
    key = _jax.random.fold_in(_jax.random.key(seed), 7919)
    shape, _ = _output_shape()
    out = dict(inp)
    out["loss_target"] = _jax.random.normal(_jax.random.fold_in(key, 0), shape, _jnp.float32)
    for i, name in enumerate(TWIN_WEIGHTS):
        w = inp[name].astype(_jnp.float32)
        if MOMENT_SCALE is None:
            s = _jnp.sqrt(_jnp.mean(_jnp.square(w)) + 1e-30)
        else:
            s = MOMENT_SCALE[name]
        km, kv = _jax.random.split(_jax.random.fold_in(key, i + 1))
        out[name] = w
        out["m_" + name] = s * _jax.random.normal(km, w.shape, _jnp.float32)
        out["v_" + name] = (s * s) * _jax.random.uniform(kv, w.shape, _jnp.float32, 0.5, 1.5)
    if N_MICROBATCH > 1:
        for name, axis in PER_EXAMPLE_BATCH_AXIS.items():
            out[name] = _to_microbatches(out[name], axis)
    return {'x': out['x'], 'mem': out['mem'], 'pre_norm_g': out['pre_norm_g'], 'post_norm_g': out['post_norm_g'], 'mem_norm_g': out['mem_norm_g'], 'w_in': out['w_in'], 'conv_w': out['conv_w'], 'conv_b': out['conv_b'], 'w_rg_a': out['w_rg_a'], 'b_rg_a': out['b_rg_a'], 'w_rg_x': out['w_rg_x'], 'b_rg_x': out['b_rg_x'], 'lru_lambda': out['lru_lambda'], 'swa_sinks': out['swa_sinks'], 'rel_bias': out['rel_bias'], 'w_mem_kv': out['w_mem_kv'], 'w_br_rg': out['w_br_rg'], 'w_br_swa': out['w_br_swa'], 'w_br_mem': out['w_br_mem'], 'w_out': out['w_out'], 'loss_target': out['loss_target'], 'm_pre_norm_g': out['m_pre_norm_g'], 'm_post_norm_g': out['m_post_norm_g'], 'm_mem_norm_g': out['m_mem_norm_g'], 'm_w_in': out['m_w_in'], 'm_conv_w': out['m_conv_w'], 'm_conv_b': out['m_conv_b'], 'm_w_rg_a': out['m_w_rg_a'], 'm_b_rg_a': out['m_b_rg_a'], 'm_w_rg_x': out['m_w_rg_x'], 'm_b_rg_x': out['m_b_rg_x'], 'm_lru_lambda': out['m_lru_lambda'], 'm_swa_sinks': out['m_swa_sinks'], 'm_rel_bias': out['m_rel_bias'], 'm_w_mem_kv': out['m_w_mem_kv'], 'm_w_br_rg': out['m_w_br_rg'], 'm_w_br_swa': out['m_w_br_swa'], 'm_w_br_mem': out['m_w_br_mem'], 'm_w_out': out['m_w_out'], 'v_pre_norm_g': out['v_pre_norm_g'], 'v_post_norm_g': out['v_post_norm_g'], 'v_mem_norm_g': out['v_mem_norm_g'], 'v_w_in': out['v_w_in'], 'v_conv_w': out['v_conv_w'], 'v_conv_b': out['v_conv_b'], 'v_w_rg_a': out['v_w_rg_a'], 'v_b_rg_a': out['v_b_rg_a'], 'v_w_rg_x': out['v_w_rg_x'], 'v_b_rg_x': out['v_b_rg_x'], 'v_lru_lambda': out['v_lru_lambda'], 'v_swa_sinks': out['v_swa_sinks'], 'v_rel_bias': out['v_rel_bias'], 'v_w_mem_kv': out['v_w_mem_kv'], 'v_w_br_rg': out['v_w_br_rg'], 'v_w_br_swa': out['v_w_br_swa'], 'v_w_br_mem': out['v_w_br_mem'], 'v_w_out': out['v_w_out']}


def _loss(weights, diff, rest, loss_target):
    with _jax.named_scope("forward"):
        args = {**rest, TWIN_DIFF_INPUT: diff, **{k: w.astype(_WEIGHT_DTYPES[k]) for k, w in weights.items()}}
        y = _forward(args)
    with _jax.named_scope("loss_head"):
        err = _jnp.square(y.astype(_jnp.float32) - loss_target)
        return 0.5 * _jnp.sum(_jnp.mean(err, axis=-1)) if err.ndim else 0.5 * err


def _adamw(w, g, m, v):
    m = ADAM_B1 * m + (1.0 - ADAM_B1) * g
    v = ADAM_B2 * v + (1.0 - ADAM_B2) * _jnp.square(g)
    m_hat = m / (1.0 - ADAM_B1 ** ADAM_STEP)
    v_hat = v / (1.0 - ADAM_B2 ** ADAM_STEP)
    delta = -ADAM_LR * (m_hat / (_jnp.sqrt(v_hat) + ADAM_EPS) + ADAM_WD * w)
    return delta, m, v


def reference(x, mem, pre_norm_g, post_norm_g, mem_norm_g, w_in, conv_w, conv_b, w_rg_a, b_rg_a, w_rg_x, b_rg_x, lru_lambda, swa_sinks, rel_bias, w_mem_kv, w_br_rg, w_br_swa, w_br_mem, w_out, loss_target, m_pre_norm_g, m_post_norm_g, m_mem_norm_g, m_w_in, m_conv_w, m_conv_b, m_w_rg_a, m_b_rg_a, m_w_rg_x, m_b_rg_x, m_lru_lambda, m_swa_sinks, m_rel_bias, m_w_mem_kv, m_w_br_rg, m_w_br_swa, m_w_br_mem, m_w_out, v_pre_norm_g, v_post_norm_g, v_mem_norm_g, v_w_in, v_conv_w, v_conv_b, v_w_rg_a, v_b_rg_a, v_w_rg_x, v_b_rg_x, v_lru_lambda, v_swa_sinks, v_rel_bias, v_w_mem_kv, v_w_br_rg, v_w_br_swa, v_w_br_mem, v_w_out):
    given = dict(x=x, mem=mem, pre_norm_g=pre_norm_g, post_norm_g=post_norm_g, mem_norm_g=mem_norm_g, w_in=w_in, conv_w=conv_w, conv_b=conv_b, w_rg_a=w_rg_a, b_rg_a=b_rg_a, w_rg_x=w_rg_x, b_rg_x=b_rg_x, lru_lambda=lru_lambda, swa_sinks=swa_sinks, rel_bias=rel_bias, w_mem_kv=w_mem_kv, w_br_rg=w_br_rg, w_br_swa=w_br_swa, w_br_mem=w_br_mem, w_out=w_out, loss_target=loss_target, m_pre_norm_g=m_pre_norm_g, m_post_norm_g=m_post_norm_g, m_mem_norm_g=m_mem_norm_g, m_w_in=m_w_in, m_conv_w=m_conv_w, m_conv_b=m_conv_b, m_w_rg_a=m_w_rg_a, m_b_rg_a=m_b_rg_a, m_w_rg_x=m_w_rg_x, m_b_rg_x=m_b_rg_x, m_lru_lambda=m_lru_lambda, m_swa_sinks=m_swa_sinks, m_rel_bias=m_rel_bias, m_w_mem_kv=m_w_mem_kv, m_w_br_rg=m_w_br_rg, m_w_br_swa=m_w_br_swa, m_w_br_mem=m_w_br_mem, m_w_out=m_w_out, v_pre_norm_g=v_pre_norm_g, v_post_norm_g=v_post_norm_g, v_mem_norm_g=v_mem_norm_g, v_w_in=v_w_in, v_conv_w=v_conv_w, v_conv_b=v_conv_b, v_w_rg_a=v_w_rg_a, v_b_rg_a=v_b_rg_a, v_w_rg_x=v_w_rg_x, v_b_rg_x=v_b_rg_x, v_lru_lambda=v_lru_lambda, v_swa_sinks=v_swa_sinks, v_rel_bias=v_rel_bias, v_w_mem_kv=v_w_mem_kv, v_w_br_rg=v_w_br_rg, v_w_br_swa=v_w_br_swa, v_w_br_mem=v_w_br_mem, v_w_out=v_w_out)
    weights = {n: given[n] for n in TWIN_WEIGHTS}
    shared = {n: given[n] for n in SHARED_INPUTS}
    per_example = {n: given[n] for n in ['x', 'mem']}
    grad_fn = _jax.value_and_grad(_loss, argnums=(0, 1))

    def one_microbatch(ex, loss_target):
        ex = dict(ex)
        diff = ex.pop(TWIN_DIFF_INPUT)
        return grad_fn(weights, diff, {**shared, **ex}, loss_target)

    if N_MICROBATCH == 1:
        loss, (grad_w, grad_x) = one_microbatch(per_example, given["loss_target"])
    else:
        def body(carry, xs):
            loss_sum, grad_sum = carry
            l_k, (gw_k, gx_k) = one_microbatch(xs[0], xs[1])
            with _jax.named_scope("update"):
                return (loss_sum + l_k, _jax.tree.map(_jnp.add, grad_sum, gw_k)), gx_k

        init = (_jnp.zeros((), _jnp.float32), _jax.tree.map(_jnp.zeros_like, weights))
        (loss, grad_w), grad_x = _jax.lax.scan(body, init, (per_example, given["loss_target"]))
    with _jax.named_scope("update"):
        delta_w, new_m, new_v = {}, {}, {}
        for n in TWIN_WEIGHTS:
            delta_w[n], new_m[n], new_v[n] = _adamw(weights[n], grad_w[n], given["m_" + n], given["v_" + n])
    return (loss, grad_x, *[grad_w[n] for n in TWIN_WEIGHTS], *[delta_w[n] for n in TWIN_WEIGHTS],
            *[new_m[n] for n in TWIN_WEIGHTS], *[new_v[n] for n in TWIN_WEIGHTS])
```

```python
import functools
import math
from typing import NamedTuple

import jax
import jax.numpy as jnp
from jax import lax
from jax.experimental import pallas as pl
from jax.experimental.pallas import tpu as pltpu

F32 = jnp.float32
BF16 = jnp.bfloat16
MESH = pl.DeviceIdType.MESH

S = 2048
D = 2048
MEM = 256
D_RNN = 1024
RNN_BLOCKS = 8
CONV_W = 4
LRU_C = 8.0
SWA_HEADS = 16
SWA_HD = 64
WINDOW = 128
MEM_HEADS = 4
MEM_HD = 256
REL_BUCKETS = 32
REL_MAX_DIST = 128
EPS = 1e-6
NEG_INF = -1e30
LANE = 128
SHARD = 3136
WIN = 3200
N_CHIPS = 4
VMEM_LIMIT = 56 * 1024 * 1024

ADAM_LR = 0.001
ADAM_B1 = 0.9
ADAM_B2 = 0.999
ADAM_EPS = 1e-08
ADAM_WD = 0.01
ADAM_STEP = 10

GROUP_TILES = {"A": 16, "B": 20, "C": 16, "D": 48}
GROUPS = ("A", "B", "C", "D")


def _params(sem=None):
    return pltpu.CompilerParams(dimension_semantics=sem, vmem_limit_bytes=VMEM_LIMIT)


def _sigmoid(v):
    return jax.nn.sigmoid(v)


def _window_tile_home(t):
    if t < 16:
        return "A", t
    if t < 24:
        return "B", t - 16
    if t < 27:
        return "B", t - 24 + 16
    if t < 35:
        return "B", t - 27 + 8
    if t < 51:
        return "C", t - 35
    if t < 75:
        return "D", t - 51
    if t == 75:
        return "B", 19
    return "D", t - 76 + 24


def _window_runs(j):
    runs = []
    for lt in range(WIN // LANE):
        g, gt = _window_tile_home(25 * j + lt)
        if runs and runs[-1][2] == g and runs[-1][3] + runs[-1][1] == gt:
            runs[-1][1] += 1
        else:
            runs.append([lt, 1, g, gt])
    return [tuple(r) for r in runs]


_DIMS = {
    "nn": (((1,), (0,)), ((), ())),
    "nt": (((1,), (1,)), ((), ())),
    "tn": (((0,), (0,)), ((), ())),
}


def _mm(a, b, mode, out_dtype, tm, tn, tk, name, acc=None):
    if mode == "nn":
        (m, k), n = a.shape, b.shape[1]
    elif mode == "nt":
        (m, k), n = a.shape, b.shape[0]
    else:
        (k, m), n = a.shape, b.shape[1]
    tm, tn, tk = min(tm, m), min(tn, n), min(tk, k)
    assert m % tm == 0 and n % tn == 0 and k % tk == 0, (name, m, n, k)
    nk = k // tk
    has_acc = acc is not None

    def body(*refs):
        a_ref, b_ref = refs[0], refs[1]
        o_ref = refs[3] if has_acc else refs[2]
        p = lax.dot_general(a_ref[...], b_ref[...], _DIMS[mode], preferred_element_type=F32)

        def finish(v):
            if has_acc:
                v = v + refs[2][...]
            o_ref[...] = v.astype(out_dtype)

        if nk == 1:
            finish(p)
        else:
            s_ref = refs[-1]
            kk = pl.program_id(2)

            @pl.when(kk == 0)
            def _():
                s_ref[...] = p

            @pl.when(kk > 0)
            def _():
                s_ref[...] += p

            @pl.when(kk == nk - 1)
            def _():
                finish(s_ref[...])

    if mode == "nn":
        a_spec = pl.BlockSpec((tm, tk), lambda i, j, kk: (i, kk))
        b_spec = pl.BlockSpec((tk, tn), lambda i, j, kk: (kk, j))
    elif mode == "nt":
        a_spec = pl.BlockSpec((tm, tk), lambda i, j, kk: (i, kk))
        b_spec = pl.BlockSpec((tn, tk), lambda i, j, kk: (j, kk))
    else:
        a_spec = pl.BlockSpec((tk, tm), lambda i, j, kk: (kk, i))
        b_spec = pl.BlockSpec((tk, tn), lambda i, j, kk: (kk, j))
    o_spec = pl.BlockSpec((tm, tn), lambda i, j, kk: (i, j))
    in_specs = [a_spec, b_spec] + ([o_spec] if has_acc else [])
    args = (a, b) + ((acc,) if has_acc else ())
    return pl.pallas_call(
        body,
        name=name,
        grid=(m // tm, n // tn, nk),
        in_specs=in_specs,
        out_specs=o_spec,
        out_shape=jax.ShapeDtypeStruct((m, n), out_dtype),
        scratch_shapes=[pltpu.VMEM((tm, tn), F32)] if nk > 1 else [],
        compiler_params=_params(("parallel", "parallel", "arbitrary")),
    )(*args)


def _rms_fwd(x, g, name, ts=256):
    r, d = x.shape

    def body(x_ref, g_ref, o_ref):
        xv = x_ref[...]
        inv = lax.rsqrt(jnp.mean(xv * xv, axis=-1, keepdims=True) + EPS)
        o_ref[...] = (xv * inv * g_ref[...]).astype(BF16)

    return pl.pallas_call(
        body,
        name=name,
        grid=(r // ts,),
        in_specs=[pl.BlockSpec((ts, d), lambda i: (i, 0)), pl.BlockSpec((1, d), lambda i: (0, 0))],
        out_specs=pl.BlockSpec((ts, d), lambda i: (i, 0)),
        out_shape=jax.ShapeDtypeStruct((r, d), BF16),
        compiler_params=_params(("parallel",)),
    )(x, g)


def _post_loss(out, x, tgt, g_post, ts=256):
    n = S // ts

    def body(o_ref, x_ref, t_ref, g_ref, sq_ref, dy_ref, do_ref, dg_ref):
        i = pl.program_id(0)

        @pl.when(i == 0)
        def _():
            sq_ref[...] = jnp.zeros_like(sq_ref)
            dg_ref[...] = jnp.zeros_like(dg_ref)

        ov = o_ref[...]
        g = g_ref[...]
        inv = lax.rsqrt(jnp.mean(ov * ov, axis=-1, keepdims=True) + EPS)
        on = ov * inv
        err = x_ref[...] + on * g - t_ref[...]
        sq_ref[...] += jnp.sum(err * err)
        dy = err * (1.0 / D)
        dy_ref[...] = dy
        dg_ref[...] += jnp.sum(dy * on, axis=0, keepdims=True)
        don = dy * g
        do_ref[...] = (inv * (don - on * jnp.mean(don * on, axis=-1, keepdims=True))).astype(BF16)

    row = pl.BlockSpec((ts, D), lambda i: (i, 0))
    vec = pl.BlockSpec((1, D), lambda i: (0, 0))
    return pl.pallas_call(
        body,
        name="post_loss",
        grid=(n,),
        in_specs=[row, row, row, vec],
        out_specs=[pl.BlockSpec((8, LANE), lambda i: (0, 0)), row, row, vec],
        out_shape=[
            jax.ShapeDtypeStruct((8, LANE), F32),
            jax.ShapeDtypeStruct((S, D), F32),
            jax.ShapeDtypeStruct((S, D), BF16),
            jax.ShapeDtypeStruct((1, D), F32),
        ],
        compiler_params=_params(("arbitrary",)),
    )(out, x, tgt, g_post)


def _pre_bwd(dh, x, dy, g_pre, ts=256):
    n = S // ts

    def body(dh_ref, x_ref, dy_ref, g_ref, gx_ref, dg_ref):
        i = pl.program_id(0)

        @pl.when(i == 0)
        def _():
            dg_ref[...] = jnp.zeros_like(dg_ref)

        xv = x_ref[...]
        dhv = dh_ref[...]
        inv = lax.rsqrt(jnp.mean(xv * xv, axis=-1, keepdims=True) + EPS)
        xn = xv * inv
        dg_ref[...] += jnp.sum(dhv * xn, axis=0, keepdims=True)
        dxn = dhv * g_ref[...]
        gx_ref[...] = dy_ref[...] + inv * (dxn - xn * jnp.mean(dxn * xn, axis=-1, keepdims=True))

    row = pl.BlockSpec((ts, D), lambda i: (i, 0))
    vec = pl.BlockSpec((1, D), lambda i: (0, 0))
    return pl.pallas_call(
        body,
        name="pre_bwd",
        grid=(n,),
        in_specs=[row, row, row, vec],
        out_specs=[row, vec],
        out_shape=[jax.ShapeDtypeStruct((S, D), F32), jax.ShapeDtypeStruct((1, D), F32)],
        compiler_params=_params(("arbitrary",)),
    )(dh, x, dy, g_pre)


def _memnorm_bwd(dmemn, mem):
    def body(d_ref, m_ref, dg_ref):
        mv = m_ref[...]
        inv = lax.rsqrt(jnp.mean(mv * mv, axis=-1, keepdims=True) + EPS)
        dg_ref[...] = jnp.sum(d_ref[...] * mv * inv, axis=0, keepdims=True)

    return pl.pallas_call(
        body,
        name="memnorm_bwd",
        out_shape=jax.ShapeDtypeStruct((1, D), F32),
        compiler_params=_params(),
    )(dmemn, mem)


T_RNN = 256


def _neg_expm1(z):
    poly = -z * (1.0 + z * (0.5 + z * (1.0 / 6 + z * (1.0 / 24 + z * (1.0 / 120 + z * (1.0 / 720))))))
    return jnp.where(z > -0.1, poly, 1.0 - jnp.exp(z))


def _softplus_neg(lam):
    return jnp.maximum(-lam, 0.0) + jnp.log1p(jnp.exp(-jnp.abs(lam)))


def _rnn_gates(conv, wa_ref, ba, wx_ref, bx, lam, first_row):
    cbf = conv.astype(BF16)
    ga, gx = [], []
    for n in range(RNN_BLOCKS):
        c_n = cbf[:, n * LANE:(n + 1) * LANE]
        ga.append(jnp.dot(c_n, wa_ref[n], preferred_element_type=F32))
        gx.append(jnp.dot(c_n, wx_ref[n], preferred_element_type=F32))
    gate_r = _sigmoid(jnp.concatenate(ga, axis=1) + ba)
    gate_i = _sigmoid(jnp.concatenate(gx, axis=1) + bx)
    sp = _softplus_neg(lam)
    log_a = -LRU_C * gate_r * sp
    a = jnp.exp(log_a)
    mult_raw = jnp.sqrt(_neg_expm1(2.0 * log_a))
    mult = jnp.where(first_row, 1.0, mult_raw)
    return cbf, gate_r, gate_i, sp, a, mult_raw, mult


def _rglru_fwd(p_a, conv_w, conv_b, wa, ba, wx, bx, lam):
    t = T_RNN
    n = S // t

    def body(xr_ref, g_ref, cw_ref, cb_ref, wa_ref, ba_ref, wx_ref, bx_ref, lam_ref,
             y_ref, h_ref, xp_s, hcar, a_s, b_s):
        i = pl.program_id(0)

        @pl.when(i == 0)
        def _():
            xp_s[0:8, :] = jnp.zeros((8, D_RNN), F32)
            hcar[...] = jnp.zeros_like(hcar)

        @pl.when(i > 0)
        def _():
            xp_s[0:8, :] = xp_s[t:t + 8, :]

        xp_s[8:8 + t, :] = xr_ref[...]
        conv = cb_ref[...]
        for k in range(CONV_W):
            conv = conv + cw_ref[k:k + 1, :] * xp_s[8 - k:8 - k + t, :]
        rows = i * t + lax.broadcasted_iota(jnp.int32, (t, 1), 0)
        _, _, gate_i, _, a, _, mult = _rnn_gates(
            conv, wa_ref, ba_ref[...], wx_ref, bx_ref[...], lam_ref[...], rows == 0)
        a_s[...] = a
        b_s[...] = mult * gate_i * conv

        def step(tt, h):
            h = a_s[pl.ds(tt, 1), :] * h + b_s[pl.ds(tt, 1), :]
            h_ref[pl.ds(tt, 1), :] = h
            return h

        hcar[...] = lax.fori_loop(0, t, step, hcar[...], unroll=8)
        g = g_ref[...]
        y_ref[...] = (h_ref[...] * (g * _sigmoid(g))).astype(BF16)

    blk = lambda c: pl.BlockSpec((t, D_RNN), lambda i: (i, c))
    full = lambda shape: pl.BlockSpec(shape, lambda i: (0,) * len(shape))
    return pl.pallas_call(
        body,
        name="rglru_fwd",
        grid=(n,),
        in_specs=[blk(0), blk(1), full((CONV_W, D_RNN)), full((1, D_RNN)),
                  full((RNN_BLOCKS, LANE, LANE)), full((1, D_RNN)),
                  full((RNN_BLOCKS, LANE, LANE)), full((1, D_RNN)), full((1, D_RNN))],
        out_specs=[blk(0), blk(0)],
        out_shape=[jax.ShapeDtypeStruct((S, D_RNN), BF16), jax.ShapeDtypeStruct((S, D_RNN), F32)],
        scratch_shapes=[pltpu.VMEM((t + 8, D_RNN), F32), pltpu.VMEM((1, D_RNN), F32),
                        pltpu.VMEM((t, D_RNN), F32), pltpu.VMEM((t, D_RNN), F32)],
        compiler_params=_params(("arbitrary",)),
    )(p_a, p_a, conv_w, conv_b, wa, ba, wx, bx, lam)


def _rglru_bwd(dy, p_a, hseq, conv_w, conv_b, wa, ba, wx, bx, lam):
    t = T_RNN
    n = S // t
    rb = t // 8

    def body(dy_ref, xr_ref, g_ref, h_ref, xrp_ref, hp_ref, cw_ref, cb_ref, wa_ref, ba_ref, wx_ref, bx_ref, lam_ref,
             dp_ref, dcw_ref, dcb_ref, dwa_ref, dba_ref, dwx_ref, dbx_ref, dlam_ref,
             xp_s, hp_s, dxp_s, lamcar, a_s, dh_s, lam_s):
        i = pl.program_id(0)
        r = n - 1 - i

        @pl.when(i == 0)
        def _():
            for ref in (dcw_ref, dcb_ref, dwa_ref, dba_ref, dwx_ref, dbx_ref, dlam_ref, lamcar):
                ref[...] = jnp.zeros_like(ref)
            dxp_s[t:t + 8, :] = jnp.zeros((8, D_RNN), F32)

        @pl.when(i > 0)
        def _():
            dxp_s[t:t + 8, :] = dxp_s[0:8, :]

        has_prev = r > 0
        xp_s[0:8, :] = jnp.where(has_prev, xrp_ref[...], 0.0)
        xp_s[8:8 + t, :] = xr_ref[...]
        hp_s[0:8, :] = jnp.where(has_prev, hp_ref[...], 0.0)
        hp_s[8:8 + t, :] = h_ref[...]
        xs = [xp_s[8 - k:8 - k + t, :] for k in range(CONV_W)]
        conv = cb_ref[...]
        for k in range(CONV_W):
            conv = conv + cw_ref[k:k + 1, :] * xs[k]
        rows = r * t + lax.broadcasted_iota(jnp.int32, (t, 1), 0)
        first = rows == 0
        lam_p = lam_ref[...]
        cbf, gate_r, gate_i, sp, a, mult_raw, mult = _rnn_gates(
            conv, wa_ref, ba_ref[...], wx_ref, bx_ref[...], lam_p, first)

        g = g_ref[...]
        sg = _sigmoid(g)
        dyv = dy_ref[...]
        a_s[...] = a
        dh_s[...] = dyv * (g * sg)
        dg = dyv * h_ref[...] * (sg * (1.0 + g * (1.0 - sg)))

        def step(jj, car):
            tt = t - 1 - jj
            lm = dh_s[pl.ds(tt, 1), :] + car
            lam_s[pl.ds(tt, 1), :] = lm
            return a_s[pl.ds(tt, 1), :] * lm

        lamcar[...] = lax.fori_loop(0, t, step, lamcar[...], unroll=8)
        db = lam_s[...]
        da = db * hp_s[7:7 + t, :]
        dmult = db * gate_i * conv
        dgate_i = db * mult * conv
        dconv = db * mult * gate_i
        dlog_a = da * a + jnp.where(first, 0.0, dmult * (-(a * a) / mult_raw))
        dgate_r = dlog_a * (-LRU_C * sp)
        dsp = jnp.sum(dlog_a * (-LRU_C * gate_r), axis=0, keepdims=True)
        dlam_ref[...] += dsp * (-_sigmoid(-lam_p))
        dga = dgate_r * gate_r * (1.0 - gate_r)
        dgx = dgate_i * gate_i * (1.0 - gate_i)
        dba_ref[...] += jnp.sum(dga, axis=0, keepdims=True)
        dbx_ref[...] += jnp.sum(dgx, axis=0, keepdims=True)
        dga16, dgx16 = dga.astype(BF16), dgx.astype(BF16)
        back = []
        for nb in range(RNN_BLOCKS):
            sl = slice(nb * LANE, (nb + 1) * LANE)
            dwa_ref[nb] += lax.dot_general(cbf[:, sl], dga16[:, sl], _DIMS["tn"], preferred_element_type=F32)
            dwx_ref[nb] += lax.dot_general(cbf[:, sl], dgx16[:, sl], _DIMS["tn"], preferred_element_type=F32)
            back.append(lax.dot_general(dga16[:, sl], wa_ref[nb], _DIMS["nt"], preferred_element_type=F32)
                        + lax.dot_general(dgx16[:, sl], wx_ref[nb], _DIMS["nt"], preferred_element_type=F32))
        dconv = dconv + jnp.concatenate(back, axis=1)
        dcb_ref[...] += jnp.sum(dconv, axis=0, keepdims=True)
        for k in range(CONV_W):
            dcw_ref[k:k + 1, :] += jnp.sum(dconv * xs[k], axis=0, keepdims=True)
        dxp_s[0:t, :] = dconv
        dxr = cw_ref[0:1, :] * dconv
        for k in range(1, CONV_W):
            dxr = dxr + cw_ref[k:k + 1, :] * dxp_s[k:k + t, :]
        dp_ref[:, 0:D_RNN] = dxr.astype(BF16)
        dp_ref[:, D_RNN:2 * D_RNN] = dg.astype(BF16)

    blk = lambda c: pl.BlockSpec((t, D_RNN), lambda i: (n - 1 - i, c))
    prev8 = pl.BlockSpec((8, D_RNN), lambda i: (jnp.maximum((n - 1 - i) * rb - 1, 0), 0))
    full = lambda shape: pl.BlockSpec(shape, lambda i: (0,) * len(shape))
    vec = full((1, D_RNN))
    mat = full((RNN_BLOCKS, LANE, LANE))
    return pl.pallas_call(
        body,
        name="rglru_bwd",
        grid=(n,),
        in_specs=[blk(0), blk(0), blk(1), blk(0), prev8, prev8,
                  full((CONV_W, D_RNN)), vec, mat, vec, mat, vec, vec],
        out_specs=[pl.BlockSpec((t, 2 * D_RNN), lambda i: (n - 1 - i, 0)),
                   full((CONV_W, D_RNN)), vec, mat, vec, mat, vec, vec],
        out_shape=[jax.ShapeDtypeStruct((S, 2 * D_RNN), BF16),
                   jax.ShapeDtypeStruct((CONV_W, D_RNN), F32), jax.ShapeDtypeStruct((1, D_RNN), F32),
                   jax.ShapeDtypeStruct((RNN_BLOCKS, LANE, LANE), F32), jax.ShapeDtypeStruct((1, D_RNN), F32),
                   jax.ShapeDtypeStruct((RNN_BLOCKS, LANE, LANE), F32), jax.ShapeDtypeStruct((1, D_RNN), F32),
                   jax.ShapeDtypeStruct((1, D_RNN), F32)],
        scratch_shapes=[pltpu.VMEM((t + 8, D_RNN), F32), pltpu.VMEM((t + 8, D_RNN), F32),
                        pltpu.VMEM((t + 8, D_RNN), F32), pltpu.VMEM((1, D_RNN), F32),
                        pltpu.VMEM((t, D_RNN), F32), pltpu.VMEM((t, D_RNN), F32), pltpu.VMEM((t, D_RNN), F32)],
        compiler_params=_params(("arbitrary",)),
    )(dy, p_a, p_a, hseq, p_a, hseq, conv_w, conv_b, wa, ba, wx, bx, lam)


QB = WINDOW
KB2 = 2 * WINDOW
N_QB = S // QB
N_PAIR = SWA_HEADS // 2


def _swa_keys(kvc_ref, kvp_ref):
    kvc, kvp = kvc_ref[...], kvp_ref[...]
    kk = jnp.concatenate([kvp[:, 0:LANE] + kvp[:, LANE:2 * LANE], kvc[:, 0:LANE] + kvc[:, LANE:2 * LANE]], axis=0)
    vv = jnp.concatenate([kvp[:, 2 * LANE:3 * LANE], kvc[:, 2 * LANE:3 * LANE]], axis=0)
    lo = lax.broadcasted_iota(jnp.int32, (1, LANE), 1) < SWA_HD
    kk_sw, vv_sw = pltpu.roll(kk, SWA_HD, 1), pltpu.roll(vv, SWA_HD, 1)
    kd = [jnp.where(lo, kk, kk_sw).astype(BF16), jnp.where(lo, kk_sw, kk).astype(BF16)]
    vd = [jnp.where(lo, vv, vv_sw).astype(BF16), jnp.where(lo, vv_sw, vv).astype(BF16)]
    return lo, kd, vd


def _swa_valid(n):
    qi = lax.broadcasted_iota(jnp.int32, (QB, KB2), 0)
    kj = lax.broadcasted_iota(jnp.int32, (QB, KB2), 1)
    dist = qi + WINDOW - kj
    return (dist >= 0) & (dist < WINDOW) & ((n > 0) | (kj >= WINDOW))


def _swa_probs(qh16, kd, bias, sink, valid):
    lg = lax.dot_general(qh16, kd, _DIMS["nt"], preferred_element_type=F32) * (SWA_HD ** -0.5) + bias
    lg = jnp.where(valid, lg, NEG_INF)
    m = jnp.maximum(jnp.max(lg, axis=-1, keepdims=True), sink)
    p = jnp.exp(lg - m)
    es = jnp.exp(sink - m)
    den = jnp.sum(p, axis=-1, keepdims=True) + es
    return p / den, es / den


def _swa_specs():
    q = pl.BlockSpec((QB, D_RNN), lambda n: (n, 0))
    g = pl.BlockSpec((QB, D_RNN), lambda n: (n, 1))
    kvc = pl.BlockSpec((QB, 4 * LANE), lambda n: (n, 4))
    kvp = pl.BlockSpec((QB, 4 * LANE), lambda n: (jnp.maximum(n - 1, 0), 4))
    bias = pl.BlockSpec((SWA_HEADS, QB, KB2), lambda n: (0, 0, 0))
    sinks = pl.BlockSpec(memory_space=pltpu.SMEM)
    return q, g, kvc, kvp, bias, sinks


def _swa_fwd(p_b, bias_t, sinks):
    def body(q_ref, g_ref, kvc_ref, kvp_ref, bias_ref, sink_ref, y_ref, o_ref):
        n = pl.program_id(0)
        lo, kd, vd = _swa_keys(kvc_ref, kvp_ref)
        valid = _swa_valid(n)
        for hp in range(N_PAIR):
            sl = slice(hp * LANE, (hp + 1) * LANE)
            kvh = hp // (N_PAIR // 2)
            q = q_ref[:, sl]
            outs = []
            for j in range(2):
                mh = lo if j == 0 else jnp.logical_not(lo)
                qh16 = jnp.where(mh, q, 0.0).astype(BF16)
                probs, _ = _swa_probs(qh16, kd[kvh], bias_ref[2 * hp + j], sink_ref[2 * hp + j], valid)
                outs.append(jnp.dot(probs.astype(BF16), vd[kvh], preferred_element_type=F32))
            o = jnp.where(lo, outs[0], outs[1])
            o_ref[:, sl] = o
            g = g_ref[:, sl]
            y_ref[:, sl] = (o * (g * _sigmoid(g))).astype(BF16)

    q, g, kvc, kvp, bias, sinks_spec = _swa_specs()
    out = pl.BlockSpec((QB, D_RNN), lambda n: (n, 0))
    return pl.pallas_call(
        body,
        name="swa_fwd",
        grid=(N_QB,),
        in_specs=[q, g, kvc, kvp, bias, sinks_spec],
        out_specs=[out, out],
        out_shape=[jax.ShapeDtypeStruct((S, D_RNN), BF16), jax.ShapeDtypeStruct((S, D_RNN), F32)],
        compiler_params=_params(("parallel",)),
    )(p_b, p_b, p_b, p_b, bias_t, sinks)


def _swa_bwd(dy, p_b, o_swa, bias_t, sinks):
    def body(dy_ref, q_ref, g_ref, kvc_ref, kvp_ref, o_ref, bias_ref, sink_ref,
             dp_ref, dk_ref, dv_ref, dbias_ref, dsink_ref):
        n = pl.program_id(0)

        @pl.when(n == 0)
        def _():
            for ref in (dk_ref, dv_ref, dbias_ref, dsink_ref):
                ref[...] = jnp.zeros_like(ref)

        lo, kd, vd = _swa_keys(kvc_ref, kvp_ref)
        hi = jnp.logical_not(lo)
        valid = _swa_valid(n)
        dk_blk = jnp.zeros((KB2, LANE), F32)
        dv_blk = jnp.zeros((KB2, LANE), F32)
        for kvh in range(2):
            dk_pair = jnp.zeros((KB2, LANE), F32)
            dv_pair = jnp.zeros((KB2, LANE), F32)
            for hp in range(kvh * (N_PAIR // 2), (kvh + 1) * (N_PAIR // 2)):
                sl = slice(hp * LANE, (hp + 1) * LANE)
                q = q_ref[:, sl]
                g = g_ref[:, sl]
                o = o_ref[:, sl]
                dyv = dy_ref[:, sl]
                sg = _sigmoid(g)
                do = dyv * (g * sg)
                dp_ref[:, D_RNN + hp * LANE:D_RNN + (hp + 1) * LANE] = (
                    dyv * o * (sg * (1.0 + g * (1.0 - sg)))).astype(BF16)
                dqs = []
                for j in range(2):
                    h = 2 * hp + j
                    mh = lo if j == 0 else hi
                    qh16 = jnp.where(mh, q, 0.0).astype(BF16)
                    sink = sink_ref[h]
                    probs, psink = _swa_probs(qh16, kd[kvh], bias_ref[h], sink, valid)
                    doh = jnp.where(mh, do, 0.0)
                    doh16 = doh.astype(BF16)
                    delta = jnp.sum(doh * o, axis=-1, keepdims=True)
                    dpr = lax.dot_general(doh16, vd[kvh], _DIMS["nt"], preferred_element_type=F32)
                    ds = probs * (dpr - delta)
                    dbias_ref[h] += ds
                    dsink_ref[h:h + 1, :] += jnp.zeros((1, LANE), F32) - jnp.sum(psink * delta)
                    ds16 = (ds * (SWA_HD ** -0.5)).astype(BF16)
                    dqs.append(jnp.dot(ds16, kd[kvh], preferred_element_type=F32))
                    dk_pair = dk_pair + lax.dot_general(ds16, qh16, _DIMS["tn"], preferred_element_type=F32)
                    dv_pair = dv_pair + lax.dot_general(probs.astype(BF16), doh16, _DIMS["tn"],
                                                        preferred_element_type=F32)
                dp_ref[:, sl] = jnp.where(lo, dqs[0], dqs[1]).astype(BF16)
            keep = lo if kvh == 0 else hi
            dk_blk = dk_blk + jnp.where(keep, dk_pair + pltpu.roll(dk_pair, SWA_HD, 1), 0.0)
            dv_blk = dv_blk + jnp.where(keep, dv_pair + pltpu.roll(dv_pair, SWA_HD, 1), 0.0)

        cur = pl.ds(pl.multiple_of(n * QB, QB), QB)
        dk_ref[cur, :] += dk_blk[QB:KB2]
        dv_ref[cur, :] += dv_blk[QB:KB2]

        @pl.when(n > 0)
        def _():
            prev = pl.ds(pl.multiple_of((n - 1) * QB, QB), QB)
            dk_ref[prev, :] += dk_blk[0:QB]
            dv_ref[prev, :] += dv_blk[0:QB]

    q, g, kvc, kvp, bias, sinks_spec = _swa_specs()
    row = pl.BlockSpec((QB, D_RNN), lambda n: (n, 0))
    acc = pl.BlockSpec((S, LANE), lambda n: (0, 0))
    return pl.pallas_call(
        body,
        name="swa_bwd",
        grid=(N_QB,),
        in_specs=[row, q, g, kvc, kvp, row, bias, sinks_spec],
        out_specs=[pl.BlockSpec((QB, 2 * D_RNN), lambda n: (n, 0)), acc, acc, bias,
                   pl.BlockSpec((SWA_HEADS, LANE), lambda n: (0, 0))],
        out_shape=[jax.ShapeDtypeStruct((S, GROUP_TILES["B"] * LANE), BF16),
                   jax.ShapeDtypeStruct((S, LANE), F32), jax.ShapeDtypeStruct((S, LANE), F32),
                   jax.ShapeDtypeStruct((SWA_HEADS, QB, KB2), F32),
                   jax.ShapeDtypeStruct((SWA_HEADS, LANE), F32)],
        compiler_params=_params(("arbitrary",)),
    )(dy, p_b, p_b, p_b, p_b, o_swa, bias_t, sinks)


def _swa_pack(dp_b, dk, dv, dp_d, ts=512):
    def body(_, dk_ref, dv_ref, seam_ref, o_ref):
        dk16 = dk_ref[...].astype(BF16)
        o_ref[:, 0:LANE] = dk16
        o_ref[:, LANE:2 * LANE] = dk16
        o_ref[:, 2 * LANE:3 * LANE] = dv_ref[...].astype(BF16)
        o_ref[:, 3 * LANE:4 * LANE] = seam_ref[...]

    tile = pl.BlockSpec((ts, LANE), lambda i: (i, 0))
    return pl.pallas_call(
        body,
        name="swa_pack",
        grid=(S // ts,),
        in_specs=[pl.BlockSpec(memory_space=pl.ANY), tile, tile, pl.BlockSpec((ts, LANE), lambda i: (i, 23))],
        out_specs=pl.BlockSpec((ts, 4 * LANE), lambda i: (i, 4)),
        out_shape=jax.ShapeDtypeStruct(dp_b.shape, dp_b.dtype),
        input_output_aliases={0: 0},
        compiler_params=_params(("parallel",)),
    )(dp_b, dk, dv, dp_d)


def _split3(v):
    a = v.astype(BF16)
    r = v - a.astype(F32)
    b = r.astype(BF16)
    c = (r - b.astype(F32)).astype(BF16)
    return a, b, c


def _relbias_grad(dbias_flat, onehot_t):
    def body(d_ref, e_ref, o_ref):
        e = e_ref[...]
        acc = jnp.zeros((SWA_HEADS, REL_BUCKETS), F32)
        for term in _split3(d_ref[...]):
            acc = acc + lax.dot_general(term, e, _DIMS["nt"], preferred_element_type=F32)
        o_ref[...] = acc

    return pl.pallas_call(
        body,
        name="relbias_grad",
        out_shape=jax.ShapeDtypeStruct((SWA_HEADS, REL_BUCKETS), F32),
        compiler_params=_params(),
    )(dbias_flat, onehot_t)


TS_MEM = 512


def _mem_probs(q16, mk):
    lg = lax.dot_general(q16, mk, _DIMS["nt"], preferred_element_type=F32) * (MEM_HD ** -0.5)
    p = jnp.exp(lg - jnp.max(lg, axis=-1, keepdims=True))
    return p / jnp.sum(p, axis=-1, keepdims=True)


def _mem_fwd(p_c, mkv):
    def body(q_ref, g_ref, mkv_ref, y_ref, o_ref):
        for hm in range(MEM_HEADS):
            sl = slice(hm * MEM_HD, (hm + 1) * MEM_HD)
            probs = _mem_probs(q_ref[:, sl].astype(BF16), mkv_ref[:, sl])
            o = jnp.dot(probs.astype(BF16), mkv_ref[:, D_RNN + hm * MEM_HD:D_RNN + (hm + 1) * MEM_HD],
                        preferred_element_type=F32)
            o_ref[:, sl] = o
            g = g_ref[:, sl]
            y_ref[:, sl] = (o * (g * _sigmoid(g))).astype(BF16)

    blk = lambda c: pl.BlockSpec((TS_MEM, D_RNN), lambda i: (i, c))
    return pl.pallas_call(
        body,
        name="mem_fwd",
        grid=(S // TS_MEM,),
        in_specs=[blk(0), blk(1), pl.BlockSpec((MEM, 2 * D_RNN), lambda i: (0, 0))],
        out_specs=[blk(0), blk(0)],
        out_shape=[jax.ShapeDtypeStruct((S, D_RNN), BF16), jax.ShapeDtypeStruct((S, D_RNN), F32)],
        compiler_params=_params(("parallel",)),
    )(p_c, p_c, mkv)


def _mem_bwd(dy, p_c, o_mem, mkv):
    def body(dy_ref, q_ref, g_ref, o_ref, mkv_ref, dp_ref, dmkv_ref):
        @pl.when(pl.program_id(0) == 0)
        def _():
            dmkv_ref[...] = jnp.zeros_like(dmkv_ref)

        for hm in range(MEM_HEADS):
            sl = slice(hm * MEM_HD, (hm + 1) * MEM_HD)
            sv = slice(D_RNN + hm * MEM_HD, D_RNN + (hm + 1) * MEM_HD)
            q16 = q_ref[:, sl].astype(BF16)
            mk, mv = mkv_ref[:, sl], mkv_ref[:, sv]
            probs = _mem_probs(q16, mk)
            g, o, dyv = g_ref[:, sl], o_ref[:, sl], dy_ref[:, sl]
            sg = _sigmoid(g)
            do = dyv * (g * sg)
            dp_ref[:, sv] = (dyv * o * (sg * (1.0 + g * (1.0 - sg)))).astype(BF16)
            do16 = do.astype(BF16)
            delta = jnp.sum(do * o, axis=-1, keepdims=True)
            dpr = lax.dot_general(do16, mv, _DIMS["nt"], preferred_element_type=F32)
            ds16 = (probs * (dpr - delta) * (MEM_HD ** -0.5)).astype(BF16)
            dp_ref[:, sl] = jnp.dot(ds16, mk, preferred_element_type=F32).astype(BF16)
            dmkv_ref[:, sl] += lax.dot_general(ds16, q16, _DIMS["tn"], preferred_element_type=F32)
            dmkv_ref[:, sv] += lax.dot_general(probs.astype(BF16), do16, _DIMS["tn"], preferred_element_type=F32)

    blk = lambda c: pl.BlockSpec((TS_MEM, D_RNN), lambda i: (i, c))
    kv = pl.BlockSpec((MEM, 2 * D_RNN), lambda i: (0, 0))
    return pl.pallas_call(
        body,
        name="mem_bwd",
        grid=(S // TS_MEM,),
        in_specs=[blk(0), blk(0), blk(1), blk(0), kv],
        out_specs=[pl.BlockSpec((TS_MEM, 2 * D_RNN), lambda i: (i, 0)), kv],
        out_shape=[jax.ShapeDtypeStruct((S, 2 * D_RNN), BF16), jax.ShapeDtypeStruct((MEM, 2 * D_RNN), F32)],
        compiler_params=_params(("arbitrary",)),
    )(dy, p_c, p_c, o_mem, mkv)


TS_MRG = 512
TD_MRG = 512
N_DBLK = D // TD_MRG


def _seam_fix(p_d, p_b, ts=512):
    def body(d_ref, b_ref, o_ref):
        o_ref[...] = d_ref[...] + b_ref[...]

    return pl.pallas_call(
        body,
        name="seam_fix",
        grid=(S // ts,),
        in_specs=[pl.BlockSpec((ts, LANE), lambda i: (i, 23)), pl.BlockSpec((ts, LANE), lambda i: (i, 19))],
        out_specs=pl.BlockSpec((ts, LANE), lambda i: (i, 23)),
        out_shape=jax.ShapeDtypeStruct(p_d.shape, p_d.dtype),
        input_output_aliases={0: 0},
        compiler_params=_params(("parallel",)),
    )(p_d, p_b)


def _merge_fwd(z, p_d):
    def body(z0, z1, z2, g0, g1, g2, o_ref):
        o_ref[...] = (_sigmoid(g0[...]) * z0[...] + _sigmoid(g1[...]) * z1[...]
                      + _sigmoid(g2[...]) * z2[...]).astype(BF16)

    blk = pl.BlockSpec((TS_MRG, TD_MRG), lambda i, d: (i, d))
    gate = lambda b: pl.BlockSpec((TS_MRG, TD_MRG), lambda i, d: (i, b * N_DBLK + d))
    return pl.pallas_call(
        body,
        name="merge_fwd",
        grid=(S // TS_MRG, N_DBLK),
        in_specs=[blk, blk, blk, gate(0), gate(1), gate(2)],
        out_specs=blk,
        out_shape=jax.ShapeDtypeStruct((S, D), BF16),
        compiler_params=_params(("parallel", "parallel")),
    )(z[0], z[1], z[2], p_d, p_d, p_d)


def _merge_bwd(dmerged, z_b, p_d, b, dp_d):
    def body(dm_ref, z_ref, g_ref, *refs):
        dz_ref, dg_ref = refs[-2], refs[-1]
        sg = _sigmoid(g_ref[...])
        dm = dm_ref[...]
        dz_ref[...] = (dm * sg).astype(BF16)
        dg_ref[...] = (dm * z_ref[...] * sg * (1.0 - sg)).astype(BF16)

    blk = pl.BlockSpec((TS_MRG, TD_MRG), lambda i, d: (i, d))
    gate = pl.BlockSpec((TS_MRG, TD_MRG), lambda i, d: (i, b * N_DBLK + d))
    in_specs = [blk, blk, gate]
    args = [dmerged, z_b, p_d]
    aliases = {}
    if dp_d is not None:
        in_specs.append(pl.BlockSpec(memory_space=pl.ANY))
        args.append(dp_d)
        aliases = {3: 1}
    return pl.pallas_call(
        body,
        name=f"merge_bwd{b}",
        grid=(S // TS_MRG, N_DBLK),
        in_specs=in_specs,
        out_specs=[blk, gate],
        out_shape=[jax.ShapeDtypeStruct((S, D), BF16),
                   jax.ShapeDtypeStruct((S, GROUP_TILES["D"] * LANE), BF16)],
        input_output_aliases=aliases,
        compiler_params=_params(("parallel", "parallel")),
    )(*args)


def _bucket_table():
    import numpy as np
    qi = np.arange(QB)[:, None]
    kj = np.arange(KB2)[None, :]
    n = np.maximum(qi + WINDOW - kj, 0)
    max_exact = REL_BUCKETS // 2
    ratio = np.log(np.maximum(n, 1).astype(np.float32) / max_exact) / np.float32(math.log(REL_MAX_DIST / max_exact))
    large = np.minimum(max_exact + (ratio * (REL_BUCKETS - max_exact)).astype(np.int32), REL_BUCKETS - 1)
    bucket = np.where(n < max_exact, n, large).reshape(1, QB * KB2)
    return (bucket == np.arange(REL_BUCKETS)[:, None]).astype(np.float32)


def _bias_expand(rel_bias_t, onehot_t):
    def body(r_ref, e_ref, o_ref):
        e = e_ref[...]
        acc = jnp.zeros((SWA_HEADS, QB * KB2), F32)
        for term in _split3(r_ref[...]):
            acc = acc + jnp.dot(term, e, preferred_element_type=F32)
        o_ref[...] = acc

    return pl.pallas_call(
        body,
        name="bias_expand",
        out_shape=jax.ShapeDtypeStruct((SWA_HEADS, QB * KB2), F32),
        compiler_params=_params(),
    )(rel_bias_t, onehot_t)


PROJ_TN = {"A": 1024, "B": 1280, "C": 1024, "D": 1536}


def _local_step(x, mem, tgt, sp, w_grp, wmk, wbr, wout):
    onehot_t = jnp.asarray(_bucket_table(), BF16)
    bias_t = _bias_expand(sp["rel_bias"].T, onehot_t).reshape(SWA_HEADS, QB, KB2)
    sinks = sp["swa_sinks"].reshape(SWA_HEADS)
    wa16, wx16 = sp["w_rg_a"].astype(BF16), sp["w_rg_x"].astype(BF16)
    rnn = (sp["conv_w"], sp["conv_b"], wa16, sp["b_rg_a"], wx16, sp["b_rg_x"], sp["lru_lambda"])

    h = _rms_fwd(x, sp["pre_norm_g"], "rms_pre")
    p = {g: _mm(h, w_grp[g], "nn", F32, 1024, PROJ_TN[g], D, f"proj_{g}") for g in GROUPS}
    p["D"] = _seam_fix(p["D"], p["B"])
    memn = _rms_fwd(mem, sp["mem_norm_g"], "rms_mem")
    mkv = _mm(memn, wmk, "nn", BF16, MEM, 1024, D, "mkv")
    y_rg, hseq = _rglru_fwd(p["A"], *rnn)
    y_swa, o_swa = _swa_fwd(p["B"], bias_t, sinks)
    y_mem, o_mem = _mem_fwd(p["C"], mkv)
    ys = (y_rg, y_swa, y_mem)
    z = [_mm(ys[b], wbr[b], "nn", F32, 1024, 1024, D_RNN, f"branch_out{b}") for b in range(3)]
    merged = _merge_fwd(z, p["D"])
    out = _mm(merged, wout, "nn", F32, 1024, 1024, D, "out_proj")
    sq, dy, dout, d_post = _post_loss(out, x, tgt, sp["post_norm_g"])

    dmerged = _mm(dout, wout, "nt", F32, 1024, 1024, D, "d_merged")
    d_wout = _mm(merged, dout, "tn", BF16, 1024, 1024, S, "d_wout")
    dz, dp_d = [], None
    for b in range(3):
        dz_b, dp_d = _merge_bwd(dmerged, z[b], p["D"], b, dp_d)
        dz.append(dz_b)
    dys = [_mm(dz[b], wbr[b], "nt", F32, 1024, 1024, D, f"d_branch{b}") for b in range(3)]
    d_wbr = [_mm(ys[b], dz[b], "tn", BF16, 1024, 1024, S, f"d_wbr{b}") for b in range(3)]
    dp_a, d_cw, d_cb, d_wa, d_ba, d_wx, d_bx, d_lam = _rglru_bwd(dys[0], p["A"], hseq, *rnn)
    dp_b, dk, dv, d_bias, d_sink = _swa_bwd(dys[1], p["B"], o_swa, bias_t, sinks)
    dp_b = _swa_pack(dp_b, dk, dv, dp_d)
    d_rel = _relbias_grad(d_bias.reshape(SWA_HEADS, QB * KB2), onehot_t).T
    dp_c, dmkv = _mem_bwd(dys[2], p["C"], o_mem, mkv)
    dmkv16 = dmkv.astype(BF16)
    d_wmk = _mm(memn, dmkv16, "tn", BF16, 1024, 1024, MEM, "d_wmk")
    dmemn = _mm(dmkv16, wmk, "nt", F32, MEM, 1024, D, "d_memn")
    d_memg = _memnorm_bwd(dmemn, mem)
    dp = {"A": dp_a, "B": dp_b, "C": dp_c, "D": dp_d}
    dh = None
    for g in GROUPS:
        dh = _mm(dp[g], w_grp[g], "nt", F32, 1024, 1024, 2560 if g == "B" else 2048, f"d_h_{g}", acc=dh)
    d_w = {g: _mm(h, dp[g], "tn", BF16, 1024, PROJ_TN[g], S, f"d_win_{g}") for g in GROUPS}
    grad_x, d_pre = _pre_bwd(dh, x, dy, sp["pre_norm_g"])

    d_small = {
        "pre_norm_g": d_pre, "post_norm_g": d_post, "mem_norm_g": d_memg, "conv_w": d_cw, "conv_b": d_cb,
        "w_rg_a": d_wa, "b_rg_a": d_ba, "w_rg_x": d_wx, "b_rg_x": d_bx, "lru_lambda": d_lam,
        "swa_sinks": d_sink[:, 0].reshape(1, SWA_HEADS), "rel_bias": d_rel,
    }
    return sq, grad_x, d_small, d_w, d_wmk, d_wbr, d_wout


ANY = pl.BlockSpec(memory_space=pl.ANY)
SHARD_ROWS = D // N_CHIPS
GATHERED = {"A": (D, 2048), "B": (D, 2560), "C": (D, 2048), "D": (D, 6144), "mk": (D, D),
            "br0": (D_RNN, D), "br1": (D_RNN, D), "br2": (D_RNN, D), "out": (D, D)}
SHARDS = ("win", "mk", "br0", "br1", "br2", "out")


class Piece(NamedTuple):
    src: str
    dst: str
    rows: int
    sr0: int
    sc0: int
    dr0: int
    dc0: int
    ncols: int


def _pieces_of(jj):
    out = [Piece("win", g, D, 0, lt * LANE, 0, gt * LANE, nt * LANE) for lt, nt, g, gt in _window_runs(jj)]
    out.append(Piece("mk", "mk", SHARD_ROWS, 0, 0, SHARD_ROWS * jj, 0, D))
    out += [Piece(f"br{b}", f"br{b}", D_RNN, 0, 0, 0, SHARD_ROWS * jj, SHARD_ROWS) for b in range(3)]
    out.append(Piece("out", "out", SHARD_ROWS, 0, 0, SHARD_ROWS * jj, 0, D))
    return out


MAX_PIECES = max(len(_pieces_of(jj)) for jj in range(N_CHIPS))


def _rect(ref, r0, rows, c0, ncols):
    return ref.at[pl.ds(r0, rows), pl.ds(c0, ncols)]


def _position():
    x, y, c = lax.axis_index("x"), lax.axis_index("y"), lax.axis_index("c")
    return x, y, c, 2 * x + y


def _all_gather_weights(shards):
    names = tuple(GATHERED)

    def body(*refs):
        src = dict(zip(SHARDS, refs[:len(SHARDS)]))
        dst = dict(zip(names, refs[len(SHARDS):len(SHARDS) + len(names)]))
        send_sems, ici_sems, fwd_sems, d2d_sems, loc_sems = refs[len(SHARDS) + len(names):]
        x, y, c, j = _position()

        def src_half(p, which):
            return _rect(src[p.src], p.sr0 + which * (p.rows // 2), p.rows // 2, p.sc0, p.ncols)

        def dst_half(p, which):
            return _rect(dst[p.dst], p.dr0 + which * (p.rows // 2), p.rows // 2, p.dc0, p.ncols)

        def ici_copy(jj, i, p, kk):
            return pltpu.make_async_remote_copy(
                src_ref=src_half(p, c), dst_ref=dst_half(p, c), send_sem=send_sems.at[(i * N_CHIPS) + kk],
                recv_sem=ici_sems.at[jj * MAX_PIECES + i], device_id=(kk // 2, kk % 2, c), device_id_type=MESH)

        def fwd_copy(jj, i, p, which):
            return pltpu.make_async_remote_copy(
                src_ref=dst_half(p, which), dst_ref=dst_half(p, which), send_sem=fwd_sems.at[jj * MAX_PIECES + i],
                recv_sem=d2d_sems.at[jj * MAX_PIECES + i], device_id=(x, y, 1 - c), device_id_type=MESH)

        def loc_copy(i, p):
            return pltpu.make_async_copy(_rect(src[p.src], p.sr0, p.rows, p.sc0, p.ncols),
                                         _rect(dst[p.dst], p.dr0, p.rows, p.dc0, p.ncols), loc_sems.at[i])

        for jj in range(N_CHIPS):
            @pl.when(j == jj)
            def _():
                for i, p in enumerate(_pieces_of(jj)):
                    loc_copy(i, p).start()
                    for kk in range(N_CHIPS):
                        if kk != jj:
                            ici_copy(jj, i, p, kk).start()
        for jj in range(N_CHIPS):
            @pl.when(j != jj)
            def _():
                for i, p in enumerate(_pieces_of(jj)):
                    ici_copy(jj, i, p, jj).wait_recv()
                    fwd_copy(jj, i, p, c).start()
        for jj in range(N_CHIPS):
            @pl.when(j != jj)
            def _():
                for i, p in enumerate(_pieces_of(jj)):
                    fwd_copy(jj, i, p, 1 - c).wait_recv()
        for jj in range(N_CHIPS):
            @pl.when(j != jj)
            def _():
                for i, p in enumerate(_pieces_of(jj)):
                    fwd_copy(jj, i, p, c).wait_send()
        for jj in range(N_CHIPS):
            @pl.when(j == jj)
            def _():
                for i, p in enumerate(_pieces_of(jj)):
                    for kk in range(N_CHIPS):
                        if kk != jj:
                            ici_copy(jj, i, p, kk).wait_send()
                    loc_copy(i, p).wait()

    outs = pl.pallas_call(
        body,
        name="all_gather_weights",
        in_specs=[ANY] * len(SHARDS),
        out_specs=[ANY] * len(names),
        out_shape=[jax.ShapeDtypeStruct(GATHERED[n], BF16) for n in names],
        scratch_shapes=[pltpu.SemaphoreType.DMA((MAX_PIECES * N_CHIPS,)),
                        pltpu.SemaphoreType.DMA((MAX_PIECES * N_CHIPS,)),
                        pltpu.SemaphoreType.DMA((MAX_PIECES * N_CHIPS,)),
                        pltpu.SemaphoreType.DMA((MAX_PIECES * N_CHIPS,)),
                        pltpu.SemaphoreType.DMA((MAX_PIECES,))],
        compiler_params=pltpu.CompilerParams(has_side_effects=True),
    )(*[shards[n] for n in SHARDS])
    return dict(zip(names, outs))


def _half_rows(name):
    return GATHERED[name][0] // 2


def _row_blocks(name, which):
    if name in ("mk", "out"):
        q = SHARD_ROWS // 2
        return [(SHARD_ROWS * jj + q * which, q * jj, q) for jj in range(N_CHIPS)]
    h = _half_rows(name)
    return [(h * which, 0, h)]


def _swap_halves(grads):
    names = tuple(GATHERED)
    n_tr = sum(len(_row_blocks(n, 0)) for n in names)

    def body(*refs):
        src = dict(zip(names, refs[:len(names)]))
        dst = dict(zip(names, refs[len(names):2 * len(names)]))
        send_sems, recv_sems = refs[2 * len(names):]
        x, y, c, _ = _position()
        copies = []
        for n in names:
            for r0, h0, rows in _row_blocks(n, 1 - c):
                k = len(copies)
                copies.append(pltpu.make_async_remote_copy(
                    src_ref=src[n].at[pl.ds(r0, rows)], dst_ref=dst[n].at[pl.ds(h0, rows)],
                    send_sem=send_sems.at[k], recv_sem=recv_sems.at[k],
                    device_id=(x, y, 1 - c), device_id_type=MESH))
        for cp in copies:
            cp.start()
        for cp in copies:
            cp.wait_recv()
        for cp in copies:
            cp.wait_send()

    outs = pl.pallas_call(
        body,
        name="swap_halves",
        in_specs=[ANY] * len(names),
        out_specs=[ANY] * len(names),
        out_shape=[jax.ShapeDtypeStruct((_half_rows(n), GATHERED[n][1]), BF16) for n in names],
        scratch_shapes=[pltpu.SemaphoreType.DMA((n_tr,)), pltpu.SemaphoreType.DMA((n_tr,))],
        compiler_params=pltpu.CompilerParams(has_side_effects=True),
    )(*[grads[n] for n in names])
    return dict(zip(names, outs))


ADD_ROWS = 256


def _add_half(full, recv, c_arr, name):
    rows, cols = recv.shape
    if name in ("mk", "out"):
        index = lambda i, c_ref: (2 * i + c_ref[0], 0)
    else:
        nb = rows // ADD_ROWS
        index = lambda i, c_ref: (nb * c_ref[0] + i, 0)

    def body(c_ref, a_ref, b_ref, o_ref):
        o_ref[...] = (a_ref[...].astype(F32) + b_ref[...].astype(F32)).astype(BF16)

    return pl.pallas_call(
        body,
        name=f"add_half_{name}",
        grid_spec=pltpu.PrefetchScalarGridSpec(
            num_scalar_prefetch=1,
            grid=(rows // ADD_ROWS,),
            in_specs=[pl.BlockSpec((ADD_ROWS, cols), index), pl.BlockSpec((ADD_ROWS, cols), lambda i, c_ref: (i, 0))],
            out_specs=pl.BlockSpec((ADD_ROWS, cols), lambda i, c_ref: (i, 0)),
        ),
        out_shape=jax.ShapeDtypeStruct((rows, cols), BF16),
        compiler_params=_params(("parallel",)),
    )(c_arr, full, recv)


SLOT_SHAPES = {"win": (D // 2, WIN), "mk": (SHARD_ROWS // 2, D), "br0": (D_RNN // 2, SHARD_ROWS),
               "br1": (D_RNN // 2, SHARD_ROWS), "br2": (D_RNN // 2, SHARD_ROWS), "out": (SHARD_ROWS // 2, D)}


def _scatter_to_owners(halves):
    names = tuple(GATHERED)

    def body(*refs):
        src = dict(zip(names, refs[:len(names)]))
        dst = dict(zip(SHARDS, refs[len(names):len(names) + len(SHARDS)]))
        send_sems, recv_sems, loc_sems = refs[len(names) + len(SHARDS):]
        x, y, c, j = _position()

        def src_rect(p):
            if p.dst in ("mk", "out"):
                return _rect(src[p.dst], p.dr0 // 2, p.rows // 2, p.dc0, p.ncols)
            return _rect(src[p.dst], 0, p.rows // 2, p.dc0, p.ncols)

        def dst_rect(p, slot):
            return _rect(dst[p.src].at[slot], 0, p.rows // 2, p.sc0, p.ncols)

        def remote(jj, kk, i, p):
            return pltpu.make_async_remote_copy(
                src_ref=src_rect(p), dst_ref=dst_rect(p, jj), send_sem=send_sems.at[kk * MAX_PIECES + i],
                recv_sem=recv_sems.at[jj * MAX_PIECES + i], device_id=(kk // 2, kk % 2, c), device_id_type=MESH)

        def local(jj, i, p):
            return pltpu.make_async_copy(src_rect(p), dst_rect(p, jj), loc_sems.at[i])

        for jj in range(N_CHIPS):
            @pl.when(j == jj)
            def _():
                for kk in range(N_CHIPS):
                    for i, p in enumerate(_pieces_of(kk)):
                        (local(jj, i, p) if kk == jj else remote(jj, kk, i, p)).start()
                for ss in range(N_CHIPS):
                    if ss != jj:
                        for i, p in enumerate(_pieces_of(jj)):
                            remote(ss, jj, i, p).wait_recv()
                for kk in range(N_CHIPS):
                    for i, p in enumerate(_pieces_of(kk)):
                        if kk == jj:
                            local(jj, i, p).wait()
                        else:
                            remote(jj, kk, i, p).wait_send()

    outs = pl.pallas_call(
        body,
        name="scatter_to_owners",
        in_specs=[ANY] * len(names),
        out_specs=[ANY] * len(SHARDS),
        out_shape=[jax.ShapeDtypeStruct((N_CHIPS,) + SLOT_SHAPES[n], BF16) for n in SHARDS],
        scratch_shapes=[pltpu.SemaphoreType.DMA((MAX_PIECES * N_CHIPS,)),
                        pltpu.SemaphoreType.DMA((MAX_PIECES * N_CHIPS,)),
                        pltpu.SemaphoreType.DMA((MAX_PIECES,))],
        compiler_params=pltpu.CompilerParams(has_side_effects=True),
    )(*[halves[n] for n in names])
    return dict(zip(SHARDS, outs))


def _sum_slots(slots, c_arr, name, tr=128):
    _, rows, cols = slots.shape

    def body(c_ref, s_ref, o_ref):
        acc = s_ref[0].astype(F32)
        for k in range(1, N_CHIPS):
            acc = acc + s_ref[k].astype(F32)
        o_ref[...] = acc

    return pl.pallas_call(
        body,
        name=f"sum_slots_{name}",
        grid_spec=pltpu.PrefetchScalarGridSpec(
            num_scalar_prefetch=1,
            grid=(rows // tr,),
            in_specs=[pl.BlockSpec((N_CHIPS, tr, cols), lambda i, c_ref: (0, i, 0))],
            out_specs=pl.BlockSpec((None, tr, cols), lambda i, c_ref: (c_ref[0], i, 0)),
        ),
        out_shape=jax.ShapeDtypeStruct((2, rows, cols), F32),
        compiler_params=_params(("parallel",)),
    )(c_arr, slots)


def _share_sums(sums):
    def body(*refs):
        bufs = refs[len(SHARDS):2 * len(SHARDS)]
        send_sems, recv_sems = refs[2 * len(SHARDS):]
        x, y, c, _ = _position()
        copies = [pltpu.make_async_remote_copy(
            src_ref=b.at[c], dst_ref=b.at[c], send_sem=send_sems.at[k], recv_sem=recv_sems.at[k],
            device_id=(x, y, 1 - c), device_id_type=MESH) for k, b in enumerate(bufs)]
        for cp in copies:
            cp.start()
        for cp in copies:
            cp.wait_recv()
        for cp in copies:
            cp.wait_send()

    outs = pl.pallas_call(
        body,
        name="share_sums",
        in_specs=[ANY] * len(SHARDS),
        out_specs=[ANY] * len(SHARDS),
        out_shape=[jax.ShapeDtypeStruct(sums[n].shape, F32) for n in SHARDS],
        input_output_aliases={k: k for k in range(len(SHARDS))},
        scratch_shapes=[pltpu.SemaphoreType.DMA((len(SHARDS),)), pltpu.SemaphoreType.DMA((len(SHARDS),))],
        compiler_params=pltpu.CompilerParams(has_side_effects=True),
    )(*[sums[n] for n in SHARDS])
    return dict(zip(SHARDS, outs))


N_DEV = 8


def _all_reduce_small(pack, name):
    rows = pack.shape[0]

    def body(p_ref, o_ref, land, send_sems, recv_sems):
        x, y, c, _ = _position()
        me = 4 * x + 2 * y + c

        def copy(o):
            return pltpu.make_async_remote_copy(
                src_ref=p_ref, dst_ref=land.at[me], send_sem=send_sems.at[o], recv_sem=recv_sems.at[me],
                device_id=(o // 4, (o // 2) % 2, o % 2), device_id_type=MESH)

        def arrival(o):
            return pltpu.make_async_remote_copy(
                src_ref=p_ref, dst_ref=land.at[o], send_sem=send_sems.at[o], recv_sem=recv_sems.at[o],
                device_id=(o // 4, (o // 2) % 2, o % 2), device_id_type=MESH)

        for o in range(N_DEV):
            @pl.when(me != o)
            def _():
                copy(o).start()
        land[me] = p_ref[...]
        for o in range(N_DEV):
            @pl.when(me != o)
            def _():
                arrival(o).wait_recv()
        acc = land[0]
        for o in range(1, N_DEV):
            acc = acc + land[o]
        o_ref[...] = acc
        for o in range(N_DEV):
            @pl.when(me != o)
            def _():
                copy(o).wait_send()

    vmem = pl.BlockSpec(memory_space=pltpu.VMEM)
    return pl.pallas_call(
        body,
        name=name,
        in_specs=[vmem],
        out_specs=vmem,
        out_shape=jax.ShapeDtypeStruct((rows, LANE), F32),
        scratch_shapes=[pltpu.VMEM((N_DEV, rows, LANE), F32), pltpu.SemaphoreType.DMA((N_DEV,)),
                        pltpu.SemaphoreType.DMA((N_DEV,))],
        compiler_params=pltpu.CompilerParams(has_side_effects=True, vmem_limit_bytes=VMEM_LIMIT),
    )(pack)


def _adamw(w, g, m, v, name, tr):
    rows, cols = w.shape
    tr = min(tr, rows)

    def body(w_ref, g_ref, m_ref, v_ref, d_ref, nm_ref, nv_ref):
        gv = g_ref[...]
        nm = ADAM_B1 * m_ref[...] + (1.0 - ADAM_B1) * gv
        nv = ADAM_B2 * v_ref[...] + (1.0 - ADAM_B2) * (gv * gv)
        nm_ref[...] = nm
        nv_ref[...] = nv
        m_hat = nm / (1.0 - ADAM_B1 ** ADAM_STEP)
        v_hat = nv / (1.0 - ADAM_B2 ** ADAM_STEP)
        d_ref[...] = -ADAM_LR * (m_hat / (jnp.sqrt(v_hat) + ADAM_EPS) + ADAM_WD * w_ref[...])

    blk = pl.BlockSpec((tr, cols), lambda i: (i, 0))
    shape = jax.ShapeDtypeStruct((rows, cols), F32)
    return pl.pallas_call(
        body,
        name=f"adamw_{name}",
        grid=(rows // tr,),
        in_specs=[blk] * 4,
        out_specs=[blk] * 3,
        out_shape=[shape] * 3,
        compiler_params=_params(("parallel",)),
    )(w, g, m, v)


SMALL = (("pre_norm_g", (1, D)), ("post_norm_g", (1, D)), ("mem_norm_g", (1, D)), ("conv_w", (CONV_W, D_RNN)),
         ("conv_b", (1, D_RNN)), ("w_rg_a", (RNN_BLOCKS, LANE, LANE)), ("b_rg_a", (1, D_RNN)),
         ("w_rg_x", (RNN_BLOCKS, LANE, LANE)), ("b_rg_x", (1, D_RNN)), ("lru_lambda", (1, D_RNN)),
         ("swa_sinks", (1, SWA_HEADS)), ("rel_bias", (REL_BUCKETS, SWA_HEADS)))
PACK_ROWS = 2176


def _slot_len(shape):
    return -(-math.prod(shape) // LANE) * LANE


def _pack(values):
    parts = []
    for name, shape in SMALL:
        flat = values[name].reshape(-1).astype(F32)
        parts.append(jnp.pad(flat, (0, _slot_len(shape) - flat.shape[0])))
    flat = jnp.concatenate(parts)
    return jnp.pad(flat, (0, PACK_ROWS * LANE - flat.shape[0])).reshape(PACK_ROWS, LANE)


def _unpack(pack, shapes=None):
    flat = pack.reshape(-1)
    out, off = {}, 0
    for name, shape in SMALL:
        shp = shape if shapes is None or name not in shapes else shapes[name]
        out[name] = flat[off:off + math.prod(shp)].reshape(shp)
        off += _slot_len(shape)
    return out


TWIN_WEIGHTS = ("pre_norm_g", "post_norm_g", "mem_norm_g", "w_in", "conv_w", "conv_b", "w_rg_a", "b_rg_a", "w_rg_x",
                "b_rg_x", "lru_lambda", "swa_sinks", "rel_bias", "w_mem_kv", "w_br_rg", "w_br_swa", "w_br_mem", "w_out")
BIG = {"w_in": "win", "w_mem_kv": "mk", "w_br_rg": "br0", "w_br_swa": "br1", "w_br_mem": "br2", "w_out": "out"}


def kernel(x, mem, pre_norm_g, post_norm_g, mem_norm_g, w_in, conv_w, conv_b, w_rg_a, b_rg_a, w_rg_x, b_rg_x, lru_lambda, swa_sinks, rel_bias, w_mem_kv, w_br_rg, w_br_swa, w_br_mem, w_out, loss_target, m_pre_norm_g, m_post_norm_g, m_mem_norm_g, m_w_in, m_conv_w, m_conv_b, m_w_rg_a, m_b_rg_a, m_w_rg_x, m_b_rg_x, m_lru_lambda, m_swa_sinks, m_rel_bias, m_w_mem_kv, m_w_br_rg, m_w_br_swa, m_w_br_mem, m_w_out, v_pre_norm_g, v_post_norm_g, v_mem_norm_g, v_w_in, v_conv_w, v_conv_b, v_w_rg_a, v_b_rg_a, v_w_rg_x, v_b_rg_x, v_lru_lambda, v_swa_sinks, v_rel_bias, v_w_mem_kv, v_w_br_rg, v_w_br_swa, v_w_br_mem, v_w_out):
    args = dict(locals())
    out_shapes = {n: args[n].shape for n in TWIN_WEIGHTS}
    w = {n: (args[n] if n == "rel_bias" else args[n][0]) for n in TWIN_WEIGHTS}
    m = {n: (args["m_" + n] if n == "rel_bias" else args["m_" + n][0]) for n in TWIN_WEIGHTS}
    v = {n: (args["v_" + n] if n == "rel_bias" else args["v_" + n][0]) for n in TWIN_WEIGHTS}
    for d in (w, m, v):
        for n, shape in SMALL:
            if n != "conv_w":
                d[n] = d[n].reshape(shape)

    xi, yi, ci = lax.axis_index("x"), lax.axis_index("y"), lax.axis_index("c")
    chip = 2 * xi + yi
    c_arr = ci.astype(jnp.int32).reshape(1)
    col0 = (SWA_HD * yi).astype(jnp.int32)
    zero = jnp.zeros((), jnp.int32)
    cw0 = (chip * (D_RNN // N_CHIPS)).astype(jnp.int32)

    def to_window(a, dtype):
        return lax.dynamic_update_slice(jnp.zeros((D, WIN), dtype), a.astype(dtype), (zero, col0))

    def from_window(a):
        return lax.dynamic_slice(a, (zero, col0), (D, SHARD))

    placed = lax.dynamic_update_slice(jnp.zeros((CONV_W, D_RNN), F32), w["conv_w"], (zero, cw0))
    placed = jnp.where(ci == 0, placed, 0.0).reshape(CONV_W * D_RNN // LANE, LANE)
    conv_w_full = _all_reduce_small(placed, "gather_conv_w").reshape(CONV_W, D_RNN)

    shards = {"win": to_window(w["w_in"], BF16)}
    for n, s in BIG.items():
        if n != "w_in":
            shards[s] = w[n].astype(BF16)
    gathered = _all_gather_weights(shards)

    sp = {n: w[n] for n, _ in SMALL}
    sp["conv_w"] = conv_w_full
    sq, grad_x, d_small, d_w, d_wmk, d_wbr, d_wout = _local_step(
        x[0], mem[0], loss_target[0], sp, {g: gathered[g] for g in GROUPS}, gathered["mk"],
        [gathered["br0"], gathered["br1"], gathered["br2"]], gathered["out"])
    loss = lax.psum(sq[0, 0] * (0.5 / D), ("x", "y", "c"))

    grads = dict(d_w)
    grads.update({"mk": d_wmk, "br0": d_wbr[0], "br1": d_wbr[1], "br2": d_wbr[2], "out": d_wout})
    received = _swap_halves(grads)
    halves = {n: _add_half(grads[n], received[n], c_arr, n) for n in GATHERED}
    slots = _scatter_to_owners(halves)
    sums = _share_sums({n: _sum_slots(slots[n], c_arr, n) for n in SHARDS})
    g_big = {n: sums[s].reshape(2 * SLOT_SHAPES[s][0], SLOT_SHAPES[s][1]) for n, s in BIG.items()}

    g_small = _unpack(_all_reduce_small(_pack(d_small), "all_reduce_small"))
    g_small["conv_w"] = lax.dynamic_slice(g_small["conv_w"], (zero, cw0), (CONV_W, D_RNN // N_CHIPS))

    grad, delta, new_m, new_v = {}, {}, {}, {}
    for n, s in BIG.items():
        if n == "w_in":
            d_, m_, v_ = _adamw(to_window(w[n], F32), g_big[n], to_window(m[n], F32), to_window(v[n], F32), s, 128)
            grad[n], delta[n], new_m[n], new_v[n] = (from_window(a) for a in (g_big[n], d_, m_, v_))
        else:
            grad[n] = g_big[n]
            delta[n], new_m[n], new_v[n] = _adamw(w[n], g_big[n], m[n], v[n], s, 128)
    d_, m_, v_ = _adamw(_pack(w), _pack(g_small), _pack(m), _pack(v), "small", PACK_ROWS)
    shard_shapes = {"conv_w": (CONV_W, D_RNN // N_CHIPS)}
    d_, m_, v_ = (_unpack(a, shard_shapes) for a in (d_, m_, v_))
    for n, _ in SMALL:
        grad[n], delta[n], new_m[n], new_v[n] = g_small[n], d_[n], m_[n], v_[n]

    outs = [loss, grad_x.reshape(1, S, D)]
    for group in (grad, delta, new_m, new_v):
        outs += [group[n].reshape(out_shapes[n]) for n in TWIN_WEIGHTS]
    return tuple(outs)
```

```python
import functools
import math
from typing import NamedTuple

import jax
import jax.numpy as jnp
from jax import lax
from jax.experimental import pallas as pl
from jax.experimental.pallas import tpu as pltpu

F32 = jnp.float32
BF16 = jnp.bfloat16
MESH = pl.DeviceIdType.MESH

S = 2048
D = 2048
MEM = 256
D_RNN = 1024
RNN_BLOCKS = 8
CONV_W = 4
LRU_C = 8.0
SWA_HEADS = 16
SWA_HD = 64
WINDOW = 128
MEM_HEADS = 4
MEM_HD = 256
REL_BUCKETS = 32
REL_MAX_DIST = 128
EPS = 1e-6
NEG_INF = -1e30
LANE = 128
SHARD = 3136
WIN = 3200
N_CHIPS = 4
VMEM_LIMIT = 56 * 1024 * 1024

ADAM_LR = 0.001
ADAM_B1 = 0.9
ADAM_B2 = 0.999
ADAM_EPS = 1e-08
ADAM_WD = 0.01
ADAM_STEP = 10

GROUP_TILES = {"A": 16, "B": 20, "C": 16, "D": 48}
GROUPS = ("A", "B", "C", "D")


def _params(sem=None):
    return pltpu.CompilerParams(dimension_semantics=sem, vmem_limit_bytes=VMEM_LIMIT)


def _sigmoid(v):
    return jax.nn.sigmoid(v)


def _window_tile_home(t):
    if t < 16:
        return "A", t
    if t < 24:
        return "B", t - 16
    if t < 27:
        return "B", t - 24 + 16
    if t < 35:
        return "B", t - 27 + 8
    if t < 51:
        return "C", t - 35
    if t < 75:
        return "D", t - 51
    if t == 75:
        return "B", 19
    return "D", t - 76 + 24


def _window_runs(j):
    runs = []
    for lt in range(WIN // LANE):
        g, gt = _window_tile_home(25 * j + lt)
        if runs and runs[-1][2] == g and runs[-1][3] + runs[-1][1] == gt:
            runs[-1][1] += 1
        else:
            runs.append([lt, 1, g, gt])
    return [tuple(r) for r in runs]


_DIMS = {
    "nn": (((1,), (0,)), ((), ())),
    "nt": (((1,), (1,)), ((), ())),
    "tn": (((0,), (0,)), ((), ())),
}


def _mm(a, b, mode, out_dtype, tm, tn, tk, name, acc=None):
    if mode == "nn":
        (m, k), n = a.shape, b.shape[1]
    elif mode == "nt":
        (m, k), n = a.shape, b.shape[0]
    else:
        (k, m), n = a.shape, b.shape[1]
    tm, tn, tk = min(tm, m), min(tn, n), min(tk, k)
    assert m % tm == 0 and n % tn == 0 and k % tk == 0, (name, m, n, k)
    nk = k // tk
    has_acc = acc is not None

    def body(*refs):
        a_ref, b_ref = refs[0], refs[1]
        o_ref = refs[3] if has_acc else refs[2]
        p = lax.dot_general(a_ref[...], b_ref[...], _DIMS[mode], preferred_element_type=F32)

        def finish(v):
            if has_acc:
                v = v + refs[2][...]
            o_ref[...] = v.astype(out_dtype)

        if nk == 1:
            finish(p)
        else:
            s_ref = refs[-1]
            kk = pl.program_id(2)

            @pl.when(kk == 0)
            def _():
                s_ref[...] = p

            @pl.when(kk > 0)
            def _():
                s_ref[...] += p

            @pl.when(kk == nk - 1)
            def _():
                finish(s_ref[...])

    if mode == "nn":
        a_spec = pl.BlockSpec((tm, tk), lambda i, j, kk: (i, kk))
        b_spec = pl.BlockSpec((tk, tn), lambda i, j, kk: (kk, j))
    elif mode == "nt":
        a_spec = pl.BlockSpec((tm, tk), lambda i, j, kk: (i, kk))
        b_spec = pl.BlockSpec((tn, tk), lambda i, j, kk: (j, kk))
    else:
        a_spec = pl.BlockSpec((tk, tm), lambda i, j, kk: (kk, i))
        b_spec = pl.BlockSpec((tk, tn), lambda i, j, kk: (kk, j))
    o_spec = pl.BlockSpec((tm, tn), lambda i, j, kk: (i, j))
    in_specs = [a_spec, b_spec] + ([o_spec] if has_acc else [])
    args = (a, b) + ((acc,) if has_acc else ())
    return pl.pallas_call(
        body,
        name=name,
        grid=(m // tm, n // tn, nk),
        in_specs=in_specs,
        out_specs=o_spec,
        out_shape=jax.ShapeDtypeStruct((m, n), out_dtype),
        scratch_shapes=[pltpu.VMEM((tm, tn), F32)] if nk > 1 else [],
        compiler_params=_params(("parallel", "parallel", "arbitrary")),
    )(*args)


def _rms_fwd(x, g, name, ts=256):
    r, d = x.shape

    def body(x_ref, g_ref, o_ref):
        xv = x_ref[...]
        inv = lax.rsqrt(jnp.mean(xv * xv, axis=-1, keepdims=True) + EPS)
        o_ref[...] = (xv * inv * g_ref[...]).astype(BF16)

    return pl.pallas_call(
        body,
        name=name,
        grid=(r // ts,),
        in_specs=[pl.BlockSpec((ts, d), lambda i: (i, 0)), pl.BlockSpec((1, d), lambda i: (0, 0))],
        out_specs=pl.BlockSpec((ts, d), lambda i: (i, 0)),
        out_shape=jax.ShapeDtypeStruct((r, d), BF16),
        compiler_params=_params(("parallel",)),
    )(x, g)


def _post_loss(out, x, tgt, g_post, ts=256):
    n = S // ts

    def body(o_ref, x_ref, t_ref, g_ref, sq_ref, dy_ref, do_ref, dg_ref):
        i = pl.program_id(0)

        @pl.when(i == 0)
        def _():
            sq_ref[...] = jnp.zeros_like(sq_ref)
            dg_ref[...] = jnp.zeros_like(dg_ref)

        ov = o_ref[...]
        g = g_ref[...]
        inv = lax.rsqrt(jnp.mean(ov * ov, axis=-1, keepdims=True) + EPS)
        on = ov * inv
        err = x_ref[...] + on * g - t_ref[...]
        sq_ref[...] += jnp.sum(err * err)
        dy = err * (1.0 / D)
        dy_ref[...] = dy
        dg_ref[...] += jnp.sum(dy * on, axis=0, keepdims=True)
        don = dy * g
        do_ref[...] = (inv * (don - on * jnp.mean(don * on, axis=-1, keepdims=True))).astype(BF16)

    row = pl.BlockSpec((ts, D), lambda i: (i, 0))
    vec = pl.BlockSpec((1, D), lambda i: (0, 0))
    return pl.pallas_call(
        body,
        name="post_loss",
        grid=(n,),
        in_specs=[row, row, row, vec],
        out_specs=[pl.BlockSpec((8, LANE), lambda i: (0, 0)), row, row, vec],
        out_shape=[
            jax.ShapeDtypeStruct((8, LANE), F32),
            jax.ShapeDtypeStruct((S, D), F32),
            jax.ShapeDtypeStruct((S, D), BF16),
            jax.ShapeDtypeStruct((1, D), F32),
        ],
        compiler_params=_params(("arbitrary",)),
    )(out, x, tgt, g_post)


def _pre_bwd(dh, x, dy, g_pre, ts=256):
    n = S // ts

    def body(dh_ref, x_ref, dy_ref, g_ref, gx_ref, dg_ref):
        i = pl.program_id(0)

        @pl.when(i == 0)
        def _():
            dg_ref[...] = jnp.zeros_like(dg_ref)

        xv = x_ref[...]
        dhv = dh_ref[...]
        inv = lax.rsqrt(jnp.mean(xv * xv, axis=-1, keepdims=True) + EPS)
        xn = xv * inv
        dg_ref[...] += jnp.sum(dhv * xn, axis=0, keepdims=True)
        dxn = dhv * g_ref[...]
        gx_ref[...] = dy_ref[...] + inv * (dxn - xn * jnp.mean(dxn * xn, axis=-1, keepdims=True))

    row = pl.BlockSpec((ts, D), lambda i: (i, 0))
    vec = pl.BlockSpec((1, D), lambda i: (0, 0))
    return pl.pallas_call(
        body,
        name="pre_bwd",
        grid=(n,),
        in_specs=[row, row, row, vec],
        out_specs=[row, vec],
        out_shape=[jax.ShapeDtypeStruct((S, D), F32), jax.ShapeDtypeStruct((1, D), F32)],
        compiler_params=_params(("arbitrary",)),
    )(dh, x, dy, g_pre)


def _memnorm_bwd(dmemn, mem):
    def body(d_ref, m_ref, dg_ref):
        mv = m_ref[...]
        inv = lax.rsqrt(jnp.mean(mv * mv, axis=-1, keepdims=True) + EPS)
        dg_ref[...] = jnp.sum(d_ref[...] * mv * inv, axis=0, keepdims=True)

    return pl.pallas_call(
        body,
        name="memnorm_bwd",
        out_shape=jax.ShapeDtypeStruct((1, D), F32),
        compiler_params=_params(),
    )(dmemn, mem)


T_RNN = 256


def _neg_expm1(z):
    poly = -z * (1.0 + z * (0.5 + z * (1.0 / 6 + z * (1.0 / 24 + z * (1.0 / 120 + z * (1.0 / 720))))))
    return jnp.where(z > -0.1, poly, 1.0 - jnp.exp(z))


def _softplus_neg(lam):
    return jnp.maximum(-lam, 0.0) + jnp.log1p(jnp.exp(-jnp.abs(lam)))


def _rnn_gates(conv, wa_ref, ba, wx_ref, bx, lam, first_row):
    cbf = conv.astype(BF16)
    ga, gx = [], []
    for n in range(RNN_BLOCKS):
        c_n = cbf[:, n * LANE:(n + 1) * LANE]
        ga.append(jnp.dot(c_n, wa_ref[n], preferred_element_type=F32))
        gx.append(jnp.dot(c_n, wx_ref[n], preferred_element_type=F32))
    gate_r = _sigmoid(jnp.concatenate(ga, axis=1) + ba)
    gate_i = _sigmoid(jnp.concatenate(gx, axis=1) + bx)
    sp = _softplus_neg(lam)
    log_a = -LRU_C * gate_r * sp
    a = jnp.exp(log_a)
    mult_raw = jnp.sqrt(_neg_expm1(2.0 * log_a))
    mult = jnp.where(first_row, 1.0, mult_raw)
    return cbf, gate_r, gate_i, sp, a, mult_raw, mult


def _rglru_fwd(p_a, conv_w, conv_b, wa, ba, wx, bx, lam):
    t = T_RNN
    n = S // t

    def body(xr_ref, g_ref, cw_ref, cb_ref, wa_ref, ba_ref, wx_ref, bx_ref, lam_ref,
             y_ref, h_ref, xp_s, hcar, a_s, b_s):
        i = pl.program_id(0)

        @pl.when(i == 0)
        def _():
            xp_s[0:8, :] = jnp.zeros((8, D_RNN), F32)
            hcar[...] = jnp.zeros_like(hcar)

        @pl.when(i > 0)
        def _():
            xp_s[0:8, :] = xp_s[t:t + 8, :]

        xp_s[8:8 + t, :] = xr_ref[...]
        conv = cb_ref[...]
        for k in range(CONV_W):
            conv = conv + cw_ref[k:k + 1, :] * xp_s[8 - k:8 - k + t, :]
        rows = i * t + lax.broadcasted_iota(jnp.int32, (t, 1), 0)
        _, _, gate_i, _, a, _, mult = _rnn_gates(
            conv, wa_ref, ba_ref[...], wx_ref, bx_ref[...], lam_ref[...], rows == 0)
        a_s[...] = a
        b_s[...] = mult * gate_i * conv

        def step(tt, h):
            h = a_s[pl.ds(tt, 1), :] * h + b_s[pl.ds(tt, 1), :]
            h_ref[pl.ds(tt, 1), :] = h
            return h

        hcar[...] = lax.fori_loop(0, t, step, hcar[...], unroll=8)
        g = g_ref[...]
        y_ref[...] = (h_ref[...] * (g * _sigmoid(g))).astype(BF16)

    blk = lambda c: pl.BlockSpec((t, D_RNN), lambda i: (i, c))
    full = lambda shape: pl.BlockSpec(shape, lambda i: (0,) * len(shape))
    return pl.pallas_call(
        body,
        name="rglru_fwd",
        grid=(n,),
        in_specs=[blk(0), blk(1), full((CONV_W, D_RNN)), full((1, D_RNN)),
                  full((RNN_BLOCKS, LANE, LANE)), full((1, D_RNN)),
                  full((RNN_BLOCKS, LANE, LANE)), full((1, D_RNN)), full((1, D_RNN))],
        out_specs=[blk(0), blk(0)],
        out_shape=[jax.ShapeDtypeStruct((S, D_RNN), BF16), jax.ShapeDtypeStruct((S, D_RNN), F32)],
        scratch_shapes=[pltpu.VMEM((t + 8, D_RNN), F32), pltpu.VMEM((1, D_RNN), F32),
                        pltpu.VMEM((t, D_RNN), F32), pltpu.VMEM((t, D_RNN), F32)],
        compiler_params=_params(("arbitrary",)),
    )(p_a, p_a, conv_w, conv_b, wa, ba, wx, bx, lam)


def _rglru_bwd(dy, p_a, hseq, conv_w, conv_b, wa, ba, wx, bx, lam):
    t = T_RNN
    n = S // t
    rb = t // 8

    def body(dy_ref, xr_ref, g_ref, h_ref, xrp_ref, hp_ref, cw_ref, cb_ref, wa_ref, ba_ref, wx_ref, bx_ref, lam_ref,
             dp_ref, dcw_ref, dcb_ref, dwa_ref, dba_ref, dwx_ref, dbx_ref, dlam_ref,
             xp_s, hp_s, dxp_s, lamcar, a_s, dh_s, lam_s):
        i = pl.program_id(0)
        r = n - 1 - i

        @pl.when(i == 0)
        def _():
            for ref in (dcw_ref, dcb_ref, dwa_ref, dba_ref, dwx_ref, dbx_ref, dlam_ref, lamcar):
                ref[...] = jnp.zeros_like(ref)
            dxp_s[t:t + 8, :] = jnp.zeros((8, D_RNN), F32)

        @pl.when(i > 0)
        def _():
            dxp_s[t:t + 8, :] = dxp_s[0:8, :]

        has_prev = r > 0
        xp_s[0:8, :] = jnp.where(has_prev, xrp_ref[...], 0.0)
        xp_s[8:8 + t, :] = xr_ref[...]
        hp_s[0:8, :] = jnp.where(has_prev, hp_ref[...], 0.0)
        hp_s[8:8 + t, :] = h_ref[...]
        xs = [xp_s[8 - k:8 - k + t, :] for k in range(CONV_W)]
        conv = cb_ref[...]
        for k in range(CONV_W):
            conv = conv + cw_ref[k:k + 1, :] * xs[k]
        rows = r * t + lax.broadcasted_iota(jnp.int32, (t, 1), 0)
        first = rows == 0
        lam_p = lam_ref[...]
        cbf, gate_r, gate_i, sp, a, mult_raw, mult = _rnn_gates(
            conv, wa_ref, ba_ref[...], wx_ref, bx_ref[...], lam_p, first)

        g = g_ref[...]
        sg = _sigmoid(g)
        dyv = dy_ref[...]
        a_s[...] = a
        dh_s[...] = dyv * (g * sg)
        dg = dyv * h_ref[...] * (sg * (1.0 + g * (1.0 - sg)))

        def step(jj, car):
            tt = t - 1 - jj
            lm = dh_s[pl.ds(tt, 1), :] + car
            lam_s[pl.ds(tt, 1), :] = lm
            return a_s[pl.ds(tt, 1), :] * lm

        lamcar[...] = lax.fori_loop(0, t, step, lamcar[...], unroll=8)
        db = lam_s[...]
        da = db * hp_s[7:7 + t, :]
        dmult = db * gate_i * conv
        dgate_i = db * mult * conv
        dconv = db * mult * gate_i
        dlog_a = da * a + jnp.where(first, 0.0, dmult * (-(a * a) / mult_raw))
        dgate_r = dlog_a * (-LRU_C * sp)
        dsp = jnp.sum(dlog_a * (-LRU_C * gate_r), axis=0, keepdims=True)
        dlam_ref[...] += dsp * (-_sigmoid(-lam_p))
        dga = dgate_r * gate_r * (1.0 - gate_r)
        dgx = dgate_i * gate_i * (1.0 - gate_i)
        dba_ref[...] += jnp.sum(dga, axis=0, keepdims=True)
        dbx_ref[...] += jnp.sum(dgx, axis=0, keepdims=True)
        dga16, dgx16 = dga.astype(BF16), dgx.astype(BF16)
        back = []
        for nb in range(RNN_BLOCKS):
            sl = slice(nb * LANE, (nb + 1) * LANE)
            dwa_ref[nb] += lax.dot_general(cbf[:, sl], dga16[:, sl], _DIMS["tn"], preferred_element_type=F32)
            dwx_ref[nb] += lax.dot_general(cbf[:, sl], dgx16[:, sl], _DIMS["tn"], preferred_element_type=F32)
            back.append(lax.dot_general(dga16[:, sl], wa_ref[nb], _DIMS["nt"], preferred_element_type=F32)
                        + lax.dot_general(dgx16[:, sl], wx_ref[nb], _DIMS["nt"], preferred_element_type=F32))
        dconv = dconv + jnp.concatenate(back, axis=1)
        dcb_ref[...] += jnp.sum(dconv, axis=0, keepdims=True)
        for k in range(CONV_W):
            dcw_ref[k:k + 1, :] += jnp.sum(dconv * xs[k], axis=0, keepdims=True)
        dxp_s[0:t, :] = dconv
        dxr = cw_ref[0:1, :] * dconv
        for k in range(1, CONV_W):
            dxr = dxr + cw_ref[k:k + 1, :] * dxp_s[k:k + t, :]
        dp_ref[:, 0:D_RNN] = dxr.astype(BF16)
        dp_ref[:, D_RNN:2 * D_RNN] = dg.astype(BF16)

    blk = lambda c: pl.BlockSpec((t, D_RNN), lambda i: (n - 1 - i, c))
    prev8 = pl.BlockSpec((8, D_RNN), lambda i: (jnp.maximum((n - 1 - i) * rb - 1, 0), 0))
    full = lambda shape: pl.BlockSpec(shape, lambda i: (0,) * len(shape))
    vec = full((1, D_RNN))
    mat = full((RNN_BLOCKS, LANE, LANE))
    return pl.pallas_call(
        body,
        name="rglru_bwd",
        grid=(n,),
        in_specs=[blk(0), blk(0), blk(1), blk(0), prev8, prev8,
                  full((CONV_W, D_RNN)), vec, mat, vec, mat, vec, vec],
        out_specs=[pl.BlockSpec((t, 2 * D_RNN), lambda i: (n - 1 - i, 0)),
                   full((CONV_W, D_RNN)), vec, mat, vec, mat, vec, vec],
        out_shape=[jax.ShapeDtypeStruct((S, 2 * D_RNN), BF16),
                   jax.ShapeDtypeStruct((CONV_W, D_RNN), F32), jax.ShapeDtypeStruct((1, D_RNN), F32),
                   jax.ShapeDtypeStruct((RNN_BLOCKS, LANE, LANE), F32), jax.ShapeDtypeStruct((1, D_RNN), F32),
                   jax.ShapeDtypeStruct((RNN_BLOCKS, LANE, LANE), F32), jax.ShapeDtypeStruct((1, D_RNN), F32),
                   jax.ShapeDtypeStruct((1, D_RNN), F32)],
        scratch_shapes=[pltpu.VMEM((t + 8, D_RNN), F32), pltpu.VMEM((t + 8, D_RNN), F32),
                        pltpu.VMEM((t + 8, D_RNN), F32), pltpu.VMEM((1, D_RNN), F32),
                        pltpu.VMEM((t, D_RNN), F32), pltpu.VMEM((t, D_RNN), F32), pltpu.VMEM((t, D_RNN), F32)],
        compiler_params=_params(("arbitrary",)),
    )(dy, p_a, p_a, hseq, p_a, hseq, conv_w, conv_b, wa, ba, wx, bx, lam)


QB = WINDOW
KB2 = 2 * WINDOW
N_QB = S // QB
N_PAIR = SWA_HEADS // 2


def _swa_keys(kvc_ref, kvp_ref):
    kvc, kvp = kvc_ref[...], kvp_ref[...]
    kk = jnp.concatenate([kvp[:, 0:LANE] + kvp[:, LANE:2 * LANE], kvc[:, 0:LANE] + kvc[:, LANE:2 * LANE]], axis=0)
    vv = jnp.concatenate([kvp[:, 2 * LANE:3 * LANE], kvc[:, 2 * LANE:3 * LANE]], axis=0)
    lo = lax.broadcasted_iota(jnp.int32, (1, LANE), 1) < SWA_HD
    kk_sw, vv_sw = pltpu.roll(kk, SWA_HD, 1), pltpu.roll(vv, SWA_HD, 1)
    kd = [jnp.where(lo, kk, kk_sw).astype(BF16), jnp.where(lo, kk_sw, kk).astype(BF16)]
    vd = [jnp.where(lo, vv, vv_sw).astype(BF16), jnp.where(lo, vv_sw, vv).astype(BF16)]
    return lo, kd, vd


def _swa_valid(n):
    qi = lax.broadcasted_iota(jnp.int32, (QB, KB2), 0)
    kj = lax.broadcasted_iota(jnp.int32, (QB, KB2), 1)
    dist = qi + WINDOW - kj
    return (dist >= 0) & (dist < WINDOW) & ((n > 0) | (kj >= WINDOW))


def _swa_probs(qh16, kd, bias, sink, valid):
    lg = lax.dot_general(qh16, kd, _DIMS["nt"], preferred_element_type=F32) * (SWA_HD ** -0.5) + bias
    lg = jnp.where(valid, lg, NEG_INF)
    m = jnp.maximum(jnp.max(lg, axis=-1, keepdims=True), sink)
    p = jnp.exp(lg - m)
    es = jnp.exp(sink - m)
    den = jnp.sum(p, axis=-1, keepdims=True) + es
    return p / den, es / den


def _swa_specs():
    q = pl.BlockSpec((QB, D_RNN), lambda n: (n, 0))
    g = pl.BlockSpec((QB, D_RNN), lambda n: (n, 1))
    kvc = pl.BlockSpec((QB, 4 * LANE), lambda n: (n, 4))
    kvp = pl.BlockSpec((QB, 4 * LANE), lambda n: (jnp.maximum(n - 1, 0), 4))
    bias = pl.BlockSpec((SWA_HEADS, QB, KB2), lambda n: (0, 0, 0))
    sinks = pl.BlockSpec(memory_space=pltpu.SMEM)
    return q, g, kvc, kvp, bias, sinks


def _swa_fwd(p_b, bias_t, sinks):
    def body(q_ref, g_ref, kvc_ref, kvp_ref, bias_ref, sink_ref, y_ref, o_ref):
        n = pl.program_id(0)
        lo, kd, vd = _swa_keys(kvc_ref, kvp_ref)
        valid = _swa_valid(n)
        for hp in range(N_PAIR):
            sl = slice(hp * LANE, (hp + 1) * LANE)
            kvh = hp // (N_PAIR // 2)
            q = q_ref[:, sl]
            outs = []
            for j in range(2):
                mh = lo if j == 0 else jnp.logical_not(lo)
                qh16 = jnp.where(mh, q, 0.0).astype(BF16)
                probs, _ = _swa_probs(qh16, kd[kvh], bias_ref[2 * hp + j], sink_ref[2 * hp + j], valid)
                outs.append(jnp.dot(probs.astype(BF16), vd[kvh], preferred_element_type=F32))
            o = jnp.where(lo, outs[0], outs[1])
            o_ref[:, sl] = o
            g = g_ref[:, sl]
            y_ref[:, sl] = (o * (g * _sigmoid(g))).astype(BF16)

    q, g, kvc, kvp, bias, sinks_spec = _swa_specs()
    out = pl.BlockSpec((QB, D_RNN), lambda n: (n, 0))
    return pl.pallas_call(
        body,
        name="swa_fwd",
        grid=(N_QB,),
        in_specs=[q, g, kvc, kvp, bias, sinks_spec],
        out_specs=[out, out],
        out_shape=[jax.ShapeDtypeStruct((S, D_RNN), BF16), jax.ShapeDtypeStruct((S, D_RNN), F32)],
        compiler_params=_params(("parallel",)),
    )(p_b, p_b, p_b, p_b, bias_t, sinks)


def _swa_bwd(dy, p_b, o_swa, bias_t, sinks):
    def body(dy_ref, q_ref, g_ref, kvc_ref, kvp_ref, o_ref, bias_ref, sink_ref,
             dp_ref, dk_ref, dv_ref, dbias_ref, dsink_ref):
        n = pl.program_id(0)

        @pl.when(n == 0)
        def _():
            for ref in (dk_ref, dv_ref, dbias_ref, dsink_ref):
                ref[...] = jnp.zeros_like(ref)

        lo, kd, vd = _swa_keys(kvc_ref, kvp_ref)
        hi = jnp.logical_not(lo)
        valid = _swa_valid(n)
        dk_blk = jnp.zeros((KB2, LANE), F32)
        dv_blk = jnp.zeros((KB2, LANE), F32)
        for kvh in range(2):
            dk_pair = jnp.zeros((KB2, LANE), F32)
            dv_pair = jnp.zeros((KB2, LANE), F32)
            for hp in range(kvh * (N_PAIR // 2), (kvh + 1) * (N_PAIR // 2)):
                sl = slice(hp * LANE, (hp + 1) * LANE)
                q = q_ref[:, sl]
                g = g_ref[:, sl]
                o = o_ref[:, sl]
                dyv = dy_ref[:, sl]
                sg = _sigmoid(g)
                do = dyv * (g * sg)
                dp_ref[:, D_RNN + hp * LANE:D_RNN + (hp + 1) * LANE] = (
                    dyv * o * (sg * (1.0 + g * (1.0 - sg)))).astype(BF16)
                dqs = []
                for j in range(2):
                    h = 2 * hp + j
                    mh = lo if j == 0 else hi
                    qh16 = jnp.where(mh, q, 0.0).astype(BF16)
                    sink = sink_ref[h]
                    probs, psink = _swa_probs(qh16, kd[kvh], bias_ref[h], sink, valid)
                    doh = jnp.where(mh, do, 0.0)
                    doh16 = doh.astype(BF16)
                    delta = jnp.sum(doh * o, axis=-1, keepdims=True)
                    dpr = lax.dot_general(doh16, vd[kvh], _DIMS["nt"], preferred_element_type=F32)
                    ds = probs * (dpr - delta)
                    dbias_ref[h] += ds
                    dsink_ref[h:h + 1, :] += jnp.zeros((1, LANE), F32) - jnp.sum(psink * delta)
                    ds16 = (ds * (SWA_HD ** -0.5)).astype(BF16)
                    dqs.append(jnp.dot(ds16, kd[kvh], preferred_element_type=F32))
                    dk_pair = dk_pair + lax.dot_general(ds16, qh16, _DIMS["tn"], preferred_element_type=F32)
                    dv_pair = dv_pair + lax.dot_general(probs.astype(BF16), doh16, _DIMS["tn"],
                                                        preferred_element_type=F32)
                dp_ref[:, sl] = jnp.where(lo, dqs[0], dqs[1]).astype(BF16)
            keep = lo if kvh == 0 else hi
            dk_blk = dk_blk + jnp.where(keep, dk_pair + pltpu.roll(dk_pair, SWA_HD, 1), 0.0)
            dv_blk = dv_blk + jnp.where(keep, dv_pair + pltpu.roll(dv_pair, SWA_HD, 1), 0.0)

        cur = pl.ds(pl.multiple_of(n * QB, QB), QB)
        dk_ref[cur, :] += dk_blk[QB:KB2]
        dv_ref[cur, :] += dv_blk[QB:KB2]

        @pl.when(n > 0)
        def _():
            prev = pl.ds(pl.multiple_of((n - 1) * QB, QB), QB)
            dk_ref[prev, :] += dk_blk[0:QB]
            dv_ref[prev, :] += dv_blk[0:QB]

    q, g, kvc, kvp, bias, sinks_spec = _swa_specs()
    row = pl.BlockSpec((QB, D_RNN), lambda n: (n, 0))
    acc = pl.BlockSpec((S, LANE), lambda n: (0, 0))
    return pl.pallas_call(
        body,
        name="swa_bwd",
        grid=(N_QB,),
        in_specs=[row, q, g, kvc, kvp, row, bias, sinks_spec],
        out_specs=[pl.BlockSpec((QB, 2 * D_RNN), lambda n: (n, 0)), acc, acc, bias,
                   pl.BlockSpec((SWA_HEADS, LANE), lambda n: (0, 0))],
        out_shape=[jax.ShapeDtypeStruct((S, GROUP_TILES["B"] * LANE), BF16),
                   jax.ShapeDtypeStruct((S, LANE), F32), jax.ShapeDtypeStruct((S, LANE), F32),
                   jax.ShapeDtypeStruct((SWA_HEADS, QB, KB2), F32),
                   jax.ShapeDtypeStruct((SWA_HEADS, LANE), F32)],
        compiler_params=_params(("arbitrary",)),
    )(dy, p_b, p_b, p_b, p_b, o_swa, bias_t, sinks)


def _swa_pack(dp_b, dk, dv, dp_d, ts=512):
    def body(_, dk_ref, dv_ref, seam_ref, o_ref):
        dk16 = dk_ref[...].astype(BF16)
        o_ref[:, 0:LANE] = dk16
        o_ref[:, LANE:2 * LANE] = dk16
        o_ref[:, 2 * LANE:3 * LANE] = dv_ref[...].astype(BF16)
        o_ref[:, 3 * LANE:4 * LANE] = seam_ref[...]

    tile = pl.BlockSpec((ts, LANE), lambda i: (i, 0))
    return pl.pallas_call(
        body,
        name="swa_pack",
        grid=(S // ts,),
        in_specs=[pl.BlockSpec(memory_space=pl.ANY), tile, tile, pl.BlockSpec((ts, LANE), lambda i: (i, 23))],
        out_specs=pl.BlockSpec((ts, 4 * LANE), lambda i: (i, 4)),
        out_shape=jax.ShapeDtypeStruct(dp_b.shape, dp_b.dtype),
        input_output_aliases={0: 0},
        compiler_params=_params(("parallel",)),
    )(dp_b, dk, dv, dp_d)


def _split3(v):
    a = v.astype(BF16)
    r = v - a.astype(F32)
    b = r.astype(BF16)
    c = (r - b.astype(F32)).astype(BF16)
    return a, b, c


def _relbias_grad(dbias_flat, onehot_t):
    def body(d_ref, e_ref, o_ref):
        e = e_ref[...]
        acc = jnp.zeros((SWA_HEADS, REL_BUCKETS), F32)
        for term in _split3(d_ref[...]):
            acc = acc + lax.dot_general(term, e, _DIMS["nt"], preferred_element_type=F32)
        o_ref[...] = acc

    return pl.pallas_call(
        body,
        name="relbias_grad",
        out_shape=jax.ShapeDtypeStruct((SWA_HEADS, REL_BUCKETS), F32),
        compiler_params=_params(),
    )(dbias_flat, onehot_t)


TS_MEM = 512


def _mem_probs(q16, mk):
    lg = lax.dot_general(q16, mk, _DIMS["nt"], preferred_element_type=F32) * (MEM_HD ** -0.5)
    p = jnp.exp(lg - jnp.max(lg, axis=-1, keepdims=True))
    return p / jnp.sum(p, axis=-1, keepdims=True)


def _mem_fwd(p_c, mkv):
    def body(q_ref, g_ref, mkv_ref, y_ref, o_ref):
        for hm in range(MEM_HEADS):
            sl = slice(hm * MEM_HD, (hm + 1) * MEM_HD)
            probs = _mem_probs(q_ref[:, sl].astype(BF16), mkv_ref[:, sl])
            o = jnp.dot(probs.astype(BF16), mkv_ref[:, D_RNN + hm * MEM_HD:D_RNN + (hm + 1) * MEM_HD],
                        preferred_element_type=F32)
            o_ref[:, sl] = o
            g = g_ref[:, sl]
            y_ref[:, sl] = (o * (g * _sigmoid(g))).astype(BF16)

    blk = lambda c: pl.BlockSpec((TS_MEM, D_RNN), lambda i: (i, c))
    return pl.pallas_call(
        body,
        name="mem_fwd",
        grid=(S // TS_MEM,),
        in_specs=[blk(0), blk(1), pl.BlockSpec((MEM, 2 * D_RNN), lambda i: (0, 0))],
        out_specs=[blk(0), blk(0)],
        out_shape=[jax.ShapeDtypeStruct((S, D_RNN), BF16), jax.ShapeDtypeStruct((S, D_RNN), F32)],
        compiler_params=_params(("parallel",)),
    )(p_c, p_c, mkv)


def _mem_bwd(dy, p_c, o_mem, mkv):
    def body(dy_ref, q_ref, g_ref, o_ref, mkv_ref, dp_ref, dmkv_ref):
        @pl.when(pl.program_id(0) == 0)
        def _():
            dmkv_ref[...] = jnp.zeros_like(dmkv_ref)

        for hm in range(MEM_HEADS):
            sl = slice(hm * MEM_HD, (hm + 1) * MEM_HD)
            sv = slice(D_RNN + hm * MEM_HD, D_RNN + (hm + 1) * MEM_HD)
            q16 = q_ref[:, sl].astype(BF16)
            mk, mv = mkv_ref[:, sl], mkv_ref[:, sv]
            probs = _mem_probs(q16, mk)
            g, o, dyv = g_ref[:, sl], o_ref[:, sl], dy_ref[:, sl]
            sg = _sigmoid(g)
            do = dyv * (g * sg)
            dp_ref[:, sv] = (dyv * o * (sg * (1.0 + g * (1.0 - sg)))).astype(BF16)
            do16 = do.astype(BF16)
            delta = jnp.sum(do * o, axis=-1, keepdims=True)
            dpr = lax.dot_general(do16, mv, _DIMS["nt"], preferred_element_type=F32)
            ds16 = (probs * (dpr - delta) * (MEM_HD ** -0.5)).astype(BF16)
            dp_ref[:, sl] = jnp.dot(ds16, mk, preferred_element_type=F32).astype(BF16)
            dmkv_ref[:, sl] += lax.dot_general(ds16, q16, _DIMS["tn"], preferred_element_type=F32)
            dmkv_ref[:, sv] += lax.dot_general(probs.astype(BF16), do16, _DIMS["tn"], preferred_element_type=F32)

    blk = lambda c: pl.BlockSpec((TS_MEM, D_RNN), lambda i: (i, c))
    kv = pl.BlockSpec((MEM, 2 * D_RNN), lambda i: (0, 0))
    return pl.pallas_call(
        body,
        name="mem_bwd",
        grid=(S // TS_MEM,),
        in_specs=[blk(0), blk(0), blk(1), blk(0), kv],
        out_specs=[pl.BlockSpec((TS_MEM, 2 * D_RNN), lambda i: (i, 0)), kv],
        out_shape=[jax.ShapeDtypeStruct((S, 2 * D_RNN), BF16), jax.ShapeDtypeStruct((MEM, 2 * D_RNN), F32)],
        compiler_params=_params(("arbitrary",)),
    )(dy, p_c, p_c, o_mem, mkv)


TS_MRG = 512
TD_MRG = 512
N_DBLK = D // TD_MRG


def _seam_fix(p_d, p_b, ts=512):
    def body(d_ref, b_ref, o_ref):
        o_ref[...] = d_ref[...] + b_ref[...]

    return pl.pallas_call(
        body,
        name="seam_fix",
        grid=(S // ts,),
        in_specs=[pl.BlockSpec((ts, LANE), lambda i: (i, 23)), pl.BlockSpec((ts, LANE), lambda i: (i, 19))],
        out_specs=pl.BlockSpec((ts, LANE), lambda i: (i, 23)),
        out_shape=jax.ShapeDtypeStruct(p_d.shape, p_d.dtype),
        input_output_aliases={0: 0},
        compiler_params=_params(("parallel",)),
    )(p_d, p_b)


def _merge_fwd(z, p_d):
    def body(z0, z1, z2, g0, g1, g2, o_ref):
        o_ref[...] = (_sigmoid(g0[...]) * z0[...] + _sigmoid(g1[...]) * z1[...]
                      + _sigmoid(g2[...]) * z2[...]).astype(BF16)

    blk = pl.BlockSpec((TS_MRG, TD_MRG), lambda i, d: (i, d))
    gate = lambda b: pl.BlockSpec((TS_MRG, TD_MRG), lambda i, d: (i, b * N_DBLK + d))
    return pl.pallas_call(
        body,
        name="merge_fwd",
        grid=(S // TS_MRG, N_DBLK),
        in_specs=[blk, blk, blk, gate(0), gate(1), gate(2)],
        out_specs=blk,
        out_shape=jax.ShapeDtypeStruct((S, D), BF16),
        compiler_params=_params(("parallel", "parallel")),
    )(z[0], z[1], z[2], p_d, p_d, p_d)


def _merge_bwd(dmerged, z_b, p_d, b, dp_d):
    def body(dm_ref, z_ref, g_ref, *refs):
        dz_ref, dg_ref = refs[-2], refs[-1]
        sg = _sigmoid(g_ref[...])
        dm = dm_ref[...]
        dz_ref[...] = (dm * sg).astype(BF16)
        dg_ref[...] = (dm * z_ref[...] * sg * (1.0 - sg)).astype(BF16)

    blk = pl.BlockSpec((TS_MRG, TD_MRG), lambda i, d: (i, d))
    gate = pl.BlockSpec((TS_MRG, TD_MRG), lambda i, d: (i, b * N_DBLK + d))
    in_specs = [blk, blk, gate]
    args = [dmerged, z_b, p_d]
    aliases = {}
    if dp_d is not None:
        in_specs.append(pl.BlockSpec(memory_space=pl.ANY))
        args.append(dp_d)
        aliases = {3: 1}
    return pl.pallas_call(
        body,
        name=f"merge_bwd{b}",
        grid=(S // TS_MRG, N_DBLK),
        in_specs=in_specs,
        out_specs=[blk, gate],
        out_shape=[jax.ShapeDtypeStruct((S, D), BF16),
                   jax.ShapeDtypeStruct((S, GROUP_TILES["D"] * LANE), BF16)],
        input_output_aliases=aliases,
        compiler_params=_params(("parallel", "parallel")),
    )(*args)


def _bucket_table():
    import numpy as np
    qi = np.arange(QB)[:, None]
    kj = np.arange(KB2)[None, :]
    n = np.maximum(qi + WINDOW - kj, 0)
    max_exact = REL_BUCKETS // 2
    ratio = np.log(np.maximum(n, 1).astype(np.float32) / max_exact) / np.float32(math.log(REL_MAX_DIST / max_exact))
    large = np.minimum(max_exact + (ratio * (REL_BUCKETS - max_exact)).astype(np.int32), REL_BUCKETS - 1)
    bucket = np.where(n < max_exact, n, large).reshape(1, QB * KB2)
    return (bucket == np.arange(REL_BUCKETS)[:, None]).astype(np.float32)


def _bias_expand(rel_bias_t, onehot_t):
    def body(r_ref, e_ref, o_ref):
        e = e_ref[...]
        acc = jnp.zeros((SWA_HEADS, QB * KB2), F32)
        for term in _split3(r_ref[...]):
            acc = acc + jnp.dot(term, e, preferred_element_type=F32)
        o_ref[...] = acc

    return pl.pallas_call(
        body,
        name="bias_expand",
        out_shape=jax.ShapeDtypeStruct((SWA_HEADS, QB * KB2), F32),
        compiler_params=_params(),
    )(rel_bias_t, onehot_t)


PROJ_TN = {"A": 1024, "B": 1280, "C": 1024, "D": 1536}


def _local_step(x, mem, tgt, sp, w_grp, wmk, wbr, wout):
    onehot_t = jnp.asarray(_bucket_table(), BF16)
    bias_t = _bias_expand(sp["rel_bias"].T, onehot_t).reshape(SWA_HEADS, QB, KB2)
    sinks = sp["swa_sinks"].reshape(SWA_HEADS)
    wa16, wx16 = sp["w_rg_a"].astype(BF16), sp["w_rg_x"].astype(BF16)
    rnn = (sp["conv_w"], sp["conv_b"], wa16, sp["b_rg_a"], wx16, sp["b_rg_x"], sp["lru_lambda"])

    h = _rms_fwd(x, sp["pre_norm_g"], "rms_pre")
    p = {g: _mm(h, w_grp[g], "nn", F32, 1024, PROJ_TN[g], D, f"proj_{g}") for g in GROUPS}
    p["D"] = _seam_fix(p["D"], p["B"])
    memn = _rms_fwd(mem, sp["mem_norm_g"], "rms_mem")
    mkv = _mm(memn, wmk, "nn", BF16, MEM, 1024, D, "mkv")
    y_rg, hseq = _rglru_fwd(p["A"], *rnn)
    y_swa, o_swa = _swa_fwd(p["B"], bias_t, sinks)
    y_mem, o_mem = _mem_fwd(p["C"], mkv)
    ys = (y_rg, y_swa, y_mem)
    z = [_mm(ys[b], wbr[b], "nn", F32, 1024, 1024, D_RNN, f"branch_out{b}") for b in range(3)]
    merged = _merge_fwd(z, p["D"])
    out = _mm(merged, wout, "nn", F32, 1024, 1024, D, "out_proj")
    sq, dy, dout, d_post = _post_loss(out, x, tgt, sp["post_norm_g"])

    dmerged = _mm(dout, wout, "nt", F32, 1024, 1024, D, "d_merged")
    d_wout = _mm(merged, dout, "tn", BF16, 1024, 1024, S, "d_wout")
    dz, dp_d = [], None
    for b in range(3):
        dz_b, dp_d = _merge_bwd(dmerged, z[b], p["D"], b, dp_d)
        dz.append(dz_b)
    dys = [_mm(dz[b], wbr[b], "nt", F32, 1024, 1024, D, f"d_branch{b}") for b in range(3)]
    d_wbr = [_mm(ys[b], dz[b], "tn", BF16, 1024, 1024, S, f"d_wbr{b}") for b in range(3)]
    dp_a, d_cw, d_cb, d_wa, d_ba, d_wx, d_bx, d_lam = _rglru_bwd(dys[0], p["A"], hseq, *rnn)
    dp_b, dk, dv, d_bias, d_sink = _swa_bwd(dys[1], p["B"], o_swa, bias_t, sinks)
    dp_b = _swa_pack(dp_b, dk, dv, dp_d)
    d_rel = _relbias_grad(d_bias.reshape(SWA_HEADS, QB * KB2), onehot_t).T
    dp_c, dmkv = _mem_bwd(dys[2], p["C"], o_mem, mkv)
    dmkv16 = dmkv.astype(BF16)
    d_wmk = _mm(memn, dmkv16, "tn", BF16, 1024, 1024, MEM, "d_wmk")
    dmemn = _mm(dmkv16, wmk, "nt", F32, MEM, 1024, D, "d_memn")
    d_memg = _memnorm_bwd(dmemn, mem)
    dp = {"A": dp_a, "B": dp_b, "C": dp_c, "D": dp_d}
    dh = None
    for g in GROUPS:
        dh = _mm(dp[g], w_grp[g], "nt", F32, 1024, 1024, 2560 if g == "B" else 2048, f"d_h_{g}", acc=dh)
    d_w = {g: _mm(h, dp[g], "tn", BF16, 1024, PROJ_TN[g], S, f"d_win_{g}") for g in GROUPS}
    grad_x, d_pre = _pre_bwd(dh, x, dy, sp["pre_norm_g"])

    d_small = {
        "pre_norm_g": d_pre, "post_norm_g": d_post, "mem_norm_g": d_memg, "conv_w": d_cw, "conv_b": d_cb,
        "w_rg_a": d_wa, "b_rg_a": d_ba, "w_rg_x": d_wx, "b_rg_x": d_bx, "lru_lambda": d_lam,
        "swa_sinks": d_sink[:, 0].reshape(1, SWA_HEADS), "rel_bias": d_rel,
    }
    return sq, grad_x, d_small, d_w, d_wmk, d_wbr, d_wout


ANY = pl.BlockSpec(memory_space=pl.ANY)
SHARD_ROWS = D // N_CHIPS
GATHERED = {"A": (D, 2048), "B": (D, 2560), "C": (D, 2048), "D": (D, 6144), "mk": (D, D),
            "br0": (D_RNN, D), "br1": (D_RNN, D), "br2": (D_RNN, D), "out": (D, D)}
SHARDS = ("win", "mk", "br0", "br1", "br2", "out")


class Piece(NamedTuple):
    src: str
    dst: str
    rows: int
    sr0: int
    sc0: int
    dr0: int
    dc0: int
    ncols: int


def _pieces_of(jj):
    out = [Piece("win", g, D, 0, lt * LANE, 0, gt * LANE, nt * LANE) for lt, nt, g, gt in _window_runs(jj)]
    out.append(Piece("mk", "mk", SHARD_ROWS, 0, 0, SHARD_ROWS * jj, 0, D))
    out += [Piece(f"br{b}", f"br{b}", D_RNN, 0, 0, 0, SHARD_ROWS * jj, SHARD_ROWS) for b in range(3)]
    out.append(Piece("out", "out", SHARD_ROWS, 0, 0, SHARD_ROWS * jj, 0, D))
    return out


MAX_PIECES = max(len(_pieces_of(jj)) for jj in range(N_CHIPS))


def _rect(ref, r0, rows, c0, ncols):
    return ref.at[pl.ds(r0, rows), pl.ds(c0, ncols)]


def _position():
    x, y, c = lax.axis_index("x"), lax.axis_index("y"), lax.axis_index("c")
    return x, y, c, 2 * x + y


def _all_gather_weights(shards):
    names = tuple(GATHERED)

    def body(*refs):
        src = dict(zip(SHARDS, refs[:len(SHARDS)]))
        dst = dict(zip(names, refs[len(SHARDS):len(SHARDS) + len(names)]))
        send_sems, ici_sems, fwd_sems, d2d_sems, loc_sems = refs[len(SHARDS) + len(names):]
        x, y, c, j = _position()

        def src_half(p, which):
            return _rect(src[p.src], p.sr0 + which * (p.rows // 2), p.rows // 2, p.sc0, p.ncols)

        def dst_half(p, which):
            return _rect(dst[p.dst], p.dr0 + which * (p.rows // 2), p.rows // 2, p.dc0, p.ncols)

        def ici_copy(jj, i, p, kk):
            return pltpu.make_async_remote_copy(
                src_ref=src_half(p, c), dst_ref=dst_half(p, c), send_sem=send_sems.at[(i * N_CHIPS) + kk],
                recv_sem=ici_sems.at[jj * MAX_PIECES + i], device_id=(kk // 2, kk % 2, c), device_id_type=MESH)

        def fwd_copy(jj, i, p, which):
            return pltpu.make_async_remote_copy(
                src_ref=dst_half(p, which), dst_ref=dst_half(p, which), send_sem=fwd_sems.at[jj * MAX_PIECES + i],
                recv_sem=d2d_sems.at[jj * MAX_PIECES + i], device_id=(x, y, 1 - c), device_id_type=MESH)

        def loc_copy(i, p):
            return pltpu.make_async_copy(_rect(src[p.src], p.sr0, p.rows, p.sc0, p.ncols),
                                         _rect(dst[p.dst], p.dr0, p.rows, p.dc0, p.ncols), loc_sems.at[i])

        for jj in range(N_CHIPS):
            @pl.when(j == jj)
            def _():
                for i, p in enumerate(_pieces_of(jj)):
                    loc_copy(i, p).start()
                    for kk in range(N_CHIPS):
                        if kk != jj:
                            ici_copy(jj, i, p, kk).start()
        for jj in range(N_CHIPS):
            @pl.when(j != jj)
            def _():
                for i, p in enumerate(_pieces_of(jj)):
                    ici_copy(jj, i, p, jj).wait_recv()
                    fwd_copy(jj, i, p, c).start()
        for jj in range(N_CHIPS):
            @pl.when(j != jj)
            def _():
                for i, p in enumerate(_pieces_of(jj)):
                    fwd_copy(jj, i, p, 1 - c).wait_recv()
        for jj in range(N_CHIPS):
            @pl.when(j != jj)
            def _():
                for i, p in enumerate(_pieces_of(jj)):
                    fwd_copy(jj, i, p, c).wait_send()
        for jj in range(N_CHIPS):
            @pl.when(j == jj)
            def _():
                for i, p in enumerate(_pieces_of(jj)):
                    for kk in range(N_CHIPS):
                        if kk != jj:
                            ici_copy(jj, i, p, kk).wait_send()
                    loc_copy(i, p).wait()

    outs = pl.pallas_call(
        body,
        name="all_gather_weights",
        in_specs=[ANY] * len(SHARDS),
        out_specs=[ANY] * len(names),
        out_shape=[jax.ShapeDtypeStruct(GATHERED[n], BF16) for n in names],
        scratch_shapes=[pltpu.SemaphoreType.DMA((MAX_PIECES * N_CHIPS,)),
                        pltpu.SemaphoreType.DMA((MAX_PIECES * N_CHIPS,)),
                        pltpu.SemaphoreType.DMA((MAX_PIECES * N_CHIPS,)),
                        pltpu.SemaphoreType.DMA((MAX_PIECES * N_CHIPS,)),
                        pltpu.SemaphoreType.DMA((MAX_PIECES,))],
        compiler_params=pltpu.CompilerParams(has_side_effects=True),
    )(*[shards[n] for n in SHARDS])
    return dict(zip(names, outs))


def _half_rows(name):
    return GATHERED[name][0] // 2


def _row_blocks(name, which):
    if name in ("mk", "out"):
        q = SHARD_ROWS // 2
        return [(SHARD_ROWS * jj + q * which, q * jj, q) for jj in range(N_CHIPS)]
    h = _half_rows(name)
    return [(h * which, 0, h)]


def _swap_halves(grads):
    names = tuple(GATHERED)
    n_tr = sum(len(_row_blocks(n, 0)) for n in names)

    def body(*refs):
        src = dict(zip(names, refs[:len(names)]))
        dst = dict(zip(names, refs[len(names):2 * len(names)]))
        send_sems, recv_sems = refs[2 * len(names):]
        x, y, c, _ = _position()
        copies = []
        for n in names:
            for r0, h0, rows in _row_blocks(n, 1 - c):
                k = len(copies)
                copies.append(pltpu.make_async_remote_copy(
                    src_ref=src[n].at[pl.ds(r0, rows)], dst_ref=dst[n].at[pl.ds(h0, rows)],
                    send_sem=send_sems.at[k], recv_sem=recv_sems.at[k],
                    device_id=(x, y, 1 - c), device_id_type=MESH))
        for cp in copies:
            cp.start()
        for cp in copies:
            cp.wait_recv()
        for cp in copies:
            cp.wait_send()

    outs = pl.pallas_call(
        body,
        name="swap_halves",
        in_specs=[ANY] * len(names),
        out_specs=[ANY] * len(names),
        out_shape=[jax.ShapeDtypeStruct((_half_rows(n), GATHERED[n][1]), BF16) for n in names],
        scratch_shapes=[pltpu.SemaphoreType.DMA((n_tr,)), pltpu.SemaphoreType.DMA((n_tr,))],
        compiler_params=pltpu.CompilerParams(has_side_effects=True),
    )(*[grads[n] for n in names])
    return dict(zip(names, outs))


ADD_ROWS = 256


def _add_half(full, recv, c_arr, name):
    rows, cols = recv.shape
    if name in ("mk", "out"):
        index = lambda i, c_ref: (2 * i + c_ref[0], 0)
    else:
        nb = rows // ADD_ROWS
        index = lambda i, c_ref: (nb * c_ref[0] + i, 0)

    def body(c_ref, a_ref, b_ref, o_ref):
        o_ref[...] = (a_ref[...].astype(F32) + b_ref[...].astype(F32)).astype(BF16)

    return pl.pallas_call(
        body,
        name=f"add_half_{name}",
        grid_spec=pltpu.PrefetchScalarGridSpec(
            num_scalar_prefetch=1,
            grid=(rows // ADD_ROWS,),
            in_specs=[pl.BlockSpec((ADD_ROWS, cols), index), pl.BlockSpec((ADD_ROWS, cols), lambda i, c_ref: (i, 0))],
            out_specs=pl.BlockSpec((ADD_ROWS, cols), lambda i, c_ref: (i, 0)),
        ),
        out_shape=jax.ShapeDtypeStruct((rows, cols), BF16),
        compiler_params=_params(("parallel",)),
    )(c_arr, full, recv)


SLOT_SHAPES = {"win": (D // 2, WIN), "mk": (SHARD_ROWS // 2, D), "br0": (D_RNN // 2, SHARD_ROWS),
               "br1": (D_RNN // 2, SHARD_ROWS), "br2": (D_RNN // 2, SHARD_ROWS), "out": (SHARD_ROWS // 2, D)}


def _scatter_to_owners(halves):
    names = tuple(GATHERED)

    def body(*refs):
        src = dict(zip(names, refs[:len(names)]))
        dst = dict(zip(SHARDS, refs[len(names):len(names) + len(SHARDS)]))
        send_sems, recv_sems, loc_sems = refs[len(names) + len(SHARDS):]
        x, y, c, j = _position()

        def src_rect(p):
            if p.dst in ("mk", "out"):
                return _rect(src[p.dst], p.dr0 // 2, p.rows // 2, p.dc0, p.ncols)
            return _rect(src[p.dst], 0, p.rows // 2, p.dc0, p.ncols)

        def dst_rect(p, slot):
            return _rect(dst[p.src].at[slot], 0, p.rows // 2, p.sc0, p.ncols)

        def remote(jj, kk, i, p):
            return pltpu.make_async_remote_copy(
                src_ref=src_rect(p), dst_ref=dst_rect(p, jj), send_sem=send_sems.at[kk * MAX_PIECES + i],
                recv_sem=recv_sems.at[jj * MAX_PIECES + i], device_id=(kk // 2, kk % 2, c), device_id_type=MESH)

        def local(jj, i, p):
            return pltpu.make_async_copy(src_rect(p), dst_rect(p, jj), loc_sems.at[i])

        for jj in range(N_CHIPS):
            @pl.when(j == jj)
            def _():
                for kk in range(N_CHIPS):
                    for i, p in enumerate(_pieces_of(kk)):
                        (local(jj, i, p) if kk == jj else remote(jj, kk, i, p)).start()
                for ss in range(N_CHIPS):
                    if ss != jj:
                        for i, p in enumerate(_pieces_of(jj)):
                            remote(ss, jj, i, p).wait_recv()
                for kk in range(N_CHIPS):
                    for i, p in enumerate(_pieces_of(kk)):
                        if kk == jj:
                            local(jj, i, p).wait()
                        else:
                            remote(jj, kk, i, p).wait_send()

    outs = pl.pallas_call(
        body,
        name="scatter_to_owners",
        in_specs=[ANY] * len(names),
        out_specs=[ANY] * len(SHARDS),
        out_shape=[jax.ShapeDtypeStruct((N_CHIPS,) + SLOT_SHAPES[n], BF16) for n in SHARDS],
        scratch_shapes=[pltpu.SemaphoreType.DMA((MAX_PIECES * N_CHIPS,)),
                        pltpu.SemaphoreType.DMA((MAX_PIECES * N_CHIPS,)),
                        pltpu.SemaphoreType.DMA((MAX_PIECES,))],
        compiler_params=pltpu.CompilerParams(has_side_effects=True),
    )(*[halves[n] for n in names])
    return dict(zip(SHARDS, outs))


def _sum_slots(slots, c_arr, name, tr=128):
    _, rows, cols = slots.shape

    def body(c_ref, s_ref, o_ref):
        acc = s_ref[0].astype(F32)
        for k in range(1, N_CHIPS):
            acc = acc + s_ref[k].astype(F32)
        o_ref[...] = acc

    return pl.pallas_call(
        body,
        name=f"sum_slots_{name}",
        grid_spec=pltpu.PrefetchScalarGridSpec(
            num_scalar_prefetch=1,
            grid=(rows // tr,),
            in_specs=[pl.BlockSpec((N_CHIPS, tr, cols), lambda i, c_ref: (0, i, 0))],
            out_specs=pl.BlockSpec((None, tr, cols), lambda i, c_ref: (c_ref[0], i, 0)),
        ),
        out_shape=jax.ShapeDtypeStruct((2, rows, cols), F32),
        compiler_params=_params(("parallel",)),
    )(c_arr, slots)


def _share_sums(sums):
    def body(*refs):
        bufs = refs[len(SHARDS):2 * len(SHARDS)]
        send_sems, recv_sems = refs[2 * len(SHARDS):]
        x, y, c, _ = _position()
        copies = [pltpu.make_async_remote_copy(
            src_ref=b.at[c], dst_ref=b.at[c], send_sem=send_sems.at[k], recv_sem=recv_sems.at[k],
            device_id=(x, y, 1 - c), device_id_type=MESH) for k, b in enumerate(bufs)]
        for cp in copies:
            cp.start()
        for cp in copies:
            cp.wait_recv()
        for cp in copies:
            cp.wait_send()

    outs = pl.pallas_call(
        body,
        name="share_sums",
        in_specs=[ANY] * len(SHARDS),
        out_specs=[ANY] * len(SHARDS),
        out_shape=[jax.ShapeDtypeStruct(sums[n].shape, F32) for n in SHARDS],
        input_output_aliases={k: k for k in range(len(SHARDS))},
        scratch_shapes=[pltpu.SemaphoreType.DMA((len(SHARDS),)), pltpu.SemaphoreType.DMA((len(SHARDS),))],
        compiler_params=pltpu.CompilerParams(has_side_effects=True),
    )(*[sums[n] for n in SHARDS])
    return dict(zip(SHARDS, outs))


N_DEV = 8


def _all_reduce_small(pack, name):
    rows = pack.shape[0]

    def body(p_ref, o_ref, land, send_sems, recv_sems):
        x, y, c, _ = _position()
        me = 4 * x + 2 * y + c

        def copy(o):
            return pltpu.make_async_remote_copy(
                src_ref=p_ref, dst_ref=land.at[me], send_sem=send_sems.at[o], recv_sem=recv_sems.at[me],
                device_id=(o // 4, (o // 2) % 2, o % 2), device_id_type=MESH)

        def arrival(o):
            return pltpu.make_async_remote_copy(
                src_ref=p_ref, dst_ref=land.at[o], send_sem=send_sems.at[o], recv_sem=recv_sems.at[o],
                device_id=(o // 4, (o // 2) % 2, o % 2), device_id_type=MESH)

        for o in range(N_DEV):
            @pl.when(me != o)
            def _():
                copy(o).start()
        land[me] = p_ref[...]
        for o in range(N_DEV):
            @pl.when(me != o)
            def _():
                arrival(o).wait_recv()
        acc = land[0]
        for o in range(1, N_DEV):
            acc = acc + land[o]
        o_ref[...] = acc
        for o in range(N_DEV):
            @pl.when(me != o)
            def _():
                copy(o).wait_send()

    vmem = pl.BlockSpec(memory_space=pltpu.VMEM)
    return pl.pallas_call(
        body,
        name=name,
        in_specs=[vmem],
        out_specs=vmem,
        out_shape=jax.ShapeDtypeStruct((rows, LANE), F32),
        scratch_shapes=[pltpu.VMEM((N_DEV, rows, LANE), F32), pltpu.SemaphoreType.DMA((N_DEV,)),
                        pltpu.SemaphoreType.DMA((N_DEV,))],
        compiler_params=pltpu.CompilerParams(has_side_effects=True, vmem_limit_bytes=VMEM_LIMIT),
    )(pack)


def _adamw(w, g, m, v, name, tr):
    rows, cols = w.shape
    tr = min(tr, rows)

    def body(w_ref, g_ref, m_ref, v_ref, d_ref, nm_ref, nv_ref):
        gv = g_ref[...]
        nm = ADAM_B1 * m_ref[...] + (1.0 - ADAM_B1) * gv
        nv = ADAM_B2 * v_ref[...] + (1.0 - ADAM_B2) * (gv * gv)
        nm_ref[...] = nm
        nv_ref[...] = nv
        m_hat = nm / (1.0 - ADAM_B1 ** ADAM_STEP)
        v_hat = nv / (1.0 - ADAM_B2 ** ADAM_STEP)
        d_ref[...] = -ADAM_LR * (m_hat / (jnp.sqrt(v_hat) + ADAM_EPS) + ADAM_WD * w_ref[...])

    blk = pl.BlockSpec((tr, cols), lambda i: (i, 0))
    shape = jax.ShapeDtypeStruct((rows, cols), F32)
    return pl.pallas_call(
        body,
        name=f"adamw_{name}",
        grid=(rows // tr,),
        in_specs=[blk] * 4,
        out_specs=[blk] * 3,
        out_shape=[shape] * 3,
        compiler_params=_params(("parallel",)),
    )(w, g, m, v)


WIN_TILES = WIN // LANE


def _half_swapped(ref, t):
    tile = ref[:, t * LANE:(t + 1) * LANE]
    return tile, pltpu.roll(tile, SWA_HD, 1)


def _to_window(w_shard, odd_arr, tr=256):
    def body(odd_ref, x_ref, o_ref, s_ref):
        s_ref[...] = jnp.zeros_like(s_ref)
        s_ref[:, 0:SHARD] = x_ref[...]
        odd = odd_ref[0] == 1
        lo = lax.broadcasted_iota(jnp.int32, (1, LANE), 1) < SWA_HD
        prev = jnp.zeros((tr, LANE), F32)
        for t in range(WIN_TILES):
            tile, swapped = _half_swapped(s_ref, t)
            o_ref[:, t * LANE:(t + 1) * LANE] = jnp.where(odd, jnp.where(lo, prev, swapped), tile).astype(BF16)
            prev = swapped

    return pl.pallas_call(
        body,
        name="to_window",
        grid_spec=pltpu.PrefetchScalarGridSpec(
            num_scalar_prefetch=1,
            grid=(D // tr,),
            in_specs=[pl.BlockSpec((tr, SHARD), lambda i, o: (i, 0))],
            out_specs=pl.BlockSpec((tr, WIN), lambda i, o: (i, 0)),
            scratch_shapes=[pltpu.VMEM((tr, WIN), F32)],
        ),
        out_shape=jax.ShapeDtypeStruct((D, WIN), BF16),
        compiler_params=_params(("parallel",)),
    )(odd_arr, w_shard)


def _adamw_window(w, g_win, m, v, odd_arr, tr=128):
    def body(odd_ref, w_ref, g_ref, m_ref, v_ref, go_ref, d_ref, nm_ref, nv_ref, s_ref):
        odd = odd_ref[0] == 1
        lo = lax.broadcasted_iota(jnp.int32, (1, LANE), 1) < SWA_HD
        tile, swapped = _half_swapped(g_ref, 0)
        for t in range(WIN_TILES):
            if t + 1 < WIN_TILES:
                nxt_tile, nxt_swapped = _half_swapped(g_ref, t + 1)
            else:
                nxt_tile = nxt_swapped = jnp.zeros((tr, LANE), F32)
            s_ref[:, t * LANE:(t + 1) * LANE] = jnp.where(odd, jnp.where(lo, swapped, nxt_swapped), tile)
            tile, swapped = nxt_tile, nxt_swapped
        gv = s_ref[:, 0:SHARD]
        go_ref[...] = gv
        nm = ADAM_B1 * m_ref[...] + (1.0 - ADAM_B1) * gv
        nv = ADAM_B2 * v_ref[...] + (1.0 - ADAM_B2) * (gv * gv)
        nm_ref[...] = nm
        nv_ref[...] = nv
        m_hat = nm / (1.0 - ADAM_B1 ** ADAM_STEP)
        v_hat = nv / (1.0 - ADAM_B2 ** ADAM_STEP)
        d_ref[...] = -ADAM_LR * (m_hat / (jnp.sqrt(v_hat) + ADAM_EPS) + ADAM_WD * w_ref[...])

    blk = pl.BlockSpec((tr, SHARD), lambda i, o: (i, 0))
    shape = jax.ShapeDtypeStruct((D, SHARD), F32)
    return pl.pallas_call(
        body,
        name="adamw_win",
        grid_spec=pltpu.PrefetchScalarGridSpec(
            num_scalar_prefetch=1,
            grid=(D // tr,),
            in_specs=[blk, pl.BlockSpec((tr, WIN), lambda i, o: (i, 0)), blk, blk],
            out_specs=[blk] * 4,
            scratch_shapes=[pltpu.VMEM((tr, WIN), F32)],
        ),
        out_shape=[shape] * 4,
        compiler_params=_params(("parallel",)),
    )(odd_arr, w, g_win, m, v)


SMALL = (("pre_norm_g", (1, D)), ("post_norm_g", (1, D)), ("mem_norm_g", (1, D)), ("conv_w", (CONV_W, D_RNN)),
         ("conv_b", (1, D_RNN)), ("w_rg_a", (RNN_BLOCKS, LANE, LANE)), ("b_rg_a", (1, D_RNN)),
         ("w_rg_x", (RNN_BLOCKS, LANE, LANE)), ("b_rg_x", (1, D_RNN)), ("lru_lambda", (1, D_RNN)),
         ("swa_sinks", (1, SWA_HEADS)), ("rel_bias", (REL_BUCKETS, SWA_HEADS)))
PACK_ROWS = 2176


def _slot_len(shape):
    return -(-math.prod(shape) // LANE) * LANE


def _pack(values):
    parts = []
    for name, shape in SMALL:
        flat = values[name].reshape(-1).astype(F32)
        parts.append(jnp.pad(flat, (0, _slot_len(shape) - flat.shape[0])))
    flat = jnp.concatenate(parts)
    return jnp.pad(flat, (0, PACK_ROWS * LANE - flat.shape[0])).reshape(PACK_ROWS, LANE)


def _unpack(pack, shapes=None):
    flat = pack.reshape(-1)
    out, off = {}, 0
    for name, shape in SMALL:
        shp = shape if shapes is None or name not in shapes else shapes[name]
        out[name] = flat[off:off + math.prod(shp)].reshape(shp)
        off += _slot_len(shape)
    return out


TWIN_WEIGHTS = ("pre_norm_g", "post_norm_g", "mem_norm_g", "w_in", "conv_w", "conv_b", "w_rg_a", "b_rg_a", "w_rg_x",
                "b_rg_x", "lru_lambda", "swa_sinks", "rel_bias", "w_mem_kv", "w_br_rg", "w_br_swa", "w_br_mem", "w_out")
BIG = {"w_in": "win", "w_mem_kv": "mk", "w_br_rg": "br0", "w_br_swa": "br1", "w_br_mem": "br2", "w_out": "out"}


def kernel(x, mem, pre_norm_g, post_norm_g, mem_norm_g, w_in, conv_w, conv_b, w_rg_a, b_rg_a, w_rg_x, b_rg_x, lru_lambda, swa_sinks, rel_bias, w_mem_kv, w_br_rg, w_br_swa, w_br_mem, w_out, loss_target, m_pre_norm_g, m_post_norm_g, m_mem_norm_g, m_w_in, m_conv_w, m_conv_b, m_w_rg_a, m_b_rg_a, m_w_rg_x, m_b_rg_x, m_lru_lambda, m_swa_sinks, m_rel_bias, m_w_mem_kv, m_w_br_rg, m_w_br_swa, m_w_br_mem, m_w_out, v_pre_norm_g, v_post_norm_g, v_mem_norm_g, v_w_in, v_conv_w, v_conv_b, v_w_rg_a, v_b_rg_a, v_w_rg_x, v_b_rg_x, v_lru_lambda, v_swa_sinks, v_rel_bias, v_w_mem_kv, v_w_br_rg, v_w_br_swa, v_w_br_mem, v_w_out):
    args = dict(locals())
    out_shapes = {n: args[n].shape for n in TWIN_WEIGHTS}
    w = {n: (args[n] if n == "rel_bias" else args[n][0]) for n in TWIN_WEIGHTS}
    m = {n: (args["m_" + n] if n == "rel_bias" else args["m_" + n][0]) for n in TWIN_WEIGHTS}
    v = {n: (args["v_" + n] if n == "rel_bias" else args["v_" + n][0]) for n in TWIN_WEIGHTS}
    for d in (w, m, v):
        for n, shape in SMALL:
            if n != "conv_w":
                d[n] = d[n].reshape(shape)

    xi, yi, ci = lax.axis_index("x"), lax.axis_index("y"), lax.axis_index("c")
    chip = 2 * xi + yi
    c_arr = ci.astype(jnp.int32).reshape(1)
    odd_arr = yi.astype(jnp.int32).reshape(1)
    zero = jnp.zeros((), jnp.int32)
    cw0 = (chip * (D_RNN // N_CHIPS)).astype(jnp.int32)

    placed = lax.dynamic_update_slice(jnp.zeros((CONV_W, D_RNN), F32), w["conv_w"], (zero, cw0))
    placed = jnp.where(ci == 0, placed, 0.0).reshape(CONV_W * D_RNN // LANE, LANE)
    conv_w_full = _all_reduce_small(placed, "gather_conv_w").reshape(CONV_W, D_RNN)

    shards = {"win": _to_window(w["w_in"], odd_arr)}
    for n, s in BIG.items():
        if n != "w_in":
            shards[s] = w[n].astype(BF16)
    gathered = _all_gather_weights(shards)

    sp = {n: w[n] for n, _ in SMALL}
    sp["conv_w"] = conv_w_full
    sq, grad_x, d_small, d_w, d_wmk, d_wbr, d_wout = _local_step(
        x[0], mem[0], loss_target[0], sp, {g: gathered[g] for g in GROUPS}, gathered["mk"],
        [gathered["br0"], gathered["br1"], gathered["br2"]], gathered["out"])
    loss = lax.psum(sq[0, 0] * (0.5 / D), ("x", "y", "c"))

    grads = dict(d_w)
    grads.update({"mk": d_wmk, "br0": d_wbr[0], "br1": d_wbr[1], "br2": d_wbr[2], "out": d_wout})
    received = _swap_halves(grads)
    halves = {n: _add_half(grads[n], received[n], c_arr, n) for n in GATHERED}
    slots = _scatter_to_owners(halves)
    sums = _share_sums({n: _sum_slots(slots[n], c_arr, n) for n in SHARDS})
    g_big = {n: sums[s].reshape(2 * SLOT_SHAPES[s][0], SLOT_SHAPES[s][1]) for n, s in BIG.items()}

    g_small = _unpack(_all_reduce_small(_pack(d_small), "all_reduce_small"))
    g_small["conv_w"] = lax.dynamic_slice(g_small["conv_w"], (zero, cw0), (CONV_W, D_RNN // N_CHIPS))

    grad, delta, new_m, new_v = {}, {}, {}, {}
    for n, s in BIG.items():
        if n == "w_in":
            grad[n], delta[n], new_m[n], new_v[n] = _adamw_window(w[n], g_big[n], m[n], v[n], odd_arr)
        else:
            grad[n] = g_big[n]
            delta[n], new_m[n], new_v[n] = _adamw(w[n], g_big[n], m[n], v[n], s, 128)
    d_, m_, v_ = _adamw(_pack(w), _pack(g_small), _pack(m), _pack(v), "small", PACK_ROWS)
    shard_shapes = {"conv_w": (CONV_W, D_RNN // N_CHIPS)}
    d_, m_, v_ = (_unpack(a, shard_shapes) for a in (d_, m_, v_))
    for n, _ in SMALL:
        grad[n], delta[n], new_m[n], new_v[n] = g_small[n], d_[n], m_[n], v_[n]

    outs = [loss, grad_x.reshape(1, S, D)]
    for group in (grad, delta, new_m, new_v):
        outs += [group[n].reshape(out_shapes[n]) for n in TWIN_WEIGHTS]
    return tuple(outs)
```

```python
import functools
import math
from typing import NamedTuple

import jax
import jax.numpy as jnp
from jax import lax
from jax.experimental import pallas as pl
from jax.experimental.pallas import tpu as pltpu

F32 = jnp.float32
BF16 = jnp.bfloat16
MESH = pl.DeviceIdType.MESH

S = 2048
D = 2048
MEM = 256
D_RNN = 1024
RNN_BLOCKS = 8
CONV_W = 4
LRU_C = 8.0
SWA_HEADS = 16
SWA_HD = 64
WINDOW = 128
MEM_HEADS = 4
MEM_HD = 256
REL_BUCKETS = 32
REL_MAX_DIST = 128
EPS = 1e-6
NEG_INF = -1e30
LANE = 128
SHARD = 3136
HALF_TILE = 64
N_CHIPS = 4
VMEM_LIMIT = 56 * 1024 * 1024

ADAM_LR = 0.001
ADAM_B1 = 0.9
ADAM_B2 = 0.999
ADAM_EPS = 1e-08
ADAM_WD = 0.01
ADAM_STEP = 10

GROUP_TILES = {"A": 16, "B": 18, "C": 16, "D": 48}
GROUPS = ("A", "B", "C", "D")


def _params(sem=None):
    return pltpu.CompilerParams(dimension_semantics=sem, vmem_limit_bytes=VMEM_LIMIT)


def _sigmoid(v):
    return jax.nn.sigmoid(v)


def _tile_home(t):
    if t < 16:
        return "A", t
    if t < 24:
        return "B", t - 16
    if t < 26:
        return "B", t - 24 + 16
    if t < 34:
        return "B", t - 26 + 8
    if t < 50:
        return "C", t - 34
    return "D", t - 50


def _shard_runs(j):
    runs = []
    per_shard = SHARD // HALF_TILE
    for q in range(per_shard * j, per_shard * (j + 1)):
        g, gt = _tile_home(q // 2)
        row = gt * LANE + (q % 2) * HALF_TILE
        if runs and runs[-1][2] == g and runs[-1][3] + runs[-1][1] == row:
            runs[-1][1] += HALF_TILE
        else:
            runs.append([(q - per_shard * j) * HALF_TILE, HALF_TILE, g, row])
    return [tuple(r) for r in runs]


_DIMS = {
    "nn": (((1,), (0,)), ((), ())),
    "nt": (((1,), (1,)), ((), ())),
    "tn": (((0,), (0,)), ((), ())),
}


def _mm(a, b, mode, out_dtype, tm, tn, tk, name, acc=None):
    if mode == "nn":
        (m, k), n = a.shape, b.shape[1]
    elif mode == "nt":
        (m, k), n = a.shape, b.shape[0]
    else:
        (k, m), n = a.shape, b.shape[1]
    tm, tn, tk = min(tm, m), min(tn, n), min(tk, k)
    assert m % tm == 0 and n % tn == 0 and k % tk == 0, (name, m, n, k)
    nk = k // tk
    has_acc = acc is not None

    def body(*refs):
        a_ref, b_ref = refs[0], refs[1]
        o_ref = refs[3] if has_acc else refs[2]
        p = lax.dot_general(a_ref[...], b_ref[...], _DIMS[mode], preferred_element_type=F32)

        def finish(v):
            if has_acc:
                v = v + refs[2][...]
            o_ref[...] = v.astype(out_dtype)

        if nk == 1:
            finish(p)
        else:
            s_ref = refs[-1]
            kk = pl.program_id(2)

            @pl.when(kk == 0)
            def _():
                s_ref[...] = p

            @pl.when(kk > 0)
            def _():
                s_ref[...] += p

            @pl.when(kk == nk - 1)
            def _():
                finish(s_ref[...])

    if mode == "nn":
        a_spec = pl.BlockSpec((tm, tk), lambda i, j, kk: (i, kk))
        b_spec = pl.BlockSpec((tk, tn), lambda i, j, kk: (kk, j))
    elif mode == "nt":
        a_spec = pl.BlockSpec((tm, tk), lambda i, j, kk: (i, kk))
        b_spec = pl.BlockSpec((tn, tk), lambda i, j, kk: (j, kk))
    else:
        a_spec = pl.BlockSpec((tk, tm), lambda i, j, kk: (kk, i))
        b_spec = pl.BlockSpec((tk, tn), lambda i, j, kk: (kk, j))
    o_spec = pl.BlockSpec((tm, tn), lambda i, j, kk: (i, j))
    in_specs = [a_spec, b_spec] + ([o_spec] if has_acc else [])
    args = (a, b) + ((acc,) if has_acc else ())
    return pl.pallas_call(
        body,
        name=name,
        grid=(m // tm, n // tn, nk),
        in_specs=in_specs,
        out_specs=o_spec,
        out_shape=jax.ShapeDtypeStruct((m, n), out_dtype),
        scratch_shapes=[pltpu.VMEM((tm, tn), F32)] if nk > 1 else [],
        compiler_params=_params(("parallel", "parallel", "arbitrary")),
    )(*args)


def _rms_fwd(x, g, name, ts=256):
    r, d = x.shape

    def body(x_ref, g_ref, o_ref):
        xv = x_ref[...]
        inv = lax.rsqrt(jnp.mean(xv * xv, axis=-1, keepdims=True) + EPS)
        o_ref[...] = (xv * inv * g_ref[...]).astype(BF16)

    return pl.pallas_call(
        body,
        name=name,
        grid=(r // ts,),
        in_specs=[pl.BlockSpec((ts, d), lambda i: (i, 0)), pl.BlockSpec((1, d), lambda i: (0, 0))],
        out_specs=pl.BlockSpec((ts, d), lambda i: (i, 0)),
        out_shape=jax.ShapeDtypeStruct((r, d), BF16),
        compiler_params=_params(("parallel",)),
    )(x, g)


def _post_loss(out, x, tgt, g_post, ts=256):
    n = S // ts

    def body(o_ref, x_ref, t_ref, g_ref, sq_ref, dy_ref, do_ref, dg_ref):
        i = pl.program_id(0)

        @pl.when(i == 0)
        def _():
            sq_ref[...] = jnp.zeros_like(sq_ref)
            dg_ref[...] = jnp.zeros_like(dg_ref)

        ov = o_ref[...]
        g = g_ref[...]
        inv = lax.rsqrt(jnp.mean(ov * ov, axis=-1, keepdims=True) + EPS)
        on = ov * inv
        err = x_ref[...] + on * g - t_ref[...]
        sq_ref[...] += jnp.sum(err * err)
        dy = err * (1.0 / D)
        dy_ref[...] = dy
        dg_ref[...] += jnp.sum(dy * on, axis=0, keepdims=True)
        don = dy * g
        do_ref[...] = (inv * (don - on * jnp.mean(don * on, axis=-1, keepdims=True))).astype(BF16)

    row = pl.BlockSpec((ts, D), lambda i: (i, 0))
    vec = pl.BlockSpec((1, D), lambda i: (0, 0))
    return pl.pallas_call(
        body,
        name="post_loss",
        grid=(n,),
        in_specs=[row, row, row, vec],
        out_specs=[pl.BlockSpec((8, LANE), lambda i: (0, 0)), row, row, vec],
        out_shape=[
            jax.ShapeDtypeStruct((8, LANE), F32),
            jax.ShapeDtypeStruct((S, D), F32),
            jax.ShapeDtypeStruct((S, D), BF16),
            jax.ShapeDtypeStruct((1, D), F32),
        ],
        compiler_params=_params(("arbitrary",)),
    )(out, x, tgt, g_post)


def _pre_bwd(dh, x, dy, g_pre, ts=256):
    n = S // ts

    def body(dh_ref, x_ref, dy_ref, g_ref, gx_ref, dg_ref):
        i = pl.program_id(0)

        @pl.when(i == 0)
        def _():
            dg_ref[...] = jnp.zeros_like(dg_ref)

        xv = x_ref[...]
        dhv = dh_ref[...]
        inv = lax.rsqrt(jnp.mean(xv * xv, axis=-1, keepdims=True) + EPS)
        xn = xv * inv
        dg_ref[...] += jnp.sum(dhv * xn, axis=0, keepdims=True)
        dxn = dhv * g_ref[...]
        gx_ref[...] = dy_ref[...] + inv * (dxn - xn * jnp.mean(dxn * xn, axis=-1, keepdims=True))

    row = pl.BlockSpec((ts, D), lambda i: (i, 0))
    vec = pl.BlockSpec((1, D), lambda i: (0, 0))
    return pl.pallas_call(
        body,
        name="pre_bwd",
        grid=(n,),
        in_specs=[row, row, row, vec],
        out_specs=[row, vec],
        out_shape=[jax.ShapeDtypeStruct((S, D), F32), jax.ShapeDtypeStruct((1, D), F32)],
        compiler_params=_params(("arbitrary",)),
    )(dh, x, dy, g_pre)


def _memnorm_bwd(dmemn, mem):
    def body(d_ref, m_ref, dg_ref):
        mv = m_ref[...]
        inv = lax.rsqrt(jnp.mean(mv * mv, axis=-1, keepdims=True) + EPS)
        dg_ref[...] = jnp.sum(d_ref[...] * mv * inv, axis=0, keepdims=True)

    return pl.pallas_call(
        body,
        name="memnorm_bwd",
        out_shape=jax.ShapeDtypeStruct((1, D), F32),
        compiler_params=_params(),
    )(dmemn, mem)


T_RNN = 256


def _neg_expm1(z):
    poly = -z * (1.0 + z * (0.5 + z * (1.0 / 6 + z * (1.0 / 24 + z * (1.0 / 120 + z * (1.0 / 720))))))
    return jnp.where(z > -0.1, poly, 1.0 - jnp.exp(z))


def _softplus_neg(lam):
    return jnp.maximum(-lam, 0.0) + jnp.log1p(jnp.exp(-jnp.abs(lam)))


def _rnn_gates(conv, wa_ref, ba, wx_ref, bx, lam, first_row):
    cbf = conv.astype(BF16)
    ga, gx = [], []
    for n in range(RNN_BLOCKS):
        c_n = cbf[:, n * LANE:(n + 1) * LANE]
        ga.append(jnp.dot(c_n, wa_ref[n], preferred_element_type=F32))
        gx.append(jnp.dot(c_n, wx_ref[n], preferred_element_type=F32))
    gate_r = _sigmoid(jnp.concatenate(ga, axis=1) + ba)
    gate_i = _sigmoid(jnp.concatenate(gx, axis=1) + bx)
    sp = _softplus_neg(lam)
    log_a = -LRU_C * gate_r * sp
    a = jnp.exp(log_a)
    mult_raw = jnp.sqrt(_neg_expm1(2.0 * log_a))
    mult = jnp.where(first_row, 1.0, mult_raw)
    return cbf, gate_r, gate_i, sp, a, mult_raw, mult


def _rglru_fwd(p_a, conv_w, conv_b, wa, ba, wx, bx, lam):
    t = T_RNN
    n = S // t

    def body(xr_ref, g_ref, cw_ref, cb_ref, wa_ref, ba_ref, wx_ref, bx_ref, lam_ref,
             y_ref, h_ref, xp_s, hcar, a_s, b_s):
        i = pl.program_id(0)

        @pl.when(i == 0)
        def _():
            xp_s[0:8, :] = jnp.zeros((8, D_RNN), F32)
            hcar[...] = jnp.zeros_like(hcar)

        @pl.when(i > 0)
        def _():
            xp_s[0:8, :] = xp_s[t:t + 8, :]

        xp_s[8:8 + t, :] = xr_ref[...]
        conv = cb_ref[...]
        for k in range(CONV_W):
            conv = conv + cw_ref[k:k + 1, :] * xp_s[8 - k:8 - k + t, :]
        rows = i * t + lax.broadcasted_iota(jnp.int32, (t, 1), 0)
        _, _, gate_i, _, a, _, mult = _rnn_gates(
            conv, wa_ref, ba_ref[...], wx_ref, bx_ref[...], lam_ref[...], rows == 0)
        a_s[...] = a
        b_s[...] = mult * gate_i * conv

        def step(tt, h):
            h = a_s[pl.ds(tt, 1), :] * h + b_s[pl.ds(tt, 1), :]
            h_ref[pl.ds(tt, 1), :] = h
            return h

        hcar[...] = lax.fori_loop(0, t, step, hcar[...], unroll=8)
        g = g_ref[...]
        y_ref[...] = (h_ref[...] * (g * _sigmoid(g))).astype(BF16)

    blk = lambda c: pl.BlockSpec((t, D_RNN), lambda i: (i, c))
    full = lambda shape: pl.BlockSpec(shape, lambda i: (0,) * len(shape))
    return pl.pallas_call(
        body,
        name="rglru_fwd",
        grid=(n,),
        in_specs=[blk(0), blk(1), full((CONV_W, D_RNN)), full((1, D_RNN)),
                  full((RNN_BLOCKS, LANE, LANE)), full((1, D_RNN)),
                  full((RNN_BLOCKS, LANE, LANE)), full((1, D_RNN)), full((1, D_RNN))],
        out_specs=[blk(0), blk(0)],
        out_shape=[jax.ShapeDtypeStruct((S, D_RNN), BF16), jax.ShapeDtypeStruct((S, D_RNN), F32)],
        scratch_shapes=[pltpu.VMEM((t + 8, D_RNN), F32), pltpu.VMEM((1, D_RNN), F32),
                        pltpu.VMEM((t, D_RNN), F32), pltpu.VMEM((t, D_RNN), F32)],
        compiler_params=_params(("arbitrary",)),
    )(p_a, p_a, conv_w, conv_b, wa, ba, wx, bx, lam)


def _rglru_bwd(dy, p_a, hseq, conv_w, conv_b, wa, ba, wx, bx, lam):
    t = T_RNN
    n = S // t
    rb = t // 8

    def body(dy_ref, xr_ref, g_ref, h_ref, xrp_ref, hp_ref, cw_ref, cb_ref, wa_ref, ba_ref, wx_ref, bx_ref, lam_ref,
             dp_ref, dcw_ref, dcb_ref, dwa_ref, dba_ref, dwx_ref, dbx_ref, dlam_ref,
             xp_s, hp_s, dxp_s, lamcar, a_s, dh_s, lam_s):
        i = pl.program_id(0)
        r = n - 1 - i

        @pl.when(i == 0)
        def _():
            for ref in (dcw_ref, dcb_ref, dwa_ref, dba_ref, dwx_ref, dbx_ref, dlam_ref, lamcar):
                ref[...] = jnp.zeros_like(ref)
            dxp_s[t:t + 8, :] = jnp.zeros((8, D_RNN), F32)

        @pl.when(i > 0)
        def _():
            dxp_s[t:t + 8, :] = dxp_s[0:8, :]

        has_prev = r > 0
        xp_s[0:8, :] = jnp.where(has_prev, xrp_ref[...], 0.0)
        xp_s[8:8 + t, :] = xr_ref[...]
        hp_s[0:8, :] = jnp.where(has_prev, hp_ref[...], 0.0)
        hp_s[8:8 + t, :] = h_ref[...]
        xs = [xp_s[8 - k:8 - k + t, :] for k in range(CONV_W)]
        conv = cb_ref[...]
        for k in range(CONV_W):
            conv = conv + cw_ref[k:k + 1, :] * xs[k]
        rows = r * t + lax.broadcasted_iota(jnp.int32, (t, 1), 0)
        first = rows == 0
        lam_p = lam_ref[...]
        cbf, gate_r, gate_i, sp, a, mult_raw, mult = _rnn_gates(
            conv, wa_ref, ba_ref[...], wx_ref, bx_ref[...], lam_p, first)

        g = g_ref[...]
        sg = _sigmoid(g)
        dyv = dy_ref[...]
        a_s[...] = a
        dh_s[...] = dyv * (g * sg)
        dg = dyv * h_ref[...] * (sg * (1.0 + g * (1.0 - sg)))

        def step(jj, car):
            tt = t - 1 - jj
            lm = dh_s[pl.ds(tt, 1), :] + car
            lam_s[pl.ds(tt, 1), :] = lm
            return a_s[pl.ds(tt, 1), :] * lm

        lamcar[...] = lax.fori_loop(0, t, step, lamcar[...], unroll=8)
        db = lam_s[...]
        da = db * hp_s[7:7 + t, :]
        dmult = db * gate_i * conv
        dgate_i = db * mult * conv
        dconv = db * mult * gate_i
        dlog_a = da * a + jnp.where(first, 0.0, dmult * (-(a * a) / mult_raw))
        dgate_r = dlog_a * (-LRU_C * sp)
        dsp = jnp.sum(dlog_a * (-LRU_C * gate_r), axis=0, keepdims=True)
        dlam_ref[...] += dsp * (-_sigmoid(-lam_p))
        dga = dgate_r * gate_r * (1.0 - gate_r)
        dgx = dgate_i * gate_i * (1.0 - gate_i)
        dba_ref[...] += jnp.sum(dga, axis=0, keepdims=True)
        dbx_ref[...] += jnp.sum(dgx, axis=0, keepdims=True)
        dga16, dgx16 = dga.astype(BF16), dgx.astype(BF16)
        back = []
        for nb in range(RNN_BLOCKS):
            sl = slice(nb * LANE, (nb + 1) * LANE)
            dwa_ref[nb] += lax.dot_general(cbf[:, sl], dga16[:, sl], _DIMS["tn"], preferred_element_type=F32)
            dwx_ref[nb] += lax.dot_general(cbf[:, sl], dgx16[:, sl], _DIMS["tn"], preferred_element_type=F32)
            back.append(lax.dot_general(dga16[:, sl], wa_ref[nb], _DIMS["nt"], preferred_element_type=F32)
                        + lax.dot_general(dgx16[:, sl], wx_ref[nb], _DIMS["nt"], preferred_element_type=F32))
        dconv = dconv + jnp.concatenate(back, axis=1)
        dcb_ref[...] += jnp.sum(dconv, axis=0, keepdims=True)
        for k in range(CONV_W):
            dcw_ref[k:k + 1, :] += jnp.sum(dconv * xs[k], axis=0, keepdims=True)
        dxp_s[0:t, :] = dconv
        dxr = cw_ref[0:1, :] * dconv
        for k in range(1, CONV_W):
            dxr = dxr + cw_ref[k:k + 1, :] * dxp_s[k:k + t, :]
        dp_ref[:, 0:D_RNN] = dxr.astype(BF16)
        dp_ref[:, D_RNN:2 * D_RNN] = dg.astype(BF16)

    blk = lambda c: pl.BlockSpec((t, D_RNN), lambda i: (n - 1 - i, c))
    prev8 = pl.BlockSpec((8, D_RNN), lambda i: (jnp.maximum((n - 1 - i) * rb - 1, 0), 0))
    full = lambda shape: pl.BlockSpec(shape, lambda i: (0,) * len(shape))
    vec = full((1, D_RNN))
    mat = full((RNN_BLOCKS, LANE, LANE))
    return pl.pallas_call(
        body,
        name="rglru_bwd",
        grid=(n,),
        in_specs=[blk(0), blk(0), blk(1), blk(0), prev8, prev8,
                  full((CONV_W, D_RNN)), vec, mat, vec, mat, vec, vec],
        out_specs=[pl.BlockSpec((t, 2 * D_RNN), lambda i: (n - 1 - i, 0)),
                   full((CONV_W, D_RNN)), vec, mat, vec, mat, vec, vec],
        out_shape=[jax.ShapeDtypeStruct((S, 2 * D_RNN), BF16),
                   jax.ShapeDtypeStruct((CONV_W, D_RNN), F32), jax.ShapeDtypeStruct((1, D_RNN), F32),
                   jax.ShapeDtypeStruct((RNN_BLOCKS, LANE, LANE), F32), jax.ShapeDtypeStruct((1, D_RNN), F32),
                   jax.ShapeDtypeStruct((RNN_BLOCKS, LANE, LANE), F32), jax.ShapeDtypeStruct((1, D_RNN), F32),
                   jax.ShapeDtypeStruct((1, D_RNN), F32)],
        scratch_shapes=[pltpu.VMEM((t + 8, D_RNN), F32), pltpu.VMEM((t + 8, D_RNN), F32),
                        pltpu.VMEM((t + 8, D_RNN), F32), pltpu.VMEM((1, D_RNN), F32),
                        pltpu.VMEM((t, D_RNN), F32), pltpu.VMEM((t, D_RNN), F32), pltpu.VMEM((t, D_RNN), F32)],
        compiler_params=_params(("arbitrary",)),
    )(dy, p_a, p_a, hseq, p_a, hseq, conv_w, conv_b, wa, ba, wx, bx, lam)


QB = WINDOW
KB2 = 2 * WINDOW
N_QB = S // QB
N_PAIR = SWA_HEADS // 2


def _swa_keys(kvc_ref, kvp_ref):
    kk = jnp.concatenate([kvp_ref[:, 0:LANE], kvc_ref[:, 0:LANE]], axis=0)
    vv = jnp.concatenate([kvp_ref[:, LANE:2 * LANE], kvc_ref[:, LANE:2 * LANE]], axis=0)
    lo = lax.broadcasted_iota(jnp.int32, (1, LANE), 1) < SWA_HD
    kk_sw, vv_sw = pltpu.roll(kk, SWA_HD, 1), pltpu.roll(vv, SWA_HD, 1)
    kd = [jnp.where(lo, kk, kk_sw).astype(BF16), jnp.where(lo, kk_sw, kk).astype(BF16)]
    vd = [jnp.where(lo, vv, vv_sw).astype(BF16), jnp.where(lo, vv_sw, vv).astype(BF16)]
    return lo, kd, vd


def _swa_valid(n):
    qi = lax.broadcasted_iota(jnp.int32, (QB, KB2), 0)
    kj = lax.broadcasted_iota(jnp.int32, (QB, KB2), 1)
    dist = qi + WINDOW - kj
    return (dist >= 0) & (dist < WINDOW) & ((n > 0) | (kj >= WINDOW))


def _swa_probs(qh16, kd, bias, sink, valid):
    lg = lax.dot_general(qh16, kd, _DIMS["nt"], preferred_element_type=F32) * (SWA_HD ** -0.5) + bias
    lg = jnp.where(valid, lg, NEG_INF)
    m = jnp.maximum(jnp.max(lg, axis=-1, keepdims=True), sink)
    p = jnp.exp(lg - m)
    es = jnp.exp(sink - m)
    den = jnp.sum(p, axis=-1, keepdims=True) + es
    return p / den, es / den


def _swa_specs():
    q = pl.BlockSpec((QB, D_RNN), lambda n: (n, 0))
    g = pl.BlockSpec((QB, D_RNN), lambda n: (n, 1))
    kvc = pl.BlockSpec((QB, 2 * LANE), lambda n: (n, 8))
    kvp = pl.BlockSpec((QB, 2 * LANE), lambda n: (jnp.maximum(n - 1, 0), 8))
    bias = pl.BlockSpec((SWA_HEADS, QB, KB2), lambda n: (0, 0, 0))
    sinks = pl.BlockSpec(memory_space=pltpu.SMEM)
    return q, g, kvc, kvp, bias, sinks


def _swa_fwd(p_b, bias_t, sinks):
    def body(q_ref, g_ref, kvc_ref, kvp_ref, bias_ref, sink_ref, y_ref, o_ref):
        n = pl.program_id(0)
        lo, kd, vd = _swa_keys(kvc_ref, kvp_ref)
        valid = _swa_valid(n)
        for hp in range(N_PAIR):
            sl = slice(hp * LANE, (hp + 1) * LANE)
            kvh = hp // (N_PAIR // 2)
            q = q_ref[:, sl]
            outs = []
            for j in range(2):
                mh = lo if j == 0 else jnp.logical_not(lo)
                qh16 = jnp.where(mh, q, 0.0).astype(BF16)
                probs, _ = _swa_probs(qh16, kd[kvh], bias_ref[2 * hp + j], sink_ref[2 * hp + j], valid)
                outs.append(jnp.dot(probs.astype(BF16), vd[kvh], preferred_element_type=F32))
            o = jnp.where(lo, outs[0], outs[1])
            o_ref[:, sl] = o
            g = g_ref[:, sl]
            y_ref[:, sl] = (o * (g * _sigmoid(g))).astype(BF16)

    q, g, kvc, kvp, bias, sinks_spec = _swa_specs()
    out = pl.BlockSpec((QB, D_RNN), lambda n: (n, 0))
    return pl.pallas_call(
        body,
        name="swa_fwd",
        grid=(N_QB,),
        in_specs=[q, g, kvc, kvp, bias, sinks_spec],
        out_specs=[out, out],
        out_shape=[jax.ShapeDtypeStruct((S, D_RNN), BF16), jax.ShapeDtypeStruct((S, D_RNN), F32)],
        compiler_params=_params(("parallel",)),
    )(p_b, p_b, p_b, p_b, bias_t, sinks)


def _swa_bwd(dy, p_b, o_swa, bias_t, sinks):
    def body(dy_ref, q_ref, g_ref, kvc_ref, kvp_ref, o_ref, bias_ref, sink_ref,
             dp_ref, dk_ref, dv_ref, dbias_ref, dsink_ref):
        n = pl.program_id(0)

        @pl.when(n == 0)
        def _():
            for ref in (dk_ref, dv_ref, dbias_ref, dsink_ref):
                ref[...] = jnp.zeros_like(ref)

        lo, kd, vd = _swa_keys(kvc_ref, kvp_ref)
        hi = jnp.logical_not(lo)
        valid = _swa_valid(n)
        dk_blk = jnp.zeros((KB2, LANE), F32)
        dv_blk = jnp.zeros((KB2, LANE), F32)
        for kvh in range(2):
            dk_pair = jnp.zeros((KB2, LANE), F32)
            dv_pair = jnp.zeros((KB2, LANE), F32)
            for hp in range(kvh * (N_PAIR // 2), (kvh + 1) * (N_PAIR // 2)):
                sl = slice(hp * LANE, (hp + 1) * LANE)
                q = q_ref[:, sl]
                g = g_ref[:, sl]
                o = o_ref[:, sl]
                dyv = dy_ref[:, sl]
                sg = _sigmoid(g)
                do = dyv * (g * sg)
                dp_ref[:, D_RNN + hp * LANE:D_RNN + (hp + 1) * LANE] = (
                    dyv * o * (sg * (1.0 + g * (1.0 - sg)))).astype(BF16)
                dqs = []
                for j in range(2):
                    h = 2 * hp + j
                    mh = lo if j == 0 else hi
                    qh16 = jnp.where(mh, q, 0.0).astype(BF16)
                    sink = sink_ref[h]
                    probs, psink = _swa_probs(qh16, kd[kvh], bias_ref[h], sink, valid)
                    doh = jnp.where(mh, do, 0.0)
                    doh16 = doh.astype(BF16)
                    delta = jnp.sum(doh * o, axis=-1, keepdims=True)
                    dpr = lax.dot_general(doh16, vd[kvh], _DIMS["nt"], preferred_element_type=F32)
                    ds = probs * (dpr - delta)
                    dbias_ref[h] += ds
                    dsink_ref[h:h + 1, :] += jnp.zeros((1, LANE), F32) - jnp.sum(psink * delta)
                    ds16 = (ds * (SWA_HD ** -0.5)).astype(BF16)
                    dqs.append(jnp.dot(ds16, kd[kvh], preferred_element_type=F32))
                    dk_pair = dk_pair + lax.dot_general(ds16, qh16, _DIMS["tn"], preferred_element_type=F32)
                    dv_pair = dv_pair + lax.dot_general(probs.astype(BF16), doh16, _DIMS["tn"],
                                                        preferred_element_type=F32)
                dp_ref[:, sl] = jnp.where(lo, dqs[0], dqs[1]).astype(BF16)
            keep = lo if kvh == 0 else hi
            dk_blk = dk_blk + jnp.where(keep, dk_pair + pltpu.roll(dk_pair, SWA_HD, 1), 0.0)
            dv_blk = dv_blk + jnp.where(keep, dv_pair + pltpu.roll(dv_pair, SWA_HD, 1), 0.0)

        cur = pl.ds(pl.multiple_of(n * QB, QB), QB)
        dk_ref[cur, :] += dk_blk[QB:KB2]
        dv_ref[cur, :] += dv_blk[QB:KB2]

        @pl.when(n > 0)
        def _():
            prev = pl.ds(pl.multiple_of((n - 1) * QB, QB), QB)
            dk_ref[prev, :] += dk_blk[0:QB]
            dv_ref[prev, :] += dv_blk[0:QB]

    q, g, kvc, kvp, bias, sinks_spec = _swa_specs()
    row = pl.BlockSpec((QB, D_RNN), lambda n: (n, 0))
    acc = pl.BlockSpec((S, LANE), lambda n: (0, 0))
    return pl.pallas_call(
        body,
        name="swa_bwd",
        grid=(N_QB,),
        in_specs=[row, q, g, kvc, kvp, row, bias, sinks_spec],
        out_specs=[pl.BlockSpec((QB, 2 * D_RNN), lambda n: (n, 0)), acc, acc, bias,
                   pl.BlockSpec((SWA_HEADS, LANE), lambda n: (0, 0))],
        out_shape=[jax.ShapeDtypeStruct((S, GROUP_TILES["B"] * LANE), BF16),
                   jax.ShapeDtypeStruct((S, LANE), F32), jax.ShapeDtypeStruct((S, LANE), F32),
                   jax.ShapeDtypeStruct((SWA_HEADS, QB, KB2), F32),
                   jax.ShapeDtypeStruct((SWA_HEADS, LANE), F32)],
        compiler_params=_params(("arbitrary",)),
    )(dy, p_b, p_b, p_b, p_b, o_swa, bias_t, sinks)


def _swa_pack(dp_b, dk, dv, ts=512):
    def body(_, dk_ref, dv_ref, o_ref):
        o_ref[:, 0:LANE] = dk_ref[...].astype(BF16)
        o_ref[:, LANE:2 * LANE] = dv_ref[...].astype(BF16)

    tile = pl.BlockSpec((ts, LANE), lambda i: (i, 0))
    return pl.pallas_call(
        body,
        name="swa_pack",
        grid=(S // ts,),
        in_specs=[pl.BlockSpec(memory_space=pl.ANY), tile, tile],
        out_specs=pl.BlockSpec((ts, 2 * LANE), lambda i: (i, 8)),
        out_shape=jax.ShapeDtypeStruct(dp_b.shape, dp_b.dtype),
        input_output_aliases={0: 0},
        compiler_params=_params(("parallel",)),
    )(dp_b, dk, dv)


def _split3(v):
    a = v.astype(BF16)
    r = v - a.astype(F32)
    b = r.astype(BF16)
    c = (r - b.astype(F32)).astype(BF16)
    return a, b, c


def _relbias_grad(dbias_flat, onehot_t):
    def body(d_ref, e_ref, o_ref):
        e = e_ref[...]
        acc = jnp.zeros((SWA_HEADS, REL_BUCKETS), F32)
        for term in _split3(d_ref[...]):
            acc = acc + lax.dot_general(term, e, _DIMS["nt"], preferred_element_type=F32)
        o_ref[...] = acc

    return pl.pallas_call(
        body,
        name="relbias_grad",
        out_shape=jax.ShapeDtypeStruct((SWA_HEADS, REL_BUCKETS), F32),
        compiler_params=_params(),
    )(dbias_flat, onehot_t)


TS_MEM = 512


def _mem_probs(q16, mk):
    lg = lax.dot_general(q16, mk, _DIMS["nt"], preferred_element_type=F32) * (MEM_HD ** -0.5)
    p = jnp.exp(lg - jnp.max(lg, axis=-1, keepdims=True))
    return p / jnp.sum(p, axis=-1, keepdims=True)


def _mem_fwd(p_c, mkv):
    def body(q_ref, g_ref, mkv_ref, y_ref, o_ref):
        for hm in range(MEM_HEADS):
            sl = slice(hm * MEM_HD, (hm + 1) * MEM_HD)
            probs = _mem_probs(q_ref[:, sl].astype(BF16), mkv_ref[:, sl])
            o = jnp.dot(probs.astype(BF16), mkv_ref[:, D_RNN + hm * MEM_HD:D_RNN + (hm + 1) * MEM_HD],
                        preferred_element_type=F32)
            o_ref[:, sl] = o
            g = g_ref[:, sl]
            y_ref[:, sl] = (o * (g * _sigmoid(g))).astype(BF16)

    blk = lambda c: pl.BlockSpec((TS_MEM, D_RNN), lambda i: (i, c))
    return pl.pallas_call(
        body,
        name="mem_fwd",
        grid=(S // TS_MEM,),
        in_specs=[blk(0), blk(1), pl.BlockSpec((MEM, 2 * D_RNN), lambda i: (0, 0))],
        out_specs=[blk(0), blk(0)],
        out_shape=[jax.ShapeDtypeStruct((S, D_RNN), BF16), jax.ShapeDtypeStruct((S, D_RNN), F32)],
        compiler_params=_params(("parallel",)),
    )(p_c, p_c, mkv)


def _mem_bwd(dy, p_c, o_mem, mkv):
    def body(dy_ref, q_ref, g_ref, o_ref, mkv_ref, dp_ref, dmkv_ref):
        @pl.when(pl.program_id(0) == 0)
        def _():
            dmkv_ref[...] = jnp.zeros_like(dmkv_ref)

        for hm in range(MEM_HEADS):
            sl = slice(hm * MEM_HD, (hm + 1) * MEM_HD)
            sv = slice(D_RNN + hm * MEM_HD, D_RNN + (hm + 1) * MEM_HD)
            q16 = q_ref[:, sl].astype(BF16)
            mk, mv = mkv_ref[:, sl], mkv_ref[:, sv]
            probs = _mem_probs(q16, mk)
            g, o, dyv = g_ref[:, sl], o_ref[:, sl], dy_ref[:, sl]
            sg = _sigmoid(g)
            do = dyv * (g * sg)
            dp_ref[:, sv] = (dyv * o * (sg * (1.0 + g * (1.0 - sg)))).astype(BF16)
            do16 = do.astype(BF16)
            delta = jnp.sum(do * o, axis=-1, keepdims=True)
            dpr = lax.dot_general(do16, mv, _DIMS["nt"], preferred_element_type=F32)
            ds16 = (probs * (dpr - delta) * (MEM_HD ** -0.5)).astype(BF16)
            dp_ref[:, sl] = jnp.dot(ds16, mk, preferred_element_type=F32).astype(BF16)
            dmkv_ref[:, sl] += lax.dot_general(ds16, q16, _DIMS["tn"], preferred_element_type=F32)
            dmkv_ref[:, sv] += lax.dot_general(probs.astype(BF16), do16, _DIMS["tn"], preferred_element_type=F32)

    blk = lambda c: pl.BlockSpec((TS_MEM, D_RNN), lambda i: (i, c))
    kv = pl.BlockSpec((MEM, 2 * D_RNN), lambda i: (0, 0))
    return pl.pallas_call(
        body,
        name="mem_bwd",
        grid=(S // TS_MEM,),
        in_specs=[blk(0), blk(0), blk(1), blk(0), kv],
        out_specs=[pl.BlockSpec((TS_MEM, 2 * D_RNN), lambda i: (i, 0)), kv],
        out_shape=[jax.ShapeDtypeStruct((S, 2 * D_RNN), BF16), jax.ShapeDtypeStruct((MEM, 2 * D_RNN), F32)],
        compiler_params=_params(("arbitrary",)),
    )(dy, p_c, p_c, o_mem, mkv)


TS_MRG = 512
TD_MRG = 512
N_DBLK = D // TD_MRG


def _merge_fwd(z, p_d):
    def body(z0, z1, z2, g0, g1, g2, o_ref):
        o_ref[...] = (_sigmoid(g0[...]) * z0[...] + _sigmoid(g1[...]) * z1[...]
                      + _sigmoid(g2[...]) * z2[...]).astype(BF16)

    blk = pl.BlockSpec((TS_MRG, TD_MRG), lambda i, d: (i, d))
    gate = lambda b: pl.BlockSpec((TS_MRG, TD_MRG), lambda i, d: (i, b * N_DBLK + d))
    return pl.pallas_call(
        body,
        name="merge_fwd",
        grid=(S // TS_MRG, N_DBLK),
        in_specs=[blk, blk, blk, gate(0), gate(1), gate(2)],
        out_specs=blk,
        out_shape=jax.ShapeDtypeStruct((S, D), BF16),
        compiler_params=_params(("parallel", "parallel")),
    )(z[0], z[1], z[2], p_d, p_d, p_d)


def _merge_bwd(dmerged, z_b, p_d, b, dp_d):
    def body(dm_ref, z_ref, g_ref, *refs):
        dz_ref, dg_ref = refs[-2], refs[-1]
        sg = _sigmoid(g_ref[...])
        dm = dm_ref[...]
        dz_ref[...] = (dm * sg).astype(BF16)
        dg_ref[...] = (dm * z_ref[...] * sg * (1.0 - sg)).astype(BF16)

    blk = pl.BlockSpec((TS_MRG, TD_MRG), lambda i, d: (i, d))
    gate = pl.BlockSpec((TS_MRG, TD_MRG), lambda i, d: (i, b * N_DBLK + d))
    in_specs = [blk, blk, gate]
    args = [dmerged, z_b, p_d]
    aliases = {}
    if dp_d is not None:
        in_specs.append(pl.BlockSpec(memory_space=pl.ANY))
        args.append(dp_d)
        aliases = {3: 1}
    return pl.pallas_call(
        body,
        name=f"merge_bwd{b}",
        grid=(S // TS_MRG, N_DBLK),
        in_specs=in_specs,
        out_specs=[blk, gate],
        out_shape=[jax.ShapeDtypeStruct((S, D), BF16),
                   jax.ShapeDtypeStruct((S, GROUP_TILES["D"] * LANE), BF16)],
        input_output_aliases=aliases,
        compiler_params=_params(("parallel", "parallel")),
    )(*args)


def _bucket_table():
    import numpy as np
    qi = np.arange(QB)[:, None]
    kj = np.arange(KB2)[None, :]
    n = np.maximum(qi + WINDOW - kj, 0)
    max_exact = REL_BUCKETS // 2
    ratio = np.log(np.maximum(n, 1).astype(np.float32) / max_exact) / np.float32(math.log(REL_MAX_DIST / max_exact))
    large = np.minimum(max_exact + (ratio * (REL_BUCKETS - max_exact)).astype(np.int32), REL_BUCKETS - 1)
    bucket = np.where(n < max_exact, n, large).reshape(1, QB * KB2)
    return (bucket == np.arange(REL_BUCKETS)[:, None]).astype(np.float32)


def _bias_expand(rel_bias_t, onehot_t):
    def body(r_ref, e_ref, o_ref):
        e = e_ref[...]
        acc = jnp.zeros((SWA_HEADS, QB * KB2), F32)
        for term in _split3(r_ref[...]):
            acc = acc + jnp.dot(term, e, preferred_element_type=F32)
        o_ref[...] = acc

    return pl.pallas_call(
        body,
        name="bias_expand",
        out_shape=jax.ShapeDtypeStruct((SWA_HEADS, QB * KB2), F32),
        compiler_params=_params(),
    )(rel_bias_t, onehot_t)


PROJ_TN = {"A": 1024, "B": 1152, "C": 1024, "D": 1536}


def _local_step(x, mem, tgt, sp, w_grp, wmk, wbr, wout):
    onehot_t = jnp.asarray(_bucket_table(), BF16)
    bias_t = _bias_expand(sp["rel_bias"].T, onehot_t).reshape(SWA_HEADS, QB, KB2)
    sinks = sp["swa_sinks"].reshape(SWA_HEADS)
    wa16, wx16 = sp["w_rg_a"].astype(BF16), sp["w_rg_x"].astype(BF16)
    rnn = (sp["conv_w"], sp["conv_b"], wa16, sp["b_rg_a"], wx16, sp["b_rg_x"], sp["lru_lambda"])

    h = _rms_fwd(x, sp["pre_norm_g"], "rms_pre")
    p = {g: _mm(h, w_grp[g], "nt", F32, 1024, PROJ_TN[g], D, f"proj_{g}") for g in GROUPS}
    memn = _rms_fwd(mem, sp["mem_norm_g"], "rms_mem")
    mkv = _mm(memn, wmk, "nn", BF16, MEM, 1024, D, "mkv")
    y_rg, hseq = _rglru_fwd(p["A"], *rnn)
    y_swa, o_swa = _swa_fwd(p["B"], bias_t, sinks)
    y_mem, o_mem = _mem_fwd(p["C"], mkv)
    ys = (y_rg, y_swa, y_mem)
    z = [_mm(ys[b], wbr[b], "nn", F32, 1024, 1024, D_RNN, f"branch_out{b}") for b in range(3)]
    merged = _merge_fwd(z, p["D"])
    out = _mm(merged, wout, "nn", F32, 1024, 1024, D, "out_proj")
    sq, dy, dout, d_post = _post_loss(out, x, tgt, sp["post_norm_g"])

    dmerged = _mm(dout, wout, "nt", F32, 1024, 1024, D, "d_merged")
    d_wout = _mm(merged, dout, "tn", BF16, 1024, 1024, S, "d_wout")
    dz, dp_d = [], None
    for b in range(3):
        dz_b, dp_d = _merge_bwd(dmerged, z[b], p["D"], b, dp_d)
        dz.append(dz_b)
    dys = [_mm(dz[b], wbr[b], "nt", F32, 1024, 1024, D, f"d_branch{b}") for b in range(3)]
    d_wbr = [_mm(ys[b], dz[b], "tn", BF16, 1024, 1024, S, f"d_wbr{b}") for b in range(3)]
    dp_a, d_cw, d_cb, d_wa, d_ba, d_wx, d_bx, d_lam = _rglru_bwd(dys[0], p["A"], hseq, *rnn)
    dp_b, dk, dv, d_bias, d_sink = _swa_bwd(dys[1], p["B"], o_swa, bias_t, sinks)
    dp_b = _swa_pack(dp_b, dk, dv)
    d_rel = _relbias_grad(d_bias.reshape(SWA_HEADS, QB * KB2), onehot_t).T
    dp_c, dmkv = _mem_bwd(dys[2], p["C"], o_mem, mkv)
    dmkv16 = dmkv.astype(BF16)
    d_wmk = _mm(memn, dmkv16, "tn", BF16, 1024, 1024, MEM, "d_wmk")
    dmemn = _mm(dmkv16, wmk, "nt", F32, MEM, 1024, D, "d_memn")
    d_memg = _memnorm_bwd(dmemn, mem)
    dp = {"A": dp_a, "B": dp_b, "C": dp_c, "D": dp_d}
    dh = None
    for g in GROUPS:
        dh = _mm(dp[g], w_grp[g], "nn", F32, 1024, 1024, 2304 if g == "B" else 2048, f"d_h_{g}", acc=dh)
    d_w = {g: _mm(dp[g], h, "tn", BF16, PROJ_TN[g], 1024, S, f"d_win_{g}") for g in GROUPS}
    grad_x, d_pre = _pre_bwd(dh, x, dy, sp["pre_norm_g"])

    d_small = {
        "pre_norm_g": d_pre, "post_norm_g": d_post, "mem_norm_g": d_memg, "conv_w": d_cw, "conv_b": d_cb,
        "w_rg_a": d_wa, "b_rg_a": d_ba, "w_rg_x": d_wx, "b_rg_x": d_bx, "lru_lambda": d_lam,
        "swa_sinks": d_sink[:, 0].reshape(1, SWA_HEADS), "rel_bias": d_rel,
    }
    return sq, grad_x, d_small, d_w, d_wmk, d_wbr, d_wout


ANY = pl.BlockSpec(memory_space=pl.ANY)
SHARD_ROWS = D // N_CHIPS
GATHERED = {"A": (2048, D), "B": (2304, D), "C": (2048, D), "D": (6144, D), "mk": (D, D),
            "br0": (D_RNN, D), "br1": (D_RNN, D), "br2": (D_RNN, D), "out": (D, D)}
SHARD_SHAPES = {"win": (SHARD, D), "mk": (SHARD_ROWS, D), "br0": (D_RNN, SHARD_ROWS), "br1": (D_RNN, SHARD_ROWS),
                "br2": (D_RNN, SHARD_ROWS), "out": (SHARD_ROWS, D)}
SHARDS = tuple(SHARD_SHAPES)
HALF_AXIS = {"win": 1, "mk": 1, "br0": 0, "br1": 0, "br2": 0, "out": 1,
             "A": 1, "B": 1, "C": 1, "D": 1}


def _halved(shape, axis):
    return (shape[0] // 2, shape[1]) if axis == 0 else (shape[0], shape[1] // 2)


class Piece(NamedTuple):
    src: str
    dst: str
    rows: int
    sr0: int
    sc0: int
    dr0: int
    dc0: int
    ncols: int


def _pieces_of(jj):
    out = [Piece("win", g, n, r, 0, gr, 0, D) for r, n, g, gr in _shard_runs(jj)]
    out.append(Piece("mk", "mk", SHARD_ROWS, 0, 0, SHARD_ROWS * jj, 0, D))
    out += [Piece(f"br{b}", f"br{b}", D_RNN, 0, 0, 0, SHARD_ROWS * jj, SHARD_ROWS) for b in range(3)]
    out.append(Piece("out", "out", SHARD_ROWS, 0, 0, SHARD_ROWS * jj, 0, D))
    return out


def _half_rect(ref, p, side, which):
    r0, c0 = (p.sr0, p.sc0) if side == "src" else (p.dr0, p.dc0)
    if HALF_AXIS[p.src] == 1:
        return _rect(ref, r0, p.rows, c0 + which * (p.ncols // 2), p.ncols // 2)
    return _rect(ref, r0 + which * (p.rows // 2), p.rows // 2, c0, p.ncols)


def _rect_in_half(ref, p, side):
    r0, c0 = (p.sr0, p.sc0) if side == "src" else (p.dr0, p.dc0)
    if HALF_AXIS[p.src] == 1:
        return _rect(ref, r0, p.rows, 0, p.ncols // 2)
    return _rect(ref, 0, p.rows // 2, c0, p.ncols)


MAX_PIECES = max(len(_pieces_of(jj)) for jj in range(N_CHIPS))


def _rect(ref, r0, rows, c0, ncols):
    return ref.at[pl.ds(r0, rows), pl.ds(c0, ncols)]


def _position():
    x, y, c = lax.axis_index("x"), lax.axis_index("y"), lax.axis_index("c")
    return x, y, c, 2 * x + y


def _all_gather_weights(shards):
    names = tuple(GATHERED)

    def body(*refs):
        src = dict(zip(SHARDS, refs[:len(SHARDS)]))
        dst = dict(zip(names, refs[len(SHARDS):len(SHARDS) + len(names)]))
        send_sems, ici_sems, fwd_sems, d2d_sems, loc_sems = refs[len(SHARDS) + len(names):]
        x, y, c, j = _position()

        def src_half(p, which):
            return _half_rect(src[p.src], p, "src", which)

        def dst_half(p, which):
            return _half_rect(dst[p.dst], p, "dst", which)

        def ici_copy(jj, i, p, kk):
            return pltpu.make_async_remote_copy(
                src_ref=src_half(p, c), dst_ref=dst_half(p, c), send_sem=send_sems.at[(i * N_CHIPS) + kk],
                recv_sem=ici_sems.at[jj * MAX_PIECES + i], device_id=(kk // 2, kk % 2, c), device_id_type=MESH)

        def fwd_copy(jj, i, p, which):
            return pltpu.make_async_remote_copy(
                src_ref=dst_half(p, which), dst_ref=dst_half(p, which), send_sem=fwd_sems.at[jj * MAX_PIECES + i],
                recv_sem=d2d_sems.at[jj * MAX_PIECES + i], device_id=(x, y, 1 - c), device_id_type=MESH)

        def loc_copy(i, p):
            return pltpu.make_async_copy(_rect(src[p.src], p.sr0, p.rows, p.sc0, p.ncols),
                                         _rect(dst[p.dst], p.dr0, p.rows, p.dc0, p.ncols), loc_sems.at[i])

        for jj in range(N_CHIPS):
            @pl.when(j == jj)
            def _():
                for i, p in enumerate(_pieces_of(jj)):
                    loc_copy(i, p).start()
                    for kk in range(N_CHIPS):
                        if kk != jj:
                            ici_copy(jj, i, p, kk).start()
        for jj in range(N_CHIPS):
            @pl.when(j != jj)
            def _():
                for i, p in enumerate(_pieces_of(jj)):
                    ici_copy(jj, i, p, jj).wait_recv()
                    fwd_copy(jj, i, p, c).start()
        for jj in range(N_CHIPS):
            @pl.when(j != jj)
            def _():
                for i, p in enumerate(_pieces_of(jj)):
                    fwd_copy(jj, i, p, 1 - c).wait_recv()
        for jj in range(N_CHIPS):
            @pl.when(j != jj)
            def _():
                for i, p in enumerate(_pieces_of(jj)):
                    fwd_copy(jj, i, p, c).wait_send()
        for jj in range(N_CHIPS):
            @pl.when(j == jj)
            def _():
                for i, p in enumerate(_pieces_of(jj)):
                    for kk in range(N_CHIPS):
                        if kk != jj:
                            ici_copy(jj, i, p, kk).wait_send()
                    loc_copy(i, p).wait()

    outs = pl.pallas_call(
        body,
        name="all_gather_weights",
        in_specs=[ANY] * len(SHARDS),
        out_specs=[ANY] * len(names),
        out_shape=[jax.ShapeDtypeStruct(GATHERED[n], BF16) for n in names],
        scratch_shapes=[pltpu.SemaphoreType.DMA((MAX_PIECES * N_CHIPS,)),
                        pltpu.SemaphoreType.DMA((MAX_PIECES * N_CHIPS,)),
                        pltpu.SemaphoreType.DMA((MAX_PIECES * N_CHIPS,)),
                        pltpu.SemaphoreType.DMA((MAX_PIECES * N_CHIPS,)),
                        pltpu.SemaphoreType.DMA((MAX_PIECES,))],
        compiler_params=pltpu.CompilerParams(has_side_effects=True),
    )(*[shards[n] for n in SHARDS])
    return dict(zip(names, outs))


def _own_half(ref, shape, axis, which):
    if axis == 1:
        return ref.at[:, pl.ds(which * (shape[1] // 2), shape[1] // 2)]
    return ref.at[pl.ds(which * (shape[0] // 2), shape[0] // 2), :]


def _swap_halves(grads):
    names = tuple(GATHERED)
    n_tr = len(names)

    def body(*refs):
        src = dict(zip(names, refs[:len(names)]))
        dst = dict(zip(names, refs[len(names):2 * len(names)]))
        send_sems, recv_sems = refs[2 * len(names):]
        x, y, c, _ = _position()
        copies = [pltpu.make_async_remote_copy(
            src_ref=_own_half(src[n], GATHERED[n], HALF_AXIS[n], 1 - c), dst_ref=dst[n],
            send_sem=send_sems.at[k], recv_sem=recv_sems.at[k],
            device_id=(x, y, 1 - c), device_id_type=MESH) for k, n in enumerate(names)]
        for cp in copies:
            cp.start()
        for cp in copies:
            cp.wait_recv()
        for cp in copies:
            cp.wait_send()

    outs = pl.pallas_call(
        body,
        name="swap_halves",
        in_specs=[ANY] * len(names),
        out_specs=[ANY] * len(names),
        out_shape=[jax.ShapeDtypeStruct(_halved(GATHERED[n], HALF_AXIS[n]), BF16) for n in names],
        scratch_shapes=[pltpu.SemaphoreType.DMA((n_tr,)), pltpu.SemaphoreType.DMA((n_tr,))],
        compiler_params=pltpu.CompilerParams(has_side_effects=True),
    )(*[grads[n] for n in names])
    return dict(zip(names, outs))


ADD_ROWS = 256


def _add_half(full, recv, c_arr, name):
    rows, cols = recv.shape
    if HALF_AXIS[name] == 1:
        index = lambda i, c_ref: (i, c_ref[0])
    else:
        nb = rows // ADD_ROWS
        index = lambda i, c_ref: (nb * c_ref[0] + i, 0)

    def body(c_ref, a_ref, b_ref, o_ref):
        o_ref[...] = (a_ref[...].astype(F32) + b_ref[...].astype(F32)).astype(BF16)

    return pl.pallas_call(
        body,
        name=f"add_half_{name}",
        grid_spec=pltpu.PrefetchScalarGridSpec(
            num_scalar_prefetch=1,
            grid=(rows // ADD_ROWS,),
            in_specs=[pl.BlockSpec((ADD_ROWS, cols), index), pl.BlockSpec((ADD_ROWS, cols), lambda i, c_ref: (i, 0))],
            out_specs=pl.BlockSpec((ADD_ROWS, cols), lambda i, c_ref: (i, 0)),
        ),
        out_shape=jax.ShapeDtypeStruct((rows, cols), BF16),
        compiler_params=_params(("parallel",)),
    )(c_arr, full, recv)


SLOT_SHAPES = {n: _halved(SHARD_SHAPES[n], HALF_AXIS[n]) for n in SHARDS}


def _scatter_to_owners(halves):
    names = tuple(GATHERED)

    def body(*refs):
        src = dict(zip(names, refs[:len(names)]))
        dst = dict(zip(SHARDS, refs[len(names):len(names) + len(SHARDS)]))
        send_sems, recv_sems, loc_sems = refs[len(names) + len(SHARDS):]
        x, y, c, j = _position()

        def src_rect(p):
            return _rect_in_half(src[p.dst], p, "dst")

        def dst_rect(p, slot):
            return _rect_in_half(dst[p.src].at[slot], p, "src")

        def remote(jj, kk, i, p):
            return pltpu.make_async_remote_copy(
                src_ref=src_rect(p), dst_ref=dst_rect(p, jj), send_sem=send_sems.at[kk * MAX_PIECES + i],
                recv_sem=recv_sems.at[jj * MAX_PIECES + i], device_id=(kk // 2, kk % 2, c), device_id_type=MESH)

        def local(jj, i, p):
            return pltpu.make_async_copy(src_rect(p), dst_rect(p, jj), loc_sems.at[i])

        for jj in range(N_CHIPS):
            @pl.when(j == jj)
            def _():
                for kk in range(N_CHIPS):
                    for i, p in enumerate(_pieces_of(kk)):
                        (local(jj, i, p) if kk == jj else remote(jj, kk, i, p)).start()
                for ss in range(N_CHIPS):
                    if ss != jj:
                        for i, p in enumerate(_pieces_of(jj)):
                            remote(ss, jj, i, p).wait_recv()
                for kk in range(N_CHIPS):
                    for i, p in enumerate(_pieces_of(kk)):
                        if kk == jj:
                            local(jj, i, p).wait()
                        else:
                            remote(jj, kk, i, p).wait_send()

    outs = pl.pallas_call(
        body,
        name="scatter_to_owners",
        in_specs=[ANY] * len(names),
        out_specs=[ANY] * len(SHARDS),
        out_shape=[jax.ShapeDtypeStruct((N_CHIPS,) + SLOT_SHAPES[n], BF16) for n in SHARDS],
        scratch_shapes=[pltpu.SemaphoreType.DMA((MAX_PIECES * N_CHIPS,)),
                        pltpu.SemaphoreType.DMA((MAX_PIECES * N_CHIPS,)),
                        pltpu.SemaphoreType.DMA((MAX_PIECES,))],
        compiler_params=pltpu.CompilerParams(has_side_effects=True),
    )(*[halves[n] for n in names])
    return dict(zip(SHARDS, outs))


SUM_ROWS = {"win": 448, "mk": 256, "br0": 256, "br1": 256, "br2": 256, "out": 256}


def _sum_slots(slots, c_arr, name):
    _, rows, cols = slots.shape
    tr = SUM_ROWS[name]
    nb = rows // tr
    if HALF_AXIS[name] == 1:
        out_index = lambda i, c_ref: (i, c_ref[0])
    else:
        out_index = lambda i, c_ref: (nb * c_ref[0] + i, 0)

    def body(c_ref, s_ref, o_ref):
        acc = s_ref[0].astype(F32)
        for k in range(1, N_CHIPS):
            acc = acc + s_ref[k].astype(F32)
        o_ref[...] = acc

    return pl.pallas_call(
        body,
        name=f"sum_slots_{name}",
        grid_spec=pltpu.PrefetchScalarGridSpec(
            num_scalar_prefetch=1,
            grid=(nb,),
            in_specs=[pl.BlockSpec((N_CHIPS, tr, cols), lambda i, c_ref: (0, i, 0))],
            out_specs=pl.BlockSpec((tr, cols), out_index),
        ),
        out_shape=jax.ShapeDtypeStruct(SHARD_SHAPES[name], F32),
        compiler_params=_params(("parallel",)),
    )(c_arr, slots)


def _share_sums(sums):
    def body(*refs):
        bufs = refs[len(SHARDS):2 * len(SHARDS)]
        send_sems, recv_sems = refs[2 * len(SHARDS):]
        x, y, c, _ = _position()
        copies = []
        for k, (n, b) in enumerate(zip(SHARDS, bufs)):
            mine = _own_half(b, SHARD_SHAPES[n], HALF_AXIS[n], c)
            copies.append(pltpu.make_async_remote_copy(
                src_ref=mine, dst_ref=mine, send_sem=send_sems.at[k], recv_sem=recv_sems.at[k],
                device_id=(x, y, 1 - c), device_id_type=MESH))
        for cp in copies:
            cp.start()
        for cp in copies:
            cp.wait_recv()
        for cp in copies:
            cp.wait_send()

    outs = pl.pallas_call(
        body,
        name="share_sums",
        in_specs=[ANY] * len(SHARDS),
        out_specs=[ANY] * len(SHARDS),
        out_shape=[jax.ShapeDtypeStruct(sums[n].shape, F32) for n in SHARDS],
        input_output_aliases={k: k for k in range(len(SHARDS))},
        scratch_shapes=[pltpu.SemaphoreType.DMA((len(SHARDS),)), pltpu.SemaphoreType.DMA((len(SHARDS),))],
        compiler_params=pltpu.CompilerParams(has_side_effects=True),
    )(*[sums[n] for n in SHARDS])
    return dict(zip(SHARDS, outs))


N_DEV = 8


def _all_reduce_small(pack, name):
    rows = pack.shape[0]

    def body(p_ref, o_ref, land, send_sems, recv_sems):
        x, y, c, _ = _position()
        me = 4 * x + 2 * y + c

        def copy(o):
            return pltpu.make_async_remote_copy(
                src_ref=p_ref, dst_ref=land.at[me], send_sem=send_sems.at[o], recv_sem=recv_sems.at[me],
                device_id=(o // 4, (o // 2) % 2, o % 2), device_id_type=MESH)

        def arrival(o):
            return pltpu.make_async_remote_copy(
                src_ref=p_ref, dst_ref=land.at[o], send_sem=send_sems.at[o], recv_sem=recv_sems.at[o],
                device_id=(o // 4, (o // 2) % 2, o % 2), device_id_type=MESH)

        for o in range(N_DEV):
            @pl.when(me != o)
            def _():
                copy(o).start()
        land[me] = p_ref[...]
        for o in range(N_DEV):
            @pl.when(me != o)
            def _():
                arrival(o).wait_recv()
        acc = land[0]
        for o in range(1, N_DEV):
            acc = acc + land[o]
        o_ref[...] = acc
        for o in range(N_DEV):
            @pl.when(me != o)
            def _():
                copy(o).wait_send()

    vmem = pl.BlockSpec(memory_space=pltpu.VMEM)
    return pl.pallas_call(
        body,
        name=name,
        in_specs=[vmem],
        out_specs=vmem,
        out_shape=jax.ShapeDtypeStruct((rows, LANE), F32),
        scratch_shapes=[pltpu.VMEM((N_DEV, rows, LANE), F32), pltpu.SemaphoreType.DMA((N_DEV,)),
                        pltpu.SemaphoreType.DMA((N_DEV,))],
        compiler_params=pltpu.CompilerParams(has_side_effects=True, vmem_limit_bytes=VMEM_LIMIT),
    )(pack)


def _adamw(w, g, m, v, name, tr):
    rows, cols = w.shape
    tr = min(tr, rows)

    def body(w_ref, g_ref, m_ref, v_ref, d_ref, nm_ref, nv_ref):
        gv = g_ref[...]
        nm = ADAM_B1 * m_ref[...] + (1.0 - ADAM_B1) * gv
        nv = ADAM_B2 * v_ref[...] + (1.0 - ADAM_B2) * (gv * gv)
        nm_ref[...] = nm
        nv_ref[...] = nv
        m_hat = nm / (1.0 - ADAM_B1 ** ADAM_STEP)
        v_hat = nv / (1.0 - ADAM_B2 ** ADAM_STEP)
        d_ref[...] = -ADAM_LR * (m_hat / (jnp.sqrt(v_hat) + ADAM_EPS) + ADAM_WD * w_ref[...])

    blk = pl.BlockSpec((tr, cols), lambda i: (i, 0))
    shape = jax.ShapeDtypeStruct((rows, cols), F32)
    return pl.pallas_call(
        body,
        name=f"adamw_{name}",
        grid=(rows // tr,),
        in_specs=[blk] * 4,
        out_specs=[blk] * 3,
        out_shape=[shape] * 3,
        compiler_params=_params(("parallel",)),
    )(w, g, m, v)


SMALL = (("pre_norm_g", (1, D)), ("post_norm_g", (1, D)), ("mem_norm_g", (1, D)), ("conv_w", (CONV_W, D_RNN)),
         ("conv_b", (1, D_RNN)), ("w_rg_a", (RNN_BLOCKS, LANE, LANE)), ("b_rg_a", (1, D_RNN)),
         ("w_rg_x", (RNN_BLOCKS, LANE, LANE)), ("b_rg_x", (1, D_RNN)), ("lru_lambda", (1, D_RNN)),
         ("swa_sinks", (1, SWA_HEADS)), ("rel_bias", (REL_BUCKETS, SWA_HEADS)))
PACK_ROWS = 2176


def _slot_len(shape):
    return -(-math.prod(shape) // LANE) * LANE


def _pack(values):
    parts = []
    for name, shape in SMALL:
        flat = values[name].reshape(-1).astype(F32)
        parts.append(jnp.pad(flat, (0, _slot_len(shape) - flat.shape[0])))
    flat = jnp.concatenate(parts)
    return jnp.pad(flat, (0, PACK_ROWS * LANE - flat.shape[0])).reshape(PACK_ROWS, LANE)


def _unpack(pack, shapes=None):
    flat = pack.reshape(-1)
    out, off = {}, 0
    for name, shape in SMALL:
        shp = shape if shapes is None or name not in shapes else shapes[name]
        out[name] = flat[off:off + math.prod(shp)].reshape(shp)
        off += _slot_len(shape)
    return out


TWIN_WEIGHTS = ("pre_norm_g", "post_norm_g", "mem_norm_g", "w_in", "conv_w", "conv_b", "w_rg_a", "b_rg_a", "w_rg_x",
                "b_rg_x", "lru_lambda", "swa_sinks", "rel_bias", "w_mem_kv", "w_br_rg", "w_br_swa", "w_br_mem", "w_out")
BIG = {"w_in": "win", "w_mem_kv": "mk", "w_br_rg": "br0", "w_br_swa": "br1", "w_br_mem": "br2", "w_out": "out"}


def kernel(x, mem, pre_norm_g, post_norm_g, mem_norm_g, w_in, conv_w, conv_b, w_rg_a, b_rg_a, w_rg_x, b_rg_x, lru_lambda, swa_sinks, rel_bias, w_mem_kv, w_br_rg, w_br_swa, w_br_mem, w_out, loss_target, m_pre_norm_g, m_post_norm_g, m_mem_norm_g, m_w_in, m_conv_w, m_conv_b, m_w_rg_a, m_b_rg_a, m_w_rg_x, m_b_rg_x, m_lru_lambda, m_swa_sinks, m_rel_bias, m_w_mem_kv, m_w_br_rg, m_w_br_swa, m_w_br_mem, m_w_out, v_pre_norm_g, v_post_norm_g, v_mem_norm_g, v_w_in, v_conv_w, v_conv_b, v_w_rg_a, v_b_rg_a, v_w_rg_x, v_b_rg_x, v_lru_lambda, v_swa_sinks, v_rel_bias, v_w_mem_kv, v_w_br_rg, v_w_br_swa, v_w_br_mem, v_w_out):
    args = dict(locals())
    out_shapes = {n: args[n].shape for n in TWIN_WEIGHTS}
    w = {n: (args[n] if n == "rel_bias" else args[n][0]) for n in TWIN_WEIGHTS}
    m = {n: (args["m_" + n] if n == "rel_bias" else args["m_" + n][0]) for n in TWIN_WEIGHTS}
    v = {n: (args["v_" + n] if n == "rel_bias" else args["v_" + n][0]) for n in TWIN_WEIGHTS}
    for d in (w, m, v):
        for n, shape in SMALL:
            if n != "conv_w":
                d[n] = d[n].reshape(shape)

    xi, yi, ci = lax.axis_index("x"), lax.axis_index("y"), lax.axis_index("c")
    chip = 2 * xi + yi
    c_arr = ci.astype(jnp.int32).reshape(1)
    zero = jnp.zeros((), jnp.int32)
    cw0 = (chip * (D_RNN // N_CHIPS)).astype(jnp.int32)

    placed = lax.dynamic_update_slice(jnp.zeros((CONV_W, D_RNN), F32), w["conv_w"], (zero, cw0))
    placed = jnp.where(ci == 0, placed, 0.0).reshape(CONV_W * D_RNN // LANE, LANE)
    conv_w_full = _all_reduce_small(placed, "gather_conv_w").reshape(CONV_W, D_RNN)

    for d in (w, m, v):
        d["w_in"] = d["w_in"].T
    shards = {s: w[n].astype(BF16) for n, s in BIG.items()}
    gathered = _all_gather_weights(shards)

    sp = {n: w[n] for n, _ in SMALL}
    sp["conv_w"] = conv_w_full
    sq, grad_x, d_small, d_w, d_wmk, d_wbr, d_wout = _local_step(
        x[0], mem[0], loss_target[0], sp, {g: gathered[g] for g in GROUPS}, gathered["mk"],
        [gathered["br0"], gathered["br1"], gathered["br2"]], gathered["out"])
    loss = lax.psum(sq[0, 0] * (0.5 / D), ("x", "y", "c"))

    grads = dict(d_w)
    grads.update({"mk": d_wmk, "br0": d_wbr[0], "br1": d_wbr[1], "br2": d_wbr[2], "out": d_wout})
    received = _swap_halves(grads)
    halves = {n: _add_half(grads[n], received[n], c_arr, n) for n in GATHERED}
    slots = _scatter_to_owners(halves)
    sums = _share_sums({n: _sum_slots(slots[n], c_arr, n) for n in SHARDS})
    g_big = {n: sums[s] for n, s in BIG.items()}

    g_small = _unpack(_all_reduce_small(_pack(d_small), "all_reduce_small"))
    g_small["conv_w"] = lax.dynamic_slice(g_small["conv_w"], (zero, cw0), (CONV_W, D_RNN // N_CHIPS))

    grad, delta, new_m, new_v = {}, {}, {}, {}
    for n, s in BIG.items():
        grad[n] = g_big[n]
        delta[n], new_m[n], new_v[n] = _adamw(w[n], g_big[n], m[n], v[n], s, 224 if n == "w_in" else 128)
    for group in (grad, delta, new_m, new_v):
        group["w_in"] = group["w_in"].T
    d_, m_, v_ = _adamw(_pack(w), _pack(g_small), _pack(m), _pack(v), "small", PACK_ROWS)
    shard_shapes = {"conv_w": (CONV_W, D_RNN // N_CHIPS)}
    d_, m_, v_ = (_unpack(a, shard_shapes) for a in (d_, m_, v_))
    for n, _ in SMALL:
        grad[n], delta[n], new_m[n], new_v[n] = g_small[n], d_[n], m_[n], v_[n]

    outs = [loss, grad_x.reshape(1, S, D)]
    for group in (grad, delta, new_m, new_v):
        outs += [group[n].reshape(out_shapes[n]) for n in TWIN_WEIGHTS]
    return tuple(outs)
```

```python
import functools
import math
from typing import NamedTuple

import jax
import jax.numpy as jnp
from jax import lax
from jax.experimental import pallas as pl
from jax.experimental.pallas import tpu as pltpu

F32 = jnp.float32
BF16 = jnp.bfloat16
MESH = pl.DeviceIdType.MESH

S = 2048
D = 2048
MEM = 256
D_RNN = 1024
RNN_BLOCKS = 8
CONV_W = 4
LRU_C = 8.0
SWA_HEADS = 16
SWA_HD = 64
WINDOW = 128
MEM_HEADS = 4
MEM_HD = 256
REL_BUCKETS = 32
REL_MAX_DIST = 128
EPS = 1e-6
NEG_INF = -1e30
LANE = 128
SHARD = 3136
HALF_TILE = 64
N_CHIPS = 4
VMEM_LIMIT = 56 * 1024 * 1024

ADAM_LR = 0.001
ADAM_B1 = 0.9
ADAM_B2 = 0.999
ADAM_EPS = 1e-08
ADAM_WD = 0.01
ADAM_STEP = 10

GROUP_TILES = {"A": 16, "B": 18, "C": 16, "D": 48}
GROUPS = ("A", "B", "C", "D")


def _params(sem=None):
    return pltpu.CompilerParams(dimension_semantics=sem, vmem_limit_bytes=VMEM_LIMIT)


def _sigmoid(v):
    return jax.nn.sigmoid(v)


def _tile_home(t):
    if t < 16:
        return "A", t
    if t < 24:
        return "B", t - 16
    if t < 26:
        return "B", t - 24 + 16
    if t < 34:
        return "B", t - 26 + 8
    if t < 50:
        return "C", t - 34
    return "D", t - 50


def _shard_runs(j):
    runs = []
    per_shard = SHARD // HALF_TILE
    for q in range(per_shard * j, per_shard * (j + 1)):
        g, gt = _tile_home(q // 2)
        row = gt * LANE + (q % 2) * HALF_TILE
        if runs and runs[-1][2] == g and runs[-1][3] + runs[-1][1] == row:
            runs[-1][1] += HALF_TILE
        else:
            runs.append([(q - per_shard * j) * HALF_TILE, HALF_TILE, g, row])
    return [tuple(r) for r in runs]


_DIMS = {
    "nn": (((1,), (0,)), ((), ())),
    "nt": (((1,), (1,)), ((), ())),
    "tn": (((0,), (0,)), ((), ())),
}


def _mm(a, b, mode, out_dtype, tm, tn, tk, name, acc=None):
    if mode == "nn":
        (m, k), n = a.shape, b.shape[1]
    elif mode == "nt":
        (m, k), n = a.shape, b.shape[0]
    else:
        (k, m), n = a.shape, b.shape[1]
    tm, tn, tk = min(tm, m), min(tn, n), min(tk, k)
    assert m % tm == 0 and n % tn == 0 and k % tk == 0, (name, m, n, k)
    nk = k // tk
    has_acc = acc is not None

    def body(*refs):
        a_ref, b_ref = refs[0], refs[1]
        o_ref = refs[3] if has_acc else refs[2]
        p = lax.dot_general(a_ref[...], b_ref[...], _DIMS[mode], preferred_element_type=F32)

        def finish(v):
            if has_acc:
                v = v + refs[2][...]
            o_ref[...] = v.astype(out_dtype)

        if nk == 1:
            finish(p)
        else:
            s_ref = refs[-1]
            kk = pl.program_id(2)

            @pl.when(kk == 0)
            def _():
                s_ref[...] = p

            @pl.when(kk > 0)
            def _():
                s_ref[...] += p

            @pl.when(kk == nk - 1)
            def _():
                finish(s_ref[...])

    if mode == "nn":
        a_spec = pl.BlockSpec((tm, tk), lambda i, j, kk: (i, kk))
        b_spec = pl.BlockSpec((tk, tn), lambda i, j, kk: (kk, j))
    elif mode == "nt":
        a_spec = pl.BlockSpec((tm, tk), lambda i, j, kk: (i, kk))
        b_spec = pl.BlockSpec((tn, tk), lambda i, j, kk: (j, kk))
    else:
        a_spec = pl.BlockSpec((tk, tm), lambda i, j, kk: (kk, i))
        b_spec = pl.BlockSpec((tk, tn), lambda i, j, kk: (kk, j))
    o_spec = pl.BlockSpec((tm, tn), lambda i, j, kk: (i, j))
    in_specs = [a_spec, b_spec] + ([o_spec] if has_acc else [])
    args = (a, b) + ((acc,) if has_acc else ())
    return pl.pallas_call(
        body,
        name=name,
        grid=(m // tm, n // tn, nk),
        in_specs=in_specs,
        out_specs=o_spec,
        out_shape=jax.ShapeDtypeStruct((m, n), out_dtype),
        scratch_shapes=[pltpu.VMEM((tm, tn), F32)] if nk > 1 else [],
        compiler_params=_params(("parallel", "parallel", "arbitrary")),
    )(*args)


def _rms_fwd(x, g, name, ts=256):
    r, d = x.shape

    def body(x_ref, g_ref, o_ref):
        xv = x_ref[...]
        inv = lax.rsqrt(jnp.mean(xv * xv, axis=-1, keepdims=True) + EPS)
        o_ref[...] = (xv * inv * g_ref[...]).astype(BF16)

    return pl.pallas_call(
        body,
        name=name,
        grid=(r // ts,),
        in_specs=[pl.BlockSpec((ts, d), lambda i: (i, 0)), pl.BlockSpec((1, d), lambda i: (0, 0))],
        out_specs=pl.BlockSpec((ts, d), lambda i: (i, 0)),
        out_shape=jax.ShapeDtypeStruct((r, d), BF16),
        compiler_params=_params(("parallel",)),
    )(x, g)


def _post_loss(out, x, tgt, g_post, ts=256):
    n = S // ts

    def body(o_ref, x_ref, t_ref, g_ref, sq_ref, dy_ref, do_ref, dg_ref):
        i = pl.program_id(0)

        @pl.when(i == 0)
        def _():
            sq_ref[...] = jnp.zeros_like(sq_ref)
            dg_ref[...] = jnp.zeros_like(dg_ref)

        ov = o_ref[...]
        g = g_ref[...]
        inv = lax.rsqrt(jnp.mean(ov * ov, axis=-1, keepdims=True) + EPS)
        on = ov * inv
        err = x_ref[...] + on * g - t_ref[...]
        sq_ref[...] += jnp.sum(err * err)
        dy = err * (1.0 / D)
        dy_ref[...] = dy
        dg_ref[...] += jnp.sum(dy * on, axis=0, keepdims=True)
        don = dy * g
        do_ref[...] = (inv * (don - on * jnp.mean(don * on, axis=-1, keepdims=True))).astype(BF16)

    row = pl.BlockSpec((ts, D), lambda i: (i, 0))
    vec = pl.BlockSpec((1, D), lambda i: (0, 0))
    return pl.pallas_call(
        body,
        name="post_loss",
        grid=(n,),
        in_specs=[row, row, row, vec],
        out_specs=[pl.BlockSpec((8, LANE), lambda i: (0, 0)), row, row, vec],
        out_shape=[
            jax.ShapeDtypeStruct((8, LANE), F32),
            jax.ShapeDtypeStruct((S, D), F32),
            jax.ShapeDtypeStruct((S, D), BF16),
            jax.ShapeDtypeStruct((1, D), F32),
        ],
        compiler_params=_params(("arbitrary",)),
    )(out, x, tgt, g_post)


def _pre_bwd(dh, x, dy, g_pre, ts=256):
    n = S // ts

    def body(dh_ref, x_ref, dy_ref, g_ref, gx_ref, dg_ref):
        i = pl.program_id(0)

        @pl.when(i == 0)
        def _():
            dg_ref[...] = jnp.zeros_like(dg_ref)

        xv = x_ref[...]
        dhv = dh_ref[...]
        inv = lax.rsqrt(jnp.mean(xv * xv, axis=-1, keepdims=True) + EPS)
        xn = xv * inv
        dg_ref[...] += jnp.sum(dhv * xn, axis=0, keepdims=True)
        dxn = dhv * g_ref[...]
        gx_ref[...] = dy_ref[...] + inv * (dxn - xn * jnp.mean(dxn * xn, axis=-1, keepdims=True))

    row = pl.BlockSpec((ts, D), lambda i: (i, 0))
    vec = pl.BlockSpec((1, D), lambda i: (0, 0))
    return pl.pallas_call(
        body,
        name="pre_bwd",
        grid=(n,),
        in_specs=[row, row, row, vec],
        out_specs=[row, vec],
        out_shape=[jax.ShapeDtypeStruct((S, D), F32), jax.ShapeDtypeStruct((1, D), F32)],
        compiler_params=_params(("arbitrary",)),
    )(dh, x, dy, g_pre)


def _memnorm_bwd(dmemn, mem):
    def body(d_ref, m_ref, dg_ref):
        mv = m_ref[...]
        inv = lax.rsqrt(jnp.mean(mv * mv, axis=-1, keepdims=True) + EPS)
        dg_ref[...] = jnp.sum(d_ref[...] * mv * inv, axis=0, keepdims=True)

    return pl.pallas_call(
        body,
        name="memnorm_bwd",
        out_shape=jax.ShapeDtypeStruct((1, D), F32),
        compiler_params=_params(),
    )(dmemn, mem)


T_RNN = 256


def _neg_expm1(z):
    poly = -z * (1.0 + z * (0.5 + z * (1.0 / 6 + z * (1.0 / 24 + z * (1.0 / 120 + z * (1.0 / 720))))))
    return jnp.where(z > -0.1, poly, 1.0 - jnp.exp(z))


def _softplus_neg(lam):
    return jnp.maximum(-lam, 0.0) + jnp.log1p(jnp.exp(-jnp.abs(lam)))


def _rnn_gates(conv, wa_ref, ba, wx_ref, bx, lam, first_row):
    cbf = conv.astype(BF16)
    ga, gx = [], []
    for n in range(RNN_BLOCKS):
        c_n = cbf[:, n * LANE:(n + 1) * LANE]
        ga.append(jnp.dot(c_n, wa_ref[n], preferred_element_type=F32))
        gx.append(jnp.dot(c_n, wx_ref[n], preferred_element_type=F32))
    gate_r = _sigmoid(jnp.concatenate(ga, axis=1) + ba)
    gate_i = _sigmoid(jnp.concatenate(gx, axis=1) + bx)
    sp = _softplus_neg(lam)
    log_a = -LRU_C * gate_r * sp
    a = jnp.exp(log_a)
    mult_raw = jnp.sqrt(_neg_expm1(2.0 * log_a))
    mult = jnp.where(first_row, 1.0, mult_raw)
    return cbf, gate_r, gate_i, sp, a, mult_raw, mult


def _rglru_fwd(p_a, conv_w, conv_b, wa, ba, wx, bx, lam):
    t = T_RNN
    n = S // t

    def body(xr_ref, g_ref, cw_ref, cb_ref, wa_ref, ba_ref, wx_ref, bx_ref, lam_ref,
             y_ref, h_ref, xp_s, hcar, a_s, b_s):
        i = pl.program_id(0)

        @pl.when(i == 0)
        def _():
            xp_s[0:8, :] = jnp.zeros((8, D_RNN), F32)
            hcar[...] = jnp.zeros_like(hcar)

        @pl.when(i > 0)
        def _():
            xp_s[0:8, :] = xp_s[t:t + 8, :]

        xp_s[8:8 + t, :] = xr_ref[...]
        conv = cb_ref[...]
        for k in range(CONV_W):
            conv = conv + cw_ref[k:k + 1, :] * xp_s[8 - k:8 - k + t, :]
        rows = i * t + lax.broadcasted_iota(jnp.int32, (t, 1), 0)
        _, _, gate_i, _, a, _, mult = _rnn_gates(
            conv, wa_ref, ba_ref[...], wx_ref, bx_ref[...], lam_ref[...], rows == 0)
        a_s[...] = a
        b_s[...] = mult * gate_i * conv

        def step(tt, h):
            h = a_s[pl.ds(tt, 1), :] * h + b_s[pl.ds(tt, 1), :]
            h_ref[pl.ds(tt, 1), :] = h
            return h

        hcar[...] = lax.fori_loop(0, t, step, hcar[...], unroll=8)
        g = g_ref[...]
        y_ref[...] = (h_ref[...] * (g * _sigmoid(g))).astype(BF16)

    blk = lambda c: pl.BlockSpec((t, D_RNN), lambda i: (i, c))
    full = lambda shape: pl.BlockSpec(shape, lambda i: (0,) * len(shape))
    return pl.pallas_call(
        body,
        name="rglru_fwd",
        grid=(n,),
        in_specs=[blk(0), blk(1), full((CONV_W, D_RNN)), full((1, D_RNN)),
                  full((RNN_BLOCKS, LANE, LANE)), full((1, D_RNN)),
                  full((RNN_BLOCKS, LANE, LANE)), full((1, D_RNN)), full((1, D_RNN))],
        out_specs=[blk(0), blk(0)],
        out_shape=[jax.ShapeDtypeStruct((S, D_RNN), BF16), jax.ShapeDtypeStruct((S, D_RNN), F32)],
        scratch_shapes=[pltpu.VMEM((t + 8, D_RNN), F32), pltpu.VMEM((1, D_RNN), F32),
                        pltpu.VMEM((t, D_RNN), F32), pltpu.VMEM((t, D_RNN), F32)],
        compiler_params=_params(("arbitrary",)),
    )(p_a, p_a, conv_w, conv_b, wa, ba, wx, bx, lam)


def _rglru_bwd(dy, p_a, hseq, conv_w, conv_b, wa, ba, wx, bx, lam):
    t = T_RNN
    n = S // t
    rb = t // 8

    def body(dy_ref, xr_ref, g_ref, h_ref, xrp_ref, hp_ref, cw_ref, cb_ref, wa_ref, ba_ref, wx_ref, bx_ref, lam_ref,
             dp_ref, dcw_ref, dcb_ref, dwa_ref, dba_ref, dwx_ref, dbx_ref, dlam_ref,
             xp_s, hp_s, dxp_s, lamcar, a_s, dh_s, lam_s):
        i = pl.program_id(0)
        r = n - 1 - i

        @pl.when(i == 0)
        def _():
            for ref in (dcw_ref, dcb_ref, dwa_ref, dba_ref, dwx_ref, dbx_ref, dlam_ref, lamcar):
                ref[...] = jnp.zeros_like(ref)
            dxp_s[t:t + 8, :] = jnp.zeros((8, D_RNN), F32)

        @pl.when(i > 0)
        def _():
            dxp_s[t:t + 8, :] = dxp_s[0:8, :]

        has_prev = r > 0
        xp_s[0:8, :] = jnp.where(has_prev, xrp_ref[...], 0.0)
        xp_s[8:8 + t, :] = xr_ref[...]
        hp_s[0:8, :] = jnp.where(has_prev, hp_ref[...], 0.0)
        hp_s[8:8 + t, :] = h_ref[...]
        xs = [xp_s[8 - k:8 - k + t, :] for k in range(CONV_W)]
        conv = cb_ref[...]
        for k in range(CONV_W):
            conv = conv + cw_ref[k:k + 1, :] * xs[k]
        rows = r * t + lax.broadcasted_iota(jnp.int32, (t, 1), 0)
        first = rows == 0
        lam_p = lam_ref[...]
        cbf, gate_r, gate_i, sp, a, mult_raw, mult = _rnn_gates(
            conv, wa_ref, ba_ref[...], wx_ref, bx_ref[...], lam_p, first)

        g = g_ref[...]
        sg = _sigmoid(g)
        dyv = dy_ref[...]
        a_s[...] = a
        dh_s[...] = dyv * (g * sg)
        dg = dyv * h_ref[...] * (sg * (1.0 + g * (1.0 - sg)))

        def step(jj, car):
            tt = t - 1 - jj
            lm = dh_s[pl.ds(tt, 1), :] + car
            lam_s[pl.ds(tt, 1), :] = lm
            return a_s[pl.ds(tt, 1), :] * lm

        lamcar[...] = lax.fori_loop(0, t, step, lamcar[...], unroll=8)
        db = lam_s[...]
        da = db * hp_s[7:7 + t, :]
        dmult = db * gate_i * conv
        dgate_i = db * mult * conv
        dconv = db * mult * gate_i
        dlog_a = da * a + jnp.where(first, 0.0, dmult * (-(a * a) / mult_raw))
        dgate_r = dlog_a * (-LRU_C * sp)
        dsp = jnp.sum(dlog_a * (-LRU_C * gate_r), axis=0, keepdims=True)
        dlam_ref[...] += dsp * (-_sigmoid(-lam_p))
        dga = dgate_r * gate_r * (1.0 - gate_r)
        dgx = dgate_i * gate_i * (1.0 - gate_i)
        dba_ref[...] += jnp.sum(dga, axis=0, keepdims=True)
        dbx_ref[...] += jnp.sum(dgx, axis=0, keepdims=True)
        dga16, dgx16 = dga.astype(BF16), dgx.astype(BF16)
        back = []
        for nb in range(RNN_BLOCKS):
            sl = slice(nb * LANE, (nb + 1) * LANE)
            dwa_ref[nb] += lax.dot_general(cbf[:, sl], dga16[:, sl], _DIMS["tn"], preferred_element_type=F32)
            dwx_ref[nb] += lax.dot_general(cbf[:, sl], dgx16[:, sl], _DIMS["tn"], preferred_element_type=F32)
            back.append(lax.dot_general(dga16[:, sl], wa_ref[nb], _DIMS["nt"], preferred_element_type=F32)
                        + lax.dot_general(dgx16[:, sl], wx_ref[nb], _DIMS["nt"], preferred_element_type=F32))
        dconv = dconv + jnp.concatenate(back, axis=1)
        dcb_ref[...] += jnp.sum(dconv, axis=0, keepdims=True)
        for k in range(CONV_W):
            dcw_ref[k:k + 1, :] += jnp.sum(dconv * xs[k], axis=0, keepdims=True)
        dxp_s[0:t, :] = dconv
        dxr = cw_ref[0:1, :] * dconv
        for k in range(1, CONV_W):
            dxr = dxr + cw_ref[k:k + 1, :] * dxp_s[k:k + t, :]
        dp_ref[:, 0:D_RNN] = dxr.astype(BF16)
        dp_ref[:, D_RNN:2 * D_RNN] = dg.astype(BF16)

    blk = lambda c: pl.BlockSpec((t, D_RNN), lambda i: (n - 1 - i, c))
    prev8 = pl.BlockSpec((8, D_RNN), lambda i: (jnp.maximum((n - 1 - i) * rb - 1, 0), 0))
    full = lambda shape: pl.BlockSpec(shape, lambda i: (0,) * len(shape))
    vec = full((1, D_RNN))
    mat = full((RNN_BLOCKS, LANE, LANE))
    return pl.pallas_call(
        body,
        name="rglru_bwd",
        grid=(n,),
        in_specs=[blk(0), blk(0), blk(1), blk(0), prev8, prev8,
                  full((CONV_W, D_RNN)), vec, mat, vec, mat, vec, vec],
        out_specs=[pl.BlockSpec((t, 2 * D_RNN), lambda i: (n - 1 - i, 0)),
                   full((CONV_W, D_RNN)), vec, mat, vec, mat, vec, vec],
        out_shape=[jax.ShapeDtypeStruct((S, 2 * D_RNN), BF16),
                   jax.ShapeDtypeStruct((CONV_W, D_RNN), F32), jax.ShapeDtypeStruct((1, D_RNN), F32),
                   jax.ShapeDtypeStruct((RNN_BLOCKS, LANE, LANE), F32), jax.ShapeDtypeStruct((1, D_RNN), F32),
                   jax.ShapeDtypeStruct((RNN_BLOCKS, LANE, LANE), F32), jax.ShapeDtypeStruct((1, D_RNN), F32),
                   jax.ShapeDtypeStruct((1, D_RNN), F32)],
        scratch_shapes=[pltpu.VMEM((t + 8, D_RNN), F32), pltpu.VMEM((t + 8, D_RNN), F32),
                        pltpu.VMEM((t + 8, D_RNN), F32), pltpu.VMEM((1, D_RNN), F32),
                        pltpu.VMEM((t, D_RNN), F32), pltpu.VMEM((t, D_RNN), F32), pltpu.VMEM((t, D_RNN), F32)],
        compiler_params=_params(("arbitrary",)),
    )(dy, p_a, p_a, hseq, p_a, hseq, conv_w, conv_b, wa, ba, wx, bx, lam)


QB = WINDOW
KB2 = 2 * WINDOW
N_QB = S // QB
N_PAIR = SWA_HEADS // 2


def _swa_keys(kvc_ref, kvp_ref):
    kk = jnp.concatenate([kvp_ref[:, 0:LANE], kvc_ref[:, 0:LANE]], axis=0)
    vv = jnp.concatenate([kvp_ref[:, LANE:2 * LANE], kvc_ref[:, LANE:2 * LANE]], axis=0)
    lo = lax.broadcasted_iota(jnp.int32, (1, LANE), 1) < SWA_HD
    kk_sw, vv_sw = pltpu.roll(kk, SWA_HD, 1), pltpu.roll(vv, SWA_HD, 1)
    kd = [jnp.where(lo, kk, kk_sw).astype(BF16), jnp.where(lo, kk_sw, kk).astype(BF16)]
    vd = [jnp.where(lo, vv, vv_sw).astype(BF16), jnp.where(lo, vv_sw, vv).astype(BF16)]
    return lo, kd, vd


def _swa_valid(n):
    qi = lax.broadcasted_iota(jnp.int32, (QB, KB2), 0)
    kj = lax.broadcasted_iota(jnp.int32, (QB, KB2), 1)
    dist = qi + WINDOW - kj
    return (dist >= 0) & (dist < WINDOW) & ((n > 0) | (kj >= WINDOW))


def _swa_probs(qh16, kd, bias, sink, valid):
    lg = lax.dot_general(qh16, kd, _DIMS["nt"], preferred_element_type=F32) * (SWA_HD ** -0.5) + bias
    lg = jnp.where(valid, lg, NEG_INF)
    m = jnp.maximum(jnp.max(lg, axis=-1, keepdims=True), sink)
    p = jnp.exp(lg - m)
    es = jnp.exp(sink - m)
    den = jnp.sum(p, axis=-1, keepdims=True) + es
    return p / den, es / den


def _swa_specs():
    q = pl.BlockSpec((QB, D_RNN), lambda n: (n, 0))
    g = pl.BlockSpec((QB, D_RNN), lambda n: (n, 1))
    kvc = pl.BlockSpec((QB, 2 * LANE), lambda n: (n, 8))
    kvp = pl.BlockSpec((QB, 2 * LANE), lambda n: (jnp.maximum(n - 1, 0), 8))
    bias = pl.BlockSpec((SWA_HEADS, QB, KB2), lambda n: (0, 0, 0))
    sinks = pl.BlockSpec(memory_space=pltpu.SMEM)
    return q, g, kvc, kvp, bias, sinks


def _swa_fwd(p_b, bias_t, sinks):
    def body(q_ref, g_ref, kvc_ref, kvp_ref, bias_ref, sink_ref, y_ref, o_ref):
        n = pl.program_id(0)
        lo, kd, vd = _swa_keys(kvc_ref, kvp_ref)
        valid = _swa_valid(n)
        for hp in range(N_PAIR):
            sl = slice(hp * LANE, (hp + 1) * LANE)
            kvh = hp // (N_PAIR // 2)
            q = q_ref[:, sl]
            outs = []
            for j in range(2):
                mh = lo if j == 0 else jnp.logical_not(lo)
                qh16 = jnp.where(mh, q, 0.0).astype(BF16)
                probs, _ = _swa_probs(qh16, kd[kvh], bias_ref[2 * hp + j], sink_ref[2 * hp + j], valid)
                outs.append(jnp.dot(probs.astype(BF16), vd[kvh], preferred_element_type=F32))
            o = jnp.where(lo, outs[0], outs[1])
            o_ref[:, sl] = o
            g = g_ref[:, sl]
            y_ref[:, sl] = (o * (g * _sigmoid(g))).astype(BF16)

    q, g, kvc, kvp, bias, sinks_spec = _swa_specs()
    out = pl.BlockSpec((QB, D_RNN), lambda n: (n, 0))
    return pl.pallas_call(
        body,
        name="swa_fwd",
        grid=(N_QB,),
        in_specs=[q, g, kvc, kvp, bias, sinks_spec],
        out_specs=[out, out],
        out_shape=[jax.ShapeDtypeStruct((S, D_RNN), BF16), jax.ShapeDtypeStruct((S, D_RNN), F32)],
        compiler_params=_params(("parallel",)),
    )(p_b, p_b, p_b, p_b, bias_t, sinks)


def _swa_bwd(dy, p_b, o_swa, bias_t, sinks):
    def body(dy_ref, q_ref, g_ref, kvc_ref, kvp_ref, o_ref, bias_ref, sink_ref,
             dp_ref, dk_ref, dv_ref, dbias_ref, dsink_ref):
        n = pl.program_id(0)

        @pl.when(n == 0)
        def _():
            for ref in (dk_ref, dv_ref, dbias_ref, dsink_ref):
                ref[...] = jnp.zeros_like(ref)

        lo, kd, vd = _swa_keys(kvc_ref, kvp_ref)
        hi = jnp.logical_not(lo)
        valid = _swa_valid(n)
        dk_blk = jnp.zeros((KB2, LANE), F32)
        dv_blk = jnp.zeros((KB2, LANE), F32)
        for kvh in range(2):
            dk_pair = jnp.zeros((KB2, LANE), F32)
            dv_pair = jnp.zeros((KB2, LANE), F32)
            for hp in range(kvh * (N_PAIR // 2), (kvh + 1) * (N_PAIR // 2)):
                sl = slice(hp * LANE, (hp + 1) * LANE)
                q = q_ref[:, sl]
                g = g_ref[:, sl]
                o = o_ref[:, sl]
                dyv = dy_ref[:, sl]
                sg = _sigmoid(g)
                do = dyv * (g * sg)
                dp_ref[:, D_RNN + hp * LANE:D_RNN + (hp + 1) * LANE] = (
                    dyv * o * (sg * (1.0 + g * (1.0 - sg)))).astype(BF16)
                dqs = []
                for j in range(2):
                    h = 2 * hp + j
                    mh = lo if j == 0 else hi
                    qh16 = jnp.where(mh, q, 0.0).astype(BF16)
                    sink = sink_ref[h]
                    probs, psink = _swa_probs(qh16, kd[kvh], bias_ref[h], sink, valid)
                    doh = jnp.where(mh, do, 0.0)
                    doh16 = doh.astype(BF16)
                    delta = jnp.sum(doh * o, axis=-1, keepdims=True)
                    dpr = lax.dot_general(doh16, vd[kvh], _DIMS["nt"], preferred_element_type=F32)
                    ds = probs * (dpr - delta)
                    dbias_ref[h] += ds
                    dsink_ref[h:h + 1, :] += jnp.zeros((1, LANE), F32) - jnp.sum(psink * delta)
                    ds16 = (ds * (SWA_HD ** -0.5)).astype(BF16)
                    dqs.append(jnp.dot(ds16, kd[kvh], preferred_element_type=F32))
                    dk_pair = dk_pair + lax.dot_general(ds16, qh16, _DIMS["tn"], preferred_element_type=F32)
                    dv_pair = dv_pair + lax.dot_general(probs.astype(BF16), doh16, _DIMS["tn"],
                                                        preferred_element_type=F32)
                dp_ref[:, sl] = jnp.where(lo, dqs[0], dqs[1]).astype(BF16)
            keep = lo if kvh == 0 else hi
            dk_blk = dk_blk + jnp.where(keep, dk_pair + pltpu.roll(dk_pair, SWA_HD, 1), 0.0)
            dv_blk = dv_blk + jnp.where(keep, dv_pair + pltpu.roll(dv_pair, SWA_HD, 1), 0.0)

        cur = pl.ds(pl.multiple_of(n * QB, QB), QB)
        dk_ref[cur, :] += dk_blk[QB:KB2]
        dv_ref[cur, :] += dv_blk[QB:KB2]

        @pl.when(n > 0)
        def _():
            prev = pl.ds(pl.multiple_of((n - 1) * QB, QB), QB)
            dk_ref[prev, :] += dk_blk[0:QB]
            dv_ref[prev, :] += dv_blk[0:QB]

    q, g, kvc, kvp, bias, sinks_spec = _swa_specs()
    row = pl.BlockSpec((QB, D_RNN), lambda n: (n, 0))
    acc = pl.BlockSpec((S, LANE), lambda n: (0, 0))
    return pl.pallas_call(
        body,
        name="swa_bwd",
        grid=(N_QB,),
        in_specs=[row, q, g, kvc, kvp, row, bias, sinks_spec],
        out_specs=[pl.BlockSpec((QB, 2 * D_RNN), lambda n: (n, 0)), acc, acc, bias,
                   pl.BlockSpec((SWA_HEADS, LANE), lambda n: (0, 0))],
        out_shape=[jax.ShapeDtypeStruct((S, GROUP_TILES["B"] * LANE), BF16),
                   jax.ShapeDtypeStruct((S, LANE), F32), jax.ShapeDtypeStruct((S, LANE), F32),
                   jax.ShapeDtypeStruct((SWA_HEADS, QB, KB2), F32),
                   jax.ShapeDtypeStruct((SWA_HEADS, LANE), F32)],
        compiler_params=_params(("arbitrary",)),
    )(dy, p_b, p_b, p_b, p_b, o_swa, bias_t, sinks)


def _swa_pack(dp_b, dk, dv, ts=512):
    def body(_, dk_ref, dv_ref, o_ref):
        o_ref[:, 0:LANE] = dk_ref[...].astype(BF16)
        o_ref[:, LANE:2 * LANE] = dv_ref[...].astype(BF16)

    tile = pl.BlockSpec((ts, LANE), lambda i: (i, 0))
    return pl.pallas_call(
        body,
        name="swa_pack",
        grid=(S // ts,),
        in_specs=[pl.BlockSpec(memory_space=pl.ANY), tile, tile],
        out_specs=pl.BlockSpec((ts, 2 * LANE), lambda i: (i, 8)),
        out_shape=jax.ShapeDtypeStruct(dp_b.shape, dp_b.dtype),
        input_output_aliases={0: 0},
        compiler_params=_params(("parallel",)),
    )(dp_b, dk, dv)


def _split3(v):
    a = v.astype(BF16)
    r = v - a.astype(F32)
    b = r.astype(BF16)
    c = (r - b.astype(F32)).astype(BF16)
    return a, b, c


def _relbias_grad(dbias_flat, onehot_t):
    def body(d_ref, e_ref, o_ref):
        e = e_ref[...]
        acc = jnp.zeros((SWA_HEADS, REL_BUCKETS), F32)
        for term in _split3(d_ref[...]):
            acc = acc + lax.dot_general(term, e, _DIMS["nt"], preferred_element_type=F32)
        o_ref[...] = acc

    return pl.pallas_call(
        body,
        name="relbias_grad",
        out_shape=jax.ShapeDtypeStruct((SWA_HEADS, REL_BUCKETS), F32),
        compiler_params=_params(),
    )(dbias_flat, onehot_t)


TS_MEM = 512


def _mem_probs(q16, mk):
    lg = lax.dot_general(q16, mk, _DIMS["nt"], preferred_element_type=F32) * (MEM_HD ** -0.5)
    p = jnp.exp(lg - jnp.max(lg, axis=-1, keepdims=True))
    return p / jnp.sum(p, axis=-1, keepdims=True)


def _mem_fwd(p_c, mkv):
    def body(q_ref, g_ref, mkv_ref, y_ref, o_ref):
        for hm in range(MEM_HEADS):
            sl = slice(hm * MEM_HD, (hm + 1) * MEM_HD)
            probs = _mem_probs(q_ref[:, sl].astype(BF16), mkv_ref[:, sl])
            o = jnp.dot(probs.astype(BF16), mkv_ref[:, D_RNN + hm * MEM_HD:D_RNN + (hm + 1) * MEM_HD],
                        preferred_element_type=F32)
            o_ref[:, sl] = o
            g = g_ref[:, sl]
            y_ref[:, sl] = (o * (g * _sigmoid(g))).astype(BF16)

    blk = lambda c: pl.BlockSpec((TS_MEM, D_RNN), lambda i: (i, c))
    return pl.pallas_call(
        body,
        name="mem_fwd",
        grid=(S // TS_MEM,),
        in_specs=[blk(0), blk(1), pl.BlockSpec((MEM, 2 * D_RNN), lambda i: (0, 0))],
        out_specs=[blk(0), blk(0)],
        out_shape=[jax.ShapeDtypeStruct((S, D_RNN), BF16), jax.ShapeDtypeStruct((S, D_RNN), F32)],
        compiler_params=_params(("parallel",)),
    )(p_c, p_c, mkv)


def _mem_bwd(dy, p_c, o_mem, mkv):
    def body(dy_ref, q_ref, g_ref, o_ref, mkv_ref, dp_ref, dmkv_ref):
        @pl.when(pl.program_id(0) == 0)
        def _():
            dmkv_ref[...] = jnp.zeros_like(dmkv_ref)

        for hm in range(MEM_HEADS):
            sl = slice(hm * MEM_HD, (hm + 1) * MEM_HD)
            sv = slice(D_RNN + hm * MEM_HD, D_RNN + (hm + 1) * MEM_HD)
            q16 = q_ref[:, sl].astype(BF16)
            mk, mv = mkv_ref[:, sl], mkv_ref[:, sv]
            probs = _mem_probs(q16, mk)
            g, o, dyv = g_ref[:, sl], o_ref[:, sl], dy_ref[:, sl]
            sg = _sigmoid(g)
            do = dyv * (g * sg)
            dp_ref[:, sv] = (dyv * o * (sg * (1.0 + g * (1.0 - sg)))).astype(BF16)
            do16 = do.astype(BF16)
            delta = jnp.sum(do * o, axis=-1, keepdims=True)
            dpr = lax.dot_general(do16, mv, _DIMS["nt"], preferred_element_type=F32)
            ds16 = (probs * (dpr - delta) * (MEM_HD ** -0.5)).astype(BF16)
            dp_ref[:, sl] = jnp.dot(ds16, mk, preferred_element_type=F32).astype(BF16)
            dmkv_ref[:, sl] += lax.dot_general(ds16, q16, _DIMS["tn"], preferred_element_type=F32)
            dmkv_ref[:, sv] += lax.dot_general(probs.astype(BF16), do16, _DIMS["tn"], preferred_element_type=F32)

    blk = lambda c: pl.BlockSpec((TS_MEM, D_RNN), lambda i: (i, c))
    kv = pl.BlockSpec((MEM, 2 * D_RNN), lambda i: (0, 0))
    return pl.pallas_call(
        body,
        name="mem_bwd",
        grid=(S // TS_MEM,),
        in_specs=[blk(0), blk(0), blk(1), blk(0), kv],
        out_specs=[pl.BlockSpec((TS_MEM, 2 * D_RNN), lambda i: (i, 0)), kv],
        out_shape=[jax.ShapeDtypeStruct((S, 2 * D_RNN), BF16), jax.ShapeDtypeStruct((MEM, 2 * D_RNN), F32)],
        compiler_params=_params(("arbitrary",)),
    )(dy, p_c, p_c, o_mem, mkv)


TS_MRG = 512
TD_MRG = 512
N_DBLK = D // TD_MRG


def _merge_fwd(z, p_d):
    def body(z0, z1, z2, g0, g1, g2, o_ref):
        o_ref[...] = (_sigmoid(g0[...]) * z0[...] + _sigmoid(g1[...]) * z1[...]
                      + _sigmoid(g2[...]) * z2[...]).astype(BF16)

    blk = pl.BlockSpec((TS_MRG, TD_MRG), lambda i, d: (i, d))
    gate = lambda b: pl.BlockSpec((TS_MRG, TD_MRG), lambda i, d: (i, b * N_DBLK + d))
    return pl.pallas_call(
        body,
        name="merge_fwd",
        grid=(S // TS_MRG, N_DBLK),
        in_specs=[blk, blk, blk, gate(0), gate(1), gate(2)],
        out_specs=blk,
        out_shape=jax.ShapeDtypeStruct((S, D), BF16),
        compiler_params=_params(("parallel", "parallel")),
    )(z[0], z[1], z[2], p_d, p_d, p_d)


def _merge_bwd(dmerged, z_b, p_d, b, dp_d):
    def body(dm_ref, z_ref, g_ref, *refs):
        dz_ref, dg_ref = refs[-2], refs[-1]
        sg = _sigmoid(g_ref[...])
        dm = dm_ref[...]
        dz_ref[...] = (dm * sg).astype(BF16)
        dg_ref[...] = (dm * z_ref[...] * sg * (1.0 - sg)).astype(BF16)

    blk = pl.BlockSpec((TS_MRG, TD_MRG), lambda i, d: (i, d))
    gate = pl.BlockSpec((TS_MRG, TD_MRG), lambda i, d: (i, b * N_DBLK + d))
    in_specs = [blk, blk, gate]
    args = [dmerged, z_b, p_d]
    aliases = {}
    if dp_d is not None:
        in_specs.append(pl.BlockSpec(memory_space=pl.ANY))
        args.append(dp_d)
        aliases = {3: 1}
    return pl.pallas_call(
        body,
        name=f"merge_bwd{b}",
        grid=(S // TS_MRG, N_DBLK),
        in_specs=in_specs,
        out_specs=[blk, gate],
        out_shape=[jax.ShapeDtypeStruct((S, D), BF16),
                   jax.ShapeDtypeStruct((S, GROUP_TILES["D"] * LANE), BF16)],
        input_output_aliases=aliases,
        compiler_params=_params(("parallel", "parallel")),
    )(*args)


def _bucket_table():
    import numpy as np
    qi = np.arange(QB)[:, None]
    kj = np.arange(KB2)[None, :]
    n = np.maximum(qi + WINDOW - kj, 0)
    max_exact = REL_BUCKETS // 2
    ratio = np.log(np.maximum(n, 1).astype(np.float32) / max_exact) / np.float32(math.log(REL_MAX_DIST / max_exact))
    large = np.minimum(max_exact + (ratio * (REL_BUCKETS - max_exact)).astype(np.int32), REL_BUCKETS - 1)
    bucket = np.where(n < max_exact, n, large).reshape(1, QB * KB2)
    return (bucket == np.arange(REL_BUCKETS)[:, None]).astype(np.float32)


def _bias_expand(rel_bias_t, onehot_t):
    def body(r_ref, e_ref, o_ref):
        e = e_ref[...]
        acc = jnp.zeros((SWA_HEADS, QB * KB2), F32)
        for term in _split3(r_ref[...]):
            acc = acc + jnp.dot(term, e, preferred_element_type=F32)
        o_ref[...] = acc

    return pl.pallas_call(
        body,
        name="bias_expand",
        out_shape=jax.ShapeDtypeStruct((SWA_HEADS, QB * KB2), F32),
        compiler_params=_params(),
    )(rel_bias_t, onehot_t)


PROJ_TN = {"A": 1024, "B": 1152, "C": 1024, "D": 1536}


def _local_step(x, mem, tgt, sp, fetch, emit):
    onehot_t = jnp.asarray(_bucket_table(), BF16)
    bias_t = _bias_expand(sp["rel_bias"].T, onehot_t).reshape(SWA_HEADS, QB, KB2)
    sinks = sp["swa_sinks"].reshape(SWA_HEADS)
    wa16, wx16 = sp["w_rg_a"].astype(BF16), sp["w_rg_x"].astype(BF16)
    rnn = (sp["conv_w"], sp["conv_b"], wa16, sp["b_rg_a"], wx16, sp["b_rg_x"], sp["lru_lambda"])

    h = _rms_fwd(x, sp["pre_norm_g"], "rms_pre")
    memn = _rms_fwd(mem, sp["mem_norm_g"], "rms_mem")
    w_grp, p, last = {}, {}, h
    for g in GROUPS:
        (w_grp[g],) = fetch((g,), last)
        p[g] = last = _mm(h, w_grp[g], "nt", F32, 1024, PROJ_TN[g], D, f"proj_{g}")
    y_rg, hseq = _rglru_fwd(p["A"], *rnn)
    y_swa, o_swa = _swa_fwd(p["B"], bias_t, sinks)
    (wmk,) = fetch(("mk",), y_swa)
    mkv = _mm(memn, wmk, "nn", BF16, MEM, 1024, D, "mkv")
    y_mem, o_mem = _mem_fwd(p["C"], mkv)
    ys = (y_rg, y_swa, y_mem)
    wbr = fetch(("br0", "br1", "br2"), y_mem)
    z = [_mm(ys[b], wbr[b], "nn", F32, 1024, 1024, D_RNN, f"branch_out{b}") for b in range(3)]
    merged = _merge_fwd(z, p["D"])
    (wout,) = fetch(("out",), merged)
    out = _mm(merged, wout, "nn", F32, 1024, 1024, D, "out_proj")
    sq, dy, dout, d_post = _post_loss(out, x, tgt, sp["post_norm_g"])

    emit({"out": _mm(merged, dout, "tn", BF16, 1024, 1024, S, "d_wout")})
    dmerged = _mm(dout, wout, "nt", F32, 1024, 1024, D, "d_merged")
    dz, dp_d = [], None
    for b in range(3):
        dz_b, dp_d = _merge_bwd(dmerged, z[b], p["D"], b, dp_d)
        dz.append(dz_b)
    emit({f"br{b}": _mm(ys[b], dz[b], "tn", BF16, 1024, 1024, S, f"d_wbr{b}") for b in range(3)})
    emit({"D": _mm(dp_d, h, "tn", BF16, PROJ_TN["D"], 1024, S, "d_win_D")})
    dys = [_mm(dz[b], wbr[b], "nt", F32, 1024, 1024, D, f"d_branch{b}") for b in range(3)]
    dp_c, dmkv = _mem_bwd(dys[2], p["C"], o_mem, mkv)
    dmkv16 = dmkv.astype(BF16)
    emit({"mk": _mm(memn, dmkv16, "tn", BF16, 1024, 1024, MEM, "d_wmk")})
    dmemn = _mm(dmkv16, wmk, "nt", F32, MEM, 1024, D, "d_memn")
    d_memg = _memnorm_bwd(dmemn, mem)
    dp_a, d_cw, d_cb, d_wa, d_ba, d_wx, d_bx, d_lam = _rglru_bwd(dys[0], p["A"], hseq, *rnn)
    dp_b, dk, dv, d_bias, d_sink = _swa_bwd(dys[1], p["B"], o_swa, bias_t, sinks)
    dp_b = _swa_pack(dp_b, dk, dv)
    d_rel = _relbias_grad(d_bias.reshape(SWA_HEADS, QB * KB2), onehot_t).T
    dp = {"A": dp_a, "B": dp_b, "C": dp_c, "D": dp_d}
    emit({g: _mm(dp[g], h, "tn", BF16, PROJ_TN[g], 1024, S, f"d_win_{g}") for g in ("A", "B", "C")})
    dh = None
    for g in GROUPS:
        dh = _mm(dp[g], w_grp[g], "nn", F32, 1024, 1024, 2304 if g == "B" else 2048, f"d_h_{g}", acc=dh)
    grad_x, d_pre = _pre_bwd(dh, x, dy, sp["pre_norm_g"])

    d_small = {
        "pre_norm_g": d_pre, "post_norm_g": d_post, "mem_norm_g": d_memg, "conv_w": d_cw, "conv_b": d_cb,
        "w_rg_a": d_wa, "b_rg_a": d_ba, "w_rg_x": d_wx, "b_rg_x": d_bx, "lru_lambda": d_lam,
        "swa_sinks": d_sink[:, 0].reshape(1, SWA_HEADS), "rel_bias": d_rel,
    }
    return sq, grad_x, d_small


ANY = pl.BlockSpec(memory_space=pl.ANY)
SHARD_ROWS = D // N_CHIPS
GATHERED = {"A": (2048, D), "B": (2304, D), "C": (2048, D), "D": (6144, D), "mk": (D, D),
            "br0": (D_RNN, D), "br1": (D_RNN, D), "br2": (D_RNN, D), "out": (D, D)}
SHARD_SHAPES = {"win": (SHARD, D), "mk": (SHARD_ROWS, D), "br0": (D_RNN, SHARD_ROWS), "br1": (D_RNN, SHARD_ROWS),
                "br2": (D_RNN, SHARD_ROWS), "out": (SHARD_ROWS, D)}
SHARDS = tuple(SHARD_SHAPES)
HALF_AXIS = {"win": 1, "mk": 1, "br0": 0, "br1": 0, "br2": 0, "out": 1,
             "A": 1, "B": 1, "C": 1, "D": 1}


def _halved(shape, axis):
    return (shape[0] // 2, shape[1]) if axis == 0 else (shape[0], shape[1] // 2)


class Piece(NamedTuple):
    src: str
    dst: str
    rows: int
    sr0: int
    sc0: int
    dr0: int
    dc0: int
    ncols: int


def _pieces_of(jj):
    out = [Piece("win", g, n, r, 0, gr, 0, D) for r, n, g, gr in _shard_runs(jj)]
    out.append(Piece("mk", "mk", SHARD_ROWS, 0, 0, SHARD_ROWS * jj, 0, D))
    out += [Piece(f"br{b}", f"br{b}", D_RNN, 0, 0, 0, SHARD_ROWS * jj, SHARD_ROWS) for b in range(3)]
    out.append(Piece("out", "out", SHARD_ROWS, 0, 0, SHARD_ROWS * jj, 0, D))
    return out


def _half_rect(ref, p, side, which):
    r0, c0 = (p.sr0, p.sc0) if side == "src" else (p.dr0, p.dc0)
    if HALF_AXIS[p.src] == 1:
        return _rect(ref, r0, p.rows, c0 + which * (p.ncols // 2), p.ncols // 2)
    return _rect(ref, r0 + which * (p.rows // 2), p.rows // 2, c0, p.ncols)


def _rect_in_half(ref, p, side):
    r0, c0 = (p.sr0, p.sc0) if side == "src" else (p.dr0, p.dc0)
    if HALF_AXIS[p.src] == 1:
        return _rect(ref, r0, p.rows, 0, p.ncols // 2)
    return _rect(ref, 0, p.rows // 2, c0, p.ncols)


MAX_PIECES = max(len(_pieces_of(jj)) for jj in range(N_CHIPS))


def _rect(ref, r0, rows, c0, ncols):
    return ref.at[pl.ds(r0, rows), pl.ds(c0, ncols)]


def _position():
    x, y, c = lax.axis_index("x"), lax.axis_index("y"), lax.axis_index("c")
    return x, y, c, 2 * x + y


HBM = pl.BlockSpec(memory_space=pltpu.HBM)
SEM = pl.BlockSpec(memory_space=pltpu.SEMAPHORE)
EFFECT = pltpu.SideEffectType.DATAFLOW_SIDE_EFFECTING
N_SEM = MAX_PIECES * N_CHIPS


def _in_hbm(a):
    return pltpu.with_memory_space_constraint(a, pltpu.HBM)


def _stage_pieces(jj, stage):
    return [(i, p) for i, p in enumerate(_pieces_of(jj)) if p.dst in stage]


def _gather_place_own(shards):
    names = tuple(GATHERED)

    def body(*refs):
        src = dict(zip(SHARDS, refs[:len(SHARDS)]))
        dst = dict(zip(names, refs[len(SHARDS):len(SHARDS) + len(names)]))
        loc_sems = refs[-1]
        _, _, _, j = _position()
        for jj in range(N_CHIPS):
            @pl.when(j == jj)
            def _():
                copies = [pltpu.make_async_copy(_rect(src[p.src], p.sr0, p.rows, p.sc0, p.ncols),
                                                _rect(dst[p.dst], p.dr0, p.rows, p.dc0, p.ncols), loc_sems.at[i])
                          for i, p in enumerate(_pieces_of(jj))]
                for cp in copies:
                    cp.start()
                for cp in copies:
                    cp.wait()

    outs = pl.pallas_call(
        body,
        name="gather_place_own",
        in_specs=[ANY] * len(SHARDS),
        out_specs=[ANY] * len(names),
        out_shape=[jax.ShapeDtypeStruct(GATHERED[n], BF16) for n in names],
        scratch_shapes=[pltpu.SemaphoreType.DMA((MAX_PIECES,))],
        compiler_params=pltpu.CompilerParams(has_side_effects=True),
    )(*[shards[n] for n in SHARDS])
    return dict(zip(names, outs))


def _gather_copy(src, dst, send_sems, recv_sems, c, jj, i, p, kk):
    return pltpu.make_async_remote_copy(
        src_ref=_half_rect(src[p.src], p, "src", c), dst_ref=_half_rect(dst[p.dst], p, "dst", c),
        send_sem=send_sems.at[i * N_CHIPS + kk], recv_sem=recv_sems.at[jj * MAX_PIECES + i],
        device_id=(kk // 2, kk % 2, c), device_id_type=MESH)


def _gather_start(shards, gathered):
    names = tuple(GATHERED)
    ns, ng = len(SHARDS), len(names)

    def body(*refs):
        src = dict(zip(SHARDS, refs[:ns]))
        dst = dict(zip(names, refs[ns:ns + ng]))
        send_sems, recv_sems = refs[ns + ng], refs[ns + ng + 1]
        token = refs[-1]
        _, _, c, j = _position()
        for jj in range(N_CHIPS):
            @pl.when(j == jj)
            def _():
                for i, p in enumerate(_pieces_of(jj)):
                    for kk in range(N_CHIPS):
                        if kk != jj:
                            _gather_copy(src, dst, send_sems, recv_sems, c, jj, i, p, kk).start()
        token[...] = jnp.zeros_like(token)

    outs = pl.pallas_call(
        body,
        name="gather_start",
        in_specs=[HBM] * (ns + ng),
        out_specs=[SEM, SEM] + [HBM] * (ns + ng) + [pl.BlockSpec(memory_space=pltpu.VMEM)],
        out_shape=[pltpu.SemaphoreType.DMA((N_SEM,)), pltpu.SemaphoreType.DMA((N_SEM,))]
        + [pltpu.HBM(SHARD_SHAPES[n], BF16) for n in SHARDS] + [pltpu.HBM(GATHERED[n], BF16) for n in names]
        + [jax.ShapeDtypeStruct((8, LANE), F32)],
        input_output_aliases={k: 2 + k for k in range(ns + ng)},
        compiler_params=pltpu.CompilerParams(has_side_effects=EFFECT),
    )(*[_in_hbm(shards[n]) for n in SHARDS], *[_in_hbm(gathered[n]) for n in names])
    return outs[0], outs[1], dict(zip(SHARDS, outs[2:2 + ns])), dict(zip(names, outs[2 + ns:2 + ns + ng])), outs[-1]


def _gather_wait(stage, send_sems, recv_sems, shards, arrays, after):
    ns, na = len(SHARDS), len(stage)

    def body(*refs):
        src = dict(zip(SHARDS, refs[:ns]))
        dst = dict(zip(stage, refs[ns:ns + na]))
        sems_s, sems_r = refs[ns + na], refs[ns + na + 1]
        _, _, c, j = _position()
        for jj in range(N_CHIPS):
            @pl.when(j != jj)
            def _():
                for i, p in _stage_pieces(jj, stage):
                    _gather_copy(src, dst, sems_s, sems_r, c, jj, i, p, jj).wait_recv()

            @pl.when(j == jj)
            def _():
                for i, p in _stage_pieces(jj, stage):
                    for kk in range(N_CHIPS):
                        if kk != jj:
                            _gather_copy(src, dst, sems_s, sems_r, c, jj, i, p, kk).wait_send()

    outs = pl.pallas_call(
        body,
        name=f"gather_wait_{stage[0]}",
        in_specs=[HBM] * (ns + na) + [SEM, SEM, ANY],
        out_specs=[HBM] * (ns + na),
        out_shape=[pltpu.HBM(SHARD_SHAPES[n], BF16) for n in SHARDS] + [pltpu.HBM(GATHERED[n], BF16) for n in stage],
        input_output_aliases={k: k for k in range(ns + na)},
        compiler_params=pltpu.CompilerParams(has_side_effects=EFFECT),
    )(*[shards[n] for n in SHARDS], *[arrays[n] for n in stage], send_sems, recv_sems, after)
    return dict(zip(SHARDS, outs[:ns])), dict(zip(stage, outs[ns:]))


def _gather_swap(arrays):
    stage = tuple(arrays)
    na = len(stage)

    def body(*refs):
        dst = dict(zip(stage, refs[na:2 * na]))
        send_sems, recv_sems = refs[2 * na:]
        x, y, c, j = _position()

        def fwd(jj, i, p, which):
            rect = _half_rect(dst[p.dst], p, "dst", which)
            return pltpu.make_async_remote_copy(
                src_ref=rect, dst_ref=rect, send_sem=send_sems.at[jj * MAX_PIECES + i],
                recv_sem=recv_sems.at[jj * MAX_PIECES + i], device_id=(x, y, 1 - c), device_id_type=MESH)

        for jj in range(N_CHIPS):
            @pl.when(j != jj)
            def _():
                for i, p in _stage_pieces(jj, stage):
                    fwd(jj, i, p, c).start()
        for jj in range(N_CHIPS):
            @pl.when(j != jj)
            def _():
                for i, p in _stage_pieces(jj, stage):
                    fwd(jj, i, p, 1 - c).wait_recv()
        for jj in range(N_CHIPS):
            @pl.when(j != jj)
            def _():
                for i, p in _stage_pieces(jj, stage):
                    fwd(jj, i, p, c).wait_send()

    outs = pl.pallas_call(
        body,
        name=f"gather_swap_{stage[0]}",
        in_specs=[ANY] * na,
        out_specs=[ANY] * na,
        out_shape=[jax.ShapeDtypeStruct(GATHERED[n], BF16) for n in stage],
        input_output_aliases={k: k for k in range(na)},
        scratch_shapes=[pltpu.SemaphoreType.DMA((N_SEM,)), pltpu.SemaphoreType.DMA((N_SEM,))],
        compiler_params=pltpu.CompilerParams(has_side_effects=True),
    )(*[arrays[n] for n in stage])
    return dict(zip(stage, outs))


def _own_half(ref, shape, axis, which):
    if axis == 1:
        return ref.at[:, pl.ds(which * (shape[1] // 2), shape[1] // 2)]
    return ref.at[pl.ds(which * (shape[0] // 2), shape[0] // 2), :]


def _swap_halves(grads):
    names = tuple(grads)
    n_tr = len(names)

    def body(*refs):
        src = dict(zip(names, refs[:len(names)]))
        dst = dict(zip(names, refs[len(names):2 * len(names)]))
        send_sems, recv_sems = refs[2 * len(names):]
        x, y, c, _ = _position()
        copies = [pltpu.make_async_remote_copy(
            src_ref=_own_half(src[n], GATHERED[n], HALF_AXIS[n], 1 - c), dst_ref=dst[n],
            send_sem=send_sems.at[k], recv_sem=recv_sems.at[k],
            device_id=(x, y, 1 - c), device_id_type=MESH) for k, n in enumerate(names)]
        for cp in copies:
            cp.start()
        for cp in copies:
            cp.wait_recv()
        for cp in copies:
            cp.wait_send()

    outs = pl.pallas_call(
        body,
        name=f"swap_halves_{names[0]}",
        in_specs=[ANY] * len(names),
        out_specs=[ANY] * len(names),
        out_shape=[jax.ShapeDtypeStruct(_halved(GATHERED[n], HALF_AXIS[n]), BF16) for n in names],
        scratch_shapes=[pltpu.SemaphoreType.DMA((n_tr,)), pltpu.SemaphoreType.DMA((n_tr,))],
        compiler_params=pltpu.CompilerParams(has_side_effects=True),
    )(*[grads[n] for n in names])
    return dict(zip(names, outs))


ADD_ROWS = 256


def _add_half(full, recv, c_arr, name):
    rows, cols = recv.shape
    if HALF_AXIS[name] == 1:
        index = lambda i, c_ref: (i, c_ref[0])
    else:
        nb = rows // ADD_ROWS
        index = lambda i, c_ref: (nb * c_ref[0] + i, 0)

    def body(c_ref, a_ref, b_ref, o_ref):
        o_ref[...] = (a_ref[...].astype(F32) + b_ref[...].astype(F32)).astype(BF16)

    return pl.pallas_call(
        body,
        name=f"add_half_{name}",
        grid_spec=pltpu.PrefetchScalarGridSpec(
            num_scalar_prefetch=1,
            grid=(rows // ADD_ROWS,),
            in_specs=[pl.BlockSpec((ADD_ROWS, cols), index), pl.BlockSpec((ADD_ROWS, cols), lambda i, c_ref: (i, 0))],
            out_specs=pl.BlockSpec((ADD_ROWS, cols), lambda i, c_ref: (i, 0)),
        ),
        out_shape=jax.ShapeDtypeStruct((rows, cols), BF16),
        compiler_params=_params(("parallel",)),
    )(c_arr, full, recv)


SLOT_SHAPES = {n: _halved(SHARD_SHAPES[n], HALF_AXIS[n]) for n in SHARDS}


def _slot_shape(n):
    return (N_CHIPS,) + SLOT_SHAPES[n]


def _stage_shards(stage):
    pieces = [p for jj in range(N_CHIPS) for p in _pieces_of(jj)]
    return tuple(s for s in SHARDS if any(p.src == s and p.dst in stage for p in pieces))


def _scatter_place_own(halves, slots):
    stage = tuple(halves)
    touched = _stage_shards(stage)
    old = tuple(s for s in touched if s in slots)

    def body(*refs):
        src = dict(zip(stage, refs[:len(stage)]))
        dst = dict(zip(touched, refs[len(stage) + len(old):len(stage) + len(old) + len(touched)]))
        loc_sems = refs[-1]
        _, _, _, j = _position()
        for jj in range(N_CHIPS):
            @pl.when(j == jj)
            def _():
                copies = [pltpu.make_async_copy(_rect_in_half(src[p.dst], p, "dst"),
                                                _rect_in_half(dst[p.src].at[jj], p, "src"), loc_sems.at[i])
                          for i, p in _stage_pieces(jj, stage)]
                for cp in copies:
                    cp.start()
                for cp in copies:
                    cp.wait()

    outs = pl.pallas_call(
        body,
        name=f"scatter_place_own_{stage[0]}",
        in_specs=[ANY] * (len(stage) + len(old)),
        out_specs=[ANY] * len(touched),
        out_shape=[jax.ShapeDtypeStruct(_slot_shape(s), BF16) for s in touched],
        input_output_aliases={len(stage) + k: touched.index(s) for k, s in enumerate(old)},
        scratch_shapes=[pltpu.SemaphoreType.DMA((MAX_PIECES,))],
        compiler_params=pltpu.CompilerParams(has_side_effects=True),
    )(*[halves[n] for n in stage], *[slots[s] for s in old])
    return dict(zip(touched, outs))


def _scatter_copy(src, dst, send_sems, recv_sems, c, jj, kk, i, p):
    return pltpu.make_async_remote_copy(
        src_ref=_rect_in_half(src[p.dst], p, "dst"), dst_ref=_rect_in_half(dst[p.src].at[jj], p, "src"),
        send_sem=send_sems.at[kk * MAX_PIECES + i], recv_sem=recv_sems.at[jj * MAX_PIECES + i],
        device_id=(kk // 2, kk % 2, c), device_id_type=MESH)


def _scatter_start(halves, slots):
    stage, touched = tuple(halves), tuple(slots)
    nh, nt = len(stage), len(touched)

    def body(*refs):
        src = dict(zip(stage, refs[:nh]))
        dst = dict(zip(touched, refs[nh:nh + nt]))
        send_sems, recv_sems = refs[nh + nt], refs[nh + nt + 1]
        token = refs[-1]
        _, _, c, j = _position()
        for jj in range(N_CHIPS):
            @pl.when(j == jj)
            def _():
                for kk in range(N_CHIPS):
                    if kk != jj:
                        for i, p in _stage_pieces(kk, stage):
                            _scatter_copy(src, dst, send_sems, recv_sems, c, jj, kk, i, p).start()
        token[...] = jnp.zeros_like(token)

    outs = pl.pallas_call(
        body,
        name=f"scatter_start_{stage[0]}",
        in_specs=[HBM] * (nh + nt),
        out_specs=[SEM, SEM] + [HBM] * (nh + nt) + [pl.BlockSpec(memory_space=pltpu.VMEM)],
        out_shape=[pltpu.SemaphoreType.DMA((N_SEM,)), pltpu.SemaphoreType.DMA((N_SEM,))]
        + [pltpu.HBM(halves[n].shape, BF16) for n in stage] + [pltpu.HBM(_slot_shape(s), BF16) for s in touched]
        + [jax.ShapeDtypeStruct((8, LANE), F32)],
        input_output_aliases={k: 2 + k for k in range(nh + nt)},
        compiler_params=pltpu.CompilerParams(has_side_effects=EFFECT),
    )(*[_in_hbm(halves[n]) for n in stage], *[_in_hbm(slots[s]) for s in touched])
    return outs[0], outs[1], dict(zip(stage, outs[2:2 + nh])), dict(zip(touched, outs[2 + nh:2 + nh + nt]))


def _scatter_wait(send_sems, recv_sems, halves, slots, after):
    stage, touched = tuple(halves), tuple(slots)
    nh, nt = len(stage), len(touched)

    def body(*refs):
        src = dict(zip(stage, refs[:nh]))
        dst = dict(zip(touched, refs[nh:nh + nt]))
        sems_s, sems_r = refs[nh + nt], refs[nh + nt + 1]
        _, _, c, j = _position()
        for jj in range(N_CHIPS):
            @pl.when(j == jj)
            def _():
                for ss in range(N_CHIPS):
                    if ss != jj:
                        for i, p in _stage_pieces(jj, stage):
                            _scatter_copy(src, dst, sems_s, sems_r, c, ss, jj, i, p).wait_recv()
                for kk in range(N_CHIPS):
                    if kk != jj:
                        for i, p in _stage_pieces(kk, stage):
                            _scatter_copy(src, dst, sems_s, sems_r, c, jj, kk, i, p).wait_send()

    outs = pl.pallas_call(
        body,
        name=f"scatter_wait_{stage[0]}",
        in_specs=[HBM] * (nh + nt) + [SEM, SEM, ANY],
        out_specs=[HBM] * (nh + nt),
        out_shape=[pltpu.HBM(halves[n].shape, BF16) for n in stage] + [pltpu.HBM(_slot_shape(s), BF16) for s in touched],
        input_output_aliases={k: k for k in range(nh + nt)},
        compiler_params=pltpu.CompilerParams(has_side_effects=EFFECT),
    )(*[halves[n] for n in stage], *[slots[s] for s in touched], send_sems, recv_sems, after)
    return dict(zip(touched, outs[nh:]))


SUM_ROWS = {"win": 448, "mk": 256, "br0": 256, "br1": 256, "br2": 256, "out": 256}


def _sum_slots(slots, c_arr, name):
    _, rows, cols = slots.shape
    tr = SUM_ROWS[name]
    nb = rows // tr
    if HALF_AXIS[name] == 1:
        out_index = lambda i, c_ref: (i, c_ref[0])
    else:
        out_index = lambda i, c_ref: (nb * c_ref[0] + i, 0)

    def body(c_ref, s_ref, o_ref):
        acc = s_ref[0].astype(F32)
        for k in range(1, N_CHIPS):
            acc = acc + s_ref[k].astype(F32)
        o_ref[...] = acc

    return pl.pallas_call(
        body,
        name=f"sum_slots_{name}",
        grid_spec=pltpu.PrefetchScalarGridSpec(
            num_scalar_prefetch=1,
            grid=(nb,),
            in_specs=[pl.BlockSpec((N_CHIPS, tr, cols), lambda i, c_ref: (0, i, 0))],
            out_specs=pl.BlockSpec((tr, cols), out_index),
        ),
        out_shape=jax.ShapeDtypeStruct(SHARD_SHAPES[name], F32),
        compiler_params=_params(("parallel",)),
    )(c_arr, slots)


def _share_sums(sums):
    def body(*refs):
        bufs = refs[len(SHARDS):2 * len(SHARDS)]
        send_sems, recv_sems = refs[2 * len(SHARDS):]
        x, y, c, _ = _position()
        copies = []
        for k, (n, b) in enumerate(zip(SHARDS, bufs)):
            mine = _own_half(b, SHARD_SHAPES[n], HALF_AXIS[n], c)
            copies.append(pltpu.make_async_remote_copy(
                src_ref=mine, dst_ref=mine, send_sem=send_sems.at[k], recv_sem=recv_sems.at[k],
                device_id=(x, y, 1 - c), device_id_type=MESH))
        for cp in copies:
            cp.start()
        for cp in copies:
            cp.wait_recv()
        for cp in copies:
            cp.wait_send()

    outs = pl.pallas_call(
        body,
        name="share_sums",
        in_specs=[ANY] * len(SHARDS),
        out_specs=[ANY] * len(SHARDS),
        out_shape=[jax.ShapeDtypeStruct(sums[n].shape, F32) for n in SHARDS],
        input_output_aliases={k: k for k in range(len(SHARDS))},
        scratch_shapes=[pltpu.SemaphoreType.DMA((len(SHARDS),)), pltpu.SemaphoreType.DMA((len(SHARDS),))],
        compiler_params=pltpu.CompilerParams(has_side_effects=True),
    )(*[sums[n] for n in SHARDS])
    return dict(zip(SHARDS, outs))


N_DEV = 8


def _all_reduce_small(pack, name):
    rows = pack.shape[0]

    def body(p_ref, o_ref, land, send_sems, recv_sems):
        x, y, c, _ = _position()
        me = 4 * x + 2 * y + c

        def copy(o):
            return pltpu.make_async_remote_copy(
                src_ref=p_ref, dst_ref=land.at[me], send_sem=send_sems.at[o], recv_sem=recv_sems.at[me],
                device_id=(o // 4, (o // 2) % 2, o % 2), device_id_type=MESH)

        def arrival(o):
            return pltpu.make_async_remote_copy(
                src_ref=p_ref, dst_ref=land.at[o], send_sem=send_sems.at[o], recv_sem=recv_sems.at[o],
                device_id=(o // 4, (o // 2) % 2, o % 2), device_id_type=MESH)

        for o in range(N_DEV):
            @pl.when(me != o)
            def _():
                copy(o).start()
        land[me] = p_ref[...]
        for o in range(N_DEV):
            @pl.when(me != o)
            def _():
                arrival(o).wait_recv()
        acc = land[0]
        for o in range(1, N_DEV):
            acc = acc + land[o]
        o_ref[...] = acc
        for o in range(N_DEV):
            @pl.when(me != o)
            def _():
                copy(o).wait_send()

    vmem = pl.BlockSpec(memory_space=pltpu.VMEM)
    return pl.pallas_call(
        body,
        name=name,
        in_specs=[vmem],
        out_specs=vmem,
        out_shape=jax.ShapeDtypeStruct((rows, LANE), F32),
        scratch_shapes=[pltpu.VMEM((N_DEV, rows, LANE), F32), pltpu.SemaphoreType.DMA((N_DEV,)),
                        pltpu.SemaphoreType.DMA((N_DEV,))],
        compiler_params=pltpu.CompilerParams(has_side_effects=True, vmem_limit_bytes=VMEM_LIMIT),
    )(pack)


def _adamw(w, g, m, v, name, tr):
    rows, cols = w.shape
    tr = min(tr, rows)

    def body(w_ref, g_ref, m_ref, v_ref, d_ref, nm_ref, nv_ref):
        gv = g_ref[...]
        nm = ADAM_B1 * m_ref[...] + (1.0 - ADAM_B1) * gv
        nv = ADAM_B2 * v_ref[...] + (1.0 - ADAM_B2) * (gv * gv)
        nm_ref[...] = nm
        nv_ref[...] = nv
        m_hat = nm / (1.0 - ADAM_B1 ** ADAM_STEP)
        v_hat = nv / (1.0 - ADAM_B2 ** ADAM_STEP)
        d_ref[...] = -ADAM_LR * (m_hat / (jnp.sqrt(v_hat) + ADAM_EPS) + ADAM_WD * w_ref[...])

    blk = pl.BlockSpec((tr, cols), lambda i: (i, 0))
    shape = jax.ShapeDtypeStruct((rows, cols), F32)
    return pl.pallas_call(
        body,
        name=f"adamw_{name}",
        grid=(rows // tr,),
        in_specs=[blk] * 4,
        out_specs=[blk] * 3,
        out_shape=[shape] * 3,
        compiler_params=_params(("parallel",)),
    )(w, g, m, v)


SMALL = (("pre_norm_g", (1, D)), ("post_norm_g", (1, D)), ("mem_norm_g", (1, D)), ("conv_w", (CONV_W, D_RNN)),
         ("conv_b", (1, D_RNN)), ("w_rg_a", (RNN_BLOCKS, LANE, LANE)), ("b_rg_a", (1, D_RNN)),
         ("w_rg_x", (RNN_BLOCKS, LANE, LANE)), ("b_rg_x", (1, D_RNN)), ("lru_lambda", (1, D_RNN)),
         ("swa_sinks", (1, SWA_HEADS)), ("rel_bias", (REL_BUCKETS, SWA_HEADS)))
PACK_ROWS = 2176


def _slot_len(shape):
    return -(-math.prod(shape) // LANE) * LANE


def _pack(values):
    parts = []
    for name, shape in SMALL:
        flat = values[name].reshape(-1).astype(F32)
        parts.append(jnp.pad(flat, (0, _slot_len(shape) - flat.shape[0])))
    flat = jnp.concatenate(parts)
    return jnp.pad(flat, (0, PACK_ROWS * LANE - flat.shape[0])).reshape(PACK_ROWS, LANE)


def _unpack(pack, shapes=None):
    flat = pack.reshape(-1)
    out, off = {}, 0
    for name, shape in SMALL:
        shp = shape if shapes is None or name not in shapes else shapes[name]
        out[name] = flat[off:off + math.prod(shp)].reshape(shp)
        off += _slot_len(shape)
    return out


TWIN_WEIGHTS = ("pre_norm_g", "post_norm_g", "mem_norm_g", "w_in", "conv_w", "conv_b", "w_rg_a", "b_rg_a", "w_rg_x",
                "b_rg_x", "lru_lambda", "swa_sinks", "rel_bias", "w_mem_kv", "w_br_rg", "w_br_swa", "w_br_mem", "w_out")
BIG = {"w_in": "win", "w_mem_kv": "mk", "w_br_rg": "br0", "w_br_swa": "br1", "w_br_mem": "br2", "w_out": "out"}


def kernel(x, mem, pre_norm_g, post_norm_g, mem_norm_g, w_in, conv_w, conv_b, w_rg_a, b_rg_a, w_rg_x, b_rg_x, lru_lambda, swa_sinks, rel_bias, w_mem_kv, w_br_rg, w_br_swa, w_br_mem, w_out, loss_target, m_pre_norm_g, m_post_norm_g, m_mem_norm_g, m_w_in, m_conv_w, m_conv_b, m_w_rg_a, m_b_rg_a, m_w_rg_x, m_b_rg_x, m_lru_lambda, m_swa_sinks, m_rel_bias, m_w_mem_kv, m_w_br_rg, m_w_br_swa, m_w_br_mem, m_w_out, v_pre_norm_g, v_post_norm_g, v_mem_norm_g, v_w_in, v_conv_w, v_conv_b, v_w_rg_a, v_b_rg_a, v_w_rg_x, v_b_rg_x, v_lru_lambda, v_swa_sinks, v_rel_bias, v_w_mem_kv, v_w_br_rg, v_w_br_swa, v_w_br_mem, v_w_out):
    args = dict(locals())
    out_shapes = {n: args[n].shape for n in TWIN_WEIGHTS}
    w = {n: (args[n] if n == "rel_bias" else args[n][0]) for n in TWIN_WEIGHTS}
    m = {n: (args["m_" + n] if n == "rel_bias" else args["m_" + n][0]) for n in TWIN_WEIGHTS}
    v = {n: (args["v_" + n] if n == "rel_bias" else args["v_" + n][0]) for n in TWIN_WEIGHTS}
    for d in (w, m, v):
        for n, shape in SMALL:
            if n != "conv_w":
                d[n] = d[n].reshape(shape)

    xi, yi, ci = lax.axis_index("x"), lax.axis_index("y"), lax.axis_index("c")
    chip = 2 * xi + yi
    c_arr = ci.astype(jnp.int32).reshape(1)
    zero = jnp.zeros((), jnp.int32)
    cw0 = (chip * (D_RNN // N_CHIPS)).astype(jnp.int32)

    placed = lax.dynamic_update_slice(jnp.zeros((CONV_W, D_RNN), F32), w["conv_w"], (zero, cw0))
    placed = jnp.where(ci == 0, placed, 0.0).reshape(CONV_W * D_RNN // LANE, LANE)
    conv_w_full = _all_reduce_small(placed, "gather_conv_w").reshape(CONV_W, D_RNN)

    for d in (w, m, v):
        d["w_in"] = d["w_in"].T
    shards = {s: w[n].astype(BF16) for n, s in BIG.items()}
    ag_send, ag_recv, shards_live, in_flight, _ = _gather_start(shards, _gather_place_own(shards))
    ag = {"shards": shards_live}

    def fetch(names, after):
        ag["shards"], landed = _gather_wait(names, ag_send, ag_recv, ag["shards"],
                                            {n: in_flight[n] for n in names}, after)
        ready = _gather_swap(landed)
        return tuple(ready[n] for n in names)

    rs = {"slots": {}, "pending": []}

    def emit(grads):
        received = _swap_halves(grads)
        halves = {n: _add_half(grads[n], received[n], c_arr, n) for n in grads}
        landing = _scatter_place_own(halves, rs["slots"])
        send, recv, halves, landing = _scatter_start(halves, landing)
        rs["slots"].update(landing)
        rs["pending"].append((send, recv, halves, tuple(landing)))

    sp = {n: w[n] for n, _ in SMALL}
    sp["conv_w"] = conv_w_full
    sq, grad_x, d_small = _local_step(x[0], mem[0], loss_target[0], sp, fetch, emit)
    loss = lax.psum(sq[0, 0] * (0.5 / D), ("x", "y", "c"))

    small_total = _all_reduce_small(_pack(d_small), "all_reduce_small")

    for send, recv, halves, touched in rs["pending"]:
        rs["slots"].update(_scatter_wait(send, recv, halves, {s: rs["slots"][s] for s in touched}, small_total))
    sums = _share_sums({n: _sum_slots(rs["slots"][n], c_arr, n) for n in SHARDS})
    g_big = {n: sums[s] for n, s in BIG.items()}

    g_small = _unpack(small_total)
    g_small["conv_w"] = lax.dynamic_slice(g_small["conv_w"], (zero, cw0), (CONV_W, D_RNN // N_CHIPS))

    grad, delta, new_m, new_v = {}, {}, {}, {}
    for n, s in BIG.items():
        grad[n] = g_big[n]
        delta[n], new_m[n], new_v[n] = _adamw(w[n], g_big[n], m[n], v[n], s, 224 if n == "w_in" else 128)
    for group in (grad, delta, new_m, new_v):
        group["w_in"] = group["w_in"].T
    d_, m_, v_ = _adamw(_pack(w), _pack(g_small), _pack(m), _pack(v), "small", PACK_ROWS)
    shard_shapes = {"conv_w": (CONV_W, D_RNN // N_CHIPS)}
    d_, m_, v_ = (_unpack(a, shard_shapes) for a in (d_, m_, v_))
    for n, _ in SMALL:
        grad[n], delta[n], new_m[n], new_v[n] = g_small[n], d_[n], m_[n], v_[n]

    outs = [loss, grad_x.reshape(1, S, D)]
    for group in (grad, delta, new_m, new_v):
        outs += [group[n].reshape(out_shapes[n]) for n in TWIN_WEIGHTS]
    return tuple(outs)
```

```python
import functools
import math
from typing import NamedTuple

import jax
import jax.numpy as jnp
from jax import lax
from jax.experimental import pallas as pl
from jax.experimental.pallas import tpu as pltpu

F32 = jnp.float32
BF16 = jnp.bfloat16
MESH = pl.DeviceIdType.MESH

S = 2048
D = 2048
MEM = 256
D_RNN = 1024
RNN_BLOCKS = 8
CONV_W = 4
LRU_C = 8.0
SWA_HEADS = 16
SWA_HD = 64
WINDOW = 128
MEM_HEADS = 4
MEM_HD = 256
REL_BUCKETS = 32
REL_MAX_DIST = 128
EPS = 1e-6
NEG_INF = -1e30
LANE = 128
SHARD = 3136
HALF_TILE = 64
N_CHIPS = 4
VMEM_LIMIT = 56 * 1024 * 1024

ADAM_LR = 0.001
ADAM_B1 = 0.9
ADAM_B2 = 0.999
ADAM_EPS = 1e-08
ADAM_WD = 0.01
ADAM_STEP = 10

GROUP_TILES = {"A": 16, "B": 18, "C": 16, "D": 48}
GROUPS = ("A", "B", "C", "D")


def _params(sem=None):
    return pltpu.CompilerParams(dimension_semantics=sem, vmem_limit_bytes=VMEM_LIMIT)


def _sigmoid(v):
    return jax.nn.sigmoid(v)


def _tile_home(t):
    if t < 16:
        return "A", t
    if t < 24:
        return "B", t - 16
    if t < 26:
        return "B", t - 24 + 16
    if t < 34:
        return "B", t - 26 + 8
    if t < 50:
        return "C", t - 34
    return "D", t - 50


def _shard_runs(j):
    runs = []
    per_shard = SHARD // HALF_TILE
    for q in range(per_shard * j, per_shard * (j + 1)):
        g, gt = _tile_home(q // 2)
        row = gt * LANE + (q % 2) * HALF_TILE
        if runs and runs[-1][2] == g and runs[-1][3] + runs[-1][1] == row:
            runs[-1][1] += HALF_TILE
        else:
            runs.append([(q - per_shard * j) * HALF_TILE, HALF_TILE, g, row])
    return [tuple(r) for r in runs]


_DIMS = {
    "nn": (((1,), (0,)), ((), ())),
    "nt": (((1,), (1,)), ((), ())),
    "tn": (((0,), (0,)), ((), ())),
}


def _mm(a, b, mode, out_dtype, tm, tn, tk, name, acc=None, after=None):
    if mode == "nn":
        (m, k), n = a.shape, b.shape[1]
    elif mode == "nt":
        (m, k), n = a.shape, b.shape[0]
    else:
        (k, m), n = a.shape, b.shape[1]
    tm, tn, tk = min(tm, m), min(tn, n), min(tk, k)
    assert m % tm == 0 and n % tn == 0 and k % tk == 0, (name, m, n, k)
    nk = k // tk
    has_acc = acc is not None

    def body(*refs):
        a_ref, b_ref = refs[0], refs[1]
        o_ref = refs[3] if has_acc else refs[2]
        p = lax.dot_general(a_ref[...], b_ref[...], _DIMS[mode], preferred_element_type=F32)

        def finish(v):
            if has_acc:
                v = v + refs[2][...]
            o_ref[...] = v.astype(out_dtype)

        if nk == 1:
            finish(p)
        else:
            s_ref = refs[-1]
            kk = pl.program_id(2)

            @pl.when(kk == 0)
            def _():
                s_ref[...] = p

            @pl.when(kk > 0)
            def _():
                s_ref[...] += p

            @pl.when(kk == nk - 1)
            def _():
                finish(s_ref[...])

    if mode == "nn":
        a_spec = pl.BlockSpec((tm, tk), lambda i, j, kk: (i, kk))
        b_spec = pl.BlockSpec((tk, tn), lambda i, j, kk: (kk, j))
    elif mode == "nt":
        a_spec = pl.BlockSpec((tm, tk), lambda i, j, kk: (i, kk))
        b_spec = pl.BlockSpec((tn, tk), lambda i, j, kk: (j, kk))
    else:
        a_spec = pl.BlockSpec((tk, tm), lambda i, j, kk: (kk, i))
        b_spec = pl.BlockSpec((tk, tn), lambda i, j, kk: (kk, j))
    o_spec = pl.BlockSpec((tm, tn), lambda i, j, kk: (i, j))
    in_specs = [a_spec, b_spec] + ([o_spec] if has_acc else [])
    args = (a, b) + ((acc,) if has_acc else ())
    if after is not None:
        in_specs.append(pl.BlockSpec(memory_space=pl.ANY))
        args += (after,)
    n_in = len(args)
    kernel_body = body

    def body(*refs):
        kernel_body(*(refs[:n_in - (after is not None)] + refs[n_in:]))

    return pl.pallas_call(
        body,
        name=name,
        grid=(m // tm, n // tn, nk),
        in_specs=in_specs,
        out_specs=o_spec,
        out_shape=jax.ShapeDtypeStruct((m, n), out_dtype),
        scratch_shapes=[pltpu.VMEM((tm, tn), F32)] if nk > 1 else [],
        compiler_params=_params(("parallel", "parallel", "arbitrary")),
    )(*args)


def _rms_fwd(x, g, name, ts=256):
    r, d = x.shape

    def body(x_ref, g_ref, o_ref):
        xv = x_ref[...]
        inv = lax.rsqrt(jnp.mean(xv * xv, axis=-1, keepdims=True) + EPS)
        o_ref[...] = (xv * inv * g_ref[...]).astype(BF16)

    return pl.pallas_call(
        body,
        name=name,
        grid=(r // ts,),
        in_specs=[pl.BlockSpec((ts, d), lambda i: (i, 0)), pl.BlockSpec((1, d), lambda i: (0, 0))],
        out_specs=pl.BlockSpec((ts, d), lambda i: (i, 0)),
        out_shape=jax.ShapeDtypeStruct((r, d), BF16),
        compiler_params=_params(("parallel",)),
    )(x, g)


def _post_loss(out, x, tgt, g_post, ts=256):
    n = S // ts

    def body(o_ref, x_ref, t_ref, g_ref, sq_ref, dy_ref, do_ref, dg_ref):
        i = pl.program_id(0)

        @pl.when(i == 0)
        def _():
            sq_ref[...] = jnp.zeros_like(sq_ref)
            dg_ref[...] = jnp.zeros_like(dg_ref)

        ov = o_ref[...]
        g = g_ref[...]
        inv = lax.rsqrt(jnp.mean(ov * ov, axis=-1, keepdims=True) + EPS)
        on = ov * inv
        err = x_ref[...] + on * g - t_ref[...]
        sq_ref[...] += jnp.sum(err * err)
        dy = err * (1.0 / D)
        dy_ref[...] = dy
        dg_ref[...] += jnp.sum(dy * on, axis=0, keepdims=True)
        don = dy * g
        do_ref[...] = (inv * (don - on * jnp.mean(don * on, axis=-1, keepdims=True))).astype(BF16)

    row = pl.BlockSpec((ts, D), lambda i: (i, 0))
    vec = pl.BlockSpec((1, D), lambda i: (0, 0))
    return pl.pallas_call(
        body,
        name="post_loss",
        grid=(n,),
        in_specs=[row, row, row, vec],
        out_specs=[pl.BlockSpec((8, LANE), lambda i: (0, 0)), row, row, vec],
        out_shape=[
            jax.ShapeDtypeStruct((8, LANE), F32),
            jax.ShapeDtypeStruct((S, D), F32),
            jax.ShapeDtypeStruct((S, D), BF16),
            jax.ShapeDtypeStruct((1, D), F32),
        ],
        compiler_params=_params(("arbitrary",)),
    )(out, x, tgt, g_post)


def _pre_bwd(dh, x, dy, g_pre, ts=256):
    n = S // ts

    def body(dh_ref, x_ref, dy_ref, g_ref, gx_ref, dg_ref):
        i = pl.program_id(0)

        @pl.when(i == 0)
        def _():
            dg_ref[...] = jnp.zeros_like(dg_ref)

        xv = x_ref[...]
        dhv = dh_ref[...]
        inv = lax.rsqrt(jnp.mean(xv * xv, axis=-1, keepdims=True) + EPS)
        xn = xv * inv
        dg_ref[...] += jnp.sum(dhv * xn, axis=0, keepdims=True)
        dxn = dhv * g_ref[...]
        gx_ref[...] = dy_ref[...] + inv * (dxn - xn * jnp.mean(dxn * xn, axis=-1, keepdims=True))

    row = pl.BlockSpec((ts, D), lambda i: (i, 0))
    vec = pl.BlockSpec((1, D), lambda i: (0, 0))
    return pl.pallas_call(
        body,
        name="pre_bwd",
        grid=(n,),
        in_specs=[row, row, row, vec],
        out_specs=[row, vec],
        out_shape=[jax.ShapeDtypeStruct((S, D), F32), jax.ShapeDtypeStruct((1, D), F32)],
        compiler_params=_params(("arbitrary",)),
    )(dh, x, dy, g_pre)


def _memnorm_bwd(dmemn, mem):
    def body(d_ref, m_ref, dg_ref):
        mv = m_ref[...]
        inv = lax.rsqrt(jnp.mean(mv * mv, axis=-1, keepdims=True) + EPS)
        dg_ref[...] = jnp.sum(d_ref[...] * mv * inv, axis=0, keepdims=True)

    return pl.pallas_call(
        body,
        name="memnorm_bwd",
        out_shape=jax.ShapeDtypeStruct((1, D), F32),
        compiler_params=_params(),
    )(dmemn, mem)


T_RNN = 256


def _neg_expm1(z):
    poly = -z * (1.0 + z * (0.5 + z * (1.0 / 6 + z * (1.0 / 24 + z * (1.0 / 120 + z * (1.0 / 720))))))
    return jnp.where(z > -0.1, poly, 1.0 - jnp.exp(z))


def _softplus_neg(lam):
    return jnp.maximum(-lam, 0.0) + jnp.log1p(jnp.exp(-jnp.abs(lam)))


def _rnn_gates(conv, wa_ref, ba, wx_ref, bx, lam, first_row):
    cbf = conv.astype(BF16)
    ga, gx = [], []
    for n in range(RNN_BLOCKS):
        c_n = cbf[:, n * LANE:(n + 1) * LANE]
        ga.append(jnp.dot(c_n, wa_ref[n], preferred_element_type=F32))
        gx.append(jnp.dot(c_n, wx_ref[n], preferred_element_type=F32))
    gate_r = _sigmoid(jnp.concatenate(ga, axis=1) + ba)
    gate_i = _sigmoid(jnp.concatenate(gx, axis=1) + bx)
    sp = _softplus_neg(lam)
    log_a = -LRU_C * gate_r * sp
    a = jnp.exp(log_a)
    mult_raw = jnp.sqrt(_neg_expm1(2.0 * log_a))
    mult = jnp.where(first_row, 1.0, mult_raw)
    return cbf, gate_r, gate_i, sp, a, mult_raw, mult


def _rglru_fwd(p_a, conv_w, conv_b, wa, ba, wx, bx, lam):
    t = T_RNN
    n = S // t

    def body(xr_ref, g_ref, cw_ref, cb_ref, wa_ref, ba_ref, wx_ref, bx_ref, lam_ref,
             y_ref, h_ref, xp_s, hcar, a_s, b_s):
        i = pl.program_id(0)

        @pl.when(i == 0)
        def _():
            xp_s[0:8, :] = jnp.zeros((8, D_RNN), F32)
            hcar[...] = jnp.zeros_like(hcar)

        @pl.when(i > 0)
        def _():
            xp_s[0:8, :] = xp_s[t:t + 8, :]

        xp_s[8:8 + t, :] = xr_ref[...]
        conv = cb_ref[...]
        for k in range(CONV_W):
            conv = conv + cw_ref[k:k + 1, :] * xp_s[8 - k:8 - k + t, :]
        rows = i * t + lax.broadcasted_iota(jnp.int32, (t, 1), 0)
        _, _, gate_i, _, a, _, mult = _rnn_gates(
            conv, wa_ref, ba_ref[...], wx_ref, bx_ref[...], lam_ref[...], rows == 0)
        a_s[...] = a
        b_s[...] = mult * gate_i * conv

        def step(tt, h):
            h = a_s[pl.ds(tt, 1), :] * h + b_s[pl.ds(tt, 1), :]
            h_ref[pl.ds(tt, 1), :] = h
            return h

        hcar[...] = lax.fori_loop(0, t, step, hcar[...], unroll=8)
        g = g_ref[...]
        y_ref[...] = (h_ref[...] * (g * _sigmoid(g))).astype(BF16)

    blk = lambda c: pl.BlockSpec((t, D_RNN), lambda i: (i, c))
    full = lambda shape: pl.BlockSpec(shape, lambda i: (0,) * len(shape))
    return pl.pallas_call(
        body,
        name="rglru_fwd",
        grid=(n,),
        in_specs=[blk(0), blk(1), full((CONV_W, D_RNN)), full((1, D_RNN)),
                  full((RNN_BLOCKS, LANE, LANE)), full((1, D_RNN)),
                  full((RNN_BLOCKS, LANE, LANE)), full((1, D_RNN)), full((1, D_RNN))],
        out_specs=[blk(0), blk(0)],
        out_shape=[jax.ShapeDtypeStruct((S, D_RNN), BF16), jax.ShapeDtypeStruct((S, D_RNN), F32)],
        scratch_shapes=[pltpu.VMEM((t + 8, D_RNN), F32), pltpu.VMEM((1, D_RNN), F32),
                        pltpu.VMEM((t, D_RNN), F32), pltpu.VMEM((t, D_RNN), F32)],
        compiler_params=_params(("arbitrary",)),
    )(p_a, p_a, conv_w, conv_b, wa, ba, wx, bx, lam)


def _rglru_bwd(dy, p_a, hseq, conv_w, conv_b, wa, ba, wx, bx, lam):
    t = T_RNN
    n = S // t
    rb = t // 8

    def body(dy_ref, xr_ref, g_ref, h_ref, xrp_ref, hp_ref, cw_ref, cb_ref, wa_ref, ba_ref, wx_ref, bx_ref, lam_ref,
             dp_ref, dcw_ref, dcb_ref, dwa_ref, dba_ref, dwx_ref, dbx_ref, dlam_ref,
             xp_s, hp_s, dxp_s, lamcar, a_s, dh_s, lam_s):
        i = pl.program_id(0)
        r = n - 1 - i

        @pl.when(i == 0)
        def _():
            for ref in (dcw_ref, dcb_ref, dwa_ref, dba_ref, dwx_ref, dbx_ref, dlam_ref, lamcar):
                ref[...] = jnp.zeros_like(ref)
            dxp_s[t:t + 8, :] = jnp.zeros((8, D_RNN), F32)

        @pl.when(i > 0)
        def _():
            dxp_s[t:t + 8, :] = dxp_s[0:8, :]

        has_prev = r > 0
        xp_s[0:8, :] = jnp.where(has_prev, xrp_ref[...], 0.0)
        xp_s[8:8 + t, :] = xr_ref[...]
        hp_s[0:8, :] = jnp.where(has_prev, hp_ref[...], 0.0)
        hp_s[8:8 + t, :] = h_ref[...]
        xs = [xp_s[8 - k:8 - k + t, :] for k in range(CONV_W)]
        conv = cb_ref[...]
        for k in range(CONV_W):
            conv = conv + cw_ref[k:k + 1, :] * xs[k]
        rows = r * t + lax.broadcasted_iota(jnp.int32, (t, 1), 0)
        first = rows == 0
        lam_p = lam_ref[...]
        cbf, gate_r, gate_i, sp, a, mult_raw, mult = _rnn_gates(
            conv, wa_ref, ba_ref[...], wx_ref, bx_ref[...], lam_p, first)

        g = g_ref[...]
        sg = _sigmoid(g)
        dyv = dy_ref[...]
        a_s[...] = a
        dh_s[...] = dyv * (g * sg)
        dg = dyv * h_ref[...] * (sg * (1.0 + g * (1.0 - sg)))

        def step(jj, car):
            tt = t - 1 - jj
            lm = dh_s[pl.ds(tt, 1), :] + car
            lam_s[pl.ds(tt, 1), :] = lm
            return a_s[pl.ds(tt, 1), :] * lm

        lamcar[...] = lax.fori_loop(0, t, step, lamcar[...], unroll=8)
        db = lam_s[...]
        da = db * hp_s[7:7 + t, :]
        dmult = db * gate_i * conv
        dgate_i = db * mult * conv
        dconv = db * mult * gate_i
        dlog_a = da * a + jnp.where(first, 0.0, dmult * (-(a * a) / mult_raw))
        dgate_r = dlog_a * (-LRU_C * sp)
        dsp = jnp.sum(dlog_a * (-LRU_C * gate_r), axis=0, keepdims=True)
        dlam_ref[...] += dsp * (-_sigmoid(-lam_p))
        dga = dgate_r * gate_r * (1.0 - gate_r)
        dgx = dgate_i * gate_i * (1.0 - gate_i)
        dba_ref[...] += jnp.sum(dga, axis=0, keepdims=True)
        dbx_ref[...] += jnp.sum(dgx, axis=0, keepdims=True)
        dga16, dgx16 = dga.astype(BF16), dgx.astype(BF16)
        back = []
        for nb in range(RNN_BLOCKS):
            sl = slice(nb * LANE, (nb + 1) * LANE)
            dwa_ref[nb] += lax.dot_general(cbf[:, sl], dga16[:, sl], _DIMS["tn"], preferred_element_type=F32)
            dwx_ref[nb] += lax.dot_general(cbf[:, sl], dgx16[:, sl], _DIMS["tn"], preferred_element_type=F32)
            back.append(lax.dot_general(dga16[:, sl], wa_ref[nb], _DIMS["nt"], preferred_element_type=F32)
                        + lax.dot_general(dgx16[:, sl], wx_ref[nb], _DIMS["nt"], preferred_element_type=F32))
        dconv = dconv + jnp.concatenate(back, axis=1)
        dcb_ref[...] += jnp.sum(dconv, axis=0, keepdims=True)
        for k in range(CONV_W):
            dcw_ref[k:k + 1, :] += jnp.sum(dconv * xs[k], axis=0, keepdims=True)
        dxp_s[0:t, :] = dconv
        dxr = cw_ref[0:1, :] * dconv
        for k in range(1, CONV_W):
            dxr = dxr + cw_ref[k:k + 1, :] * dxp_s[k:k + t, :]
        dp_ref[:, 0:D_RNN] = dxr.astype(BF16)
        dp_ref[:, D_RNN:2 * D_RNN] = dg.astype(BF16)

    blk = lambda c: pl.BlockSpec((t, D_RNN), lambda i: (n - 1 - i, c))
    prev8 = pl.BlockSpec((8, D_RNN), lambda i: (jnp.maximum((n - 1 - i) * rb - 1, 0), 0))
    full = lambda shape: pl.BlockSpec(shape, lambda i: (0,) * len(shape))
    vec = full((1, D_RNN))
    mat = full((RNN_BLOCKS, LANE, LANE))
    return pl.pallas_call(
        body,
        name="rglru_bwd",
        grid=(n,),
        in_specs=[blk(0), blk(0), blk(1), blk(0), prev8, prev8,
                  full((CONV_W, D_RNN)), vec, mat, vec, mat, vec, vec],
        out_specs=[pl.BlockSpec((t, 2 * D_RNN), lambda i: (n - 1 - i, 0)),
                   full((CONV_W, D_RNN)), vec, mat, vec, mat, vec, vec],
        out_shape=[jax.ShapeDtypeStruct((S, 2 * D_RNN), BF16),
                   jax.ShapeDtypeStruct((CONV_W, D_RNN), F32), jax.ShapeDtypeStruct((1, D_RNN), F32),
                   jax.ShapeDtypeStruct((RNN_BLOCKS, LANE, LANE), F32), jax.ShapeDtypeStruct((1, D_RNN), F32),
                   jax.ShapeDtypeStruct((RNN_BLOCKS, LANE, LANE), F32), jax.ShapeDtypeStruct((1, D_RNN), F32),
                   jax.ShapeDtypeStruct((1, D_RNN), F32)],
        scratch_shapes=[pltpu.VMEM((t + 8, D_RNN), F32), pltpu.VMEM((t + 8, D_RNN), F32),
                        pltpu.VMEM((t + 8, D_RNN), F32), pltpu.VMEM((1, D_RNN), F32),
                        pltpu.VMEM((t, D_RNN), F32), pltpu.VMEM((t, D_RNN), F32), pltpu.VMEM((t, D_RNN), F32)],
        compiler_params=_params(("arbitrary",)),
    )(dy, p_a, p_a, hseq, p_a, hseq, conv_w, conv_b, wa, ba, wx, bx, lam)


QB = WINDOW
KB2 = 2 * WINDOW
N_QB = S // QB
N_PAIR = SWA_HEADS // 2


def _swa_keys(kvc_ref, kvp_ref):
    kk = jnp.concatenate([kvp_ref[:, 0:LANE], kvc_ref[:, 0:LANE]], axis=0)
    vv = jnp.concatenate([kvp_ref[:, LANE:2 * LANE], kvc_ref[:, LANE:2 * LANE]], axis=0)
    lo = lax.broadcasted_iota(jnp.int32, (1, LANE), 1) < SWA_HD
    kk_sw, vv_sw = pltpu.roll(kk, SWA_HD, 1), pltpu.roll(vv, SWA_HD, 1)
    kd = [jnp.where(lo, kk, kk_sw).astype(BF16), jnp.where(lo, kk_sw, kk).astype(BF16)]
    vd = [jnp.where(lo, vv, vv_sw).astype(BF16), jnp.where(lo, vv_sw, vv).astype(BF16)]
    return lo, kd, vd


def _swa_valid(n):
    qi = lax.broadcasted_iota(jnp.int32, (QB, KB2), 0)
    kj = lax.broadcasted_iota(jnp.int32, (QB, KB2), 1)
    dist = qi + WINDOW - kj
    return (dist >= 0) & (dist < WINDOW) & ((n > 0) | (kj >= WINDOW))


def _swa_probs(qh16, kd, bias, sink, valid):
    lg = lax.dot_general(qh16, kd, _DIMS["nt"], preferred_element_type=F32) * (SWA_HD ** -0.5) + bias
    lg = jnp.where(valid, lg, NEG_INF)
    m = jnp.maximum(jnp.max(lg, axis=-1, keepdims=True), sink)
    p = jnp.exp(lg - m)
    es = jnp.exp(sink - m)
    den = jnp.sum(p, axis=-1, keepdims=True) + es
    return p / den, es / den


def _swa_specs():
    q = pl.BlockSpec((QB, D_RNN), lambda n: (n, 0))
    g = pl.BlockSpec((QB, D_RNN), lambda n: (n, 1))
    kvc = pl.BlockSpec((QB, 2 * LANE), lambda n: (n, 8))
    kvp = pl.BlockSpec((QB, 2 * LANE), lambda n: (jnp.maximum(n - 1, 0), 8))
    bias = pl.BlockSpec((SWA_HEADS, QB, KB2), lambda n: (0, 0, 0))
    sinks = pl.BlockSpec(memory_space=pltpu.SMEM)
    return q, g, kvc, kvp, bias, sinks


def _swa_fwd(p_b, bias_t, sinks):
    def body(q_ref, g_ref, kvc_ref, kvp_ref, bias_ref, sink_ref, y_ref, o_ref):
        n = pl.program_id(0)
        lo, kd, vd = _swa_keys(kvc_ref, kvp_ref)
        valid = _swa_valid(n)
        for hp in range(N_PAIR):
            sl = slice(hp * LANE, (hp + 1) * LANE)
            kvh = hp // (N_PAIR // 2)
            q = q_ref[:, sl]
            outs = []
            for j in range(2):
                mh = lo if j == 0 else jnp.logical_not(lo)
                qh16 = jnp.where(mh, q, 0.0).astype(BF16)
                probs, _ = _swa_probs(qh16, kd[kvh], bias_ref[2 * hp + j], sink_ref[2 * hp + j], valid)
                outs.append(jnp.dot(probs.astype(BF16), vd[kvh], preferred_element_type=F32))
            o = jnp.where(lo, outs[0], outs[1])
            o_ref[:, sl] = o
            g = g_ref[:, sl]
            y_ref[:, sl] = (o * (g * _sigmoid(g))).astype(BF16)

    q, g, kvc, kvp, bias, sinks_spec = _swa_specs()
    out = pl.BlockSpec((QB, D_RNN), lambda n: (n, 0))
    return pl.pallas_call(
        body,
        name="swa_fwd",
        grid=(N_QB,),
        in_specs=[q, g, kvc, kvp, bias, sinks_spec],
        out_specs=[out, out],
        out_shape=[jax.ShapeDtypeStruct((S, D_RNN), BF16), jax.ShapeDtypeStruct((S, D_RNN), F32)],
        compiler_params=_params(("parallel",)),
    )(p_b, p_b, p_b, p_b, bias_t, sinks)


def _swa_bwd(dy, p_b, o_swa, bias_t, sinks):
    def body(dy_ref, q_ref, g_ref, kvc_ref, kvp_ref, o_ref, bias_ref, sink_ref,
             dp_ref, dk_ref, dv_ref, dbias_ref, dsink_ref):
        n = pl.program_id(0)

        @pl.when(n == 0)
        def _():
            for ref in (dk_ref, dv_ref, dbias_ref, dsink_ref):
                ref[...] = jnp.zeros_like(ref)

        lo, kd, vd = _swa_keys(kvc_ref, kvp_ref)
        hi = jnp.logical_not(lo)
        valid = _swa_valid(n)
        dk_blk = jnp.zeros((KB2, LANE), F32)
        dv_blk = jnp.zeros((KB2, LANE), F32)
        for kvh in range(2):
            dk_pair = jnp.zeros((KB2, LANE), F32)
            dv_pair = jnp.zeros((KB2, LANE), F32)
            for hp in range(kvh * (N_PAIR // 2), (kvh + 1) * (N_PAIR // 2)):
                sl = slice(hp * LANE, (hp + 1) * LANE)
                q = q_ref[:, sl]
                g = g_ref[:, sl]
                o = o_ref[:, sl]
                dyv = dy_ref[:, sl]
                sg = _sigmoid(g)
                do = dyv * (g * sg)
                dp_ref[:, D_RNN + hp * LANE:D_RNN + (hp + 1) * LANE] = (
                    dyv * o * (sg * (1.0 + g * (1.0 - sg)))).astype(BF16)
                dqs = []
                for j in range(2):
                    h = 2 * hp + j
                    mh = lo if j == 0 else hi
                    qh16 = jnp.where(mh, q, 0.0).astype(BF16)
                    sink = sink_ref[h]
                    probs, psink = _swa_probs(qh16, kd[kvh], bias_ref[h], sink, valid)
                    doh = jnp.where(mh, do, 0.0)
                    doh16 = doh.astype(BF16)
                    delta = jnp.sum(doh * o, axis=-1, keepdims=True)
                    dpr = lax.dot_general(doh16, vd[kvh], _DIMS["nt"], preferred_element_type=F32)
                    ds = probs * (dpr - delta)
                    dbias_ref[h] += ds
                    dsink_ref[h:h + 1, :] += jnp.zeros((1, LANE), F32) - jnp.sum(psink * delta)
                    ds16 = (ds * (SWA_HD ** -0.5)).astype(BF16)
                    dqs.append(jnp.dot(ds16, kd[kvh], preferred_element_type=F32))
                    dk_pair = dk_pair + lax.dot_general(ds16, qh16, _DIMS["tn"], preferred_element_type=F32)
                    dv_pair = dv_pair + lax.dot_general(probs.astype(BF16), doh16, _DIMS["tn"],
                                                        preferred_element_type=F32)
                dp_ref[:, sl] = jnp.where(lo, dqs[0], dqs[1]).astype(BF16)
            keep = lo if kvh == 0 else hi
            dk_blk = dk_blk + jnp.where(keep, dk_pair + pltpu.roll(dk_pair, SWA_HD, 1), 0.0)
            dv_blk = dv_blk + jnp.where(keep, dv_pair + pltpu.roll(dv_pair, SWA_HD, 1), 0.0)

        cur = pl.ds(pl.multiple_of(n * QB, QB), QB)
        dk_ref[cur, :] += dk_blk[QB:KB2]
        dv_ref[cur, :] += dv_blk[QB:KB2]

        @pl.when(n > 0)
        def _():
            prev = pl.ds(pl.multiple_of((n - 1) * QB, QB), QB)
            dk_ref[prev, :] += dk_blk[0:QB]
            dv_ref[prev, :] += dv_blk[0:QB]

    q, g, kvc, kvp, bias, sinks_spec = _swa_specs()
    row = pl.BlockSpec((QB, D_RNN), lambda n: (n, 0))
    acc = pl.BlockSpec((S, LANE), lambda n: (0, 0))
    return pl.pallas_call(
        body,
        name="swa_bwd",
        grid=(N_QB,),
        in_specs=[row, q, g, kvc, kvp, row, bias, sinks_spec],
        out_specs=[pl.BlockSpec((QB, 2 * D_RNN), lambda n: (n, 0)), acc, acc, bias,
                   pl.BlockSpec((SWA_HEADS, LANE), lambda n: (0, 0))],
        out_shape=[jax.ShapeDtypeStruct((S, GROUP_TILES["B"] * LANE), BF16),
                   jax.ShapeDtypeStruct((S, LANE), F32), jax.ShapeDtypeStruct((S, LANE), F32),
                   jax.ShapeDtypeStruct((SWA_HEADS, QB, KB2), F32),
                   jax.ShapeDtypeStruct((SWA_HEADS, LANE), F32)],
        compiler_params=_params(("arbitrary",)),
    )(dy, p_b, p_b, p_b, p_b, o_swa, bias_t, sinks)


def _swa_pack(dp_b, dk, dv, ts=512):
    def body(_, dk_ref, dv_ref, o_ref):
        o_ref[:, 0:LANE] = dk_ref[...].astype(BF16)
        o_ref[:, LANE:2 * LANE] = dv_ref[...].astype(BF16)

    tile = pl.BlockSpec((ts, LANE), lambda i: (i, 0))
    return pl.pallas_call(
        body,
        name="swa_pack",
        grid=(S // ts,),
        in_specs=[pl.BlockSpec(memory_space=pl.ANY), tile, tile],
        out_specs=pl.BlockSpec((ts, 2 * LANE), lambda i: (i, 8)),
        out_shape=jax.ShapeDtypeStruct(dp_b.shape, dp_b.dtype),
        input_output_aliases={0: 0},
        compiler_params=_params(("parallel",)),
    )(dp_b, dk, dv)


def _split3(v):
    a = v.astype(BF16)
    r = v - a.astype(F32)
    b = r.astype(BF16)
    c = (r - b.astype(F32)).astype(BF16)
    return a, b, c


def _relbias_grad(dbias_flat, onehot_t):
    def body(d_ref, e_ref, o_ref):
        e = e_ref[...]
        acc = jnp.zeros((SWA_HEADS, REL_BUCKETS), F32)
        for term in _split3(d_ref[...]):
            acc = acc + lax.dot_general(term, e, _DIMS["nt"], preferred_element_type=F32)
        o_ref[...] = acc

    return pl.pallas_call(
        body,
        name="relbias_grad",
        out_shape=jax.ShapeDtypeStruct((SWA_HEADS, REL_BUCKETS), F32),
        compiler_params=_params(),
    )(dbias_flat, onehot_t)


TS_MEM = 512


def _mem_probs(q16, mk):
    lg = lax.dot_general(q16, mk, _DIMS["nt"], preferred_element_type=F32) * (MEM_HD ** -0.5)
    p = jnp.exp(lg - jnp.max(lg, axis=-1, keepdims=True))
    return p / jnp.sum(p, axis=-1, keepdims=True)


def _mem_fwd(p_c, mkv):
    def body(q_ref, g_ref, mkv_ref, y_ref, o_ref):
        for hm in range(MEM_HEADS):
            sl = slice(hm * MEM_HD, (hm + 1) * MEM_HD)
            probs = _mem_probs(q_ref[:, sl].astype(BF16), mkv_ref[:, sl])
            o = jnp.dot(probs.astype(BF16), mkv_ref[:, D_RNN + hm * MEM_HD:D_RNN + (hm + 1) * MEM_HD],
                        preferred_element_type=F32)
            o_ref[:, sl] = o
            g = g_ref[:, sl]
            y_ref[:, sl] = (o * (g * _sigmoid(g))).astype(BF16)

    blk = lambda c: pl.BlockSpec((TS_MEM, D_RNN), lambda i: (i, c))
    return pl.pallas_call(
        body,
        name="mem_fwd",
        grid=(S // TS_MEM,),
        in_specs=[blk(0), blk(1), pl.BlockSpec((MEM, 2 * D_RNN), lambda i: (0, 0))],
        out_specs=[blk(0), blk(0)],
        out_shape=[jax.ShapeDtypeStruct((S, D_RNN), BF16), jax.ShapeDtypeStruct((S, D_RNN), F32)],
        compiler_params=_params(("parallel",)),
    )(p_c, p_c, mkv)


def _mem_bwd(dy, p_c, o_mem, mkv):
    def body(dy_ref, q_ref, g_ref, o_ref, mkv_ref, dp_ref, dmkv_ref):
        @pl.when(pl.program_id(0) == 0)
        def _():
            dmkv_ref[...] = jnp.zeros_like(dmkv_ref)

        for hm in range(MEM_HEADS):
            sl = slice(hm * MEM_HD, (hm + 1) * MEM_HD)
            sv = slice(D_RNN + hm * MEM_HD, D_RNN + (hm + 1) * MEM_HD)
            q16 = q_ref[:, sl].astype(BF16)
            mk, mv = mkv_ref[:, sl], mkv_ref[:, sv]
            probs = _mem_probs(q16, mk)
            g, o, dyv = g_ref[:, sl], o_ref[:, sl], dy_ref[:, sl]
            sg = _sigmoid(g)
            do = dyv * (g * sg)
            dp_ref[:, sv] = (dyv * o * (sg * (1.0 + g * (1.0 - sg)))).astype(BF16)
            do16 = do.astype(BF16)
            delta = jnp.sum(do * o, axis=-1, keepdims=True)
            dpr = lax.dot_general(do16, mv, _DIMS["nt"], preferred_element_type=F32)
            ds16 = (probs * (dpr - delta) * (MEM_HD ** -0.5)).astype(BF16)
            dp_ref[:, sl] = jnp.dot(ds16, mk, preferred_element_type=F32).astype(BF16)
            dmkv_ref[:, sl] += lax.dot_general(ds16, q16, _DIMS["tn"], preferred_element_type=F32)
            dmkv_ref[:, sv] += lax.dot_general(probs.astype(BF16), do16, _DIMS["tn"], preferred_element_type=F32)

    blk = lambda c: pl.BlockSpec((TS_MEM, D_RNN), lambda i: (i, c))
    kv = pl.BlockSpec((MEM, 2 * D_RNN), lambda i: (0, 0))
    return pl.pallas_call(
        body,
        name="mem_bwd",
        grid=(S // TS_MEM,),
        in_specs=[blk(0), blk(0), blk(1), blk(0), kv],
        out_specs=[pl.BlockSpec((TS_MEM, 2 * D_RNN), lambda i: (i, 0)), kv],
        out_shape=[jax.ShapeDtypeStruct((S, 2 * D_RNN), BF16), jax.ShapeDtypeStruct((MEM, 2 * D_RNN), F32)],
        compiler_params=_params(("arbitrary",)),
    )(dy, p_c, p_c, o_mem, mkv)


TS_MRG = 512
TD_MRG = 512
N_DBLK = D // TD_MRG


def _merge_fwd(z, p_d):
    def body(z0, z1, z2, g0, g1, g2, o_ref):
        o_ref[...] = (_sigmoid(g0[...]) * z0[...] + _sigmoid(g1[...]) * z1[...]
                      + _sigmoid(g2[...]) * z2[...]).astype(BF16)

    blk = pl.BlockSpec((TS_MRG, TD_MRG), lambda i, d: (i, d))
    gate = lambda b: pl.BlockSpec((TS_MRG, TD_MRG), lambda i, d: (i, b * N_DBLK + d))
    return pl.pallas_call(
        body,
        name="merge_fwd",
        grid=(S // TS_MRG, N_DBLK),
        in_specs=[blk, blk, blk, gate(0), gate(1), gate(2)],
        out_specs=blk,
        out_shape=jax.ShapeDtypeStruct((S, D), BF16),
        compiler_params=_params(("parallel", "parallel")),
    )(z[0], z[1], z[2], p_d, p_d, p_d)


def _merge_bwd(dmerged, z_b, p_d, b, dp_d):
    def body(dm_ref, z_ref, g_ref, *refs):
        dz_ref, dg_ref = refs[-2], refs[-1]
        sg = _sigmoid(g_ref[...])
        dm = dm_ref[...]
        dz_ref[...] = (dm * sg).astype(BF16)
        dg_ref[...] = (dm * z_ref[...] * sg * (1.0 - sg)).astype(BF16)

    blk = pl.BlockSpec((TS_MRG, TD_MRG), lambda i, d: (i, d))
    gate = pl.BlockSpec((TS_MRG, TD_MRG), lambda i, d: (i, b * N_DBLK + d))
    in_specs = [blk, blk, gate]
    args = [dmerged, z_b, p_d]
    aliases = {}
    if dp_d is not None:
        in_specs.append(pl.BlockSpec(memory_space=pl.ANY))
        args.append(dp_d)
        aliases = {3: 1}
    return pl.pallas_call(
        body,
        name=f"merge_bwd{b}",
        grid=(S // TS_MRG, N_DBLK),
        in_specs=in_specs,
        out_specs=[blk, gate],
        out_shape=[jax.ShapeDtypeStruct((S, D), BF16),
                   jax.ShapeDtypeStruct((S, GROUP_TILES["D"] * LANE), BF16)],
        input_output_aliases=aliases,
        compiler_params=_params(("parallel", "parallel")),
    )(*args)


def _bucket_table():
    import numpy as np
    qi = np.arange(QB)[:, None]
    kj = np.arange(KB2)[None, :]
    n = np.maximum(qi + WINDOW - kj, 0)
    max_exact = REL_BUCKETS // 2
    ratio = np.log(np.maximum(n, 1).astype(np.float32) / max_exact) / np.float32(math.log(REL_MAX_DIST / max_exact))
    large = np.minimum(max_exact + (ratio * (REL_BUCKETS - max_exact)).astype(np.int32), REL_BUCKETS - 1)
    bucket = np.where(n < max_exact, n, large).reshape(1, QB * KB2)
    return (bucket == np.arange(REL_BUCKETS)[:, None]).astype(np.float32)


def _bias_expand(rel_bias_t, onehot_t):
    def body(r_ref, e_ref, o_ref):
        e = e_ref[...]
        acc = jnp.zeros((SWA_HEADS, QB * KB2), F32)
        for term in _split3(r_ref[...]):
            acc = acc + jnp.dot(term, e, preferred_element_type=F32)
        o_ref[...] = acc

    return pl.pallas_call(
        body,
        name="bias_expand",
        out_shape=jax.ShapeDtypeStruct((SWA_HEADS, QB * KB2), F32),
        compiler_params=_params(),
    )(rel_bias_t, onehot_t)


PROJ_TN = {"A": 1024, "B": 1152, "C": 1024, "D": 1536}


def _local_step(x, mem, tgt, sp, fetch, emit):
    onehot_t = jnp.asarray(_bucket_table(), BF16)
    bias_t = _bias_expand(sp["rel_bias"].T, onehot_t).reshape(SWA_HEADS, QB, KB2)
    sinks = sp["swa_sinks"].reshape(SWA_HEADS)
    wa16, wx16 = sp["w_rg_a"].astype(BF16), sp["w_rg_x"].astype(BF16)
    rnn = (sp["conv_w"], sp["conv_b"], wa16, sp["b_rg_a"], wx16, sp["b_rg_x"], sp["lru_lambda"])

    h = _rms_fwd(x, sp["pre_norm_g"], "rms_pre")
    memn = _rms_fwd(mem, sp["mem_norm_g"], "rms_mem")
    w_grp, p, last = {}, {}, h
    for g in GROUPS:
        (w_grp[g],) = fetch((g,), last)
        p[g] = last = _mm(h, w_grp[g], "nt", F32, 1024, PROJ_TN[g], D, f"proj_{g}")
    y_rg, hseq = _rglru_fwd(p["A"], *rnn)
    y_swa, o_swa = _swa_fwd(p["B"], bias_t, sinks)
    (wmk,) = fetch(("mk",), y_swa)
    mkv = _mm(memn, wmk, "nn", BF16, MEM, 1024, D, "mkv")
    y_mem, o_mem = _mem_fwd(p["C"], mkv)
    ys = (y_rg, y_swa, y_mem)
    wbr = fetch(("br0", "br1", "br2"), y_mem)
    z = [_mm(ys[b], wbr[b], "nn", F32, 1024, 1024, D_RNN, f"branch_out{b}") for b in range(3)]
    merged = _merge_fwd(z, p["D"])
    (wout,) = fetch(("out",), merged)
    out = _mm(merged, wout, "nn", F32, 1024, 1024, D, "out_proj")
    sq, dy, dout, d_post = _post_loss(out, x, tgt, sp["post_norm_g"])

    tok = emit({"out": _mm(merged, dout, "tn", BF16, 1024, 1024, S, "d_wout")})
    dmerged = _mm(dout, wout, "nt", F32, 1024, 1024, D, "d_merged", after=tok)
    dz, dp_d = [], None
    for b in range(3):
        dz_b, dp_d = _merge_bwd(dmerged, z[b], p["D"], b, dp_d)
        dz.append(dz_b)
    tok = emit({f"br{b}": _mm(ys[b], dz[b], "tn", BF16, 1024, 1024, S, f"d_wbr{b}") for b in range(3)})
    tok = emit({"D": _mm(dp_d, h, "tn", BF16, PROJ_TN["D"], 1024, S, "d_win_D", after=tok)})
    dys = [_mm(dz[b], wbr[b], "nt", F32, 1024, 1024, D, f"d_branch{b}", after=tok) for b in range(3)]
    dp_c, dmkv = _mem_bwd(dys[2], p["C"], o_mem, mkv)
    dmkv16 = dmkv.astype(BF16)
    tok = emit({"mk": _mm(memn, dmkv16, "tn", BF16, 1024, 1024, MEM, "d_wmk")})
    dmemn = _mm(dmkv16, wmk, "nt", F32, MEM, 1024, D, "d_memn", after=tok)
    d_memg = _memnorm_bwd(dmemn, mem)
    dp_a, d_cw, d_cb, d_wa, d_ba, d_wx, d_bx, d_lam = _rglru_bwd(dys[0], p["A"], hseq, *rnn)
    dp_b, dk, dv, d_bias, d_sink = _swa_bwd(dys[1], p["B"], o_swa, bias_t, sinks)
    dp_b = _swa_pack(dp_b, dk, dv)
    d_rel = _relbias_grad(d_bias.reshape(SWA_HEADS, QB * KB2), onehot_t).T
    dp = {"A": dp_a, "B": dp_b, "C": dp_c, "D": dp_d}
    tok = emit({g: _mm(dp[g], h, "tn", BF16, PROJ_TN[g], 1024, S, f"d_win_{g}") for g in ("A", "B", "C")})
    dh = None
    for g in GROUPS:
        dh = _mm(dp[g], w_grp[g], "nn", F32, 1024, 1024, 2304 if g == "B" else 2048, f"d_h_{g}", acc=dh,
                 after=tok if g == "A" else None)
    grad_x, d_pre = _pre_bwd(dh, x, dy, sp["pre_norm_g"])

    d_small = {
        "pre_norm_g": d_pre, "post_norm_g": d_post, "mem_norm_g": d_memg, "conv_w": d_cw, "conv_b": d_cb,
        "w_rg_a": d_wa, "b_rg_a": d_ba, "w_rg_x": d_wx, "b_rg_x": d_bx, "lru_lambda": d_lam,
        "swa_sinks": d_sink[:, 0].reshape(1, SWA_HEADS), "rel_bias": d_rel,
    }
    return sq, grad_x, d_small


ANY = pl.BlockSpec(memory_space=pl.ANY)
SHARD_ROWS = D // N_CHIPS
GATHERED = {"A": (2048, D), "B": (2304, D), "C": (2048, D), "D": (6144, D), "mk": (D, D),
            "br0": (D_RNN, D), "br1": (D_RNN, D), "br2": (D_RNN, D), "out": (D, D)}
SHARD_SHAPES = {"win": (SHARD, D), "mk": (SHARD_ROWS, D), "br0": (D_RNN, SHARD_ROWS), "br1": (D_RNN, SHARD_ROWS),
                "br2": (D_RNN, SHARD_ROWS), "out": (SHARD_ROWS, D)}
SHARDS = tuple(SHARD_SHAPES)
HALF_AXIS = {"win": 1, "mk": 1, "br0": 0, "br1": 0, "br2": 0, "out": 1,
             "A": 1, "B": 1, "C": 1, "D": 1}


def _halved(shape, axis):
    return (shape[0] // 2, shape[1]) if axis == 0 else (shape[0], shape[1] // 2)


class Piece(NamedTuple):
    src: str
    dst: str
    rows: int
    sr0: int
    sc0: int
    dr0: int
    dc0: int
    ncols: int


def _pieces_of(jj):
    out = [Piece("win", g, n, r, 0, gr, 0, D) for r, n, g, gr in _shard_runs(jj)]
    out.append(Piece("mk", "mk", SHARD_ROWS, 0, 0, SHARD_ROWS * jj, 0, D))
    out += [Piece(f"br{b}", f"br{b}", D_RNN, 0, 0, 0, SHARD_ROWS * jj, SHARD_ROWS) for b in range(3)]
    out.append(Piece("out", "out", SHARD_ROWS, 0, 0, SHARD_ROWS * jj, 0, D))
    return out


def _half_rect(ref, p, side, which):
    r0, c0 = (p.sr0, p.sc0) if side == "src" else (p.dr0, p.dc0)
    if HALF_AXIS[p.src] == 1:
        return _rect(ref, r0, p.rows, c0 + which * (p.ncols // 2), p.ncols // 2)
    return _rect(ref, r0 + which * (p.rows // 2), p.rows // 2, c0, p.ncols)


def _rect_in_half(ref, p, side):
    r0, c0 = (p.sr0, p.sc0) if side == "src" else (p.dr0, p.dc0)
    if HALF_AXIS[p.src] == 1:
        return _rect(ref, r0, p.rows, 0, p.ncols // 2)
    return _rect(ref, 0, p.rows // 2, c0, p.ncols)


MAX_PIECES = max(len(_pieces_of(jj)) for jj in range(N_CHIPS))


def _rect(ref, r0, rows, c0, ncols):
    return ref.at[pl.ds(r0, rows), pl.ds(c0, ncols)]


def _position():
    x, y, c = lax.axis_index("x"), lax.axis_index("y"), lax.axis_index("c")
    return x, y, c, 2 * x + y


HBM = pl.BlockSpec(memory_space=pltpu.HBM)
SEM = pl.BlockSpec(memory_space=pltpu.SEMAPHORE)
EFFECT = pltpu.SideEffectType.DATAFLOW_SIDE_EFFECTING
N_SEM = MAX_PIECES * N_CHIPS


def _in_hbm(a):
    return pltpu.with_memory_space_constraint(a, pltpu.HBM)


def _stage_pieces(jj, stage):
    return [(i, p) for i, p in enumerate(_pieces_of(jj)) if p.dst in stage]


def _own_block_table(g):
    import numpy as np
    tbl = np.zeros((N_CHIPS, GATHERED[g][0] // HALF_TILE), np.int32)
    for jj in range(N_CHIPS):
        for r, n, grp, gr in _shard_runs(jj):
            if grp == g:
                for k in range(n // HALF_TILE):
                    tbl[jj, gr // HALF_TILE + k] = r // HALF_TILE + k
    return tbl


def _place_group(w_t, g, table):
    nb = GATHERED[g][0] // HALF_TILE

    def body(t_ref, x_ref, o_ref):
        o_ref[...] = x_ref[...].astype(BF16)

    return pl.pallas_call(
        body,
        name=f"place_{g}",
        grid_spec=pltpu.PrefetchScalarGridSpec(
            num_scalar_prefetch=1,
            grid=(nb,),
            in_specs=[pl.BlockSpec((HALF_TILE, D), lambda b, t: (t[b], 0))],
            out_specs=pl.BlockSpec((HALF_TILE, D), lambda b, t: (b, 0)),
        ),
        out_shape=jax.ShapeDtypeStruct(GATHERED[g], BF16),
        compiler_params=_params(("parallel",)),
    )(table, w_t)


def _place_shard(shard, name):
    rows, cols = shard.shape
    by_rows = HALF_AXIS[name] == 1

    def body(x_ref, o_ref):
        o_ref[...] = x_ref[...].astype(BF16)

    return pl.pallas_call(
        body,
        name=f"place_{name}",
        grid=(N_CHIPS,),
        in_specs=[pl.BlockSpec((rows, cols), lambda b: (0, 0))],
        out_specs=pl.BlockSpec((rows, cols), (lambda b: (b, 0)) if by_rows else (lambda b: (0, b))),
        out_shape=jax.ShapeDtypeStruct(GATHERED[name], BF16),
        compiler_params=_params(("parallel",)),
    )(shard)


def _gather_copy(arr, send_sems, recv_sems, c, jj, i, p, kk):
    rect = _half_rect(arr[p.dst], p, "dst", c)
    return pltpu.make_async_remote_copy(
        src_ref=rect, dst_ref=rect, send_sem=send_sems.at[i * N_CHIPS + kk],
        recv_sem=recv_sems.at[jj * MAX_PIECES + i], device_id=(kk // 2, kk % 2, c), device_id_type=MESH)


def _gather_start(gathered):
    names = tuple(GATHERED)
    ng = len(names)

    def body(*refs):
        arr = dict(zip(names, refs[:ng]))
        send_sems, recv_sems = refs[ng], refs[ng + 1]
        token = refs[-1]
        _, _, c, j = _position()
        for jj in range(N_CHIPS):
            @pl.when(j == jj)
            def _():
                for i, p in enumerate(_pieces_of(jj)):
                    for kk in range(N_CHIPS):
                        if kk != jj:
                            _gather_copy(arr, send_sems, recv_sems, c, jj, i, p, kk).start()
        token[...] = jnp.zeros_like(token)

    outs = pl.pallas_call(
        body,
        name="gather_start",
        in_specs=[HBM] * ng,
        out_specs=[SEM, SEM] + [HBM] * ng + [pl.BlockSpec(memory_space=pltpu.VMEM)],
        out_shape=[pltpu.SemaphoreType.DMA((N_SEM,)), pltpu.SemaphoreType.DMA((N_SEM,))]
        + [pltpu.HBM(GATHERED[n], BF16) for n in names] + [jax.ShapeDtypeStruct((8, LANE), F32)],
        input_output_aliases={k: 2 + k for k in range(ng)},
        compiler_params=pltpu.CompilerParams(has_side_effects=EFFECT),
    )(*[_in_hbm(gathered[n]) for n in names])
    return outs[0], outs[1], dict(zip(names, outs[2:2 + ng])), outs[-1]


def _gather_wait(send_sems, recv_sems, arrays, after):
    stage = tuple(arrays)
    na = len(stage)

    def body(*refs):
        arr = dict(zip(stage, refs[:na]))
        sems_s, sems_r = refs[na], refs[na + 1]
        _, _, c, j = _position()
        for jj in range(N_CHIPS):
            @pl.when(j != jj)
            def _():
                for i, p in _stage_pieces(jj, stage):
                    _gather_copy(arr, sems_s, sems_r, c, jj, i, p, jj).wait_recv()

            @pl.when(j == jj)
            def _():
                for i, p in _stage_pieces(jj, stage):
                    for kk in range(N_CHIPS):
                        if kk != jj:
                            _gather_copy(arr, sems_s, sems_r, c, jj, i, p, kk).wait_send()

    outs = pl.pallas_call(
        body,
        name=f"gather_wait_{stage[0]}",
        in_specs=[HBM] * na + [SEM, SEM, ANY],
        out_specs=[HBM] * na,
        out_shape=[pltpu.HBM(GATHERED[n], BF16) for n in stage],
        input_output_aliases={k: k for k in range(na)},
        compiler_params=pltpu.CompilerParams(has_side_effects=EFFECT),
    )(*[arrays[n] for n in stage], send_sems, recv_sems, after)
    return dict(zip(stage, outs))


def _gather_swap(arrays):
    stage = tuple(arrays)
    na = len(stage)

    def body(*refs):
        dst = dict(zip(stage, refs[na:2 * na]))
        send_sems, recv_sems = refs[2 * na:]
        x, y, c, j = _position()

        def fwd(jj, i, p, which):
            rect = _half_rect(dst[p.dst], p, "dst", which)
            return pltpu.make_async_remote_copy(
                src_ref=rect, dst_ref=rect, send_sem=send_sems.at[jj * MAX_PIECES + i],
                recv_sem=recv_sems.at[jj * MAX_PIECES + i], device_id=(x, y, 1 - c), device_id_type=MESH)

        for jj in range(N_CHIPS):
            @pl.when(j != jj)
            def _():
                for i, p in _stage_pieces(jj, stage):
                    fwd(jj, i, p, c).start()
        for jj in range(N_CHIPS):
            @pl.when(j != jj)
            def _():
                for i, p in _stage_pieces(jj, stage):
                    fwd(jj, i, p, 1 - c).wait_recv()
        for jj in range(N_CHIPS):
            @pl.when(j != jj)
            def _():
                for i, p in _stage_pieces(jj, stage):
                    fwd(jj, i, p, c).wait_send()

    outs = pl.pallas_call(
        body,
        name=f"gather_swap_{stage[0]}",
        in_specs=[ANY] * na,
        out_specs=[ANY] * na,
        out_shape=[jax.ShapeDtypeStruct(GATHERED[n], BF16) for n in stage],
        input_output_aliases={k: k for k in range(na)},
        scratch_shapes=[pltpu.SemaphoreType.DMA((N_SEM,)), pltpu.SemaphoreType.DMA((N_SEM,))],
        compiler_params=pltpu.CompilerParams(has_side_effects=True),
    )(*[arrays[n] for n in stage])
    return dict(zip(stage, outs))


def _own_half(ref, shape, axis, which):
    if axis == 1:
        return ref.at[:, pl.ds(which * (shape[1] // 2), shape[1] // 2)]
    return ref.at[pl.ds(which * (shape[0] // 2), shape[0] // 2), :]


def _swap_halves(grads):
    names = tuple(grads)
    n_tr = len(names)

    def body(*refs):
        src = dict(zip(names, refs[:len(names)]))
        dst = dict(zip(names, refs[len(names):2 * len(names)]))
        send_sems, recv_sems = refs[2 * len(names):]
        x, y, c, _ = _position()
        copies = [pltpu.make_async_remote_copy(
            src_ref=_own_half(src[n], GATHERED[n], HALF_AXIS[n], 1 - c), dst_ref=dst[n],
            send_sem=send_sems.at[k], recv_sem=recv_sems.at[k],
            device_id=(x, y, 1 - c), device_id_type=MESH) for k, n in enumerate(names)]
        for cp in copies:
            cp.start()
        for cp in copies:
            cp.wait_recv()
        for cp in copies:
            cp.wait_send()

    outs = pl.pallas_call(
        body,
        name=f"swap_halves_{names[0]}",
        in_specs=[ANY] * len(names),
        out_specs=[ANY] * len(names),
        out_shape=[jax.ShapeDtypeStruct(_halved(GATHERED[n], HALF_AXIS[n]), BF16) for n in names],
        scratch_shapes=[pltpu.SemaphoreType.DMA((n_tr,)), pltpu.SemaphoreType.DMA((n_tr,))],
        compiler_params=pltpu.CompilerParams(has_side_effects=True),
    )(*[grads[n] for n in names])
    return dict(zip(names, outs))


ADD_ROWS = 256


def _add_half(full, recv, c_arr, name):
    rows, cols = recv.shape
    if HALF_AXIS[name] == 1:
        index = lambda i, c_ref: (i, c_ref[0])
    else:
        nb = rows // ADD_ROWS
        index = lambda i, c_ref: (nb * c_ref[0] + i, 0)

    def body(c_ref, a_ref, b_ref, o_ref):
        o_ref[...] = (a_ref[...].astype(F32) + b_ref[...].astype(F32)).astype(BF16)

    return pl.pallas_call(
        body,
        name=f"add_half_{name}",
        grid_spec=pltpu.PrefetchScalarGridSpec(
            num_scalar_prefetch=1,
            grid=(rows // ADD_ROWS,),
            in_specs=[pl.BlockSpec((ADD_ROWS, cols), index), pl.BlockSpec((ADD_ROWS, cols), lambda i, c_ref: (i, 0))],
            out_specs=pl.BlockSpec((ADD_ROWS, cols), lambda i, c_ref: (i, 0)),
        ),
        out_shape=jax.ShapeDtypeStruct((rows, cols), BF16),
        compiler_params=_params(("parallel",)),
    )(c_arr, full, recv)


SLOT_SHAPES = {n: _halved(SHARD_SHAPES[n], HALF_AXIS[n]) for n in SHARDS}


def _slot_shape(n):
    return (N_CHIPS,) + SLOT_SHAPES[n]


def _stage_shards(stage):
    pieces = [p for jj in range(N_CHIPS) for p in _pieces_of(jj)]
    return tuple(s for s in SHARDS if any(p.src == s and p.dst in stage for p in pieces))


def _scatter_copy(src, dst, send_sems, recv_sems, c, jj, kk, i, p):
    return pltpu.make_async_remote_copy(
        src_ref=_rect_in_half(src[p.dst], p, "dst"), dst_ref=_rect_in_half(dst[p.src].at[jj], p, "src"),
        send_sem=send_sems.at[kk * MAX_PIECES + i], recv_sem=recv_sems.at[jj * MAX_PIECES + i],
        device_id=(kk // 2, kk % 2, c), device_id_type=MESH)


def _scatter_start(halves, slots):
    stage, touched = tuple(halves), tuple(slots)
    nh, nt = len(stage), len(touched)

    def body(*refs):
        src = dict(zip(stage, refs[:nh]))
        dst = dict(zip(touched, refs[nh:nh + nt]))
        send_sems, recv_sems = refs[nh + nt], refs[nh + nt + 1]
        token = refs[-1]
        _, _, c, j = _position()
        for jj in range(N_CHIPS):
            @pl.when(j == jj)
            def _():
                for kk in range(N_CHIPS):
                    if kk != jj:
                        for i, p in _stage_pieces(kk, stage):
                            _scatter_copy(src, dst, send_sems, recv_sems, c, jj, kk, i, p).start()
        token[...] = jnp.zeros_like(token)

    outs = pl.pallas_call(
        body,
        name=f"scatter_start_{stage[0]}",
        in_specs=[HBM] * (nh + nt),
        out_specs=[SEM, SEM] + [HBM] * (nh + nt) + [pl.BlockSpec(memory_space=pltpu.VMEM)],
        out_shape=[pltpu.SemaphoreType.DMA((N_SEM,)), pltpu.SemaphoreType.DMA((N_SEM,))]
        + [pltpu.HBM(halves[n].shape, BF16) for n in stage] + [pltpu.HBM(_slot_shape(s), BF16) for s in touched]
        + [jax.ShapeDtypeStruct((8, LANE), F32)],
        input_output_aliases={k: 2 + k for k in range(nh + nt)},
        compiler_params=pltpu.CompilerParams(has_side_effects=EFFECT),
    )(*[_in_hbm(halves[n]) for n in stage], *[_in_hbm(slots[s]) for s in touched])
    return outs[0], outs[1], dict(zip(stage, outs[2:2 + nh])), dict(zip(touched, outs[2 + nh:2 + nh + nt])), outs[-1]


def _scatter_wait(send_sems, recv_sems, halves, slots, after):
    stage, touched = tuple(halves), tuple(slots)
    nh, nt = len(stage), len(touched)

    def body(*refs):
        src = dict(zip(stage, refs[:nh]))
        dst = dict(zip(touched, refs[nh:nh + nt]))
        sems_s, sems_r = refs[nh + nt], refs[nh + nt + 1]
        _, _, c, j = _position()
        for jj in range(N_CHIPS):
            @pl.when(j == jj)
            def _():
                for ss in range(N_CHIPS):
                    if ss != jj:
                        for i, p in _stage_pieces(jj, stage):
                            _scatter_copy(src, dst, sems_s, sems_r, c, ss, jj, i, p).wait_recv()
                for kk in range(N_CHIPS):
                    if kk != jj:
                        for i, p in _stage_pieces(kk, stage):
                            _scatter_copy(src, dst, sems_s, sems_r, c, jj, kk, i, p).wait_send()

    outs = pl.pallas_call(
        body,
        name=f"scatter_wait_{stage[0]}",
        in_specs=[HBM] * (nh + nt) + [SEM, SEM, ANY],
        out_specs=[HBM] * (nh + nt),
        out_shape=[pltpu.HBM(halves[n].shape, BF16) for n in stage] + [pltpu.HBM(_slot_shape(s), BF16) for s in touched],
        input_output_aliases={k: k for k in range(nh + nt)},
        compiler_params=pltpu.CompilerParams(has_side_effects=EFFECT),
    )(*[halves[n] for n in stage], *[slots[s] for s in touched], send_sems, recv_sems, after)
    return dict(zip(stage, outs[:nh])), dict(zip(touched, outs[nh:]))


SUM_ROWS = {"win": 448, "mk": 256, "br0": 256, "br1": 256, "br2": 256, "out": 256}


def _sum_in_chip_order(chip, own, s_ref):
    acc = None
    for k in range(N_CHIPS):
        term = jnp.where(chip == k, own, s_ref[k].astype(F32))
        acc = term if acc is None else acc + term
    return acc


def _sum_slots(slots, own_half, pos_arr, name):
    _, rows, cols = slots.shape
    tr = SUM_ROWS[name]
    nb = rows // tr
    if HALF_AXIS[name] == 1:
        own_index = lambda i, pos: (nb * pos[1] + i, 0)
        out_index = lambda i, pos: (i, pos[0])
    else:
        own_index = lambda i, pos: (i, pos[1])
        out_index = lambda i, pos: (nb * pos[0] + i, 0)

    def body(pos, s_ref, own_ref, o_ref):
        o_ref[...] = _sum_in_chip_order(pos[1], own_ref[...].astype(F32), s_ref)

    return pl.pallas_call(
        body,
        name=f"sum_slots_{name}",
        grid_spec=pltpu.PrefetchScalarGridSpec(
            num_scalar_prefetch=1,
            grid=(nb,),
            in_specs=[pl.BlockSpec((N_CHIPS, tr, cols), lambda i, pos: (0, i, 0)),
                      pl.BlockSpec((tr, cols), own_index)],
            out_specs=pl.BlockSpec((tr, cols), out_index),
        ),
        out_shape=jax.ShapeDtypeStruct(SHARD_SHAPES[name], F32),
        compiler_params=_params(("parallel",)),
    )(pos_arr, slots, own_half)


def _own_partial_tables():
    import numpy as np
    nb = SHARD // HALF_TILE
    grp, blk = np.zeros((N_CHIPS, nb), np.int32), np.zeros((N_CHIPS, nb), np.int32)
    for jj in range(N_CHIPS):
        for r, n, g, gr in _shard_runs(jj):
            for k in range(n // HALF_TILE):
                grp[jj, r // HALF_TILE + k] = GROUPS.index(g)
                blk[jj, r // HALF_TILE + k] = gr // HALF_TILE + k
    return grp, blk


def _sum_slots_win(slots, own_halves, pos_arr, grp_tbl, blk_tbl):
    nb = SHARD // HALF_TILE
    cols = D // 2

    def own_spec(gi):
        return pl.BlockSpec((HALF_TILE, cols), lambda b, pos, grp, blk: (jnp.where(grp[b] == gi, blk[b], 0), 0))

    def body(pos, grp, blk, s_ref, a_ref, b_ref, c_ref, d_ref, o_ref):
        g = grp[pl.program_id(0)]
        own = a_ref[...]
        for gi, ref in ((1, b_ref), (2, c_ref), (3, d_ref)):
            own = jnp.where(g == gi, ref[...], own)
        o_ref[...] = _sum_in_chip_order(pos[1], own.astype(F32), s_ref)

    return pl.pallas_call(
        body,
        name="sum_slots_win",
        grid_spec=pltpu.PrefetchScalarGridSpec(
            num_scalar_prefetch=3,
            grid=(nb,),
            in_specs=[pl.BlockSpec((N_CHIPS, HALF_TILE, cols), lambda b, pos, grp, blk: (0, b, 0))]
            + [own_spec(gi) for gi in range(len(GROUPS))],
            out_specs=pl.BlockSpec((HALF_TILE, cols), lambda b, pos, grp, blk: (b, pos[0])),
        ),
        out_shape=jax.ShapeDtypeStruct(SHARD_SHAPES["win"], F32),
        compiler_params=_params(("parallel",)),
    )(pos_arr, grp_tbl, blk_tbl, slots, *[own_halves[g] for g in GROUPS])


def _share_sums(sums):
    def body(*refs):
        bufs = refs[len(SHARDS):2 * len(SHARDS)]
        send_sems, recv_sems = refs[2 * len(SHARDS):]
        x, y, c, _ = _position()
        copies = []
        for k, (n, b) in enumerate(zip(SHARDS, bufs)):
            mine = _own_half(b, SHARD_SHAPES[n], HALF_AXIS[n], c)
            copies.append(pltpu.make_async_remote_copy(
                src_ref=mine, dst_ref=mine, send_sem=send_sems.at[k], recv_sem=recv_sems.at[k],
                device_id=(x, y, 1 - c), device_id_type=MESH))
        for cp in copies:
            cp.start()
        for cp in copies:
            cp.wait_recv()
        for cp in copies:
            cp.wait_send()

    outs = pl.pallas_call(
        body,
        name="share_sums",
        in_specs=[ANY] * len(SHARDS),
        out_specs=[ANY] * len(SHARDS),
        out_shape=[jax.ShapeDtypeStruct(sums[n].shape, F32) for n in SHARDS],
        input_output_aliases={k: k for k in range(len(SHARDS))},
        scratch_shapes=[pltpu.SemaphoreType.DMA((len(SHARDS),)), pltpu.SemaphoreType.DMA((len(SHARDS),))],
        compiler_params=pltpu.CompilerParams(has_side_effects=True),
    )(*[sums[n] for n in SHARDS])
    return dict(zip(SHARDS, outs))


N_DEV = 8


def _all_reduce_small(pack, name):
    rows = pack.shape[0]

    def body(p_ref, o_ref, land, send_sems, recv_sems):
        x, y, c, _ = _position()
        me = 4 * x + 2 * y + c

        def copy(o):
            return pltpu.make_async_remote_copy(
                src_ref=p_ref, dst_ref=land.at[me], send_sem=send_sems.at[o], recv_sem=recv_sems.at[me],
                device_id=(o // 4, (o // 2) % 2, o % 2), device_id_type=MESH)

        def arrival(o):
            return pltpu.make_async_remote_copy(
                src_ref=p_ref, dst_ref=land.at[o], send_sem=send_sems.at[o], recv_sem=recv_sems.at[o],
                device_id=(o // 4, (o // 2) % 2, o % 2), device_id_type=MESH)

        for o in range(N_DEV):
            @pl.when(me != o)
            def _():
                copy(o).start()
        land[me] = p_ref[...]
        for o in range(N_DEV):
            @pl.when(me != o)
            def _():
                arrival(o).wait_recv()
        acc = land[0]
        for o in range(1, N_DEV):
            acc = acc + land[o]
        o_ref[...] = acc
        for o in range(N_DEV):
            @pl.when(me != o)
            def _():
                copy(o).wait_send()

    vmem = pl.BlockSpec(memory_space=pltpu.VMEM)
    return pl.pallas_call(
        body,
        name=name,
        in_specs=[vmem],
        out_specs=vmem,
        out_shape=jax.ShapeDtypeStruct((rows, LANE), F32),
        scratch_shapes=[pltpu.VMEM((N_DEV, rows, LANE), F32), pltpu.SemaphoreType.DMA((N_DEV,)),
                        pltpu.SemaphoreType.DMA((N_DEV,))],
        compiler_params=pltpu.CompilerParams(has_side_effects=True, vmem_limit_bytes=VMEM_LIMIT),
    )(pack)


def _adamw(w, g, m, v, name, tr):
    rows, cols = w.shape
    tr = min(tr, rows)

    def body(w_ref, g_ref, m_ref, v_ref, d_ref, nm_ref, nv_ref):
        gv = g_ref[...]
        nm = ADAM_B1 * m_ref[...] + (1.0 - ADAM_B1) * gv
        nv = ADAM_B2 * v_ref[...] + (1.0 - ADAM_B2) * (gv * gv)
        nm_ref[...] = nm
        nv_ref[...] = nv
        m_hat = nm / (1.0 - ADAM_B1 ** ADAM_STEP)
        v_hat = nv / (1.0 - ADAM_B2 ** ADAM_STEP)
        d_ref[...] = -ADAM_LR * (m_hat / (jnp.sqrt(v_hat) + ADAM_EPS) + ADAM_WD * w_ref[...])

    blk = pl.BlockSpec((tr, cols), lambda i: (i, 0))
    shape = jax.ShapeDtypeStruct((rows, cols), F32)
    return pl.pallas_call(
        body,
        name=f"adamw_{name}",
        grid=(rows // tr,),
        in_specs=[blk] * 4,
        out_specs=[blk] * 3,
        out_shape=[shape] * 3,
        compiler_params=_params(("parallel",)),
    )(w, g, m, v)


SMALL = (("pre_norm_g", (1, D)), ("post_norm_g", (1, D)), ("mem_norm_g", (1, D)), ("conv_w", (CONV_W, D_RNN)),
         ("conv_b", (1, D_RNN)), ("w_rg_a", (RNN_BLOCKS, LANE, LANE)), ("b_rg_a", (1, D_RNN)),
         ("w_rg_x", (RNN_BLOCKS, LANE, LANE)), ("b_rg_x", (1, D_RNN)), ("lru_lambda", (1, D_RNN)),
         ("swa_sinks", (1, SWA_HEADS)), ("rel_bias", (REL_BUCKETS, SWA_HEADS)))
PACK_ROWS = 2176


def _slot_len(shape):
    return -(-math.prod(shape) // LANE) * LANE


def _pack(values):
    parts = []
    for name, shape in SMALL:
        flat = values[name].reshape(-1).astype(F32)
        parts.append(jnp.pad(flat, (0, _slot_len(shape) - flat.shape[0])))
    flat = jnp.concatenate(parts)
    return jnp.pad(flat, (0, PACK_ROWS * LANE - flat.shape[0])).reshape(PACK_ROWS, LANE)


def _unpack(pack, shapes=None):
    flat = pack.reshape(-1)
    out, off = {}, 0
    for name, shape in SMALL:
        shp = shape if shapes is None or name not in shapes else shapes[name]
        out[name] = flat[off:off + math.prod(shp)].reshape(shp)
        off += _slot_len(shape)
    return out


TWIN_WEIGHTS = ("pre_norm_g", "post_norm_g", "mem_norm_g", "w_in", "conv_w", "conv_b", "w_rg_a", "b_rg_a", "w_rg_x",
                "b_rg_x", "lru_lambda", "swa_sinks", "rel_bias", "w_mem_kv", "w_br_rg", "w_br_swa", "w_br_mem", "w_out")
BIG = {"w_in": "win", "w_mem_kv": "mk", "w_br_rg": "br0", "w_br_swa": "br1", "w_br_mem": "br2", "w_out": "out"}


def kernel(x, mem, pre_norm_g, post_norm_g, mem_norm_g, w_in, conv_w, conv_b, w_rg_a, b_rg_a, w_rg_x, b_rg_x, lru_lambda, swa_sinks, rel_bias, w_mem_kv, w_br_rg, w_br_swa, w_br_mem, w_out, loss_target, m_pre_norm_g, m_post_norm_g, m_mem_norm_g, m_w_in, m_conv_w, m_conv_b, m_w_rg_a, m_b_rg_a, m_w_rg_x, m_b_rg_x, m_lru_lambda, m_swa_sinks, m_rel_bias, m_w_mem_kv, m_w_br_rg, m_w_br_swa, m_w_br_mem, m_w_out, v_pre_norm_g, v_post_norm_g, v_mem_norm_g, v_w_in, v_conv_w, v_conv_b, v_w_rg_a, v_b_rg_a, v_w_rg_x, v_b_rg_x, v_lru_lambda, v_swa_sinks, v_rel_bias, v_w_mem_kv, v_w_br_rg, v_w_br_swa, v_w_br_mem, v_w_out):
    args = dict(locals())
    out_shapes = {n: args[n].shape for n in TWIN_WEIGHTS}
    w = {n: (args[n] if n == "rel_bias" else args[n][0]) for n in TWIN_WEIGHTS}
    m = {n: (args["m_" + n] if n == "rel_bias" else args["m_" + n][0]) for n in TWIN_WEIGHTS}
    v = {n: (args["v_" + n] if n == "rel_bias" else args["v_" + n][0]) for n in TWIN_WEIGHTS}
    for d in (w, m, v):
        for n, shape in SMALL:
            if n != "conv_w":
                d[n] = d[n].reshape(shape)

    xi, yi, ci = lax.axis_index("x"), lax.axis_index("y"), lax.axis_index("c")
    chip = 2 * xi + yi
    c_arr = ci.astype(jnp.int32).reshape(1)
    zero = jnp.zeros((), jnp.int32)
    cw0 = (chip * (D_RNN // N_CHIPS)).astype(jnp.int32)

    placed = lax.dynamic_update_slice(jnp.zeros((CONV_W, D_RNN), F32), w["conv_w"], (zero, cw0))
    placed = jnp.where(ci == 0, placed, 0.0).reshape(CONV_W * D_RNN // LANE, LANE)
    conv_w_full = _all_reduce_small(placed, "gather_conv_w").reshape(CONV_W, D_RNN)

    for d in (w, m, v):
        d["w_in"] = d["w_in"].T
    chip_row = lambda tbl: lax.dynamic_slice(jnp.asarray(tbl), (chip.astype(jnp.int32), zero), (1, tbl.shape[1]))[0]
    placed = {g: _place_group(w["w_in"], g, chip_row(_own_block_table(g))) for g in GROUPS}
    placed.update({s: _place_shard(w[n], s) for n, s in BIG.items() if n != "w_in"})
    ag_send, ag_recv, in_flight, _ = _gather_start(placed)

    def fetch(names, after):
        landed = _gather_wait(ag_send, ag_recv, {n: in_flight[n] for n in names}, after)
        ready = _gather_swap(landed)
        return tuple(ready[n] for n in names)

    rs = {"slots": {}, "halves": {}, "pending": []}

    def emit(grads):
        received = _swap_halves(grads)
        halves = {n: _add_half(grads[n], received[n], c_arr, n) for n in grads}
        landing = {s: rs["slots"][s] if s in rs["slots"] else lax.empty(_slot_shape(s), BF16)
                   for s in _stage_shards(tuple(grads))}
        send, recv, halves, landing, token = _scatter_start(halves, landing)
        rs["slots"].update(landing)
        rs["pending"].append((send, recv, halves, tuple(landing)))
        return token

    sp = {n: w[n] for n, _ in SMALL}
    sp["conv_w"] = conv_w_full
    sq, grad_x, d_small = _local_step(x[0], mem[0], loss_target[0], sp, fetch, emit)
    loss = lax.psum(sq[0, 0] * (0.5 / D), ("x", "y", "c"))

    small_total = _all_reduce_small(_pack(d_small), "all_reduce_small")

    for send, recv, halves, touched in rs["pending"]:
        halves, landed = _scatter_wait(send, recv, halves, {s: rs["slots"][s] for s in touched}, small_total)
        rs["slots"].update(landed)
        rs["halves"].update(halves)
    pos_arr = jnp.stack([ci, chip]).astype(jnp.int32)
    grp_tbl, blk_tbl = (chip_row(t) for t in _own_partial_tables())
    sums = {s: _sum_slots(rs["slots"][s], rs["halves"][s], pos_arr, s) for s in SHARDS if s != "win"}
    sums["win"] = _sum_slots_win(rs["slots"]["win"], rs["halves"], pos_arr, grp_tbl, blk_tbl)
    sums = _share_sums(sums)
    g_big = {n: sums[s] for n, s in BIG.items()}

    g_small = _unpack(small_total)
    g_small["conv_w"] = lax.dynamic_slice(g_small["conv_w"], (zero, cw0), (CONV_W, D_RNN // N_CHIPS))

    grad, delta, new_m, new_v = {}, {}, {}, {}
    for n, s in BIG.items():
        grad[n] = g_big[n]
        delta[n], new_m[n], new_v[n] = _adamw(w[n], g_big[n], m[n], v[n], s, 224 if n == "w_in" else 128)
    for group in (grad, delta, new_m, new_v):
        group["w_in"] = group["w_in"].T
    d_, m_, v_ = _adamw(_pack(w), _pack(g_small), _pack(m), _pack(v), "small", PACK_ROWS)
    shard_shapes = {"conv_w": (CONV_W, D_RNN // N_CHIPS)}
    d_, m_, v_ = (_unpack(a, shard_shapes) for a in (d_, m_, v_))
    for n, _ in SMALL:
        grad[n], delta[n], new_m[n], new_v[n] = g_small[n], d_[n], m_[n], v_[n]

    outs = [loss, grad_x.reshape(1, S, D)]
    for group in (grad, delta, new_m, new_v):
        outs += [group[n].reshape(out_shapes[n]) for n in TWIN_WEIGHTS]
    return tuple(outs)
```

```python
import functools
import math
from typing import NamedTuple

import jax
import jax.numpy as jnp
from jax import lax
from jax.experimental import pallas as pl
from jax.experimental.pallas import tpu as pltpu

F32 = jnp.float32
BF16 = jnp.bfloat16
MESH = pl.DeviceIdType.MESH

S = 2048
D = 2048
MEM = 256
D_RNN = 1024
RNN_BLOCKS = 8
CONV_W = 4
LRU_C = 8.0
SWA_HEADS = 16
SWA_HD = 64
WINDOW = 128
MEM_HEADS = 4
MEM_HD = 256
REL_BUCKETS = 32
REL_MAX_DIST = 128
EPS = 1e-6
NEG_INF = -1e30
LANE = 128
SHARD = 3136
HALF_TILE = 64
N_CHIPS = 4
VMEM_LIMIT = 56 * 1024 * 1024

ADAM_LR = 0.001
ADAM_B1 = 0.9
ADAM_B2 = 0.999
ADAM_EPS = 1e-08
ADAM_WD = 0.01
ADAM_STEP = 10

GROUP_TILES = {"A": 16, "B": 18, "C": 16, "D": 48}
GROUPS = ("A", "B", "C", "D")


def _params(sem=None):
    return pltpu.CompilerParams(dimension_semantics=sem, vmem_limit_bytes=VMEM_LIMIT)


def _sigmoid(v):
    return jax.nn.sigmoid(v)


def _tile_home(t):
    if t < 16:
        return "A", t
    if t < 24:
        return "B", t - 16
    if t < 26:
        return "B", t - 24 + 16
    if t < 34:
        return "B", t - 26 + 8
    if t < 50:
        return "C", t - 34
    return "D", t - 50


def _shard_runs(j):
    runs = []
    per_shard = SHARD // HALF_TILE
    for q in range(per_shard * j, per_shard * (j + 1)):
        g, gt = _tile_home(q // 2)
        row = gt * LANE + (q % 2) * HALF_TILE
        if runs and runs[-1][2] == g and runs[-1][3] + runs[-1][1] == row:
            runs[-1][1] += HALF_TILE
        else:
            runs.append([(q - per_shard * j) * HALF_TILE, HALF_TILE, g, row])
    return [tuple(r) for r in runs]


_DIMS = {
    "nn": (((1,), (0,)), ((), ())),
    "nt": (((1,), (1,)), ((), ())),
    "tn": (((0,), (0,)), ((), ())),
}


def _mm(a, b, mode, out_dtype, tm, tn, tk, name, acc=None, after=None):
    if mode == "nn":
        (m, k), n = a.shape, b.shape[1]
    elif mode == "nt":
        (m, k), n = a.shape, b.shape[0]
    else:
        (k, m), n = a.shape, b.shape[1]
    tm, tn, tk = min(tm, m), min(tn, n), min(tk, k)
    assert m % tm == 0 and n % tn == 0 and k % tk == 0, (name, m, n, k)
    nk = k // tk
    has_acc = acc is not None

    def body(*refs):
        a_ref, b_ref = refs[0], refs[1]
        o_ref = refs[3] if has_acc else refs[2]
        p = lax.dot_general(a_ref[...], b_ref[...], _DIMS[mode], preferred_element_type=F32)

        def finish(v):
            if has_acc:
                v = v + refs[2][...]
            o_ref[...] = v.astype(out_dtype)

        if nk == 1:
            finish(p)
        else:
            s_ref = refs[-1]
            kk = pl.program_id(2)

            @pl.when(kk == 0)
            def _():
                s_ref[...] = p

            @pl.when(kk > 0)
            def _():
                s_ref[...] += p

            @pl.when(kk == nk - 1)
            def _():
                finish(s_ref[...])

    if mode == "nn":
        a_spec = pl.BlockSpec((tm, tk), lambda i, j, kk: (i, kk))
        b_spec = pl.BlockSpec((tk, tn), lambda i, j, kk: (kk, j))
    elif mode == "nt":
        a_spec = pl.BlockSpec((tm, tk), lambda i, j, kk: (i, kk))
        b_spec = pl.BlockSpec((tn, tk), lambda i, j, kk: (j, kk))
    else:
        a_spec = pl.BlockSpec((tk, tm), lambda i, j, kk: (kk, i))
        b_spec = pl.BlockSpec((tk, tn), lambda i, j, kk: (kk, j))
    o_spec = pl.BlockSpec((tm, tn), lambda i, j, kk: (i, j))
    in_specs = [a_spec, b_spec] + ([o_spec] if has_acc else [])
    args = (a, b) + ((acc,) if has_acc else ())
    if after is not None:
        in_specs.append(pl.BlockSpec(memory_space=pl.ANY))
        args += (after,)
    n_in = len(args)
    kernel_body = body

    def body(*refs):
        kernel_body(*(refs[:n_in - (after is not None)] + refs[n_in:]))

    return pl.pallas_call(
        body,
        name=name,
        grid=(m // tm, n // tn, nk),
        in_specs=in_specs,
        out_specs=o_spec,
        out_shape=jax.ShapeDtypeStruct((m, n), out_dtype),
        scratch_shapes=[pltpu.VMEM((tm, tn), F32)] if nk > 1 else [],
        compiler_params=_params(("parallel", "parallel", "arbitrary")),
    )(*args)


def _rms_fwd(x, g, name, ts=256):
    r, d = x.shape

    def body(x_ref, g_ref, o_ref):
        xv = x_ref[...]
        inv = lax.rsqrt(jnp.mean(xv * xv, axis=-1, keepdims=True) + EPS)
        o_ref[...] = (xv * inv * g_ref[...]).astype(BF16)

    return pl.pallas_call(
        body,
        name=name,
        grid=(r // ts,),
        in_specs=[pl.BlockSpec((ts, d), lambda i: (i, 0)), pl.BlockSpec((1, d), lambda i: (0, 0))],
        out_specs=pl.BlockSpec((ts, d), lambda i: (i, 0)),
        out_shape=jax.ShapeDtypeStruct((r, d), BF16),
        compiler_params=_params(("parallel",)),
    )(x, g)


def _post_loss(out, x, tgt, g_post, ts=256):
    n = S // ts

    def body(o_ref, x_ref, t_ref, g_ref, sq_ref, dy_ref, do_ref, dg_ref):
        i = pl.program_id(0)

        @pl.when(i == 0)
        def _():
            sq_ref[...] = jnp.zeros_like(sq_ref)
            dg_ref[...] = jnp.zeros_like(dg_ref)

        ov = o_ref[...]
        g = g_ref[...]
        inv = lax.rsqrt(jnp.mean(ov * ov, axis=-1, keepdims=True) + EPS)
        on = ov * inv
        err = x_ref[...] + on * g - t_ref[...]
        sq_ref[...] += jnp.sum(err * err)
        dy = err * (1.0 / D)
        dy_ref[...] = dy
        dg_ref[...] += jnp.sum(dy * on, axis=0, keepdims=True)
        don = dy * g
        do_ref[...] = (inv * (don - on * jnp.mean(don * on, axis=-1, keepdims=True))).astype(BF16)

    row = pl.BlockSpec((ts, D), lambda i: (i, 0))
    vec = pl.BlockSpec((1, D), lambda i: (0, 0))
    return pl.pallas_call(
        body,
        name="post_loss",
        grid=(n,),
        in_specs=[row, row, row, vec],
        out_specs=[pl.BlockSpec((8, LANE), lambda i: (0, 0)), row, row, vec],
        out_shape=[
            jax.ShapeDtypeStruct((8, LANE), F32),
            jax.ShapeDtypeStruct((S, D), F32),
            jax.ShapeDtypeStruct((S, D), BF16),
            jax.ShapeDtypeStruct((1, D), F32),
        ],
        compiler_params=_params(("arbitrary",)),
    )(out, x, tgt, g_post)


def _pre_bwd(dh, x, dy, g_pre, ts=256):
    n = S // ts

    def body(dh_ref, x_ref, dy_ref, g_ref, gx_ref, dg_ref):
        i = pl.program_id(0)

        @pl.when(i == 0)
        def _():
            dg_ref[...] = jnp.zeros_like(dg_ref)

        xv = x_ref[...]
        dhv = dh_ref[...]
        inv = lax.rsqrt(jnp.mean(xv * xv, axis=-1, keepdims=True) + EPS)
        xn = xv * inv
        dg_ref[...] += jnp.sum(dhv * xn, axis=0, keepdims=True)
        dxn = dhv * g_ref[...]
        gx_ref[...] = dy_ref[...] + inv * (dxn - xn * jnp.mean(dxn * xn, axis=-1, keepdims=True))

    row = pl.BlockSpec((ts, D), lambda i: (i, 0))
    vec = pl.BlockSpec((1, D), lambda i: (0, 0))
    return pl.pallas_call(
        body,
        name="pre_bwd",
        grid=(n,),
        in_specs=[row, row, row, vec],
        out_specs=[row, vec],
        out_shape=[jax.ShapeDtypeStruct((S, D), F32), jax.ShapeDtypeStruct((1, D), F32)],
        compiler_params=_params(("arbitrary",)),
    )(dh, x, dy, g_pre)


def _memnorm_bwd(dmemn, mem):
    def body(d_ref, m_ref, dg_ref):
        mv = m_ref[...]
        inv = lax.rsqrt(jnp.mean(mv * mv, axis=-1, keepdims=True) + EPS)
        dg_ref[...] = jnp.sum(d_ref[...] * mv * inv, axis=0, keepdims=True)

    return pl.pallas_call(
        body,
        name="memnorm_bwd",
        out_shape=jax.ShapeDtypeStruct((1, D), F32),
        compiler_params=_params(),
    )(dmemn, mem)


T_RNN = 256


def _neg_expm1(z):
    poly = -z * (1.0 + z * (0.5 + z * (1.0 / 6 + z * (1.0 / 24 + z * (1.0 / 120 + z * (1.0 / 720))))))
    return jnp.where(z > -0.1, poly, 1.0 - jnp.exp(z))


def _softplus_neg(lam):
    return jnp.maximum(-lam, 0.0) + jnp.log1p(jnp.exp(-jnp.abs(lam)))


def _rnn_gates(conv, wa_ref, ba, wx_ref, bx, lam, first_row):
    cbf = conv.astype(BF16)
    ga, gx = [], []
    for n in range(RNN_BLOCKS):
        c_n = cbf[:, n * LANE:(n + 1) * LANE]
        ga.append(jnp.dot(c_n, wa_ref[n], preferred_element_type=F32))
        gx.append(jnp.dot(c_n, wx_ref[n], preferred_element_type=F32))
    gate_r = _sigmoid(jnp.concatenate(ga, axis=1) + ba)
    gate_i = _sigmoid(jnp.concatenate(gx, axis=1) + bx)
    sp = _softplus_neg(lam)
    log_a = -LRU_C * gate_r * sp
    a = jnp.exp(log_a)
    mult_raw = jnp.sqrt(_neg_expm1(2.0 * log_a))
    mult = jnp.where(first_row, 1.0, mult_raw)
    return cbf, gate_r, gate_i, sp, a, mult_raw, mult


def _rglru_fwd(p_a, conv_w, conv_b, wa, ba, wx, bx, lam):
    t = T_RNN
    n = S // t

    def body(xr_ref, g_ref, cw_ref, cb_ref, wa_ref, ba_ref, wx_ref, bx_ref, lam_ref,
             y_ref, h_ref, xp_s, hcar, a_s, b_s):
        i = pl.program_id(0)

        @pl.when(i == 0)
        def _():
            xp_s[0:8, :] = jnp.zeros((8, D_RNN), F32)
            hcar[...] = jnp.zeros_like(hcar)

        @pl.when(i > 0)
        def _():
            xp_s[0:8, :] = xp_s[t:t + 8, :]

        xp_s[8:8 + t, :] = xr_ref[...]
        conv = cb_ref[...]
        for k in range(CONV_W):
            conv = conv + cw_ref[k:k + 1, :] * xp_s[8 - k:8 - k + t, :]
        rows = i * t + lax.broadcasted_iota(jnp.int32, (t, 1), 0)
        _, _, gate_i, _, a, _, mult = _rnn_gates(
            conv, wa_ref, ba_ref[...], wx_ref, bx_ref[...], lam_ref[...], rows == 0)
        a_s[...] = a
        b_s[...] = mult * gate_i * conv

        def step(tt, h):
            h = a_s[pl.ds(tt, 1), :] * h + b_s[pl.ds(tt, 1), :]
            h_ref[pl.ds(tt, 1), :] = h
            return h

        hcar[...] = lax.fori_loop(0, t, step, hcar[...], unroll=8)
        g = g_ref[...]
        y_ref[...] = (h_ref[...] * (g * _sigmoid(g))).astype(BF16)

    blk = lambda c: pl.BlockSpec((t, D_RNN), lambda i: (i, c))
    full = lambda shape: pl.BlockSpec(shape, lambda i: (0,) * len(shape))
    return pl.pallas_call(
        body,
        name="rglru_fwd",
        grid=(n,),
        in_specs=[blk(0), blk(1), full((CONV_W, D_RNN)), full((1, D_RNN)),
                  full((RNN_BLOCKS, LANE, LANE)), full((1, D_RNN)),
                  full((RNN_BLOCKS, LANE, LANE)), full((1, D_RNN)), full((1, D_RNN))],
        out_specs=[blk(0), blk(0)],
        out_shape=[jax.ShapeDtypeStruct((S, D_RNN), BF16), jax.ShapeDtypeStruct((S, D_RNN), F32)],
        scratch_shapes=[pltpu.VMEM((t + 8, D_RNN), F32), pltpu.VMEM((1, D_RNN), F32),
                        pltpu.VMEM((t, D_RNN), F32), pltpu.VMEM((t, D_RNN), F32)],
        compiler_params=_params(("arbitrary",)),
    )(p_a, p_a, conv_w, conv_b, wa, ba, wx, bx, lam)


def _rglru_bwd(dy, p_a, hseq, conv_w, conv_b, wa, ba, wx, bx, lam):
    t = T_RNN
    n = S // t
    rb = t // 8

    def body(dy_ref, xr_ref, g_ref, h_ref, xrp_ref, hp_ref, cw_ref, cb_ref, wa_ref, ba_ref, wx_ref, bx_ref, lam_ref,
             dp_ref, dcw_ref, dcb_ref, dwa_ref, dba_ref, dwx_ref, dbx_ref, dlam_ref,
             xp_s, hp_s, dxp_s, lamcar, a_s, dh_s, lam_s):
        i = pl.program_id(0)
        r = n - 1 - i

        @pl.when(i == 0)
        def _():
            for ref in (dcw_ref, dcb_ref, dwa_ref, dba_ref, dwx_ref, dbx_ref, dlam_ref, lamcar):
                ref[...] = jnp.zeros_like(ref)
            dxp_s[t:t + 8, :] = jnp.zeros((8, D_RNN), F32)

        @pl.when(i > 0)
        def _():
            dxp_s[t:t + 8, :] = dxp_s[0:8, :]

        has_prev = r > 0
        xp_s[0:8, :] = jnp.where(has_prev, xrp_ref[...], 0.0)
        xp_s[8:8 + t, :] = xr_ref[...]
        hp_s[0:8, :] = jnp.where(has_prev, hp_ref[...], 0.0)
        hp_s[8:8 + t, :] = h_ref[...]
        xs = [xp_s[8 - k:8 - k + t, :] for k in range(CONV_W)]
        conv = cb_ref[...]
        for k in range(CONV_W):
            conv = conv + cw_ref[k:k + 1, :] * xs[k]
        rows = r * t + lax.broadcasted_iota(jnp.int32, (t, 1), 0)
        first = rows == 0
        lam_p = lam_ref[...]
        cbf, gate_r, gate_i, sp, a, mult_raw, mult = _rnn_gates(
            conv, wa_ref, ba_ref[...], wx_ref, bx_ref[...], lam_p, first)

        g = g_ref[...]
        sg = _sigmoid(g)
        dyv = dy_ref[...]
        a_s[...] = a
        dh_s[...] = dyv * (g * sg)
        dg = dyv * h_ref[...] * (sg * (1.0 + g * (1.0 - sg)))

        def step(jj, car):
            tt = t - 1 - jj
            lm = dh_s[pl.ds(tt, 1), :] + car
            lam_s[pl.ds(tt, 1), :] = lm
            return a_s[pl.ds(tt, 1), :] * lm

        lamcar[...] = lax.fori_loop(0, t, step, lamcar[...], unroll=8)
        db = lam_s[...]
        da = db * hp_s[7:7 + t, :]
        dmult = db * gate_i * conv
        dgate_i = db * mult * conv
        dconv = db * mult * gate_i
        dlog_a = da * a + jnp.where(first, 0.0, dmult * (-(a * a) / mult_raw))
        dgate_r = dlog_a * (-LRU_C * sp)
        dsp = jnp.sum(dlog_a * (-LRU_C * gate_r), axis=0, keepdims=True)
        dlam_ref[...] += dsp * (-_sigmoid(-lam_p))
        dga = dgate_r * gate_r * (1.0 - gate_r)
        dgx = dgate_i * gate_i * (1.0 - gate_i)
        dba_ref[...] += jnp.sum(dga, axis=0, keepdims=True)
        dbx_ref[...] += jnp.sum(dgx, axis=0, keepdims=True)
        dga16, dgx16 = dga.astype(BF16), dgx.astype(BF16)
        back = []
        for nb in range(RNN_BLOCKS):
            sl = slice(nb * LANE, (nb + 1) * LANE)
            dwa_ref[nb] += lax.dot_general(cbf[:, sl], dga16[:, sl], _DIMS["tn"], preferred_element_type=F32)
            dwx_ref[nb] += lax.dot_general(cbf[:, sl], dgx16[:, sl], _DIMS["tn"], preferred_element_type=F32)
            back.append(lax.dot_general(dga16[:, sl], wa_ref[nb], _DIMS["nt"], preferred_element_type=F32)
                        + lax.dot_general(dgx16[:, sl], wx_ref[nb], _DIMS["nt"], preferred_element_type=F32))
        dconv = dconv + jnp.concatenate(back, axis=1)
        dcb_ref[...] += jnp.sum(dconv, axis=0, keepdims=True)
        for k in range(CONV_W):
            dcw_ref[k:k + 1, :] += jnp.sum(dconv * xs[k], axis=0, keepdims=True)
        dxp_s[0:t, :] = dconv
        dxr = cw_ref[0:1, :] * dconv
        for k in range(1, CONV_W):
            dxr = dxr + cw_ref[k:k + 1, :] * dxp_s[k:k + t, :]
        dp_ref[:, 0:D_RNN] = dxr.astype(BF16)
        dp_ref[:, D_RNN:2 * D_RNN] = dg.astype(BF16)

    blk = lambda c: pl.BlockSpec((t, D_RNN), lambda i: (n - 1 - i, c))
    prev8 = pl.BlockSpec((8, D_RNN), lambda i: (jnp.maximum((n - 1 - i) * rb - 1, 0), 0))
    full = lambda shape: pl.BlockSpec(shape, lambda i: (0,) * len(shape))
    vec = full((1, D_RNN))
    mat = full((RNN_BLOCKS, LANE, LANE))
    return pl.pallas_call(
        body,
        name="rglru_bwd",
        grid=(n,),
        in_specs=[blk(0), blk(0), blk(1), blk(0), prev8, prev8,
                  full((CONV_W, D_RNN)), vec, mat, vec, mat, vec, vec],
        out_specs=[pl.BlockSpec((t, 2 * D_RNN), lambda i: (n - 1 - i, 0)),
                   full((CONV_W, D_RNN)), vec, mat, vec, mat, vec, vec],
        out_shape=[jax.ShapeDtypeStruct((S, 2 * D_RNN), BF16),
                   jax.ShapeDtypeStruct((CONV_W, D_RNN), F32), jax.ShapeDtypeStruct((1, D_RNN), F32),
                   jax.ShapeDtypeStruct((RNN_BLOCKS, LANE, LANE), F32), jax.ShapeDtypeStruct((1, D_RNN), F32),
                   jax.ShapeDtypeStruct((RNN_BLOCKS, LANE, LANE), F32), jax.ShapeDtypeStruct((1, D_RNN), F32),
                   jax.ShapeDtypeStruct((1, D_RNN), F32)],
        scratch_shapes=[pltpu.VMEM((t + 8, D_RNN), F32), pltpu.VMEM((t + 8, D_RNN), F32),
                        pltpu.VMEM((t + 8, D_RNN), F32), pltpu.VMEM((1, D_RNN), F32),
                        pltpu.VMEM((t, D_RNN), F32), pltpu.VMEM((t, D_RNN), F32), pltpu.VMEM((t, D_RNN), F32)],
        compiler_params=_params(("arbitrary",)),
    )(dy, p_a, p_a, hseq, p_a, hseq, conv_w, conv_b, wa, ba, wx, bx, lam)


QB = WINDOW
KB2 = 2 * WINDOW
N_QB = S // QB
N_PAIR = SWA_HEADS // 2


def _swa_keys(kvc_ref, kvp_ref):
    kk = jnp.concatenate([kvp_ref[:, 0:LANE], kvc_ref[:, 0:LANE]], axis=0)
    vv = jnp.concatenate([kvp_ref[:, LANE:2 * LANE], kvc_ref[:, LANE:2 * LANE]], axis=0)
    lo = lax.broadcasted_iota(jnp.int32, (1, LANE), 1) < SWA_HD
    kk_sw, vv_sw = pltpu.roll(kk, SWA_HD, 1), pltpu.roll(vv, SWA_HD, 1)
    kd = [jnp.where(lo, kk, kk_sw).astype(BF16), jnp.where(lo, kk_sw, kk).astype(BF16)]
    vd = [jnp.where(lo, vv, vv_sw).astype(BF16), jnp.where(lo, vv_sw, vv).astype(BF16)]
    return lo, kd, vd


def _swa_valid(n):
    qi = lax.broadcasted_iota(jnp.int32, (QB, KB2), 0)
    kj = lax.broadcasted_iota(jnp.int32, (QB, KB2), 1)
    dist = qi + WINDOW - kj
    return (dist >= 0) & (dist < WINDOW) & ((n > 0) | (kj >= WINDOW))


def _swa_probs(qh16, kd, bias, sink, valid):
    lg = lax.dot_general(qh16, kd, _DIMS["nt"], preferred_element_type=F32) * (SWA_HD ** -0.5) + bias
    lg = jnp.where(valid, lg, NEG_INF)
    m = jnp.maximum(jnp.max(lg, axis=-1, keepdims=True), sink)
    p = jnp.exp(lg - m)
    es = jnp.exp(sink - m)
    den = jnp.sum(p, axis=-1, keepdims=True) + es
    return p / den, es / den


def _swa_specs():
    q = pl.BlockSpec((QB, D_RNN), lambda n: (n, 0))
    g = pl.BlockSpec((QB, D_RNN), lambda n: (n, 1))
    kvc = pl.BlockSpec((QB, 2 * LANE), lambda n: (n, 8))
    kvp = pl.BlockSpec((QB, 2 * LANE), lambda n: (jnp.maximum(n - 1, 0), 8))
    bias = pl.BlockSpec((SWA_HEADS, QB, KB2), lambda n: (0, 0, 0))
    sinks = pl.BlockSpec(memory_space=pltpu.SMEM)
    return q, g, kvc, kvp, bias, sinks


def _swa_fwd(p_b, bias_t, sinks):
    def body(q_ref, g_ref, kvc_ref, kvp_ref, bias_ref, sink_ref, y_ref, o_ref):
        n = pl.program_id(0)
        lo, kd, vd = _swa_keys(kvc_ref, kvp_ref)
        valid = _swa_valid(n)
        for hp in range(N_PAIR):
            sl = slice(hp * LANE, (hp + 1) * LANE)
            kvh = hp // (N_PAIR // 2)
            q = q_ref[:, sl]
            outs = []
            for j in range(2):
                mh = lo if j == 0 else jnp.logical_not(lo)
                qh16 = jnp.where(mh, q, 0.0).astype(BF16)
                probs, _ = _swa_probs(qh16, kd[kvh], bias_ref[2 * hp + j], sink_ref[2 * hp + j], valid)
                outs.append(jnp.dot(probs.astype(BF16), vd[kvh], preferred_element_type=F32))
            o = jnp.where(lo, outs[0], outs[1])
            o_ref[:, sl] = o
            g = g_ref[:, sl]
            y_ref[:, sl] = (o * (g * _sigmoid(g))).astype(BF16)

    q, g, kvc, kvp, bias, sinks_spec = _swa_specs()
    out = pl.BlockSpec((QB, D_RNN), lambda n: (n, 0))
    return pl.pallas_call(
        body,
        name="swa_fwd",
        grid=(N_QB,),
        in_specs=[q, g, kvc, kvp, bias, sinks_spec],
        out_specs=[out, out],
        out_shape=[jax.ShapeDtypeStruct((S, D_RNN), BF16), jax.ShapeDtypeStruct((S, D_RNN), F32)],
        compiler_params=_params(("parallel",)),
    )(p_b, p_b, p_b, p_b, bias_t, sinks)


def _swa_bwd(dy, p_b, o_swa, bias_t, sinks):
    def body(dy_ref, q_ref, g_ref, kvc_ref, kvp_ref, o_ref, bias_ref, sink_ref,
             dp_ref, dk_ref, dv_ref, dbias_ref, dsink_ref):
        n = pl.program_id(0)

        @pl.when(n == 0)
        def _():
            for ref in (dk_ref, dv_ref, dbias_ref, dsink_ref):
                ref[...] = jnp.zeros_like(ref)

        lo, kd, vd = _swa_keys(kvc_ref, kvp_ref)
        hi = jnp.logical_not(lo)
        valid = _swa_valid(n)
        dk_blk = jnp.zeros((KB2, LANE), F32)
        dv_blk = jnp.zeros((KB2, LANE), F32)
        for kvh in range(2):
            dk_pair = jnp.zeros((KB2, LANE), F32)
            dv_pair = jnp.zeros((KB2, LANE), F32)
            for hp in range(kvh * (N_PAIR // 2), (kvh + 1) * (N_PAIR // 2)):
                sl = slice(hp * LANE, (hp + 1) * LANE)
                q = q_ref[:, sl]
                g = g_ref[:, sl]
                o = o_ref[:, sl]
                dyv = dy_ref[:, sl]
                sg = _sigmoid(g)
                do = dyv * (g * sg)
                dp_ref[:, D_RNN + hp * LANE:D_RNN + (hp + 1) * LANE] = (
                    dyv * o * (sg * (1.0 + g * (1.0 - sg)))).astype(BF16)
                dqs = []
                for j in range(2):
                    h = 2 * hp + j
                    mh = lo if j == 0 else hi
                    qh16 = jnp.where(mh, q, 0.0).astype(BF16)
                    sink = sink_ref[h]
                    probs, psink = _swa_probs(qh16, kd[kvh], bias_ref[h], sink, valid)
                    doh = jnp.where(mh, do, 0.0)
                    doh16 = doh.astype(BF16)
                    delta = jnp.sum(doh * o, axis=-1, keepdims=True)
                    dpr = lax.dot_general(doh16, vd[kvh], _DIMS["nt"], preferred_element_type=F32)
                    ds = probs * (dpr - delta)
                    dbias_ref[h] += ds
                    dsink_ref[h:h + 1, :] += jnp.zeros((1, LANE), F32) - jnp.sum(psink * delta)
                    ds16 = (ds * (SWA_HD ** -0.5)).astype(BF16)
                    dqs.append(jnp.dot(ds16, kd[kvh], preferred_element_type=F32))
                    dk_pair = dk_pair + lax.dot_general(ds16, qh16, _DIMS["tn"], preferred_element_type=F32)
                    dv_pair = dv_pair + lax.dot_general(probs.astype(BF16), doh16, _DIMS["tn"],
                                                        preferred_element_type=F32)
                dp_ref[:, sl] = jnp.where(lo, dqs[0], dqs[1]).astype(BF16)
            keep = lo if kvh == 0 else hi
            dk_blk = dk_blk + jnp.where(keep, dk_pair + pltpu.roll(dk_pair, SWA_HD, 1), 0.0)
            dv_blk = dv_blk + jnp.where(keep, dv_pair + pltpu.roll(dv_pair, SWA_HD, 1), 0.0)

        cur = pl.ds(pl.multiple_of(n * QB, QB), QB)
        dk_ref[cur, :] += dk_blk[QB:KB2]
        dv_ref[cur, :] += dv_blk[QB:KB2]

        @pl.when(n > 0)
        def _():
            prev = pl.ds(pl.multiple_of((n - 1) * QB, QB), QB)
            dk_ref[prev, :] += dk_blk[0:QB]
            dv_ref[prev, :] += dv_blk[0:QB]

    q, g, kvc, kvp, bias, sinks_spec = _swa_specs()
    row = pl.BlockSpec((QB, D_RNN), lambda n: (n, 0))
    acc = pl.BlockSpec((S, LANE), lambda n: (0, 0))
    return pl.pallas_call(
        body,
        name="swa_bwd",
        grid=(N_QB,),
        in_specs=[row, q, g, kvc, kvp, row, bias, sinks_spec],
        out_specs=[pl.BlockSpec((QB, 2 * D_RNN), lambda n: (n, 0)), acc, acc, bias,
                   pl.BlockSpec((SWA_HEADS, LANE), lambda n: (0, 0))],
        out_shape=[jax.ShapeDtypeStruct((S, GROUP_TILES["B"] * LANE), BF16),
                   jax.ShapeDtypeStruct((S, LANE), F32), jax.ShapeDtypeStruct((S, LANE), F32),
                   jax.ShapeDtypeStruct((SWA_HEADS, QB, KB2), F32),
                   jax.ShapeDtypeStruct((SWA_HEADS, LANE), F32)],
        compiler_params=_params(("arbitrary",)),
    )(dy, p_b, p_b, p_b, p_b, o_swa, bias_t, sinks)


def _swa_pack(dp_b, dk, dv, ts=512):
    def body(_, dk_ref, dv_ref, o_ref):
        o_ref[:, 0:LANE] = dk_ref[...].astype(BF16)
        o_ref[:, LANE:2 * LANE] = dv_ref[...].astype(BF16)

    tile = pl.BlockSpec((ts, LANE), lambda i: (i, 0))
    return pl.pallas_call(
        body,
        name="swa_pack",
        grid=(S // ts,),
        in_specs=[pl.BlockSpec(memory_space=pl.ANY), tile, tile],
        out_specs=pl.BlockSpec((ts, 2 * LANE), lambda i: (i, 8)),
        out_shape=jax.ShapeDtypeStruct(dp_b.shape, dp_b.dtype),
        input_output_aliases={0: 0},
        compiler_params=_params(("parallel",)),
    )(dp_b, dk, dv)


def _split3(v):
    a = v.astype(BF16)
    r = v - a.astype(F32)
    b = r.astype(BF16)
    c = (r - b.astype(F32)).astype(BF16)
    return a, b, c


def _relbias_grad(dbias_flat, onehot_t):
    def body(d_ref, e_ref, o_ref):
        e = e_ref[...]
        acc = jnp.zeros((SWA_HEADS, REL_BUCKETS), F32)
        for term in _split3(d_ref[...]):
            acc = acc + lax.dot_general(term, e, _DIMS["nt"], preferred_element_type=F32)
        o_ref[...] = acc

    return pl.pallas_call(
        body,
        name="relbias_grad",
        out_shape=jax.ShapeDtypeStruct((SWA_HEADS, REL_BUCKETS), F32),
        compiler_params=_params(),
    )(dbias_flat, onehot_t)


TS_MEM = 512


def _mem_probs(q16, mk):
    lg = lax.dot_general(q16, mk, _DIMS["nt"], preferred_element_type=F32) * (MEM_HD ** -0.5)
    p = jnp.exp(lg - jnp.max(lg, axis=-1, keepdims=True))
    return p / jnp.sum(p, axis=-1, keepdims=True)


def _mem_fwd(p_c, mkv):
    def body(q_ref, g_ref, mkv_ref, y_ref, o_ref):
        for hm in range(MEM_HEADS):
            sl = slice(hm * MEM_HD, (hm + 1) * MEM_HD)
            probs = _mem_probs(q_ref[:, sl].astype(BF16), mkv_ref[:, sl])
            o = jnp.dot(probs.astype(BF16), mkv_ref[:, D_RNN + hm * MEM_HD:D_RNN + (hm + 1) * MEM_HD],
                        preferred_element_type=F32)
            o_ref[:, sl] = o
            g = g_ref[:, sl]
            y_ref[:, sl] = (o * (g * _sigmoid(g))).astype(BF16)

    blk = lambda c: pl.BlockSpec((TS_MEM, D_RNN), lambda i: (i, c))
    return pl.pallas_call(
        body,
        name="mem_fwd",
        grid=(S // TS_MEM,),
        in_specs=[blk(0), blk(1), pl.BlockSpec((MEM, 2 * D_RNN), lambda i: (0, 0))],
        out_specs=[blk(0), blk(0)],
        out_shape=[jax.ShapeDtypeStruct((S, D_RNN), BF16), jax.ShapeDtypeStruct((S, D_RNN), F32)],
        compiler_params=_params(("parallel",)),
    )(p_c, p_c, mkv)


def _mem_bwd(dy, p_c, o_mem, mkv):
    def body(dy_ref, q_ref, g_ref, o_ref, mkv_ref, dp_ref, dmkv_ref):
        @pl.when(pl.program_id(0) == 0)
        def _():
            dmkv_ref[...] = jnp.zeros_like(dmkv_ref)

        for hm in range(MEM_HEADS):
            sl = slice(hm * MEM_HD, (hm + 1) * MEM_HD)
            sv = slice(D_RNN + hm * MEM_HD, D_RNN + (hm + 1) * MEM_HD)
            q16 = q_ref[:, sl].astype(BF16)
            mk, mv = mkv_ref[:, sl], mkv_ref[:, sv]
            probs = _mem_probs(q16, mk)
            g, o, dyv = g_ref[:, sl], o_ref[:, sl], dy_ref[:, sl]
            sg = _sigmoid(g)
            do = dyv * (g * sg)
            dp_ref[:, sv] = (dyv * o * (sg * (1.0 + g * (1.0 - sg)))).astype(BF16)
            do16 = do.astype(BF16)
            delta = jnp.sum(do * o, axis=-1, keepdims=True)
            dpr = lax.dot_general(do16, mv, _DIMS["nt"], preferred_element_type=F32)
            ds16 = (probs * (dpr - delta) * (MEM_HD ** -0.5)).astype(BF16)
            dp_ref[:, sl] = jnp.dot(ds16, mk, preferred_element_type=F32).astype(BF16)
            dmkv_ref[:, sl] += lax.dot_general(ds16, q16, _DIMS["tn"], preferred_element_type=F32)
            dmkv_ref[:, sv] += lax.dot_general(probs.astype(BF16), do16, _DIMS["tn"], preferred_element_type=F32)

    blk = lambda c: pl.BlockSpec((TS_MEM, D_RNN), lambda i: (i, c))
    kv = pl.BlockSpec((MEM, 2 * D_RNN), lambda i: (0, 0))
    return pl.pallas_call(
        body,
        name="mem_bwd",
        grid=(S // TS_MEM,),
        in_specs=[blk(0), blk(0), blk(1), blk(0), kv],
        out_specs=[pl.BlockSpec((TS_MEM, 2 * D_RNN), lambda i: (i, 0)), kv],
        out_shape=[jax.ShapeDtypeStruct((S, 2 * D_RNN), BF16), jax.ShapeDtypeStruct((MEM, 2 * D_RNN), F32)],
        compiler_params=_params(("arbitrary",)),
    )(dy, p_c, p_c, o_mem, mkv)


TS_MRG = 512
TD_MRG = 512
N_DBLK = D // TD_MRG


def _merge_fwd(z, p_d):
    def body(z0, z1, z2, g0, g1, g2, o_ref):
        o_ref[...] = (_sigmoid(g0[...]) * z0[...] + _sigmoid(g1[...]) * z1[...]
                      + _sigmoid(g2[...]) * z2[...]).astype(BF16)

    blk = pl.BlockSpec((TS_MRG, TD_MRG), lambda i, d: (i, d))
    gate = lambda b: pl.BlockSpec((TS_MRG, TD_MRG), lambda i, d: (i, b * N_DBLK + d))
    return pl.pallas_call(
        body,
        name="merge_fwd",
        grid=(S // TS_MRG, N_DBLK),
        in_specs=[blk, blk, blk, gate(0), gate(1), gate(2)],
        out_specs=blk,
        out_shape=jax.ShapeDtypeStruct((S, D), BF16),
        compiler_params=_params(("parallel", "parallel")),
    )(z[0], z[1], z[2], p_d, p_d, p_d)


def _merge_bwd(dmerged, z_b, p_d, b, dp_d):
    def body(dm_ref, z_ref, g_ref, *refs):
        dz_ref, dg_ref = refs[-2], refs[-1]
        sg = _sigmoid(g_ref[...])
        dm = dm_ref[...]
        dz_ref[...] = (dm * sg).astype(BF16)
        dg_ref[...] = (dm * z_ref[...] * sg * (1.0 - sg)).astype(BF16)

    blk = pl.BlockSpec((TS_MRG, TD_MRG), lambda i, d: (i, d))
    gate = pl.BlockSpec((TS_MRG, TD_MRG), lambda i, d: (i, b * N_DBLK + d))
    in_specs = [blk, blk, gate]
    args = [dmerged, z_b, p_d]
    aliases = {}
    if dp_d is not None:
        in_specs.append(pl.BlockSpec(memory_space=pl.ANY))
        args.append(dp_d)
        aliases = {3: 1}
    return pl.pallas_call(
        body,
        name=f"merge_bwd{b}",
        grid=(S // TS_MRG, N_DBLK),
        in_specs=in_specs,
        out_specs=[blk, gate],
        out_shape=[jax.ShapeDtypeStruct((S, D), BF16),
                   jax.ShapeDtypeStruct((S, GROUP_TILES["D"] * LANE), BF16)],
        input_output_aliases=aliases,
        compiler_params=_params(("parallel", "parallel")),
    )(*args)


def _bucket_table():
    import numpy as np
    qi = np.arange(QB)[:, None]
    kj = np.arange(KB2)[None, :]
    n = np.maximum(qi + WINDOW - kj, 0)
    max_exact = REL_BUCKETS // 2
    ratio = np.log(np.maximum(n, 1).astype(np.float32) / max_exact) / np.float32(math.log(REL_MAX_DIST / max_exact))
    large = np.minimum(max_exact + (ratio * (REL_BUCKETS - max_exact)).astype(np.int32), REL_BUCKETS - 1)
    bucket = np.where(n < max_exact, n, large).reshape(1, QB * KB2)
    return (bucket == np.arange(REL_BUCKETS)[:, None]).astype(np.float32)


def _bias_expand(rel_bias_t, onehot_t):
    def body(r_ref, e_ref, o_ref):
        e = e_ref[...]
        acc = jnp.zeros((SWA_HEADS, QB * KB2), F32)
        for term in _split3(r_ref[...]):
            acc = acc + jnp.dot(term, e, preferred_element_type=F32)
        o_ref[...] = acc

    return pl.pallas_call(
        body,
        name="bias_expand",
        out_shape=jax.ShapeDtypeStruct((SWA_HEADS, QB * KB2), F32),
        compiler_params=_params(),
    )(rel_bias_t, onehot_t)


PROJ_TN = {"A": 1024, "B": 1152, "C": 1024, "D": 1536}


def _local_step(x, mem, tgt, sp, fetch, emit):
    onehot_t = jnp.asarray(_bucket_table(), BF16)
    bias_t = _bias_expand(sp["rel_bias"].T, onehot_t).reshape(SWA_HEADS, QB, KB2)
    sinks = sp["swa_sinks"].reshape(SWA_HEADS)
    wa16, wx16 = sp["w_rg_a"].astype(BF16), sp["w_rg_x"].astype(BF16)
    rnn = (sp["conv_w"], sp["conv_b"], wa16, sp["b_rg_a"], wx16, sp["b_rg_x"], sp["lru_lambda"])

    h = _rms_fwd(x, sp["pre_norm_g"], "rms_pre")
    memn = _rms_fwd(mem, sp["mem_norm_g"], "rms_mem")
    w_grp, p = {}, {}

    def project(g, after):
        (w_grp[g],) = fetch((g,), after)
        p[g] = _mm(h, w_grp[g], "nt", F32, 1024, PROJ_TN[g], D, f"proj_{g}")

    project("A", h)
    y_rg, hseq = _rglru_fwd(p["A"], *rnn)
    project("B", y_rg)
    y_swa, o_swa = _swa_fwd(p["B"], bias_t, sinks)
    project("C", y_swa)
    (wmk,) = fetch(("mk",), p["C"])
    mkv = _mm(memn, wmk, "nn", BF16, MEM, 1024, D, "mkv")
    y_mem, o_mem = _mem_fwd(p["C"], mkv)
    ys = (y_rg, y_swa, y_mem)
    wbr = fetch(("br0", "br1", "br2"), y_mem)
    z = [_mm(ys[b], wbr[b], "nn", F32, 1024, 1024, D_RNN, f"branch_out{b}") for b in range(3)]
    project("D", z[2])
    merged = _merge_fwd(z, p["D"])
    (wout,) = fetch(("out",), merged)
    out = _mm(merged, wout, "nn", F32, 1024, 1024, D, "out_proj")
    sq, dy, dout, d_post = _post_loss(out, x, tgt, sp["post_norm_g"])

    tok = emit({"out": _mm(merged, dout, "tn", BF16, 1024, 1024, S, "d_wout")})
    dmerged = _mm(dout, wout, "nt", F32, 1024, 1024, D, "d_merged", after=tok)
    dz, dp_d = [], None
    for b in range(3):
        dz_b, dp_d = _merge_bwd(dmerged, z[b], p["D"], b, dp_d)
        dz.append(dz_b)
    tok = emit({f"br{b}": _mm(ys[b], dz[b], "tn", BF16, 1024, 1024, S, f"d_wbr{b}") for b in range(3)})
    tok = emit({"D": _mm(dp_d, h, "tn", BF16, PROJ_TN["D"], 1024, S, "d_win_D", after=tok)})
    dy_mem = _mm(dz[2], wbr[2], "nt", F32, 1024, 1024, D, "d_branch2", after=tok)
    dp_c, dmkv = _mem_bwd(dy_mem, p["C"], o_mem, mkv)
    dmkv16 = dmkv.astype(BF16)
    tok = emit({"mk": _mm(memn, dmkv16, "tn", BF16, 1024, 1024, MEM, "d_wmk")})
    dmemn = _mm(dmkv16, wmk, "nt", F32, MEM, 1024, D, "d_memn", after=tok)
    d_memg = _memnorm_bwd(dmemn, mem)
    dys = [_mm(dz[b], wbr[b], "nt", F32, 1024, 1024, D, f"d_branch{b}", after=dmemn) for b in range(2)]
    dp_a, d_cw, d_cb, d_wa, d_ba, d_wx, d_bx, d_lam = _rglru_bwd(dys[0], p["A"], hseq, *rnn)
    dp_b, dk, dv, d_bias, d_sink = _swa_bwd(dys[1], p["B"], o_swa, bias_t, sinks)
    dp_b = _swa_pack(dp_b, dk, dv)
    d_rel = _relbias_grad(d_bias.reshape(SWA_HEADS, QB * KB2), onehot_t).T
    dp = {"A": dp_a, "B": dp_b, "C": dp_c, "D": dp_d}
    tok = emit({g: _mm(dp[g], h, "tn", BF16, PROJ_TN[g], 1024, S, f"d_win_{g}") for g in ("A", "B", "C")})
    dh = None
    for g in GROUPS:
        dh = _mm(dp[g], w_grp[g], "nn", F32, 1024, 1024, 2304 if g == "B" else 2048, f"d_h_{g}", acc=dh,
                 after=tok if g == "A" else None)
    grad_x, d_pre = _pre_bwd(dh, x, dy, sp["pre_norm_g"])

    d_small = {
        "pre_norm_g": d_pre, "post_norm_g": d_post, "mem_norm_g": d_memg, "conv_w": d_cw, "conv_b": d_cb,
        "w_rg_a": d_wa, "b_rg_a": d_ba, "w_rg_x": d_wx, "b_rg_x": d_bx, "lru_lambda": d_lam,
        "swa_sinks": d_sink[:, 0].reshape(1, SWA_HEADS), "rel_bias": d_rel,
    }
    return sq, grad_x, d_small


ANY = pl.BlockSpec(memory_space=pl.ANY)
SHARD_ROWS = D // N_CHIPS
GATHERED = {"A": (2048, D), "B": (2304, D), "C": (2048, D), "D": (6144, D), "mk": (D, D),
            "br0": (D_RNN, D), "br1": (D_RNN, D), "br2": (D_RNN, D), "out": (D, D)}
SHARD_SHAPES = {"win": (SHARD, D), "mk": (SHARD_ROWS, D), "br0": (D_RNN, SHARD_ROWS), "br1": (D_RNN, SHARD_ROWS),
                "br2": (D_RNN, SHARD_ROWS), "out": (SHARD_ROWS, D)}
SHARDS = tuple(SHARD_SHAPES)
HALF_AXIS = {"win": 1, "mk": 1, "br0": 0, "br1": 0, "br2": 0, "out": 1,
             "A": 1, "B": 1, "C": 1, "D": 1}


def _halved(shape, axis):
    return (shape[0] // 2, shape[1]) if axis == 0 else (shape[0], shape[1] // 2)


class Piece(NamedTuple):
    src: str
    dst: str
    rows: int
    sr0: int
    sc0: int
    dr0: int
    dc0: int
    ncols: int


def _pieces_of(jj):
    out = [Piece("win", g, n, r, 0, gr, 0, D) for r, n, g, gr in _shard_runs(jj)]
    out.append(Piece("mk", "mk", SHARD_ROWS, 0, 0, SHARD_ROWS * jj, 0, D))
    out += [Piece(f"br{b}", f"br{b}", D_RNN, 0, 0, 0, SHARD_ROWS * jj, SHARD_ROWS) for b in range(3)]
    out.append(Piece("out", "out", SHARD_ROWS, 0, 0, SHARD_ROWS * jj, 0, D))
    return out


def _half_rect(ref, p, side, which):
    r0, c0 = (p.sr0, p.sc0) if side == "src" else (p.dr0, p.dc0)
    if HALF_AXIS[p.src] == 1:
        return _rect(ref, r0, p.rows, c0 + which * (p.ncols // 2), p.ncols // 2)
    return _rect(ref, r0 + which * (p.rows // 2), p.rows // 2, c0, p.ncols)


def _rect_in_half(ref, p, side):
    r0, c0 = (p.sr0, p.sc0) if side == "src" else (p.dr0, p.dc0)
    if HALF_AXIS[p.src] == 1:
        return _rect(ref, r0, p.rows, 0, p.ncols // 2)
    return _rect(ref, 0, p.rows // 2, c0, p.ncols)


MAX_PIECES = max(len(_pieces_of(jj)) for jj in range(N_CHIPS))


def _rect(ref, r0, rows, c0, ncols):
    return ref.at[pl.ds(r0, rows), pl.ds(c0, ncols)]


def _position():
    x, y, c = lax.axis_index("x"), lax.axis_index("y"), lax.axis_index("c")
    return x, y, c, 2 * x + y


HBM = pl.BlockSpec(memory_space=pltpu.HBM)
SEM = pl.BlockSpec(memory_space=pltpu.SEMAPHORE)
EFFECT = pltpu.SideEffectType.DATAFLOW_SIDE_EFFECTING
N_SEM = MAX_PIECES * N_CHIPS
GATHER_STAGES = (("A",), ("B",), ("C",), ("mk",), ("br0", "br1", "br2"), ("D",), ("out",))


def _in_hbm(a):
    return pltpu.with_memory_space_constraint(a, pltpu.HBM)


def _stage_pieces(jj, stage):
    return [(i, p) for i, p in enumerate(_pieces_of(jj)) if p.dst in stage]


def _own_block_table(g):
    import numpy as np
    tbl = np.zeros((N_CHIPS, GATHERED[g][0] // HALF_TILE), np.int32)
    for jj in range(N_CHIPS):
        for r, n, grp, gr in _shard_runs(jj):
            if grp == g:
                for k in range(n // HALF_TILE):
                    tbl[jj, gr // HALF_TILE + k] = r // HALF_TILE + k
    return tbl


def _place_group(w_t, g, table):
    nb = GATHERED[g][0] // HALF_TILE

    def body(t_ref, x_ref, o_ref):
        o_ref[...] = x_ref[...].astype(BF16)

    return pl.pallas_call(
        body,
        name=f"place_{g}",
        grid_spec=pltpu.PrefetchScalarGridSpec(
            num_scalar_prefetch=1,
            grid=(nb,),
            in_specs=[pl.BlockSpec((HALF_TILE, D), lambda b, t: (t[b], 0))],
            out_specs=pl.BlockSpec((HALF_TILE, D), lambda b, t: (b, 0)),
        ),
        out_shape=jax.ShapeDtypeStruct(GATHERED[g], BF16),
        compiler_params=_params(("parallel",)),
    )(table, w_t)


def _place_shard(shard, name):
    rows, cols = shard.shape
    by_rows = HALF_AXIS[name] == 1

    def body(x_ref, o_ref):
        o_ref[...] = x_ref[...].astype(BF16)

    return pl.pallas_call(
        body,
        name=f"place_{name}",
        grid=(N_CHIPS,),
        in_specs=[pl.BlockSpec((rows, cols), lambda b: (0, 0))],
        out_specs=pl.BlockSpec((rows, cols), (lambda b: (b, 0)) if by_rows else (lambda b: (0, b))),
        out_shape=jax.ShapeDtypeStruct(GATHERED[name], BF16),
        compiler_params=_params(("parallel",)),
    )(shard)


def _gather_copy(arr, send_sems, recv_sems, c, jj, i, p, kk):
    rect = _half_rect(arr[p.dst], p, "dst", c)
    return pltpu.make_async_remote_copy(
        src_ref=rect, dst_ref=rect, send_sem=send_sems.at[i * N_CHIPS + kk],
        recv_sem=recv_sems.at[jj * MAX_PIECES + i], device_id=(kk // 2, kk % 2, c), device_id_type=MESH)


def _gather_start(arrays, after):
    stage = tuple(arrays)
    na = len(stage)

    def body(*refs):
        arr = dict(zip(stage, refs[:na]))
        send_sems, recv_sems = refs[na + 1], refs[na + 2]
        token = refs[-1]
        _, _, c, j = _position()
        for jj in range(N_CHIPS):
            @pl.when(j == jj)
            def _():
                for i, p in _stage_pieces(jj, stage):
                    for kk in range(N_CHIPS):
                        if kk != jj:
                            _gather_copy(arr, send_sems, recv_sems, c, jj, i, p, kk).start()
        token[...] = jnp.zeros_like(token)

    outs = pl.pallas_call(
        body,
        name=f"gather_start_{stage[0]}",
        in_specs=[HBM] * na + [ANY],
        out_specs=[SEM, SEM] + [HBM] * na + [pl.BlockSpec(memory_space=pltpu.VMEM)],
        out_shape=[pltpu.SemaphoreType.DMA((N_SEM,)), pltpu.SemaphoreType.DMA((N_SEM,))]
        + [pltpu.HBM(GATHERED[n], BF16) for n in stage] + [jax.ShapeDtypeStruct((8, LANE), F32)],
        input_output_aliases={k: 2 + k for k in range(na)},
        compiler_params=pltpu.CompilerParams(has_side_effects=EFFECT),
    )(*[_in_hbm(arrays[n]) for n in stage], after)
    return outs[0], outs[1], dict(zip(stage, outs[2:2 + na])), outs[-1]


def _gather_wait(send_sems, recv_sems, arrays, after):
    stage = tuple(arrays)
    na = len(stage)

    def body(*refs):
        arr = dict(zip(stage, refs[:na]))
        sems_s, sems_r = refs[na], refs[na + 1]
        _, _, c, j = _position()
        for jj in range(N_CHIPS):
            @pl.when(j != jj)
            def _():
                for i, p in _stage_pieces(jj, stage):
                    _gather_copy(arr, sems_s, sems_r, c, jj, i, p, jj).wait_recv()

            @pl.when(j == jj)
            def _():
                for i, p in _stage_pieces(jj, stage):
                    for kk in range(N_CHIPS):
                        if kk != jj:
                            _gather_copy(arr, sems_s, sems_r, c, jj, i, p, kk).wait_send()

    outs = pl.pallas_call(
        body,
        name=f"gather_wait_{stage[0]}",
        in_specs=[HBM] * na + [SEM, SEM, ANY],
        out_specs=[HBM] * na,
        out_shape=[pltpu.HBM(GATHERED[n], BF16) for n in stage],
        input_output_aliases={k: k for k in range(na)},
        compiler_params=pltpu.CompilerParams(has_side_effects=EFFECT),
    )(*[arrays[n] for n in stage], send_sems, recv_sems, after)
    return dict(zip(stage, outs))


def _gather_swap(arrays):
    stage = tuple(arrays)
    na = len(stage)

    def body(*refs):
        dst = dict(zip(stage, refs[na:2 * na]))
        send_sems, recv_sems = refs[2 * na:]
        x, y, c, j = _position()

        def fwd(jj, i, p, which):
            rect = _half_rect(dst[p.dst], p, "dst", which)
            return pltpu.make_async_remote_copy(
                src_ref=rect, dst_ref=rect, send_sem=send_sems.at[jj * MAX_PIECES + i],
                recv_sem=recv_sems.at[jj * MAX_PIECES + i], device_id=(x, y, 1 - c), device_id_type=MESH)

        for jj in range(N_CHIPS):
            @pl.when(j != jj)
            def _():
                for i, p in _stage_pieces(jj, stage):
                    fwd(jj, i, p, c).start()
        for jj in range(N_CHIPS):
            @pl.when(j != jj)
            def _():
                for i, p in _stage_pieces(jj, stage):
                    fwd(jj, i, p, 1 - c).wait_recv()
        for jj in range(N_CHIPS):
            @pl.when(j != jj)
            def _():
                for i, p in _stage_pieces(jj, stage):
                    fwd(jj, i, p, c).wait_send()

    outs = pl.pallas_call(
        body,
        name=f"gather_swap_{stage[0]}",
        in_specs=[ANY] * na,
        out_specs=[ANY] * na,
        out_shape=[jax.ShapeDtypeStruct(GATHERED[n], BF16) for n in stage],
        input_output_aliases={k: k for k in range(na)},
        scratch_shapes=[pltpu.SemaphoreType.DMA((N_SEM,)), pltpu.SemaphoreType.DMA((N_SEM,))],
        compiler_params=pltpu.CompilerParams(has_side_effects=True),
    )(*[arrays[n] for n in stage])
    return dict(zip(stage, outs))


def _own_half(ref, shape, axis, which):
    if axis == 1:
        return ref.at[:, pl.ds(which * (shape[1] // 2), shape[1] // 2)]
    return ref.at[pl.ds(which * (shape[0] // 2), shape[0] // 2), :]


def _swap_halves(grads):
    names = tuple(grads)
    n_tr = len(names)

    def body(*refs):
        src = dict(zip(names, refs[:len(names)]))
        dst = dict(zip(names, refs[len(names):2 * len(names)]))
        send_sems, recv_sems = refs[2 * len(names):]
        x, y, c, _ = _position()
        copies = [pltpu.make_async_remote_copy(
            src_ref=_own_half(src[n], GATHERED[n], HALF_AXIS[n], 1 - c), dst_ref=dst[n],
            send_sem=send_sems.at[k], recv_sem=recv_sems.at[k],
            device_id=(x, y, 1 - c), device_id_type=MESH) for k, n in enumerate(names)]
        for cp in copies:
            cp.start()
        for cp in copies:
            cp.wait_recv()
        for cp in copies:
            cp.wait_send()

    outs = pl.pallas_call(
        body,
        name=f"swap_halves_{names[0]}",
        in_specs=[ANY] * len(names),
        out_specs=[ANY] * len(names),
        out_shape=[jax.ShapeDtypeStruct(_halved(GATHERED[n], HALF_AXIS[n]), BF16) for n in names],
        scratch_shapes=[pltpu.SemaphoreType.DMA((n_tr,)), pltpu.SemaphoreType.DMA((n_tr,))],
        compiler_params=pltpu.CompilerParams(has_side_effects=True),
    )(*[grads[n] for n in names])
    return dict(zip(names, outs))


ADD_ROWS = 256


def _add_half(full, recv, c_arr, name):
    rows, cols = recv.shape
    if HALF_AXIS[name] == 1:
        index = lambda i, c_ref: (i, c_ref[0])
    else:
        nb = rows // ADD_ROWS
        index = lambda i, c_ref: (nb * c_ref[0] + i, 0)

    def body(c_ref, a_ref, b_ref, o_ref):
        o_ref[...] = (a_ref[...].astype(F32) + b_ref[...].astype(F32)).astype(BF16)

    return pl.pallas_call(
        body,
        name=f"add_half_{name}",
        grid_spec=pltpu.PrefetchScalarGridSpec(
            num_scalar_prefetch=1,
            grid=(rows // ADD_ROWS,),
            in_specs=[pl.BlockSpec((ADD_ROWS, cols), index), pl.BlockSpec((ADD_ROWS, cols), lambda i, c_ref: (i, 0))],
            out_specs=pl.BlockSpec((ADD_ROWS, cols), lambda i, c_ref: (i, 0)),
        ),
        out_shape=jax.ShapeDtypeStruct((rows, cols), BF16),
        compiler_params=_params(("parallel",)),
    )(c_arr, full, recv)


SLOT_SHAPES = {n: _halved(SHARD_SHAPES[n], HALF_AXIS[n]) for n in SHARDS}


def _slot_shape(n):
    return (N_CHIPS,) + SLOT_SHAPES[n]


def _stage_shards(stage):
    pieces = [p for jj in range(N_CHIPS) for p in _pieces_of(jj)]
    return tuple(s for s in SHARDS if any(p.src == s and p.dst in stage for p in pieces))


def _scatter_copy(src, dst, send_sems, recv_sems, c, jj, kk, i, p):
    return pltpu.make_async_remote_copy(
        src_ref=_rect_in_half(src[p.dst], p, "dst"), dst_ref=_rect_in_half(dst[p.src].at[jj], p, "src"),
        send_sem=send_sems.at[kk * MAX_PIECES + i], recv_sem=recv_sems.at[jj * MAX_PIECES + i],
        device_id=(kk // 2, kk % 2, c), device_id_type=MESH)


def _scatter_start(halves, slots):
    stage, touched = tuple(halves), tuple(slots)
    nh, nt = len(stage), len(touched)

    def body(*refs):
        src = dict(zip(stage, refs[:nh]))
        dst = dict(zip(touched, refs[nh:nh + nt]))
        send_sems, recv_sems = refs[nh + nt], refs[nh + nt + 1]
        token = refs[-1]
        _, _, c, j = _position()
        for jj in range(N_CHIPS):
            @pl.when(j == jj)
            def _():
                for kk in range(N_CHIPS):
                    if kk != jj:
                        for i, p in _stage_pieces(kk, stage):
                            _scatter_copy(src, dst, send_sems, recv_sems, c, jj, kk, i, p).start()
        token[...] = jnp.zeros_like(token)

    outs = pl.pallas_call(
        body,
        name=f"scatter_start_{stage[0]}",
        in_specs=[HBM] * (nh + nt),
        out_specs=[SEM, SEM] + [HBM] * (nh + nt) + [pl.BlockSpec(memory_space=pltpu.VMEM)],
        out_shape=[pltpu.SemaphoreType.DMA((N_SEM,)), pltpu.SemaphoreType.DMA((N_SEM,))]
        + [pltpu.HBM(halves[n].shape, BF16) for n in stage] + [pltpu.HBM(_slot_shape(s), BF16) for s in touched]
        + [jax.ShapeDtypeStruct((8, LANE), F32)],
        input_output_aliases={k: 2 + k for k in range(nh + nt)},
        compiler_params=pltpu.CompilerParams(has_side_effects=EFFECT),
    )(*[_in_hbm(halves[n]) for n in stage], *[_in_hbm(slots[s]) for s in touched])
    return outs[0], outs[1], dict(zip(stage, outs[2:2 + nh])), dict(zip(touched, outs[2 + nh:2 + nh + nt])), outs[-1]


def _scatter_wait(send_sems, recv_sems, halves, slots, after):
    stage, touched = tuple(halves), tuple(slots)
    nh, nt = len(stage), len(touched)

    def body(*refs):
        src = dict(zip(stage, refs[:nh]))
        dst = dict(zip(touched, refs[nh:nh + nt]))
        sems_s, sems_r = refs[nh + nt], refs[nh + nt + 1]
        _, _, c, j = _position()
        for jj in range(N_CHIPS):
            @pl.when(j == jj)
            def _():
                for ss in range(N_CHIPS):
                    if ss != jj:
                        for i, p in _stage_pieces(jj, stage):
                            _scatter_copy(src, dst, sems_s, sems_r, c, ss, jj, i, p).wait_recv()
                for kk in range(N_CHIPS):
                    if kk != jj:
                        for i, p in _stage_pieces(kk, stage):
                            _scatter_copy(src, dst, sems_s, sems_r, c, jj, kk, i, p).wait_send()

    outs = pl.pallas_call(
        body,
        name=f"scatter_wait_{stage[0]}",
        in_specs=[HBM] * (nh + nt) + [SEM, SEM, ANY],
        out_specs=[HBM] * (nh + nt),
        out_shape=[pltpu.HBM(halves[n].shape, BF16) for n in stage] + [pltpu.HBM(_slot_shape(s), BF16) for s in touched],
        input_output_aliases={k: k for k in range(nh + nt)},
        compiler_params=pltpu.CompilerParams(has_side_effects=EFFECT),
    )(*[halves[n] for n in stage], *[slots[s] for s in touched], send_sems, recv_sems, after)
    return dict(zip(stage, outs[:nh])), dict(zip(touched, outs[nh:]))


SUM_ROWS = {"win": 448, "mk": 256, "br0": 256, "br1": 256, "br2": 256, "out": 256}


def _sum_in_chip_order(chip, own, s_ref):
    acc = None
    for k in range(N_CHIPS):
        term = jnp.where(chip == k, own, s_ref[k].astype(F32))
        acc = term if acc is None else acc + term
    return acc


def _sum_slots(slots, own_half, pos_arr, name):
    _, rows, cols = slots.shape
    tr = SUM_ROWS[name]
    nb = rows // tr
    if HALF_AXIS[name] == 1:
        own_index = lambda i, pos: (nb * pos[1] + i, 0)
        out_index = lambda i, pos: (i, pos[0])
    else:
        own_index = lambda i, pos: (i, pos[1])
        out_index = lambda i, pos: (nb * pos[0] + i, 0)

    def body(pos, s_ref, own_ref, o_ref):
        o_ref[...] = _sum_in_chip_order(pos[1], own_ref[...].astype(F32), s_ref)

    return pl.pallas_call(
        body,
        name=f"sum_slots_{name}",
        grid_spec=pltpu.PrefetchScalarGridSpec(
            num_scalar_prefetch=1,
            grid=(nb,),
            in_specs=[pl.BlockSpec((N_CHIPS, tr, cols), lambda i, pos: (0, i, 0)),
                      pl.BlockSpec((tr, cols), own_index)],
            out_specs=pl.BlockSpec((tr, cols), out_index),
        ),
        out_shape=jax.ShapeDtypeStruct(SHARD_SHAPES[name], F32),
        compiler_params=_params(("parallel",)),
    )(pos_arr, slots, own_half)


def _own_partial_tables():
    import numpy as np
    nb = SHARD // HALF_TILE
    grp, blk = np.zeros((N_CHIPS, nb), np.int32), np.zeros((N_CHIPS, nb), np.int32)
    for jj in range(N_CHIPS):
        for r, n, g, gr in _shard_runs(jj):
            for k in range(n // HALF_TILE):
                grp[jj, r // HALF_TILE + k] = GROUPS.index(g)
                blk[jj, r // HALF_TILE + k] = gr // HALF_TILE + k
    return grp, blk


def _sum_slots_win(slots, own_halves, pos_arr, grp_tbl, blk_tbl):
    nb = SHARD // HALF_TILE
    cols = D // 2

    def own_spec(gi):
        return pl.BlockSpec((HALF_TILE, cols), lambda b, pos, grp, blk: (jnp.where(grp[b] == gi, blk[b], 0), 0))

    def body(pos, grp, blk, s_ref, a_ref, b_ref, c_ref, d_ref, o_ref):
        g = grp[pl.program_id(0)]
        own = a_ref[...]
        for gi, ref in ((1, b_ref), (2, c_ref), (3, d_ref)):
            own = jnp.where(g == gi, ref[...], own)
        o_ref[...] = _sum_in_chip_order(pos[1], own.astype(F32), s_ref)

    return pl.pallas_call(
        body,
        name="sum_slots_win",
        grid_spec=pltpu.PrefetchScalarGridSpec(
            num_scalar_prefetch=3,
            grid=(nb,),
            in_specs=[pl.BlockSpec((N_CHIPS, HALF_TILE, cols), lambda b, pos, grp, blk: (0, b, 0))]
            + [own_spec(gi) for gi in range(len(GROUPS))],
            out_specs=pl.BlockSpec((HALF_TILE, cols), lambda b, pos, grp, blk: (b, pos[0])),
        ),
        out_shape=jax.ShapeDtypeStruct(SHARD_SHAPES["win"], F32),
        compiler_params=_params(("parallel",)),
    )(pos_arr, grp_tbl, blk_tbl, slots, *[own_halves[g] for g in GROUPS])


def _share_sums(sums):
    def body(*refs):
        bufs = refs[len(SHARDS):2 * len(SHARDS)]
        send_sems, recv_sems = refs[2 * len(SHARDS):]
        x, y, c, _ = _position()
        copies = []
        for k, (n, b) in enumerate(zip(SHARDS, bufs)):
            mine = _own_half(b, SHARD_SHAPES[n], HALF_AXIS[n], c)
            copies.append(pltpu.make_async_remote_copy(
                src_ref=mine, dst_ref=mine, send_sem=send_sems.at[k], recv_sem=recv_sems.at[k],
                device_id=(x, y, 1 - c), device_id_type=MESH))
        for cp in copies:
            cp.start()
        for cp in copies:
            cp.wait_recv()
        for cp in copies:
            cp.wait_send()

    outs = pl.pallas_call(
        body,
        name="share_sums",
        in_specs=[ANY] * len(SHARDS),
        out_specs=[ANY] * len(SHARDS),
        out_shape=[jax.ShapeDtypeStruct(sums[n].shape, F32) for n in SHARDS],
        input_output_aliases={k: k for k in range(len(SHARDS))},
        scratch_shapes=[pltpu.SemaphoreType.DMA((len(SHARDS),)), pltpu.SemaphoreType.DMA((len(SHARDS),))],
        compiler_params=pltpu.CompilerParams(has_side_effects=True),
    )(*[sums[n] for n in SHARDS])
    return dict(zip(SHARDS, outs))


N_DEV = 8


def _all_reduce_small(pack, name):
    rows = pack.shape[0]
    half = rows // 2

    def body(p_ref, o_ref, sib, land, sems):
        x, y, c, j = _position()
        sibling = (x, y, 1 - c)
        swap = pltpu.make_async_remote_copy(src_ref=p_ref, dst_ref=sib, send_sem=sems.at[0], recv_sem=sems.at[1],
                                            device_id=sibling, device_id_type=MESH)
        swap.start()
        swap.wait_recv()
        land[j] = p_ref[...] + sib[...]

        def mine(k, which):
            return land.at[k, pl.ds(which * half, half)]

        def ici(kk):
            return pltpu.make_async_remote_copy(
                src_ref=mine(j, c), dst_ref=mine(j, c), send_sem=sems.at[2 + kk], recv_sem=sems.at[6 + j],
                device_id=(kk // 2, kk % 2, c), device_id_type=MESH)

        def arrival(kk):
            return pltpu.make_async_remote_copy(
                src_ref=mine(kk, c), dst_ref=mine(kk, c), send_sem=sems.at[2 + kk], recv_sem=sems.at[6 + kk],
                device_id=(kk // 2, kk % 2, c), device_id_type=MESH)

        def passed_on(kk, which):
            return pltpu.make_async_remote_copy(
                src_ref=mine(kk, which), dst_ref=mine(kk, which), send_sem=sems.at[10 + kk],
                recv_sem=sems.at[14 + kk], device_id=sibling, device_id_type=MESH)

        for kk in range(N_CHIPS):
            @pl.when(j != kk)
            def _():
                ici(kk).start()
        for kk in range(N_CHIPS):
            @pl.when(j != kk)
            def _():
                arrival(kk).wait_recv()
                passed_on(kk, c).start()
        for kk in range(N_CHIPS):
            @pl.when(j != kk)
            def _():
                passed_on(kk, 1 - c).wait_recv()
        acc = land[0]
        for kk in range(1, N_CHIPS):
            acc = acc + land[kk]
        o_ref[...] = acc
        swap.wait_send()
        for kk in range(N_CHIPS):
            @pl.when(j != kk)
            def _():
                ici(kk).wait_send()
                passed_on(kk, c).wait_send()

    vmem = pl.BlockSpec(memory_space=pltpu.VMEM)
    return pl.pallas_call(
        body,
        name=name,
        in_specs=[vmem],
        out_specs=vmem,
        out_shape=jax.ShapeDtypeStruct((rows, LANE), F32),
        scratch_shapes=[pltpu.VMEM((rows, LANE), F32), pltpu.VMEM((N_CHIPS, rows, LANE), F32),
                        pltpu.SemaphoreType.DMA((18,))],
        compiler_params=pltpu.CompilerParams(has_side_effects=True, vmem_limit_bytes=VMEM_LIMIT),
    )(pack)


def _adamw(w, g, m, v, name, tr):
    rows, cols = w.shape
    tr = min(tr, rows)

    def body(w_ref, g_ref, m_ref, v_ref, d_ref, nm_ref, nv_ref):
        gv = g_ref[...]
        nm = ADAM_B1 * m_ref[...] + (1.0 - ADAM_B1) * gv
        nv = ADAM_B2 * v_ref[...] + (1.0 - ADAM_B2) * (gv * gv)
        nm_ref[...] = nm
        nv_ref[...] = nv
        m_hat = nm / (1.0 - ADAM_B1 ** ADAM_STEP)
        v_hat = nv / (1.0 - ADAM_B2 ** ADAM_STEP)
        d_ref[...] = -ADAM_LR * (m_hat / (jnp.sqrt(v_hat) + ADAM_EPS) + ADAM_WD * w_ref[...])

    blk = pl.BlockSpec((tr, cols), lambda i: (i, 0))
    shape = jax.ShapeDtypeStruct((rows, cols), F32)
    return pl.pallas_call(
        body,
        name=f"adamw_{name}",
        grid=(rows // tr,),
        in_specs=[blk] * 4,
        out_specs=[blk] * 3,
        out_shape=[shape] * 3,
        compiler_params=_params(("parallel",)),
    )(w, g, m, v)


SMALL = (("pre_norm_g", (1, D)), ("post_norm_g", (1, D)), ("mem_norm_g", (1, D)), ("conv_w", (CONV_W, D_RNN)),
         ("conv_b", (1, D_RNN)), ("w_rg_a", (RNN_BLOCKS, LANE, LANE)), ("b_rg_a", (1, D_RNN)),
         ("w_rg_x", (RNN_BLOCKS, LANE, LANE)), ("b_rg_x", (1, D_RNN)), ("lru_lambda", (1, D_RNN)),
         ("swa_sinks", (1, SWA_HEADS)), ("rel_bias", (REL_BUCKETS, SWA_HEADS)))
PACK_ROWS = 2176


def _slot_len(shape):
    return -(-math.prod(shape) // LANE) * LANE


def _pack(values):
    parts = []
    for name, shape in SMALL:
        flat = values[name].reshape(-1).astype(F32)
        parts.append(jnp.pad(flat, (0, _slot_len(shape) - flat.shape[0])))
    flat = jnp.concatenate(parts)
    return jnp.pad(flat, (0, PACK_ROWS * LANE - flat.shape[0])).reshape(PACK_ROWS, LANE)


def _unpack(pack, shapes=None):
    flat = pack.reshape(-1)
    out, off = {}, 0
    for name, shape in SMALL:
        shp = shape if shapes is None or name not in shapes else shapes[name]
        out[name] = flat[off:off + math.prod(shp)].reshape(shp)
        off += _slot_len(shape)
    return out


TWIN_WEIGHTS = ("pre_norm_g", "post_norm_g", "mem_norm_g", "w_in", "conv_w", "conv_b", "w_rg_a", "b_rg_a", "w_rg_x",
                "b_rg_x", "lru_lambda", "swa_sinks", "rel_bias", "w_mem_kv", "w_br_rg", "w_br_swa", "w_br_mem", "w_out")
BIG = {"w_in": "win", "w_mem_kv": "mk", "w_br_rg": "br0", "w_br_swa": "br1", "w_br_mem": "br2", "w_out": "out"}


def kernel(x, mem, pre_norm_g, post_norm_g, mem_norm_g, w_in, conv_w, conv_b, w_rg_a, b_rg_a, w_rg_x, b_rg_x, lru_lambda, swa_sinks, rel_bias, w_mem_kv, w_br_rg, w_br_swa, w_br_mem, w_out, loss_target, m_pre_norm_g, m_post_norm_g, m_mem_norm_g, m_w_in, m_conv_w, m_conv_b, m_w_rg_a, m_b_rg_a, m_w_rg_x, m_b_rg_x, m_lru_lambda, m_swa_sinks, m_rel_bias, m_w_mem_kv, m_w_br_rg, m_w_br_swa, m_w_br_mem, m_w_out, v_pre_norm_g, v_post_norm_g, v_mem_norm_g, v_w_in, v_conv_w, v_conv_b, v_w_rg_a, v_b_rg_a, v_w_rg_x, v_b_rg_x, v_lru_lambda, v_swa_sinks, v_rel_bias, v_w_mem_kv, v_w_br_rg, v_w_br_swa, v_w_br_mem, v_w_out):
    args = dict(locals())
    out_shapes = {n: args[n].shape for n in TWIN_WEIGHTS}
    w = {n: (args[n] if n == "rel_bias" else args[n][0]) for n in TWIN_WEIGHTS}
    m = {n: (args["m_" + n] if n == "rel_bias" else args["m_" + n][0]) for n in TWIN_WEIGHTS}
    v = {n: (args["v_" + n] if n == "rel_bias" else args["v_" + n][0]) for n in TWIN_WEIGHTS}
    for d in (w, m, v):
        for n, shape in SMALL:
            if n != "conv_w":
                d[n] = d[n].reshape(shape)

    xi, yi, ci = lax.axis_index("x"), lax.axis_index("y"), lax.axis_index("c")
    chip = 2 * xi + yi
    c_arr = ci.astype(jnp.int32).reshape(1)
    zero = jnp.zeros((), jnp.int32)
    cw0 = (chip * (D_RNN // N_CHIPS)).astype(jnp.int32)

    placed = lax.dynamic_update_slice(jnp.zeros((CONV_W, D_RNN), F32), w["conv_w"], (zero, cw0))
    placed = jnp.where(ci == 0, placed, 0.0).reshape(CONV_W * D_RNN // LANE, LANE)
    conv_w_full = _all_reduce_small(placed, "gather_conv_w").reshape(CONV_W, D_RNN)

    for d in (w, m, v):
        d["w_in"] = d["w_in"].T
    chip_row = lambda tbl: lax.dynamic_slice(jnp.asarray(tbl), (chip.astype(jnp.int32), zero), (1, tbl.shape[1]))[0]
    big_of = {s: n for n, s in BIG.items()}
    ag, token = {}, conv_w_full
    for stage in GATHER_STAGES:
        placed = {n: (_place_group(w["w_in"], n, chip_row(_own_block_table(n))) if n in GROUPS
                      else _place_shard(w[big_of[n]], n)) for n in stage}
        send, recv, in_flight, token = _gather_start(placed, token)
        ag[stage] = (send, recv, in_flight)

    def fetch(names, after):
        send, recv, in_flight = ag[names]
        ready = _gather_swap(_gather_wait(send, recv, in_flight, after))
        return tuple(ready[n] for n in names)

    rs = {"slots": {}, "halves": {}, "pending": []}

    def emit(grads):
        received = _swap_halves(grads)
        halves = {n: _add_half(grads[n], received[n], c_arr, n) for n in grads}
        landing = {s: rs["slots"][s] if s in rs["slots"] else lax.empty(_slot_shape(s), BF16)
                   for s in _stage_shards(tuple(grads))}
        send, recv, halves, landing, token = _scatter_start(halves, landing)
        rs["slots"].update(landing)
        rs["pending"].append((send, recv, halves, tuple(landing)))
        return token

    sp = {n: w[n] for n, _ in SMALL}
    sp["conv_w"] = conv_w_full
    sq, grad_x, d_small = _local_step(x[0], mem[0], loss_target[0], sp, fetch, emit)
    loss = lax.psum(sq[0, 0] * (0.5 / D), ("x", "y", "c"))

    small_total = _all_reduce_small(_pack(d_small), "all_reduce_small")

    for send, recv, halves, touched in rs["pending"]:
        halves, landed = _scatter_wait(send, recv, halves, {s: rs["slots"][s] for s in touched}, small_total)
        rs["slots"].update(landed)
        rs["halves"].update(halves)
    pos_arr = jnp.stack([ci, chip]).astype(jnp.int32)
    grp_tbl, blk_tbl = (chip_row(t) for t in _own_partial_tables())
    sums = {s: _sum_slots(rs["slots"][s], rs["halves"][s], pos_arr, s) for s in SHARDS if s != "win"}
    sums["win"] = _sum_slots_win(rs["slots"]["win"], rs["halves"], pos_arr, grp_tbl, blk_tbl)
    sums = _share_sums(sums)
    g_big = {n: sums[s] for n, s in BIG.items()}

    g_small = _unpack(small_total)
    g_small["conv_w"] = lax.dynamic_slice(g_small["conv_w"], (zero, cw0), (CONV_W, D_RNN // N_CHIPS))

    grad, delta, new_m, new_v = {}, {}, {}, {}
    for n, s in BIG.items():
        grad[n] = g_big[n]
        delta[n], new_m[n], new_v[n] = _adamw(w[n], g_big[n], m[n], v[n], s, 224 if n == "w_in" else 128)
    for group in (grad, delta, new_m, new_v):
        group["w_in"] = group["w_in"].T
    d_, m_, v_ = _adamw(_pack(w), _pack(g_small), _pack(m), _pack(v), "small", PACK_ROWS)
    shard_shapes = {"conv_w": (CONV_W, D_RNN // N_CHIPS)}
    d_, m_, v_ = (_unpack(a, shard_shapes) for a in (d_, m_, v_))
    for n, _ in SMALL:
        grad[n], delta[n], new_m[n], new_v[n] = g_small[n], d_[n], m_[n], v_[n]

    outs = [loss, grad_x.reshape(1, S, D)]
    for group in (grad, delta, new_m, new_v):
        outs += [group[n].reshape(out_shapes[n]) for n in TWIN_WEIGHTS]
    return tuple(outs)
```

```python
import functools
import math
from typing import NamedTuple

import jax
import jax.numpy as jnp
from jax import lax
from jax.experimental import pallas as pl
from jax.experimental.pallas import tpu as pltpu

F32 = jnp.float32
BF16 = jnp.bfloat16
MESH = pl.DeviceIdType.MESH

S = 2048
D = 2048
MEM = 256
D_RNN = 1024
RNN_BLOCKS = 8
CONV_W = 4
LRU_C = 8.0
SWA_HEADS = 16
SWA_HD = 64
WINDOW = 128
MEM_HEADS = 4
MEM_HD = 256
REL_BUCKETS = 32
REL_MAX_DIST = 128
EPS = 1e-6
NEG_INF = -1e30
LANE = 128
SHARD = 3136
HALF_TILE = 64
N_CHIPS = 4
VMEM_LIMIT = 56 * 1024 * 1024

ADAM_LR = 0.001
ADAM_B1 = 0.9
ADAM_B2 = 0.999
ADAM_EPS = 1e-08
ADAM_WD = 0.01
ADAM_STEP = 10

GROUP_TILES = {"A": 16, "B": 18, "C": 16, "D": 48}
GROUPS = ("A", "B", "C", "D")


def _params(sem=None):
    return pltpu.CompilerParams(dimension_semantics=sem, vmem_limit_bytes=VMEM_LIMIT)


def _sigmoid(v):
    return jax.nn.sigmoid(v)


def _tile_home(t):
    if t < 16:
        return "A", t
    if t < 24:
        return "B", t - 16
    if t < 26:
        return "B", t - 24 + 16
    if t < 34:
        return "B", t - 26 + 8
    if t < 50:
        return "C", t - 34
    return "D", t - 50


def _shard_runs(j):
    runs = []
    per_shard = SHARD // HALF_TILE
    for q in range(per_shard * j, per_shard * (j + 1)):
        g, gt = _tile_home(q // 2)
        row = gt * LANE + (q % 2) * HALF_TILE
        if runs and runs[-1][2] == g and runs[-1][3] + runs[-1][1] == row:
            runs[-1][1] += HALF_TILE
        else:
            runs.append([(q - per_shard * j) * HALF_TILE, HALF_TILE, g, row])
    return [tuple(r) for r in runs]


_DIMS = {
    "nn": (((1,), (0,)), ((), ())),
    "nt": (((1,), (1,)), ((), ())),
    "tn": (((0,), (0,)), ((), ())),
}


def _mm(a, b, mode, out_dtype, tm, tn, tk, name, acc=None, after=None):
    if mode == "nn":
        (m, k), n = a.shape, b.shape[1]
    elif mode == "nt":
        (m, k), n = a.shape, b.shape[0]
    else:
        (k, m), n = a.shape, b.shape[1]
    tm, tn, tk = min(tm, m), min(tn, n), min(tk, k)
    assert m % tm == 0 and n % tn == 0 and k % tk == 0, (name, m, n, k)
    nk = k // tk
    has_acc = acc is not None

    def body(*refs):
        a_ref, b_ref = refs[0], refs[1]
        o_ref = refs[3] if has_acc else refs[2]
        p = lax.dot_general(a_ref[...], b_ref[...], _DIMS[mode], preferred_element_type=F32)

        def finish(v):
            if has_acc:
                v = v + refs[2][...]
            o_ref[...] = v.astype(out_dtype)

        if nk == 1:
            finish(p)
        else:
            s_ref = refs[-1]
            kk = pl.program_id(2)

            @pl.when(kk == 0)
            def _():
                s_ref[...] = p

            @pl.when(kk > 0)
            def _():
                s_ref[...] += p

            @pl.when(kk == nk - 1)
            def _():
                finish(s_ref[...])

    if mode == "nn":
        a_spec = pl.BlockSpec((tm, tk), lambda i, j, kk: (i, kk))
        b_spec = pl.BlockSpec((tk, tn), lambda i, j, kk: (kk, j))
    elif mode == "nt":
        a_spec = pl.BlockSpec((tm, tk), lambda i, j, kk: (i, kk))
        b_spec = pl.BlockSpec((tn, tk), lambda i, j, kk: (j, kk))
    else:
        a_spec = pl.BlockSpec((tk, tm), lambda i, j, kk: (kk, i))
        b_spec = pl.BlockSpec((tk, tn), lambda i, j, kk: (kk, j))
    o_spec = pl.BlockSpec((tm, tn), lambda i, j, kk: (i, j))
    in_specs = [a_spec, b_spec] + ([o_spec] if has_acc else [])
    args = (a, b) + ((acc,) if has_acc else ())
    if after is not None:
        in_specs.append(pl.BlockSpec(memory_space=pl.ANY))
        args += (after,)
    n_in = len(args)
    kernel_body = body

    def body(*refs):
        kernel_body(*(refs[:n_in - (after is not None)] + refs[n_in:]))

    return pl.pallas_call(
        body,
        name=name,
        grid=(m // tm, n // tn, nk),
        in_specs=in_specs,
        out_specs=o_spec,
        out_shape=jax.ShapeDtypeStruct((m, n), out_dtype),
        scratch_shapes=[pltpu.VMEM((tm, tn), F32)] if nk > 1 else [],
        compiler_params=_params(("parallel", "parallel", "arbitrary")),
    )(*args)


def _rms_fwd(x, g, name, ts=256):
    r, d = x.shape

    def body(x_ref, g_ref, o_ref):
        xv = x_ref[...]
        inv = lax.rsqrt(jnp.mean(xv * xv, axis=-1, keepdims=True) + EPS)
        o_ref[...] = (xv * inv * g_ref[...]).astype(BF16)

    return pl.pallas_call(
        body,
        name=name,
        grid=(r // ts,),
        in_specs=[pl.BlockSpec((ts, d), lambda i: (i, 0)), pl.BlockSpec((1, d), lambda i: (0, 0))],
        out_specs=pl.BlockSpec((ts, d), lambda i: (i, 0)),
        out_shape=jax.ShapeDtypeStruct((r, d), BF16),
        compiler_params=_params(("parallel",)),
    )(x, g)


def _post_loss(out, x, tgt, g_post, ts=256):
    n = S // ts

    def body(o_ref, x_ref, t_ref, g_ref, sq_ref, dy_ref, do_ref, dg_ref):
        i = pl.program_id(0)

        @pl.when(i == 0)
        def _():
            sq_ref[...] = jnp.zeros_like(sq_ref)
            dg_ref[...] = jnp.zeros_like(dg_ref)

        ov = o_ref[...]
        g = g_ref[...]
        inv = lax.rsqrt(jnp.mean(ov * ov, axis=-1, keepdims=True) + EPS)
        on = ov * inv
        err = x_ref[...] + on * g - t_ref[...]
        sq_ref[...] += jnp.sum(err * err)
        dy = err * (1.0 / D)
        dy_ref[...] = dy
        dg_ref[...] += jnp.sum(dy * on, axis=0, keepdims=True)
        don = dy * g
        do_ref[...] = (inv * (don - on * jnp.mean(don * on, axis=-1, keepdims=True))).astype(BF16)

    row = pl.BlockSpec((ts, D), lambda i: (i, 0))
    vec = pl.BlockSpec((1, D), lambda i: (0, 0))
    return pl.pallas_call(
        body,
        name="post_loss",
        grid=(n,),
        in_specs=[row, row, row, vec],
        out_specs=[pl.BlockSpec((8, LANE), lambda i: (0, 0)), row, row, vec],
        out_shape=[
            jax.ShapeDtypeStruct((8, LANE), F32),
            jax.ShapeDtypeStruct((S, D), F32),
            jax.ShapeDtypeStruct((S, D), BF16),
            jax.ShapeDtypeStruct((1, D), F32),
        ],
        compiler_params=_params(("arbitrary",)),
    )(out, x, tgt, g_post)


def _pre_bwd(dh, x, dy, g_pre, ts=256):
    n = S // ts

    def body(dh_ref, x_ref, dy_ref, g_ref, gx_ref, dg_ref):
        i = pl.program_id(0)

        @pl.when(i == 0)
        def _():
            dg_ref[...] = jnp.zeros_like(dg_ref)

        xv = x_ref[...]
        dhv = dh_ref[...]
        inv = lax.rsqrt(jnp.mean(xv * xv, axis=-1, keepdims=True) + EPS)
        xn = xv * inv
        dg_ref[...] += jnp.sum(dhv * xn, axis=0, keepdims=True)
        dxn = dhv * g_ref[...]
        gx_ref[...] = dy_ref[...] + inv * (dxn - xn * jnp.mean(dxn * xn, axis=-1, keepdims=True))

    row = pl.BlockSpec((ts, D), lambda i: (i, 0))
    vec = pl.BlockSpec((1, D), lambda i: (0, 0))
    return pl.pallas_call(
        body,
        name="pre_bwd",
        grid=(n,),
        in_specs=[row, row, row, vec],
        out_specs=[row, vec],
        out_shape=[jax.ShapeDtypeStruct((S, D), F32), jax.ShapeDtypeStruct((1, D), F32)],
        compiler_params=_params(("arbitrary",)),
    )(dh, x, dy, g_pre)


def _memnorm_bwd(dmemn, mem):
    def body(d_ref, m_ref, dg_ref):
        mv = m_ref[...]
        inv = lax.rsqrt(jnp.mean(mv * mv, axis=-1, keepdims=True) + EPS)
        dg_ref[...] = jnp.sum(d_ref[...] * mv * inv, axis=0, keepdims=True)

    return pl.pallas_call(
        body,
        name="memnorm_bwd",
        out_shape=jax.ShapeDtypeStruct((1, D), F32),
        compiler_params=_params(),
    )(dmemn, mem)


T_RNN = 256


def _neg_expm1(z):
    poly = -z * (1.0 + z * (0.5 + z * (1.0 / 6 + z * (1.0 / 24 + z * (1.0 / 120 + z * (1.0 / 720))))))
    return jnp.where(z > -0.1, poly, 1.0 - jnp.exp(z))


def _softplus_neg(lam):
    return jnp.maximum(-lam, 0.0) + jnp.log1p(jnp.exp(-jnp.abs(lam)))


def _rnn_gates(conv, wa_ref, ba, wx_ref, bx, lam, first_row):
    cbf = conv.astype(BF16)
    ga, gx = [], []
    for n in range(RNN_BLOCKS):
        c_n = cbf[:, n * LANE:(n + 1) * LANE]
        ga.append(jnp.dot(c_n, wa_ref[n], preferred_element_type=F32))
        gx.append(jnp.dot(c_n, wx_ref[n], preferred_element_type=F32))
    gate_r = _sigmoid(jnp.concatenate(ga, axis=1) + ba)
    gate_i = _sigmoid(jnp.concatenate(gx, axis=1) + bx)
    sp = _softplus_neg(lam)
    log_a = -LRU_C * gate_r * sp
    a = jnp.exp(log_a)
    mult_raw = jnp.sqrt(_neg_expm1(2.0 * log_a))
    mult = jnp.where(first_row, 1.0, mult_raw)
    return cbf, gate_r, gate_i, sp, a, mult_raw, mult


def _rglru_fwd(p_a, conv_w, conv_b, wa, ba, wx, bx, lam):
    t = T_RNN
    n = S // t

    def body(xr_ref, g_ref, cw_ref, cb_ref, wa_ref, ba_ref, wx_ref, bx_ref, lam_ref,
             y_ref, h_ref, xp_s, hcar, a_s, b_s):
        i = pl.program_id(0)

        @pl.when(i == 0)
        def _():
            xp_s[0:8, :] = jnp.zeros((8, D_RNN), F32)
            hcar[...] = jnp.zeros_like(hcar)

        @pl.when(i > 0)
        def _():
            xp_s[0:8, :] = xp_s[t:t + 8, :]

        xp_s[8:8 + t, :] = xr_ref[...]
        conv = cb_ref[...]
        for k in range(CONV_W):
            conv = conv + cw_ref[k:k + 1, :] * xp_s[8 - k:8 - k + t, :]
        rows = i * t + lax.broadcasted_iota(jnp.int32, (t, 1), 0)
        _, _, gate_i, _, a, _, mult = _rnn_gates(
            conv, wa_ref, ba_ref[...], wx_ref, bx_ref[...], lam_ref[...], rows == 0)
        a_s[...] = a
        b_s[...] = mult * gate_i * conv

        def step(tt, h):
            h = a_s[pl.ds(tt, 1), :] * h + b_s[pl.ds(tt, 1), :]
            h_ref[pl.ds(tt, 1), :] = h
            return h

        hcar[...] = lax.fori_loop(0, t, step, hcar[...], unroll=8)
        g = g_ref[...]
        y_ref[...] = (h_ref[...] * (g * _sigmoid(g))).astype(BF16)

    blk = lambda c: pl.BlockSpec((t, D_RNN), lambda i: (i, c))
    full = lambda shape: pl.BlockSpec(shape, lambda i: (0,) * len(shape))
    return pl.pallas_call(
        body,
        name="rglru_fwd",
        grid=(n,),
        in_specs=[blk(0), blk(1), full((CONV_W, D_RNN)), full((1, D_RNN)),
                  full((RNN_BLOCKS, LANE, LANE)), full((1, D_RNN)),
                  full((RNN_BLOCKS, LANE, LANE)), full((1, D_RNN)), full((1, D_RNN))],
        out_specs=[blk(0), blk(0)],
        out_shape=[jax.ShapeDtypeStruct((S, D_RNN), BF16), jax.ShapeDtypeStruct((S, D_RNN), F32)],
        scratch_shapes=[pltpu.VMEM((t + 8, D_RNN), F32), pltpu.VMEM((1, D_RNN), F32),
                        pltpu.VMEM((t, D_RNN), F32), pltpu.VMEM((t, D_RNN), F32)],
        compiler_params=_params(("arbitrary",)),
    )(p_a, p_a, conv_w, conv_b, wa, ba, wx, bx, lam)


def _rglru_bwd(dy, p_a, hseq, conv_w, conv_b, wa, ba, wx, bx, lam):
    t = T_RNN
    n = S // t
    rb = t // 8

    def body(dy_ref, xr_ref, g_ref, h_ref, xrp_ref, hp_ref, cw_ref, cb_ref, wa_ref, ba_ref, wx_ref, bx_ref, lam_ref,
             dp_ref, dcw_ref, dcb_ref, dwa_ref, dba_ref, dwx_ref, dbx_ref, dlam_ref,
             xp_s, hp_s, dxp_s, lamcar, a_s, dh_s, lam_s):
        i = pl.program_id(0)
        r = n - 1 - i

        @pl.when(i == 0)
        def _():
            for ref in (dcw_ref, dcb_ref, dwa_ref, dba_ref, dwx_ref, dbx_ref, dlam_ref, lamcar):
                ref[...] = jnp.zeros_like(ref)
            dxp_s[t:t + 8, :] = jnp.zeros((8, D_RNN), F32)

        @pl.when(i > 0)
        def _():
            dxp_s[t:t + 8, :] = dxp_s[0:8, :]

        has_prev = r > 0
        xp_s[0:8, :] = jnp.where(has_prev, xrp_ref[...], 0.0)
        xp_s[8:8 + t, :] = xr_ref[...]
        hp_s[0:8, :] = jnp.where(has_prev, hp_ref[...], 0.0)
        hp_s[8:8 + t, :] = h_ref[...]
        xs = [xp_s[8 - k:8 - k + t, :] for k in range(CONV_W)]
        conv = cb_ref[...]
        for k in range(CONV_W):
            conv = conv + cw_ref[k:k + 1, :] * xs[k]
        rows = r * t + lax.broadcasted_iota(jnp.int32, (t, 1), 0)
        first = rows == 0
        lam_p = lam_ref[...]
        cbf, gate_r, gate_i, sp, a, mult_raw, mult = _rnn_gates(
            conv, wa_ref, ba_ref[...], wx_ref, bx_ref[...], lam_p, first)

        g = g_ref[...]
        sg = _sigmoid(g)
        dyv = dy_ref[...]
        a_s[...] = a
        dh_s[...] = dyv * (g * sg)
        dg = dyv * h_ref[...] * (sg * (1.0 + g * (1.0 - sg)))

        def step(jj, car):
            tt = t - 1 - jj
            lm = dh_s[pl.ds(tt, 1), :] + car
            lam_s[pl.ds(tt, 1), :] = lm
            return a_s[pl.ds(tt, 1), :] * lm

        lamcar[...] = lax.fori_loop(0, t, step, lamcar[...], unroll=8)
        db = lam_s[...]
        da = db * hp_s[7:7 + t, :]
        dmult = db * gate_i * conv
        dgate_i = db * mult * conv
        dconv = db * mult * gate_i
        dlog_a = da * a + jnp.where(first, 0.0, dmult * (-(a * a) / mult_raw))
        dgate_r = dlog_a * (-LRU_C * sp)
        dsp = jnp.sum(dlog_a * (-LRU_C * gate_r), axis=0, keepdims=True)
        dlam_ref[...] += dsp * (-_sigmoid(-lam_p))
        dga = dgate_r * gate_r * (1.0 - gate_r)
        dgx = dgate_i * gate_i * (1.0 - gate_i)
        dba_ref[...] += jnp.sum(dga, axis=0, keepdims=True)
        dbx_ref[...] += jnp.sum(dgx, axis=0, keepdims=True)
        dga16, dgx16 = dga.astype(BF16), dgx.astype(BF16)
        back = []
        for nb in range(RNN_BLOCKS):
            sl = slice(nb * LANE, (nb + 1) * LANE)
            dwa_ref[nb] += lax.dot_general(cbf[:, sl], dga16[:, sl], _DIMS["tn"], preferred_element_type=F32)
            dwx_ref[nb] += lax.dot_general(cbf[:, sl], dgx16[:, sl], _DIMS["tn"], preferred_element_type=F32)
            back.append(lax.dot_general(dga16[:, sl], wa_ref[nb], _DIMS["nt"], preferred_element_type=F32)
                        + lax.dot_general(dgx16[:, sl], wx_ref[nb], _DIMS["nt"], preferred_element_type=F32))
        dconv = dconv + jnp.concatenate(back, axis=1)
        dcb_ref[...] += jnp.sum(dconv, axis=0, keepdims=True)
        for k in range(CONV_W):
            dcw_ref[k:k + 1, :] += jnp.sum(dconv * xs[k], axis=0, keepdims=True)
        dxp_s[0:t, :] = dconv
        dxr = cw_ref[0:1, :] * dconv
        for k in range(1, CONV_W):
            dxr = dxr + cw_ref[k:k + 1, :] * dxp_s[k:k + t, :]
        dp_ref[:, 0:D_RNN] = dxr.astype(BF16)
        dp_ref[:, D_RNN:2 * D_RNN] = dg.astype(BF16)

    blk = lambda c: pl.BlockSpec((t, D_RNN), lambda i: (n - 1 - i, c))
    prev8 = pl.BlockSpec((8, D_RNN), lambda i: (jnp.maximum((n - 1 - i) * rb - 1, 0), 0))
    full = lambda shape: pl.BlockSpec(shape, lambda i: (0,) * len(shape))
    vec = full((1, D_RNN))
    mat = full((RNN_BLOCKS, LANE, LANE))
    return pl.pallas_call(
        body,
        name="rglru_bwd",
        grid=(n,),
        in_specs=[blk(0), blk(0), blk(1), blk(0), prev8, prev8,
                  full((CONV_W, D_RNN)), vec, mat, vec, mat, vec, vec],
        out_specs=[pl.BlockSpec((t, 2 * D_RNN), lambda i: (n - 1 - i, 0)),
                   full((CONV_W, D_RNN)), vec, mat, vec, mat, vec, vec],
        out_shape=[jax.ShapeDtypeStruct((S, 2 * D_RNN), BF16),
                   jax.ShapeDtypeStruct((CONV_W, D_RNN), F32), jax.ShapeDtypeStruct((1, D_RNN), F32),
                   jax.ShapeDtypeStruct((RNN_BLOCKS, LANE, LANE), F32), jax.ShapeDtypeStruct((1, D_RNN), F32),
                   jax.ShapeDtypeStruct((RNN_BLOCKS, LANE, LANE), F32), jax.ShapeDtypeStruct((1, D_RNN), F32),
                   jax.ShapeDtypeStruct((1, D_RNN), F32)],
        scratch_shapes=[pltpu.VMEM((t + 8, D_RNN), F32), pltpu.VMEM((t + 8, D_RNN), F32),
                        pltpu.VMEM((t + 8, D_RNN), F32), pltpu.VMEM((1, D_RNN), F32),
                        pltpu.VMEM((t, D_RNN), F32), pltpu.VMEM((t, D_RNN), F32), pltpu.VMEM((t, D_RNN), F32)],
        compiler_params=_params(("arbitrary",)),
    )(dy, p_a, p_a, hseq, p_a, hseq, conv_w, conv_b, wa, ba, wx, bx, lam)


QB = WINDOW
KB2 = 2 * WINDOW
N_QB = S // QB
N_PAIR = SWA_HEADS // 2


def _swa_keys(kvc_ref, kvp_ref):
    kk = jnp.concatenate([kvp_ref[:, 0:LANE], kvc_ref[:, 0:LANE]], axis=0)
    vv = jnp.concatenate([kvp_ref[:, LANE:2 * LANE], kvc_ref[:, LANE:2 * LANE]], axis=0)
    lo = lax.broadcasted_iota(jnp.int32, (1, LANE), 1) < SWA_HD
    kk_sw, vv_sw = pltpu.roll(kk, SWA_HD, 1), pltpu.roll(vv, SWA_HD, 1)
    kd = [jnp.where(lo, kk, kk_sw).astype(BF16), jnp.where(lo, kk_sw, kk).astype(BF16)]
    vd = [jnp.where(lo, vv, vv_sw).astype(BF16), jnp.where(lo, vv_sw, vv).astype(BF16)]
    return lo, kd, vd


def _swa_valid(n):
    qi = lax.broadcasted_iota(jnp.int32, (QB, KB2), 0)
    kj = lax.broadcasted_iota(jnp.int32, (QB, KB2), 1)
    dist = qi + WINDOW - kj
    return (dist >= 0) & (dist < WINDOW) & ((n > 0) | (kj >= WINDOW))


def _swa_probs(qh16, kd, bias, sink, valid):
    lg = lax.dot_general(qh16, kd, _DIMS["nt"], preferred_element_type=F32) * (SWA_HD ** -0.5) + bias
    lg = jnp.where(valid, lg, NEG_INF)
    m = jnp.maximum(jnp.max(lg, axis=-1, keepdims=True), sink)
    p = jnp.exp(lg - m)
    es = jnp.exp(sink - m)
    den = jnp.sum(p, axis=-1, keepdims=True) + es
    return p / den, es / den


def _swa_specs():
    q = pl.BlockSpec((QB, D_RNN), lambda n: (n, 0))
    g = pl.BlockSpec((QB, D_RNN), lambda n: (n, 1))
    kvc = pl.BlockSpec((QB, 2 * LANE), lambda n: (n, 8))
    kvp = pl.BlockSpec((QB, 2 * LANE), lambda n: (jnp.maximum(n - 1, 0), 8))
    bias = pl.BlockSpec((SWA_HEADS, QB, KB2), lambda n: (0, 0, 0))
    sinks = pl.BlockSpec(memory_space=pltpu.SMEM)
    return q, g, kvc, kvp, bias, sinks


def _swa_fwd(p_b, bias_t, sinks):
    def body(q_ref, g_ref, kvc_ref, kvp_ref, bias_ref, sink_ref, y_ref, o_ref):
        n = pl.program_id(0)
        lo, kd, vd = _swa_keys(kvc_ref, kvp_ref)
        valid = _swa_valid(n)
        for hp in range(N_PAIR):
            sl = slice(hp * LANE, (hp + 1) * LANE)
            kvh = hp // (N_PAIR // 2)
            q = q_ref[:, sl]
            outs = []
            for j in range(2):
                mh = lo if j == 0 else jnp.logical_not(lo)
                qh16 = jnp.where(mh, q, 0.0).astype(BF16)
                probs, _ = _swa_probs(qh16, kd[kvh], bias_ref[2 * hp + j], sink_ref[2 * hp + j], valid)
                outs.append(jnp.dot(probs.astype(BF16), vd[kvh], preferred_element_type=F32))
            o = jnp.where(lo, outs[0], outs[1])
            o_ref[:, sl] = o
            g = g_ref[:, sl]
            y_ref[:, sl] = (o * (g * _sigmoid(g))).astype(BF16)

    q, g, kvc, kvp, bias, sinks_spec = _swa_specs()
    out = pl.BlockSpec((QB, D_RNN), lambda n: (n, 0))
    return pl.pallas_call(
        body,
        name="swa_fwd",
        grid=(N_QB,),
        in_specs=[q, g, kvc, kvp, bias, sinks_spec],
        out_specs=[out, out],
        out_shape=[jax.ShapeDtypeStruct((S, D_RNN), BF16), jax.ShapeDtypeStruct((S, D_RNN), F32)],
        compiler_params=_params(("parallel",)),
    )(p_b, p_b, p_b, p_b, bias_t, sinks)


def _swa_bwd(dy, p_b, o_swa, bias_t, sinks):
    def body(dy_ref, q_ref, g_ref, kvc_ref, kvp_ref, o_ref, bias_ref, sink_ref,
             dp_ref, dk_ref, dv_ref, dbias_ref, dsink_ref):
        n = pl.program_id(0)

        @pl.when(n == 0)
        def _():
            for ref in (dk_ref, dv_ref, dbias_ref, dsink_ref):
                ref[...] = jnp.zeros_like(ref)

        lo, kd, vd = _swa_keys(kvc_ref, kvp_ref)
        hi = jnp.logical_not(lo)
        valid = _swa_valid(n)
        dk_blk = jnp.zeros((KB2, LANE), F32)
        dv_blk = jnp.zeros((KB2, LANE), F32)
        for kvh in range(2):
            dk_pair = jnp.zeros((KB2, LANE), F32)
            dv_pair = jnp.zeros((KB2, LANE), F32)
            for hp in range(kvh * (N_PAIR // 2), (kvh + 1) * (N_PAIR // 2)):
                sl = slice(hp * LANE, (hp + 1) * LANE)
                q = q_ref[:, sl]
                g = g_ref[:, sl]
                o = o_ref[:, sl]
                dyv = dy_ref[:, sl]
                sg = _sigmoid(g)
                do = dyv * (g * sg)
                dp_ref[:, D_RNN + hp * LANE:D_RNN + (hp + 1) * LANE] = (
                    dyv * o * (sg * (1.0 + g * (1.0 - sg)))).astype(BF16)
                dqs = []
                for j in range(2):
                    h = 2 * hp + j
                    mh = lo if j == 0 else hi
                    qh16 = jnp.where(mh, q, 0.0).astype(BF16)
                    sink = sink_ref[h]
                    probs, psink = _swa_probs(qh16, kd[kvh], bias_ref[h], sink, valid)
                    doh = jnp.where(mh, do, 0.0)
                    doh16 = doh.astype(BF16)
                    delta = jnp.sum(doh * o, axis=-1, keepdims=True)
                    dpr = lax.dot_general(doh16, vd[kvh], _DIMS["nt"], preferred_element_type=F32)
                    ds = probs * (dpr - delta)
                    dbias_ref[h] += ds
                    dsink_ref[h:h + 1, :] += jnp.zeros((1, LANE), F32) - jnp.sum(psink * delta)
                    ds16 = (ds * (SWA_HD ** -0.5)).astype(BF16)
                    dqs.append(jnp.dot(ds16, kd[kvh], preferred_element_type=F32))
                    dk_pair = dk_pair + lax.dot_general(ds16, qh16, _DIMS["tn"], preferred_element_type=F32)
                    dv_pair = dv_pair + lax.dot_general(probs.astype(BF16), doh16, _DIMS["tn"],
                                                        preferred_element_type=F32)
                dp_ref[:, sl] = jnp.where(lo, dqs[0], dqs[1]).astype(BF16)
            keep = lo if kvh == 0 else hi
            dk_blk = dk_blk + jnp.where(keep, dk_pair + pltpu.roll(dk_pair, SWA_HD, 1), 0.0)
            dv_blk = dv_blk + jnp.where(keep, dv_pair + pltpu.roll(dv_pair, SWA_HD, 1), 0.0)

        cur = pl.ds(pl.multiple_of(n * QB, QB), QB)
        dk_ref[cur, :] += dk_blk[QB:KB2]
        dv_ref[cur, :] += dv_blk[QB:KB2]

        @pl.when(n > 0)
        def _():
            prev = pl.ds(pl.multiple_of((n - 1) * QB, QB), QB)
            dk_ref[prev, :] += dk_blk[0:QB]
            dv_ref[prev, :] += dv_blk[0:QB]

    q, g, kvc, kvp, bias, sinks_spec = _swa_specs()
    row = pl.BlockSpec((QB, D_RNN), lambda n: (n, 0))
    acc = pl.BlockSpec((S, LANE), lambda n: (0, 0))
    return pl.pallas_call(
        body,
        name="swa_bwd",
        grid=(N_QB,),
        in_specs=[row, q, g, kvc, kvp, row, bias, sinks_spec],
        out_specs=[pl.BlockSpec((QB, 2 * D_RNN), lambda n: (n, 0)), acc, acc, bias,
                   pl.BlockSpec((SWA_HEADS, LANE), lambda n: (0, 0))],
        out_shape=[jax.ShapeDtypeStruct((S, GROUP_TILES["B"] * LANE), BF16),
                   jax.ShapeDtypeStruct((S, LANE), F32), jax.ShapeDtypeStruct((S, LANE), F32),
                   jax.ShapeDtypeStruct((SWA_HEADS, QB, KB2), F32),
                   jax.ShapeDtypeStruct((SWA_HEADS, LANE), F32)],
        compiler_params=_params(("arbitrary",)),
    )(dy, p_b, p_b, p_b, p_b, o_swa, bias_t, sinks)


def _swa_pack(dp_b, dk, dv, ts=512):
    def body(_, dk_ref, dv_ref, o_ref):
        o_ref[:, 0:LANE] = dk_ref[...].astype(BF16)
        o_ref[:, LANE:2 * LANE] = dv_ref[...].astype(BF16)

    tile = pl.BlockSpec((ts, LANE), lambda i: (i, 0))
    return pl.pallas_call(
        body,
        name="swa_pack",
        grid=(S // ts,),
        in_specs=[pl.BlockSpec(memory_space=pl.ANY), tile, tile],
        out_specs=pl.BlockSpec((ts, 2 * LANE), lambda i: (i, 8)),
        out_shape=jax.ShapeDtypeStruct(dp_b.shape, dp_b.dtype),
        input_output_aliases={0: 0},
        compiler_params=_params(("parallel",)),
    )(dp_b, dk, dv)


def _split3(v):
    a = v.astype(BF16)
    r = v - a.astype(F32)
    b = r.astype(BF16)
    c = (r - b.astype(F32)).astype(BF16)
    return a, b, c


def _relbias_grad(dbias_flat, onehot_t):
    def body(d_ref, e_ref, o_ref):
        e = e_ref[...]
        acc = jnp.zeros((SWA_HEADS, REL_BUCKETS), F32)
        for term in _split3(d_ref[...]):
            acc = acc + lax.dot_general(term, e, _DIMS["nt"], preferred_element_type=F32)
        o_ref[...] = acc

    return pl.pallas_call(
        body,
        name="relbias_grad",
        out_shape=jax.ShapeDtypeStruct((SWA_HEADS, REL_BUCKETS), F32),
        compiler_params=_params(),
    )(dbias_flat, onehot_t)


TS_MEM = 512


def _mem_probs(q16, mk):
    lg = lax.dot_general(q16, mk, _DIMS["nt"], preferred_element_type=F32) * (MEM_HD ** -0.5)
    p = jnp.exp(lg - jnp.max(lg, axis=-1, keepdims=True))
    return p / jnp.sum(p, axis=-1, keepdims=True)


def _mem_fwd(p_c, mkv):
    def body(q_ref, g_ref, mkv_ref, y_ref, o_ref):
        for hm in range(MEM_HEADS):
            sl = slice(hm * MEM_HD, (hm + 1) * MEM_HD)
            probs = _mem_probs(q_ref[:, sl].astype(BF16), mkv_ref[:, sl])
            o = jnp.dot(probs.astype(BF16), mkv_ref[:, D_RNN + hm * MEM_HD:D_RNN + (hm + 1) * MEM_HD],
                        preferred_element_type=F32)
            o_ref[:, sl] = o
            g = g_ref[:, sl]
            y_ref[:, sl] = (o * (g * _sigmoid(g))).astype(BF16)

    blk = lambda c: pl.BlockSpec((TS_MEM, D_RNN), lambda i: (i, c))
    return pl.pallas_call(
        body,
        name="mem_fwd",
        grid=(S // TS_MEM,),
        in_specs=[blk(0), blk(1), pl.BlockSpec((MEM, 2 * D_RNN), lambda i: (0, 0))],
        out_specs=[blk(0), blk(0)],
        out_shape=[jax.ShapeDtypeStruct((S, D_RNN), BF16), jax.ShapeDtypeStruct((S, D_RNN), F32)],
        compiler_params=_params(("parallel",)),
    )(p_c, p_c, mkv)


def _mem_bwd(dy, p_c, o_mem, mkv):
    def body(dy_ref, q_ref, g_ref, o_ref, mkv_ref, dp_ref, dmkv_ref):
        @pl.when(pl.program_id(0) == 0)
        def _():
            dmkv_ref[...] = jnp.zeros_like(dmkv_ref)

        for hm in range(MEM_HEADS):
            sl = slice(hm * MEM_HD, (hm + 1) * MEM_HD)
            sv = slice(D_RNN + hm * MEM_HD, D_RNN + (hm + 1) * MEM_HD)
            q16 = q_ref[:, sl].astype(BF16)
            mk, mv = mkv_ref[:, sl], mkv_ref[:, sv]
            probs = _mem_probs(q16, mk)
            g, o, dyv = g_ref[:, sl], o_ref[:, sl], dy_ref[:, sl]
            sg = _sigmoid(g)
            do = dyv * (g * sg)
            dp_ref[:, sv] = (dyv * o * (sg * (1.0 + g * (1.0 - sg)))).astype(BF16)
            do16 = do.astype(BF16)
            delta = jnp.sum(do * o, axis=-1, keepdims=True)
            dpr = lax.dot_general(do16, mv, _DIMS["nt"], preferred_element_type=F32)
            ds16 = (probs * (dpr - delta) * (MEM_HD ** -0.5)).astype(BF16)
            dp_ref[:, sl] = jnp.dot(ds16, mk, preferred_element_type=F32).astype(BF16)
            dmkv_ref[:, sl] += lax.dot_general(ds16, q16, _DIMS["tn"], preferred_element_type=F32)
            dmkv_ref[:, sv] += lax.dot_general(probs.astype(BF16), do16, _DIMS["tn"], preferred_element_type=F32)

    blk = lambda c: pl.BlockSpec((TS_MEM, D_RNN), lambda i: (i, c))
    kv = pl.BlockSpec((MEM, 2 * D_RNN), lambda i: (0, 0))
    return pl.pallas_call(
        body,
        name="mem_bwd",
        grid=(S // TS_MEM,),
        in_specs=[blk(0), blk(0), blk(1), blk(0), kv],
        out_specs=[pl.BlockSpec((TS_MEM, 2 * D_RNN), lambda i: (i, 0)), kv],
        out_shape=[jax.ShapeDtypeStruct((S, 2 * D_RNN), BF16), jax.ShapeDtypeStruct((MEM, 2 * D_RNN), F32)],
        compiler_params=_params(("arbitrary",)),
    )(dy, p_c, p_c, o_mem, mkv)


TS_MRG = 512
TD_MRG = 512
N_DBLK = D // TD_MRG


def _merge_fwd(z, p_d):
    def body(z0, z1, z2, g0, g1, g2, o_ref):
        o_ref[...] = (_sigmoid(g0[...]) * z0[...] + _sigmoid(g1[...]) * z1[...]
                      + _sigmoid(g2[...]) * z2[...]).astype(BF16)

    blk = pl.BlockSpec((TS_MRG, TD_MRG), lambda i, d: (i, d))
    gate = lambda b: pl.BlockSpec((TS_MRG, TD_MRG), lambda i, d: (i, b * N_DBLK + d))
    return pl.pallas_call(
        body,
        name="merge_fwd",
        grid=(S // TS_MRG, N_DBLK),
        in_specs=[blk, blk, blk, gate(0), gate(1), gate(2)],
        out_specs=blk,
        out_shape=jax.ShapeDtypeStruct((S, D), BF16),
        compiler_params=_params(("parallel", "parallel")),
    )(z[0], z[1], z[2], p_d, p_d, p_d)


def _merge_bwd(dmerged, z_b, p_d, b, dp_d):
    def body(dm_ref, z_ref, g_ref, *refs):
        dz_ref, dg_ref = refs[-2], refs[-1]
        sg = _sigmoid(g_ref[...])
        dm = dm_ref[...]
        dz_ref[...] = (dm * sg).astype(BF16)
        dg_ref[...] = (dm * z_ref[...] * sg * (1.0 - sg)).astype(BF16)

    blk = pl.BlockSpec((TS_MRG, TD_MRG), lambda i, d: (i, d))
    gate = pl.BlockSpec((TS_MRG, TD_MRG), lambda i, d: (i, b * N_DBLK + d))
    in_specs = [blk, blk, gate]
    args = [dmerged, z_b, p_d]
    aliases = {}
    if dp_d is not None:
        in_specs.append(pl.BlockSpec(memory_space=pl.ANY))
        args.append(dp_d)
        aliases = {3: 1}
    return pl.pallas_call(
        body,
        name=f"merge_bwd{b}",
        grid=(S // TS_MRG, N_DBLK),
        in_specs=in_specs,
        out_specs=[blk, gate],
        out_shape=[jax.ShapeDtypeStruct((S, D), BF16),
                   jax.ShapeDtypeStruct((S, GROUP_TILES["D"] * LANE), BF16)],
        input_output_aliases=aliases,
        compiler_params=_params(("parallel", "parallel")),
    )(*args)


def _bucket_table():
    import numpy as np
    qi = np.arange(QB)[:, None]
    kj = np.arange(KB2)[None, :]
    n = np.maximum(qi + WINDOW - kj, 0)
    max_exact = REL_BUCKETS // 2
    ratio = np.log(np.maximum(n, 1).astype(np.float32) / max_exact) / np.float32(math.log(REL_MAX_DIST / max_exact))
    large = np.minimum(max_exact + (ratio * (REL_BUCKETS - max_exact)).astype(np.int32), REL_BUCKETS - 1)
    bucket = np.where(n < max_exact, n, large).reshape(1, QB * KB2)
    return (bucket == np.arange(REL_BUCKETS)[:, None]).astype(np.float32)


def _bias_expand(rel_bias_t, onehot_t):
    def body(r_ref, e_ref, o_ref):
        e = e_ref[...]
        acc = jnp.zeros((SWA_HEADS, QB * KB2), F32)
        for term in _split3(r_ref[...]):
            acc = acc + jnp.dot(term, e, preferred_element_type=F32)
        o_ref[...] = acc

    return pl.pallas_call(
        body,
        name="bias_expand",
        out_shape=jax.ShapeDtypeStruct((SWA_HEADS, QB * KB2), F32),
        compiler_params=_params(),
    )(rel_bias_t, onehot_t)


PROJ_TN = {"A": 1024, "B": 1152, "C": 1024, "D": 1536}


def _local_step(x, mem, tgt, sp, fetch, emit):
    onehot_t = jnp.asarray(_bucket_table(), BF16)
    bias_t = _bias_expand(sp["rel_bias"].T, onehot_t).reshape(SWA_HEADS, QB, KB2)
    sinks = sp["swa_sinks"].reshape(SWA_HEADS)
    wa16, wx16 = sp["w_rg_a"].astype(BF16), sp["w_rg_x"].astype(BF16)
    rnn = (sp["conv_w"], sp["conv_b"], wa16, sp["b_rg_a"], wx16, sp["b_rg_x"], sp["lru_lambda"])

    h = _rms_fwd(x, sp["pre_norm_g"], "rms_pre")
    memn = _rms_fwd(mem, sp["mem_norm_g"], "rms_mem")
    w_grp, p = {}, {}

    def project(g, after):
        (w_grp[g],) = fetch((g,), after)
        p[g] = _mm(h, w_grp[g], "nt", F32, 1024, PROJ_TN[g], D, f"proj_{g}")

    project("A", h)
    y_rg, hseq = _rglru_fwd(p["A"], *rnn)
    project("B", y_rg)
    y_swa, o_swa = _swa_fwd(p["B"], bias_t, sinks)
    project("C", y_swa)
    (wmk,) = fetch(("mk",), p["C"])
    mkv = _mm(memn, wmk, "nn", BF16, MEM, 1024, D, "mkv")
    y_mem, o_mem = _mem_fwd(p["C"], mkv)
    ys = (y_rg, y_swa, y_mem)
    wbr = fetch(("br0", "br1", "br2"), y_mem)
    z = [_mm(ys[b], wbr[b], "nn", F32, 1024, 1024, D_RNN, f"branch_out{b}") for b in range(3)]
    project("D", z[2])
    merged = _merge_fwd(z, p["D"])
    (wout,) = fetch(("out",), merged)
    out = _mm(merged, wout, "nn", F32, 1024, 1024, D, "out_proj")
    sq, dy, dout, d_post = _post_loss(out, x, tgt, sp["post_norm_g"])

    tok = emit({"out": _mm(merged, dout, "tn", BF16, 1024, 1024, S, "d_wout")})
    dmerged = _mm(dout, wout, "nt", F32, 1024, 1024, D, "d_merged", after=tok)
    dz, dp_d = [], None
    for b in range(3):
        dz_b, dp_d = _merge_bwd(dmerged, z[b], p["D"], b, dp_d)
        dz.append(dz_b)
    tok = emit({f"br{b}": _mm(ys[b], dz[b], "tn", BF16, 1024, 1024, S, f"d_wbr{b}") for b in range(3)})
    tok = emit({"D": _mm(dp_d, h, "tn", BF16, PROJ_TN["D"], 1024, S, "d_win_D", after=tok)})
    dy_mem = _mm(dz[2], wbr[2], "nt", F32, 1024, 1024, D, "d_branch2", after=tok)
    dp_c, dmkv = _mem_bwd(dy_mem, p["C"], o_mem, mkv)
    dmkv16 = dmkv.astype(BF16)
    tok = emit({"mk": _mm(memn, dmkv16, "tn", BF16, 1024, 1024, MEM, "d_wmk")})
    dmemn = _mm(dmkv16, wmk, "nt", F32, MEM, 1024, D, "d_memn", after=tok)
    d_memg = _memnorm_bwd(dmemn, mem)
    dys = [_mm(dz[b], wbr[b], "nt", F32, 1024, 1024, D, f"d_branch{b}", after=dmemn) for b in range(2)]
    dp_a, d_cw, d_cb, d_wa, d_ba, d_wx, d_bx, d_lam = _rglru_bwd(dys[0], p["A"], hseq, *rnn)
    dp_b, dk, dv, d_bias, d_sink = _swa_bwd(dys[1], p["B"], o_swa, bias_t, sinks)
    dp_b = _swa_pack(dp_b, dk, dv)
    d_rel = _relbias_grad(d_bias.reshape(SWA_HEADS, QB * KB2), onehot_t).T
    dp = {"A": dp_a, "B": dp_b, "C": dp_c, "D": dp_d}
    tok = emit({g: _mm(dp[g], h, "tn", BF16, PROJ_TN[g], 1024, S, f"d_win_{g}") for g in ("A", "B", "C")})
    dh = None
    for g in GROUPS:
        dh = _mm(dp[g], w_grp[g], "nn", F32, 1024, 1024, 2304 if g == "B" else 2048, f"d_h_{g}", acc=dh,
                 after=tok if g == "A" else None)
    grad_x, d_pre = _pre_bwd(dh, x, dy, sp["pre_norm_g"])

    d_small = {
        "pre_norm_g": d_pre, "post_norm_g": d_post, "mem_norm_g": d_memg, "conv_w": d_cw, "conv_b": d_cb,
        "w_rg_a": d_wa, "b_rg_a": d_ba, "w_rg_x": d_wx, "b_rg_x": d_bx, "lru_lambda": d_lam,
        "swa_sinks": d_sink[:, 0].reshape(1, SWA_HEADS), "rel_bias": d_rel,
    }
    return sq, grad_x, d_small


ANY = pl.BlockSpec(memory_space=pl.ANY)
SHARD_ROWS = D // N_CHIPS
GATHERED = {"A": (2048, D), "B": (2304, D), "C": (2048, D), "D": (6144, D), "mk": (D, D),
            "br0": (D_RNN, D), "br1": (D_RNN, D), "br2": (D_RNN, D), "out": (D, D)}
SHARD_SHAPES = {"win": (SHARD, D), "mk": (SHARD_ROWS, D), "br0": (D_RNN, SHARD_ROWS), "br1": (D_RNN, SHARD_ROWS),
                "br2": (D_RNN, SHARD_ROWS), "out": (SHARD_ROWS, D)}
SHARDS = tuple(SHARD_SHAPES)
HALF_AXIS = {"win": 1, "mk": 1, "br0": 0, "br1": 0, "br2": 0, "out": 1,
             "A": 1, "B": 1, "C": 1, "D": 1}


def _halved(shape, axis):
    return (shape[0] // 2, shape[1]) if axis == 0 else (shape[0], shape[1] // 2)


class Piece(NamedTuple):
    src: str
    dst: str
    rows: int
    sr0: int
    sc0: int
    dr0: int
    dc0: int
    ncols: int


def _pieces_of(jj):
    out = [Piece("win", g, n, r, 0, gr, 0, D) for r, n, g, gr in _shard_runs(jj)]
    out.append(Piece("mk", "mk", SHARD_ROWS, 0, 0, SHARD_ROWS * jj, 0, D))
    out += [Piece(f"br{b}", f"br{b}", D_RNN, 0, 0, 0, SHARD_ROWS * jj, SHARD_ROWS) for b in range(3)]
    out.append(Piece("out", "out", SHARD_ROWS, 0, 0, SHARD_ROWS * jj, 0, D))
    return out


def _half_rect(ref, p, side, which):
    r0, c0 = (p.sr0, p.sc0) if side == "src" else (p.dr0, p.dc0)
    if HALF_AXIS[p.src] == 1:
        return _rect(ref, r0, p.rows, c0 + which * (p.ncols // 2), p.ncols // 2)
    return _rect(ref, r0 + which * (p.rows // 2), p.rows // 2, c0, p.ncols)


def _rect_in_half(ref, p, side):
    r0, c0 = (p.sr0, p.sc0) if side == "src" else (p.dr0, p.dc0)
    if HALF_AXIS[p.src] == 1:
        return _rect(ref, r0, p.rows, 0, p.ncols // 2)
    return _rect(ref, 0, p.rows // 2, c0, p.ncols)


MAX_PIECES = max(len(_pieces_of(jj)) for jj in range(N_CHIPS))


def _rect(ref, r0, rows, c0, ncols):
    return ref.at[pl.ds(r0, rows), pl.ds(c0, ncols)]


def _position():
    x, y, c = lax.axis_index("x"), lax.axis_index("y"), lax.axis_index("c")
    return x, y, c, 2 * x + y


HBM = pl.BlockSpec(memory_space=pltpu.HBM)
SEM = pl.BlockSpec(memory_space=pltpu.SEMAPHORE)
EFFECT = pltpu.SideEffectType.DATAFLOW_SIDE_EFFECTING
N_SEM = MAX_PIECES * N_CHIPS
GATHER_STAGES = (("A",), ("B",), ("C",), ("mk",), ("br0", "br1", "br2"), ("D",), ("out",))


def _in_hbm(a):
    return pltpu.with_memory_space_constraint(a, pltpu.HBM)


def _stage_pieces(jj, stage):
    return [(i, p) for i, p in enumerate(_pieces_of(jj)) if p.dst in stage]


def _own_block_table(g):
    import numpy as np
    tbl = np.zeros((N_CHIPS, GATHERED[g][0] // HALF_TILE), np.int32)
    for jj in range(N_CHIPS):
        for r, n, grp, gr in _shard_runs(jj):
            if grp == g:
                for k in range(n // HALF_TILE):
                    tbl[jj, gr // HALF_TILE + k] = r // HALF_TILE + k
    return tbl


def _place_group(w_t, g, table):
    nb = GATHERED[g][0] // HALF_TILE

    def body(t_ref, x_ref, o_ref):
        o_ref[...] = x_ref[...].astype(BF16)

    return pl.pallas_call(
        body,
        name=f"place_{g}",
        grid_spec=pltpu.PrefetchScalarGridSpec(
            num_scalar_prefetch=1,
            grid=(nb,),
            in_specs=[pl.BlockSpec((HALF_TILE, D), lambda b, t: (t[b], 0))],
            out_specs=pl.BlockSpec((HALF_TILE, D), lambda b, t: (b, 0)),
        ),
        out_shape=jax.ShapeDtypeStruct(GATHERED[g], BF16),
        compiler_params=_params(("parallel",)),
    )(table, w_t)


def _place_shard(shard, name):
    rows, cols = shard.shape
    by_rows = HALF_AXIS[name] == 1

    def body(x_ref, o_ref):
        o_ref[...] = x_ref[...].astype(BF16)

    return pl.pallas_call(
        body,
        name=f"place_{name}",
        grid=(N_CHIPS,),
        in_specs=[pl.BlockSpec((rows, cols), lambda b: (0, 0))],
        out_specs=pl.BlockSpec((rows, cols), (lambda b: (b, 0)) if by_rows else (lambda b: (0, b))),
        out_shape=jax.ShapeDtypeStruct(GATHERED[name], BF16),
        compiler_params=_params(("parallel",)),
    )(shard)


def _gather_copy(arr, send_sems, recv_sems, c, jj, i, p, kk):
    rect = _half_rect(arr[p.dst], p, "dst", c)
    return pltpu.make_async_remote_copy(
        src_ref=rect, dst_ref=rect, send_sem=send_sems.at[i * N_CHIPS + kk],
        recv_sem=recv_sems.at[jj * MAX_PIECES + i], device_id=(kk // 2, kk % 2, c), device_id_type=MESH)


def _gather_start(arrays, after):
    stage = tuple(arrays)
    na = len(stage)

    def body(*refs):
        arr = dict(zip(stage, refs[:na]))
        send_sems, recv_sems = refs[na + 1], refs[na + 2]
        token = refs[-1]
        _, _, c, j = _position()
        for jj in range(N_CHIPS):
            @pl.when(j == jj)
            def _():
                for i, p in _stage_pieces(jj, stage):
                    for kk in range(N_CHIPS):
                        if kk != jj:
                            _gather_copy(arr, send_sems, recv_sems, c, jj, i, p, kk).start()
        token[...] = jnp.zeros_like(token)

    outs = pl.pallas_call(
        body,
        name=f"gather_start_{stage[0]}",
        in_specs=[HBM] * na + [ANY],
        out_specs=[SEM, SEM] + [HBM] * na + [pl.BlockSpec(memory_space=pltpu.VMEM)],
        out_shape=[pltpu.SemaphoreType.DMA((N_SEM,)), pltpu.SemaphoreType.DMA((N_SEM,))]
        + [pltpu.HBM(GATHERED[n], BF16) for n in stage] + [jax.ShapeDtypeStruct((8, LANE), F32)],
        input_output_aliases={k: 2 + k for k in range(na)},
        compiler_params=pltpu.CompilerParams(has_side_effects=EFFECT),
    )(*[_in_hbm(arrays[n]) for n in stage], after)
    return outs[0], outs[1], dict(zip(stage, outs[2:2 + na])), outs[-1]


def _gather_wait(send_sems, recv_sems, arrays, after):
    stage = tuple(arrays)
    na = len(stage)

    def body(*refs):
        arr = dict(zip(stage, refs[:na]))
        sems_s, sems_r = refs[na], refs[na + 1]
        _, _, c, j = _position()
        for jj in range(N_CHIPS):
            @pl.when(j != jj)
            def _():
                for i, p in _stage_pieces(jj, stage):
                    _gather_copy(arr, sems_s, sems_r, c, jj, i, p, jj).wait_recv()

            @pl.when(j == jj)
            def _():
                for i, p in _stage_pieces(jj, stage):
                    for kk in range(N_CHIPS):
                        if kk != jj:
                            _gather_copy(arr, sems_s, sems_r, c, jj, i, p, kk).wait_send()

    outs = pl.pallas_call(
        body,
        name=f"gather_wait_{stage[0]}",
        in_specs=[HBM] * na + [SEM, SEM, ANY],
        out_specs=[HBM] * na,
        out_shape=[pltpu.HBM(GATHERED[n], BF16) for n in stage],
        input_output_aliases={k: k for k in range(na)},
        compiler_params=pltpu.CompilerParams(has_side_effects=EFFECT),
    )(*[arrays[n] for n in stage], send_sems, recv_sems, after)
    return dict(zip(stage, outs))


def _gather_swap(arrays):
    stage = tuple(arrays)
    na = len(stage)

    def body(*refs):
        dst = dict(zip(stage, refs[na:2 * na]))
        send_sems, recv_sems = refs[2 * na:]
        x, y, c, j = _position()

        def fwd(jj, i, p, which):
            rect = _half_rect(dst[p.dst], p, "dst", which)
            return pltpu.make_async_remote_copy(
                src_ref=rect, dst_ref=rect, send_sem=send_sems.at[jj * MAX_PIECES + i],
                recv_sem=recv_sems.at[jj * MAX_PIECES + i], device_id=(x, y, 1 - c), device_id_type=MESH)

        for jj in range(N_CHIPS):
            @pl.when(j != jj)
            def _():
                for i, p in _stage_pieces(jj, stage):
                    fwd(jj, i, p, c).start()
        for jj in range(N_CHIPS):
            @pl.when(j != jj)
            def _():
                for i, p in _stage_pieces(jj, stage):
                    fwd(jj, i, p, 1 - c).wait_recv()
        for jj in range(N_CHIPS):
            @pl.when(j != jj)
            def _():
                for i, p in _stage_pieces(jj, stage):
                    fwd(jj, i, p, c).wait_send()

    outs = pl.pallas_call(
        body,
        name=f"gather_swap_{stage[0]}",
        in_specs=[ANY] * na,
        out_specs=[ANY] * na,
        out_shape=[jax.ShapeDtypeStruct(GATHERED[n], BF16) for n in stage],
        input_output_aliases={k: k for k in range(na)},
        scratch_shapes=[pltpu.SemaphoreType.DMA((N_SEM,)), pltpu.SemaphoreType.DMA((N_SEM,))],
        compiler_params=pltpu.CompilerParams(has_side_effects=True),
    )(*[arrays[n] for n in stage])
    return dict(zip(stage, outs))


def _own_half(ref, shape, axis, which):
    if axis == 1:
        return ref.at[:, pl.ds(which * (shape[1] // 2), shape[1] // 2)]
    return ref.at[pl.ds(which * (shape[0] // 2), shape[0] // 2), :]


def _swap_halves(grads):
    names = tuple(grads)
    n_tr = len(names)

    def body(*refs):
        src = dict(zip(names, refs[:len(names)]))
        dst = dict(zip(names, refs[len(names):2 * len(names)]))
        send_sems, recv_sems = refs[2 * len(names):]
        x, y, c, _ = _position()
        copies = [pltpu.make_async_remote_copy(
            src_ref=_own_half(src[n], GATHERED[n], HALF_AXIS[n], 1 - c), dst_ref=dst[n],
            send_sem=send_sems.at[k], recv_sem=recv_sems.at[k],
            device_id=(x, y, 1 - c), device_id_type=MESH) for k, n in enumerate(names)]
        for cp in copies:
            cp.start()
        for cp in copies:
            cp.wait_recv()
        for cp in copies:
            cp.wait_send()

    outs = pl.pallas_call(
        body,
        name=f"swap_halves_{names[0]}",
        in_specs=[ANY] * len(names),
        out_specs=[ANY] * len(names),
        out_shape=[jax.ShapeDtypeStruct(_halved(GATHERED[n], HALF_AXIS[n]), BF16) for n in names],
        scratch_shapes=[pltpu.SemaphoreType.DMA((n_tr,)), pltpu.SemaphoreType.DMA((n_tr,))],
        compiler_params=pltpu.CompilerParams(has_side_effects=True),
    )(*[grads[n] for n in names])
    return dict(zip(names, outs))


ADD_ROWS = 256


def _add_half(full, recv, c_arr, name):
    rows, cols = recv.shape
    if HALF_AXIS[name] == 1:
        index = lambda i, c_ref: (i, c_ref[0])
    else:
        nb = rows // ADD_ROWS
        index = lambda i, c_ref: (nb * c_ref[0] + i, 0)

    def body(c_ref, a_ref, b_ref, o_ref):
        o_ref[...] = (a_ref[...].astype(F32) + b_ref[...].astype(F32)).astype(BF16)

    return pl.pallas_call(
        body,
        name=f"add_half_{name}",
        grid_spec=pltpu.PrefetchScalarGridSpec(
            num_scalar_prefetch=1,
            grid=(rows // ADD_ROWS,),
            in_specs=[pl.BlockSpec((ADD_ROWS, cols), index), pl.BlockSpec((ADD_ROWS, cols), lambda i, c_ref: (i, 0))],
            out_specs=pl.BlockSpec((ADD_ROWS, cols), lambda i, c_ref: (i, 0)),
        ),
        out_shape=jax.ShapeDtypeStruct((rows, cols), BF16),
        compiler_params=_params(("parallel",)),
    )(c_arr, full, recv)


SLOT_SHAPES = {n: _halved(SHARD_SHAPES[n], HALF_AXIS[n]) for n in SHARDS}


def _slot_shape(n):
    return (N_CHIPS,) + SLOT_SHAPES[n]


def _stage_shards(stage):
    pieces = [p for jj in range(N_CHIPS) for p in _pieces_of(jj)]
    return tuple(s for s in SHARDS if any(p.src == s and p.dst in stage for p in pieces))


def _scatter_copy(src, dst, send_sems, recv_sems, c, jj, kk, i, p):
    return pltpu.make_async_remote_copy(
        src_ref=_rect_in_half(src[p.dst], p, "dst"), dst_ref=_rect_in_half(dst[p.src].at[jj], p, "src"),
        send_sem=send_sems.at[kk * MAX_PIECES + i], recv_sem=recv_sems.at[jj * MAX_PIECES + i],
        device_id=(kk // 2, kk % 2, c), device_id_type=MESH)


def _scatter_start(halves, slots):
    stage, touched = tuple(halves), tuple(slots)
    nh, nt = len(stage), len(touched)

    def body(*refs):
        src = dict(zip(stage, refs[:nh]))
        dst = dict(zip(touched, refs[nh:nh + nt]))
        send_sems, recv_sems = refs[nh + nt], refs[nh + nt + 1]
        token = refs[-1]
        _, _, c, j = _position()
        for jj in range(N_CHIPS):
            @pl.when(j == jj)
            def _():
                for kk in range(N_CHIPS):
                    if kk != jj:
                        for i, p in _stage_pieces(kk, stage):
                            _scatter_copy(src, dst, send_sems, recv_sems, c, jj, kk, i, p).start()
        token[...] = jnp.zeros_like(token)

    outs = pl.pallas_call(
        body,
        name=f"scatter_start_{stage[0]}",
        in_specs=[HBM] * (nh + nt),
        out_specs=[SEM, SEM] + [HBM] * (nh + nt) + [pl.BlockSpec(memory_space=pltpu.VMEM)],
        out_shape=[pltpu.SemaphoreType.DMA((N_SEM,)), pltpu.SemaphoreType.DMA((N_SEM,))]
        + [pltpu.HBM(halves[n].shape, BF16) for n in stage] + [pltpu.HBM(_slot_shape(s), BF16) for s in touched]
        + [jax.ShapeDtypeStruct((8, LANE), F32)],
        input_output_aliases={k: 2 + k for k in range(nh + nt)},
        compiler_params=pltpu.CompilerParams(has_side_effects=EFFECT),
    )(*[_in_hbm(halves[n]) for n in stage], *[_in_hbm(slots[s]) for s in touched])
    return outs[0], outs[1], dict(zip(stage, outs[2:2 + nh])), dict(zip(touched, outs[2 + nh:2 + nh + nt])), outs[-1]


def _scatter_wait(send_sems, recv_sems, halves, slots, after):
    stage, touched = tuple(halves), tuple(slots)
    nh, nt = len(stage), len(touched)

    def body(*refs):
        src = dict(zip(stage, refs[:nh]))
        dst = dict(zip(touched, refs[nh:nh + nt]))
        sems_s, sems_r = refs[nh + nt], refs[nh + nt + 1]
        _, _, c, j = _position()
        for jj in range(N_CHIPS):
            @pl.when(j == jj)
            def _():
                for ss in range(N_CHIPS):
                    if ss != jj:
                        for i, p in _stage_pieces(jj, stage):
                            _scatter_copy(src, dst, sems_s, sems_r, c, ss, jj, i, p).wait_recv()
                for kk in range(N_CHIPS):
                    if kk != jj:
                        for i, p in _stage_pieces(kk, stage):
                            _scatter_copy(src, dst, sems_s, sems_r, c, jj, kk, i, p).wait_send()

    outs = pl.pallas_call(
        body,
        name=f"scatter_wait_{stage[0]}",
        in_specs=[HBM] * (nh + nt) + [SEM, SEM, ANY],
        out_specs=[HBM] * (nh + nt),
        out_shape=[pltpu.HBM(halves[n].shape, BF16) for n in stage] + [pltpu.HBM(_slot_shape(s), BF16) for s in touched],
        input_output_aliases={k: k for k in range(nh + nt)},
        compiler_params=pltpu.CompilerParams(has_side_effects=EFFECT),
    )(*[halves[n] for n in stage], *[slots[s] for s in touched], send_sems, recv_sems, after)
    return dict(zip(stage, outs[:nh])), dict(zip(touched, outs[nh:]))


SUM_ROWS = {"win": 448, "mk": 256, "br0": 256, "br1": 256, "br2": 256, "out": 256}


def _sum_in_chip_order(chip, own, s_ref):
    acc = None
    for k in range(N_CHIPS):
        term = jnp.where(chip == k, own, s_ref[k].astype(F32))
        acc = term if acc is None else acc + term
    return acc


def _sum_slots(slots, own_half, pos_arr, name):
    _, rows, cols = slots.shape
    tr = SUM_ROWS[name]
    nb = rows // tr
    if HALF_AXIS[name] == 1:
        own_index = lambda i, pos: (nb * pos[1] + i, 0)
        out_index = lambda i, pos: (i, pos[0])
    else:
        own_index = lambda i, pos: (i, pos[1])
        out_index = lambda i, pos: (nb * pos[0] + i, 0)

    def body(pos, s_ref, own_ref, o_ref):
        o_ref[...] = _sum_in_chip_order(pos[1], own_ref[...].astype(F32), s_ref)

    return pl.pallas_call(
        body,
        name=f"sum_slots_{name}",
        grid_spec=pltpu.PrefetchScalarGridSpec(
            num_scalar_prefetch=1,
            grid=(nb,),
            in_specs=[pl.BlockSpec((N_CHIPS, tr, cols), lambda i, pos: (0, i, 0)),
                      pl.BlockSpec((tr, cols), own_index)],
            out_specs=pl.BlockSpec((tr, cols), out_index),
        ),
        out_shape=jax.ShapeDtypeStruct(SHARD_SHAPES[name], F32),
        compiler_params=_params(("parallel",)),
    )(pos_arr, slots, own_half)


def _own_partial_tables():
    import numpy as np
    nb = SHARD // HALF_TILE
    grp, blk = np.zeros((N_CHIPS, nb), np.int32), np.zeros((N_CHIPS, nb), np.int32)
    for jj in range(N_CHIPS):
        for r, n, g, gr in _shard_runs(jj):
            for k in range(n // HALF_TILE):
                grp[jj, r // HALF_TILE + k] = GROUPS.index(g)
                blk[jj, r // HALF_TILE + k] = gr // HALF_TILE + k
    return grp, blk


def _sum_slots_win(slots, own_halves, pos_arr, grp_tbl, blk_tbl):
    nb = SHARD // HALF_TILE
    cols = D // 2

    def own_spec(gi):
        return pl.BlockSpec((HALF_TILE, cols), lambda b, pos, grp, blk: (jnp.where(grp[b] == gi, blk[b], 0), 0))

    def body(pos, grp, blk, s_ref, a_ref, b_ref, c_ref, d_ref, o_ref):
        g = grp[pl.program_id(0)]
        own = a_ref[...]
        for gi, ref in ((1, b_ref), (2, c_ref), (3, d_ref)):
            own = jnp.where(g == gi, ref[...], own)
        o_ref[...] = _sum_in_chip_order(pos[1], own.astype(F32), s_ref)

    return pl.pallas_call(
        body,
        name="sum_slots_win",
        grid_spec=pltpu.PrefetchScalarGridSpec(
            num_scalar_prefetch=3,
            grid=(nb,),
            in_specs=[pl.BlockSpec((N_CHIPS, HALF_TILE, cols), lambda b, pos, grp, blk: (0, b, 0))]
            + [own_spec(gi) for gi in range(len(GROUPS))],
            out_specs=pl.BlockSpec((HALF_TILE, cols), lambda b, pos, grp, blk: (b, pos[0])),
        ),
        out_shape=jax.ShapeDtypeStruct(SHARD_SHAPES["win"], F32),
        compiler_params=_params(("parallel",)),
    )(pos_arr, grp_tbl, blk_tbl, slots, *[own_halves[g] for g in GROUPS])


def _share_sums(sums):
    def body(*refs):
        bufs = refs[len(SHARDS):2 * len(SHARDS)]
        send_sems, recv_sems = refs[2 * len(SHARDS):]
        x, y, c, _ = _position()
        copies = []
        for k, (n, b) in enumerate(zip(SHARDS, bufs)):
            mine = _own_half(b, SHARD_SHAPES[n], HALF_AXIS[n], c)
            copies.append(pltpu.make_async_remote_copy(
                src_ref=mine, dst_ref=mine, send_sem=send_sems.at[k], recv_sem=recv_sems.at[k],
                device_id=(x, y, 1 - c), device_id_type=MESH))
        for cp in copies:
            cp.start()
        for cp in copies:
            cp.wait_recv()
        for cp in copies:
            cp.wait_send()

    outs = pl.pallas_call(
        body,
        name="share_sums",
        in_specs=[ANY] * len(SHARDS),
        out_specs=[ANY] * len(SHARDS),
        out_shape=[jax.ShapeDtypeStruct(sums[n].shape, F32) for n in SHARDS],
        input_output_aliases={k: k for k in range(len(SHARDS))},
        scratch_shapes=[pltpu.SemaphoreType.DMA((len(SHARDS),)), pltpu.SemaphoreType.DMA((len(SHARDS),))],
        compiler_params=pltpu.CompilerParams(has_side_effects=True),
    )(*[sums[n] for n in SHARDS])
    return dict(zip(SHARDS, outs))


N_DEV = 8


def _all_reduce_small(pack, name):
    rows = pack.shape[0]
    half = rows // 2

    def body(p_ref, o_ref, sib, land, sems):
        x, y, c, j = _position()
        sibling = (x, y, 1 - c)
        swap = pltpu.make_async_remote_copy(src_ref=p_ref, dst_ref=sib, send_sem=sems.at[0], recv_sem=sems.at[1],
                                            device_id=sibling, device_id_type=MESH)
        swap.start()
        swap.wait_recv()
        land[j] = p_ref[...] + sib[...]

        def mine(k, which):
            return land.at[k, pl.ds(which * half, half)]

        def ici(kk):
            return pltpu.make_async_remote_copy(
                src_ref=mine(j, c), dst_ref=mine(j, c), send_sem=sems.at[2 + kk], recv_sem=sems.at[6 + j],
                device_id=(kk // 2, kk % 2, c), device_id_type=MESH)

        def arrival(kk):
            return pltpu.make_async_remote_copy(
                src_ref=mine(kk, c), dst_ref=mine(kk, c), send_sem=sems.at[2 + kk], recv_sem=sems.at[6 + kk],
                device_id=(kk // 2, kk % 2, c), device_id_type=MESH)

        def passed_on(kk, which):
            return pltpu.make_async_remote_copy(
                src_ref=mine(kk, which), dst_ref=mine(kk, which), send_sem=sems.at[10 + kk],
                recv_sem=sems.at[14 + kk], device_id=sibling, device_id_type=MESH)

        for kk in range(N_CHIPS):
            @pl.when(j != kk)
            def _():
                ici(kk).start()
        for kk in range(N_CHIPS):
            @pl.when(j != kk)
            def _():
                arrival(kk).wait_recv()
                passed_on(kk, c).start()
        for kk in range(N_CHIPS):
            @pl.when(j != kk)
            def _():
                passed_on(kk, 1 - c).wait_recv()
        acc = land[0]
        for kk in range(1, N_CHIPS):
            acc = acc + land[kk]
        o_ref[...] = acc
        swap.wait_send()
        for kk in range(N_CHIPS):
            @pl.when(j != kk)
            def _():
                ici(kk).wait_send()
                passed_on(kk, c).wait_send()

    vmem = pl.BlockSpec(memory_space=pltpu.VMEM)
    return pl.pallas_call(
        body,
        name=name,
        in_specs=[vmem],
        out_specs=vmem,
        out_shape=jax.ShapeDtypeStruct((rows, LANE), F32),
        scratch_shapes=[pltpu.VMEM((rows, LANE), F32), pltpu.VMEM((N_CHIPS, rows, LANE), F32),
                        pltpu.SemaphoreType.DMA((18,))],
        compiler_params=pltpu.CompilerParams(has_side_effects=True, vmem_limit_bytes=VMEM_LIMIT),
    )(pack)


def _adamw(w, g, m, v, name, tr):
    rows, cols = w.shape
    tr = min(tr, rows)

    def body(w_ref, g_ref, m_ref, v_ref, d_ref, nm_ref, nv_ref):
        gv = g_ref[...]
        nm = ADAM_B1 * m_ref[...] + (1.0 - ADAM_B1) * gv
        nv = ADAM_B2 * v_ref[...] + (1.0 - ADAM_B2) * (gv * gv)
        nm_ref[...] = nm
        nv_ref[...] = nv
        m_hat = nm / (1.0 - ADAM_B1 ** ADAM_STEP)
        v_hat = nv / (1.0 - ADAM_B2 ** ADAM_STEP)
        d_ref[...] = -ADAM_LR * (m_hat / (jnp.sqrt(v_hat) + ADAM_EPS) + ADAM_WD * w_ref[...])

    blk = pl.BlockSpec((tr, cols), lambda i: (i, 0))
    shape = jax.ShapeDtypeStruct((rows, cols), F32)
    return pl.pallas_call(
        body,
        name=f"adamw_{name}",
        grid=(rows // tr,),
        in_specs=[blk] * 4,
        out_specs=[blk] * 3,
        out_shape=[shape] * 3,
        compiler_params=_params(("parallel",)),
    )(w, g, m, v)


SMALL = (("pre_norm_g", (1, D)), ("post_norm_g", (1, D)), ("mem_norm_g", (1, D)), ("conv_w", (CONV_W, D_RNN)),
         ("conv_b", (1, D_RNN)), ("w_rg_a", (RNN_BLOCKS, LANE, LANE)), ("b_rg_a", (1, D_RNN)),
         ("w_rg_x", (RNN_BLOCKS, LANE, LANE)), ("b_rg_x", (1, D_RNN)), ("lru_lambda", (1, D_RNN)),
         ("swa_sinks", (1, SWA_HEADS)), ("rel_bias", (REL_BUCKETS, SWA_HEADS)))
PACK_ROWS = 2176


def _slot_len(shape):
    return -(-math.prod(shape) // LANE) * LANE


def _pack(values):
    parts = []
    for name, shape in SMALL:
        flat = values[name].reshape(-1).astype(F32)
        parts.append(jnp.pad(flat, (0, _slot_len(shape) - flat.shape[0])))
    flat = jnp.concatenate(parts)
    return jnp.pad(flat, (0, PACK_ROWS * LANE - flat.shape[0])).reshape(PACK_ROWS, LANE)


def _unpack(pack, shapes=None):
    flat = pack.reshape(-1)
    out, off = {}, 0
    for name, shape in SMALL:
        shp = shape if shapes is None or name not in shapes else shapes[name]
        out[name] = flat[off:off + math.prod(shp)].reshape(shp)
        off += _slot_len(shape)
    return out


TWIN_WEIGHTS = ("pre_norm_g", "post_norm_g", "mem_norm_g", "w_in", "conv_w", "conv_b", "w_rg_a", "b_rg_a", "w_rg_x",
                "b_rg_x", "lru_lambda", "swa_sinks", "rel_bias", "w_mem_kv", "w_br_rg", "w_br_swa", "w_br_mem", "w_out")
BIG = {"w_in": "win", "w_mem_kv": "mk", "w_br_rg": "br0", "w_br_swa": "br1", "w_br_mem": "br2", "w_out": "out"}


def kernel(x, mem, pre_norm_g, post_norm_g, mem_norm_g, w_in, conv_w, conv_b, w_rg_a, b_rg_a, w_rg_x, b_rg_x, lru_lambda, swa_sinks, rel_bias, w_mem_kv, w_br_rg, w_br_swa, w_br_mem, w_out, loss_target, m_pre_norm_g, m_post_norm_g, m_mem_norm_g, m_w_in, m_conv_w, m_conv_b, m_w_rg_a, m_b_rg_a, m_w_rg_x, m_b_rg_x, m_lru_lambda, m_swa_sinks, m_rel_bias, m_w_mem_kv, m_w_br_rg, m_w_br_swa, m_w_br_mem, m_w_out, v_pre_norm_g, v_post_norm_g, v_mem_norm_g, v_w_in, v_conv_w, v_conv_b, v_w_rg_a, v_b_rg_a, v_w_rg_x, v_b_rg_x, v_lru_lambda, v_swa_sinks, v_rel_bias, v_w_mem_kv, v_w_br_rg, v_w_br_swa, v_w_br_mem, v_w_out):
    args = dict(locals())
    out_shapes = {n: args[n].shape for n in TWIN_WEIGHTS}
    w = {n: (args[n] if n == "rel_bias" else args[n][0]) for n in TWIN_WEIGHTS}
    m = {n: (args["m_" + n] if n == "rel_bias" else args["m_" + n][0]) for n in TWIN_WEIGHTS}
    v = {n: (args["v_" + n] if n == "rel_bias" else args["v_" + n][0]) for n in TWIN_WEIGHTS}
    for d in (w, m, v):
        for n, shape in SMALL:
            if n != "conv_w":
                d[n] = d[n].reshape(shape)

    xi, yi, ci = lax.axis_index("x"), lax.axis_index("y"), lax.axis_index("c")
    chip = 2 * xi + yi
    c_arr = ci.astype(jnp.int32).reshape(1)
    zero = jnp.zeros((), jnp.int32)
    cw0 = (chip * (D_RNN // N_CHIPS)).astype(jnp.int32)

    placed = lax.dynamic_update_slice(jnp.zeros((CONV_W, D_RNN), F32), w["conv_w"], (zero, cw0))
    placed = jnp.where(ci == 0, placed, 0.0).reshape(CONV_W * D_RNN // LANE, LANE)
    conv_w_full = _all_reduce_small(placed, "gather_conv_w").reshape(CONV_W, D_RNN)

    for d in (w, m, v):
        d["w_in"] = d["w_in"].T
    chip_row = lambda tbl: lax.dynamic_slice(jnp.asarray(tbl), (chip.astype(jnp.int32), zero), (1, tbl.shape[1]))[0]
    big_of = {s: n for n, s in BIG.items()}
    ag, token = {}, conv_w_full
    for stage in GATHER_STAGES:
        placed = {n: (_place_group(w["w_in"], n, chip_row(_own_block_table(n))) if n in GROUPS
                      else _place_shard(w[big_of[n]], n)) for n in stage}
        send, recv, in_flight, token = _gather_start(placed, token)
        ag[stage] = (send, recv, in_flight)

    all_started = token

    def fetch(names, after):
        send, recv, in_flight = ag[names]
        after = all_started if names == GATHER_STAGES[0] else after
        ready = _gather_swap(_gather_wait(send, recv, in_flight, after))
        return tuple(ready[n] for n in names)

    rs = {"slots": {}, "halves": {}, "pending": []}

    def emit(grads):
        received = _swap_halves(grads)
        halves = {n: _add_half(grads[n], received[n], c_arr, n) for n in grads}
        landing = {s: rs["slots"][s] if s in rs["slots"] else lax.empty(_slot_shape(s), BF16)
                   for s in _stage_shards(tuple(grads))}
        send, recv, halves, landing, token = _scatter_start(halves, landing)
        rs["slots"].update(landing)
        rs["pending"].append((send, recv, halves, tuple(landing)))
        return token

    sp = {n: w[n] for n, _ in SMALL}
    sp["conv_w"] = conv_w_full
    sq, grad_x, d_small = _local_step(x[0], mem[0], loss_target[0], sp, fetch, emit)
    loss = lax.psum(sq[0, 0] * (0.5 / D), ("x", "y", "c"))

    small_total = _all_reduce_small(_pack(d_small), "all_reduce_small")

    for send, recv, halves, touched in rs["pending"]:
        halves, landed = _scatter_wait(send, recv, halves, {s: rs["slots"][s] for s in touched}, small_total)
        rs["slots"].update(landed)
        rs["halves"].update(halves)
    pos_arr = jnp.stack([ci, chip]).astype(jnp.int32)
    grp_tbl, blk_tbl = (chip_row(t) for t in _own_partial_tables())
    sums = {s: _sum_slots(rs["slots"][s], rs["halves"][s], pos_arr, s) for s in SHARDS if s != "win"}
    sums["win"] = _sum_slots_win(rs["slots"]["win"], rs["halves"], pos_arr, grp_tbl, blk_tbl)
    sums = _share_sums(sums)
    g_big = {n: sums[s] for n, s in BIG.items()}

    g_small = _unpack(small_total)
    g_small["conv_w"] = lax.dynamic_slice(g_small["conv_w"], (zero, cw0), (CONV_W, D_RNN // N_CHIPS))

    grad, delta, new_m, new_v = {}, {}, {}, {}
    for n, s in BIG.items():
        grad[n] = g_big[n]
        delta[n], new_m[n], new_v[n] = _adamw(w[n], g_big[n], m[n], v[n], s, 224 if n == "w_in" else 128)
    for group in (grad, delta, new_m, new_v):
        group["w_in"] = group["w_in"].T
    d_, m_, v_ = _adamw(_pack(w), _pack(g_small), _pack(m), _pack(v), "small", PACK_ROWS)
    shard_shapes = {"conv_w": (CONV_W, D_RNN // N_CHIPS)}
    d_, m_, v_ = (_unpack(a, shard_shapes) for a in (d_, m_, v_))
    for n, _ in SMALL:
        grad[n], delta[n], new_m[n], new_v[n] = g_small[n], d_[n], m_[n], v_[n]

    outs = [loss, grad_x.reshape(1, S, D)]
    for group in (grad, delta, new_m, new_v):
        outs += [group[n].reshape(out_shapes[n]) for n in TWIN_WEIGHTS]
    return tuple(outs)
```

```python
import functools
import math
from typing import NamedTuple

import jax
import jax.numpy as jnp
from jax import lax
from jax.experimental import pallas as pl
from jax.experimental.pallas import tpu as pltpu

F32 = jnp.float32
BF16 = jnp.bfloat16
MESH = pl.DeviceIdType.MESH

S = 2048
D = 2048
MEM = 256
D_RNN = 1024
RNN_BLOCKS = 8
CONV_W = 4
LRU_C = 8.0
SWA_HEADS = 16
SWA_HD = 64
WINDOW = 128
MEM_HEADS = 4
MEM_HD = 256
REL_BUCKETS = 32
REL_MAX_DIST = 128
EPS = 1e-6
NEG_INF = -1e30
LANE = 128
SHARD = 3136
HALF_TILE = 64
N_CHIPS = 4
VMEM_LIMIT = 56 * 1024 * 1024

ADAM_LR = 0.001
ADAM_B1 = 0.9
ADAM_B2 = 0.999
ADAM_EPS = 1e-08
ADAM_WD = 0.01
ADAM_STEP = 10

GROUP_TILES = {"A": 16, "B": 18, "C": 16, "D": 48}
GROUPS = ("A", "B", "C", "D")


def _params(sem=None):
    return pltpu.CompilerParams(dimension_semantics=sem, vmem_limit_bytes=VMEM_LIMIT)


def _sigmoid(v):
    return jax.nn.sigmoid(v)


def _tile_home(t):
    if t < 16:
        return "A", t
    if t < 24:
        return "B", t - 16
    if t < 26:
        return "B", t - 24 + 16
    if t < 34:
        return "B", t - 26 + 8
    if t < 50:
        return "C", t - 34
    return "D", t - 50


def _shard_runs(j):
    runs = []
    per_shard = SHARD // HALF_TILE
    for q in range(per_shard * j, per_shard * (j + 1)):
        g, gt = _tile_home(q // 2)
        row = gt * LANE + (q % 2) * HALF_TILE
        if runs and runs[-1][2] == g and runs[-1][3] + runs[-1][1] == row:
            runs[-1][1] += HALF_TILE
        else:
            runs.append([(q - per_shard * j) * HALF_TILE, HALF_TILE, g, row])
    return [tuple(r) for r in runs]


_DIMS = {
    "nn": (((1,), (0,)), ((), ())),
    "nt": (((1,), (1,)), ((), ())),
    "tn": (((0,), (0,)), ((), ())),
}


def _mm(a, b, mode, out_dtype, tm, tn, tk, name, acc=None, after=None):
    if mode == "nn":
        (m, k), n = a.shape, b.shape[1]
    elif mode == "nt":
        (m, k), n = a.shape, b.shape[0]
    else:
        (k, m), n = a.shape, b.shape[1]
    tm, tn, tk = min(tm, m), min(tn, n), min(tk, k)
    assert m % tm == 0 and n % tn == 0 and k % tk == 0, (name, m, n, k)
    nk = k // tk
    has_acc = acc is not None

    def body(*refs):
        a_ref, b_ref = refs[0], refs[1]
        o_ref = refs[3] if has_acc else refs[2]
        p = lax.dot_general(a_ref[...], b_ref[...], _DIMS[mode], preferred_element_type=F32)

        def finish(v):
            if has_acc:
                v = v + refs[2][...]
            o_ref[...] = v.astype(out_dtype)

        if nk == 1:
            finish(p)
        else:
            s_ref = refs[-1]
            kk = pl.program_id(2)

            @pl.when(kk == 0)
            def _():
                s_ref[...] = p

            @pl.when(kk > 0)
            def _():
                s_ref[...] += p

            @pl.when(kk == nk - 1)
            def _():
                finish(s_ref[...])

    if mode == "nn":
        a_spec = pl.BlockSpec((tm, tk), lambda i, j, kk: (i, kk))
        b_spec = pl.BlockSpec((tk, tn), lambda i, j, kk: (kk, j))
    elif mode == "nt":
        a_spec = pl.BlockSpec((tm, tk), lambda i, j, kk: (i, kk))
        b_spec = pl.BlockSpec((tn, tk), lambda i, j, kk: (j, kk))
    else:
        a_spec = pl.BlockSpec((tk, tm), lambda i, j, kk: (kk, i))
        b_spec = pl.BlockSpec((tk, tn), lambda i, j, kk: (kk, j))
    o_spec = pl.BlockSpec((tm, tn), lambda i, j, kk: (i, j))
    in_specs = [a_spec, b_spec] + ([o_spec] if has_acc else [])
    args = (a, b) + ((acc,) if has_acc else ())
    if after is not None:
        in_specs.append(pl.BlockSpec(memory_space=pl.ANY))
        args += (after,)
    n_in = len(args)
    kernel_body = body

    def body(*refs):
        kernel_body(*(refs[:n_in - (after is not None)] + refs[n_in:]))

    return pl.pallas_call(
        body,
        name=name,
        grid=(m // tm, n // tn, nk),
        in_specs=in_specs,
        out_specs=o_spec,
        out_shape=jax.ShapeDtypeStruct((m, n), out_dtype),
        scratch_shapes=[pltpu.VMEM((tm, tn), F32)] if nk > 1 else [],
        compiler_params=_params(("parallel", "parallel", "arbitrary")),
    )(*args)


def _rms_fwd(x, g, name, ts=256):
    r, d = x.shape

    def body(x_ref, g_ref, o_ref):
        xv = x_ref[...]
        inv = lax.rsqrt(jnp.mean(xv * xv, axis=-1, keepdims=True) + EPS)
        o_ref[...] = (xv * inv * g_ref[...]).astype(BF16)

    return pl.pallas_call(
        body,
        name=name,
        grid=(r // ts,),
        in_specs=[pl.BlockSpec((ts, d), lambda i: (i, 0)), pl.BlockSpec((1, d), lambda i: (0, 0))],
        out_specs=pl.BlockSpec((ts, d), lambda i: (i, 0)),
        out_shape=jax.ShapeDtypeStruct((r, d), BF16),
        compiler_params=_params(("parallel",)),
    )(x, g)


def _post_loss(out, x, tgt, g_post, ts=256):
    n = S // ts

    def body(o_ref, x_ref, t_ref, g_ref, sq_ref, dy_ref, do_ref, dg_ref):
        i = pl.program_id(0)

        @pl.when(i == 0)
        def _():
            sq_ref[...] = jnp.zeros_like(sq_ref)
            dg_ref[...] = jnp.zeros_like(dg_ref)

        ov = o_ref[...]
        g = g_ref[...]
        inv = lax.rsqrt(jnp.mean(ov * ov, axis=-1, keepdims=True) + EPS)
        on = ov * inv
        err = x_ref[...] + on * g - t_ref[...]
        sq_ref[...] += jnp.sum(err * err)
        dy = err * (1.0 / D)
        dy_ref[...] = dy
        dg_ref[...] += jnp.sum(dy * on, axis=0, keepdims=True)
        don = dy * g
        do_ref[...] = (inv * (don - on * jnp.mean(don * on, axis=-1, keepdims=True))).astype(BF16)

    row = pl.BlockSpec((ts, D), lambda i: (i, 0))
    vec = pl.BlockSpec((1, D), lambda i: (0, 0))
    return pl.pallas_call(
        body,
        name="post_loss",
        grid=(n,),
        in_specs=[row, row, row, vec],
        out_specs=[pl.BlockSpec((8, LANE), lambda i: (0, 0)), row, row, vec],
        out_shape=[
            jax.ShapeDtypeStruct((8, LANE), F32),
            jax.ShapeDtypeStruct((S, D), F32),
            jax.ShapeDtypeStruct((S, D), BF16),
            jax.ShapeDtypeStruct((1, D), F32),
        ],
        compiler_params=_params(("arbitrary",)),
    )(out, x, tgt, g_post)


def _pre_bwd(dh, x, dy, g_pre, ts=256):
    n = S // ts

    def body(dh_ref, x_ref, dy_ref, g_ref, gx_ref, dg_ref):
        i = pl.program_id(0)

        @pl.when(i == 0)
        def _():
            dg_ref[...] = jnp.zeros_like(dg_ref)

        xv = x_ref[...]
        dhv = dh_ref[...]
        inv = lax.rsqrt(jnp.mean(xv * xv, axis=-1, keepdims=True) + EPS)
        xn = xv * inv
        dg_ref[...] += jnp.sum(dhv * xn, axis=0, keepdims=True)
        dxn = dhv * g_ref[...]
        gx_ref[...] = dy_ref[...] + inv * (dxn - xn * jnp.mean(dxn * xn, axis=-1, keepdims=True))

    row = pl.BlockSpec((ts, D), lambda i: (i, 0))
    vec = pl.BlockSpec((1, D), lambda i: (0, 0))
    return pl.pallas_call(
        body,
        name="pre_bwd",
        grid=(n,),
        in_specs=[row, row, row, vec],
        out_specs=[row, vec],
        out_shape=[jax.ShapeDtypeStruct((S, D), F32), jax.ShapeDtypeStruct((1, D), F32)],
        compiler_params=_params(("arbitrary",)),
    )(dh, x, dy, g_pre)


def _memnorm_bwd(dmemn, mem):
    def body(d_ref, m_ref, dg_ref):
        mv = m_ref[...]
        inv = lax.rsqrt(jnp.mean(mv * mv, axis=-1, keepdims=True) + EPS)
        dg_ref[...] = jnp.sum(d_ref[...] * mv * inv, axis=0, keepdims=True)

    return pl.pallas_call(
        body,
        name="memnorm_bwd",
        out_shape=jax.ShapeDtypeStruct((1, D), F32),
        compiler_params=_params(),
    )(dmemn, mem)


T_RNN = 256


def _neg_expm1(z):
    poly = -z * (1.0 + z * (0.5 + z * (1.0 / 6 + z * (1.0 / 24 + z * (1.0 / 120 + z * (1.0 / 720))))))
    return jnp.where(z > -0.1, poly, 1.0 - jnp.exp(z))


def _softplus_neg(lam):
    return jnp.maximum(-lam, 0.0) + jnp.log1p(jnp.exp(-jnp.abs(lam)))


def _rnn_gates(conv, wa_ref, ba, wx_ref, bx, lam, first_row):
    cbf = conv.astype(BF16)
    ga, gx = [], []
    for n in range(RNN_BLOCKS):
        c_n = cbf[:, n * LANE:(n + 1) * LANE]
        ga.append(jnp.dot(c_n, wa_ref[n], preferred_element_type=F32))
        gx.append(jnp.dot(c_n, wx_ref[n], preferred_element_type=F32))
    gate_r = _sigmoid(jnp.concatenate(ga, axis=1) + ba)
    gate_i = _sigmoid(jnp.concatenate(gx, axis=1) + bx)
    sp = _softplus_neg(lam)
    log_a = -LRU_C * gate_r * sp
    a = jnp.exp(log_a)
    mult_raw = jnp.sqrt(_neg_expm1(2.0 * log_a))
    mult = jnp.where(first_row, 1.0, mult_raw)
    return cbf, gate_r, gate_i, sp, a, mult_raw, mult


def _rglru_fwd(p_a, conv_w, conv_b, wa, ba, wx, bx, lam):
    t = T_RNN
    n = S // t

    def body(xr_ref, g_ref, cw_ref, cb_ref, wa_ref, ba_ref, wx_ref, bx_ref, lam_ref,
             y_ref, h_ref, xp_s, hcar, a_s, b_s):
        i = pl.program_id(0)

        @pl.when(i == 0)
        def _():
            xp_s[0:8, :] = jnp.zeros((8, D_RNN), F32)
            hcar[...] = jnp.zeros_like(hcar)

        @pl.when(i > 0)
        def _():
            xp_s[0:8, :] = xp_s[t:t + 8, :]

        xp_s[8:8 + t, :] = xr_ref[...]
        conv = cb_ref[...]
        for k in range(CONV_W):
            conv = conv + cw_ref[k:k + 1, :] * xp_s[8 - k:8 - k + t, :]
        rows = i * t + lax.broadcasted_iota(jnp.int32, (t, 1), 0)
        _, _, gate_i, _, a, _, mult = _rnn_gates(
            conv, wa_ref, ba_ref[...], wx_ref, bx_ref[...], lam_ref[...], rows == 0)
        a_s[...] = a
        b_s[...] = mult * gate_i * conv

        def step(tt, h):
            h = a_s[pl.ds(tt, 1), :] * h + b_s[pl.ds(tt, 1), :]
            h_ref[pl.ds(tt, 1), :] = h
            return h

        hcar[...] = lax.fori_loop(0, t, step, hcar[...], unroll=8)
        g = g_ref[...]
        y_ref[...] = (h_ref[...] * (g * _sigmoid(g))).astype(BF16)

    blk = lambda c: pl.BlockSpec((t, D_RNN), lambda i: (i, c))
    full = lambda shape: pl.BlockSpec(shape, lambda i: (0,) * len(shape))
    return pl.pallas_call(
        body,
        name="rglru_fwd",
        grid=(n,),
        in_specs=[blk(0), blk(1), full((CONV_W, D_RNN)), full((1, D_RNN)),
                  full((RNN_BLOCKS, LANE, LANE)), full((1, D_RNN)),
                  full((RNN_BLOCKS, LANE, LANE)), full((1, D_RNN)), full((1, D_RNN))],
        out_specs=[blk(0), blk(0)],
        out_shape=[jax.ShapeDtypeStruct((S, D_RNN), BF16), jax.ShapeDtypeStruct((S, D_RNN), F32)],
        scratch_shapes=[pltpu.VMEM((t + 8, D_RNN), F32), pltpu.VMEM((1, D_RNN), F32),
                        pltpu.VMEM((t, D_RNN), F32), pltpu.VMEM((t, D_RNN), F32)],
        compiler_params=_params(("arbitrary",)),
    )(p_a, p_a, conv_w, conv_b, wa, ba, wx, bx, lam)


def _rglru_bwd(dy, p_a, hseq, conv_w, conv_b, wa, ba, wx, bx, lam):
    t = T_RNN
    n = S // t
    rb = t // 8

    def body(dy_ref, xr_ref, g_ref, h_ref, xrp_ref, hp_ref, cw_ref, cb_ref, wa_ref, ba_ref, wx_ref, bx_ref, lam_ref,
             dp_ref, dcw_ref, dcb_ref, dwa_ref, dba_ref, dwx_ref, dbx_ref, dlam_ref,
             xp_s, hp_s, dxp_s, lamcar, a_s, dh_s, lam_s):
        i = pl.program_id(0)
        r = n - 1 - i

        @pl.when(i == 0)
        def _():
            for ref in (dcw_ref, dcb_ref, dwa_ref, dba_ref, dwx_ref, dbx_ref, dlam_ref, lamcar):
                ref[...] = jnp.zeros_like(ref)
            dxp_s[t:t + 8, :] = jnp.zeros((8, D_RNN), F32)

        @pl.when(i > 0)
        def _():
            dxp_s[t:t + 8, :] = dxp_s[0:8, :]

        has_prev = r > 0
        xp_s[0:8, :] = jnp.where(has_prev, xrp_ref[...], 0.0)
        xp_s[8:8 + t, :] = xr_ref[...]
        hp_s[0:8, :] = jnp.where(has_prev, hp_ref[...], 0.0)
        hp_s[8:8 + t, :] = h_ref[...]
        xs = [xp_s[8 - k:8 - k + t, :] for k in range(CONV_W)]
        conv = cb_ref[...]
        for k in range(CONV_W):
            conv = conv + cw_ref[k:k + 1, :] * xs[k]
        rows = r * t + lax.broadcasted_iota(jnp.int32, (t, 1), 0)
        first = rows == 0
        lam_p = lam_ref[...]
        cbf, gate_r, gate_i, sp, a, mult_raw, mult = _rnn_gates(
            conv, wa_ref, ba_ref[...], wx_ref, bx_ref[...], lam_p, first)

        g = g_ref[...]
        sg = _sigmoid(g)
        dyv = dy_ref[...]
        a_s[...] = a
        dh_s[...] = dyv * (g * sg)
        dg = dyv * h_ref[...] * (sg * (1.0 + g * (1.0 - sg)))

        def step(jj, car):
            tt = t - 1 - jj
            lm = dh_s[pl.ds(tt, 1), :] + car
            lam_s[pl.ds(tt, 1), :] = lm
            return a_s[pl.ds(tt, 1), :] * lm

        lamcar[...] = lax.fori_loop(0, t, step, lamcar[...], unroll=8)
        db = lam_s[...]
        da = db * hp_s[7:7 + t, :]
        dmult = db * gate_i * conv
        dgate_i = db * mult * conv
        dconv = db * mult * gate_i
        dlog_a = da * a + jnp.where(first, 0.0, dmult * (-(a * a) / mult_raw))
        dgate_r = dlog_a * (-LRU_C * sp)
        dsp = jnp.sum(dlog_a * (-LRU_C * gate_r), axis=0, keepdims=True)
        dlam_ref[...] += dsp * (-_sigmoid(-lam_p))
        dga = dgate_r * gate_r * (1.0 - gate_r)
        dgx = dgate_i * gate_i * (1.0 - gate_i)
        dba_ref[...] += jnp.sum(dga, axis=0, keepdims=True)
        dbx_ref[...] += jnp.sum(dgx, axis=0, keepdims=True)
        dga16, dgx16 = dga.astype(BF16), dgx.astype(BF16)
        back = []
        for nb in range(RNN_BLOCKS):
            sl = slice(nb * LANE, (nb + 1) * LANE)
            dwa_ref[nb] += lax.dot_general(cbf[:, sl], dga16[:, sl], _DIMS["tn"], preferred_element_type=F32)
            dwx_ref[nb] += lax.dot_general(cbf[:, sl], dgx16[:, sl], _DIMS["tn"], preferred_element_type=F32)
            back.append(lax.dot_general(dga16[:, sl], wa_ref[nb], _DIMS["nt"], preferred_element_type=F32)
                        + lax.dot_general(dgx16[:, sl], wx_ref[nb], _DIMS["nt"], preferred_element_type=F32))
        dconv = dconv + jnp.concatenate(back, axis=1)
        dcb_ref[...] += jnp.sum(dconv, axis=0, keepdims=True)
        for k in range(CONV_W):
            dcw_ref[k:k + 1, :] += jnp.sum(dconv * xs[k], axis=0, keepdims=True)
        dxp_s[0:t, :] = dconv
        dxr = cw_ref[0:1, :] * dconv
        for k in range(1, CONV_W):
            dxr = dxr + cw_ref[k:k + 1, :] * dxp_s[k:k + t, :]
        dp_ref[:, 0:D_RNN] = dxr.astype(BF16)
        dp_ref[:, D_RNN:2 * D_RNN] = dg.astype(BF16)

    blk = lambda c: pl.BlockSpec((t, D_RNN), lambda i: (n - 1 - i, c))
    prev8 = pl.BlockSpec((8, D_RNN), lambda i: (jnp.maximum((n - 1 - i) * rb - 1, 0), 0))
    full = lambda shape: pl.BlockSpec(shape, lambda i: (0,) * len(shape))
    vec = full((1, D_RNN))
    mat = full((RNN_BLOCKS, LANE, LANE))
    return pl.pallas_call(
        body,
        name="rglru_bwd",
        grid=(n,),
        in_specs=[blk(0), blk(0), blk(1), blk(0), prev8, prev8,
                  full((CONV_W, D_RNN)), vec, mat, vec, mat, vec, vec],
        out_specs=[pl.BlockSpec((t, 2 * D_RNN), lambda i: (n - 1 - i, 0)),
                   full((CONV_W, D_RNN)), vec, mat, vec, mat, vec, vec],
        out_shape=[jax.ShapeDtypeStruct((S, 2 * D_RNN), BF16),
                   jax.ShapeDtypeStruct((CONV_W, D_RNN), F32), jax.ShapeDtypeStruct((1, D_RNN), F32),
                   jax.ShapeDtypeStruct((RNN_BLOCKS, LANE, LANE), F32), jax.ShapeDtypeStruct((1, D_RNN), F32),
                   jax.ShapeDtypeStruct((RNN_BLOCKS, LANE, LANE), F32), jax.ShapeDtypeStruct((1, D_RNN), F32),
                   jax.ShapeDtypeStruct((1, D_RNN), F32)],
        scratch_shapes=[pltpu.VMEM((t + 8, D_RNN), F32), pltpu.VMEM((t + 8, D_RNN), F32),
                        pltpu.VMEM((t + 8, D_RNN), F32), pltpu.VMEM((1, D_RNN), F32),
                        pltpu.VMEM((t, D_RNN), F32), pltpu.VMEM((t, D_RNN), F32), pltpu.VMEM((t, D_RNN), F32)],
        compiler_params=_params(("arbitrary",)),
    )(dy, p_a, p_a, hseq, p_a, hseq, conv_w, conv_b, wa, ba, wx, bx, lam)


QB = WINDOW
KB2 = 2 * WINDOW
N_QB = S // QB
N_PAIR = SWA_HEADS // 2


def _swa_keys(kvc_ref, kvp_ref):
    kk = jnp.concatenate([kvp_ref[:, 0:LANE], kvc_ref[:, 0:LANE]], axis=0)
    vv = jnp.concatenate([kvp_ref[:, LANE:2 * LANE], kvc_ref[:, LANE:2 * LANE]], axis=0)
    lo = lax.broadcasted_iota(jnp.int32, (1, LANE), 1) < SWA_HD
    kk_sw, vv_sw = pltpu.roll(kk, SWA_HD, 1), pltpu.roll(vv, SWA_HD, 1)
    kd = [jnp.where(lo, kk, kk_sw).astype(BF16), jnp.where(lo, kk_sw, kk).astype(BF16)]
    vd = [jnp.where(lo, vv, vv_sw).astype(BF16), jnp.where(lo, vv_sw, vv).astype(BF16)]
    return lo, kd, vd


def _swa_valid(n):
    qi = lax.broadcasted_iota(jnp.int32, (QB, KB2), 0)
    kj = lax.broadcasted_iota(jnp.int32, (QB, KB2), 1)
    dist = qi + WINDOW - kj
    return (dist >= 0) & (dist < WINDOW) & ((n > 0) | (kj >= WINDOW))


def _swa_probs(qh16, kd, bias, sink, valid):
    lg = lax.dot_general(qh16, kd, _DIMS["nt"], preferred_element_type=F32) * (SWA_HD ** -0.5) + bias
    lg = jnp.where(valid, lg, NEG_INF)
    m = jnp.maximum(jnp.max(lg, axis=-1, keepdims=True), sink)
    p = jnp.exp(lg - m)
    es = jnp.exp(sink - m)
    den = jnp.sum(p, axis=-1, keepdims=True) + es
    return p / den, es / den


def _swa_specs():
    q = pl.BlockSpec((QB, D_RNN), lambda n: (n, 0))
    g = pl.BlockSpec((QB, D_RNN), lambda n: (n, 1))
    kvc = pl.BlockSpec((QB, 2 * LANE), lambda n: (n, 8))
    kvp = pl.BlockSpec((QB, 2 * LANE), lambda n: (jnp.maximum(n - 1, 0), 8))
    bias = pl.BlockSpec((SWA_HEADS, QB, KB2), lambda n: (0, 0, 0))
    sinks = pl.BlockSpec(memory_space=pltpu.SMEM)
    return q, g, kvc, kvp, bias, sinks


def _swa_fwd(p_b, bias_t, sinks):
    def body(q_ref, g_ref, kvc_ref, kvp_ref, bias_ref, sink_ref, y_ref, o_ref):
        n = pl.program_id(0)
        lo, kd, vd = _swa_keys(kvc_ref, kvp_ref)
        valid = _swa_valid(n)
        for hp in range(N_PAIR):
            sl = slice(hp * LANE, (hp + 1) * LANE)
            kvh = hp // (N_PAIR // 2)
            q = q_ref[:, sl]
            outs = []
            for j in range(2):
                mh = lo if j == 0 else jnp.logical_not(lo)
                qh16 = jnp.where(mh, q, 0.0).astype(BF16)
                probs, _ = _swa_probs(qh16, kd[kvh], bias_ref[2 * hp + j], sink_ref[2 * hp + j], valid)
                outs.append(jnp.dot(probs.astype(BF16), vd[kvh], preferred_element_type=F32))
            o = jnp.where(lo, outs[0], outs[1])
            o_ref[:, sl] = o
            g = g_ref[:, sl]
            y_ref[:, sl] = (o * (g * _sigmoid(g))).astype(BF16)

    q, g, kvc, kvp, bias, sinks_spec = _swa_specs()
    out = pl.BlockSpec((QB, D_RNN), lambda n: (n, 0))
    return pl.pallas_call(
        body,
        name="swa_fwd",
        grid=(N_QB,),
        in_specs=[q, g, kvc, kvp, bias, sinks_spec],
        out_specs=[out, out],
        out_shape=[jax.ShapeDtypeStruct((S, D_RNN), BF16), jax.ShapeDtypeStruct((S, D_RNN), F32)],
        compiler_params=_params(("parallel",)),
    )(p_b, p_b, p_b, p_b, bias_t, sinks)


def _swa_bwd(dy, p_b, o_swa, bias_t, sinks):
    def body(dy_ref, q_ref, g_ref, kvc_ref, kvp_ref, o_ref, bias_ref, sink_ref,
             dp_ref, dk_ref, dv_ref, dbias_ref, dsink_ref):
        n = pl.program_id(0)

        @pl.when(n == 0)
        def _():
            for ref in (dk_ref, dv_ref, dbias_ref, dsink_ref):
                ref[...] = jnp.zeros_like(ref)

        lo, kd, vd = _swa_keys(kvc_ref, kvp_ref)
        hi = jnp.logical_not(lo)
        valid = _swa_valid(n)
        dk_blk = jnp.zeros((KB2, LANE), F32)
        dv_blk = jnp.zeros((KB2, LANE), F32)
        for kvh in range(2):
            dk_pair = jnp.zeros((KB2, LANE), F32)
            dv_pair = jnp.zeros((KB2, LANE), F32)
            for hp in range(kvh * (N_PAIR // 2), (kvh + 1) * (N_PAIR // 2)):
                sl = slice(hp * LANE, (hp + 1) * LANE)
                q = q_ref[:, sl]
                g = g_ref[:, sl]
                o = o_ref[:, sl]
                dyv = dy_ref[:, sl]
                sg = _sigmoid(g)
                do = dyv * (g * sg)
                dp_ref[:, D_RNN + hp * LANE:D_RNN + (hp + 1) * LANE] = (
                    dyv * o * (sg * (1.0 + g * (1.0 - sg)))).astype(BF16)
                dqs = []
                for j in range(2):
                    h = 2 * hp + j
                    mh = lo if j == 0 else hi
                    qh16 = jnp.where(mh, q, 0.0).astype(BF16)
                    sink = sink_ref[h]
                    probs, psink = _swa_probs(qh16, kd[kvh], bias_ref[h], sink, valid)
                    doh = jnp.where(mh, do, 0.0)
                    doh16 = doh.astype(BF16)
                    delta = jnp.sum(doh * o, axis=-1, keepdims=True)
                    dpr = lax.dot_general(doh16, vd[kvh], _DIMS["nt"], preferred_element_type=F32)
                    ds = probs * (dpr - delta)
                    dbias_ref[h] += ds
                    dsink_ref[h:h + 1, :] += jnp.zeros((1, LANE), F32) - jnp.sum(psink * delta)
                    ds16 = (ds * (SWA_HD ** -0.5)).astype(BF16)
                    dqs.append(jnp.dot(ds16, kd[kvh], preferred_element_type=F32))
                    dk_pair = dk_pair + lax.dot_general(ds16, qh16, _DIMS["tn"], preferred_element_type=F32)
                    dv_pair = dv_pair + lax.dot_general(probs.astype(BF16), doh16, _DIMS["tn"],
                                                        preferred_element_type=F32)
                dp_ref[:, sl] = jnp.where(lo, dqs[0], dqs[1]).astype(BF16)
            keep = lo if kvh == 0 else hi
            dk_blk = dk_blk + jnp.where(keep, dk_pair + pltpu.roll(dk_pair, SWA_HD, 1), 0.0)
            dv_blk = dv_blk + jnp.where(keep, dv_pair + pltpu.roll(dv_pair, SWA_HD, 1), 0.0)

        cur = pl.ds(pl.multiple_of(n * QB, QB), QB)
        dk_ref[cur, :] += dk_blk[QB:KB2]
        dv_ref[cur, :] += dv_blk[QB:KB2]

        @pl.when(n > 0)
        def _():
            prev = pl.ds(pl.multiple_of((n - 1) * QB, QB), QB)
            dk_ref[prev, :] += dk_blk[0:QB]
            dv_ref[prev, :] += dv_blk[0:QB]

    q, g, kvc, kvp, bias, sinks_spec = _swa_specs()
    row = pl.BlockSpec((QB, D_RNN), lambda n: (n, 0))
    acc = pl.BlockSpec((S, LANE), lambda n: (0, 0))
    return pl.pallas_call(
        body,
        name="swa_bwd",
        grid=(N_QB,),
        in_specs=[row, q, g, kvc, kvp, row, bias, sinks_spec],
        out_specs=[pl.BlockSpec((QB, 2 * D_RNN), lambda n: (n, 0)), acc, acc, bias,
                   pl.BlockSpec((SWA_HEADS, LANE), lambda n: (0, 0))],
        out_shape=[jax.ShapeDtypeStruct((S, GROUP_TILES["B"] * LANE), BF16),
                   jax.ShapeDtypeStruct((S, LANE), F32), jax.ShapeDtypeStruct((S, LANE), F32),
                   jax.ShapeDtypeStruct((SWA_HEADS, QB, KB2), F32),
                   jax.ShapeDtypeStruct((SWA_HEADS, LANE), F32)],
        compiler_params=_params(("arbitrary",)),
    )(dy, p_b, p_b, p_b, p_b, o_swa, bias_t, sinks)


def _swa_pack(dp_b, dk, dv, ts=512):
    def body(_, dk_ref, dv_ref, o_ref):
        o_ref[:, 0:LANE] = dk_ref[...].astype(BF16)
        o_ref[:, LANE:2 * LANE] = dv_ref[...].astype(BF16)

    tile = pl.BlockSpec((ts, LANE), lambda i: (i, 0))
    return pl.pallas_call(
        body,
        name="swa_pack",
        grid=(S // ts,),
        in_specs=[pl.BlockSpec(memory_space=pl.ANY), tile, tile],
        out_specs=pl.BlockSpec((ts, 2 * LANE), lambda i: (i, 8)),
        out_shape=jax.ShapeDtypeStruct(dp_b.shape, dp_b.dtype),
        input_output_aliases={0: 0},
        compiler_params=_params(("parallel",)),
    )(dp_b, dk, dv)


def _split3(v):
    a = v.astype(BF16)
    r = v - a.astype(F32)
    b = r.astype(BF16)
    c = (r - b.astype(F32)).astype(BF16)
    return a, b, c


def _relbias_grad(dbias_flat, onehot_t):
    def body(d_ref, e_ref, o_ref):
        e = e_ref[...]
        acc = jnp.zeros((SWA_HEADS, REL_BUCKETS), F32)
        for term in _split3(d_ref[...]):
            acc = acc + lax.dot_general(term, e, _DIMS["nt"], preferred_element_type=F32)
        o_ref[...] = acc

    return pl.pallas_call(
        body,
        name="relbias_grad",
        out_shape=jax.ShapeDtypeStruct((SWA_HEADS, REL_BUCKETS), F32),
        compiler_params=_params(),
    )(dbias_flat, onehot_t)


TS_MEM = 512


def _mem_probs(q16, mk):
    lg = lax.dot_general(q16, mk, _DIMS["nt"], preferred_element_type=F32) * (MEM_HD ** -0.5)
    p = jnp.exp(lg - jnp.max(lg, axis=-1, keepdims=True))
    return p / jnp.sum(p, axis=-1, keepdims=True)


def _mem_fwd(p_c, mkv):
    def body(q_ref, g_ref, mkv_ref, y_ref, o_ref):
        for hm in range(MEM_HEADS):
            sl = slice(hm * MEM_HD, (hm + 1) * MEM_HD)
            probs = _mem_probs(q_ref[:, sl].astype(BF16), mkv_ref[:, sl])
            o = jnp.dot(probs.astype(BF16), mkv_ref[:, D_RNN + hm * MEM_HD:D_RNN + (hm + 1) * MEM_HD],
                        preferred_element_type=F32)
            o_ref[:, sl] = o
            g = g_ref[:, sl]
            y_ref[:, sl] = (o * (g * _sigmoid(g))).astype(BF16)

    blk = lambda c: pl.BlockSpec((TS_MEM, D_RNN), lambda i: (i, c))
    return pl.pallas_call(
        body,
        name="mem_fwd",
        grid=(S // TS_MEM,),
        in_specs=[blk(0), blk(1), pl.BlockSpec((MEM, 2 * D_RNN), lambda i: (0, 0))],
        out_specs=[blk(0), blk(0)],
        out_shape=[jax.ShapeDtypeStruct((S, D_RNN), BF16), jax.ShapeDtypeStruct((S, D_RNN), F32)],
        compiler_params=_params(("parallel",)),
    )(p_c, p_c, mkv)


def _mem_bwd(dy, p_c, o_mem, mkv):
    def body(dy_ref, q_ref, g_ref, o_ref, mkv_ref, dp_ref, dmkv_ref):
        @pl.when(pl.program_id(0) == 0)
        def _():
            dmkv_ref[...] = jnp.zeros_like(dmkv_ref)

        for hm in range(MEM_HEADS):
            sl = slice(hm * MEM_HD, (hm + 1) * MEM_HD)
            sv = slice(D_RNN + hm * MEM_HD, D_RNN + (hm + 1) * MEM_HD)
            q16 = q_ref[:, sl].astype(BF16)
            mk, mv = mkv_ref[:, sl], mkv_ref[:, sv]
            probs = _mem_probs(q16, mk)
            g, o, dyv = g_ref[:, sl], o_ref[:, sl], dy_ref[:, sl]
            sg = _sigmoid(g)
            do = dyv * (g * sg)
            dp_ref[:, sv] = (dyv * o * (sg * (1.0 + g * (1.0 - sg)))).astype(BF16)
            do16 = do.astype(BF16)
            delta = jnp.sum(do * o, axis=-1, keepdims=True)
            dpr = lax.dot_general(do16, mv, _DIMS["nt"], preferred_element_type=F32)
            ds16 = (probs * (dpr - delta) * (MEM_HD ** -0.5)).astype(BF16)
            dp_ref[:, sl] = jnp.dot(ds16, mk, preferred_element_type=F32).astype(BF16)
            dmkv_ref[:, sl] += lax.dot_general(ds16, q16, _DIMS["tn"], preferred_element_type=F32)
            dmkv_ref[:, sv] += lax.dot_general(probs.astype(BF16), do16, _DIMS["tn"], preferred_element_type=F32)

    blk = lambda c: pl.BlockSpec((TS_MEM, D_RNN), lambda i: (i, c))
    kv = pl.BlockSpec((MEM, 2 * D_RNN), lambda i: (0, 0))
    return pl.pallas_call(
        body,
        name="mem_bwd",
        grid=(S // TS_MEM,),
        in_specs=[blk(0), blk(0), blk(1), blk(0), kv],
        out_specs=[pl.BlockSpec((TS_MEM, 2 * D_RNN), lambda i: (i, 0)), kv],
        out_shape=[jax.ShapeDtypeStruct((S, 2 * D_RNN), BF16), jax.ShapeDtypeStruct((MEM, 2 * D_RNN), F32)],
        compiler_params=_params(("arbitrary",)),
    )(dy, p_c, p_c, o_mem, mkv)


TS_MRG = 512
TD_MRG = 512
N_DBLK = D // TD_MRG


def _merge_fwd(z, p_d):
    def body(z0, z1, z2, g0, g1, g2, o_ref):
        o_ref[...] = (_sigmoid(g0[...]) * z0[...] + _sigmoid(g1[...]) * z1[...]
                      + _sigmoid(g2[...]) * z2[...]).astype(BF16)

    blk = pl.BlockSpec((TS_MRG, TD_MRG), lambda i, d: (i, d))
    gate = lambda b: pl.BlockSpec((TS_MRG, TD_MRG), lambda i, d: (i, b * N_DBLK + d))
    return pl.pallas_call(
        body,
        name="merge_fwd",
        grid=(S // TS_MRG, N_DBLK),
        in_specs=[blk, blk, blk, gate(0), gate(1), gate(2)],
        out_specs=blk,
        out_shape=jax.ShapeDtypeStruct((S, D), BF16),
        compiler_params=_params(("parallel", "parallel")),
    )(z[0], z[1], z[2], p_d, p_d, p_d)


def _merge_bwd(dmerged, z_b, p_d, b, dp_d):
    def body(dm_ref, z_ref, g_ref, *refs):
        dz_ref, dg_ref = refs[-2], refs[-1]
        sg = _sigmoid(g_ref[...])
        dm = dm_ref[...]
        dz_ref[...] = (dm * sg).astype(BF16)
        dg_ref[...] = (dm * z_ref[...] * sg * (1.0 - sg)).astype(BF16)

    blk = pl.BlockSpec((TS_MRG, TD_MRG), lambda i, d: (i, d))
    gate = pl.BlockSpec((TS_MRG, TD_MRG), lambda i, d: (i, b * N_DBLK + d))
    in_specs = [blk, blk, gate]
    args = [dmerged, z_b, p_d]
    aliases = {}
    if dp_d is not None:
        in_specs.append(pl.BlockSpec(memory_space=pl.ANY))
        args.append(dp_d)
        aliases = {3: 1}
    return pl.pallas_call(
        body,
        name=f"merge_bwd{b}",
        grid=(S // TS_MRG, N_DBLK),
        in_specs=in_specs,
        out_specs=[blk, gate],
        out_shape=[jax.ShapeDtypeStruct((S, D), BF16),
                   jax.ShapeDtypeStruct((S, GROUP_TILES["D"] * LANE), BF16)],
        input_output_aliases=aliases,
        compiler_params=_params(("parallel", "parallel")),
    )(*args)


def _bucket_table():
    import numpy as np
    qi = np.arange(QB)[:, None]
    kj = np.arange(KB2)[None, :]
    n = np.maximum(qi + WINDOW - kj, 0)
    max_exact = REL_BUCKETS // 2
    ratio = np.log(np.maximum(n, 1).astype(np.float32) / max_exact) / np.float32(math.log(REL_MAX_DIST / max_exact))
    large = np.minimum(max_exact + (ratio * (REL_BUCKETS - max_exact)).astype(np.int32), REL_BUCKETS - 1)
    bucket = np.where(n < max_exact, n, large).reshape(1, QB * KB2)
    return (bucket == np.arange(REL_BUCKETS)[:, None]).astype(np.float32)


def _bias_expand(rel_bias_t, onehot_t):
    def body(r_ref, e_ref, o_ref):
        e = e_ref[...]
        acc = jnp.zeros((SWA_HEADS, QB * KB2), F32)
        for term in _split3(r_ref[...]):
            acc = acc + jnp.dot(term, e, preferred_element_type=F32)
        o_ref[...] = acc

    return pl.pallas_call(
        body,
        name="bias_expand",
        out_shape=jax.ShapeDtypeStruct((SWA_HEADS, QB * KB2), F32),
        compiler_params=_params(),
    )(rel_bias_t, onehot_t)


PROJ_TN = {"A": 1024, "B": 1152, "C": 1024, "D": 1536}


def _local_step(x, mem, tgt, sp, fetch, emit):
    onehot_t = jnp.asarray(_bucket_table(), BF16)
    bias_t = _bias_expand(sp["rel_bias"].T, onehot_t).reshape(SWA_HEADS, QB, KB2)
    sinks = sp["swa_sinks"].reshape(SWA_HEADS)
    wa16, wx16 = sp["w_rg_a"].astype(BF16), sp["w_rg_x"].astype(BF16)
    rnn = (sp["conv_w"], sp["conv_b"], wa16, sp["b_rg_a"], wx16, sp["b_rg_x"], sp["lru_lambda"])

    h = _rms_fwd(x, sp["pre_norm_g"], "rms_pre")
    memn = _rms_fwd(mem, sp["mem_norm_g"], "rms_mem")
    w_grp, p = {}, {}

    def project(g, after):
        (w_grp[g],) = fetch((g,), after)
        p[g] = _mm(h, w_grp[g], "nt", F32, 1024, PROJ_TN[g], D, f"proj_{g}")

    project("A", h)
    y_rg, hseq = _rglru_fwd(p["A"], *rnn)
    project("B", y_rg)
    y_swa, o_swa = _swa_fwd(p["B"], bias_t, sinks)
    project("C", y_swa)
    (wmk,) = fetch(("mk",), p["C"])
    mkv = _mm(memn, wmk, "nn", BF16, MEM, 1024, D, "mkv")
    y_mem, o_mem = _mem_fwd(p["C"], mkv)
    ys = (y_rg, y_swa, y_mem)
    wbr = fetch(("br0", "br1", "br2"), y_mem)
    z = [_mm(ys[b], wbr[b], "nn", F32, 1024, 1024, D_RNN, f"branch_out{b}") for b in range(3)]
    project("D", z[2])
    merged = _merge_fwd(z, p["D"])
    (wout,) = fetch(("out",), merged)
    out = _mm(merged, wout, "nn", F32, 1024, 1024, D, "out_proj")
    sq, dy, dout, d_post = _post_loss(out, x, tgt, sp["post_norm_g"])

    tok = emit({"out": _mm(merged, dout, "tn", BF16, 1024, 1024, S, "d_wout")})
    dmerged = _mm(dout, wout, "nt", F32, 1024, 1024, D, "d_merged", after=tok)
    dz, dp_d = [], None
    for b in range(3):
        dz_b, dp_d = _merge_bwd(dmerged, z[b], p["D"], b, dp_d)
        dz.append(dz_b)
    tok = emit({f"br{b}": _mm(ys[b], dz[b], "tn", BF16, 1024, 1024, S, f"d_wbr{b}") for b in range(3)})
    tok = emit({"D": _mm(dp_d, h, "tn", BF16, PROJ_TN["D"], 1024, S, "d_win_D", after=tok)})
    dy_mem = _mm(dz[2], wbr[2], "nt", F32, 1024, 1024, D, "d_branch2", after=tok)
    dp_c, dmkv = _mem_bwd(dy_mem, p["C"], o_mem, mkv)
    dmkv16 = dmkv.astype(BF16)
    d_win = lambda g, dp_g, after=None: _mm(dp_g, h, "tn", BF16, PROJ_TN[g], 1024, S, f"d_win_{g}", after=after)
    tok = emit({"mk": _mm(memn, dmkv16, "tn", BF16, 1024, 1024, MEM, "d_wmk"), "C": d_win("C", dp_c)})
    dmemn = _mm(dmkv16, wmk, "nt", F32, MEM, 1024, D, "d_memn", after=tok)
    d_memg = _memnorm_bwd(dmemn, mem)
    dy_rg = _mm(dz[0], wbr[0], "nt", F32, 1024, 1024, D, "d_branch0", after=dmemn)
    dp_a, d_cw, d_cb, d_wa, d_ba, d_wx, d_bx, d_lam = _rglru_bwd(dy_rg, p["A"], hseq, *rnn)
    tok = emit({"A": d_win("A", dp_a)})
    dy_swa = _mm(dz[1], wbr[1], "nt", F32, 1024, 1024, D, "d_branch1", after=tok)
    dp_b, dk, dv, d_bias, d_sink = _swa_bwd(dy_swa, p["B"], o_swa, bias_t, sinks)
    dp_b = _swa_pack(dp_b, dk, dv)
    d_rel = _relbias_grad(d_bias.reshape(SWA_HEADS, QB * KB2), onehot_t).T
    dp = {"A": dp_a, "B": dp_b, "C": dp_c, "D": dp_d}
    tok = emit({"B": d_win("B", dp_b)})
    dh = None
    for g in GROUPS:
        dh = _mm(dp[g], w_grp[g], "nn", F32, 1024, 1024, 2304 if g == "B" else 2048, f"d_h_{g}", acc=dh,
                 after=tok if g == "A" else None)
    grad_x, d_pre = _pre_bwd(dh, x, dy, sp["pre_norm_g"])

    d_small = {
        "pre_norm_g": d_pre, "post_norm_g": d_post, "mem_norm_g": d_memg, "conv_w": d_cw, "conv_b": d_cb,
        "w_rg_a": d_wa, "b_rg_a": d_ba, "w_rg_x": d_wx, "b_rg_x": d_bx, "lru_lambda": d_lam,
        "swa_sinks": d_sink[:, 0].reshape(1, SWA_HEADS), "rel_bias": d_rel,
    }
    return sq, grad_x, d_small


ANY = pl.BlockSpec(memory_space=pl.ANY)
SHARD_ROWS = D // N_CHIPS
GATHERED = {"A": (2048, D), "B": (2304, D), "C": (2048, D), "D": (6144, D), "mk": (D, D),
            "br0": (D_RNN, D), "br1": (D_RNN, D), "br2": (D_RNN, D), "out": (D, D)}
SHARD_SHAPES = {"win": (SHARD, D), "mk": (SHARD_ROWS, D), "br0": (D_RNN, SHARD_ROWS), "br1": (D_RNN, SHARD_ROWS),
                "br2": (D_RNN, SHARD_ROWS), "out": (SHARD_ROWS, D)}
SHARDS = tuple(SHARD_SHAPES)
HALF_AXIS = {"win": 1, "mk": 1, "br0": 0, "br1": 0, "br2": 0, "out": 1,
             "A": 1, "B": 1, "C": 1, "D": 1}


def _halved(shape, axis):
    return (shape[0] // 2, shape[1]) if axis == 0 else (shape[0], shape[1] // 2)


class Piece(NamedTuple):
    src: str
    dst: str
    rows: int
    sr0: int
    sc0: int
    dr0: int
    dc0: int
    ncols: int


def _pieces_of(jj):
    out = [Piece("win", g, n, r, 0, gr, 0, D) for r, n, g, gr in _shard_runs(jj)]
    out.append(Piece("mk", "mk", SHARD_ROWS, 0, 0, SHARD_ROWS * jj, 0, D))
    out += [Piece(f"br{b}", f"br{b}", D_RNN, 0, 0, 0, SHARD_ROWS * jj, SHARD_ROWS) for b in range(3)]
    out.append(Piece("out", "out", SHARD_ROWS, 0, 0, SHARD_ROWS * jj, 0, D))
    return out


def _half_rect(ref, p, side, which):
    r0, c0 = (p.sr0, p.sc0) if side == "src" else (p.dr0, p.dc0)
    if HALF_AXIS[p.src] == 1:
        return _rect(ref, r0, p.rows, c0 + which * (p.ncols // 2), p.ncols // 2)
    return _rect(ref, r0 + which * (p.rows // 2), p.rows // 2, c0, p.ncols)


def _rect_in_half(ref, p, side):
    r0, c0 = (p.sr0, p.sc0) if side == "src" else (p.dr0, p.dc0)
    if HALF_AXIS[p.src] == 1:
        return _rect(ref, r0, p.rows, 0, p.ncols // 2)
    return _rect(ref, 0, p.rows // 2, c0, p.ncols)


MAX_PIECES = max(len(_pieces_of(jj)) for jj in range(N_CHIPS))


def _rect(ref, r0, rows, c0, ncols):
    return ref.at[pl.ds(r0, rows), pl.ds(c0, ncols)]


def _position():
    x, y, c = lax.axis_index("x"), lax.axis_index("y"), lax.axis_index("c")
    return x, y, c, 2 * x + y


HBM = pl.BlockSpec(memory_space=pltpu.HBM)
SEM = pl.BlockSpec(memory_space=pltpu.SEMAPHORE)
EFFECT = pltpu.SideEffectType.DATAFLOW_SIDE_EFFECTING
N_SEM = MAX_PIECES * N_CHIPS
GATHER_STAGES = (("A",), ("B",), ("C",), ("mk",), ("br0", "br1", "br2"), ("D",), ("out",))


def _in_hbm(a):
    return pltpu.with_memory_space_constraint(a, pltpu.HBM)


def _stage_pieces(jj, stage):
    return [(i, p) for i, p in enumerate(_pieces_of(jj)) if p.dst in stage]


def _own_block_table(g):
    import numpy as np
    tbl = np.zeros((N_CHIPS, GATHERED[g][0] // HALF_TILE), np.int32)
    for jj in range(N_CHIPS):
        for r, n, grp, gr in _shard_runs(jj):
            if grp == g:
                for k in range(n // HALF_TILE):
                    tbl[jj, gr // HALF_TILE + k] = r // HALF_TILE + k
    return tbl


def _place_group(w_t, g, table):
    nb = GATHERED[g][0] // HALF_TILE

    def body(t_ref, x_ref, o_ref):
        o_ref[...] = x_ref[...].astype(BF16)

    return pl.pallas_call(
        body,
        name=f"place_{g}",
        grid_spec=pltpu.PrefetchScalarGridSpec(
            num_scalar_prefetch=1,
            grid=(nb,),
            in_specs=[pl.BlockSpec((HALF_TILE, D), lambda b, t: (t[b], 0))],
            out_specs=pl.BlockSpec((HALF_TILE, D), lambda b, t: (b, 0)),
        ),
        out_shape=jax.ShapeDtypeStruct(GATHERED[g], BF16),
        compiler_params=_params(("parallel",)),
    )(table, w_t)


def _place_shard(shard, name):
    rows, cols = shard.shape
    by_rows = HALF_AXIS[name] == 1

    def body(x_ref, o_ref):
        o_ref[...] = x_ref[...].astype(BF16)

    return pl.pallas_call(
        body,
        name=f"place_{name}",
        grid=(N_CHIPS,),
        in_specs=[pl.BlockSpec((rows, cols), lambda b: (0, 0))],
        out_specs=pl.BlockSpec((rows, cols), (lambda b: (b, 0)) if by_rows else (lambda b: (0, b))),
        out_shape=jax.ShapeDtypeStruct(GATHERED[name], BF16),
        compiler_params=_params(("parallel",)),
    )(shard)


def _gather_copy(arr, send_sems, recv_sems, c, jj, i, p, kk):
    rect = _half_rect(arr[p.dst], p, "dst", c)
    return pltpu.make_async_remote_copy(
        src_ref=rect, dst_ref=rect, send_sem=send_sems.at[i * N_CHIPS + kk],
        recv_sem=recv_sems.at[jj * MAX_PIECES + i], device_id=(kk // 2, kk % 2, c), device_id_type=MESH)


def _gather_start(arrays, after):
    stage = tuple(arrays)
    na = len(stage)

    def body(*refs):
        arr = dict(zip(stage, refs[:na]))
        send_sems, recv_sems = refs[na + 1], refs[na + 2]
        token = refs[-1]
        _, _, c, j = _position()
        for jj in range(N_CHIPS):
            @pl.when(j == jj)
            def _():
                for i, p in _stage_pieces(jj, stage):
                    for kk in range(N_CHIPS):
                        if kk != jj:
                            _gather_copy(arr, send_sems, recv_sems, c, jj, i, p, kk).start()
        token[...] = jnp.zeros_like(token)

    outs = pl.pallas_call(
        body,
        name=f"gather_start_{stage[0]}",
        in_specs=[HBM] * na + [ANY],
        out_specs=[SEM, SEM] + [HBM] * na + [pl.BlockSpec(memory_space=pltpu.VMEM)],
        out_shape=[pltpu.SemaphoreType.DMA((N_SEM,)), pltpu.SemaphoreType.DMA((N_SEM,))]
        + [pltpu.HBM(GATHERED[n], BF16) for n in stage] + [jax.ShapeDtypeStruct((8, LANE), F32)],
        input_output_aliases={k: 2 + k for k in range(na)},
        compiler_params=pltpu.CompilerParams(has_side_effects=EFFECT),
    )(*[_in_hbm(arrays[n]) for n in stage], after)
    return outs[0], outs[1], dict(zip(stage, outs[2:2 + na])), outs[-1]


def _gather_wait(send_sems, recv_sems, arrays, after):
    stage = tuple(arrays)
    na = len(stage)

    def body(*refs):
        arr = dict(zip(stage, refs[:na]))
        sems_s, sems_r = refs[na], refs[na + 1]
        _, _, c, j = _position()
        for jj in range(N_CHIPS):
            @pl.when(j != jj)
            def _():
                for i, p in _stage_pieces(jj, stage):
                    _gather_copy(arr, sems_s, sems_r, c, jj, i, p, jj).wait_recv()

            @pl.when(j == jj)
            def _():
                for i, p in _stage_pieces(jj, stage):
                    for kk in range(N_CHIPS):
                        if kk != jj:
                            _gather_copy(arr, sems_s, sems_r, c, jj, i, p, kk).wait_send()

    outs = pl.pallas_call(
        body,
        name=f"gather_wait_{stage[0]}",
        in_specs=[HBM] * na + [SEM, SEM, ANY],
        out_specs=[HBM] * na,
        out_shape=[pltpu.HBM(GATHERED[n], BF16) for n in stage],
        input_output_aliases={k: k for k in range(na)},
        compiler_params=pltpu.CompilerParams(has_side_effects=EFFECT),
    )(*[arrays[n] for n in stage], send_sems, recv_sems, after)
    return dict(zip(stage, outs))


def _gather_swap(arrays):
    stage = tuple(arrays)
    na = len(stage)

    def body(*refs):
        dst = dict(zip(stage, refs[na:2 * na]))
        send_sems, recv_sems = refs[2 * na:]
        x, y, c, j = _position()

        def fwd(jj, i, p, which):
            rect = _half_rect(dst[p.dst], p, "dst", which)
            return pltpu.make_async_remote_copy(
                src_ref=rect, dst_ref=rect, send_sem=send_sems.at[jj * MAX_PIECES + i],
                recv_sem=recv_sems.at[jj * MAX_PIECES + i], device_id=(x, y, 1 - c), device_id_type=MESH)

        for jj in range(N_CHIPS):
            @pl.when(j != jj)
            def _():
                for i, p in _stage_pieces(jj, stage):
                    fwd(jj, i, p, c).start()
        for jj in range(N_CHIPS):
            @pl.when(j != jj)
            def _():
                for i, p in _stage_pieces(jj, stage):
                    fwd(jj, i, p, 1 - c).wait_recv()
        for jj in range(N_CHIPS):
            @pl.when(j != jj)
            def _():
                for i, p in _stage_pieces(jj, stage):
                    fwd(jj, i, p, c).wait_send()

    outs = pl.pallas_call(
        body,
        name=f"gather_swap_{stage[0]}",
        in_specs=[ANY] * na,
        out_specs=[ANY] * na,
        out_shape=[jax.ShapeDtypeStruct(GATHERED[n], BF16) for n in stage],
        input_output_aliases={k: k for k in range(na)},
        scratch_shapes=[pltpu.SemaphoreType.DMA((N_SEM,)), pltpu.SemaphoreType.DMA((N_SEM,))],
        compiler_params=pltpu.CompilerParams(has_side_effects=True),
    )(*[arrays[n] for n in stage])
    return dict(zip(stage, outs))


def _own_half(ref, shape, axis, which):
    if axis == 1:
        return ref.at[:, pl.ds(which * (shape[1] // 2), shape[1] // 2)]
    return ref.at[pl.ds(which * (shape[0] // 2), shape[0] // 2), :]


def _swap_halves(grads):
    names = tuple(grads)
    n_tr = len(names)

    def body(*refs):
        src = dict(zip(names, refs[:len(names)]))
        dst = dict(zip(names, refs[len(names):2 * len(names)]))
        send_sems, recv_sems = refs[2 * len(names):]
        x, y, c, _ = _position()
        copies = [pltpu.make_async_remote_copy(
            src_ref=_own_half(src[n], GATHERED[n], HALF_AXIS[n], 1 - c), dst_ref=dst[n],
            send_sem=send_sems.at[k], recv_sem=recv_sems.at[k],
            device_id=(x, y, 1 - c), device_id_type=MESH) for k, n in enumerate(names)]
        for cp in copies:
            cp.start()
        for cp in copies:
            cp.wait_recv()
        for cp in copies:
            cp.wait_send()

    outs = pl.pallas_call(
        body,
        name=f"swap_halves_{names[0]}",
        in_specs=[ANY] * len(names),
        out_specs=[ANY] * len(names),
        out_shape=[jax.ShapeDtypeStruct(_halved(GATHERED[n], HALF_AXIS[n]), BF16) for n in names],
        scratch_shapes=[pltpu.SemaphoreType.DMA((n_tr,)), pltpu.SemaphoreType.DMA((n_tr,))],
        compiler_params=pltpu.CompilerParams(has_side_effects=True),
    )(*[grads[n] for n in names])
    return dict(zip(names, outs))


ADD_ROWS = 256


def _add_half(full, recv, c_arr, name):
    rows, cols = recv.shape
    if HALF_AXIS[name] == 1:
        index = lambda i, c_ref: (i, c_ref[0])
    else:
        nb = rows // ADD_ROWS
        index = lambda i, c_ref: (nb * c_ref[0] + i, 0)

    def body(c_ref, a_ref, b_ref, o_ref):
        o_ref[...] = (a_ref[...].astype(F32) + b_ref[...].astype(F32)).astype(BF16)

    return pl.pallas_call(
        body,
        name=f"add_half_{name}",
        grid_spec=pltpu.PrefetchScalarGridSpec(
            num_scalar_prefetch=1,
            grid=(rows // ADD_ROWS,),
            in_specs=[pl.BlockSpec((ADD_ROWS, cols), index), pl.BlockSpec((ADD_ROWS, cols), lambda i, c_ref: (i, 0))],
            out_specs=pl.BlockSpec((ADD_ROWS, cols), lambda i, c_ref: (i, 0)),
        ),
        out_shape=jax.ShapeDtypeStruct((rows, cols), BF16),
        compiler_params=_params(("parallel",)),
    )(c_arr, full, recv)


SLOT_SHAPES = {n: _halved(SHARD_SHAPES[n], HALF_AXIS[n]) for n in SHARDS}


def _slot_shape(n):
    return (N_CHIPS,) + SLOT_SHAPES[n]


def _stage_shards(stage):
    pieces = [p for jj in range(N_CHIPS) for p in _pieces_of(jj)]
    return tuple(s for s in SHARDS if any(p.src == s and p.dst in stage for p in pieces))


def _scatter_copy(src, dst, send_sems, recv_sems, c, jj, kk, i, p):
    return pltpu.make_async_remote_copy(
        src_ref=_rect_in_half(src[p.dst], p, "dst"), dst_ref=_rect_in_half(dst[p.src].at[jj], p, "src"),
        send_sem=send_sems.at[kk * MAX_PIECES + i], recv_sem=recv_sems.at[jj * MAX_PIECES + i],
        device_id=(kk // 2, kk % 2, c), device_id_type=MESH)


def _scatter_start(halves, slots):
    stage, touched = tuple(halves), tuple(slots)
    nh, nt = len(stage), len(touched)

    def body(*refs):
        src = dict(zip(stage, refs[:nh]))
        dst = dict(zip(touched, refs[nh:nh + nt]))
        send_sems, recv_sems = refs[nh + nt], refs[nh + nt + 1]
        token = refs[-1]
        _, _, c, j = _position()
        for jj in range(N_CHIPS):
            @pl.when(j == jj)
            def _():
                for kk in range(N_CHIPS):
                    if kk != jj:
                        for i, p in _stage_pieces(kk, stage):
                            _scatter_copy(src, dst, send_sems, recv_sems, c, jj, kk, i, p).start()
        token[...] = jnp.zeros_like(token)

    outs = pl.pallas_call(
        body,
        name=f"scatter_start_{stage[0]}",
        in_specs=[HBM] * (nh + nt),
        out_specs=[SEM, SEM] + [HBM] * (nh + nt) + [pl.BlockSpec(memory_space=pltpu.VMEM)],
        out_shape=[pltpu.SemaphoreType.DMA((N_SEM,)), pltpu.SemaphoreType.DMA((N_SEM,))]
        + [pltpu.HBM(halves[n].shape, BF16) for n in stage] + [pltpu.HBM(_slot_shape(s), BF16) for s in touched]
        + [jax.ShapeDtypeStruct((8, LANE), F32)],
        input_output_aliases={k: 2 + k for k in range(nh + nt)},
        compiler_params=pltpu.CompilerParams(has_side_effects=EFFECT),
    )(*[_in_hbm(halves[n]) for n in stage], *[_in_hbm(slots[s]) for s in touched])
    return outs[0], outs[1], dict(zip(stage, outs[2:2 + nh])), dict(zip(touched, outs[2 + nh:2 + nh + nt])), outs[-1]


def _scatter_wait(send_sems, recv_sems, halves, slots, after):
    stage, touched = tuple(halves), tuple(slots)
    nh, nt = len(stage), len(touched)

    def body(*refs):
        src = dict(zip(stage, refs[:nh]))
        dst = dict(zip(touched, refs[nh:nh + nt]))
        sems_s, sems_r = refs[nh + nt], refs[nh + nt + 1]
        _, _, c, j = _position()
        for jj in range(N_CHIPS):
            @pl.when(j == jj)
            def _():
                for ss in range(N_CHIPS):
                    if ss != jj:
                        for i, p in _stage_pieces(jj, stage):
                            _scatter_copy(src, dst, sems_s, sems_r, c, ss, jj, i, p).wait_recv()
                for kk in range(N_CHIPS):
                    if kk != jj:
                        for i, p in _stage_pieces(kk, stage):
                            _scatter_copy(src, dst, sems_s, sems_r, c, jj, kk, i, p).wait_send()

    outs = pl.pallas_call(
        body,
        name=f"scatter_wait_{stage[0]}",
        in_specs=[HBM] * (nh + nt) + [SEM, SEM, ANY],
        out_specs=[HBM] * (nh + nt),
        out_shape=[pltpu.HBM(halves[n].shape, BF16) for n in stage] + [pltpu.HBM(_slot_shape(s), BF16) for s in touched],
        input_output_aliases={k: k for k in range(nh + nt)},
        compiler_params=pltpu.CompilerParams(has_side_effects=EFFECT),
    )(*[halves[n] for n in stage], *[slots[s] for s in touched], send_sems, recv_sems, after)
    return dict(zip(stage, outs[:nh])), dict(zip(touched, outs[nh:]))


SUM_ROWS = {"win": 448, "mk": 256, "br0": 256, "br1": 256, "br2": 256, "out": 256}


def _sum_in_chip_order(chip, own, s_ref):
    acc = None
    for k in range(N_CHIPS):
        term = jnp.where(chip == k, own, s_ref[k].astype(F32))
        acc = term if acc is None else acc + term
    return acc


def _sum_slots(slots, own_half, pos_arr, name):
    _, rows, cols = slots.shape
    tr = SUM_ROWS[name]
    nb = rows // tr
    if HALF_AXIS[name] == 1:
        own_index = lambda i, pos: (nb * pos[1] + i, 0)
        out_index = lambda i, pos: (i, pos[0])
    else:
        own_index = lambda i, pos: (i, pos[1])
        out_index = lambda i, pos: (nb * pos[0] + i, 0)

    def body(pos, s_ref, own_ref, o_ref):
        o_ref[...] = _sum_in_chip_order(pos[1], own_ref[...].astype(F32), s_ref)

    return pl.pallas_call(
        body,
        name=f"sum_slots_{name}",
        grid_spec=pltpu.PrefetchScalarGridSpec(
            num_scalar_prefetch=1,
            grid=(nb,),
            in_specs=[pl.BlockSpec((N_CHIPS, tr, cols), lambda i, pos: (0, i, 0)),
                      pl.BlockSpec((tr, cols), own_index)],
            out_specs=pl.BlockSpec((tr, cols), out_index),
        ),
        out_shape=jax.ShapeDtypeStruct(SHARD_SHAPES[name], F32),
        compiler_params=_params(("parallel",)),
    )(pos_arr, slots, own_half)


def _own_partial_tables():
    import numpy as np
    nb = SHARD // HALF_TILE
    grp, blk = np.zeros((N_CHIPS, nb), np.int32), np.zeros((N_CHIPS, nb), np.int32)
    for jj in range(N_CHIPS):
        for r, n, g, gr in _shard_runs(jj):
            for k in range(n // HALF_TILE):
                grp[jj, r // HALF_TILE + k] = GROUPS.index(g)
                blk[jj, r // HALF_TILE + k] = gr // HALF_TILE + k
    return grp, blk


def _sum_slots_win(slots, own_halves, pos_arr, grp_tbl, blk_tbl):
    nb = SHARD // HALF_TILE
    cols = D // 2

    def own_spec(gi):
        return pl.BlockSpec((HALF_TILE, cols), lambda b, pos, grp, blk: (jnp.where(grp[b] == gi, blk[b], 0), 0))

    def body(pos, grp, blk, s_ref, a_ref, b_ref, c_ref, d_ref, o_ref):
        g = grp[pl.program_id(0)]
        own = a_ref[...]
        for gi, ref in ((1, b_ref), (2, c_ref), (3, d_ref)):
            own = jnp.where(g == gi, ref[...], own)
        o_ref[...] = _sum_in_chip_order(pos[1], own.astype(F32), s_ref)

    return pl.pallas_call(
        body,
        name="sum_slots_win",
        grid_spec=pltpu.PrefetchScalarGridSpec(
            num_scalar_prefetch=3,
            grid=(nb,),
            in_specs=[pl.BlockSpec((N_CHIPS, HALF_TILE, cols), lambda b, pos, grp, blk: (0, b, 0))]
            + [own_spec(gi) for gi in range(len(GROUPS))],
            out_specs=pl.BlockSpec((HALF_TILE, cols), lambda b, pos, grp, blk: (b, pos[0])),
        ),
        out_shape=jax.ShapeDtypeStruct(SHARD_SHAPES["win"], F32),
        compiler_params=_params(("parallel",)),
    )(pos_arr, grp_tbl, blk_tbl, slots, *[own_halves[g] for g in GROUPS])


def _share_sums(sums):
    def body(*refs):
        bufs = refs[len(SHARDS):2 * len(SHARDS)]
        send_sems, recv_sems = refs[2 * len(SHARDS):]
        x, y, c, _ = _position()
        copies = []
        for k, (n, b) in enumerate(zip(SHARDS, bufs)):
            mine = _own_half(b, SHARD_SHAPES[n], HALF_AXIS[n], c)
            copies.append(pltpu.make_async_remote_copy(
                src_ref=mine, dst_ref=mine, send_sem=send_sems.at[k], recv_sem=recv_sems.at[k],
                device_id=(x, y, 1 - c), device_id_type=MESH))
        for cp in copies:
            cp.start()
        for cp in copies:
            cp.wait_recv()
        for cp in copies:
            cp.wait_send()

    outs = pl.pallas_call(
        body,
        name="share_sums",
        in_specs=[ANY] * len(SHARDS),
        out_specs=[ANY] * len(SHARDS),
        out_shape=[jax.ShapeDtypeStruct(sums[n].shape, F32) for n in SHARDS],
        input_output_aliases={k: k for k in range(len(SHARDS))},
        scratch_shapes=[pltpu.SemaphoreType.DMA((len(SHARDS),)), pltpu.SemaphoreType.DMA((len(SHARDS),))],
        compiler_params=pltpu.CompilerParams(has_side_effects=True),
    )(*[sums[n] for n in SHARDS])
    return dict(zip(SHARDS, outs))


N_DEV = 8


def _all_reduce_small(pack, name):
    rows = pack.shape[0]
    half = rows // 2

    def body(p_ref, o_ref, sib, land, sems):
        x, y, c, j = _position()
        sibling = (x, y, 1 - c)
        swap = pltpu.make_async_remote_copy(src_ref=p_ref, dst_ref=sib, send_sem=sems.at[0], recv_sem=sems.at[1],
                                            device_id=sibling, device_id_type=MESH)
        swap.start()
        swap.wait_recv()
        land[j] = p_ref[...] + sib[...]

        def mine(k, which):
            return land.at[k, pl.ds(which * half, half)]

        def ici(kk):
            return pltpu.make_async_remote_copy(
                src_ref=mine(j, c), dst_ref=mine(j, c), send_sem=sems.at[2 + kk], recv_sem=sems.at[6 + j],
                device_id=(kk // 2, kk % 2, c), device_id_type=MESH)

        def arrival(kk):
            return pltpu.make_async_remote_copy(
                src_ref=mine(kk, c), dst_ref=mine(kk, c), send_sem=sems.at[2 + kk], recv_sem=sems.at[6 + kk],
                device_id=(kk // 2, kk % 2, c), device_id_type=MESH)

        def passed_on(kk, which):
            return pltpu.make_async_remote_copy(
                src_ref=mine(kk, which), dst_ref=mine(kk, which), send_sem=sems.at[10 + kk],
                recv_sem=sems.at[14 + kk], device_id=sibling, device_id_type=MESH)

        for kk in range(N_CHIPS):
            @pl.when(j != kk)
            def _():
                ici(kk).start()
        for kk in range(N_CHIPS):
            @pl.when(j != kk)
            def _():
                arrival(kk).wait_recv()
                passed_on(kk, c).start()
        for kk in range(N_CHIPS):
            @pl.when(j != kk)
            def _():
                passed_on(kk, 1 - c).wait_recv()
        acc = land[0]
        for kk in range(1, N_CHIPS):
            acc = acc + land[kk]
        o_ref[...] = acc
        swap.wait_send()
        for kk in range(N_CHIPS):
            @pl.when(j != kk)
            def _():
                ici(kk).wait_send()
                passed_on(kk, c).wait_send()

    vmem = pl.BlockSpec(memory_space=pltpu.VMEM)
    return pl.pallas_call(
        body,
        name=name,
        in_specs=[vmem],
        out_specs=vmem,
        out_shape=jax.ShapeDtypeStruct((rows, LANE), F32),
        scratch_shapes=[pltpu.VMEM((rows, LANE), F32), pltpu.VMEM((N_CHIPS, rows, LANE), F32),
                        pltpu.SemaphoreType.DMA((18,))],
        compiler_params=pltpu.CompilerParams(has_side_effects=True, vmem_limit_bytes=VMEM_LIMIT),
    )(pack)


def _adamw(w, g, m, v, name, tr):
    rows, cols = w.shape
    tr = min(tr, rows)

    def body(w_ref, g_ref, m_ref, v_ref, go_ref, d_ref, nm_ref, nv_ref):
        gv = g_ref[...]
        go_ref[...] = gv
        nm = ADAM_B1 * m_ref[...] + (1.0 - ADAM_B1) * gv
        nv = ADAM_B2 * v_ref[...] + (1.0 - ADAM_B2) * (gv * gv)
        nm_ref[...] = nm
        nv_ref[...] = nv
        m_hat = nm / (1.0 - ADAM_B1 ** ADAM_STEP)
        v_hat = nv / (1.0 - ADAM_B2 ** ADAM_STEP)
        d_ref[...] = -ADAM_LR * (m_hat / (jnp.sqrt(v_hat) + ADAM_EPS) + ADAM_WD * w_ref[...])

    blk = pl.BlockSpec((tr, cols), lambda i: (i, 0))
    shape = jax.ShapeDtypeStruct((rows, cols), F32)
    return pl.pallas_call(
        body,
        name=f"adamw_{name}",
        grid=(rows // tr,),
        in_specs=[blk] * 4,
        out_specs=[blk] * 4,
        out_shape=[shape] * 4,
        compiler_params=_params(("parallel",)),
    )(w, g, m, v)


SMALL = (("pre_norm_g", (1, D)), ("post_norm_g", (1, D)), ("mem_norm_g", (1, D)), ("conv_w", (CONV_W, D_RNN)),
         ("conv_b", (1, D_RNN)), ("w_rg_a", (RNN_BLOCKS, LANE, LANE)), ("b_rg_a", (1, D_RNN)),
         ("w_rg_x", (RNN_BLOCKS, LANE, LANE)), ("b_rg_x", (1, D_RNN)), ("lru_lambda", (1, D_RNN)),
         ("swa_sinks", (1, SWA_HEADS)), ("rel_bias", (REL_BUCKETS, SWA_HEADS)))
PACK_ROWS = 2176


def _slot_len(shape):
    return -(-math.prod(shape) // LANE) * LANE


def _pack(values):
    parts = []
    for name, shape in SMALL:
        flat = values[name].reshape(-1).astype(F32)
        parts.append(jnp.pad(flat, (0, _slot_len(shape) - flat.shape[0])))
    flat = jnp.concatenate(parts)
    return jnp.pad(flat, (0, PACK_ROWS * LANE - flat.shape[0])).reshape(PACK_ROWS, LANE)


def _unpack(pack, shapes=None):
    flat = pack.reshape(-1)
    out, off = {}, 0
    for name, shape in SMALL:
        shp = shape if shapes is None or name not in shapes else shapes[name]
        out[name] = flat[off:off + math.prod(shp)].reshape(shp)
        off += _slot_len(shape)
    return out


TWIN_WEIGHTS = ("pre_norm_g", "post_norm_g", "mem_norm_g", "w_in", "conv_w", "conv_b", "w_rg_a", "b_rg_a", "w_rg_x",
                "b_rg_x", "lru_lambda", "swa_sinks", "rel_bias", "w_mem_kv", "w_br_rg", "w_br_swa", "w_br_mem", "w_out")
BIG = {"w_in": "win", "w_mem_kv": "mk", "w_br_rg": "br0", "w_br_swa": "br1", "w_br_mem": "br2", "w_out": "out"}


def kernel(x, mem, pre_norm_g, post_norm_g, mem_norm_g, w_in, conv_w, conv_b, w_rg_a, b_rg_a, w_rg_x, b_rg_x, lru_lambda, swa_sinks, rel_bias, w_mem_kv, w_br_rg, w_br_swa, w_br_mem, w_out, loss_target, m_pre_norm_g, m_post_norm_g, m_mem_norm_g, m_w_in, m_conv_w, m_conv_b, m_w_rg_a, m_b_rg_a, m_w_rg_x, m_b_rg_x, m_lru_lambda, m_swa_sinks, m_rel_bias, m_w_mem_kv, m_w_br_rg, m_w_br_swa, m_w_br_mem, m_w_out, v_pre_norm_g, v_post_norm_g, v_mem_norm_g, v_w_in, v_conv_w, v_conv_b, v_w_rg_a, v_b_rg_a, v_w_rg_x, v_b_rg_x, v_lru_lambda, v_swa_sinks, v_rel_bias, v_w_mem_kv, v_w_br_rg, v_w_br_swa, v_w_br_mem, v_w_out):
    args = dict(locals())
    out_shapes = {n: args[n].shape for n in TWIN_WEIGHTS}
    w = {n: (args[n] if n == "rel_bias" else args[n][0]) for n in TWIN_WEIGHTS}
    m = {n: (args["m_" + n] if n == "rel_bias" else args["m_" + n][0]) for n in TWIN_WEIGHTS}
    v = {n: (args["v_" + n] if n == "rel_bias" else args["v_" + n][0]) for n in TWIN_WEIGHTS}
    for d in (w, m, v):
        for n, shape in SMALL:
            if n != "conv_w":
                d[n] = d[n].reshape(shape)

    xi, yi, ci = lax.axis_index("x"), lax.axis_index("y"), lax.axis_index("c")
    chip = 2 * xi + yi
    c_arr = ci.astype(jnp.int32).reshape(1)
    zero = jnp.zeros((), jnp.int32)
    cw0 = (chip * (D_RNN // N_CHIPS)).astype(jnp.int32)

    placed = lax.dynamic_update_slice(jnp.zeros((CONV_W, D_RNN), F32), w["conv_w"], (zero, cw0))
    placed = jnp.where(ci == 0, placed, 0.0).reshape(CONV_W * D_RNN // LANE, LANE)
    conv_w_full = _all_reduce_small(placed, "gather_conv_w").reshape(CONV_W, D_RNN)

    for d in (w, m, v):
        d["w_in"] = d["w_in"].T
    chip_row = lambda tbl: lax.dynamic_slice(jnp.asarray(tbl), (chip.astype(jnp.int32), zero), (1, tbl.shape[1]))[0]
    big_of = {s: n for n, s in BIG.items()}
    ag, token = {}, conv_w_full
    for stage in GATHER_STAGES:
        placed = {n: (_place_group(w["w_in"], n, chip_row(_own_block_table(n))) if n in GROUPS
                      else _place_shard(w[big_of[n]], n)) for n in stage}
        send, recv, in_flight, token = _gather_start(placed, token)
        ag[stage] = (send, recv, in_flight)

    all_started = token

    def fetch(names, after):
        send, recv, in_flight = ag[names]
        after = all_started if names == GATHER_STAGES[0] else after
        ready = _gather_swap(_gather_wait(send, recv, in_flight, after))
        return tuple(ready[n] for n in names)

    rs = {"slots": {}, "halves": {}, "pending": []}

    def emit(grads):
        received = _swap_halves(grads)
        halves = {n: _add_half(grads[n], received[n], c_arr, n) for n in grads}
        landing = {s: rs["slots"][s] if s in rs["slots"] else lax.empty(_slot_shape(s), BF16)
                   for s in _stage_shards(tuple(grads))}
        send, recv, halves, landing, token = _scatter_start(halves, landing)
        rs["slots"].update(landing)
        rs["pending"].append((send, recv, halves, tuple(landing)))
        return token

    sp = {n: w[n] for n, _ in SMALL}
    sp["conv_w"] = conv_w_full
    sq, grad_x, d_small = _local_step(x[0], mem[0], loss_target[0], sp, fetch, emit)
    loss = lax.psum(sq[0, 0] * (0.5 / D), ("x", "y", "c"))

    small_total = _all_reduce_small(_pack(d_small), "all_reduce_small")

    for send, recv, halves, touched in rs["pending"]:
        halves, landed = _scatter_wait(send, recv, halves, {s: rs["slots"][s] for s in touched}, small_total)
        rs["slots"].update(landed)
        rs["halves"].update(halves)
    pos_arr = jnp.stack([ci, chip]).astype(jnp.int32)
    grp_tbl, blk_tbl = (chip_row(t) for t in _own_partial_tables())
    sums = {s: _sum_slots(rs["slots"][s], rs["halves"][s], pos_arr, s) for s in SHARDS if s != "win"}
    sums["win"] = _sum_slots_win(rs["slots"]["win"], rs["halves"], pos_arr, grp_tbl, blk_tbl)
    sums = _share_sums(sums)
    g_big = {n: sums[s] for n, s in BIG.items()}

    g_small = _unpack(small_total)
    g_small["conv_w"] = lax.dynamic_slice(g_small["conv_w"], (zero, cw0), (CONV_W, D_RNN // N_CHIPS))

    grad, delta, new_m, new_v = {}, {}, {}, {}
    for n, s in BIG.items():
        grad[n], delta[n], new_m[n], new_v[n] = _adamw(w[n], g_big[n], m[n], v[n], s, 224 if n == "w_in" else 128)
    for group in (grad, delta, new_m, new_v):
        group["w_in"] = group["w_in"].T
    _, d_, m_, v_ = _adamw(_pack(w), _pack(g_small), _pack(m), _pack(v), "small", PACK_ROWS)
    shard_shapes = {"conv_w": (CONV_W, D_RNN // N_CHIPS)}
    d_, m_, v_ = (_unpack(a, shard_shapes) for a in (d_, m_, v_))
    for n, _ in SMALL:
        grad[n], delta[n], new_m[n], new_v[n] = g_small[n], d_[n], m_[n], v_[n]

    outs = [loss, grad_x.reshape(1, S, D)]
    for group in (grad, delta, new_m, new_v):
        outs += [group[n].reshape(out_shapes[n]) for n in TWIN_WEIGHTS]
    return tuple(outs)
```

```python
import functools
import math
from typing import NamedTuple

import jax
import jax.numpy as jnp
from jax import lax
from jax.experimental import pallas as pl
from jax.experimental.pallas import tpu as pltpu

F32 = jnp.float32
BF16 = jnp.bfloat16
MESH = pl.DeviceIdType.MESH

S = 2048
D = 2048
MEM = 256
D_RNN = 1024
RNN_BLOCKS = 8
CONV_W = 4
LRU_C = 8.0
SWA_HEADS = 16
SWA_HD = 64
WINDOW = 128
MEM_HEADS = 4
MEM_HD = 256
REL_BUCKETS = 32
REL_MAX_DIST = 128
EPS = 1e-6
NEG_INF = -1e30
LANE = 128
SHARD = 3136
HALF_TILE = 64
N_CHIPS = 4
VMEM_LIMIT = 56 * 1024 * 1024

ADAM_LR = 0.001
ADAM_B1 = 0.9
ADAM_B2 = 0.999
ADAM_EPS = 1e-08
ADAM_WD = 0.01
ADAM_STEP = 10

GROUP_TILES = {"A": 16, "B": 18, "C": 16, "D": 48}
GROUPS = ("A", "B", "C", "D")


def _params(sem=None):
    return pltpu.CompilerParams(dimension_semantics=sem, vmem_limit_bytes=VMEM_LIMIT)


def _sigmoid(v):
    return jax.nn.sigmoid(v)


def _tile_home(t):
    if t < 16:
        return "A", t
    if t < 24:
        return "B", t - 16
    if t < 26:
        return "B", t - 24 + 16
    if t < 34:
        return "B", t - 26 + 8
    if t < 50:
        return "C", t - 34
    return "D", t - 50


def _shard_runs(j):
    runs = []
    per_shard = SHARD // HALF_TILE
    for q in range(per_shard * j, per_shard * (j + 1)):
        g, gt = _tile_home(q // 2)
        row = gt * LANE + (q % 2) * HALF_TILE
        if runs and runs[-1][2] == g and runs[-1][3] + runs[-1][1] == row:
            runs[-1][1] += HALF_TILE
        else:
            runs.append([(q - per_shard * j) * HALF_TILE, HALF_TILE, g, row])
    return [tuple(r) for r in runs]


_DIMS = {
    "nn": (((1,), (0,)), ((), ())),
    "nt": (((1,), (1,)), ((), ())),
    "tn": (((0,), (0,)), ((), ())),
}


def _mm(a, b, mode, out_dtype, tm, tn, tk, name, acc=None, after=None):
    if mode == "nn":
        (m, k), n = a.shape, b.shape[1]
    elif mode == "nt":
        (m, k), n = a.shape, b.shape[0]
    else:
        (k, m), n = a.shape, b.shape[1]
    tm, tn, tk = min(tm, m), min(tn, n), min(tk, k)
    assert m % tm == 0 and n % tn == 0 and k % tk == 0, (name, m, n, k)
    nk = k // tk
    has_acc = acc is not None

    def body(*refs):
        a_ref, b_ref = refs[0], refs[1]
        o_ref = refs[3] if has_acc else refs[2]
        p = lax.dot_general(a_ref[...], b_ref[...], _DIMS[mode], preferred_element_type=F32)

        def finish(v):
            if has_acc:
                v = v + refs[2][...]
            o_ref[...] = v.astype(out_dtype)

        if nk == 1:
            finish(p)
        else:
            s_ref = refs[-1]
            kk = pl.program_id(2)

            @pl.when(kk == 0)
            def _():
                s_ref[...] = p

            @pl.when(kk > 0)
            def _():
                s_ref[...] += p

            @pl.when(kk == nk - 1)
            def _():
                finish(s_ref[...])

    if mode == "nn":
        a_spec = pl.BlockSpec((tm, tk), lambda i, j, kk: (i, kk))
        b_spec = pl.BlockSpec((tk, tn), lambda i, j, kk: (kk, j))
    elif mode == "nt":
        a_spec = pl.BlockSpec((tm, tk), lambda i, j, kk: (i, kk))
        b_spec = pl.BlockSpec((tn, tk), lambda i, j, kk: (j, kk))
    else:
        a_spec = pl.BlockSpec((tk, tm), lambda i, j, kk: (kk, i))
        b_spec = pl.BlockSpec((tk, tn), lambda i, j, kk: (kk, j))
    o_spec = pl.BlockSpec((tm, tn), lambda i, j, kk: (i, j))
    in_specs = [a_spec, b_spec] + ([o_spec] if has_acc else [])
    args = (a, b) + ((acc,) if has_acc else ())
    if after is not None:
        in_specs.append(pl.BlockSpec(memory_space=pl.ANY))
        args += (after,)
    n_in = len(args)
    kernel_body = body

    def body(*refs):
        kernel_body(*(refs[:n_in - (after is not None)] + refs[n_in:]))

    return pl.pallas_call(
        body,
        name=name,
        grid=(m // tm, n // tn, nk),
        in_specs=in_specs,
        out_specs=o_spec,
        out_shape=jax.ShapeDtypeStruct((m, n), out_dtype),
        scratch_shapes=[pltpu.VMEM((tm, tn), F32)] if nk > 1 else [],
        compiler_params=_params(("parallel", "parallel", "arbitrary")),
    )(*args)


def _rms_fwd(x, g, name, ts=256):
    r, d = x.shape

    def body(x_ref, g_ref, o_ref):
        xv = x_ref[...]
        inv = lax.rsqrt(jnp.mean(xv * xv, axis=-1, keepdims=True) + EPS)
        o_ref[...] = (xv * inv * g_ref[...]).astype(BF16)

    return pl.pallas_call(
        body,
        name=name,
        grid=(r // ts,),
        in_specs=[pl.BlockSpec((ts, d), lambda i: (i, 0)), pl.BlockSpec((1, d), lambda i: (0, 0))],
        out_specs=pl.BlockSpec((ts, d), lambda i: (i, 0)),
        out_shape=jax.ShapeDtypeStruct((r, d), BF16),
        compiler_params=_params(("parallel",)),
    )(x, g)


def _post_loss(out, x, tgt, g_post, ts=256):
    n = S // ts

    def body(o_ref, x_ref, t_ref, g_ref, sq_ref, dy_ref, do_ref, dg_ref):
        i = pl.program_id(0)

        @pl.when(i == 0)
        def _():
            sq_ref[...] = jnp.zeros_like(sq_ref)
            dg_ref[...] = jnp.zeros_like(dg_ref)

        ov = o_ref[...]
        g = g_ref[...]
        inv = lax.rsqrt(jnp.mean(ov * ov, axis=-1, keepdims=True) + EPS)
        on = ov * inv
        err = x_ref[...] + on * g - t_ref[...]
        sq_ref[...] += jnp.sum(err * err)
        dy = err * (1.0 / D)
        dy_ref[...] = dy
        dg_ref[...] += jnp.sum(dy * on, axis=0, keepdims=True)
        don = dy * g
        do_ref[...] = (inv * (don - on * jnp.mean(don * on, axis=-1, keepdims=True))).astype(BF16)

    row = pl.BlockSpec((ts, D), lambda i: (i, 0))
    vec = pl.BlockSpec((1, D), lambda i: (0, 0))
    return pl.pallas_call(
        body,
        name="post_loss",
        grid=(n,),
        in_specs=[row, row, row, vec],
        out_specs=[pl.BlockSpec((8, LANE), lambda i: (0, 0)), row, row, vec],
        out_shape=[
            jax.ShapeDtypeStruct((8, LANE), F32),
            jax.ShapeDtypeStruct((S, D), F32),
            jax.ShapeDtypeStruct((S, D), BF16),
            jax.ShapeDtypeStruct((1, D), F32),
        ],
        compiler_params=_params(("arbitrary",)),
    )(out, x, tgt, g_post)


def _pre_bwd(dh, x, dy, g_pre, ts=256):
    n = S // ts

    def body(dh_ref, x_ref, dy_ref, g_ref, gx_ref, dg_ref):
        i = pl.program_id(0)

        @pl.when(i == 0)
        def _():
            dg_ref[...] = jnp.zeros_like(dg_ref)

        xv = x_ref[...]
        dhv = dh_ref[...]
        inv = lax.rsqrt(jnp.mean(xv * xv, axis=-1, keepdims=True) + EPS)
        xn = xv * inv
        dg_ref[...] += jnp.sum(dhv * xn, axis=0, keepdims=True)
        dxn = dhv * g_ref[...]
        gx_ref[...] = dy_ref[...] + inv * (dxn - xn * jnp.mean(dxn * xn, axis=-1, keepdims=True))

    row = pl.BlockSpec((ts, D), lambda i: (i, 0))
    vec = pl.BlockSpec((1, D), lambda i: (0, 0))
    return pl.pallas_call(
        body,
        name="pre_bwd",
        grid=(n,),
        in_specs=[row, row, row, vec],
        out_specs=[row, vec],
        out_shape=[jax.ShapeDtypeStruct((S, D), F32), jax.ShapeDtypeStruct((1, D), F32)],
        compiler_params=_params(("arbitrary",)),
    )(dh, x, dy, g_pre)


def _memnorm_bwd(dmemn, mem):
    def body(d_ref, m_ref, dg_ref):
        mv = m_ref[...]
        inv = lax.rsqrt(jnp.mean(mv * mv, axis=-1, keepdims=True) + EPS)
        dg_ref[...] = jnp.sum(d_ref[...] * mv * inv, axis=0, keepdims=True)

    return pl.pallas_call(
        body,
        name="memnorm_bwd",
        out_shape=jax.ShapeDtypeStruct((1, D), F32),
        compiler_params=_params(),
    )(dmemn, mem)


T_RNN = 256


def _neg_expm1(z):
    poly = -z * (1.0 + z * (0.5 + z * (1.0 / 6 + z * (1.0 / 24 + z * (1.0 / 120 + z * (1.0 / 720))))))
    return jnp.where(z > -0.1, poly, 1.0 - jnp.exp(z))


def _softplus_neg(lam):
    return jnp.maximum(-lam, 0.0) + jnp.log1p(jnp.exp(-jnp.abs(lam)))


def _rnn_gates(conv, wa_ref, ba, wx_ref, bx, lam, first_row):
    cbf = conv.astype(BF16)
    ga, gx = [], []
    for n in range(RNN_BLOCKS):
        c_n = cbf[:, n * LANE:(n + 1) * LANE]
        ga.append(jnp.dot(c_n, wa_ref[n], preferred_element_type=F32))
        gx.append(jnp.dot(c_n, wx_ref[n], preferred_element_type=F32))
    gate_r = _sigmoid(jnp.concatenate(ga, axis=1) + ba)
    gate_i = _sigmoid(jnp.concatenate(gx, axis=1) + bx)
    sp = _softplus_neg(lam)
    log_a = -LRU_C * gate_r * sp
    a = jnp.exp(log_a)
    mult_raw = jnp.sqrt(_neg_expm1(2.0 * log_a))
    mult = jnp.where(first_row, 1.0, mult_raw)
    return cbf, gate_r, gate_i, sp, a, mult_raw, mult


def _rglru_fwd(p_a, conv_w, conv_b, wa, ba, wx, bx, lam):
    t = T_RNN
    n = S // t

    def body(xr_ref, g_ref, cw_ref, cb_ref, wa_ref, ba_ref, wx_ref, bx_ref, lam_ref,
             y_ref, h_ref, xp_s, hcar, a_s, b_s):
        i = pl.program_id(0)

        @pl.when(i == 0)
        def _():
            xp_s[0:8, :] = jnp.zeros((8, D_RNN), F32)
            hcar[...] = jnp.zeros_like(hcar)

        @pl.when(i > 0)
        def _():
            xp_s[0:8, :] = xp_s[t:t + 8, :]

        xp_s[8:8 + t, :] = xr_ref[...]
        conv = cb_ref[...]
        for k in range(CONV_W):
            conv = conv + cw_ref[k:k + 1, :] * xp_s[8 - k:8 - k + t, :]
        rows = i * t + lax.broadcasted_iota(jnp.int32, (t, 1), 0)
        _, _, gate_i, _, a, _, mult = _rnn_gates(
            conv, wa_ref, ba_ref[...], wx_ref, bx_ref[...], lam_ref[...], rows == 0)
        a_s[...] = a
        b_s[...] = mult * gate_i * conv

        def step(tt, h):
            h = a_s[pl.ds(tt, 1), :] * h + b_s[pl.ds(tt, 1), :]
            h_ref[pl.ds(tt, 1), :] = h
            return h

        hcar[...] = lax.fori_loop(0, t, step, hcar[...], unroll=8)
        g = g_ref[...]
        y_ref[...] = (h_ref[...] * (g * _sigmoid(g))).astype(BF16)

    blk = lambda c: pl.BlockSpec((t, D_RNN), lambda i: (i, c))
    full = lambda shape: pl.BlockSpec(shape, lambda i: (0,) * len(shape))
    return pl.pallas_call(
        body,
        name="rglru_fwd",
        grid=(n,),
        in_specs=[blk(0), blk(1), full((CONV_W, D_RNN)), full((1, D_RNN)),
                  full((RNN_BLOCKS, LANE, LANE)), full((1, D_RNN)),
                  full((RNN_BLOCKS, LANE, LANE)), full((1, D_RNN)), full((1, D_RNN))],
        out_specs=[blk(0), blk(0)],
        out_shape=[jax.ShapeDtypeStruct((S, D_RNN), BF16), jax.ShapeDtypeStruct((S, D_RNN), F32)],
        scratch_shapes=[pltpu.VMEM((t + 8, D_RNN), F32), pltpu.VMEM((1, D_RNN), F32),
                        pltpu.VMEM((t, D_RNN), F32), pltpu.VMEM((t, D_RNN), F32)],
        compiler_params=_params(("arbitrary",)),
    )(p_a, p_a, conv_w, conv_b, wa, ba, wx, bx, lam)


def _rglru_bwd(dy, p_a, hseq, conv_w, conv_b, wa, ba, wx, bx, lam):
    t = T_RNN
    n = S // t
    rb = t // 8

    def body(dy_ref, xr_ref, g_ref, h_ref, xrp_ref, hp_ref, cw_ref, cb_ref, wa_ref, ba_ref, wx_ref, bx_ref, lam_ref,
             dp_ref, dcw_ref, dcb_ref, dwa_ref, dba_ref, dwx_ref, dbx_ref, dlam_ref,
             xp_s, hp_s, dxp_s, lamcar, a_s, dh_s, lam_s):
        i = pl.program_id(0)
        r = n - 1 - i

        @pl.when(i == 0)
        def _():
            for ref in (dcw_ref, dcb_ref, dwa_ref, dba_ref, dwx_ref, dbx_ref, dlam_ref, lamcar):
                ref[...] = jnp.zeros_like(ref)
            dxp_s[t:t + 8, :] = jnp.zeros((8, D_RNN), F32)

        @pl.when(i > 0)
        def _():
            dxp_s[t:t + 8, :] = dxp_s[0:8, :]

        has_prev = r > 0
        xp_s[0:8, :] = jnp.where(has_prev, xrp_ref[...], 0.0)
        xp_s[8:8 + t, :] = xr_ref[...]
        hp_s[0:8, :] = jnp.where(has_prev, hp_ref[...], 0.0)
        hp_s[8:8 + t, :] = h_ref[...]
        xs = [xp_s[8 - k:8 - k + t, :] for k in range(CONV_W)]
        conv = cb_ref[...]
        for k in range(CONV_W):
            conv = conv + cw_ref[k:k + 1, :] * xs[k]
        rows = r * t + lax.broadcasted_iota(jnp.int32, (t, 1), 0)
        first = rows == 0
        lam_p = lam_ref[...]
        cbf, gate_r, gate_i, sp, a, mult_raw, mult = _rnn_gates(
            conv, wa_ref, ba_ref[...], wx_ref, bx_ref[...], lam_p, first)

        g = g_ref[...]
        sg = _sigmoid(g)
        dyv = dy_ref[...]
        a_s[...] = a
        dh_s[...] = dyv * (g * sg)
        dg = dyv * h_ref[...] * (sg * (1.0 + g * (1.0 - sg)))

        def step(jj, car):
            tt = t - 1 - jj
            lm = dh_s[pl.ds(tt, 1), :] + car
            lam_s[pl.ds(tt, 1), :] = lm
            return a_s[pl.ds(tt, 1), :] * lm

        lamcar[...] = lax.fori_loop(0, t, step, lamcar[...], unroll=8)
        db = lam_s[...]
        da = db * hp_s[7:7 + t, :]
        dmult = db * gate_i * conv
        dgate_i = db * mult * conv
        dconv = db * mult * gate_i
        dlog_a = da * a + jnp.where(first, 0.0, dmult * (-(a * a) / mult_raw))
        dgate_r = dlog_a * (-LRU_C * sp)
        dsp = jnp.sum(dlog_a * (-LRU_C * gate_r), axis=0, keepdims=True)
        dlam_ref[...] += dsp * (-_sigmoid(-lam_p))
        dga = dgate_r * gate_r * (1.0 - gate_r)
        dgx = dgate_i * gate_i * (1.0 - gate_i)
        dba_ref[...] += jnp.sum(dga, axis=0, keepdims=True)
        dbx_ref[...] += jnp.sum(dgx, axis=0, keepdims=True)
        dga16, dgx16 = dga.astype(BF16), dgx.astype(BF16)
        back = []
        for nb in range(RNN_BLOCKS):
            sl = slice(nb * LANE, (nb + 1) * LANE)
            dwa_ref[nb] += lax.dot_general(cbf[:, sl], dga16[:, sl], _DIMS["tn"], preferred_element_type=F32)
            dwx_ref[nb] += lax.dot_general(cbf[:, sl], dgx16[:, sl], _DIMS["tn"], preferred_element_type=F32)
            back.append(lax.dot_general(dga16[:, sl], wa_ref[nb], _DIMS["nt"], preferred_element_type=F32)
                        + lax.dot_general(dgx16[:, sl], wx_ref[nb], _DIMS["nt"], preferred_element_type=F32))
        dconv = dconv + jnp.concatenate(back, axis=1)
        dcb_ref[...] += jnp.sum(dconv, axis=0, keepdims=True)
        for k in range(CONV_W):
            dcw_ref[k:k + 1, :] += jnp.sum(dconv * xs[k], axis=0, keepdims=True)
        dxp_s[0:t, :] = dconv
        dxr = cw_ref[0:1, :] * dconv
        for k in range(1, CONV_W):
            dxr = dxr + cw_ref[k:k + 1, :] * dxp_s[k:k + t, :]
        dp_ref[:, 0:D_RNN] = dxr.astype(BF16)
        dp_ref[:, D_RNN:2 * D_RNN] = dg.astype(BF16)

    blk = lambda c: pl.BlockSpec((t, D_RNN), lambda i: (n - 1 - i, c))
    prev8 = pl.BlockSpec((8, D_RNN), lambda i: (jnp.maximum((n - 1 - i) * rb - 1, 0), 0))
    full = lambda shape: pl.BlockSpec(shape, lambda i: (0,) * len(shape))
    vec = full((1, D_RNN))
    mat = full((RNN_BLOCKS, LANE, LANE))
    return pl.pallas_call(
        body,
        name="rglru_bwd",
        grid=(n,),
        in_specs=[blk(0), blk(0), blk(1), blk(0), prev8, prev8,
                  full((CONV_W, D_RNN)), vec, mat, vec, mat, vec, vec],
        out_specs=[pl.BlockSpec((t, 2 * D_RNN), lambda i: (n - 1 - i, 0)),
                   full((CONV_W, D_RNN)), vec, mat, vec, mat, vec, vec],
        out_shape=[jax.ShapeDtypeStruct((S, 2 * D_RNN), BF16),
                   jax.ShapeDtypeStruct((CONV_W, D_RNN), F32), jax.ShapeDtypeStruct((1, D_RNN), F32),
                   jax.ShapeDtypeStruct((RNN_BLOCKS, LANE, LANE), F32), jax.ShapeDtypeStruct((1, D_RNN), F32),
                   jax.ShapeDtypeStruct((RNN_BLOCKS, LANE, LANE), F32), jax.ShapeDtypeStruct((1, D_RNN), F32),
                   jax.ShapeDtypeStruct((1, D_RNN), F32)],
        scratch_shapes=[pltpu.VMEM((t + 8, D_RNN), F32), pltpu.VMEM((t + 8, D_RNN), F32),
                        pltpu.VMEM((t + 8, D_RNN), F32), pltpu.VMEM((1, D_RNN), F32),
                        pltpu.VMEM((t, D_RNN), F32), pltpu.VMEM((t, D_RNN), F32), pltpu.VMEM((t, D_RNN), F32)],
        compiler_params=_params(("arbitrary",)),
    )(dy, p_a, p_a, hseq, p_a, hseq, conv_w, conv_b, wa, ba, wx, bx, lam)


QB = WINDOW
KB2 = 2 * WINDOW
N_QB = S // QB
N_PAIR = SWA_HEADS // 2


def _swa_keys(kvc_ref, kvp_ref):
    kk = jnp.concatenate([kvp_ref[:, 0:LANE], kvc_ref[:, 0:LANE]], axis=0)
    vv = jnp.concatenate([kvp_ref[:, LANE:2 * LANE], kvc_ref[:, LANE:2 * LANE]], axis=0)
    lo = lax.broadcasted_iota(jnp.int32, (1, LANE), 1) < SWA_HD
    kk_sw, vv_sw = pltpu.roll(kk, SWA_HD, 1), pltpu.roll(vv, SWA_HD, 1)
    kd = [jnp.where(lo, kk, kk_sw).astype(BF16), jnp.where(lo, kk_sw, kk).astype(BF16)]
    vd = [jnp.where(lo, vv, vv_sw).astype(BF16), jnp.where(lo, vv_sw, vv).astype(BF16)]
    return lo, kd, vd


def _swa_valid(n):
    qi = lax.broadcasted_iota(jnp.int32, (QB, KB2), 0)
    kj = lax.broadcasted_iota(jnp.int32, (QB, KB2), 1)
    dist = qi + WINDOW - kj
    return (dist >= 0) & (dist < WINDOW) & ((n > 0) | (kj >= WINDOW))


def _swa_probs(qh16, kd, bias, sink, valid):
    lg = lax.dot_general(qh16, kd, _DIMS["nt"], preferred_element_type=F32) * (SWA_HD ** -0.5) + bias
    lg = jnp.where(valid, lg, NEG_INF)
    m = jnp.maximum(jnp.max(lg, axis=-1, keepdims=True), sink)
    p = jnp.exp(lg - m)
    es = jnp.exp(sink - m)
    den = jnp.sum(p, axis=-1, keepdims=True) + es
    return p / den, es / den


def _swa_specs():
    q = pl.BlockSpec((QB, D_RNN), lambda n: (n, 0))
    g = pl.BlockSpec((QB, D_RNN), lambda n: (n, 1))
    kvc = pl.BlockSpec((QB, 2 * LANE), lambda n: (n, 8))
    kvp = pl.BlockSpec((QB, 2 * LANE), lambda n: (jnp.maximum(n - 1, 0), 8))
    bias = pl.BlockSpec((SWA_HEADS, QB, KB2), lambda n: (0, 0, 0))
    sinks = pl.BlockSpec(memory_space=pltpu.SMEM)
    return q, g, kvc, kvp, bias, sinks


def _swa_fwd(p_b, bias_t, sinks):
    def body(q_ref, g_ref, kvc_ref, kvp_ref, bias_ref, sink_ref, y_ref, o_ref):
        n = pl.program_id(0)
        lo, kd, vd = _swa_keys(kvc_ref, kvp_ref)
        valid = _swa_valid(n)
        for hp in range(N_PAIR):
            sl = slice(hp * LANE, (hp + 1) * LANE)
            kvh = hp // (N_PAIR // 2)
            q = q_ref[:, sl]
            outs = []
            for j in range(2):
                mh = lo if j == 0 else jnp.logical_not(lo)
                qh16 = jnp.where(mh, q, 0.0).astype(BF16)
                probs, _ = _swa_probs(qh16, kd[kvh], bias_ref[2 * hp + j], sink_ref[2 * hp + j], valid)
                outs.append(jnp.dot(probs.astype(BF16), vd[kvh], preferred_element_type=F32))
            o = jnp.where(lo, outs[0], outs[1])
            o_ref[:, sl] = o
            g = g_ref[:, sl]
            y_ref[:, sl] = (o * (g * _sigmoid(g))).astype(BF16)

    q, g, kvc, kvp, bias, sinks_spec = _swa_specs()
    out = pl.BlockSpec((QB, D_RNN), lambda n: (n, 0))
    return pl.pallas_call(
        body,
        name="swa_fwd",
        grid=(N_QB,),
        in_specs=[q, g, kvc, kvp, bias, sinks_spec],
        out_specs=[out, out],
        out_shape=[jax.ShapeDtypeStruct((S, D_RNN), BF16), jax.ShapeDtypeStruct((S, D_RNN), F32)],
        compiler_params=_params(("parallel",)),
    )(p_b, p_b, p_b, p_b, bias_t, sinks)


def _swa_bwd(dy, p_b, o_swa, bias_t, sinks, after=None):
    def body(dy_ref, q_ref, g_ref, kvc_ref, kvp_ref, o_ref, bias_ref, sink_ref, *rest):
        dp_ref, dk_ref, dv_ref, dbias_ref, dsink_ref = rest[-5:]
        n = pl.program_id(0)

        @pl.when(n == 0)
        def _():
            for ref in (dk_ref, dv_ref, dbias_ref, dsink_ref):
                ref[...] = jnp.zeros_like(ref)

        lo, kd, vd = _swa_keys(kvc_ref, kvp_ref)
        hi = jnp.logical_not(lo)
        valid = _swa_valid(n)
        dk_blk = jnp.zeros((KB2, LANE), F32)
        dv_blk = jnp.zeros((KB2, LANE), F32)
        for kvh in range(2):
            dk_pair = jnp.zeros((KB2, LANE), F32)
            dv_pair = jnp.zeros((KB2, LANE), F32)
            for hp in range(kvh * (N_PAIR // 2), (kvh + 1) * (N_PAIR // 2)):
                sl = slice(hp * LANE, (hp + 1) * LANE)
                q = q_ref[:, sl]
                g = g_ref[:, sl]
                o = o_ref[:, sl]
                dyv = dy_ref[:, sl]
                sg = _sigmoid(g)
                do = dyv * (g * sg)
                dp_ref[:, D_RNN + hp * LANE:D_RNN + (hp + 1) * LANE] = (
                    dyv * o * (sg * (1.0 + g * (1.0 - sg)))).astype(BF16)
                dqs = []
                for j in range(2):
                    h = 2 * hp + j
                    mh = lo if j == 0 else hi
                    qh16 = jnp.where(mh, q, 0.0).astype(BF16)
                    sink = sink_ref[h]
                    probs, psink = _swa_probs(qh16, kd[kvh], bias_ref[h], sink, valid)
                    doh = jnp.where(mh, do, 0.0)
                    doh16 = doh.astype(BF16)
                    delta = jnp.sum(doh * o, axis=-1, keepdims=True)
                    dpr = lax.dot_general(doh16, vd[kvh], _DIMS["nt"], preferred_element_type=F32)
                    ds = probs * (dpr - delta)
                    dbias_ref[h] += ds
                    dsink_ref[h:h + 1, :] += jnp.zeros((1, LANE), F32) - jnp.sum(psink * delta)
                    ds16 = (ds * (SWA_HD ** -0.5)).astype(BF16)
                    dqs.append(jnp.dot(ds16, kd[kvh], preferred_element_type=F32))
                    dk_pair = dk_pair + lax.dot_general(ds16, qh16, _DIMS["tn"], preferred_element_type=F32)
                    dv_pair = dv_pair + lax.dot_general(probs.astype(BF16), doh16, _DIMS["tn"],
                                                        preferred_element_type=F32)
                dp_ref[:, sl] = jnp.where(lo, dqs[0], dqs[1]).astype(BF16)
            keep = lo if kvh == 0 else hi
            dk_blk = dk_blk + jnp.where(keep, dk_pair + pltpu.roll(dk_pair, SWA_HD, 1), 0.0)
            dv_blk = dv_blk + jnp.where(keep, dv_pair + pltpu.roll(dv_pair, SWA_HD, 1), 0.0)

        cur = pl.ds(pl.multiple_of(n * QB, QB), QB)
        dk_ref[cur, :] += dk_blk[QB:KB2]
        dv_ref[cur, :] += dv_blk[QB:KB2]

        @pl.when(n > 0)
        def _():
            prev = pl.ds(pl.multiple_of((n - 1) * QB, QB), QB)
            dk_ref[prev, :] += dk_blk[0:QB]
            dv_ref[prev, :] += dv_blk[0:QB]

    q, g, kvc, kvp, bias, sinks_spec = _swa_specs()
    row = pl.BlockSpec((QB, D_RNN), lambda n: (n, 0))
    acc = pl.BlockSpec((S, LANE), lambda n: (0, 0))
    return pl.pallas_call(
        body,
        name="swa_bwd",
        grid=(N_QB,),
        in_specs=[row, q, g, kvc, kvp, row, bias, sinks_spec] + ([ANY] if after is not None else []),
        out_specs=[pl.BlockSpec((QB, 2 * D_RNN), lambda n: (n, 0)), acc, acc, bias,
                   pl.BlockSpec((SWA_HEADS, LANE), lambda n: (0, 0))],
        out_shape=[jax.ShapeDtypeStruct((S, GROUP_TILES["B"] * LANE), BF16),
                   jax.ShapeDtypeStruct((S, LANE), F32), jax.ShapeDtypeStruct((S, LANE), F32),
                   jax.ShapeDtypeStruct((SWA_HEADS, QB, KB2), F32),
                   jax.ShapeDtypeStruct((SWA_HEADS, LANE), F32)],
        compiler_params=_params(("arbitrary",)),
    )(dy, p_b, p_b, p_b, p_b, o_swa, bias_t, sinks, *([after] if after is not None else []))


def _swa_pack(dp_b, dk, dv, ts=512):
    def body(_, dk_ref, dv_ref, o_ref):
        o_ref[:, 0:LANE] = dk_ref[...].astype(BF16)
        o_ref[:, LANE:2 * LANE] = dv_ref[...].astype(BF16)

    tile = pl.BlockSpec((ts, LANE), lambda i: (i, 0))
    return pl.pallas_call(
        body,
        name="swa_pack",
        grid=(S // ts,),
        in_specs=[pl.BlockSpec(memory_space=pl.ANY), tile, tile],
        out_specs=pl.BlockSpec((ts, 2 * LANE), lambda i: (i, 8)),
        out_shape=jax.ShapeDtypeStruct(dp_b.shape, dp_b.dtype),
        input_output_aliases={0: 0},
        compiler_params=_params(("parallel",)),
    )(dp_b, dk, dv)


def _split3(v):
    a = v.astype(BF16)
    r = v - a.astype(F32)
    b = r.astype(BF16)
    c = (r - b.astype(F32)).astype(BF16)
    return a, b, c


def _relbias_grad(dbias_flat, onehot_t):
    def body(d_ref, e_ref, o_ref):
        e = e_ref[...]
        acc = jnp.zeros((SWA_HEADS, REL_BUCKETS), F32)
        for term in _split3(d_ref[...]):
            acc = acc + lax.dot_general(term, e, _DIMS["nt"], preferred_element_type=F32)
        o_ref[...] = acc

    return pl.pallas_call(
        body,
        name="relbias_grad",
        out_shape=jax.ShapeDtypeStruct((SWA_HEADS, REL_BUCKETS), F32),
        compiler_params=_params(),
    )(dbias_flat, onehot_t)


TS_MEM = 512


def _mem_probs(q16, mk):
    lg = lax.dot_general(q16, mk, _DIMS["nt"], preferred_element_type=F32) * (MEM_HD ** -0.5)
    p = jnp.exp(lg - jnp.max(lg, axis=-1, keepdims=True))
    return p / jnp.sum(p, axis=-1, keepdims=True)


def _mem_fwd(p_c, mkv):
    def body(q_ref, g_ref, mkv_ref, y_ref, o_ref):
        for hm in range(MEM_HEADS):
            sl = slice(hm * MEM_HD, (hm + 1) * MEM_HD)
            probs = _mem_probs(q_ref[:, sl].astype(BF16), mkv_ref[:, sl])
            o = jnp.dot(probs.astype(BF16), mkv_ref[:, D_RNN + hm * MEM_HD:D_RNN + (hm + 1) * MEM_HD],
                        preferred_element_type=F32)
            o_ref[:, sl] = o
            g = g_ref[:, sl]
            y_ref[:, sl] = (o * (g * _sigmoid(g))).astype(BF16)

    blk = lambda c: pl.BlockSpec((TS_MEM, D_RNN), lambda i: (i, c))
    return pl.pallas_call(
        body,
        name="mem_fwd",
        grid=(S // TS_MEM,),
        in_specs=[blk(0), blk(1), pl.BlockSpec((MEM, 2 * D_RNN), lambda i: (0, 0))],
        out_specs=[blk(0), blk(0)],
        out_shape=[jax.ShapeDtypeStruct((S, D_RNN), BF16), jax.ShapeDtypeStruct((S, D_RNN), F32)],
        compiler_params=_params(("parallel",)),
    )(p_c, p_c, mkv)


def _mem_bwd(dy, p_c, o_mem, mkv):
    def body(dy_ref, q_ref, g_ref, o_ref, mkv_ref, dp_ref, dmkv_ref):
        @pl.when(pl.program_id(0) == 0)
        def _():
            dmkv_ref[...] = jnp.zeros_like(dmkv_ref)

        for hm in range(MEM_HEADS):
            sl = slice(hm * MEM_HD, (hm + 1) * MEM_HD)
            sv = slice(D_RNN + hm * MEM_HD, D_RNN + (hm + 1) * MEM_HD)
            q16 = q_ref[:, sl].astype(BF16)
            mk, mv = mkv_ref[:, sl], mkv_ref[:, sv]
            probs = _mem_probs(q16, mk)
            g, o, dyv = g_ref[:, sl], o_ref[:, sl], dy_ref[:, sl]
            sg = _sigmoid(g)
            do = dyv * (g * sg)
            dp_ref[:, sv] = (dyv * o * (sg * (1.0 + g * (1.0 - sg)))).astype(BF16)
            do16 = do.astype(BF16)
            delta = jnp.sum(do * o, axis=-1, keepdims=True)
            dpr = lax.dot_general(do16, mv, _DIMS["nt"], preferred_element_type=F32)
            ds16 = (probs * (dpr - delta) * (MEM_HD ** -0.5)).astype(BF16)
            dp_ref[:, sl] = jnp.dot(ds16, mk, preferred_element_type=F32).astype(BF16)
            dmkv_ref[:, sl] += lax.dot_general(ds16, q16, _DIMS["tn"], preferred_element_type=F32)
            dmkv_ref[:, sv] += lax.dot_general(probs.astype(BF16), do16, _DIMS["tn"], preferred_element_type=F32)

    blk = lambda c: pl.BlockSpec((TS_MEM, D_RNN), lambda i: (i, c))
    kv = pl.BlockSpec((MEM, 2 * D_RNN), lambda i: (0, 0))
    return pl.pallas_call(
        body,
        name="mem_bwd",
        grid=(S // TS_MEM,),
        in_specs=[blk(0), blk(0), blk(1), blk(0), kv],
        out_specs=[pl.BlockSpec((TS_MEM, 2 * D_RNN), lambda i: (i, 0)), kv],
        out_shape=[jax.ShapeDtypeStruct((S, 2 * D_RNN), BF16), jax.ShapeDtypeStruct((MEM, 2 * D_RNN), F32)],
        compiler_params=_params(("arbitrary",)),
    )(dy, p_c, p_c, o_mem, mkv)


TS_MRG = 512
TD_MRG = 512
N_DBLK = D // TD_MRG


def _merge_fwd(z, p_d):
    def body(z0, z1, z2, g0, g1, g2, o_ref):
        o_ref[...] = (_sigmoid(g0[...]) * z0[...] + _sigmoid(g1[...]) * z1[...]
                      + _sigmoid(g2[...]) * z2[...]).astype(BF16)

    blk = pl.BlockSpec((TS_MRG, TD_MRG), lambda i, d: (i, d))
    gate = lambda b: pl.BlockSpec((TS_MRG, TD_MRG), lambda i, d: (i, b * N_DBLK + d))
    return pl.pallas_call(
        body,
        name="merge_fwd",
        grid=(S // TS_MRG, N_DBLK),
        in_specs=[blk, blk, blk, gate(0), gate(1), gate(2)],
        out_specs=blk,
        out_shape=jax.ShapeDtypeStruct((S, D), BF16),
        compiler_params=_params(("parallel", "parallel")),
    )(z[0], z[1], z[2], p_d, p_d, p_d)


def _merge_bwd(dmerged, z_b, p_d, b, dp_d, after=None):
    def body(dm_ref, z_ref, g_ref, *refs):
        dz_ref, dg_ref = refs[-2], refs[-1]
        sg = _sigmoid(g_ref[...])
        dm = dm_ref[...]
        dz_ref[...] = (dm * sg).astype(BF16)
        dg_ref[...] = (dm * z_ref[...] * sg * (1.0 - sg)).astype(BF16)

    blk = pl.BlockSpec((TS_MRG, TD_MRG), lambda i, d: (i, d))
    gate = pl.BlockSpec((TS_MRG, TD_MRG), lambda i, d: (i, b * N_DBLK + d))
    in_specs = [blk, blk, gate]
    args = [dmerged, z_b, p_d]
    aliases = {}
    if dp_d is not None:
        in_specs.append(pl.BlockSpec(memory_space=pl.ANY))
        args.append(dp_d)
        aliases = {3: 1}
    if after is not None:
        in_specs.append(pl.BlockSpec(memory_space=pl.ANY))
        args.append(after)
    return pl.pallas_call(
        body,
        name=f"merge_bwd{b}",
        grid=(S // TS_MRG, N_DBLK),
        in_specs=in_specs,
        out_specs=[blk, gate],
        out_shape=[jax.ShapeDtypeStruct((S, D), BF16),
                   jax.ShapeDtypeStruct((S, GROUP_TILES["D"] * LANE), BF16)],
        input_output_aliases=aliases,
        compiler_params=_params(("parallel", "parallel")),
    )(*args)


def _bucket_table():
    import numpy as np
    qi = np.arange(QB)[:, None]
    kj = np.arange(KB2)[None, :]
    n = np.maximum(qi + WINDOW - kj, 0)
    max_exact = REL_BUCKETS // 2
    ratio = np.log(np.maximum(n, 1).astype(np.float32) / max_exact) / np.float32(math.log(REL_MAX_DIST / max_exact))
    large = np.minimum(max_exact + (ratio * (REL_BUCKETS - max_exact)).astype(np.int32), REL_BUCKETS - 1)
    bucket = np.where(n < max_exact, n, large).reshape(1, QB * KB2)
    return (bucket == np.arange(REL_BUCKETS)[:, None]).astype(np.float32)


def _bias_expand(rel_bias_t, onehot_t):
    def body(r_ref, e_ref, o_ref):
        e = e_ref[...]
        acc = jnp.zeros((SWA_HEADS, QB * KB2), F32)
        for term in _split3(r_ref[...]):
            acc = acc + jnp.dot(term, e, preferred_element_type=F32)
        o_ref[...] = acc

    return pl.pallas_call(
        body,
        name="bias_expand",
        out_shape=jax.ShapeDtypeStruct((SWA_HEADS, QB * KB2), F32),
        compiler_params=_params(),
    )(rel_bias_t, onehot_t)


PROJ_TN = {"A": 1024, "B": 1152, "C": 1024, "D": 1536}


def _local_step(x, mem, tgt, sp, fetch, emit, advance):
    onehot_t = jnp.asarray(_bucket_table(), BF16)
    bias_t = _bias_expand(sp["rel_bias"].T, onehot_t).reshape(SWA_HEADS, QB, KB2)
    sinks = sp["swa_sinks"].reshape(SWA_HEADS)
    wa16, wx16 = sp["w_rg_a"].astype(BF16), sp["w_rg_x"].astype(BF16)
    rnn = (sp["conv_w"], sp["conv_b"], wa16, sp["b_rg_a"], wx16, sp["b_rg_x"], sp["lru_lambda"])

    h = _rms_fwd(x, sp["pre_norm_g"], "rms_pre")
    memn = _rms_fwd(mem, sp["mem_norm_g"], "rms_mem")
    w_grp, p = {}, {}

    def project(g, after):
        (w_grp[g],) = fetch((g,), after)
        p[g] = _mm(h, w_grp[g], "nt", F32, 1024, PROJ_TN[g], D, f"proj_{g}")

    project("A", h)
    y_rg, hseq = _rglru_fwd(p["A"], *rnn)
    project("B", y_rg)
    y_swa, o_swa = _swa_fwd(p["B"], bias_t, sinks)
    project("C", y_swa)
    (wmk,) = fetch(("mk",), p["C"])
    mkv = _mm(memn, wmk, "nn", BF16, MEM, 1024, D, "mkv")
    y_mem, o_mem = _mem_fwd(p["C"], mkv)
    ys = (y_rg, y_swa, y_mem)
    wbr = fetch(("br0", "br1", "br2"), y_mem)
    z = [_mm(ys[b], wbr[b], "nn", F32, 1024, 1024, D_RNN, f"branch_out{b}") for b in range(3)]
    project("D", z[2])
    merged = _merge_fwd(z, p["D"])
    (wout,) = fetch(("out",), merged)
    out = _mm(merged, wout, "nn", F32, 1024, 1024, D, "out_proj")
    sq, dy, dout, d_post = _post_loss(out, x, tgt, sp["post_norm_g"])

    tok = emit({"out": _mm(merged, dout, "tn", BF16, 1024, 1024, S, "d_wout")})
    dmerged = _mm(dout, wout, "nt", F32, 1024, 1024, D, "d_merged", after=tok)
    dz, dp_d = [], None
    tok = advance(dmerged)
    for b in range(3):
        dz_b, dp_d = _merge_bwd(dmerged, z[b], p["D"], b, dp_d, after=tok if b == 0 else None)
        dz.append(dz_b)
    d_win = lambda g, dp_g, after=None: _mm(dp_g, h, "tn", BF16, PROJ_TN[g], 1024, S, f"d_win_{g}", after=after)
    tok = emit({f"br{b}": _mm(ys[b], dz[b], "tn", BF16, 1024, 1024, S, f"d_wbr{b}") for b in range(3)}, tok)
    d_w_d = d_win("D", dp_d, tok)
    tok = emit({"D": d_w_d}, advance(d_w_d))
    dy_mem = _mm(dz[2], wbr[2], "nt", F32, 1024, 1024, D, "d_branch2", after=tok)
    tok = advance(dy_mem)
    dp_c, dmkv = _mem_bwd(dy_mem, p["C"], o_mem, mkv)
    dmkv16 = dmkv.astype(BF16)
    tok = emit({"mk": _mm(memn, dmkv16, "tn", BF16, 1024, 1024, MEM, "d_wmk", after=tok), "C": d_win("C", dp_c)}, tok)
    dmemn = _mm(dmkv16, wmk, "nt", F32, MEM, 1024, D, "d_memn", after=tok)
    tok = advance(dmemn)
    d_memg = _memnorm_bwd(dmemn, mem)
    dy_rg = _mm(dz[0], wbr[0], "nt", F32, 1024, 1024, D, "d_branch0", after=tok)
    dp_a, d_cw, d_cb, d_wa, d_ba, d_wx, d_bx, d_lam = _rglru_bwd(dy_rg, p["A"], hseq, *rnn)
    tok = emit({"A": d_win("A", dp_a)}, tok)
    dy_swa = _mm(dz[1], wbr[1], "nt", F32, 1024, 1024, D, "d_branch1", after=tok)
    tok = advance(dy_swa)
    dp_b, dk, dv, d_bias, d_sink = _swa_bwd(dy_swa, p["B"], o_swa, bias_t, sinks, after=tok)
    dp_b = _swa_pack(dp_b, dk, dv)
    d_rel = _relbias_grad(d_bias.reshape(SWA_HEADS, QB * KB2), onehot_t).T
    dp = {"A": dp_a, "B": dp_b, "C": dp_c, "D": dp_d}
    tok = emit({"B": d_win("B", dp_b)}, tok)
    dh = None
    for g in GROUPS:
        dh = _mm(dp[g], w_grp[g], "nn", F32, 1024, 1024, 2304 if g == "B" else 2048, f"d_h_{g}", acc=dh,
                 after=tok if g in ("A", "B") else None)
        if g == "A":
            tok = advance(dh)
    grad_x, d_pre = _pre_bwd(dh, x, dy, sp["pre_norm_g"])

    d_small = {
        "pre_norm_g": d_pre, "post_norm_g": d_post, "mem_norm_g": d_memg, "conv_w": d_cw, "conv_b": d_cb,
        "w_rg_a": d_wa, "b_rg_a": d_ba, "w_rg_x": d_wx, "b_rg_x": d_bx, "lru_lambda": d_lam,
        "swa_sinks": d_sink[:, 0].reshape(1, SWA_HEADS), "rel_bias": d_rel,
    }
    return sq, grad_x, d_small


ANY = pl.BlockSpec(memory_space=pl.ANY)
SHARD_ROWS = D // N_CHIPS
GATHERED = {"A": (2048, D), "B": (2304, D), "C": (2048, D), "D": (6144, D), "mk": (D, D),
            "br0": (D_RNN, D), "br1": (D_RNN, D), "br2": (D_RNN, D), "out": (D, D)}
SHARD_SHAPES = {"win": (SHARD, D), "mk": (SHARD_ROWS, D), "br0": (D_RNN, SHARD_ROWS), "br1": (D_RNN, SHARD_ROWS),
                "br2": (D_RNN, SHARD_ROWS), "out": (SHARD_ROWS, D)}
SHARDS = tuple(SHARD_SHAPES)
HALF_AXIS = {"win": 1, "mk": 1, "br0": 0, "br1": 0, "br2": 0, "out": 1,
             "A": 1, "B": 1, "C": 1, "D": 1}


def _halved(shape, axis):
    return (shape[0] // 2, shape[1]) if axis == 0 else (shape[0], shape[1] // 2)


class Piece(NamedTuple):
    src: str
    dst: str
    rows: int
    sr0: int
    sc0: int
    dr0: int
    dc0: int
    ncols: int


def _pieces_of(jj):
    out = [Piece("win", g, n, r, 0, gr, 0, D) for r, n, g, gr in _shard_runs(jj)]
    out.append(Piece("mk", "mk", SHARD_ROWS, 0, 0, SHARD_ROWS * jj, 0, D))
    out += [Piece(f"br{b}", f"br{b}", D_RNN, 0, 0, 0, SHARD_ROWS * jj, SHARD_ROWS) for b in range(3)]
    out.append(Piece("out", "out", SHARD_ROWS, 0, 0, SHARD_ROWS * jj, 0, D))
    return out


def _half_rect(ref, p, side, which):
    r0, c0 = (p.sr0, p.sc0) if side == "src" else (p.dr0, p.dc0)
    if HALF_AXIS[p.src] == 1:
        return _rect(ref, r0, p.rows, c0 + which * (p.ncols // 2), p.ncols // 2)
    return _rect(ref, r0 + which * (p.rows // 2), p.rows // 2, c0, p.ncols)


def _rect_in_half(ref, p, side):
    r0, c0 = (p.sr0, p.sc0) if side == "src" else (p.dr0, p.dc0)
    if HALF_AXIS[p.src] == 1:
        return _rect(ref, r0, p.rows, 0, p.ncols // 2)
    return _rect(ref, 0, p.rows // 2, c0, p.ncols)


MAX_PIECES = max(len(_pieces_of(jj)) for jj in range(N_CHIPS))


def _rect(ref, r0, rows, c0, ncols):
    return ref.at[pl.ds(r0, rows), pl.ds(c0, ncols)]


def _position():
    x, y, c = lax.axis_index("x"), lax.axis_index("y"), lax.axis_index("c")
    return x, y, c, 2 * x + y


HBM = pl.BlockSpec(memory_space=pltpu.HBM)
SEM = pl.BlockSpec(memory_space=pltpu.SEMAPHORE)
EFFECT = pltpu.SideEffectType.DATAFLOW_SIDE_EFFECTING
N_SEM = MAX_PIECES * N_CHIPS
GATHER_STAGES = (("A",), ("B",), ("C",), ("mk",), ("br0", "br1", "br2"), ("D",), ("out",))


def _in_hbm(a):
    return pltpu.with_memory_space_constraint(a, pltpu.HBM)


def _stage_pieces(jj, stage):
    return [(i, p) for i, p in enumerate(_pieces_of(jj)) if p.dst in stage]


def _own_block_table(g):
    import numpy as np
    tbl = np.zeros((N_CHIPS, GATHERED[g][0] // HALF_TILE), np.int32)
    for jj in range(N_CHIPS):
        for r, n, grp, gr in _shard_runs(jj):
            if grp == g:
                for k in range(n // HALF_TILE):
                    tbl[jj, gr // HALF_TILE + k] = r // HALF_TILE + k
    return tbl


def _place_group(w_t, g, table):
    nb = GATHERED[g][0] // HALF_TILE

    def body(t_ref, x_ref, o_ref):
        o_ref[...] = x_ref[...].astype(BF16)

    return pl.pallas_call(
        body,
        name=f"place_{g}",
        grid_spec=pltpu.PrefetchScalarGridSpec(
            num_scalar_prefetch=1,
            grid=(nb,),
            in_specs=[pl.BlockSpec((HALF_TILE, D), lambda b, t: (t[b], 0))],
            out_specs=pl.BlockSpec((HALF_TILE, D), lambda b, t: (b, 0)),
        ),
        out_shape=jax.ShapeDtypeStruct(GATHERED[g], BF16),
        compiler_params=_params(("parallel",)),
    )(table, w_t)


def _place_shard(shard, name):
    rows, cols = shard.shape
    by_rows = HALF_AXIS[name] == 1

    def body(x_ref, o_ref):
        o_ref[...] = x_ref[...].astype(BF16)

    return pl.pallas_call(
        body,
        name=f"place_{name}",
        grid=(N_CHIPS,),
        in_specs=[pl.BlockSpec((rows, cols), lambda b: (0, 0))],
        out_specs=pl.BlockSpec((rows, cols), (lambda b: (b, 0)) if by_rows else (lambda b: (0, b))),
        out_shape=jax.ShapeDtypeStruct(GATHERED[name], BF16),
        compiler_params=_params(("parallel",)),
    )(shard)


def _gather_copy(arr, send_sems, recv_sems, c, jj, i, p, kk):
    rect = _half_rect(arr[p.dst], p, "dst", c)
    return pltpu.make_async_remote_copy(
        src_ref=rect, dst_ref=rect, send_sem=send_sems.at[i * N_CHIPS + kk],
        recv_sem=recv_sems.at[jj * MAX_PIECES + i], device_id=(kk // 2, kk % 2, c), device_id_type=MESH)


def _gather_start(arrays, after):
    stage = tuple(arrays)
    na = len(stage)

    def body(*refs):
        arr = dict(zip(stage, refs[:na]))
        send_sems, recv_sems = refs[na + 1], refs[na + 2]
        token = refs[-1]
        _, _, c, j = _position()
        for jj in range(N_CHIPS):
            @pl.when(j == jj)
            def _():
                for i, p in _stage_pieces(jj, stage):
                    for kk in range(N_CHIPS):
                        if kk != jj:
                            _gather_copy(arr, send_sems, recv_sems, c, jj, i, p, kk).start()
        token[...] = jnp.zeros_like(token)

    outs = pl.pallas_call(
        body,
        name=f"gather_start_{stage[0]}",
        in_specs=[HBM] * na + [ANY],
        out_specs=[SEM, SEM] + [HBM] * na + [pl.BlockSpec(memory_space=pltpu.VMEM)],
        out_shape=[pltpu.SemaphoreType.DMA((N_SEM,)), pltpu.SemaphoreType.DMA((N_SEM,))]
        + [pltpu.HBM(GATHERED[n], BF16) for n in stage] + [jax.ShapeDtypeStruct((8, LANE), F32)],
        input_output_aliases={k: 2 + k for k in range(na)},
        compiler_params=pltpu.CompilerParams(has_side_effects=EFFECT),
    )(*[_in_hbm(arrays[n]) for n in stage], after)
    return outs[0], outs[1], dict(zip(stage, outs[2:2 + na])), outs[-1]


def _gather_wait(send_sems, recv_sems, arrays, after):
    stage = tuple(arrays)
    na = len(stage)

    def body(*refs):
        arr = dict(zip(stage, refs[:na]))
        sems_s, sems_r = refs[na], refs[na + 1]
        _, _, c, j = _position()
        for jj in range(N_CHIPS):
            @pl.when(j != jj)
            def _():
                for i, p in _stage_pieces(jj, stage):
                    _gather_copy(arr, sems_s, sems_r, c, jj, i, p, jj).wait_recv()

            @pl.when(j == jj)
            def _():
                for i, p in _stage_pieces(jj, stage):
                    for kk in range(N_CHIPS):
                        if kk != jj:
                            _gather_copy(arr, sems_s, sems_r, c, jj, i, p, kk).wait_send()

    outs = pl.pallas_call(
        body,
        name=f"gather_wait_{stage[0]}",
        in_specs=[HBM] * na + [SEM, SEM, ANY],
        out_specs=[HBM] * na,
        out_shape=[pltpu.HBM(GATHERED[n], BF16) for n in stage],
        input_output_aliases={k: k for k in range(na)},
        compiler_params=pltpu.CompilerParams(has_side_effects=EFFECT),
    )(*[arrays[n] for n in stage], send_sems, recv_sems, after)
    return dict(zip(stage, outs))


def _gather_swap(arrays):
    stage = tuple(arrays)
    na = len(stage)

    def body(*refs):
        dst = dict(zip(stage, refs[na:2 * na]))
        send_sems, recv_sems = refs[2 * na:]
        x, y, c, j = _position()

        def fwd(jj, i, p, which):
            rect = _half_rect(dst[p.dst], p, "dst", which)
            return pltpu.make_async_remote_copy(
                src_ref=rect, dst_ref=rect, send_sem=send_sems.at[jj * MAX_PIECES + i],
                recv_sem=recv_sems.at[jj * MAX_PIECES + i], device_id=(x, y, 1 - c), device_id_type=MESH)

        for jj in range(N_CHIPS):
            @pl.when(j != jj)
            def _():
                for i, p in _stage_pieces(jj, stage):
                    fwd(jj, i, p, c).start()
        for jj in range(N_CHIPS):
            @pl.when(j != jj)
            def _():
                for i, p in _stage_pieces(jj, stage):
                    fwd(jj, i, p, 1 - c).wait_recv()
        for jj in range(N_CHIPS):
            @pl.when(j != jj)
            def _():
                for i, p in _stage_pieces(jj, stage):
                    fwd(jj, i, p, c).wait_send()

    outs = pl.pallas_call(
        body,
        name=f"gather_swap_{stage[0]}",
        in_specs=[ANY] * na,
        out_specs=[ANY] * na,
        out_shape=[jax.ShapeDtypeStruct(GATHERED[n], BF16) for n in stage],
        input_output_aliases={k: k for k in range(na)},
        scratch_shapes=[pltpu.SemaphoreType.DMA((N_SEM,)), pltpu.SemaphoreType.DMA((N_SEM,))],
        compiler_params=pltpu.CompilerParams(has_side_effects=True),
    )(*[arrays[n] for n in stage])
    return dict(zip(stage, outs))


def _own_half(ref, shape, axis, which):
    if axis == 1:
        return ref.at[:, pl.ds(which * (shape[1] // 2), shape[1] // 2)]
    return ref.at[pl.ds(which * (shape[0] // 2), shape[0] // 2), :]


def _swap_copies(names, src, dst, send_sems, recv_sems):
    x, y, c, _ = _position()
    return [pltpu.make_async_remote_copy(
        src_ref=_own_half(src[n], GATHERED[n], HALF_AXIS[n], 1 - c), dst_ref=dst[n],
        send_sem=send_sems.at[k], recv_sem=recv_sems.at[k],
        device_id=(x, y, 1 - c), device_id_type=MESH) for k, n in enumerate(names)]


def _swap_start(grads, after):
    names = tuple(grads)
    n = len(names)

    def body(*refs):
        src, dst = dict(zip(names, refs[:n])), dict(zip(names, refs[n:2 * n]))
        for cp in _swap_copies(names, src, dst, refs[2 * n + 1], refs[2 * n + 2]):
            cp.start()
        refs[-1][...] = jnp.zeros_like(refs[-1])

    half_shape = lambda nm: _halved(GATHERED[nm], HALF_AXIS[nm])
    args = [_in_hbm(grads[nm]) for nm in names] + [_in_hbm(lax.empty(half_shape(nm), BF16)) for nm in names]
    if after is None:
        after = jnp.zeros((8, LANE), F32)
    outs = pl.pallas_call(
        body,
        name=f"swap_start_{names[0]}",
        in_specs=[HBM] * (2 * n) + [ANY],
        out_specs=[SEM, SEM] + [HBM] * (2 * n) + [pl.BlockSpec(memory_space=pltpu.VMEM)],
        out_shape=[pltpu.SemaphoreType.DMA((n,)), pltpu.SemaphoreType.DMA((n,))]
        + [pltpu.HBM(GATHERED[nm], BF16) for nm in names] + [pltpu.HBM(half_shape(nm), BF16) for nm in names]
        + [jax.ShapeDtypeStruct((8, LANE), F32)],
        input_output_aliases={k: 2 + k for k in range(2 * n)},
        compiler_params=pltpu.CompilerParams(has_side_effects=EFFECT),
    )(*args, after)
    return outs[0], outs[1], dict(zip(names, outs[2:2 + n])), dict(zip(names, outs[2 + n:2 + 2 * n])), outs[-1]


def _swap_wait(send_sems, recv_sems, grads, landing, after):
    names = tuple(grads)
    n = len(names)

    def body(*refs):
        src, dst = dict(zip(names, refs[:n])), dict(zip(names, refs[n:2 * n]))
        copies = _swap_copies(names, src, dst, refs[2 * n], refs[2 * n + 1])
        for cp in copies:
            cp.wait_recv()
        for cp in copies:
            cp.wait_send()

    half_shape = lambda nm: _halved(GATHERED[nm], HALF_AXIS[nm])
    outs = pl.pallas_call(
        body,
        name=f"swap_wait_{names[0]}",
        in_specs=[HBM] * (2 * n) + [SEM, SEM, ANY],
        out_specs=[HBM] * (2 * n),
        out_shape=[pltpu.HBM(GATHERED[nm], BF16) for nm in names] + [pltpu.HBM(half_shape(nm), BF16) for nm in names],
        input_output_aliases={k: k for k in range(2 * n)},
        compiler_params=pltpu.CompilerParams(has_side_effects=EFFECT),
    )(*[grads[nm] for nm in names], *[landing[nm] for nm in names], send_sems, recv_sems, after)
    return dict(zip(names, outs[:n])), dict(zip(names, outs[n:]))


ADD_ROWS = 256


def _add_half(full, recv, c_arr, name):
    rows, cols = recv.shape
    if HALF_AXIS[name] == 1:
        index = lambda i, c_ref: (i, c_ref[0])
    else:
        nb = rows // ADD_ROWS
        index = lambda i, c_ref: (nb * c_ref[0] + i, 0)

    def body(c_ref, a_ref, b_ref, o_ref):
        o_ref[...] = (a_ref[...].astype(F32) + b_ref[...].astype(F32)).astype(BF16)

    return pl.pallas_call(
        body,
        name=f"add_half_{name}",
        grid_spec=pltpu.PrefetchScalarGridSpec(
            num_scalar_prefetch=1,
            grid=(rows // ADD_ROWS,),
            in_specs=[pl.BlockSpec((ADD_ROWS, cols), index), pl.BlockSpec((ADD_ROWS, cols), lambda i, c_ref: (i, 0))],
            out_specs=pl.BlockSpec((ADD_ROWS, cols), lambda i, c_ref: (i, 0)),
        ),
        out_shape=jax.ShapeDtypeStruct((rows, cols), BF16),
        compiler_params=_params(("parallel",)),
    )(c_arr, full, recv)


SLOT_SHAPES = {n: _halved(SHARD_SHAPES[n], HALF_AXIS[n]) for n in SHARDS}


def _slot_shape(n):
    return (N_CHIPS,) + SLOT_SHAPES[n]


def _stage_shards(stage):
    pieces = [p for jj in range(N_CHIPS) for p in _pieces_of(jj)]
    return tuple(s for s in SHARDS if any(p.src == s and p.dst in stage for p in pieces))


def _scatter_copy(src, dst, send_sems, recv_sems, c, jj, kk, i, p):
    return pltpu.make_async_remote_copy(
        src_ref=_rect_in_half(src[p.dst], p, "dst"), dst_ref=_rect_in_half(dst[p.src].at[jj], p, "src"),
        send_sem=send_sems.at[kk * MAX_PIECES + i], recv_sem=recv_sems.at[jj * MAX_PIECES + i],
        device_id=(kk // 2, kk % 2, c), device_id_type=MESH)


def _scatter_start(halves, slots):
    stage, touched = tuple(halves), tuple(slots)
    nh, nt = len(stage), len(touched)

    def body(*refs):
        src = dict(zip(stage, refs[:nh]))
        dst = dict(zip(touched, refs[nh:nh + nt]))
        send_sems, recv_sems = refs[nh + nt], refs[nh + nt + 1]
        token = refs[-1]
        _, _, c, j = _position()
        for jj in range(N_CHIPS):
            @pl.when(j == jj)
            def _():
                for kk in range(N_CHIPS):
                    if kk != jj:
                        for i, p in _stage_pieces(kk, stage):
                            _scatter_copy(src, dst, send_sems, recv_sems, c, jj, kk, i, p).start()
        token[...] = jnp.zeros_like(token)

    outs = pl.pallas_call(
        body,
        name=f"scatter_start_{stage[0]}",
        in_specs=[HBM] * (nh + nt),
        out_specs=[SEM, SEM] + [HBM] * (nh + nt) + [pl.BlockSpec(memory_space=pltpu.VMEM)],
        out_shape=[pltpu.SemaphoreType.DMA((N_SEM,)), pltpu.SemaphoreType.DMA((N_SEM,))]
        + [pltpu.HBM(halves[n].shape, BF16) for n in stage] + [pltpu.HBM(_slot_shape(s), BF16) for s in touched]
        + [jax.ShapeDtypeStruct((8, LANE), F32)],
        input_output_aliases={k: 2 + k for k in range(nh + nt)},
        compiler_params=pltpu.CompilerParams(has_side_effects=EFFECT),
    )(*[_in_hbm(halves[n]) for n in stage], *[_in_hbm(slots[s]) for s in touched])
    return outs[0], outs[1], dict(zip(stage, outs[2:2 + nh])), dict(zip(touched, outs[2 + nh:2 + nh + nt])), outs[-1]


def _scatter_wait(send_sems, recv_sems, halves, slots, after):
    stage, touched = tuple(halves), tuple(slots)
    nh, nt = len(stage), len(touched)

    def body(*refs):
        src = dict(zip(stage, refs[:nh]))
        dst = dict(zip(touched, refs[nh:nh + nt]))
        sems_s, sems_r = refs[nh + nt], refs[nh + nt + 1]
        _, _, c, j = _position()
        for jj in range(N_CHIPS):
            @pl.when(j == jj)
            def _():
                for ss in range(N_CHIPS):
                    if ss != jj:
                        for i, p in _stage_pieces(jj, stage):
                            _scatter_copy(src, dst, sems_s, sems_r, c, ss, jj, i, p).wait_recv()
                for kk in range(N_CHIPS):
                    if kk != jj:
                        for i, p in _stage_pieces(kk, stage):
                            _scatter_copy(src, dst, sems_s, sems_r, c, jj, kk, i, p).wait_send()

    outs = pl.pallas_call(
        body,
        name=f"scatter_wait_{stage[0]}",
        in_specs=[HBM] * (nh + nt) + [SEM, SEM, ANY],
        out_specs=[HBM] * (nh + nt),
        out_shape=[pltpu.HBM(halves[n].shape, BF16) for n in stage] + [pltpu.HBM(_slot_shape(s), BF16) for s in touched],
        input_output_aliases={k: k for k in range(nh + nt)},
        compiler_params=pltpu.CompilerParams(has_side_effects=EFFECT),
    )(*[halves[n] for n in stage], *[slots[s] for s in touched], send_sems, recv_sems, after)
    return dict(zip(stage, outs[:nh])), dict(zip(touched, outs[nh:]))


SUM_ROWS = {"win": 448, "mk": 256, "br0": 256, "br1": 256, "br2": 256, "out": 256}


def _sum_in_chip_order(chip, own, s_ref):
    acc = None
    for k in range(N_CHIPS):
        term = jnp.where(chip == k, own, s_ref[k].astype(F32))
        acc = term if acc is None else acc + term
    return acc


def _sum_slots(slots, own_half, pos_arr, name):
    _, rows, cols = slots.shape
    tr = SUM_ROWS[name]
    nb = rows // tr
    if HALF_AXIS[name] == 1:
        own_index = lambda i, pos: (nb * pos[1] + i, 0)
        out_index = lambda i, pos: (i, pos[0])
    else:
        own_index = lambda i, pos: (i, pos[1])
        out_index = lambda i, pos: (nb * pos[0] + i, 0)

    def body(pos, s_ref, own_ref, o_ref):
        o_ref[...] = _sum_in_chip_order(pos[1], own_ref[...].astype(F32), s_ref)

    return pl.pallas_call(
        body,
        name=f"sum_slots_{name}",
        grid_spec=pltpu.PrefetchScalarGridSpec(
            num_scalar_prefetch=1,
            grid=(nb,),
            in_specs=[pl.BlockSpec((N_CHIPS, tr, cols), lambda i, pos: (0, i, 0)),
                      pl.BlockSpec((tr, cols), own_index)],
            out_specs=pl.BlockSpec((tr, cols), out_index),
        ),
        out_shape=jax.ShapeDtypeStruct(SHARD_SHAPES[name], F32),
        compiler_params=_params(("parallel",)),
    )(pos_arr, slots, own_half)


def _own_partial_tables():
    import numpy as np
    nb = SHARD // HALF_TILE
    grp, blk = np.zeros((N_CHIPS, nb), np.int32), np.zeros((N_CHIPS, nb), np.int32)
    for jj in range(N_CHIPS):
        for r, n, g, gr in _shard_runs(jj):
            for k in range(n // HALF_TILE):
                grp[jj, r // HALF_TILE + k] = GROUPS.index(g)
                blk[jj, r // HALF_TILE + k] = gr // HALF_TILE + k
    return grp, blk


def _sum_slots_win(slots, own_halves, pos_arr, grp_tbl, blk_tbl):
    nb = SHARD // HALF_TILE
    cols = D // 2

    def own_spec(gi):
        return pl.BlockSpec((HALF_TILE, cols), lambda b, pos, grp, blk: (jnp.where(grp[b] == gi, blk[b], 0), 0))

    def body(pos, grp, blk, s_ref, a_ref, b_ref, c_ref, d_ref, o_ref):
        g = grp[pl.program_id(0)]
        own = a_ref[...]
        for gi, ref in ((1, b_ref), (2, c_ref), (3, d_ref)):
            own = jnp.where(g == gi, ref[...], own)
        o_ref[...] = _sum_in_chip_order(pos[1], own.astype(F32), s_ref)

    return pl.pallas_call(
        body,
        name="sum_slots_win",
        grid_spec=pltpu.PrefetchScalarGridSpec(
            num_scalar_prefetch=3,
            grid=(nb,),
            in_specs=[pl.BlockSpec((N_CHIPS, HALF_TILE, cols), lambda b, pos, grp, blk: (0, b, 0))]
            + [own_spec(gi) for gi in range(len(GROUPS))],
            out_specs=pl.BlockSpec((HALF_TILE, cols), lambda b, pos, grp, blk: (b, pos[0])),
        ),
        out_shape=jax.ShapeDtypeStruct(SHARD_SHAPES["win"], F32),
        compiler_params=_params(("parallel",)),
    )(pos_arr, grp_tbl, blk_tbl, slots, *[own_halves[g] for g in GROUPS])


def _share_sums(sums):
    def body(*refs):
        bufs = refs[len(SHARDS):2 * len(SHARDS)]
        send_sems, recv_sems = refs[2 * len(SHARDS):]
        x, y, c, _ = _position()
        copies = []
        for k, (n, b) in enumerate(zip(SHARDS, bufs)):
            mine = _own_half(b, SHARD_SHAPES[n], HALF_AXIS[n], c)
            copies.append(pltpu.make_async_remote_copy(
                src_ref=mine, dst_ref=mine, send_sem=send_sems.at[k], recv_sem=recv_sems.at[k],
                device_id=(x, y, 1 - c), device_id_type=MESH))
        for cp in copies:
            cp.start()
        for cp in copies:
            cp.wait_recv()
        for cp in copies:
            cp.wait_send()

    outs = pl.pallas_call(
        body,
        name="share_sums",
        in_specs=[ANY] * len(SHARDS),
        out_specs=[ANY] * len(SHARDS),
        out_shape=[jax.ShapeDtypeStruct(sums[n].shape, F32) for n in SHARDS],
        input_output_aliases={k: k for k in range(len(SHARDS))},
        scratch_shapes=[pltpu.SemaphoreType.DMA((len(SHARDS),)), pltpu.SemaphoreType.DMA((len(SHARDS),))],
        compiler_params=pltpu.CompilerParams(has_side_effects=True),
    )(*[sums[n] for n in SHARDS])
    return dict(zip(SHARDS, outs))


N_DEV = 8


def _all_reduce_small(pack, name):
    rows = pack.shape[0]
    half = rows // 2

    def body(p_ref, o_ref, sib, land, sems):
        x, y, c, j = _position()
        sibling = (x, y, 1 - c)
        swap = pltpu.make_async_remote_copy(src_ref=p_ref, dst_ref=sib, send_sem=sems.at[0], recv_sem=sems.at[1],
                                            device_id=sibling, device_id_type=MESH)
        swap.start()
        swap.wait_recv()
        land[j] = p_ref[...] + sib[...]

        def mine(k, which):
            return land.at[k, pl.ds(which * half, half)]

        def ici(kk):
            return pltpu.make_async_remote_copy(
                src_ref=mine(j, c), dst_ref=mine(j, c), send_sem=sems.at[2 + kk], recv_sem=sems.at[6 + j],
                device_id=(kk // 2, kk % 2, c), device_id_type=MESH)

        def arrival(kk):
            return pltpu.make_async_remote_copy(
                src_ref=mine(kk, c), dst_ref=mine(kk, c), send_sem=sems.at[2 + kk], recv_sem=sems.at[6 + kk],
                device_id=(kk // 2, kk % 2, c), device_id_type=MESH)

        def passed_on(kk, which):
            return pltpu.make_async_remote_copy(
                src_ref=mine(kk, which), dst_ref=mine(kk, which), send_sem=sems.at[10 + kk],
                recv_sem=sems.at[14 + kk], device_id=sibling, device_id_type=MESH)

        for kk in range(N_CHIPS):
            @pl.when(j != kk)
            def _():
                ici(kk).start()
        for kk in range(N_CHIPS):
            @pl.when(j != kk)
            def _():
                arrival(kk).wait_recv()
                passed_on(kk, c).start()
        for kk in range(N_CHIPS):
            @pl.when(j != kk)
            def _():
                passed_on(kk, 1 - c).wait_recv()
        acc = land[0]
        for kk in range(1, N_CHIPS):
            acc = acc + land[kk]
        o_ref[...] = acc
        swap.wait_send()
        for kk in range(N_CHIPS):
            @pl.when(j != kk)
            def _():
                ici(kk).wait_send()
                passed_on(kk, c).wait_send()

    vmem = pl.BlockSpec(memory_space=pltpu.VMEM)
    return pl.pallas_call(
        body,
        name=name,
        in_specs=[vmem],
        out_specs=vmem,
        out_shape=jax.ShapeDtypeStruct((rows, LANE), F32),
        scratch_shapes=[pltpu.VMEM((rows, LANE), F32), pltpu.VMEM((N_CHIPS, rows, LANE), F32),
                        pltpu.SemaphoreType.DMA((18,))],
        compiler_params=pltpu.CompilerParams(has_side_effects=True, vmem_limit_bytes=VMEM_LIMIT),
    )(pack)


def _adamw(w, g, m, v, name, tr):
    rows, cols = w.shape
    tr = min(tr, rows)

    def body(w_ref, g_ref, m_ref, v_ref, go_ref, d_ref, nm_ref, nv_ref):
        gv = g_ref[...]
        go_ref[...] = gv
        nm = ADAM_B1 * m_ref[...] + (1.0 - ADAM_B1) * gv
        nv = ADAM_B2 * v_ref[...] + (1.0 - ADAM_B2) * (gv * gv)
        nm_ref[...] = nm
        nv_ref[...] = nv
        m_hat = nm / (1.0 - ADAM_B1 ** ADAM_STEP)
        v_hat = nv / (1.0 - ADAM_B2 ** ADAM_STEP)
        d_ref[...] = -ADAM_LR * (m_hat / (jnp.sqrt(v_hat) + ADAM_EPS) + ADAM_WD * w_ref[...])

    blk = pl.BlockSpec((tr, cols), lambda i: (i, 0))
    shape = jax.ShapeDtypeStruct((rows, cols), F32)
    return pl.pallas_call(
        body,
        name=f"adamw_{name}",
        grid=(rows // tr,),
        in_specs=[blk] * 4,
        out_specs=[blk] * 4,
        out_shape=[shape] * 4,
        compiler_params=_params(("parallel",)),
    )(w, g, m, v)


SMALL = (("pre_norm_g", (1, D)), ("post_norm_g", (1, D)), ("mem_norm_g", (1, D)), ("conv_w", (CONV_W, D_RNN)),
         ("conv_b", (1, D_RNN)), ("w_rg_a", (RNN_BLOCKS, LANE, LANE)), ("b_rg_a", (1, D_RNN)),
         ("w_rg_x", (RNN_BLOCKS, LANE, LANE)), ("b_rg_x", (1, D_RNN)), ("lru_lambda", (1, D_RNN)),
         ("swa_sinks", (1, SWA_HEADS)), ("rel_bias", (REL_BUCKETS, SWA_HEADS)))
PACK_ROWS = 2176


def _slot_len(shape):
    return -(-math.prod(shape) // LANE) * LANE


def _pack(values):
    parts = []
    for name, shape in SMALL:
        flat = values[name].reshape(-1).astype(F32)
        parts.append(jnp.pad(flat, (0, _slot_len(shape) - flat.shape[0])))
    flat = jnp.concatenate(parts)
    return jnp.pad(flat, (0, PACK_ROWS * LANE - flat.shape[0])).reshape(PACK_ROWS, LANE)


def _unpack(pack, shapes=None):
    flat = pack.reshape(-1)
    out, off = {}, 0
    for name, shape in SMALL:
        shp = shape if shapes is None or name not in shapes else shapes[name]
        out[name] = flat[off:off + math.prod(shp)].reshape(shp)
        off += _slot_len(shape)
    return out


TWIN_WEIGHTS = ("pre_norm_g", "post_norm_g", "mem_norm_g", "w_in", "conv_w", "conv_b", "w_rg_a", "b_rg_a", "w_rg_x",
                "b_rg_x", "lru_lambda", "swa_sinks", "rel_bias", "w_mem_kv", "w_br_rg", "w_br_swa", "w_br_mem", "w_out")
BIG = {"w_in": "win", "w_mem_kv": "mk", "w_br_rg": "br0", "w_br_swa": "br1", "w_br_mem": "br2", "w_out": "out"}


def kernel(x, mem, pre_norm_g, post_norm_g, mem_norm_g, w_in, conv_w, conv_b, w_rg_a, b_rg_a, w_rg_x, b_rg_x, lru_lambda, swa_sinks, rel_bias, w_mem_kv, w_br_rg, w_br_swa, w_br_mem, w_out, loss_target, m_pre_norm_g, m_post_norm_g, m_mem_norm_g, m_w_in, m_conv_w, m_conv_b, m_w_rg_a, m_b_rg_a, m_w_rg_x, m_b_rg_x, m_lru_lambda, m_swa_sinks, m_rel_bias, m_w_mem_kv, m_w_br_rg, m_w_br_swa, m_w_br_mem, m_w_out, v_pre_norm_g, v_post_norm_g, v_mem_norm_g, v_w_in, v_conv_w, v_conv_b, v_w_rg_a, v_b_rg_a, v_w_rg_x, v_b_rg_x, v_lru_lambda, v_swa_sinks, v_rel_bias, v_w_mem_kv, v_w_br_rg, v_w_br_swa, v_w_br_mem, v_w_out):
    args = dict(locals())
    out_shapes = {n: args[n].shape for n in TWIN_WEIGHTS}
    w = {n: (args[n] if n == "rel_bias" else args[n][0]) for n in TWIN_WEIGHTS}
    m = {n: (args["m_" + n] if n == "rel_bias" else args["m_" + n][0]) for n in TWIN_WEIGHTS}
    v = {n: (args["v_" + n] if n == "rel_bias" else args["v_" + n][0]) for n in TWIN_WEIGHTS}
    for d in (w, m, v):
        for n, shape in SMALL:
            if n != "conv_w":
                d[n] = d[n].reshape(shape)

    xi, yi, ci = lax.axis_index("x"), lax.axis_index("y"), lax.axis_index("c")
    chip = 2 * xi + yi
    c_arr = ci.astype(jnp.int32).reshape(1)
    zero = jnp.zeros((), jnp.int32)
    cw0 = (chip * (D_RNN // N_CHIPS)).astype(jnp.int32)

    placed = lax.dynamic_update_slice(jnp.zeros((CONV_W, D_RNN), F32), w["conv_w"], (zero, cw0))
    placed = jnp.where(ci == 0, placed, 0.0).reshape(CONV_W * D_RNN // LANE, LANE)
    conv_w_full = _all_reduce_small(placed, "gather_conv_w").reshape(CONV_W, D_RNN)

    for d in (w, m, v):
        d["w_in"] = d["w_in"].T
    chip_row = lambda tbl: lax.dynamic_slice(jnp.asarray(tbl), (chip.astype(jnp.int32), zero), (1, tbl.shape[1]))[0]
    big_of = {s: n for n, s in BIG.items()}
    ag, token = {}, conv_w_full
    for stage in GATHER_STAGES:
        placed = {n: (_place_group(w["w_in"], n, chip_row(_own_block_table(n))) if n in GROUPS
                      else _place_shard(w[big_of[n]], n)) for n in stage}
        send, recv, in_flight, token = _gather_start(placed, token)
        ag[stage] = (send, recv, in_flight)

    all_started = token

    def fetch(names, after):
        send, recv, in_flight = ag[names]
        after = all_started if names == GATHER_STAGES[0] else after
        ready = _gather_swap(_gather_wait(send, recv, in_flight, after))
        return tuple(ready[n] for n in names)

    rs = {"slots": {}, "halves": {}, "pending": [], "swap": None}

    def emit(grads, after=None):
        assert rs["swap"] is None
        *rs["swap"], token = _swap_start(grads, after)
        return token

    def advance(after):
        grads, received = _swap_wait(*rs["swap"], after)
        rs["swap"] = None
        halves = {n: _add_half(grads[n], received[n], c_arr, n) for n in grads}
        landing = {s: rs["slots"][s] if s in rs["slots"] else lax.empty(_slot_shape(s), BF16)
                   for s in _stage_shards(tuple(grads))}
        send, recv, halves, landing, token = _scatter_start(halves, landing)
        rs["slots"].update(landing)
        rs["pending"].append((send, recv, halves, tuple(landing)))
        return token

    sp = {n: w[n] for n, _ in SMALL}
    sp["conv_w"] = conv_w_full
    sq, grad_x, d_small = _local_step(x[0], mem[0], loss_target[0], sp, fetch, emit, advance)
    loss = lax.psum(sq[0, 0] * (0.5 / D), ("x", "y", "c"))

    small_total = _all_reduce_small(_pack(d_small), "all_reduce_small")

    for send, recv, halves, touched in rs["pending"]:
        halves, landed = _scatter_wait(send, recv, halves, {s: rs["slots"][s] for s in touched}, small_total)
        rs["slots"].update(landed)
        rs["halves"].update(halves)
    pos_arr = jnp.stack([ci, chip]).astype(jnp.int32)
    grp_tbl, blk_tbl = (chip_row(t) for t in _own_partial_tables())
    sums = {s: _sum_slots(rs["slots"][s], rs["halves"][s], pos_arr, s) for s in SHARDS if s != "win"}
    sums["win"] = _sum_slots_win(rs["slots"]["win"], rs["halves"], pos_arr, grp_tbl, blk_tbl)
    sums = _share_sums(sums)
    g_big = {n: sums[s] for n, s in BIG.items()}

    g_small = _unpack(small_total)
    g_small["conv_w"] = lax.dynamic_slice(g_small["conv_w"], (zero, cw0), (CONV_W, D_RNN // N_CHIPS))

    grad, delta, new_m, new_v = {}, {}, {}, {}
    for n, s in BIG.items():
        grad[n], delta[n], new_m[n], new_v[n] = _adamw(w[n], g_big[n], m[n], v[n], s, 224 if n == "w_in" else 128)
    for group in (grad, delta, new_m, new_v):
        group["w_in"] = group["w_in"].T
    _, d_, m_, v_ = _adamw(_pack(w), _pack(g_small), _pack(m), _pack(v), "small", PACK_ROWS)
    shard_shapes = {"conv_w": (CONV_W, D_RNN // N_CHIPS)}
    d_, m_, v_ = (_unpack(a, shard_shapes) for a in (d_, m_, v_))
    for n, _ in SMALL:
        grad[n], delta[n], new_m[n], new_v[n] = g_small[n], d_[n], m_[n], v_[n]

    outs = [loss, grad_x.reshape(1, S, D)]
    for group in (grad, delta, new_m, new_v):
        outs += [group[n].reshape(out_shapes[n]) for n in TWIN_WEIGHTS]
    return tuple(outs)
```

```python
import functools
import math
from typing import NamedTuple

import jax
import jax.numpy as jnp
from jax import lax
from jax.experimental import pallas as pl
from jax.experimental.pallas import tpu as pltpu

F32 = jnp.float32
BF16 = jnp.bfloat16
MESH = pl.DeviceIdType.MESH

S = 2048
D = 2048
MEM = 256
D_RNN = 1024
RNN_BLOCKS = 8
CONV_W = 4
LRU_C = 8.0
SWA_HEADS = 16
SWA_HD = 64
WINDOW = 128
MEM_HEADS = 4
MEM_HD = 256
REL_BUCKETS = 32
REL_MAX_DIST = 128
EPS = 1e-6
NEG_INF = -1e30
LANE = 128
SHARD = 3136
HALF_TILE = 64
N_CHIPS = 4
VMEM_LIMIT = 56 * 1024 * 1024

ADAM_LR = 0.001
ADAM_B1 = 0.9
ADAM_B2 = 0.999
ADAM_EPS = 1e-08
ADAM_WD = 0.01
ADAM_STEP = 10

GROUP_TILES = {"A": 16, "B": 18, "C": 16, "D": 48}
GROUPS = ("A", "B", "C", "D")


def _params(sem=None):
    return pltpu.CompilerParams(dimension_semantics=sem, vmem_limit_bytes=VMEM_LIMIT)


def _sigmoid(v):
    return jax.nn.sigmoid(v)


def _tile_home(t):
    if t < 16:
        return "A", t
    if t < 24:
        return "B", t - 16
    if t < 26:
        return "B", t - 24 + 16
    if t < 34:
        return "B", t - 26 + 8
    if t < 50:
        return "C", t - 34
    return "D", t - 50


def _shard_runs(j):
    runs = []
    per_shard = SHARD // HALF_TILE
    for q in range(per_shard * j, per_shard * (j + 1)):
        g, gt = _tile_home(q // 2)
        row = gt * LANE + (q % 2) * HALF_TILE
        if runs and runs[-1][2] == g and runs[-1][3] + runs[-1][1] == row:
            runs[-1][1] += HALF_TILE
        else:
            runs.append([(q - per_shard * j) * HALF_TILE, HALF_TILE, g, row])
    return [tuple(r) for r in runs]


_DIMS = {
    "nn": (((1,), (0,)), ((), ())),
    "nt": (((1,), (1,)), ((), ())),
    "tn": (((0,), (0,)), ((), ())),
}


def _mm(a, b, mode, out_dtype, tm, tn, tk, name, acc=None, after=None):
    if mode == "nn":
        (m, k), n = a.shape, b.shape[1]
    elif mode == "nt":
        (m, k), n = a.shape, b.shape[0]
    else:
        (k, m), n = a.shape, b.shape[1]
    tm, tn, tk = min(tm, m), min(tn, n), min(tk, k)
    assert m % tm == 0 and n % tn == 0 and k % tk == 0, (name, m, n, k)
    nk = k // tk
    has_acc = acc is not None

    def body(*refs):
        a_ref, b_ref = refs[0], refs[1]
        o_ref = refs[3] if has_acc else refs[2]
        p = lax.dot_general(a_ref[...], b_ref[...], _DIMS[mode], preferred_element_type=F32)

        def finish(v):
            if has_acc:
                v = v + refs[2][...]
            o_ref[...] = v.astype(out_dtype)

        if nk == 1:
            finish(p)
        else:
            s_ref = refs[-1]
            kk = pl.program_id(2)

            @pl.when(kk == 0)
            def _():
                s_ref[...] = p

            @pl.when(kk > 0)
            def _():
                s_ref[...] += p

            @pl.when(kk == nk - 1)
            def _():
                finish(s_ref[...])

    if mode == "nn":
        a_spec = pl.BlockSpec((tm, tk), lambda i, j, kk: (i, kk))
        b_spec = pl.BlockSpec((tk, tn), lambda i, j, kk: (kk, j))
    elif mode == "nt":
        a_spec = pl.BlockSpec((tm, tk), lambda i, j, kk: (i, kk))
        b_spec = pl.BlockSpec((tn, tk), lambda i, j, kk: (j, kk))
    else:
        a_spec = pl.BlockSpec((tk, tm), lambda i, j, kk: (kk, i))
        b_spec = pl.BlockSpec((tk, tn), lambda i, j, kk: (kk, j))
    o_spec = pl.BlockSpec((tm, tn), lambda i, j, kk: (i, j))
    in_specs = [a_spec, b_spec] + ([o_spec] if has_acc else [])
    args = (a, b) + ((acc,) if has_acc else ())
    if after is not None:
        in_specs.append(pl.BlockSpec(memory_space=pl.ANY))
        args += (after,)
    n_in = len(args)
    kernel_body = body

    def body(*refs):
        kernel_body(*(refs[:n_in - (after is not None)] + refs[n_in:]))

    return pl.pallas_call(
        body,
        name=name,
        grid=(m // tm, n // tn, nk),
        in_specs=in_specs,
        out_specs=o_spec,
        out_shape=jax.ShapeDtypeStruct((m, n), out_dtype),
        scratch_shapes=[pltpu.VMEM((tm, tn), F32)] if nk > 1 else [],
        compiler_params=_params(("parallel", "parallel", "arbitrary")),
    )(*args)


def _rms_fwd(x, g, name, ts=256):
    r, d = x.shape

    def body(x_ref, g_ref, o_ref):
        xv = x_ref[...]
        inv = lax.rsqrt(jnp.mean(xv * xv, axis=-1, keepdims=True) + EPS)
        o_ref[...] = (xv * inv * g_ref[...]).astype(BF16)

    return pl.pallas_call(
        body,
        name=name,
        grid=(r // ts,),
        in_specs=[pl.BlockSpec((ts, d), lambda i: (i, 0)), pl.BlockSpec((1, d), lambda i: (0, 0))],
        out_specs=pl.BlockSpec((ts, d), lambda i: (i, 0)),
        out_shape=jax.ShapeDtypeStruct((r, d), BF16),
        compiler_params=_params(("parallel",)),
    )(x, g)


def _post_loss(out, x, tgt, g_post, ts=256):
    n = S // ts

    def body(o_ref, x_ref, t_ref, g_ref, sq_ref, dy_ref, do_ref, dg_ref):
        i = pl.program_id(0)

        @pl.when(i == 0)
        def _():
            sq_ref[...] = jnp.zeros_like(sq_ref)
            dg_ref[...] = jnp.zeros_like(dg_ref)

        ov = o_ref[...]
        g = g_ref[...]
        inv = lax.rsqrt(jnp.mean(ov * ov, axis=-1, keepdims=True) + EPS)
        on = ov * inv
        err = x_ref[...] + on * g - t_ref[...]
        sq_ref[...] += jnp.sum(err * err)
        dy = err * (1.0 / D)
        dy_ref[...] = dy
        dg_ref[...] += jnp.sum(dy * on, axis=0, keepdims=True)
        don = dy * g
        do_ref[...] = (inv * (don - on * jnp.mean(don * on, axis=-1, keepdims=True))).astype(BF16)

    row = pl.BlockSpec((ts, D), lambda i: (i, 0))
    vec = pl.BlockSpec((1, D), lambda i: (0, 0))
    return pl.pallas_call(
        body,
        name="post_loss",
        grid=(n,),
        in_specs=[row, row, row, vec],
        out_specs=[pl.BlockSpec((8, LANE), lambda i: (0, 0)), row, row, vec],
        out_shape=[
            jax.ShapeDtypeStruct((8, LANE), F32),
            jax.ShapeDtypeStruct((S, D), F32),
            jax.ShapeDtypeStruct((S, D), BF16),
            jax.ShapeDtypeStruct((1, D), F32),
        ],
        compiler_params=_params(("arbitrary",)),
    )(out, x, tgt, g_post)


def _pre_bwd(dh, x, dy, g_pre, ts=256):
    n = S // ts

    def body(dh_ref, x_ref, dy_ref, g_ref, gx_ref, dg_ref):
        i = pl.program_id(0)

        @pl.when(i == 0)
        def _():
            dg_ref[...] = jnp.zeros_like(dg_ref)

        xv = x_ref[...]
        dhv = dh_ref[...]
        inv = lax.rsqrt(jnp.mean(xv * xv, axis=-1, keepdims=True) + EPS)
        xn = xv * inv
        dg_ref[...] += jnp.sum(dhv * xn, axis=0, keepdims=True)
        dxn = dhv * g_ref[...]
        gx_ref[...] = dy_ref[...] + inv * (dxn - xn * jnp.mean(dxn * xn, axis=-1, keepdims=True))

    row = pl.BlockSpec((ts, D), lambda i: (i, 0))
    vec = pl.BlockSpec((1, D), lambda i: (0, 0))
    return pl.pallas_call(
        body,
        name="pre_bwd",
        grid=(n,),
        in_specs=[row, row, row, vec],
        out_specs=[row, vec],
        out_shape=[jax.ShapeDtypeStruct((S, D), F32), jax.ShapeDtypeStruct((1, D), F32)],
        compiler_params=_params(("arbitrary",)),
    )(dh, x, dy, g_pre)


def _memnorm_bwd(dmemn, mem):
    def body(d_ref, m_ref, dg_ref):
        mv = m_ref[...]
        inv = lax.rsqrt(jnp.mean(mv * mv, axis=-1, keepdims=True) + EPS)
        dg_ref[...] = jnp.sum(d_ref[...] * mv * inv, axis=0, keepdims=True)

    return pl.pallas_call(
        body,
        name="memnorm_bwd",
        out_shape=jax.ShapeDtypeStruct((1, D), F32),
        compiler_params=_params(),
    )(dmemn, mem)


T_RNN = 256


def _neg_expm1(z):
    poly = -z * (1.0 + z * (0.5 + z * (1.0 / 6 + z * (1.0 / 24 + z * (1.0 / 120 + z * (1.0 / 720))))))
    return jnp.where(z > -0.1, poly, 1.0 - jnp.exp(z))


def _softplus_neg(lam):
    return jnp.maximum(-lam, 0.0) + jnp.log1p(jnp.exp(-jnp.abs(lam)))


def _rnn_gates(conv, wa_ref, ba, wx_ref, bx, lam, first_row):
    cbf = conv.astype(BF16)
    ga, gx = [], []
    for n in range(RNN_BLOCKS):
        c_n = cbf[:, n * LANE:(n + 1) * LANE]
        ga.append(jnp.dot(c_n, wa_ref[n], preferred_element_type=F32))
        gx.append(jnp.dot(c_n, wx_ref[n], preferred_element_type=F32))
    gate_r = _sigmoid(jnp.concatenate(ga, axis=1) + ba)
    gate_i = _sigmoid(jnp.concatenate(gx, axis=1) + bx)
    sp = _softplus_neg(lam)
    log_a = -LRU_C * gate_r * sp
    a = jnp.exp(log_a)
    mult_raw = jnp.sqrt(_neg_expm1(2.0 * log_a))
    mult = jnp.where(first_row, 1.0, mult_raw)
    return cbf, gate_r, gate_i, sp, a, mult_raw, mult


def _rglru_fwd(p_a, conv_w, conv_b, wa, ba, wx, bx, lam):
    t = T_RNN
    n = S // t

    def body(xr_ref, g_ref, cw_ref, cb_ref, wa_ref, ba_ref, wx_ref, bx_ref, lam_ref,
             y_ref, h_ref, xp_s, hcar, a_s, b_s):
        i = pl.program_id(0)

        @pl.when(i == 0)
        def _():
            xp_s[0:8, :] = jnp.zeros((8, D_RNN), F32)
            hcar[...] = jnp.zeros_like(hcar)

        @pl.when(i > 0)
        def _():
            xp_s[0:8, :] = xp_s[t:t + 8, :]

        xp_s[8:8 + t, :] = xr_ref[...]
        conv = cb_ref[...]
        for k in range(CONV_W):
            conv = conv + cw_ref[k:k + 1, :] * xp_s[8 - k:8 - k + t, :]
        rows = i * t + lax.broadcasted_iota(jnp.int32, (t, 1), 0)
        _, _, gate_i, _, a, _, mult = _rnn_gates(
            conv, wa_ref, ba_ref[...], wx_ref, bx_ref[...], lam_ref[...], rows == 0)
        a_s[...] = a
        b_s[...] = mult * gate_i * conv

        def step(tt, h):
            h = a_s[pl.ds(tt, 1), :] * h + b_s[pl.ds(tt, 1), :]
            h_ref[pl.ds(tt, 1), :] = h
            return h

        hcar[...] = lax.fori_loop(0, t, step, hcar[...], unroll=8)
        g = g_ref[...]
        y_ref[...] = (h_ref[...] * (g * _sigmoid(g))).astype(BF16)

    blk = lambda c: pl.BlockSpec((t, D_RNN), lambda i: (i, c))
    full = lambda shape: pl.BlockSpec(shape, lambda i: (0,) * len(shape))
    return pl.pallas_call(
        body,
        name="rglru_fwd",
        grid=(n,),
        in_specs=[blk(0), blk(1), full((CONV_W, D_RNN)), full((1, D_RNN)),
                  full((RNN_BLOCKS, LANE, LANE)), full((1, D_RNN)),
                  full((RNN_BLOCKS, LANE, LANE)), full((1, D_RNN)), full((1, D_RNN))],
        out_specs=[blk(0), blk(0)],
        out_shape=[jax.ShapeDtypeStruct((S, D_RNN), BF16), jax.ShapeDtypeStruct((S, D_RNN), F32)],
        scratch_shapes=[pltpu.VMEM((t + 8, D_RNN), F32), pltpu.VMEM((1, D_RNN), F32),
                        pltpu.VMEM((t, D_RNN), F32), pltpu.VMEM((t, D_RNN), F32)],
        compiler_params=_params(("arbitrary",)),
    )(p_a, p_a, conv_w, conv_b, wa, ba, wx, bx, lam)


def _rglru_bwd(dy, p_a, hseq, conv_w, conv_b, wa, ba, wx, bx, lam):
    t = T_RNN
    n = S // t
    rb = t // 8

    def body(dy_ref, xr_ref, g_ref, h_ref, xrp_ref, hp_ref, cw_ref, cb_ref, wa_ref, ba_ref, wx_ref, bx_ref, lam_ref,
             dp_ref, dcw_ref, dcb_ref, dwa_ref, dba_ref, dwx_ref, dbx_ref, dlam_ref,
             xp_s, hp_s, dxp_s, lamcar, a_s, dh_s, lam_s):
        i = pl.program_id(0)
        r = n - 1 - i

        @pl.when(i == 0)
        def _():
            for ref in (dcw_ref, dcb_ref, dwa_ref, dba_ref, dwx_ref, dbx_ref, dlam_ref, lamcar):
                ref[...] = jnp.zeros_like(ref)
            dxp_s[t:t + 8, :] = jnp.zeros((8, D_RNN), F32)

        @pl.when(i > 0)
        def _():
            dxp_s[t:t + 8, :] = dxp_s[0:8, :]

        has_prev = r > 0
        xp_s[0:8, :] = jnp.where(has_prev, xrp_ref[...], 0.0)
        xp_s[8:8 + t, :] = xr_ref[...]
        hp_s[0:8, :] = jnp.where(has_prev, hp_ref[...], 0.0)
        hp_s[8:8 + t, :] = h_ref[...]
        xs = [xp_s[8 - k:8 - k + t, :] for k in range(CONV_W)]
        conv = cb_ref[...]
        for k in range(CONV_W):
            conv = conv + cw_ref[k:k + 1, :] * xs[k]
        rows = r * t + lax.broadcasted_iota(jnp.int32, (t, 1), 0)
        first = rows == 0
        lam_p = lam_ref[...]
        cbf, gate_r, gate_i, sp, a, mult_raw, mult = _rnn_gates(
            conv, wa_ref, ba_ref[...], wx_ref, bx_ref[...], lam_p, first)

        g = g_ref[...]
        sg = _sigmoid(g)
        dyv = dy_ref[...]
        a_s[...] = a
        dh_s[...] = dyv * (g * sg)
        dg = dyv * h_ref[...] * (sg * (1.0 + g * (1.0 - sg)))

        def step(jj, car):
            tt = t - 1 - jj
            lm = dh_s[pl.ds(tt, 1), :] + car
            lam_s[pl.ds(tt, 1), :] = lm
            return a_s[pl.ds(tt, 1), :] * lm

        lamcar[...] = lax.fori_loop(0, t, step, lamcar[...], unroll=8)
        db = lam_s[...]
        da = db * hp_s[7:7 + t, :]
        dmult = db * gate_i * conv
        dgate_i = db * mult * conv
        dconv = db * mult * gate_i
        dlog_a = da * a + jnp.where(first, 0.0, dmult * (-(a * a) / mult_raw))
        dgate_r = dlog_a * (-LRU_C * sp)
        dsp = jnp.sum(dlog_a * (-LRU_C * gate_r), axis=0, keepdims=True)
        dlam_ref[...] += dsp * (-_sigmoid(-lam_p))
        dga = dgate_r * gate_r * (1.0 - gate_r)
        dgx = dgate_i * gate_i * (1.0 - gate_i)
        dba_ref[...] += jnp.sum(dga, axis=0, keepdims=True)
        dbx_ref[...] += jnp.sum(dgx, axis=0, keepdims=True)
        dga16, dgx16 = dga.astype(BF16), dgx.astype(BF16)
        back = []
        for nb in range(RNN_BLOCKS):
            sl = slice(nb * LANE, (nb + 1) * LANE)
            dwa_ref[nb] += lax.dot_general(cbf[:, sl], dga16[:, sl], _DIMS["tn"], preferred_element_type=F32)
            dwx_ref[nb] += lax.dot_general(cbf[:, sl], dgx16[:, sl], _DIMS["tn"], preferred_element_type=F32)
            back.append(lax.dot_general(dga16[:, sl], wa_ref[nb], _DIMS["nt"], preferred_element_type=F32)
                        + lax.dot_general(dgx16[:, sl], wx_ref[nb], _DIMS["nt"], preferred_element_type=F32))
        dconv = dconv + jnp.concatenate(back, axis=1)
        dcb_ref[...] += jnp.sum(dconv, axis=0, keepdims=True)
        for k in range(CONV_W):
            dcw_ref[k:k + 1, :] += jnp.sum(dconv * xs[k], axis=0, keepdims=True)
        dxp_s[0:t, :] = dconv
        dxr = cw_ref[0:1, :] * dconv
        for k in range(1, CONV_W):
            dxr = dxr + cw_ref[k:k + 1, :] * dxp_s[k:k + t, :]
        dp_ref[:, 0:D_RNN] = dxr.astype(BF16)
        dp_ref[:, D_RNN:2 * D_RNN] = dg.astype(BF16)

    blk = lambda c: pl.BlockSpec((t, D_RNN), lambda i: (n - 1 - i, c))
    prev8 = pl.BlockSpec((8, D_RNN), lambda i: (jnp.maximum((n - 1 - i) * rb - 1, 0), 0))
    full = lambda shape: pl.BlockSpec(shape, lambda i: (0,) * len(shape))
    vec = full((1, D_RNN))
    mat = full((RNN_BLOCKS, LANE, LANE))
    return pl.pallas_call(
        body,
        name="rglru_bwd",
        grid=(n,),
        in_specs=[blk(0), blk(0), blk(1), blk(0), prev8, prev8,
                  full((CONV_W, D_RNN)), vec, mat, vec, mat, vec, vec],
        out_specs=[pl.BlockSpec((t, 2 * D_RNN), lambda i: (n - 1 - i, 0)),
                   full((CONV_W, D_RNN)), vec, mat, vec, mat, vec, vec],
        out_shape=[jax.ShapeDtypeStruct((S, 2 * D_RNN), BF16),
                   jax.ShapeDtypeStruct((CONV_W, D_RNN), F32), jax.ShapeDtypeStruct((1, D_RNN), F32),
                   jax.ShapeDtypeStruct((RNN_BLOCKS, LANE, LANE), F32), jax.ShapeDtypeStruct((1, D_RNN), F32),
                   jax.ShapeDtypeStruct((RNN_BLOCKS, LANE, LANE), F32), jax.ShapeDtypeStruct((1, D_RNN), F32),
                   jax.ShapeDtypeStruct((1, D_RNN), F32)],
        scratch_shapes=[pltpu.VMEM((t + 8, D_RNN), F32), pltpu.VMEM((t + 8, D_RNN), F32),
                        pltpu.VMEM((t + 8, D_RNN), F32), pltpu.VMEM((1, D_RNN), F32),
                        pltpu.VMEM((t, D_RNN), F32), pltpu.VMEM((t, D_RNN), F32), pltpu.VMEM((t, D_RNN), F32)],
        compiler_params=_params(("arbitrary",)),
    )(dy, p_a, p_a, hseq, p_a, hseq, conv_w, conv_b, wa, ba, wx, bx, lam)


QB = WINDOW
KB2 = 2 * WINDOW
N_QB = S // QB
N_PAIR = SWA_HEADS // 2


def _swa_keys(kvc_ref, kvp_ref):
    kk = jnp.concatenate([kvp_ref[:, 0:LANE], kvc_ref[:, 0:LANE]], axis=0)
    vv = jnp.concatenate([kvp_ref[:, LANE:2 * LANE], kvc_ref[:, LANE:2 * LANE]], axis=0)
    lo = lax.broadcasted_iota(jnp.int32, (1, LANE), 1) < SWA_HD
    kk_sw, vv_sw = pltpu.roll(kk, SWA_HD, 1), pltpu.roll(vv, SWA_HD, 1)
    kd = [jnp.where(lo, kk, kk_sw).astype(BF16), jnp.where(lo, kk_sw, kk).astype(BF16)]
    vd = [jnp.where(lo, vv, vv_sw).astype(BF16), jnp.where(lo, vv_sw, vv).astype(BF16)]
    return lo, kd, vd


def _swa_valid(n):
    qi = lax.broadcasted_iota(jnp.int32, (QB, KB2), 0)
    kj = lax.broadcasted_iota(jnp.int32, (QB, KB2), 1)
    dist = qi + WINDOW - kj
    return (dist >= 0) & (dist < WINDOW) & ((n > 0) | (kj >= WINDOW))


def _swa_probs(qh16, kd, bias, sink, valid):
    lg = lax.dot_general(qh16, kd, _DIMS["nt"], preferred_element_type=F32) * (SWA_HD ** -0.5) + bias
    lg = jnp.where(valid, lg, NEG_INF)
    m = jnp.maximum(jnp.max(lg, axis=-1, keepdims=True), sink)
    p = jnp.exp(lg - m)
    es = jnp.exp(sink - m)
    den = jnp.sum(p, axis=-1, keepdims=True) + es
    return p / den, es / den


def _swa_specs():
    q = pl.BlockSpec((QB, D_RNN), lambda n: (n, 0))
    g = pl.BlockSpec((QB, D_RNN), lambda n: (n, 1))
    kvc = pl.BlockSpec((QB, 2 * LANE), lambda n: (n, 8))
    kvp = pl.BlockSpec((QB, 2 * LANE), lambda n: (jnp.maximum(n - 1, 0), 8))
    bias = pl.BlockSpec((SWA_HEADS, QB, KB2), lambda n: (0, 0, 0))
    sinks = pl.BlockSpec(memory_space=pltpu.SMEM)
    return q, g, kvc, kvp, bias, sinks


def _swa_fwd(p_b, bias_t, sinks):
    def body(q_ref, g_ref, kvc_ref, kvp_ref, bias_ref, sink_ref, y_ref, o_ref):
        n = pl.program_id(0)
        lo, kd, vd = _swa_keys(kvc_ref, kvp_ref)
        valid = _swa_valid(n)
        for hp in range(N_PAIR):
            sl = slice(hp * LANE, (hp + 1) * LANE)
            kvh = hp // (N_PAIR // 2)
            q = q_ref[:, sl]
            outs = []
            for j in range(2):
                mh = lo if j == 0 else jnp.logical_not(lo)
                qh16 = jnp.where(mh, q, 0.0).astype(BF16)
                probs, _ = _swa_probs(qh16, kd[kvh], bias_ref[2 * hp + j], sink_ref[2 * hp + j], valid)
                outs.append(jnp.dot(probs.astype(BF16), vd[kvh], preferred_element_type=F32))
            o = jnp.where(lo, outs[0], outs[1])
            o_ref[:, sl] = o
            g = g_ref[:, sl]
            y_ref[:, sl] = (o * (g * _sigmoid(g))).astype(BF16)

    q, g, kvc, kvp, bias, sinks_spec = _swa_specs()
    out = pl.BlockSpec((QB, D_RNN), lambda n: (n, 0))
    return pl.pallas_call(
        body,
        name="swa_fwd",
        grid=(N_QB,),
        in_specs=[q, g, kvc, kvp, bias, sinks_spec],
        out_specs=[out, out],
        out_shape=[jax.ShapeDtypeStruct((S, D_RNN), BF16), jax.ShapeDtypeStruct((S, D_RNN), F32)],
        compiler_params=_params(("parallel",)),
    )(p_b, p_b, p_b, p_b, bias_t, sinks)


def _swa_bwd(dy, p_b, o_swa, bias_t, sinks, after=None):
    def body(dy_ref, q_ref, g_ref, kvc_ref, kvp_ref, o_ref, bias_ref, sink_ref, *rest):
        dp_ref, dk_ref, dv_ref, dbias_ref, dsink_ref = rest[-5:]
        n = pl.program_id(0)

        @pl.when(n == 0)
        def _():
            for ref in (dk_ref, dv_ref, dbias_ref, dsink_ref):
                ref[...] = jnp.zeros_like(ref)

        lo, kd, vd = _swa_keys(kvc_ref, kvp_ref)
        hi = jnp.logical_not(lo)
        valid = _swa_valid(n)
        dk_blk = jnp.zeros((KB2, LANE), F32)
        dv_blk = jnp.zeros((KB2, LANE), F32)
        for kvh in range(2):
            dk_pair = jnp.zeros((KB2, LANE), F32)
            dv_pair = jnp.zeros((KB2, LANE), F32)
            for hp in range(kvh * (N_PAIR // 2), (kvh + 1) * (N_PAIR // 2)):
                sl = slice(hp * LANE, (hp + 1) * LANE)
                q = q_ref[:, sl]
                g = g_ref[:, sl]
                o = o_ref[:, sl]
                dyv = dy_ref[:, sl]
                sg = _sigmoid(g)
                do = dyv * (g * sg)
                dp_ref[:, D_RNN + hp * LANE:D_RNN + (hp + 1) * LANE] = (
                    dyv * o * (sg * (1.0 + g * (1.0 - sg)))).astype(BF16)
                dqs = []
                for j in range(2):
                    h = 2 * hp + j
                    mh = lo if j == 0 else hi
                    qh16 = jnp.where(mh, q, 0.0).astype(BF16)
                    sink = sink_ref[h]
                    probs, psink = _swa_probs(qh16, kd[kvh], bias_ref[h], sink, valid)
                    doh = jnp.where(mh, do, 0.0)
                    doh16 = doh.astype(BF16)
                    delta = jnp.sum(doh * o, axis=-1, keepdims=True)
                    dpr = lax.dot_general(doh16, vd[kvh], _DIMS["nt"], preferred_element_type=F32)
                    ds = probs * (dpr - delta)
                    dbias_ref[h] += ds
                    dsink_ref[h:h + 1, :] += jnp.zeros((1, LANE), F32) - jnp.sum(psink * delta)
                    ds16 = (ds * (SWA_HD ** -0.5)).astype(BF16)
                    dqs.append(jnp.dot(ds16, kd[kvh], preferred_element_type=F32))
                    dk_pair = dk_pair + lax.dot_general(ds16, qh16, _DIMS["tn"], preferred_element_type=F32)
                    dv_pair = dv_pair + lax.dot_general(probs.astype(BF16), doh16, _DIMS["tn"],
                                                        preferred_element_type=F32)
                dp_ref[:, sl] = jnp.where(lo, dqs[0], dqs[1]).astype(BF16)
            keep = lo if kvh == 0 else hi
            dk_blk = dk_blk + jnp.where(keep, dk_pair + pltpu.roll(dk_pair, SWA_HD, 1), 0.0)
            dv_blk = dv_blk + jnp.where(keep, dv_pair + pltpu.roll(dv_pair, SWA_HD, 1), 0.0)

        cur = pl.ds(pl.multiple_of(n * QB, QB), QB)
        dk_ref[cur, :] += dk_blk[QB:KB2]
        dv_ref[cur, :] += dv_blk[QB:KB2]

        @pl.when(n > 0)
        def _():
            prev = pl.ds(pl.multiple_of((n - 1) * QB, QB), QB)
            dk_ref[prev, :] += dk_blk[0:QB]
            dv_ref[prev, :] += dv_blk[0:QB]

    q, g, kvc, kvp, bias, sinks_spec = _swa_specs()
    row = pl.BlockSpec((QB, D_RNN), lambda n: (n, 0))
    acc = pl.BlockSpec((S, LANE), lambda n: (0, 0))
    return pl.pallas_call(
        body,
        name="swa_bwd",
        grid=(N_QB,),
        in_specs=[row, q, g, kvc, kvp, row, bias, sinks_spec] + ([ANY] if after is not None else []),
        out_specs=[pl.BlockSpec((QB, 2 * D_RNN), lambda n: (n, 0)), acc, acc, bias,
                   pl.BlockSpec((SWA_HEADS, LANE), lambda n: (0, 0))],
        out_shape=[jax.ShapeDtypeStruct((S, GROUP_TILES["B"] * LANE), BF16),
                   jax.ShapeDtypeStruct((S, LANE), F32), jax.ShapeDtypeStruct((S, LANE), F32),
                   jax.ShapeDtypeStruct((SWA_HEADS, QB, KB2), F32),
                   jax.ShapeDtypeStruct((SWA_HEADS, LANE), F32)],
        compiler_params=_params(("arbitrary",)),
    )(dy, p_b, p_b, p_b, p_b, o_swa, bias_t, sinks, *([after] if after is not None else []))


def _swa_pack(dp_b, dk, dv, ts=512):
    def body(_, dk_ref, dv_ref, o_ref):
        o_ref[:, 0:LANE] = dk_ref[...].astype(BF16)
        o_ref[:, LANE:2 * LANE] = dv_ref[...].astype(BF16)

    tile = pl.BlockSpec((ts, LANE), lambda i: (i, 0))
    return pl.pallas_call(
        body,
        name="swa_pack",
        grid=(S // ts,),
        in_specs=[pl.BlockSpec(memory_space=pl.ANY), tile, tile],
        out_specs=pl.BlockSpec((ts, 2 * LANE), lambda i: (i, 8)),
        out_shape=jax.ShapeDtypeStruct(dp_b.shape, dp_b.dtype),
        input_output_aliases={0: 0},
        compiler_params=_params(("parallel",)),
    )(dp_b, dk, dv)


def _split3(v):
    a = v.astype(BF16)
    r = v - a.astype(F32)
    b = r.astype(BF16)
    c = (r - b.astype(F32)).astype(BF16)
    return a, b, c


def _relbias_grad(dbias_flat, onehot_t):
    def body(d_ref, e_ref, o_ref):
        e = e_ref[...]
        acc = jnp.zeros((SWA_HEADS, REL_BUCKETS), F32)
        for term in _split3(d_ref[...]):
            acc = acc + lax.dot_general(term, e, _DIMS["nt"], preferred_element_type=F32)
        o_ref[...] = acc

    return pl.pallas_call(
        body,
        name="relbias_grad",
        out_shape=jax.ShapeDtypeStruct((SWA_HEADS, REL_BUCKETS), F32),
        compiler_params=_params(),
    )(dbias_flat, onehot_t)


TS_MEM = 512


def _mem_probs(q16, mk):
    lg = lax.dot_general(q16, mk, _DIMS["nt"], preferred_element_type=F32) * (MEM_HD ** -0.5)
    p = jnp.exp(lg - jnp.max(lg, axis=-1, keepdims=True))
    return p / jnp.sum(p, axis=-1, keepdims=True)


def _mem_fwd(p_c, mkv):
    def body(q_ref, g_ref, mkv_ref, y_ref, o_ref):
        for hm in range(MEM_HEADS):
            sl = slice(hm * MEM_HD, (hm + 1) * MEM_HD)
            probs = _mem_probs(q_ref[:, sl].astype(BF16), mkv_ref[:, sl])
            o = jnp.dot(probs.astype(BF16), mkv_ref[:, D_RNN + hm * MEM_HD:D_RNN + (hm + 1) * MEM_HD],
                        preferred_element_type=F32)
            o_ref[:, sl] = o
            g = g_ref[:, sl]
            y_ref[:, sl] = (o * (g * _sigmoid(g))).astype(BF16)

    blk = lambda c: pl.BlockSpec((TS_MEM, D_RNN), lambda i: (i, c))
    return pl.pallas_call(
        body,
        name="mem_fwd",
        grid=(S // TS_MEM,),
        in_specs=[blk(0), blk(1), pl.BlockSpec((MEM, 2 * D_RNN), lambda i: (0, 0))],
        out_specs=[blk(0), blk(0)],
        out_shape=[jax.ShapeDtypeStruct((S, D_RNN), BF16), jax.ShapeDtypeStruct((S, D_RNN), F32)],
        compiler_params=_params(("parallel",)),
    )(p_c, p_c, mkv)


def _mem_bwd(dy, p_c, o_mem, mkv):
    def body(dy_ref, q_ref, g_ref, o_ref, mkv_ref, dp_ref, dmkv_ref):
        @pl.when(pl.program_id(0) == 0)
        def _():
            dmkv_ref[...] = jnp.zeros_like(dmkv_ref)

        for hm in range(MEM_HEADS):
            sl = slice(hm * MEM_HD, (hm + 1) * MEM_HD)
            sv = slice(D_RNN + hm * MEM_HD, D_RNN + (hm + 1) * MEM_HD)
            q16 = q_ref[:, sl].astype(BF16)
            mk, mv = mkv_ref[:, sl], mkv_ref[:, sv]
            probs = _mem_probs(q16, mk)
            g, o, dyv = g_ref[:, sl], o_ref[:, sl], dy_ref[:, sl]
            sg = _sigmoid(g)
            do = dyv * (g * sg)
            dp_ref[:, sv] = (dyv * o * (sg * (1.0 + g * (1.0 - sg)))).astype(BF16)
            do16 = do.astype(BF16)
            delta = jnp.sum(do * o, axis=-1, keepdims=True)
            dpr = lax.dot_general(do16, mv, _DIMS["nt"], preferred_element_type=F32)
            ds16 = (probs * (dpr - delta) * (MEM_HD ** -0.5)).astype(BF16)
            dp_ref[:, sl] = jnp.dot(ds16, mk, preferred_element_type=F32).astype(BF16)
            dmkv_ref[:, sl] += lax.dot_general(ds16, q16, _DIMS["tn"], preferred_element_type=F32)
            dmkv_ref[:, sv] += lax.dot_general(probs.astype(BF16), do16, _DIMS["tn"], preferred_element_type=F32)

    blk = lambda c: pl.BlockSpec((TS_MEM, D_RNN), lambda i: (i, c))
    kv = pl.BlockSpec((MEM, 2 * D_RNN), lambda i: (0, 0))
    return pl.pallas_call(
        body,
        name="mem_bwd",
        grid=(S // TS_MEM,),
        in_specs=[blk(0), blk(0), blk(1), blk(0), kv],
        out_specs=[pl.BlockSpec((TS_MEM, 2 * D_RNN), lambda i: (i, 0)), kv],
        out_shape=[jax.ShapeDtypeStruct((S, 2 * D_RNN), BF16), jax.ShapeDtypeStruct((MEM, 2 * D_RNN), F32)],
        compiler_params=_params(("arbitrary",)),
    )(dy, p_c, p_c, o_mem, mkv)


TS_MRG = 512
TD_MRG = 512
N_DBLK = D // TD_MRG


def _merge_fwd(z, p_d):
    def body(z0, z1, z2, g0, g1, g2, o_ref):
        o_ref[...] = (_sigmoid(g0[...]) * z0[...] + _sigmoid(g1[...]) * z1[...]
                      + _sigmoid(g2[...]) * z2[...]).astype(BF16)

    blk = pl.BlockSpec((TS_MRG, TD_MRG), lambda i, d: (i, d))
    gate = lambda b: pl.BlockSpec((TS_MRG, TD_MRG), lambda i, d: (i, b * N_DBLK + d))
    return pl.pallas_call(
        body,
        name="merge_fwd",
        grid=(S // TS_MRG, N_DBLK),
        in_specs=[blk, blk, blk, gate(0), gate(1), gate(2)],
        out_specs=blk,
        out_shape=jax.ShapeDtypeStruct((S, D), BF16),
        compiler_params=_params(("parallel", "parallel")),
    )(z[0], z[1], z[2], p_d, p_d, p_d)


def _merge_bwd(dmerged, z_b, p_d, b, dp_d, after=None):
    def body(dm_ref, z_ref, g_ref, *refs):
        dz_ref, dg_ref = refs[-2], refs[-1]
        sg = _sigmoid(g_ref[...])
        dm = dm_ref[...]
        dz_ref[...] = (dm * sg).astype(BF16)
        dg_ref[...] = (dm * z_ref[...] * sg * (1.0 - sg)).astype(BF16)

    blk = pl.BlockSpec((TS_MRG, TD_MRG), lambda i, d: (i, d))
    gate = pl.BlockSpec((TS_MRG, TD_MRG), lambda i, d: (i, b * N_DBLK + d))
    in_specs = [blk, blk, gate]
    args = [dmerged, z_b, p_d]
    aliases = {}
    if dp_d is not None:
        in_specs.append(pl.BlockSpec(memory_space=pl.ANY))
        args.append(dp_d)
        aliases = {3: 1}
    if after is not None:
        in_specs.append(pl.BlockSpec(memory_space=pl.ANY))
        args.append(after)
    return pl.pallas_call(
        body,
        name=f"merge_bwd{b}",
        grid=(S // TS_MRG, N_DBLK),
        in_specs=in_specs,
        out_specs=[blk, gate],
        out_shape=[jax.ShapeDtypeStruct((S, D), BF16),
                   jax.ShapeDtypeStruct((S, GROUP_TILES["D"] * LANE), BF16)],
        input_output_aliases=aliases,
        compiler_params=_params(("parallel", "parallel")),
    )(*args)


def _bucket_table():
    import numpy as np
    qi = np.arange(QB)[:, None]
    kj = np.arange(KB2)[None, :]
    n = np.maximum(qi + WINDOW - kj, 0)
    max_exact = REL_BUCKETS // 2
    ratio = np.log(np.maximum(n, 1).astype(np.float32) / max_exact) / np.float32(math.log(REL_MAX_DIST / max_exact))
    large = np.minimum(max_exact + (ratio * (REL_BUCKETS - max_exact)).astype(np.int32), REL_BUCKETS - 1)
    bucket = np.where(n < max_exact, n, large).reshape(1, QB * KB2)
    return (bucket == np.arange(REL_BUCKETS)[:, None]).astype(np.float32)


def _bias_expand(rel_bias_t, onehot_t):
    def body(r_ref, e_ref, o_ref):
        e = e_ref[...]
        acc = jnp.zeros((SWA_HEADS, QB * KB2), F32)
        for term in _split3(r_ref[...]):
            acc = acc + jnp.dot(term, e, preferred_element_type=F32)
        o_ref[...] = acc

    return pl.pallas_call(
        body,
        name="bias_expand",
        out_shape=jax.ShapeDtypeStruct((SWA_HEADS, QB * KB2), F32),
        compiler_params=_params(),
    )(rel_bias_t, onehot_t)


PROJ_TN = {"A": 1024, "B": 1152, "C": 1024, "D": 1536}


def _local_step(x, mem, tgt, sp, fetch, prefetch, emit, advance):
    onehot_t = jnp.asarray(_bucket_table(), BF16)
    bias_t = _bias_expand(sp["rel_bias"].T, onehot_t).reshape(SWA_HEADS, QB, KB2)
    sinks = sp["swa_sinks"].reshape(SWA_HEADS)
    wa16, wx16 = sp["w_rg_a"].astype(BF16), sp["w_rg_x"].astype(BF16)
    rnn = (sp["conv_w"], sp["conv_b"], wa16, sp["b_rg_a"], wx16, sp["b_rg_x"], sp["lru_lambda"])

    h = _rms_fwd(x, sp["pre_norm_g"], "rms_pre")
    memn = _rms_fwd(mem, sp["mem_norm_g"], "rms_mem")
    w_grp, p = {}, {}

    def project(g, after, then=None):
        (w_grp[g],) = fetch((g,), after)
        tok = prefetch(then, w_grp[g]) if then is not None else None
        p[g] = _mm(h, w_grp[g], "nt", F32, 1024, PROJ_TN[g], D, f"proj_{g}", after=tok)

    project("A", h)
    y_rg, hseq = _rglru_fwd(p["A"], *rnn)
    project("B", y_rg)
    y_swa, o_swa = _swa_fwd(p["B"], bias_t, sinks)
    project("C", y_swa, then=("mk",))
    (wmk,) = fetch(("mk",), p["C"])
    tok = prefetch(("br0", "br1", "br2"), wmk)
    mkv = _mm(memn, wmk, "nn", BF16, MEM, 1024, D, "mkv", after=tok)
    y_mem, o_mem = _mem_fwd(p["C"], mkv)
    ys = (y_rg, y_swa, y_mem)
    wbr = fetch(("br0", "br1", "br2"), y_mem)
    tok = prefetch(("D",), wbr[2])
    z = [_mm(ys[b], wbr[b], "nn", F32, 1024, 1024, D_RNN, f"branch_out{b}", after=tok if b == 0 else None)
         for b in range(3)]
    project("D", z[2], then=("out",))
    merged = _merge_fwd(z, p["D"])
    (wout,) = fetch(("out",), merged)
    out = _mm(merged, wout, "nn", F32, 1024, 1024, D, "out_proj")
    sq, dy, dout, d_post = _post_loss(out, x, tgt, sp["post_norm_g"])

    tok = emit({"out": _mm(merged, dout, "tn", BF16, 1024, 1024, S, "d_wout")})
    dmerged = _mm(dout, wout, "nt", F32, 1024, 1024, D, "d_merged", after=tok)
    dz, dp_d = [], None
    tok = advance(dmerged)
    for b in range(3):
        dz_b, dp_d = _merge_bwd(dmerged, z[b], p["D"], b, dp_d, after=tok if b == 0 else None)
        dz.append(dz_b)
    d_win = lambda g, dp_g, after=None: _mm(dp_g, h, "tn", BF16, PROJ_TN[g], 1024, S, f"d_win_{g}", after=after)
    tok = emit({f"br{b}": _mm(ys[b], dz[b], "tn", BF16, 1024, 1024, S, f"d_wbr{b}") for b in range(3)}, tok)
    d_w_d = d_win("D", dp_d, tok)
    tok = emit({"D": d_w_d}, advance(d_w_d))
    dy_mem = _mm(dz[2], wbr[2], "nt", F32, 1024, 1024, D, "d_branch2", after=tok)
    tok = advance(dy_mem)
    dp_c, dmkv = _mem_bwd(dy_mem, p["C"], o_mem, mkv)
    dmkv16 = dmkv.astype(BF16)
    tok = emit({"mk": _mm(memn, dmkv16, "tn", BF16, 1024, 1024, MEM, "d_wmk", after=tok), "C": d_win("C", dp_c)}, tok)
    dmemn = _mm(dmkv16, wmk, "nt", F32, MEM, 1024, D, "d_memn", after=tok)
    tok = advance(dmemn)
    d_memg = _memnorm_bwd(dmemn, mem)
    dy_rg = _mm(dz[0], wbr[0], "nt", F32, 1024, 1024, D, "d_branch0", after=tok)
    dp_a, d_cw, d_cb, d_wa, d_ba, d_wx, d_bx, d_lam = _rglru_bwd(dy_rg, p["A"], hseq, *rnn)
    tok = emit({"A": d_win("A", dp_a)}, tok)
    dy_swa = _mm(dz[1], wbr[1], "nt", F32, 1024, 1024, D, "d_branch1", after=tok)
    tok = advance(dy_swa)
    dp_b, dk, dv, d_bias, d_sink = _swa_bwd(dy_swa, p["B"], o_swa, bias_t, sinks, after=tok)
    dp_b = _swa_pack(dp_b, dk, dv)
    d_rel = _relbias_grad(d_bias.reshape(SWA_HEADS, QB * KB2), onehot_t).T
    dp = {"A": dp_a, "B": dp_b, "C": dp_c, "D": dp_d}
    tok = emit({"B": d_win("B", dp_b)}, tok)
    dh = None
    for g in GROUPS:
        dh = _mm(dp[g], w_grp[g], "nn", F32, 1024, 1024, 2304 if g == "B" else 2048, f"d_h_{g}", acc=dh,
                 after=tok if g in ("A", "B") else None)
        if g == "A":
            tok = advance(dh)
    grad_x, d_pre = _pre_bwd(dh, x, dy, sp["pre_norm_g"])

    d_small = {
        "pre_norm_g": d_pre, "post_norm_g": d_post, "mem_norm_g": d_memg, "conv_w": d_cw, "conv_b": d_cb,
        "w_rg_a": d_wa, "b_rg_a": d_ba, "w_rg_x": d_wx, "b_rg_x": d_bx, "lru_lambda": d_lam,
        "swa_sinks": d_sink[:, 0].reshape(1, SWA_HEADS), "rel_bias": d_rel,
    }
    return sq, grad_x, d_small


ANY = pl.BlockSpec(memory_space=pl.ANY)
SHARD_ROWS = D // N_CHIPS
GATHERED = {"A": (2048, D), "B": (2304, D), "C": (2048, D), "D": (6144, D), "mk": (D, D),
            "br0": (D_RNN, D), "br1": (D_RNN, D), "br2": (D_RNN, D), "out": (D, D)}
SHARD_SHAPES = {"win": (SHARD, D), "mk": (SHARD_ROWS, D), "br0": (D_RNN, SHARD_ROWS), "br1": (D_RNN, SHARD_ROWS),
                "br2": (D_RNN, SHARD_ROWS), "out": (SHARD_ROWS, D)}
SHARDS = tuple(SHARD_SHAPES)
HALF_AXIS = {"win": 1, "mk": 1, "br0": 0, "br1": 0, "br2": 0, "out": 1,
             "A": 1, "B": 1, "C": 1, "D": 1}


def _halved(shape, axis):
    return (shape[0] // 2, shape[1]) if axis == 0 else (shape[0], shape[1] // 2)


class Piece(NamedTuple):
    src: str
    dst: str
    rows: int
    sr0: int
    sc0: int
    dr0: int
    dc0: int
    ncols: int


def _pieces_of(jj):
    out = [Piece("win", g, n, r, 0, gr, 0, D) for r, n, g, gr in _shard_runs(jj)]
    out.append(Piece("mk", "mk", SHARD_ROWS, 0, 0, SHARD_ROWS * jj, 0, D))
    out += [Piece(f"br{b}", f"br{b}", D_RNN, 0, 0, 0, SHARD_ROWS * jj, SHARD_ROWS) for b in range(3)]
    out.append(Piece("out", "out", SHARD_ROWS, 0, 0, SHARD_ROWS * jj, 0, D))
    return out


def _half_rect(ref, p, side, which):
    r0, c0 = (p.sr0, p.sc0) if side == "src" else (p.dr0, p.dc0)
    if HALF_AXIS[p.src] == 1:
        return _rect(ref, r0, p.rows, c0 + which * (p.ncols // 2), p.ncols // 2)
    return _rect(ref, r0 + which * (p.rows // 2), p.rows // 2, c0, p.ncols)


def _rect_in_half(ref, p, side):
    r0, c0 = (p.sr0, p.sc0) if side == "src" else (p.dr0, p.dc0)
    if HALF_AXIS[p.src] == 1:
        return _rect(ref, r0, p.rows, 0, p.ncols // 2)
    return _rect(ref, 0, p.rows // 2, c0, p.ncols)


MAX_PIECES = max(len(_pieces_of(jj)) for jj in range(N_CHIPS))


def _rect(ref, r0, rows, c0, ncols):
    return ref.at[pl.ds(r0, rows), pl.ds(c0, ncols)]


def _position():
    x, y, c = lax.axis_index("x"), lax.axis_index("y"), lax.axis_index("c")
    return x, y, c, 2 * x + y


HBM = pl.BlockSpec(memory_space=pltpu.HBM)
SEM = pl.BlockSpec(memory_space=pltpu.SEMAPHORE)
EFFECT = pltpu.SideEffectType.DATAFLOW_SIDE_EFFECTING
N_SEM = MAX_PIECES * N_CHIPS
GATHER_STAGES = (("A",), ("B",), ("C",), ("mk",), ("br0", "br1", "br2"), ("D",), ("out",))


def _in_hbm(a):
    return pltpu.with_memory_space_constraint(a, pltpu.HBM)


def _stage_pieces(jj, stage):
    return [(i, p) for i, p in enumerate(_pieces_of(jj)) if p.dst in stage]


def _own_block_table(g):
    import numpy as np
    tbl = np.zeros((N_CHIPS, GATHERED[g][0] // HALF_TILE), np.int32)
    for jj in range(N_CHIPS):
        for r, n, grp, gr in _shard_runs(jj):
            if grp == g:
                for k in range(n // HALF_TILE):
                    tbl[jj, gr // HALF_TILE + k] = r // HALF_TILE + k
    return tbl


def _place_group(w_t, g, table, after):
    nb = GATHERED[g][0] // HALF_TILE

    def body(t_ref, x_ref, _, o_ref):
        o_ref[...] = x_ref[...].astype(BF16)

    return pl.pallas_call(
        body,
        name=f"place_{g}",
        grid_spec=pltpu.PrefetchScalarGridSpec(
            num_scalar_prefetch=1,
            grid=(nb,),
            in_specs=[pl.BlockSpec((HALF_TILE, D), lambda b, t: (t[b], 0)), ANY],
            out_specs=pl.BlockSpec((HALF_TILE, D), lambda b, t: (b, 0)),
        ),
        out_shape=jax.ShapeDtypeStruct(GATHERED[g], BF16),
        compiler_params=_params(("parallel",)),
    )(table, w_t, after)


def _place_shard(shard, name, after):
    rows, cols = shard.shape
    by_rows = HALF_AXIS[name] == 1

    def body(x_ref, _, o_ref):
        o_ref[...] = x_ref[...].astype(BF16)

    return pl.pallas_call(
        body,
        name=f"place_{name}",
        grid=(N_CHIPS,),
        in_specs=[pl.BlockSpec((rows, cols), lambda b: (0, 0)), ANY],
        out_specs=pl.BlockSpec((rows, cols), (lambda b: (b, 0)) if by_rows else (lambda b: (0, b))),
        out_shape=jax.ShapeDtypeStruct(GATHERED[name], BF16),
        compiler_params=_params(("parallel",)),
    )(shard, after)


def _gather_copy(arr, send_sems, recv_sems, c, jj, i, p, kk):
    rect = _half_rect(arr[p.dst], p, "dst", c)
    return pltpu.make_async_remote_copy(
        src_ref=rect, dst_ref=rect, send_sem=send_sems.at[i * N_CHIPS + kk],
        recv_sem=recv_sems.at[jj * MAX_PIECES + i], device_id=(kk // 2, kk % 2, c), device_id_type=MESH)


def _gather_start(arrays, after):
    stage = tuple(arrays)
    na = len(stage)

    def body(*refs):
        arr = dict(zip(stage, refs[:na]))
        send_sems, recv_sems = refs[na + 1], refs[na + 2]
        token = refs[-1]
        _, _, c, j = _position()
        for jj in range(N_CHIPS):
            @pl.when(j == jj)
            def _():
                for i, p in _stage_pieces(jj, stage):
                    for kk in range(N_CHIPS):
                        if kk != jj:
                            _gather_copy(arr, send_sems, recv_sems, c, jj, i, p, kk).start()
        token[...] = jnp.zeros_like(token)

    outs = pl.pallas_call(
        body,
        name=f"gather_start_{stage[0]}",
        in_specs=[HBM] * na + [ANY],
        out_specs=[SEM, SEM] + [HBM] * na + [pl.BlockSpec(memory_space=pltpu.VMEM)],
        out_shape=[pltpu.SemaphoreType.DMA((N_SEM,)), pltpu.SemaphoreType.DMA((N_SEM,))]
        + [pltpu.HBM(GATHERED[n], BF16) for n in stage] + [jax.ShapeDtypeStruct((8, LANE), F32)],
        input_output_aliases={k: 2 + k for k in range(na)},
        compiler_params=pltpu.CompilerParams(has_side_effects=EFFECT),
    )(*[_in_hbm(arrays[n]) for n in stage], after)
    return outs[0], outs[1], dict(zip(stage, outs[2:2 + na])), outs[-1]


def _gather_wait(send_sems, recv_sems, arrays, after):
    stage = tuple(arrays)
    na = len(stage)

    def body(*refs):
        arr = dict(zip(stage, refs[:na]))
        sems_s, sems_r = refs[na], refs[na + 1]
        _, _, c, j = _position()
        for jj in range(N_CHIPS):
            @pl.when(j != jj)
            def _():
                for i, p in _stage_pieces(jj, stage):
                    _gather_copy(arr, sems_s, sems_r, c, jj, i, p, jj).wait_recv()

            @pl.when(j == jj)
            def _():
                for i, p in _stage_pieces(jj, stage):
                    for kk in range(N_CHIPS):
                        if kk != jj:
                            _gather_copy(arr, sems_s, sems_r, c, jj, i, p, kk).wait_send()

    outs = pl.pallas_call(
        body,
        name=f"gather_wait_{stage[0]}",
        in_specs=[HBM] * na + [SEM, SEM, ANY],
        out_specs=[HBM] * na,
        out_shape=[pltpu.HBM(GATHERED[n], BF16) for n in stage],
        input_output_aliases={k: k for k in range(na)},
        compiler_params=pltpu.CompilerParams(has_side_effects=EFFECT),
    )(*[arrays[n] for n in stage], send_sems, recv_sems, after)
    return dict(zip(stage, outs))


def _gather_swap(arrays):
    stage = tuple(arrays)
    na = len(stage)

    def body(*refs):
        dst = dict(zip(stage, refs[na:2 * na]))
        send_sems, recv_sems = refs[2 * na:]
        x, y, c, j = _position()

        def fwd(jj, i, p, which):
            rect = _half_rect(dst[p.dst], p, "dst", which)
            return pltpu.make_async_remote_copy(
                src_ref=rect, dst_ref=rect, send_sem=send_sems.at[jj * MAX_PIECES + i],
                recv_sem=recv_sems.at[jj * MAX_PIECES + i], device_id=(x, y, 1 - c), device_id_type=MESH)

        for jj in range(N_CHIPS):
            @pl.when(j != jj)
            def _():
                for i, p in _stage_pieces(jj, stage):
                    fwd(jj, i, p, c).start()
        for jj in range(N_CHIPS):
            @pl.when(j != jj)
            def _():
                for i, p in _stage_pieces(jj, stage):
                    fwd(jj, i, p, 1 - c).wait_recv()
        for jj in range(N_CHIPS):
            @pl.when(j != jj)
            def _():
                for i, p in _stage_pieces(jj, stage):
                    fwd(jj, i, p, c).wait_send()

    outs = pl.pallas_call(
        body,
        name=f"gather_swap_{stage[0]}",
        in_specs=[ANY] * na,
        out_specs=[ANY] * na,
        out_shape=[jax.ShapeDtypeStruct(GATHERED[n], BF16) for n in stage],
        input_output_aliases={k: k for k in range(na)},
        scratch_shapes=[pltpu.SemaphoreType.DMA((N_SEM,)), pltpu.SemaphoreType.DMA((N_SEM,))],
        compiler_params=pltpu.CompilerParams(has_side_effects=True),
    )(*[arrays[n] for n in stage])
    return dict(zip(stage, outs))


def _pass_on_copy(arr, send_sems, recv_sems, x, y, c, jj, i, p, which):
    rect = _half_rect(arr[p.dst], p, "dst", which)
    return pltpu.make_async_remote_copy(
        src_ref=rect, dst_ref=rect, send_sem=send_sems.at[jj * MAX_PIECES + i],
        recv_sem=recv_sems.at[jj * MAX_PIECES + i], device_id=(x, y, 1 - c), device_id_type=MESH)


def _gather_pass_start(arrays, after):
    stage = tuple(arrays)
    na = len(stage)

    def body(*refs):
        arr = dict(zip(stage, refs[:na]))
        x, y, c, j = _position()
        for jj in range(N_CHIPS):
            @pl.when(j != jj)
            def _():
                for i, p in _stage_pieces(jj, stage):
                    _pass_on_copy(arr, refs[na + 1], refs[na + 2], x, y, c, jj, i, p, c).start()
        refs[-1][...] = jnp.zeros_like(refs[-1])

    outs = pl.pallas_call(
        body,
        name=f"gather_pass_start_{stage[0]}",
        in_specs=[HBM] * na + [ANY],
        out_specs=[SEM, SEM] + [HBM] * na + [pl.BlockSpec(memory_space=pltpu.VMEM)],
        out_shape=[pltpu.SemaphoreType.DMA((N_SEM,)), pltpu.SemaphoreType.DMA((N_SEM,))]
        + [pltpu.HBM(GATHERED[n], BF16) for n in stage] + [jax.ShapeDtypeStruct((8, LANE), F32)],
        input_output_aliases={k: 2 + k for k in range(na)},
        compiler_params=pltpu.CompilerParams(has_side_effects=EFFECT),
    )(*[arrays[n] for n in stage], after)
    return outs[0], outs[1], dict(zip(stage, outs[2:2 + na])), outs[-1]


def _gather_pass_wait(send_sems, recv_sems, arrays, after):
    stage = tuple(arrays)
    na = len(stage)

    def body(*refs):
        arr = dict(zip(stage, refs[:na]))
        x, y, c, j = _position()
        for jj in range(N_CHIPS):
            @pl.when(j != jj)
            def _():
                for i, p in _stage_pieces(jj, stage):
                    _pass_on_copy(arr, refs[na], refs[na + 1], x, y, c, jj, i, p, 1 - c).wait_recv()
                    _pass_on_copy(arr, refs[na], refs[na + 1], x, y, c, jj, i, p, c).wait_send()

    outs = pl.pallas_call(
        body,
        name=f"gather_pass_wait_{stage[0]}",
        in_specs=[HBM] * na + [SEM, SEM, ANY],
        out_specs=[HBM] * na,
        out_shape=[pltpu.HBM(GATHERED[n], BF16) for n in stage],
        input_output_aliases={k: k for k in range(na)},
        compiler_params=pltpu.CompilerParams(has_side_effects=EFFECT),
    )(*[arrays[n] for n in stage], send_sems, recv_sems, after)
    return dict(zip(stage, outs))


def _own_half(ref, shape, axis, which):
    if axis == 1:
        return ref.at[:, pl.ds(which * (shape[1] // 2), shape[1] // 2)]
    return ref.at[pl.ds(which * (shape[0] // 2), shape[0] // 2), :]


def _swap_copies(names, src, dst, send_sems, recv_sems):
    x, y, c, _ = _position()
    return [pltpu.make_async_remote_copy(
        src_ref=_own_half(src[n], GATHERED[n], HALF_AXIS[n], 1 - c), dst_ref=dst[n],
        send_sem=send_sems.at[k], recv_sem=recv_sems.at[k],
        device_id=(x, y, 1 - c), device_id_type=MESH) for k, n in enumerate(names)]


def _swap_start(grads, after):
    names = tuple(grads)
    n = len(names)

    def body(*refs):
        src, dst = dict(zip(names, refs[:n])), dict(zip(names, refs[n:2 * n]))
        for cp in _swap_copies(names, src, dst, refs[2 * n + 1], refs[2 * n + 2]):
            cp.start()
        refs[-1][...] = jnp.zeros_like(refs[-1])

    half_shape = lambda nm: _halved(GATHERED[nm], HALF_AXIS[nm])
    args = [_in_hbm(grads[nm]) for nm in names] + [_in_hbm(lax.empty(half_shape(nm), BF16)) for nm in names]
    if after is None:
        after = jnp.zeros((8, LANE), F32)
    outs = pl.pallas_call(
        body,
        name=f"swap_start_{names[0]}",
        in_specs=[HBM] * (2 * n) + [ANY],
        out_specs=[SEM, SEM] + [HBM] * (2 * n) + [pl.BlockSpec(memory_space=pltpu.VMEM)],
        out_shape=[pltpu.SemaphoreType.DMA((n,)), pltpu.SemaphoreType.DMA((n,))]
        + [pltpu.HBM(GATHERED[nm], BF16) for nm in names] + [pltpu.HBM(half_shape(nm), BF16) for nm in names]
        + [jax.ShapeDtypeStruct((8, LANE), F32)],
        input_output_aliases={k: 2 + k for k in range(2 * n)},
        compiler_params=pltpu.CompilerParams(has_side_effects=EFFECT),
    )(*args, after)
    return outs[0], outs[1], dict(zip(names, outs[2:2 + n])), dict(zip(names, outs[2 + n:2 + 2 * n])), outs[-1]


def _swap_wait(send_sems, recv_sems, grads, landing, after):
    names = tuple(grads)
    n = len(names)

    def body(*refs):
        src, dst = dict(zip(names, refs[:n])), dict(zip(names, refs[n:2 * n]))
        copies = _swap_copies(names, src, dst, refs[2 * n], refs[2 * n + 1])
        for cp in copies:
            cp.wait_recv()
        for cp in copies:
            cp.wait_send()

    half_shape = lambda nm: _halved(GATHERED[nm], HALF_AXIS[nm])
    outs = pl.pallas_call(
        body,
        name=f"swap_wait_{names[0]}",
        in_specs=[HBM] * (2 * n) + [SEM, SEM, ANY],
        out_specs=[HBM] * (2 * n),
        out_shape=[pltpu.HBM(GATHERED[nm], BF16) for nm in names] + [pltpu.HBM(half_shape(nm), BF16) for nm in names],
        input_output_aliases={k: k for k in range(2 * n)},
        compiler_params=pltpu.CompilerParams(has_side_effects=EFFECT),
    )(*[grads[nm] for nm in names], *[landing[nm] for nm in names], send_sems, recv_sems, after)
    return dict(zip(names, outs[:n])), dict(zip(names, outs[n:]))


ADD_ROWS = 256


def _add_half(full, recv, c_arr, name):
    rows, cols = recv.shape
    if HALF_AXIS[name] == 1:
        index = lambda i, c_ref: (i, c_ref[0])
    else:
        nb = rows // ADD_ROWS
        index = lambda i, c_ref: (nb * c_ref[0] + i, 0)

    def body(c_ref, a_ref, b_ref, o_ref):
        o_ref[...] = (a_ref[...].astype(F32) + b_ref[...].astype(F32)).astype(BF16)

    return pl.pallas_call(
        body,
        name=f"add_half_{name}",
        grid_spec=pltpu.PrefetchScalarGridSpec(
            num_scalar_prefetch=1,
            grid=(rows // ADD_ROWS,),
            in_specs=[pl.BlockSpec((ADD_ROWS, cols), index), pl.BlockSpec((ADD_ROWS, cols), lambda i, c_ref: (i, 0))],
            out_specs=pl.BlockSpec((ADD_ROWS, cols), lambda i, c_ref: (i, 0)),
        ),
        out_shape=jax.ShapeDtypeStruct((rows, cols), BF16),
        compiler_params=_params(("parallel",)),
    )(c_arr, full, recv)


SLOT_SHAPES = {n: _halved(SHARD_SHAPES[n], HALF_AXIS[n]) for n in SHARDS}


def _slot_shape(n):
    return (N_CHIPS,) + SLOT_SHAPES[n]


def _stage_shards(stage):
    pieces = [p for jj in range(N_CHIPS) for p in _pieces_of(jj)]
    return tuple(s for s in SHARDS if any(p.src == s and p.dst in stage for p in pieces))


def _scatter_copy(src, dst, send_sems, recv_sems, c, jj, kk, i, p):
    return pltpu.make_async_remote_copy(
        src_ref=_rect_in_half(src[p.dst], p, "dst"), dst_ref=_rect_in_half(dst[p.src].at[jj], p, "src"),
        send_sem=send_sems.at[kk * MAX_PIECES + i], recv_sem=recv_sems.at[jj * MAX_PIECES + i],
        device_id=(kk // 2, kk % 2, c), device_id_type=MESH)


def _scatter_start(halves, slots):
    stage, touched = tuple(halves), tuple(slots)
    nh, nt = len(stage), len(touched)

    def body(*refs):
        src = dict(zip(stage, refs[:nh]))
        dst = dict(zip(touched, refs[nh:nh + nt]))
        send_sems, recv_sems = refs[nh + nt], refs[nh + nt + 1]
        token = refs[-1]
        _, _, c, j = _position()
        for jj in range(N_CHIPS):
            @pl.when(j == jj)
            def _():
                for kk in range(N_CHIPS):
                    if kk != jj:
                        for i, p in _stage_pieces(kk, stage):
                            _scatter_copy(src, dst, send_sems, recv_sems, c, jj, kk, i, p).start()
        token[...] = jnp.zeros_like(token)

    outs = pl.pallas_call(
        body,
        name=f"scatter_start_{stage[0]}",
        in_specs=[HBM] * (nh + nt),
        out_specs=[SEM, SEM] + [HBM] * (nh + nt) + [pl.BlockSpec(memory_space=pltpu.VMEM)],
        out_shape=[pltpu.SemaphoreType.DMA((N_SEM,)), pltpu.SemaphoreType.DMA((N_SEM,))]
        + [pltpu.HBM(halves[n].shape, BF16) for n in stage] + [pltpu.HBM(_slot_shape(s), BF16) for s in touched]
        + [jax.ShapeDtypeStruct((8, LANE), F32)],
        input_output_aliases={k: 2 + k for k in range(nh + nt)},
        compiler_params=pltpu.CompilerParams(has_side_effects=EFFECT),
    )(*[_in_hbm(halves[n]) for n in stage], *[_in_hbm(slots[s]) for s in touched])
    return outs[0], outs[1], dict(zip(stage, outs[2:2 + nh])), dict(zip(touched, outs[2 + nh:2 + nh + nt])), outs[-1]


def _scatter_wait(send_sems, recv_sems, halves, slots, after):
    stage, touched = tuple(halves), tuple(slots)
    nh, nt = len(stage), len(touched)

    def body(*refs):
        src = dict(zip(stage, refs[:nh]))
        dst = dict(zip(touched, refs[nh:nh + nt]))
        sems_s, sems_r = refs[nh + nt], refs[nh + nt + 1]
        _, _, c, j = _position()
        for jj in range(N_CHIPS):
            @pl.when(j == jj)
            def _():
                for ss in range(N_CHIPS):
                    if ss != jj:
                        for i, p in _stage_pieces(jj, stage):
                            _scatter_copy(src, dst, sems_s, sems_r, c, ss, jj, i, p).wait_recv()
                for kk in range(N_CHIPS):
                    if kk != jj:
                        for i, p in _stage_pieces(kk, stage):
                            _scatter_copy(src, dst, sems_s, sems_r, c, jj, kk, i, p).wait_send()

    outs = pl.pallas_call(
        body,
        name=f"scatter_wait_{stage[0]}",
        in_specs=[HBM] * (nh + nt) + [SEM, SEM, ANY],
        out_specs=[HBM] * (nh + nt),
        out_shape=[pltpu.HBM(halves[n].shape, BF16) for n in stage] + [pltpu.HBM(_slot_shape(s), BF16) for s in touched],
        input_output_aliases={k: k for k in range(nh + nt)},
        compiler_params=pltpu.CompilerParams(has_side_effects=EFFECT),
    )(*[halves[n] for n in stage], *[slots[s] for s in touched], send_sems, recv_sems, after)
    return dict(zip(stage, outs[:nh])), dict(zip(touched, outs[nh:]))


SUM_ROWS = {"win": 448, "mk": 256, "br0": 256, "br1": 256, "br2": 256, "out": 256}


def _sum_in_chip_order(chip, own, s_ref):
    acc = None
    for k in range(N_CHIPS):
        term = jnp.where(chip == k, own, s_ref[k].astype(F32))
        acc = term if acc is None else acc + term
    return acc


def _sum_slots(slots, own_half, pos_arr, name):
    _, rows, cols = slots.shape
    tr = SUM_ROWS[name]
    nb = rows // tr
    if HALF_AXIS[name] == 1:
        own_index = lambda i, pos: (nb * pos[1] + i, 0)
        out_index = lambda i, pos: (i, pos[0])
    else:
        own_index = lambda i, pos: (i, pos[1])
        out_index = lambda i, pos: (nb * pos[0] + i, 0)

    def body(pos, s_ref, own_ref, o_ref):
        o_ref[...] = _sum_in_chip_order(pos[1], own_ref[...].astype(F32), s_ref)

    return pl.pallas_call(
        body,
        name=f"sum_slots_{name}",
        grid_spec=pltpu.PrefetchScalarGridSpec(
            num_scalar_prefetch=1,
            grid=(nb,),
            in_specs=[pl.BlockSpec((N_CHIPS, tr, cols), lambda i, pos: (0, i, 0)),
                      pl.BlockSpec((tr, cols), own_index)],
            out_specs=pl.BlockSpec((tr, cols), out_index),
        ),
        out_shape=jax.ShapeDtypeStruct(SHARD_SHAPES[name], F32),
        compiler_params=_params(("parallel",)),
    )(pos_arr, slots, own_half)


def _own_partial_tables():
    import numpy as np
    nb = SHARD // HALF_TILE
    grp, blk = np.zeros((N_CHIPS, nb), np.int32), np.zeros((N_CHIPS, nb), np.int32)
    for jj in range(N_CHIPS):
        for r, n, g, gr in _shard_runs(jj):
            for k in range(n // HALF_TILE):
                grp[jj, r // HALF_TILE + k] = GROUPS.index(g)
                blk[jj, r // HALF_TILE + k] = gr // HALF_TILE + k
    return grp, blk


def _sum_slots_win(slots, own_halves, pos_arr, grp_tbl, blk_tbl):
    nb = SHARD // HALF_TILE
    cols = D // 2

    def own_spec(gi):
        return pl.BlockSpec((HALF_TILE, cols), lambda b, pos, grp, blk: (jnp.where(grp[b] == gi, blk[b], 0), 0))

    def body(pos, grp, blk, s_ref, a_ref, b_ref, c_ref, d_ref, o_ref):
        g = grp[pl.program_id(0)]
        own = a_ref[...]
        for gi, ref in ((1, b_ref), (2, c_ref), (3, d_ref)):
            own = jnp.where(g == gi, ref[...], own)
        o_ref[...] = _sum_in_chip_order(pos[1], own.astype(F32), s_ref)

    return pl.pallas_call(
        body,
        name="sum_slots_win",
        grid_spec=pltpu.PrefetchScalarGridSpec(
            num_scalar_prefetch=3,
            grid=(nb,),
            in_specs=[pl.BlockSpec((N_CHIPS, HALF_TILE, cols), lambda b, pos, grp, blk: (0, b, 0))]
            + [own_spec(gi) for gi in range(len(GROUPS))],
            out_specs=pl.BlockSpec((HALF_TILE, cols), lambda b, pos, grp, blk: (b, pos[0])),
        ),
        out_shape=jax.ShapeDtypeStruct(SHARD_SHAPES["win"], F32),
        compiler_params=_params(("parallel",)),
    )(pos_arr, grp_tbl, blk_tbl, slots, *[own_halves[g] for g in GROUPS])


def _share_sums(sums):
    def body(*refs):
        bufs = refs[len(SHARDS):2 * len(SHARDS)]
        send_sems, recv_sems = refs[2 * len(SHARDS):]
        x, y, c, _ = _position()
        copies = []
        for k, (n, b) in enumerate(zip(SHARDS, bufs)):
            mine = _own_half(b, SHARD_SHAPES[n], HALF_AXIS[n], c)
            copies.append(pltpu.make_async_remote_copy(
                src_ref=mine, dst_ref=mine, send_sem=send_sems.at[k], recv_sem=recv_sems.at[k],
                device_id=(x, y, 1 - c), device_id_type=MESH))
        for cp in copies:
            cp.start()
        for cp in copies:
            cp.wait_recv()
        for cp in copies:
            cp.wait_send()

    outs = pl.pallas_call(
        body,
        name="share_sums",
        in_specs=[ANY] * len(SHARDS),
        out_specs=[ANY] * len(SHARDS),
        out_shape=[jax.ShapeDtypeStruct(sums[n].shape, F32) for n in SHARDS],
        input_output_aliases={k: k for k in range(len(SHARDS))},
        scratch_shapes=[pltpu.SemaphoreType.DMA((len(SHARDS),)), pltpu.SemaphoreType.DMA((len(SHARDS),))],
        compiler_params=pltpu.CompilerParams(has_side_effects=True),
    )(*[sums[n] for n in SHARDS])
    return dict(zip(SHARDS, outs))


N_DEV = 8


def _all_reduce_small(pack, name):
    rows = pack.shape[0]
    half = rows // 2

    def body(p_ref, o_ref, sib, land, sems):
        x, y, c, j = _position()
        sibling = (x, y, 1 - c)
        swap = pltpu.make_async_remote_copy(src_ref=p_ref, dst_ref=sib, send_sem=sems.at[0], recv_sem=sems.at[1],
                                            device_id=sibling, device_id_type=MESH)
        swap.start()
        swap.wait_recv()
        land[j] = p_ref[...] + sib[...]

        def mine(k, which):
            return land.at[k, pl.ds(which * half, half)]

        def ici(kk):
            return pltpu.make_async_remote_copy(
                src_ref=mine(j, c), dst_ref=mine(j, c), send_sem=sems.at[2 + kk], recv_sem=sems.at[6 + j],
                device_id=(kk // 2, kk % 2, c), device_id_type=MESH)

        def arrival(kk):
            return pltpu.make_async_remote_copy(
                src_ref=mine(kk, c), dst_ref=mine(kk, c), send_sem=sems.at[2 + kk], recv_sem=sems.at[6 + kk],
                device_id=(kk // 2, kk % 2, c), device_id_type=MESH)

        def passed_on(kk, which):
            return pltpu.make_async_remote_copy(
                src_ref=mine(kk, which), dst_ref=mine(kk, which), send_sem=sems.at[10 + kk],
                recv_sem=sems.at[14 + kk], device_id=sibling, device_id_type=MESH)

        for kk in range(N_CHIPS):
            @pl.when(j != kk)
            def _():
                ici(kk).start()
        for kk in range(N_CHIPS):
            @pl.when(j != kk)
            def _():
                arrival(kk).wait_recv()
                passed_on(kk, c).start()
        for kk in range(N_CHIPS):
            @pl.when(j != kk)
            def _():
                passed_on(kk, 1 - c).wait_recv()
        acc = land[0]
        for kk in range(1, N_CHIPS):
            acc = acc + land[kk]
        o_ref[...] = acc
        swap.wait_send()
        for kk in range(N_CHIPS):
            @pl.when(j != kk)
            def _():
                ici(kk).wait_send()
                passed_on(kk, c).wait_send()

    vmem = pl.BlockSpec(memory_space=pltpu.VMEM)
    return pl.pallas_call(
        body,
        name=name,
        in_specs=[vmem],
        out_specs=vmem,
        out_shape=jax.ShapeDtypeStruct((rows, LANE), F32),
        scratch_shapes=[pltpu.VMEM((rows, LANE), F32), pltpu.VMEM((N_CHIPS, rows, LANE), F32),
                        pltpu.SemaphoreType.DMA((18,))],
        compiler_params=pltpu.CompilerParams(has_side_effects=True, vmem_limit_bytes=VMEM_LIMIT),
    )(pack)


def _adamw(w, g, m, v, name, tr):
    rows, cols = w.shape
    tr = min(tr, rows)

    def body(w_ref, g_ref, m_ref, v_ref, go_ref, d_ref, nm_ref, nv_ref):
        gv = g_ref[...]
        go_ref[...] = gv
        nm = ADAM_B1 * m_ref[...] + (1.0 - ADAM_B1) * gv
        nv = ADAM_B2 * v_ref[...] + (1.0 - ADAM_B2) * (gv * gv)
        nm_ref[...] = nm
        nv_ref[...] = nv
        m_hat = nm / (1.0 - ADAM_B1 ** ADAM_STEP)
        v_hat = nv / (1.0 - ADAM_B2 ** ADAM_STEP)
        d_ref[...] = -ADAM_LR * (m_hat / (jnp.sqrt(v_hat) + ADAM_EPS) + ADAM_WD * w_ref[...])

    blk = pl.BlockSpec((tr, cols), lambda i: (i, 0))
    shape = jax.ShapeDtypeStruct((rows, cols), F32)
    return pl.pallas_call(
        body,
        name=f"adamw_{name}",
        grid=(rows // tr,),
        in_specs=[blk] * 4,
        out_specs=[blk] * 4,
        out_shape=[shape] * 4,
        compiler_params=_params(("parallel",)),
    )(w, g, m, v)


SMALL = (("pre_norm_g", (1, D)), ("post_norm_g", (1, D)), ("mem_norm_g", (1, D)), ("conv_w", (CONV_W, D_RNN)),
         ("conv_b", (1, D_RNN)), ("w_rg_a", (RNN_BLOCKS, LANE, LANE)), ("b_rg_a", (1, D_RNN)),
         ("w_rg_x", (RNN_BLOCKS, LANE, LANE)), ("b_rg_x", (1, D_RNN)), ("lru_lambda", (1, D_RNN)),
         ("swa_sinks", (1, SWA_HEADS)), ("rel_bias", (REL_BUCKETS, SWA_HEADS)))
PACK_ROWS = 2176


def _slot_len(shape):
    return -(-math.prod(shape) // LANE) * LANE


def _pack(values):
    parts = []
    for name, shape in SMALL:
        flat = values[name].reshape(-1).astype(F32)
        parts.append(jnp.pad(flat, (0, _slot_len(shape) - flat.shape[0])))
    flat = jnp.concatenate(parts)
    return jnp.pad(flat, (0, PACK_ROWS * LANE - flat.shape[0])).reshape(PACK_ROWS, LANE)


def _unpack(pack, shapes=None):
    flat = pack.reshape(-1)
    out, off = {}, 0
    for name, shape in SMALL:
        shp = shape if shapes is None or name not in shapes else shapes[name]
        out[name] = flat[off:off + math.prod(shp)].reshape(shp)
        off += _slot_len(shape)
    return out


TWIN_WEIGHTS = ("pre_norm_g", "post_norm_g", "mem_norm_g", "w_in", "conv_w", "conv_b", "w_rg_a", "b_rg_a", "w_rg_x",
                "b_rg_x", "lru_lambda", "swa_sinks", "rel_bias", "w_mem_kv", "w_br_rg", "w_br_swa", "w_br_mem", "w_out")
BIG = {"w_in": "win", "w_mem_kv": "mk", "w_br_rg": "br0", "w_br_swa": "br1", "w_br_mem": "br2", "w_out": "out"}


def kernel(x, mem, pre_norm_g, post_norm_g, mem_norm_g, w_in, conv_w, conv_b, w_rg_a, b_rg_a, w_rg_x, b_rg_x, lru_lambda, swa_sinks, rel_bias, w_mem_kv, w_br_rg, w_br_swa, w_br_mem, w_out, loss_target, m_pre_norm_g, m_post_norm_g, m_mem_norm_g, m_w_in, m_conv_w, m_conv_b, m_w_rg_a, m_b_rg_a, m_w_rg_x, m_b_rg_x, m_lru_lambda, m_swa_sinks, m_rel_bias, m_w_mem_kv, m_w_br_rg, m_w_br_swa, m_w_br_mem, m_w_out, v_pre_norm_g, v_post_norm_g, v_mem_norm_g, v_w_in, v_conv_w, v_conv_b, v_w_rg_a, v_b_rg_a, v_w_rg_x, v_b_rg_x, v_lru_lambda, v_swa_sinks, v_rel_bias, v_w_mem_kv, v_w_br_rg, v_w_br_swa, v_w_br_mem, v_w_out):
    args = dict(locals())
    out_shapes = {n: args[n].shape for n in TWIN_WEIGHTS}
    w = {n: (args[n] if n == "rel_bias" else args[n][0]) for n in TWIN_WEIGHTS}
    m = {n: (args["m_" + n] if n == "rel_bias" else args["m_" + n][0]) for n in TWIN_WEIGHTS}
    v = {n: (args["v_" + n] if n == "rel_bias" else args["v_" + n][0]) for n in TWIN_WEIGHTS}
    for d in (w, m, v):
        for n, shape in SMALL:
            if n != "conv_w":
                d[n] = d[n].reshape(shape)

    xi, yi, ci = lax.axis_index("x"), lax.axis_index("y"), lax.axis_index("c")
    chip = 2 * xi + yi
    c_arr = ci.astype(jnp.int32).reshape(1)
    zero = jnp.zeros((), jnp.int32)
    cw0 = (chip * (D_RNN // N_CHIPS)).astype(jnp.int32)

    placed = lax.dynamic_update_slice(jnp.zeros((CONV_W, D_RNN), F32), w["conv_w"], (zero, cw0))
    placed = jnp.where(ci == 0, placed, 0.0).reshape(CONV_W * D_RNN // LANE, LANE)
    conv_w_full = _all_reduce_small(placed, "gather_conv_w").reshape(CONV_W, D_RNN)

    for d in (w, m, v):
        d["w_in"] = d["w_in"].T
    chip_row = lambda tbl: lax.dynamic_slice(jnp.asarray(tbl), (chip.astype(jnp.int32), zero), (1, tbl.shape[1]))[0]
    big_of = {s: n for n, s in BIG.items()}
    ag, token = {}, conv_w_full
    for stage in GATHER_STAGES:
        behind = c_arr if stage == GATHER_STAGES[0] else token
        placed = {n: (_place_group(w["w_in"], n, chip_row(_own_block_table(n)), behind) if n in GROUPS
                      else _place_shard(w[big_of[n]], n, behind)) for n in stage}
        send, recv, in_flight, token = _gather_start(placed, token)
        ag[stage] = (send, recv, in_flight)

    all_started = token

    passing = {}

    def prefetch(names, after):
        send, recv, in_flight = ag[names]
        *passing[names], token = _gather_pass_start(_gather_wait(send, recv, in_flight, after), after)
        return token

    def fetch(names, after):
        if names in passing:
            ready = _gather_pass_wait(*passing.pop(names), after)
        else:
            send, recv, in_flight = ag[names]
            after = all_started if names == GATHER_STAGES[0] else after
            ready = _gather_swap(_gather_wait(send, recv, in_flight, after))
        return tuple(ready[n] for n in names)

    rs = {"slots": {}, "halves": {}, "pending": [], "swap": None}

    def emit(grads, after=None):
        assert rs["swap"] is None
        *rs["swap"], token = _swap_start(grads, after)
        return token

    def advance(after):
        grads, received = _swap_wait(*rs["swap"], after)
        rs["swap"] = None
        halves = {n: _add_half(grads[n], received[n], c_arr, n) for n in grads}
        landing = {s: rs["slots"][s] if s in rs["slots"] else lax.empty(_slot_shape(s), BF16)
                   for s in _stage_shards(tuple(grads))}
        send, recv, halves, landing, token = _scatter_start(halves, landing)
        rs["slots"].update(landing)
        rs["pending"].append((send, recv, halves, tuple(landing)))
        return token

    sp = {n: w[n] for n, _ in SMALL}
    sp["conv_w"] = conv_w_full
    sq, grad_x, d_small = _local_step(x[0], mem[0], loss_target[0], sp, fetch, prefetch, emit, advance)
    loss = lax.psum(sq[0, 0] * (0.5 / D), ("x", "y", "c"))

    small_total = _all_reduce_small(_pack(d_small), "all_reduce_small")

    for send, recv, halves, touched in rs["pending"]:
        halves, landed = _scatter_wait(send, recv, halves, {s: rs["slots"][s] for s in touched}, small_total)
        rs["slots"].update(landed)
        rs["halves"].update(halves)
    pos_arr = jnp.stack([ci, chip]).astype(jnp.int32)
    grp_tbl, blk_tbl = (chip_row(t) for t in _own_partial_tables())
    sums = {s: _sum_slots(rs["slots"][s], rs["halves"][s], pos_arr, s) for s in SHARDS if s != "win"}
    sums["win"] = _sum_slots_win(rs["slots"]["win"], rs["halves"], pos_arr, grp_tbl, blk_tbl)
    sums = _share_sums(sums)
    g_big = {n: sums[s] for n, s in BIG.items()}

    g_small = _unpack(small_total)
    g_small["conv_w"] = lax.dynamic_slice(g_small["conv_w"], (zero, cw0), (CONV_W, D_RNN // N_CHIPS))

    grad, delta, new_m, new_v = {}, {}, {}, {}
    for n, s in BIG.items():
        grad[n], delta[n], new_m[n], new_v[n] = _adamw(w[n], g_big[n], m[n], v[n], s, 224 if n == "w_in" else 128)
    for group in (grad, delta, new_m, new_v):
        group["w_in"] = group["w_in"].T
    _, d_, m_, v_ = _adamw(_pack(w), _pack(g_small), _pack(m), _pack(v), "small", PACK_ROWS)
    shard_shapes = {"conv_w": (CONV_W, D_RNN // N_CHIPS)}
    d_, m_, v_ = (_unpack(a, shard_shapes) for a in (d_, m_, v_))
    for n, _ in SMALL:
        grad[n], delta[n], new_m[n], new_v[n] = g_small[n], d_[n], m_[n], v_[n]

    outs = [loss, grad_x.reshape(1, S, D)]
    for group in (grad, delta, new_m, new_v):
        outs += [group[n].reshape(out_shapes[n]) for n in TWIN_WEIGHTS]
    return tuple(outs)
```

```python
import functools
import math
from typing import NamedTuple

import jax
import jax.numpy as jnp
from jax import lax
from jax.experimental import pallas as pl
from jax.experimental.pallas import tpu as pltpu

F32 = jnp.float32
BF16 = jnp.bfloat16
MESH = pl.DeviceIdType.MESH

S = 2048
D = 2048
MEM = 256
D_RNN = 1024
RNN_BLOCKS = 8
CONV_W = 4
LRU_C = 8.0
SWA_HEADS = 16
SWA_HD = 64
WINDOW = 128
MEM_HEADS = 4
MEM_HD = 256
REL_BUCKETS = 32
REL_MAX_DIST = 128
EPS = 1e-6
NEG_INF = -1e30
LANE = 128
SHARD = 3136
HALF_TILE = 64
N_CHIPS = 4
VMEM_LIMIT = 56 * 1024 * 1024

ADAM_LR = 0.001
ADAM_B1 = 0.9
ADAM_B2 = 0.999
ADAM_EPS = 1e-08
ADAM_WD = 0.01
ADAM_STEP = 10

GROUP_TILES = {"A": 16, "B": 18, "C": 16, "D": 48}
GROUPS = ("A", "B", "C", "D")


def _params(sem=None):
    return pltpu.CompilerParams(dimension_semantics=sem, vmem_limit_bytes=VMEM_LIMIT)


def _sigmoid(v):
    return jax.nn.sigmoid(v)


def _tile_home(t):
    if t < 16:
        return "A", t
    if t < 24:
        return "B", t - 16
    if t < 26:
        return "B", t - 24 + 16
    if t < 34:
        return "B", t - 26 + 8
    if t < 50:
        return "C", t - 34
    return "D", t - 50


def _shard_runs(j):
    runs = []
    per_shard = SHARD // HALF_TILE
    for q in range(per_shard * j, per_shard * (j + 1)):
        g, gt = _tile_home(q // 2)
        row = gt * LANE + (q % 2) * HALF_TILE
        if runs and runs[-1][2] == g and runs[-1][3] + runs[-1][1] == row:
            runs[-1][1] += HALF_TILE
        else:
            runs.append([(q - per_shard * j) * HALF_TILE, HALF_TILE, g, row])
    return [tuple(r) for r in runs]


_DIMS = {
    "nn": (((1,), (0,)), ((), ())),
    "nt": (((1,), (1,)), ((), ())),
    "tn": (((0,), (0,)), ((), ())),
}


def _mm(a, b, mode, out_dtype, tm, tn, tk, name, acc=None, after=None):
    if mode == "nn":
        (m, k), n = a.shape, b.shape[1]
    elif mode == "nt":
        (m, k), n = a.shape, b.shape[0]
    else:
        (k, m), n = a.shape, b.shape[1]
    tm, tn, tk = min(tm, m), min(tn, n), min(tk, k)
    assert m % tm == 0 and n % tn == 0 and k % tk == 0, (name, m, n, k)
    nk = k // tk
    has_acc = acc is not None

    def body(*refs):
        a_ref, b_ref = refs[0], refs[1]
        o_ref = refs[3] if has_acc else refs[2]
        p = lax.dot_general(a_ref[...], b_ref[...], _DIMS[mode], preferred_element_type=F32)

        def finish(v):
            if has_acc:
                v = v + refs[2][...]
            o_ref[...] = v.astype(out_dtype)

        if nk == 1:
            finish(p)
        else:
            s_ref = refs[-1]
            kk = pl.program_id(2)

            @pl.when(kk == 0)
            def _():
                s_ref[...] = p

            @pl.when(kk > 0)
            def _():
                s_ref[...] += p

            @pl.when(kk == nk - 1)
            def _():
                finish(s_ref[...])

    if mode == "nn":
        a_spec = pl.BlockSpec((tm, tk), lambda i, j, kk: (i, kk))
        b_spec = pl.BlockSpec((tk, tn), lambda i, j, kk: (kk, j))
    elif mode == "nt":
        a_spec = pl.BlockSpec((tm, tk), lambda i, j, kk: (i, kk))
        b_spec = pl.BlockSpec((tn, tk), lambda i, j, kk: (j, kk))
    else:
        a_spec = pl.BlockSpec((tk, tm), lambda i, j, kk: (kk, i))
        b_spec = pl.BlockSpec((tk, tn), lambda i, j, kk: (kk, j))
    o_spec = pl.BlockSpec((tm, tn), lambda i, j, kk: (i, j))
    in_specs = [a_spec, b_spec] + ([o_spec] if has_acc else [])
    args = (a, b) + ((acc,) if has_acc else ())
    if after is not None:
        in_specs.append(pl.BlockSpec(memory_space=pl.ANY))
        args += (after,)
    n_in = len(args)
    kernel_body = body

    def body(*refs):
        kernel_body(*(refs[:n_in - (after is not None)] + refs[n_in:]))

    return pl.pallas_call(
        body,
        name=name,
        grid=(m // tm, n // tn, nk),
        in_specs=in_specs,
        out_specs=o_spec,
        out_shape=jax.ShapeDtypeStruct((m, n), out_dtype),
        scratch_shapes=[pltpu.VMEM((tm, tn), F32)] if nk > 1 else [],
        compiler_params=_params(("parallel", "parallel", "arbitrary")),
    )(*args)


def _rms_fwd(x, g, name, ts=256):
    r, d = x.shape

    def body(x_ref, g_ref, o_ref):
        xv = x_ref[...]
        inv = lax.rsqrt(jnp.mean(xv * xv, axis=-1, keepdims=True) + EPS)
        o_ref[...] = (xv * inv * g_ref[...]).astype(BF16)

    return pl.pallas_call(
        body,
        name=name,
        grid=(r // ts,),
        in_specs=[pl.BlockSpec((ts, d), lambda i: (i, 0)), pl.BlockSpec((1, d), lambda i: (0, 0))],
        out_specs=pl.BlockSpec((ts, d), lambda i: (i, 0)),
        out_shape=jax.ShapeDtypeStruct((r, d), BF16),
        compiler_params=_params(("parallel",)),
    )(x, g)


def _post_loss(out, x, tgt, g_post, ts=256):
    n = S // ts

    def body(o_ref, x_ref, t_ref, g_ref, sq_ref, dy_ref, do_ref, dg_ref):
        i = pl.program_id(0)

        @pl.when(i == 0)
        def _():
            sq_ref[...] = jnp.zeros_like(sq_ref)
            dg_ref[...] = jnp.zeros_like(dg_ref)

        ov = o_ref[...]
        g = g_ref[...]
        inv = lax.rsqrt(jnp.mean(ov * ov, axis=-1, keepdims=True) + EPS)
        on = ov * inv
        err = x_ref[...] + on * g - t_ref[...]
        sq_ref[...] += jnp.sum(err * err)
        dy = err * (1.0 / D)
        dy_ref[...] = dy
        dg_ref[...] += jnp.sum(dy * on, axis=0, keepdims=True)
        don = dy * g
        do_ref[...] = (inv * (don - on * jnp.mean(don * on, axis=-1, keepdims=True))).astype(BF16)

    row = pl.BlockSpec((ts, D), lambda i: (i, 0))
    vec = pl.BlockSpec((1, D), lambda i: (0, 0))
    return pl.pallas_call(
        body,
        name="post_loss",
        grid=(n,),
        in_specs=[row, row, row, vec],
        out_specs=[pl.BlockSpec((8, LANE), lambda i: (0, 0)), row, row, vec],
        out_shape=[
            jax.ShapeDtypeStruct((8, LANE), F32),
            jax.ShapeDtypeStruct((S, D), F32),
            jax.ShapeDtypeStruct((S, D), BF16),
            jax.ShapeDtypeStruct((1, D), F32),
        ],
        compiler_params=_params(("arbitrary",)),
    )(out, x, tgt, g_post)


def _pre_bwd(dh, x, dy, g_pre, ts=256):
    n = S // ts

    def body(dh_ref, x_ref, dy_ref, g_ref, gx_ref, dg_ref):
        i = pl.program_id(0)

        @pl.when(i == 0)
        def _():
            dg_ref[...] = jnp.zeros_like(dg_ref)

        xv = x_ref[...]
        dhv = dh_ref[...]
        inv = lax.rsqrt(jnp.mean(xv * xv, axis=-1, keepdims=True) + EPS)
        xn = xv * inv
        dg_ref[...] += jnp.sum(dhv * xn, axis=0, keepdims=True)
        dxn = dhv * g_ref[...]
        gx_ref[...] = dy_ref[...] + inv * (dxn - xn * jnp.mean(dxn * xn, axis=-1, keepdims=True))

    row = pl.BlockSpec((ts, D), lambda i: (i, 0))
    vec = pl.BlockSpec((1, D), lambda i: (0, 0))
    return pl.pallas_call(
        body,
        name="pre_bwd",
        grid=(n,),
        in_specs=[row, row, row, vec],
        out_specs=[row, vec],
        out_shape=[jax.ShapeDtypeStruct((S, D), F32), jax.ShapeDtypeStruct((1, D), F32)],
        compiler_params=_params(("arbitrary",)),
    )(dh, x, dy, g_pre)


def _memnorm_bwd(dmemn, mem):
    def body(d_ref, m_ref, dg_ref):
        mv = m_ref[...]
        inv = lax.rsqrt(jnp.mean(mv * mv, axis=-1, keepdims=True) + EPS)
        dg_ref[...] = jnp.sum(d_ref[...] * mv * inv, axis=0, keepdims=True)

    return pl.pallas_call(
        body,
        name="memnorm_bwd",
        out_shape=jax.ShapeDtypeStruct((1, D), F32),
        compiler_params=_params(),
    )(dmemn, mem)


T_RNN = 256


def _neg_expm1(z):
    poly = -z * (1.0 + z * (0.5 + z * (1.0 / 6 + z * (1.0 / 24 + z * (1.0 / 120 + z * (1.0 / 720))))))
    return jnp.where(z > -0.1, poly, 1.0 - jnp.exp(z))


def _softplus_neg(lam):
    return jnp.maximum(-lam, 0.0) + jnp.log1p(jnp.exp(-jnp.abs(lam)))


def _rnn_gates(conv, wa_ref, ba, wx_ref, bx, lam, first_row):
    cbf = conv.astype(BF16)
    ga, gx = [], []
    for n in range(RNN_BLOCKS):
        c_n = cbf[:, n * LANE:(n + 1) * LANE]
        ga.append(jnp.dot(c_n, wa_ref[n], preferred_element_type=F32))
        gx.append(jnp.dot(c_n, wx_ref[n], preferred_element_type=F32))
    gate_r = _sigmoid(jnp.concatenate(ga, axis=1) + ba)
    gate_i = _sigmoid(jnp.concatenate(gx, axis=1) + bx)
    sp = _softplus_neg(lam)
    log_a = -LRU_C * gate_r * sp
    a = jnp.exp(log_a)
    mult_raw = jnp.sqrt(_neg_expm1(2.0 * log_a))
    mult = jnp.where(first_row, 1.0, mult_raw)
    return cbf, gate_r, gate_i, sp, a, mult_raw, mult


def _rglru_fwd(p_a, conv_w, conv_b, wa, ba, wx, bx, lam):
    t = T_RNN
    n = S // t

    def body(xr_ref, g_ref, cw_ref, cb_ref, wa_ref, ba_ref, wx_ref, bx_ref, lam_ref,
             y_ref, h_ref, xp_s, hcar, a_s, b_s):
        i = pl.program_id(0)

        @pl.when(i == 0)
        def _():
            xp_s[0:8, :] = jnp.zeros((8, D_RNN), F32)
            hcar[...] = jnp.zeros_like(hcar)

        @pl.when(i > 0)
        def _():
            xp_s[0:8, :] = xp_s[t:t + 8, :]

        xp_s[8:8 + t, :] = xr_ref[...]
        conv = cb_ref[...]
        for k in range(CONV_W):
            conv = conv + cw_ref[k:k + 1, :] * xp_s[8 - k:8 - k + t, :]
        rows = i * t + lax.broadcasted_iota(jnp.int32, (t, 1), 0)
        _, _, gate_i, _, a, _, mult = _rnn_gates(
            conv, wa_ref, ba_ref[...], wx_ref, bx_ref[...], lam_ref[...], rows == 0)
        a_s[...] = a
        b_s[...] = mult * gate_i * conv

        def step(tt, h):
            h = a_s[pl.ds(tt, 1), :] * h + b_s[pl.ds(tt, 1), :]
            h_ref[pl.ds(tt, 1), :] = h
            return h

        hcar[...] = lax.fori_loop(0, t, step, hcar[...], unroll=8)
        g = g_ref[...]
        y_ref[...] = (h_ref[...] * (g * _sigmoid(g))).astype(BF16)

    blk = lambda c: pl.BlockSpec((t, D_RNN), lambda i: (i, c))
    full = lambda shape: pl.BlockSpec(shape, lambda i: (0,) * len(shape))
    return pl.pallas_call(
        body,
        name="rglru_fwd",
        grid=(n,),
        in_specs=[blk(0), blk(1), full((CONV_W, D_RNN)), full((1, D_RNN)),
                  full((RNN_BLOCKS, LANE, LANE)), full((1, D_RNN)),
                  full((RNN_BLOCKS, LANE, LANE)), full((1, D_RNN)), full((1, D_RNN))],
        out_specs=[blk(0), blk(0)],
        out_shape=[jax.ShapeDtypeStruct((S, D_RNN), BF16), jax.ShapeDtypeStruct((S, D_RNN), F32)],
        scratch_shapes=[pltpu.VMEM((t + 8, D_RNN), F32), pltpu.VMEM((1, D_RNN), F32),
                        pltpu.VMEM((t, D_RNN), F32), pltpu.VMEM((t, D_RNN), F32)],
        compiler_params=_params(("arbitrary",)),
    )(p_a, p_a, conv_w, conv_b, wa, ba, wx, bx, lam)


def _rglru_bwd(dy, p_a, hseq, conv_w, conv_b, wa, ba, wx, bx, lam):
    t = T_RNN
    n = S // t
    rb = t // 8

    def body(dy_ref, xr_ref, g_ref, h_ref, xrp_ref, hp_ref, cw_ref, cb_ref, wa_ref, ba_ref, wx_ref, bx_ref, lam_ref,
             dp_ref, dcw_ref, dcb_ref, dwa_ref, dba_ref, dwx_ref, dbx_ref, dlam_ref,
             xp_s, hp_s, dxp_s, lamcar, a_s, dh_s, lam_s):
        i = pl.program_id(0)
        r = n - 1 - i

        @pl.when(i == 0)
        def _():
            for ref in (dcw_ref, dcb_ref, dwa_ref, dba_ref, dwx_ref, dbx_ref, dlam_ref, lamcar):
                ref[...] = jnp.zeros_like(ref)
            dxp_s[t:t + 8, :] = jnp.zeros((8, D_RNN), F32)

        @pl.when(i > 0)
        def _():
            dxp_s[t:t + 8, :] = dxp_s[0:8, :]

        has_prev = r > 0
        xp_s[0:8, :] = jnp.where(has_prev, xrp_ref[...], 0.0)
        xp_s[8:8 + t, :] = xr_ref[...]
        hp_s[0:8, :] = jnp.where(has_prev, hp_ref[...], 0.0)
        hp_s[8:8 + t, :] = h_ref[...]
        xs = [xp_s[8 - k:8 - k + t, :] for k in range(CONV_W)]
        conv = cb_ref[...]
        for k in range(CONV_W):
            conv = conv + cw_ref[k:k + 1, :] * xs[k]
        rows = r * t + lax.broadcasted_iota(jnp.int32, (t, 1), 0)
        first = rows == 0
        lam_p = lam_ref[...]
        cbf, gate_r, gate_i, sp, a, mult_raw, mult = _rnn_gates(
            conv, wa_ref, ba_ref[...], wx_ref, bx_ref[...], lam_p, first)

        g = g_ref[...]
        sg = _sigmoid(g)
        dyv = dy_ref[...]
        a_s[...] = a
        dh_s[...] = dyv * (g * sg)
        dg = dyv * h_ref[...] * (sg * (1.0 + g * (1.0 - sg)))

        def step(jj, car):
            tt = t - 1 - jj
            lm = dh_s[pl.ds(tt, 1), :] + car
            lam_s[pl.ds(tt, 1), :] = lm
            return a_s[pl.ds(tt, 1), :] * lm

        lamcar[...] = lax.fori_loop(0, t, step, lamcar[...], unroll=8)
        db = lam_s[...]
        da = db * hp_s[7:7 + t, :]
        dmult = db * gate_i * conv
        dgate_i = db * mult * conv
        dconv = db * mult * gate_i
        dlog_a = da * a + jnp.where(first, 0.0, dmult * (-(a * a) / mult_raw))
        dgate_r = dlog_a * (-LRU_C * sp)
        dsp = jnp.sum(dlog_a * (-LRU_C * gate_r), axis=0, keepdims=True)
        dlam_ref[...] += dsp * (-_sigmoid(-lam_p))
        dga = dgate_r * gate_r * (1.0 - gate_r)
        dgx = dgate_i * gate_i * (1.0 - gate_i)
        dba_ref[...] += jnp.sum(dga, axis=0, keepdims=True)
        dbx_ref[...] += jnp.sum(dgx, axis=0, keepdims=True)
        dga16, dgx16 = dga.astype(BF16), dgx.astype(BF16)
        back = []
        for nb in range(RNN_BLOCKS):
            sl = slice(nb * LANE, (nb + 1) * LANE)
            dwa_ref[nb] += lax.dot_general(cbf[:, sl], dga16[:, sl], _DIMS["tn"], preferred_element_type=F32)
            dwx_ref[nb] += lax.dot_general(cbf[:, sl], dgx16[:, sl], _DIMS["tn"], preferred_element_type=F32)
            back.append(lax.dot_general(dga16[:, sl], wa_ref[nb], _DIMS["nt"], preferred_element_type=F32)
                        + lax.dot_general(dgx16[:, sl], wx_ref[nb], _DIMS["nt"], preferred_element_type=F32))
        dconv = dconv + jnp.concatenate(back, axis=1)
        dcb_ref[...] += jnp.sum(dconv, axis=0, keepdims=True)
        for k in range(CONV_W):
            dcw_ref[k:k + 1, :] += jnp.sum(dconv * xs[k], axis=0, keepdims=True)
        dxp_s[0:t, :] = dconv
        dxr = cw_ref[0:1, :] * dconv
        for k in range(1, CONV_W):
            dxr = dxr + cw_ref[k:k + 1, :] * dxp_s[k:k + t, :]
        dp_ref[:, 0:D_RNN] = dxr.astype(BF16)
        dp_ref[:, D_RNN:2 * D_RNN] = dg.astype(BF16)

    blk = lambda c: pl.BlockSpec((t, D_RNN), lambda i: (n - 1 - i, c))
    prev8 = pl.BlockSpec((8, D_RNN), lambda i: (jnp.maximum((n - 1 - i) * rb - 1, 0), 0))
    full = lambda shape: pl.BlockSpec(shape, lambda i: (0,) * len(shape))
    vec = full((1, D_RNN))
    mat = full((RNN_BLOCKS, LANE, LANE))
    return pl.pallas_call(
        body,
        name="rglru_bwd",
        grid=(n,),
        in_specs=[blk(0), blk(0), blk(1), blk(0), prev8, prev8,
                  full((CONV_W, D_RNN)), vec, mat, vec, mat, vec, vec],
        out_specs=[pl.BlockSpec((t, 2 * D_RNN), lambda i: (n - 1 - i, 0)),
                   full((CONV_W, D_RNN)), vec, mat, vec, mat, vec, vec],
        out_shape=[jax.ShapeDtypeStruct((S, 2 * D_RNN), BF16),
                   jax.ShapeDtypeStruct((CONV_W, D_RNN), F32), jax.ShapeDtypeStruct((1, D_RNN), F32),
                   jax.ShapeDtypeStruct((RNN_BLOCKS, LANE, LANE), F32), jax.ShapeDtypeStruct((1, D_RNN), F32),
                   jax.ShapeDtypeStruct((RNN_BLOCKS, LANE, LANE), F32), jax.ShapeDtypeStruct((1, D_RNN), F32),
                   jax.ShapeDtypeStruct((1, D_RNN), F32)],
        scratch_shapes=[pltpu.VMEM((t + 8, D_RNN), F32), pltpu.VMEM((t + 8, D_RNN), F32),
                        pltpu.VMEM((t + 8, D_RNN), F32), pltpu.VMEM((1, D_RNN), F32),
                        pltpu.VMEM((t, D_RNN), F32), pltpu.VMEM((t, D_RNN), F32), pltpu.VMEM((t, D_RNN), F32)],
        compiler_params=_params(("arbitrary",)),
    )(dy, p_a, p_a, hseq, p_a, hseq, conv_w, conv_b, wa, ba, wx, bx, lam)


QB = WINDOW
KB2 = 2 * WINDOW
N_QB = S // QB
N_PAIR = SWA_HEADS // 2


def _swa_keys(kvc_ref, kvp_ref):
    kk = jnp.concatenate([kvp_ref[:, 0:LANE], kvc_ref[:, 0:LANE]], axis=0)
    vv = jnp.concatenate([kvp_ref[:, LANE:2 * LANE], kvc_ref[:, LANE:2 * LANE]], axis=0)
    lo = lax.broadcasted_iota(jnp.int32, (1, LANE), 1) < SWA_HD
    kk_sw, vv_sw = pltpu.roll(kk, SWA_HD, 1), pltpu.roll(vv, SWA_HD, 1)
    kd = [jnp.where(lo, kk, kk_sw).astype(BF16), jnp.where(lo, kk_sw, kk).astype(BF16)]
    vd = [jnp.where(lo, vv, vv_sw).astype(BF16), jnp.where(lo, vv_sw, vv).astype(BF16)]
    return lo, kd, vd


def _swa_valid(n):
    qi = lax.broadcasted_iota(jnp.int32, (QB, KB2), 0)
    kj = lax.broadcasted_iota(jnp.int32, (QB, KB2), 1)
    dist = qi + WINDOW - kj
    return (dist >= 0) & (dist < WINDOW) & ((n > 0) | (kj >= WINDOW))


def _swa_probs(qh16, kd, bias, sink, valid):
    lg = lax.dot_general(qh16, kd, _DIMS["nt"], preferred_element_type=F32) * (SWA_HD ** -0.5) + bias
    lg = jnp.where(valid, lg, NEG_INF)
    m = jnp.maximum(jnp.max(lg, axis=-1, keepdims=True), sink)
    p = jnp.exp(lg - m)
    es = jnp.exp(sink - m)
    den = jnp.sum(p, axis=-1, keepdims=True) + es
    return p / den, es / den


def _swa_specs():
    q = pl.BlockSpec((QB, D_RNN), lambda n: (n, 0))
    g = pl.BlockSpec((QB, D_RNN), lambda n: (n, 1))
    kvc = pl.BlockSpec((QB, 2 * LANE), lambda n: (n, 8))
    kvp = pl.BlockSpec((QB, 2 * LANE), lambda n: (jnp.maximum(n - 1, 0), 8))
    bias = pl.BlockSpec((SWA_HEADS, QB, KB2), lambda n: (0, 0, 0))
    sinks = pl.BlockSpec(memory_space=pltpu.SMEM)
    return q, g, kvc, kvp, bias, sinks


def _swa_fwd(p_b, bias_t, sinks):
    def body(q_ref, g_ref, kvc_ref, kvp_ref, bias_ref, sink_ref, y_ref, o_ref):
        n = pl.program_id(0)
        lo, kd, vd = _swa_keys(kvc_ref, kvp_ref)
        valid = _swa_valid(n)
        for hp in range(N_PAIR):
            sl = slice(hp * LANE, (hp + 1) * LANE)
            kvh = hp // (N_PAIR // 2)
            q = q_ref[:, sl]
            outs = []
            for j in range(2):
                mh = lo if j == 0 else jnp.logical_not(lo)
                qh16 = jnp.where(mh, q, 0.0).astype(BF16)
                probs, _ = _swa_probs(qh16, kd[kvh], bias_ref[2 * hp + j], sink_ref[2 * hp + j], valid)
                outs.append(jnp.dot(probs.astype(BF16), vd[kvh], preferred_element_type=F32))
            o = jnp.where(lo, outs[0], outs[1])
            o_ref[:, sl] = o
            g = g_ref[:, sl]
            y_ref[:, sl] = (o * (g * _sigmoid(g))).astype(BF16)

    q, g, kvc, kvp, bias, sinks_spec = _swa_specs()
    out = pl.BlockSpec((QB, D_RNN), lambda n: (n, 0))
    return pl.pallas_call(
        body,
        name="swa_fwd",
        grid=(N_QB,),
        in_specs=[q, g, kvc, kvp, bias, sinks_spec],
        out_specs=[out, out],
        out_shape=[jax.ShapeDtypeStruct((S, D_RNN), BF16), jax.ShapeDtypeStruct((S, D_RNN), F32)],
        compiler_params=_params(("parallel",)),
    )(p_b, p_b, p_b, p_b, bias_t, sinks)


def _swa_bwd(dy, p_b, o_swa, bias_t, sinks, after=None):
    def body(dy_ref, q_ref, g_ref, kvc_ref, kvp_ref, o_ref, bias_ref, sink_ref, *rest):
        dp_ref, dk_ref, dv_ref, dbias_ref, dsink_ref = rest[-5:]
        n = pl.program_id(0)

        @pl.when(n == 0)
        def _():
            for ref in (dk_ref, dv_ref, dbias_ref, dsink_ref):
                ref[...] = jnp.zeros_like(ref)

        lo, kd, vd = _swa_keys(kvc_ref, kvp_ref)
        hi = jnp.logical_not(lo)
        valid = _swa_valid(n)
        dk_blk = jnp.zeros((KB2, LANE), F32)
        dv_blk = jnp.zeros((KB2, LANE), F32)
        for kvh in range(2):
            dk_pair = jnp.zeros((KB2, LANE), F32)
            dv_pair = jnp.zeros((KB2, LANE), F32)
            for hp in range(kvh * (N_PAIR // 2), (kvh + 1) * (N_PAIR // 2)):
                sl = slice(hp * LANE, (hp + 1) * LANE)
                q = q_ref[:, sl]
                g = g_ref[:, sl]
                o = o_ref[:, sl]
                dyv = dy_ref[:, sl]
                sg = _sigmoid(g)
                do = dyv * (g * sg)
                dp_ref[:, D_RNN + hp * LANE:D_RNN + (hp + 1) * LANE] = (
                    dyv * o * (sg * (1.0 + g * (1.0 - sg)))).astype(BF16)
                dqs = []
                for j in range(2):
                    h = 2 * hp + j
                    mh = lo if j == 0 else hi
                    qh16 = jnp.where(mh, q, 0.0).astype(BF16)
                    sink = sink_ref[h]
                    probs, psink = _swa_probs(qh16, kd[kvh], bias_ref[h], sink, valid)
                    doh = jnp.where(mh, do, 0.0)
                    doh16 = doh.astype(BF16)
                    delta = jnp.sum(doh * o, axis=-1, keepdims=True)
                    dpr = lax.dot_general(doh16, vd[kvh], _DIMS["nt"], preferred_element_type=F32)
                    ds = probs * (dpr - delta)
                    dbias_ref[h] += ds
                    dsink_ref[h:h + 1, :] += jnp.zeros((1, LANE), F32) - jnp.sum(psink * delta)
                    ds16 = (ds * (SWA_HD ** -0.5)).astype(BF16)
                    dqs.append(jnp.dot(ds16, kd[kvh], preferred_element_type=F32))
                    dk_pair = dk_pair + lax.dot_general(ds16, qh16, _DIMS["tn"], preferred_element_type=F32)
                    dv_pair = dv_pair + lax.dot_general(probs.astype(BF16), doh16, _DIMS["tn"],
                                                        preferred_element_type=F32)
                dp_ref[:, sl] = jnp.where(lo, dqs[0], dqs[1]).astype(BF16)
            keep = lo if kvh == 0 else hi
            dk_blk = dk_blk + jnp.where(keep, dk_pair + pltpu.roll(dk_pair, SWA_HD, 1), 0.0)
            dv_blk = dv_blk + jnp.where(keep, dv_pair + pltpu.roll(dv_pair, SWA_HD, 1), 0.0)

        cur = pl.ds(pl.multiple_of(n * QB, QB), QB)
        dk_ref[cur, :] += dk_blk[QB:KB2]
        dv_ref[cur, :] += dv_blk[QB:KB2]

        @pl.when(n > 0)
        def _():
            prev = pl.ds(pl.multiple_of((n - 1) * QB, QB), QB)
            dk_ref[prev, :] += dk_blk[0:QB]
            dv_ref[prev, :] += dv_blk[0:QB]

    q, g, kvc, kvp, bias, sinks_spec = _swa_specs()
    row = pl.BlockSpec((QB, D_RNN), lambda n: (n, 0))
    acc = pl.BlockSpec((S, LANE), lambda n: (0, 0))
    return pl.pallas_call(
        body,
        name="swa_bwd",
        grid=(N_QB,),
        in_specs=[row, q, g, kvc, kvp, row, bias, sinks_spec] + ([ANY] if after is not None else []),
        out_specs=[pl.BlockSpec((QB, 2 * D_RNN), lambda n: (n, 0)), acc, acc, bias,
                   pl.BlockSpec((SWA_HEADS, LANE), lambda n: (0, 0))],
        out_shape=[jax.ShapeDtypeStruct((S, GROUP_TILES["B"] * LANE), BF16),
                   jax.ShapeDtypeStruct((S, LANE), F32), jax.ShapeDtypeStruct((S, LANE), F32),
                   jax.ShapeDtypeStruct((SWA_HEADS, QB, KB2), F32),
                   jax.ShapeDtypeStruct((SWA_HEADS, LANE), F32)],
        compiler_params=_params(("arbitrary",)),
    )(dy, p_b, p_b, p_b, p_b, o_swa, bias_t, sinks, *([after] if after is not None else []))


def _swa_pack(dp_b, dk, dv, ts=512):
    def body(_, dk_ref, dv_ref, o_ref):
        o_ref[:, 0:LANE] = dk_ref[...].astype(BF16)
        o_ref[:, LANE:2 * LANE] = dv_ref[...].astype(BF16)

    tile = pl.BlockSpec((ts, LANE), lambda i: (i, 0))
    return pl.pallas_call(
        body,
        name="swa_pack",
        grid=(S // ts,),
        in_specs=[pl.BlockSpec(memory_space=pl.ANY), tile, tile],
        out_specs=pl.BlockSpec((ts, 2 * LANE), lambda i: (i, 8)),
        out_shape=jax.ShapeDtypeStruct(dp_b.shape, dp_b.dtype),
        input_output_aliases={0: 0},
        compiler_params=_params(("parallel",)),
    )(dp_b, dk, dv)


def _split3(v):
    a = v.astype(BF16)
    r = v - a.astype(F32)
    b = r.astype(BF16)
    c = (r - b.astype(F32)).astype(BF16)
    return a, b, c


def _relbias_grad(dbias_flat, onehot_t):
    def body(d_ref, e_ref, o_ref):
        e = e_ref[...]
        acc = jnp.zeros((SWA_HEADS, REL_BUCKETS), F32)
        for term in _split3(d_ref[...]):
            acc = acc + lax.dot_general(term, e, _DIMS["nt"], preferred_element_type=F32)
        o_ref[...] = acc

    return pl.pallas_call(
        body,
        name="relbias_grad",
        out_shape=jax.ShapeDtypeStruct((SWA_HEADS, REL_BUCKETS), F32),
        compiler_params=_params(),
    )(dbias_flat, onehot_t)


TS_MEM = 512


def _mem_probs(q16, mk):
    lg = lax.dot_general(q16, mk, _DIMS["nt"], preferred_element_type=F32) * (MEM_HD ** -0.5)
    p = jnp.exp(lg - jnp.max(lg, axis=-1, keepdims=True))
    return p / jnp.sum(p, axis=-1, keepdims=True)


def _mem_fwd(p_c, mkv):
    def body(q_ref, g_ref, mkv_ref, y_ref, o_ref):
        for hm in range(MEM_HEADS):
            sl = slice(hm * MEM_HD, (hm + 1) * MEM_HD)
            probs = _mem_probs(q_ref[:, sl].astype(BF16), mkv_ref[:, sl])
            o = jnp.dot(probs.astype(BF16), mkv_ref[:, D_RNN + hm * MEM_HD:D_RNN + (hm + 1) * MEM_HD],
                        preferred_element_type=F32)
            o_ref[:, sl] = o
            g = g_ref[:, sl]
            y_ref[:, sl] = (o * (g * _sigmoid(g))).astype(BF16)

    blk = lambda c: pl.BlockSpec((TS_MEM, D_RNN), lambda i: (i, c))
    return pl.pallas_call(
        body,
        name="mem_fwd",
        grid=(S // TS_MEM,),
        in_specs=[blk(0), blk(1), pl.BlockSpec((MEM, 2 * D_RNN), lambda i: (0, 0))],
        out_specs=[blk(0), blk(0)],
        out_shape=[jax.ShapeDtypeStruct((S, D_RNN), BF16), jax.ShapeDtypeStruct((S, D_RNN), F32)],
        compiler_params=_params(("parallel",)),
    )(p_c, p_c, mkv)


def _mem_bwd(dy, p_c, o_mem, mkv):
    def body(dy_ref, q_ref, g_ref, o_ref, mkv_ref, dp_ref, dmkv_ref):
        @pl.when(pl.program_id(0) == 0)
        def _():
            dmkv_ref[...] = jnp.zeros_like(dmkv_ref)

        for hm in range(MEM_HEADS):
            sl = slice(hm * MEM_HD, (hm + 1) * MEM_HD)
            sv = slice(D_RNN + hm * MEM_HD, D_RNN + (hm + 1) * MEM_HD)
            q16 = q_ref[:, sl].astype(BF16)
            mk, mv = mkv_ref[:, sl], mkv_ref[:, sv]
            probs = _mem_probs(q16, mk)
            g, o, dyv = g_ref[:, sl], o_ref[:, sl], dy_ref[:, sl]
            sg = _sigmoid(g)
            do = dyv * (g * sg)
            dp_ref[:, sv] = (dyv * o * (sg * (1.0 + g * (1.0 - sg)))).astype(BF16)
            do16 = do.astype(BF16)
            delta = jnp.sum(do * o, axis=-1, keepdims=True)
            dpr = lax.dot_general(do16, mv, _DIMS["nt"], preferred_element_type=F32)
            ds16 = (probs * (dpr - delta) * (MEM_HD ** -0.5)).astype(BF16)
            dp_ref[:, sl] = jnp.dot(ds16, mk, preferred_element_type=F32).astype(BF16)
            dmkv_ref[:, sl] += lax.dot_general(ds16, q16, _DIMS["tn"], preferred_element_type=F32)
            dmkv_ref[:, sv] += lax.dot_general(probs.astype(BF16), do16, _DIMS["tn"], preferred_element_type=F32)

    blk = lambda c: pl.BlockSpec((TS_MEM, D_RNN), lambda i: (i, c))
    kv = pl.BlockSpec((MEM, 2 * D_RNN), lambda i: (0, 0))
    return pl.pallas_call(
        body,
        name="mem_bwd",
        grid=(S // TS_MEM,),
        in_specs=[blk(0), blk(0), blk(1), blk(0), kv],
        out_specs=[pl.BlockSpec((TS_MEM, 2 * D_RNN), lambda i: (i, 0)), kv],
        out_shape=[jax.ShapeDtypeStruct((S, 2 * D_RNN), BF16), jax.ShapeDtypeStruct((MEM, 2 * D_RNN), F32)],
        compiler_params=_params(("arbitrary",)),
    )(dy, p_c, p_c, o_mem, mkv)


TS_MRG = 512
TD_MRG = 512
N_DBLK = D // TD_MRG


def _merge_fwd(z, p_d):
    def body(z0, z1, z2, g0, g1, g2, o_ref):
        o_ref[...] = (_sigmoid(g0[...]) * z0[...] + _sigmoid(g1[...]) * z1[...]
                      + _sigmoid(g2[...]) * z2[...]).astype(BF16)

    blk = pl.BlockSpec((TS_MRG, TD_MRG), lambda i, d: (i, d))
    gate = lambda b: pl.BlockSpec((TS_MRG, TD_MRG), lambda i, d: (i, b * N_DBLK + d))
    return pl.pallas_call(
        body,
        name="merge_fwd",
        grid=(S // TS_MRG, N_DBLK),
        in_specs=[blk, blk, blk, gate(0), gate(1), gate(2)],
        out_specs=blk,
        out_shape=jax.ShapeDtypeStruct((S, D), BF16),
        compiler_params=_params(("parallel", "parallel")),
    )(z[0], z[1], z[2], p_d, p_d, p_d)


def _merge_bwd(dmerged, z_b, p_d, b, dp_d, after=None):
    def body(dm_ref, z_ref, g_ref, *refs):
        dz_ref, dg_ref = refs[-2], refs[-1]
        sg = _sigmoid(g_ref[...])
        dm = dm_ref[...]
        dz_ref[...] = (dm * sg).astype(BF16)
        dg_ref[...] = (dm * z_ref[...] * sg * (1.0 - sg)).astype(BF16)

    blk = pl.BlockSpec((TS_MRG, TD_MRG), lambda i, d: (i, d))
    gate = pl.BlockSpec((TS_MRG, TD_MRG), lambda i, d: (i, b * N_DBLK + d))
    in_specs = [blk, blk, gate]
    args = [dmerged, z_b, p_d]
    aliases = {}
    if dp_d is not None:
        in_specs.append(pl.BlockSpec(memory_space=pl.ANY))
        args.append(dp_d)
        aliases = {3: 1}
    if after is not None:
        in_specs.append(pl.BlockSpec(memory_space=pl.ANY))
        args.append(after)
    return pl.pallas_call(
        body,
        name=f"merge_bwd{b}",
        grid=(S // TS_MRG, N_DBLK),
        in_specs=in_specs,
        out_specs=[blk, gate],
        out_shape=[jax.ShapeDtypeStruct((S, D), BF16),
                   jax.ShapeDtypeStruct((S, GROUP_TILES["D"] * LANE), BF16)],
        input_output_aliases=aliases,
        compiler_params=_params(("parallel", "parallel")),
    )(*args)


def _bucket_table():
    import numpy as np
    qi = np.arange(QB)[:, None]
    kj = np.arange(KB2)[None, :]
    n = np.maximum(qi + WINDOW - kj, 0)
    max_exact = REL_BUCKETS // 2
    ratio = np.log(np.maximum(n, 1).astype(np.float32) / max_exact) / np.float32(math.log(REL_MAX_DIST / max_exact))
    large = np.minimum(max_exact + (ratio * (REL_BUCKETS - max_exact)).astype(np.int32), REL_BUCKETS - 1)
    bucket = np.where(n < max_exact, n, large).reshape(1, QB * KB2)
    return (bucket == np.arange(REL_BUCKETS)[:, None]).astype(np.float32)


def _bias_expand(rel_bias_t, onehot_t):
    def body(r_ref, e_ref, o_ref):
        e = e_ref[...]
        acc = jnp.zeros((SWA_HEADS, QB * KB2), F32)
        for term in _split3(r_ref[...]):
            acc = acc + jnp.dot(term, e, preferred_element_type=F32)
        o_ref[...] = acc

    return pl.pallas_call(
        body,
        name="bias_expand",
        out_shape=jax.ShapeDtypeStruct((SWA_HEADS, QB * KB2), F32),
        compiler_params=_params(),
    )(rel_bias_t, onehot_t)


PROJ_TN = {"A": 1024, "B": 1152, "C": 1024, "D": 1536}


def _local_step(x, mem, tgt, sp, fetch, prefetch, emit, advance):
    onehot_t = jnp.asarray(_bucket_table(), BF16)
    bias_t = _bias_expand(sp["rel_bias"].T, onehot_t).reshape(SWA_HEADS, QB, KB2)
    sinks = sp["swa_sinks"].reshape(SWA_HEADS)
    wa16, wx16 = sp["w_rg_a"].astype(BF16), sp["w_rg_x"].astype(BF16)
    rnn = (sp["conv_w"], sp["conv_b"], wa16, sp["b_rg_a"], wx16, sp["b_rg_x"], sp["lru_lambda"])

    h = _rms_fwd(x, sp["pre_norm_g"], "rms_pre")
    memn = _rms_fwd(mem, sp["mem_norm_g"], "rms_mem")
    w_grp, p = {}, {}

    def project(g, after, then=None):
        (w_grp[g],) = fetch((g,), after)
        tok = prefetch(then, w_grp[g]) if then is not None else None
        p[g] = _mm(h, w_grp[g], "nt", F32, 1024, PROJ_TN[g], D, f"proj_{g}", after=tok)

    project("A", h)
    y_rg, hseq = _rglru_fwd(p["A"], *rnn)
    project("B", y_rg)
    y_swa, o_swa = _swa_fwd(p["B"], bias_t, sinks)
    project("C", y_swa, then=("mk",))
    (wmk,) = fetch(("mk",), p["C"])
    tok = prefetch(("br0", "br1", "br2"), wmk)
    mkv = _mm(memn, wmk, "nn", BF16, MEM, 1024, D, "mkv", after=tok)
    y_mem, o_mem = _mem_fwd(p["C"], mkv)
    ys = (y_rg, y_swa, y_mem)
    wbr = fetch(("br0", "br1", "br2"), y_mem)
    tok = prefetch(("D",), wbr[2])
    z = [_mm(ys[b], wbr[b], "nn", F32, 1024, 1024, D_RNN, f"branch_out{b}", after=tok if b == 0 else None)
         for b in range(3)]
    project("D", z[2], then=("out",))
    merged = _merge_fwd(z, p["D"])
    (wout,) = fetch(("out",), merged)
    out = _mm(merged, wout, "nn", F32, 1024, 1024, D, "out_proj")
    sq, dy, dout, d_post = _post_loss(out, x, tgt, sp["post_norm_g"])

    tok = emit({"out": _mm(merged, dout, "tn", BF16, 1024, 1024, S, "d_wout")})
    dmerged = _mm(dout, wout, "nt", F32, 1024, 1024, D, "d_merged", after=tok)
    dz, dp_d = [], None
    tok = advance(dmerged)
    for b in range(3):
        dz_b, dp_d = _merge_bwd(dmerged, z[b], p["D"], b, dp_d, after=tok if b == 0 else None)
        dz.append(dz_b)
    d_win = lambda g, dp_g, after=None: _mm(dp_g, h, "tn", BF16, PROJ_TN[g], 1024, S, f"d_win_{g}", after=after)
    tok = emit({f"br{b}": _mm(ys[b], dz[b], "tn", BF16, 1024, 1024, S, f"d_wbr{b}") for b in range(3)}, tok)
    d_w_d = d_win("D", dp_d, tok)
    tok = emit({"D": d_w_d}, advance(d_w_d))
    dy_mem = _mm(dz[2], wbr[2], "nt", F32, 1024, 1024, D, "d_branch2", after=tok)
    tok = advance(dy_mem)
    dp_c, dmkv = _mem_bwd(dy_mem, p["C"], o_mem, mkv)
    dmkv16 = dmkv.astype(BF16)
    tok = emit({"mk": _mm(memn, dmkv16, "tn", BF16, 1024, 1024, MEM, "d_wmk", after=tok), "C": d_win("C", dp_c)}, tok)
    dmemn = _mm(dmkv16, wmk, "nt", F32, MEM, 1024, D, "d_memn", after=tok)
    tok = advance(dmemn)
    d_memg = _memnorm_bwd(dmemn, mem)
    dy_rg = _mm(dz[0], wbr[0], "nt", F32, 1024, 1024, D, "d_branch0", after=tok)
    dp_a, d_cw, d_cb, d_wa, d_ba, d_wx, d_bx, d_lam = _rglru_bwd(dy_rg, p["A"], hseq, *rnn)
    tok = emit({"A": d_win("A", dp_a)}, tok)
    dy_swa = _mm(dz[1], wbr[1], "nt", F32, 1024, 1024, D, "d_branch1", after=tok)
    tok = advance(dy_swa)
    dp_b, dk, dv, d_bias, d_sink = _swa_bwd(dy_swa, p["B"], o_swa, bias_t, sinks, after=tok)
    dp_b = _swa_pack(dp_b, dk, dv)
    d_rel = _relbias_grad(d_bias.reshape(SWA_HEADS, QB * KB2), onehot_t).T
    dp = {"A": dp_a, "B": dp_b, "C": dp_c, "D": dp_d}
    tok = emit({"B": d_win("B", dp_b)}, tok)
    dh = None
    for g in GROUPS:
        dh = _mm(dp[g], w_grp[g], "nn", F32, 1024, 1024, 2304 if g == "B" else 2048, f"d_h_{g}", acc=dh,
                 after=tok if g in ("A", "B") else None)
        if g == "A":
            tok = advance(dh)
    grad_x, d_pre = _pre_bwd(dh, x, dy, sp["pre_norm_g"])

    d_small = {
        "pre_norm_g": d_pre, "post_norm_g": d_post, "mem_norm_g": d_memg, "conv_w": d_cw, "conv_b": d_cb,
        "w_rg_a": d_wa, "b_rg_a": d_ba, "w_rg_x": d_wx, "b_rg_x": d_bx, "lru_lambda": d_lam,
        "swa_sinks": d_sink[:, 0].reshape(1, SWA_HEADS), "rel_bias": d_rel,
    }
    return sq, grad_x, d_small


ANY = pl.BlockSpec(memory_space=pl.ANY)
SHARD_ROWS = D // N_CHIPS
GATHERED = {"A": (2048, D), "B": (2304, D), "C": (2048, D), "D": (6144, D), "mk": (D, D),
            "br0": (D_RNN, D), "br1": (D_RNN, D), "br2": (D_RNN, D), "out": (D, D)}
SHARD_SHAPES = {"win": (SHARD, D), "mk": (SHARD_ROWS, D), "br0": (D_RNN, SHARD_ROWS), "br1": (D_RNN, SHARD_ROWS),
                "br2": (D_RNN, SHARD_ROWS), "out": (SHARD_ROWS, D)}
SHARDS = tuple(SHARD_SHAPES)
HALF_AXIS = {"win": 1, "mk": 1, "br0": 0, "br1": 0, "br2": 0, "out": 1,
             "A": 1, "B": 1, "C": 1, "D": 1}


def _halved(shape, axis):
    return (shape[0] // 2, shape[1]) if axis == 0 else (shape[0], shape[1] // 2)


class Piece(NamedTuple):
    src: str
    dst: str
    rows: int
    sr0: int
    sc0: int
    dr0: int
    dc0: int
    ncols: int


def _pieces_of(jj):
    out = [Piece("win", g, n, r, 0, gr, 0, D) for r, n, g, gr in _shard_runs(jj)]
    out.append(Piece("mk", "mk", SHARD_ROWS, 0, 0, SHARD_ROWS * jj, 0, D))
    out += [Piece(f"br{b}", f"br{b}", D_RNN, 0, 0, 0, SHARD_ROWS * jj, SHARD_ROWS) for b in range(3)]
    out.append(Piece("out", "out", SHARD_ROWS, 0, 0, SHARD_ROWS * jj, 0, D))
    return out


def _half_rect(ref, p, side, which):
    r0, c0 = (p.sr0, p.sc0) if side == "src" else (p.dr0, p.dc0)
    if HALF_AXIS[p.src] == 1:
        return _rect(ref, r0, p.rows, c0 + which * (p.ncols // 2), p.ncols // 2)
    return _rect(ref, r0 + which * (p.rows // 2), p.rows // 2, c0, p.ncols)


def _rect_in_half(ref, p, side):
    r0, c0 = (p.sr0, p.sc0) if side == "src" else (p.dr0, p.dc0)
    if HALF_AXIS[p.src] == 1:
        return _rect(ref, r0, p.rows, 0, p.ncols // 2)
    return _rect(ref, 0, p.rows // 2, c0, p.ncols)


MAX_PIECES = max(len(_pieces_of(jj)) for jj in range(N_CHIPS))


def _rect(ref, r0, rows, c0, ncols):
    return ref.at[pl.ds(r0, rows), pl.ds(c0, ncols)]


def _position():
    x, y, c = lax.axis_index("x"), lax.axis_index("y"), lax.axis_index("c")
    return x, y, c, 2 * x + y


HBM = pl.BlockSpec(memory_space=pltpu.HBM)
SEM = pl.BlockSpec(memory_space=pltpu.SEMAPHORE)
EFFECT = pltpu.SideEffectType.DATAFLOW_SIDE_EFFECTING
N_SEM = MAX_PIECES * N_CHIPS
GATHER_STAGES = (("A",), ("B",), ("C",), ("mk",), ("br0", "br1", "br2"), ("D",), ("out",))


def _in_hbm(a):
    return pltpu.with_memory_space_constraint(a, pltpu.HBM)


def _stage_pieces(jj, stage):
    return [(i, p) for i, p in enumerate(_pieces_of(jj)) if p.dst in stage]


def _own_block_table(g):
    import numpy as np
    tbl = np.zeros((N_CHIPS, GATHERED[g][0] // HALF_TILE), np.int32)
    for jj in range(N_CHIPS):
        for r, n, grp, gr in _shard_runs(jj):
            if grp == g:
                for k in range(n // HALF_TILE):
                    tbl[jj, gr // HALF_TILE + k] = r // HALF_TILE + k
    return tbl


def _place_group(w_t, g, table, after):
    nb = GATHERED[g][0] // HALF_TILE

    def body(t_ref, x_ref, _, o_ref):
        o_ref[...] = x_ref[...].astype(BF16)

    return pl.pallas_call(
        body,
        name=f"place_{g}",
        grid_spec=pltpu.PrefetchScalarGridSpec(
            num_scalar_prefetch=1,
            grid=(nb,),
            in_specs=[pl.BlockSpec((HALF_TILE, D), lambda b, t: (t[b], 0)), ANY],
            out_specs=pl.BlockSpec((HALF_TILE, D), lambda b, t: (b, 0)),
        ),
        out_shape=jax.ShapeDtypeStruct(GATHERED[g], BF16),
        compiler_params=_params(("parallel",)),
    )(table, w_t, after)


def _place_shard(shard, name, after):
    rows, cols = shard.shape
    by_rows = HALF_AXIS[name] == 1

    def body(x_ref, _, o_ref):
        o_ref[...] = x_ref[...].astype(BF16)

    return pl.pallas_call(
        body,
        name=f"place_{name}",
        grid=(N_CHIPS,),
        in_specs=[pl.BlockSpec((rows, cols), lambda b: (0, 0)), ANY],
        out_specs=pl.BlockSpec((rows, cols), (lambda b: (b, 0)) if by_rows else (lambda b: (0, b))),
        out_shape=jax.ShapeDtypeStruct(GATHERED[name], BF16),
        compiler_params=_params(("parallel",)),
    )(shard, after)


def _gather_copy(arr, send_sems, recv_sems, c, jj, i, p, kk):
    rect = _half_rect(arr[p.dst], p, "dst", c)
    return pltpu.make_async_remote_copy(
        src_ref=rect, dst_ref=rect, send_sem=send_sems.at[i * N_CHIPS + kk],
        recv_sem=recv_sems.at[jj * MAX_PIECES + i], device_id=(kk // 2, kk % 2, c), device_id_type=MESH)


def _gather_start(arrays, after):
    stage = tuple(arrays)
    na = len(stage)

    def body(*refs):
        arr = dict(zip(stage, refs[:na]))
        send_sems, recv_sems = refs[na + 1], refs[na + 2]
        token = refs[-1]
        _, _, c, j = _position()
        for jj in range(N_CHIPS):
            @pl.when(j == jj)
            def _():
                for i, p in _stage_pieces(jj, stage):
                    for kk in range(N_CHIPS):
                        if kk != jj:
                            _gather_copy(arr, send_sems, recv_sems, c, jj, i, p, kk).start()
        token[...] = jnp.zeros_like(token)

    outs = pl.pallas_call(
        body,
        name=f"gather_start_{stage[0]}",
        in_specs=[HBM] * na + [ANY],
        out_specs=[SEM, SEM] + [HBM] * na + [pl.BlockSpec(memory_space=pltpu.VMEM)],
        out_shape=[pltpu.SemaphoreType.DMA((N_SEM,)), pltpu.SemaphoreType.DMA((N_SEM,))]
        + [pltpu.HBM(GATHERED[n], BF16) for n in stage] + [jax.ShapeDtypeStruct((8, LANE), F32)],
        input_output_aliases={k: 2 + k for k in range(na)},
        compiler_params=pltpu.CompilerParams(has_side_effects=EFFECT),
    )(*[_in_hbm(arrays[n]) for n in stage], after)
    return outs[0], outs[1], dict(zip(stage, outs[2:2 + na])), outs[-1]


def _gather_wait(send_sems, recv_sems, arrays, after):
    stage = tuple(arrays)
    na = len(stage)

    def body(*refs):
        arr = dict(zip(stage, refs[:na]))
        sems_s, sems_r = refs[na], refs[na + 1]
        _, _, c, j = _position()
        for jj in range(N_CHIPS):
            @pl.when(j != jj)
            def _():
                for i, p in _stage_pieces(jj, stage):
                    _gather_copy(arr, sems_s, sems_r, c, jj, i, p, jj).wait_recv()

            @pl.when(j == jj)
            def _():
                for i, p in _stage_pieces(jj, stage):
                    for kk in range(N_CHIPS):
                        if kk != jj:
                            _gather_copy(arr, sems_s, sems_r, c, jj, i, p, kk).wait_send()

    outs = pl.pallas_call(
        body,
        name=f"gather_wait_{stage[0]}",
        in_specs=[HBM] * na + [SEM, SEM, ANY],
        out_specs=[HBM] * na,
        out_shape=[pltpu.HBM(GATHERED[n], BF16) for n in stage],
        input_output_aliases={k: k for k in range(na)},
        compiler_params=pltpu.CompilerParams(has_side_effects=EFFECT),
    )(*[arrays[n] for n in stage], send_sems, recv_sems, after)
    return dict(zip(stage, outs))


def _gather_swap(arrays):
    stage = tuple(arrays)
    na = len(stage)

    def body(*refs):
        dst = dict(zip(stage, refs[na:2 * na]))
        send_sems, recv_sems = refs[2 * na:]
        x, y, c, j = _position()

        def fwd(jj, i, p, which):
            rect = _half_rect(dst[p.dst], p, "dst", which)
            return pltpu.make_async_remote_copy(
                src_ref=rect, dst_ref=rect, send_sem=send_sems.at[jj * MAX_PIECES + i],
                recv_sem=recv_sems.at[jj * MAX_PIECES + i], device_id=(x, y, 1 - c), device_id_type=MESH)

        for jj in range(N_CHIPS):
            @pl.when(j != jj)
            def _():
                for i, p in _stage_pieces(jj, stage):
                    fwd(jj, i, p, c).start()
        for jj in range(N_CHIPS):
            @pl.when(j != jj)
            def _():
                for i, p in _stage_pieces(jj, stage):
                    fwd(jj, i, p, 1 - c).wait_recv()
        for jj in range(N_CHIPS):
            @pl.when(j != jj)
            def _():
                for i, p in _stage_pieces(jj, stage):
                    fwd(jj, i, p, c).wait_send()

    outs = pl.pallas_call(
        body,
        name=f"gather_swap_{stage[0]}",
        in_specs=[ANY] * na,
        out_specs=[ANY] * na,
        out_shape=[jax.ShapeDtypeStruct(GATHERED[n], BF16) for n in stage],
        input_output_aliases={k: k for k in range(na)},
        scratch_shapes=[pltpu.SemaphoreType.DMA((N_SEM,)), pltpu.SemaphoreType.DMA((N_SEM,))],
        compiler_params=pltpu.CompilerParams(has_side_effects=True),
    )(*[arrays[n] for n in stage])
    return dict(zip(stage, outs))


def _pass_on_copy(arr, send_sems, recv_sems, x, y, c, jj, i, p, which):
    rect = _half_rect(arr[p.dst], p, "dst", which)
    return pltpu.make_async_remote_copy(
        src_ref=rect, dst_ref=rect, send_sem=send_sems.at[jj * MAX_PIECES + i],
        recv_sem=recv_sems.at[jj * MAX_PIECES + i], device_id=(x, y, 1 - c), device_id_type=MESH)


def _gather_pass_start(arrays, after):
    stage = tuple(arrays)
    na = len(stage)

    def body(*refs):
        arr = dict(zip(stage, refs[:na]))
        x, y, c, j = _position()
        for jj in range(N_CHIPS):
            @pl.when(j != jj)
            def _():
                for i, p in _stage_pieces(jj, stage):
                    _pass_on_copy(arr, refs[na + 1], refs[na + 2], x, y, c, jj, i, p, c).start()
        refs[-1][...] = jnp.zeros_like(refs[-1])

    outs = pl.pallas_call(
        body,
        name=f"gather_pass_start_{stage[0]}",
        in_specs=[HBM] * na + [ANY],
        out_specs=[SEM, SEM] + [HBM] * na + [pl.BlockSpec(memory_space=pltpu.VMEM)],
        out_shape=[pltpu.SemaphoreType.DMA((N_SEM,)), pltpu.SemaphoreType.DMA((N_SEM,))]
        + [pltpu.HBM(GATHERED[n], BF16) for n in stage] + [jax.ShapeDtypeStruct((8, LANE), F32)],
        input_output_aliases={k: 2 + k for k in range(na)},
        compiler_params=pltpu.CompilerParams(has_side_effects=EFFECT),
    )(*[arrays[n] for n in stage], after)
    return outs[0], outs[1], dict(zip(stage, outs[2:2 + na])), outs[-1]


def _gather_pass_wait(send_sems, recv_sems, arrays, after):
    stage = tuple(arrays)
    na = len(stage)

    def body(*refs):
        arr = dict(zip(stage, refs[:na]))
        x, y, c, j = _position()
        for jj in range(N_CHIPS):
            @pl.when(j != jj)
            def _():
                for i, p in _stage_pieces(jj, stage):
                    _pass_on_copy(arr, refs[na], refs[na + 1], x, y, c, jj, i, p, 1 - c).wait_recv()
                    _pass_on_copy(arr, refs[na], refs[na + 1], x, y, c, jj, i, p, c).wait_send()

    outs = pl.pallas_call(
        body,
        name=f"gather_pass_wait_{stage[0]}",
        in_specs=[HBM] * na + [SEM, SEM, ANY],
        out_specs=[HBM] * na,
        out_shape=[pltpu.HBM(GATHERED[n], BF16) for n in stage],
        input_output_aliases={k: k for k in range(na)},
        compiler_params=pltpu.CompilerParams(has_side_effects=EFFECT),
    )(*[arrays[n] for n in stage], send_sems, recv_sems, after)
    return dict(zip(stage, outs))


def _own_half(ref, shape, axis, which):
    if axis == 1:
        return ref.at[:, pl.ds(which * (shape[1] // 2), shape[1] // 2)]
    return ref.at[pl.ds(which * (shape[0] // 2), shape[0] // 2), :]


def _swap_copies(names, src, dst, send_sems, recv_sems):
    x, y, c, _ = _position()
    return [pltpu.make_async_remote_copy(
        src_ref=_own_half(src[n], GATHERED[n], HALF_AXIS[n], 1 - c), dst_ref=dst[n],
        send_sem=send_sems.at[k], recv_sem=recv_sems.at[k],
        device_id=(x, y, 1 - c), device_id_type=MESH) for k, n in enumerate(names)]


def _swap_start(grads, after):
    names = tuple(grads)
    n = len(names)

    def body(*refs):
        src, dst = dict(zip(names, refs[:n])), dict(zip(names, refs[n:2 * n]))
        for cp in _swap_copies(names, src, dst, refs[2 * n + 1], refs[2 * n + 2]):
            cp.start()
        refs[-1][...] = jnp.zeros_like(refs[-1])

    half_shape = lambda nm: _halved(GATHERED[nm], HALF_AXIS[nm])
    args = [_in_hbm(grads[nm]) for nm in names] + [_in_hbm(lax.empty(half_shape(nm), BF16)) for nm in names]
    if after is None:
        after = jnp.zeros((8, LANE), F32)
    outs = pl.pallas_call(
        body,
        name=f"swap_start_{names[0]}",
        in_specs=[HBM] * (2 * n) + [ANY],
        out_specs=[SEM, SEM] + [HBM] * (2 * n) + [pl.BlockSpec(memory_space=pltpu.VMEM)],
        out_shape=[pltpu.SemaphoreType.DMA((n,)), pltpu.SemaphoreType.DMA((n,))]
        + [pltpu.HBM(GATHERED[nm], BF16) for nm in names] + [pltpu.HBM(half_shape(nm), BF16) for nm in names]
        + [jax.ShapeDtypeStruct((8, LANE), F32)],
        input_output_aliases={k: 2 + k for k in range(2 * n)},
        compiler_params=pltpu.CompilerParams(has_side_effects=EFFECT),
    )(*args, after)
    return outs[0], outs[1], dict(zip(names, outs[2:2 + n])), dict(zip(names, outs[2 + n:2 + 2 * n])), outs[-1]


def _swap_wait(send_sems, recv_sems, grads, landing, after):
    names = tuple(grads)
    n = len(names)

    def body(*refs):
        src, dst = dict(zip(names, refs[:n])), dict(zip(names, refs[n:2 * n]))
        copies = _swap_copies(names, src, dst, refs[2 * n], refs[2 * n + 1])
        for cp in copies:
            cp.wait_recv()
        for cp in copies:
            cp.wait_send()

    half_shape = lambda nm: _halved(GATHERED[nm], HALF_AXIS[nm])
    outs = pl.pallas_call(
        body,
        name=f"swap_wait_{names[0]}",
        in_specs=[HBM] * (2 * n) + [SEM, SEM, ANY],
        out_specs=[HBM] * (2 * n),
        out_shape=[pltpu.HBM(GATHERED[nm], BF16) for nm in names] + [pltpu.HBM(half_shape(nm), BF16) for nm in names],
        input_output_aliases={k: k for k in range(2 * n)},
        compiler_params=pltpu.CompilerParams(has_side_effects=EFFECT),
    )(*[grads[nm] for nm in names], *[landing[nm] for nm in names], send_sems, recv_sems, after)
    return dict(zip(names, outs[:n])), dict(zip(names, outs[n:]))


ADD_ROWS = 256


def _add_half(full, recv, c_arr, name):
    rows, cols = recv.shape
    if HALF_AXIS[name] == 1:
        index = lambda i, c_ref: (i, c_ref[0])
    else:
        nb = rows // ADD_ROWS
        index = lambda i, c_ref: (nb * c_ref[0] + i, 0)

    def body(c_ref, a_ref, b_ref, o_ref):
        o_ref[...] = (a_ref[...].astype(F32) + b_ref[...].astype(F32)).astype(BF16)

    return pl.pallas_call(
        body,
        name=f"add_half_{name}",
        grid_spec=pltpu.PrefetchScalarGridSpec(
            num_scalar_prefetch=1,
            grid=(rows // ADD_ROWS,),
            in_specs=[pl.BlockSpec((ADD_ROWS, cols), index), pl.BlockSpec((ADD_ROWS, cols), lambda i, c_ref: (i, 0))],
            out_specs=pl.BlockSpec((ADD_ROWS, cols), lambda i, c_ref: (i, 0)),
        ),
        out_shape=jax.ShapeDtypeStruct((rows, cols), BF16),
        compiler_params=_params(("parallel",)),
    )(c_arr, full, recv)


SLOT_SHAPES = {n: _halved(SHARD_SHAPES[n], HALF_AXIS[n]) for n in SHARDS}


def _slot_shape(n):
    return (N_CHIPS,) + SLOT_SHAPES[n]


def _stage_shards(stage):
    pieces = [p for jj in range(N_CHIPS) for p in _pieces_of(jj)]
    return tuple(s for s in SHARDS if any(p.src == s and p.dst in stage for p in pieces))


def _scatter_copy(src, dst, send_sems, recv_sems, c, jj, kk, i, p):
    return pltpu.make_async_remote_copy(
        src_ref=_rect_in_half(src[p.dst], p, "dst"), dst_ref=_rect_in_half(dst[p.src].at[jj], p, "src"),
        send_sem=send_sems.at[kk * MAX_PIECES + i], recv_sem=recv_sems.at[jj * MAX_PIECES + i],
        device_id=(kk // 2, kk % 2, c), device_id_type=MESH)


def _scatter_start(halves, slots):
    stage, touched = tuple(halves), tuple(slots)
    nh, nt = len(stage), len(touched)

    def body(*refs):
        src = dict(zip(stage, refs[:nh]))
        dst = dict(zip(touched, refs[nh:nh + nt]))
        send_sems, recv_sems = refs[nh + nt], refs[nh + nt + 1]
        token = refs[-1]
        _, _, c, j = _position()
        for jj in range(N_CHIPS):
            @pl.when(j == jj)
            def _():
                for kk in range(N_CHIPS):
                    if kk != jj:
                        for i, p in _stage_pieces(kk, stage):
                            _scatter_copy(src, dst, send_sems, recv_sems, c, jj, kk, i, p).start()
        token[...] = jnp.zeros_like(token)

    outs = pl.pallas_call(
        body,
        name=f"scatter_start_{stage[0]}",
        in_specs=[HBM] * (nh + nt),
        out_specs=[SEM, SEM] + [HBM] * (nh + nt) + [pl.BlockSpec(memory_space=pltpu.VMEM)],
        out_shape=[pltpu.SemaphoreType.DMA((N_SEM,)), pltpu.SemaphoreType.DMA((N_SEM,))]
        + [pltpu.HBM(halves[n].shape, BF16) for n in stage] + [pltpu.HBM(_slot_shape(s), BF16) for s in touched]
        + [jax.ShapeDtypeStruct((8, LANE), F32)],
        input_output_aliases={k: 2 + k for k in range(nh + nt)},
        compiler_params=pltpu.CompilerParams(has_side_effects=EFFECT),
    )(*[_in_hbm(halves[n]) for n in stage], *[_in_hbm(slots[s]) for s in touched])
    return outs[0], outs[1], dict(zip(stage, outs[2:2 + nh])), dict(zip(touched, outs[2 + nh:2 + nh + nt])), outs[-1]


def _scatter_wait(send_sems, recv_sems, halves, slots, after):
    stage, touched = tuple(halves), tuple(slots)
    nh, nt = len(stage), len(touched)

    def body(*refs):
        src = dict(zip(stage, refs[:nh]))
        dst = dict(zip(touched, refs[nh:nh + nt]))
        sems_s, sems_r = refs[nh + nt], refs[nh + nt + 1]
        _, _, c, j = _position()
        for jj in range(N_CHIPS):
            @pl.when(j == jj)
            def _():
                for ss in range(N_CHIPS):
                    if ss != jj:
                        for i, p in _stage_pieces(jj, stage):
                            _scatter_copy(src, dst, sems_s, sems_r, c, ss, jj, i, p).wait_recv()
                for kk in range(N_CHIPS):
                    if kk != jj:
                        for i, p in _stage_pieces(kk, stage):
                            _scatter_copy(src, dst, sems_s, sems_r, c, jj, kk, i, p).wait_send()

    outs = pl.pallas_call(
        body,
        name=f"scatter_wait_{stage[0]}",
        in_specs=[HBM] * (nh + nt) + [SEM, SEM, ANY],
        out_specs=[HBM] * (nh + nt),
        out_shape=[pltpu.HBM(halves[n].shape, BF16) for n in stage] + [pltpu.HBM(_slot_shape(s), BF16) for s in touched],
        input_output_aliases={k: k for k in range(nh + nt)},
        compiler_params=pltpu.CompilerParams(has_side_effects=EFFECT),
    )(*[halves[n] for n in stage], *[slots[s] for s in touched], send_sems, recv_sems, after)
    return dict(zip(stage, outs[:nh])), dict(zip(touched, outs[nh:]))


SUM_ROWS = {"win": 448, "mk": 256, "br0": 256, "br1": 256, "br2": 256, "out": 256}


def _sum_in_chip_order(chip, own, s_ref):
    acc = None
    for k in range(N_CHIPS):
        term = jnp.where(chip == k, own, s_ref[k].astype(F32))
        acc = term if acc is None else acc + term
    return acc


def _sum_slots(slots, own_half, pos_arr, name):
    _, rows, cols = slots.shape
    tr = SUM_ROWS[name]
    nb = rows // tr
    if HALF_AXIS[name] == 1:
        own_index = lambda i, pos: (nb * pos[1] + i, 0)
        out_index = lambda i, pos: (i, pos[0])
    else:
        own_index = lambda i, pos: (i, pos[1])
        out_index = lambda i, pos: (nb * pos[0] + i, 0)

    def body(pos, s_ref, own_ref, o_ref):
        o_ref[...] = _sum_in_chip_order(pos[1], own_ref[...].astype(F32), s_ref)

    return pl.pallas_call(
        body,
        name=f"sum_slots_{name}",
        grid_spec=pltpu.PrefetchScalarGridSpec(
            num_scalar_prefetch=1,
            grid=(nb,),
            in_specs=[pl.BlockSpec((N_CHIPS, tr, cols), lambda i, pos: (0, i, 0)),
                      pl.BlockSpec((tr, cols), own_index)],
            out_specs=pl.BlockSpec((tr, cols), out_index),
        ),
        out_shape=jax.ShapeDtypeStruct(SHARD_SHAPES[name], F32),
        compiler_params=_params(("parallel",)),
    )(pos_arr, slots, own_half)


def _own_partial_tables():
    import numpy as np
    nb = SHARD // HALF_TILE
    grp, blk = np.zeros((N_CHIPS, nb), np.int32), np.zeros((N_CHIPS, nb), np.int32)
    for jj in range(N_CHIPS):
        for r, n, g, gr in _shard_runs(jj):
            for k in range(n // HALF_TILE):
                grp[jj, r // HALF_TILE + k] = GROUPS.index(g)
                blk[jj, r // HALF_TILE + k] = gr // HALF_TILE + k
    return grp, blk


def _sum_slots_win(slots, own_halves, pos_arr, grp_tbl, blk_tbl):
    nb = SHARD // HALF_TILE
    cols = D // 2

    def own_spec(gi):
        return pl.BlockSpec((HALF_TILE, cols), lambda b, pos, grp, blk: (jnp.where(grp[b] == gi, blk[b], 0), 0))

    def body(pos, grp, blk, s_ref, a_ref, b_ref, c_ref, d_ref, o_ref):
        g = grp[pl.program_id(0)]
        own = a_ref[...]
        for gi, ref in ((1, b_ref), (2, c_ref), (3, d_ref)):
            own = jnp.where(g == gi, ref[...], own)
        o_ref[...] = _sum_in_chip_order(pos[1], own.astype(F32), s_ref)

    return pl.pallas_call(
        body,
        name="sum_slots_win",
        grid_spec=pltpu.PrefetchScalarGridSpec(
            num_scalar_prefetch=3,
            grid=(nb,),
            in_specs=[pl.BlockSpec((N_CHIPS, HALF_TILE, cols), lambda b, pos, grp, blk: (0, b, 0))]
            + [own_spec(gi) for gi in range(len(GROUPS))],
            out_specs=pl.BlockSpec((HALF_TILE, cols), lambda b, pos, grp, blk: (b, pos[0])),
        ),
        out_shape=jax.ShapeDtypeStruct(SHARD_SHAPES["win"], F32),
        compiler_params=_params(("parallel",)),
    )(pos_arr, grp_tbl, blk_tbl, slots, *[own_halves[g] for g in GROUPS])


def _share_copy(buf, name, send_sems, recv_sems, k, which):
    x, y, c, _ = _position()
    half = _own_half(buf, SHARD_SHAPES[name], HALF_AXIS[name], which)
    return pltpu.make_async_remote_copy(src_ref=half, dst_ref=half, send_sem=send_sems.at[k], recv_sem=recv_sems.at[k],
                                        device_id=(x, y, 1 - c), device_id_type=MESH)


def _share_start(sums, after):
    names = tuple(sums)
    n = len(names)

    def body(*refs):
        _, _, c, _ = _position()
        for k, nm in enumerate(names):
            _share_copy(refs[k], nm, refs[n + 1], refs[n + 2], k, c).start()
        refs[-1][...] = jnp.zeros_like(refs[-1])

    outs = pl.pallas_call(
        body,
        name=f"share_start_{names[0]}",
        in_specs=[HBM] * n + [ANY],
        out_specs=[SEM, SEM] + [HBM] * n + [pl.BlockSpec(memory_space=pltpu.VMEM)],
        out_shape=[pltpu.SemaphoreType.DMA((n,)), pltpu.SemaphoreType.DMA((n,))]
        + [pltpu.HBM(SHARD_SHAPES[nm], F32) for nm in names] + [jax.ShapeDtypeStruct((8, LANE), F32)],
        input_output_aliases={k: 2 + k for k in range(n)},
        compiler_params=pltpu.CompilerParams(has_side_effects=EFFECT),
    )(*[_in_hbm(sums[nm]) for nm in names], after)
    return outs[0], outs[1], dict(zip(names, outs[2:2 + n])), outs[-1]


def _share_wait(send_sems, recv_sems, sums, after):
    names = tuple(sums)
    n = len(names)

    def body(*refs):
        _, _, c, _ = _position()
        for k, nm in enumerate(names):
            _share_copy(refs[k], nm, refs[n], refs[n + 1], k, 1 - c).wait_recv()
            _share_copy(refs[k], nm, refs[n], refs[n + 1], k, c).wait_send()

    outs = pl.pallas_call(
        body,
        name=f"share_wait_{names[0]}",
        in_specs=[HBM] * n + [SEM, SEM, ANY],
        out_specs=[HBM] * n,
        out_shape=[pltpu.HBM(SHARD_SHAPES[nm], F32) for nm in names],
        input_output_aliases={k: k for k in range(n)},
        compiler_params=pltpu.CompilerParams(has_side_effects=EFFECT),
    )(*[sums[nm] for nm in names], send_sems, recv_sems, after)
    return dict(zip(names, outs))


def _all_reduce_small(pack, name):
    rows = pack.shape[0]
    half = rows // 2

    def body(p_ref, o_ref, sib, land, sems):
        x, y, c, j = _position()
        sibling = (x, y, 1 - c)
        swap = pltpu.make_async_remote_copy(src_ref=p_ref, dst_ref=sib, send_sem=sems.at[0], recv_sem=sems.at[1],
                                            device_id=sibling, device_id_type=MESH)
        swap.start()
        swap.wait_recv()
        land[j] = p_ref[...] + sib[...]

        def mine(k, which):
            return land.at[k, pl.ds(which * half, half)]

        def ici(kk):
            return pltpu.make_async_remote_copy(
                src_ref=mine(j, c), dst_ref=mine(j, c), send_sem=sems.at[2 + kk], recv_sem=sems.at[6 + j],
                device_id=(kk // 2, kk % 2, c), device_id_type=MESH)

        def arrival(kk):
            return pltpu.make_async_remote_copy(
                src_ref=mine(kk, c), dst_ref=mine(kk, c), send_sem=sems.at[2 + kk], recv_sem=sems.at[6 + kk],
                device_id=(kk // 2, kk % 2, c), device_id_type=MESH)

        def passed_on(kk, which):
            return pltpu.make_async_remote_copy(
                src_ref=mine(kk, which), dst_ref=mine(kk, which), send_sem=sems.at[10 + kk],
                recv_sem=sems.at[14 + kk], device_id=sibling, device_id_type=MESH)

        for kk in range(N_CHIPS):
            @pl.when(j != kk)
            def _():
                ici(kk).start()
        for kk in range(N_CHIPS):
            @pl.when(j != kk)
            def _():
                arrival(kk).wait_recv()
                passed_on(kk, c).start()
        for kk in range(N_CHIPS):
            @pl.when(j != kk)
            def _():
                passed_on(kk, 1 - c).wait_recv()
        acc = land[0]
        for kk in range(1, N_CHIPS):
            acc = acc + land[kk]
        o_ref[...] = acc
        swap.wait_send()
        for kk in range(N_CHIPS):
            @pl.when(j != kk)
            def _():
                ici(kk).wait_send()
                passed_on(kk, c).wait_send()

    vmem = pl.BlockSpec(memory_space=pltpu.VMEM)
    return pl.pallas_call(
        body,
        name=name,
        in_specs=[vmem],
        out_specs=vmem,
        out_shape=jax.ShapeDtypeStruct((rows, LANE), F32),
        scratch_shapes=[pltpu.VMEM((rows, LANE), F32), pltpu.VMEM((N_CHIPS, rows, LANE), F32),
                        pltpu.SemaphoreType.DMA((18,))],
        compiler_params=pltpu.CompilerParams(has_side_effects=True, vmem_limit_bytes=VMEM_LIMIT),
    )(pack)


def _adamw(w, g, m, v, name, tr):
    rows, cols = w.shape
    tr = min(tr, rows)

    def body(w_ref, g_ref, m_ref, v_ref, go_ref, d_ref, nm_ref, nv_ref):
        gv = g_ref[...]
        go_ref[...] = gv
        nm = ADAM_B1 * m_ref[...] + (1.0 - ADAM_B1) * gv
        nv = ADAM_B2 * v_ref[...] + (1.0 - ADAM_B2) * (gv * gv)
        nm_ref[...] = nm
        nv_ref[...] = nv
        m_hat = nm / (1.0 - ADAM_B1 ** ADAM_STEP)
        v_hat = nv / (1.0 - ADAM_B2 ** ADAM_STEP)
        d_ref[...] = -ADAM_LR * (m_hat / (jnp.sqrt(v_hat) + ADAM_EPS) + ADAM_WD * w_ref[...])

    blk = pl.BlockSpec((tr, cols), lambda i: (i, 0))
    shape = jax.ShapeDtypeStruct((rows, cols), F32)
    return pl.pallas_call(
        body,
        name=f"adamw_{name}",
        grid=(rows // tr,),
        in_specs=[blk] * 4,
        out_specs=[blk] * 4,
        out_shape=[shape] * 4,
        compiler_params=_params(("parallel",)),
    )(w, g, m, v)


SMALL = (("pre_norm_g", (1, D)), ("post_norm_g", (1, D)), ("mem_norm_g", (1, D)), ("conv_w", (CONV_W, D_RNN)),
         ("conv_b", (1, D_RNN)), ("w_rg_a", (RNN_BLOCKS, LANE, LANE)), ("b_rg_a", (1, D_RNN)),
         ("w_rg_x", (RNN_BLOCKS, LANE, LANE)), ("b_rg_x", (1, D_RNN)), ("lru_lambda", (1, D_RNN)),
         ("swa_sinks", (1, SWA_HEADS)), ("rel_bias", (REL_BUCKETS, SWA_HEADS)))
PACK_ROWS = 2176


def _slot_len(shape):
    return -(-math.prod(shape) // LANE) * LANE


def _pack(values, last_row=None):
    parts = []
    for name, shape in SMALL:
        flat = values[name].reshape(-1).astype(F32)
        parts.append(jnp.pad(flat, (0, _slot_len(shape) - flat.shape[0])))
    flat = jnp.concatenate(parts)
    tail = jnp.zeros((LANE,), F32) if last_row is None else last_row
    return jnp.concatenate([jnp.pad(flat, (0, (PACK_ROWS - 1) * LANE - flat.shape[0])), tail]).reshape(PACK_ROWS, LANE)


def _unpack(pack, shapes=None):
    flat = pack.reshape(-1)
    out, off = {}, 0
    for name, shape in SMALL:
        shp = shape if shapes is None or name not in shapes else shapes[name]
        out[name] = flat[off:off + math.prod(shp)].reshape(shp)
        off += _slot_len(shape)
    return out


TWIN_WEIGHTS = ("pre_norm_g", "post_norm_g", "mem_norm_g", "w_in", "conv_w", "conv_b", "w_rg_a", "b_rg_a", "w_rg_x",
                "b_rg_x", "lru_lambda", "swa_sinks", "rel_bias", "w_mem_kv", "w_br_rg", "w_br_swa", "w_br_mem", "w_out")
BIG = {"w_in": "win", "w_mem_kv": "mk", "w_br_rg": "br0", "w_br_swa": "br1", "w_br_mem": "br2", "w_out": "out"}


def kernel(x, mem, pre_norm_g, post_norm_g, mem_norm_g, w_in, conv_w, conv_b, w_rg_a, b_rg_a, w_rg_x, b_rg_x, lru_lambda, swa_sinks, rel_bias, w_mem_kv, w_br_rg, w_br_swa, w_br_mem, w_out, loss_target, m_pre_norm_g, m_post_norm_g, m_mem_norm_g, m_w_in, m_conv_w, m_conv_b, m_w_rg_a, m_b_rg_a, m_w_rg_x, m_b_rg_x, m_lru_lambda, m_swa_sinks, m_rel_bias, m_w_mem_kv, m_w_br_rg, m_w_br_swa, m_w_br_mem, m_w_out, v_pre_norm_g, v_post_norm_g, v_mem_norm_g, v_w_in, v_conv_w, v_conv_b, v_w_rg_a, v_b_rg_a, v_w_rg_x, v_b_rg_x, v_lru_lambda, v_swa_sinks, v_rel_bias, v_w_mem_kv, v_w_br_rg, v_w_br_swa, v_w_br_mem, v_w_out):
    args = dict(locals())
    out_shapes = {n: args[n].shape for n in TWIN_WEIGHTS}
    w = {n: (args[n] if n == "rel_bias" else args[n][0]) for n in TWIN_WEIGHTS}
    m = {n: (args["m_" + n] if n == "rel_bias" else args["m_" + n][0]) for n in TWIN_WEIGHTS}
    v = {n: (args["v_" + n] if n == "rel_bias" else args["v_" + n][0]) for n in TWIN_WEIGHTS}
    for d in (w, m, v):
        for n, shape in SMALL:
            if n != "conv_w":
                d[n] = d[n].reshape(shape)

    xi, yi, ci = lax.axis_index("x"), lax.axis_index("y"), lax.axis_index("c")
    chip = 2 * xi + yi
    c_arr = ci.astype(jnp.int32).reshape(1)
    zero = jnp.zeros((), jnp.int32)
    cw0 = (chip * (D_RNN // N_CHIPS)).astype(jnp.int32)

    placed = lax.dynamic_update_slice(jnp.zeros((CONV_W, D_RNN), F32), w["conv_w"], (zero, cw0))
    placed = jnp.where(ci == 0, placed, 0.0).reshape(CONV_W * D_RNN // LANE, LANE)
    conv_w_full = _all_reduce_small(placed, "gather_conv_w").reshape(CONV_W, D_RNN)

    for d in (w, m, v):
        d["w_in"] = d["w_in"].T
    chip_row = lambda tbl: lax.dynamic_slice(jnp.asarray(tbl), (chip.astype(jnp.int32), zero), (1, tbl.shape[1]))[0]
    big_of = {s: n for n, s in BIG.items()}
    ag, token = {}, conv_w_full
    for stage in GATHER_STAGES:
        behind = c_arr if stage == GATHER_STAGES[0] else token
        placed = {n: (_place_group(w["w_in"], n, chip_row(_own_block_table(n)), behind) if n in GROUPS
                      else _place_shard(w[big_of[n]], n, behind)) for n in stage}
        send, recv, in_flight, token = _gather_start(placed, token)
        ag[stage] = (send, recv, in_flight)

    all_started = token

    passing = {}

    def prefetch(names, after):
        send, recv, in_flight = ag[names]
        *passing[names], token = _gather_pass_start(_gather_wait(send, recv, in_flight, after), after)
        return token

    def fetch(names, after):
        if names in passing:
            ready = _gather_pass_wait(*passing.pop(names), after)
        else:
            send, recv, in_flight = ag[names]
            after = all_started if names == GATHER_STAGES[0] else after
            ready = _gather_swap(_gather_wait(send, recv, in_flight, after))
        return tuple(ready[n] for n in names)

    rs = {"slots": {}, "halves": {}, "pending": [], "swap": None}

    def emit(grads, after=None):
        assert rs["swap"] is None
        *rs["swap"], token = _swap_start(grads, after)
        return token

    def advance(after):
        grads, received = _swap_wait(*rs["swap"], after)
        rs["swap"] = None
        halves = {n: _add_half(grads[n], received[n], c_arr, n) for n in grads}
        landing = {s: rs["slots"][s] if s in rs["slots"] else lax.empty(_slot_shape(s), BF16)
                   for s in _stage_shards(tuple(grads))}
        send, recv, halves, landing, token = _scatter_start(halves, landing)
        rs["slots"].update(landing)
        rs["pending"].append((send, recv, halves, tuple(landing)))
        return token

    sp = {n: w[n] for n, _ in SMALL}
    sp["conv_w"] = conv_w_full
    sq, grad_x, d_small = _local_step(x[0], mem[0], loss_target[0], sp, fetch, prefetch, emit, advance)
    small_total = _all_reduce_small(_pack(d_small, sq[0]), "all_reduce_small")
    loss = small_total[PACK_ROWS - 1, 0] * (0.5 / D)

    for send, recv, halves, touched in rs["pending"]:
        halves, landed = _scatter_wait(send, recv, halves, {s: rs["slots"][s] for s in touched}, small_total)
        rs["slots"].update(landed)
        rs["halves"].update(halves)
    pos_arr = jnp.stack([ci, chip]).astype(jnp.int32)
    grp_tbl, blk_tbl = (chip_row(t) for t in _own_partial_tables())
    rest = {s: _sum_slots(rs["slots"][s], rs["halves"][s], pos_arr, s) for s in SHARDS if s != "win"}
    *rest_share, tok = _share_start(rest, small_total)
    win_sum = _sum_slots_win(rs["slots"]["win"], rs["halves"], pos_arr, grp_tbl, blk_tbl)
    *win_share, tok = _share_start({"win": win_sum}, tok)
    sums = _share_wait(*rest_share, tok)

    g_small = _unpack(small_total)
    g_small["conv_w"] = lax.dynamic_slice(g_small["conv_w"], (zero, cw0), (CONV_W, D_RNN // N_CHIPS))

    grad, delta, new_m, new_v = {}, {}, {}, {}
    for n, s in BIG.items():
        if n == "w_in":
            continue
        grad[n], delta[n], new_m[n], new_v[n] = _adamw(w[n], sums[s], m[n], v[n], s, 128)
    g_win = _share_wait(*win_share, delta["w_out"])["win"]
    n = "w_in"
    grad[n], delta[n], new_m[n], new_v[n] = _adamw(w[n], g_win, m[n], v[n], "win", 224)
    for group in (grad, delta, new_m, new_v):
        group["w_in"] = group["w_in"].T
    _, d_, m_, v_ = _adamw(_pack(w), _pack(g_small), _pack(m), _pack(v), "small", PACK_ROWS)
    shard_shapes = {"conv_w": (CONV_W, D_RNN // N_CHIPS)}
    d_, m_, v_ = (_unpack(a, shard_shapes) for a in (d_, m_, v_))
    for n, _ in SMALL:
        grad[n], delta[n], new_m[n], new_v[n] = g_small[n], d_[n], m_[n], v_[n]

    outs = [loss, grad_x.reshape(1, S, D)]
    for group in (grad, delta, new_m, new_v):
        outs += [group[n].reshape(out_shapes[n]) for n in TWIN_WEIGHTS]
    return tuple(outs)
```

```python
import functools
import math
from typing import NamedTuple

import jax
import jax.numpy as jnp
from jax import lax
from jax.experimental import pallas as pl
from jax.experimental.pallas import tpu as pltpu

F32 = jnp.float32
BF16 = jnp.bfloat16
MESH = pl.DeviceIdType.MESH

S = 2048
D = 2048
MEM = 256
D_RNN = 1024
RNN_BLOCKS = 8
CONV_W = 4
LRU_C = 8.0
SWA_HEADS = 16
SWA_HD = 64
WINDOW = 128
MEM_HEADS = 4
MEM_HD = 256
REL_BUCKETS = 32
REL_MAX_DIST = 128
EPS = 1e-6
NEG_INF = -1e30
LANE = 128
SHARD = 3136
HALF_TILE = 64
N_CHIPS = 4
VMEM_LIMIT = 56 * 1024 * 1024

ADAM_LR = 0.001
ADAM_B1 = 0.9
ADAM_B2 = 0.999
ADAM_EPS = 1e-08
ADAM_WD = 0.01
ADAM_STEP = 10

GROUP_TILES = {"A": 16, "B": 18, "C": 16, "D": 48}
GROUPS = ("A", "B", "C", "D")


def _params(sem=None):
    return pltpu.CompilerParams(dimension_semantics=sem, vmem_limit_bytes=VMEM_LIMIT)


def _sigmoid(v):
    return jax.nn.sigmoid(v)


def _tile_home(t):
    if t < 16:
        return "A", t
    if t < 24:
        return "B", t - 16
    if t < 26:
        return "B", t - 24 + 16
    if t < 34:
        return "B", t - 26 + 8
    if t < 50:
        return "C", t - 34
    return "D", t - 50


def _shard_runs(j):
    runs = []
    per_shard = SHARD // HALF_TILE
    for q in range(per_shard * j, per_shard * (j + 1)):
        g, gt = _tile_home(q // 2)
        row = gt * LANE + (q % 2) * HALF_TILE
        if runs and runs[-1][2] == g and runs[-1][3] + runs[-1][1] == row:
            runs[-1][1] += HALF_TILE
        else:
            runs.append([(q - per_shard * j) * HALF_TILE, HALF_TILE, g, row])
    return [tuple(r) for r in runs]


_DIMS = {
    "nn": (((1,), (0,)), ((), ())),
    "nt": (((1,), (1,)), ((), ())),
    "tn": (((0,), (0,)), ((), ())),
}


def _mm(a, b, mode, out_dtype, tm, tn, tk, name, acc=None, after=None):
    if mode == "nn":
        (m, k), n = a.shape, b.shape[1]
    elif mode == "nt":
        (m, k), n = a.shape, b.shape[0]
    else:
        (k, m), n = a.shape, b.shape[1]
    tm, tn, tk = min(tm, m), min(tn, n), min(tk, k)
    assert m % tm == 0 and n % tn == 0 and k % tk == 0, (name, m, n, k)
    nk = k // tk
    has_acc = acc is not None

    def body(*refs):
        a_ref, b_ref = refs[0], refs[1]
        o_ref = refs[3] if has_acc else refs[2]
        p = lax.dot_general(a_ref[...], b_ref[...], _DIMS[mode], preferred_element_type=F32)

        def finish(v):
            if has_acc:
                v = v + refs[2][...]
            o_ref[...] = v.astype(out_dtype)

        if nk == 1:
            finish(p)
        else:
            s_ref = refs[-1]
            kk = pl.program_id(2)

            @pl.when(kk == 0)
            def _():
                s_ref[...] = p

            @pl.when(kk > 0)
            def _():
                s_ref[...] += p

            @pl.when(kk == nk - 1)
            def _():
                finish(s_ref[...])

    if mode == "nn":
        a_spec = pl.BlockSpec((tm, tk), lambda i, j, kk: (i, kk))
        b_spec = pl.BlockSpec((tk, tn), lambda i, j, kk: (kk, j))
    elif mode == "nt":
        a_spec = pl.BlockSpec((tm, tk), lambda i, j, kk: (i, kk))
        b_spec = pl.BlockSpec((tn, tk), lambda i, j, kk: (j, kk))
    else:
        a_spec = pl.BlockSpec((tk, tm), lambda i, j, kk: (kk, i))
        b_spec = pl.BlockSpec((tk, tn), lambda i, j, kk: (kk, j))
    o_spec = pl.BlockSpec((tm, tn), lambda i, j, kk: (i, j))
    in_specs = [a_spec, b_spec] + ([o_spec] if has_acc else [])
    args = (a, b) + ((acc,) if has_acc else ())
    if after is not None:
        in_specs.append(pl.BlockSpec(memory_space=pl.ANY))
        args += (after,)
    n_in = len(args)
    kernel_body = body

    def body(*refs):
        kernel_body(*(refs[:n_in - (after is not None)] + refs[n_in:]))

    return pl.pallas_call(
        body,
        name=name,
        grid=(m // tm, n // tn, nk),
        in_specs=in_specs,
        out_specs=o_spec,
        out_shape=jax.ShapeDtypeStruct((m, n), out_dtype),
        scratch_shapes=[pltpu.VMEM((tm, tn), F32)] if nk > 1 else [],
        compiler_params=_params(("parallel", "parallel", "arbitrary")),
    )(*args)


def _rms_fwd(x, g, name, ts=256):
    r, d = x.shape

    def body(x_ref, g_ref, o_ref):
        xv = x_ref[...]
        inv = lax.rsqrt(jnp.mean(xv * xv, axis=-1, keepdims=True) + EPS)
        o_ref[...] = (xv * inv * g_ref[...]).astype(BF16)

    return pl.pallas_call(
        body,
        name=name,
        grid=(r // ts,),
        in_specs=[pl.BlockSpec((ts, d), lambda i: (i, 0)), pl.BlockSpec((1, d), lambda i: (0, 0))],
        out_specs=pl.BlockSpec((ts, d), lambda i: (i, 0)),
        out_shape=jax.ShapeDtypeStruct((r, d), BF16),
        compiler_params=_params(("parallel",)),
    )(x, g)


def _post_loss(out, x, tgt, g_post, ts=256):
    n = S // ts

    def body(o_ref, x_ref, t_ref, g_ref, sq_ref, dy_ref, do_ref, dg_ref):
        i = pl.program_id(0)

        @pl.when(i == 0)
        def _():
            sq_ref[...] = jnp.zeros_like(sq_ref)
            dg_ref[...] = jnp.zeros_like(dg_ref)

        ov = o_ref[...]
        g = g_ref[...]
        inv = lax.rsqrt(jnp.mean(ov * ov, axis=-1, keepdims=True) + EPS)
        on = ov * inv
        err = x_ref[...] + on * g - t_ref[...]
        sq_ref[...] += jnp.sum(err * err)
        dy = err * (1.0 / D)
        dy_ref[...] = dy
        dg_ref[...] += jnp.sum(dy * on, axis=0, keepdims=True)
        don = dy * g
        do_ref[...] = (inv * (don - on * jnp.mean(don * on, axis=-1, keepdims=True))).astype(BF16)

    row = pl.BlockSpec((ts, D), lambda i: (i, 0))
    vec = pl.BlockSpec((1, D), lambda i: (0, 0))
    return pl.pallas_call(
        body,
        name="post_loss",
        grid=(n,),
        in_specs=[row, row, row, vec],
        out_specs=[pl.BlockSpec((8, LANE), lambda i: (0, 0)), row, row, vec],
        out_shape=[
            jax.ShapeDtypeStruct((8, LANE), F32),
            jax.ShapeDtypeStruct((S, D), F32),
            jax.ShapeDtypeStruct((S, D), BF16),
            jax.ShapeDtypeStruct((1, D), F32),
        ],
        compiler_params=_params(("arbitrary",)),
    )(out, x, tgt, g_post)


def _pre_bwd(dh, x, dy, g_pre, ts=256):
    n = S // ts

    def body(dh_ref, x_ref, dy_ref, g_ref, gx_ref, dg_ref):
        i = pl.program_id(0)

        @pl.when(i == 0)
        def _():
            dg_ref[...] = jnp.zeros_like(dg_ref)

        xv = x_ref[...]
        dhv = dh_ref[...]
        inv = lax.rsqrt(jnp.mean(xv * xv, axis=-1, keepdims=True) + EPS)
        xn = xv * inv
        dg_ref[...] += jnp.sum(dhv * xn, axis=0, keepdims=True)
        dxn = dhv * g_ref[...]
        gx_ref[...] = dy_ref[...] + inv * (dxn - xn * jnp.mean(dxn * xn, axis=-1, keepdims=True))

    row = pl.BlockSpec((ts, D), lambda i: (i, 0))
    vec = pl.BlockSpec((1, D), lambda i: (0, 0))
    return pl.pallas_call(
        body,
        name="pre_bwd",
        grid=(n,),
        in_specs=[row, row, row, vec],
        out_specs=[row, vec],
        out_shape=[jax.ShapeDtypeStruct((S, D), F32), jax.ShapeDtypeStruct((1, D), F32)],
        compiler_params=_params(("arbitrary",)),
    )(dh, x, dy, g_pre)


def _memnorm_bwd(dmemn, mem):
    def body(d_ref, m_ref, dg_ref):
        mv = m_ref[...]
        inv = lax.rsqrt(jnp.mean(mv * mv, axis=-1, keepdims=True) + EPS)
        dg_ref[...] = jnp.sum(d_ref[...] * mv * inv, axis=0, keepdims=True)

    return pl.pallas_call(
        body,
        name="memnorm_bwd",
        out_shape=jax.ShapeDtypeStruct((1, D), F32),
        compiler_params=_params(),
    )(dmemn, mem)


T_RNN = 256


def _neg_expm1(z):
    poly = -z * (1.0 + z * (0.5 + z * (1.0 / 6 + z * (1.0 / 24 + z * (1.0 / 120 + z * (1.0 / 720))))))
    return jnp.where(z > -0.1, poly, 1.0 - jnp.exp(z))


def _softplus_neg(lam):
    return jnp.maximum(-lam, 0.0) + jnp.log1p(jnp.exp(-jnp.abs(lam)))


def _rnn_gates(conv, wa_ref, ba, wx_ref, bx, lam, first_row):
    cbf = conv.astype(BF16)
    ga, gx = [], []
    for n in range(RNN_BLOCKS):
        c_n = cbf[:, n * LANE:(n + 1) * LANE]
        ga.append(jnp.dot(c_n, wa_ref[n], preferred_element_type=F32))
        gx.append(jnp.dot(c_n, wx_ref[n], preferred_element_type=F32))
    gate_r = _sigmoid(jnp.concatenate(ga, axis=1) + ba)
    gate_i = _sigmoid(jnp.concatenate(gx, axis=1) + bx)
    sp = _softplus_neg(lam)
    log_a = -LRU_C * gate_r * sp
    a = jnp.exp(log_a)
    mult_raw = jnp.sqrt(_neg_expm1(2.0 * log_a))
    mult = jnp.where(first_row, 1.0, mult_raw)
    return cbf, gate_r, gate_i, sp, a, mult_raw, mult


def _rglru_fwd(p_a, conv_w, conv_b, wa, ba, wx, bx, lam):
    t = T_RNN
    n = S // t

    def body(xr_ref, g_ref, cw_ref, cb_ref, wa_ref, ba_ref, wx_ref, bx_ref, lam_ref,
             y_ref, h_ref, xp_s, hcar, a_s, b_s):
        i = pl.program_id(0)

        @pl.when(i == 0)
        def _():
            xp_s[0:8, :] = jnp.zeros((8, D_RNN), F32)
            hcar[...] = jnp.zeros_like(hcar)

        @pl.when(i > 0)
        def _():
            xp_s[0:8, :] = xp_s[t:t + 8, :]

        xp_s[8:8 + t, :] = xr_ref[...].astype(F32)
        conv = cb_ref[...]
        for k in range(CONV_W):
            conv = conv + cw_ref[k:k + 1, :] * xp_s[8 - k:8 - k + t, :]
        rows = i * t + lax.broadcasted_iota(jnp.int32, (t, 1), 0)
        _, _, gate_i, _, a, _, mult = _rnn_gates(
            conv, wa_ref, ba_ref[...], wx_ref, bx_ref[...], lam_ref[...], rows == 0)
        a_s[...] = a
        b_s[...] = mult * gate_i * conv

        def step(tt, h):
            h = a_s[pl.ds(tt, 1), :] * h + b_s[pl.ds(tt, 1), :]
            h_ref[pl.ds(tt, 1), :] = h
            return h

        hcar[...] = lax.fori_loop(0, t, step, hcar[...], unroll=8)
        g = g_ref[...].astype(F32)
        y_ref[...] = (h_ref[...] * (g * _sigmoid(g))).astype(BF16)

    blk = lambda c: pl.BlockSpec((t, D_RNN), lambda i: (i, c))
    full = lambda shape: pl.BlockSpec(shape, lambda i: (0,) * len(shape))
    return pl.pallas_call(
        body,
        name="rglru_fwd",
        grid=(n,),
        in_specs=[blk(0), blk(1), full((CONV_W, D_RNN)), full((1, D_RNN)),
                  full((RNN_BLOCKS, LANE, LANE)), full((1, D_RNN)),
                  full((RNN_BLOCKS, LANE, LANE)), full((1, D_RNN)), full((1, D_RNN))],
        out_specs=[blk(0), blk(0)],
        out_shape=[jax.ShapeDtypeStruct((S, D_RNN), BF16), jax.ShapeDtypeStruct((S, D_RNN), F32)],
        scratch_shapes=[pltpu.VMEM((t + 8, D_RNN), F32), pltpu.VMEM((1, D_RNN), F32),
                        pltpu.VMEM((t, D_RNN), F32), pltpu.VMEM((t, D_RNN), F32)],
        compiler_params=_params(("arbitrary",)),
    )(p_a, p_a, conv_w, conv_b, wa, ba, wx, bx, lam)


def _rglru_bwd(dy, p_a, hseq, conv_w, conv_b, wa, ba, wx, bx, lam):
    t = T_RNN
    n = S // t
    rb = t // 8

    def body(dy_ref, xr_ref, g_ref, h_ref, xrp_ref, hp_ref, cw_ref, cb_ref, wa_ref, ba_ref, wx_ref, bx_ref, lam_ref,
             dp_ref, dcw_ref, dcb_ref, dwa_ref, dba_ref, dwx_ref, dbx_ref, dlam_ref,
             xp_s, hp_s, dxp_s, lamcar, a_s, dh_s, lam_s):
        i = pl.program_id(0)
        r = n - 1 - i

        @pl.when(i == 0)
        def _():
            for ref in (dcw_ref, dcb_ref, dwa_ref, dba_ref, dwx_ref, dbx_ref, dlam_ref, lamcar):
                ref[...] = jnp.zeros_like(ref)
            dxp_s[t:t + 8, :] = jnp.zeros((8, D_RNN), F32)

        @pl.when(i > 0)
        def _():
            dxp_s[t:t + 8, :] = dxp_s[0:8, :]

        has_prev = r > 0
        xp_s[0:8, :] = jnp.where(has_prev, xrp_ref[8:16, :].astype(F32), 0.0)
        xp_s[8:8 + t, :] = xr_ref[...].astype(F32)
        hp_s[0:8, :] = jnp.where(has_prev, hp_ref[...], 0.0)
        hp_s[8:8 + t, :] = h_ref[...]
        xs = [xp_s[8 - k:8 - k + t, :] for k in range(CONV_W)]
        conv = cb_ref[...]
        for k in range(CONV_W):
            conv = conv + cw_ref[k:k + 1, :] * xs[k]
        rows = r * t + lax.broadcasted_iota(jnp.int32, (t, 1), 0)
        first = rows == 0
        lam_p = lam_ref[...]
        cbf, gate_r, gate_i, sp, a, mult_raw, mult = _rnn_gates(
            conv, wa_ref, ba_ref[...], wx_ref, bx_ref[...], lam_p, first)

        g = g_ref[...].astype(F32)
        sg = _sigmoid(g)
        dyv = dy_ref[...]
        a_s[...] = a
        dh_s[...] = dyv * (g * sg)
        dg = dyv * h_ref[...] * (sg * (1.0 + g * (1.0 - sg)))

        def step(jj, car):
            tt = t - 1 - jj
            lm = dh_s[pl.ds(tt, 1), :] + car
            lam_s[pl.ds(tt, 1), :] = lm
            return a_s[pl.ds(tt, 1), :] * lm

        lamcar[...] = lax.fori_loop(0, t, step, lamcar[...], unroll=8)
        db = lam_s[...]
        da = db * hp_s[7:7 + t, :]
        dmult = db * gate_i * conv
        dgate_i = db * mult * conv
        dconv = db * mult * gate_i
        dlog_a = da * a + jnp.where(first, 0.0, dmult * (-(a * a) / mult_raw))
        dgate_r = dlog_a * (-LRU_C * sp)
        dsp = jnp.sum(dlog_a * (-LRU_C * gate_r), axis=0, keepdims=True)
        dlam_ref[...] += dsp * (-_sigmoid(-lam_p))
        dga = dgate_r * gate_r * (1.0 - gate_r)
        dgx = dgate_i * gate_i * (1.0 - gate_i)
        dba_ref[...] += jnp.sum(dga, axis=0, keepdims=True)
        dbx_ref[...] += jnp.sum(dgx, axis=0, keepdims=True)
        dga16, dgx16 = dga.astype(BF16), dgx.astype(BF16)
        back = []
        for nb in range(RNN_BLOCKS):
            sl = slice(nb * LANE, (nb + 1) * LANE)
            dwa_ref[nb] += lax.dot_general(cbf[:, sl], dga16[:, sl], _DIMS["tn"], preferred_element_type=F32)
            dwx_ref[nb] += lax.dot_general(cbf[:, sl], dgx16[:, sl], _DIMS["tn"], preferred_element_type=F32)
            back.append(lax.dot_general(dga16[:, sl], wa_ref[nb], _DIMS["nt"], preferred_element_type=F32)
                        + lax.dot_general(dgx16[:, sl], wx_ref[nb], _DIMS["nt"], preferred_element_type=F32))
        dconv = dconv + jnp.concatenate(back, axis=1)
        dcb_ref[...] += jnp.sum(dconv, axis=0, keepdims=True)
        for k in range(CONV_W):
            dcw_ref[k:k + 1, :] += jnp.sum(dconv * xs[k], axis=0, keepdims=True)
        dxp_s[0:t, :] = dconv
        dxr = cw_ref[0:1, :] * dconv
        for k in range(1, CONV_W):
            dxr = dxr + cw_ref[k:k + 1, :] * dxp_s[k:k + t, :]
        dp_ref[:, 0:D_RNN] = dxr.astype(BF16)
        dp_ref[:, D_RNN:2 * D_RNN] = dg.astype(BF16)

    blk = lambda c: pl.BlockSpec((t, D_RNN), lambda i: (n - 1 - i, c))
    prev8 = pl.BlockSpec((8, D_RNN), lambda i: (jnp.maximum((n - 1 - i) * rb - 1, 0), 0))
    prev16 = pl.BlockSpec((16, D_RNN), lambda i: (jnp.maximum((n - 1 - i) * (rb // 2) - 1, 0), 0))
    full = lambda shape: pl.BlockSpec(shape, lambda i: (0,) * len(shape))
    vec = full((1, D_RNN))
    mat = full((RNN_BLOCKS, LANE, LANE))
    return pl.pallas_call(
        body,
        name="rglru_bwd",
        grid=(n,),
        in_specs=[blk(0), blk(0), blk(1), blk(0), prev16, prev8,
                  full((CONV_W, D_RNN)), vec, mat, vec, mat, vec, vec],
        out_specs=[pl.BlockSpec((t, 2 * D_RNN), lambda i: (n - 1 - i, 0)),
                   full((CONV_W, D_RNN)), vec, mat, vec, mat, vec, vec],
        out_shape=[jax.ShapeDtypeStruct((S, 2 * D_RNN), BF16),
                   jax.ShapeDtypeStruct((CONV_W, D_RNN), F32), jax.ShapeDtypeStruct((1, D_RNN), F32),
                   jax.ShapeDtypeStruct((RNN_BLOCKS, LANE, LANE), F32), jax.ShapeDtypeStruct((1, D_RNN), F32),
                   jax.ShapeDtypeStruct((RNN_BLOCKS, LANE, LANE), F32), jax.ShapeDtypeStruct((1, D_RNN), F32),
                   jax.ShapeDtypeStruct((1, D_RNN), F32)],
        scratch_shapes=[pltpu.VMEM((t + 8, D_RNN), F32), pltpu.VMEM((t + 8, D_RNN), F32),
                        pltpu.VMEM((t + 8, D_RNN), F32), pltpu.VMEM((1, D_RNN), F32),
                        pltpu.VMEM((t, D_RNN), F32), pltpu.VMEM((t, D_RNN), F32), pltpu.VMEM((t, D_RNN), F32)],
        compiler_params=_params(("arbitrary",)),
    )(dy, p_a, p_a, hseq, p_a, hseq, conv_w, conv_b, wa, ba, wx, bx, lam)


QB = WINDOW
KB2 = 2 * WINDOW
N_QB = S // QB
N_PAIR = SWA_HEADS // 2


def _swa_keys(kvc_ref, kvp_ref):
    kk = jnp.concatenate([kvp_ref[:, 0:LANE], kvc_ref[:, 0:LANE]], axis=0).astype(F32)
    vv = jnp.concatenate([kvp_ref[:, LANE:2 * LANE], kvc_ref[:, LANE:2 * LANE]], axis=0).astype(F32)
    lo = lax.broadcasted_iota(jnp.int32, (1, LANE), 1) < SWA_HD
    kk_sw, vv_sw = pltpu.roll(kk, SWA_HD, 1), pltpu.roll(vv, SWA_HD, 1)
    kd = [jnp.where(lo, kk, kk_sw).astype(BF16), jnp.where(lo, kk_sw, kk).astype(BF16)]
    vd = [jnp.where(lo, vv, vv_sw).astype(BF16), jnp.where(lo, vv_sw, vv).astype(BF16)]
    return lo, kd, vd


def _swa_valid(n):
    qi = lax.broadcasted_iota(jnp.int32, (QB, KB2), 0)
    kj = lax.broadcasted_iota(jnp.int32, (QB, KB2), 1)
    dist = qi + WINDOW - kj
    return (dist >= 0) & (dist < WINDOW) & ((n > 0) | (kj >= WINDOW))


def _swa_probs(qh16, kd, bias, sink, valid):
    lg = lax.dot_general(qh16, kd, _DIMS["nt"], preferred_element_type=F32) * (SWA_HD ** -0.5) + bias
    lg = jnp.where(valid, lg, NEG_INF)
    m = jnp.maximum(jnp.max(lg, axis=-1, keepdims=True), sink)
    p = jnp.exp(lg - m)
    es = jnp.exp(sink - m)
    den = jnp.sum(p, axis=-1, keepdims=True) + es
    return p / den, es / den


def _swa_specs():
    q = pl.BlockSpec((QB, D_RNN), lambda n: (n, 0))
    g = pl.BlockSpec((QB, D_RNN), lambda n: (n, 1))
    kvc = pl.BlockSpec((QB, 2 * LANE), lambda n: (n, 8))
    kvp = pl.BlockSpec((QB, 2 * LANE), lambda n: (jnp.maximum(n - 1, 0), 8))
    bias = pl.BlockSpec((SWA_HEADS, QB, KB2), lambda n: (0, 0, 0))
    sinks = pl.BlockSpec(memory_space=pltpu.SMEM)
    return q, g, kvc, kvp, bias, sinks


def _swa_fwd(p_b, bias_t, sinks):
    def body(q_ref, g_ref, kvc_ref, kvp_ref, bias_ref, sink_ref, y_ref, o_ref):
        n = pl.program_id(0)
        lo, kd, vd = _swa_keys(kvc_ref, kvp_ref)
        valid = _swa_valid(n)
        for hp in range(N_PAIR):
            sl = slice(hp * LANE, (hp + 1) * LANE)
            kvh = hp // (N_PAIR // 2)
            q = q_ref[:, sl].astype(F32)
            outs = []
            for j in range(2):
                mh = lo if j == 0 else jnp.logical_not(lo)
                qh16 = jnp.where(mh, q, 0.0).astype(BF16)
                probs, _ = _swa_probs(qh16, kd[kvh], bias_ref[2 * hp + j], sink_ref[2 * hp + j], valid)
                outs.append(jnp.dot(probs.astype(BF16), vd[kvh], preferred_element_type=F32))
            o = jnp.where(lo, outs[0], outs[1])
            o_ref[:, sl] = o
            g = g_ref[:, sl].astype(F32)
            y_ref[:, sl] = (o * (g * _sigmoid(g))).astype(BF16)

    q, g, kvc, kvp, bias, sinks_spec = _swa_specs()
    out = pl.BlockSpec((QB, D_RNN), lambda n: (n, 0))
    return pl.pallas_call(
        body,
        name="swa_fwd",
        grid=(N_QB,),
        in_specs=[q, g, kvc, kvp, bias, sinks_spec],
        out_specs=[out, out],
        out_shape=[jax.ShapeDtypeStruct((S, D_RNN), BF16), jax.ShapeDtypeStruct((S, D_RNN), F32)],
        compiler_params=_params(("parallel",)),
    )(p_b, p_b, p_b, p_b, bias_t, sinks)


def _swa_bwd(dy, p_b, o_swa, bias_t, sinks, after=None):
    def body(dy_ref, q_ref, g_ref, kvc_ref, kvp_ref, o_ref, bias_ref, sink_ref, *rest):
        dp_ref, dk_ref, dv_ref, dbias_ref, dsink_ref = rest[-5:]
        n = pl.program_id(0)

        @pl.when(n == 0)
        def _():
            for ref in (dk_ref, dv_ref, dbias_ref, dsink_ref):
                ref[...] = jnp.zeros_like(ref)

        lo, kd, vd = _swa_keys(kvc_ref, kvp_ref)
        hi = jnp.logical_not(lo)
        valid = _swa_valid(n)
        dk_blk = jnp.zeros((KB2, LANE), F32)
        dv_blk = jnp.zeros((KB2, LANE), F32)
        for kvh in range(2):
            dk_pair = jnp.zeros((KB2, LANE), F32)
            dv_pair = jnp.zeros((KB2, LANE), F32)
            for hp in range(kvh * (N_PAIR // 2), (kvh + 1) * (N_PAIR // 2)):
                sl = slice(hp * LANE, (hp + 1) * LANE)
                q = q_ref[:, sl].astype(F32)
                g = g_ref[:, sl].astype(F32)
                o = o_ref[:, sl]
                dyv = dy_ref[:, sl]
                sg = _sigmoid(g)
                do = dyv * (g * sg)
                dp_ref[:, D_RNN + hp * LANE:D_RNN + (hp + 1) * LANE] = (
                    dyv * o * (sg * (1.0 + g * (1.0 - sg)))).astype(BF16)
                dqs = []
                for j in range(2):
                    h = 2 * hp + j
                    mh = lo if j == 0 else hi
                    qh16 = jnp.where(mh, q, 0.0).astype(BF16)
                    sink = sink_ref[h]
                    probs, psink = _swa_probs(qh16, kd[kvh], bias_ref[h], sink, valid)
                    doh = jnp.where(mh, do, 0.0)
                    doh16 = doh.astype(BF16)
                    delta = jnp.sum(doh * o, axis=-1, keepdims=True)
                    dpr = lax.dot_general(doh16, vd[kvh], _DIMS["nt"], preferred_element_type=F32)
                    ds = probs * (dpr - delta)
                    dbias_ref[h] += ds
                    dsink_ref[h:h + 1, :] += jnp.zeros((1, LANE), F32) - jnp.sum(psink * delta)
                    ds16 = (ds * (SWA_HD ** -0.5)).astype(BF16)
                    dqs.append(jnp.dot(ds16, kd[kvh], preferred_element_type=F32))
                    dk_pair = dk_pair + lax.dot_general(ds16, qh16, _DIMS["tn"], preferred_element_type=F32)
                    dv_pair = dv_pair + lax.dot_general(probs.astype(BF16), doh16, _DIMS["tn"],
                                                        preferred_element_type=F32)
                dp_ref[:, sl] = jnp.where(lo, dqs[0], dqs[1]).astype(BF16)
            keep = lo if kvh == 0 else hi
            dk_blk = dk_blk + jnp.where(keep, dk_pair + pltpu.roll(dk_pair, SWA_HD, 1), 0.0)
            dv_blk = dv_blk + jnp.where(keep, dv_pair + pltpu.roll(dv_pair, SWA_HD, 1), 0.0)

        cur = pl.ds(pl.multiple_of(n * QB, QB), QB)
        dk_ref[cur, :] += dk_blk[QB:KB2]
        dv_ref[cur, :] += dv_blk[QB:KB2]

        @pl.when(n > 0)
        def _():
            prev = pl.ds(pl.multiple_of((n - 1) * QB, QB), QB)
            dk_ref[prev, :] += dk_blk[0:QB]
            dv_ref[prev, :] += dv_blk[0:QB]

    q, g, kvc, kvp, bias, sinks_spec = _swa_specs()
    row = pl.BlockSpec((QB, D_RNN), lambda n: (n, 0))
    acc = pl.BlockSpec((S, LANE), lambda n: (0, 0))
    return pl.pallas_call(
        body,
        name="swa_bwd",
        grid=(N_QB,),
        in_specs=[row, q, g, kvc, kvp, row, bias, sinks_spec] + ([ANY] if after is not None else []),
        out_specs=[pl.BlockSpec((QB, 2 * D_RNN), lambda n: (n, 0)), acc, acc, bias,
                   pl.BlockSpec((SWA_HEADS, LANE), lambda n: (0, 0))],
        out_shape=[jax.ShapeDtypeStruct((S, GROUP_TILES["B"] * LANE), BF16),
                   jax.ShapeDtypeStruct((S, LANE), F32), jax.ShapeDtypeStruct((S, LANE), F32),
                   jax.ShapeDtypeStruct((SWA_HEADS, QB, KB2), F32),
                   jax.ShapeDtypeStruct((SWA_HEADS, LANE), F32)],
        compiler_params=_params(("arbitrary",)),
    )(dy, p_b, p_b, p_b, p_b, o_swa, bias_t, sinks, *([after] if after is not None else []))


def _swa_pack(dp_b, dk, dv, ts=512):
    def body(_, dk_ref, dv_ref, o_ref):
        o_ref[:, 0:LANE] = dk_ref[...].astype(BF16)
        o_ref[:, LANE:2 * LANE] = dv_ref[...].astype(BF16)

    tile = pl.BlockSpec((ts, LANE), lambda i: (i, 0))
    return pl.pallas_call(
        body,
        name="swa_pack",
        grid=(S // ts,),
        in_specs=[pl.BlockSpec(memory_space=pl.ANY), tile, tile],
        out_specs=pl.BlockSpec((ts, 2 * LANE), lambda i: (i, 8)),
        out_shape=jax.ShapeDtypeStruct(dp_b.shape, dp_b.dtype),
        input_output_aliases={0: 0},
        compiler_params=_params(("parallel",)),
    )(dp_b, dk, dv)


def _split3(v):
    a = v.astype(BF16)
    r = v - a.astype(F32)
    b = r.astype(BF16)
    c = (r - b.astype(F32)).astype(BF16)
    return a, b, c


def _relbias_grad(dbias_flat, onehot_t):
    def body(d_ref, e_ref, o_ref):
        e = e_ref[...]
        acc = jnp.zeros((SWA_HEADS, REL_BUCKETS), F32)
        for term in _split3(d_ref[...]):
            acc = acc + lax.dot_general(term, e, _DIMS["nt"], preferred_element_type=F32)
        o_ref[...] = acc

    return pl.pallas_call(
        body,
        name="relbias_grad",
        out_shape=jax.ShapeDtypeStruct((SWA_HEADS, REL_BUCKETS), F32),
        compiler_params=_params(),
    )(dbias_flat, onehot_t)


TS_MEM = 512


def _mem_probs(q16, mk):
    lg = lax.dot_general(q16, mk, _DIMS["nt"], preferred_element_type=F32) * (MEM_HD ** -0.5)
    p = jnp.exp(lg - jnp.max(lg, axis=-1, keepdims=True))
    return p / jnp.sum(p, axis=-1, keepdims=True)


def _mem_fwd(p_c, mkv):
    def body(q_ref, g_ref, mkv_ref, y_ref, o_ref):
        for hm in range(MEM_HEADS):
            sl = slice(hm * MEM_HD, (hm + 1) * MEM_HD)
            probs = _mem_probs(q_ref[:, sl].astype(BF16), mkv_ref[:, sl])
            o = jnp.dot(probs.astype(BF16), mkv_ref[:, D_RNN + hm * MEM_HD:D_RNN + (hm + 1) * MEM_HD],
                        preferred_element_type=F32)
            o_ref[:, sl] = o
            g = g_ref[:, sl].astype(F32)
            y_ref[:, sl] = (o * (g * _sigmoid(g))).astype(BF16)

    blk = lambda c: pl.BlockSpec((TS_MEM, D_RNN), lambda i: (i, c))
    return pl.pallas_call(
        body,
        name="mem_fwd",
        grid=(S // TS_MEM,),
        in_specs=[blk(0), blk(1), pl.BlockSpec((MEM, 2 * D_RNN), lambda i: (0, 0))],
        out_specs=[blk(0), blk(0)],
        out_shape=[jax.ShapeDtypeStruct((S, D_RNN), BF16), jax.ShapeDtypeStruct((S, D_RNN), F32)],
        compiler_params=_params(("parallel",)),
    )(p_c, p_c, mkv)


def _mem_bwd(dy, p_c, o_mem, mkv):
    def body(dy_ref, q_ref, g_ref, o_ref, mkv_ref, dp_ref, dmkv_ref):
        @pl.when(pl.program_id(0) == 0)
        def _():
            dmkv_ref[...] = jnp.zeros_like(dmkv_ref)

        for hm in range(MEM_HEADS):
            sl = slice(hm * MEM_HD, (hm + 1) * MEM_HD)
            sv = slice(D_RNN + hm * MEM_HD, D_RNN + (hm + 1) * MEM_HD)
            q16 = q_ref[:, sl].astype(BF16)
            mk, mv = mkv_ref[:, sl], mkv_ref[:, sv]
            probs = _mem_probs(q16, mk)
            g, o, dyv = g_ref[:, sl].astype(F32), o_ref[:, sl], dy_ref[:, sl]
            sg = _sigmoid(g)
            do = dyv * (g * sg)
            dp_ref[:, sv] = (dyv * o * (sg * (1.0 + g * (1.0 - sg)))).astype(BF16)
            do16 = do.astype(BF16)
            delta = jnp.sum(do * o, axis=-1, keepdims=True)
            dpr = lax.dot_general(do16, mv, _DIMS["nt"], preferred_element_type=F32)
            ds16 = (probs * (dpr - delta) * (MEM_HD ** -0.5)).astype(BF16)
            dp_ref[:, sl] = jnp.dot(ds16, mk, preferred_element_type=F32).astype(BF16)
            dmkv_ref[:, sl] += lax.dot_general(ds16, q16, _DIMS["tn"], preferred_element_type=F32)
            dmkv_ref[:, sv] += lax.dot_general(probs.astype(BF16), do16, _DIMS["tn"], preferred_element_type=F32)

    blk = lambda c: pl.BlockSpec((TS_MEM, D_RNN), lambda i: (i, c))
    kv = pl.BlockSpec((MEM, 2 * D_RNN), lambda i: (0, 0))
    return pl.pallas_call(
        body,
        name="mem_bwd",
        grid=(S // TS_MEM,),
        in_specs=[blk(0), blk(0), blk(1), blk(0), kv],
        out_specs=[pl.BlockSpec((TS_MEM, 2 * D_RNN), lambda i: (i, 0)), kv],
        out_shape=[jax.ShapeDtypeStruct((S, 2 * D_RNN), BF16), jax.ShapeDtypeStruct((MEM, 2 * D_RNN), F32)],
        compiler_params=_params(("arbitrary",)),
    )(dy, p_c, p_c, o_mem, mkv)


TS_MRG = 512
TD_MRG = 512
N_DBLK = D // TD_MRG


def _merge_fwd(z, p_d):
    def body(z0, z1, z2, g0, g1, g2, o_ref):
        gate = lambda g: _sigmoid(g[...].astype(F32))
        o_ref[...] = (gate(g0) * z0[...] + gate(g1) * z1[...] + gate(g2) * z2[...]).astype(BF16)

    blk = pl.BlockSpec((TS_MRG, TD_MRG), lambda i, d: (i, d))
    gate = lambda b: pl.BlockSpec((TS_MRG, TD_MRG), lambda i, d: (i, b * N_DBLK + d))
    return pl.pallas_call(
        body,
        name="merge_fwd",
        grid=(S // TS_MRG, N_DBLK),
        in_specs=[blk, blk, blk, gate(0), gate(1), gate(2)],
        out_specs=blk,
        out_shape=jax.ShapeDtypeStruct((S, D), BF16),
        compiler_params=_params(("parallel", "parallel")),
    )(z[0], z[1], z[2], p_d, p_d, p_d)


def _merge_bwd(dmerged, z_b, p_d, b, dp_d, after=None):
    def body(dm_ref, z_ref, g_ref, *refs):
        dz_ref, dg_ref = refs[-2], refs[-1]
        sg = _sigmoid(g_ref[...].astype(F32))
        dm = dm_ref[...]
        dz_ref[...] = (dm * sg).astype(BF16)
        dg_ref[...] = (dm * z_ref[...] * sg * (1.0 - sg)).astype(BF16)

    blk = pl.BlockSpec((TS_MRG, TD_MRG), lambda i, d: (i, d))
    gate = pl.BlockSpec((TS_MRG, TD_MRG), lambda i, d: (i, b * N_DBLK + d))
    in_specs = [blk, blk, gate]
    args = [dmerged, z_b, p_d]
    aliases = {}
    if dp_d is not None:
        in_specs.append(pl.BlockSpec(memory_space=pl.ANY))
        args.append(dp_d)
        aliases = {3: 1}
    if after is not None:
        in_specs.append(pl.BlockSpec(memory_space=pl.ANY))
        args.append(after)
    return pl.pallas_call(
        body,
        name=f"merge_bwd{b}",
        grid=(S // TS_MRG, N_DBLK),
        in_specs=in_specs,
        out_specs=[blk, gate],
        out_shape=[jax.ShapeDtypeStruct((S, D), BF16),
                   jax.ShapeDtypeStruct((S, GROUP_TILES["D"] * LANE), BF16)],
        input_output_aliases=aliases,
        compiler_params=_params(("parallel", "parallel")),
    )(*args)


def _bucket_table():
    import numpy as np
    qi = np.arange(QB)[:, None]
    kj = np.arange(KB2)[None, :]
    n = np.maximum(qi + WINDOW - kj, 0)
    max_exact = REL_BUCKETS // 2
    ratio = np.log(np.maximum(n, 1).astype(np.float32) / max_exact) / np.float32(math.log(REL_MAX_DIST / max_exact))
    large = np.minimum(max_exact + (ratio * (REL_BUCKETS - max_exact)).astype(np.int32), REL_BUCKETS - 1)
    bucket = np.where(n < max_exact, n, large).reshape(1, QB * KB2)
    return (bucket == np.arange(REL_BUCKETS)[:, None]).astype(np.float32)


def _bias_expand(rel_bias_t, onehot_t):
    def body(r_ref, e_ref, o_ref):
        e = e_ref[...]
        acc = jnp.zeros((SWA_HEADS, QB * KB2), F32)
        for term in _split3(r_ref[...]):
            acc = acc + jnp.dot(term, e, preferred_element_type=F32)
        o_ref[...] = acc

    return pl.pallas_call(
        body,
        name="bias_expand",
        out_shape=jax.ShapeDtypeStruct((SWA_HEADS, QB * KB2), F32),
        compiler_params=_params(),
    )(rel_bias_t, onehot_t)


PROJ_TN = {"A": 1024, "B": 1152, "C": 1024, "D": 1536}


def _local_step(x, mem, tgt, sp, fetch, prefetch, emit, advance):
    onehot_t = jnp.asarray(_bucket_table(), BF16)
    bias_t = _bias_expand(sp["rel_bias"].T, onehot_t).reshape(SWA_HEADS, QB, KB2)
    sinks = sp["swa_sinks"].reshape(SWA_HEADS)
    wa16, wx16 = sp["w_rg_a"].astype(BF16), sp["w_rg_x"].astype(BF16)
    rnn = (sp["conv_w"], sp["conv_b"], wa16, sp["b_rg_a"], wx16, sp["b_rg_x"], sp["lru_lambda"])

    h = _rms_fwd(x, sp["pre_norm_g"], "rms_pre")
    memn = _rms_fwd(mem, sp["mem_norm_g"], "rms_mem")
    w_grp, p = {}, {}

    def project(g, after, then=None):
        (w_grp[g],) = fetch((g,), after)
        tok = prefetch(then, w_grp[g]) if then is not None else None
        p[g] = _mm(h, w_grp[g], "nt", BF16, 1024, PROJ_TN[g], D, f"proj_{g}", after=tok)

    project("A", h)
    y_rg, hseq = _rglru_fwd(p["A"], *rnn)
    project("B", y_rg)
    y_swa, o_swa = _swa_fwd(p["B"], bias_t, sinks)
    project("C", y_swa, then=("mk",))
    (wmk,) = fetch(("mk",), p["C"])
    tok = prefetch(("br0", "br1", "br2"), wmk)
    mkv = _mm(memn, wmk, "nn", BF16, MEM, 1024, D, "mkv", after=tok)
    y_mem, o_mem = _mem_fwd(p["C"], mkv)
    ys = (y_rg, y_swa, y_mem)
    wbr = fetch(("br0", "br1", "br2"), y_mem)
    tok = prefetch(("D",), wbr[2])
    z = [_mm(ys[b], wbr[b], "nn", F32, 1024, 1024, D_RNN, f"branch_out{b}", after=tok if b == 0 else None)
         for b in range(3)]
    project("D", z[2], then=("out",))
    merged = _merge_fwd(z, p["D"])
    (wout,) = fetch(("out",), merged)
    out = _mm(merged, wout, "nn", F32, 1024, 1024, D, "out_proj")
    sq, dy, dout, d_post = _post_loss(out, x, tgt, sp["post_norm_g"])

    tok = emit({"out": _mm(merged, dout, "tn", BF16, 1024, 1024, S, "d_wout")})
    dmerged = _mm(dout, wout, "nt", F32, 1024, 1024, D, "d_merged", after=tok)
    dz, dp_d = [], None
    tok = advance(dmerged)
    for b in range(3):
        dz_b, dp_d = _merge_bwd(dmerged, z[b], p["D"], b, dp_d, after=tok if b == 0 else None)
        dz.append(dz_b)
    d_win = lambda g, dp_g, after=None: _mm(dp_g, h, "tn", BF16, PROJ_TN[g], 1024, S, f"d_win_{g}", after=after)
    tok = emit({f"br{b}": _mm(ys[b], dz[b], "tn", BF16, 1024, 1024, S, f"d_wbr{b}") for b in range(3)}, tok)
    d_w_d = d_win("D", dp_d, tok)
    tok = emit({"D": d_w_d}, advance(d_w_d))
    dy_mem = _mm(dz[2], wbr[2], "nt", F32, 1024, 1024, D, "d_branch2", after=tok)
    tok = advance(dy_mem)
    dp_c, dmkv = _mem_bwd(dy_mem, p["C"], o_mem, mkv)
    dmkv16 = dmkv.astype(BF16)
    tok = emit({"mk": _mm(memn, dmkv16, "tn", BF16, 1024, 1024, MEM, "d_wmk", after=tok), "C": d_win("C", dp_c)}, tok)
    dmemn = _mm(dmkv16, wmk, "nt", F32, MEM, 1024, D, "d_memn", after=tok)
    tok = advance(dmemn)
    d_memg = _memnorm_bwd(dmemn, mem)
    dy_rg = _mm(dz[0], wbr[0], "nt", F32, 1024, 1024, D, "d_branch0", after=tok)
    dp_a, d_cw, d_cb, d_wa, d_ba, d_wx, d_bx, d_lam = _rglru_bwd(dy_rg, p["A"], hseq, *rnn)
    tok = emit({"A": d_win("A", dp_a)}, tok)
    dy_swa = _mm(dz[1], wbr[1], "nt", F32, 1024, 1024, D, "d_branch1", after=tok)
    tok = advance(dy_swa)
    dp_b, dk, dv, d_bias, d_sink = _swa_bwd(dy_swa, p["B"], o_swa, bias_t, sinks, after=tok)
    dp_b = _swa_pack(dp_b, dk, dv)
    d_rel = _relbias_grad(d_bias.reshape(SWA_HEADS, QB * KB2), onehot_t).T
    dp = {"A": dp_a, "B": dp_b, "C": dp_c, "D": dp_d}
    tok = emit({"B": d_win("B", dp_b)}, tok)
    dh = None
    for g in GROUPS:
        dh = _mm(dp[g], w_grp[g], "nn", F32, 1024, 1024, 2304 if g == "B" else 2048, f"d_h_{g}", acc=dh,
                 after=tok if g in ("A", "B") else None)
        if g == "A":
            tok = advance(dh)
    grad_x, d_pre = _pre_bwd(dh, x, dy, sp["pre_norm_g"])

    d_small = {
        "pre_norm_g": d_pre, "post_norm_g": d_post, "mem_norm_g": d_memg, "conv_w": d_cw, "conv_b": d_cb,
        "w_rg_a": d_wa, "b_rg_a": d_ba, "w_rg_x": d_wx, "b_rg_x": d_bx, "lru_lambda": d_lam,
        "swa_sinks": d_sink[:, 0].reshape(1, SWA_HEADS), "rel_bias": d_rel,
    }
    return sq, grad_x, d_small


ANY = pl.BlockSpec(memory_space=pl.ANY)
SHARD_ROWS = D // N_CHIPS
GATHERED = {"A": (2048, D), "B": (2304, D), "C": (2048, D), "D": (6144, D), "mk": (D, D),
            "br0": (D_RNN, D), "br1": (D_RNN, D), "br2": (D_RNN, D), "out": (D, D)}
SHARD_SHAPES = {"win": (SHARD, D), "mk": (SHARD_ROWS, D), "br0": (D_RNN, SHARD_ROWS), "br1": (D_RNN, SHARD_ROWS),
                "br2": (D_RNN, SHARD_ROWS), "out": (SHARD_ROWS, D)}
SHARDS = tuple(SHARD_SHAPES)
HALF_AXIS = {"win": 1, "mk": 1, "br0": 0, "br1": 0, "br2": 0, "out": 1,
             "A": 1, "B": 1, "C": 1, "D": 1}


def _halved(shape, axis):
    return (shape[0] // 2, shape[1]) if axis == 0 else (shape[0], shape[1] // 2)


class Piece(NamedTuple):
    src: str
    dst: str
    rows: int
    sr0: int
    sc0: int
    dr0: int
    dc0: int
    ncols: int


def _pieces_of(jj):
    out = [Piece("win", g, n, r, 0, gr, 0, D) for r, n, g, gr in _shard_runs(jj)]
    out.append(Piece("mk", "mk", SHARD_ROWS, 0, 0, SHARD_ROWS * jj, 0, D))
    out += [Piece(f"br{b}", f"br{b}", D_RNN, 0, 0, 0, SHARD_ROWS * jj, SHARD_ROWS) for b in range(3)]
    out.append(Piece("out", "out", SHARD_ROWS, 0, 0, SHARD_ROWS * jj, 0, D))
    return out


def _half_rect(ref, p, side, which):
    r0, c0 = (p.sr0, p.sc0) if side == "src" else (p.dr0, p.dc0)
    if HALF_AXIS[p.src] == 1:
        return _rect(ref, r0, p.rows, c0 + which * (p.ncols // 2), p.ncols // 2)
    return _rect(ref, r0 + which * (p.rows // 2), p.rows // 2, c0, p.ncols)


def _rect_in_half(ref, p, side):
    r0, c0 = (p.sr0, p.sc0) if side == "src" else (p.dr0, p.dc0)
    if HALF_AXIS[p.src] == 1:
        return _rect(ref, r0, p.rows, 0, p.ncols // 2)
    return _rect(ref, 0, p.rows // 2, c0, p.ncols)


MAX_PIECES = max(len(_pieces_of(jj)) for jj in range(N_CHIPS))


def _rect(ref, r0, rows, c0, ncols):
    return ref.at[pl.ds(r0, rows), pl.ds(c0, ncols)]


def _position():
    x, y, c = lax.axis_index("x"), lax.axis_index("y"), lax.axis_index("c")
    return x, y, c, 2 * x + y


HBM = pl.BlockSpec(memory_space=pltpu.HBM)
SEM = pl.BlockSpec(memory_space=pltpu.SEMAPHORE)
EFFECT = pltpu.SideEffectType.DATAFLOW_SIDE_EFFECTING
N_SEM = MAX_PIECES * N_CHIPS
GATHER_STAGES = (("A",), ("B",), ("C",), ("mk",), ("br0", "br1", "br2"), ("D",), ("out",))


def _in_hbm(a):
    return pltpu.with_memory_space_constraint(a, pltpu.HBM)


def _stage_pieces(jj, stage):
    return [(i, p) for i, p in enumerate(_pieces_of(jj)) if p.dst in stage]


def _own_block_table(g):
    import numpy as np
    tbl = np.zeros((N_CHIPS, GATHERED[g][0] // HALF_TILE), np.int32)
    for jj in range(N_CHIPS):
        for r, n, grp, gr in _shard_runs(jj):
            if grp == g:
                for k in range(n // HALF_TILE):
                    tbl[jj, gr // HALF_TILE + k] = r // HALF_TILE + k
    return tbl


def _place_group(w_t, g, table, after):
    nb = GATHERED[g][0] // HALF_TILE

    def body(t_ref, x_ref, _, o_ref):
        o_ref[...] = x_ref[...].astype(BF16)

    return pl.pallas_call(
        body,
        name=f"place_{g}",
        grid_spec=pltpu.PrefetchScalarGridSpec(
            num_scalar_prefetch=1,
            grid=(nb,),
            in_specs=[pl.BlockSpec((HALF_TILE, D), lambda b, t: (t[b], 0)), ANY],
            out_specs=pl.BlockSpec((HALF_TILE, D), lambda b, t: (b, 0)),
        ),
        out_shape=jax.ShapeDtypeStruct(GATHERED[g], BF16),
        compiler_params=_params(("parallel",)),
    )(table, w_t, after)


def _place_shard(shard, name, after):
    rows, cols = shard.shape
    by_rows = HALF_AXIS[name] == 1

    def body(x_ref, _, o_ref):
        o_ref[...] = x_ref[...].astype(BF16)

    return pl.pallas_call(
        body,
        name=f"place_{name}",
        grid=(N_CHIPS,),
        in_specs=[pl.BlockSpec((rows, cols), lambda b: (0, 0)), ANY],
        out_specs=pl.BlockSpec((rows, cols), (lambda b: (b, 0)) if by_rows else (lambda b: (0, b))),
        out_shape=jax.ShapeDtypeStruct(GATHERED[name], BF16),
        compiler_params=_params(("parallel",)),
    )(shard, after)


def _gather_copy(arr, send_sems, recv_sems, c, jj, i, p, kk):
    rect = _half_rect(arr[p.dst], p, "dst", c)
    return pltpu.make_async_remote_copy(
        src_ref=rect, dst_ref=rect, send_sem=send_sems.at[i * N_CHIPS + kk],
        recv_sem=recv_sems.at[jj * MAX_PIECES + i], device_id=(kk // 2, kk % 2, c), device_id_type=MESH)


def _gather_start(arrays, after):
    stage = tuple(arrays)
    na = len(stage)

    def body(*refs):
        arr = dict(zip(stage, refs[:na]))
        send_sems, recv_sems = refs[na + 1], refs[na + 2]
        token = refs[-1]
        _, _, c, j = _position()
        for jj in range(N_CHIPS):
            @pl.when(j == jj)
            def _():
                for i, p in _stage_pieces(jj, stage):
                    for kk in range(N_CHIPS):
                        if kk != jj:
                            _gather_copy(arr, send_sems, recv_sems, c, jj, i, p, kk).start()
        token[...] = jnp.zeros_like(token)

    outs = pl.pallas_call(
        body,
        name=f"gather_start_{stage[0]}",
        in_specs=[HBM] * na + [ANY],
        out_specs=[SEM, SEM] + [HBM] * na + [pl.BlockSpec(memory_space=pltpu.VMEM)],
        out_shape=[pltpu.SemaphoreType.DMA((N_SEM,)), pltpu.SemaphoreType.DMA((N_SEM,))]
        + [pltpu.HBM(GATHERED[n], BF16) for n in stage] + [jax.ShapeDtypeStruct((8, LANE), F32)],
        input_output_aliases={k: 2 + k for k in range(na)},
        compiler_params=pltpu.CompilerParams(has_side_effects=EFFECT),
    )(*[_in_hbm(arrays[n]) for n in stage], after)
    return outs[0], outs[1], dict(zip(stage, outs[2:2 + na])), outs[-1]


def _gather_wait(send_sems, recv_sems, arrays, after):
    stage = tuple(arrays)
    na = len(stage)

    def body(*refs):
        arr = dict(zip(stage, refs[:na]))
        sems_s, sems_r = refs[na], refs[na + 1]
        _, _, c, j = _position()
        for jj in range(N_CHIPS):
            @pl.when(j != jj)
            def _():
                for i, p in _stage_pieces(jj, stage):
                    _gather_copy(arr, sems_s, sems_r, c, jj, i, p, jj).wait_recv()

            @pl.when(j == jj)
            def _():
                for i, p in _stage_pieces(jj, stage):
                    for kk in range(N_CHIPS):
                        if kk != jj:
                            _gather_copy(arr, sems_s, sems_r, c, jj, i, p, kk).wait_send()

    outs = pl.pallas_call(
        body,
        name=f"gather_wait_{stage[0]}",
        in_specs=[HBM] * na + [SEM, SEM, ANY],
        out_specs=[HBM] * na,
        out_shape=[pltpu.HBM(GATHERED[n], BF16) for n in stage],
        input_output_aliases={k: k for k in range(na)},
        compiler_params=pltpu.CompilerParams(has_side_effects=EFFECT),
    )(*[arrays[n] for n in stage], send_sems, recv_sems, after)
    return dict(zip(stage, outs))


def _gather_swap(arrays):
    stage = tuple(arrays)
    na = len(stage)

    def body(*refs):
        dst = dict(zip(stage, refs[na:2 * na]))
        send_sems, recv_sems = refs[2 * na:]
        x, y, c, j = _position()

        def fwd(jj, i, p, which):
            rect = _half_rect(dst[p.dst], p, "dst", which)
            return pltpu.make_async_remote_copy(
                src_ref=rect, dst_ref=rect, send_sem=send_sems.at[jj * MAX_PIECES + i],
                recv_sem=recv_sems.at[jj * MAX_PIECES + i], device_id=(x, y, 1 - c), device_id_type=MESH)

        for jj in range(N_CHIPS):
            @pl.when(j != jj)
            def _():
                for i, p in _stage_pieces(jj, stage):
                    fwd(jj, i, p, c).start()
        for jj in range(N_CHIPS):
            @pl.when(j != jj)
            def _():
                for i, p in _stage_pieces(jj, stage):
                    fwd(jj, i, p, 1 - c).wait_recv()
        for jj in range(N_CHIPS):
            @pl.when(j != jj)
            def _():
                for i, p in _stage_pieces(jj, stage):
                    fwd(jj, i, p, c).wait_send()

    outs = pl.pallas_call(
        body,
        name=f"gather_swap_{stage[0]}",
        in_specs=[ANY] * na,
        out_specs=[ANY] * na,
        out_shape=[jax.ShapeDtypeStruct(GATHERED[n], BF16) for n in stage],
        input_output_aliases={k: k for k in range(na)},
        scratch_shapes=[pltpu.SemaphoreType.DMA((N_SEM,)), pltpu.SemaphoreType.DMA((N_SEM,))],
        compiler_params=pltpu.CompilerParams(has_side_effects=True),
    )(*[arrays[n] for n in stage])
    return dict(zip(stage, outs))


def _pass_on_copy(arr, send_sems, recv_sems, x, y, c, jj, i, p, which):
    rect = _half_rect(arr[p.dst], p, "dst", which)
    return pltpu.make_async_remote_copy(
        src_ref=rect, dst_ref=rect, send_sem=send_sems.at[jj * MAX_PIECES + i],
        recv_sem=recv_sems.at[jj * MAX_PIECES + i], device_id=(x, y, 1 - c), device_id_type=MESH)


def _gather_pass_start(arrays, after):
    stage = tuple(arrays)
    na = len(stage)

    def body(*refs):
        arr = dict(zip(stage, refs[:na]))
        x, y, c, j = _position()
        for jj in range(N_CHIPS):
            @pl.when(j != jj)
            def _():
                for i, p in _stage_pieces(jj, stage):
                    _pass_on_copy(arr, refs[na + 1], refs[na + 2], x, y, c, jj, i, p, c).start()
        refs[-1][...] = jnp.zeros_like(refs[-1])

    outs = pl.pallas_call(
        body,
        name=f"gather_pass_start_{stage[0]}",
        in_specs=[HBM] * na + [ANY],
        out_specs=[SEM, SEM] + [HBM] * na + [pl.BlockSpec(memory_space=pltpu.VMEM)],
        out_shape=[pltpu.SemaphoreType.DMA((N_SEM,)), pltpu.SemaphoreType.DMA((N_SEM,))]
        + [pltpu.HBM(GATHERED[n], BF16) for n in stage] + [jax.ShapeDtypeStruct((8, LANE), F32)],
        input_output_aliases={k: 2 + k for k in range(na)},
        compiler_params=pltpu.CompilerParams(has_side_effects=EFFECT),
    )(*[arrays[n] for n in stage], after)
    return outs[0], outs[1], dict(zip(stage, outs[2:2 + na])), outs[-1]


def _gather_pass_wait(send_sems, recv_sems, arrays, after):
    stage = tuple(arrays)
    na = len(stage)

    def body(*refs):
        arr = dict(zip(stage, refs[:na]))
        x, y, c, j = _position()
        for jj in range(N_CHIPS):
            @pl.when(j != jj)
            def _():
                for i, p in _stage_pieces(jj, stage):
                    _pass_on_copy(arr, refs[na], refs[na + 1], x, y, c, jj, i, p, 1 - c).wait_recv()
                    _pass_on_copy(arr, refs[na], refs[na + 1], x, y, c, jj, i, p, c).wait_send()

    outs = pl.pallas_call(
        body,
        name=f"gather_pass_wait_{stage[0]}",
        in_specs=[HBM] * na + [SEM, SEM, ANY],
        out_specs=[HBM] * na,
        out_shape=[pltpu.HBM(GATHERED[n], BF16) for n in stage],
        input_output_aliases={k: k for k in range(na)},
        compiler_params=pltpu.CompilerParams(has_side_effects=EFFECT),
    )(*[arrays[n] for n in stage], send_sems, recv_sems, after)
    return dict(zip(stage, outs))


def _own_half(ref, shape, axis, which):
    if axis == 1:
        return ref.at[:, pl.ds(which * (shape[1] // 2), shape[1] // 2)]
    return ref.at[pl.ds(which * (shape[0] // 2), shape[0] // 2), :]


def _swap_copies(names, src, dst, send_sems, recv_sems):
    x, y, c, _ = _position()
    return [pltpu.make_async_remote_copy(
        src_ref=_own_half(src[n], GATHERED[n], HALF_AXIS[n], 1 - c), dst_ref=dst[n],
        send_sem=send_sems.at[k], recv_sem=recv_sems.at[k],
        device_id=(x, y, 1 - c), device_id_type=MESH) for k, n in enumerate(names)]


def _swap_start(grads, after):
    names = tuple(grads)
    n = len(names)

    def body(*refs):
        src, dst = dict(zip(names, refs[:n])), dict(zip(names, refs[n:2 * n]))
        for cp in _swap_copies(names, src, dst, refs[2 * n + 1], refs[2 * n + 2]):
            cp.start()
        refs[-1][...] = jnp.zeros_like(refs[-1])

    half_shape = lambda nm: _halved(GATHERED[nm], HALF_AXIS[nm])
    args = [_in_hbm(grads[nm]) for nm in names] + [_in_hbm(lax.empty(half_shape(nm), BF16)) for nm in names]
    if after is None:
        after = jnp.zeros((8, LANE), F32)
    outs = pl.pallas_call(
        body,
        name=f"swap_start_{names[0]}",
        in_specs=[HBM] * (2 * n) + [ANY],
        out_specs=[SEM, SEM] + [HBM] * (2 * n) + [pl.BlockSpec(memory_space=pltpu.VMEM)],
        out_shape=[pltpu.SemaphoreType.DMA((n,)), pltpu.SemaphoreType.DMA((n,))]
        + [pltpu.HBM(GATHERED[nm], BF16) for nm in names] + [pltpu.HBM(half_shape(nm), BF16) for nm in names]
        + [jax.ShapeDtypeStruct((8, LANE), F32)],
        input_output_aliases={k: 2 + k for k in range(2 * n)},
        compiler_params=pltpu.CompilerParams(has_side_effects=EFFECT),
    )(*args, after)
    return outs[0], outs[1], dict(zip(names, outs[2:2 + n])), dict(zip(names, outs[2 + n:2 + 2 * n])), outs[-1]


def _swap_wait(send_sems, recv_sems, grads, landing, after):
    names = tuple(grads)
    n = len(names)

    def body(*refs):
        src, dst = dict(zip(names, refs[:n])), dict(zip(names, refs[n:2 * n]))
        copies = _swap_copies(names, src, dst, refs[2 * n], refs[2 * n + 1])
        for cp in copies:
            cp.wait_recv()
        for cp in copies:
            cp.wait_send()

    half_shape = lambda nm: _halved(GATHERED[nm], HALF_AXIS[nm])
    outs = pl.pallas_call(
        body,
        name=f"swap_wait_{names[0]}",
        in_specs=[HBM] * (2 * n) + [SEM, SEM, ANY],
        out_specs=[HBM] * (2 * n),
        out_shape=[pltpu.HBM(GATHERED[nm], BF16) for nm in names] + [pltpu.HBM(half_shape(nm), BF16) for nm in names],
        input_output_aliases={k: k for k in range(2 * n)},
        compiler_params=pltpu.CompilerParams(has_side_effects=EFFECT),
    )(*[grads[nm] for nm in names], *[landing[nm] for nm in names], send_sems, recv_sems, after)
    return dict(zip(names, outs[:n])), dict(zip(names, outs[n:]))


ADD_ROWS = 256


def _add_half(full, recv, c_arr, name):
    rows, cols = recv.shape
    if HALF_AXIS[name] == 1:
        index = lambda i, c_ref: (i, c_ref[0])
    else:
        nb = rows // ADD_ROWS
        index = lambda i, c_ref: (nb * c_ref[0] + i, 0)

    def body(c_ref, a_ref, b_ref, o_ref):
        o_ref[...] = (a_ref[...].astype(F32) + b_ref[...].astype(F32)).astype(BF16)

    return pl.pallas_call(
        body,
        name=f"add_half_{name}",
        grid_spec=pltpu.PrefetchScalarGridSpec(
            num_scalar_prefetch=1,
            grid=(rows // ADD_ROWS,),
            in_specs=[pl.BlockSpec((ADD_ROWS, cols), index), pl.BlockSpec((ADD_ROWS, cols), lambda i, c_ref: (i, 0))],
            out_specs=pl.BlockSpec((ADD_ROWS, cols), lambda i, c_ref: (i, 0)),
        ),
        out_shape=jax.ShapeDtypeStruct((rows, cols), BF16),
        compiler_params=_params(("parallel",)),
    )(c_arr, full, recv)


SLOT_SHAPES = {n: _halved(SHARD_SHAPES[n], HALF_AXIS[n]) for n in SHARDS}


def _slot_shape(n):
    return (N_CHIPS,) + SLOT_SHAPES[n]


def _stage_shards(stage):
    pieces = [p for jj in range(N_CHIPS) for p in _pieces_of(jj)]
    return tuple(s for s in SHARDS if any(p.src == s and p.dst in stage for p in pieces))


def _scatter_copy(src, dst, send_sems, recv_sems, c, jj, kk, i, p):
    return pltpu.make_async_remote_copy(
        src_ref=_rect_in_half(src[p.dst], p, "dst"), dst_ref=_rect_in_half(dst[p.src].at[jj], p, "src"),
        send_sem=send_sems.at[kk * MAX_PIECES + i], recv_sem=recv_sems.at[jj * MAX_PIECES + i],
        device_id=(kk // 2, kk % 2, c), device_id_type=MESH)


def _scatter_start(halves, slots):
    stage, touched = tuple(halves), tuple(slots)
    nh, nt = len(stage), len(touched)

    def body(*refs):
        src = dict(zip(stage, refs[:nh]))
        dst = dict(zip(touched, refs[nh:nh + nt]))
        send_sems, recv_sems = refs[nh + nt], refs[nh + nt + 1]
        token = refs[-1]
        _, _, c, j = _position()
        for jj in range(N_CHIPS):
            @pl.when(j == jj)
            def _():
                for kk in range(N_CHIPS):
                    if kk != jj:
                        for i, p in _stage_pieces(kk, stage):
                            _scatter_copy(src, dst, send_sems, recv_sems, c, jj, kk, i, p).start()
        token[...] = jnp.zeros_like(token)

    outs = pl.pallas_call(
        body,
        name=f"scatter_start_{stage[0]}",
        in_specs=[HBM] * (nh + nt),
        out_specs=[SEM, SEM] + [HBM] * (nh + nt) + [pl.BlockSpec(memory_space=pltpu.VMEM)],
        out_shape=[pltpu.SemaphoreType.DMA((N_SEM,)), pltpu.SemaphoreType.DMA((N_SEM,))]
        + [pltpu.HBM(halves[n].shape, BF16) for n in stage] + [pltpu.HBM(_slot_shape(s), BF16) for s in touched]
        + [jax.ShapeDtypeStruct((8, LANE), F32)],
        input_output_aliases={k: 2 + k for k in range(nh + nt)},
        compiler_params=pltpu.CompilerParams(has_side_effects=EFFECT),
    )(*[_in_hbm(halves[n]) for n in stage], *[_in_hbm(slots[s]) for s in touched])
    return outs[0], outs[1], dict(zip(stage, outs[2:2 + nh])), dict(zip(touched, outs[2 + nh:2 + nh + nt])), outs[-1]


def _scatter_wait(send_sems, recv_sems, halves, slots, after):
    stage, touched = tuple(halves), tuple(slots)
    nh, nt = len(stage), len(touched)

    def body(*refs):
        src = dict(zip(stage, refs[:nh]))
        dst = dict(zip(touched, refs[nh:nh + nt]))
        sems_s, sems_r = refs[nh + nt], refs[nh + nt + 1]
        _, _, c, j = _position()
        for jj in range(N_CHIPS):
            @pl.when(j == jj)
            def _():
                for ss in range(N_CHIPS):
                    if ss != jj:
                        for i, p in _stage_pieces(jj, stage):
                            _scatter_copy(src, dst, sems_s, sems_r, c, ss, jj, i, p).wait_recv()
                for kk in range(N_CHIPS):
                    if kk != jj:
                        for i, p in _stage_pieces(kk, stage):
                            _scatter_copy(src, dst, sems_s, sems_r, c, jj, kk, i, p).wait_send()

    outs = pl.pallas_call(
        body,
        name=f"scatter_wait_{stage[0]}",
        in_specs=[HBM] * (nh + nt) + [SEM, SEM, ANY],
        out_specs=[HBM] * (nh + nt),
        out_shape=[pltpu.HBM(halves[n].shape, BF16) for n in stage] + [pltpu.HBM(_slot_shape(s), BF16) for s in touched],
        input_output_aliases={k: k for k in range(nh + nt)},
        compiler_params=pltpu.CompilerParams(has_side_effects=EFFECT),
    )(*[halves[n] for n in stage], *[slots[s] for s in touched], send_sems, recv_sems, after)
    return dict(zip(stage, outs[:nh])), dict(zip(touched, outs[nh:]))


SUM_ROWS = {"win": 448, "mk": 256, "br0": 256, "br1": 256, "br2": 256, "out": 256}


def _sum_in_chip_order(chip, own, s_ref):
    acc = None
    for k in range(N_CHIPS):
        term = jnp.where(chip == k, own, s_ref[k].astype(F32))
        acc = term if acc is None else acc + term
    return acc


def _sum_slots(slots, own_half, pos_arr, name):
    _, rows, cols = slots.shape
    tr = SUM_ROWS[name]
    nb = rows // tr
    if HALF_AXIS[name] == 1:
        own_index = lambda i, pos: (nb * pos[1] + i, 0)
        out_index = lambda i, pos: (i, pos[0])
    else:
        own_index = lambda i, pos: (i, pos[1])
        out_index = lambda i, pos: (nb * pos[0] + i, 0)

    def body(pos, s_ref, own_ref, o_ref):
        o_ref[...] = _sum_in_chip_order(pos[1], own_ref[...].astype(F32), s_ref)

    return pl.pallas_call(
        body,
        name=f"sum_slots_{name}",
        grid_spec=pltpu.PrefetchScalarGridSpec(
            num_scalar_prefetch=1,
            grid=(nb,),
            in_specs=[pl.BlockSpec((N_CHIPS, tr, cols), lambda i, pos: (0, i, 0)),
                      pl.BlockSpec((tr, cols), own_index)],
            out_specs=pl.BlockSpec((tr, cols), out_index),
        ),
        out_shape=jax.ShapeDtypeStruct(SHARD_SHAPES[name], F32),
        compiler_params=_params(("parallel",)),
    )(pos_arr, slots, own_half)


def _own_partial_tables():
    import numpy as np
    nb = SHARD // HALF_TILE
    grp, blk = np.zeros((N_CHIPS, nb), np.int32), np.zeros((N_CHIPS, nb), np.int32)
    for jj in range(N_CHIPS):
        for r, n, g, gr in _shard_runs(jj):
            for k in range(n // HALF_TILE):
                grp[jj, r // HALF_TILE + k] = GROUPS.index(g)
                blk[jj, r // HALF_TILE + k] = gr // HALF_TILE + k
    return grp, blk


def _sum_slots_win(slots, own_halves, pos_arr, grp_tbl, blk_tbl):
    nb = SHARD // HALF_TILE
    cols = D // 2

    def own_spec(gi):
        return pl.BlockSpec((HALF_TILE, cols), lambda b, pos, grp, blk: (jnp.where(grp[b] == gi, blk[b], 0), 0))

    def body(pos, grp, blk, s_ref, a_ref, b_ref, c_ref, d_ref, o_ref):
        g = grp[pl.program_id(0)]
        own = a_ref[...]
        for gi, ref in ((1, b_ref), (2, c_ref), (3, d_ref)):
            own = jnp.where(g == gi, ref[...], own)
        o_ref[...] = _sum_in_chip_order(pos[1], own.astype(F32), s_ref)

    return pl.pallas_call(
        body,
        name="sum_slots_win",
        grid_spec=pltpu.PrefetchScalarGridSpec(
            num_scalar_prefetch=3,
            grid=(nb,),
            in_specs=[pl.BlockSpec((N_CHIPS, HALF_TILE, cols), lambda b, pos, grp, blk: (0, b, 0))]
            + [own_spec(gi) for gi in range(len(GROUPS))],
            out_specs=pl.BlockSpec((HALF_TILE, cols), lambda b, pos, grp, blk: (b, pos[0])),
        ),
        out_shape=jax.ShapeDtypeStruct(SHARD_SHAPES["win"], F32),
        compiler_params=_params(("parallel",)),
    )(pos_arr, grp_tbl, blk_tbl, slots, *[own_halves[g] for g in GROUPS])


def _share_copy(buf, name, send_sems, recv_sems, k, which):
    x, y, c, _ = _position()
    half = _own_half(buf, SHARD_SHAPES[name], HALF_AXIS[name], which)
    return pltpu.make_async_remote_copy(src_ref=half, dst_ref=half, send_sem=send_sems.at[k], recv_sem=recv_sems.at[k],
                                        device_id=(x, y, 1 - c), device_id_type=MESH)


def _share_start(sums, after):
    names = tuple(sums)
    n = len(names)

    def body(*refs):
        _, _, c, _ = _position()
        for k, nm in enumerate(names):
            _share_copy(refs[k], nm, refs[n + 1], refs[n + 2], k, c).start()
        refs[-1][...] = jnp.zeros_like(refs[-1])

    outs = pl.pallas_call(
        body,
        name=f"share_start_{names[0]}",
        in_specs=[HBM] * n + [ANY],
        out_specs=[SEM, SEM] + [HBM] * n + [pl.BlockSpec(memory_space=pltpu.VMEM)],
        out_shape=[pltpu.SemaphoreType.DMA((n,)), pltpu.SemaphoreType.DMA((n,))]
        + [pltpu.HBM(SHARD_SHAPES[nm], F32) for nm in names] + [jax.ShapeDtypeStruct((8, LANE), F32)],
        input_output_aliases={k: 2 + k for k in range(n)},
        compiler_params=pltpu.CompilerParams(has_side_effects=EFFECT),
    )(*[_in_hbm(sums[nm]) for nm in names], after)
    return outs[0], outs[1], dict(zip(names, outs[2:2 + n])), outs[-1]


def _share_wait(send_sems, recv_sems, sums, after):
    names = tuple(sums)
    n = len(names)

    def body(*refs):
        _, _, c, _ = _position()
        for k, nm in enumerate(names):
            _share_copy(refs[k], nm, refs[n], refs[n + 1], k, 1 - c).wait_recv()
            _share_copy(refs[k], nm, refs[n], refs[n + 1], k, c).wait_send()

    outs = pl.pallas_call(
        body,
        name=f"share_wait_{names[0]}",
        in_specs=[HBM] * n + [SEM, SEM, ANY],
        out_specs=[HBM] * n,
        out_shape=[pltpu.HBM(SHARD_SHAPES[nm], F32) for nm in names],
        input_output_aliases={k: k for k in range(n)},
        compiler_params=pltpu.CompilerParams(has_side_effects=EFFECT),
    )(*[sums[nm] for nm in names], send_sems, recv_sems, after)
    return dict(zip(names, outs))


def _all_reduce_small(pack, name):
    rows = pack.shape[0]
    half = rows // 2

    def body(p_ref, o_ref, sib, land, sems):
        x, y, c, j = _position()
        sibling = (x, y, 1 - c)
        swap = pltpu.make_async_remote_copy(src_ref=p_ref, dst_ref=sib, send_sem=sems.at[0], recv_sem=sems.at[1],
                                            device_id=sibling, device_id_type=MESH)
        swap.start()
        swap.wait_recv()
        land[j] = p_ref[...] + sib[...]

        def mine(k, which):
            return land.at[k, pl.ds(which * half, half)]

        def ici(kk):
            return pltpu.make_async_remote_copy(
                src_ref=mine(j, c), dst_ref=mine(j, c), send_sem=sems.at[2 + kk], recv_sem=sems.at[6 + j],
                device_id=(kk // 2, kk % 2, c), device_id_type=MESH)

        def arrival(kk):
            return pltpu.make_async_remote_copy(
                src_ref=mine(kk, c), dst_ref=mine(kk, c), send_sem=sems.at[2 + kk], recv_sem=sems.at[6 + kk],
                device_id=(kk // 2, kk % 2, c), device_id_type=MESH)

        def passed_on(kk, which):
            return pltpu.make_async_remote_copy(
                src_ref=mine(kk, which), dst_ref=mine(kk, which), send_sem=sems.at[10 + kk],
                recv_sem=sems.at[14 + kk], device_id=sibling, device_id_type=MESH)

        for kk in range(N_CHIPS):
            @pl.when(j != kk)
            def _():
                ici(kk).start()
        for kk in range(N_CHIPS):
            @pl.when(j != kk)
            def _():
                arrival(kk).wait_recv()
                passed_on(kk, c).start()
        for kk in range(N_CHIPS):
            @pl.when(j != kk)
            def _():
                passed_on(kk, 1 - c).wait_recv()
        acc = land[0]
        for kk in range(1, N_CHIPS):
            acc = acc + land[kk]
        o_ref[...] = acc
        swap.wait_send()
        for kk in range(N_CHIPS):
            @pl.when(j != kk)
            def _():
                ici(kk).wait_send()
                passed_on(kk, c).wait_send()

    vmem = pl.BlockSpec(memory_space=pltpu.VMEM)
    return pl.pallas_call(
        body,
        name=name,
        in_specs=[vmem],
        out_specs=vmem,
        out_shape=jax.ShapeDtypeStruct((rows, LANE), F32),
        scratch_shapes=[pltpu.VMEM((rows, LANE), F32), pltpu.VMEM((N_CHIPS, rows, LANE), F32),
                        pltpu.SemaphoreType.DMA((18,))],
        compiler_params=pltpu.CompilerParams(has_side_effects=True, vmem_limit_bytes=VMEM_LIMIT),
    )(pack)


def _adamw(w, g, m, v, name, tr):
    rows, cols = w.shape
    tr = min(tr, rows)

    def body(w_ref, g_ref, m_ref, v_ref, go_ref, d_ref, nm_ref, nv_ref):
        gv = g_ref[...]
        go_ref[...] = gv
        nm = ADAM_B1 * m_ref[...] + (1.0 - ADAM_B1) * gv
        nv = ADAM_B2 * v_ref[...] + (1.0 - ADAM_B2) * (gv * gv)
        nm_ref[...] = nm
        nv_ref[...] = nv
        m_hat = nm / (1.0 - ADAM_B1 ** ADAM_STEP)
        v_hat = nv / (1.0 - ADAM_B2 ** ADAM_STEP)
        d_ref[...] = -ADAM_LR * (m_hat / (jnp.sqrt(v_hat) + ADAM_EPS) + ADAM_WD * w_ref[...])

    blk = pl.BlockSpec((tr, cols), lambda i: (i, 0))
    shape = jax.ShapeDtypeStruct((rows, cols), F32)
    return pl.pallas_call(
        body,
        name=f"adamw_{name}",
        grid=(rows // tr,),
        in_specs=[blk] * 4,
        out_specs=[blk] * 4,
        out_shape=[shape] * 4,
        compiler_params=_params(("parallel",)),
    )(w, g, m, v)


SMALL = (("pre_norm_g", (1, D)), ("post_norm_g", (1, D)), ("mem_norm_g", (1, D)), ("conv_w", (CONV_W, D_RNN)),
         ("conv_b", (1, D_RNN)), ("w_rg_a", (RNN_BLOCKS, LANE, LANE)), ("b_rg_a", (1, D_RNN)),
         ("w_rg_x", (RNN_BLOCKS, LANE, LANE)), ("b_rg_x", (1, D_RNN)), ("lru_lambda", (1, D_RNN)),
         ("swa_sinks", (1, SWA_HEADS)), ("rel_bias", (REL_BUCKETS, SWA_HEADS)))
PACK_ROWS = 2176


def _slot_len(shape):
    return -(-math.prod(shape) // LANE) * LANE


def _pack(values, last_row=None):
    parts = []
    for name, shape in SMALL:
        flat = values[name].reshape(-1).astype(F32)
        parts.append(jnp.pad(flat, (0, _slot_len(shape) - flat.shape[0])))
    flat = jnp.concatenate(parts)
    tail = jnp.zeros((LANE,), F32) if last_row is None else last_row
    return jnp.concatenate([jnp.pad(flat, (0, (PACK_ROWS - 1) * LANE - flat.shape[0])), tail]).reshape(PACK_ROWS, LANE)


def _unpack(pack, shapes=None):
    flat = pack.reshape(-1)
    out, off = {}, 0
    for name, shape in SMALL:
        shp = shape if shapes is None or name not in shapes else shapes[name]
        out[name] = flat[off:off + math.prod(shp)].reshape(shp)
        off += _slot_len(shape)
    return out


TWIN_WEIGHTS = ("pre_norm_g", "post_norm_g", "mem_norm_g", "w_in", "conv_w", "conv_b", "w_rg_a", "b_rg_a", "w_rg_x",
                "b_rg_x", "lru_lambda", "swa_sinks", "rel_bias", "w_mem_kv", "w_br_rg", "w_br_swa", "w_br_mem", "w_out")
BIG = {"w_in": "win", "w_mem_kv": "mk", "w_br_rg": "br0", "w_br_swa": "br1", "w_br_mem": "br2", "w_out": "out"}


def kernel(x, mem, pre_norm_g, post_norm_g, mem_norm_g, w_in, conv_w, conv_b, w_rg_a, b_rg_a, w_rg_x, b_rg_x, lru_lambda, swa_sinks, rel_bias, w_mem_kv, w_br_rg, w_br_swa, w_br_mem, w_out, loss_target, m_pre_norm_g, m_post_norm_g, m_mem_norm_g, m_w_in, m_conv_w, m_conv_b, m_w_rg_a, m_b_rg_a, m_w_rg_x, m_b_rg_x, m_lru_lambda, m_swa_sinks, m_rel_bias, m_w_mem_kv, m_w_br_rg, m_w_br_swa, m_w_br_mem, m_w_out, v_pre_norm_g, v_post_norm_g, v_mem_norm_g, v_w_in, v_conv_w, v_conv_b, v_w_rg_a, v_b_rg_a, v_w_rg_x, v_b_rg_x, v_lru_lambda, v_swa_sinks, v_rel_bias, v_w_mem_kv, v_w_br_rg, v_w_br_swa, v_w_br_mem, v_w_out):
    args = dict(locals())
    out_shapes = {n: args[n].shape for n in TWIN_WEIGHTS}
    w = {n: (args[n] if n == "rel_bias" else args[n][0]) for n in TWIN_WEIGHTS}
    m = {n: (args["m_" + n] if n == "rel_bias" else args["m_" + n][0]) for n in TWIN_WEIGHTS}
    v = {n: (args["v_" + n] if n == "rel_bias" else args["v_" + n][0]) for n in TWIN_WEIGHTS}
    for d in (w, m, v):
        for n, shape in SMALL:
            if n != "conv_w":
                d[n] = d[n].reshape(shape)

    xi, yi, ci = lax.axis_index("x"), lax.axis_index("y"), lax.axis_index("c")
    chip = 2 * xi + yi
    c_arr = ci.astype(jnp.int32).reshape(1)
    zero = jnp.zeros((), jnp.int32)
    cw0 = (chip * (D_RNN // N_CHIPS)).astype(jnp.int32)

    placed = lax.dynamic_update_slice(jnp.zeros((CONV_W, D_RNN), F32), w["conv_w"], (zero, cw0))
    placed = jnp.where(ci == 0, placed, 0.0).reshape(CONV_W * D_RNN // LANE, LANE)
    conv_w_full = _all_reduce_small(placed, "gather_conv_w").reshape(CONV_W, D_RNN)

    for d in (w, m, v):
        d["w_in"] = d["w_in"].T
    chip_row = lambda tbl: lax.dynamic_slice(jnp.asarray(tbl), (chip.astype(jnp.int32), zero), (1, tbl.shape[1]))[0]
    big_of = {s: n for n, s in BIG.items()}
    ag, token = {}, conv_w_full
    for stage in GATHER_STAGES:
        behind = c_arr if stage == GATHER_STAGES[0] else token
        placed = {n: (_place_group(w["w_in"], n, chip_row(_own_block_table(n)), behind) if n in GROUPS
                      else _place_shard(w[big_of[n]], n, behind)) for n in stage}
        send, recv, in_flight, token = _gather_start(placed, token)
        ag[stage] = (send, recv, in_flight)

    all_started = token

    passing = {}

    def prefetch(names, after):
        send, recv, in_flight = ag[names]
        *passing[names], token = _gather_pass_start(_gather_wait(send, recv, in_flight, after), after)
        return token

    def fetch(names, after):
        if names in passing:
            ready = _gather_pass_wait(*passing.pop(names), after)
        else:
            send, recv, in_flight = ag[names]
            after = all_started if names == GATHER_STAGES[0] else after
            ready = _gather_swap(_gather_wait(send, recv, in_flight, after))
        return tuple(ready[n] for n in names)

    rs = {"slots": {}, "halves": {}, "pending": [], "swap": None}

    def emit(grads, after=None):
        assert rs["swap"] is None
        *rs["swap"], token = _swap_start(grads, after)
        return token

    def advance(after):
        grads, received = _swap_wait(*rs["swap"], after)
        rs["swap"] = None
        halves = {n: _add_half(grads[n], received[n], c_arr, n) for n in grads}
        landing = {s: rs["slots"][s] if s in rs["slots"] else lax.empty(_slot_shape(s), BF16)
                   for s in _stage_shards(tuple(grads))}
        send, recv, halves, landing, token = _scatter_start(halves, landing)
        rs["slots"].update(landing)
        rs["pending"].append((send, recv, halves, tuple(landing)))
        return token

    sp = {n: w[n] for n, _ in SMALL}
    sp["conv_w"] = conv_w_full
    sq, grad_x, d_small = _local_step(x[0], mem[0], loss_target[0], sp, fetch, prefetch, emit, advance)
    small_total = _all_reduce_small(_pack(d_small, sq[0]), "all_reduce_small")
    loss = small_total[PACK_ROWS - 1, 0] * (0.5 / D)

    for send, recv, halves, touched in rs["pending"]:
        halves, landed = _scatter_wait(send, recv, halves, {s: rs["slots"][s] for s in touched}, small_total)
        rs["slots"].update(landed)
        rs["halves"].update(halves)
    pos_arr = jnp.stack([ci, chip]).astype(jnp.int32)
    grp_tbl, blk_tbl = (chip_row(t) for t in _own_partial_tables())
    rest = {s: _sum_slots(rs["slots"][s], rs["halves"][s], pos_arr, s) for s in SHARDS if s != "win"}
    *rest_share, tok = _share_start(rest, small_total)
    win_sum = _sum_slots_win(rs["slots"]["win"], rs["halves"], pos_arr, grp_tbl, blk_tbl)
    *win_share, tok = _share_start({"win": win_sum}, tok)
    sums = _share_wait(*rest_share, tok)

    g_small = _unpack(small_total)
    g_small["conv_w"] = lax.dynamic_slice(g_small["conv_w"], (zero, cw0), (CONV_W, D_RNN // N_CHIPS))

    grad, delta, new_m, new_v = {}, {}, {}, {}
    for n, s in BIG.items():
        if n == "w_in":
            continue
        grad[n], delta[n], new_m[n], new_v[n] = _adamw(w[n], sums[s], m[n], v[n], s, 128)
    g_win = _share_wait(*win_share, delta["w_out"])["win"]
    n = "w_in"
    grad[n], delta[n], new_m[n], new_v[n] = _adamw(w[n], g_win, m[n], v[n], "win", 224)
    for group in (grad, delta, new_m, new_v):
        group["w_in"] = group["w_in"].T
    _, d_, m_, v_ = _adamw(_pack(w), _pack(g_small), _pack(m), _pack(v), "small", PACK_ROWS)
    shard_shapes = {"conv_w": (CONV_W, D_RNN // N_CHIPS)}
    d_, m_, v_ = (_unpack(a, shard_shapes) for a in (d_, m_, v_))
    for n, _ in SMALL:
        grad[n], delta[n], new_m[n], new_v[n] = g_small[n], d_[n], m_[n], v_[n]

    outs = [loss, grad_x.reshape(1, S, D)]
    for group in (grad, delta, new_m, new_v):
        outs += [group[n].reshape(out_shapes[n]) for n in TWIN_WEIGHTS]
    return tuple(outs)
```

```python
import math
from typing import NamedTuple

import jax
import jax.numpy as jnp
from jax import lax
from jax.experimental import pallas as pl
from jax.experimental.pallas import tpu as pltpu

F32 = jnp.float32
BF16 = jnp.bfloat16
MESH = pl.DeviceIdType.MESH

S = 2048
D = 2048
MEM = 256
D_RNN = 1024
RNN_BLOCKS = 8
CONV_W = 4
LRU_C = 8.0
SWA_HEADS = 16
SWA_HD = 64
WINDOW = 128
MEM_HEADS = 4
MEM_HD = 256
REL_BUCKETS = 32
REL_MAX_DIST = 128
EPS = 1e-6
NEG_INF = -1e30
LANE = 128
SHARD = 3136
HALF_TILE = 64
N_CHIPS = 4
VMEM_LIMIT = 56 * 1024 * 1024

ADAM_LR = 0.001
ADAM_B1 = 0.9
ADAM_B2 = 0.999
ADAM_EPS = 1e-08
ADAM_WD = 0.01
ADAM_STEP = 10

GROUP_TILES = {"A": 16, "B": 18, "C": 16, "D": 48}
GROUPS = ("A", "B", "C", "D")


def _params(sem=None):
    return pltpu.CompilerParams(dimension_semantics=sem, vmem_limit_bytes=VMEM_LIMIT)


def _sigmoid(v):
    return jax.nn.sigmoid(v)


def _tile_home(t):
    if t < 16:
        return "A", t
    if t < 24:
        return "B", t - 16
    if t < 26:
        return "B", t - 24 + 16
    if t < 34:
        return "B", t - 26 + 8
    if t < 50:
        return "C", t - 34
    return "D", t - 50


def _shard_runs(j):
    runs = []
    per_shard = SHARD // HALF_TILE
    for q in range(per_shard * j, per_shard * (j + 1)):
        g, gt = _tile_home(q // 2)
        row = gt * LANE + (q % 2) * HALF_TILE
        if runs and runs[-1][2] == g and runs[-1][3] + runs[-1][1] == row:
            runs[-1][1] += HALF_TILE
        else:
            runs.append([(q - per_shard * j) * HALF_TILE, HALF_TILE, g, row])
    return [tuple(r) for r in runs]


_DIMS = {
    "nn": (((1,), (0,)), ((), ())),
    "nt": (((1,), (1,)), ((), ())),
    "tn": (((0,), (0,)), ((), ())),
}


def _mm(a, b, mode, out_dtype, tm, tn, tk, name, acc=None, after=None):
    if mode == "nn":
        (m, k), n = a.shape, b.shape[1]
    elif mode == "nt":
        (m, k), n = a.shape, b.shape[0]
    else:
        (k, m), n = a.shape, b.shape[1]
    tm, tn, tk = min(tm, m), min(tn, n), min(tk, k)
    assert m % tm == 0 and n % tn == 0 and k % tk == 0, (name, m, n, k)
    nk = k // tk
    has_acc = acc is not None

    def body(*refs):
        a_ref, b_ref = refs[0], refs[1]
        o_ref = refs[3] if has_acc else refs[2]
        p = lax.dot_general(a_ref[...], b_ref[...], _DIMS[mode], preferred_element_type=F32)

        def finish(v):
            if has_acc:
                v = v + refs[2][...]
            o_ref[...] = v.astype(out_dtype)

        if nk == 1:
            finish(p)
        else:
            s_ref = refs[-1]
            kk = pl.program_id(2)

            @pl.when(kk == 0)
            def _():
                s_ref[...] = p

            @pl.when(kk > 0)
            def _():
                s_ref[...] += p

            @pl.when(kk == nk - 1)
            def _():
                finish(s_ref[...])

    if mode == "nn":
        a_spec = pl.BlockSpec((tm, tk), lambda i, j, kk: (i, kk))
        b_spec = pl.BlockSpec((tk, tn), lambda i, j, kk: (kk, j))
    elif mode == "nt":
        a_spec = pl.BlockSpec((tm, tk), lambda i, j, kk: (i, kk))
        b_spec = pl.BlockSpec((tn, tk), lambda i, j, kk: (j, kk))
    else:
        a_spec = pl.BlockSpec((tk, tm), lambda i, j, kk: (kk, i))
        b_spec = pl.BlockSpec((tk, tn), lambda i, j, kk: (kk, j))
    o_spec = pl.BlockSpec((tm, tn), lambda i, j, kk: (i, j))
    in_specs = [a_spec, b_spec] + ([o_spec] if has_acc else [])
    args = (a, b) + ((acc,) if has_acc else ())
    if after is not None:
        in_specs.append(pl.BlockSpec(memory_space=pl.ANY))
        args += (after,)
    n_in = len(args)
    kernel_body = body

    def body(*refs):
        kernel_body(*(refs[:n_in - (after is not None)] + refs[n_in:]))

    return pl.pallas_call(
        body,
        name=name,
        grid=(m // tm, n // tn, nk),
        in_specs=in_specs,
        out_specs=o_spec,
        out_shape=jax.ShapeDtypeStruct((m, n), out_dtype),
        scratch_shapes=[pltpu.VMEM((tm, tn), F32)] if nk > 1 else [],
        compiler_params=_params(("parallel", "parallel", "arbitrary")),
    )(*args)


def _rms_fwd(x, g, name, ts=256):
    r, d = x.shape

    def body(x_ref, g_ref, o_ref):
        xv = x_ref[...]
        inv = lax.rsqrt(jnp.mean(xv * xv, axis=-1, keepdims=True) + EPS)
        o_ref[...] = (xv * inv * g_ref[...]).astype(BF16)

    return pl.pallas_call(
        body,
        name=name,
        grid=(r // ts,),
        in_specs=[pl.BlockSpec((ts, d), lambda i: (i, 0)), pl.BlockSpec((1, d), lambda i: (0, 0))],
        out_specs=pl.BlockSpec((ts, d), lambda i: (i, 0)),
        out_shape=jax.ShapeDtypeStruct((r, d), BF16),
        compiler_params=_params(("parallel",)),
    )(x, g)


def _post_loss(out, x, tgt, g_post, ts=256):
    n = S // ts

    def body(o_ref, x_ref, t_ref, g_ref, sq_ref, dy_ref, do_ref, dg_ref):
        i = pl.program_id(0)

        @pl.when(i == 0)
        def _():
            sq_ref[...] = jnp.zeros_like(sq_ref)
            dg_ref[...] = jnp.zeros_like(dg_ref)

        ov = o_ref[...]
        g = g_ref[...]
        inv = lax.rsqrt(jnp.mean(ov * ov, axis=-1, keepdims=True) + EPS)
        on = ov * inv
        err = x_ref[...] + on * g - t_ref[...]
        sq_ref[...] += jnp.sum(err * err)
        dy = err * (1.0 / D)
        dy_ref[...] = dy
        dg_ref[...] += jnp.sum(dy * on, axis=0, keepdims=True)
        don = dy * g
        do_ref[...] = (inv * (don - on * jnp.mean(don * on, axis=-1, keepdims=True))).astype(BF16)

    row = pl.BlockSpec((ts, D), lambda i: (i, 0))
    vec = pl.BlockSpec((1, D), lambda i: (0, 0))
    return pl.pallas_call(
        body,
        name="post_loss",
        grid=(n,),
        in_specs=[row, row, row, vec],
        out_specs=[pl.BlockSpec((8, LANE), lambda i: (0, 0)), row, row, vec],
        out_shape=[
            jax.ShapeDtypeStruct((8, LANE), F32),
            jax.ShapeDtypeStruct((S, D), F32),
            jax.ShapeDtypeStruct((S, D), BF16),
            jax.ShapeDtypeStruct((1, D), F32),
        ],
        compiler_params=_params(("arbitrary",)),
    )(out, x, tgt, g_post)


def _pre_bwd(dh, x, dy, g_pre, ts=256):
    n = S // ts

    def body(dh_ref, x_ref, dy_ref, g_ref, gx_ref, dg_ref):
        i = pl.program_id(0)

        @pl.when(i == 0)
        def _():
            dg_ref[...] = jnp.zeros_like(dg_ref)

        xv = x_ref[...]
        dhv = dh_ref[...]
        inv = lax.rsqrt(jnp.mean(xv * xv, axis=-1, keepdims=True) + EPS)
        xn = xv * inv
        dg_ref[...] += jnp.sum(dhv * xn, axis=0, keepdims=True)
        dxn = dhv * g_ref[...]
        gx_ref[...] = dy_ref[...] + inv * (dxn - xn * jnp.mean(dxn * xn, axis=-1, keepdims=True))

    row = pl.BlockSpec((ts, D), lambda i: (i, 0))
    vec = pl.BlockSpec((1, D), lambda i: (0, 0))
    return pl.pallas_call(
        body,
        name="pre_bwd",
        grid=(n,),
        in_specs=[row, row, row, vec],
        out_specs=[row, vec],
        out_shape=[jax.ShapeDtypeStruct((S, D), F32), jax.ShapeDtypeStruct((1, D), F32)],
        compiler_params=_params(("arbitrary",)),
    )(dh, x, dy, g_pre)


def _memnorm_bwd(dmemn, mem):
    def body(d_ref, m_ref, dg_ref):
        mv = m_ref[...]
        inv = lax.rsqrt(jnp.mean(mv * mv, axis=-1, keepdims=True) + EPS)
        dg_ref[...] = jnp.sum(d_ref[...] * mv * inv, axis=0, keepdims=True)

    return pl.pallas_call(
        body,
        name="memnorm_bwd",
        out_shape=jax.ShapeDtypeStruct((1, D), F32),
        compiler_params=_params(),
    )(dmemn, mem)


T_RNN = 256


def _neg_expm1(z):
    poly = -z * (1.0 + z * (0.5 + z * (1.0 / 6 + z * (1.0 / 24 + z * (1.0 / 120 + z * (1.0 / 720))))))
    return jnp.where(z > -0.1, poly, 1.0 - jnp.exp(z))


def _softplus_neg(lam):
    return jnp.maximum(-lam, 0.0) + jnp.log1p(jnp.exp(-jnp.abs(lam)))


def _rnn_gates(conv, wa_ref, ba, wx_ref, bx, lam, first_row):
    cbf = conv.astype(BF16)
    ga, gx = [], []
    for n in range(RNN_BLOCKS):
        c_n = cbf[:, n * LANE:(n + 1) * LANE]
        ga.append(jnp.dot(c_n, wa_ref[n], preferred_element_type=F32))
        gx.append(jnp.dot(c_n, wx_ref[n], preferred_element_type=F32))
    gate_r = _sigmoid(jnp.concatenate(ga, axis=1) + ba)
    gate_i = _sigmoid(jnp.concatenate(gx, axis=1) + bx)
    sp = _softplus_neg(lam)
    log_a = -LRU_C * gate_r * sp
    a = jnp.exp(log_a)
    mult_raw = jnp.sqrt(_neg_expm1(2.0 * log_a))
    mult = jnp.where(first_row, 1.0, mult_raw)
    return cbf, gate_r, gate_i, sp, a, mult_raw, mult


def _rglru_fwd(p_a, conv_w, conv_b, wa, ba, wx, bx, lam):
    t = T_RNN
    n = S // t

    def body(xr_ref, g_ref, cw_ref, cb_ref, wa_ref, ba_ref, wx_ref, bx_ref, lam_ref,
             y_ref, h_ref, xp_s, hcar, a_s, b_s):
        i = pl.program_id(0)

        @pl.when(i == 0)
        def _():
            xp_s[0:8, :] = jnp.zeros((8, D_RNN), F32)
            hcar[...] = jnp.zeros_like(hcar)

        @pl.when(i > 0)
        def _():
            xp_s[0:8, :] = xp_s[t:t + 8, :]

        xp_s[8:8 + t, :] = xr_ref[...]
        conv = cb_ref[...]
        for k in range(CONV_W):
            conv = conv + cw_ref[k:k + 1, :] * xp_s[8 - k:8 - k + t, :]
        rows = i * t + lax.broadcasted_iota(jnp.int32, (t, 1), 0)
        _, _, gate_i, _, a, _, mult = _rnn_gates(
            conv, wa_ref, ba_ref[...], wx_ref, bx_ref[...], lam_ref[...], rows == 0)
        a_s[...] = a
        b_s[...] = mult * gate_i * conv

        def step(tt, h):
            h = a_s[pl.ds(tt, 1), :] * h + b_s[pl.ds(tt, 1), :]
            h_ref[pl.ds(tt, 1), :] = h
            return h

        hcar[...] = lax.fori_loop(0, t, step, hcar[...], unroll=8)
        g = g_ref[...]
        y_ref[...] = (h_ref[...] * (g * _sigmoid(g))).astype(BF16)

    blk = lambda c: pl.BlockSpec((t, D_RNN), lambda i: (i, c))
    full = lambda shape: pl.BlockSpec(shape, lambda i: (0,) * len(shape))
    return pl.pallas_call(
        body,
        name="rglru_fwd",
        grid=(n,),
        in_specs=[blk(0), blk(1), full((CONV_W, D_RNN)), full((1, D_RNN)),
                  full((RNN_BLOCKS, LANE, LANE)), full((1, D_RNN)),
                  full((RNN_BLOCKS, LANE, LANE)), full((1, D_RNN)), full((1, D_RNN))],
        out_specs=[blk(0), blk(0)],
        out_shape=[jax.ShapeDtypeStruct((S, D_RNN), BF16), jax.ShapeDtypeStruct((S, D_RNN), F32)],
        scratch_shapes=[pltpu.VMEM((t + 8, D_RNN), F32), pltpu.VMEM((1, D_RNN), F32),
                        pltpu.VMEM((t, D_RNN), F32), pltpu.VMEM((t, D_RNN), F32)],
        compiler_params=_params(("arbitrary",)),
    )(p_a, p_a, conv_w, conv_b, wa, ba, wx, bx, lam)


def _rglru_bwd(dy, p_a, hseq, conv_w, conv_b, wa, ba, wx, bx, lam):
    t = T_RNN
    n = S // t
    rb = t // 8

    def body(dy_ref, xr_ref, g_ref, h_ref, xrp_ref, hp_ref, cw_ref, cb_ref, wa_ref, ba_ref, wx_ref, bx_ref, lam_ref,
             dp_ref, dcw_ref, dcb_ref, dwa_ref, dba_ref, dwx_ref, dbx_ref, dlam_ref,
             xp_s, hp_s, dxp_s, lamcar, a_s, dh_s, lam_s):
        i = pl.program_id(0)
        r = n - 1 - i

        @pl.when(i == 0)
        def _():
            for ref in (dcw_ref, dcb_ref, dwa_ref, dba_ref, dwx_ref, dbx_ref, dlam_ref, lamcar):
                ref[...] = jnp.zeros_like(ref)
            dxp_s[t:t + 8, :] = jnp.zeros((8, D_RNN), F32)

        @pl.when(i > 0)
        def _():
            dxp_s[t:t + 8, :] = dxp_s[0:8, :]

        has_prev = r > 0
        xp_s[0:8, :] = jnp.where(has_prev, xrp_ref[...], 0.0)
        xp_s[8:8 + t, :] = xr_ref[...]
        hp_s[0:8, :] = jnp.where(has_prev, hp_ref[...], 0.0)
        hp_s[8:8 + t, :] = h_ref[...]
        xs = [xp_s[8 - k:8 - k + t, :] for k in range(CONV_W)]
        conv = cb_ref[...]
        for k in range(CONV_W):
            conv = conv + cw_ref[k:k + 1, :] * xs[k]
        rows = r * t + lax.broadcasted_iota(jnp.int32, (t, 1), 0)
        first = rows == 0
        lam_p = lam_ref[...]
        cbf, gate_r, gate_i, sp, a, mult_raw, mult = _rnn_gates(
            conv, wa_ref, ba_ref[...], wx_ref, bx_ref[...], lam_p, first)

        g = g_ref[...]
        sg = _sigmoid(g)
        dyv = dy_ref[...]
        a_s[...] = a
        dh_s[...] = dyv * (g * sg)
        dg = dyv * h_ref[...] * (sg * (1.0 + g * (1.0 - sg)))

        def step(jj, car):
            tt = t - 1 - jj
            lm = dh_s[pl.ds(tt, 1), :] + car
            lam_s[pl.ds(tt, 1), :] = lm
            return a_s[pl.ds(tt, 1), :] * lm

        lamcar[...] = lax.fori_loop(0, t, step, lamcar[...], unroll=8)
        db = lam_s[...]
        da = db * hp_s[7:7 + t, :]
        dmult = db * gate_i * conv
        dgate_i = db * mult * conv
        dconv = db * mult * gate_i
        dlog_a = da * a + jnp.where(first, 0.0, dmult * (-(a * a) / mult_raw))
        dgate_r = dlog_a * (-LRU_C * sp)
        dsp = jnp.sum(dlog_a * (-LRU_C * gate_r), axis=0, keepdims=True)
        dlam_ref[...] += dsp * (-_sigmoid(-lam_p))
        dga = dgate_r * gate_r * (1.0 - gate_r)
        dgx = dgate_i * gate_i * (1.0 - gate_i)
        dba_ref[...] += jnp.sum(dga, axis=0, keepdims=True)
        dbx_ref[...] += jnp.sum(dgx, axis=0, keepdims=True)
        dga16, dgx16 = dga.astype(BF16), dgx.astype(BF16)
        back = []
        for nb in range(RNN_BLOCKS):
            sl = slice(nb * LANE, (nb + 1) * LANE)
            dwa_ref[nb] += lax.dot_general(cbf[:, sl], dga16[:, sl], _DIMS["tn"], preferred_element_type=F32)
            dwx_ref[nb] += lax.dot_general(cbf[:, sl], dgx16[:, sl], _DIMS["tn"], preferred_element_type=F32)
            back.append(lax.dot_general(dga16[:, sl], wa_ref[nb], _DIMS["nt"], preferred_element_type=F32)
                        + lax.dot_general(dgx16[:, sl], wx_ref[nb], _DIMS["nt"], preferred_element_type=F32))
        dconv = dconv + jnp.concatenate(back, axis=1)
        dcb_ref[...] += jnp.sum(dconv, axis=0, keepdims=True)
        for k in range(CONV_W):
            dcw_ref[k:k + 1, :] += jnp.sum(dconv * xs[k], axis=0, keepdims=True)
        dxp_s[0:t, :] = dconv
        dxr = cw_ref[0:1, :] * dconv
        for k in range(1, CONV_W):
            dxr = dxr + cw_ref[k:k + 1, :] * dxp_s[k:k + t, :]
        dp_ref[:, 0:D_RNN] = dxr.astype(BF16)
        dp_ref[:, D_RNN:2 * D_RNN] = dg.astype(BF16)

    blk = lambda c: pl.BlockSpec((t, D_RNN), lambda i: (n - 1 - i, c))
    prev8 = pl.BlockSpec((8, D_RNN), lambda i: (jnp.maximum((n - 1 - i) * rb - 1, 0), 0))
    full = lambda shape: pl.BlockSpec(shape, lambda i: (0,) * len(shape))
    vec = full((1, D_RNN))
    mat = full((RNN_BLOCKS, LANE, LANE))
    return pl.pallas_call(
        body,
        name="rglru_bwd",
        grid=(n,),
        in_specs=[blk(0), blk(0), blk(1), blk(0), prev8, prev8,
                  full((CONV_W, D_RNN)), vec, mat, vec, mat, vec, vec],
        out_specs=[pl.BlockSpec((t, 2 * D_RNN), lambda i: (n - 1 - i, 0)),
                   full((CONV_W, D_RNN)), vec, mat, vec, mat, vec, vec],
        out_shape=[jax.ShapeDtypeStruct((S, 2 * D_RNN), BF16),
                   jax.ShapeDtypeStruct((CONV_W, D_RNN), F32), jax.ShapeDtypeStruct((1, D_RNN), F32),
                   jax.ShapeDtypeStruct((RNN_BLOCKS, LANE, LANE), F32), jax.ShapeDtypeStruct((1, D_RNN), F32),
                   jax.ShapeDtypeStruct((RNN_BLOCKS, LANE, LANE), F32), jax.ShapeDtypeStruct((1, D_RNN), F32),
                   jax.ShapeDtypeStruct((1, D_RNN), F32)],
        scratch_shapes=[pltpu.VMEM((t + 8, D_RNN), F32), pltpu.VMEM((t + 8, D_RNN), F32),
                        pltpu.VMEM((t + 8, D_RNN), F32), pltpu.VMEM((1, D_RNN), F32),
                        pltpu.VMEM((t, D_RNN), F32), pltpu.VMEM((t, D_RNN), F32), pltpu.VMEM((t, D_RNN), F32)],
        compiler_params=_params(("arbitrary",)),
    )(dy, p_a, p_a, hseq, p_a, hseq, conv_w, conv_b, wa, ba, wx, bx, lam)


QB = WINDOW
KB2 = 2 * WINDOW
N_QB = S // QB
N_PAIR = SWA_HEADS // 2


def _swa_keys(kvc_ref, kvp_ref):
    kk = jnp.concatenate([kvp_ref[:, 0:LANE], kvc_ref[:, 0:LANE]], axis=0)
    vv = jnp.concatenate([kvp_ref[:, LANE:2 * LANE], kvc_ref[:, LANE:2 * LANE]], axis=0)
    lo = lax.broadcasted_iota(jnp.int32, (1, LANE), 1) < SWA_HD
    kk_sw, vv_sw = pltpu.roll(kk, SWA_HD, 1), pltpu.roll(vv, SWA_HD, 1)
    kd = [jnp.where(lo, kk, kk_sw).astype(BF16), jnp.where(lo, kk_sw, kk).astype(BF16)]
    vd = [jnp.where(lo, vv, vv_sw).astype(BF16), jnp.where(lo, vv_sw, vv).astype(BF16)]
    return lo, kd, vd


def _swa_valid(n):
    qi = lax.broadcasted_iota(jnp.int32, (QB, KB2), 0)
    kj = lax.broadcasted_iota(jnp.int32, (QB, KB2), 1)
    dist = qi + WINDOW - kj
    return (dist >= 0) & (dist < WINDOW) & ((n > 0) | (kj >= WINDOW))


def _swa_probs(qh16, kd, bias, sink, valid):
    lg = lax.dot_general(qh16, kd, _DIMS["nt"], preferred_element_type=F32) * (SWA_HD ** -0.5) + bias
    lg = jnp.where(valid, lg, NEG_INF)
    m = jnp.maximum(jnp.max(lg, axis=-1, keepdims=True), sink)
    p = jnp.exp(lg - m)
    es = jnp.exp(sink - m)
    den = jnp.sum(p, axis=-1, keepdims=True) + es
    return p / den, es / den


def _swa_specs():
    q = pl.BlockSpec((QB, D_RNN), lambda n: (n, 0))
    g = pl.BlockSpec((QB, D_RNN), lambda n: (n, 1))
    kvc = pl.BlockSpec((QB, 2 * LANE), lambda n: (n, 8))
    kvp = pl.BlockSpec((QB, 2 * LANE), lambda n: (jnp.maximum(n - 1, 0), 8))
    bias = pl.BlockSpec((SWA_HEADS, QB, KB2), lambda n: (0, 0, 0))
    sinks = pl.BlockSpec(memory_space=pltpu.SMEM)
    return q, g, kvc, kvp, bias, sinks


def _swa_fwd(p_b, bias_t, sinks):
    def body(q_ref, g_ref, kvc_ref, kvp_ref, bias_ref, sink_ref, y_ref, o_ref):
        n = pl.program_id(0)
        lo, kd, vd = _swa_keys(kvc_ref, kvp_ref)
        valid = _swa_valid(n)
        for hp in range(N_PAIR):
            sl = slice(hp * LANE, (hp + 1) * LANE)
            kvh = hp // (N_PAIR // 2)
            q = q_ref[:, sl]
            outs = []
            for j in range(2):
                mh = lo if j == 0 else jnp.logical_not(lo)
                qh16 = jnp.where(mh, q, 0.0).astype(BF16)
                probs, _ = _swa_probs(qh16, kd[kvh], bias_ref[2 * hp + j], sink_ref[2 * hp + j], valid)
                outs.append(jnp.dot(probs.astype(BF16), vd[kvh], preferred_element_type=F32))
            o = jnp.where(lo, outs[0], outs[1])
            o_ref[:, sl] = o
            g = g_ref[:, sl]
            y_ref[:, sl] = (o * (g * _sigmoid(g))).astype(BF16)

    q, g, kvc, kvp, bias, sinks_spec = _swa_specs()
    out = pl.BlockSpec((QB, D_RNN), lambda n: (n, 0))
    return pl.pallas_call(
        body,
        name="swa_fwd",
        grid=(N_QB,),
        in_specs=[q, g, kvc, kvp, bias, sinks_spec],
        out_specs=[out, out],
        out_shape=[jax.ShapeDtypeStruct((S, D_RNN), BF16), jax.ShapeDtypeStruct((S, D_RNN), F32)],
        compiler_params=_params(("parallel",)),
    )(p_b, p_b, p_b, p_b, bias_t, sinks)


def _swa_bwd(dy, p_b, o_swa, bias_t, sinks, after=None):
    def body(dy_ref, q_ref, g_ref, kvc_ref, kvp_ref, o_ref, bias_ref, sink_ref, *rest):
        dp_ref, dk_ref, dv_ref, dbias_ref, dsink_ref = rest[-5:]
        n = pl.program_id(0)

        @pl.when(n == 0)
        def _():
            for ref in (dk_ref, dv_ref, dbias_ref, dsink_ref):
                ref[...] = jnp.zeros_like(ref)

        lo, kd, vd = _swa_keys(kvc_ref, kvp_ref)
        hi = jnp.logical_not(lo)
        valid = _swa_valid(n)
        dk_blk = jnp.zeros((KB2, LANE), F32)
        dv_blk = jnp.zeros((KB2, LANE), F32)
        for kvh in range(2):
            dk_pair = jnp.zeros((KB2, LANE), F32)
            dv_pair = jnp.zeros((KB2, LANE), F32)
            for hp in range(kvh * (N_PAIR // 2), (kvh + 1) * (N_PAIR // 2)):
                sl = slice(hp * LANE, (hp + 1) * LANE)
                q = q_ref[:, sl]
                g = g_ref[:, sl]
                o = o_ref[:, sl]
                dyv = dy_ref[:, sl]
                sg = _sigmoid(g)
                do = dyv * (g * sg)
                dp_ref[:, D_RNN + hp * LANE:D_RNN + (hp + 1) * LANE] = (
                    dyv * o * (sg * (1.0 + g * (1.0 - sg)))).astype(BF16)
                dqs = []
                for j in range(2):
                    h = 2 * hp + j
                    mh = lo if j == 0 else hi
                    qh16 = jnp.where(mh, q, 0.0).astype(BF16)
                    sink = sink_ref[h]
                    probs, psink = _swa_probs(qh16, kd[kvh], bias_ref[h], sink, valid)
                    doh = jnp.where(mh, do, 0.0)
                    doh16 = doh.astype(BF16)
                    delta = jnp.sum(doh * o, axis=-1, keepdims=True)
                    dpr = lax.dot_general(doh16, vd[kvh], _DIMS["nt"], preferred_element_type=F32)
                    ds = probs * (dpr - delta)
                    dbias_ref[h] += ds
                    dsink_ref[h:h + 1, :] += jnp.zeros((1, LANE), F32) - jnp.sum(psink * delta)
                    ds16 = (ds * (SWA_HD ** -0.5)).astype(BF16)
                    dqs.append(jnp.dot(ds16, kd[kvh], preferred_element_type=F32))
                    dk_pair = dk_pair + lax.dot_general(ds16, qh16, _DIMS["tn"], preferred_element_type=F32)
                    dv_pair = dv_pair + lax.dot_general(probs.astype(BF16), doh16, _DIMS["tn"],
                                                        preferred_element_type=F32)
                dp_ref[:, sl] = jnp.where(lo, dqs[0], dqs[1]).astype(BF16)
            keep = lo if kvh == 0 else hi
            dk_blk = dk_blk + jnp.where(keep, dk_pair + pltpu.roll(dk_pair, SWA_HD, 1), 0.0)
            dv_blk = dv_blk + jnp.where(keep, dv_pair + pltpu.roll(dv_pair, SWA_HD, 1), 0.0)

        cur = pl.ds(pl.multiple_of(n * QB, QB), QB)
        dk_ref[cur, :] += dk_blk[QB:KB2]
        dv_ref[cur, :] += dv_blk[QB:KB2]

        @pl.when(n > 0)
        def _():
            prev = pl.ds(pl.multiple_of((n - 1) * QB, QB), QB)
            dk_ref[prev, :] += dk_blk[0:QB]
            dv_ref[prev, :] += dv_blk[0:QB]

    q, g, kvc, kvp, bias, sinks_spec = _swa_specs()
    row = pl.BlockSpec((QB, D_RNN), lambda n: (n, 0))
    acc = pl.BlockSpec((S, LANE), lambda n: (0, 0))
    return pl.pallas_call(
        body,
        name="swa_bwd",
        grid=(N_QB,),
        in_specs=[row, q, g, kvc, kvp, row, bias, sinks_spec] + ([ANY] if after is not None else []),
        out_specs=[pl.BlockSpec((QB, 2 * D_RNN), lambda n: (n, 0)), acc, acc, bias,
                   pl.BlockSpec((SWA_HEADS, LANE), lambda n: (0, 0))],
        out_shape=[jax.ShapeDtypeStruct((S, GROUP_TILES["B"] * LANE), BF16),
                   jax.ShapeDtypeStruct((S, LANE), F32), jax.ShapeDtypeStruct((S, LANE), F32),
                   jax.ShapeDtypeStruct((SWA_HEADS, QB, KB2), F32),
                   jax.ShapeDtypeStruct((SWA_HEADS, LANE), F32)],
        compiler_params=_params(("arbitrary",)),
    )(dy, p_b, p_b, p_b, p_b, o_swa, bias_t, sinks, *([after] if after is not None else []))


def _swa_pack(dp_b, dk, dv, ts=512):
    def body(_, dk_ref, dv_ref, o_ref):
        o_ref[:, 0:LANE] = dk_ref[...].astype(BF16)
        o_ref[:, LANE:2 * LANE] = dv_ref[...].astype(BF16)

    tile = pl.BlockSpec((ts, LANE), lambda i: (i, 0))
    return pl.pallas_call(
        body,
        name="swa_pack",
        grid=(S // ts,),
        in_specs=[pl.BlockSpec(memory_space=pl.ANY), tile, tile],
        out_specs=pl.BlockSpec((ts, 2 * LANE), lambda i: (i, 8)),
        out_shape=jax.ShapeDtypeStruct(dp_b.shape, dp_b.dtype),
        input_output_aliases={0: 0},
        compiler_params=_params(("parallel",)),
    )(dp_b, dk, dv)


def _split3(v):
    a = v.astype(BF16)
    r = v - a.astype(F32)
    b = r.astype(BF16)
    c = (r - b.astype(F32)).astype(BF16)
    return a, b, c


def _relbias_grad(dbias_flat, onehot_t):
    def body(d_ref, e_ref, o_ref):
        e = e_ref[...]
        acc = jnp.zeros((SWA_HEADS, REL_BUCKETS), F32)
        for term in _split3(d_ref[...]):
            acc = acc + lax.dot_general(term, e, _DIMS["nt"], preferred_element_type=F32)
        o_ref[...] = acc

    return pl.pallas_call(
        body,
        name="relbias_grad",
        out_shape=jax.ShapeDtypeStruct((SWA_HEADS, REL_BUCKETS), F32),
        compiler_params=_params(),
    )(dbias_flat, onehot_t)


TS_MEM = 512


def _mem_probs(q16, mk):
    lg = lax.dot_general(q16, mk, _DIMS["nt"], preferred_element_type=F32) * (MEM_HD ** -0.5)
    p = jnp.exp(lg - jnp.max(lg, axis=-1, keepdims=True))
    return p / jnp.sum(p, axis=-1, keepdims=True)


def _mem_fwd(p_c, mkv):
    def body(q_ref, g_ref, mkv_ref, y_ref, o_ref):
        for hm in range(MEM_HEADS):
            sl = slice(hm * MEM_HD, (hm + 1) * MEM_HD)
            probs = _mem_probs(q_ref[:, sl].astype(BF16), mkv_ref[:, sl])
            o = jnp.dot(probs.astype(BF16), mkv_ref[:, D_RNN + hm * MEM_HD:D_RNN + (hm + 1) * MEM_HD],
                        preferred_element_type=F32)
            o_ref[:, sl] = o
            g = g_ref[:, sl]
            y_ref[:, sl] = (o * (g * _sigmoid(g))).astype(BF16)

    blk = lambda c: pl.BlockSpec((TS_MEM, D_RNN), lambda i: (i, c))
    return pl.pallas_call(
        body,
        name="mem_fwd",
        grid=(S // TS_MEM,),
        in_specs=[blk(0), blk(1), pl.BlockSpec((MEM, 2 * D_RNN), lambda i: (0, 0))],
        out_specs=[blk(0), blk(0)],
        out_shape=[jax.ShapeDtypeStruct((S, D_RNN), BF16), jax.ShapeDtypeStruct((S, D_RNN), F32)],
        compiler_params=_params(("parallel",)),
    )(p_c, p_c, mkv)


def _mem_bwd(dy, p_c, o_mem, mkv):
    def body(dy_ref, q_ref, g_ref, o_ref, mkv_ref, dp_ref, dmkv_ref):
        @pl.when(pl.program_id(0) == 0)
        def _():
            dmkv_ref[...] = jnp.zeros_like(dmkv_ref)

        for hm in range(MEM_HEADS):
            sl = slice(hm * MEM_HD, (hm + 1) * MEM_HD)
            sv = slice(D_RNN + hm * MEM_HD, D_RNN + (hm + 1) * MEM_HD)
            q16 = q_ref[:, sl].astype(BF16)
            mk, mv = mkv_ref[:, sl], mkv_ref[:, sv]
            probs = _mem_probs(q16, mk)
            g, o, dyv = g_ref[:, sl], o_ref[:, sl], dy_ref[:, sl]
            sg = _sigmoid(g)
            do = dyv * (g * sg)
            dp_ref[:, sv] = (dyv * o * (sg * (1.0 + g * (1.0 - sg)))).astype(BF16)
            do16 = do.astype(BF16)
            delta = jnp.sum(do * o, axis=-1, keepdims=True)
            dpr = lax.dot_general(do16, mv, _DIMS["nt"], preferred_element_type=F32)
            ds16 = (probs * (dpr - delta) * (MEM_HD ** -0.5)).astype(BF16)
            dp_ref[:, sl] = jnp.dot(ds16, mk, preferred_element_type=F32).astype(BF16)
            dmkv_ref[:, sl] += lax.dot_general(ds16, q16, _DIMS["tn"], preferred_element_type=F32)
            dmkv_ref[:, sv] += lax.dot_general(probs.astype(BF16), do16, _DIMS["tn"], preferred_element_type=F32)

    blk = lambda c: pl.BlockSpec((TS_MEM, D_RNN), lambda i: (i, c))
    kv = pl.BlockSpec((MEM, 2 * D_RNN), lambda i: (0, 0))
    return pl.pallas_call(
        body,
        name="mem_bwd",
        grid=(S // TS_MEM,),
        in_specs=[blk(0), blk(0), blk(1), blk(0), kv],
        out_specs=[pl.BlockSpec((TS_MEM, 2 * D_RNN), lambda i: (i, 0)), kv],
        out_shape=[jax.ShapeDtypeStruct((S, 2 * D_RNN), BF16), jax.ShapeDtypeStruct((MEM, 2 * D_RNN), F32)],
        compiler_params=_params(("arbitrary",)),
    )(dy, p_c, p_c, o_mem, mkv)


TS_MRG = 512
TD_MRG = 512
N_DBLK = D // TD_MRG


def _merge_fwd(z, p_d):
    def body(z0, z1, z2, g0, g1, g2, o_ref):
        o_ref[...] = (_sigmoid(g0[...]) * z0[...] + _sigmoid(g1[...]) * z1[...]
                      + _sigmoid(g2[...]) * z2[...]).astype(BF16)

    blk = pl.BlockSpec((TS_MRG, TD_MRG), lambda i, d: (i, d))
    gate = lambda b: pl.BlockSpec((TS_MRG, TD_MRG), lambda i, d: (i, b * N_DBLK + d))
    return pl.pallas_call(
        body,
        name="merge_fwd",
        grid=(S // TS_MRG, N_DBLK),
        in_specs=[blk, blk, blk, gate(0), gate(1), gate(2)],
        out_specs=blk,
        out_shape=jax.ShapeDtypeStruct((S, D), BF16),
        compiler_params=_params(("parallel", "parallel")),
    )(z[0], z[1], z[2], p_d, p_d, p_d)


def _merge_bwd(dmerged, z_b, p_d, b, dp_d, after=None):
    def body(dm_ref, z_ref, g_ref, *refs):
        dz_ref, dg_ref = refs[-2], refs[-1]
        sg = _sigmoid(g_ref[...])
        dm = dm_ref[...]
        dz_ref[...] = (dm * sg).astype(BF16)
        dg_ref[...] = (dm * z_ref[...] * sg * (1.0 - sg)).astype(BF16)

    blk = pl.BlockSpec((TS_MRG, TD_MRG), lambda i, d: (i, d))
    gate = pl.BlockSpec((TS_MRG, TD_MRG), lambda i, d: (i, b * N_DBLK + d))
    in_specs = [blk, blk, gate]
    args = [dmerged, z_b, p_d]
    aliases = {}
    if dp_d is not None:
        in_specs.append(pl.BlockSpec(memory_space=pl.ANY))
        args.append(dp_d)
        aliases = {3: 1}
    if after is not None:
        in_specs.append(pl.BlockSpec(memory_space=pl.ANY))
        args.append(after)
    return pl.pallas_call(
        body,
        name=f"merge_bwd{b}",
        grid=(S // TS_MRG, N_DBLK),
        in_specs=in_specs,
        out_specs=[blk, gate],
        out_shape=[jax.ShapeDtypeStruct((S, D), BF16),
                   jax.ShapeDtypeStruct((S, GROUP_TILES["D"] * LANE), BF16)],
        input_output_aliases=aliases,
        compiler_params=_params(("parallel", "parallel")),
    )(*args)


def _bucket_table():
    import numpy as np
    qi = np.arange(QB)[:, None]
    kj = np.arange(KB2)[None, :]
    n = np.maximum(qi + WINDOW - kj, 0)
    max_exact = REL_BUCKETS // 2
    ratio = np.log(np.maximum(n, 1).astype(np.float32) / max_exact) / np.float32(math.log(REL_MAX_DIST / max_exact))
    large = np.minimum(max_exact + (ratio * (REL_BUCKETS - max_exact)).astype(np.int32), REL_BUCKETS - 1)
    bucket = np.where(n < max_exact, n, large).reshape(1, QB * KB2)
    return (bucket == np.arange(REL_BUCKETS)[:, None]).astype(np.float32)


def _bias_expand(rel_bias_t, onehot_t):
    def body(r_ref, e_ref, o_ref):
        e = e_ref[...]
        acc = jnp.zeros((SWA_HEADS, QB * KB2), F32)
        for term in _split3(r_ref[...]):
            acc = acc + jnp.dot(term, e, preferred_element_type=F32)
        o_ref[...] = acc

    return pl.pallas_call(
        body,
        name="bias_expand",
        out_shape=jax.ShapeDtypeStruct((SWA_HEADS, QB * KB2), F32),
        compiler_params=_params(),
    )(rel_bias_t, onehot_t)


PROJ_TN = {"A": 1024, "B": 1152, "C": 1024, "D": 1536}


def _local_step(x, mem, tgt, sp, fetch, prefetch, emit, advance):
    onehot_t = jnp.asarray(_bucket_table(), BF16)
    bias_t = _bias_expand(sp["rel_bias"].T, onehot_t).reshape(SWA_HEADS, QB, KB2)
    sinks = sp["swa_sinks"].reshape(SWA_HEADS)
    wa16, wx16 = sp["w_rg_a"].astype(BF16), sp["w_rg_x"].astype(BF16)
    rnn = (sp["conv_w"], sp["conv_b"], wa16, sp["b_rg_a"], wx16, sp["b_rg_x"], sp["lru_lambda"])

    h = _rms_fwd(x, sp["pre_norm_g"], "rms_pre")
    memn = _rms_fwd(mem, sp["mem_norm_g"], "rms_mem")
    w_grp, p = {}, {}

    def project(g, after, then=None):
        (w_grp[g],) = fetch((g,), after)
        tok = prefetch(then, w_grp[g]) if then is not None else None
        p[g] = _mm(h, w_grp[g], "nt", F32, 1024, PROJ_TN[g], D, f"proj_{g}", after=tok)

    project("A", h)
    y_rg, hseq = _rglru_fwd(p["A"], *rnn)
    project("B", y_rg)
    y_swa, o_swa = _swa_fwd(p["B"], bias_t, sinks)
    project("C", y_swa, then=("mk",))
    (wmk,) = fetch(("mk",), p["C"])
    tok = prefetch(("br0", "br1", "br2"), wmk)
    mkv = _mm(memn, wmk, "nn", BF16, MEM, 1024, D, "mkv", after=tok)
    y_mem, o_mem = _mem_fwd(p["C"], mkv)
    ys = (y_rg, y_swa, y_mem)
    wbr = fetch(("br0", "br1", "br2"), y_mem)
    tok = prefetch(("D",), wbr[2])
    z = [_mm(ys[b], wbr[b], "nn", F32, 1024, 1024, D_RNN, f"branch_out{b}", after=tok if b == 0 else None)
         for b in range(3)]
    project("D", z[2], then=("out",))
    merged = _merge_fwd(z, p["D"])
    (wout,) = fetch(("out",), merged)
    out = _mm(merged, wout, "nn", F32, 1024, 1024, D, "out_proj")
    sq, dy, dout, d_post = _post_loss(out, x, tgt, sp["post_norm_g"])

    tok = emit({"out": _mm(merged, dout, "tn", BF16, 1024, 1024, S, "d_wout")})
    dmerged = _mm(dout, wout, "nt", F32, 1024, 1024, D, "d_merged", after=tok)
    dz, dp_d = [], None
    tok = advance(dmerged)
    for b in range(3):
        dz_b, dp_d = _merge_bwd(dmerged, z[b], p["D"], b, dp_d, after=tok if b == 0 else None)
        dz.append(dz_b)
    d_win = lambda g, dp_g, after=None: _mm(dp_g, h, "tn", BF16, PROJ_TN[g], 1024, S, f"d_win_{g}", after=after)
    tok = emit({f"br{b}": _mm(ys[b], dz[b], "tn", BF16, 1024, 1024, S, f"d_wbr{b}") for b in range(3)}, tok)
    d_w_d = d_win("D", dp_d, tok)
    tok = emit({"D": d_w_d}, advance(d_w_d))
    dy_mem = _mm(dz[2], wbr[2], "nt", F32, 1024, 1024, D, "d_branch2", after=tok)
    tok = advance(dy_mem)
    dp_c, dmkv = _mem_bwd(dy_mem, p["C"], o_mem, mkv)
    dmkv16 = dmkv.astype(BF16)
    tok = emit({"mk": _mm(memn, dmkv16, "tn", BF16, 1024, 1024, MEM, "d_wmk", after=tok), "C": d_win("C", dp_c)}, tok)
    dmemn = _mm(dmkv16, wmk, "nt", F32, MEM, 1024, D, "d_memn", after=tok)
    tok = advance(dmemn)
    d_memg = _memnorm_bwd(dmemn, mem)
    dy_rg = _mm(dz[0], wbr[0], "nt", F32, 1024, 1024, D, "d_branch0", after=tok)
    dp_a, d_cw, d_cb, d_wa, d_ba, d_wx, d_bx, d_lam = _rglru_bwd(dy_rg, p["A"], hseq, *rnn)
    tok = emit({"A": d_win("A", dp_a)}, tok)
    dy_swa = _mm(dz[1], wbr[1], "nt", F32, 1024, 1024, D, "d_branch1", after=tok)
    tok = advance(dy_swa)
    dp_b, dk, dv, d_bias, d_sink = _swa_bwd(dy_swa, p["B"], o_swa, bias_t, sinks, after=tok)
    dp_b = _swa_pack(dp_b, dk, dv)
    d_rel = _relbias_grad(d_bias.reshape(SWA_HEADS, QB * KB2), onehot_t).T
    dp = {"A": dp_a, "B": dp_b, "C": dp_c, "D": dp_d}
    tok = emit({"B": d_win("B", dp_b)}, tok)
    dh = None
    for g in GROUPS:
        dh = _mm(dp[g], w_grp[g], "nn", F32, 1024, 1024, 2304 if g == "B" else 2048, f"d_h_{g}", acc=dh,
                 after=tok if g in ("A", "B") else None)
        if g == "A":
            tok = advance(dh)
    grad_x, d_pre = _pre_bwd(dh, x, dy, sp["pre_norm_g"])

    d_small = {
        "pre_norm_g": d_pre, "post_norm_g": d_post, "mem_norm_g": d_memg, "conv_w": d_cw, "conv_b": d_cb,
        "w_rg_a": d_wa, "b_rg_a": d_ba, "w_rg_x": d_wx, "b_rg_x": d_bx, "lru_lambda": d_lam,
        "swa_sinks": d_sink[:, 0].reshape(1, SWA_HEADS), "rel_bias": d_rel,
    }
    return sq, grad_x, d_small


ANY = pl.BlockSpec(memory_space=pl.ANY)
SHARD_ROWS = D // N_CHIPS
GATHERED = {"A": (2048, D), "B": (2304, D), "C": (2048, D), "D": (6144, D), "mk": (D, D),
            "br0": (D_RNN, D), "br1": (D_RNN, D), "br2": (D_RNN, D), "out": (D, D)}
SHARD_SHAPES = {"win": (SHARD, D), "mk": (SHARD_ROWS, D), "br0": (D_RNN, SHARD_ROWS), "br1": (D_RNN, SHARD_ROWS),
                "br2": (D_RNN, SHARD_ROWS), "out": (SHARD_ROWS, D)}
SHARDS = tuple(SHARD_SHAPES)
HALF_AXIS = {"win": 1, "mk": 1, "br0": 0, "br1": 0, "br2": 0, "out": 1,
             "A": 1, "B": 1, "C": 1, "D": 1}


def _halved(shape, axis):
    return (shape[0] // 2, shape[1]) if axis == 0 else (shape[0], shape[1] // 2)


class Piece(NamedTuple):
    src: str
    dst: str
    rows: int
    sr0: int
    sc0: int
    dr0: int
    dc0: int
    ncols: int


def _pieces_of(jj):
    out = [Piece("win", g, n, r, 0, gr, 0, D) for r, n, g, gr in _shard_runs(jj)]
    out.append(Piece("mk", "mk", SHARD_ROWS, 0, 0, SHARD_ROWS * jj, 0, D))
    out += [Piece(f"br{b}", f"br{b}", D_RNN, 0, 0, 0, SHARD_ROWS * jj, SHARD_ROWS) for b in range(3)]
    out.append(Piece("out", "out", SHARD_ROWS, 0, 0, SHARD_ROWS * jj, 0, D))
    return out


def _half_rect(ref, p, side, which):
    r0, c0 = (p.sr0, p.sc0) if side == "src" else (p.dr0, p.dc0)
    if HALF_AXIS[p.src] == 1:
        return _rect(ref, r0, p.rows, c0 + which * (p.ncols // 2), p.ncols // 2)
    return _rect(ref, r0 + which * (p.rows // 2), p.rows // 2, c0, p.ncols)


def _rect_in_half(ref, p, side):
    r0, c0 = (p.sr0, p.sc0) if side == "src" else (p.dr0, p.dc0)
    if HALF_AXIS[p.src] == 1:
        return _rect(ref, r0, p.rows, 0, p.ncols // 2)
    return _rect(ref, 0, p.rows // 2, c0, p.ncols)


MAX_PIECES = max(len(_pieces_of(jj)) for jj in range(N_CHIPS))


def _rect(ref, r0, rows, c0, ncols):
    return ref.at[pl.ds(r0, rows), pl.ds(c0, ncols)]


def _position():
    x, y, c = lax.axis_index("x"), lax.axis_index("y"), lax.axis_index("c")
    return x, y, c, 2 * x + y


HBM = pl.BlockSpec(memory_space=pltpu.HBM)
SEM = pl.BlockSpec(memory_space=pltpu.SEMAPHORE)
EFFECT = pltpu.SideEffectType.DATAFLOW_SIDE_EFFECTING
N_SEM = MAX_PIECES * N_CHIPS
GATHER_STAGES = (("A",), ("B",), ("C",), ("mk",), ("br0", "br1", "br2"), ("D",), ("out",))


def _in_hbm(a):
    return pltpu.with_memory_space_constraint(a, pltpu.HBM)


def _stage_pieces(jj, stage):
    return [(i, p) for i, p in enumerate(_pieces_of(jj)) if p.dst in stage]


def _own_block_table(g):
    import numpy as np
    tbl = np.zeros((N_CHIPS, GATHERED[g][0] // HALF_TILE), np.int32)
    for jj in range(N_CHIPS):
        for r, n, grp, gr in _shard_runs(jj):
            if grp == g:
                for k in range(n // HALF_TILE):
                    tbl[jj, gr // HALF_TILE + k] = r // HALF_TILE + k
    return tbl


def _place_group(w_t, g, table, after):
    nb = GATHERED[g][0] // HALF_TILE

    def body(t_ref, x_ref, _, o_ref):
        o_ref[...] = x_ref[...].astype(BF16)

    return pl.pallas_call(
        body,
        name=f"place_{g}",
        grid_spec=pltpu.PrefetchScalarGridSpec(
            num_scalar_prefetch=1,
            grid=(nb,),
            in_specs=[pl.BlockSpec((HALF_TILE, D), lambda b, t: (t[b], 0)), ANY],
            out_specs=pl.BlockSpec((HALF_TILE, D), lambda b, t: (b, 0)),
        ),
        out_shape=jax.ShapeDtypeStruct(GATHERED[g], BF16),
        compiler_params=_params(("parallel",)),
    )(table, w_t, after)


def _place_shard(shard, name, after):
    rows, cols = shard.shape
    by_rows = HALF_AXIS[name] == 1

    def body(x_ref, _, o_ref):
        o_ref[...] = x_ref[...].astype(BF16)

    return pl.pallas_call(
        body,
        name=f"place_{name}",
        grid=(N_CHIPS,),
        in_specs=[pl.BlockSpec((rows, cols), lambda b: (0, 0)), ANY],
        out_specs=pl.BlockSpec((rows, cols), (lambda b: (b, 0)) if by_rows else (lambda b: (0, b))),
        out_shape=jax.ShapeDtypeStruct(GATHERED[name], BF16),
        compiler_params=_params(("parallel",)),
    )(shard, after)


def _gather_copy(arr, send_sems, recv_sems, c, jj, i, p, kk):
    rect = _half_rect(arr[p.dst], p, "dst", c)
    return pltpu.make_async_remote_copy(
        src_ref=rect, dst_ref=rect, send_sem=send_sems.at[i * N_CHIPS + kk],
        recv_sem=recv_sems.at[jj * MAX_PIECES + i], device_id=(kk // 2, kk % 2, c), device_id_type=MESH)


def _gather_start(arrays, after):
    stage = tuple(arrays)
    na = len(stage)

    def body(*refs):
        arr = dict(zip(stage, refs[:na]))
        send_sems, recv_sems = refs[na + 1], refs[na + 2]
        token = refs[-1]
        _, _, c, j = _position()
        for jj in range(N_CHIPS):
            @pl.when(j == jj)
            def _():
                for i, p in _stage_pieces(jj, stage):
                    for kk in range(N_CHIPS):
                        if kk != jj:
                            _gather_copy(arr, send_sems, recv_sems, c, jj, i, p, kk).start()
        token[...] = jnp.zeros_like(token)

    outs = pl.pallas_call(
        body,
        name=f"gather_start_{stage[0]}",
        in_specs=[HBM] * na + [ANY],
        out_specs=[SEM, SEM] + [HBM] * na + [pl.BlockSpec(memory_space=pltpu.VMEM)],
        out_shape=[pltpu.SemaphoreType.DMA((N_SEM,)), pltpu.SemaphoreType.DMA((N_SEM,))]
        + [pltpu.HBM(GATHERED[n], BF16) for n in stage] + [jax.ShapeDtypeStruct((8, LANE), F32)],
        input_output_aliases={k: 2 + k for k in range(na)},
        compiler_params=pltpu.CompilerParams(has_side_effects=EFFECT),
    )(*[_in_hbm(arrays[n]) for n in stage], after)
    return outs[0], outs[1], dict(zip(stage, outs[2:2 + na])), outs[-1]


def _gather_wait(send_sems, recv_sems, arrays, after):
    stage = tuple(arrays)
    na = len(stage)

    def body(*refs):
        arr = dict(zip(stage, refs[:na]))
        sems_s, sems_r = refs[na], refs[na + 1]
        _, _, c, j = _position()
        for jj in range(N_CHIPS):
            @pl.when(j != jj)
            def _():
                for i, p in _stage_pieces(jj, stage):
                    _gather_copy(arr, sems_s, sems_r, c, jj, i, p, jj).wait_recv()

            @pl.when(j == jj)
            def _():
                for i, p in _stage_pieces(jj, stage):
                    for kk in range(N_CHIPS):
                        if kk != jj:
                            _gather_copy(arr, sems_s, sems_r, c, jj, i, p, kk).wait_send()

    outs = pl.pallas_call(
        body,
        name=f"gather_wait_{stage[0]}",
        in_specs=[HBM] * na + [SEM, SEM, ANY],
        out_specs=[HBM] * na,
        out_shape=[pltpu.HBM(GATHERED[n], BF16) for n in stage],
        input_output_aliases={k: k for k in range(na)},
        compiler_params=pltpu.CompilerParams(has_side_effects=EFFECT),
    )(*[arrays[n] for n in stage], send_sems, recv_sems, after)
    return dict(zip(stage, outs))


def _gather_swap(arrays):
    stage = tuple(arrays)
    na = len(stage)

    def body(*refs):
        dst = dict(zip(stage, refs[na:2 * na]))
        send_sems, recv_sems = refs[2 * na:]
        x, y, c, j = _position()

        def fwd(jj, i, p, which):
            rect = _half_rect(dst[p.dst], p, "dst", which)
            return pltpu.make_async_remote_copy(
                src_ref=rect, dst_ref=rect, send_sem=send_sems.at[jj * MAX_PIECES + i],
                recv_sem=recv_sems.at[jj * MAX_PIECES + i], device_id=(x, y, 1 - c), device_id_type=MESH)

        for jj in range(N_CHIPS):
            @pl.when(j != jj)
            def _():
                for i, p in _stage_pieces(jj, stage):
                    fwd(jj, i, p, c).start()
        for jj in range(N_CHIPS):
            @pl.when(j != jj)
            def _():
                for i, p in _stage_pieces(jj, stage):
                    fwd(jj, i, p, 1 - c).wait_recv()
        for jj in range(N_CHIPS):
            @pl.when(j != jj)
            def _():
                for i, p in _stage_pieces(jj, stage):
                    fwd(jj, i, p, c).wait_send()

    outs = pl.pallas_call(
        body,
        name=f"gather_swap_{stage[0]}",
        in_specs=[ANY] * na,
        out_specs=[ANY] * na,
        out_shape=[jax.ShapeDtypeStruct(GATHERED[n], BF16) for n in stage],
        input_output_aliases={k: k for k in range(na)},
        scratch_shapes=[pltpu.SemaphoreType.DMA((N_SEM,)), pltpu.SemaphoreType.DMA((N_SEM,))],
        compiler_params=pltpu.CompilerParams(has_side_effects=True),
    )(*[arrays[n] for n in stage])
    return dict(zip(stage, outs))


def _pass_on_copy(arr, send_sems, recv_sems, x, y, c, jj, i, p, which):
    rect = _half_rect(arr[p.dst], p, "dst", which)
    return pltpu.make_async_remote_copy(
        src_ref=rect, dst_ref=rect, send_sem=send_sems.at[jj * MAX_PIECES + i],
        recv_sem=recv_sems.at[jj * MAX_PIECES + i], device_id=(x, y, 1 - c), device_id_type=MESH)


def _gather_pass_start(arrays, after):
    stage = tuple(arrays)
    na = len(stage)

    def body(*refs):
        arr = dict(zip(stage, refs[:na]))
        x, y, c, j = _position()
        for jj in range(N_CHIPS):
            @pl.when(j != jj)
            def _():
                for i, p in _stage_pieces(jj, stage):
                    _pass_on_copy(arr, refs[na + 1], refs[na + 2], x, y, c, jj, i, p, c).start()
        refs[-1][...] = jnp.zeros_like(refs[-1])

    outs = pl.pallas_call(
        body,
        name=f"gather_pass_start_{stage[0]}",
        in_specs=[HBM] * na + [ANY],
        out_specs=[SEM, SEM] + [HBM] * na + [pl.BlockSpec(memory_space=pltpu.VMEM)],
        out_shape=[pltpu.SemaphoreType.DMA((N_SEM,)), pltpu.SemaphoreType.DMA((N_SEM,))]
        + [pltpu.HBM(GATHERED[n], BF16) for n in stage] + [jax.ShapeDtypeStruct((8, LANE), F32)],
        input_output_aliases={k: 2 + k for k in range(na)},
        compiler_params=pltpu.CompilerParams(has_side_effects=EFFECT),
    )(*[arrays[n] for n in stage], after)
    return outs[0], outs[1], dict(zip(stage, outs[2:2 + na])), outs[-1]


def _gather_pass_wait(send_sems, recv_sems, arrays, after):
    stage = tuple(arrays)
    na = len(stage)

    def body(*refs):
        arr = dict(zip(stage, refs[:na]))
        x, y, c, j = _position()
        for jj in range(N_CHIPS):
            @pl.when(j != jj)
            def _():
                for i, p in _stage_pieces(jj, stage):
                    _pass_on_copy(arr, refs[na], refs[na + 1], x, y, c, jj, i, p, 1 - c).wait_recv()
                    _pass_on_copy(arr, refs[na], refs[na + 1], x, y, c, jj, i, p, c).wait_send()

    outs = pl.pallas_call(
        body,
        name=f"gather_pass_wait_{stage[0]}",
        in_specs=[HBM] * na + [SEM, SEM, ANY],
        out_specs=[HBM] * na,
        out_shape=[pltpu.HBM(GATHERED[n], BF16) for n in stage],
        input_output_aliases={k: k for k in range(na)},
        compiler_params=pltpu.CompilerParams(has_side_effects=EFFECT),
    )(*[arrays[n] for n in stage], send_sems, recv_sems, after)
    return dict(zip(stage, outs))


def _own_half(ref, shape, axis, which):
    if axis == 1:
        return ref.at[:, pl.ds(which * (shape[1] // 2), shape[1] // 2)]
    return ref.at[pl.ds(which * (shape[0] // 2), shape[0] // 2), :]


def _swap_copies(names, src, dst, send_sems, recv_sems):
    x, y, c, _ = _position()
    return [pltpu.make_async_remote_copy(
        src_ref=_own_half(src[n], GATHERED[n], HALF_AXIS[n], 1 - c), dst_ref=dst[n],
        send_sem=send_sems.at[k], recv_sem=recv_sems.at[k],
        device_id=(x, y, 1 - c), device_id_type=MESH) for k, n in enumerate(names)]


def _swap_start(grads, after):
    names = tuple(grads)
    n = len(names)

    def body(*refs):
        src, dst = dict(zip(names, refs[:n])), dict(zip(names, refs[n:2 * n]))
        for cp in _swap_copies(names, src, dst, refs[2 * n + 1], refs[2 * n + 2]):
            cp.start()
        refs[-1][...] = jnp.zeros_like(refs[-1])

    half_shape = lambda nm: _halved(GATHERED[nm], HALF_AXIS[nm])
    args = [_in_hbm(grads[nm]) for nm in names] + [_in_hbm(lax.empty(half_shape(nm), BF16)) for nm in names]
    if after is None:
        after = jnp.zeros((8, LANE), F32)
    outs = pl.pallas_call(
        body,
        name=f"swap_start_{names[0]}",
        in_specs=[HBM] * (2 * n) + [ANY],
        out_specs=[SEM, SEM] + [HBM] * (2 * n) + [pl.BlockSpec(memory_space=pltpu.VMEM)],
        out_shape=[pltpu.SemaphoreType.DMA((n,)), pltpu.SemaphoreType.DMA((n,))]
        + [pltpu.HBM(GATHERED[nm], BF16) for nm in names] + [pltpu.HBM(half_shape(nm), BF16) for nm in names]
        + [jax.ShapeDtypeStruct((8, LANE), F32)],
        input_output_aliases={k: 2 + k for k in range(2 * n)},
        compiler_params=pltpu.CompilerParams(has_side_effects=EFFECT),
    )(*args, after)
    return outs[0], outs[1], dict(zip(names, outs[2:2 + n])), dict(zip(names, outs[2 + n:2 + 2 * n])), outs[-1]


def _swap_wait(send_sems, recv_sems, grads, landing, after):
    names = tuple(grads)
    n = len(names)

    def body(*refs):
        src, dst = dict(zip(names, refs[:n])), dict(zip(names, refs[n:2 * n]))
        copies = _swap_copies(names, src, dst, refs[2 * n], refs[2 * n + 1])
        for cp in copies:
            cp.wait_recv()
        for cp in copies:
            cp.wait_send()

    half_shape = lambda nm: _halved(GATHERED[nm], HALF_AXIS[nm])
    outs = pl.pallas_call(
        body,
        name=f"swap_wait_{names[0]}",
        in_specs=[HBM] * (2 * n) + [SEM, SEM, ANY],
        out_specs=[HBM] * (2 * n),
        out_shape=[pltpu.HBM(GATHERED[nm], BF16) for nm in names] + [pltpu.HBM(half_shape(nm), BF16) for nm in names],
        input_output_aliases={k: k for k in range(2 * n)},
        compiler_params=pltpu.CompilerParams(has_side_effects=EFFECT),
    )(*[grads[nm] for nm in names], *[landing[nm] for nm in names], send_sems, recv_sems, after)
    return dict(zip(names, outs[:n])), dict(zip(names, outs[n:]))


ADD_ROWS = 256


def _add_half(full, recv, c_arr, name):
    rows, cols = recv.shape
    if HALF_AXIS[name] == 1:
        index = lambda i, c_ref: (i, c_ref[0])
    else:
        nb = rows // ADD_ROWS
        index = lambda i, c_ref: (nb * c_ref[0] + i, 0)

    def body(c_ref, a_ref, b_ref, o_ref):
        o_ref[...] = (a_ref[...].astype(F32) + b_ref[...].astype(F32)).astype(BF16)

    return pl.pallas_call(
        body,
        name=f"add_half_{name}",
        grid_spec=pltpu.PrefetchScalarGridSpec(
            num_scalar_prefetch=1,
            grid=(rows // ADD_ROWS,),
            in_specs=[pl.BlockSpec((ADD_ROWS, cols), index), pl.BlockSpec((ADD_ROWS, cols), lambda i, c_ref: (i, 0))],
            out_specs=pl.BlockSpec((ADD_ROWS, cols), lambda i, c_ref: (i, 0)),
        ),
        out_shape=jax.ShapeDtypeStruct((rows, cols), BF16),
        compiler_params=_params(("parallel",)),
    )(c_arr, full, recv)


SLOT_SHAPES = {n: _halved(SHARD_SHAPES[n], HALF_AXIS[n]) for n in SHARDS}


def _slot_shape(n):
    return (N_CHIPS,) + SLOT_SHAPES[n]


def _stage_shards(stage):
    pieces = [p for jj in range(N_CHIPS) for p in _pieces_of(jj)]
    return tuple(s for s in SHARDS if any(p.src == s and p.dst in stage for p in pieces))


def _scatter_copy(src, dst, send_sems, recv_sems, c, jj, kk, i, p):
    return pltpu.make_async_remote_copy(
        src_ref=_rect_in_half(src[p.dst], p, "dst"), dst_ref=_rect_in_half(dst[p.src].at[jj], p, "src"),
        send_sem=send_sems.at[kk * MAX_PIECES + i], recv_sem=recv_sems.at[jj * MAX_PIECES + i],
        device_id=(kk // 2, kk % 2, c), device_id_type=MESH)


def _scatter_start(halves, slots):
    stage, touched = tuple(halves), tuple(slots)
    nh, nt = len(stage), len(touched)

    def body(*refs):
        src = dict(zip(stage, refs[:nh]))
        dst = dict(zip(touched, refs[nh:nh + nt]))
        send_sems, recv_sems = refs[nh + nt], refs[nh + nt + 1]
        token = refs[-1]
        _, _, c, j = _position()
        for jj in range(N_CHIPS):
            @pl.when(j == jj)
            def _():
                for kk in range(N_CHIPS):
                    if kk != jj:
                        for i, p in _stage_pieces(kk, stage):
                            _scatter_copy(src, dst, send_sems, recv_sems, c, jj, kk, i, p).start()
        token[...] = jnp.zeros_like(token)

    outs = pl.pallas_call(
        body,
        name=f"scatter_start_{stage[0]}",
        in_specs=[HBM] * (nh + nt),
        out_specs=[SEM, SEM] + [HBM] * (nh + nt) + [pl.BlockSpec(memory_space=pltpu.VMEM)],
        out_shape=[pltpu.SemaphoreType.DMA((N_SEM,)), pltpu.SemaphoreType.DMA((N_SEM,))]
        + [pltpu.HBM(halves[n].shape, BF16) for n in stage] + [pltpu.HBM(_slot_shape(s), BF16) for s in touched]
        + [jax.ShapeDtypeStruct((8, LANE), F32)],
        input_output_aliases={k: 2 + k for k in range(nh + nt)},
        compiler_params=pltpu.CompilerParams(has_side_effects=EFFECT),
    )(*[_in_hbm(halves[n]) for n in stage], *[_in_hbm(slots[s]) for s in touched])
    return outs[0], outs[1], dict(zip(stage, outs[2:2 + nh])), dict(zip(touched, outs[2 + nh:2 + nh + nt])), outs[-1]


def _scatter_wait(send_sems, recv_sems, halves, slots, after):
    stage, touched = tuple(halves), tuple(slots)
    nh, nt = len(stage), len(touched)

    def body(*refs):
        src = dict(zip(stage, refs[:nh]))
        dst = dict(zip(touched, refs[nh:nh + nt]))
        sems_s, sems_r = refs[nh + nt], refs[nh + nt + 1]
        _, _, c, j = _position()
        for jj in range(N_CHIPS):
            @pl.when(j == jj)
            def _():
                for ss in range(N_CHIPS):
                    if ss != jj:
                        for i, p in _stage_pieces(jj, stage):
                            _scatter_copy(src, dst, sems_s, sems_r, c, ss, jj, i, p).wait_recv()
                for kk in range(N_CHIPS):
                    if kk != jj:
                        for i, p in _stage_pieces(kk, stage):
                            _scatter_copy(src, dst, sems_s, sems_r, c, jj, kk, i, p).wait_send()

    outs = pl.pallas_call(
        body,
        name=f"scatter_wait_{stage[0]}",
        in_specs=[HBM] * (nh + nt) + [SEM, SEM, ANY],
        out_specs=[HBM] * (nh + nt),
        out_shape=[pltpu.HBM(halves[n].shape, BF16) for n in stage] + [pltpu.HBM(_slot_shape(s), BF16) for s in touched],
        input_output_aliases={k: k for k in range(nh + nt)},
        compiler_params=pltpu.CompilerParams(has_side_effects=EFFECT),
    )(*[halves[n] for n in stage], *[slots[s] for s in touched], send_sems, recv_sems, after)
    return dict(zip(stage, outs[:nh])), dict(zip(touched, outs[nh:]))


SUM_ROWS = {"mk": 256, "br0": 256, "br1": 256, "br2": 256, "out": 256}


def _sum_in_chip_order(chip, own, s_ref):
    acc = None
    for k in range(N_CHIPS):
        term = jnp.where(chip == k, own, s_ref[k].astype(F32))
        acc = term if acc is None else acc + term
    return acc


def _sum_slots(slots, own_half, pos_arr, name):
    _, rows, cols = slots.shape
    tr = SUM_ROWS[name]
    nb = rows // tr
    if HALF_AXIS[name] == 1:
        own_index = lambda i, pos: (nb * pos[1] + i, 0)
        out_index = lambda i, pos: (i, pos[0])
    else:
        own_index = lambda i, pos: (i, pos[1])
        out_index = lambda i, pos: (nb * pos[0] + i, 0)

    def body(pos, s_ref, own_ref, o_ref):
        o_ref[...] = _sum_in_chip_order(pos[1], own_ref[...].astype(F32), s_ref)

    return pl.pallas_call(
        body,
        name=f"sum_slots_{name}",
        grid_spec=pltpu.PrefetchScalarGridSpec(
            num_scalar_prefetch=1,
            grid=(nb,),
            in_specs=[pl.BlockSpec((N_CHIPS, tr, cols), lambda i, pos: (0, i, 0)),
                      pl.BlockSpec((tr, cols), own_index)],
            out_specs=pl.BlockSpec((tr, cols), out_index),
        ),
        out_shape=jax.ShapeDtypeStruct(SHARD_SHAPES[name], F32),
        compiler_params=_params(("parallel",)),
    )(pos_arr, slots, own_half)


def _own_partial_tables():
    import numpy as np
    nb = SHARD // HALF_TILE
    grp, blk = np.zeros((N_CHIPS, nb), np.int32), np.zeros((N_CHIPS, nb), np.int32)
    for jj in range(N_CHIPS):
        for r, n, g, gr in _shard_runs(jj):
            for k in range(n // HALF_TILE):
                grp[jj, r // HALF_TILE + k] = GROUPS.index(g)
                blk[jj, r // HALF_TILE + k] = gr // HALF_TILE + k
    return grp, blk


def _sum_slots_win(slots, own_halves, pos_arr, grp_tbl, blk_tbl):
    nb = SHARD // HALF_TILE
    cols = D // 2

    def own_spec(gi):
        return pl.BlockSpec((HALF_TILE, cols), lambda b, pos, grp, blk: (jnp.where(grp[b] == gi, blk[b], 0), 0))

    def body(pos, grp, blk, s_ref, a_ref, b_ref, c_ref, d_ref, o_ref):
        g = grp[pl.program_id(0)]
        own = a_ref[...]
        for gi, ref in ((1, b_ref), (2, c_ref), (3, d_ref)):
            own = jnp.where(g == gi, ref[...], own)
        o_ref[...] = _sum_in_chip_order(pos[1], own.astype(F32), s_ref)

    return pl.pallas_call(
        body,
        name="sum_slots_win",
        grid_spec=pltpu.PrefetchScalarGridSpec(
            num_scalar_prefetch=3,
            grid=(nb,),
            in_specs=[pl.BlockSpec((N_CHIPS, HALF_TILE, cols), lambda b, pos, grp, blk: (0, b, 0))]
            + [own_spec(gi) for gi in range(len(GROUPS))],
            out_specs=pl.BlockSpec((HALF_TILE, cols), lambda b, pos, grp, blk: (b, pos[0])),
        ),
        out_shape=jax.ShapeDtypeStruct(SHARD_SHAPES["win"], F32),
        compiler_params=_params(("parallel",)),
    )(pos_arr, grp_tbl, blk_tbl, slots, *[own_halves[g] for g in GROUPS])


def _share_copy(buf, name, send_sems, recv_sems, k, which):
    x, y, c, _ = _position()
    half = _own_half(buf, SHARD_SHAPES[name], HALF_AXIS[name], which)
    return pltpu.make_async_remote_copy(src_ref=half, dst_ref=half, send_sem=send_sems.at[k], recv_sem=recv_sems.at[k],
                                        device_id=(x, y, 1 - c), device_id_type=MESH)


def _share_start(sums, after):
    names = tuple(sums)
    n = len(names)

    def body(*refs):
        _, _, c, _ = _position()
        for k, nm in enumerate(names):
            _share_copy(refs[k], nm, refs[n + 1], refs[n + 2], k, c).start()
        refs[-1][...] = jnp.zeros_like(refs[-1])

    outs = pl.pallas_call(
        body,
        name=f"share_start_{names[0]}",
        in_specs=[HBM] * n + [ANY],
        out_specs=[SEM, SEM] + [HBM] * n + [pl.BlockSpec(memory_space=pltpu.VMEM)],
        out_shape=[pltpu.SemaphoreType.DMA((n,)), pltpu.SemaphoreType.DMA((n,))]
        + [pltpu.HBM(SHARD_SHAPES[nm], F32) for nm in names] + [jax.ShapeDtypeStruct((8, LANE), F32)],
        input_output_aliases={k: 2 + k for k in range(n)},
        compiler_params=pltpu.CompilerParams(has_side_effects=EFFECT),
    )(*[_in_hbm(sums[nm]) for nm in names], after)
    return outs[0], outs[1], dict(zip(names, outs[2:2 + n])), outs[-1]


def _share_wait(send_sems, recv_sems, sums, after):
    names = tuple(sums)
    n = len(names)

    def body(*refs):
        _, _, c, _ = _position()
        for k, nm in enumerate(names):
            _share_copy(refs[k], nm, refs[n], refs[n + 1], k, 1 - c).wait_recv()
            _share_copy(refs[k], nm, refs[n], refs[n + 1], k, c).wait_send()

    outs = pl.pallas_call(
        body,
        name=f"share_wait_{names[0]}",
        in_specs=[HBM] * n + [SEM, SEM, ANY],
        out_specs=[HBM] * n,
        out_shape=[pltpu.HBM(SHARD_SHAPES[nm], F32) for nm in names],
        input_output_aliases={k: k for k in range(n)},
        compiler_params=pltpu.CompilerParams(has_side_effects=EFFECT),
    )(*[sums[nm] for nm in names], send_sems, recv_sems, after)
    return dict(zip(names, outs))


def _all_reduce_small(pack, name):
    rows = pack.shape[0]
    half = rows // 2

    def body(p_ref, o_ref, sib, land, sems):
        x, y, c, j = _position()
        sibling = (x, y, 1 - c)
        swap = pltpu.make_async_remote_copy(src_ref=p_ref, dst_ref=sib, send_sem=sems.at[0], recv_sem=sems.at[1],
                                            device_id=sibling, device_id_type=MESH)
        swap.start()
        swap.wait_recv()
        land[j] = p_ref[...] + sib[...]

        def mine(k, which):
            return land.at[k, pl.ds(which * half, half)]

        def ici(kk):
            return pltpu.make_async_remote_copy(
                src_ref=mine(j, c), dst_ref=mine(j, c), send_sem=sems.at[2 + kk], recv_sem=sems.at[6 + j],
                device_id=(kk // 2, kk % 2, c), device_id_type=MESH)

        def arrival(kk):
            return pltpu.make_async_remote_copy(
                src_ref=mine(kk, c), dst_ref=mine(kk, c), send_sem=sems.at[2 + kk], recv_sem=sems.at[6 + kk],
                device_id=(kk // 2, kk % 2, c), device_id_type=MESH)

        def passed_on(kk, which):
            return pltpu.make_async_remote_copy(
                src_ref=mine(kk, which), dst_ref=mine(kk, which), send_sem=sems.at[10 + kk],
                recv_sem=sems.at[14 + kk], device_id=sibling, device_id_type=MESH)

        for kk in range(N_CHIPS):
            @pl.when(j != kk)
            def _():
                ici(kk).start()
        for kk in range(N_CHIPS):
            @pl.when(j != kk)
            def _():
                arrival(kk).wait_recv()
                passed_on(kk, c).start()
        for kk in range(N_CHIPS):
            @pl.when(j != kk)
            def _():
                passed_on(kk, 1 - c).wait_recv()
        acc = land[0]
        for kk in range(1, N_CHIPS):
            acc = acc + land[kk]
        o_ref[...] = acc
        swap.wait_send()
        for kk in range(N_CHIPS):
            @pl.when(j != kk)
            def _():
                ici(kk).wait_send()
                passed_on(kk, c).wait_send()

    vmem = pl.BlockSpec(memory_space=pltpu.VMEM)
    return pl.pallas_call(
        body,
        name=name,
        in_specs=[vmem],
        out_specs=vmem,
        out_shape=jax.ShapeDtypeStruct((rows, LANE), F32),
        scratch_shapes=[pltpu.VMEM((rows, LANE), F32), pltpu.VMEM((N_CHIPS, rows, LANE), F32),
                        pltpu.SemaphoreType.DMA((18,))],
        compiler_params=pltpu.CompilerParams(has_side_effects=True, vmem_limit_bytes=VMEM_LIMIT),
    )(pack)


ADAM_ROWS = {"win": 224, "mk": 256, "br0": 512, "br1": 512, "br2": 512, "out": 256}


def _adamw(w, g, m, v, name, tr):
    rows, cols = w.shape
    tr = min(tr, rows)

    def body(w_ref, g_ref, m_ref, v_ref, go_ref, d_ref, nm_ref, nv_ref):
        gv = g_ref[...]
        go_ref[...] = gv
        nm = ADAM_B1 * m_ref[...] + (1.0 - ADAM_B1) * gv
        nv = ADAM_B2 * v_ref[...] + (1.0 - ADAM_B2) * (gv * gv)
        nm_ref[...] = nm
        nv_ref[...] = nv
        m_hat = nm / (1.0 - ADAM_B1 ** ADAM_STEP)
        v_hat = nv / (1.0 - ADAM_B2 ** ADAM_STEP)
        d_ref[...] = -ADAM_LR * (m_hat / (jnp.sqrt(v_hat) + ADAM_EPS) + ADAM_WD * w_ref[...])

    blk = pl.BlockSpec((tr, cols), lambda i: (i, 0))
    shape = jax.ShapeDtypeStruct((rows, cols), F32)
    return pl.pallas_call(
        body,
        name=f"adamw_{name}",
        grid=(rows // tr,),
        in_specs=[blk] * 4,
        out_specs=[blk] * 4,
        out_shape=[shape] * 4,
        compiler_params=_params(("parallel",)),
    )(w, g, m, v)


SMALL = (("pre_norm_g", (1, D)), ("post_norm_g", (1, D)), ("mem_norm_g", (1, D)), ("conv_w", (CONV_W, D_RNN)),
         ("conv_b", (1, D_RNN)), ("w_rg_a", (RNN_BLOCKS, LANE, LANE)), ("b_rg_a", (1, D_RNN)),
         ("w_rg_x", (RNN_BLOCKS, LANE, LANE)), ("b_rg_x", (1, D_RNN)), ("lru_lambda", (1, D_RNN)),
         ("swa_sinks", (1, SWA_HEADS)), ("rel_bias", (REL_BUCKETS, SWA_HEADS)))
PACK_ROWS = 2176


def _slot_len(shape):
    return -(-math.prod(shape) // LANE) * LANE


def _pack(values, last_row=None):
    parts = []
    for name, shape in SMALL:
        flat = values[name].reshape(-1).astype(F32)
        parts.append(jnp.pad(flat, (0, _slot_len(shape) - flat.shape[0])))
    flat = jnp.concatenate(parts)
    tail = jnp.zeros((LANE,), F32) if last_row is None else last_row
    return jnp.concatenate([jnp.pad(flat, (0, (PACK_ROWS - 1) * LANE - flat.shape[0])), tail]).reshape(PACK_ROWS, LANE)


def _unpack(pack):
    flat = pack.reshape(-1)
    out, off = {}, 0
    for name, shape in SMALL:
        out[name] = flat[off:off + math.prod(shape)].reshape(shape)
        off += _slot_len(shape)
    return out


TWIN_WEIGHTS = ("pre_norm_g", "post_norm_g", "mem_norm_g", "w_in", "conv_w", "conv_b", "w_rg_a", "b_rg_a", "w_rg_x",
                "b_rg_x", "lru_lambda", "swa_sinks", "rel_bias", "w_mem_kv", "w_br_rg", "w_br_swa", "w_br_mem", "w_out")
BIG = {"w_in": "win", "w_mem_kv": "mk", "w_br_rg": "br0", "w_br_swa": "br1", "w_br_mem": "br2", "w_out": "out"}


def kernel(x, mem, pre_norm_g, post_norm_g, mem_norm_g, w_in, conv_w, conv_b, w_rg_a, b_rg_a, w_rg_x, b_rg_x, lru_lambda, swa_sinks, rel_bias, w_mem_kv, w_br_rg, w_br_swa, w_br_mem, w_out, loss_target, m_pre_norm_g, m_post_norm_g, m_mem_norm_g, m_w_in, m_conv_w, m_conv_b, m_w_rg_a, m_b_rg_a, m_w_rg_x, m_b_rg_x, m_lru_lambda, m_swa_sinks, m_rel_bias, m_w_mem_kv, m_w_br_rg, m_w_br_swa, m_w_br_mem, m_w_out, v_pre_norm_g, v_post_norm_g, v_mem_norm_g, v_w_in, v_conv_w, v_conv_b, v_w_rg_a, v_b_rg_a, v_w_rg_x, v_b_rg_x, v_lru_lambda, v_swa_sinks, v_rel_bias, v_w_mem_kv, v_w_br_rg, v_w_br_swa, v_w_br_mem, v_w_out):
    args = dict(locals())
    out_shapes = {n: args[n].shape for n in TWIN_WEIGHTS}
    w = {n: (args[n] if n == "rel_bias" else args[n][0]) for n in TWIN_WEIGHTS}
    m = {n: (args["m_" + n] if n == "rel_bias" else args["m_" + n][0]) for n in TWIN_WEIGHTS}
    v = {n: (args["v_" + n] if n == "rel_bias" else args["v_" + n][0]) for n in TWIN_WEIGHTS}
    for d in (w, m, v):
        for n, shape in SMALL:
            if n != "conv_w":
                d[n] = d[n].reshape(shape)

    xi, yi, ci = lax.axis_index("x"), lax.axis_index("y"), lax.axis_index("c")
    chip = 2 * xi + yi
    c_arr = ci.astype(jnp.int32).reshape(1)
    zero = jnp.zeros((), jnp.int32)
    cw0 = (chip * (D_RNN // N_CHIPS)).astype(jnp.int32)

    placed = lax.dynamic_update_slice(jnp.zeros((CONV_W, D_RNN), F32), w["conv_w"], (zero, cw0))
    placed = jnp.where(ci == 0, placed, 0.0).reshape(CONV_W * D_RNN // LANE, LANE)
    conv_w_full = _all_reduce_small(placed, "gather_conv_w").reshape(CONV_W, D_RNN)

    for d in (w, m, v):
        d["w_in"] = d["w_in"].T
    chip_row = lambda tbl: lax.dynamic_slice(jnp.asarray(tbl), (chip.astype(jnp.int32), zero), (1, tbl.shape[1]))[0]
    big_of = {s: n for n, s in BIG.items()}
    ag, token = {}, conv_w_full
    for stage in GATHER_STAGES:
        behind = c_arr if stage == GATHER_STAGES[0] else token
        placed = {n: (_place_group(w["w_in"], n, chip_row(_own_block_table(n)), behind) if n in GROUPS
                      else _place_shard(w[big_of[n]], n, behind)) for n in stage}
        send, recv, in_flight, token = _gather_start(placed, token)
        ag[stage] = (send, recv, in_flight)

    all_started = token

    passing = {}

    def prefetch(names, after):
        send, recv, in_flight = ag[names]
        *passing[names], token = _gather_pass_start(_gather_wait(send, recv, in_flight, after), after)
        return token

    def fetch(names, after):
        if names in passing:
            ready = _gather_pass_wait(*passing.pop(names), after)
        else:
            send, recv, in_flight = ag[names]
            after = all_started if names == GATHER_STAGES[0] else after
            ready = _gather_swap(_gather_wait(send, recv, in_flight, after))
        return tuple(ready[n] for n in names)

    rs = {"slots": {}, "halves": {}, "pending": [], "swap": None}

    def emit(grads, after=None):
        assert rs["swap"] is None
        *rs["swap"], token = _swap_start(grads, after)
        return token

    def advance(after):
        grads, received = _swap_wait(*rs["swap"], after)
        rs["swap"] = None
        halves = {n: _add_half(grads[n], received[n], c_arr, n) for n in grads}
        landing = {s: rs["slots"][s] if s in rs["slots"] else lax.empty(_slot_shape(s), BF16)
                   for s in _stage_shards(tuple(grads))}
        send, recv, halves, landing, token = _scatter_start(halves, landing)
        rs["slots"].update(landing)
        rs["pending"].append((send, recv, halves, tuple(landing)))
        return token

    sp = {n: w[n] for n, _ in SMALL}
    sp["conv_w"] = conv_w_full
    sq, grad_x, d_small = _local_step(x[0], mem[0], loss_target[0], sp, fetch, prefetch, emit, advance)
    small_total = _all_reduce_small(_pack(d_small, sq[0]), "all_reduce_small")
    loss = small_total[PACK_ROWS - 1, 0] * (0.5 / D)

    for send, recv, halves, touched in rs["pending"]:
        halves, landed = _scatter_wait(send, recv, halves, {s: rs["slots"][s] for s in touched}, small_total)
        rs["slots"].update(landed)
        rs["halves"].update(halves)
    pos_arr = jnp.stack([ci, chip]).astype(jnp.int32)
    grp_tbl, blk_tbl = (chip_row(t) for t in _own_partial_tables())
    rest = {s: _sum_slots(rs["slots"][s], rs["halves"][s], pos_arr, s) for s in SHARDS if s != "win"}
    *rest_share, tok = _share_start(rest, small_total)
    win_sum = _sum_slots_win(rs["slots"]["win"], rs["halves"], pos_arr, grp_tbl, blk_tbl)
    *win_share, tok = _share_start({"win": win_sum}, tok)
    sums = _share_wait(*rest_share, tok)

    grad, delta, new_m, new_v = {}, {}, {}, {}
    for n, s in BIG.items():
        if n == "w_in":
            continue
        grad[n], delta[n], new_m[n], new_v[n] = _adamw(w[n], sums[s], m[n], v[n], s, ADAM_ROWS[s])
    g_win = _share_wait(*win_share, delta["w_out"])["win"]
    n = "w_in"
    grad[n], delta[n], new_m[n], new_v[n] = _adamw(w[n], g_win, m[n], v[n], "win", ADAM_ROWS["win"])
    for group in (grad, delta, new_m, new_v):
        group["w_in"] = group["w_in"].T
    def conv_w_in_place(d):
        return dict(d, conv_w=lax.dynamic_update_slice(jnp.zeros((CONV_W, D_RNN), F32), d["conv_w"], (zero, cw0)))

    _, d_, m_, v_ = _adamw(_pack(conv_w_in_place(w)), small_total, _pack(conv_w_in_place(m)),
                           _pack(conv_w_in_place(v)), "small", PACK_ROWS)
    for group, pack in ((grad, small_total), (delta, d_), (new_m, m_), (new_v, v_)):
        group.update(_unpack(pack))
    for group in (grad, delta, new_m, new_v):
        group["conv_w"] = lax.dynamic_slice(group["conv_w"], (zero, cw0), (CONV_W, D_RNN // N_CHIPS))

    outs = [loss, grad_x.reshape(1, S, D)]
    for group in (grad, delta, new_m, new_v):
        outs += [group[n].reshape(out_shapes[n]) for n in TWIN_WEIGHTS]
    return tuple(outs)
```

```python
import math
from typing import NamedTuple

import jax
import jax.numpy as jnp
from jax import lax
from jax.experimental import pallas as pl
from jax.experimental.pallas import tpu as pltpu

F32 = jnp.float32
BF16 = jnp.bfloat16
MESH = pl.DeviceIdType.MESH

S = 2048
D = 2048
MEM = 256
D_RNN = 1024
RNN_BLOCKS = 8
CONV_W = 4
LRU_C = 8.0
SWA_HEADS = 16
SWA_HD = 64
WINDOW = 128
MEM_HEADS = 4
MEM_HD = 256
REL_BUCKETS = 32
REL_MAX_DIST = 128
EPS = 1e-6
NEG_INF = -1e30
LANE = 128
SHARD = 3136
HALF_TILE = 64
N_CHIPS = 4
VMEM_LIMIT = 56 * 1024 * 1024

ADAM_LR = 0.001
ADAM_B1 = 0.9
ADAM_B2 = 0.999
ADAM_EPS = 1e-08
ADAM_WD = 0.01
ADAM_STEP = 10

GROUP_TILES = {"A": 16, "B": 18, "C": 16, "D": 48}
GROUPS = ("A", "B", "C", "D")


def _params(sem=None):
    return pltpu.CompilerParams(dimension_semantics=sem, vmem_limit_bytes=VMEM_LIMIT)


def _sigmoid(v):
    return jax.nn.sigmoid(v)


def _tile_home(t):
    if t < 16:
        return "A", t
    if t < 24:
        return "B", t - 16
    if t < 26:
        return "B", t - 24 + 16
    if t < 34:
        return "B", t - 26 + 8
    if t < 50:
        return "C", t - 34
    return "D", t - 50


def _shard_runs(j):
    runs = []
    per_shard = SHARD // HALF_TILE
    for q in range(per_shard * j, per_shard * (j + 1)):
        g, gt = _tile_home(q // 2)
        row = gt * LANE + (q % 2) * HALF_TILE
        if runs and runs[-1][2] == g and runs[-1][3] + runs[-1][1] == row:
            runs[-1][1] += HALF_TILE
        else:
            runs.append([(q - per_shard * j) * HALF_TILE, HALF_TILE, g, row])
    return [tuple(r) for r in runs]


_DIMS = {
    "nn": (((1,), (0,)), ((), ())),
    "nt": (((1,), (1,)), ((), ())),
    "tn": (((0,), (0,)), ((), ())),
}


def _mm(a, b, mode, out_dtype, tm, tn, tk, name, acc=None, after=None):
    if mode == "nn":
        (m, k), n = a.shape, b.shape[1]
    elif mode == "nt":
        (m, k), n = a.shape, b.shape[0]
    else:
        (k, m), n = a.shape, b.shape[1]
    tm, tn, tk = min(tm, m), min(tn, n), min(tk, k)
    assert m % tm == 0 and n % tn == 0 and k % tk == 0, (name, m, n, k)
    nk = k // tk
    has_acc = acc is not None

    def body(*refs):
        a_ref, b_ref = refs[0], refs[1]
        o_ref = refs[3] if has_acc else refs[2]
        p = lax.dot_general(a_ref[...], b_ref[...], _DIMS[mode], preferred_element_type=F32)

        def finish(v):
            if has_acc:
                v = v + refs[2][...]
            o_ref[...] = v.astype(out_dtype)

        if nk == 1:
            finish(p)
        else:
            s_ref = refs[-1]
            kk = pl.program_id(2)

            @pl.when(kk == 0)
            def _():
                s_ref[...] = p

            @pl.when(kk > 0)
            def _():
                s_ref[...] += p

            @pl.when(kk == nk - 1)
            def _():
                finish(s_ref[...])

    if mode == "nn":
        a_spec = pl.BlockSpec((tm, tk), lambda i, j, kk: (i, kk))
        b_spec = pl.BlockSpec((tk, tn), lambda i, j, kk: (kk, j))
    elif mode == "nt":
        a_spec = pl.BlockSpec((tm, tk), lambda i, j, kk: (i, kk))
        b_spec = pl.BlockSpec((tn, tk), lambda i, j, kk: (j, kk))
    else:
        a_spec = pl.BlockSpec((tk, tm), lambda i, j, kk: (kk, i))
        b_spec = pl.BlockSpec((tk, tn), lambda i, j, kk: (kk, j))
    o_spec = pl.BlockSpec((tm, tn), lambda i, j, kk: (i, j))
    in_specs = [a_spec, b_spec] + ([o_spec] if has_acc else [])
    args = (a, b) + ((acc,) if has_acc else ())
    if after is not None:
        in_specs.append(pl.BlockSpec(memory_space=pl.ANY))
        args += (after,)
    n_in = len(args)
    kernel_body = body

    def body(*refs):
        kernel_body(*(refs[:n_in - (after is not None)] + refs[n_in:]))

    return pl.pallas_call(
        body,
        name=name,
        grid=(m // tm, n // tn, nk),
        in_specs=in_specs,
        out_specs=o_spec,
        out_shape=jax.ShapeDtypeStruct((m, n), out_dtype),
        scratch_shapes=[pltpu.VMEM((tm, tn), F32)] if nk > 1 else [],
        compiler_params=_params(("parallel", "parallel", "arbitrary")),
    )(*args)


def _rms_fwd(x, g, name, after, ts=256):
    r, d = x.shape

    def body(x_ref, g_ref, _, o_ref):
        xv = x_ref[...]
        inv = lax.rsqrt(jnp.mean(xv * xv, axis=-1, keepdims=True) + EPS)
        o_ref[...] = (xv * inv * g_ref[...]).astype(BF16)

    return pl.pallas_call(
        body,
        name=name,
        grid=(r // ts,),
        in_specs=[pl.BlockSpec((ts, d), lambda i: (i, 0)), pl.BlockSpec((1, d), lambda i: (0, 0)),
                  pl.BlockSpec(memory_space=pl.ANY)],
        out_specs=pl.BlockSpec((ts, d), lambda i: (i, 0)),
        out_shape=jax.ShapeDtypeStruct((r, d), BF16),
        compiler_params=_params(("parallel",)),
    )(x, g, after)


def _post_loss(out, x, tgt, g_post, ts=256):
    n = S // ts

    def body(o_ref, x_ref, t_ref, g_ref, sq_ref, dy_ref, do_ref, dg_ref):
        i = pl.program_id(0)

        @pl.when(i == 0)
        def _():
            sq_ref[...] = jnp.zeros_like(sq_ref)
            dg_ref[...] = jnp.zeros_like(dg_ref)

        ov = o_ref[...]
        g = g_ref[...]
        inv = lax.rsqrt(jnp.mean(ov * ov, axis=-1, keepdims=True) + EPS)
        on = ov * inv
        err = x_ref[...] + on * g - t_ref[...]
        sq_ref[...] += jnp.sum(err * err)
        dy = err * (1.0 / D)
        dy_ref[...] = dy
        dg_ref[...] += jnp.sum(dy * on, axis=0, keepdims=True)
        don = dy * g
        do_ref[...] = (inv * (don - on * jnp.mean(don * on, axis=-1, keepdims=True))).astype(BF16)

    row = pl.BlockSpec((ts, D), lambda i: (i, 0))
    vec = pl.BlockSpec((1, D), lambda i: (0, 0))
    return pl.pallas_call(
        body,
        name="post_loss",
        grid=(n,),
        in_specs=[row, row, row, vec],
        out_specs=[pl.BlockSpec((8, LANE), lambda i: (0, 0)), row, row, vec],
        out_shape=[
            jax.ShapeDtypeStruct((8, LANE), F32),
            jax.ShapeDtypeStruct((S, D), F32),
            jax.ShapeDtypeStruct((S, D), BF16),
            jax.ShapeDtypeStruct((1, D), F32),
        ],
        compiler_params=_params(("arbitrary",)),
    )(out, x, tgt, g_post)


def _pre_bwd(dh, x, dy, g_pre, ts=256):
    n = S // ts

    def body(dh_ref, x_ref, dy_ref, g_ref, gx_ref, dg_ref):
        i = pl.program_id(0)

        @pl.when(i == 0)
        def _():
            dg_ref[...] = jnp.zeros_like(dg_ref)

        xv = x_ref[...]
        dhv = dh_ref[...]
        inv = lax.rsqrt(jnp.mean(xv * xv, axis=-1, keepdims=True) + EPS)
        xn = xv * inv
        dg_ref[...] += jnp.sum(dhv * xn, axis=0, keepdims=True)
        dxn = dhv * g_ref[...]
        gx_ref[...] = dy_ref[...] + inv * (dxn - xn * jnp.mean(dxn * xn, axis=-1, keepdims=True))

    row = pl.BlockSpec((ts, D), lambda i: (i, 0))
    vec = pl.BlockSpec((1, D), lambda i: (0, 0))
    return pl.pallas_call(
        body,
        name="pre_bwd",
        grid=(n,),
        in_specs=[row, row, row, vec],
        out_specs=[row, vec],
        out_shape=[jax.ShapeDtypeStruct((S, D), F32), jax.ShapeDtypeStruct((1, D), F32)],
        compiler_params=_params(("arbitrary",)),
    )(dh, x, dy, g_pre)


def _memnorm_bwd(dmemn, mem):
    def body(d_ref, m_ref, dg_ref):
        mv = m_ref[...]
        inv = lax.rsqrt(jnp.mean(mv * mv, axis=-1, keepdims=True) + EPS)
        dg_ref[...] = jnp.sum(d_ref[...] * mv * inv, axis=0, keepdims=True)

    return pl.pallas_call(
        body,
        name="memnorm_bwd",
        out_shape=jax.ShapeDtypeStruct((1, D), F32),
        compiler_params=_params(),
    )(dmemn, mem)


T_RNN = 256


def _neg_expm1(z):
    poly = -z * (1.0 + z * (0.5 + z * (1.0 / 6 + z * (1.0 / 24 + z * (1.0 / 120 + z * (1.0 / 720))))))
    return jnp.where(z > -0.1, poly, 1.0 - jnp.exp(z))


def _softplus_neg(lam):
    return jnp.maximum(-lam, 0.0) + jnp.log1p(jnp.exp(-jnp.abs(lam)))


def _rnn_gates(conv, wa_ref, ba, wx_ref, bx, lam, first_row):
    cbf = conv.astype(BF16)
    ga, gx = [], []
    for n in range(RNN_BLOCKS):
        c_n = cbf[:, n * LANE:(n + 1) * LANE]
        ga.append(jnp.dot(c_n, wa_ref[n], preferred_element_type=F32))
        gx.append(jnp.dot(c_n, wx_ref[n], preferred_element_type=F32))
    gate_r = _sigmoid(jnp.concatenate(ga, axis=1) + ba)
    gate_i = _sigmoid(jnp.concatenate(gx, axis=1) + bx)
    sp = _softplus_neg(lam)
    log_a = -LRU_C * gate_r * sp
    a = jnp.exp(log_a)
    mult_raw = jnp.sqrt(_neg_expm1(2.0 * log_a))
    mult = jnp.where(first_row, 1.0, mult_raw)
    return cbf, gate_r, gate_i, sp, a, mult_raw, mult


def _rglru_fwd(p_a, conv_w, conv_b, wa, ba, wx, bx, lam):
    t = T_RNN
    n = S // t

    def body(xr_ref, g_ref, cw_ref, cb_ref, wa_ref, ba_ref, wx_ref, bx_ref, lam_ref,
             y_ref, h_ref, xp_s, hcar, a_s, b_s):
        i = pl.program_id(0)

        @pl.when(i == 0)
        def _():
            xp_s[0:8, :] = jnp.zeros((8, D_RNN), F32)
            hcar[...] = jnp.zeros_like(hcar)

        @pl.when(i > 0)
        def _():
            xp_s[0:8, :] = xp_s[t:t + 8, :]

        xp_s[8:8 + t, :] = xr_ref[...]
        conv = cb_ref[...]
        for k in range(CONV_W):
            conv = conv + cw_ref[k:k + 1, :] * xp_s[8 - k:8 - k + t, :]
        rows = i * t + lax.broadcasted_iota(jnp.int32, (t, 1), 0)
        _, _, gate_i, _, a, _, mult = _rnn_gates(
            conv, wa_ref, ba_ref[...], wx_ref, bx_ref[...], lam_ref[...], rows == 0)
        a_s[...] = a
        b_s[...] = mult * gate_i * conv

        def step(tt, h):
            h = a_s[pl.ds(tt, 1), :] * h + b_s[pl.ds(tt, 1), :]
            h_ref[pl.ds(tt, 1), :] = h
            return h

        hcar[...] = lax.fori_loop(0, t, step, hcar[...], unroll=8)
        g = g_ref[...]
        y_ref[...] = (h_ref[...] * (g * _sigmoid(g))).astype(BF16)

    blk = lambda c: pl.BlockSpec((t, D_RNN), lambda i: (i, c))
    full = lambda shape: pl.BlockSpec(shape, lambda i: (0,) * len(shape))
    return pl.pallas_call(
        body,
        name="rglru_fwd",
        grid=(n,),
        in_specs=[blk(0), blk(1), full((CONV_W, D_RNN)), full((1, D_RNN)),
                  full((RNN_BLOCKS, LANE, LANE)), full((1, D_RNN)),
                  full((RNN_BLOCKS, LANE, LANE)), full((1, D_RNN)), full((1, D_RNN))],
        out_specs=[blk(0), blk(0)],
        out_shape=[jax.ShapeDtypeStruct((S, D_RNN), BF16), jax.ShapeDtypeStruct((S, D_RNN), F32)],
        scratch_shapes=[pltpu.VMEM((t + 8, D_RNN), F32), pltpu.VMEM((1, D_RNN), F32),
                        pltpu.VMEM((t, D_RNN), F32), pltpu.VMEM((t, D_RNN), F32)],
        compiler_params=_params(("arbitrary",)),
    )(p_a, p_a, conv_w, conv_b, wa, ba, wx, bx, lam)


def _rglru_bwd(dy, p_a, hseq, conv_w, conv_b, wa, ba, wx, bx, lam):
    t = T_RNN
    n = S // t
    rb = t // 8

    def body(dy_ref, xr_ref, g_ref, h_ref, xrp_ref, hp_ref, cw_ref, cb_ref, wa_ref, ba_ref, wx_ref, bx_ref, lam_ref,
             dp_ref, dcw_ref, dcb_ref, dwa_ref, dba_ref, dwx_ref, dbx_ref, dlam_ref,
             xp_s, hp_s, dxp_s, lamcar, a_s, dh_s, lam_s):
        i = pl.program_id(0)
        r = n - 1 - i

        @pl.when(i == 0)
        def _():
            for ref in (dcw_ref, dcb_ref, dwa_ref, dba_ref, dwx_ref, dbx_ref, dlam_ref, lamcar):
                ref[...] = jnp.zeros_like(ref)
            dxp_s[t:t + 8, :] = jnp.zeros((8, D_RNN), F32)

        @pl.when(i > 0)
        def _():
            dxp_s[t:t + 8, :] = dxp_s[0:8, :]

        has_prev = r > 0
        xp_s[0:8, :] = jnp.where(has_prev, xrp_ref[...], 0.0)
        xp_s[8:8 + t, :] = xr_ref[...]
        hp_s[0:8, :] = jnp.where(has_prev, hp_ref[...], 0.0)
        hp_s[8:8 + t, :] = h_ref[...]
        xs = [xp_s[8 - k:8 - k + t, :] for k in range(CONV_W)]
        conv = cb_ref[...]
        for k in range(CONV_W):
            conv = conv + cw_ref[k:k + 1, :] * xs[k]
        rows = r * t + lax.broadcasted_iota(jnp.int32, (t, 1), 0)
        first = rows == 0
        lam_p = lam_ref[...]
        cbf, gate_r, gate_i, sp, a, mult_raw, mult = _rnn_gates(
            conv, wa_ref, ba_ref[...], wx_ref, bx_ref[...], lam_p, first)

        g = g_ref[...]
        sg = _sigmoid(g)
        dyv = dy_ref[...]
        a_s[...] = a
        dh_s[...] = dyv * (g * sg)
        dg = dyv * h_ref[...] * (sg * (1.0 + g * (1.0 - sg)))

        def step(jj, car):
            tt = t - 1 - jj
            lm = dh_s[pl.ds(tt, 1), :] + car
            lam_s[pl.ds(tt, 1), :] = lm
            return a_s[pl.ds(tt, 1), :] * lm

        lamcar[...] = lax.fori_loop(0, t, step, lamcar[...], unroll=8)
        db = lam_s[...]
        da = db * hp_s[7:7 + t, :]
        dmult = db * gate_i * conv
        dgate_i = db * mult * conv
        dconv = db * mult * gate_i
        dlog_a = da * a + jnp.where(first, 0.0, dmult * (-(a * a) / mult_raw))
        dgate_r = dlog_a * (-LRU_C * sp)
        dsp = jnp.sum(dlog_a * (-LRU_C * gate_r), axis=0, keepdims=True)
        dlam_ref[...] += dsp * (-_sigmoid(-lam_p))
        dga = dgate_r * gate_r * (1.0 - gate_r)
        dgx = dgate_i * gate_i * (1.0 - gate_i)
        dba_ref[...] += jnp.sum(dga, axis=0, keepdims=True)
        dbx_ref[...] += jnp.sum(dgx, axis=0, keepdims=True)
        dga16, dgx16 = dga.astype(BF16), dgx.astype(BF16)
        back = []
        for nb in range(RNN_BLOCKS):
            sl = slice(nb * LANE, (nb + 1) * LANE)
            dwa_ref[nb] += lax.dot_general(cbf[:, sl], dga16[:, sl], _DIMS["tn"], preferred_element_type=F32)
            dwx_ref[nb] += lax.dot_general(cbf[:, sl], dgx16[:, sl], _DIMS["tn"], preferred_element_type=F32)
            back.append(lax.dot_general(dga16[:, sl], wa_ref[nb], _DIMS["nt"], preferred_element_type=F32)
                        + lax.dot_general(dgx16[:, sl], wx_ref[nb], _DIMS["nt"], preferred_element_type=F32))
        dconv = dconv + jnp.concatenate(back, axis=1)
        dcb_ref[...] += jnp.sum(dconv, axis=0, keepdims=True)
        for k in range(CONV_W):
            dcw_ref[k:k + 1, :] += jnp.sum(dconv * xs[k], axis=0, keepdims=True)
        dxp_s[0:t, :] = dconv
        dxr = cw_ref[0:1, :] * dconv
        for k in range(1, CONV_W):
            dxr = dxr + cw_ref[k:k + 1, :] * dxp_s[k:k + t, :]
        dp_ref[:, 0:D_RNN] = dxr.astype(BF16)
        dp_ref[:, D_RNN:2 * D_RNN] = dg.astype(BF16)

    blk = lambda c: pl.BlockSpec((t, D_RNN), lambda i: (n - 1 - i, c))
    prev8 = pl.BlockSpec((8, D_RNN), lambda i: (jnp.maximum((n - 1 - i) * rb - 1, 0), 0))
    full = lambda shape: pl.BlockSpec(shape, lambda i: (0,) * len(shape))
    vec = full((1, D_RNN))
    mat = full((RNN_BLOCKS, LANE, LANE))
    return pl.pallas_call(
        body,
        name="rglru_bwd",
        grid=(n,),
        in_specs=[blk(0), blk(0), blk(1), blk(0), prev8, prev8,
                  full((CONV_W, D_RNN)), vec, mat, vec, mat, vec, vec],
        out_specs=[pl.BlockSpec((t, 2 * D_RNN), lambda i: (n - 1 - i, 0)),
                   full((CONV_W, D_RNN)), vec, mat, vec, mat, vec, vec],
        out_shape=[jax.ShapeDtypeStruct((S, 2 * D_RNN), BF16),
                   jax.ShapeDtypeStruct((CONV_W, D_RNN), F32), jax.ShapeDtypeStruct((1, D_RNN), F32),
                   jax.ShapeDtypeStruct((RNN_BLOCKS, LANE, LANE), F32), jax.ShapeDtypeStruct((1, D_RNN), F32),
                   jax.ShapeDtypeStruct((RNN_BLOCKS, LANE, LANE), F32), jax.ShapeDtypeStruct((1, D_RNN), F32),
                   jax.ShapeDtypeStruct((1, D_RNN), F32)],
        scratch_shapes=[pltpu.VMEM((t + 8, D_RNN), F32), pltpu.VMEM((t + 8, D_RNN), F32),
                        pltpu.VMEM((t + 8, D_RNN), F32), pltpu.VMEM((1, D_RNN), F32),
                        pltpu.VMEM((t, D_RNN), F32), pltpu.VMEM((t, D_RNN), F32), pltpu.VMEM((t, D_RNN), F32)],
        compiler_params=_params(("arbitrary",)),
    )(dy, p_a, p_a, hseq, p_a, hseq, conv_w, conv_b, wa, ba, wx, bx, lam)


QB = WINDOW
KB2 = 2 * WINDOW
N_QB = S // QB
N_PAIR = SWA_HEADS // 2


def _swa_keys(kvc_ref, kvp_ref):
    kk = jnp.concatenate([kvp_ref[:, 0:LANE], kvc_ref[:, 0:LANE]], axis=0)
    vv = jnp.concatenate([kvp_ref[:, LANE:2 * LANE], kvc_ref[:, LANE:2 * LANE]], axis=0)
    lo = lax.broadcasted_iota(jnp.int32, (1, LANE), 1) < SWA_HD
    kk_sw, vv_sw = pltpu.roll(kk, SWA_HD, 1), pltpu.roll(vv, SWA_HD, 1)
    kd = [jnp.where(lo, kk, kk_sw).astype(BF16), jnp.where(lo, kk_sw, kk).astype(BF16)]
    vd = [jnp.where(lo, vv, vv_sw).astype(BF16), jnp.where(lo, vv_sw, vv).astype(BF16)]
    return lo, kd, vd


def _swa_valid(n):
    qi = lax.broadcasted_iota(jnp.int32, (QB, KB2), 0)
    kj = lax.broadcasted_iota(jnp.int32, (QB, KB2), 1)
    dist = qi + WINDOW - kj
    return (dist >= 0) & (dist < WINDOW) & ((n > 0) | (kj >= WINDOW))


def _swa_probs(qh16, kd, bias, sink, valid):
    lg = lax.dot_general(qh16, kd, _DIMS["nt"], preferred_element_type=F32) * (SWA_HD ** -0.5) + bias
    lg = jnp.where(valid, lg, NEG_INF)
    m = jnp.maximum(jnp.max(lg, axis=-1, keepdims=True), sink)
    p = jnp.exp(lg - m)
    es = jnp.exp(sink - m)
    den = jnp.sum(p, axis=-1, keepdims=True) + es
    return p / den, es / den


def _swa_specs():
    q = pl.BlockSpec((QB, D_RNN), lambda n: (n, 0))
    g = pl.BlockSpec((QB, D_RNN), lambda n: (n, 1))
    kvc = pl.BlockSpec((QB, 2 * LANE), lambda n: (n, 8))
    kvp = pl.BlockSpec((QB, 2 * LANE), lambda n: (jnp.maximum(n - 1, 0), 8))
    bias = pl.BlockSpec((SWA_HEADS, QB, KB2), lambda n: (0, 0, 0))
    sinks = pl.BlockSpec(memory_space=pltpu.SMEM)
    return q, g, kvc, kvp, bias, sinks


def _swa_fwd(p_b, bias_t, sinks):
    def body(q_ref, g_ref, kvc_ref, kvp_ref, bias_ref, sink_ref, y_ref, o_ref):
        n = pl.program_id(0)
        lo, kd, vd = _swa_keys(kvc_ref, kvp_ref)
        valid = _swa_valid(n)
        for hp in range(N_PAIR):
            sl = slice(hp * LANE, (hp + 1) * LANE)
            kvh = hp // (N_PAIR // 2)
            q = q_ref[:, sl]
            outs = []
            for j in range(2):
                mh = lo if j == 0 else jnp.logical_not(lo)
                qh16 = jnp.where(mh, q, 0.0).astype(BF16)
                probs, _ = _swa_probs(qh16, kd[kvh], bias_ref[2 * hp + j], sink_ref[2 * hp + j], valid)
                outs.append(jnp.dot(probs.astype(BF16), vd[kvh], preferred_element_type=F32))
            o = jnp.where(lo, outs[0], outs[1])
            o_ref[:, sl] = o
            g = g_ref[:, sl]
            y_ref[:, sl] = (o * (g * _sigmoid(g))).astype(BF16)

    q, g, kvc, kvp, bias, sinks_spec = _swa_specs()
    out = pl.BlockSpec((QB, D_RNN), lambda n: (n, 0))
    return pl.pallas_call(
        body,
        name="swa_fwd",
        grid=(N_QB,),
        in_specs=[q, g, kvc, kvp, bias, sinks_spec],
        out_specs=[out, out],
        out_shape=[jax.ShapeDtypeStruct((S, D_RNN), BF16), jax.ShapeDtypeStruct((S, D_RNN), F32)],
        compiler_params=_params(("parallel",)),
    )(p_b, p_b, p_b, p_b, bias_t, sinks)


def _swa_bwd(dy, p_b, o_swa, bias_t, sinks, after=None):
    def body(dy_ref, q_ref, g_ref, kvc_ref, kvp_ref, o_ref, bias_ref, sink_ref, *rest):
        dp_ref, dk_ref, dv_ref, dbias_ref, dsink_ref = rest[-5:]
        n = pl.program_id(0)

        @pl.when(n == 0)
        def _():
            for ref in (dk_ref, dv_ref, dbias_ref, dsink_ref):
                ref[...] = jnp.zeros_like(ref)

        lo, kd, vd = _swa_keys(kvc_ref, kvp_ref)
        hi = jnp.logical_not(lo)
        valid = _swa_valid(n)
        dk_blk = jnp.zeros((KB2, LANE), F32)
        dv_blk = jnp.zeros((KB2, LANE), F32)
        for kvh in range(2):
            dk_pair = jnp.zeros((KB2, LANE), F32)
            dv_pair = jnp.zeros((KB2, LANE), F32)
            for hp in range(kvh * (N_PAIR // 2), (kvh + 1) * (N_PAIR // 2)):
                sl = slice(hp * LANE, (hp + 1) * LANE)
                q = q_ref[:, sl]
                g = g_ref[:, sl]
                o = o_ref[:, sl]
                dyv = dy_ref[:, sl]
                sg = _sigmoid(g)
                do = dyv * (g * sg)
                dp_ref[:, D_RNN + hp * LANE:D_RNN + (hp + 1) * LANE] = (
                    dyv * o * (sg * (1.0 + g * (1.0 - sg)))).astype(BF16)
                dqs = []
                for j in range(2):
                    h = 2 * hp + j
                    mh = lo if j == 0 else hi
                    qh16 = jnp.where(mh, q, 0.0).astype(BF16)
                    sink = sink_ref[h]
                    probs, psink = _swa_probs(qh16, kd[kvh], bias_ref[h], sink, valid)
                    doh = jnp.where(mh, do, 0.0)
                    doh16 = doh.astype(BF16)
                    delta = jnp.sum(doh * o, axis=-1, keepdims=True)
                    dpr = lax.dot_general(doh16, vd[kvh], _DIMS["nt"], preferred_element_type=F32)
                    ds = probs * (dpr - delta)
                    dbias_ref[h] += ds
                    dsink_ref[h:h + 1, :] += jnp.zeros((1, LANE), F32) - jnp.sum(psink * delta)
                    ds16 = (ds * (SWA_HD ** -0.5)).astype(BF16)
                    dqs.append(jnp.dot(ds16, kd[kvh], preferred_element_type=F32))
                    dk_pair = dk_pair + lax.dot_general(ds16, qh16, _DIMS["tn"], preferred_element_type=F32)
                    dv_pair = dv_pair + lax.dot_general(probs.astype(BF16), doh16, _DIMS["tn"],
                                                        preferred_element_type=F32)
                dp_ref[:, sl] = jnp.where(lo, dqs[0], dqs[1]).astype(BF16)
            keep = lo if kvh == 0 else hi
            dk_blk = dk_blk + jnp.where(keep, dk_pair + pltpu.roll(dk_pair, SWA_HD, 1), 0.0)
            dv_blk = dv_blk + jnp.where(keep, dv_pair + pltpu.roll(dv_pair, SWA_HD, 1), 0.0)

        cur = pl.ds(pl.multiple_of(n * QB, QB), QB)
        dk_ref[cur, :] += dk_blk[QB:KB2]
        dv_ref[cur, :] += dv_blk[QB:KB2]

        @pl.when(n > 0)
        def _():
            prev = pl.ds(pl.multiple_of((n - 1) * QB, QB), QB)
            dk_ref[prev, :] += dk_blk[0:QB]
            dv_ref[prev, :] += dv_blk[0:QB]

    q, g, kvc, kvp, bias, sinks_spec = _swa_specs()
    row = pl.BlockSpec((QB, D_RNN), lambda n: (n, 0))
    acc = pl.BlockSpec((S, LANE), lambda n: (0, 0))
    return pl.pallas_call(
        body,
        name="swa_bwd",
        grid=(N_QB,),
        in_specs=[row, q, g, kvc, kvp, row, bias, sinks_spec] + ([ANY] if after is not None else []),
        out_specs=[pl.BlockSpec((QB, 2 * D_RNN), lambda n: (n, 0)), acc, acc, bias,
                   pl.BlockSpec((SWA_HEADS, LANE), lambda n: (0, 0))],
        out_shape=[jax.ShapeDtypeStruct((S, GROUP_TILES["B"] * LANE), BF16),
                   jax.ShapeDtypeStruct((S, LANE), F32), jax.ShapeDtypeStruct((S, LANE), F32),
                   jax.ShapeDtypeStruct((SWA_HEADS, QB, KB2), F32),
                   jax.ShapeDtypeStruct((SWA_HEADS, LANE), F32)],
        compiler_params=_params(("arbitrary",)),
    )(dy, p_b, p_b, p_b, p_b, o_swa, bias_t, sinks, *([after] if after is not None else []))


def _swa_pack(dp_b, dk, dv, ts=512):
    def body(_, dk_ref, dv_ref, o_ref):
        o_ref[:, 0:LANE] = dk_ref[...].astype(BF16)
        o_ref[:, LANE:2 * LANE] = dv_ref[...].astype(BF16)

    tile = pl.BlockSpec((ts, LANE), lambda i: (i, 0))
    return pl.pallas_call(
        body,
        name="swa_pack",
        grid=(S // ts,),
        in_specs=[pl.BlockSpec(memory_space=pl.ANY), tile, tile],
        out_specs=pl.BlockSpec((ts, 2 * LANE), lambda i: (i, 8)),
        out_shape=jax.ShapeDtypeStruct(dp_b.shape, dp_b.dtype),
        input_output_aliases={0: 0},
        compiler_params=_params(("parallel",)),
    )(dp_b, dk, dv)


def _split3(v):
    a = v.astype(BF16)
    r = v - a.astype(F32)
    b = r.astype(BF16)
    c = (r - b.astype(F32)).astype(BF16)
    return a, b, c


def _relbias_grad(dbias_flat, onehot_t):
    def body(d_ref, e_ref, o_ref):
        e = e_ref[...]
        acc = jnp.zeros((SWA_HEADS, REL_BUCKETS), F32)
        for term in _split3(d_ref[...]):
            acc = acc + lax.dot_general(term, e, _DIMS["nt"], preferred_element_type=F32)
        o_ref[...] = acc

    return pl.pallas_call(
        body,
        name="relbias_grad",
        out_shape=jax.ShapeDtypeStruct((SWA_HEADS, REL_BUCKETS), F32),
        compiler_params=_params(),
    )(dbias_flat, onehot_t)


TS_MEM = 512


def _mem_probs(q16, mk):
    lg = lax.dot_general(q16, mk, _DIMS["nt"], preferred_element_type=F32) * (MEM_HD ** -0.5)
    p = jnp.exp(lg - jnp.max(lg, axis=-1, keepdims=True))
    return p / jnp.sum(p, axis=-1, keepdims=True)


def _mem_fwd(p_c, mkv):
    def body(q_ref, g_ref, mkv_ref, y_ref, o_ref):
        for hm in range(MEM_HEADS):
            sl = slice(hm * MEM_HD, (hm + 1) * MEM_HD)
            probs = _mem_probs(q_ref[:, sl].astype(BF16), mkv_ref[:, sl])
            o = jnp.dot(probs.astype(BF16), mkv_ref[:, D_RNN + hm * MEM_HD:D_RNN + (hm + 1) * MEM_HD],
                        preferred_element_type=F32)
            o_ref[:, sl] = o
            g = g_ref[:, sl]
            y_ref[:, sl] = (o * (g * _sigmoid(g))).astype(BF16)

    blk = lambda c: pl.BlockSpec((TS_MEM, D_RNN), lambda i: (i, c))
    return pl.pallas_call(
        body,
        name="mem_fwd",
        grid=(S // TS_MEM,),
        in_specs=[blk(0), blk(1), pl.BlockSpec((MEM, 2 * D_RNN), lambda i: (0, 0))],
        out_specs=[blk(0), blk(0)],
        out_shape=[jax.ShapeDtypeStruct((S, D_RNN), BF16), jax.ShapeDtypeStruct((S, D_RNN), F32)],
        compiler_params=_params(("parallel",)),
    )(p_c, p_c, mkv)


def _mem_bwd(dy, p_c, o_mem, mkv):
    def body(dy_ref, q_ref, g_ref, o_ref, mkv_ref, dp_ref, dmkv_ref):
        @pl.when(pl.program_id(0) == 0)
        def _():
            dmkv_ref[...] = jnp.zeros_like(dmkv_ref)

        for hm in range(MEM_HEADS):
            sl = slice(hm * MEM_HD, (hm + 1) * MEM_HD)
            sv = slice(D_RNN + hm * MEM_HD, D_RNN + (hm + 1) * MEM_HD)
            q16 = q_ref[:, sl].astype(BF16)
            mk, mv = mkv_ref[:, sl], mkv_ref[:, sv]
            probs = _mem_probs(q16, mk)
            g, o, dyv = g_ref[:, sl], o_ref[:, sl], dy_ref[:, sl]
            sg = _sigmoid(g)
            do = dyv * (g * sg)
            dp_ref[:, sv] = (dyv * o * (sg * (1.0 + g * (1.0 - sg)))).astype(BF16)
            do16 = do.astype(BF16)
            delta = jnp.sum(do * o, axis=-1, keepdims=True)
            dpr = lax.dot_general(do16, mv, _DIMS["nt"], preferred_element_type=F32)
            ds16 = (probs * (dpr - delta) * (MEM_HD ** -0.5)).astype(BF16)
            dp_ref[:, sl] = jnp.dot(ds16, mk, preferred_element_type=F32).astype(BF16)
            dmkv_ref[:, sl] += lax.dot_general(ds16, q16, _DIMS["tn"], preferred_element_type=F32)
            dmkv_ref[:, sv] += lax.dot_general(probs.astype(BF16), do16, _DIMS["tn"], preferred_element_type=F32)

    blk = lambda c: pl.BlockSpec((TS_MEM, D_RNN), lambda i: (i, c))
    kv = pl.BlockSpec((MEM, 2 * D_RNN), lambda i: (0, 0))
    return pl.pallas_call(
        body,
        name="mem_bwd",
        grid=(S // TS_MEM,),
        in_specs=[blk(0), blk(0), blk(1), blk(0), kv],
        out_specs=[pl.BlockSpec((TS_MEM, 2 * D_RNN), lambda i: (i, 0)), kv],
        out_shape=[jax.ShapeDtypeStruct((S, 2 * D_RNN), BF16), jax.ShapeDtypeStruct((MEM, 2 * D_RNN), F32)],
        compiler_params=_params(("arbitrary",)),
    )(dy, p_c, p_c, o_mem, mkv)


TS_MRG = 512
TD_MRG = 512
N_DBLK = D // TD_MRG


def _merge_fwd(z, p_d):
    def body(z0, z1, z2, g0, g1, g2, o_ref):
        o_ref[...] = (_sigmoid(g0[...]) * z0[...] + _sigmoid(g1[...]) * z1[...]
                      + _sigmoid(g2[...]) * z2[...]).astype(BF16)

    blk = pl.BlockSpec((TS_MRG, TD_MRG), lambda i, d: (i, d))
    gate = lambda b: pl.BlockSpec((TS_MRG, TD_MRG), lambda i, d: (i, b * N_DBLK + d))
    return pl.pallas_call(
        body,
        name="merge_fwd",
        grid=(S // TS_MRG, N_DBLK),
        in_specs=[blk, blk, blk, gate(0), gate(1), gate(2)],
        out_specs=blk,
        out_shape=jax.ShapeDtypeStruct((S, D), BF16),
        compiler_params=_params(("parallel", "parallel")),
    )(z[0], z[1], z[2], p_d, p_d, p_d)


def _merge_bwd(dmerged, z_b, p_d, b, dp_d, after=None):
    def body(dm_ref, z_ref, g_ref, *refs):
        dz_ref, dg_ref = refs[-2], refs[-1]
        sg = _sigmoid(g_ref[...])
        dm = dm_ref[...]
        dz_ref[...] = (dm * sg).astype(BF16)
        dg_ref[...] = (dm * z_ref[...] * sg * (1.0 - sg)).astype(BF16)

    blk = pl.BlockSpec((TS_MRG, TD_MRG), lambda i, d: (i, d))
    gate = pl.BlockSpec((TS_MRG, TD_MRG), lambda i, d: (i, b * N_DBLK + d))
    in_specs = [blk, blk, gate]
    args = [dmerged, z_b, p_d]
    aliases = {}
    if dp_d is not None:
        in_specs.append(pl.BlockSpec(memory_space=pl.ANY))
        args.append(dp_d)
        aliases = {3: 1}
    if after is not None:
        in_specs.append(pl.BlockSpec(memory_space=pl.ANY))
        args.append(after)
    return pl.pallas_call(
        body,
        name=f"merge_bwd{b}",
        grid=(S // TS_MRG, N_DBLK),
        in_specs=in_specs,
        out_specs=[blk, gate],
        out_shape=[jax.ShapeDtypeStruct((S, D), BF16),
                   jax.ShapeDtypeStruct((S, GROUP_TILES["D"] * LANE), BF16)],
        input_output_aliases=aliases,
        compiler_params=_params(("parallel", "parallel")),
    )(*args)


def _bucket_table():
    import numpy as np
    qi = np.arange(QB)[:, None]
    kj = np.arange(KB2)[None, :]
    n = np.maximum(qi + WINDOW - kj, 0)
    max_exact = REL_BUCKETS // 2
    ratio = np.log(np.maximum(n, 1).astype(np.float32) / max_exact) / np.float32(math.log(REL_MAX_DIST / max_exact))
    large = np.minimum(max_exact + (ratio * (REL_BUCKETS - max_exact)).astype(np.int32), REL_BUCKETS - 1)
    bucket = np.where(n < max_exact, n, large).reshape(1, QB * KB2)
    return (bucket == np.arange(REL_BUCKETS)[:, None]).astype(np.float32)


def _bias_expand(rel_bias_t, onehot_t):
    def body(r_ref, e_ref, o_ref):
        e = e_ref[...]
        acc = jnp.zeros((SWA_HEADS, QB * KB2), F32)
        for term in _split3(r_ref[...]):
            acc = acc + jnp.dot(term, e, preferred_element_type=F32)
        o_ref[...] = acc

    return pl.pallas_call(
        body,
        name="bias_expand",
        out_shape=jax.ShapeDtypeStruct((SWA_HEADS, QB * KB2), F32),
        compiler_params=_params(),
    )(rel_bias_t, onehot_t)


PROJ_TN = {"A": 1024, "B": 1152, "C": 1024, "D": 1536}


def _local_step(x, h, mem, tgt, sp, fetch, prefetch, emit, advance):
    onehot_t = jnp.asarray(_bucket_table(), BF16)
    bias_t = _bias_expand(sp["rel_bias"].T, onehot_t).reshape(SWA_HEADS, QB, KB2)
    sinks = sp["swa_sinks"].reshape(SWA_HEADS)
    wa16, wx16 = sp["w_rg_a"].astype(BF16), sp["w_rg_x"].astype(BF16)
    rnn = (sp["conv_w"], sp["conv_b"], wa16, sp["b_rg_a"], wx16, sp["b_rg_x"], sp["lru_lambda"])

    memn = _rms_fwd(mem, sp["mem_norm_g"], "rms_mem", h)
    w_grp, p = {}, {}

    def project(g, after, then=None):
        (w_grp[g],) = fetch((g,), after)
        tok = prefetch(then, w_grp[g]) if then is not None else None
        p[g] = _mm(h, w_grp[g], "nt", F32, 1024, PROJ_TN[g], D, f"proj_{g}", after=tok)

    project("A", h)
    y_rg, hseq = _rglru_fwd(p["A"], *rnn)
    project("B", y_rg)
    y_swa, o_swa = _swa_fwd(p["B"], bias_t, sinks)
    project("C", y_swa, then=("mk",))
    (wmk,) = fetch(("mk",), p["C"])
    tok = prefetch(("br0", "br1", "br2"), wmk)
    mkv = _mm(memn, wmk, "nn", BF16, MEM, 1024, D, "mkv", after=tok)
    y_mem, o_mem = _mem_fwd(p["C"], mkv)
    ys = (y_rg, y_swa, y_mem)
    wbr = fetch(("br0", "br1", "br2"), y_mem)
    tok = prefetch(("D",), wbr[2])
    z = []
    for b in range(3):
        z.append(_mm(ys[b], wbr[b], "nn", F32, 1024, 1024, D_RNN, f"branch_out{b}", after=z[-1] if z else tok))
    project("D", z[2], then=("out",))
    merged = _merge_fwd(z, p["D"])
    (wout,) = fetch(("out",), merged)
    out = _mm(merged, wout, "nn", F32, 1024, 1024, D, "out_proj")
    sq, dy, dout, d_post = _post_loss(out, x, tgt, sp["post_norm_g"])

    tok = emit({"out": _mm(merged, dout, "tn", BF16, 1024, 1024, S, "d_wout")})
    dmerged = _mm(dout, wout, "nt", F32, 1024, 1024, D, "d_merged", after=tok)
    dz, dp_d = [], None
    tok = advance(dmerged)
    for b in range(3):
        dz_b, dp_d = _merge_bwd(dmerged, z[b], p["D"], b, dp_d, after=tok if b == 0 else None)
        dz.append(dz_b)
    d_win = lambda g, dp_g, after=None: _mm(dp_g, h, "tn", BF16, PROJ_TN[g], 1024, S, f"d_win_{g}", after=after)
    tok = emit({f"br{b}": _mm(ys[b], dz[b], "tn", BF16, 1024, 1024, S, f"d_wbr{b}") for b in range(3)}, tok)
    d_w_d = d_win("D", dp_d, tok)
    tok = emit({"D": d_w_d}, advance(d_w_d))
    dy_mem = _mm(dz[2], wbr[2], "nt", F32, 1024, 1024, D, "d_branch2", after=tok)
    tok = advance(dy_mem)
    dp_c, dmkv = _mem_bwd(dy_mem, p["C"], o_mem, mkv)
    dmkv16 = dmkv.astype(BF16)
    tok = emit({"mk": _mm(memn, dmkv16, "tn", BF16, 1024, 1024, MEM, "d_wmk", after=tok), "C": d_win("C", dp_c)}, tok)
    dmemn = _mm(dmkv16, wmk, "nt", F32, MEM, 1024, D, "d_memn", after=tok)
    tok = advance(dmemn)
    d_memg = _memnorm_bwd(dmemn, mem)
    dy_rg = _mm(dz[0], wbr[0], "nt", F32, 1024, 1024, D, "d_branch0", after=tok)
    dp_a, d_cw, d_cb, d_wa, d_ba, d_wx, d_bx, d_lam = _rglru_bwd(dy_rg, p["A"], hseq, *rnn)
    tok = emit({"A": d_win("A", dp_a)}, tok)
    dy_swa = _mm(dz[1], wbr[1], "nt", F32, 1024, 1024, D, "d_branch1", after=tok)
    tok = advance(dy_swa)
    dp_b, dk, dv, d_bias, d_sink = _swa_bwd(dy_swa, p["B"], o_swa, bias_t, sinks, after=tok)
    dp_b = _swa_pack(dp_b, dk, dv)
    d_rel = _relbias_grad(d_bias.reshape(SWA_HEADS, QB * KB2), onehot_t).T
    dp = {"A": dp_a, "B": dp_b, "C": dp_c, "D": dp_d}
    tok = emit({"B": d_win("B", dp_b)}, tok)
    dh = None
    for g in GROUPS:
        dh = _mm(dp[g], w_grp[g], "nn", F32, 1024, 1024, 2304 if g == "B" else 2048, f"d_h_{g}", acc=dh,
                 after=tok if g in ("A", "B") else None)
        if g == "A":
            tok = advance(dh)
    grad_x, d_pre = _pre_bwd(dh, x, dy, sp["pre_norm_g"])

    d_small = {
        "pre_norm_g": d_pre, "post_norm_g": d_post, "mem_norm_g": d_memg, "conv_w": d_cw, "conv_b": d_cb,
        "w_rg_a": d_wa, "b_rg_a": d_ba, "w_rg_x": d_wx, "b_rg_x": d_bx, "lru_lambda": d_lam,
        "swa_sinks": d_sink[:, 0].reshape(1, SWA_HEADS), "rel_bias": d_rel,
    }
    return sq, grad_x, d_small


ANY = pl.BlockSpec(memory_space=pl.ANY)
SHARD_ROWS = D // N_CHIPS
GATHERED = {"A": (2048, D), "B": (2304, D), "C": (2048, D), "D": (6144, D), "mk": (D, D),
            "br0": (D_RNN, D), "br1": (D_RNN, D), "br2": (D_RNN, D), "out": (D, D)}
SHARD_SHAPES = {"win": (SHARD, D), "mk": (SHARD_ROWS, D), "br0": (D_RNN, SHARD_ROWS), "br1": (D_RNN, SHARD_ROWS),
                "br2": (D_RNN, SHARD_ROWS), "out": (SHARD_ROWS, D)}
SHARDS = tuple(SHARD_SHAPES)
HALF_AXIS = {"win": 1, "mk": 1, "br0": 0, "br1": 0, "br2": 0, "out": 1,
             "A": 1, "B": 1, "C": 1, "D": 1}


def _halved(shape, axis):
    return (shape[0] // 2, shape[1]) if axis == 0 else (shape[0], shape[1] // 2)


class Piece(NamedTuple):
    src: str
    dst: str
    rows: int
    sr0: int
    sc0: int
    dr0: int
    dc0: int
    ncols: int


def _pieces_of(jj):
    out = [Piece("win", g, n, r, 0, gr, 0, D) for r, n, g, gr in _shard_runs(jj)]
    out.append(Piece("mk", "mk", SHARD_ROWS, 0, 0, SHARD_ROWS * jj, 0, D))
    out += [Piece(f"br{b}", f"br{b}", D_RNN, 0, 0, 0, SHARD_ROWS * jj, SHARD_ROWS) for b in range(3)]
    out.append(Piece("out", "out", SHARD_ROWS, 0, 0, SHARD_ROWS * jj, 0, D))
    return out


def _half_rect(ref, p, side, which):
    r0, c0 = (p.sr0, p.sc0) if side == "src" else (p.dr0, p.dc0)
    if HALF_AXIS[p.src] == 1:
        return _rect(ref, r0, p.rows, c0 + which * (p.ncols // 2), p.ncols // 2)
    return _rect(ref, r0 + which * (p.rows // 2), p.rows // 2, c0, p.ncols)


def _rect_in_half(ref, p, side):
    r0, c0 = (p.sr0, p.sc0) if side == "src" else (p.dr0, p.dc0)
    if HALF_AXIS[p.src] == 1:
        return _rect(ref, r0, p.rows, 0, p.ncols // 2)
    return _rect(ref, 0, p.rows // 2, c0, p.ncols)


MAX_PIECES = max(len(_pieces_of(jj)) for jj in range(N_CHIPS))


def _rect(ref, r0, rows, c0, ncols):
    return ref.at[pl.ds(r0, rows), pl.ds(c0, ncols)]


def _position():
    x, y, c = lax.axis_index("x"), lax.axis_index("y"), lax.axis_index("c")
    return x, y, c, 2 * x + y


HBM = pl.BlockSpec(memory_space=pltpu.HBM)
SEM = pl.BlockSpec(memory_space=pltpu.SEMAPHORE)
EFFECT = pltpu.SideEffectType.DATAFLOW_SIDE_EFFECTING
N_SEM = MAX_PIECES * N_CHIPS
GATHER_STAGES = (("A",), ("B",), ("C",), ("mk",), ("br0", "br1", "br2"), ("D",), ("out",))


def _in_hbm(a):
    return pltpu.with_memory_space_constraint(a, pltpu.HBM)


def _stage_pieces(jj, stage):
    return [(i, p) for i, p in enumerate(_pieces_of(jj)) if p.dst in stage]


def _own_block_table(g):
    import numpy as np
    units = np.full((N_CHIPS, GATHERED[g][0] // HALF_TILE), -1, np.int64)
    for jj in range(N_CHIPS):
        for r, n, grp, gr in _shard_runs(jj):
            if grp == g:
                for k in range(n // HALF_TILE):
                    units[jj, gr // HALF_TILE + k] = r // HALF_TILE + k
    tbl = np.zeros((N_CHIPS, 2, GATHERED[g][0] // LANE), np.int32)
    for jj in range(N_CHIPS):
        for b in range(tbl.shape[2]):
            first, second = units[jj, 2 * b], units[jj, 2 * b + 1]
            if jj % 2 == 0:
                src = first if first >= 0 else second - 1
                if first >= 0 or second >= 0:
                    assert src % 2 == 0
                    tbl[jj, :, b] = src // 2
            else:
                if first >= 0:
                    assert first % 2 == 1
                    tbl[jj, 0, b] = first // 2
                if second >= 0:
                    assert second % 2 == 0
                    tbl[jj, 1, b] = second // 2
    return tbl


def _place_group(w_t, g, tables, odd_arr, after):
    nb = GATHERED[g][0] // LANE

    def body(t_ref, odd_ref, a_ref, b_ref, _, o_ref):
        odd = odd_ref[0] == 1
        o_ref[0:HALF_TILE, :] = jnp.where(odd, a_ref[HALF_TILE:LANE, :], a_ref[0:HALF_TILE, :]).astype(BF16)
        o_ref[HALF_TILE:LANE, :] = jnp.where(odd, b_ref[0:HALF_TILE, :], a_ref[HALF_TILE:LANE, :]).astype(BF16)

    return pl.pallas_call(
        body,
        name=f"place_{g}",
        grid_spec=pltpu.PrefetchScalarGridSpec(
            num_scalar_prefetch=2,
            grid=(nb,),
            in_specs=[pl.BlockSpec((LANE, D), lambda b, t, o: (t[0, b], 0)),
                      pl.BlockSpec((LANE, D), lambda b, t, o: (t[1, b], 0)), ANY],
            out_specs=pl.BlockSpec((LANE, D), lambda b, t, o: (b, 0)),
        ),
        out_shape=jax.ShapeDtypeStruct(GATHERED[g], BF16),
        compiler_params=_params(("parallel",)),
    )(tables, odd_arr, w_t, w_t, after)


def _place_shard(shard, name, after):
    rows, cols = shard.shape
    by_rows = HALF_AXIS[name] == 1

    def body(x_ref, _, o_ref):
        o_ref[...] = x_ref[...].astype(BF16)

    return pl.pallas_call(
        body,
        name=f"place_{name}",
        grid=(N_CHIPS,),
        in_specs=[pl.BlockSpec((rows, cols), lambda b: (0, 0)), ANY],
        out_specs=pl.BlockSpec((rows, cols), (lambda b: (b, 0)) if by_rows else (lambda b: (0, b))),
        out_shape=jax.ShapeDtypeStruct(GATHERED[name], BF16),
        compiler_params=_params(("parallel",)),
    )(shard, after)


def _gather_copy(arr, send_sems, recv_sems, c, jj, i, p, kk):
    rect = _half_rect(arr[p.dst], p, "dst", c)
    return pltpu.make_async_remote_copy(
        src_ref=rect, dst_ref=rect, send_sem=send_sems.at[i * N_CHIPS + kk],
        recv_sem=recv_sems.at[jj * MAX_PIECES + i], device_id=(kk // 2, kk % 2, c), device_id_type=MESH)


def _gather_start(arrays, after):
    stage = tuple(arrays)
    na = len(stage)

    def body(*refs):
        arr = dict(zip(stage, refs[:na]))
        send_sems, recv_sems = refs[na + 1], refs[na + 2]
        token = refs[-1]
        _, _, c, j = _position()
        for jj in range(N_CHIPS):
            @pl.when(j == jj)
            def _():
                for i, p in _stage_pieces(jj, stage):
                    for kk in range(N_CHIPS):
                        if kk != jj:
                            _gather_copy(arr, send_sems, recv_sems, c, jj, i, p, kk).start()
        token[...] = jnp.zeros_like(token)

    outs = pl.pallas_call(
        body,
        name=f"gather_start_{stage[0]}",
        in_specs=[HBM] * na + [ANY],
        out_specs=[SEM, SEM] + [HBM] * na + [pl.BlockSpec(memory_space=pltpu.VMEM)],
        out_shape=[pltpu.SemaphoreType.DMA((N_SEM,)), pltpu.SemaphoreType.DMA((N_SEM,))]
        + [pltpu.HBM(GATHERED[n], BF16) for n in stage] + [jax.ShapeDtypeStruct((8, LANE), F32)],
        input_output_aliases={k: 2 + k for k in range(na)},
        compiler_params=pltpu.CompilerParams(has_side_effects=EFFECT),
    )(*[_in_hbm(arrays[n]) for n in stage], after)
    return outs[0], outs[1], dict(zip(stage, outs[2:2 + na])), outs[-1]


def _gather_wait(send_sems, recv_sems, arrays, after):
    stage = tuple(arrays)
    na = len(stage)

    def body(*refs):
        arr = dict(zip(stage, refs[:na]))
        sems_s, sems_r = refs[na], refs[na + 1]
        _, _, c, j = _position()
        for jj in range(N_CHIPS):
            @pl.when(j != jj)
            def _():
                for i, p in _stage_pieces(jj, stage):
                    _gather_copy(arr, sems_s, sems_r, c, jj, i, p, jj).wait_recv()

            @pl.when(j == jj)
            def _():
                for i, p in _stage_pieces(jj, stage):
                    for kk in range(N_CHIPS):
                        if kk != jj:
                            _gather_copy(arr, sems_s, sems_r, c, jj, i, p, kk).wait_send()

    outs = pl.pallas_call(
        body,
        name=f"gather_wait_{stage[0]}",
        in_specs=[HBM] * na + [SEM, SEM, ANY],
        out_specs=[HBM] * na,
        out_shape=[pltpu.HBM(GATHERED[n], BF16) for n in stage],
        input_output_aliases={k: k for k in range(na)},
        compiler_params=pltpu.CompilerParams(has_side_effects=EFFECT),
    )(*[arrays[n] for n in stage], send_sems, recv_sems, after)
    return dict(zip(stage, outs))


def _gather_swap(arrays):
    stage = tuple(arrays)
    na = len(stage)

    def body(*refs):
        dst = dict(zip(stage, refs[na:2 * na]))
        send_sems, recv_sems = refs[2 * na:]
        x, y, c, j = _position()

        def fwd(jj, i, p, which):
            rect = _half_rect(dst[p.dst], p, "dst", which)
            return pltpu.make_async_remote_copy(
                src_ref=rect, dst_ref=rect, send_sem=send_sems.at[jj * MAX_PIECES + i],
                recv_sem=recv_sems.at[jj * MAX_PIECES + i], device_id=(x, y, 1 - c), device_id_type=MESH)

        for jj in range(N_CHIPS):
            @pl.when(j != jj)
            def _():
                for i, p in _stage_pieces(jj, stage):
                    fwd(jj, i, p, c).start()
        for jj in range(N_CHIPS):
            @pl.when(j != jj)
            def _():
                for i, p in _stage_pieces(jj, stage):
                    fwd(jj, i, p, 1 - c).wait_recv()
        for jj in range(N_CHIPS):
            @pl.when(j != jj)
            def _():
                for i, p in _stage_pieces(jj, stage):
                    fwd(jj, i, p, c).wait_send()

    outs = pl.pallas_call(
        body,
        name=f"gather_swap_{stage[0]}",
        in_specs=[ANY] * na,
        out_specs=[ANY] * na,
        out_shape=[jax.ShapeDtypeStruct(GATHERED[n], BF16) for n in stage],
        input_output_aliases={k: k for k in range(na)},
        scratch_shapes=[pltpu.SemaphoreType.DMA((N_SEM,)), pltpu.SemaphoreType.DMA((N_SEM,))],
        compiler_params=pltpu.CompilerParams(has_side_effects=True),
    )(*[arrays[n] for n in stage])
    return dict(zip(stage, outs))


def _pass_on_copy(arr, send_sems, recv_sems, x, y, c, jj, i, p, which):
    rect = _half_rect(arr[p.dst], p, "dst", which)
    return pltpu.make_async_remote_copy(
        src_ref=rect, dst_ref=rect, send_sem=send_sems.at[jj * MAX_PIECES + i],
        recv_sem=recv_sems.at[jj * MAX_PIECES + i], device_id=(x, y, 1 - c), device_id_type=MESH)


def _gather_pass_start(arrays, after):
    stage = tuple(arrays)
    na = len(stage)

    def body(*refs):
        arr = dict(zip(stage, refs[:na]))
        x, y, c, j = _position()
        for jj in range(N_CHIPS):
            @pl.when(j != jj)
            def _():
                for i, p in _stage_pieces(jj, stage):
                    _pass_on_copy(arr, refs[na + 1], refs[na + 2], x, y, c, jj, i, p, c).start()
        refs[-1][...] = jnp.zeros_like(refs[-1])

    outs = pl.pallas_call(
        body,
        name=f"gather_pass_start_{stage[0]}",
        in_specs=[HBM] * na + [ANY],
        out_specs=[SEM, SEM] + [HBM] * na + [pl.BlockSpec(memory_space=pltpu.VMEM)],
        out_shape=[pltpu.SemaphoreType.DMA((N_SEM,)), pltpu.SemaphoreType.DMA((N_SEM,))]
        + [pltpu.HBM(GATHERED[n], BF16) for n in stage] + [jax.ShapeDtypeStruct((8, LANE), F32)],
        input_output_aliases={k: 2 + k for k in range(na)},
        compiler_params=pltpu.CompilerParams(has_side_effects=EFFECT),
    )(*[arrays[n] for n in stage], after)
    return outs[0], outs[1], dict(zip(stage, outs[2:2 + na])), outs[-1]


def _gather_pass_wait(send_sems, recv_sems, arrays, after):
    stage = tuple(arrays)
    na = len(stage)

    def body(*refs):
        arr = dict(zip(stage, refs[:na]))
        x, y, c, j = _position()
        for jj in range(N_CHIPS):
            @pl.when(j != jj)
            def _():
                for i, p in _stage_pieces(jj, stage):
                    _pass_on_copy(arr, refs[na], refs[na + 1], x, y, c, jj, i, p, 1 - c).wait_recv()
                    _pass_on_copy(arr, refs[na], refs[na + 1], x, y, c, jj, i, p, c).wait_send()

    outs = pl.pallas_call(
        body,
        name=f"gather_pass_wait_{stage[0]}",
        in_specs=[HBM] * na + [SEM, SEM, ANY],
        out_specs=[HBM] * na,
        out_shape=[pltpu.HBM(GATHERED[n], BF16) for n in stage],
        input_output_aliases={k: k for k in range(na)},
        compiler_params=pltpu.CompilerParams(has_side_effects=EFFECT),
    )(*[arrays[n] for n in stage], send_sems, recv_sems, after)
    return dict(zip(stage, outs))


def _own_half(ref, shape, axis, which):
    if axis == 1:
        return ref.at[:, pl.ds(which * (shape[1] // 2), shape[1] // 2)]
    return ref.at[pl.ds(which * (shape[0] // 2), shape[0] // 2), :]


def _swap_copies(names, src, dst, send_sems, recv_sems):
    x, y, c, _ = _position()
    return [pltpu.make_async_remote_copy(
        src_ref=_own_half(src[n], GATHERED[n], HALF_AXIS[n], 1 - c), dst_ref=dst[n],
        send_sem=send_sems.at[k], recv_sem=recv_sems.at[k],
        device_id=(x, y, 1 - c), device_id_type=MESH) for k, n in enumerate(names)]


def _swap_start(grads, after):
    names = tuple(grads)
    n = len(names)

    def body(*refs):
        src, dst = dict(zip(names, refs[:n])), dict(zip(names, refs[n:2 * n]))
        for cp in _swap_copies(names, src, dst, refs[2 * n + 1], refs[2 * n + 2]):
            cp.start()
        refs[-1][...] = jnp.zeros_like(refs[-1])

    half_shape = lambda nm: _halved(GATHERED[nm], HALF_AXIS[nm])
    args = [_in_hbm(grads[nm]) for nm in names] + [_in_hbm(lax.empty(half_shape(nm), BF16)) for nm in names]
    if after is None:
        after = jnp.zeros((8, LANE), F32)
    outs = pl.pallas_call(
        body,
        name=f"swap_start_{names[0]}",
        in_specs=[HBM] * (2 * n) + [ANY],
        out_specs=[SEM, SEM] + [HBM] * (2 * n) + [pl.BlockSpec(memory_space=pltpu.VMEM)],
        out_shape=[pltpu.SemaphoreType.DMA((n,)), pltpu.SemaphoreType.DMA((n,))]
        + [pltpu.HBM(GATHERED[nm], BF16) for nm in names] + [pltpu.HBM(half_shape(nm), BF16) for nm in names]
        + [jax.ShapeDtypeStruct((8, LANE), F32)],
        input_output_aliases={k: 2 + k for k in range(2 * n)},
        compiler_params=pltpu.CompilerParams(has_side_effects=EFFECT),
    )(*args, after)
    return outs[0], outs[1], dict(zip(names, outs[2:2 + n])), dict(zip(names, outs[2 + n:2 + 2 * n])), outs[-1]


def _swap_wait(send_sems, recv_sems, grads, landing, after):
    names = tuple(grads)
    n = len(names)

    def body(*refs):
        src, dst = dict(zip(names, refs[:n])), dict(zip(names, refs[n:2 * n]))
        copies = _swap_copies(names, src, dst, refs[2 * n], refs[2 * n + 1])
        for cp in copies:
            cp.wait_recv()
        for cp in copies:
            cp.wait_send()

    half_shape = lambda nm: _halved(GATHERED[nm], HALF_AXIS[nm])
    outs = pl.pallas_call(
        body,
        name=f"swap_wait_{names[0]}",
        in_specs=[HBM] * (2 * n) + [SEM, SEM, ANY],
        out_specs=[HBM] * (2 * n),
        out_shape=[pltpu.HBM(GATHERED[nm], BF16) for nm in names] + [pltpu.HBM(half_shape(nm), BF16) for nm in names],
        input_output_aliases={k: k for k in range(2 * n)},
        compiler_params=pltpu.CompilerParams(has_side_effects=EFFECT),
    )(*[grads[nm] for nm in names], *[landing[nm] for nm in names], send_sems, recv_sems, after)
    return dict(zip(names, outs[:n])), dict(zip(names, outs[n:]))


ADD_ROWS = 256


def _add_half(full, recv, c_arr, name):
    rows, cols = recv.shape
    if HALF_AXIS[name] == 1:
        index = lambda i, c_ref: (i, c_ref[0])
    else:
        nb = rows // ADD_ROWS
        index = lambda i, c_ref: (nb * c_ref[0] + i, 0)

    def body(c_ref, a_ref, b_ref, o_ref):
        o_ref[...] = (a_ref[...].astype(F32) + b_ref[...].astype(F32)).astype(BF16)

    return pl.pallas_call(
        body,
        name=f"add_half_{name}",
        grid_spec=pltpu.PrefetchScalarGridSpec(
            num_scalar_prefetch=1,
            grid=(rows // ADD_ROWS,),
            in_specs=[pl.BlockSpec((ADD_ROWS, cols), index), pl.BlockSpec((ADD_ROWS, cols), lambda i, c_ref: (i, 0))],
            out_specs=pl.BlockSpec((ADD_ROWS, cols), lambda i, c_ref: (i, 0)),
        ),
        out_shape=jax.ShapeDtypeStruct((rows, cols), BF16),
        compiler_params=_params(("parallel",)),
    )(c_arr, full, recv)


SLOT_SHAPES = {n: _halved(SHARD_SHAPES[n], HALF_AXIS[n]) for n in SHARDS}


def _slot_shape(n):
    return (N_CHIPS,) + SLOT_SHAPES[n]


def _stage_shards(stage):
    pieces = [p for jj in range(N_CHIPS) for p in _pieces_of(jj)]
    return tuple(s for s in SHARDS if any(p.src == s and p.dst in stage for p in pieces))


def _scatter_copy(src, dst, send_sems, recv_sems, c, jj, kk, i, p):
    return pltpu.make_async_remote_copy(
        src_ref=_rect_in_half(src[p.dst], p, "dst"), dst_ref=_rect_in_half(dst[p.src].at[jj], p, "src"),
        send_sem=send_sems.at[kk * MAX_PIECES + i], recv_sem=recv_sems.at[jj * MAX_PIECES + i],
        device_id=(kk // 2, kk % 2, c), device_id_type=MESH)


def _scatter_start(halves, slots):
    stage, touched = tuple(halves), tuple(slots)
    nh, nt = len(stage), len(touched)

    def body(*refs):
        src = dict(zip(stage, refs[:nh]))
        dst = dict(zip(touched, refs[nh:nh + nt]))
        send_sems, recv_sems = refs[nh + nt], refs[nh + nt + 1]
        token = refs[-1]
        _, _, c, j = _position()
        for jj in range(N_CHIPS):
            @pl.when(j == jj)
            def _():
                for kk in range(N_CHIPS):
                    if kk != jj:
                        for i, p in _stage_pieces(kk, stage):
                            _scatter_copy(src, dst, send_sems, recv_sems, c, jj, kk, i, p).start()
        token[...] = jnp.zeros_like(token)

    outs = pl.pallas_call(
        body,
        name=f"scatter_start_{stage[0]}",
        in_specs=[HBM] * (nh + nt),
        out_specs=[SEM, SEM] + [HBM] * (nh + nt) + [pl.BlockSpec(memory_space=pltpu.VMEM)],
        out_shape=[pltpu.SemaphoreType.DMA((N_SEM,)), pltpu.SemaphoreType.DMA((N_SEM,))]
        + [pltpu.HBM(halves[n].shape, BF16) for n in stage] + [pltpu.HBM(_slot_shape(s), BF16) for s in touched]
        + [jax.ShapeDtypeStruct((8, LANE), F32)],
        input_output_aliases={k: 2 + k for k in range(nh + nt)},
        compiler_params=pltpu.CompilerParams(has_side_effects=EFFECT),
    )(*[_in_hbm(halves[n]) for n in stage], *[_in_hbm(slots[s]) for s in touched])
    return outs[0], outs[1], dict(zip(stage, outs[2:2 + nh])), dict(zip(touched, outs[2 + nh:2 + nh + nt])), outs[-1]


def _scatter_wait(send_sems, recv_sems, halves, slots, after):
    stage, touched = tuple(halves), tuple(slots)
    nh, nt = len(stage), len(touched)

    def body(*refs):
        src = dict(zip(stage, refs[:nh]))
        dst = dict(zip(touched, refs[nh:nh + nt]))
        sems_s, sems_r = refs[nh + nt], refs[nh + nt + 1]
        _, _, c, j = _position()
        for jj in range(N_CHIPS):
            @pl.when(j == jj)
            def _():
                for ss in range(N_CHIPS):
                    if ss != jj:
                        for i, p in _stage_pieces(jj, stage):
                            _scatter_copy(src, dst, sems_s, sems_r, c, ss, jj, i, p).wait_recv()
                for kk in range(N_CHIPS):
                    if kk != jj:
                        for i, p in _stage_pieces(kk, stage):
                            _scatter_copy(src, dst, sems_s, sems_r, c, jj, kk, i, p).wait_send()

    outs = pl.pallas_call(
        body,
        name=f"scatter_wait_{stage[0]}",
        in_specs=[HBM] * (nh + nt) + [SEM, SEM, ANY],
        out_specs=[HBM] * (nh + nt),
        out_shape=[pltpu.HBM(halves[n].shape, BF16) for n in stage] + [pltpu.HBM(_slot_shape(s), BF16) for s in touched],
        input_output_aliases={k: k for k in range(nh + nt)},
        compiler_params=pltpu.CompilerParams(has_side_effects=EFFECT),
    )(*[halves[n] for n in stage], *[slots[s] for s in touched], send_sems, recv_sems, after)
    return dict(zip(stage, outs[:nh])), dict(zip(touched, outs[nh:]))


SUM_ROWS = {"mk": 256, "br0": 256, "br1": 256, "br2": 256, "out": 256}


def _sum_in_chip_order(chip, own, s_ref):
    acc = None
    for k in range(N_CHIPS):
        term = jnp.where(chip == k, own, s_ref[k].astype(F32))
        acc = term if acc is None else acc + term
    return acc


def _sum_slots(slots, own_half, pos_arr, name):
    _, rows, cols = slots.shape
    tr = SUM_ROWS[name]
    nb = rows // tr
    if HALF_AXIS[name] == 1:
        own_index = lambda i, pos: (nb * pos[1] + i, 0)
        out_index = lambda i, pos: (i, pos[0])
    else:
        own_index = lambda i, pos: (i, pos[1])
        out_index = lambda i, pos: (nb * pos[0] + i, 0)

    def body(pos, s_ref, own_ref, o_ref):
        o_ref[...] = _sum_in_chip_order(pos[1], own_ref[...].astype(F32), s_ref)

    return pl.pallas_call(
        body,
        name=f"sum_slots_{name}",
        grid_spec=pltpu.PrefetchScalarGridSpec(
            num_scalar_prefetch=1,
            grid=(nb,),
            in_specs=[pl.BlockSpec((N_CHIPS, tr, cols), lambda i, pos: (0, i, 0)),
                      pl.BlockSpec((tr, cols), own_index)],
            out_specs=pl.BlockSpec((tr, cols), out_index),
        ),
        out_shape=jax.ShapeDtypeStruct(SHARD_SHAPES[name], F32),
        compiler_params=_params(("parallel",)),
    )(pos_arr, slots, own_half)


def _own_partial_tables():
    import numpy as np
    nb = SHARD // HALF_TILE
    grp, blk = np.zeros((N_CHIPS, nb), np.int32), np.zeros((N_CHIPS, nb), np.int32)
    for jj in range(N_CHIPS):
        for r, n, g, gr in _shard_runs(jj):
            for k in range(n // HALF_TILE):
                grp[jj, r // HALF_TILE + k] = GROUPS.index(g)
                blk[jj, r // HALF_TILE + k] = gr // HALF_TILE + k
    return grp, blk


def _sum_slots_win(slots, own_halves, pos_arr, grp_tbl, blk_tbl):
    nb = SHARD // HALF_TILE
    cols = D // 2

    def own_spec(gi):
        return pl.BlockSpec((HALF_TILE, cols), lambda b, pos, grp, blk: (jnp.where(grp[b] == gi, blk[b], 0), 0))

    def body(pos, grp, blk, s_ref, a_ref, b_ref, c_ref, d_ref, o_ref):
        g = grp[pl.program_id(0)]
        own = a_ref[...]
        for gi, ref in ((1, b_ref), (2, c_ref), (3, d_ref)):
            own = jnp.where(g == gi, ref[...], own)
        o_ref[...] = _sum_in_chip_order(pos[1], own.astype(F32), s_ref)

    return pl.pallas_call(
        body,
        name="sum_slots_win",
        grid_spec=pltpu.PrefetchScalarGridSpec(
            num_scalar_prefetch=3,
            grid=(nb,),
            in_specs=[pl.BlockSpec((N_CHIPS, HALF_TILE, cols), lambda b, pos, grp, blk: (0, b, 0))]
            + [own_spec(gi) for gi in range(len(GROUPS))],
            out_specs=pl.BlockSpec((HALF_TILE, cols), lambda b, pos, grp, blk: (b, pos[0])),
        ),
        out_shape=jax.ShapeDtypeStruct(SHARD_SHAPES["win"], F32),
        compiler_params=_params(("parallel",)),
    )(pos_arr, grp_tbl, blk_tbl, slots, *[own_halves[g] for g in GROUPS])


def _share_copy(buf, name, send_sems, recv_sems, k, which):
    x, y, c, _ = _position()
    half = _own_half(buf, SHARD_SHAPES[name], HALF_AXIS[name], which)
    return pltpu.make_async_remote_copy(src_ref=half, dst_ref=half, send_sem=send_sems.at[k], recv_sem=recv_sems.at[k],
                                        device_id=(x, y, 1 - c), device_id_type=MESH)


def _share_start(sums, after):
    names = tuple(sums)
    n = len(names)

    def body(*refs):
        _, _, c, _ = _position()
        for k, nm in enumerate(names):
            _share_copy(refs[k], nm, refs[n + 1], refs[n + 2], k, c).start()
        refs[-1][...] = jnp.zeros_like(refs[-1])

    outs = pl.pallas_call(
        body,
        name=f"share_start_{names[0]}",
        in_specs=[HBM] * n + [ANY],
        out_specs=[SEM, SEM] + [HBM] * n + [pl.BlockSpec(memory_space=pltpu.VMEM)],
        out_shape=[pltpu.SemaphoreType.DMA((n,)), pltpu.SemaphoreType.DMA((n,))]
        + [pltpu.HBM(SHARD_SHAPES[nm], F32) for nm in names] + [jax.ShapeDtypeStruct((8, LANE), F32)],
        input_output_aliases={k: 2 + k for k in range(n)},
        compiler_params=pltpu.CompilerParams(has_side_effects=EFFECT),
    )(*[_in_hbm(sums[nm]) for nm in names], after)
    return outs[0], outs[1], dict(zip(names, outs[2:2 + n])), outs[-1]


def _share_wait(send_sems, recv_sems, sums, after):
    names = tuple(sums)
    n = len(names)

    def body(*refs):
        _, _, c, _ = _position()
        for k, nm in enumerate(names):
            _share_copy(refs[k], nm, refs[n], refs[n + 1], k, 1 - c).wait_recv()
            _share_copy(refs[k], nm, refs[n], refs[n + 1], k, c).wait_send()

    outs = pl.pallas_call(
        body,
        name=f"share_wait_{names[0]}",
        in_specs=[HBM] * n + [SEM, SEM, ANY],
        out_specs=[HBM] * n,
        out_shape=[pltpu.HBM(SHARD_SHAPES[nm], F32) for nm in names],
        input_output_aliases={k: k for k in range(n)},
        compiler_params=pltpu.CompilerParams(has_side_effects=EFFECT),
    )(*[sums[nm] for nm in names], send_sems, recv_sems, after)
    return dict(zip(names, outs))


def _all_reduce_small(pack, name):
    rows = pack.shape[0]
    half = rows // 2

    def body(p_ref, o_ref, sib, land, sems):
        x, y, c, j = _position()
        sibling = (x, y, 1 - c)
        swap = pltpu.make_async_remote_copy(src_ref=p_ref, dst_ref=sib, send_sem=sems.at[0], recv_sem=sems.at[1],
                                            device_id=sibling, device_id_type=MESH)
        swap.start()
        swap.wait_recv()
        land[j] = p_ref[...] + sib[...]

        def mine(k, which):
            return land.at[k, pl.ds(which * half, half)]

        def ici(kk):
            return pltpu.make_async_remote_copy(
                src_ref=mine(j, c), dst_ref=mine(j, c), send_sem=sems.at[2 + kk], recv_sem=sems.at[6 + j],
                device_id=(kk // 2, kk % 2, c), device_id_type=MESH)

        def arrival(kk):
            return pltpu.make_async_remote_copy(
                src_ref=mine(kk, c), dst_ref=mine(kk, c), send_sem=sems.at[2 + kk], recv_sem=sems.at[6 + kk],
                device_id=(kk // 2, kk % 2, c), device_id_type=MESH)

        def passed_on(kk, which):
            return pltpu.make_async_remote_copy(
                src_ref=mine(kk, which), dst_ref=mine(kk, which), send_sem=sems.at[10 + kk],
                recv_sem=sems.at[14 + kk], device_id=sibling, device_id_type=MESH)

        for kk in range(N_CHIPS):
            @pl.when(j != kk)
            def _():
                ici(kk).start()
        for kk in range(N_CHIPS):
            @pl.when(j != kk)
            def _():
                arrival(kk).wait_recv()
                passed_on(kk, c).start()
        for kk in range(N_CHIPS):
            @pl.when(j != kk)
            def _():
                passed_on(kk, 1 - c).wait_recv()
        acc = land[0]
        for kk in range(1, N_CHIPS):
            acc = acc + land[kk]
        o_ref[...] = acc
        swap.wait_send()
        for kk in range(N_CHIPS):
            @pl.when(j != kk)
            def _():
                ici(kk).wait_send()
                passed_on(kk, c).wait_send()

    vmem = pl.BlockSpec(memory_space=pltpu.VMEM)
    return pl.pallas_call(
        body,
        name=name,
        in_specs=[vmem],
        out_specs=vmem,
        out_shape=jax.ShapeDtypeStruct((rows, LANE), F32),
        scratch_shapes=[pltpu.VMEM((rows, LANE), F32), pltpu.VMEM((N_CHIPS, rows, LANE), F32),
                        pltpu.SemaphoreType.DMA((18,))],
        compiler_params=pltpu.CompilerParams(has_side_effects=True, vmem_limit_bytes=VMEM_LIMIT),
    )(pack)


ADAM_ROWS = {"win": 224, "mk": 256, "br0": 512, "br1": 512, "br2": 512, "out": 256}


def _adamw(w, g, m, v, name, tr):
    rows, cols = w.shape
    tr = min(tr, rows)

    def body(w_ref, g_ref, m_ref, v_ref, go_ref, d_ref, nm_ref, nv_ref):
        gv = g_ref[...]
        go_ref[...] = gv
        nm = ADAM_B1 * m_ref[...] + (1.0 - ADAM_B1) * gv
        nv = ADAM_B2 * v_ref[...] + (1.0 - ADAM_B2) * (gv * gv)
        nm_ref[...] = nm
        nv_ref[...] = nv
        m_hat = nm / (1.0 - ADAM_B1 ** ADAM_STEP)
        v_hat = nv / (1.0 - ADAM_B2 ** ADAM_STEP)
        d_ref[...] = -ADAM_LR * (m_hat / (jnp.sqrt(v_hat) + ADAM_EPS) + ADAM_WD * w_ref[...])

    blk = pl.BlockSpec((tr, cols), lambda i: (i, 0))
    shape = jax.ShapeDtypeStruct((rows, cols), F32)
    return pl.pallas_call(
        body,
        name=f"adamw_{name}",
        grid=(rows // tr,),
        in_specs=[blk] * 4,
        out_specs=[blk] * 4,
        out_shape=[shape] * 4,
        compiler_params=_params(("parallel",)),
    )(w, g, m, v)


SMALL = (("pre_norm_g", (1, D)), ("post_norm_g", (1, D)), ("mem_norm_g", (1, D)), ("conv_w", (CONV_W, D_RNN)),
         ("conv_b", (1, D_RNN)), ("w_rg_a", (RNN_BLOCKS, LANE, LANE)), ("b_rg_a", (1, D_RNN)),
         ("w_rg_x", (RNN_BLOCKS, LANE, LANE)), ("b_rg_x", (1, D_RNN)), ("lru_lambda", (1, D_RNN)),
         ("swa_sinks", (1, SWA_HEADS)), ("rel_bias", (REL_BUCKETS, SWA_HEADS)))
PACK_ROWS = 2176


def _slot_len(shape):
    return -(-math.prod(shape) // LANE) * LANE


def _pack(values, last_row=None):
    parts = []
    for name, shape in SMALL:
        flat = values[name].reshape(-1).astype(F32)
        parts.append(jnp.pad(flat, (0, _slot_len(shape) - flat.shape[0])))
    flat = jnp.concatenate(parts)
    tail = jnp.zeros((LANE,), F32) if last_row is None else last_row
    return jnp.concatenate([jnp.pad(flat, (0, (PACK_ROWS - 1) * LANE - flat.shape[0])), tail]).reshape(PACK_ROWS, LANE)


def _unpack(pack):
    flat = pack.reshape(-1)
    out, off = {}, 0
    for name, shape in SMALL:
        out[name] = flat[off:off + math.prod(shape)].reshape(shape)
        off += _slot_len(shape)
    return out


TWIN_WEIGHTS = ("pre_norm_g", "post_norm_g", "mem_norm_g", "w_in", "conv_w", "conv_b", "w_rg_a", "b_rg_a", "w_rg_x",
                "b_rg_x", "lru_lambda", "swa_sinks", "rel_bias", "w_mem_kv", "w_br_rg", "w_br_swa", "w_br_mem", "w_out")
BIG = {"w_in": "win", "w_mem_kv": "mk", "w_br_rg": "br0", "w_br_swa": "br1", "w_br_mem": "br2", "w_out": "out"}


def kernel(x, mem, pre_norm_g, post_norm_g, mem_norm_g, w_in, conv_w, conv_b, w_rg_a, b_rg_a, w_rg_x, b_rg_x, lru_lambda, swa_sinks, rel_bias, w_mem_kv, w_br_rg, w_br_swa, w_br_mem, w_out, loss_target, m_pre_norm_g, m_post_norm_g, m_mem_norm_g, m_w_in, m_conv_w, m_conv_b, m_w_rg_a, m_b_rg_a, m_w_rg_x, m_b_rg_x, m_lru_lambda, m_swa_sinks, m_rel_bias, m_w_mem_kv, m_w_br_rg, m_w_br_swa, m_w_br_mem, m_w_out, v_pre_norm_g, v_post_norm_g, v_mem_norm_g, v_w_in, v_conv_w, v_conv_b, v_w_rg_a, v_b_rg_a, v_w_rg_x, v_b_rg_x, v_lru_lambda, v_swa_sinks, v_rel_bias, v_w_mem_kv, v_w_br_rg, v_w_br_swa, v_w_br_mem, v_w_out):
    args = dict(locals())
    out_shapes = {n: args[n].shape for n in TWIN_WEIGHTS}
    w = {n: (args[n] if n == "rel_bias" else args[n][0]) for n in TWIN_WEIGHTS}
    m = {n: (args["m_" + n] if n == "rel_bias" else args["m_" + n][0]) for n in TWIN_WEIGHTS}
    v = {n: (args["v_" + n] if n == "rel_bias" else args["v_" + n][0]) for n in TWIN_WEIGHTS}
    for d in (w, m, v):
        for n, shape in SMALL:
            if n != "conv_w":
                d[n] = d[n].reshape(shape)

    xi, yi, ci = lax.axis_index("x"), lax.axis_index("y"), lax.axis_index("c")
    chip = 2 * xi + yi
    c_arr = ci.astype(jnp.int32).reshape(1)
    zero = jnp.zeros((), jnp.int32)
    cw0 = (chip * (D_RNN // N_CHIPS)).astype(jnp.int32)

    placed = lax.dynamic_update_slice(jnp.zeros((CONV_W, D_RNN), F32), w["conv_w"], (zero, cw0))
    placed = jnp.where(ci == 0, placed, 0.0).reshape(CONV_W * D_RNN // LANE, LANE)
    conv_w_full = _all_reduce_small(placed, "gather_conv_w").reshape(CONV_W, D_RNN)

    for d in (w, m, v):
        d["w_in"] = d["w_in"].T
    chip_row = lambda tbl: lax.dynamic_slice(jnp.asarray(tbl), (chip.astype(jnp.int32), zero), (1, tbl.shape[1]))[0]
    chip_tables = lambda tbl: lax.dynamic_slice(jnp.asarray(tbl), (chip.astype(jnp.int32), zero, zero),
                                                (1,) + tbl.shape[1:])[0]
    odd_arr = yi.astype(jnp.int32).reshape(1)
    big_of = {s: n for n, s in BIG.items()}
    ag, token = {}, conv_w_full
    for stage in GATHER_STAGES:
        behind = c_arr if stage == GATHER_STAGES[0] else token
        placed = {n: (_place_group(w["w_in"], n, chip_tables(_own_block_table(n)), odd_arr, behind) if n in GROUPS
                      else _place_shard(w[big_of[n]], n, behind)) for n in stage}
        send, recv, in_flight, token = _gather_start(placed, token)
        ag[stage] = (send, recv, in_flight)
    h = token = _rms_fwd(x[0], w["pre_norm_g"], "rms_pre", token)

    all_started = token

    passing = {}

    def prefetch(names, after):
        send, recv, in_flight = ag[names]
        *passing[names], token = _gather_pass_start(_gather_wait(send, recv, in_flight, after), after)
        return token

    def fetch(names, after):
        if names in passing:
            ready = _gather_pass_wait(*passing.pop(names), after)
        else:
            send, recv, in_flight = ag[names]
            after = all_started if names == GATHER_STAGES[0] else after
            ready = _gather_swap(_gather_wait(send, recv, in_flight, after))
        return tuple(ready[n] for n in names)

    rs = {"slots": {}, "halves": {}, "pending": [], "swap": None}

    def emit(grads, after=None):
        assert rs["swap"] is None
        *rs["swap"], token = _swap_start(grads, after)
        return token

    def advance(after):
        grads, received = _swap_wait(*rs["swap"], after)
        rs["swap"] = None
        halves = {n: _add_half(grads[n], received[n], c_arr, n) for n in grads}
        landing = {s: rs["slots"][s] if s in rs["slots"] else lax.empty(_slot_shape(s), BF16)
                   for s in _stage_shards(tuple(grads))}
        send, recv, halves, landing, token = _scatter_start(halves, landing)
        rs["slots"].update(landing)
        rs["pending"].append((send, recv, halves, tuple(landing)))
        return token

    sp = {n: w[n] for n, _ in SMALL}
    sp["conv_w"] = conv_w_full
    sq, grad_x, d_small = _local_step(x[0], h, mem[0], loss_target[0], sp, fetch, prefetch, emit, advance)
    small_total = _all_reduce_small(_pack(d_small, sq[0]), "all_reduce_small")
    loss = small_total[PACK_ROWS - 1, 0] * (0.5 / D)

    for send, recv, halves, touched in rs["pending"]:
        halves, landed = _scatter_wait(send, recv, halves, {s: rs["slots"][s] for s in touched}, small_total)
        rs["slots"].update(landed)
        rs["halves"].update(halves)
    pos_arr = jnp.stack([ci, chip]).astype(jnp.int32)
    grp_tbl, blk_tbl = (chip_row(t) for t in _own_partial_tables())
    rest = {s: _sum_slots(rs["slots"][s], rs["halves"][s], pos_arr, s) for s in SHARDS if s != "win"}
    *rest_share, tok = _share_start(rest, small_total)
    win_sum = _sum_slots_win(rs["slots"]["win"], rs["halves"], pos_arr, grp_tbl, blk_tbl)
    *win_share, tok = _share_start({"win": win_sum}, tok)
    sums = _share_wait(*rest_share, tok)

    grad, delta, new_m, new_v = {}, {}, {}, {}
    for n, s in BIG.items():
        if n == "w_in":
            continue
        grad[n], delta[n], new_m[n], new_v[n] = _adamw(w[n], sums[s], m[n], v[n], s, ADAM_ROWS[s])
    g_win = _share_wait(*win_share, delta["w_out"])["win"]
    n = "w_in"
    grad[n], delta[n], new_m[n], new_v[n] = _adamw(w[n], g_win, m[n], v[n], "win", ADAM_ROWS["win"])
    for group in (grad, delta, new_m, new_v):
        group["w_in"] = group["w_in"].T
    def conv_w_in_place(d):
        return dict(d, conv_w=lax.dynamic_update_slice(jnp.zeros((CONV_W, D_RNN), F32), d["conv_w"], (zero, cw0)))

    _, d_, m_, v_ = _adamw(_pack(conv_w_in_place(w)), small_total, _pack(conv_w_in_place(m)),
                           _pack(conv_w_in_place(v)), "small", PACK_ROWS)
    for group, pack in ((grad, small_total), (delta, d_), (new_m, m_), (new_v, v_)):
        group.update(_unpack(pack))
    for group in (grad, delta, new_m, new_v):
        group["conv_w"] = lax.dynamic_slice(group["conv_w"], (zero, cw0), (CONV_W, D_RNN // N_CHIPS))

    outs = [loss, grad_x.reshape(1, S, D)]
    for group in (grad, delta, new_m, new_v):
        outs += [group[n].reshape(out_shapes[n]) for n in TWIN_WEIGHTS]
    return tuple(outs)
```

```python
import math
from typing import NamedTuple

import jax
import jax.numpy as jnp
from jax import lax
from jax.experimental import pallas as pl
from jax.experimental.pallas import tpu as pltpu

F32 = jnp.float32
BF16 = jnp.bfloat16
MESH = pl.DeviceIdType.MESH

S = 2048
D = 2048
MEM = 256
D_RNN = 1024
RNN_BLOCKS = 8
CONV_W = 4
LRU_C = 8.0
SWA_HEADS = 16
SWA_HD = 64
WINDOW = 128
MEM_HEADS = 4
MEM_HD = 256
REL_BUCKETS = 32
REL_MAX_DIST = 128
EPS = 1e-6
NEG_INF = -1e30
LANE = 128
SHARD = 3136
HALF_TILE = 64
N_CHIPS = 4
VMEM_LIMIT = 56 * 1024 * 1024

ADAM_LR = 0.001
ADAM_B1 = 0.9
ADAM_B2 = 0.999
ADAM_EPS = 1e-08
ADAM_WD = 0.01
ADAM_STEP = 10

GROUP_TILES = {"A": 16, "B": 18, "C": 16, "D": 48}
GROUPS = ("A", "B", "C", "D")


def _params(sem=None):
    return pltpu.CompilerParams(dimension_semantics=sem, vmem_limit_bytes=VMEM_LIMIT)


def _sigmoid(v):
    return jax.nn.sigmoid(v)


def _tile_home(t):
    if t < 16:
        return "A", t
    if t < 24:
        return "B", t - 16
    if t < 26:
        return "B", t - 24 + 16
    if t < 34:
        return "B", t - 26 + 8
    if t < 50:
        return "C", t - 34
    return "D", t - 50


def _shard_runs(j):
    runs = []
    per_shard = SHARD // HALF_TILE
    for q in range(per_shard * j, per_shard * (j + 1)):
        g, gt = _tile_home(q // 2)
        row = gt * LANE + (q % 2) * HALF_TILE
        if runs and runs[-1][2] == g and runs[-1][3] + runs[-1][1] == row:
            runs[-1][1] += HALF_TILE
        else:
            runs.append([(q - per_shard * j) * HALF_TILE, HALF_TILE, g, row])
    return [tuple(r) for r in runs]


_DIMS = {
    "nn": (((1,), (0,)), ((), ())),
    "nt": (((1,), (1,)), ((), ())),
    "tn": (((0,), (0,)), ((), ())),
}


def _mm(a, b, mode, out_dtype, tm, tn, tk, name, acc=None, after=None):
    if mode == "nn":
        (m, k), n = a.shape, b.shape[1]
    elif mode == "nt":
        (m, k), n = a.shape, b.shape[0]
    else:
        (k, m), n = a.shape, b.shape[1]
    tm, tn, tk = min(tm, m), min(tn, n), min(tk, k)
    assert m % tm == 0 and n % tn == 0 and k % tk == 0, (name, m, n, k)
    nk = k // tk
    has_acc = acc is not None

    def body(*refs):
        a_ref, b_ref = refs[0], refs[1]
        o_ref = refs[3] if has_acc else refs[2]
        p = lax.dot_general(a_ref[...], b_ref[...], _DIMS[mode], preferred_element_type=F32)

        def finish(v):
            if has_acc:
                v = v + refs[2][...]
            o_ref[...] = v.astype(out_dtype)

        if nk == 1:
            finish(p)
        else:
            s_ref = refs[-1]
            kk = pl.program_id(2)

            @pl.when(kk == 0)
            def _():
                s_ref[...] = p

            @pl.when(kk > 0)
            def _():
                s_ref[...] += p

            @pl.when(kk == nk - 1)
            def _():
                finish(s_ref[...])

    if mode == "nn":
        a_spec = pl.BlockSpec((tm, tk), lambda i, j, kk: (i, kk))
        b_spec = pl.BlockSpec((tk, tn), lambda i, j, kk: (kk, j))
    elif mode == "nt":
        a_spec = pl.BlockSpec((tm, tk), lambda i, j, kk: (i, kk))
        b_spec = pl.BlockSpec((tn, tk), lambda i, j, kk: (j, kk))
    else:
        a_spec = pl.BlockSpec((tk, tm), lambda i, j, kk: (kk, i))
        b_spec = pl.BlockSpec((tk, tn), lambda i, j, kk: (kk, j))
    o_spec = pl.BlockSpec((tm, tn), lambda i, j, kk: (i, j))
    in_specs = [a_spec, b_spec] + ([o_spec] if has_acc else [])
    args = (a, b) + ((acc,) if has_acc else ())
    if after is not None:
        in_specs.append(pl.BlockSpec(memory_space=pl.ANY))
        args += (after,)
    n_in = len(args)
    kernel_body = body

    def body(*refs):
        kernel_body(*(refs[:n_in - (after is not None)] + refs[n_in:]))

    return pl.pallas_call(
        body,
        name=name,
        grid=(m // tm, n // tn, nk),
        in_specs=in_specs,
        out_specs=o_spec,
        out_shape=jax.ShapeDtypeStruct((m, n), out_dtype),
        scratch_shapes=[pltpu.VMEM((tm, tn), F32)] if nk > 1 else [],
        compiler_params=_params(("parallel", "parallel", "arbitrary")),
    )(*args)


def _rms_fwd(x, g, name, after, ts=256):
    r, d = x.shape

    def body(x_ref, g_ref, _, o_ref):
        xv = x_ref[...]
        inv = lax.rsqrt(jnp.mean(xv * xv, axis=-1, keepdims=True) + EPS)
        o_ref[...] = (xv * inv * g_ref[...]).astype(BF16)

    return pl.pallas_call(
        body,
        name=name,
        grid=(r // ts,),
        in_specs=[pl.BlockSpec((ts, d), lambda i: (i, 0)), pl.BlockSpec((1, d), lambda i: (0, 0)),
                  pl.BlockSpec(memory_space=pl.ANY)],
        out_specs=pl.BlockSpec((ts, d), lambda i: (i, 0)),
        out_shape=jax.ShapeDtypeStruct((r, d), BF16),
        compiler_params=_params(("parallel",)),
    )(x, g, after)


def _post_loss(out, x, tgt, g_post, ts=256):
    n = S // ts

    def body(o_ref, x_ref, t_ref, g_ref, sq_ref, dy_ref, do_ref, dg_ref):
        i = pl.program_id(0)

        @pl.when(i == 0)
        def _():
            sq_ref[...] = jnp.zeros_like(sq_ref)
            dg_ref[...] = jnp.zeros_like(dg_ref)

        ov = o_ref[...]
        g = g_ref[...]
        inv = lax.rsqrt(jnp.mean(ov * ov, axis=-1, keepdims=True) + EPS)
        on = ov * inv
        err = x_ref[...] + on * g - t_ref[...]
        sq_ref[...] += jnp.sum(err * err)
        dy = err * (1.0 / D)
        dy_ref[...] = dy
        dg_ref[...] += jnp.sum(dy * on, axis=0, keepdims=True)
        don = dy * g
        do_ref[...] = (inv * (don - on * jnp.mean(don * on, axis=-1, keepdims=True))).astype(BF16)

    row = pl.BlockSpec((ts, D), lambda i: (i, 0))
    vec = pl.BlockSpec((1, D), lambda i: (0, 0))
    return pl.pallas_call(
        body,
        name="post_loss",
        grid=(n,),
        in_specs=[row, row, row, vec],
        out_specs=[pl.BlockSpec((8, LANE), lambda i: (0, 0)), row, row, vec],
        out_shape=[
            jax.ShapeDtypeStruct((8, LANE), F32),
            jax.ShapeDtypeStruct((S, D), F32),
            jax.ShapeDtypeStruct((S, D), BF16),
            jax.ShapeDtypeStruct((1, D), F32),
        ],
        compiler_params=_params(("arbitrary",)),
    )(out, x, tgt, g_post)


def _pre_bwd(dh, x, dy, g_pre, ts=256):
    n = S // ts

    def body(dh_ref, x_ref, dy_ref, g_ref, gx_ref, dg_ref):
        i = pl.program_id(0)

        @pl.when(i == 0)
        def _():
            dg_ref[...] = jnp.zeros_like(dg_ref)

        xv = x_ref[...]
        dhv = dh_ref[...]
        inv = lax.rsqrt(jnp.mean(xv * xv, axis=-1, keepdims=True) + EPS)
        xn = xv * inv
        dg_ref[...] += jnp.sum(dhv * xn, axis=0, keepdims=True)
        dxn = dhv * g_ref[...]
        gx_ref[...] = dy_ref[...] + inv * (dxn - xn * jnp.mean(dxn * xn, axis=-1, keepdims=True))

    row = pl.BlockSpec((ts, D), lambda i: (i, 0))
    vec = pl.BlockSpec((1, D), lambda i: (0, 0))
    return pl.pallas_call(
        body,
        name="pre_bwd",
        grid=(n,),
        in_specs=[row, row, row, vec],
        out_specs=[row, vec],
        out_shape=[jax.ShapeDtypeStruct((S, D), F32), jax.ShapeDtypeStruct((1, D), F32)],
        compiler_params=_params(("arbitrary",)),
    )(dh, x, dy, g_pre)


def _memnorm_bwd(dmemn, mem):
    def body(d_ref, m_ref, dg_ref):
        mv = m_ref[...]
        inv = lax.rsqrt(jnp.mean(mv * mv, axis=-1, keepdims=True) + EPS)
        dg_ref[...] = jnp.sum(d_ref[...] * mv * inv, axis=0, keepdims=True)

    return pl.pallas_call(
        body,
        name="memnorm_bwd",
        out_shape=jax.ShapeDtypeStruct((1, D), F32),
        compiler_params=_params(),
    )(dmemn, mem)


T_RNN = 256


def _neg_expm1(z):
    poly = -z * (1.0 + z * (0.5 + z * (1.0 / 6 + z * (1.0 / 24 + z * (1.0 / 120 + z * (1.0 / 720))))))
    return jnp.where(z > -0.1, poly, 1.0 - jnp.exp(z))


def _softplus_neg(lam):
    return jnp.maximum(-lam, 0.0) + jnp.log1p(jnp.exp(-jnp.abs(lam)))


def _rnn_gates(conv, wa_ref, ba, wx_ref, bx, lam, first_row):
    cbf = conv.astype(BF16)
    ga, gx = [], []
    for n in range(RNN_BLOCKS):
        c_n = cbf[:, n * LANE:(n + 1) * LANE]
        ga.append(jnp.dot(c_n, wa_ref[n], preferred_element_type=F32))
        gx.append(jnp.dot(c_n, wx_ref[n], preferred_element_type=F32))
    gate_r = _sigmoid(jnp.concatenate(ga, axis=1) + ba)
    gate_i = _sigmoid(jnp.concatenate(gx, axis=1) + bx)
    sp = _softplus_neg(lam)
    log_a = -LRU_C * gate_r * sp
    a = jnp.exp(log_a)
    mult_raw = jnp.sqrt(_neg_expm1(2.0 * log_a))
    mult = jnp.where(first_row, 1.0, mult_raw)
    return cbf, gate_r, gate_i, sp, a, mult_raw, mult


def _rglru_fwd(p_a, conv_w, conv_b, wa, ba, wx, bx, lam):
    t = T_RNN
    n = S // t

    def body(xr_ref, g_ref, cw_ref, cb_ref, wa_ref, ba_ref, wx_ref, bx_ref, lam_ref,
             y_ref, h_ref, xp_s, hcar, a_s, b_s):
        i = pl.program_id(0)

        @pl.when(i == 0)
        def _():
            xp_s[0:8, :] = jnp.zeros((8, D_RNN), F32)
            hcar[...] = jnp.zeros_like(hcar)

        @pl.when(i > 0)
        def _():
            xp_s[0:8, :] = xp_s[t:t + 8, :]

        xp_s[8:8 + t, :] = xr_ref[...]
        conv = cb_ref[...]
        for k in range(CONV_W):
            conv = conv + cw_ref[k:k + 1, :] * xp_s[8 - k:8 - k + t, :]
        rows = i * t + lax.broadcasted_iota(jnp.int32, (t, 1), 0)
        _, _, gate_i, _, a, _, mult = _rnn_gates(
            conv, wa_ref, ba_ref[...], wx_ref, bx_ref[...], lam_ref[...], rows == 0)
        a_s[...] = a
        b_s[...] = mult * gate_i * conv

        def step(tt, h):
            h = a_s[pl.ds(tt, 1), :] * h + b_s[pl.ds(tt, 1), :]
            h_ref[pl.ds(tt, 1), :] = h
            return h

        hcar[...] = lax.fori_loop(0, t, step, hcar[...], unroll=8)
        g = g_ref[...]
        y_ref[...] = (h_ref[...] * (g * _sigmoid(g))).astype(BF16)

    blk = lambda c: pl.BlockSpec((t, D_RNN), lambda i: (i, c))
    full = lambda shape: pl.BlockSpec(shape, lambda i: (0,) * len(shape))
    return pl.pallas_call(
        body,
        name="rglru_fwd",
        grid=(n,),
        in_specs=[blk(0), blk(1), full((CONV_W, D_RNN)), full((1, D_RNN)),
                  full((RNN_BLOCKS, LANE, LANE)), full((1, D_RNN)),
                  full((RNN_BLOCKS, LANE, LANE)), full((1, D_RNN)), full((1, D_RNN))],
        out_specs=[blk(0), blk(0)],
        out_shape=[jax.ShapeDtypeStruct((S, D_RNN), BF16), jax.ShapeDtypeStruct((S, D_RNN), F32)],
        scratch_shapes=[pltpu.VMEM((t + 8, D_RNN), F32), pltpu.VMEM((1, D_RNN), F32),
                        pltpu.VMEM((t, D_RNN), F32), pltpu.VMEM((t, D_RNN), F32)],
        compiler_params=_params(("arbitrary",)),
    )(p_a, p_a, conv_w, conv_b, wa, ba, wx, bx, lam)


def _rglru_bwd(dy, p_a, hseq, conv_w, conv_b, wa, ba, wx, bx, lam):
    t = T_RNN
    n = S // t
    rb = t // 8

    def body(dy_ref, xr_ref, g_ref, h_ref, xrp_ref, hp_ref, cw_ref, cb_ref, wa_ref, ba_ref, wx_ref, bx_ref, lam_ref,
             dp_ref, dcw_ref, dcb_ref, dwa_ref, dba_ref, dwx_ref, dbx_ref, dlam_ref,
             xp_s, hp_s, dxp_s, lamcar, a_s, dh_s, lam_s):
        i = pl.program_id(0)
        r = n - 1 - i

        @pl.when(i == 0)
        def _():
            for ref in (dcw_ref, dcb_ref, dwa_ref, dba_ref, dwx_ref, dbx_ref, dlam_ref, lamcar):
                ref[...] = jnp.zeros_like(ref)
            dxp_s[t:t + 8, :] = jnp.zeros((8, D_RNN), F32)

        @pl.when(i > 0)
        def _():
            dxp_s[t:t + 8, :] = dxp_s[0:8, :]

        has_prev = r > 0
        xp_s[0:8, :] = jnp.where(has_prev, xrp_ref[...], 0.0)
        xp_s[8:8 + t, :] = xr_ref[...]
        hp_s[0:8, :] = jnp.where(has_prev, hp_ref[...], 0.0)
        hp_s[8:8 + t, :] = h_ref[...]
        xs = [xp_s[8 - k:8 - k + t, :] for k in range(CONV_W)]
        conv = cb_ref[...]
        for k in range(CONV_W):
            conv = conv + cw_ref[k:k + 1, :] * xs[k]
        rows = r * t + lax.broadcasted_iota(jnp.int32, (t, 1), 0)
        first = rows == 0
        lam_p = lam_ref[...]
        cbf, gate_r, gate_i, sp, a, mult_raw, mult = _rnn_gates(
            conv, wa_ref, ba_ref[...], wx_ref, bx_ref[...], lam_p, first)

        g = g_ref[...]
        sg = _sigmoid(g)
        dyv = dy_ref[...]
        a_s[...] = a
        dh_s[...] = dyv * (g * sg)
        dg = dyv * h_ref[...] * (sg * (1.0 + g * (1.0 - sg)))

        def step(jj, car):
            tt = t - 1 - jj
            lm = dh_s[pl.ds(tt, 1), :] + car
            lam_s[pl.ds(tt, 1), :] = lm
            return a_s[pl.ds(tt, 1), :] * lm

        lamcar[...] = lax.fori_loop(0, t, step, lamcar[...], unroll=8)
        db = lam_s[...]
        da = db * hp_s[7:7 + t, :]
        dmult = db * gate_i * conv
        dgate_i = db * mult * conv
        dconv = db * mult * gate_i
        dlog_a = da * a + jnp.where(first, 0.0, dmult * (-(a * a) / mult_raw))
        dgate_r = dlog_a * (-LRU_C * sp)
        dsp = jnp.sum(dlog_a * (-LRU_C * gate_r), axis=0, keepdims=True)
        dlam_ref[...] += dsp * (-_sigmoid(-lam_p))
        dga = dgate_r * gate_r * (1.0 - gate_r)
        dgx = dgate_i * gate_i * (1.0 - gate_i)
        dba_ref[...] += jnp.sum(dga, axis=0, keepdims=True)
        dbx_ref[...] += jnp.sum(dgx, axis=0, keepdims=True)
        dga16, dgx16 = dga.astype(BF16), dgx.astype(BF16)
        back = []
        for nb in range(RNN_BLOCKS):
            sl = slice(nb * LANE, (nb + 1) * LANE)
            dwa_ref[nb] += lax.dot_general(cbf[:, sl], dga16[:, sl], _DIMS["tn"], preferred_element_type=F32)
            dwx_ref[nb] += lax.dot_general(cbf[:, sl], dgx16[:, sl], _DIMS["tn"], preferred_element_type=F32)
            back.append(lax.dot_general(dga16[:, sl], wa_ref[nb], _DIMS["nt"], preferred_element_type=F32)
                        + lax.dot_general(dgx16[:, sl], wx_ref[nb], _DIMS["nt"], preferred_element_type=F32))
        dconv = dconv + jnp.concatenate(back, axis=1)
        dcb_ref[...] += jnp.sum(dconv, axis=0, keepdims=True)
        for k in range(CONV_W):
            dcw_ref[k:k + 1, :] += jnp.sum(dconv * xs[k], axis=0, keepdims=True)
        dxp_s[0:t, :] = dconv
        dxr = cw_ref[0:1, :] * dconv
        for k in range(1, CONV_W):
            dxr = dxr + cw_ref[k:k + 1, :] * dxp_s[k:k + t, :]
        dp_ref[:, 0:D_RNN] = dxr.astype(BF16)
        dp_ref[:, D_RNN:2 * D_RNN] = dg.astype(BF16)

    blk = lambda c: pl.BlockSpec((t, D_RNN), lambda i: (n - 1 - i, c))
    prev8 = pl.BlockSpec((8, D_RNN), lambda i: (jnp.maximum((n - 1 - i) * rb - 1, 0), 0))
    full = lambda shape: pl.BlockSpec(shape, lambda i: (0,) * len(shape))
    vec = full((1, D_RNN))
    mat = full((RNN_BLOCKS, LANE, LANE))
    return pl.pallas_call(
        body,
        name="rglru_bwd",
        grid=(n,),
        in_specs=[blk(0), blk(0), blk(1), blk(0), prev8, prev8,
                  full((CONV_W, D_RNN)), vec, mat, vec, mat, vec, vec],
        out_specs=[pl.BlockSpec((t, 2 * D_RNN), lambda i: (n - 1 - i, 0)),
                   full((CONV_W, D_RNN)), vec, mat, vec, mat, vec, vec],
        out_shape=[jax.ShapeDtypeStruct((S, 2 * D_RNN), BF16),
                   jax.ShapeDtypeStruct((CONV_W, D_RNN), F32), jax.ShapeDtypeStruct((1, D_RNN), F32),
                   jax.ShapeDtypeStruct((RNN_BLOCKS, LANE, LANE), F32), jax.ShapeDtypeStruct((1, D_RNN), F32),
                   jax.ShapeDtypeStruct((RNN_BLOCKS, LANE, LANE), F32), jax.ShapeDtypeStruct((1, D_RNN), F32),
                   jax.ShapeDtypeStruct((1, D_RNN), F32)],
        scratch_shapes=[pltpu.VMEM((t + 8, D_RNN), F32), pltpu.VMEM((t + 8, D_RNN), F32),
                        pltpu.VMEM((t + 8, D_RNN), F32), pltpu.VMEM((1, D_RNN), F32),
                        pltpu.VMEM((t, D_RNN), F32), pltpu.VMEM((t, D_RNN), F32), pltpu.VMEM((t, D_RNN), F32)],
        compiler_params=_params(("arbitrary",)),
    )(dy, p_a, p_a, hseq, p_a, hseq, conv_w, conv_b, wa, ba, wx, bx, lam)


QB = WINDOW
KB2 = 2 * WINDOW
N_QB = S // QB
N_PAIR = SWA_HEADS // 2


def _swa_keys(kvc_ref, kvp_ref):
    kk = jnp.concatenate([kvp_ref[:, 0:LANE], kvc_ref[:, 0:LANE]], axis=0)
    vv = jnp.concatenate([kvp_ref[:, LANE:2 * LANE], kvc_ref[:, LANE:2 * LANE]], axis=0)
    lo = lax.broadcasted_iota(jnp.int32, (1, LANE), 1) < SWA_HD
    kk_sw, vv_sw = pltpu.roll(kk, SWA_HD, 1), pltpu.roll(vv, SWA_HD, 1)
    kd = [jnp.where(lo, kk, kk_sw).astype(BF16), jnp.where(lo, kk_sw, kk).astype(BF16)]
    vd = [jnp.where(lo, vv, vv_sw).astype(BF16), jnp.where(lo, vv_sw, vv).astype(BF16)]
    return lo, kd, vd


GRP = SWA_HEADS // 2
STACK = GRP
GQ = STACK * QB


def _swa_valid(n, rows):
    qi = lax.broadcasted_iota(jnp.int32, (rows, KB2), 0) % QB
    kj = lax.broadcasted_iota(jnp.int32, (rows, KB2), 1)
    dist = qi + WINDOW - kj
    return (dist >= 0) & (dist < WINDOW) & ((n > 0) | (kj >= WINDOW))


def _swa_stack(tile_of, lo, h0, masked):
    parts = []
    for h in range(h0, h0 + STACK):
        t = tile_of(h // 2)
        if masked:
            t = jnp.where(lo if h % 2 == 0 else jnp.logical_not(lo), t, 0.0)
        parts.append(t)
    return jnp.concatenate(parts, axis=0)


def _swa_unstack(stacked, lo, pair):
    return jnp.where(lo, stacked[2 * pair * QB:(2 * pair + 1) * QB], stacked[(2 * pair + 1) * QB:(2 * pair + 2) * QB])


def _swa_softmax(lg, sink, valid):
    lg = jnp.where(valid, lg, NEG_INF)
    m = jnp.maximum(jnp.max(lg, axis=-1, keepdims=True), sink)
    p = jnp.exp(lg - m)
    es = jnp.exp(sink - m)
    den = jnp.sum(p, axis=-1, keepdims=True) + es
    return p / den, es / den


def _swa_probs_head(qh16, kd, bias, sink, valid):
    lg = lax.dot_general(qh16, kd, _DIMS["nt"], preferred_element_type=F32) * (SWA_HD ** -0.5) + bias
    return _swa_softmax(lg, sink, valid)[0]


def _swa_probs(q16, kd, bias_ref, sink_ref, h0, valid):
    bias = bias_ref[h0:h0 + STACK].reshape(GQ, KB2)
    sink = jnp.concatenate([jnp.full((QB, 1), sink_ref[h], F32) for h in range(h0, h0 + STACK)], axis=0)
    lg = lax.dot_general(q16, kd, _DIMS["nt"], preferred_element_type=F32) * (SWA_HD ** -0.5) + bias
    return _swa_softmax(lg, sink, valid)


def _swa_specs():
    q = pl.BlockSpec((QB, D_RNN), lambda n: (n, 0))
    g = pl.BlockSpec((QB, D_RNN), lambda n: (n, 1))
    kvc = pl.BlockSpec((QB, 2 * LANE), lambda n: (n, 8))
    kvp = pl.BlockSpec((QB, 2 * LANE), lambda n: (jnp.maximum(n - 1, 0), 8))
    bias = pl.BlockSpec((SWA_HEADS, QB, KB2), lambda n: (0, 0, 0))
    sinks = pl.BlockSpec(memory_space=pltpu.SMEM)
    return q, g, kvc, kvp, bias, sinks


def _swa_fwd(p_b, bias_t, sinks):
    def body(q_ref, g_ref, kvc_ref, kvp_ref, bias_ref, sink_ref, y_ref, o_ref):
        n = pl.program_id(0)
        lo, kd, vd = _swa_keys(kvc_ref, kvp_ref)
        valid = _swa_valid(n, QB)
        for hp in range(N_PAIR):
            sl = slice(hp * LANE, (hp + 1) * LANE)
            kvh = hp // (N_PAIR // 2)
            q = q_ref[:, sl]
            outs = []
            for j in range(2):
                qh16 = jnp.where(lo if j == 0 else jnp.logical_not(lo), q, 0.0).astype(BF16)
                probs = _swa_probs_head(qh16, kd[kvh], bias_ref[2 * hp + j], sink_ref[2 * hp + j], valid)
                outs.append(jnp.dot(probs.astype(BF16), vd[kvh], preferred_element_type=F32))
            o = jnp.where(lo, outs[0], outs[1])
            o_ref[:, sl] = o
            g = g_ref[:, sl]
            y_ref[:, sl] = (o * (g * _sigmoid(g))).astype(BF16)

    q, g, kvc, kvp, bias, sinks_spec = _swa_specs()
    out = pl.BlockSpec((QB, D_RNN), lambda n: (n, 0))
    return pl.pallas_call(
        body,
        name="swa_fwd",
        grid=(N_QB,),
        in_specs=[q, g, kvc, kvp, bias, sinks_spec],
        out_specs=[out, out],
        out_shape=[jax.ShapeDtypeStruct((S, D_RNN), BF16), jax.ShapeDtypeStruct((S, D_RNN), F32)],
        compiler_params=_params(("parallel",)),
    )(p_b, p_b, p_b, p_b, bias_t, sinks)


def _swa_bwd(dy, p_b, o_swa, bias_t, sinks, after=None):
    def body(dy_ref, q_ref, g_ref, kvc_ref, kvp_ref, o_ref, bias_ref, sink_ref, *rest):
        dp_ref, dk_ref, dv_ref, dbias_ref, dsink_ref, do_s = rest[-6:]
        n = pl.program_id(0)

        @pl.when(n == 0)
        def _():
            for ref in (dk_ref, dv_ref, dbias_ref, dsink_ref):
                ref[...] = jnp.zeros_like(ref)

        lo, kd, vd = _swa_keys(kvc_ref, kvp_ref)
        hi = jnp.logical_not(lo)
        valid = _swa_valid(n, GQ)
        tile = lambda ref: (lambda hp: ref[:, hp * LANE:(hp + 1) * LANE])
        for hp in range(N_PAIR):
            sl = slice(hp * LANE, (hp + 1) * LANE)
            g, dyv = g_ref[:, sl], dy_ref[:, sl]
            sg = _sigmoid(g)
            do_s[:, sl] = dyv * (g * sg)
            dp_ref[:, D_RNN + hp * LANE:D_RNN + (hp + 1) * LANE] = (
                dyv * o_ref[:, sl] * (sg * (1.0 + g * (1.0 - sg)))).astype(BF16)

        dk_blk = jnp.zeros((KB2, LANE), F32)
        dv_blk = jnp.zeros((KB2, LANE), F32)
        for h0 in range(0, SWA_HEADS, STACK):
            kvh = h0 // GRP
            q16 = _swa_stack(tile(q_ref), lo, h0, masked=True).astype(BF16)
            do8 = _swa_stack(tile(do_s), lo, h0, masked=True)
            do16 = do8.astype(BF16)
            delta = jnp.sum(do8 * _swa_stack(tile(o_ref), lo, h0, masked=False), axis=-1, keepdims=True)
            probs, psink = _swa_probs(q16, kd[kvh], bias_ref, sink_ref, h0, valid)
            dpr = lax.dot_general(do16, vd[kvh], _DIMS["nt"], preferred_element_type=F32)
            ds = probs * (dpr - delta)
            sink_term = psink * delta
            for g in range(STACK):
                h, rows = h0 + g, slice(g * QB, (g + 1) * QB)
                dbias_ref[h] += ds[rows]
                dsink_ref[h:h + 1, :] += jnp.zeros((1, LANE), F32) - jnp.sum(sink_term[rows])
            ds16 = (ds * (SWA_HD ** -0.5)).astype(BF16)
            dq_all = jnp.dot(ds16, kd[kvh], preferred_element_type=F32)
            for pair in range(STACK // 2):
                sl = slice((h0 // 2 + pair) * LANE, (h0 // 2 + pair + 1) * LANE)
                dp_ref[:, sl] = _swa_unstack(dq_all, lo, pair).astype(BF16)
            dk_pair = lax.dot_general(ds16, q16, _DIMS["tn"], preferred_element_type=F32)
            dv_pair = lax.dot_general(probs.astype(BF16), do16, _DIMS["tn"], preferred_element_type=F32)
            keep = lo if kvh == 0 else hi
            dk_blk = dk_blk + jnp.where(keep, dk_pair + pltpu.roll(dk_pair, SWA_HD, 1), 0.0)
            dv_blk = dv_blk + jnp.where(keep, dv_pair + pltpu.roll(dv_pair, SWA_HD, 1), 0.0)

        cur = pl.ds(pl.multiple_of(n * QB, QB), QB)
        dk_ref[cur, :] += dk_blk[QB:KB2]
        dv_ref[cur, :] += dv_blk[QB:KB2]

        @pl.when(n > 0)
        def _():
            prev = pl.ds(pl.multiple_of((n - 1) * QB, QB), QB)
            dk_ref[prev, :] += dk_blk[0:QB]
            dv_ref[prev, :] += dv_blk[0:QB]

    q, g, kvc, kvp, bias, sinks_spec = _swa_specs()
    row = pl.BlockSpec((QB, D_RNN), lambda n: (n, 0))
    acc = pl.BlockSpec((S, LANE), lambda n: (0, 0))
    return pl.pallas_call(
        body,
        name="swa_bwd",
        grid=(N_QB,),
        in_specs=[row, q, g, kvc, kvp, row, bias, sinks_spec] + ([ANY] if after is not None else []),
        out_specs=[pl.BlockSpec((QB, 2 * D_RNN), lambda n: (n, 0)), acc, acc, bias,
                   pl.BlockSpec((SWA_HEADS, LANE), lambda n: (0, 0))],
        out_shape=[jax.ShapeDtypeStruct((S, GROUP_TILES["B"] * LANE), BF16),
                   jax.ShapeDtypeStruct((S, LANE), F32), jax.ShapeDtypeStruct((S, LANE), F32),
                   jax.ShapeDtypeStruct((SWA_HEADS, QB, KB2), F32),
                   jax.ShapeDtypeStruct((SWA_HEADS, LANE), F32)],
        scratch_shapes=[pltpu.VMEM((QB, D_RNN), F32)],
        compiler_params=_params(("arbitrary",)),
    )(dy, p_b, p_b, p_b, p_b, o_swa, bias_t, sinks, *([after] if after is not None else []))


def _swa_pack(dp_b, dk, dv, ts=512):
    def body(_, dk_ref, dv_ref, o_ref):
        o_ref[:, 0:LANE] = dk_ref[...].astype(BF16)
        o_ref[:, LANE:2 * LANE] = dv_ref[...].astype(BF16)

    tile = pl.BlockSpec((ts, LANE), lambda i: (i, 0))
    return pl.pallas_call(
        body,
        name="swa_pack",
        grid=(S // ts,),
        in_specs=[pl.BlockSpec(memory_space=pl.ANY), tile, tile],
        out_specs=pl.BlockSpec((ts, 2 * LANE), lambda i: (i, 8)),
        out_shape=jax.ShapeDtypeStruct(dp_b.shape, dp_b.dtype),
        input_output_aliases={0: 0},
        compiler_params=_params(("parallel",)),
    )(dp_b, dk, dv)


def _split3(v):
    a = v.astype(BF16)
    r = v - a.astype(F32)
    b = r.astype(BF16)
    c = (r - b.astype(F32)).astype(BF16)
    return a, b, c


def _relbias_grad(dbias_flat, onehot_t):
    def body(d_ref, e_ref, o_ref):
        e = e_ref[...]
        acc = jnp.zeros((SWA_HEADS, REL_BUCKETS), F32)
        for term in _split3(d_ref[...]):
            acc = acc + lax.dot_general(term, e, _DIMS["nt"], preferred_element_type=F32)
        o_ref[...] = acc

    return pl.pallas_call(
        body,
        name="relbias_grad",
        out_shape=jax.ShapeDtypeStruct((SWA_HEADS, REL_BUCKETS), F32),
        compiler_params=_params(),
    )(dbias_flat, onehot_t)


TS_MEM = 512


def _mem_probs(q16, mk):
    lg = lax.dot_general(q16, mk, _DIMS["nt"], preferred_element_type=F32) * (MEM_HD ** -0.5)
    p = jnp.exp(lg - jnp.max(lg, axis=-1, keepdims=True))
    return p / jnp.sum(p, axis=-1, keepdims=True)


def _mem_fwd(p_c, mkv):
    def body(q_ref, g_ref, mkv_ref, y_ref, o_ref):
        for hm in range(MEM_HEADS):
            sl = slice(hm * MEM_HD, (hm + 1) * MEM_HD)
            probs = _mem_probs(q_ref[:, sl].astype(BF16), mkv_ref[:, sl])
            o = jnp.dot(probs.astype(BF16), mkv_ref[:, D_RNN + hm * MEM_HD:D_RNN + (hm + 1) * MEM_HD],
                        preferred_element_type=F32)
            o_ref[:, sl] = o
            g = g_ref[:, sl]
            y_ref[:, sl] = (o * (g * _sigmoid(g))).astype(BF16)

    blk = lambda c: pl.BlockSpec((TS_MEM, D_RNN), lambda i: (i, c))
    return pl.pallas_call(
        body,
        name="mem_fwd",
        grid=(S // TS_MEM,),
        in_specs=[blk(0), blk(1), pl.BlockSpec((MEM, 2 * D_RNN), lambda i: (0, 0))],
        out_specs=[blk(0), blk(0)],
        out_shape=[jax.ShapeDtypeStruct((S, D_RNN), BF16), jax.ShapeDtypeStruct((S, D_RNN), F32)],
        compiler_params=_params(("parallel",)),
    )(p_c, p_c, mkv)


def _mem_bwd(dy, p_c, o_mem, mkv):
    def body(dy_ref, q_ref, g_ref, o_ref, mkv_ref, dp_ref, dmkv_ref):
        @pl.when(pl.program_id(0) == 0)
        def _():
            dmkv_ref[...] = jnp.zeros_like(dmkv_ref)

        for hm in range(MEM_HEADS):
            sl = slice(hm * MEM_HD, (hm + 1) * MEM_HD)
            sv = slice(D_RNN + hm * MEM_HD, D_RNN + (hm + 1) * MEM_HD)
            q16 = q_ref[:, sl].astype(BF16)
            mk, mv = mkv_ref[:, sl], mkv_ref[:, sv]
            probs = _mem_probs(q16, mk)
            g, o, dyv = g_ref[:, sl], o_ref[:, sl], dy_ref[:, sl]
            sg = _sigmoid(g)
            do = dyv * (g * sg)
            dp_ref[:, sv] = (dyv * o * (sg * (1.0 + g * (1.0 - sg)))).astype(BF16)
            do16 = do.astype(BF16)
            delta = jnp.sum(do * o, axis=-1, keepdims=True)
            dpr = lax.dot_general(do16, mv, _DIMS["nt"], preferred_element_type=F32)
            ds16 = (probs * (dpr - delta) * (MEM_HD ** -0.5)).astype(BF16)
            dp_ref[:, sl] = jnp.dot(ds16, mk, preferred_element_type=F32).astype(BF16)
            dmkv_ref[:, sl] += lax.dot_general(ds16, q16, _DIMS["tn"], preferred_element_type=F32)
            dmkv_ref[:, sv] += lax.dot_general(probs.astype(BF16), do16, _DIMS["tn"], preferred_element_type=F32)

    blk = lambda c: pl.BlockSpec((TS_MEM, D_RNN), lambda i: (i, c))
    kv = pl.BlockSpec((MEM, 2 * D_RNN), lambda i: (0, 0))
    return pl.pallas_call(
        body,
        name="mem_bwd",
        grid=(S // TS_MEM,),
        in_specs=[blk(0), blk(0), blk(1), blk(0), kv],
        out_specs=[pl.BlockSpec((TS_MEM, 2 * D_RNN), lambda i: (i, 0)), kv],
        out_shape=[jax.ShapeDtypeStruct((S, 2 * D_RNN), BF16), jax.ShapeDtypeStruct((MEM, 2 * D_RNN), F32)],
        compiler_params=_params(("arbitrary",)),
    )(dy, p_c, p_c, o_mem, mkv)


TS_MRG = 512
TD_MRG = 512
N_DBLK = D // TD_MRG


def _merge_fwd(z, p_d):
    def body(z0, z1, z2, g0, g1, g2, o_ref):
        o_ref[...] = (_sigmoid(g0[...]) * z0[...] + _sigmoid(g1[...]) * z1[...]
                      + _sigmoid(g2[...]) * z2[...]).astype(BF16)

    blk = pl.BlockSpec((TS_MRG, TD_MRG), lambda i, d: (i, d))
    gate = lambda b: pl.BlockSpec((TS_MRG, TD_MRG), lambda i, d: (i, b * N_DBLK + d))
    return pl.pallas_call(
        body,
        name="merge_fwd",
        grid=(S // TS_MRG, N_DBLK),
        in_specs=[blk, blk, blk, gate(0), gate(1), gate(2)],
        out_specs=blk,
        out_shape=jax.ShapeDtypeStruct((S, D), BF16),
        compiler_params=_params(("parallel", "parallel")),
    )(z[0], z[1], z[2], p_d, p_d, p_d)


def _merge_bwd(dmerged, z_b, p_d, b, dp_d, after=None):
    def body(dm_ref, z_ref, g_ref, *refs):
        dz_ref, dg_ref = refs[-2], refs[-1]
        sg = _sigmoid(g_ref[...])
        dm = dm_ref[...]
        dz_ref[...] = (dm * sg).astype(BF16)
        dg_ref[...] = (dm * z_ref[...] * sg * (1.0 - sg)).astype(BF16)

    blk = pl.BlockSpec((TS_MRG, TD_MRG), lambda i, d: (i, d))
    gate = pl.BlockSpec((TS_MRG, TD_MRG), lambda i, d: (i, b * N_DBLK + d))
    in_specs = [blk, blk, gate]
    args = [dmerged, z_b, p_d]
    aliases = {}
    if dp_d is not None:
        in_specs.append(pl.BlockSpec(memory_space=pl.ANY))
        args.append(dp_d)
        aliases = {3: 1}
    if after is not None:
        in_specs.append(pl.BlockSpec(memory_space=pl.ANY))
        args.append(after)
    return pl.pallas_call(
        body,
        name=f"merge_bwd{b}",
        grid=(S // TS_MRG, N_DBLK),
        in_specs=in_specs,
        out_specs=[blk, gate],
        out_shape=[jax.ShapeDtypeStruct((S, D), BF16),
                   jax.ShapeDtypeStruct((S, GROUP_TILES["D"] * LANE), BF16)],
        input_output_aliases=aliases,
        compiler_params=_params(("parallel", "parallel")),
    )(*args)


def _bucket_table():
    import numpy as np
    qi = np.arange(QB)[:, None]
    kj = np.arange(KB2)[None, :]
    n = np.maximum(qi + WINDOW - kj, 0)
    max_exact = REL_BUCKETS // 2
    ratio = np.log(np.maximum(n, 1).astype(np.float32) / max_exact) / np.float32(math.log(REL_MAX_DIST / max_exact))
    large = np.minimum(max_exact + (ratio * (REL_BUCKETS - max_exact)).astype(np.int32), REL_BUCKETS - 1)
    bucket = np.where(n < max_exact, n, large).reshape(1, QB * KB2)
    return (bucket == np.arange(REL_BUCKETS)[:, None]).astype(np.float32)


def _bias_expand(rel_bias_t, onehot_t):
    def body(r_ref, e_ref, o_ref):
        e = e_ref[...]
        acc = jnp.zeros((SWA_HEADS, QB * KB2), F32)
        for term in _split3(r_ref[...]):
            acc = acc + jnp.dot(term, e, preferred_element_type=F32)
        o_ref[...] = acc

    return pl.pallas_call(
        body,
        name="bias_expand",
        out_shape=jax.ShapeDtypeStruct((SWA_HEADS, QB * KB2), F32),
        compiler_params=_params(),
    )(rel_bias_t, onehot_t)


PROJ_TN = {"A": 1024, "B": 1152, "C": 1024, "D": 1536}


def _local_step(x, h, mem, tgt, sp, fetch, prefetch, emit, advance):
    onehot_t = jnp.asarray(_bucket_table(), BF16)
    bias_t = _bias_expand(sp["rel_bias"].T, onehot_t).reshape(SWA_HEADS, QB, KB2)
    sinks = sp["swa_sinks"].reshape(SWA_HEADS)
    wa16, wx16 = sp["w_rg_a"].astype(BF16), sp["w_rg_x"].astype(BF16)
    rnn = (sp["conv_w"], sp["conv_b"], wa16, sp["b_rg_a"], wx16, sp["b_rg_x"], sp["lru_lambda"])

    memn = _rms_fwd(mem, sp["mem_norm_g"], "rms_mem", h)
    w_grp, p = {}, {}

    def project(g, after, then=None):
        (w_grp[g],) = fetch((g,), after)
        tok = prefetch(then, w_grp[g]) if then is not None else None
        p[g] = _mm(h, w_grp[g], "nt", F32, 1024, PROJ_TN[g], D, f"proj_{g}", after=tok)

    project("A", h)
    y_rg, hseq = _rglru_fwd(p["A"], *rnn)
    project("B", y_rg)
    y_swa, o_swa = _swa_fwd(p["B"], bias_t, sinks)
    project("C", y_swa, then=("mk",))
    (wmk,) = fetch(("mk",), p["C"])
    tok = prefetch(("br0", "br1", "br2"), wmk)
    mkv = _mm(memn, wmk, "nn", BF16, MEM, 1024, D, "mkv", after=tok)
    y_mem, o_mem = _mem_fwd(p["C"], mkv)
    ys = (y_rg, y_swa, y_mem)
    wbr = fetch(("br0", "br1", "br2"), y_mem)
    tok = prefetch(("D",), wbr[2])
    z = []
    for b in range(3):
        z.append(_mm(ys[b], wbr[b], "nn", F32, 1024, 1024, D_RNN, f"branch_out{b}", after=z[-1] if z else tok))
    project("D", z[2], then=("out",))
    merged = _merge_fwd(z, p["D"])
    (wout,) = fetch(("out",), merged)
    out = _mm(merged, wout, "nn", F32, 1024, 1024, D, "out_proj")
    sq, dy, dout, d_post = _post_loss(out, x, tgt, sp["post_norm_g"])

    tok = emit({"out": _mm(merged, dout, "tn", BF16, 1024, 1024, S, "d_wout")})
    dmerged = _mm(dout, wout, "nt", F32, 1024, 1024, D, "d_merged", after=tok)
    dz, dp_d = [], None
    tok = advance(dmerged)
    for b in range(3):
        dz_b, dp_d = _merge_bwd(dmerged, z[b], p["D"], b, dp_d, after=tok if b == 0 else None)
        dz.append(dz_b)
    d_win = lambda g, dp_g, after=None: _mm(dp_g, h, "tn", BF16, PROJ_TN[g], 1024, S, f"d_win_{g}", after=after)
    tok = emit({f"br{b}": _mm(ys[b], dz[b], "tn", BF16, 1024, 1024, S, f"d_wbr{b}") for b in range(3)}, tok)
    d_w_d = d_win("D", dp_d, tok)
    tok = emit({"D": d_w_d}, advance(d_w_d))
    dy_mem = _mm(dz[2], wbr[2], "nt", F32, 1024, 1024, D, "d_branch2", after=tok)
    tok = advance(dy_mem)
    dp_c, dmkv = _mem_bwd(dy_mem, p["C"], o_mem, mkv)
    dmkv16 = dmkv.astype(BF16)
    tok = emit({"mk": _mm(memn, dmkv16, "tn", BF16, 1024, 1024, MEM, "d_wmk", after=tok), "C": d_win("C", dp_c)}, tok)
    dmemn = _mm(dmkv16, wmk, "nt", F32, MEM, 1024, D, "d_memn", after=tok)
    tok = advance(dmemn)
    d_memg = _memnorm_bwd(dmemn, mem)
    dy_rg = _mm(dz[0], wbr[0], "nt", F32, 1024, 1024, D, "d_branch0", after=tok)
    dp_a, d_cw, d_cb, d_wa, d_ba, d_wx, d_bx, d_lam = _rglru_bwd(dy_rg, p["A"], hseq, *rnn)
    tok = emit({"A": d_win("A", dp_a)}, tok)
    dy_swa = _mm(dz[1], wbr[1], "nt", F32, 1024, 1024, D, "d_branch1", after=tok)
    tok = advance(dy_swa)
    dp_b, dk, dv, d_bias, d_sink = _swa_bwd(dy_swa, p["B"], o_swa, bias_t, sinks, after=tok)
    dp_b = _swa_pack(dp_b, dk, dv)
    d_rel = _relbias_grad(d_bias.reshape(SWA_HEADS, QB * KB2), onehot_t).T
    dp = {"A": dp_a, "B": dp_b, "C": dp_c, "D": dp_d}
    tok = emit({"B": d_win("B", dp_b)}, tok)
    dh = None
    for g in GROUPS:
        dh = _mm(dp[g], w_grp[g], "nn", F32, 1024, 1024, 2304 if g == "B" else 2048, f"d_h_{g}", acc=dh,
                 after=tok if g in ("A", "B") else None)
        if g == "A":
            tok = advance(dh)
    grad_x, d_pre = _pre_bwd(dh, x, dy, sp["pre_norm_g"])

    d_small = {
        "pre_norm_g": d_pre, "post_norm_g": d_post, "mem_norm_g": d_memg, "conv_w": d_cw, "conv_b": d_cb,
        "w_rg_a": d_wa, "b_rg_a": d_ba, "w_rg_x": d_wx, "b_rg_x": d_bx, "lru_lambda": d_lam,
        "swa_sinks": d_sink[:, 0].reshape(1, SWA_HEADS), "rel_bias": d_rel,
    }
    return sq, grad_x, d_small


ANY = pl.BlockSpec(memory_space=pl.ANY)
SHARD_ROWS = D // N_CHIPS
GATHERED = {"A": (2048, D), "B": (2304, D), "C": (2048, D), "D": (6144, D), "mk": (D, D),
            "br0": (D_RNN, D), "br1": (D_RNN, D), "br2": (D_RNN, D), "out": (D, D)}
SHARD_SHAPES = {"win": (SHARD, D), "mk": (SHARD_ROWS, D), "br0": (D_RNN, SHARD_ROWS), "br1": (D_RNN, SHARD_ROWS),
                "br2": (D_RNN, SHARD_ROWS), "out": (SHARD_ROWS, D)}
SHARDS = tuple(SHARD_SHAPES)
HALF_AXIS = {"win": 1, "mk": 1, "br0": 0, "br1": 0, "br2": 0, "out": 1,
             "A": 1, "B": 1, "C": 1, "D": 1}


def _halved(shape, axis):
    return (shape[0] // 2, shape[1]) if axis == 0 else (shape[0], shape[1] // 2)


class Piece(NamedTuple):
    src: str
    dst: str
    rows: int
    sr0: int
    sc0: int
    dr0: int
    dc0: int
    ncols: int


def _pieces_of(jj):
    out = [Piece("win", g, n, r, 0, gr, 0, D) for r, n, g, gr in _shard_runs(jj)]
    out.append(Piece("mk", "mk", SHARD_ROWS, 0, 0, SHARD_ROWS * jj, 0, D))
    out += [Piece(f"br{b}", f"br{b}", D_RNN, 0, 0, 0, SHARD_ROWS * jj, SHARD_ROWS) for b in range(3)]
    out.append(Piece("out", "out", SHARD_ROWS, 0, 0, SHARD_ROWS * jj, 0, D))
    return out


def _half_rect(ref, p, side, which):
    r0, c0 = (p.sr0, p.sc0) if side == "src" else (p.dr0, p.dc0)
    if HALF_AXIS[p.src] == 1:
        return _rect(ref, r0, p.rows, c0 + which * (p.ncols // 2), p.ncols // 2)
    return _rect(ref, r0 + which * (p.rows // 2), p.rows // 2, c0, p.ncols)


def _rect_in_half(ref, p, side):
    r0, c0 = (p.sr0, p.sc0) if side == "src" else (p.dr0, p.dc0)
    if HALF_AXIS[p.src] == 1:
        return _rect(ref, r0, p.rows, 0, p.ncols // 2)
    return _rect(ref, 0, p.rows // 2, c0, p.ncols)


MAX_PIECES = max(len(_pieces_of(jj)) for jj in range(N_CHIPS))


def _rect(ref, r0, rows, c0, ncols):
    return ref.at[pl.ds(r0, rows), pl.ds(c0, ncols)]


def _position():
    x, y, c = lax.axis_index("x"), lax.axis_index("y"), lax.axis_index("c")
    return x, y, c, 2 * x + y


HBM = pl.BlockSpec(memory_space=pltpu.HBM)
SEM = pl.BlockSpec(memory_space=pltpu.SEMAPHORE)
EFFECT = pltpu.SideEffectType.DATAFLOW_SIDE_EFFECTING
N_SEM = MAX_PIECES * N_CHIPS
GATHER_STAGES = (("A",), ("B",), ("C",), ("mk",), ("br0", "br1", "br2"), ("D",), ("out",))


def _in_hbm(a):
    return pltpu.with_memory_space_constraint(a, pltpu.HBM)


def _stage_pieces(jj, stage):
    return [(i, p) for i, p in enumerate(_pieces_of(jj)) if p.dst in stage]


def _own_block_table(g):
    import numpy as np
    units = np.full((N_CHIPS, GATHERED[g][0] // HALF_TILE), -1, np.int64)
    for jj in range(N_CHIPS):
        for r, n, grp, gr in _shard_runs(jj):
            if grp == g:
                for k in range(n // HALF_TILE):
                    units[jj, gr // HALF_TILE + k] = r // HALF_TILE + k
    tbl = np.zeros((N_CHIPS, 2, GATHERED[g][0] // LANE), np.int32)
    for jj in range(N_CHIPS):
        for b in range(tbl.shape[2]):
            first, second = units[jj, 2 * b], units[jj, 2 * b + 1]
            if jj % 2 == 0:
                src = first if first >= 0 else second - 1
                if first >= 0 or second >= 0:
                    assert src % 2 == 0
                    tbl[jj, :, b] = src // 2
            else:
                if first >= 0:
                    assert first % 2 == 1
                    tbl[jj, 0, b] = first // 2
                if second >= 0:
                    assert second % 2 == 0
                    tbl[jj, 1, b] = second // 2
    return tbl


def _place_group(w_t, g, tables, odd_arr, after):
    nb = GATHERED[g][0] // LANE

    def body(t_ref, odd_ref, a_ref, b_ref, _, o_ref):
        odd = odd_ref[0] == 1
        o_ref[0:HALF_TILE, :] = jnp.where(odd, a_ref[HALF_TILE:LANE, :], a_ref[0:HALF_TILE, :]).astype(BF16)
        o_ref[HALF_TILE:LANE, :] = jnp.where(odd, b_ref[0:HALF_TILE, :], a_ref[HALF_TILE:LANE, :]).astype(BF16)

    return pl.pallas_call(
        body,
        name=f"place_{g}",
        grid_spec=pltpu.PrefetchScalarGridSpec(
            num_scalar_prefetch=2,
            grid=(nb,),
            in_specs=[pl.BlockSpec((LANE, D), lambda b, t, o: (t[0, b], 0)),
                      pl.BlockSpec((LANE, D), lambda b, t, o: (t[1, b], 0)), ANY],
            out_specs=pl.BlockSpec((LANE, D), lambda b, t, o: (b, 0)),
        ),
        out_shape=jax.ShapeDtypeStruct(GATHERED[g], BF16),
        compiler_params=_params(("parallel",)),
    )(tables, odd_arr, w_t, w_t, after)


def _place_shard(shard, name, after):
    rows, cols = shard.shape
    by_rows = HALF_AXIS[name] == 1

    def body(x_ref, _, o_ref):
        o_ref[...] = x_ref[...].astype(BF16)

    return pl.pallas_call(
        body,
        name=f"place_{name}",
        grid=(N_CHIPS,),
        in_specs=[pl.BlockSpec((rows, cols), lambda b: (0, 0)), ANY],
        out_specs=pl.BlockSpec((rows, cols), (lambda b: (b, 0)) if by_rows else (lambda b: (0, b))),
        out_shape=jax.ShapeDtypeStruct(GATHERED[name], BF16),
        compiler_params=_params(("parallel",)),
    )(shard, after)


def _gather_copy(arr, send_sems, recv_sems, c, jj, i, p, kk):
    rect = _half_rect(arr[p.dst], p, "dst", c)
    return pltpu.make_async_remote_copy(
        src_ref=rect, dst_ref=rect, send_sem=send_sems.at[i * N_CHIPS + kk],
        recv_sem=recv_sems.at[jj * MAX_PIECES + i], device_id=(kk // 2, kk % 2, c), device_id_type=MESH)


def _gather_start(arrays, after):
    stage = tuple(arrays)
    na = len(stage)

    def body(*refs):
        arr = dict(zip(stage, refs[:na]))
        send_sems, recv_sems = refs[na + 1], refs[na + 2]
        token = refs[-1]
        _, _, c, j = _position()
        for jj in range(N_CHIPS):
            @pl.when(j == jj)
            def _():
                for i, p in _stage_pieces(jj, stage):
                    for kk in range(N_CHIPS):
                        if kk != jj:
                            _gather_copy(arr, send_sems, recv_sems, c, jj, i, p, kk).start()
        token[...] = jnp.zeros_like(token)

    outs = pl.pallas_call(
        body,
        name=f"gather_start_{stage[0]}",
        in_specs=[HBM] * na + [ANY],
        out_specs=[SEM, SEM] + [HBM] * na + [pl.BlockSpec(memory_space=pltpu.VMEM)],
        out_shape=[pltpu.SemaphoreType.DMA((N_SEM,)), pltpu.SemaphoreType.DMA((N_SEM,))]
        + [pltpu.HBM(GATHERED[n], BF16) for n in stage] + [jax.ShapeDtypeStruct((8, LANE), F32)],
        input_output_aliases={k: 2 + k for k in range(na)},
        compiler_params=pltpu.CompilerParams(has_side_effects=EFFECT),
    )(*[_in_hbm(arrays[n]) for n in stage], after)
    return outs[0], outs[1], dict(zip(stage, outs[2:2 + na])), outs[-1]


def _gather_wait(send_sems, recv_sems, arrays, after):
    stage = tuple(arrays)
    na = len(stage)

    def body(*refs):
        arr = dict(zip(stage, refs[:na]))
        sems_s, sems_r = refs[na], refs[na + 1]
        _, _, c, j = _position()
        for jj in range(N_CHIPS):
            @pl.when(j != jj)
            def _():
                for i, p in _stage_pieces(jj, stage):
                    _gather_copy(arr, sems_s, sems_r, c, jj, i, p, jj).wait_recv()

            @pl.when(j == jj)
            def _():
                for i, p in _stage_pieces(jj, stage):
                    for kk in range(N_CHIPS):
                        if kk != jj:
                            _gather_copy(arr, sems_s, sems_r, c, jj, i, p, kk).wait_send()

    outs = pl.pallas_call(
        body,
        name=f"gather_wait_{stage[0]}",
        in_specs=[HBM] * na + [SEM, SEM, ANY],
        out_specs=[HBM] * na,
        out_shape=[pltpu.HBM(GATHERED[n], BF16) for n in stage],
        input_output_aliases={k: k for k in range(na)},
        compiler_params=pltpu.CompilerParams(has_side_effects=EFFECT),
    )(*[arrays[n] for n in stage], send_sems, recv_sems, after)
    return dict(zip(stage, outs))


def _gather_swap(arrays):
    stage = tuple(arrays)
    na = len(stage)

    def body(*refs):
        dst = dict(zip(stage, refs[na:2 * na]))
        send_sems, recv_sems = refs[2 * na:]
        x, y, c, j = _position()

        def fwd(jj, i, p, which):
            rect = _half_rect(dst[p.dst], p, "dst", which)
            return pltpu.make_async_remote_copy(
                src_ref=rect, dst_ref=rect, send_sem=send_sems.at[jj * MAX_PIECES + i],
                recv_sem=recv_sems.at[jj * MAX_PIECES + i], device_id=(x, y, 1 - c), device_id_type=MESH)

        for jj in range(N_CHIPS):
            @pl.when(j != jj)
            def _():
                for i, p in _stage_pieces(jj, stage):
                    fwd(jj, i, p, c).start()
        for jj in range(N_CHIPS):
            @pl.when(j != jj)
            def _():
                for i, p in _stage_pieces(jj, stage):
                    fwd(jj, i, p, 1 - c).wait_recv()
        for jj in range(N_CHIPS):
            @pl.when(j != jj)
            def _():
                for i, p in _stage_pieces(jj, stage):
                    fwd(jj, i, p, c).wait_send()

    outs = pl.pallas_call(
        body,
        name=f"gather_swap_{stage[0]}",
        in_specs=[ANY] * na,
        out_specs=[ANY] * na,
        out_shape=[jax.ShapeDtypeStruct(GATHERED[n], BF16) for n in stage],
        input_output_aliases={k: k for k in range(na)},
        scratch_shapes=[pltpu.SemaphoreType.DMA((N_SEM,)), pltpu.SemaphoreType.DMA((N_SEM,))],
        compiler_params=pltpu.CompilerParams(has_side_effects=True),
    )(*[arrays[n] for n in stage])
    return dict(zip(stage, outs))


def _pass_on_copy(arr, send_sems, recv_sems, x, y, c, jj, i, p, which):
    rect = _half_rect(arr[p.dst], p, "dst", which)
    return pltpu.make_async_remote_copy(
        src_ref=rect, dst_ref=rect, send_sem=send_sems.at[jj * MAX_PIECES + i],
        recv_sem=recv_sems.at[jj * MAX_PIECES + i], device_id=(x, y, 1 - c), device_id_type=MESH)


def _gather_pass_start(arrays, after):
    stage = tuple(arrays)
    na = len(stage)

    def body(*refs):
        arr = dict(zip(stage, refs[:na]))
        x, y, c, j = _position()
        for jj in range(N_CHIPS):
            @pl.when(j != jj)
            def _():
                for i, p in _stage_pieces(jj, stage):
                    _pass_on_copy(arr, refs[na + 1], refs[na + 2], x, y, c, jj, i, p, c).start()
        refs[-1][...] = jnp.zeros_like(refs[-1])

    outs = pl.pallas_call(
        body,
        name=f"gather_pass_start_{stage[0]}",
        in_specs=[HBM] * na + [ANY],
        out_specs=[SEM, SEM] + [HBM] * na + [pl.BlockSpec(memory_space=pltpu.VMEM)],
        out_shape=[pltpu.SemaphoreType.DMA((N_SEM,)), pltpu.SemaphoreType.DMA((N_SEM,))]
        + [pltpu.HBM(GATHERED[n], BF16) for n in stage] + [jax.ShapeDtypeStruct((8, LANE), F32)],
        input_output_aliases={k: 2 + k for k in range(na)},
        compiler_params=pltpu.CompilerParams(has_side_effects=EFFECT),
    )(*[arrays[n] for n in stage], after)
    return outs[0], outs[1], dict(zip(stage, outs[2:2 + na])), outs[-1]


def _gather_pass_wait(send_sems, recv_sems, arrays, after):
    stage = tuple(arrays)
    na = len(stage)

    def body(*refs):
        arr = dict(zip(stage, refs[:na]))
        x, y, c, j = _position()
        for jj in range(N_CHIPS):
            @pl.when(j != jj)
            def _():
                for i, p in _stage_pieces(jj, stage):
                    _pass_on_copy(arr, refs[na], refs[na + 1], x, y, c, jj, i, p, 1 - c).wait_recv()
                    _pass_on_copy(arr, refs[na], refs[na + 1], x, y, c, jj, i, p, c).wait_send()

    outs = pl.pallas_call(
        body,
        name=f"gather_pass_wait_{stage[0]}",
        in_specs=[HBM] * na + [SEM, SEM, ANY],
        out_specs=[HBM] * na,
        out_shape=[pltpu.HBM(GATHERED[n], BF16) for n in stage],
        input_output_aliases={k: k for k in range(na)},
        compiler_params=pltpu.CompilerParams(has_side_effects=EFFECT),
    )(*[arrays[n] for n in stage], send_sems, recv_sems, after)
    return dict(zip(stage, outs))


def _own_half(ref, shape, axis, which):
    if axis == 1:
        return ref.at[:, pl.ds(which * (shape[1] // 2), shape[1] // 2)]
    return ref.at[pl.ds(which * (shape[0] // 2), shape[0] // 2), :]


def _swap_copies(names, src, dst, send_sems, recv_sems):
    x, y, c, _ = _position()
    return [pltpu.make_async_remote_copy(
        src_ref=_own_half(src[n], GATHERED[n], HALF_AXIS[n], 1 - c), dst_ref=dst[n],
        send_sem=send_sems.at[k], recv_sem=recv_sems.at[k],
        device_id=(x, y, 1 - c), device_id_type=MESH) for k, n in enumerate(names)]


def _swap_start(grads, after):
    names = tuple(grads)
    n = len(names)

    def body(*refs):
        src, dst = dict(zip(names, refs[:n])), dict(zip(names, refs[n:2 * n]))
        for cp in _swap_copies(names, src, dst, refs[2 * n + 1], refs[2 * n + 2]):
            cp.start()
        refs[-1][...] = jnp.zeros_like(refs[-1])

    half_shape = lambda nm: _halved(GATHERED[nm], HALF_AXIS[nm])
    args = [_in_hbm(grads[nm]) for nm in names] + [_in_hbm(lax.empty(half_shape(nm), BF16)) for nm in names]
    if after is None:
        after = jnp.zeros((8, LANE), F32)
    outs = pl.pallas_call(
        body,
        name=f"swap_start_{names[0]}",
        in_specs=[HBM] * (2 * n) + [ANY],
        out_specs=[SEM, SEM] + [HBM] * (2 * n) + [pl.BlockSpec(memory_space=pltpu.VMEM)],
        out_shape=[pltpu.SemaphoreType.DMA((n,)), pltpu.SemaphoreType.DMA((n,))]
        + [pltpu.HBM(GATHERED[nm], BF16) for nm in names] + [pltpu.HBM(half_shape(nm), BF16) for nm in names]
        + [jax.ShapeDtypeStruct((8, LANE), F32)],
        input_output_aliases={k: 2 + k for k in range(2 * n)},
        compiler_params=pltpu.CompilerParams(has_side_effects=EFFECT),
    )(*args, after)
    return outs[0], outs[1], dict(zip(names, outs[2:2 + n])), dict(zip(names, outs[2 + n:2 + 2 * n])), outs[-1]


def _swap_wait(send_sems, recv_sems, grads, landing, after):
    names = tuple(grads)
    n = len(names)

    def body(*refs):
        src, dst = dict(zip(names, refs[:n])), dict(zip(names, refs[n:2 * n]))
        copies = _swap_copies(names, src, dst, refs[2 * n], refs[2 * n + 1])
        for cp in copies:
            cp.wait_recv()
        for cp in copies:
            cp.wait_send()

    half_shape = lambda nm: _halved(GATHERED[nm], HALF_AXIS[nm])
    outs = pl.pallas_call(
        body,
        name=f"swap_wait_{names[0]}",
        in_specs=[HBM] * (2 * n) + [SEM, SEM, ANY],
        out_specs=[HBM] * (2 * n),
        out_shape=[pltpu.HBM(GATHERED[nm], BF16) for nm in names] + [pltpu.HBM(half_shape(nm), BF16) for nm in names],
        input_output_aliases={k: k for k in range(2 * n)},
        compiler_params=pltpu.CompilerParams(has_side_effects=EFFECT),
    )(*[grads[nm] for nm in names], *[landing[nm] for nm in names], send_sems, recv_sems, after)
    return dict(zip(names, outs[:n])), dict(zip(names, outs[n:]))


ADD_ROWS = 256


def _add_half(full, recv, c_arr, name):
    rows, cols = recv.shape
    if HALF_AXIS[name] == 1:
        index = lambda i, c_ref: (i, c_ref[0])
    else:
        nb = rows // ADD_ROWS
        index = lambda i, c_ref: (nb * c_ref[0] + i, 0)

    def body(c_ref, a_ref, b_ref, o_ref):
        o_ref[...] = (a_ref[...].astype(F32) + b_ref[...].astype(F32)).astype(BF16)

    return pl.pallas_call(
        body,
        name=f"add_half_{name}",
        grid_spec=pltpu.PrefetchScalarGridSpec(
            num_scalar_prefetch=1,
            grid=(rows // ADD_ROWS,),
            in_specs=[pl.BlockSpec((ADD_ROWS, cols), index), pl.BlockSpec((ADD_ROWS, cols), lambda i, c_ref: (i, 0))],
            out_specs=pl.BlockSpec((ADD_ROWS, cols), lambda i, c_ref: (i, 0)),
        ),
        out_shape=jax.ShapeDtypeStruct((rows, cols), BF16),
        compiler_params=_params(("parallel",)),
    )(c_arr, full, recv)


SLOT_SHAPES = {n: _halved(SHARD_SHAPES[n], HALF_AXIS[n]) for n in SHARDS}


def _slot_shape(n):
    return (N_CHIPS,) + SLOT_SHAPES[n]


def _stage_shards(stage):
    pieces = [p for jj in range(N_CHIPS) for p in _pieces_of(jj)]
    return tuple(s for s in SHARDS if any(p.src == s and p.dst in stage for p in pieces))


def _scatter_copy(src, dst, send_sems, recv_sems, c, jj, kk, i, p):
    return pltpu.make_async_remote_copy(
        src_ref=_rect_in_half(src[p.dst], p, "dst"), dst_ref=_rect_in_half(dst[p.src].at[jj], p, "src"),
        send_sem=send_sems.at[kk * MAX_PIECES + i], recv_sem=recv_sems.at[jj * MAX_PIECES + i],
        device_id=(kk // 2, kk % 2, c), device_id_type=MESH)


def _scatter_start(halves, slots):
    stage, touched = tuple(halves), tuple(slots)
    nh, nt = len(stage), len(touched)

    def body(*refs):
        src = dict(zip(stage, refs[:nh]))
        dst = dict(zip(touched, refs[nh:nh + nt]))
        send_sems, recv_sems = refs[nh + nt], refs[nh + nt + 1]
        token = refs[-1]
        _, _, c, j = _position()
        for jj in range(N_CHIPS):
            @pl.when(j == jj)
            def _():
                for kk in range(N_CHIPS):
                    if kk != jj:
                        for i, p in _stage_pieces(kk, stage):
                            _scatter_copy(src, dst, send_sems, recv_sems, c, jj, kk, i, p).start()
        token[...] = jnp.zeros_like(token)

    outs = pl.pallas_call(
        body,
        name=f"scatter_start_{stage[0]}",
        in_specs=[HBM] * (nh + nt),
        out_specs=[SEM, SEM] + [HBM] * (nh + nt) + [pl.BlockSpec(memory_space=pltpu.VMEM)],
        out_shape=[pltpu.SemaphoreType.DMA((N_SEM,)), pltpu.SemaphoreType.DMA((N_SEM,))]
        + [pltpu.HBM(halves[n].shape, BF16) for n in stage] + [pltpu.HBM(_slot_shape(s), BF16) for s in touched]
        + [jax.ShapeDtypeStruct((8, LANE), F32)],
        input_output_aliases={k: 2 + k for k in range(nh + nt)},
        compiler_params=pltpu.CompilerParams(has_side_effects=EFFECT),
    )(*[_in_hbm(halves[n]) for n in stage], *[_in_hbm(slots[s]) for s in touched])
    return outs[0], outs[1], dict(zip(stage, outs[2:2 + nh])), dict(zip(touched, outs[2 + nh:2 + nh + nt])), outs[-1]


def _scatter_wait(send_sems, recv_sems, halves, slots, after):
    stage, touched = tuple(halves), tuple(slots)
    nh, nt = len(stage), len(touched)

    def body(*refs):
        src = dict(zip(stage, refs[:nh]))
        dst = dict(zip(touched, refs[nh:nh + nt]))
        sems_s, sems_r = refs[nh + nt], refs[nh + nt + 1]
        _, _, c, j = _position()
        for jj in range(N_CHIPS):
            @pl.when(j == jj)
            def _():
                for ss in range(N_CHIPS):
                    if ss != jj:
                        for i, p in _stage_pieces(jj, stage):
                            _scatter_copy(src, dst, sems_s, sems_r, c, ss, jj, i, p).wait_recv()
                for kk in range(N_CHIPS):
                    if kk != jj:
                        for i, p in _stage_pieces(kk, stage):
                            _scatter_copy(src, dst, sems_s, sems_r, c, jj, kk, i, p).wait_send()

    outs = pl.pallas_call(
        body,
        name=f"scatter_wait_{stage[0]}",
        in_specs=[HBM] * (nh + nt) + [SEM, SEM, ANY],
        out_specs=[HBM] * (nh + nt),
        out_shape=[pltpu.HBM(halves[n].shape, BF16) for n in stage] + [pltpu.HBM(_slot_shape(s), BF16) for s in touched],
        input_output_aliases={k: k for k in range(nh + nt)},
        compiler_params=pltpu.CompilerParams(has_side_effects=EFFECT),
    )(*[halves[n] for n in stage], *[slots[s] for s in touched], send_sems, recv_sems, after)
    return dict(zip(stage, outs[:nh])), dict(zip(touched, outs[nh:]))


SUM_ROWS = {"mk": 256, "br0": 256, "br1": 256, "br2": 256, "out": 256}


def _sum_in_chip_order(chip, own, s_ref):
    acc = None
    for k in range(N_CHIPS):
        term = jnp.where(chip == k, own, s_ref[k].astype(F32))
        acc = term if acc is None else acc + term
    return acc


def _sum_slots(slots, own_half, pos_arr, name):
    _, rows, cols = slots.shape
    tr = SUM_ROWS[name]
    nb = rows // tr
    if HALF_AXIS[name] == 1:
        own_index = lambda i, pos: (nb * pos[1] + i, 0)
        out_index = lambda i, pos: (i, pos[0])
    else:
        own_index = lambda i, pos: (i, pos[1])
        out_index = lambda i, pos: (nb * pos[0] + i, 0)

    def body(pos, s_ref, own_ref, o_ref):
        o_ref[...] = _sum_in_chip_order(pos[1], own_ref[...].astype(F32), s_ref)

    return pl.pallas_call(
        body,
        name=f"sum_slots_{name}",
        grid_spec=pltpu.PrefetchScalarGridSpec(
            num_scalar_prefetch=1,
            grid=(nb,),
            in_specs=[pl.BlockSpec((N_CHIPS, tr, cols), lambda i, pos: (0, i, 0)),
                      pl.BlockSpec((tr, cols), own_index)],
            out_specs=pl.BlockSpec((tr, cols), out_index),
        ),
        out_shape=jax.ShapeDtypeStruct(SHARD_SHAPES[name], F32),
        compiler_params=_params(("parallel",)),
    )(pos_arr, slots, own_half)


def _own_partial_tables():
    import numpy as np
    nb = SHARD // HALF_TILE
    grp, blk = np.zeros((N_CHIPS, nb), np.int32), np.zeros((N_CHIPS, nb), np.int32)
    for jj in range(N_CHIPS):
        for r, n, g, gr in _shard_runs(jj):
            for k in range(n // HALF_TILE):
                grp[jj, r // HALF_TILE + k] = GROUPS.index(g)
                blk[jj, r // HALF_TILE + k] = gr // HALF_TILE + k
    return grp, blk


def _sum_slots_win(slots, own_halves, pos_arr, grp_tbl, blk_tbl):
    nb = SHARD // HALF_TILE
    cols = D // 2

    def own_spec(gi):
        return pl.BlockSpec((HALF_TILE, cols), lambda b, pos, grp, blk: (jnp.where(grp[b] == gi, blk[b], 0), 0))

    def body(pos, grp, blk, s_ref, a_ref, b_ref, c_ref, d_ref, o_ref):
        g = grp[pl.program_id(0)]
        own = a_ref[...]
        for gi, ref in ((1, b_ref), (2, c_ref), (3, d_ref)):
            own = jnp.where(g == gi, ref[...], own)
        o_ref[...] = _sum_in_chip_order(pos[1], own.astype(F32), s_ref)

    return pl.pallas_call(
        body,
        name="sum_slots_win",
        grid_spec=pltpu.PrefetchScalarGridSpec(
            num_scalar_prefetch=3,
            grid=(nb,),
            in_specs=[pl.BlockSpec((N_CHIPS, HALF_TILE, cols), lambda b, pos, grp, blk: (0, b, 0))]
            + [own_spec(gi) for gi in range(len(GROUPS))],
            out_specs=pl.BlockSpec((HALF_TILE, cols), lambda b, pos, grp, blk: (b, pos[0])),
        ),
        out_shape=jax.ShapeDtypeStruct(SHARD_SHAPES["win"], F32),
        compiler_params=_params(("parallel",)),
    )(pos_arr, grp_tbl, blk_tbl, slots, *[own_halves[g] for g in GROUPS])


def _share_copy(buf, name, send_sems, recv_sems, k, which):
    x, y, c, _ = _position()
    half = _own_half(buf, SHARD_SHAPES[name], HALF_AXIS[name], which)
    return pltpu.make_async_remote_copy(src_ref=half, dst_ref=half, send_sem=send_sems.at[k], recv_sem=recv_sems.at[k],
                                        device_id=(x, y, 1 - c), device_id_type=MESH)


def _share_start(sums, after):
    names = tuple(sums)
    n = len(names)

    def body(*refs):
        _, _, c, _ = _position()
        for k, nm in enumerate(names):
            _share_copy(refs[k], nm, refs[n + 1], refs[n + 2], k, c).start()
        refs[-1][...] = jnp.zeros_like(refs[-1])

    outs = pl.pallas_call(
        body,
        name=f"share_start_{names[0]}",
        in_specs=[HBM] * n + [ANY],
        out_specs=[SEM, SEM] + [HBM] * n + [pl.BlockSpec(memory_space=pltpu.VMEM)],
        out_shape=[pltpu.SemaphoreType.DMA((n,)), pltpu.SemaphoreType.DMA((n,))]
        + [pltpu.HBM(SHARD_SHAPES[nm], F32) for nm in names] + [jax.ShapeDtypeStruct((8, LANE), F32)],
        input_output_aliases={k: 2 + k for k in range(n)},
        compiler_params=pltpu.CompilerParams(has_side_effects=EFFECT),
    )(*[_in_hbm(sums[nm]) for nm in names], after)
    return outs[0], outs[1], dict(zip(names, outs[2:2 + n])), outs[-1]


def _share_wait(send_sems, recv_sems, sums, after):
    names = tuple(sums)
    n = len(names)

    def body(*refs):
        _, _, c, _ = _position()
        for k, nm in enumerate(names):
            _share_copy(refs[k], nm, refs[n], refs[n + 1], k, 1 - c).wait_recv()
            _share_copy(refs[k], nm, refs[n], refs[n + 1], k, c).wait_send()

    outs = pl.pallas_call(
        body,
        name=f"share_wait_{names[0]}",
        in_specs=[HBM] * n + [SEM, SEM, ANY],
        out_specs=[HBM] * n,
        out_shape=[pltpu.HBM(SHARD_SHAPES[nm], F32) for nm in names],
        input_output_aliases={k: k for k in range(n)},
        compiler_params=pltpu.CompilerParams(has_side_effects=EFFECT),
    )(*[sums[nm] for nm in names], send_sems, recv_sems, after)
    return dict(zip(names, outs))


def _all_reduce_small(pack, name):
    rows = pack.shape[0]
    half = rows // 2

    def body(p_ref, o_ref, sib, land, sems):
        x, y, c, j = _position()
        sibling = (x, y, 1 - c)
        swap = pltpu.make_async_remote_copy(src_ref=p_ref, dst_ref=sib, send_sem=sems.at[0], recv_sem=sems.at[1],
                                            device_id=sibling, device_id_type=MESH)
        swap.start()
        swap.wait_recv()
        land[j] = p_ref[...] + sib[...]

        def mine(k, which):
            return land.at[k, pl.ds(which * half, half)]

        def ici(kk):
            return pltpu.make_async_remote_copy(
                src_ref=mine(j, c), dst_ref=mine(j, c), send_sem=sems.at[2 + kk], recv_sem=sems.at[6 + j],
                device_id=(kk // 2, kk % 2, c), device_id_type=MESH)

        def arrival(kk):
            return pltpu.make_async_remote_copy(
                src_ref=mine(kk, c), dst_ref=mine(kk, c), send_sem=sems.at[2 + kk], recv_sem=sems.at[6 + kk],
                device_id=(kk // 2, kk % 2, c), device_id_type=MESH)

        def passed_on(kk, which):
            return pltpu.make_async_remote_copy(
                src_ref=mine(kk, which), dst_ref=mine(kk, which), send_sem=sems.at[10 + kk],
                recv_sem=sems.at[14 + kk], device_id=sibling, device_id_type=MESH)

        for kk in range(N_CHIPS):
            @pl.when(j != kk)
            def _():
                ici(kk).start()
        for kk in range(N_CHIPS):
            @pl.when(j != kk)
            def _():
                arrival(kk).wait_recv()
                passed_on(kk, c).start()
        for kk in range(N_CHIPS):
            @pl.when(j != kk)
            def _():
                passed_on(kk, 1 - c).wait_recv()
        acc = land[0]
        for kk in range(1, N_CHIPS):
            acc = acc + land[kk]
        o_ref[...] = acc
        swap.wait_send()
        for kk in range(N_CHIPS):
            @pl.when(j != kk)
            def _():
                ici(kk).wait_send()
                passed_on(kk, c).wait_send()

    vmem = pl.BlockSpec(memory_space=pltpu.VMEM)
    return pl.pallas_call(
        body,
        name=name,
        in_specs=[vmem],
        out_specs=vmem,
        out_shape=jax.ShapeDtypeStruct((rows, LANE), F32),
        scratch_shapes=[pltpu.VMEM((rows, LANE), F32), pltpu.VMEM((N_CHIPS, rows, LANE), F32),
                        pltpu.SemaphoreType.DMA((18,))],
        compiler_params=pltpu.CompilerParams(has_side_effects=True, vmem_limit_bytes=VMEM_LIMIT),
    )(pack)


ADAM_ROWS = {"win": 224, "mk": 256, "br0": 512, "br1": 512, "br2": 512, "out": 256}


def _adamw(w, g, m, v, name, tr):
    rows, cols = w.shape
    tr = min(tr, rows)

    def body(w_ref, g_ref, m_ref, v_ref, go_ref, d_ref, nm_ref, nv_ref):
        gv = g_ref[...]
        go_ref[...] = gv
        nm = ADAM_B1 * m_ref[...] + (1.0 - ADAM_B1) * gv
        nv = ADAM_B2 * v_ref[...] + (1.0 - ADAM_B2) * (gv * gv)
        nm_ref[...] = nm
        nv_ref[...] = nv
        m_hat = nm / (1.0 - ADAM_B1 ** ADAM_STEP)
        v_hat = nv / (1.0 - ADAM_B2 ** ADAM_STEP)
        d_ref[...] = -ADAM_LR * (m_hat / (jnp.sqrt(v_hat) + ADAM_EPS) + ADAM_WD * w_ref[...])

    blk = pl.BlockSpec((tr, cols), lambda i: (i, 0))
    shape = jax.ShapeDtypeStruct((rows, cols), F32)
    return pl.pallas_call(
        body,
        name=f"adamw_{name}",
        grid=(rows // tr,),
        in_specs=[blk] * 4,
        out_specs=[blk] * 4,
        out_shape=[shape] * 4,
        compiler_params=_params(("parallel",)),
    )(w, g, m, v)


SMALL = (("pre_norm_g", (1, D)), ("post_norm_g", (1, D)), ("mem_norm_g", (1, D)), ("conv_w", (CONV_W, D_RNN)),
         ("conv_b", (1, D_RNN)), ("w_rg_a", (RNN_BLOCKS, LANE, LANE)), ("b_rg_a", (1, D_RNN)),
         ("w_rg_x", (RNN_BLOCKS, LANE, LANE)), ("b_rg_x", (1, D_RNN)), ("lru_lambda", (1, D_RNN)),
         ("swa_sinks", (1, SWA_HEADS)), ("rel_bias", (REL_BUCKETS, SWA_HEADS)))
PACK_ROWS = 2176


def _slot_len(shape):
    return -(-math.prod(shape) // LANE) * LANE


def _pack(values, last_row=None):
    parts = []
    for name, shape in SMALL:
        flat = values[name].reshape(-1).astype(F32)
        parts.append(jnp.pad(flat, (0, _slot_len(shape) - flat.shape[0])))
    flat = jnp.concatenate(parts)
    tail = jnp.zeros((LANE,), F32) if last_row is None else last_row
    return jnp.concatenate([jnp.pad(flat, (0, (PACK_ROWS - 1) * LANE - flat.shape[0])), tail]).reshape(PACK_ROWS, LANE)


def _unpack(pack):
    flat = pack.reshape(-1)
    out, off = {}, 0
    for name, shape in SMALL:
        out[name] = flat[off:off + math.prod(shape)].reshape(shape)
        off += _slot_len(shape)
    return out


TWIN_WEIGHTS = ("pre_norm_g", "post_norm_g", "mem_norm_g", "w_in", "conv_w", "conv_b", "w_rg_a", "b_rg_a", "w_rg_x",
                "b_rg_x", "lru_lambda", "swa_sinks", "rel_bias", "w_mem_kv", "w_br_rg", "w_br_swa", "w_br_mem", "w_out")
BIG = {"w_in": "win", "w_mem_kv": "mk", "w_br_rg": "br0", "w_br_swa": "br1", "w_br_mem": "br2", "w_out": "out"}


def kernel(x, mem, pre_norm_g, post_norm_g, mem_norm_g, w_in, conv_w, conv_b, w_rg_a, b_rg_a, w_rg_x, b_rg_x, lru_lambda, swa_sinks, rel_bias, w_mem_kv, w_br_rg, w_br_swa, w_br_mem, w_out, loss_target, m_pre_norm_g, m_post_norm_g, m_mem_norm_g, m_w_in, m_conv_w, m_conv_b, m_w_rg_a, m_b_rg_a, m_w_rg_x, m_b_rg_x, m_lru_lambda, m_swa_sinks, m_rel_bias, m_w_mem_kv, m_w_br_rg, m_w_br_swa, m_w_br_mem, m_w_out, v_pre_norm_g, v_post_norm_g, v_mem_norm_g, v_w_in, v_conv_w, v_conv_b, v_w_rg_a, v_b_rg_a, v_w_rg_x, v_b_rg_x, v_lru_lambda, v_swa_sinks, v_rel_bias, v_w_mem_kv, v_w_br_rg, v_w_br_swa, v_w_br_mem, v_w_out):
    args = dict(locals())
    out_shapes = {n: args[n].shape for n in TWIN_WEIGHTS}
    w = {n: (args[n] if n == "rel_bias" else args[n][0]) for n in TWIN_WEIGHTS}
    m = {n: (args["m_" + n] if n == "rel_bias" else args["m_" + n][0]) for n in TWIN_WEIGHTS}
    v = {n: (args["v_" + n] if n == "rel_bias" else args["v_" + n][0]) for n in TWIN_WEIGHTS}
    for d in (w, m, v):
        for n, shape in SMALL:
            if n != "conv_w":
                d[n] = d[n].reshape(shape)

    xi, yi, ci = lax.axis_index("x"), lax.axis_index("y"), lax.axis_index("c")
    chip = 2 * xi + yi
    c_arr = ci.astype(jnp.int32).reshape(1)
    zero = jnp.zeros((), jnp.int32)
    cw0 = (chip * (D_RNN // N_CHIPS)).astype(jnp.int32)

    placed = lax.dynamic_update_slice(jnp.zeros((CONV_W, D_RNN), F32), w["conv_w"], (zero, cw0))
    placed = jnp.where(ci == 0, placed, 0.0).reshape(CONV_W * D_RNN // LANE, LANE)
    conv_w_full = _all_reduce_small(placed, "gather_conv_w").reshape(CONV_W, D_RNN)

    for d in (w, m, v):
        d["w_in"] = d["w_in"].T
    chip_row = lambda tbl: lax.dynamic_slice(jnp.asarray(tbl), (chip.astype(jnp.int32), zero), (1, tbl.shape[1]))[0]
    chip_tables = lambda tbl: lax.dynamic_slice(jnp.asarray(tbl), (chip.astype(jnp.int32), zero, zero),
                                                (1,) + tbl.shape[1:])[0]
    odd_arr = yi.astype(jnp.int32).reshape(1)
    big_of = {s: n for n, s in BIG.items()}
    ag, token = {}, conv_w_full
    for stage in GATHER_STAGES:
        behind = c_arr if stage == GATHER_STAGES[0] else token
        placed = {n: (_place_group(w["w_in"], n, chip_tables(_own_block_table(n)), odd_arr, behind) if n in GROUPS
                      else _place_shard(w[big_of[n]], n, behind)) for n in stage}
        send, recv, in_flight, token = _gather_start(placed, token)
        ag[stage] = (send, recv, in_flight)
    h = token = _rms_fwd(x[0], w["pre_norm_g"], "rms_pre", token)

    all_started = token

    passing = {}

    def prefetch(names, after):
        send, recv, in_flight = ag[names]
        *passing[names], token = _gather_pass_start(_gather_wait(send, recv, in_flight, after), after)
        return token

    def fetch(names, after):
        if names in passing:
            ready = _gather_pass_wait(*passing.pop(names), after)
        else:
            send, recv, in_flight = ag[names]
            after = all_started if names == GATHER_STAGES[0] else after
            ready = _gather_swap(_gather_wait(send, recv, in_flight, after))
        return tuple(ready[n] for n in names)

    rs = {"slots": {}, "halves": {}, "pending": [], "swap": None}

    def emit(grads, after=None):
        assert rs["swap"] is None
        *rs["swap"], token = _swap_start(grads, after)
        return token

    def advance(after):
        grads, received = _swap_wait(*rs["swap"], after)
        rs["swap"] = None
        halves = {n: _add_half(grads[n], received[n], c_arr, n) for n in grads}
        landing = {s: rs["slots"][s] if s in rs["slots"] else lax.empty(_slot_shape(s), BF16)
                   for s in _stage_shards(tuple(grads))}
        send, recv, halves, landing, token = _scatter_start(halves, landing)
        rs["slots"].update(landing)
        rs["pending"].append((send, recv, halves, tuple(landing)))
        return token

    sp = {n: w[n] for n, _ in SMALL}
    sp["conv_w"] = conv_w_full
    sq, grad_x, d_small = _local_step(x[0], h, mem[0], loss_target[0], sp, fetch, prefetch, emit, advance)
    small_total = _all_reduce_small(_pack(d_small, sq[0]), "all_reduce_small")
    loss = small_total[PACK_ROWS - 1, 0] * (0.5 / D)

    for send, recv, halves, touched in rs["pending"]:
        halves, landed = _scatter_wait(send, recv, halves, {s: rs["slots"][s] for s in touched}, small_total)
        rs["slots"].update(landed)
        rs["halves"].update(halves)
    pos_arr = jnp.stack([ci, chip]).astype(jnp.int32)
    grp_tbl, blk_tbl = (chip_row(t) for t in _own_partial_tables())
    rest = {s: _sum_slots(rs["slots"][s], rs["halves"][s], pos_arr, s) for s in SHARDS if s != "win"}
    *rest_share, tok = _share_start(rest, small_total)
    win_sum = _sum_slots_win(rs["slots"]["win"], rs["halves"], pos_arr, grp_tbl, blk_tbl)
    *win_share, tok = _share_start({"win": win_sum}, tok)
    sums = _share_wait(*rest_share, tok)

    grad, delta, new_m, new_v = {}, {}, {}, {}
    for n, s in BIG.items():
        if n == "w_in":
            continue
        grad[n], delta[n], new_m[n], new_v[n] = _adamw(w[n], sums[s], m[n], v[n], s, ADAM_ROWS[s])
    g_win = _share_wait(*win_share, delta["w_out"])["win"]
    n = "w_in"
    grad[n], delta[n], new_m[n], new_v[n] = _adamw(w[n], g_win, m[n], v[n], "win", ADAM_ROWS["win"])
    for group in (grad, delta, new_m, new_v):
        group["w_in"] = group["w_in"].T
    def conv_w_in_place(d):
        return dict(d, conv_w=lax.dynamic_update_slice(jnp.zeros((CONV_W, D_RNN), F32), d["conv_w"], (zero, cw0)))

    _, d_, m_, v_ = _adamw(_pack(conv_w_in_place(w)), small_total, _pack(conv_w_in_place(m)),
                           _pack(conv_w_in_place(v)), "small", PACK_ROWS)
    for group, pack in ((grad, small_total), (delta, d_), (new_m, m_), (new_v, v_)):
        group.update(_unpack(pack))
    for group in (grad, delta, new_m, new_v):
        group["conv_w"] = lax.dynamic_slice(group["conv_w"], (zero, cw0), (CONV_W, D_RNN // N_CHIPS))

    outs = [loss, grad_x.reshape(1, S, D)]
    for group in (grad, delta, new_m, new_v):
        outs += [group[n].reshape(out_shapes[n]) for n in TWIN_WEIGHTS]
    return tuple(outs)
```

```python
import math
from typing import NamedTuple

import jax
import jax.numpy as jnp
from jax import lax
from jax.experimental import pallas as pl
from jax.experimental.pallas import tpu as pltpu

F32 = jnp.float32
BF16 = jnp.bfloat16
MESH = pl.DeviceIdType.MESH

S = 2048
D = 2048
MEM = 256
D_RNN = 1024
RNN_BLOCKS = 8
CONV_W = 4
LRU_C = 8.0
SWA_HEADS = 16
SWA_HD = 64
WINDOW = 128
MEM_HEADS = 4
MEM_HD = 256
REL_BUCKETS = 32
REL_MAX_DIST = 128
EPS = 1e-6
NEG_INF = -1e30
LANE = 128
SHARD = 3136
HALF_TILE = 64
N_CHIPS = 4
VMEM_LIMIT = 56 * 1024 * 1024

ADAM_LR = 0.001
ADAM_B1 = 0.9
ADAM_B2 = 0.999
ADAM_EPS = 1e-08
ADAM_WD = 0.01
ADAM_STEP = 10

GROUP_TILES = {"A": 16, "B": 18, "C": 16, "D": 48}
GROUPS = ("A", "B", "C", "D")


def _params(sem=None):
    return pltpu.CompilerParams(dimension_semantics=sem, vmem_limit_bytes=VMEM_LIMIT)


def _sigmoid(v):
    return jax.nn.sigmoid(v)


def _tile_home(t):
    if t < 16:
        return "A", t
    if t < 24:
        return "B", t - 16
    if t < 26:
        return "B", t - 24 + 16
    if t < 34:
        return "B", t - 26 + 8
    if t < 50:
        return "C", t - 34
    return "D", t - 50


def _shard_runs(j):
    runs = []
    per_shard = SHARD // HALF_TILE
    for q in range(per_shard * j, per_shard * (j + 1)):
        g, gt = _tile_home(q // 2)
        row = gt * LANE + (q % 2) * HALF_TILE
        if runs and runs[-1][2] == g and runs[-1][3] + runs[-1][1] == row:
            runs[-1][1] += HALF_TILE
        else:
            runs.append([(q - per_shard * j) * HALF_TILE, HALF_TILE, g, row])
    return [tuple(r) for r in runs]


_DIMS = {
    "nn": (((1,), (0,)), ((), ())),
    "nt": (((1,), (1,)), ((), ())),
    "tn": (((0,), (0,)), ((), ())),
}


def _mm(a, b, mode, out_dtype, tm, tn, tk, name, acc=None, after=None):
    if mode == "nn":
        (m, k), n = a.shape, b.shape[1]
    elif mode == "nt":
        (m, k), n = a.shape, b.shape[0]
    else:
        (k, m), n = a.shape, b.shape[1]
    tm, tn, tk = min(tm, m), min(tn, n), min(tk, k)
    assert m % tm == 0 and n % tn == 0 and k % tk == 0, (name, m, n, k)
    nk = k // tk
    has_acc = acc is not None

    def body(*refs):
        a_ref, b_ref = refs[0], refs[1]
        o_ref = refs[3] if has_acc else refs[2]
        p = lax.dot_general(a_ref[...], b_ref[...], _DIMS[mode], preferred_element_type=F32)

        def finish(v):
            if has_acc:
                v = v + refs[2][...]
            o_ref[...] = v.astype(out_dtype)

        if nk == 1:
            finish(p)
        else:
            s_ref = refs[-1]
            kk = pl.program_id(2)

            @pl.when(kk == 0)
            def _():
                s_ref[...] = p

            @pl.when(kk > 0)
            def _():
                s_ref[...] += p

            @pl.when(kk == nk - 1)
            def _():
                finish(s_ref[...])

    if mode == "nn":
        a_spec = pl.BlockSpec((tm, tk), lambda i, j, kk: (i, kk))
        b_spec = pl.BlockSpec((tk, tn), lambda i, j, kk: (kk, j))
    elif mode == "nt":
        a_spec = pl.BlockSpec((tm, tk), lambda i, j, kk: (i, kk))
        b_spec = pl.BlockSpec((tn, tk), lambda i, j, kk: (j, kk))
    else:
        a_spec = pl.BlockSpec((tk, tm), lambda i, j, kk: (kk, i))
        b_spec = pl.BlockSpec((tk, tn), lambda i, j, kk: (kk, j))
    o_spec = pl.BlockSpec((tm, tn), lambda i, j, kk: (i, j))
    in_specs = [a_spec, b_spec] + ([o_spec] if has_acc else [])
    args = (a, b) + ((acc,) if has_acc else ())
    if after is not None:
        in_specs.append(pl.BlockSpec(memory_space=pl.ANY))
        args += (after,)
    n_in = len(args)
    kernel_body = body

    def body(*refs):
        kernel_body(*(refs[:n_in - (after is not None)] + refs[n_in:]))

    return pl.pallas_call(
        body,
        name=name,
        grid=(m // tm, n // tn, nk),
        in_specs=in_specs,
        out_specs=o_spec,
        out_shape=jax.ShapeDtypeStruct((m, n), out_dtype),
        scratch_shapes=[pltpu.VMEM((tm, tn), F32)] if nk > 1 else [],
        compiler_params=_params(("parallel", "parallel", "arbitrary")),
    )(*args)


def _rms_fwd(x, g, name, after, ts=256):
    r, d = x.shape

    def body(x_ref, g_ref, _, o_ref):
        xv = x_ref[...]
        inv = lax.rsqrt(jnp.mean(xv * xv, axis=-1, keepdims=True) + EPS)
        o_ref[...] = (xv * inv * g_ref[...]).astype(BF16)

    return pl.pallas_call(
        body,
        name=name,
        grid=(r // ts,),
        in_specs=[pl.BlockSpec((ts, d), lambda i: (i, 0)), pl.BlockSpec((1, d), lambda i: (0, 0)),
                  pl.BlockSpec(memory_space=pl.ANY)],
        out_specs=pl.BlockSpec((ts, d), lambda i: (i, 0)),
        out_shape=jax.ShapeDtypeStruct((r, d), BF16),
        compiler_params=_params(("parallel",)),
    )(x, g, after)


def _post_loss(out, x, tgt, g_post, ts=256):
    n = S // ts

    def body(o_ref, x_ref, t_ref, g_ref, sq_ref, dy_ref, do_ref, dg_ref):
        i = pl.program_id(0)

        @pl.when(i == 0)
        def _():
            sq_ref[...] = jnp.zeros_like(sq_ref)
            dg_ref[...] = jnp.zeros_like(dg_ref)

        ov = o_ref[...]
        g = g_ref[...]
        inv = lax.rsqrt(jnp.mean(ov * ov, axis=-1, keepdims=True) + EPS)
        on = ov * inv
        err = x_ref[...] + on * g - t_ref[...]
        sq_ref[...] += jnp.sum(err * err)
        dy = err * (1.0 / D)
        dy_ref[...] = dy
        dg_ref[...] += jnp.sum(dy * on, axis=0, keepdims=True)
        don = dy * g
        do_ref[...] = (inv * (don - on * jnp.mean(don * on, axis=-1, keepdims=True))).astype(BF16)

    row = pl.BlockSpec((ts, D), lambda i: (i, 0))
    vec = pl.BlockSpec((1, D), lambda i: (0, 0))
    return pl.pallas_call(
        body,
        name="post_loss",
        grid=(n,),
        in_specs=[row, row, row, vec],
        out_specs=[pl.BlockSpec((8, LANE), lambda i: (0, 0)), row, row, vec],
        out_shape=[
            jax.ShapeDtypeStruct((8, LANE), F32),
            jax.ShapeDtypeStruct((S, D), F32),
            jax.ShapeDtypeStruct((S, D), BF16),
            jax.ShapeDtypeStruct((1, D), F32),
        ],
        compiler_params=_params(("arbitrary",)),
    )(out, x, tgt, g_post)


def _pre_bwd(dh, x, dy, g_pre, ts=256):
    n = S // ts

    def body(dh_ref, x_ref, dy_ref, g_ref, gx_ref, dg_ref):
        i = pl.program_id(0)

        @pl.when(i == 0)
        def _():
            dg_ref[...] = jnp.zeros_like(dg_ref)

        xv = x_ref[...]
        dhv = dh_ref[...]
        inv = lax.rsqrt(jnp.mean(xv * xv, axis=-1, keepdims=True) + EPS)
        xn = xv * inv
        dg_ref[...] += jnp.sum(dhv * xn, axis=0, keepdims=True)
        dxn = dhv * g_ref[...]
        gx_ref[...] = dy_ref[...] + inv * (dxn - xn * jnp.mean(dxn * xn, axis=-1, keepdims=True))

    row = pl.BlockSpec((ts, D), lambda i: (i, 0))
    vec = pl.BlockSpec((1, D), lambda i: (0, 0))
    return pl.pallas_call(
        body,
        name="pre_bwd",
        grid=(n,),
        in_specs=[row, row, row, vec],
        out_specs=[row, vec],
        out_shape=[jax.ShapeDtypeStruct((S, D), F32), jax.ShapeDtypeStruct((1, D), F32)],
        compiler_params=_params(("arbitrary",)),
    )(dh, x, dy, g_pre)


def _memnorm_bwd(dmemn, mem):
    def body(d_ref, m_ref, dg_ref):
        mv = m_ref[...]
        inv = lax.rsqrt(jnp.mean(mv * mv, axis=-1, keepdims=True) + EPS)
        dg_ref[...] = jnp.sum(d_ref[...] * mv * inv, axis=0, keepdims=True)

    return pl.pallas_call(
        body,
        name="memnorm_bwd",
        out_shape=jax.ShapeDtypeStruct((1, D), F32),
        compiler_params=_params(),
    )(dmemn, mem)


T_RNN = 256


def _neg_expm1(z):
    poly = -z * (1.0 + z * (0.5 + z * (1.0 / 6 + z * (1.0 / 24 + z * (1.0 / 120 + z * (1.0 / 720))))))
    return jnp.where(z > -0.1, poly, 1.0 - jnp.exp(z))


def _softplus_neg(lam):
    return jnp.maximum(-lam, 0.0) + jnp.log1p(jnp.exp(-jnp.abs(lam)))


def _rnn_gates(conv, wa_ref, ba, wx_ref, bx, lam, first_row):
    cbf = conv.astype(BF16)
    ga, gx = [], []
    for n in range(RNN_BLOCKS):
        c_n = cbf[:, n * LANE:(n + 1) * LANE]
        ga.append(jnp.dot(c_n, wa_ref[n], preferred_element_type=F32))
        gx.append(jnp.dot(c_n, wx_ref[n], preferred_element_type=F32))
    gate_r = _sigmoid(jnp.concatenate(ga, axis=1) + ba)
    gate_i = _sigmoid(jnp.concatenate(gx, axis=1) + bx)
    sp = _softplus_neg(lam)
    log_a = -LRU_C * gate_r * sp
    a = jnp.exp(log_a)
    mult_raw = jnp.sqrt(_neg_expm1(2.0 * log_a))
    mult = jnp.where(first_row, 1.0, mult_raw)
    return cbf, gate_r, gate_i, sp, a, mult_raw, mult


def _rglru_fwd(p_a, conv_w, conv_b, wa, ba, wx, bx, lam):
    t = T_RNN
    n = S // t

    def body(xr_ref, g_ref, cw_ref, cb_ref, wa_ref, ba_ref, wx_ref, bx_ref, lam_ref,
             y_ref, h_ref, xp_s, hcar, a_s, b_s):
        i = pl.program_id(0)

        @pl.when(i == 0)
        def _():
            xp_s[0:8, :] = jnp.zeros((8, D_RNN), F32)
            hcar[...] = jnp.zeros_like(hcar)

        @pl.when(i > 0)
        def _():
            xp_s[0:8, :] = xp_s[t:t + 8, :]

        xp_s[8:8 + t, :] = xr_ref[...]
        conv = cb_ref[...]
        for k in range(CONV_W):
            conv = conv + cw_ref[k:k + 1, :] * xp_s[8 - k:8 - k + t, :]
        rows = i * t + lax.broadcasted_iota(jnp.int32, (t, 1), 0)
        _, _, gate_i, _, a, _, mult = _rnn_gates(
            conv, wa_ref, ba_ref[...], wx_ref, bx_ref[...], lam_ref[...], rows == 0)
        a_s[...] = a
        b_s[...] = mult * gate_i * conv

        def step(tt, h):
            h = a_s[pl.ds(tt, 1), :] * h + b_s[pl.ds(tt, 1), :]
            h_ref[pl.ds(tt, 1), :] = h
            return h

        hcar[...] = lax.fori_loop(0, t, step, hcar[...], unroll=8)
        g = g_ref[...]
        y_ref[...] = (h_ref[...] * (g * _sigmoid(g))).astype(BF16)

    blk = lambda c: pl.BlockSpec((t, D_RNN), lambda i: (i, c))
    full = lambda shape: pl.BlockSpec(shape, lambda i: (0,) * len(shape))
    return pl.pallas_call(
        body,
        name="rglru_fwd",
        grid=(n,),
        in_specs=[blk(0), blk(1), full((CONV_W, D_RNN)), full((1, D_RNN)),
                  full((RNN_BLOCKS, LANE, LANE)), full((1, D_RNN)),
                  full((RNN_BLOCKS, LANE, LANE)), full((1, D_RNN)), full((1, D_RNN))],
        out_specs=[blk(0), blk(0)],
        out_shape=[jax.ShapeDtypeStruct((S, D_RNN), BF16), jax.ShapeDtypeStruct((S, D_RNN), F32)],
        scratch_shapes=[pltpu.VMEM((t + 8, D_RNN), F32), pltpu.VMEM((1, D_RNN), F32),
                        pltpu.VMEM((t, D_RNN), F32), pltpu.VMEM((t, D_RNN), F32)],
        compiler_params=_params(("arbitrary",)),
    )(p_a, p_a, conv_w, conv_b, wa, ba, wx, bx, lam)


def _rglru_bwd(dy, p_a, hseq, conv_w, conv_b, wa, ba, wx, bx, lam):
    t = T_RNN
    n = S // t
    rb = t // 8

    def body(dy_ref, xr_ref, g_ref, h_ref, xrp_ref, hp_ref, cw_ref, cb_ref, wa_ref, ba_ref, wx_ref, bx_ref, lam_ref,
             dp_ref, dcw_ref, dcb_ref, dwa_ref, dba_ref, dwx_ref, dbx_ref, dlam_ref,
             xp_s, hp_s, dxp_s, lamcar, a_s, dh_s, lam_s):
        i = pl.program_id(0)
        r = n - 1 - i

        @pl.when(i == 0)
        def _():
            for ref in (dcw_ref, dcb_ref, dwa_ref, dba_ref, dwx_ref, dbx_ref, dlam_ref, lamcar):
                ref[...] = jnp.zeros_like(ref)
            dxp_s[t:t + 8, :] = jnp.zeros((8, D_RNN), F32)

        @pl.when(i > 0)
        def _():
            dxp_s[t:t + 8, :] = dxp_s[0:8, :]

        has_prev = r > 0
        xp_s[0:8, :] = jnp.where(has_prev, xrp_ref[...], 0.0)
        xp_s[8:8 + t, :] = xr_ref[...]
        hp_s[0:8, :] = jnp.where(has_prev, hp_ref[...], 0.0)
        hp_s[8:8 + t, :] = h_ref[...]
        xs = [xp_s[8 - k:8 - k + t, :] for k in range(CONV_W)]
        conv = cb_ref[...]
        for k in range(CONV_W):
            conv = conv + cw_ref[k:k + 1, :] * xs[k]
        rows = r * t + lax.broadcasted_iota(jnp.int32, (t, 1), 0)
        first = rows == 0
        lam_p = lam_ref[...]
        cbf, gate_r, gate_i, sp, a, mult_raw, mult = _rnn_gates(
            conv, wa_ref, ba_ref[...], wx_ref, bx_ref[...], lam_p, first)

        g = g_ref[...]
        sg = _sigmoid(g)
        dyv = dy_ref[...]
        a_s[...] = a
        dh_s[...] = dyv * (g * sg)
        dg = dyv * h_ref[...] * (sg * (1.0 + g * (1.0 - sg)))

        def step(jj, car):
            tt = t - 1 - jj
            lm = dh_s[pl.ds(tt, 1), :] + car
            lam_s[pl.ds(tt, 1), :] = lm
            return a_s[pl.ds(tt, 1), :] * lm

        lamcar[...] = lax.fori_loop(0, t, step, lamcar[...], unroll=8)
        db = lam_s[...]
        da = db * hp_s[7:7 + t, :]
        dmult = db * gate_i * conv
        dgate_i = db * mult * conv
        dconv = db * mult * gate_i
        dlog_a = da * a + jnp.where(first, 0.0, dmult * (-(a * a) / mult_raw))
        dgate_r = dlog_a * (-LRU_C * sp)
        dsp = jnp.sum(dlog_a * (-LRU_C * gate_r), axis=0, keepdims=True)
        dlam_ref[...] += dsp * (-_sigmoid(-lam_p))
        dga = dgate_r * gate_r * (1.0 - gate_r)
        dgx = dgate_i * gate_i * (1.0 - gate_i)
        dba_ref[...] += jnp.sum(dga, axis=0, keepdims=True)
        dbx_ref[...] += jnp.sum(dgx, axis=0, keepdims=True)
        dga16, dgx16 = dga.astype(BF16), dgx.astype(BF16)
        back = []
        for nb in range(RNN_BLOCKS):
            sl = slice(nb * LANE, (nb + 1) * LANE)
            dwa_ref[nb] += lax.dot_general(cbf[:, sl], dga16[:, sl], _DIMS["tn"], preferred_element_type=F32)
            dwx_ref[nb] += lax.dot_general(cbf[:, sl], dgx16[:, sl], _DIMS["tn"], preferred_element_type=F32)
            back.append(lax.dot_general(dga16[:, sl], wa_ref[nb], _DIMS["nt"], preferred_element_type=F32)
                        + lax.dot_general(dgx16[:, sl], wx_ref[nb], _DIMS["nt"], preferred_element_type=F32))
        dconv = dconv + jnp.concatenate(back, axis=1)
        dcb_ref[...] += jnp.sum(dconv, axis=0, keepdims=True)
        for k in range(CONV_W):
            dcw_ref[k:k + 1, :] += jnp.sum(dconv * xs[k], axis=0, keepdims=True)
        dxp_s[0:t, :] = dconv
        dxr = cw_ref[0:1, :] * dconv
        for k in range(1, CONV_W):
            dxr = dxr + cw_ref[k:k + 1, :] * dxp_s[k:k + t, :]
        dp_ref[:, 0:D_RNN] = dxr.astype(BF16)
        dp_ref[:, D_RNN:2 * D_RNN] = dg.astype(BF16)

    blk = lambda c: pl.BlockSpec((t, D_RNN), lambda i: (n - 1 - i, c))
    prev8 = pl.BlockSpec((8, D_RNN), lambda i: (jnp.maximum((n - 1 - i) * rb - 1, 0), 0))
    full = lambda shape: pl.BlockSpec(shape, lambda i: (0,) * len(shape))
    vec = full((1, D_RNN))
    mat = full((RNN_BLOCKS, LANE, LANE))
    return pl.pallas_call(
        body,
        name="rglru_bwd",
        grid=(n,),
        in_specs=[blk(0), blk(0), blk(1), blk(0), prev8, prev8,
                  full((CONV_W, D_RNN)), vec, mat, vec, mat, vec, vec],
        out_specs=[pl.BlockSpec((t, 2 * D_RNN), lambda i: (n - 1 - i, 0)),
                   full((CONV_W, D_RNN)), vec, mat, vec, mat, vec, vec],
        out_shape=[jax.ShapeDtypeStruct((S, 2 * D_RNN), BF16),
                   jax.ShapeDtypeStruct((CONV_W, D_RNN), F32), jax.ShapeDtypeStruct((1, D_RNN), F32),
                   jax.ShapeDtypeStruct((RNN_BLOCKS, LANE, LANE), F32), jax.ShapeDtypeStruct((1, D_RNN), F32),
                   jax.ShapeDtypeStruct((RNN_BLOCKS, LANE, LANE), F32), jax.ShapeDtypeStruct((1, D_RNN), F32),
                   jax.ShapeDtypeStruct((1, D_RNN), F32)],
        scratch_shapes=[pltpu.VMEM((t + 8, D_RNN), F32), pltpu.VMEM((t + 8, D_RNN), F32),
                        pltpu.VMEM((t + 8, D_RNN), F32), pltpu.VMEM((1, D_RNN), F32),
                        pltpu.VMEM((t, D_RNN), F32), pltpu.VMEM((t, D_RNN), F32), pltpu.VMEM((t, D_RNN), F32)],
        compiler_params=_params(("arbitrary",)),
    )(dy, p_a, p_a, hseq, p_a, hseq, conv_w, conv_b, wa, ba, wx, bx, lam)


QB = WINDOW
KB2 = 2 * WINDOW
N_QB = S // QB
N_PAIR = SWA_HEADS // 2


def _swa_keys(kvc_ref, kvp_ref):
    kk = jnp.concatenate([kvp_ref[:, 0:LANE], kvc_ref[:, 0:LANE]], axis=0)
    vv = jnp.concatenate([kvp_ref[:, LANE:2 * LANE], kvc_ref[:, LANE:2 * LANE]], axis=0)
    lo = lax.broadcasted_iota(jnp.int32, (1, LANE), 1) < SWA_HD
    kk_sw, vv_sw = pltpu.roll(kk, SWA_HD, 1), pltpu.roll(vv, SWA_HD, 1)
    kd = [jnp.where(lo, kk, kk_sw).astype(BF16), jnp.where(lo, kk_sw, kk).astype(BF16)]
    vd = [jnp.where(lo, vv, vv_sw).astype(BF16), jnp.where(lo, vv_sw, vv).astype(BF16)]
    return lo, kd, vd


GRP = SWA_HEADS // 2
STACK = GRP
GQ = STACK * QB


def _swa_valid(n, rows):
    qi = lax.broadcasted_iota(jnp.int32, (rows, KB2), 0) % QB
    kj = lax.broadcasted_iota(jnp.int32, (rows, KB2), 1)
    dist = qi + WINDOW - kj
    return (dist >= 0) & (dist < WINDOW) & ((n > 0) | (kj >= WINDOW))


def _swa_stack(tile_of, lo, h0, masked):
    parts = []
    for h in range(h0, h0 + STACK):
        t = tile_of(h // 2)
        if masked:
            t = jnp.where(lo if h % 2 == 0 else jnp.logical_not(lo), t, 0.0)
        parts.append(t)
    return jnp.concatenate(parts, axis=0)


def _swa_unstack(stacked, lo, pair):
    return jnp.where(lo, stacked[2 * pair * QB:(2 * pair + 1) * QB], stacked[(2 * pair + 1) * QB:(2 * pair + 2) * QB])


def _swa_softmax(lg, sink, valid):
    lg = jnp.where(valid, lg, NEG_INF)
    m = jnp.maximum(jnp.max(lg, axis=-1, keepdims=True), sink)
    p = jnp.exp(lg - m)
    es = jnp.exp(sink - m)
    den = jnp.sum(p, axis=-1, keepdims=True) + es
    return p / den, es / den


def _swa_probs_head(qh16, kd, bias, sink, valid):
    lg = lax.dot_general(qh16, kd, _DIMS["nt"], preferred_element_type=F32) * (SWA_HD ** -0.5) + bias
    return _swa_softmax(lg, sink, valid)[0]


def _swa_probs(q16, kd, bias_ref, sink_ref, h0, valid):
    bias = bias_ref[h0:h0 + STACK].reshape(GQ, KB2)
    sink = jnp.concatenate([jnp.full((QB, 1), sink_ref[h], F32) for h in range(h0, h0 + STACK)], axis=0)
    lg = lax.dot_general(q16, kd, _DIMS["nt"], preferred_element_type=F32) * (SWA_HD ** -0.5) + bias
    return _swa_softmax(lg, sink, valid)


def _swa_specs():
    q = pl.BlockSpec((QB, D_RNN), lambda n: (n, 0))
    g = pl.BlockSpec((QB, D_RNN), lambda n: (n, 1))
    kvc = pl.BlockSpec((QB, 2 * LANE), lambda n: (n, 8))
    kvp = pl.BlockSpec((QB, 2 * LANE), lambda n: (jnp.maximum(n - 1, 0), 8))
    bias = pl.BlockSpec((SWA_HEADS, QB, KB2), lambda n: (0, 0, 0))
    sinks = pl.BlockSpec(memory_space=pltpu.SMEM)
    return q, g, kvc, kvp, bias, sinks


def _swa_fwd(p_b, bias_t, sinks):
    def body(q_ref, g_ref, kvc_ref, kvp_ref, bias_ref, sink_ref, y_ref, o_ref):
        n = pl.program_id(0)
        lo, kd, vd = _swa_keys(kvc_ref, kvp_ref)
        valid = _swa_valid(n, QB)
        for hp in range(N_PAIR):
            sl = slice(hp * LANE, (hp + 1) * LANE)
            kvh = hp // (N_PAIR // 2)
            q = q_ref[:, sl]
            outs = []
            for j in range(2):
                qh16 = jnp.where(lo if j == 0 else jnp.logical_not(lo), q, 0.0).astype(BF16)
                probs = _swa_probs_head(qh16, kd[kvh], bias_ref[2 * hp + j], sink_ref[2 * hp + j], valid)
                outs.append(jnp.dot(probs.astype(BF16), vd[kvh], preferred_element_type=F32))
            o = jnp.where(lo, outs[0], outs[1])
            o_ref[:, sl] = o
            g = g_ref[:, sl]
            y_ref[:, sl] = (o * (g * _sigmoid(g))).astype(BF16)

    q, g, kvc, kvp, bias, sinks_spec = _swa_specs()
    out = pl.BlockSpec((QB, D_RNN), lambda n: (n, 0))
    return pl.pallas_call(
        body,
        name="swa_fwd",
        grid=(N_QB,),
        in_specs=[q, g, kvc, kvp, bias, sinks_spec],
        out_specs=[out, out],
        out_shape=[jax.ShapeDtypeStruct((S, D_RNN), BF16), jax.ShapeDtypeStruct((S, D_RNN), F32)],
        compiler_params=_params(("parallel",)),
    )(p_b, p_b, p_b, p_b, bias_t, sinks)


def _swa_bwd(dy, p_b, o_swa, bias_t, sinks, after=None):
    def body(dy_ref, q_ref, g_ref, kvc_ref, kvp_ref, o_ref, bias_ref, sink_ref, *rest):
        dp_ref, dk_ref, dv_ref, dbias_ref, dsink_ref, do_s = rest[-6:]
        n = pl.program_id(0)

        @pl.when(n == 0)
        def _():
            for ref in (dk_ref, dv_ref, dbias_ref, dsink_ref):
                ref[...] = jnp.zeros_like(ref)

        lo, kd, vd = _swa_keys(kvc_ref, kvp_ref)
        hi = jnp.logical_not(lo)
        valid = _swa_valid(n, GQ)
        tile = lambda ref: (lambda hp: ref[:, hp * LANE:(hp + 1) * LANE])
        for hp in range(N_PAIR):
            sl = slice(hp * LANE, (hp + 1) * LANE)
            g, dyv = g_ref[:, sl], dy_ref[:, sl]
            sg = _sigmoid(g)
            do_s[:, sl] = dyv * (g * sg)
            dp_ref[:, D_RNN + hp * LANE:D_RNN + (hp + 1) * LANE] = (
                dyv * o_ref[:, sl] * (sg * (1.0 + g * (1.0 - sg)))).astype(BF16)

        dk_blk = jnp.zeros((KB2, LANE), F32)
        dv_blk = jnp.zeros((KB2, LANE), F32)
        for h0 in range(0, SWA_HEADS, STACK):
            kvh = h0 // GRP
            q16 = _swa_stack(tile(q_ref), lo, h0, masked=True).astype(BF16)
            do8 = _swa_stack(tile(do_s), lo, h0, masked=True)
            do16 = do8.astype(BF16)
            delta = jnp.sum(do8 * _swa_stack(tile(o_ref), lo, h0, masked=False), axis=-1, keepdims=True)
            probs, psink = _swa_probs(q16, kd[kvh], bias_ref, sink_ref, h0, valid)
            dpr = lax.dot_general(do16, vd[kvh], _DIMS["nt"], preferred_element_type=F32)
            ds = probs * (dpr - delta)
            sink_term = psink * delta
            for g in range(STACK):
                h, rows = h0 + g, slice(g * QB, (g + 1) * QB)
                dbias_ref[h] += ds[rows]
                dsink_ref[h:h + 1, :] += jnp.zeros((1, LANE), F32) - jnp.sum(sink_term[rows])
            ds16 = (ds * (SWA_HD ** -0.5)).astype(BF16)
            dq_all = jnp.dot(ds16, kd[kvh], preferred_element_type=F32)
            for pair in range(STACK // 2):
                sl = slice((h0 // 2 + pair) * LANE, (h0 // 2 + pair + 1) * LANE)
                dp_ref[:, sl] = _swa_unstack(dq_all, lo, pair).astype(BF16)
            dk_pair = lax.dot_general(ds16, q16, _DIMS["tn"], preferred_element_type=F32)
            dv_pair = lax.dot_general(probs.astype(BF16), do16, _DIMS["tn"], preferred_element_type=F32)
            keep = lo if kvh == 0 else hi
            dk_blk = dk_blk + jnp.where(keep, dk_pair + pltpu.roll(dk_pair, SWA_HD, 1), 0.0)
            dv_blk = dv_blk + jnp.where(keep, dv_pair + pltpu.roll(dv_pair, SWA_HD, 1), 0.0)

        cur = pl.ds(pl.multiple_of(n * QB, QB), QB)
        dk_ref[cur, :] += dk_blk[QB:KB2]
        dv_ref[cur, :] += dv_blk[QB:KB2]

        @pl.when(n > 0)
        def _():
            prev = pl.ds(pl.multiple_of((n - 1) * QB, QB), QB)
            dk_ref[prev, :] += dk_blk[0:QB]
            dv_ref[prev, :] += dv_blk[0:QB]

    q, g, kvc, kvp, bias, sinks_spec = _swa_specs()
    row = pl.BlockSpec((QB, D_RNN), lambda n: (n, 0))
    acc = pl.BlockSpec((S, LANE), lambda n: (0, 0))
    return pl.pallas_call(
        body,
        name="swa_bwd",
        grid=(N_QB,),
        in_specs=[row, q, g, kvc, kvp, row, bias, sinks_spec] + ([ANY] if after is not None else []),
        out_specs=[pl.BlockSpec((QB, 2 * D_RNN), lambda n: (n, 0)), acc, acc, bias,
                   pl.BlockSpec((SWA_HEADS, LANE), lambda n: (0, 0))],
        out_shape=[jax.ShapeDtypeStruct((S, GROUP_TILES["B"] * LANE), BF16),
                   jax.ShapeDtypeStruct((S, LANE), F32), jax.ShapeDtypeStruct((S, LANE), F32),
                   jax.ShapeDtypeStruct((SWA_HEADS, QB, KB2), F32),
                   jax.ShapeDtypeStruct((SWA_HEADS, LANE), F32)],
        scratch_shapes=[pltpu.VMEM((QB, D_RNN), F32)],
        compiler_params=_params(("arbitrary",)),
    )(dy, p_b, p_b, p_b, p_b, o_swa, bias_t, sinks, *([after] if after is not None else []))


def _swa_pack(dp_b, dk, dv, ts=512):
    def body(_, dk_ref, dv_ref, o_ref):
        o_ref[:, 0:LANE] = dk_ref[...].astype(BF16)
        o_ref[:, LANE:2 * LANE] = dv_ref[...].astype(BF16)

    tile = pl.BlockSpec((ts, LANE), lambda i: (i, 0))
    return pl.pallas_call(
        body,
        name="swa_pack",
        grid=(S // ts,),
        in_specs=[pl.BlockSpec(memory_space=pl.ANY), tile, tile],
        out_specs=pl.BlockSpec((ts, 2 * LANE), lambda i: (i, 8)),
        out_shape=jax.ShapeDtypeStruct(dp_b.shape, dp_b.dtype),
        input_output_aliases={0: 0},
        compiler_params=_params(("parallel",)),
    )(dp_b, dk, dv)


def _split3(v):
    a = v.astype(BF16)
    r = v - a.astype(F32)
    b = r.astype(BF16)
    c = (r - b.astype(F32)).astype(BF16)
    return a, b, c


def _relbias_grad(dbias_flat, onehot_t):
    def body(d_ref, e_ref, o_ref):
        e = e_ref[...]
        acc = jnp.zeros((SWA_HEADS, REL_BUCKETS), F32)
        for term in _split3(d_ref[...]):
            acc = acc + lax.dot_general(term, e, _DIMS["nt"], preferred_element_type=F32)
        o_ref[...] = acc

    return pl.pallas_call(
        body,
        name="relbias_grad",
        out_shape=jax.ShapeDtypeStruct((SWA_HEADS, REL_BUCKETS), F32),
        compiler_params=_params(),
    )(dbias_flat, onehot_t)


TS_MEM = 512


def _mem_probs(q16, mk):
    lg = lax.dot_general(q16, mk, _DIMS["nt"], preferred_element_type=F32) * (MEM_HD ** -0.5)
    p = jnp.exp(lg - jnp.max(lg, axis=-1, keepdims=True))
    return p / jnp.sum(p, axis=-1, keepdims=True)


def _mem_fwd(p_c, mkv):
    def body(q_ref, g_ref, mkv_ref, y_ref, o_ref):
        for hm in range(MEM_HEADS):
            sl = slice(hm * MEM_HD, (hm + 1) * MEM_HD)
            probs = _mem_probs(q_ref[:, sl].astype(BF16), mkv_ref[:, sl])
            o = jnp.dot(probs.astype(BF16), mkv_ref[:, D_RNN + hm * MEM_HD:D_RNN + (hm + 1) * MEM_HD],
                        preferred_element_type=F32)
            o_ref[:, sl] = o
            g = g_ref[:, sl]
            y_ref[:, sl] = (o * (g * _sigmoid(g))).astype(BF16)

    blk = lambda c: pl.BlockSpec((TS_MEM, D_RNN), lambda i: (i, c))
    return pl.pallas_call(
        body,
        name="mem_fwd",
        grid=(S // TS_MEM,),
        in_specs=[blk(0), blk(1), pl.BlockSpec((MEM, 2 * D_RNN), lambda i: (0, 0))],
        out_specs=[blk(0), blk(0)],
        out_shape=[jax.ShapeDtypeStruct((S, D_RNN), BF16), jax.ShapeDtypeStruct((S, D_RNN), F32)],
        compiler_params=_params(("parallel",)),
    )(p_c, p_c, mkv)


def _mem_bwd(dy, p_c, o_mem, mkv):
    def body(dy_ref, q_ref, g_ref, o_ref, mkv_ref, dp_ref, dmkv_ref):
        @pl.when(pl.program_id(0) == 0)
        def _():
            dmkv_ref[...] = jnp.zeros_like(dmkv_ref)

        for hm in range(MEM_HEADS):
            sl = slice(hm * MEM_HD, (hm + 1) * MEM_HD)
            sv = slice(D_RNN + hm * MEM_HD, D_RNN + (hm + 1) * MEM_HD)
            q16 = q_ref[:, sl].astype(BF16)
            mk, mv = mkv_ref[:, sl], mkv_ref[:, sv]
            probs = _mem_probs(q16, mk)
            g, o, dyv = g_ref[:, sl], o_ref[:, sl], dy_ref[:, sl]
            sg = _sigmoid(g)
            do = dyv * (g * sg)
            dp_ref[:, sv] = (dyv * o * (sg * (1.0 + g * (1.0 - sg)))).astype(BF16)
            do16 = do.astype(BF16)
            delta = jnp.sum(do * o, axis=-1, keepdims=True)
            dpr = lax.dot_general(do16, mv, _DIMS["nt"], preferred_element_type=F32)
            ds16 = (probs * (dpr - delta) * (MEM_HD ** -0.5)).astype(BF16)
            dp_ref[:, sl] = jnp.dot(ds16, mk, preferred_element_type=F32).astype(BF16)
            dmkv_ref[:, sl] += lax.dot_general(ds16, q16, _DIMS["tn"], preferred_element_type=F32)
            dmkv_ref[:, sv] += lax.dot_general(probs.astype(BF16), do16, _DIMS["tn"], preferred_element_type=F32)

    blk = lambda c: pl.BlockSpec((TS_MEM, D_RNN), lambda i: (i, c))
    kv = pl.BlockSpec((MEM, 2 * D_RNN), lambda i: (0, 0))
    return pl.pallas_call(
        body,
        name="mem_bwd",
        grid=(S // TS_MEM,),
        in_specs=[blk(0), blk(0), blk(1), blk(0), kv],
        out_specs=[pl.BlockSpec((TS_MEM, 2 * D_RNN), lambda i: (i, 0)), kv],
        out_shape=[jax.ShapeDtypeStruct((S, 2 * D_RNN), BF16), jax.ShapeDtypeStruct((MEM, 2 * D_RNN), F32)],
        compiler_params=_params(("arbitrary",)),
    )(dy, p_c, p_c, o_mem, mkv)


TS_MRG = 512
TD_MRG = 1024
N_DBLK = D // TD_MRG


def _merge_fwd(z, p_d):
    def body(z0, z1, z2, g0, g1, g2, o_ref):
        o_ref[...] = (_sigmoid(g0[...]) * z0[...] + _sigmoid(g1[...]) * z1[...]
                      + _sigmoid(g2[...]) * z2[...]).astype(BF16)

    blk = pl.BlockSpec((TS_MRG, TD_MRG), lambda i, d: (i, d))
    gate = lambda b: pl.BlockSpec((TS_MRG, TD_MRG), lambda i, d: (i, b * N_DBLK + d))
    return pl.pallas_call(
        body,
        name="merge_fwd",
        grid=(S // TS_MRG, N_DBLK),
        in_specs=[blk, blk, blk, gate(0), gate(1), gate(2)],
        out_specs=blk,
        out_shape=jax.ShapeDtypeStruct((S, D), BF16),
        compiler_params=_params(("parallel", "parallel")),
    )(z[0], z[1], z[2], p_d, p_d, p_d)


def _merge_bwd(dmerged, z_b, p_d, b, dp_d, after=None):
    def body(dm_ref, z_ref, g_ref, *refs):
        dz_ref, dg_ref = refs[-2], refs[-1]
        sg = _sigmoid(g_ref[...])
        dm = dm_ref[...]
        dz_ref[...] = (dm * sg).astype(BF16)
        dg_ref[...] = (dm * z_ref[...] * sg * (1.0 - sg)).astype(BF16)

    blk = pl.BlockSpec((TS_MRG, TD_MRG), lambda i, d: (i, d))
    gate = pl.BlockSpec((TS_MRG, TD_MRG), lambda i, d: (i, b * N_DBLK + d))
    in_specs = [blk, blk, gate]
    args = [dmerged, z_b, p_d]
    aliases = {}
    if dp_d is not None:
        in_specs.append(pl.BlockSpec(memory_space=pl.ANY))
        args.append(dp_d)
        aliases = {3: 1}
    if after is not None:
        in_specs.append(pl.BlockSpec(memory_space=pl.ANY))
        args.append(after)
    return pl.pallas_call(
        body,
        name=f"merge_bwd{b}",
        grid=(S // TS_MRG, N_DBLK),
        in_specs=in_specs,
        out_specs=[blk, gate],
        out_shape=[jax.ShapeDtypeStruct((S, D), BF16),
                   jax.ShapeDtypeStruct((S, GROUP_TILES["D"] * LANE), BF16)],
        input_output_aliases=aliases,
        compiler_params=_params(("parallel", "parallel")),
    )(*args)


def _bucket_table():
    import numpy as np
    qi = np.arange(QB)[:, None]
    kj = np.arange(KB2)[None, :]
    n = np.maximum(qi + WINDOW - kj, 0)
    max_exact = REL_BUCKETS // 2
    ratio = np.log(np.maximum(n, 1).astype(np.float32) / max_exact) / np.float32(math.log(REL_MAX_DIST / max_exact))
    large = np.minimum(max_exact + (ratio * (REL_BUCKETS - max_exact)).astype(np.int32), REL_BUCKETS - 1)
    bucket = np.where(n < max_exact, n, large).reshape(1, QB * KB2)
    return (bucket == np.arange(REL_BUCKETS)[:, None]).astype(np.float32)


def _bias_expand(rel_bias_t, onehot_t):
    def body(r_ref, e_ref, o_ref):
        e = e_ref[...]
        acc = jnp.zeros((SWA_HEADS, QB * KB2), F32)
        for term in _split3(r_ref[...]):
            acc = acc + jnp.dot(term, e, preferred_element_type=F32)
        o_ref[...] = acc

    return pl.pallas_call(
        body,
        name="bias_expand",
        out_shape=jax.ShapeDtypeStruct((SWA_HEADS, QB * KB2), F32),
        compiler_params=_params(),
    )(rel_bias_t, onehot_t)


PROJ_TN = {"A": 1024, "B": 1152, "C": 1024, "D": 1536}


def _local_step(x, h, mem, tgt, sp, fetch, prefetch, emit, advance):
    onehot_t = jnp.asarray(_bucket_table(), BF16)
    bias_t = _bias_expand(sp["rel_bias"].T, onehot_t).reshape(SWA_HEADS, QB, KB2)
    sinks = sp["swa_sinks"].reshape(SWA_HEADS)
    wa16, wx16 = sp["w_rg_a"].astype(BF16), sp["w_rg_x"].astype(BF16)
    rnn = (sp["conv_w"], sp["conv_b"], wa16, sp["b_rg_a"], wx16, sp["b_rg_x"], sp["lru_lambda"])

    memn = _rms_fwd(mem, sp["mem_norm_g"], "rms_mem", h)
    w_grp, p = {}, {}

    def project(g, after, then=None):
        (w_grp[g],) = fetch((g,), after)
        tok = prefetch(then, w_grp[g]) if then is not None else None
        p[g] = _mm(h, w_grp[g], "nt", F32, 1024, PROJ_TN[g], D, f"proj_{g}", after=tok)

    project("A", h)
    y_rg, hseq = _rglru_fwd(p["A"], *rnn)
    project("B", y_rg)
    y_swa, o_swa = _swa_fwd(p["B"], bias_t, sinks)
    project("C", y_swa, then=("mk",))
    (wmk,) = fetch(("mk",), p["C"])
    tok = prefetch(("br0", "br1", "br2"), wmk)
    mkv = _mm(memn, wmk, "nn", BF16, MEM, 1024, D, "mkv", after=tok)
    y_mem, o_mem = _mem_fwd(p["C"], mkv)
    ys = (y_rg, y_swa, y_mem)
    wbr = fetch(("br0", "br1", "br2"), y_mem)
    tok = prefetch(("D",), wbr[2])
    z = []
    for b in range(3):
        z.append(_mm(ys[b], wbr[b], "nn", F32, 1024, 1024, D_RNN, f"branch_out{b}", after=z[-1] if z else tok))
    project("D", z[2], then=("out",))
    merged = _merge_fwd(z, p["D"])
    (wout,) = fetch(("out",), merged)
    out = _mm(merged, wout, "nn", F32, 1024, 1024, D, "out_proj")
    sq, dy, dout, d_post = _post_loss(out, x, tgt, sp["post_norm_g"])

    tok = emit({"out": _mm(merged, dout, "tn", BF16, 1024, 1024, S, "d_wout")})
    dmerged = _mm(dout, wout, "nt", F32, 1024, 1024, D, "d_merged", after=tok)
    dz, dp_d = [], None
    tok = advance(dmerged)
    for b in range(3):
        dz_b, dp_d = _merge_bwd(dmerged, z[b], p["D"], b, dp_d, after=tok if b == 0 else None)
        dz.append(dz_b)
    d_win = lambda g, dp_g, after=None: _mm(dp_g, h, "tn", BF16, PROJ_TN[g], 1024, S, f"d_win_{g}", after=after)
    tok = emit({f"br{b}": _mm(ys[b], dz[b], "tn", BF16, 1024, 1024, S, f"d_wbr{b}") for b in range(3)}, tok)
    d_w_d = d_win("D", dp_d, tok)
    tok = emit({"D": d_w_d}, advance(d_w_d))
    dy_mem = _mm(dz[2], wbr[2], "nt", F32, 1024, 1024, D, "d_branch2", after=tok)
    tok = advance(dy_mem)
    dp_c, dmkv = _mem_bwd(dy_mem, p["C"], o_mem, mkv)
    dmkv16 = dmkv.astype(BF16)
    tok = emit({"mk": _mm(memn, dmkv16, "tn", BF16, 1024, 1024, MEM, "d_wmk", after=tok), "C": d_win("C", dp_c)}, tok)
    dmemn = _mm(dmkv16, wmk, "nt", F32, MEM, 1024, D, "d_memn", after=tok)
    tok = advance(dmemn)
    d_memg = _memnorm_bwd(dmemn, mem)
    dy_rg = _mm(dz[0], wbr[0], "nt", F32, 1024, 1024, D, "d_branch0", after=tok)
    dp_a, d_cw, d_cb, d_wa, d_ba, d_wx, d_bx, d_lam = _rglru_bwd(dy_rg, p["A"], hseq, *rnn)
    tok = emit({"A": d_win("A", dp_a)}, tok)
    dy_swa = _mm(dz[1], wbr[1], "nt", F32, 1024, 1024, D, "d_branch1", after=tok)
    tok = advance(dy_swa)
    dp_b, dk, dv, d_bias, d_sink = _swa_bwd(dy_swa, p["B"], o_swa, bias_t, sinks, after=tok)
    dp_b = _swa_pack(dp_b, dk, dv)
    d_rel = _relbias_grad(d_bias.reshape(SWA_HEADS, QB * KB2), onehot_t).T
    dp = {"A": dp_a, "B": dp_b, "C": dp_c, "D": dp_d}
    tok = emit({"B": d_win("B", dp_b)}, tok)
    dh = None
    for g in GROUPS:
        dh = _mm(dp[g], w_grp[g], "nn", F32, 1024, 1024, 2304 if g == "B" else 2048, f"d_h_{g}", acc=dh,
                 after=tok if g in ("A", "B") else None)
        if g == "A":
            tok = advance(dh)
    grad_x, d_pre = _pre_bwd(dh, x, dy, sp["pre_norm_g"])

    d_small = {
        "pre_norm_g": d_pre, "post_norm_g": d_post, "mem_norm_g": d_memg, "conv_w": d_cw, "conv_b": d_cb,
        "w_rg_a": d_wa, "b_rg_a": d_ba, "w_rg_x": d_wx, "b_rg_x": d_bx, "lru_lambda": d_lam,
        "swa_sinks": d_sink[:, 0].reshape(1, SWA_HEADS), "rel_bias": d_rel,
    }
    return sq, grad_x, d_small


ANY = pl.BlockSpec(memory_space=pl.ANY)
SHARD_ROWS = D // N_CHIPS
GATHERED = {"A": (2048, D), "B": (2304, D), "C": (2048, D), "D": (6144, D), "mk": (D, D),
            "br0": (D_RNN, D), "br1": (D_RNN, D), "br2": (D_RNN, D), "out": (D, D)}
SHARD_SHAPES = {"win": (SHARD, D), "mk": (SHARD_ROWS, D), "br0": (D_RNN, SHARD_ROWS), "br1": (D_RNN, SHARD_ROWS),
                "br2": (D_RNN, SHARD_ROWS), "out": (SHARD_ROWS, D)}
SHARDS = tuple(SHARD_SHAPES)
HALF_AXIS = {"win": 1, "mk": 1, "br0": 0, "br1": 0, "br2": 0, "out": 1,
             "A": 1, "B": 1, "C": 1, "D": 1}


def _halved(shape, axis):
    return (shape[0] // 2, shape[1]) if axis == 0 else (shape[0], shape[1] // 2)


class Piece(NamedTuple):
    src: str
    dst: str
    rows: int
    sr0: int
    sc0: int
    dr0: int
    dc0: int
    ncols: int


def _pieces_of(jj):
    out = [Piece("win", g, n, r, 0, gr, 0, D) for r, n, g, gr in _shard_runs(jj)]
    out.append(Piece("mk", "mk", SHARD_ROWS, 0, 0, SHARD_ROWS * jj, 0, D))
    out += [Piece(f"br{b}", f"br{b}", D_RNN, 0, 0, 0, SHARD_ROWS * jj, SHARD_ROWS) for b in range(3)]
    out.append(Piece("out", "out", SHARD_ROWS, 0, 0, SHARD_ROWS * jj, 0, D))
    return out


def _half_rect(ref, p, side, which):
    r0, c0 = (p.sr0, p.sc0) if side == "src" else (p.dr0, p.dc0)
    if HALF_AXIS[p.src] == 1:
        return _rect(ref, r0, p.rows, c0 + which * (p.ncols // 2), p.ncols // 2)
    return _rect(ref, r0 + which * (p.rows // 2), p.rows // 2, c0, p.ncols)


def _rect_in_half(ref, p, side):
    r0, c0 = (p.sr0, p.sc0) if side == "src" else (p.dr0, p.dc0)
    if HALF_AXIS[p.src] == 1:
        return _rect(ref, r0, p.rows, 0, p.ncols // 2)
    return _rect(ref, 0, p.rows // 2, c0, p.ncols)


MAX_PIECES = max(len(_pieces_of(jj)) for jj in range(N_CHIPS))


def _rect(ref, r0, rows, c0, ncols):
    return ref.at[pl.ds(r0, rows), pl.ds(c0, ncols)]


def _position():
    x, y, c = lax.axis_index("x"), lax.axis_index("y"), lax.axis_index("c")
    return x, y, c, 2 * x + y


HBM = pl.BlockSpec(memory_space=pltpu.HBM)
SEM = pl.BlockSpec(memory_space=pltpu.SEMAPHORE)
EFFECT = pltpu.SideEffectType.DATAFLOW_SIDE_EFFECTING
N_SEM = MAX_PIECES * N_CHIPS
GATHER_STAGES = (("A",), ("B",), ("C",), ("mk",), ("br0", "br1", "br2"), ("D",), ("out",))


def _in_hbm(a):
    return pltpu.with_memory_space_constraint(a, pltpu.HBM)


def _stage_pieces(jj, stage):
    return [(i, p) for i, p in enumerate(_pieces_of(jj)) if p.dst in stage]


def _own_block_table(g):
    import numpy as np
    units = np.full((N_CHIPS, GATHERED[g][0] // HALF_TILE), -1, np.int64)
    for jj in range(N_CHIPS):
        for r, n, grp, gr in _shard_runs(jj):
            if grp == g:
                for k in range(n // HALF_TILE):
                    units[jj, gr // HALF_TILE + k] = r // HALF_TILE + k
    tbl = np.zeros((N_CHIPS, 2, GATHERED[g][0] // LANE), np.int32)
    for jj in range(N_CHIPS):
        for b in range(tbl.shape[2]):
            first, second = units[jj, 2 * b], units[jj, 2 * b + 1]
            if jj % 2 == 0:
                src = first if first >= 0 else second - 1
                if first >= 0 or second >= 0:
                    assert src % 2 == 0
                    tbl[jj, :, b] = src // 2
            else:
                if first >= 0:
                    assert first % 2 == 1
                    tbl[jj, 0, b] = first // 2
                if second >= 0:
                    assert second % 2 == 0
                    tbl[jj, 1, b] = second // 2
    return tbl


def _place_group(w_t, g, tables, odd_arr, after):
    nb = GATHERED[g][0] // LANE

    def body(t_ref, odd_ref, a_ref, b_ref, _, o_ref):
        odd = odd_ref[0] == 1
        o_ref[0:HALF_TILE, :] = jnp.where(odd, a_ref[HALF_TILE:LANE, :], a_ref[0:HALF_TILE, :]).astype(BF16)
        o_ref[HALF_TILE:LANE, :] = jnp.where(odd, b_ref[0:HALF_TILE, :], a_ref[HALF_TILE:LANE, :]).astype(BF16)

    return pl.pallas_call(
        body,
        name=f"place_{g}",
        grid_spec=pltpu.PrefetchScalarGridSpec(
            num_scalar_prefetch=2,
            grid=(nb,),
            in_specs=[pl.BlockSpec((LANE, D), lambda b, t, o: (t[0, b], 0)),
                      pl.BlockSpec((LANE, D), lambda b, t, o: (t[1, b], 0)), ANY],
            out_specs=pl.BlockSpec((LANE, D), lambda b, t, o: (b, 0)),
        ),
        out_shape=jax.ShapeDtypeStruct(GATHERED[g], BF16),
        compiler_params=_params(("parallel",)),
    )(tables, odd_arr, w_t, w_t, after)


def _place_shard(shard, name, after):
    rows, cols = shard.shape
    by_rows = HALF_AXIS[name] == 1

    def body(x_ref, _, o_ref):
        o_ref[...] = x_ref[...].astype(BF16)

    return pl.pallas_call(
        body,
        name=f"place_{name}",
        grid=(N_CHIPS,),
        in_specs=[pl.BlockSpec((rows, cols), lambda b: (0, 0)), ANY],
        out_specs=pl.BlockSpec((rows, cols), (lambda b: (b, 0)) if by_rows else (lambda b: (0, b))),
        out_shape=jax.ShapeDtypeStruct(GATHERED[name], BF16),
        compiler_params=_params(("parallel",)),
    )(shard, after)


def _gather_copy(arr, send_sems, recv_sems, c, jj, i, p, kk):
    rect = _half_rect(arr[p.dst], p, "dst", c)
    return pltpu.make_async_remote_copy(
        src_ref=rect, dst_ref=rect, send_sem=send_sems.at[i * N_CHIPS + kk],
        recv_sem=recv_sems.at[jj * MAX_PIECES + i], device_id=(kk // 2, kk % 2, c), device_id_type=MESH)


def _gather_start(arrays, after):
    stage = tuple(arrays)
    na = len(stage)

    def body(*refs):
        arr = dict(zip(stage, refs[:na]))
        send_sems, recv_sems = refs[na + 1], refs[na + 2]
        token = refs[-1]
        _, _, c, j = _position()
        for jj in range(N_CHIPS):
            @pl.when(j == jj)
            def _():
                for i, p in _stage_pieces(jj, stage):
                    for kk in range(N_CHIPS):
                        if kk != jj:
                            _gather_copy(arr, send_sems, recv_sems, c, jj, i, p, kk).start()
        token[...] = jnp.zeros_like(token)

    outs = pl.pallas_call(
        body,
        name=f"gather_start_{stage[0]}",
        in_specs=[HBM] * na + [ANY],
        out_specs=[SEM, SEM] + [HBM] * na + [pl.BlockSpec(memory_space=pltpu.VMEM)],
        out_shape=[pltpu.SemaphoreType.DMA((N_SEM,)), pltpu.SemaphoreType.DMA((N_SEM,))]
        + [pltpu.HBM(GATHERED[n], BF16) for n in stage] + [jax.ShapeDtypeStruct((8, LANE), F32)],
        input_output_aliases={k: 2 + k for k in range(na)},
        compiler_params=pltpu.CompilerParams(has_side_effects=EFFECT),
    )(*[_in_hbm(arrays[n]) for n in stage], after)
    return outs[0], outs[1], dict(zip(stage, outs[2:2 + na])), outs[-1]


def _gather_wait(send_sems, recv_sems, arrays, after):
    stage = tuple(arrays)
    na = len(stage)

    def body(*refs):
        arr = dict(zip(stage, refs[:na]))
        sems_s, sems_r = refs[na], refs[na + 1]
        _, _, c, j = _position()
        for jj in range(N_CHIPS):
            @pl.when(j != jj)
            def _():
                for i, p in _stage_pieces(jj, stage):
                    _gather_copy(arr, sems_s, sems_r, c, jj, i, p, jj).wait_recv()

            @pl.when(j == jj)
            def _():
                for i, p in _stage_pieces(jj, stage):
                    for kk in range(N_CHIPS):
                        if kk != jj:
                            _gather_copy(arr, sems_s, sems_r, c, jj, i, p, kk).wait_send()

    outs = pl.pallas_call(
        body,
        name=f"gather_wait_{stage[0]}",
        in_specs=[HBM] * na + [SEM, SEM, ANY],
        out_specs=[HBM] * na,
        out_shape=[pltpu.HBM(GATHERED[n], BF16) for n in stage],
        input_output_aliases={k: k for k in range(na)},
        compiler_params=pltpu.CompilerParams(has_side_effects=EFFECT),
    )(*[arrays[n] for n in stage], send_sems, recv_sems, after)
    return dict(zip(stage, outs))


def _gather_swap(arrays):
    stage = tuple(arrays)
    na = len(stage)

    def body(*refs):
        dst = dict(zip(stage, refs[na:2 * na]))
        send_sems, recv_sems = refs[2 * na:]
        x, y, c, j = _position()

        def fwd(jj, i, p, which):
            rect = _half_rect(dst[p.dst], p, "dst", which)
            return pltpu.make_async_remote_copy(
                src_ref=rect, dst_ref=rect, send_sem=send_sems.at[jj * MAX_PIECES + i],
                recv_sem=recv_sems.at[jj * MAX_PIECES + i], device_id=(x, y, 1 - c), device_id_type=MESH)

        for jj in range(N_CHIPS):
            @pl.when(j != jj)
            def _():
                for i, p in _stage_pieces(jj, stage):
                    fwd(jj, i, p, c).start()
        for jj in range(N_CHIPS):
            @pl.when(j != jj)
            def _():
                for i, p in _stage_pieces(jj, stage):
                    fwd(jj, i, p, 1 - c).wait_recv()
        for jj in range(N_CHIPS):
            @pl.when(j != jj)
            def _():
                for i, p in _stage_pieces(jj, stage):
                    fwd(jj, i, p, c).wait_send()

    outs = pl.pallas_call(
        body,
        name=f"gather_swap_{stage[0]}",
        in_specs=[ANY] * na,
        out_specs=[ANY] * na,
        out_shape=[jax.ShapeDtypeStruct(GATHERED[n], BF16) for n in stage],
        input_output_aliases={k: k for k in range(na)},
        scratch_shapes=[pltpu.SemaphoreType.DMA((N_SEM,)), pltpu.SemaphoreType.DMA((N_SEM,))],
        compiler_params=pltpu.CompilerParams(has_side_effects=True),
    )(*[arrays[n] for n in stage])
    return dict(zip(stage, outs))


def _pass_on_copy(arr, send_sems, recv_sems, x, y, c, jj, i, p, which):
    rect = _half_rect(arr[p.dst], p, "dst", which)
    return pltpu.make_async_remote_copy(
        src_ref=rect, dst_ref=rect, send_sem=send_sems.at[jj * MAX_PIECES + i],
        recv_sem=recv_sems.at[jj * MAX_PIECES + i], device_id=(x, y, 1 - c), device_id_type=MESH)


def _gather_pass_start(arrays, after):
    stage = tuple(arrays)
    na = len(stage)

    def body(*refs):
        arr = dict(zip(stage, refs[:na]))
        x, y, c, j = _position()
        for jj in range(N_CHIPS):
            @pl.when(j != jj)
            def _():
                for i, p in _stage_pieces(jj, stage):
                    _pass_on_copy(arr, refs[na + 1], refs[na + 2], x, y, c, jj, i, p, c).start()
        refs[-1][...] = jnp.zeros_like(refs[-1])

    outs = pl.pallas_call(
        body,
        name=f"gather_pass_start_{stage[0]}",
        in_specs=[HBM] * na + [ANY],
        out_specs=[SEM, SEM] + [HBM] * na + [pl.BlockSpec(memory_space=pltpu.VMEM)],
        out_shape=[pltpu.SemaphoreType.DMA((N_SEM,)), pltpu.SemaphoreType.DMA((N_SEM,))]
        + [pltpu.HBM(GATHERED[n], BF16) for n in stage] + [jax.ShapeDtypeStruct((8, LANE), F32)],
        input_output_aliases={k: 2 + k for k in range(na)},
        compiler_params=pltpu.CompilerParams(has_side_effects=EFFECT),
    )(*[arrays[n] for n in stage], after)
    return outs[0], outs[1], dict(zip(stage, outs[2:2 + na])), outs[-1]


def _gather_pass_wait(send_sems, recv_sems, arrays, after):
    stage = tuple(arrays)
    na = len(stage)

    def body(*refs):
        arr = dict(zip(stage, refs[:na]))
        x, y, c, j = _position()
        for jj in range(N_CHIPS):
            @pl.when(j != jj)
            def _():
                for i, p in _stage_pieces(jj, stage):
                    _pass_on_copy(arr, refs[na], refs[na + 1], x, y, c, jj, i, p, 1 - c).wait_recv()
                    _pass_on_copy(arr, refs[na], refs[na + 1], x, y, c, jj, i, p, c).wait_send()

    outs = pl.pallas_call(
        body,
        name=f"gather_pass_wait_{stage[0]}",
        in_specs=[HBM] * na + [SEM, SEM, ANY],
        out_specs=[HBM] * na,
        out_shape=[pltpu.HBM(GATHERED[n], BF16) for n in stage],
        input_output_aliases={k: k for k in range(na)},
        compiler_params=pltpu.CompilerParams(has_side_effects=EFFECT),
    )(*[arrays[n] for n in stage], send_sems, recv_sems, after)
    return dict(zip(stage, outs))


def _own_half(ref, shape, axis, which):
    if axis == 1:
        return ref.at[:, pl.ds(which * (shape[1] // 2), shape[1] // 2)]
    return ref.at[pl.ds(which * (shape[0] // 2), shape[0] // 2), :]


def _swap_copies(names, src, dst, send_sems, recv_sems):
    x, y, c, _ = _position()
    return [pltpu.make_async_remote_copy(
        src_ref=_own_half(src[n], GATHERED[n], HALF_AXIS[n], 1 - c), dst_ref=dst[n],
        send_sem=send_sems.at[k], recv_sem=recv_sems.at[k],
        device_id=(x, y, 1 - c), device_id_type=MESH) for k, n in enumerate(names)]


def _swap_start(grads, after):
    names = tuple(grads)
    n = len(names)

    def body(*refs):
        src, dst = dict(zip(names, refs[:n])), dict(zip(names, refs[n:2 * n]))
        for cp in _swap_copies(names, src, dst, refs[2 * n + 1], refs[2 * n + 2]):
            cp.start()
        refs[-1][...] = jnp.zeros_like(refs[-1])

    half_shape = lambda nm: _halved(GATHERED[nm], HALF_AXIS[nm])
    args = [_in_hbm(grads[nm]) for nm in names] + [_in_hbm(lax.empty(half_shape(nm), BF16)) for nm in names]
    if after is None:
        after = jnp.zeros((8, LANE), F32)
    outs = pl.pallas_call(
        body,
        name=f"swap_start_{names[0]}",
        in_specs=[HBM] * (2 * n) + [ANY],
        out_specs=[SEM, SEM] + [HBM] * (2 * n) + [pl.BlockSpec(memory_space=pltpu.VMEM)],
        out_shape=[pltpu.SemaphoreType.DMA((n,)), pltpu.SemaphoreType.DMA((n,))]
        + [pltpu.HBM(GATHERED[nm], BF16) for nm in names] + [pltpu.HBM(half_shape(nm), BF16) for nm in names]
        + [jax.ShapeDtypeStruct((8, LANE), F32)],
        input_output_aliases={k: 2 + k for k in range(2 * n)},
        compiler_params=pltpu.CompilerParams(has_side_effects=EFFECT),
    )(*args, after)
    return outs[0], outs[1], dict(zip(names, outs[2:2 + n])), dict(zip(names, outs[2 + n:2 + 2 * n])), outs[-1]


def _swap_wait(send_sems, recv_sems, grads, landing, after):
    names = tuple(grads)
    n = len(names)

    def body(*refs):
        src, dst = dict(zip(names, refs[:n])), dict(zip(names, refs[n:2 * n]))
        copies = _swap_copies(names, src, dst, refs[2 * n], refs[2 * n + 1])
        for cp in copies:
            cp.wait_recv()
        for cp in copies:
            cp.wait_send()

    half_shape = lambda nm: _halved(GATHERED[nm], HALF_AXIS[nm])
    outs = pl.pallas_call(
        body,
        name=f"swap_wait_{names[0]}",
        in_specs=[HBM] * (2 * n) + [SEM, SEM, ANY],
        out_specs=[HBM] * (2 * n),
        out_shape=[pltpu.HBM(GATHERED[nm], BF16) for nm in names] + [pltpu.HBM(half_shape(nm), BF16) for nm in names],
        input_output_aliases={k: k for k in range(2 * n)},
        compiler_params=pltpu.CompilerParams(has_side_effects=EFFECT),
    )(*[grads[nm] for nm in names], *[landing[nm] for nm in names], send_sems, recv_sems, after)
    return dict(zip(names, outs[:n])), dict(zip(names, outs[n:]))


ADD_ROWS = {"A": 1024, "B": 768, "C": 1024, "D": 1536, "mk": 1024, "br0": 512, "br1": 512, "br2": 512, "out": 1024}


def _add_half(full, recv, c_arr, name):
    rows, cols = recv.shape
    tr = ADD_ROWS[name]
    if HALF_AXIS[name] == 1:
        index = lambda i, c_ref: (i, c_ref[0])
    else:
        nb = rows // tr
        index = lambda i, c_ref: (nb * c_ref[0] + i, 0)

    def body(c_ref, a_ref, b_ref, o_ref):
        o_ref[...] = (a_ref[...].astype(F32) + b_ref[...].astype(F32)).astype(BF16)

    return pl.pallas_call(
        body,
        name=f"add_half_{name}",
        grid_spec=pltpu.PrefetchScalarGridSpec(
            num_scalar_prefetch=1,
            grid=(rows // tr,),
            in_specs=[pl.BlockSpec((tr, cols), index), pl.BlockSpec((tr, cols), lambda i, c_ref: (i, 0))],
            out_specs=pl.BlockSpec((tr, cols), lambda i, c_ref: (i, 0)),
        ),
        out_shape=jax.ShapeDtypeStruct((rows, cols), BF16),
        compiler_params=_params(("parallel",)),
    )(c_arr, full, recv)


SLOT_SHAPES = {n: _halved(SHARD_SHAPES[n], HALF_AXIS[n]) for n in SHARDS}


def _slot_shape(n):
    return (N_CHIPS,) + SLOT_SHAPES[n]


def _stage_shards(stage):
    pieces = [p for jj in range(N_CHIPS) for p in _pieces_of(jj)]
    return tuple(s for s in SHARDS if any(p.src == s and p.dst in stage for p in pieces))


def _scatter_copy(src, dst, send_sems, recv_sems, c, jj, kk, i, p):
    return pltpu.make_async_remote_copy(
        src_ref=_rect_in_half(src[p.dst], p, "dst"), dst_ref=_rect_in_half(dst[p.src].at[jj], p, "src"),
        send_sem=send_sems.at[kk * MAX_PIECES + i], recv_sem=recv_sems.at[jj * MAX_PIECES + i],
        device_id=(kk // 2, kk % 2, c), device_id_type=MESH)


def _scatter_start(halves, slots):
    stage, touched = tuple(halves), tuple(slots)
    nh, nt = len(stage), len(touched)

    def body(*refs):
        src = dict(zip(stage, refs[:nh]))
        dst = dict(zip(touched, refs[nh:nh + nt]))
        send_sems, recv_sems = refs[nh + nt], refs[nh + nt + 1]
        token = refs[-1]
        _, _, c, j = _position()
        for jj in range(N_CHIPS):
            @pl.when(j == jj)
            def _():
                for kk in range(N_CHIPS):
                    if kk != jj:
                        for i, p in _stage_pieces(kk, stage):
                            _scatter_copy(src, dst, send_sems, recv_sems, c, jj, kk, i, p).start()
        token[...] = jnp.zeros_like(token)

    outs = pl.pallas_call(
        body,
        name=f"scatter_start_{stage[0]}",
        in_specs=[HBM] * (nh + nt),
        out_specs=[SEM, SEM] + [HBM] * (nh + nt) + [pl.BlockSpec(memory_space=pltpu.VMEM)],
        out_shape=[pltpu.SemaphoreType.DMA((N_SEM,)), pltpu.SemaphoreType.DMA((N_SEM,))]
        + [pltpu.HBM(halves[n].shape, BF16) for n in stage] + [pltpu.HBM(_slot_shape(s), BF16) for s in touched]
        + [jax.ShapeDtypeStruct((8, LANE), F32)],
        input_output_aliases={k: 2 + k for k in range(nh + nt)},
        compiler_params=pltpu.CompilerParams(has_side_effects=EFFECT),
    )(*[_in_hbm(halves[n]) for n in stage], *[_in_hbm(slots[s]) for s in touched])
    return outs[0], outs[1], dict(zip(stage, outs[2:2 + nh])), dict(zip(touched, outs[2 + nh:2 + nh + nt])), outs[-1]


def _scatter_wait(send_sems, recv_sems, halves, slots, after):
    stage, touched = tuple(halves), tuple(slots)
    nh, nt = len(stage), len(touched)

    def body(*refs):
        src = dict(zip(stage, refs[:nh]))
        dst = dict(zip(touched, refs[nh:nh + nt]))
        sems_s, sems_r = refs[nh + nt], refs[nh + nt + 1]
        _, _, c, j = _position()
        for jj in range(N_CHIPS):
            @pl.when(j == jj)
            def _():
                for ss in range(N_CHIPS):
                    if ss != jj:
                        for i, p in _stage_pieces(jj, stage):
                            _scatter_copy(src, dst, sems_s, sems_r, c, ss, jj, i, p).wait_recv()
                for kk in range(N_CHIPS):
                    if kk != jj:
                        for i, p in _stage_pieces(kk, stage):
                            _scatter_copy(src, dst, sems_s, sems_r, c, jj, kk, i, p).wait_send()

    outs = pl.pallas_call(
        body,
        name=f"scatter_wait_{stage[0]}",
        in_specs=[HBM] * (nh + nt) + [SEM, SEM, ANY],
        out_specs=[HBM] * (nh + nt),
        out_shape=[pltpu.HBM(halves[n].shape, BF16) for n in stage] + [pltpu.HBM(_slot_shape(s), BF16) for s in touched],
        input_output_aliases={k: k for k in range(nh + nt)},
        compiler_params=pltpu.CompilerParams(has_side_effects=EFFECT),
    )(*[halves[n] for n in stage], *[slots[s] for s in touched], send_sems, recv_sems, after)
    return dict(zip(stage, outs[:nh])), dict(zip(touched, outs[nh:]))


SUM_ROWS = {"mk": 512, "br0": 512, "br1": 512, "br2": 512, "out": 512}


def _sum_in_chip_order(chip, own, s_ref):
    acc = None
    for k in range(N_CHIPS):
        term = jnp.where(chip == k, own, s_ref[k].astype(F32))
        acc = term if acc is None else acc + term
    return acc


def _sum_slots(slots, own_half, pos_arr, name):
    _, rows, cols = slots.shape
    tr = SUM_ROWS[name]
    nb = rows // tr
    if HALF_AXIS[name] == 1:
        own_index = lambda i, pos: (nb * pos[1] + i, 0)
        out_index = lambda i, pos: (i, pos[0])
    else:
        own_index = lambda i, pos: (i, pos[1])
        out_index = lambda i, pos: (nb * pos[0] + i, 0)

    def body(pos, s_ref, own_ref, o_ref):
        o_ref[...] = _sum_in_chip_order(pos[1], own_ref[...].astype(F32), s_ref)

    return pl.pallas_call(
        body,
        name=f"sum_slots_{name}",
        grid_spec=pltpu.PrefetchScalarGridSpec(
            num_scalar_prefetch=1,
            grid=(nb,),
            in_specs=[pl.BlockSpec((N_CHIPS, tr, cols), lambda i, pos: (0, i, 0)),
                      pl.BlockSpec((tr, cols), own_index)],
            out_specs=pl.BlockSpec((tr, cols), out_index),
        ),
        out_shape=jax.ShapeDtypeStruct(SHARD_SHAPES[name], F32),
        compiler_params=_params(("parallel",)),
    )(pos_arr, slots, own_half)


def _own_partial_tables():
    import numpy as np
    nb = SHARD // HALF_TILE
    grp, blk = np.zeros((N_CHIPS, nb), np.int32), np.zeros((N_CHIPS, nb), np.int32)
    for jj in range(N_CHIPS):
        for r, n, g, gr in _shard_runs(jj):
            for k in range(n // HALF_TILE):
                grp[jj, r // HALF_TILE + k] = GROUPS.index(g)
                blk[jj, r // HALF_TILE + k] = gr // HALF_TILE + k
    return grp, blk


def _sum_slots_win(slots, own_halves, pos_arr, grp_tbl, blk_tbl):
    nb = SHARD // HALF_TILE
    cols = D // 2

    def own_spec(gi):
        return pl.BlockSpec((HALF_TILE, cols), lambda b, pos, grp, blk: (jnp.where(grp[b] == gi, blk[b], 0), 0))

    def body(pos, grp, blk, s_ref, a_ref, b_ref, c_ref, d_ref, o_ref):
        g = grp[pl.program_id(0)]
        own = a_ref[...]
        for gi, ref in ((1, b_ref), (2, c_ref), (3, d_ref)):
            own = jnp.where(g == gi, ref[...], own)
        o_ref[...] = _sum_in_chip_order(pos[1], own.astype(F32), s_ref)

    return pl.pallas_call(
        body,
        name="sum_slots_win",
        grid_spec=pltpu.PrefetchScalarGridSpec(
            num_scalar_prefetch=3,
            grid=(nb,),
            in_specs=[pl.BlockSpec((N_CHIPS, HALF_TILE, cols), lambda b, pos, grp, blk: (0, b, 0))]
            + [own_spec(gi) for gi in range(len(GROUPS))],
            out_specs=pl.BlockSpec((HALF_TILE, cols), lambda b, pos, grp, blk: (b, pos[0])),
        ),
        out_shape=jax.ShapeDtypeStruct(SHARD_SHAPES["win"], F32),
        compiler_params=_params(("parallel",)),
    )(pos_arr, grp_tbl, blk_tbl, slots, *[own_halves[g] for g in GROUPS])


def _share_copy(buf, name, send_sems, recv_sems, k, which):
    x, y, c, _ = _position()
    half = _own_half(buf, SHARD_SHAPES[name], HALF_AXIS[name], which)
    return pltpu.make_async_remote_copy(src_ref=half, dst_ref=half, send_sem=send_sems.at[k], recv_sem=recv_sems.at[k],
                                        device_id=(x, y, 1 - c), device_id_type=MESH)


def _share_start(sums, after):
    names = tuple(sums)
    n = len(names)

    def body(*refs):
        _, _, c, _ = _position()
        for k, nm in enumerate(names):
            _share_copy(refs[k], nm, refs[n + 1], refs[n + 2], k, c).start()
        refs[-1][...] = jnp.zeros_like(refs[-1])

    outs = pl.pallas_call(
        body,
        name=f"share_start_{names[0]}",
        in_specs=[HBM] * n + [ANY],
        out_specs=[SEM, SEM] + [HBM] * n + [pl.BlockSpec(memory_space=pltpu.VMEM)],
        out_shape=[pltpu.SemaphoreType.DMA((n,)), pltpu.SemaphoreType.DMA((n,))]
        + [pltpu.HBM(SHARD_SHAPES[nm], F32) for nm in names] + [jax.ShapeDtypeStruct((8, LANE), F32)],
        input_output_aliases={k: 2 + k for k in range(n)},
        compiler_params=pltpu.CompilerParams(has_side_effects=EFFECT),
    )(*[_in_hbm(sums[nm]) for nm in names], after)
    return outs[0], outs[1], dict(zip(names, outs[2:2 + n])), outs[-1]


def _share_wait(send_sems, recv_sems, sums, after):
    names = tuple(sums)
    n = len(names)

    def body(*refs):
        _, _, c, _ = _position()
        for k, nm in enumerate(names):
            _share_copy(refs[k], nm, refs[n], refs[n + 1], k, 1 - c).wait_recv()
            _share_copy(refs[k], nm, refs[n], refs[n + 1], k, c).wait_send()

    outs = pl.pallas_call(
        body,
        name=f"share_wait_{names[0]}",
        in_specs=[HBM] * n + [SEM, SEM, ANY],
        out_specs=[HBM] * n,
        out_shape=[pltpu.HBM(SHARD_SHAPES[nm], F32) for nm in names],
        input_output_aliases={k: k for k in range(n)},
        compiler_params=pltpu.CompilerParams(has_side_effects=EFFECT),
    )(*[sums[nm] for nm in names], send_sems, recv_sems, after)
    return dict(zip(names, outs))


def _all_reduce_small(pack, name):
    rows = pack.shape[0]
    half = rows // 2

    def body(p_ref, o_ref, sib, land, sems):
        x, y, c, j = _position()
        sibling = (x, y, 1 - c)
        swap = pltpu.make_async_remote_copy(src_ref=p_ref, dst_ref=sib, send_sem=sems.at[0], recv_sem=sems.at[1],
                                            device_id=sibling, device_id_type=MESH)
        swap.start()
        swap.wait_recv()
        land[j] = p_ref[...] + sib[...]

        def mine(k, which):
            return land.at[k, pl.ds(which * half, half)]

        def ici(kk):
            return pltpu.make_async_remote_copy(
                src_ref=mine(j, c), dst_ref=mine(j, c), send_sem=sems.at[2 + kk], recv_sem=sems.at[6 + j],
                device_id=(kk // 2, kk % 2, c), device_id_type=MESH)

        def arrival(kk):
            return pltpu.make_async_remote_copy(
                src_ref=mine(kk, c), dst_ref=mine(kk, c), send_sem=sems.at[2 + kk], recv_sem=sems.at[6 + kk],
                device_id=(kk // 2, kk % 2, c), device_id_type=MESH)

        def passed_on(kk, which):
            return pltpu.make_async_remote_copy(
                src_ref=mine(kk, which), dst_ref=mine(kk, which), send_sem=sems.at[10 + kk],
                recv_sem=sems.at[14 + kk], device_id=sibling, device_id_type=MESH)

        for kk in range(N_CHIPS):
            @pl.when(j != kk)
            def _():
                ici(kk).start()
        for kk in range(N_CHIPS):
            @pl.when(j != kk)
            def _():
                arrival(kk).wait_recv()
                passed_on(kk, c).start()
        for kk in range(N_CHIPS):
            @pl.when(j != kk)
            def _():
                passed_on(kk, 1 - c).wait_recv()
        acc = land[0]
        for kk in range(1, N_CHIPS):
            acc = acc + land[kk]
        o_ref[...] = acc
        swap.wait_send()
        for kk in range(N_CHIPS):
            @pl.when(j != kk)
            def _():
                ici(kk).wait_send()
                passed_on(kk, c).wait_send()

    vmem = pl.BlockSpec(memory_space=pltpu.VMEM)
    return pl.pallas_call(
        body,
        name=name,
        in_specs=[vmem],
        out_specs=vmem,
        out_shape=jax.ShapeDtypeStruct((rows, LANE), F32),
        scratch_shapes=[pltpu.VMEM((rows, LANE), F32), pltpu.VMEM((N_CHIPS, rows, LANE), F32),
                        pltpu.SemaphoreType.DMA((18,))],
        compiler_params=pltpu.CompilerParams(has_side_effects=True, vmem_limit_bytes=VMEM_LIMIT),
    )(pack)


ADAM_ROWS = {"win": 224, "mk": 256, "br0": 512, "br1": 512, "br2": 512, "out": 256}


def _adamw(w, g, m, v, name, tr):
    rows, cols = w.shape
    tr = min(tr, rows)

    def body(w_ref, g_ref, m_ref, v_ref, go_ref, d_ref, nm_ref, nv_ref):
        gv = g_ref[...]
        go_ref[...] = gv
        nm = ADAM_B1 * m_ref[...] + (1.0 - ADAM_B1) * gv
        nv = ADAM_B2 * v_ref[...] + (1.0 - ADAM_B2) * (gv * gv)
        nm_ref[...] = nm
        nv_ref[...] = nv
        m_hat = nm / (1.0 - ADAM_B1 ** ADAM_STEP)
        v_hat = nv / (1.0 - ADAM_B2 ** ADAM_STEP)
        d_ref[...] = -ADAM_LR * (m_hat / (jnp.sqrt(v_hat) + ADAM_EPS) + ADAM_WD * w_ref[...])

    blk = pl.BlockSpec((tr, cols), lambda i: (i, 0))
    shape = jax.ShapeDtypeStruct((rows, cols), F32)
    return pl.pallas_call(
        body,
        name=f"adamw_{name}",
        grid=(rows // tr,),
        in_specs=[blk] * 4,
        out_specs=[blk] * 4,
        out_shape=[shape] * 4,
        compiler_params=_params(("parallel",)),
    )(w, g, m, v)


SMALL = (("pre_norm_g", (1, D)), ("post_norm_g", (1, D)), ("mem_norm_g", (1, D)), ("conv_w", (CONV_W, D_RNN)),
         ("conv_b", (1, D_RNN)), ("w_rg_a", (RNN_BLOCKS, LANE, LANE)), ("b_rg_a", (1, D_RNN)),
         ("w_rg_x", (RNN_BLOCKS, LANE, LANE)), ("b_rg_x", (1, D_RNN)), ("lru_lambda", (1, D_RNN)),
         ("swa_sinks", (1, SWA_HEADS)), ("rel_bias", (REL_BUCKETS, SWA_HEADS)))
PACK_ROWS = 2176


def _slot_len(shape):
    return -(-math.prod(shape) // LANE) * LANE


def _pack(values, last_row=None):
    parts = []
    for name, shape in SMALL:
        flat = values[name].reshape(-1).astype(F32)
        parts.append(jnp.pad(flat, (0, _slot_len(shape) - flat.shape[0])))
    flat = jnp.concatenate(parts)
    tail = jnp.zeros((LANE,), F32) if last_row is None else last_row
    return jnp.concatenate([jnp.pad(flat, (0, (PACK_ROWS - 1) * LANE - flat.shape[0])), tail]).reshape(PACK_ROWS, LANE)


def _unpack(pack):
    flat = pack.reshape(-1)
    out, off = {}, 0
    for name, shape in SMALL:
        out[name] = flat[off:off + math.prod(shape)].reshape(shape)
        off += _slot_len(shape)
    return out


TWIN_WEIGHTS = ("pre_norm_g", "post_norm_g", "mem_norm_g", "w_in", "conv_w", "conv_b", "w_rg_a", "b_rg_a", "w_rg_x",
                "b_rg_x", "lru_lambda", "swa_sinks", "rel_bias", "w_mem_kv", "w_br_rg", "w_br_swa", "w_br_mem", "w_out")
BIG = {"w_in": "win", "w_mem_kv": "mk", "w_br_rg": "br0", "w_br_swa": "br1", "w_br_mem": "br2", "w_out": "out"}


def kernel(x, mem, pre_norm_g, post_norm_g, mem_norm_g, w_in, conv_w, conv_b, w_rg_a, b_rg_a, w_rg_x, b_rg_x, lru_lambda, swa_sinks, rel_bias, w_mem_kv, w_br_rg, w_br_swa, w_br_mem, w_out, loss_target, m_pre_norm_g, m_post_norm_g, m_mem_norm_g, m_w_in, m_conv_w, m_conv_b, m_w_rg_a, m_b_rg_a, m_w_rg_x, m_b_rg_x, m_lru_lambda, m_swa_sinks, m_rel_bias, m_w_mem_kv, m_w_br_rg, m_w_br_swa, m_w_br_mem, m_w_out, v_pre_norm_g, v_post_norm_g, v_mem_norm_g, v_w_in, v_conv_w, v_conv_b, v_w_rg_a, v_b_rg_a, v_w_rg_x, v_b_rg_x, v_lru_lambda, v_swa_sinks, v_rel_bias, v_w_mem_kv, v_w_br_rg, v_w_br_swa, v_w_br_mem, v_w_out):
    args = dict(locals())
    out_shapes = {n: args[n].shape for n in TWIN_WEIGHTS}
    w = {n: (args[n] if n == "rel_bias" else args[n][0]) for n in TWIN_WEIGHTS}
    m = {n: (args["m_" + n] if n == "rel_bias" else args["m_" + n][0]) for n in TWIN_WEIGHTS}
    v = {n: (args["v_" + n] if n == "rel_bias" else args["v_" + n][0]) for n in TWIN_WEIGHTS}
    for d in (w, m, v):
        for n, shape in SMALL:
            if n != "conv_w":
                d[n] = d[n].reshape(shape)

    xi, yi, ci = lax.axis_index("x"), lax.axis_index("y"), lax.axis_index("c")
    chip = 2 * xi + yi
    c_arr = ci.astype(jnp.int32).reshape(1)
    zero = jnp.zeros((), jnp.int32)
    cw0 = (chip * (D_RNN // N_CHIPS)).astype(jnp.int32)

    placed = lax.dynamic_update_slice(jnp.zeros((CONV_W, D_RNN), F32), w["conv_w"], (zero, cw0))
    placed = jnp.where(ci == 0, placed, 0.0).reshape(CONV_W * D_RNN // LANE, LANE)
    conv_w_full = _all_reduce_small(placed, "gather_conv_w").reshape(CONV_W, D_RNN)

    for d in (w, m, v):
        d["w_in"] = d["w_in"].T
    chip_row = lambda tbl: lax.dynamic_slice(jnp.asarray(tbl), (chip.astype(jnp.int32), zero), (1, tbl.shape[1]))[0]
    chip_tables = lambda tbl: lax.dynamic_slice(jnp.asarray(tbl), (chip.astype(jnp.int32), zero, zero),
                                                (1,) + tbl.shape[1:])[0]
    odd_arr = yi.astype(jnp.int32).reshape(1)
    big_of = {s: n for n, s in BIG.items()}
    ag, token = {}, conv_w_full
    for stage in GATHER_STAGES:
        behind = c_arr if stage == GATHER_STAGES[0] else token
        placed = {n: (_place_group(w["w_in"], n, chip_tables(_own_block_table(n)), odd_arr, behind) if n in GROUPS
                      else _place_shard(w[big_of[n]], n, behind)) for n in stage}
        send, recv, in_flight, token = _gather_start(placed, token)
        ag[stage] = (send, recv, in_flight)
    h = token = _rms_fwd(x[0], w["pre_norm_g"], "rms_pre", token)

    all_started = token

    passing = {}

    def prefetch(names, after):
        send, recv, in_flight = ag[names]
        *passing[names], token = _gather_pass_start(_gather_wait(send, recv, in_flight, after), after)
        return token

    def fetch(names, after):
        if names in passing:
            ready = _gather_pass_wait(*passing.pop(names), after)
        else:
            send, recv, in_flight = ag[names]
            after = all_started if names == GATHER_STAGES[0] else after
            ready = _gather_swap(_gather_wait(send, recv, in_flight, after))
        return tuple(ready[n] for n in names)

    rs = {"slots": {}, "halves": {}, "pending": [], "swap": None}

    def emit(grads, after=None):
        assert rs["swap"] is None
        *rs["swap"], token = _swap_start(grads, after)
        return token

    def advance(after):
        grads, received = _swap_wait(*rs["swap"], after)
        rs["swap"] = None
        halves = {n: _add_half(grads[n], received[n], c_arr, n) for n in grads}
        landing = {s: rs["slots"][s] if s in rs["slots"] else lax.empty(_slot_shape(s), BF16)
                   for s in _stage_shards(tuple(grads))}
        send, recv, halves, landing, token = _scatter_start(halves, landing)
        rs["slots"].update(landing)
        rs["pending"].append((send, recv, halves, tuple(landing)))
        return token

    sp = {n: w[n] for n, _ in SMALL}
    sp["conv_w"] = conv_w_full
    sq, grad_x, d_small = _local_step(x[0], h, mem[0], loss_target[0], sp, fetch, prefetch, emit, advance)
    small_total = _all_reduce_small(_pack(d_small, sq[0]), "all_reduce_small")
    loss = small_total[PACK_ROWS - 1, 0] * (0.5 / D)

    for send, recv, halves, touched in rs["pending"]:
        halves, landed = _scatter_wait(send, recv, halves, {s: rs["slots"][s] for s in touched}, small_total)
        rs["slots"].update(landed)
        rs["halves"].update(halves)
    pos_arr = jnp.stack([ci, chip]).astype(jnp.int32)
    grp_tbl, blk_tbl = (chip_row(t) for t in _own_partial_tables())
    rest = {s: _sum_slots(rs["slots"][s], rs["halves"][s], pos_arr, s) for s in SHARDS if s != "win"}
    *rest_share, tok = _share_start(rest, small_total)
    win_sum = _sum_slots_win(rs["slots"]["win"], rs["halves"], pos_arr, grp_tbl, blk_tbl)
    *win_share, tok = _share_start({"win": win_sum}, tok)
    sums = _share_wait(*rest_share, tok)

    grad, delta, new_m, new_v = {}, {}, {}, {}
    for n, s in BIG.items():
        if n == "w_in":
            continue
        grad[n], delta[n], new_m[n], new_v[n] = _adamw(w[n], sums[s], m[n], v[n], s, ADAM_ROWS[s])
    g_win = _share_wait(*win_share, delta["w_out"])["win"]
    n = "w_in"
    grad[n], delta[n], new_m[n], new_v[n] = _adamw(w[n], g_win, m[n], v[n], "win", ADAM_ROWS["win"])
    for group in (grad, delta, new_m, new_v):
        group["w_in"] = group["w_in"].T
    def conv_w_in_place(d):
        return dict(d, conv_w=lax.dynamic_update_slice(jnp.zeros((CONV_W, D_RNN), F32), d["conv_w"], (zero, cw0)))

    _, d_, m_, v_ = _adamw(_pack(conv_w_in_place(w)), small_total, _pack(conv_w_in_place(m)),
                           _pack(conv_w_in_place(v)), "small", PACK_ROWS)
    for group, pack in ((grad, small_total), (delta, d_), (new_m, m_), (new_v, v_)):
        group.update(_unpack(pack))
    for group in (grad, delta, new_m, new_v):
        group["conv_w"] = lax.dynamic_slice(group["conv_w"], (zero, cw0), (CONV_W, D_RNN // N_CHIPS))

    outs = [loss, grad_x.reshape(1, S, D)]
    for group in (grad, delta, new_m, new_v):
        outs += [group[n].reshape(out_shapes[n]) for n in TWIN_WEIGHTS]
    return tuple(outs)
```

```python
import math
from typing import NamedTuple

import jax
import jax.numpy as jnp
from jax import lax
from jax.experimental import pallas as pl
from jax.experimental.pallas import tpu as pltpu

F32 = jnp.float32
BF16 = jnp.bfloat16
MESH = pl.DeviceIdType.MESH

S = 2048
D = 2048
MEM = 256
D_RNN = 1024
RNN_BLOCKS = 8
CONV_W = 4
LRU_C = 8.0
SWA_HEADS = 16
SWA_HD = 64
WINDOW = 128
MEM_HEADS = 4
MEM_HD = 256
REL_BUCKETS = 32
REL_MAX_DIST = 128
EPS = 1e-6
NEG_INF = -1e30
LANE = 128
SHARD = 3136
HALF_TILE = 64
N_CHIPS = 4
VMEM_LIMIT = 56 * 1024 * 1024

ADAM_LR = 0.001
ADAM_B1 = 0.9
ADAM_B2 = 0.999
ADAM_EPS = 1e-08
ADAM_WD = 0.01
ADAM_STEP = 10

GROUP_TILES = {"A": 16, "B": 18, "C": 16, "D": 48}
GROUPS = ("A", "B", "C", "D")


def _params(sem=None):
    return pltpu.CompilerParams(dimension_semantics=sem, vmem_limit_bytes=VMEM_LIMIT)


def _sigmoid(v):
    return jax.nn.sigmoid(v)


def _tile_home(t):
    if t < 16:
        return "A", t
    if t < 24:
        return "B", t - 16
    if t < 26:
        return "B", t - 24 + 16
    if t < 34:
        return "B", t - 26 + 8
    if t < 50:
        return "C", t - 34
    return "D", t - 50


def _shard_runs(j):
    runs = []
    per_shard = SHARD // HALF_TILE
    for q in range(per_shard * j, per_shard * (j + 1)):
        g, gt = _tile_home(q // 2)
        row = gt * LANE + (q % 2) * HALF_TILE
        if runs and runs[-1][2] == g and runs[-1][3] + runs[-1][1] == row:
            runs[-1][1] += HALF_TILE
        else:
            runs.append([(q - per_shard * j) * HALF_TILE, HALF_TILE, g, row])
    return [tuple(r) for r in runs]


_DIMS = {
    "nn": (((1,), (0,)), ((), ())),
    "nt": (((1,), (1,)), ((), ())),
    "tn": (((0,), (0,)), ((), ())),
}


def _mm(a, b, mode, out_dtype, tm, tn, tk, name, acc=None, after=None):
    if mode == "nn":
        (m, k), n = a.shape, b.shape[1]
    elif mode == "nt":
        (m, k), n = a.shape, b.shape[0]
    else:
        (k, m), n = a.shape, b.shape[1]
    tm, tn, tk = min(tm, m), min(tn, n), min(tk, k)
    assert m % tm == 0 and n % tn == 0 and k % tk == 0, (name, m, n, k)
    nk = k // tk
    has_acc = acc is not None

    def body(*refs):
        a_ref, b_ref = refs[0], refs[1]
        o_ref = refs[3] if has_acc else refs[2]
        p = lax.dot_general(a_ref[...], b_ref[...], _DIMS[mode], preferred_element_type=F32)

        def finish(v):
            if has_acc:
                v = v + refs[2][...]
            o_ref[...] = v.astype(out_dtype)

        if nk == 1:
            finish(p)
        else:
            s_ref = refs[-1]
            kk = pl.program_id(2)

            @pl.when(kk == 0)
            def _():
                s_ref[...] = p

            @pl.when(kk > 0)
            def _():
                s_ref[...] += p

            @pl.when(kk == nk - 1)
            def _():
                finish(s_ref[...])

    if mode == "nn":
        a_spec = pl.BlockSpec((tm, tk), lambda i, j, kk: (i, kk))
        b_spec = pl.BlockSpec((tk, tn), lambda i, j, kk: (kk, j))
    elif mode == "nt":
        a_spec = pl.BlockSpec((tm, tk), lambda i, j, kk: (i, kk))
        b_spec = pl.BlockSpec((tn, tk), lambda i, j, kk: (j, kk))
    else:
        a_spec = pl.BlockSpec((tk, tm), lambda i, j, kk: (kk, i))
        b_spec = pl.BlockSpec((tk, tn), lambda i, j, kk: (kk, j))
    o_spec = pl.BlockSpec((tm, tn), lambda i, j, kk: (i, j))
    in_specs = [a_spec, b_spec] + ([o_spec] if has_acc else [])
    args = (a, b) + ((acc,) if has_acc else ())
    if after is not None:
        in_specs.append(pl.BlockSpec(memory_space=pl.ANY))
        args += (after,)
    n_in = len(args)
    kernel_body = body

    def body(*refs):
        kernel_body(*(refs[:n_in - (after is not None)] + refs[n_in:]))

    return pl.pallas_call(
        body,
        name=name,
        grid=(m // tm, n // tn, nk),
        in_specs=in_specs,
        out_specs=o_spec,
        out_shape=jax.ShapeDtypeStruct((m, n), out_dtype),
        scratch_shapes=[pltpu.VMEM((tm, tn), F32)] if nk > 1 else [],
        compiler_params=_params(("parallel", "parallel", "arbitrary")),
    )(*args)


def _rms_fwd(x, g, name, after, ts=256):
    r, d = x.shape

    def body(x_ref, g_ref, _, o_ref):
        xv = x_ref[...]
        inv = lax.rsqrt(jnp.mean(xv * xv, axis=-1, keepdims=True) + EPS)
        o_ref[...] = (xv * inv * g_ref[...]).astype(BF16)

    return pl.pallas_call(
        body,
        name=name,
        grid=(r // ts,),
        in_specs=[pl.BlockSpec((ts, d), lambda i: (i, 0)), pl.BlockSpec((1, d), lambda i: (0, 0)),
                  pl.BlockSpec(memory_space=pl.ANY)],
        out_specs=pl.BlockSpec((ts, d), lambda i: (i, 0)),
        out_shape=jax.ShapeDtypeStruct((r, d), BF16),
        compiler_params=_params(("parallel",)),
    )(x, g, after)


def _post_loss(out, x, tgt, g_post, ts=256):
    n = S // ts

    def body(o_ref, x_ref, t_ref, g_ref, sq_ref, dy_ref, do_ref, dg_ref):
        i = pl.program_id(0)

        @pl.when(i == 0)
        def _():
            sq_ref[...] = jnp.zeros_like(sq_ref)
            dg_ref[...] = jnp.zeros_like(dg_ref)

        ov = o_ref[...]
        g = g_ref[...]
        inv = lax.rsqrt(jnp.mean(ov * ov, axis=-1, keepdims=True) + EPS)
        on = ov * inv
        err = x_ref[...] + on * g - t_ref[...]
        sq_ref[...] += jnp.sum(err * err)
        dy = err * (1.0 / D)
        dy_ref[...] = dy
        dg_ref[...] += jnp.sum(dy * on, axis=0, keepdims=True)
        don = dy * g
        do_ref[...] = (inv * (don - on * jnp.mean(don * on, axis=-1, keepdims=True))).astype(BF16)

    row = pl.BlockSpec((ts, D), lambda i: (i, 0))
    vec = pl.BlockSpec((1, D), lambda i: (0, 0))
    return pl.pallas_call(
        body,
        name="post_loss",
        grid=(n,),
        in_specs=[row, row, row, vec],
        out_specs=[pl.BlockSpec((8, LANE), lambda i: (0, 0)), row, row, vec],
        out_shape=[
            jax.ShapeDtypeStruct((8, LANE), F32),
            jax.ShapeDtypeStruct((S, D), F32),
            jax.ShapeDtypeStruct((S, D), BF16),
            jax.ShapeDtypeStruct((1, D), F32),
        ],
        compiler_params=_params(("arbitrary",)),
    )(out, x, tgt, g_post)


def _pre_bwd(dh, x, dy, g_pre, ts=256):
    n = S // ts

    def body(dh_ref, x_ref, dy_ref, g_ref, gx_ref, dg_ref):
        i = pl.program_id(0)

        @pl.when(i == 0)
        def _():
            dg_ref[...] = jnp.zeros_like(dg_ref)

        xv = x_ref[...]
        dhv = dh_ref[...]
        inv = lax.rsqrt(jnp.mean(xv * xv, axis=-1, keepdims=True) + EPS)
        xn = xv * inv
        dg_ref[...] += jnp.sum(dhv * xn, axis=0, keepdims=True)
        dxn = dhv * g_ref[...]
        gx_ref[...] = dy_ref[...] + inv * (dxn - xn * jnp.mean(dxn * xn, axis=-1, keepdims=True))

    row = pl.BlockSpec((ts, D), lambda i: (i, 0))
    vec = pl.BlockSpec((1, D), lambda i: (0, 0))
    return pl.pallas_call(
        body,
        name="pre_bwd",
        grid=(n,),
        in_specs=[row, row, row, vec],
        out_specs=[row, vec],
        out_shape=[jax.ShapeDtypeStruct((S, D), F32), jax.ShapeDtypeStruct((1, D), F32)],
        compiler_params=_params(("arbitrary",)),
    )(dh, x, dy, g_pre)


def _memnorm_bwd(dmemn, mem):
    def body(d_ref, m_ref, dg_ref):
        mv = m_ref[...]
        inv = lax.rsqrt(jnp.mean(mv * mv, axis=-1, keepdims=True) + EPS)
        dg_ref[...] = jnp.sum(d_ref[...] * mv * inv, axis=0, keepdims=True)

    return pl.pallas_call(
        body,
        name="memnorm_bwd",
        out_shape=jax.ShapeDtypeStruct((1, D), F32),
        compiler_params=_params(),
    )(dmemn, mem)


T_RNN = 256


def _neg_expm1(z):
    poly = -z * (1.0 + z * (0.5 + z * (1.0 / 6 + z * (1.0 / 24 + z * (1.0 / 120 + z * (1.0 / 720))))))
    return jnp.where(z > -0.1, poly, 1.0 - jnp.exp(z))


def _softplus_neg(lam):
    return jnp.maximum(-lam, 0.0) + jnp.log1p(jnp.exp(-jnp.abs(lam)))


def _rnn_gates(conv, wa_ref, ba, wx_ref, bx, lam, first_row):
    cbf = conv.astype(BF16)
    ga, gx = [], []
    for n in range(RNN_BLOCKS):
        c_n = cbf[:, n * LANE:(n + 1) * LANE]
        ga.append(jnp.dot(c_n, wa_ref[n], preferred_element_type=F32))
        gx.append(jnp.dot(c_n, wx_ref[n], preferred_element_type=F32))
    gate_r = _sigmoid(jnp.concatenate(ga, axis=1) + ba)
    gate_i = _sigmoid(jnp.concatenate(gx, axis=1) + bx)
    sp = _softplus_neg(lam)
    log_a = -LRU_C * gate_r * sp
    a = jnp.exp(log_a)
    mult_raw = jnp.sqrt(_neg_expm1(2.0 * log_a))
    mult = jnp.where(first_row, 1.0, mult_raw)
    return cbf, gate_r, gate_i, sp, a, mult_raw, mult


def _rglru_fwd(p_a, conv_w, conv_b, wa, ba, wx, bx, lam):
    t = T_RNN
    n = S // t

    def body(xr_ref, g_ref, cw_ref, cb_ref, wa_ref, ba_ref, wx_ref, bx_ref, lam_ref,
             y_ref, h_ref, xp_s, hcar, a_s, b_s):
        i = pl.program_id(0)

        @pl.when(i == 0)
        def _():
            xp_s[0:8, :] = jnp.zeros((8, D_RNN), F32)
            hcar[...] = jnp.zeros_like(hcar)

        @pl.when(i > 0)
        def _():
            xp_s[0:8, :] = xp_s[t:t + 8, :]

        xp_s[8:8 + t, :] = xr_ref[...]
        conv = cb_ref[...]
        for k in range(CONV_W):
            conv = conv + cw_ref[k:k + 1, :] * xp_s[8 - k:8 - k + t, :]
        rows = i * t + lax.broadcasted_iota(jnp.int32, (t, 1), 0)
        _, _, gate_i, _, a, _, mult = _rnn_gates(
            conv, wa_ref, ba_ref[...], wx_ref, bx_ref[...], lam_ref[...], rows == 0)
        a_s[...] = a
        b_s[...] = mult * gate_i * conv

        def step(tt, h):
            h = a_s[pl.ds(tt, 1), :] * h + b_s[pl.ds(tt, 1), :]
            h_ref[pl.ds(tt, 1), :] = h
            return h

        hcar[...] = lax.fori_loop(0, t, step, hcar[...], unroll=8)
        g = g_ref[...]
        y_ref[...] = (h_ref[...] * (g * _sigmoid(g))).astype(BF16)

    blk = lambda c: pl.BlockSpec((t, D_RNN), lambda i: (i, c))
    full = lambda shape: pl.BlockSpec(shape, lambda i: (0,) * len(shape))
    return pl.pallas_call(
        body,
        name="rglru_fwd",
        grid=(n,),
        in_specs=[blk(0), blk(1), full((CONV_W, D_RNN)), full((1, D_RNN)),
                  full((RNN_BLOCKS, LANE, LANE)), full((1, D_RNN)),
                  full((RNN_BLOCKS, LANE, LANE)), full((1, D_RNN)), full((1, D_RNN))],
        out_specs=[blk(0), blk(0)],
        out_shape=[jax.ShapeDtypeStruct((S, D_RNN), BF16), jax.ShapeDtypeStruct((S, D_RNN), F32)],
        scratch_shapes=[pltpu.VMEM((t + 8, D_RNN), F32), pltpu.VMEM((1, D_RNN), F32),
                        pltpu.VMEM((t, D_RNN), F32), pltpu.VMEM((t, D_RNN), F32)],
        compiler_params=_params(("arbitrary",)),
    )(p_a, p_a, conv_w, conv_b, wa, ba, wx, bx, lam)


def _rglru_bwd(dy, p_a, hseq, conv_w, conv_b, wa, ba, wx, bx, lam):
    t = T_RNN
    n = S // t
    rb = t // 8

    def body(dy_ref, xr_ref, g_ref, h_ref, xrp_ref, hp_ref, cw_ref, cb_ref, wa_ref, ba_ref, wx_ref, bx_ref, lam_ref,
             dp_ref, dcw_ref, dcb_ref, dwa_ref, dba_ref, dwx_ref, dbx_ref, dlam_ref,
             xp_s, hp_s, dxp_s, lamcar, a_s, dh_s, lam_s):
        i = pl.program_id(0)
        r = n - 1 - i

        @pl.when(i == 0)
        def _():
            for ref in (dcw_ref, dcb_ref, dwa_ref, dba_ref, dwx_ref, dbx_ref, dlam_ref, lamcar):
                ref[...] = jnp.zeros_like(ref)
            dxp_s[t:t + 8, :] = jnp.zeros((8, D_RNN), F32)

        @pl.when(i > 0)
        def _():
            dxp_s[t:t + 8, :] = dxp_s[0:8, :]

        has_prev = r > 0
        xp_s[0:8, :] = jnp.where(has_prev, xrp_ref[...], 0.0)
        xp_s[8:8 + t, :] = xr_ref[...]
        hp_s[0:8, :] = jnp.where(has_prev, hp_ref[...], 0.0)
        hp_s[8:8 + t, :] = h_ref[...]
        xs = [xp_s[8 - k:8 - k + t, :] for k in range(CONV_W)]
        conv = cb_ref[...]
        for k in range(CONV_W):
            conv = conv + cw_ref[k:k + 1, :] * xs[k]
        rows = r * t + lax.broadcasted_iota(jnp.int32, (t, 1), 0)
        first = rows == 0
        lam_p = lam_ref[...]
        cbf, gate_r, gate_i, sp, a, mult_raw, mult = _rnn_gates(
            conv, wa_ref, ba_ref[...], wx_ref, bx_ref[...], lam_p, first)

        g = g_ref[...]
        sg = _sigmoid(g)
        dyv = dy_ref[...]
        a_s[...] = a
        dh_s[...] = dyv * (g * sg)
        dg = dyv * h_ref[...] * (sg * (1.0 + g * (1.0 - sg)))

        def step(jj, car):
            tt = t - 1 - jj
            lm = dh_s[pl.ds(tt, 1), :] + car
            lam_s[pl.ds(tt, 1), :] = lm
            return a_s[pl.ds(tt, 1), :] * lm

        lamcar[...] = lax.fori_loop(0, t, step, lamcar[...], unroll=8)
        db = lam_s[...]
        da = db * hp_s[7:7 + t, :]
        dmult = db * gate_i * conv
        dgate_i = db * mult * conv
        dconv = db * mult * gate_i
        dlog_a = da * a + jnp.where(first, 0.0, dmult * (-(a * a) / mult_raw))
        dgate_r = dlog_a * (-LRU_C * sp)
        dsp = jnp.sum(dlog_a * (-LRU_C * gate_r), axis=0, keepdims=True)
        dlam_ref[...] += dsp * (-_sigmoid(-lam_p))
        dga = dgate_r * gate_r * (1.0 - gate_r)
        dgx = dgate_i * gate_i * (1.0 - gate_i)
        dba_ref[...] += jnp.sum(dga, axis=0, keepdims=True)
        dbx_ref[...] += jnp.sum(dgx, axis=0, keepdims=True)
        dga16, dgx16 = dga.astype(BF16), dgx.astype(BF16)
        back = []
        for nb in range(RNN_BLOCKS):
            sl = slice(nb * LANE, (nb + 1) * LANE)
            dwa_ref[nb] += lax.dot_general(cbf[:, sl], dga16[:, sl], _DIMS["tn"], preferred_element_type=F32)
            dwx_ref[nb] += lax.dot_general(cbf[:, sl], dgx16[:, sl], _DIMS["tn"], preferred_element_type=F32)
            back.append(lax.dot_general(dga16[:, sl], wa_ref[nb], _DIMS["nt"], preferred_element_type=F32)
                        + lax.dot_general(dgx16[:, sl], wx_ref[nb], _DIMS["nt"], preferred_element_type=F32))
        dconv = dconv + jnp.concatenate(back, axis=1)
        dcb_ref[...] += jnp.sum(dconv, axis=0, keepdims=True)
        for k in range(CONV_W):
            dcw_ref[k:k + 1, :] += jnp.sum(dconv * xs[k], axis=0, keepdims=True)
        dxp_s[0:t, :] = dconv
        dxr = cw_ref[0:1, :] * dconv
        for k in range(1, CONV_W):
            dxr = dxr + cw_ref[k:k + 1, :] * dxp_s[k:k + t, :]
        dp_ref[:, 0:D_RNN] = dxr.astype(BF16)
        dp_ref[:, D_RNN:2 * D_RNN] = dg.astype(BF16)

    blk = lambda c: pl.BlockSpec((t, D_RNN), lambda i: (n - 1 - i, c))
    prev8 = pl.BlockSpec((8, D_RNN), lambda i: (jnp.maximum((n - 1 - i) * rb - 1, 0), 0))
    full = lambda shape: pl.BlockSpec(shape, lambda i: (0,) * len(shape))
    vec = full((1, D_RNN))
    mat = full((RNN_BLOCKS, LANE, LANE))
    return pl.pallas_call(
        body,
        name="rglru_bwd",
        grid=(n,),
        in_specs=[blk(0), blk(0), blk(1), blk(0), prev8, prev8,
                  full((CONV_W, D_RNN)), vec, mat, vec, mat, vec, vec],
        out_specs=[pl.BlockSpec((t, 2 * D_RNN), lambda i: (n - 1 - i, 0)),
                   full((CONV_W, D_RNN)), vec, mat, vec, mat, vec, vec],
        out_shape=[jax.ShapeDtypeStruct((S, 2 * D_RNN), BF16),
                   jax.ShapeDtypeStruct((CONV_W, D_RNN), F32), jax.ShapeDtypeStruct((1, D_RNN), F32),
                   jax.ShapeDtypeStruct((RNN_BLOCKS, LANE, LANE), F32), jax.ShapeDtypeStruct((1, D_RNN), F32),
                   jax.ShapeDtypeStruct((RNN_BLOCKS, LANE, LANE), F32), jax.ShapeDtypeStruct((1, D_RNN), F32),
                   jax.ShapeDtypeStruct((1, D_RNN), F32)],
        scratch_shapes=[pltpu.VMEM((t + 8, D_RNN), F32), pltpu.VMEM((t + 8, D_RNN), F32),
                        pltpu.VMEM((t + 8, D_RNN), F32), pltpu.VMEM((1, D_RNN), F32),
                        pltpu.VMEM((t, D_RNN), F32), pltpu.VMEM((t, D_RNN), F32), pltpu.VMEM((t, D_RNN), F32)],
        compiler_params=_params(("arbitrary",)),
    )(dy, p_a, p_a, hseq, p_a, hseq, conv_w, conv_b, wa, ba, wx, bx, lam)


QB = WINDOW
KB2 = 2 * WINDOW
N_QB = S // QB
N_PAIR = SWA_HEADS // 2


def _swa_keys(kvc_ref, kvp_ref):
    kk = jnp.concatenate([kvp_ref[:, 0:LANE], kvc_ref[:, 0:LANE]], axis=0)
    vv = jnp.concatenate([kvp_ref[:, LANE:2 * LANE], kvc_ref[:, LANE:2 * LANE]], axis=0)
    lo = lax.broadcasted_iota(jnp.int32, (1, LANE), 1) < SWA_HD
    kk_sw, vv_sw = pltpu.roll(kk, SWA_HD, 1), pltpu.roll(vv, SWA_HD, 1)
    kd = [jnp.where(lo, kk, kk_sw).astype(BF16), jnp.where(lo, kk_sw, kk).astype(BF16)]
    vd = [jnp.where(lo, vv, vv_sw).astype(BF16), jnp.where(lo, vv_sw, vv).astype(BF16)]
    return lo, kd, vd


GRP = SWA_HEADS // 2
STACK = GRP
GQ = STACK * QB


def _swa_valid(n, rows):
    qi = lax.broadcasted_iota(jnp.int32, (rows, KB2), 0) % QB
    kj = lax.broadcasted_iota(jnp.int32, (rows, KB2), 1)
    dist = qi + WINDOW - kj
    return (dist >= 0) & (dist < WINDOW) & ((n > 0) | (kj >= WINDOW))


def _swa_stack(tile_of, lo, h0, masked):
    parts = []
    for h in range(h0, h0 + STACK):
        t = tile_of(h // 2)
        if masked:
            t = jnp.where(lo if h % 2 == 0 else jnp.logical_not(lo), t, 0.0)
        parts.append(t)
    return jnp.concatenate(parts, axis=0)


def _swa_unstack(stacked, lo, pair):
    return jnp.where(lo, stacked[2 * pair * QB:(2 * pair + 1) * QB], stacked[(2 * pair + 1) * QB:(2 * pair + 2) * QB])


def _swa_softmax(lg, sink, valid):
    lg = jnp.where(valid, lg, NEG_INF)
    m = jnp.maximum(jnp.max(lg, axis=-1, keepdims=True), sink)
    p = jnp.exp(lg - m)
    es = jnp.exp(sink - m)
    den = jnp.sum(p, axis=-1, keepdims=True) + es
    return p / den, es / den


def _swa_probs_head(qh16, kd, bias, sink, valid):
    lg = lax.dot_general(qh16, kd, _DIMS["nt"], preferred_element_type=F32) * (SWA_HD ** -0.5) + bias
    return _swa_softmax(lg, sink, valid)[0]


def _swa_probs(q16, kd, bias_ref, sink_ref, h0, valid):
    bias = bias_ref[h0:h0 + STACK].reshape(GQ, KB2)
    sink = jnp.concatenate([jnp.full((QB, 1), sink_ref[h], F32) for h in range(h0, h0 + STACK)], axis=0)
    lg = lax.dot_general(q16, kd, _DIMS["nt"], preferred_element_type=F32) * (SWA_HD ** -0.5) + bias
    return _swa_softmax(lg, sink, valid)


def _swa_specs():
    q = pl.BlockSpec((QB, D_RNN), lambda n: (n, 0))
    g = pl.BlockSpec((QB, D_RNN), lambda n: (n, 1))
    kvc = pl.BlockSpec((QB, 2 * LANE), lambda n: (n, 8))
    kvp = pl.BlockSpec((QB, 2 * LANE), lambda n: (jnp.maximum(n - 1, 0), 8))
    bias = pl.BlockSpec((SWA_HEADS, QB, KB2), lambda n: (0, 0, 0))
    sinks = pl.BlockSpec(memory_space=pltpu.SMEM)
    return q, g, kvc, kvp, bias, sinks


def _swa_fwd(p_b, bias_t, sinks):
    def body(q_ref, g_ref, kvc_ref, kvp_ref, bias_ref, sink_ref, y_ref, o_ref):
        n = pl.program_id(0)
        lo, kd, vd = _swa_keys(kvc_ref, kvp_ref)
        valid = _swa_valid(n, QB)
        for hp in range(N_PAIR):
            sl = slice(hp * LANE, (hp + 1) * LANE)
            kvh = hp // (N_PAIR // 2)
            q = q_ref[:, sl]
            outs = []
            for j in range(2):
                qh16 = jnp.where(lo if j == 0 else jnp.logical_not(lo), q, 0.0).astype(BF16)
                probs = _swa_probs_head(qh16, kd[kvh], bias_ref[2 * hp + j], sink_ref[2 * hp + j], valid)
                outs.append(jnp.dot(probs.astype(BF16), vd[kvh], preferred_element_type=F32))
            o = jnp.where(lo, outs[0], outs[1])
            o_ref[:, sl] = o
            g = g_ref[:, sl]
            y_ref[:, sl] = (o * (g * _sigmoid(g))).astype(BF16)

    q, g, kvc, kvp, bias, sinks_spec = _swa_specs()
    out = pl.BlockSpec((QB, D_RNN), lambda n: (n, 0))
    return pl.pallas_call(
        body,
        name="swa_fwd",
        grid=(N_QB,),
        in_specs=[q, g, kvc, kvp, bias, sinks_spec],
        out_specs=[out, out],
        out_shape=[jax.ShapeDtypeStruct((S, D_RNN), BF16), jax.ShapeDtypeStruct((S, D_RNN), F32)],
        compiler_params=_params(("parallel",)),
    )(p_b, p_b, p_b, p_b, bias_t, sinks)


def _swa_bwd(dy, p_b, o_swa, bias_t, sinks, after=None):
    def body(dy_ref, q_ref, g_ref, kvc_ref, kvp_ref, o_ref, bias_ref, sink_ref, *rest):
        dp_ref, dk_ref, dv_ref, dbias_ref, dsink_ref, do_s = rest[-6:]
        n = pl.program_id(0)

        @pl.when(n == 0)
        def _():
            for ref in (dk_ref, dv_ref, dbias_ref, dsink_ref):
                ref[...] = jnp.zeros_like(ref)

        lo, kd, vd = _swa_keys(kvc_ref, kvp_ref)
        hi = jnp.logical_not(lo)
        valid = _swa_valid(n, GQ)
        tile = lambda ref: (lambda hp: ref[:, hp * LANE:(hp + 1) * LANE])
        for hp in range(N_PAIR):
            sl = slice(hp * LANE, (hp + 1) * LANE)
            g, dyv = g_ref[:, sl], dy_ref[:, sl]
            sg = _sigmoid(g)
            do_s[:, sl] = dyv * (g * sg)
            dp_ref[:, D_RNN + hp * LANE:D_RNN + (hp + 1) * LANE] = (
                dyv * o_ref[:, sl] * (sg * (1.0 + g * (1.0 - sg)))).astype(BF16)

        dk_blk = jnp.zeros((KB2, LANE), F32)
        dv_blk = jnp.zeros((KB2, LANE), F32)
        for h0 in range(0, SWA_HEADS, STACK):
            kvh = h0 // GRP
            q16 = _swa_stack(tile(q_ref), lo, h0, masked=True).astype(BF16)
            do8 = _swa_stack(tile(do_s), lo, h0, masked=True)
            do16 = do8.astype(BF16)
            delta = jnp.sum(do8 * _swa_stack(tile(o_ref), lo, h0, masked=False), axis=-1, keepdims=True)
            probs, psink = _swa_probs(q16, kd[kvh], bias_ref, sink_ref, h0, valid)
            dpr = lax.dot_general(do16, vd[kvh], _DIMS["nt"], preferred_element_type=F32)
            ds = probs * (dpr - delta)
            sink_term = psink * delta
            for g in range(STACK):
                h, rows = h0 + g, slice(g * QB, (g + 1) * QB)
                dbias_ref[h] += ds[rows]
                dsink_ref[h:h + 1, :] += jnp.zeros((1, LANE), F32) - jnp.sum(sink_term[rows])
            ds16 = (ds * (SWA_HD ** -0.5)).astype(BF16)
            dq_all = jnp.dot(ds16, kd[kvh], preferred_element_type=F32)
            for pair in range(STACK // 2):
                sl = slice((h0 // 2 + pair) * LANE, (h0 // 2 + pair + 1) * LANE)
                dp_ref[:, sl] = _swa_unstack(dq_all, lo, pair).astype(BF16)
            dk_pair = lax.dot_general(ds16, q16, _DIMS["tn"], preferred_element_type=F32)
            dv_pair = lax.dot_general(probs.astype(BF16), do16, _DIMS["tn"], preferred_element_type=F32)
            keep = lo if kvh == 0 else hi
            dk_blk = dk_blk + jnp.where(keep, dk_pair + pltpu.roll(dk_pair, SWA_HD, 1), 0.0)
            dv_blk = dv_blk + jnp.where(keep, dv_pair + pltpu.roll(dv_pair, SWA_HD, 1), 0.0)

        cur = pl.ds(pl.multiple_of(n * QB, QB), QB)
        dk_ref[cur, :] += dk_blk[QB:KB2]
        dv_ref[cur, :] += dv_blk[QB:KB2]

        @pl.when(n > 0)
        def _():
            prev = pl.ds(pl.multiple_of((n - 1) * QB, QB), QB)
            dk_ref[prev, :] += dk_blk[0:QB]
            dv_ref[prev, :] += dv_blk[0:QB]

    q, g, kvc, kvp, bias, sinks_spec = _swa_specs()
    row = pl.BlockSpec((QB, D_RNN), lambda n: (n, 0))
    acc = pl.BlockSpec((S, LANE), lambda n: (0, 0))
    return pl.pallas_call(
        body,
        name="swa_bwd",
        grid=(N_QB,),
        in_specs=[row, q, g, kvc, kvp, row, bias, sinks_spec] + ([ANY] if after is not None else []),
        out_specs=[pl.BlockSpec((QB, 2 * D_RNN), lambda n: (n, 0)), acc, acc, bias,
                   pl.BlockSpec((SWA_HEADS, LANE), lambda n: (0, 0))],
        out_shape=[jax.ShapeDtypeStruct((S, GROUP_TILES["B"] * LANE), BF16),
                   jax.ShapeDtypeStruct((S, LANE), F32), jax.ShapeDtypeStruct((S, LANE), F32),
                   jax.ShapeDtypeStruct((SWA_HEADS, QB, KB2), F32),
                   jax.ShapeDtypeStruct((SWA_HEADS, LANE), F32)],
        scratch_shapes=[pltpu.VMEM((QB, D_RNN), F32)],
        compiler_params=_params(("arbitrary",)),
    )(dy, p_b, p_b, p_b, p_b, o_swa, bias_t, sinks, *([after] if after is not None else []))


def _swa_pack(dp_b, dk, dv, ts=512):
    def body(_, dk_ref, dv_ref, o_ref):
        o_ref[:, 0:LANE] = dk_ref[...].astype(BF16)
        o_ref[:, LANE:2 * LANE] = dv_ref[...].astype(BF16)

    tile = pl.BlockSpec((ts, LANE), lambda i: (i, 0))
    return pl.pallas_call(
        body,
        name="swa_pack",
        grid=(S // ts,),
        in_specs=[pl.BlockSpec(memory_space=pl.ANY), tile, tile],
        out_specs=pl.BlockSpec((ts, 2 * LANE), lambda i: (i, 8)),
        out_shape=jax.ShapeDtypeStruct(dp_b.shape, dp_b.dtype),
        input_output_aliases={0: 0},
        compiler_params=_params(("parallel",)),
    )(dp_b, dk, dv)


def _split3(v):
    a = v.astype(BF16)
    r = v - a.astype(F32)
    b = r.astype(BF16)
    c = (r - b.astype(F32)).astype(BF16)
    return a, b, c


def _relbias_grad(dbias_flat, onehot_t):
    def body(d_ref, e_ref, o_ref):
        e = e_ref[...]
        acc = jnp.zeros((SWA_HEADS, REL_BUCKETS), F32)
        for term in _split3(d_ref[...]):
            acc = acc + lax.dot_general(term, e, _DIMS["nt"], preferred_element_type=F32)
        o_ref[...] = acc

    return pl.pallas_call(
        body,
        name="relbias_grad",
        out_shape=jax.ShapeDtypeStruct((SWA_HEADS, REL_BUCKETS), F32),
        compiler_params=_params(),
    )(dbias_flat, onehot_t)


TS_MEM = 512


def _mem_probs(q16, mk):
    lg = lax.dot_general(q16, mk, _DIMS["nt"], preferred_element_type=F32) * (MEM_HD ** -0.5)
    p = jnp.exp(lg - jnp.max(lg, axis=-1, keepdims=True))
    return p / jnp.sum(p, axis=-1, keepdims=True)


def _mem_fwd(p_c, mkv):
    def body(q_ref, g_ref, mkv_ref, y_ref, o_ref):
        for hm in range(MEM_HEADS):
            sl = slice(hm * MEM_HD, (hm + 1) * MEM_HD)
            probs = _mem_probs(q_ref[:, sl].astype(BF16), mkv_ref[:, sl])
            o = jnp.dot(probs.astype(BF16), mkv_ref[:, D_RNN + hm * MEM_HD:D_RNN + (hm + 1) * MEM_HD],
                        preferred_element_type=F32)
            o_ref[:, sl] = o
            g = g_ref[:, sl]
            y_ref[:, sl] = (o * (g * _sigmoid(g))).astype(BF16)

    blk = lambda c: pl.BlockSpec((TS_MEM, D_RNN), lambda i: (i, c))
    return pl.pallas_call(
        body,
        name="mem_fwd",
        grid=(S // TS_MEM,),
        in_specs=[blk(0), blk(1), pl.BlockSpec((MEM, 2 * D_RNN), lambda i: (0, 0))],
        out_specs=[blk(0), blk(0)],
        out_shape=[jax.ShapeDtypeStruct((S, D_RNN), BF16), jax.ShapeDtypeStruct((S, D_RNN), F32)],
        compiler_params=_params(("parallel",)),
    )(p_c, p_c, mkv)


def _mem_bwd(dy, p_c, o_mem, mkv):
    def body(dy_ref, q_ref, g_ref, o_ref, mkv_ref, dp_ref, dmkv_ref):
        @pl.when(pl.program_id(0) == 0)
        def _():
            dmkv_ref[...] = jnp.zeros_like(dmkv_ref)

        for hm in range(MEM_HEADS):
            sl = slice(hm * MEM_HD, (hm + 1) * MEM_HD)
            sv = slice(D_RNN + hm * MEM_HD, D_RNN + (hm + 1) * MEM_HD)
            q16 = q_ref[:, sl].astype(BF16)
            mk, mv = mkv_ref[:, sl], mkv_ref[:, sv]
            probs = _mem_probs(q16, mk)
            g, o, dyv = g_ref[:, sl], o_ref[:, sl], dy_ref[:, sl]
            sg = _sigmoid(g)
            do = dyv * (g * sg)
            dp_ref[:, sv] = (dyv * o * (sg * (1.0 + g * (1.0 - sg)))).astype(BF16)
            do16 = do.astype(BF16)
            delta = jnp.sum(do * o, axis=-1, keepdims=True)
            dpr = lax.dot_general(do16, mv, _DIMS["nt"], preferred_element_type=F32)
            ds16 = (probs * (dpr - delta) * (MEM_HD ** -0.5)).astype(BF16)
            dp_ref[:, sl] = jnp.dot(ds16, mk, preferred_element_type=F32).astype(BF16)
            dmkv_ref[:, sl] += lax.dot_general(ds16, q16, _DIMS["tn"], preferred_element_type=F32)
            dmkv_ref[:, sv] += lax.dot_general(probs.astype(BF16), do16, _DIMS["tn"], preferred_element_type=F32)

    blk = lambda c: pl.BlockSpec((TS_MEM, D_RNN), lambda i: (i, c))
    kv = pl.BlockSpec((MEM, 2 * D_RNN), lambda i: (0, 0))
    return pl.pallas_call(
        body,
        name="mem_bwd",
        grid=(S // TS_MEM,),
        in_specs=[blk(0), blk(0), blk(1), blk(0), kv],
        out_specs=[pl.BlockSpec((TS_MEM, 2 * D_RNN), lambda i: (i, 0)), kv],
        out_shape=[jax.ShapeDtypeStruct((S, 2 * D_RNN), BF16), jax.ShapeDtypeStruct((MEM, 2 * D_RNN), F32)],
        compiler_params=_params(("arbitrary",)),
    )(dy, p_c, p_c, o_mem, mkv)


TS_MRG = 512
TD_MRG = 1024
N_DBLK = D // TD_MRG


def _merge_fwd(z, p_d):
    def body(z0, z1, z2, g0, g1, g2, o_ref):
        o_ref[...] = (_sigmoid(g0[...]) * z0[...] + _sigmoid(g1[...]) * z1[...]
                      + _sigmoid(g2[...]) * z2[...]).astype(BF16)

    blk = pl.BlockSpec((TS_MRG, TD_MRG), lambda i, d: (i, d))
    gate = lambda b: pl.BlockSpec((TS_MRG, TD_MRG), lambda i, d: (i, b * N_DBLK + d))
    return pl.pallas_call(
        body,
        name="merge_fwd",
        grid=(S // TS_MRG, N_DBLK),
        in_specs=[blk, blk, blk, gate(0), gate(1), gate(2)],
        out_specs=blk,
        out_shape=jax.ShapeDtypeStruct((S, D), BF16),
        compiler_params=_params(("parallel", "parallel")),
    )(z[0], z[1], z[2], p_d, p_d, p_d)


def _merge_bwd(dmerged, z_b, p_d, b, dp_d, after=None):
    def body(dm_ref, z_ref, g_ref, *refs):
        dz_ref, dg_ref = refs[-2], refs[-1]
        sg = _sigmoid(g_ref[...])
        dm = dm_ref[...]
        dz_ref[...] = (dm * sg).astype(BF16)
        dg_ref[...] = (dm * z_ref[...] * sg * (1.0 - sg)).astype(BF16)

    blk = pl.BlockSpec((TS_MRG, TD_MRG), lambda i, d: (i, d))
    gate = pl.BlockSpec((TS_MRG, TD_MRG), lambda i, d: (i, b * N_DBLK + d))
    in_specs = [blk, blk, gate]
    args = [dmerged, z_b, p_d]
    aliases = {}
    if dp_d is not None:
        in_specs.append(pl.BlockSpec(memory_space=pl.ANY))
        args.append(dp_d)
        aliases = {3: 1}
    if after is not None:
        in_specs.append(pl.BlockSpec(memory_space=pl.ANY))
        args.append(after)
    return pl.pallas_call(
        body,
        name=f"merge_bwd{b}",
        grid=(S // TS_MRG, N_DBLK),
        in_specs=in_specs,
        out_specs=[blk, gate],
        out_shape=[jax.ShapeDtypeStruct((S, D), BF16),
                   jax.ShapeDtypeStruct((S, GROUP_TILES["D"] * LANE), BF16)],
        input_output_aliases=aliases,
        compiler_params=_params(("parallel", "parallel")),
    )(*args)


def _bucket_table():
    import numpy as np
    qi = np.arange(QB)[:, None]
    kj = np.arange(KB2)[None, :]
    n = np.maximum(qi + WINDOW - kj, 0)
    max_exact = REL_BUCKETS // 2
    ratio = np.log(np.maximum(n, 1).astype(np.float32) / max_exact) / np.float32(math.log(REL_MAX_DIST / max_exact))
    large = np.minimum(max_exact + (ratio * (REL_BUCKETS - max_exact)).astype(np.int32), REL_BUCKETS - 1)
    bucket = np.where(n < max_exact, n, large).reshape(1, QB * KB2)
    return (bucket == np.arange(REL_BUCKETS)[:, None]).astype(np.float32)


def _bias_expand(rel_bias_t, onehot_t):
    def body(r_ref, e_ref, o_ref):
        e = e_ref[...]
        acc = jnp.zeros((SWA_HEADS, QB * KB2), F32)
        for term in _split3(r_ref[...]):
            acc = acc + jnp.dot(term, e, preferred_element_type=F32)
        o_ref[...] = acc

    return pl.pallas_call(
        body,
        name="bias_expand",
        out_shape=jax.ShapeDtypeStruct((SWA_HEADS, QB * KB2), F32),
        compiler_params=_params(),
    )(rel_bias_t, onehot_t)


PROJ_TN = {"A": 1024, "B": 1152, "C": 1024, "D": 1536}


def _do_first(arrays, token):
    def body(*refs):
        refs[-1][...] = jnp.zeros_like(refs[-1])

    return pl.pallas_call(
        body,
        name="do_first",
        in_specs=[pl.BlockSpec(memory_space=pl.ANY)] * (len(arrays) + 1),
        out_specs=pl.BlockSpec(memory_space=pltpu.VMEM),
        out_shape=jax.ShapeDtypeStruct((8, LANE), F32),
    )(*arrays, token)


def _local_step(x, h, mem, tgt, sp, early, fetch, prefetch, emit, advance):
    onehot_t = jnp.asarray(_bucket_table(), BF16)
    bias_t = _bias_expand(sp["rel_bias"].T, onehot_t).reshape(SWA_HEADS, QB, KB2)
    sinks = sp["swa_sinks"].reshape(SWA_HEADS)
    wa16, wx16 = sp["w_rg_a"].astype(BF16), sp["w_rg_x"].astype(BF16)
    rnn = (sp["conv_w"], sp["conv_b"], wa16, sp["b_rg_a"], wx16, sp["b_rg_x"], sp["lru_lambda"])

    memn = _rms_fwd(mem, sp["mem_norm_g"], "rms_mem", h)
    h_and_prep = _do_first([bias_t, memn, wa16, wx16, *early], h)
    w_grp, p = {}, {}

    def project(g, after, then=None):
        (w_grp[g],) = fetch((g,), after)
        tok = prefetch(then, w_grp[g]) if then is not None else None
        p[g] = _mm(h, w_grp[g], "nt", F32, 1024, PROJ_TN[g], D, f"proj_{g}", after=tok)

    project("A", h_and_prep)
    y_rg, hseq = _rglru_fwd(p["A"], *rnn)
    project("B", y_rg)
    y_swa, o_swa = _swa_fwd(p["B"], bias_t, sinks)
    project("C", y_swa, then=("mk",))
    (wmk,) = fetch(("mk",), p["C"])
    tok = prefetch(("br0", "br1", "br2"), wmk)
    mkv = _mm(memn, wmk, "nn", BF16, MEM, 1024, D, "mkv", after=tok)
    y_mem, o_mem = _mem_fwd(p["C"], mkv)
    ys = (y_rg, y_swa, y_mem)
    wbr = fetch(("br0", "br1", "br2"), y_mem)
    tok = prefetch(("D",), wbr[2])
    z = []
    for b in range(3):
        z.append(_mm(ys[b], wbr[b], "nn", F32, 1024, 1024, D_RNN, f"branch_out{b}", after=z[-1] if z else tok))
    project("D", z[2], then=("out",))
    merged = _merge_fwd(z, p["D"])
    (wout,) = fetch(("out",), merged)
    out = _mm(merged, wout, "nn", F32, 1024, 1024, D, "out_proj")
    sq, dy, dout, d_post = _post_loss(out, x, tgt, sp["post_norm_g"])

    tok = emit({"out": _mm(merged, dout, "tn", BF16, 1024, 1024, S, "d_wout")})
    dmerged = _mm(dout, wout, "nt", F32, 1024, 1024, D, "d_merged", after=tok)
    dz, dp_d = [], None
    tok = advance(dmerged)
    for b in range(3):
        dz_b, dp_d = _merge_bwd(dmerged, z[b], p["D"], b, dp_d, after=tok if b == 0 else None)
        dz.append(dz_b)
    d_win = lambda g, dp_g, after=None: _mm(dp_g, h, "tn", BF16, PROJ_TN[g], 1024, S, f"d_win_{g}", after=after)
    tok = emit({f"br{b}": _mm(ys[b], dz[b], "tn", BF16, 1024, 1024, S, f"d_wbr{b}") for b in range(3)}, tok)
    d_w_d = d_win("D", dp_d, tok)
    tok = emit({"D": d_w_d}, advance(d_w_d))
    dy_mem = _mm(dz[2], wbr[2], "nt", F32, 1024, 1024, D, "d_branch2", after=tok)
    tok = advance(dy_mem)
    dp_c, dmkv = _mem_bwd(dy_mem, p["C"], o_mem, mkv)
    dmkv16 = dmkv.astype(BF16)
    tok = emit({"mk": _mm(memn, dmkv16, "tn", BF16, 1024, 1024, MEM, "d_wmk", after=tok), "C": d_win("C", dp_c)}, tok)
    dmemn = _mm(dmkv16, wmk, "nt", F32, MEM, 1024, D, "d_memn", after=tok)
    tok = advance(dmemn)
    d_memg = _memnorm_bwd(dmemn, mem)
    dy_rg = _mm(dz[0], wbr[0], "nt", F32, 1024, 1024, D, "d_branch0", after=tok)
    dp_a, d_cw, d_cb, d_wa, d_ba, d_wx, d_bx, d_lam = _rglru_bwd(dy_rg, p["A"], hseq, *rnn)
    tok = emit({"A": d_win("A", dp_a)}, tok)
    dy_swa = _mm(dz[1], wbr[1], "nt", F32, 1024, 1024, D, "d_branch1", after=tok)
    tok = advance(dy_swa)
    dp_b, dk, dv, d_bias, d_sink = _swa_bwd(dy_swa, p["B"], o_swa, bias_t, sinks, after=tok)
    dp_b = _swa_pack(dp_b, dk, dv)
    d_rel = _relbias_grad(d_bias.reshape(SWA_HEADS, QB * KB2), onehot_t).T
    dp = {"A": dp_a, "B": dp_b, "C": dp_c, "D": dp_d}
    tok = emit({"B": d_win("B", dp_b)}, tok)
    dh = None
    for g in GROUPS:
        dh = _mm(dp[g], w_grp[g], "nn", F32, 1024, 1024, 2304 if g == "B" else 2048, f"d_h_{g}", acc=dh,
                 after=tok if g in ("A", "B") else None)
        if g == "A":
            tok = advance(dh)
    grad_x, d_pre = _pre_bwd(dh, x, dy, sp["pre_norm_g"])

    d_small = {
        "pre_norm_g": d_pre, "post_norm_g": d_post, "mem_norm_g": d_memg, "conv_w": d_cw, "conv_b": d_cb,
        "w_rg_a": d_wa, "b_rg_a": d_ba, "w_rg_x": d_wx, "b_rg_x": d_bx, "lru_lambda": d_lam,
        "swa_sinks": d_sink[:, 0].reshape(1, SWA_HEADS), "rel_bias": d_rel,
    }
    return sq, grad_x, d_small


ANY = pl.BlockSpec(memory_space=pl.ANY)
SHARD_ROWS = D // N_CHIPS
GATHERED = {"A": (2048, D), "B": (2304, D), "C": (2048, D), "D": (6144, D), "mk": (D, D),
            "br0": (D_RNN, D), "br1": (D_RNN, D), "br2": (D_RNN, D), "out": (D, D)}
SHARD_SHAPES = {"win": (SHARD, D), "mk": (SHARD_ROWS, D), "br0": (D_RNN, SHARD_ROWS), "br1": (D_RNN, SHARD_ROWS),
                "br2": (D_RNN, SHARD_ROWS), "out": (SHARD_ROWS, D)}
SHARDS = tuple(SHARD_SHAPES)
HALF_AXIS = {"win": 1, "mk": 1, "br0": 0, "br1": 0, "br2": 0, "out": 1,
             "A": 1, "B": 1, "C": 1, "D": 1}


def _halved(shape, axis):
    return (shape[0] // 2, shape[1]) if axis == 0 else (shape[0], shape[1] // 2)


class Piece(NamedTuple):
    src: str
    dst: str
    rows: int
    sr0: int
    sc0: int
    dr0: int
    dc0: int
    ncols: int


def _pieces_of(jj):
    out = [Piece("win", g, n, r, 0, gr, 0, D) for r, n, g, gr in _shard_runs(jj)]
    out.append(Piece("mk", "mk", SHARD_ROWS, 0, 0, SHARD_ROWS * jj, 0, D))
    out += [Piece(f"br{b}", f"br{b}", D_RNN, 0, 0, 0, SHARD_ROWS * jj, SHARD_ROWS) for b in range(3)]
    out.append(Piece("out", "out", SHARD_ROWS, 0, 0, SHARD_ROWS * jj, 0, D))
    return out


def _half_rect(ref, p, side, which):
    r0, c0 = (p.sr0, p.sc0) if side == "src" else (p.dr0, p.dc0)
    if HALF_AXIS[p.src] == 1:
        return _rect(ref, r0, p.rows, c0 + which * (p.ncols // 2), p.ncols // 2)
    return _rect(ref, r0 + which * (p.rows // 2), p.rows // 2, c0, p.ncols)


def _rect_in_half(ref, p, side):
    r0, c0 = (p.sr0, p.sc0) if side == "src" else (p.dr0, p.dc0)
    if HALF_AXIS[p.src] == 1:
        return _rect(ref, r0, p.rows, 0, p.ncols // 2)
    return _rect(ref, 0, p.rows // 2, c0, p.ncols)


MAX_PIECES = max(len(_pieces_of(jj)) for jj in range(N_CHIPS))


def _rect(ref, r0, rows, c0, ncols):
    return ref.at[pl.ds(r0, rows), pl.ds(c0, ncols)]


def _position():
    x, y, c = lax.axis_index("x"), lax.axis_index("y"), lax.axis_index("c")
    return x, y, c, 2 * x + y


HBM = pl.BlockSpec(memory_space=pltpu.HBM)
SEM = pl.BlockSpec(memory_space=pltpu.SEMAPHORE)
EFFECT = pltpu.SideEffectType.DATAFLOW_SIDE_EFFECTING
N_SEM = MAX_PIECES * N_CHIPS
GATHER_STAGES = (("A",), ("B",), ("C",), ("mk",), ("br0", "br1", "br2"), ("D",), ("out",))


def _in_hbm(a):
    return pltpu.with_memory_space_constraint(a, pltpu.HBM)


def _stage_pieces(jj, stage):
    return [(i, p) for i, p in enumerate(_pieces_of(jj)) if p.dst in stage]


def _own_block_table(g):
    import numpy as np
    units = np.full((N_CHIPS, GATHERED[g][0] // HALF_TILE), -1, np.int64)
    for jj in range(N_CHIPS):
        for r, n, grp, gr in _shard_runs(jj):
            if grp == g:
                for k in range(n // HALF_TILE):
                    units[jj, gr // HALF_TILE + k] = r // HALF_TILE + k
    tbl = np.zeros((N_CHIPS, 2, GATHERED[g][0] // LANE), np.int32)
    for jj in range(N_CHIPS):
        for b in range(tbl.shape[2]):
            first, second = units[jj, 2 * b], units[jj, 2 * b + 1]
            if jj % 2 == 0:
                src = first if first >= 0 else second - 1
                if first >= 0 or second >= 0:
                    assert src % 2 == 0
                    tbl[jj, :, b] = src // 2
            else:
                if first >= 0:
                    assert first % 2 == 1
                    tbl[jj, 0, b] = first // 2
                if second >= 0:
                    assert second % 2 == 0
                    tbl[jj, 1, b] = second // 2
    return tbl


def _place_group(w_t, g, tables, odd_arr, after):
    nb = GATHERED[g][0] // LANE

    def body(t_ref, odd_ref, a_ref, b_ref, _, o_ref):
        odd = odd_ref[0] == 1
        o_ref[0:HALF_TILE, :] = jnp.where(odd, a_ref[HALF_TILE:LANE, :], a_ref[0:HALF_TILE, :]).astype(BF16)
        o_ref[HALF_TILE:LANE, :] = jnp.where(odd, b_ref[0:HALF_TILE, :], a_ref[HALF_TILE:LANE, :]).astype(BF16)

    return pl.pallas_call(
        body,
        name=f"place_{g}",
        grid_spec=pltpu.PrefetchScalarGridSpec(
            num_scalar_prefetch=2,
            grid=(nb,),
            in_specs=[pl.BlockSpec((LANE, D), lambda b, t, o: (t[0, b], 0)),
                      pl.BlockSpec((LANE, D), lambda b, t, o: (t[1, b], 0)), ANY],
            out_specs=pl.BlockSpec((LANE, D), lambda b, t, o: (b, 0)),
        ),
        out_shape=jax.ShapeDtypeStruct(GATHERED[g], BF16),
        compiler_params=_params(("parallel",)),
    )(tables, odd_arr, w_t, w_t, after)


def _place_shard(shard, name, after):
    rows, cols = shard.shape
    by_rows = HALF_AXIS[name] == 1

    def body(x_ref, _, o_ref):
        o_ref[...] = x_ref[...].astype(BF16)

    return pl.pallas_call(
        body,
        name=f"place_{name}",
        grid=(N_CHIPS,),
        in_specs=[pl.BlockSpec((rows, cols), lambda b: (0, 0)), ANY],
        out_specs=pl.BlockSpec((rows, cols), (lambda b: (b, 0)) if by_rows else (lambda b: (0, b))),
        out_shape=jax.ShapeDtypeStruct(GATHERED[name], BF16),
        compiler_params=_params(("parallel",)),
    )(shard, after)


def _gather_copy(arr, send_sems, recv_sems, c, jj, i, p, kk):
    rect = _half_rect(arr[p.dst], p, "dst", c)
    return pltpu.make_async_remote_copy(
        src_ref=rect, dst_ref=rect, send_sem=send_sems.at[i * N_CHIPS + kk],
        recv_sem=recv_sems.at[jj * MAX_PIECES + i], device_id=(kk // 2, kk % 2, c), device_id_type=MESH)


def _gather_start(arrays, after):
    stage = tuple(arrays)
    na = len(stage)

    def body(*refs):
        arr = dict(zip(stage, refs[:na]))
        send_sems, recv_sems = refs[na + 1], refs[na + 2]
        token = refs[-1]
        _, _, c, j = _position()
        for jj in range(N_CHIPS):
            @pl.when(j == jj)
            def _():
                for i, p in _stage_pieces(jj, stage):
                    for kk in range(N_CHIPS):
                        if kk != jj:
                            _gather_copy(arr, send_sems, recv_sems, c, jj, i, p, kk).start()
        token[...] = jnp.zeros_like(token)

    outs = pl.pallas_call(
        body,
        name=f"gather_start_{stage[0]}",
        in_specs=[HBM] * na + [ANY],
        out_specs=[SEM, SEM] + [HBM] * na + [pl.BlockSpec(memory_space=pltpu.VMEM)],
        out_shape=[pltpu.SemaphoreType.DMA((N_SEM,)), pltpu.SemaphoreType.DMA((N_SEM,))]
        + [pltpu.HBM(GATHERED[n], BF16) for n in stage] + [jax.ShapeDtypeStruct((8, LANE), F32)],
        input_output_aliases={k: 2 + k for k in range(na)},
        compiler_params=pltpu.CompilerParams(has_side_effects=EFFECT),
    )(*[_in_hbm(arrays[n]) for n in stage], after)
    return outs[0], outs[1], dict(zip(stage, outs[2:2 + na])), outs[-1]


def _gather_wait(send_sems, recv_sems, arrays, after):
    stage = tuple(arrays)
    na = len(stage)

    def body(*refs):
        arr = dict(zip(stage, refs[:na]))
        sems_s, sems_r = refs[na], refs[na + 1]
        _, _, c, j = _position()
        for jj in range(N_CHIPS):
            @pl.when(j != jj)
            def _():
                for i, p in _stage_pieces(jj, stage):
                    _gather_copy(arr, sems_s, sems_r, c, jj, i, p, jj).wait_recv()

            @pl.when(j == jj)
            def _():
                for i, p in _stage_pieces(jj, stage):
                    for kk in range(N_CHIPS):
                        if kk != jj:
                            _gather_copy(arr, sems_s, sems_r, c, jj, i, p, kk).wait_send()

    outs = pl.pallas_call(
        body,
        name=f"gather_wait_{stage[0]}",
        in_specs=[HBM] * na + [SEM, SEM, ANY],
        out_specs=[HBM] * na,
        out_shape=[pltpu.HBM(GATHERED[n], BF16) for n in stage],
        input_output_aliases={k: k for k in range(na)},
        compiler_params=pltpu.CompilerParams(has_side_effects=EFFECT),
    )(*[arrays[n] for n in stage], send_sems, recv_sems, after)
    return dict(zip(stage, outs))


def _gather_swap(arrays):
    stage = tuple(arrays)
    na = len(stage)

    def body(*refs):
        dst = dict(zip(stage, refs[na:2 * na]))
        send_sems, recv_sems = refs[2 * na:]
        x, y, c, j = _position()

        def fwd(jj, i, p, which):
            rect = _half_rect(dst[p.dst], p, "dst", which)
            return pltpu.make_async_remote_copy(
                src_ref=rect, dst_ref=rect, send_sem=send_sems.at[jj * MAX_PIECES + i],
                recv_sem=recv_sems.at[jj * MAX_PIECES + i], device_id=(x, y, 1 - c), device_id_type=MESH)

        for jj in range(N_CHIPS):
            @pl.when(j != jj)
            def _():
                for i, p in _stage_pieces(jj, stage):
                    fwd(jj, i, p, c).start()
        for jj in range(N_CHIPS):
            @pl.when(j != jj)
            def _():
                for i, p in _stage_pieces(jj, stage):
                    fwd(jj, i, p, 1 - c).wait_recv()
        for jj in range(N_CHIPS):
            @pl.when(j != jj)
            def _():
                for i, p in _stage_pieces(jj, stage):
                    fwd(jj, i, p, c).wait_send()

    outs = pl.pallas_call(
        body,
        name=f"gather_swap_{stage[0]}",
        in_specs=[ANY] * na,
        out_specs=[ANY] * na,
        out_shape=[jax.ShapeDtypeStruct(GATHERED[n], BF16) for n in stage],
        input_output_aliases={k: k for k in range(na)},
        scratch_shapes=[pltpu.SemaphoreType.DMA((N_SEM,)), pltpu.SemaphoreType.DMA((N_SEM,))],
        compiler_params=pltpu.CompilerParams(has_side_effects=True),
    )(*[arrays[n] for n in stage])
    return dict(zip(stage, outs))


def _pass_on_copy(arr, send_sems, recv_sems, x, y, c, jj, i, p, which):
    rect = _half_rect(arr[p.dst], p, "dst", which)
    return pltpu.make_async_remote_copy(
        src_ref=rect, dst_ref=rect, send_sem=send_sems.at[jj * MAX_PIECES + i],
        recv_sem=recv_sems.at[jj * MAX_PIECES + i], device_id=(x, y, 1 - c), device_id_type=MESH)


def _gather_pass_start(arrays, after):
    stage = tuple(arrays)
    na = len(stage)

    def body(*refs):
        arr = dict(zip(stage, refs[:na]))
        x, y, c, j = _position()
        for jj in range(N_CHIPS):
            @pl.when(j != jj)
            def _():
                for i, p in _stage_pieces(jj, stage):
                    _pass_on_copy(arr, refs[na + 1], refs[na + 2], x, y, c, jj, i, p, c).start()
        refs[-1][...] = jnp.zeros_like(refs[-1])

    outs = pl.pallas_call(
        body,
        name=f"gather_pass_start_{stage[0]}",
        in_specs=[HBM] * na + [ANY],
        out_specs=[SEM, SEM] + [HBM] * na + [pl.BlockSpec(memory_space=pltpu.VMEM)],
        out_shape=[pltpu.SemaphoreType.DMA((N_SEM,)), pltpu.SemaphoreType.DMA((N_SEM,))]
        + [pltpu.HBM(GATHERED[n], BF16) for n in stage] + [jax.ShapeDtypeStruct((8, LANE), F32)],
        input_output_aliases={k: 2 + k for k in range(na)},
        compiler_params=pltpu.CompilerParams(has_side_effects=EFFECT),
    )(*[arrays[n] for n in stage], after)
    return outs[0], outs[1], dict(zip(stage, outs[2:2 + na])), outs[-1]


def _gather_pass_wait(send_sems, recv_sems, arrays, after):
    stage = tuple(arrays)
    na = len(stage)

    def body(*refs):
        arr = dict(zip(stage, refs[:na]))
        x, y, c, j = _position()
        for jj in range(N_CHIPS):
            @pl.when(j != jj)
            def _():
                for i, p in _stage_pieces(jj, stage):
                    _pass_on_copy(arr, refs[na], refs[na + 1], x, y, c, jj, i, p, 1 - c).wait_recv()
                    _pass_on_copy(arr, refs[na], refs[na + 1], x, y, c, jj, i, p, c).wait_send()

    outs = pl.pallas_call(
        body,
        name=f"gather_pass_wait_{stage[0]}",
        in_specs=[HBM] * na + [SEM, SEM, ANY],
        out_specs=[HBM] * na,
        out_shape=[pltpu.HBM(GATHERED[n], BF16) for n in stage],
        input_output_aliases={k: k for k in range(na)},
        compiler_params=pltpu.CompilerParams(has_side_effects=EFFECT),
    )(*[arrays[n] for n in stage], send_sems, recv_sems, after)
    return dict(zip(stage, outs))


def _own_half(ref, shape, axis, which):
    if axis == 1:
        return ref.at[:, pl.ds(which * (shape[1] // 2), shape[1] // 2)]
    return ref.at[pl.ds(which * (shape[0] // 2), shape[0] // 2), :]


def _swap_copies(names, src, dst, send_sems, recv_sems):
    x, y, c, _ = _position()
    return [pltpu.make_async_remote_copy(
        src_ref=_own_half(src[n], GATHERED[n], HALF_AXIS[n], 1 - c), dst_ref=dst[n],
        send_sem=send_sems.at[k], recv_sem=recv_sems.at[k],
        device_id=(x, y, 1 - c), device_id_type=MESH) for k, n in enumerate(names)]


def _swap_start(grads, after):
    names = tuple(grads)
    n = len(names)

    def body(*refs):
        src, dst = dict(zip(names, refs[:n])), dict(zip(names, refs[n:2 * n]))
        for cp in _swap_copies(names, src, dst, refs[2 * n + 1], refs[2 * n + 2]):
            cp.start()
        refs[-1][...] = jnp.zeros_like(refs[-1])

    half_shape = lambda nm: _halved(GATHERED[nm], HALF_AXIS[nm])
    args = [_in_hbm(grads[nm]) for nm in names] + [_in_hbm(lax.empty(half_shape(nm), BF16)) for nm in names]
    if after is None:
        after = jnp.zeros((8, LANE), F32)
    outs = pl.pallas_call(
        body,
        name=f"swap_start_{names[0]}",
        in_specs=[HBM] * (2 * n) + [ANY],
        out_specs=[SEM, SEM] + [HBM] * (2 * n) + [pl.BlockSpec(memory_space=pltpu.VMEM)],
        out_shape=[pltpu.SemaphoreType.DMA((n,)), pltpu.SemaphoreType.DMA((n,))]
        + [pltpu.HBM(GATHERED[nm], BF16) for nm in names] + [pltpu.HBM(half_shape(nm), BF16) for nm in names]
        + [jax.ShapeDtypeStruct((8, LANE), F32)],
        input_output_aliases={k: 2 + k for k in range(2 * n)},
        compiler_params=pltpu.CompilerParams(has_side_effects=EFFECT),
    )(*args, after)
    return outs[0], outs[1], dict(zip(names, outs[2:2 + n])), dict(zip(names, outs[2 + n:2 + 2 * n])), outs[-1]


def _swap_wait(send_sems, recv_sems, grads, landing, after):
    names = tuple(grads)
    n = len(names)

    def body(*refs):
        src, dst = dict(zip(names, refs[:n])), dict(zip(names, refs[n:2 * n]))
        copies = _swap_copies(names, src, dst, refs[2 * n], refs[2 * n + 1])
        for cp in copies:
            cp.wait_recv()
        for cp in copies:
            cp.wait_send()

    half_shape = lambda nm: _halved(GATHERED[nm], HALF_AXIS[nm])
    outs = pl.pallas_call(
        body,
        name=f"swap_wait_{names[0]}",
        in_specs=[HBM] * (2 * n) + [SEM, SEM, ANY],
        out_specs=[HBM] * (2 * n),
        out_shape=[pltpu.HBM(GATHERED[nm], BF16) for nm in names] + [pltpu.HBM(half_shape(nm), BF16) for nm in names],
        input_output_aliases={k: k for k in range(2 * n)},
        compiler_params=pltpu.CompilerParams(has_side_effects=EFFECT),
    )(*[grads[nm] for nm in names], *[landing[nm] for nm in names], send_sems, recv_sems, after)
    return dict(zip(names, outs[:n])), dict(zip(names, outs[n:]))


ADD_ROWS = {"A": 1024, "B": 768, "C": 1024, "D": 1536, "mk": 1024, "br0": 512, "br1": 512, "br2": 512, "out": 1024}


def _add_half(full, recv, c_arr, name):
    rows, cols = recv.shape
    tr = ADD_ROWS[name]
    if HALF_AXIS[name] == 1:
        index = lambda i, c_ref: (i, c_ref[0])
    else:
        nb = rows // tr
        index = lambda i, c_ref: (nb * c_ref[0] + i, 0)

    def body(c_ref, a_ref, b_ref, o_ref):
        o_ref[...] = (a_ref[...].astype(F32) + b_ref[...].astype(F32)).astype(BF16)

    return pl.pallas_call(
        body,
        name=f"add_half_{name}",
        grid_spec=pltpu.PrefetchScalarGridSpec(
            num_scalar_prefetch=1,
            grid=(rows // tr,),
            in_specs=[pl.BlockSpec((tr, cols), index), pl.BlockSpec((tr, cols), lambda i, c_ref: (i, 0))],
            out_specs=pl.BlockSpec((tr, cols), lambda i, c_ref: (i, 0)),
        ),
        out_shape=jax.ShapeDtypeStruct((rows, cols), BF16),
        compiler_params=_params(("parallel",)),
    )(c_arr, full, recv)


SLOT_SHAPES = {n: _halved(SHARD_SHAPES[n], HALF_AXIS[n]) for n in SHARDS}


def _slot_shape(n):
    return (N_CHIPS,) + SLOT_SHAPES[n]


def _stage_shards(stage):
    pieces = [p for jj in range(N_CHIPS) for p in _pieces_of(jj)]
    return tuple(s for s in SHARDS if any(p.src == s and p.dst in stage for p in pieces))


def _scatter_copy(src, dst, send_sems, recv_sems, c, jj, kk, i, p):
    return pltpu.make_async_remote_copy(
        src_ref=_rect_in_half(src[p.dst], p, "dst"), dst_ref=_rect_in_half(dst[p.src].at[jj], p, "src"),
        send_sem=send_sems.at[kk * MAX_PIECES + i], recv_sem=recv_sems.at[jj * MAX_PIECES + i],
        device_id=(kk // 2, kk % 2, c), device_id_type=MESH)


def _scatter_start(halves, slots):
    stage, touched = tuple(halves), tuple(slots)
    nh, nt = len(stage), len(touched)

    def body(*refs):
        src = dict(zip(stage, refs[:nh]))
        dst = dict(zip(touched, refs[nh:nh + nt]))
        send_sems, recv_sems = refs[nh + nt], refs[nh + nt + 1]
        token = refs[-1]
        _, _, c, j = _position()
        for jj in range(N_CHIPS):
            @pl.when(j == jj)
            def _():
                for kk in range(N_CHIPS):
                    if kk != jj:
                        for i, p in _stage_pieces(kk, stage):
                            _scatter_copy(src, dst, send_sems, recv_sems, c, jj, kk, i, p).start()
        token[...] = jnp.zeros_like(token)

    outs = pl.pallas_call(
        body,
        name=f"scatter_start_{stage[0]}",
        in_specs=[HBM] * (nh + nt),
        out_specs=[SEM, SEM] + [HBM] * (nh + nt) + [pl.BlockSpec(memory_space=pltpu.VMEM)],
        out_shape=[pltpu.SemaphoreType.DMA((N_SEM,)), pltpu.SemaphoreType.DMA((N_SEM,))]
        + [pltpu.HBM(halves[n].shape, BF16) for n in stage] + [pltpu.HBM(_slot_shape(s), BF16) for s in touched]
        + [jax.ShapeDtypeStruct((8, LANE), F32)],
        input_output_aliases={k: 2 + k for k in range(nh + nt)},
        compiler_params=pltpu.CompilerParams(has_side_effects=EFFECT),
    )(*[_in_hbm(halves[n]) for n in stage], *[_in_hbm(slots[s]) for s in touched])
    return outs[0], outs[1], dict(zip(stage, outs[2:2 + nh])), dict(zip(touched, outs[2 + nh:2 + nh + nt])), outs[-1]


def _scatter_wait(send_sems, recv_sems, halves, slots, after):
    stage, touched = tuple(halves), tuple(slots)
    nh, nt = len(stage), len(touched)

    def body(*refs):
        src = dict(zip(stage, refs[:nh]))
        dst = dict(zip(touched, refs[nh:nh + nt]))
        sems_s, sems_r = refs[nh + nt], refs[nh + nt + 1]
        _, _, c, j = _position()
        for jj in range(N_CHIPS):
            @pl.when(j == jj)
            def _():
                for ss in range(N_CHIPS):
                    if ss != jj:
                        for i, p in _stage_pieces(jj, stage):
                            _scatter_copy(src, dst, sems_s, sems_r, c, ss, jj, i, p).wait_recv()
                for kk in range(N_CHIPS):
                    if kk != jj:
                        for i, p in _stage_pieces(kk, stage):
                            _scatter_copy(src, dst, sems_s, sems_r, c, jj, kk, i, p).wait_send()

    outs = pl.pallas_call(
        body,
        name=f"scatter_wait_{stage[0]}",
        in_specs=[HBM] * (nh + nt) + [SEM, SEM, ANY],
        out_specs=[HBM] * (nh + nt),
        out_shape=[pltpu.HBM(halves[n].shape, BF16) for n in stage] + [pltpu.HBM(_slot_shape(s), BF16) for s in touched],
        input_output_aliases={k: k for k in range(nh + nt)},
        compiler_params=pltpu.CompilerParams(has_side_effects=EFFECT),
    )(*[halves[n] for n in stage], *[slots[s] for s in touched], send_sems, recv_sems, after)
    return dict(zip(stage, outs[:nh])), dict(zip(touched, outs[nh:]))


SUM_ROWS = {"mk": 512, "br0": 512, "br1": 512, "br2": 512, "out": 512}


def _sum_in_chip_order(chip, own, s_ref):
    acc = None
    for k in range(N_CHIPS):
        term = jnp.where(chip == k, own, s_ref[k].astype(F32))
        acc = term if acc is None else acc + term
    return acc


def _sum_slots(slots, own_half, pos_arr, name):
    _, rows, cols = slots.shape
    tr = SUM_ROWS[name]
    nb = rows // tr
    if HALF_AXIS[name] == 1:
        own_index = lambda i, pos: (nb * pos[1] + i, 0)
        out_index = lambda i, pos: (i, pos[0])
    else:
        own_index = lambda i, pos: (i, pos[1])
        out_index = lambda i, pos: (nb * pos[0] + i, 0)

    def body(pos, s_ref, own_ref, o_ref):
        o_ref[...] = _sum_in_chip_order(pos[1], own_ref[...].astype(F32), s_ref)

    return pl.pallas_call(
        body,
        name=f"sum_slots_{name}",
        grid_spec=pltpu.PrefetchScalarGridSpec(
            num_scalar_prefetch=1,
            grid=(nb,),
            in_specs=[pl.BlockSpec((N_CHIPS, tr, cols), lambda i, pos: (0, i, 0)),
                      pl.BlockSpec((tr, cols), own_index)],
            out_specs=pl.BlockSpec((tr, cols), out_index),
        ),
        out_shape=jax.ShapeDtypeStruct(SHARD_SHAPES[name], F32),
        compiler_params=_params(("parallel",)),
    )(pos_arr, slots, own_half)


def _own_partial_tables():
    import numpy as np
    nb = SHARD // HALF_TILE
    grp, blk = np.zeros((N_CHIPS, nb), np.int32), np.zeros((N_CHIPS, nb), np.int32)
    for jj in range(N_CHIPS):
        for r, n, g, gr in _shard_runs(jj):
            for k in range(n // HALF_TILE):
                grp[jj, r // HALF_TILE + k] = GROUPS.index(g)
                blk[jj, r // HALF_TILE + k] = gr // HALF_TILE + k
    return grp, blk


def _sum_slots_win(slots, own_halves, pos_arr, grp_tbl, blk_tbl):
    nb = SHARD // HALF_TILE
    cols = D // 2

    def own_spec(gi):
        return pl.BlockSpec((HALF_TILE, cols), lambda b, pos, grp, blk: (jnp.where(grp[b] == gi, blk[b], 0), 0))

    def body(pos, grp, blk, s_ref, a_ref, b_ref, c_ref, d_ref, o_ref):
        g = grp[pl.program_id(0)]
        own = a_ref[...]
        for gi, ref in ((1, b_ref), (2, c_ref), (3, d_ref)):
            own = jnp.where(g == gi, ref[...], own)
        o_ref[...] = _sum_in_chip_order(pos[1], own.astype(F32), s_ref)

    return pl.pallas_call(
        body,
        name="sum_slots_win",
        grid_spec=pltpu.PrefetchScalarGridSpec(
            num_scalar_prefetch=3,
            grid=(nb,),
            in_specs=[pl.BlockSpec((N_CHIPS, HALF_TILE, cols), lambda b, pos, grp, blk: (0, b, 0))]
            + [own_spec(gi) for gi in range(len(GROUPS))],
            out_specs=pl.BlockSpec((HALF_TILE, cols), lambda b, pos, grp, blk: (b, pos[0])),
        ),
        out_shape=jax.ShapeDtypeStruct(SHARD_SHAPES["win"], F32),
        compiler_params=_params(("parallel",)),
    )(pos_arr, grp_tbl, blk_tbl, slots, *[own_halves[g] for g in GROUPS])


def _share_copy(buf, name, send_sems, recv_sems, k, which):
    x, y, c, _ = _position()
    half = _own_half(buf, SHARD_SHAPES[name], HALF_AXIS[name], which)
    return pltpu.make_async_remote_copy(src_ref=half, dst_ref=half, send_sem=send_sems.at[k], recv_sem=recv_sems.at[k],
                                        device_id=(x, y, 1 - c), device_id_type=MESH)


def _share_start(sums, after):
    names = tuple(sums)
    n = len(names)

    def body(*refs):
        _, _, c, _ = _position()
        for k, nm in enumerate(names):
            _share_copy(refs[k], nm, refs[n + 1], refs[n + 2], k, c).start()
        refs[-1][...] = jnp.zeros_like(refs[-1])

    outs = pl.pallas_call(
        body,
        name=f"share_start_{names[0]}",
        in_specs=[HBM] * n + [ANY],
        out_specs=[SEM, SEM] + [HBM] * n + [pl.BlockSpec(memory_space=pltpu.VMEM)],
        out_shape=[pltpu.SemaphoreType.DMA((n,)), pltpu.SemaphoreType.DMA((n,))]
        + [pltpu.HBM(SHARD_SHAPES[nm], F32) for nm in names] + [jax.ShapeDtypeStruct((8, LANE), F32)],
        input_output_aliases={k: 2 + k for k in range(n)},
        compiler_params=pltpu.CompilerParams(has_side_effects=EFFECT),
    )(*[_in_hbm(sums[nm]) for nm in names], after)
    return outs[0], outs[1], dict(zip(names, outs[2:2 + n])), outs[-1]


def _share_wait(send_sems, recv_sems, sums, after):
    names = tuple(sums)
    n = len(names)

    def body(*refs):
        _, _, c, _ = _position()
        for k, nm in enumerate(names):
            _share_copy(refs[k], nm, refs[n], refs[n + 1], k, 1 - c).wait_recv()
            _share_copy(refs[k], nm, refs[n], refs[n + 1], k, c).wait_send()

    outs = pl.pallas_call(
        body,
        name=f"share_wait_{names[0]}",
        in_specs=[HBM] * n + [SEM, SEM, ANY],
        out_specs=[HBM] * n,
        out_shape=[pltpu.HBM(SHARD_SHAPES[nm], F32) for nm in names],
        input_output_aliases={k: k for k in range(n)},
        compiler_params=pltpu.CompilerParams(has_side_effects=EFFECT),
    )(*[sums[nm] for nm in names], send_sems, recv_sems, after)
    return dict(zip(names, outs))


def _all_reduce_small(pack, name):
    rows = pack.shape[0]
    half = rows // 2

    def body(p_ref, o_ref, sib, land, sems):
        x, y, c, j = _position()
        sibling = (x, y, 1 - c)
        swap = pltpu.make_async_remote_copy(src_ref=p_ref, dst_ref=sib, send_sem=sems.at[0], recv_sem=sems.at[1],
                                            device_id=sibling, device_id_type=MESH)
        swap.start()
        swap.wait_recv()
        land[j] = p_ref[...] + sib[...]

        def mine(k, which):
            return land.at[k, pl.ds(which * half, half)]

        def ici(kk):
            return pltpu.make_async_remote_copy(
                src_ref=mine(j, c), dst_ref=mine(j, c), send_sem=sems.at[2 + kk], recv_sem=sems.at[6 + j],
                device_id=(kk // 2, kk % 2, c), device_id_type=MESH)

        def arrival(kk):
            return pltpu.make_async_remote_copy(
                src_ref=mine(kk, c), dst_ref=mine(kk, c), send_sem=sems.at[2 + kk], recv_sem=sems.at[6 + kk],
                device_id=(kk // 2, kk % 2, c), device_id_type=MESH)

        def passed_on(kk, which):
            return pltpu.make_async_remote_copy(
                src_ref=mine(kk, which), dst_ref=mine(kk, which), send_sem=sems.at[10 + kk],
                recv_sem=sems.at[14 + kk], device_id=sibling, device_id_type=MESH)

        for kk in range(N_CHIPS):
            @pl.when(j != kk)
            def _():
                ici(kk).start()
        for kk in range(N_CHIPS):
            @pl.when(j != kk)
            def _():
                arrival(kk).wait_recv()
                passed_on(kk, c).start()
        for kk in range(N_CHIPS):
            @pl.when(j != kk)
            def _():
                passed_on(kk, 1 - c).wait_recv()
        acc = land[0]
        for kk in range(1, N_CHIPS):
            acc = acc + land[kk]
        o_ref[...] = acc
        swap.wait_send()
        for kk in range(N_CHIPS):
            @pl.when(j != kk)
            def _():
                ici(kk).wait_send()
                passed_on(kk, c).wait_send()

    vmem = pl.BlockSpec(memory_space=pltpu.VMEM)
    return pl.pallas_call(
        body,
        name=name,
        in_specs=[vmem],
        out_specs=vmem,
        out_shape=jax.ShapeDtypeStruct((rows, LANE), F32),
        scratch_shapes=[pltpu.VMEM((rows, LANE), F32), pltpu.VMEM((N_CHIPS, rows, LANE), F32),
                        pltpu.SemaphoreType.DMA((18,))],
        compiler_params=pltpu.CompilerParams(has_side_effects=True, vmem_limit_bytes=VMEM_LIMIT),
    )(pack)


ADAM_ROWS = {"win": 224, "mk": 256, "br0": 512, "br1": 512, "br2": 512, "out": 256}


def _adamw(w, g, m, v, name, tr):
    rows, cols = w.shape
    tr = min(tr, rows)

    def body(w_ref, g_ref, m_ref, v_ref, go_ref, d_ref, nm_ref, nv_ref):
        gv = g_ref[...]
        go_ref[...] = gv
        nm = ADAM_B1 * m_ref[...] + (1.0 - ADAM_B1) * gv
        nv = ADAM_B2 * v_ref[...] + (1.0 - ADAM_B2) * (gv * gv)
        nm_ref[...] = nm
        nv_ref[...] = nv
        m_hat = nm / (1.0 - ADAM_B1 ** ADAM_STEP)
        v_hat = nv / (1.0 - ADAM_B2 ** ADAM_STEP)
        d_ref[...] = -ADAM_LR * (m_hat / (jnp.sqrt(v_hat) + ADAM_EPS) + ADAM_WD * w_ref[...])

    blk = pl.BlockSpec((tr, cols), lambda i: (i, 0))
    shape = jax.ShapeDtypeStruct((rows, cols), F32)
    return pl.pallas_call(
        body,
        name=f"adamw_{name}",
        grid=(rows // tr,),
        in_specs=[blk] * 4,
        out_specs=[blk] * 4,
        out_shape=[shape] * 4,
        compiler_params=_params(("parallel",)),
    )(w, g, m, v)


SMALL = (("pre_norm_g", (1, D)), ("post_norm_g", (1, D)), ("mem_norm_g", (1, D)), ("conv_w", (CONV_W, D_RNN)),
         ("conv_b", (1, D_RNN)), ("w_rg_a", (RNN_BLOCKS, LANE, LANE)), ("b_rg_a", (1, D_RNN)),
         ("w_rg_x", (RNN_BLOCKS, LANE, LANE)), ("b_rg_x", (1, D_RNN)), ("lru_lambda", (1, D_RNN)),
         ("swa_sinks", (1, SWA_HEADS)), ("rel_bias", (REL_BUCKETS, SWA_HEADS)))
PACK_ROWS = 2176


def _slot_len(shape):
    return -(-math.prod(shape) // LANE) * LANE


def _pack(values, last_row=None):
    parts = []
    for name, shape in SMALL:
        flat = values[name].reshape(-1).astype(F32)
        parts.append(jnp.pad(flat, (0, _slot_len(shape) - flat.shape[0])))
    flat = jnp.concatenate(parts)
    tail = jnp.zeros((LANE,), F32) if last_row is None else last_row
    return jnp.concatenate([jnp.pad(flat, (0, (PACK_ROWS - 1) * LANE - flat.shape[0])), tail]).reshape(PACK_ROWS, LANE)


def _unpack(pack):
    flat = pack.reshape(-1)
    out, off = {}, 0
    for name, shape in SMALL:
        out[name] = flat[off:off + math.prod(shape)].reshape(shape)
        off += _slot_len(shape)
    return out


TWIN_WEIGHTS = ("pre_norm_g", "post_norm_g", "mem_norm_g", "w_in", "conv_w", "conv_b", "w_rg_a", "b_rg_a", "w_rg_x",
                "b_rg_x", "lru_lambda", "swa_sinks", "rel_bias", "w_mem_kv", "w_br_rg", "w_br_swa", "w_br_mem", "w_out")
BIG = {"w_in": "win", "w_mem_kv": "mk", "w_br_rg": "br0", "w_br_swa": "br1", "w_br_mem": "br2", "w_out": "out"}


def kernel(x, mem, pre_norm_g, post_norm_g, mem_norm_g, w_in, conv_w, conv_b, w_rg_a, b_rg_a, w_rg_x, b_rg_x, lru_lambda, swa_sinks, rel_bias, w_mem_kv, w_br_rg, w_br_swa, w_br_mem, w_out, loss_target, m_pre_norm_g, m_post_norm_g, m_mem_norm_g, m_w_in, m_conv_w, m_conv_b, m_w_rg_a, m_b_rg_a, m_w_rg_x, m_b_rg_x, m_lru_lambda, m_swa_sinks, m_rel_bias, m_w_mem_kv, m_w_br_rg, m_w_br_swa, m_w_br_mem, m_w_out, v_pre_norm_g, v_post_norm_g, v_mem_norm_g, v_w_in, v_conv_w, v_conv_b, v_w_rg_a, v_b_rg_a, v_w_rg_x, v_b_rg_x, v_lru_lambda, v_swa_sinks, v_rel_bias, v_w_mem_kv, v_w_br_rg, v_w_br_swa, v_w_br_mem, v_w_out):
    args = dict(locals())
    out_shapes = {n: args[n].shape for n in TWIN_WEIGHTS}
    w = {n: (args[n] if n == "rel_bias" else args[n][0]) for n in TWIN_WEIGHTS}
    m = {n: (args["m_" + n] if n == "rel_bias" else args["m_" + n][0]) for n in TWIN_WEIGHTS}
    v = {n: (args["v_" + n] if n == "rel_bias" else args["v_" + n][0]) for n in TWIN_WEIGHTS}
    for d in (w, m, v):
        for n, shape in SMALL:
            if n != "conv_w":
                d[n] = d[n].reshape(shape)

    xi, yi, ci = lax.axis_index("x"), lax.axis_index("y"), lax.axis_index("c")
    chip = 2 * xi + yi
    c_arr = ci.astype(jnp.int32).reshape(1)
    zero = jnp.zeros((), jnp.int32)
    cw0 = (chip * (D_RNN // N_CHIPS)).astype(jnp.int32)

    placed = lax.dynamic_update_slice(jnp.zeros((CONV_W, D_RNN), F32), w["conv_w"], (zero, cw0))
    placed = jnp.where(ci == 0, placed, 0.0).reshape(CONV_W * D_RNN // LANE, LANE)
    conv_w_full = _all_reduce_small(placed, "gather_conv_w").reshape(CONV_W, D_RNN)

    for d in (w, m, v):
        d["w_in"] = d["w_in"].T
    chip_row = lambda tbl: lax.dynamic_slice(jnp.asarray(tbl), (chip.astype(jnp.int32), zero), (1, tbl.shape[1]))[0]
    chip_tables = lambda tbl: lax.dynamic_slice(jnp.asarray(tbl), (chip.astype(jnp.int32), zero, zero),
                                                (1,) + tbl.shape[1:])[0]
    odd_arr = yi.astype(jnp.int32).reshape(1)
    big_of = {s: n for n, s in BIG.items()}
    ag, token = {}, conv_w_full
    for stage in GATHER_STAGES:
        behind = c_arr if stage == GATHER_STAGES[0] else token
        placed = {n: (_place_group(w["w_in"], n, chip_tables(_own_block_table(n)), odd_arr, behind) if n in GROUPS
                      else _place_shard(w[big_of[n]], n, behind)) for n in stage}
        send, recv, in_flight, token = _gather_start(placed, token)
        ag[stage] = (send, recv, in_flight)
    h = _rms_fwd(x[0], w["pre_norm_g"], "rms_pre", token)

    def conv_w_in_place(d):
        return dict(d, conv_w=lax.dynamic_update_slice(jnp.zeros((CONV_W, D_RNN), F32), d["conv_w"], (zero, cw0)))

    small_packs = [_pack(conv_w_in_place(d)) for d in (w, m, v)]

    passing = {}

    def prefetch(names, after):
        send, recv, in_flight = ag[names]
        *passing[names], token = _gather_pass_start(_gather_wait(send, recv, in_flight, after), after)
        return token

    def fetch(names, after):
        if names in passing:
            ready = _gather_pass_wait(*passing.pop(names), after)
        else:
            send, recv, in_flight = ag[names]
            ready = _gather_swap(_gather_wait(send, recv, in_flight, after))
        return tuple(ready[n] for n in names)

    rs = {"slots": {}, "halves": {}, "pending": [], "swap": None}

    def emit(grads, after=None):
        assert rs["swap"] is None
        *rs["swap"], token = _swap_start(grads, after)
        return token

    def advance(after):
        grads, received = _swap_wait(*rs["swap"], after)
        rs["swap"] = None
        halves = {n: _add_half(grads[n], received[n], c_arr, n) for n in grads}
        landing = {s: rs["slots"][s] if s in rs["slots"] else lax.empty(_slot_shape(s), BF16)
                   for s in _stage_shards(tuple(grads))}
        send, recv, halves, landing, token = _scatter_start(halves, landing)
        rs["slots"].update(landing)
        rs["pending"].append((send, recv, halves, tuple(landing)))
        return token

    sp = {n: w[n] for n, _ in SMALL}
    sp["conv_w"] = conv_w_full
    sq, grad_x, d_small = _local_step(x[0], h, mem[0], loss_target[0], sp, small_packs, fetch, prefetch, emit, advance)
    small_total = _all_reduce_small(_pack(d_small, sq[0]), "all_reduce_small")
    loss = small_total[PACK_ROWS - 1, 0] * (0.5 / D)

    for send, recv, halves, touched in rs["pending"]:
        halves, landed = _scatter_wait(send, recv, halves, {s: rs["slots"][s] for s in touched}, small_total)
        rs["slots"].update(landed)
        rs["halves"].update(halves)
    pos_arr = jnp.stack([ci, chip]).astype(jnp.int32)
    grp_tbl, blk_tbl = (chip_row(t) for t in _own_partial_tables())
    rest = {s: _sum_slots(rs["slots"][s], rs["halves"][s], pos_arr, s) for s in SHARDS if s != "win"}
    *rest_share, tok = _share_start(rest, small_total)
    win_sum = _sum_slots_win(rs["slots"]["win"], rs["halves"], pos_arr, grp_tbl, blk_tbl)
    *win_share, tok = _share_start({"win": win_sum}, tok)
    sums = _share_wait(*rest_share, tok)

    grad, delta, new_m, new_v = {}, {}, {}, {}
    for n, s in BIG.items():
        if n == "w_in":
            continue
        grad[n], delta[n], new_m[n], new_v[n] = _adamw(w[n], sums[s], m[n], v[n], s, ADAM_ROWS[s])
    g_win = _share_wait(*win_share, delta["w_out"])["win"]
    n = "w_in"
    grad[n], delta[n], new_m[n], new_v[n] = _adamw(w[n], g_win, m[n], v[n], "win", ADAM_ROWS["win"])
    for group in (grad, delta, new_m, new_v):
        group["w_in"] = group["w_in"].T
    _, d_, m_, v_ = _adamw(small_packs[0], small_total, small_packs[1], small_packs[2], "small", PACK_ROWS)
    for group, pack in ((grad, small_total), (delta, d_), (new_m, m_), (new_v, v_)):
        group.update(_unpack(pack))
    for group in (grad, delta, new_m, new_v):
        group["conv_w"] = lax.dynamic_slice(group["conv_w"], (zero, cw0), (CONV_W, D_RNN // N_CHIPS))

    outs = [loss, grad_x.reshape(1, S, D)]
    for group in (grad, delta, new_m, new_v):
        outs += [group[n].reshape(out_shapes[n]) for n in TWIN_WEIGHTS]
    return tuple(outs)
```

```python
import math
from typing import NamedTuple

import jax
import jax.numpy as jnp
from jax import lax
from jax.experimental import pallas as pl
from jax.experimental.pallas import tpu as pltpu

F32 = jnp.float32
BF16 = jnp.bfloat16
MESH = pl.DeviceIdType.MESH

S = 2048
D = 2048
MEM = 256
D_RNN = 1024
RNN_BLOCKS = 8
CONV_W = 4
LRU_C = 8.0
SWA_HEADS = 16
SWA_HD = 64
WINDOW = 128
MEM_HEADS = 4
MEM_HD = 256
REL_BUCKETS = 32
REL_MAX_DIST = 128
EPS = 1e-6
NEG_INF = -1e30
LANE = 128
SHARD = 3136
HALF_TILE = 64
N_CHIPS = 4
VMEM_LIMIT = 56 * 1024 * 1024

ADAM_LR = 0.001
ADAM_B1 = 0.9
ADAM_B2 = 0.999
ADAM_EPS = 1e-08
ADAM_WD = 0.01
ADAM_STEP = 10

GROUP_TILES = {"A": 16, "B": 18, "C": 16, "D": 48}
GROUPS = ("A", "B", "C", "D")


def _params(sem=None):
    return pltpu.CompilerParams(dimension_semantics=sem, vmem_limit_bytes=VMEM_LIMIT)


def _sigmoid(v):
    return jax.nn.sigmoid(v)


def _tile_home(t):
    if t < 16:
        return "A", t
    if t < 24:
        return "B", t - 16
    if t < 26:
        return "B", t - 24 + 16
    if t < 34:
        return "B", t - 26 + 8
    if t < 50:
        return "C", t - 34
    return "D", t - 50


def _shard_runs(j):
    runs = []
    per_shard = SHARD // HALF_TILE
    for q in range(per_shard * j, per_shard * (j + 1)):
        g, gt = _tile_home(q // 2)
        row = gt * LANE + (q % 2) * HALF_TILE
        if runs and runs[-1][2] == g and runs[-1][3] + runs[-1][1] == row:
            runs[-1][1] += HALF_TILE
        else:
            runs.append([(q - per_shard * j) * HALF_TILE, HALF_TILE, g, row])
    return [tuple(r) for r in runs]


_DIMS = {
    "nn": (((1,), (0,)), ((), ())),
    "nt": (((1,), (1,)), ((), ())),
    "tn": (((0,), (0,)), ((), ())),
}


def _mm(a, b, mode, out_dtype, tm, tn, tk, name, acc=None, after=None):
    if mode == "nn":
        (m, k), n = a.shape, b.shape[1]
    elif mode == "nt":
        (m, k), n = a.shape, b.shape[0]
    else:
        (k, m), n = a.shape, b.shape[1]
    tm, tn, tk = min(tm, m), min(tn, n), min(tk, k)
    assert m % tm == 0 and n % tn == 0 and k % tk == 0, (name, m, n, k)
    nk = k // tk
    has_acc = acc is not None

    def body(*refs):
        a_ref, b_ref = refs[0], refs[1]
        o_ref = refs[3] if has_acc else refs[2]
        p = lax.dot_general(a_ref[...], b_ref[...], _DIMS[mode], preferred_element_type=F32)

        def finish(v):
            if has_acc:
                v = v + refs[2][...]
            o_ref[...] = v.astype(out_dtype)

        if nk == 1:
            finish(p)
        else:
            s_ref = refs[-1]
            kk = pl.program_id(2)

            @pl.when(kk == 0)
            def _():
                s_ref[...] = p

            @pl.when(kk > 0)
            def _():
                s_ref[...] += p

            @pl.when(kk == nk - 1)
            def _():
                finish(s_ref[...])

    if mode == "nn":
        a_spec = pl.BlockSpec((tm, tk), lambda i, j, kk: (i, kk))
        b_spec = pl.BlockSpec((tk, tn), lambda i, j, kk: (kk, j))
    elif mode == "nt":
        a_spec = pl.BlockSpec((tm, tk), lambda i, j, kk: (i, kk))
        b_spec = pl.BlockSpec((tn, tk), lambda i, j, kk: (j, kk))
    else:
        a_spec = pl.BlockSpec((tk, tm), lambda i, j, kk: (kk, i))
        b_spec = pl.BlockSpec((tk, tn), lambda i, j, kk: (kk, j))
    o_spec = pl.BlockSpec((tm, tn), lambda i, j, kk: (i, j))
    in_specs = [a_spec, b_spec] + ([o_spec] if has_acc else [])
    args = (a, b) + ((acc,) if has_acc else ())
    if after is not None:
        in_specs.append(pl.BlockSpec(memory_space=pl.ANY))
        args += (after,)
    n_in = len(args)
    kernel_body = body

    def body(*refs):
        kernel_body(*(refs[:n_in - (after is not None)] + refs[n_in:]))

    return pl.pallas_call(
        body,
        name=name,
        grid=(m // tm, n // tn, nk),
        in_specs=in_specs,
        out_specs=o_spec,
        out_shape=jax.ShapeDtypeStruct((m, n), out_dtype),
        scratch_shapes=[pltpu.VMEM((tm, tn), F32)] if nk > 1 else [],
        compiler_params=_params(("parallel", "parallel", "arbitrary")),
    )(*args)


def _rms_fwd(x, g, name, after, ts=256):
    r, d = x.shape

    def body(x_ref, g_ref, _, o_ref):
        xv = x_ref[...]
        inv = lax.rsqrt(jnp.mean(xv * xv, axis=-1, keepdims=True) + EPS)
        o_ref[...] = (xv * inv * g_ref[...]).astype(BF16)

    return pl.pallas_call(
        body,
        name=name,
        grid=(r // ts,),
        in_specs=[pl.BlockSpec((ts, d), lambda i: (i, 0)), pl.BlockSpec((1, d), lambda i: (0, 0)),
                  pl.BlockSpec(memory_space=pl.ANY)],
        out_specs=pl.BlockSpec((ts, d), lambda i: (i, 0)),
        out_shape=jax.ShapeDtypeStruct((r, d), BF16),
        compiler_params=_params(("parallel",)),
    )(x, g, after)


def _post_loss(out, x, tgt, g_post, ts=256):
    n = S // ts

    def body(o_ref, x_ref, t_ref, g_ref, sq_ref, dy_ref, do_ref, dg_ref):
        i = pl.program_id(0)

        @pl.when(i == 0)
        def _():
            sq_ref[...] = jnp.zeros_like(sq_ref)
            dg_ref[...] = jnp.zeros_like(dg_ref)

        ov = o_ref[...]
        g = g_ref[...]
        inv = lax.rsqrt(jnp.mean(ov * ov, axis=-1, keepdims=True) + EPS)
        on = ov * inv
        err = x_ref[...] + on * g - t_ref[...]
        sq_ref[...] += jnp.sum(err * err)
        dy = err * (1.0 / D)
        dy_ref[...] = dy
        dg_ref[...] += jnp.sum(dy * on, axis=0, keepdims=True)
        don = dy * g
        do_ref[...] = (inv * (don - on * jnp.mean(don * on, axis=-1, keepdims=True))).astype(BF16)

    row = pl.BlockSpec((ts, D), lambda i: (i, 0))
    vec = pl.BlockSpec((1, D), lambda i: (0, 0))
    return pl.pallas_call(
        body,
        name="post_loss",
        grid=(n,),
        in_specs=[row, row, row, vec],
        out_specs=[pl.BlockSpec((8, LANE), lambda i: (0, 0)), row, row, vec],
        out_shape=[
            jax.ShapeDtypeStruct((8, LANE), F32),
            jax.ShapeDtypeStruct((S, D), F32),
            jax.ShapeDtypeStruct((S, D), BF16),
            jax.ShapeDtypeStruct((1, D), F32),
        ],
        compiler_params=_params(("arbitrary",)),
    )(out, x, tgt, g_post)


def _pre_bwd(dh, x, dy, g_pre, ts=256):
    n = S // ts

    def body(dh_ref, x_ref, dy_ref, g_ref, gx_ref, dg_ref):
        i = pl.program_id(0)

        @pl.when(i == 0)
        def _():
            dg_ref[...] = jnp.zeros_like(dg_ref)

        xv = x_ref[...]
        dhv = dh_ref[...]
        inv = lax.rsqrt(jnp.mean(xv * xv, axis=-1, keepdims=True) + EPS)
        xn = xv * inv
        dg_ref[...] += jnp.sum(dhv * xn, axis=0, keepdims=True)
        dxn = dhv * g_ref[...]
        gx_ref[...] = dy_ref[...] + inv * (dxn - xn * jnp.mean(dxn * xn, axis=-1, keepdims=True))

    row = pl.BlockSpec((ts, D), lambda i: (i, 0))
    vec = pl.BlockSpec((1, D), lambda i: (0, 0))
    return pl.pallas_call(
        body,
        name="pre_bwd",
        grid=(n,),
        in_specs=[row, row, row, vec],
        out_specs=[row, vec],
        out_shape=[jax.ShapeDtypeStruct((S, D), F32), jax.ShapeDtypeStruct((1, D), F32)],
        compiler_params=_params(("arbitrary",)),
    )(dh, x, dy, g_pre)


def _memnorm_bwd(dmemn, mem):
    def body(d_ref, m_ref, dg_ref):
        mv = m_ref[...]
        inv = lax.rsqrt(jnp.mean(mv * mv, axis=-1, keepdims=True) + EPS)
        dg_ref[...] = jnp.sum(d_ref[...] * mv * inv, axis=0, keepdims=True)

    return pl.pallas_call(
        body,
        name="memnorm_bwd",
        out_shape=jax.ShapeDtypeStruct((1, D), F32),
        compiler_params=_params(),
    )(dmemn, mem)


T_RNN = 256


def _neg_expm1(z):
    poly = -z * (1.0 + z * (0.5 + z * (1.0 / 6 + z * (1.0 / 24 + z * (1.0 / 120 + z * (1.0 / 720))))))
    return jnp.where(z > -0.1, poly, 1.0 - jnp.exp(z))


def _softplus_neg(lam):
    return jnp.maximum(-lam, 0.0) + jnp.log1p(jnp.exp(-jnp.abs(lam)))


def _rnn_gates(conv, wa_ref, ba, wx_ref, bx, lam, first_row):
    cbf = conv.astype(BF16)
    ga, gx = [], []
    for n in range(RNN_BLOCKS):
        c_n = cbf[:, n * LANE:(n + 1) * LANE]
        ga.append(jnp.dot(c_n, wa_ref[n], preferred_element_type=F32))
        gx.append(jnp.dot(c_n, wx_ref[n], preferred_element_type=F32))
    gate_r = _sigmoid(jnp.concatenate(ga, axis=1) + ba)
    gate_i = _sigmoid(jnp.concatenate(gx, axis=1) + bx)
    sp = _softplus_neg(lam)
    log_a = -LRU_C * gate_r * sp
    a = jnp.exp(log_a)
    mult_raw = jnp.sqrt(_neg_expm1(2.0 * log_a))
    mult = jnp.where(first_row, 1.0, mult_raw)
    return cbf, gate_r, gate_i, sp, a, mult_raw, mult


def _rglru_fwd(p_a, conv_w, conv_b, wa, ba, wx, bx, lam):
    t = T_RNN
    n = S // t

    def body(xr_ref, g_ref, cw_ref, cb_ref, wa_ref, ba_ref, wx_ref, bx_ref, lam_ref,
             y_ref, h_ref, xp_s, hcar, a_s, b_s):
        i = pl.program_id(0)

        @pl.when(i == 0)
        def _():
            xp_s[0:8, :] = jnp.zeros((8, D_RNN), F32)
            hcar[...] = jnp.zeros_like(hcar)

        @pl.when(i > 0)
        def _():
            xp_s[0:8, :] = xp_s[t:t + 8, :]

        xp_s[8:8 + t, :] = xr_ref[...]
        conv = cb_ref[...]
        for k in range(CONV_W):
            conv = conv + cw_ref[k:k + 1, :] * xp_s[8 - k:8 - k + t, :]
        rows = i * t + lax.broadcasted_iota(jnp.int32, (t, 1), 0)
        _, _, gate_i, _, a, _, mult = _rnn_gates(
            conv, wa_ref, ba_ref[...], wx_ref, bx_ref[...], lam_ref[...], rows == 0)
        a_s[...] = a
        b_s[...] = mult * gate_i * conv

        def step(tt, h):
            h = a_s[pl.ds(tt, 1), :] * h + b_s[pl.ds(tt, 1), :]
            h_ref[pl.ds(tt, 1), :] = h
            return h

        hcar[...] = lax.fori_loop(0, t, step, hcar[...], unroll=8)
        g = g_ref[...]
        y_ref[...] = (h_ref[...] * (g * _sigmoid(g))).astype(BF16)

    blk = lambda c: pl.BlockSpec((t, D_RNN), lambda i: (i, c))
    full = lambda shape: pl.BlockSpec(shape, lambda i: (0,) * len(shape))
    return pl.pallas_call(
        body,
        name="rglru_fwd",
        grid=(n,),
        in_specs=[blk(0), blk(1), full((CONV_W, D_RNN)), full((1, D_RNN)),
                  full((RNN_BLOCKS, LANE, LANE)), full((1, D_RNN)),
                  full((RNN_BLOCKS, LANE, LANE)), full((1, D_RNN)), full((1, D_RNN))],
        out_specs=[blk(0), blk(0)],
        out_shape=[jax.ShapeDtypeStruct((S, D_RNN), BF16), jax.ShapeDtypeStruct((S, D_RNN), F32)],
        scratch_shapes=[pltpu.VMEM((t + 8, D_RNN), F32), pltpu.VMEM((1, D_RNN), F32),
                        pltpu.VMEM((t, D_RNN), F32), pltpu.VMEM((t, D_RNN), F32)],
        compiler_params=_params(("arbitrary",)),
    )(p_a, p_a, conv_w, conv_b, wa, ba, wx, bx, lam)


def _rglru_bwd(dy, p_a, hseq, conv_w, conv_b, wa, ba, wx, bx, lam):
    t = T_RNN
    n = S // t
    rb = t // 8

    def body(dy_ref, xr_ref, g_ref, h_ref, xrp_ref, hp_ref, cw_ref, cb_ref, wa_ref, ba_ref, wx_ref, bx_ref, lam_ref,
             dp_ref, dcw_ref, dcb_ref, dwa_ref, dba_ref, dwx_ref, dbx_ref, dlam_ref,
             xp_s, hp_s, dxp_s, lamcar, a_s, dh_s, lam_s):
        i = pl.program_id(0)
        r = n - 1 - i

        @pl.when(i == 0)
        def _():
            for ref in (dcw_ref, dcb_ref, dwa_ref, dba_ref, dwx_ref, dbx_ref, dlam_ref, lamcar):
                ref[...] = jnp.zeros_like(ref)
            dxp_s[t:t + 8, :] = jnp.zeros((8, D_RNN), F32)

        @pl.when(i > 0)
        def _():
            dxp_s[t:t + 8, :] = dxp_s[0:8, :]

        has_prev = r > 0
        xp_s[0:8, :] = jnp.where(has_prev, xrp_ref[...], 0.0)
        xp_s[8:8 + t, :] = xr_ref[...]
        hp_s[0:8, :] = jnp.where(has_prev, hp_ref[...], 0.0)
        hp_s[8:8 + t, :] = h_ref[...]
        xs = [xp_s[8 - k:8 - k + t, :] for k in range(CONV_W)]
        conv = cb_ref[...]
        for k in range(CONV_W):
            conv = conv + cw_ref[k:k + 1, :] * xs[k]
        rows = r * t + lax.broadcasted_iota(jnp.int32, (t, 1), 0)
        first = rows == 0
        lam_p = lam_ref[...]
        cbf, gate_r, gate_i, sp, a, mult_raw, mult = _rnn_gates(
            conv, wa_ref, ba_ref[...], wx_ref, bx_ref[...], lam_p, first)

        g = g_ref[...]
        sg = _sigmoid(g)
        dyv = dy_ref[...]
        a_s[...] = a
        dh_s[...] = dyv * (g * sg)
        dg = dyv * h_ref[...] * (sg * (1.0 + g * (1.0 - sg)))

        def step(jj, car):
            tt = t - 1 - jj
            lm = dh_s[pl.ds(tt, 1), :] + car
            lam_s[pl.ds(tt, 1), :] = lm
            return a_s[pl.ds(tt, 1), :] * lm

        lamcar[...] = lax.fori_loop(0, t, step, lamcar[...], unroll=8)
        db = lam_s[...]
        da = db * hp_s[7:7 + t, :]
        dmult = db * gate_i * conv
        dgate_i = db * mult * conv
        dconv = db * mult * gate_i
        dlog_a = da * a + jnp.where(first, 0.0, dmult * (-(a * a) / mult_raw))
        dgate_r = dlog_a * (-LRU_C * sp)
        dsp = jnp.sum(dlog_a * (-LRU_C * gate_r), axis=0, keepdims=True)
        dlam_ref[...] += dsp * (-_sigmoid(-lam_p))
        dga = dgate_r * gate_r * (1.0 - gate_r)
        dgx = dgate_i * gate_i * (1.0 - gate_i)
        dba_ref[...] += jnp.sum(dga, axis=0, keepdims=True)
        dbx_ref[...] += jnp.sum(dgx, axis=0, keepdims=True)
        dga16, dgx16 = dga.astype(BF16), dgx.astype(BF16)
        back = []
        for nb in range(RNN_BLOCKS):
            sl = slice(nb * LANE, (nb + 1) * LANE)
            dwa_ref[nb] += lax.dot_general(cbf[:, sl], dga16[:, sl], _DIMS["tn"], preferred_element_type=F32)
            dwx_ref[nb] += lax.dot_general(cbf[:, sl], dgx16[:, sl], _DIMS["tn"], preferred_element_type=F32)
            back.append(lax.dot_general(dga16[:, sl], wa_ref[nb], _DIMS["nt"], preferred_element_type=F32)
                        + lax.dot_general(dgx16[:, sl], wx_ref[nb], _DIMS["nt"], preferred_element_type=F32))
        dconv = dconv + jnp.concatenate(back, axis=1)
        dcb_ref[...] += jnp.sum(dconv, axis=0, keepdims=True)
        for k in range(CONV_W):
            dcw_ref[k:k + 1, :] += jnp.sum(dconv * xs[k], axis=0, keepdims=True)
        dxp_s[0:t, :] = dconv
        dxr = cw_ref[0:1, :] * dconv
        for k in range(1, CONV_W):
            dxr = dxr + cw_ref[k:k + 1, :] * dxp_s[k:k + t, :]
        dp_ref[:, 0:D_RNN] = dxr.astype(BF16)
        dp_ref[:, D_RNN:2 * D_RNN] = dg.astype(BF16)

    blk = lambda c: pl.BlockSpec((t, D_RNN), lambda i: (n - 1 - i, c))
    prev8 = pl.BlockSpec((8, D_RNN), lambda i: (jnp.maximum((n - 1 - i) * rb - 1, 0), 0))
    full = lambda shape: pl.BlockSpec(shape, lambda i: (0,) * len(shape))
    vec = full((1, D_RNN))
    mat = full((RNN_BLOCKS, LANE, LANE))
    return pl.pallas_call(
        body,
        name="rglru_bwd",
        grid=(n,),
        in_specs=[blk(0), blk(0), blk(1), blk(0), prev8, prev8,
                  full((CONV_W, D_RNN)), vec, mat, vec, mat, vec, vec],
        out_specs=[pl.BlockSpec((t, 2 * D_RNN), lambda i: (n - 1 - i, 0)),
                   full((CONV_W, D_RNN)), vec, mat, vec, mat, vec, vec],
        out_shape=[jax.ShapeDtypeStruct((S, 2 * D_RNN), BF16),
                   jax.ShapeDtypeStruct((CONV_W, D_RNN), F32), jax.ShapeDtypeStruct((1, D_RNN), F32),
                   jax.ShapeDtypeStruct((RNN_BLOCKS, LANE, LANE), F32), jax.ShapeDtypeStruct((1, D_RNN), F32),
                   jax.ShapeDtypeStruct((RNN_BLOCKS, LANE, LANE), F32), jax.ShapeDtypeStruct((1, D_RNN), F32),
                   jax.ShapeDtypeStruct((1, D_RNN), F32)],
        scratch_shapes=[pltpu.VMEM((t + 8, D_RNN), F32), pltpu.VMEM((t + 8, D_RNN), F32),
                        pltpu.VMEM((t + 8, D_RNN), F32), pltpu.VMEM((1, D_RNN), F32),
                        pltpu.VMEM((t, D_RNN), F32), pltpu.VMEM((t, D_RNN), F32), pltpu.VMEM((t, D_RNN), F32)],
        compiler_params=_params(("arbitrary",)),
    )(dy, p_a, p_a, hseq, p_a, hseq, conv_w, conv_b, wa, ba, wx, bx, lam)


QB = WINDOW
KB2 = 2 * WINDOW
N_QB = S // QB
N_PAIR = SWA_HEADS // 2


def _swa_keys(kvc_ref, kvp_ref):
    kk = jnp.concatenate([kvp_ref[:, 0:LANE], kvc_ref[:, 0:LANE]], axis=0)
    vv = jnp.concatenate([kvp_ref[:, LANE:2 * LANE], kvc_ref[:, LANE:2 * LANE]], axis=0)
    lo = lax.broadcasted_iota(jnp.int32, (1, LANE), 1) < SWA_HD
    kk_sw, vv_sw = pltpu.roll(kk, SWA_HD, 1), pltpu.roll(vv, SWA_HD, 1)
    kd = [jnp.where(lo, kk, kk_sw).astype(BF16), jnp.where(lo, kk_sw, kk).astype(BF16)]
    vd = [jnp.where(lo, vv, vv_sw).astype(BF16), jnp.where(lo, vv_sw, vv).astype(BF16)]
    return lo, kd, vd


GRP = SWA_HEADS // 2
STACK = GRP
GQ = STACK * QB


def _swa_valid(n, rows):
    qi = lax.broadcasted_iota(jnp.int32, (rows, KB2), 0) % QB
    kj = lax.broadcasted_iota(jnp.int32, (rows, KB2), 1)
    dist = qi + WINDOW - kj
    return (dist >= 0) & (dist < WINDOW) & ((n > 0) | (kj >= WINDOW))


def _swa_stack(tile_of, lo, h0, masked):
    parts = []
    for h in range(h0, h0 + STACK):
        t = tile_of(h // 2)
        if masked:
            t = jnp.where(lo if h % 2 == 0 else jnp.logical_not(lo), t, 0.0)
        parts.append(t)
    return jnp.concatenate(parts, axis=0)


def _swa_unstack(stacked, lo, pair):
    return jnp.where(lo, stacked[2 * pair * QB:(2 * pair + 1) * QB], stacked[(2 * pair + 1) * QB:(2 * pair + 2) * QB])


def _swa_softmax(lg, sink, valid):
    lg = jnp.where(valid, lg, NEG_INF)
    m = jnp.maximum(jnp.max(lg, axis=-1, keepdims=True), sink)
    p = jnp.exp(lg - m)
    es = jnp.exp(sink - m)
    den = jnp.sum(p, axis=-1, keepdims=True) + es
    return p / den, es / den


def _swa_probs_head(qh16, kd, bias, sink, valid):
    lg = lax.dot_general(qh16, kd, _DIMS["nt"], preferred_element_type=F32) * (SWA_HD ** -0.5) + bias
    return _swa_softmax(lg, sink, valid)[0]


def _swa_probs(q16, kd, bias_ref, sink_ref, h0, valid):
    bias = bias_ref[h0:h0 + STACK].reshape(GQ, KB2)
    sink = jnp.concatenate([jnp.full((QB, 1), sink_ref[h], F32) for h in range(h0, h0 + STACK)], axis=0)
    lg = lax.dot_general(q16, kd, _DIMS["nt"], preferred_element_type=F32) * (SWA_HD ** -0.5) + bias
    return _swa_softmax(lg, sink, valid)


def _swa_specs():
    q = pl.BlockSpec((QB, D_RNN), lambda n: (n, 0))
    g = pl.BlockSpec((QB, D_RNN), lambda n: (n, 1))
    kvc = pl.BlockSpec((QB, 2 * LANE), lambda n: (n, 8))
    kvp = pl.BlockSpec((QB, 2 * LANE), lambda n: (jnp.maximum(n - 1, 0), 8))
    bias = pl.BlockSpec((SWA_HEADS, QB, KB2), lambda n: (0, 0, 0))
    sinks = pl.BlockSpec(memory_space=pltpu.SMEM)
    return q, g, kvc, kvp, bias, sinks


def _swa_fwd(p_b, bias_t, sinks):
    def body(q_ref, g_ref, kvc_ref, kvp_ref, bias_ref, sink_ref, y_ref, o_ref):
        n = pl.program_id(0)
        lo, kd, vd = _swa_keys(kvc_ref, kvp_ref)
        valid = _swa_valid(n, QB)
        for hp in range(N_PAIR):
            sl = slice(hp * LANE, (hp + 1) * LANE)
            kvh = hp // (N_PAIR // 2)
            q = q_ref[:, sl]
            outs = []
            for j in range(2):
                qh16 = jnp.where(lo if j == 0 else jnp.logical_not(lo), q, 0.0).astype(BF16)
                probs = _swa_probs_head(qh16, kd[kvh], bias_ref[2 * hp + j], sink_ref[2 * hp + j], valid)
                outs.append(jnp.dot(probs.astype(BF16), vd[kvh], preferred_element_type=F32))
            o = jnp.where(lo, outs[0], outs[1])
            o_ref[:, sl] = o
            g = g_ref[:, sl]
            y_ref[:, sl] = (o * (g * _sigmoid(g))).astype(BF16)

    q, g, kvc, kvp, bias, sinks_spec = _swa_specs()
    out = pl.BlockSpec((QB, D_RNN), lambda n: (n, 0))
    return pl.pallas_call(
        body,
        name="swa_fwd",
        grid=(N_QB,),
        in_specs=[q, g, kvc, kvp, bias, sinks_spec],
        out_specs=[out, out],
        out_shape=[jax.ShapeDtypeStruct((S, D_RNN), BF16), jax.ShapeDtypeStruct((S, D_RNN), F32)],
        compiler_params=_params(("parallel",)),
    )(p_b, p_b, p_b, p_b, bias_t, sinks)


def _swa_bwd(dy, p_b, o_swa, bias_t, sinks, after=None):
    def body(dy_ref, q_ref, g_ref, kvc_ref, kvp_ref, o_ref, bias_ref, sink_ref, *rest):
        dp_ref, dk_ref, dv_ref, dbias_ref, dsink_ref, do_s = rest[-6:]
        n = pl.program_id(0)

        @pl.when(n == 0)
        def _():
            for ref in (dk_ref, dv_ref, dbias_ref, dsink_ref):
                ref[...] = jnp.zeros_like(ref)

        lo, kd, vd = _swa_keys(kvc_ref, kvp_ref)
        hi = jnp.logical_not(lo)
        valid = _swa_valid(n, GQ)
        tile = lambda ref: (lambda hp: ref[:, hp * LANE:(hp + 1) * LANE])
        for hp in range(N_PAIR):
            sl = slice(hp * LANE, (hp + 1) * LANE)
            g, dyv = g_ref[:, sl], dy_ref[:, sl]
            sg = _sigmoid(g)
            do_s[:, sl] = dyv * (g * sg)
            dp_ref[:, D_RNN + hp * LANE:D_RNN + (hp + 1) * LANE] = (
                dyv * o_ref[:, sl] * (sg * (1.0 + g * (1.0 - sg)))).astype(BF16)

        dk_blk = jnp.zeros((KB2, LANE), F32)
        dv_blk = jnp.zeros((KB2, LANE), F32)
        for h0 in range(0, SWA_HEADS, STACK):
            kvh = h0 // GRP
            q16 = _swa_stack(tile(q_ref), lo, h0, masked=True).astype(BF16)
            do8 = _swa_stack(tile(do_s), lo, h0, masked=True)
            do16 = do8.astype(BF16)
            delta = jnp.sum(do8 * _swa_stack(tile(o_ref), lo, h0, masked=False), axis=-1, keepdims=True)
            probs, psink = _swa_probs(q16, kd[kvh], bias_ref, sink_ref, h0, valid)
            dpr = lax.dot_general(do16, vd[kvh], _DIMS["nt"], preferred_element_type=F32)
            ds = probs * (dpr - delta)
            sink_term = psink * delta
            for g in range(STACK):
                h, rows = h0 + g, slice(g * QB, (g + 1) * QB)
                dbias_ref[h] += ds[rows]
                dsink_ref[h:h + 1, :] += jnp.zeros((1, LANE), F32) - jnp.sum(sink_term[rows])
            ds16 = (ds * (SWA_HD ** -0.5)).astype(BF16)
            dq_all = jnp.dot(ds16, kd[kvh], preferred_element_type=F32)
            for pair in range(STACK // 2):
                sl = slice((h0 // 2 + pair) * LANE, (h0 // 2 + pair + 1) * LANE)
                dp_ref[:, sl] = _swa_unstack(dq_all, lo, pair).astype(BF16)
            dk_pair = lax.dot_general(ds16, q16, _DIMS["tn"], preferred_element_type=F32)
            dv_pair = lax.dot_general(probs.astype(BF16), do16, _DIMS["tn"], preferred_element_type=F32)
            keep = lo if kvh == 0 else hi
            dk_blk = dk_blk + jnp.where(keep, dk_pair + pltpu.roll(dk_pair, SWA_HD, 1), 0.0)
            dv_blk = dv_blk + jnp.where(keep, dv_pair + pltpu.roll(dv_pair, SWA_HD, 1), 0.0)

        cur = pl.ds(pl.multiple_of(n * QB, QB), QB)
        dk_ref[cur, :] += dk_blk[QB:KB2]
        dv_ref[cur, :] += dv_blk[QB:KB2]

        @pl.when(n > 0)
        def _():
            prev = pl.ds(pl.multiple_of((n - 1) * QB, QB), QB)
            dk_ref[prev, :] += dk_blk[0:QB]
            dv_ref[prev, :] += dv_blk[0:QB]

    q, g, kvc, kvp, bias, sinks_spec = _swa_specs()
    row = pl.BlockSpec((QB, D_RNN), lambda n: (n, 0))
    acc = pl.BlockSpec((S, LANE), lambda n: (0, 0))
    return pl.pallas_call(
        body,
        name="swa_bwd",
        grid=(N_QB,),
        in_specs=[row, q, g, kvc, kvp, row, bias, sinks_spec] + ([ANY] if after is not None else []),
        out_specs=[pl.BlockSpec((QB, 2 * D_RNN), lambda n: (n, 0)), acc, acc, bias,
                   pl.BlockSpec((SWA_HEADS, LANE), lambda n: (0, 0))],
        out_shape=[jax.ShapeDtypeStruct((S, GROUP_TILES["B"] * LANE), BF16),
                   jax.ShapeDtypeStruct((S, LANE), F32), jax.ShapeDtypeStruct((S, LANE), F32),
                   jax.ShapeDtypeStruct((SWA_HEADS, QB, KB2), F32),
                   jax.ShapeDtypeStruct((SWA_HEADS, LANE), F32)],
        scratch_shapes=[pltpu.VMEM((QB, D_RNN), F32)],
        compiler_params=_params(("arbitrary",)),
    )(dy, p_b, p_b, p_b, p_b, o_swa, bias_t, sinks, *([after] if after is not None else []))


def _swa_pack(dp_b, dk, dv, ts=512):
    def body(_, dk_ref, dv_ref, o_ref):
        o_ref[:, 0:LANE] = dk_ref[...].astype(BF16)
        o_ref[:, LANE:2 * LANE] = dv_ref[...].astype(BF16)

    tile = pl.BlockSpec((ts, LANE), lambda i: (i, 0))
    return pl.pallas_call(
        body,
        name="swa_pack",
        grid=(S // ts,),
        in_specs=[pl.BlockSpec(memory_space=pl.ANY), tile, tile],
        out_specs=pl.BlockSpec((ts, 2 * LANE), lambda i: (i, 8)),
        out_shape=jax.ShapeDtypeStruct(dp_b.shape, dp_b.dtype),
        input_output_aliases={0: 0},
        compiler_params=_params(("parallel",)),
    )(dp_b, dk, dv)


def _split3(v):
    a = v.astype(BF16)
    r = v - a.astype(F32)
    b = r.astype(BF16)
    c = (r - b.astype(F32)).astype(BF16)
    return a, b, c


def _relbias_grad(dbias_flat, onehot_t):
    def body(d_ref, e_ref, o_ref):
        e = e_ref[...]
        acc = jnp.zeros((SWA_HEADS, REL_BUCKETS), F32)
        for term in _split3(d_ref[...]):
            acc = acc + lax.dot_general(term, e, _DIMS["nt"], preferred_element_type=F32)
        o_ref[...] = acc

    return pl.pallas_call(
        body,
        name="relbias_grad",
        out_shape=jax.ShapeDtypeStruct((SWA_HEADS, REL_BUCKETS), F32),
        compiler_params=_params(),
    )(dbias_flat, onehot_t)


TS_MEM = 512


def _mem_probs(q16, mk):
    lg = lax.dot_general(q16, mk, _DIMS["nt"], preferred_element_type=F32) * (MEM_HD ** -0.5)
    p = jnp.exp(lg - jnp.max(lg, axis=-1, keepdims=True))
    return p / jnp.sum(p, axis=-1, keepdims=True)


def _mem_fwd(p_c, mkv):
    def body(q_ref, g_ref, mkv_ref, y_ref, o_ref):
        for hm in range(MEM_HEADS):
            sl = slice(hm * MEM_HD, (hm + 1) * MEM_HD)
            probs = _mem_probs(q_ref[:, sl].astype(BF16), mkv_ref[:, sl])
            o = jnp.dot(probs.astype(BF16), mkv_ref[:, D_RNN + hm * MEM_HD:D_RNN + (hm + 1) * MEM_HD],
                        preferred_element_type=F32)
            o_ref[:, sl] = o
            g = g_ref[:, sl]
            y_ref[:, sl] = (o * (g * _sigmoid(g))).astype(BF16)

    blk = lambda c: pl.BlockSpec((TS_MEM, D_RNN), lambda i: (i, c))
    return pl.pallas_call(
        body,
        name="mem_fwd",
        grid=(S // TS_MEM,),
        in_specs=[blk(0), blk(1), pl.BlockSpec((MEM, 2 * D_RNN), lambda i: (0, 0))],
        out_specs=[blk(0), blk(0)],
        out_shape=[jax.ShapeDtypeStruct((S, D_RNN), BF16), jax.ShapeDtypeStruct((S, D_RNN), F32)],
        compiler_params=_params(("parallel",)),
    )(p_c, p_c, mkv)


def _mem_bwd(dy, p_c, o_mem, mkv):
    def body(dy_ref, q_ref, g_ref, o_ref, mkv_ref, dp_ref, dmkv_ref):
        @pl.when(pl.program_id(0) == 0)
        def _():
            dmkv_ref[...] = jnp.zeros_like(dmkv_ref)

        for hm in range(MEM_HEADS):
            sl = slice(hm * MEM_HD, (hm + 1) * MEM_HD)
            sv = slice(D_RNN + hm * MEM_HD, D_RNN + (hm + 1) * MEM_HD)
            q16 = q_ref[:, sl].astype(BF16)
            mk, mv = mkv_ref[:, sl], mkv_ref[:, sv]
            probs = _mem_probs(q16, mk)
            g, o, dyv = g_ref[:, sl], o_ref[:, sl], dy_ref[:, sl]
            sg = _sigmoid(g)
            do = dyv * (g * sg)
            dp_ref[:, sv] = (dyv * o * (sg * (1.0 + g * (1.0 - sg)))).astype(BF16)
            do16 = do.astype(BF16)
            delta = jnp.sum(do * o, axis=-1, keepdims=True)
            dpr = lax.dot_general(do16, mv, _DIMS["nt"], preferred_element_type=F32)
            ds16 = (probs * (dpr - delta) * (MEM_HD ** -0.5)).astype(BF16)
            dp_ref[:, sl] = jnp.dot(ds16, mk, preferred_element_type=F32).astype(BF16)
            dmkv_ref[:, sl] += lax.dot_general(ds16, q16, _DIMS["tn"], preferred_element_type=F32)
            dmkv_ref[:, sv] += lax.dot_general(probs.astype(BF16), do16, _DIMS["tn"], preferred_element_type=F32)

    blk = lambda c: pl.BlockSpec((TS_MEM, D_RNN), lambda i: (i, c))
    kv = pl.BlockSpec((MEM, 2 * D_RNN), lambda i: (0, 0))
    return pl.pallas_call(
        body,
        name="mem_bwd",
        grid=(S // TS_MEM,),
        in_specs=[blk(0), blk(0), blk(1), blk(0), kv],
        out_specs=[pl.BlockSpec((TS_MEM, 2 * D_RNN), lambda i: (i, 0)), kv],
        out_shape=[jax.ShapeDtypeStruct((S, 2 * D_RNN), BF16), jax.ShapeDtypeStruct((MEM, 2 * D_RNN), F32)],
        compiler_params=_params(("arbitrary",)),
    )(dy, p_c, p_c, o_mem, mkv)


TS_MRG = 512
TD_MRG = 1024
N_DBLK = D // TD_MRG


def _merge_fwd(z, p_d):
    def body(z0, z1, z2, g0, g1, g2, o_ref):
        o_ref[...] = (_sigmoid(g0[...]) * z0[...] + _sigmoid(g1[...]) * z1[...]
                      + _sigmoid(g2[...]) * z2[...]).astype(BF16)

    blk = pl.BlockSpec((TS_MRG, TD_MRG), lambda i, d: (i, d))
    gate = lambda b: pl.BlockSpec((TS_MRG, TD_MRG), lambda i, d: (i, b * N_DBLK + d))
    return pl.pallas_call(
        body,
        name="merge_fwd",
        grid=(S // TS_MRG, N_DBLK),
        in_specs=[blk, blk, blk, gate(0), gate(1), gate(2)],
        out_specs=blk,
        out_shape=jax.ShapeDtypeStruct((S, D), BF16),
        compiler_params=_params(("parallel", "parallel")),
    )(z[0], z[1], z[2], p_d, p_d, p_d)


TS_MRG_BWD = 128


def _merge_bwd(dmerged, z, p_d, after):
    def body(dm_ref, z0, z1, z2, g_ref, _, dz0, dz1, dz2, dg_ref):
        dm = dm_ref[...]
        for b, (z_ref, dz_ref) in enumerate(((z0, dz0), (z1, dz1), (z2, dz2))):
            cols = slice(b * D, (b + 1) * D)
            sg = _sigmoid(g_ref[:, cols])
            dz_ref[...] = (dm * sg).astype(BF16)
            dg_ref[:, cols] = (dm * z_ref[...] * sg * (1.0 - sg)).astype(BF16)

    row = pl.BlockSpec((TS_MRG_BWD, D), lambda i: (i, 0))
    wide = pl.BlockSpec((TS_MRG_BWD, 3 * D), lambda i: (i, 0))
    outs = pl.pallas_call(
        body,
        name="merge_bwd",
        grid=(S // TS_MRG_BWD,),
        in_specs=[row, row, row, row, wide, pl.BlockSpec(memory_space=pl.ANY)],
        out_specs=[row, row, row, wide],
        out_shape=[jax.ShapeDtypeStruct((S, D), BF16)] * 3 + [jax.ShapeDtypeStruct((S, 3 * D), BF16)],
        compiler_params=_params(("parallel",)),
    )(dmerged, z[0], z[1], z[2], p_d, after)
    return list(outs[:3]), outs[3]


def _bucket_table():
    import numpy as np
    qi = np.arange(QB)[:, None]
    kj = np.arange(KB2)[None, :]
    n = np.maximum(qi + WINDOW - kj, 0)
    max_exact = REL_BUCKETS // 2
    ratio = np.log(np.maximum(n, 1).astype(np.float32) / max_exact) / np.float32(math.log(REL_MAX_DIST / max_exact))
    large = np.minimum(max_exact + (ratio * (REL_BUCKETS - max_exact)).astype(np.int32), REL_BUCKETS - 1)
    bucket = np.where(n < max_exact, n, large).reshape(1, QB * KB2)
    return (bucket == np.arange(REL_BUCKETS)[:, None]).astype(np.float32)


def _bias_expand(rel_bias_t, onehot_t):
    def body(r_ref, e_ref, o_ref):
        e = e_ref[...]
        acc = jnp.zeros((SWA_HEADS, QB * KB2), F32)
        for term in _split3(r_ref[...]):
            acc = acc + jnp.dot(term, e, preferred_element_type=F32)
        o_ref[...] = acc

    return pl.pallas_call(
        body,
        name="bias_expand",
        out_shape=jax.ShapeDtypeStruct((SWA_HEADS, QB * KB2), F32),
        compiler_params=_params(),
    )(rel_bias_t, onehot_t)


PROJ_TN = {"A": 1024, "B": 1152, "C": 1024, "D": 1536}


def _do_first(arrays, token):
    def body(*refs):
        refs[-1][...] = jnp.zeros_like(refs[-1])

    return pl.pallas_call(
        body,
        name="do_first",
        in_specs=[pl.BlockSpec(memory_space=pl.ANY)] * (len(arrays) + 1),
        out_specs=pl.BlockSpec(memory_space=pltpu.VMEM),
        out_shape=jax.ShapeDtypeStruct((8, LANE), F32),
    )(*arrays, token)


def _local_step(x, h, mem, tgt, sp, early, fetch, prefetch, emit, advance):
    onehot_t = jnp.asarray(_bucket_table(), BF16)
    bias_t = _bias_expand(sp["rel_bias"].T, onehot_t).reshape(SWA_HEADS, QB, KB2)
    sinks = sp["swa_sinks"].reshape(SWA_HEADS)
    wa16, wx16 = sp["w_rg_a"].astype(BF16), sp["w_rg_x"].astype(BF16)
    rnn = (sp["conv_w"], sp["conv_b"], wa16, sp["b_rg_a"], wx16, sp["b_rg_x"], sp["lru_lambda"])

    memn = _rms_fwd(mem, sp["mem_norm_g"], "rms_mem", h)
    h_and_prep = _do_first([bias_t, memn, wa16, wx16, *early], h)
    w_grp, p = {}, {}

    def project(g, after, then=None):
        (w_grp[g],) = fetch((g,), after)
        tok = prefetch(then, w_grp[g]) if then is not None else None
        p[g] = _mm(h, w_grp[g], "nt", F32, 1024, PROJ_TN[g], D, f"proj_{g}", after=tok)

    project("A", h_and_prep)
    y_rg, hseq = _rglru_fwd(p["A"], *rnn)
    project("B", y_rg)
    y_swa, o_swa = _swa_fwd(p["B"], bias_t, sinks)
    project("C", y_swa, then=("mk",))
    (wmk,) = fetch(("mk",), p["C"])
    tok = prefetch(("br0", "br1", "br2"), wmk)
    mkv = _mm(memn, wmk, "nn", BF16, MEM, 1024, D, "mkv", after=tok)
    y_mem, o_mem = _mem_fwd(p["C"], mkv)
    ys = (y_rg, y_swa, y_mem)
    wbr = fetch(("br0", "br1", "br2"), y_mem)
    tok = prefetch(("D",), wbr[2])
    z = []
    for b in range(3):
        z.append(_mm(ys[b], wbr[b], "nn", F32, 1024, 1024, D_RNN, f"branch_out{b}", after=z[-1] if z else tok))
    project("D", z[2], then=("out",))
    merged = _merge_fwd(z, p["D"])
    (wout,) = fetch(("out",), merged)
    out = _mm(merged, wout, "nn", F32, 1024, 1024, D, "out_proj")
    sq, dy, dout, d_post = _post_loss(out, x, tgt, sp["post_norm_g"])

    tok = emit({"out": _mm(merged, dout, "tn", BF16, 1024, 1024, S, "d_wout")})
    dmerged = _mm(dout, wout, "nt", F32, 1024, 1024, D, "d_merged", after=tok)
    tok = advance(dmerged)
    dz, dp_d = _merge_bwd(dmerged, z, p["D"], tok)
    d_win = lambda g, dp_g, after=None: _mm(dp_g, h, "tn", BF16, PROJ_TN[g], 1024, S, f"d_win_{g}", after=after)
    tok = emit({f"br{b}": _mm(ys[b], dz[b], "tn", BF16, 1024, 1024, S, f"d_wbr{b}") for b in range(3)}, tok)
    d_w_d = d_win("D", dp_d, tok)
    tok = emit({"D": d_w_d}, advance(d_w_d))
    dy_mem = _mm(dz[2], wbr[2], "nt", F32, 1024, 1024, D, "d_branch2", after=tok)
    tok = advance(dy_mem)
    dp_c, dmkv = _mem_bwd(dy_mem, p["C"], o_mem, mkv)
    dmkv16 = dmkv.astype(BF16)
    tok = emit({"mk": _mm(memn, dmkv16, "tn", BF16, 1024, 1024, MEM, "d_wmk", after=tok), "C": d_win("C", dp_c)}, tok)
    dmemn = _mm(dmkv16, wmk, "nt", F32, MEM, 1024, D, "d_memn", after=tok)
    tok = advance(dmemn)
    d_memg = _memnorm_bwd(dmemn, mem)
    dy_rg = _mm(dz[0], wbr[0], "nt", F32, 1024, 1024, D, "d_branch0", after=tok)
    dp_a, d_cw, d_cb, d_wa, d_ba, d_wx, d_bx, d_lam = _rglru_bwd(dy_rg, p["A"], hseq, *rnn)
    tok = emit({"A": d_win("A", dp_a)}, tok)
    dy_swa = _mm(dz[1], wbr[1], "nt", F32, 1024, 1024, D, "d_branch1", after=tok)
    tok = advance(dy_swa)
    dp_b, dk, dv, d_bias, d_sink = _swa_bwd(dy_swa, p["B"], o_swa, bias_t, sinks, after=tok)
    dp_b = _swa_pack(dp_b, dk, dv)
    d_rel = _relbias_grad(d_bias.reshape(SWA_HEADS, QB * KB2), onehot_t).T
    dp = {"A": dp_a, "B": dp_b, "C": dp_c, "D": dp_d}
    tok = emit({"B": d_win("B", dp_b)}, tok)
    dh = None
    for g in GROUPS:
        dh = _mm(dp[g], w_grp[g], "nn", F32, 1024, 1024, 2304 if g == "B" else 2048, f"d_h_{g}", acc=dh,
                 after=tok if g in ("A", "B") else None)
        if g == "A":
            tok = advance(dh)
    grad_x, d_pre = _pre_bwd(dh, x, dy, sp["pre_norm_g"])

    d_small = {
        "pre_norm_g": d_pre, "post_norm_g": d_post, "mem_norm_g": d_memg, "conv_w": d_cw, "conv_b": d_cb,
        "w_rg_a": d_wa, "b_rg_a": d_ba, "w_rg_x": d_wx, "b_rg_x": d_bx, "lru_lambda": d_lam,
        "swa_sinks": d_sink[:, 0].reshape(1, SWA_HEADS), "rel_bias": d_rel,
    }
    return sq, grad_x, d_small


ANY = pl.BlockSpec(memory_space=pl.ANY)
SHARD_ROWS = D // N_CHIPS
GATHERED = {"A": (2048, D), "B": (2304, D), "C": (2048, D), "D": (6144, D), "mk": (D, D),
            "br0": (D_RNN, D), "br1": (D_RNN, D), "br2": (D_RNN, D), "out": (D, D)}
SHARD_SHAPES = {"win": (SHARD, D), "mk": (SHARD_ROWS, D), "br0": (D_RNN, SHARD_ROWS), "br1": (D_RNN, SHARD_ROWS),
                "br2": (D_RNN, SHARD_ROWS), "out": (SHARD_ROWS, D)}
SHARDS = tuple(SHARD_SHAPES)
HALF_AXIS = {"win": 1, "mk": 1, "br0": 0, "br1": 0, "br2": 0, "out": 1,
             "A": 1, "B": 1, "C": 1, "D": 1}


def _halved(shape, axis):
    return (shape[0] // 2, shape[1]) if axis == 0 else (shape[0], shape[1] // 2)


class Piece(NamedTuple):
    src: str
    dst: str
    rows: int
    sr0: int
    sc0: int
    dr0: int
    dc0: int
    ncols: int


def _pieces_of(jj):
    out = [Piece("win", g, n, r, 0, gr, 0, D) for r, n, g, gr in _shard_runs(jj)]
    out.append(Piece("mk", "mk", SHARD_ROWS, 0, 0, SHARD_ROWS * jj, 0, D))
    out += [Piece(f"br{b}", f"br{b}", D_RNN, 0, 0, 0, SHARD_ROWS * jj, SHARD_ROWS) for b in range(3)]
    out.append(Piece("out", "out", SHARD_ROWS, 0, 0, SHARD_ROWS * jj, 0, D))
    return out


def _half_rect(ref, p, side, which):
    r0, c0 = (p.sr0, p.sc0) if side == "src" else (p.dr0, p.dc0)
    if HALF_AXIS[p.src] == 1:
        return _rect(ref, r0, p.rows, c0 + which * (p.ncols // 2), p.ncols // 2)
    return _rect(ref, r0 + which * (p.rows // 2), p.rows // 2, c0, p.ncols)


def _rect_in_half(ref, p, side):
    r0, c0 = (p.sr0, p.sc0) if side == "src" else (p.dr0, p.dc0)
    if HALF_AXIS[p.src] == 1:
        return _rect(ref, r0, p.rows, 0, p.ncols // 2)
    return _rect(ref, 0, p.rows // 2, c0, p.ncols)


MAX_PIECES = max(len(_pieces_of(jj)) for jj in range(N_CHIPS))


def _rect(ref, r0, rows, c0, ncols):
    return ref.at[pl.ds(r0, rows), pl.ds(c0, ncols)]


def _position():
    x, y, c = lax.axis_index("x"), lax.axis_index("y"), lax.axis_index("c")
    return x, y, c, 2 * x + y


HBM = pl.BlockSpec(memory_space=pltpu.HBM)
SEM = pl.BlockSpec(memory_space=pltpu.SEMAPHORE)
EFFECT = pltpu.SideEffectType.DATAFLOW_SIDE_EFFECTING
N_SEM = MAX_PIECES * N_CHIPS
GATHER_STAGES = (("A",), ("B",), ("C",), ("mk",), ("br0", "br1", "br2"), ("D",), ("out",))


def _in_hbm(a):
    return pltpu.with_memory_space_constraint(a, pltpu.HBM)


def _stage_pieces(jj, stage):
    return [(i, p) for i, p in enumerate(_pieces_of(jj)) if p.dst in stage]


def _own_block_table(g):
    import numpy as np
    units = np.full((N_CHIPS, GATHERED[g][0] // HALF_TILE), -1, np.int64)
    for jj in range(N_CHIPS):
        for r, n, grp, gr in _shard_runs(jj):
            if grp == g:
                for k in range(n // HALF_TILE):
                    units[jj, gr // HALF_TILE + k] = r // HALF_TILE + k
    tbl = np.zeros((N_CHIPS, 2, GATHERED[g][0] // LANE), np.int32)
    for jj in range(N_CHIPS):
        for b in range(tbl.shape[2]):
            first, second = units[jj, 2 * b], units[jj, 2 * b + 1]
            if jj % 2 == 0:
                src = first if first >= 0 else second - 1
                if first >= 0 or second >= 0:
                    assert src % 2 == 0
                    tbl[jj, :, b] = src // 2
            else:
                if first >= 0:
                    assert first % 2 == 1
                    tbl[jj, 0, b] = first // 2
                if second >= 0:
                    assert second % 2 == 0
                    tbl[jj, 1, b] = second // 2
    return tbl


def _place_group(w_t, g, tables, odd_arr, after):
    nb = GATHERED[g][0] // LANE

    def body(t_ref, odd_ref, a_ref, b_ref, _, o_ref):
        odd = odd_ref[0] == 1
        o_ref[0:HALF_TILE, :] = jnp.where(odd, a_ref[HALF_TILE:LANE, :], a_ref[0:HALF_TILE, :]).astype(BF16)
        o_ref[HALF_TILE:LANE, :] = jnp.where(odd, b_ref[0:HALF_TILE, :], a_ref[HALF_TILE:LANE, :]).astype(BF16)

    return pl.pallas_call(
        body,
        name=f"place_{g}",
        grid_spec=pltpu.PrefetchScalarGridSpec(
            num_scalar_prefetch=2,
            grid=(nb,),
            in_specs=[pl.BlockSpec((LANE, D), lambda b, t, o: (t[0, b], 0)),
                      pl.BlockSpec((LANE, D), lambda b, t, o: (t[1, b], 0)), ANY],
            out_specs=pl.BlockSpec((LANE, D), lambda b, t, o: (b, 0)),
        ),
        out_shape=jax.ShapeDtypeStruct(GATHERED[g], BF16),
        compiler_params=_params(("parallel",)),
    )(tables, odd_arr, w_t, w_t, after)


def _place_shard(shard, name, after):
    rows, cols = shard.shape
    by_rows = HALF_AXIS[name] == 1

    def body(x_ref, _, o_ref):
        o_ref[...] = x_ref[...].astype(BF16)

    return pl.pallas_call(
        body,
        name=f"place_{name}",
        grid=(N_CHIPS,),
        in_specs=[pl.BlockSpec((rows, cols), lambda b: (0, 0)), ANY],
        out_specs=pl.BlockSpec((rows, cols), (lambda b: (b, 0)) if by_rows else (lambda b: (0, b))),
        out_shape=jax.ShapeDtypeStruct(GATHERED[name], BF16),
        compiler_params=_params(("parallel",)),
    )(shard, after)


def _gather_copy(arr, send_sems, recv_sems, c, jj, i, p, kk):
    rect = _half_rect(arr[p.dst], p, "dst", c)
    return pltpu.make_async_remote_copy(
        src_ref=rect, dst_ref=rect, send_sem=send_sems.at[i * N_CHIPS + kk],
        recv_sem=recv_sems.at[jj * MAX_PIECES + i], device_id=(kk // 2, kk % 2, c), device_id_type=MESH)


def _gather_start(arrays, after):
    stage = tuple(arrays)
    na = len(stage)

    def body(*refs):
        arr = dict(zip(stage, refs[:na]))
        send_sems, recv_sems = refs[na + 1], refs[na + 2]
        token = refs[-1]
        _, _, c, j = _position()
        for jj in range(N_CHIPS):
            @pl.when(j == jj)
            def _():
                for i, p in _stage_pieces(jj, stage):
                    for kk in range(N_CHIPS):
                        if kk != jj:
                            _gather_copy(arr, send_sems, recv_sems, c, jj, i, p, kk).start()
        token[...] = jnp.zeros_like(token)

    outs = pl.pallas_call(
        body,
        name=f"gather_start_{stage[0]}",
        in_specs=[HBM] * na + [ANY],
        out_specs=[SEM, SEM] + [HBM] * na + [pl.BlockSpec(memory_space=pltpu.VMEM)],
        out_shape=[pltpu.SemaphoreType.DMA((N_SEM,)), pltpu.SemaphoreType.DMA((N_SEM,))]
        + [pltpu.HBM(GATHERED[n], BF16) for n in stage] + [jax.ShapeDtypeStruct((8, LANE), F32)],
        input_output_aliases={k: 2 + k for k in range(na)},
        compiler_params=pltpu.CompilerParams(has_side_effects=EFFECT),
    )(*[_in_hbm(arrays[n]) for n in stage], after)
    return outs[0], outs[1], dict(zip(stage, outs[2:2 + na])), outs[-1]


def _gather_wait(send_sems, recv_sems, arrays, after):
    stage = tuple(arrays)
    na = len(stage)

    def body(*refs):
        arr = dict(zip(stage, refs[:na]))
        sems_s, sems_r = refs[na], refs[na + 1]
        _, _, c, j = _position()
        for jj in range(N_CHIPS):
            @pl.when(j != jj)
            def _():
                for i, p in _stage_pieces(jj, stage):
                    _gather_copy(arr, sems_s, sems_r, c, jj, i, p, jj).wait_recv()

            @pl.when(j == jj)
            def _():
                for i, p in _stage_pieces(jj, stage):
                    for kk in range(N_CHIPS):
                        if kk != jj:
                            _gather_copy(arr, sems_s, sems_r, c, jj, i, p, kk).wait_send()

    outs = pl.pallas_call(
        body,
        name=f"gather_wait_{stage[0]}",
        in_specs=[HBM] * na + [SEM, SEM, ANY],
        out_specs=[HBM] * na,
        out_shape=[pltpu.HBM(GATHERED[n], BF16) for n in stage],
        input_output_aliases={k: k for k in range(na)},
        compiler_params=pltpu.CompilerParams(has_side_effects=EFFECT),
    )(*[arrays[n] for n in stage], send_sems, recv_sems, after)
    return dict(zip(stage, outs))


def _gather_swap(arrays):
    stage = tuple(arrays)
    na = len(stage)

    def body(*refs):
        dst = dict(zip(stage, refs[na:2 * na]))
        send_sems, recv_sems = refs[2 * na:]
        x, y, c, j = _position()

        def fwd(jj, i, p, which):
            rect = _half_rect(dst[p.dst], p, "dst", which)
            return pltpu.make_async_remote_copy(
                src_ref=rect, dst_ref=rect, send_sem=send_sems.at[jj * MAX_PIECES + i],
                recv_sem=recv_sems.at[jj * MAX_PIECES + i], device_id=(x, y, 1 - c), device_id_type=MESH)

        for jj in range(N_CHIPS):
            @pl.when(j != jj)
            def _():
                for i, p in _stage_pieces(jj, stage):
                    fwd(jj, i, p, c).start()
        for jj in range(N_CHIPS):
            @pl.when(j != jj)
            def _():
                for i, p in _stage_pieces(jj, stage):
                    fwd(jj, i, p, 1 - c).wait_recv()
        for jj in range(N_CHIPS):
            @pl.when(j != jj)
            def _():
                for i, p in _stage_pieces(jj, stage):
                    fwd(jj, i, p, c).wait_send()

    outs = pl.pallas_call(
        body,
        name=f"gather_swap_{stage[0]}",
        in_specs=[ANY] * na,
        out_specs=[ANY] * na,
        out_shape=[jax.ShapeDtypeStruct(GATHERED[n], BF16) for n in stage],
        input_output_aliases={k: k for k in range(na)},
        scratch_shapes=[pltpu.SemaphoreType.DMA((N_SEM,)), pltpu.SemaphoreType.DMA((N_SEM,))],
        compiler_params=pltpu.CompilerParams(has_side_effects=True),
    )(*[arrays[n] for n in stage])
    return dict(zip(stage, outs))


def _pass_on_copy(arr, send_sems, recv_sems, x, y, c, jj, i, p, which):
    rect = _half_rect(arr[p.dst], p, "dst", which)
    return pltpu.make_async_remote_copy(
        src_ref=rect, dst_ref=rect, send_sem=send_sems.at[jj * MAX_PIECES + i],
        recv_sem=recv_sems.at[jj * MAX_PIECES + i], device_id=(x, y, 1 - c), device_id_type=MESH)


def _gather_pass_start(arrays, after):
    stage = tuple(arrays)
    na = len(stage)

    def body(*refs):
        arr = dict(zip(stage, refs[:na]))
        x, y, c, j = _position()
        for jj in range(N_CHIPS):
            @pl.when(j != jj)
            def _():
                for i, p in _stage_pieces(jj, stage):
                    _pass_on_copy(arr, refs[na + 1], refs[na + 2], x, y, c, jj, i, p, c).start()
        refs[-1][...] = jnp.zeros_like(refs[-1])

    outs = pl.pallas_call(
        body,
        name=f"gather_pass_start_{stage[0]}",
        in_specs=[HBM] * na + [ANY],
        out_specs=[SEM, SEM] + [HBM] * na + [pl.BlockSpec(memory_space=pltpu.VMEM)],
        out_shape=[pltpu.SemaphoreType.DMA((N_SEM,)), pltpu.SemaphoreType.DMA((N_SEM,))]
        + [pltpu.HBM(GATHERED[n], BF16) for n in stage] + [jax.ShapeDtypeStruct((8, LANE), F32)],
        input_output_aliases={k: 2 + k for k in range(na)},
        compiler_params=pltpu.CompilerParams(has_side_effects=EFFECT),
    )(*[arrays[n] for n in stage], after)
    return outs[0], outs[1], dict(zip(stage, outs[2:2 + na])), outs[-1]


def _gather_pass_wait(send_sems, recv_sems, arrays, after):
    stage = tuple(arrays)
    na = len(stage)

    def body(*refs):
        arr = dict(zip(stage, refs[:na]))
        x, y, c, j = _position()
        for jj in range(N_CHIPS):
            @pl.when(j != jj)
            def _():
                for i, p in _stage_pieces(jj, stage):
                    _pass_on_copy(arr, refs[na], refs[na + 1], x, y, c, jj, i, p, 1 - c).wait_recv()
                    _pass_on_copy(arr, refs[na], refs[na + 1], x, y, c, jj, i, p, c).wait_send()

    outs = pl.pallas_call(
        body,
        name=f"gather_pass_wait_{stage[0]}",
        in_specs=[HBM] * na + [SEM, SEM, ANY],
        out_specs=[HBM] * na,
        out_shape=[pltpu.HBM(GATHERED[n], BF16) for n in stage],
        input_output_aliases={k: k for k in range(na)},
        compiler_params=pltpu.CompilerParams(has_side_effects=EFFECT),
    )(*[arrays[n] for n in stage], send_sems, recv_sems, after)
    return dict(zip(stage, outs))


def _own_half(ref, shape, axis, which):
    if axis == 1:
        return ref.at[:, pl.ds(which * (shape[1] // 2), shape[1] // 2)]
    return ref.at[pl.ds(which * (shape[0] // 2), shape[0] // 2), :]


def _swap_copies(names, src, dst, send_sems, recv_sems):
    x, y, c, _ = _position()
    return [pltpu.make_async_remote_copy(
        src_ref=_own_half(src[n], GATHERED[n], HALF_AXIS[n], 1 - c), dst_ref=dst[n],
        send_sem=send_sems.at[k], recv_sem=recv_sems.at[k],
        device_id=(x, y, 1 - c), device_id_type=MESH) for k, n in enumerate(names)]


def _swap_start(grads, after):
    names = tuple(grads)
    n = len(names)

    def body(*refs):
        src, dst = dict(zip(names, refs[:n])), dict(zip(names, refs[n:2 * n]))
        for cp in _swap_copies(names, src, dst, refs[2 * n + 1], refs[2 * n + 2]):
            cp.start()
        refs[-1][...] = jnp.zeros_like(refs[-1])

    half_shape = lambda nm: _halved(GATHERED[nm], HALF_AXIS[nm])
    args = [_in_hbm(grads[nm]) for nm in names] + [_in_hbm(lax.empty(half_shape(nm), BF16)) for nm in names]
    if after is None:
        after = jnp.zeros((8, LANE), F32)
    outs = pl.pallas_call(
        body,
        name=f"swap_start_{names[0]}",
        in_specs=[HBM] * (2 * n) + [ANY],
        out_specs=[SEM, SEM] + [HBM] * (2 * n) + [pl.BlockSpec(memory_space=pltpu.VMEM)],
        out_shape=[pltpu.SemaphoreType.DMA((n,)), pltpu.SemaphoreType.DMA((n,))]
        + [pltpu.HBM(GATHERED[nm], BF16) for nm in names] + [pltpu.HBM(half_shape(nm), BF16) for nm in names]
        + [jax.ShapeDtypeStruct((8, LANE), F32)],
        input_output_aliases={k: 2 + k for k in range(2 * n)},
        compiler_params=pltpu.CompilerParams(has_side_effects=EFFECT),
    )(*args, after)
    return outs[0], outs[1], dict(zip(names, outs[2:2 + n])), dict(zip(names, outs[2 + n:2 + 2 * n])), outs[-1]


def _swap_wait(send_sems, recv_sems, grads, landing, after):
    names = tuple(grads)
    n = len(names)

    def body(*refs):
        src, dst = dict(zip(names, refs[:n])), dict(zip(names, refs[n:2 * n]))
        copies = _swap_copies(names, src, dst, refs[2 * n], refs[2 * n + 1])
        for cp in copies:
            cp.wait_recv()
        for cp in copies:
            cp.wait_send()

    half_shape = lambda nm: _halved(GATHERED[nm], HALF_AXIS[nm])
    outs = pl.pallas_call(
        body,
        name=f"swap_wait_{names[0]}",
        in_specs=[HBM] * (2 * n) + [SEM, SEM, ANY],
        out_specs=[HBM] * (2 * n),
        out_shape=[pltpu.HBM(GATHERED[nm], BF16) for nm in names] + [pltpu.HBM(half_shape(nm), BF16) for nm in names],
        input_output_aliases={k: k for k in range(2 * n)},
        compiler_params=pltpu.CompilerParams(has_side_effects=EFFECT),
    )(*[grads[nm] for nm in names], *[landing[nm] for nm in names], send_sems, recv_sems, after)
    return dict(zip(names, outs[:n])), dict(zip(names, outs[n:]))


ADD_ROWS = {"A": 1024, "B": 768, "C": 1024, "D": 1536, "mk": 1024, "br0": 512, "br1": 512, "br2": 512, "out": 1024}


def _add_half(full, recv, c_arr, name):
    rows, cols = recv.shape
    tr = ADD_ROWS[name]
    if HALF_AXIS[name] == 1:
        index = lambda i, c_ref: (i, c_ref[0])
    else:
        nb = rows // tr
        index = lambda i, c_ref: (nb * c_ref[0] + i, 0)

    def body(c_ref, a_ref, b_ref, o_ref):
        o_ref[...] = (a_ref[...].astype(F32) + b_ref[...].astype(F32)).astype(BF16)

    return pl.pallas_call(
        body,
        name=f"add_half_{name}",
        grid_spec=pltpu.PrefetchScalarGridSpec(
            num_scalar_prefetch=1,
            grid=(rows // tr,),
            in_specs=[pl.BlockSpec((tr, cols), index), pl.BlockSpec((tr, cols), lambda i, c_ref: (i, 0))],
            out_specs=pl.BlockSpec((tr, cols), lambda i, c_ref: (i, 0)),
        ),
        out_shape=jax.ShapeDtypeStruct((rows, cols), BF16),
        compiler_params=_params(("parallel",)),
    )(c_arr, full, recv)


SLOT_SHAPES = {n: _halved(SHARD_SHAPES[n], HALF_AXIS[n]) for n in SHARDS}


def _slot_shape(n):
    return (N_CHIPS,) + SLOT_SHAPES[n]


def _stage_shards(stage):
    pieces = [p for jj in range(N_CHIPS) for p in _pieces_of(jj)]
    return tuple(s for s in SHARDS if any(p.src == s and p.dst in stage for p in pieces))


def _scatter_copy(src, dst, send_sems, recv_sems, c, jj, kk, i, p):
    return pltpu.make_async_remote_copy(
        src_ref=_rect_in_half(src[p.dst], p, "dst"), dst_ref=_rect_in_half(dst[p.src].at[jj], p, "src"),
        send_sem=send_sems.at[kk * MAX_PIECES + i], recv_sem=recv_sems.at[jj * MAX_PIECES + i],
        device_id=(kk // 2, kk % 2, c), device_id_type=MESH)


def _scatter_start(halves, slots):
    stage, touched = tuple(halves), tuple(slots)
    nh, nt = len(stage), len(touched)

    def body(*refs):
        src = dict(zip(stage, refs[:nh]))
        dst = dict(zip(touched, refs[nh:nh + nt]))
        send_sems, recv_sems = refs[nh + nt], refs[nh + nt + 1]
        token = refs[-1]
        _, _, c, j = _position()
        for jj in range(N_CHIPS):
            @pl.when(j == jj)
            def _():
                for kk in range(N_CHIPS):
                    if kk != jj:
                        for i, p in _stage_pieces(kk, stage):
                            _scatter_copy(src, dst, send_sems, recv_sems, c, jj, kk, i, p).start()
        token[...] = jnp.zeros_like(token)

    outs = pl.pallas_call(
        body,
        name=f"scatter_start_{stage[0]}",
        in_specs=[HBM] * (nh + nt),
        out_specs=[SEM, SEM] + [HBM] * (nh + nt) + [pl.BlockSpec(memory_space=pltpu.VMEM)],
        out_shape=[pltpu.SemaphoreType.DMA((N_SEM,)), pltpu.SemaphoreType.DMA((N_SEM,))]
        + [pltpu.HBM(halves[n].shape, BF16) for n in stage] + [pltpu.HBM(_slot_shape(s), BF16) for s in touched]
        + [jax.ShapeDtypeStruct((8, LANE), F32)],
        input_output_aliases={k: 2 + k for k in range(nh + nt)},
        compiler_params=pltpu.CompilerParams(has_side_effects=EFFECT),
    )(*[_in_hbm(halves[n]) for n in stage], *[_in_hbm(slots[s]) for s in touched])
    return outs[0], outs[1], dict(zip(stage, outs[2:2 + nh])), dict(zip(touched, outs[2 + nh:2 + nh + nt])), outs[-1]


def _scatter_wait(send_sems, recv_sems, halves, slots, after):
    stage, touched = tuple(halves), tuple(slots)
    nh, nt = len(stage), len(touched)

    def body(*refs):
        src = dict(zip(stage, refs[:nh]))
        dst = dict(zip(touched, refs[nh:nh + nt]))
        sems_s, sems_r = refs[nh + nt], refs[nh + nt + 1]
        _, _, c, j = _position()
        for jj in range(N_CHIPS):
            @pl.when(j == jj)
            def _():
                for ss in range(N_CHIPS):
                    if ss != jj:
                        for i, p in _stage_pieces(jj, stage):
                            _scatter_copy(src, dst, sems_s, sems_r, c, ss, jj, i, p).wait_recv()
                for kk in range(N_CHIPS):
                    if kk != jj:
                        for i, p in _stage_pieces(kk, stage):
                            _scatter_copy(src, dst, sems_s, sems_r, c, jj, kk, i, p).wait_send()

    outs = pl.pallas_call(
        body,
        name=f"scatter_wait_{stage[0]}",
        in_specs=[HBM] * (nh + nt) + [SEM, SEM, ANY],
        out_specs=[HBM] * (nh + nt),
        out_shape=[pltpu.HBM(halves[n].shape, BF16) for n in stage] + [pltpu.HBM(_slot_shape(s), BF16) for s in touched],
        input_output_aliases={k: k for k in range(nh + nt)},
        compiler_params=pltpu.CompilerParams(has_side_effects=EFFECT),
    )(*[halves[n] for n in stage], *[slots[s] for s in touched], send_sems, recv_sems, after)
    return dict(zip(stage, outs[:nh])), dict(zip(touched, outs[nh:]))


SUM_ROWS = {"mk": 512, "br0": 512, "br1": 512, "br2": 512, "out": 512}


def _sum_in_chip_order(chip, own, s_ref):
    acc = None
    for k in range(N_CHIPS):
        term = jnp.where(chip == k, own, s_ref[k].astype(F32))
        acc = term if acc is None else acc + term
    return acc


def _sum_slots(slots, own_half, pos_arr, name):
    _, rows, cols = slots.shape
    tr = SUM_ROWS[name]
    nb = rows // tr
    if HALF_AXIS[name] == 1:
        own_index = lambda i, pos: (nb * pos[1] + i, 0)
        out_index = lambda i, pos: (i, pos[0])
    else:
        own_index = lambda i, pos: (i, pos[1])
        out_index = lambda i, pos: (nb * pos[0] + i, 0)

    def body(pos, s_ref, own_ref, o_ref):
        o_ref[...] = _sum_in_chip_order(pos[1], own_ref[...].astype(F32), s_ref)

    return pl.pallas_call(
        body,
        name=f"sum_slots_{name}",
        grid_spec=pltpu.PrefetchScalarGridSpec(
            num_scalar_prefetch=1,
            grid=(nb,),
            in_specs=[pl.BlockSpec((N_CHIPS, tr, cols), lambda i, pos: (0, i, 0)),
                      pl.BlockSpec((tr, cols), own_index)],
            out_specs=pl.BlockSpec((tr, cols), out_index),
        ),
        out_shape=jax.ShapeDtypeStruct(SHARD_SHAPES[name], F32),
        compiler_params=_params(("parallel",)),
    )(pos_arr, slots, own_half)


def _own_partial_tables():
    import numpy as np
    nb = SHARD // HALF_TILE
    grp, blk = np.zeros((N_CHIPS, nb), np.int32), np.zeros((N_CHIPS, nb), np.int32)
    for jj in range(N_CHIPS):
        for r, n, g, gr in _shard_runs(jj):
            for k in range(n // HALF_TILE):
                grp[jj, r // HALF_TILE + k] = GROUPS.index(g)
                blk[jj, r // HALF_TILE + k] = gr // HALF_TILE + k
    return grp, blk


def _sum_slots_win(slots, own_halves, pos_arr, grp_tbl, blk_tbl):
    nb = SHARD // HALF_TILE
    cols = D // 2

    def own_spec(gi):
        return pl.BlockSpec((HALF_TILE, cols), lambda b, pos, grp, blk: (jnp.where(grp[b] == gi, blk[b], 0), 0))

    def body(pos, grp, blk, s_ref, a_ref, b_ref, c_ref, d_ref, o_ref):
        g = grp[pl.program_id(0)]
        own = a_ref[...]
        for gi, ref in ((1, b_ref), (2, c_ref), (3, d_ref)):
            own = jnp.where(g == gi, ref[...], own)
        o_ref[...] = _sum_in_chip_order(pos[1], own.astype(F32), s_ref)

    return pl.pallas_call(
        body,
        name="sum_slots_win",
        grid_spec=pltpu.PrefetchScalarGridSpec(
            num_scalar_prefetch=3,
            grid=(nb,),
            in_specs=[pl.BlockSpec((N_CHIPS, HALF_TILE, cols), lambda b, pos, grp, blk: (0, b, 0))]
            + [own_spec(gi) for gi in range(len(GROUPS))],
            out_specs=pl.BlockSpec((HALF_TILE, cols), lambda b, pos, grp, blk: (b, pos[0])),
        ),
        out_shape=jax.ShapeDtypeStruct(SHARD_SHAPES["win"], F32),
        compiler_params=_params(("parallel",)),
    )(pos_arr, grp_tbl, blk_tbl, slots, *[own_halves[g] for g in GROUPS])


def _share_copy(buf, name, send_sems, recv_sems, k, which):
    x, y, c, _ = _position()
    half = _own_half(buf, SHARD_SHAPES[name], HALF_AXIS[name], which)
    return pltpu.make_async_remote_copy(src_ref=half, dst_ref=half, send_sem=send_sems.at[k], recv_sem=recv_sems.at[k],
                                        device_id=(x, y, 1 - c), device_id_type=MESH)


def _share_start(sums, after):
    names = tuple(sums)
    n = len(names)

    def body(*refs):
        _, _, c, _ = _position()
        for k, nm in enumerate(names):
            _share_copy(refs[k], nm, refs[n + 1], refs[n + 2], k, c).start()
        refs[-1][...] = jnp.zeros_like(refs[-1])

    outs = pl.pallas_call(
        body,
        name=f"share_start_{names[0]}",
        in_specs=[HBM] * n + [ANY],
        out_specs=[SEM, SEM] + [HBM] * n + [pl.BlockSpec(memory_space=pltpu.VMEM)],
        out_shape=[pltpu.SemaphoreType.DMA((n,)), pltpu.SemaphoreType.DMA((n,))]
        + [pltpu.HBM(SHARD_SHAPES[nm], F32) for nm in names] + [jax.ShapeDtypeStruct((8, LANE), F32)],
        input_output_aliases={k: 2 + k for k in range(n)},
        compiler_params=pltpu.CompilerParams(has_side_effects=EFFECT),
    )(*[_in_hbm(sums[nm]) for nm in names], after)
    return outs[0], outs[1], dict(zip(names, outs[2:2 + n])), outs[-1]


def _share_wait(send_sems, recv_sems, sums, after):
    names = tuple(sums)
    n = len(names)

    def body(*refs):
        _, _, c, _ = _position()
        for k, nm in enumerate(names):
            _share_copy(refs[k], nm, refs[n], refs[n + 1], k, 1 - c).wait_recv()
            _share_copy(refs[k], nm, refs[n], refs[n + 1], k, c).wait_send()

    outs = pl.pallas_call(
        body,
        name=f"share_wait_{names[0]}",
        in_specs=[HBM] * n + [SEM, SEM, ANY],
        out_specs=[HBM] * n,
        out_shape=[pltpu.HBM(SHARD_SHAPES[nm], F32) for nm in names],
        input_output_aliases={k: k for k in range(n)},
        compiler_params=pltpu.CompilerParams(has_side_effects=EFFECT),
    )(*[sums[nm] for nm in names], send_sems, recv_sems, after)
    return dict(zip(names, outs))


def _all_reduce_small(pack, name):
    rows = pack.shape[0]
    half = rows // 2

    def body(p_ref, o_ref, sib, land, sems):
        x, y, c, j = _position()
        sibling = (x, y, 1 - c)
        swap = pltpu.make_async_remote_copy(src_ref=p_ref, dst_ref=sib, send_sem=sems.at[0], recv_sem=sems.at[1],
                                            device_id=sibling, device_id_type=MESH)
        swap.start()
        swap.wait_recv()
        land[j] = p_ref[...] + sib[...]

        def mine(k, which):
            return land.at[k, pl.ds(which * half, half)]

        def ici(kk):
            return pltpu.make_async_remote_copy(
                src_ref=mine(j, c), dst_ref=mine(j, c), send_sem=sems.at[2 + kk], recv_sem=sems.at[6 + j],
                device_id=(kk // 2, kk % 2, c), device_id_type=MESH)

        def arrival(kk):
            return pltpu.make_async_remote_copy(
                src_ref=mine(kk, c), dst_ref=mine(kk, c), send_sem=sems.at[2 + kk], recv_sem=sems.at[6 + kk],
                device_id=(kk // 2, kk % 2, c), device_id_type=MESH)

        def passed_on(kk, which):
            return pltpu.make_async_remote_copy(
                src_ref=mine(kk, which), dst_ref=mine(kk, which), send_sem=sems.at[10 + kk],
                recv_sem=sems.at[14 + kk], device_id=sibling, device_id_type=MESH)

        for kk in range(N_CHIPS):
            @pl.when(j != kk)
            def _():
                ici(kk).start()
        for kk in range(N_CHIPS):
            @pl.when(j != kk)
            def _():
                arrival(kk).wait_recv()
                passed_on(kk, c).start()
        for kk in range(N_CHIPS):
            @pl.when(j != kk)
            def _():
                passed_on(kk, 1 - c).wait_recv()
        acc = land[0]
        for kk in range(1, N_CHIPS):
            acc = acc + land[kk]
        o_ref[...] = acc
        swap.wait_send()
        for kk in range(N_CHIPS):
            @pl.when(j != kk)
            def _():
                ici(kk).wait_send()
                passed_on(kk, c).wait_send()

    vmem = pl.BlockSpec(memory_space=pltpu.VMEM)
    return pl.pallas_call(
        body,
        name=name,
        in_specs=[vmem],
        out_specs=vmem,
        out_shape=jax.ShapeDtypeStruct((rows, LANE), F32),
        scratch_shapes=[pltpu.VMEM((rows, LANE), F32), pltpu.VMEM((N_CHIPS, rows, LANE), F32),
                        pltpu.SemaphoreType.DMA((18,))],
        compiler_params=pltpu.CompilerParams(has_side_effects=True, vmem_limit_bytes=VMEM_LIMIT),
    )(pack)


ADAM_ROWS = {"win": 224, "mk": 256, "br0": 512, "br1": 512, "br2": 512, "out": 256}


def _adamw(w, g, m, v, name, tr):
    rows, cols = w.shape
    tr = min(tr, rows)

    def body(w_ref, g_ref, m_ref, v_ref, go_ref, d_ref, nm_ref, nv_ref):
        gv = g_ref[...]
        go_ref[...] = gv
        nm = ADAM_B1 * m_ref[...] + (1.0 - ADAM_B1) * gv
        nv = ADAM_B2 * v_ref[...] + (1.0 - ADAM_B2) * (gv * gv)
        nm_ref[...] = nm
        nv_ref[...] = nv
        m_hat = nm / (1.0 - ADAM_B1 ** ADAM_STEP)
        v_hat = nv / (1.0 - ADAM_B2 ** ADAM_STEP)
        d_ref[...] = -ADAM_LR * (m_hat / (jnp.sqrt(v_hat) + ADAM_EPS) + ADAM_WD * w_ref[...])

    blk = pl.BlockSpec((tr, cols), lambda i: (i, 0))
    shape = jax.ShapeDtypeStruct((rows, cols), F32)
    return pl.pallas_call(
        body,
        name=f"adamw_{name}",
        grid=(rows // tr,),
        in_specs=[blk] * 4,
        out_specs=[blk] * 4,
        out_shape=[shape] * 4,
        compiler_params=_params(("parallel",)),
    )(w, g, m, v)


SMALL = (("pre_norm_g", (1, D)), ("post_norm_g", (1, D)), ("mem_norm_g", (1, D)), ("conv_w", (CONV_W, D_RNN)),
         ("conv_b", (1, D_RNN)), ("w_rg_a", (RNN_BLOCKS, LANE, LANE)), ("b_rg_a", (1, D_RNN)),
         ("w_rg_x", (RNN_BLOCKS, LANE, LANE)), ("b_rg_x", (1, D_RNN)), ("lru_lambda", (1, D_RNN)),
         ("swa_sinks", (1, SWA_HEADS)), ("rel_bias", (REL_BUCKETS, SWA_HEADS)))
PACK_ROWS = 2176


def _slot_len(shape):
    return -(-math.prod(shape) // LANE) * LANE


def _pack(values, last_row=None):
    parts = []
    for name, shape in SMALL:
        flat = values[name].reshape(-1).astype(F32)
        parts.append(jnp.pad(flat, (0, _slot_len(shape) - flat.shape[0])))
    flat = jnp.concatenate(parts)
    tail = jnp.zeros((LANE,), F32) if last_row is None else last_row
    return jnp.concatenate([jnp.pad(flat, (0, (PACK_ROWS - 1) * LANE - flat.shape[0])), tail]).reshape(PACK_ROWS, LANE)


def _unpack(pack):
    flat = pack.reshape(-1)
    out, off = {}, 0
    for name, shape in SMALL:
        out[name] = flat[off:off + math.prod(shape)].reshape(shape)
        off += _slot_len(shape)
    return out


TWIN_WEIGHTS = ("pre_norm_g", "post_norm_g", "mem_norm_g", "w_in", "conv_w", "conv_b", "w_rg_a", "b_rg_a", "w_rg_x",
                "b_rg_x", "lru_lambda", "swa_sinks", "rel_bias", "w_mem_kv", "w_br_rg", "w_br_swa", "w_br_mem", "w_out")
BIG = {"w_in": "win", "w_mem_kv": "mk", "w_br_rg": "br0", "w_br_swa": "br1", "w_br_mem": "br2", "w_out": "out"}


def kernel(x, mem, pre_norm_g, post_norm_g, mem_norm_g, w_in, conv_w, conv_b, w_rg_a, b_rg_a, w_rg_x, b_rg_x, lru_lambda, swa_sinks, rel_bias, w_mem_kv, w_br_rg, w_br_swa, w_br_mem, w_out, loss_target, m_pre_norm_g, m_post_norm_g, m_mem_norm_g, m_w_in, m_conv_w, m_conv_b, m_w_rg_a, m_b_rg_a, m_w_rg_x, m_b_rg_x, m_lru_lambda, m_swa_sinks, m_rel_bias, m_w_mem_kv, m_w_br_rg, m_w_br_swa, m_w_br_mem, m_w_out, v_pre_norm_g, v_post_norm_g, v_mem_norm_g, v_w_in, v_conv_w, v_conv_b, v_w_rg_a, v_b_rg_a, v_w_rg_x, v_b_rg_x, v_lru_lambda, v_swa_sinks, v_rel_bias, v_w_mem_kv, v_w_br_rg, v_w_br_swa, v_w_br_mem, v_w_out):
    args = dict(locals())
    out_shapes = {n: args[n].shape for n in TWIN_WEIGHTS}
    w = {n: (args[n] if n == "rel_bias" else args[n][0]) for n in TWIN_WEIGHTS}
    m = {n: (args["m_" + n] if n == "rel_bias" else args["m_" + n][0]) for n in TWIN_WEIGHTS}
    v = {n: (args["v_" + n] if n == "rel_bias" else args["v_" + n][0]) for n in TWIN_WEIGHTS}
    for d in (w, m, v):
        for n, shape in SMALL:
            if n != "conv_w":
                d[n] = d[n].reshape(shape)

    xi, yi, ci = lax.axis_index("x"), lax.axis_index("y"), lax.axis_index("c")
    chip = 2 * xi + yi
    c_arr = ci.astype(jnp.int32).reshape(1)
    zero = jnp.zeros((), jnp.int32)
    cw0 = (chip * (D_RNN // N_CHIPS)).astype(jnp.int32)

    placed = lax.dynamic_update_slice(jnp.zeros((CONV_W, D_RNN), F32), w["conv_w"], (zero, cw0))
    placed = jnp.where(ci == 0, placed, 0.0).reshape(CONV_W * D_RNN // LANE, LANE)
    conv_w_full = _all_reduce_small(placed, "gather_conv_w").reshape(CONV_W, D_RNN)

    for d in (w, m, v):
        d["w_in"] = d["w_in"].T
    chip_row = lambda tbl: lax.dynamic_slice(jnp.asarray(tbl), (chip.astype(jnp.int32), zero), (1, tbl.shape[1]))[0]
    chip_tables = lambda tbl: lax.dynamic_slice(jnp.asarray(tbl), (chip.astype(jnp.int32), zero, zero),
                                                (1,) + tbl.shape[1:])[0]
    odd_arr = yi.astype(jnp.int32).reshape(1)
    big_of = {s: n for n, s in BIG.items()}
    ag, token = {}, conv_w_full
    for stage in GATHER_STAGES:
        behind = c_arr if stage == GATHER_STAGES[0] else token
        placed = {n: (_place_group(w["w_in"], n, chip_tables(_own_block_table(n)), odd_arr, behind) if n in GROUPS
                      else _place_shard(w[big_of[n]], n, behind)) for n in stage}
        send, recv, in_flight, token = _gather_start(placed, token)
        ag[stage] = (send, recv, in_flight)
    h = _rms_fwd(x[0], w["pre_norm_g"], "rms_pre", token)

    def conv_w_in_place(d):
        return dict(d, conv_w=lax.dynamic_update_slice(jnp.zeros((CONV_W, D_RNN), F32), d["conv_w"], (zero, cw0)))

    small_packs = [_pack(conv_w_in_place(d)) for d in (w, m, v)]

    passing = {}

    def prefetch(names, after):
        send, recv, in_flight = ag[names]
        *passing[names], token = _gather_pass_start(_gather_wait(send, recv, in_flight, after), after)
        return token

    def fetch(names, after):
        if names in passing:
            ready = _gather_pass_wait(*passing.pop(names), after)
        else:
            send, recv, in_flight = ag[names]
            ready = _gather_swap(_gather_wait(send, recv, in_flight, after))
        return tuple(ready[n] for n in names)

    rs = {"slots": {}, "halves": {}, "pending": [], "swap": None}

    def emit(grads, after=None):
        assert rs["swap"] is None
        *rs["swap"], token = _swap_start(grads, after)
        return token

    def advance(after):
        grads, received = _swap_wait(*rs["swap"], after)
        rs["swap"] = None
        halves = {n: _add_half(grads[n], received[n], c_arr, n) for n in grads}
        landing = {s: rs["slots"][s] if s in rs["slots"] else lax.empty(_slot_shape(s), BF16)
                   for s in _stage_shards(tuple(grads))}
        send, recv, halves, landing, token = _scatter_start(halves, landing)
        rs["slots"].update(landing)
        rs["pending"].append((send, recv, halves, tuple(landing)))
        return token

    sp = {n: w[n] for n, _ in SMALL}
    sp["conv_w"] = conv_w_full
    sq, grad_x, d_small = _local_step(x[0], h, mem[0], loss_target[0], sp, small_packs, fetch, prefetch, emit, advance)
    small_total = _all_reduce_small(_pack(d_small, sq[0]), "all_reduce_small")
    loss = small_total[PACK_ROWS - 1, 0] * (0.5 / D)

    for send, recv, halves, touched in rs["pending"]:
        halves, landed = _scatter_wait(send, recv, halves, {s: rs["slots"][s] for s in touched}, small_total)
        rs["slots"].update(landed)
        rs["halves"].update(halves)
    pos_arr = jnp.stack([ci, chip]).astype(jnp.int32)
    grp_tbl, blk_tbl = (chip_row(t) for t in _own_partial_tables())
    rest = {s: _sum_slots(rs["slots"][s], rs["halves"][s], pos_arr, s) for s in SHARDS if s != "win"}
    *rest_share, tok = _share_start(rest, small_total)
    win_sum = _sum_slots_win(rs["slots"]["win"], rs["halves"], pos_arr, grp_tbl, blk_tbl)
    *win_share, tok = _share_start({"win": win_sum}, tok)
    sums = _share_wait(*rest_share, tok)

    grad, delta, new_m, new_v = {}, {}, {}, {}
    for n, s in BIG.items():
        if n == "w_in":
            continue
        grad[n], delta[n], new_m[n], new_v[n] = _adamw(w[n], sums[s], m[n], v[n], s, ADAM_ROWS[s])
    g_win = _share_wait(*win_share, delta["w_out"])["win"]
    n = "w_in"
    grad[n], delta[n], new_m[n], new_v[n] = _adamw(w[n], g_win, m[n], v[n], "win", ADAM_ROWS["win"])
    for group in (grad, delta, new_m, new_v):
        group["w_in"] = group["w_in"].T
    _, d_, m_, v_ = _adamw(small_packs[0], small_total, small_packs[1], small_packs[2], "small", PACK_ROWS)
    for group, pack in ((grad, small_total), (delta, d_), (new_m, m_), (new_v, v_)):
        group.update(_unpack(pack))
    for group in (grad, delta, new_m, new_v):
        group["conv_w"] = lax.dynamic_slice(group["conv_w"], (zero, cw0), (CONV_W, D_RNN // N_CHIPS))

    outs = [loss, grad_x.reshape(1, S, D)]
    for group in (grad, delta, new_m, new_v):
        outs += [group[n].reshape(out_shapes[n]) for n in TWIN_WEIGHTS]
    return tuple(outs)
```

```python
import math
from typing import NamedTuple

import jax
import jax.numpy as jnp
from jax import lax
from jax.experimental import pallas as pl
from jax.experimental.pallas import tpu as pltpu

F32 = jnp.float32
BF16 = jnp.bfloat16
MESH = pl.DeviceIdType.MESH

S = 2048
D = 2048
MEM = 256
D_RNN = 1024
RNN_BLOCKS = 8
CONV_W = 4
LRU_C = 8.0
SWA_HEADS = 16
SWA_HD = 64
WINDOW = 128
MEM_HEADS = 4
MEM_HD = 256
REL_BUCKETS = 32
REL_MAX_DIST = 128
EPS = 1e-6
NEG_INF = -1e30
LANE = 128
SHARD = 3136
HALF_TILE = 64
N_CHIPS = 4
VMEM_LIMIT = 56 * 1024 * 1024

ADAM_LR = 0.001
ADAM_B1 = 0.9
ADAM_B2 = 0.999
ADAM_EPS = 1e-08
ADAM_WD = 0.01
ADAM_STEP = 10

GROUP_TILES = {"A": 16, "B": 18, "C": 16, "D": 48}
GROUPS = ("A", "B", "C", "D")


def _params(sem=None):
    return pltpu.CompilerParams(dimension_semantics=sem, vmem_limit_bytes=VMEM_LIMIT)


def _sigmoid(v):
    return jax.nn.sigmoid(v)


def _tile_home(t):
    if t < 16:
        return "A", t
    if t < 24:
        return "B", t - 16
    if t < 26:
        return "B", t - 24 + 16
    if t < 34:
        return "B", t - 26 + 8
    if t < 50:
        return "C", t - 34
    return "D", t - 50


def _shard_runs(j):
    runs = []
    per_shard = SHARD // HALF_TILE
    for q in range(per_shard * j, per_shard * (j + 1)):
        g, gt = _tile_home(q // 2)
        row = gt * LANE + (q % 2) * HALF_TILE
        if runs and runs[-1][2] == g and runs[-1][3] + runs[-1][1] == row:
            runs[-1][1] += HALF_TILE
        else:
            runs.append([(q - per_shard * j) * HALF_TILE, HALF_TILE, g, row])
    return [tuple(r) for r in runs]


_DIMS = {
    "nn": (((1,), (0,)), ((), ())),
    "nt": (((1,), (1,)), ((), ())),
    "tn": (((0,), (0,)), ((), ())),
}


def _mm(a, b, mode, out_dtype, tm, tn, tk, name, acc=None, after=None):
    if mode == "nn":
        (m, k), n = a.shape, b.shape[1]
    elif mode == "nt":
        (m, k), n = a.shape, b.shape[0]
    else:
        (k, m), n = a.shape, b.shape[1]
    tm, tn, tk = min(tm, m), min(tn, n), min(tk, k)
    assert m % tm == 0 and n % tn == 0 and k % tk == 0, (name, m, n, k)
    nk = k // tk
    has_acc = acc is not None

    def body(*refs):
        a_ref, b_ref = refs[0], refs[1]
        o_ref = refs[3] if has_acc else refs[2]
        p = lax.dot_general(a_ref[...], b_ref[...], _DIMS[mode], preferred_element_type=F32)

        def finish(v):
            if has_acc:
                v = v + refs[2][...]
            o_ref[...] = v.astype(out_dtype)

        if nk == 1:
            finish(p)
        else:
            s_ref = refs[-1]
            kk = pl.program_id(2)

            @pl.when(kk == 0)
            def _():
                s_ref[...] = p

            @pl.when(kk > 0)
            def _():
                s_ref[...] += p

            @pl.when(kk == nk - 1)
            def _():
                finish(s_ref[...])

    if mode == "nn":
        a_spec = pl.BlockSpec((tm, tk), lambda i, j, kk: (i, kk))
        b_spec = pl.BlockSpec((tk, tn), lambda i, j, kk: (kk, j))
    elif mode == "nt":
        a_spec = pl.BlockSpec((tm, tk), lambda i, j, kk: (i, kk))
        b_spec = pl.BlockSpec((tn, tk), lambda i, j, kk: (j, kk))
    else:
        a_spec = pl.BlockSpec((tk, tm), lambda i, j, kk: (kk, i))
        b_spec = pl.BlockSpec((tk, tn), lambda i, j, kk: (kk, j))
    o_spec = pl.BlockSpec((tm, tn), lambda i, j, kk: (i, j))
    in_specs = [a_spec, b_spec] + ([o_spec] if has_acc else [])
    args = (a, b) + ((acc,) if has_acc else ())
    if after is not None:
        in_specs.append(pl.BlockSpec(memory_space=pl.ANY))
        args += (after,)
    n_in = len(args)
    kernel_body = body

    def body(*refs):
        kernel_body(*(refs[:n_in - (after is not None)] + refs[n_in:]))

    return pl.pallas_call(
        body,
        name=name,
        grid=(m // tm, n // tn, nk),
        in_specs=in_specs,
        out_specs=o_spec,
        out_shape=jax.ShapeDtypeStruct((m, n), out_dtype),
        scratch_shapes=[pltpu.VMEM((tm, tn), F32)] if nk > 1 else [],
        compiler_params=_params(("parallel", "parallel", "arbitrary")),
    )(*args)


def _rms_fwd(x, g, name, after, ts=256):
    r, d = x.shape

    def body(x_ref, g_ref, _, o_ref):
        xv = x_ref[...]
        inv = lax.rsqrt(jnp.mean(xv * xv, axis=-1, keepdims=True) + EPS)
        o_ref[...] = (xv * inv * g_ref[...]).astype(BF16)

    return pl.pallas_call(
        body,
        name=name,
        grid=(r // ts,),
        in_specs=[pl.BlockSpec((ts, d), lambda i: (i, 0)), pl.BlockSpec((1, d), lambda i: (0, 0)),
                  pl.BlockSpec(memory_space=pl.ANY)],
        out_specs=pl.BlockSpec((ts, d), lambda i: (i, 0)),
        out_shape=jax.ShapeDtypeStruct((r, d), BF16),
        compiler_params=_params(("parallel",)),
    )(x, g, after)


def _post_loss(out, x, tgt, g_post, ts=256):
    n = S // ts

    def body(o_ref, x_ref, t_ref, g_ref, sq_ref, dy_ref, do_ref, dg_ref):
        i = pl.program_id(0)

        @pl.when(i == 0)
        def _():
            sq_ref[...] = jnp.zeros_like(sq_ref)
            dg_ref[...] = jnp.zeros_like(dg_ref)

        ov = o_ref[...]
        g = g_ref[...]
        inv = lax.rsqrt(jnp.mean(ov * ov, axis=-1, keepdims=True) + EPS)
        on = ov * inv
        err = x_ref[...] + on * g - t_ref[...]
        sq_ref[...] += jnp.sum(err * err)
        dy = err * (1.0 / D)
        dy_ref[...] = dy
        dg_ref[...] += jnp.sum(dy * on, axis=0, keepdims=True)
        don = dy * g
        do_ref[...] = (inv * (don - on * jnp.mean(don * on, axis=-1, keepdims=True))).astype(BF16)

    row = pl.BlockSpec((ts, D), lambda i: (i, 0))
    vec = pl.BlockSpec((1, D), lambda i: (0, 0))
    return pl.pallas_call(
        body,
        name="post_loss",
        grid=(n,),
        in_specs=[row, row, row, vec],
        out_specs=[pl.BlockSpec((8, LANE), lambda i: (0, 0)), row, row, vec],
        out_shape=[
            jax.ShapeDtypeStruct((8, LANE), F32),
            jax.ShapeDtypeStruct((S, D), F32),
            jax.ShapeDtypeStruct((S, D), BF16),
            jax.ShapeDtypeStruct((1, D), F32),
        ],
        compiler_params=_params(("arbitrary",)),
    )(out, x, tgt, g_post)


def _pre_bwd(dh, x, dy, g_pre, ts=256):
    n = S // ts

    def body(dh_ref, x_ref, dy_ref, g_ref, gx_ref, dg_ref):
        i = pl.program_id(0)

        @pl.when(i == 0)
        def _():
            dg_ref[...] = jnp.zeros_like(dg_ref)

        xv = x_ref[...]
        dhv = dh_ref[...]
        inv = lax.rsqrt(jnp.mean(xv * xv, axis=-1, keepdims=True) + EPS)
        xn = xv * inv
        dg_ref[...] += jnp.sum(dhv * xn, axis=0, keepdims=True)
        dxn = dhv * g_ref[...]
        gx_ref[...] = dy_ref[...] + inv * (dxn - xn * jnp.mean(dxn * xn, axis=-1, keepdims=True))

    row = pl.BlockSpec((ts, D), lambda i: (i, 0))
    vec = pl.BlockSpec((1, D), lambda i: (0, 0))
    return pl.pallas_call(
        body,
        name="pre_bwd",
        grid=(n,),
        in_specs=[row, row, row, vec],
        out_specs=[row, vec],
        out_shape=[jax.ShapeDtypeStruct((S, D), F32), jax.ShapeDtypeStruct((1, D), F32)],
        compiler_params=_params(("arbitrary",)),
    )(dh, x, dy, g_pre)


def _memnorm_bwd(dmemn, mem):
    def body(d_ref, m_ref, dg_ref):
        mv = m_ref[...]
        inv = lax.rsqrt(jnp.mean(mv * mv, axis=-1, keepdims=True) + EPS)
        dg_ref[...] = jnp.sum(d_ref[...] * mv * inv, axis=0, keepdims=True)

    return pl.pallas_call(
        body,
        name="memnorm_bwd",
        out_shape=jax.ShapeDtypeStruct((1, D), F32),
        compiler_params=_params(),
    )(dmemn, mem)


T_RNN = 256


def _neg_expm1(z):
    poly = -z * (1.0 + z * (0.5 + z * (1.0 / 6 + z * (1.0 / 24 + z * (1.0 / 120 + z * (1.0 / 720))))))
    return jnp.where(z > -0.1, poly, 1.0 - jnp.exp(z))


def _softplus_neg(lam):
    return jnp.maximum(-lam, 0.0) + jnp.log1p(jnp.exp(-jnp.abs(lam)))


def _rnn_gates(conv, wa_ref, ba, wx_ref, bx, lam, first_row):
    cbf = conv.astype(BF16)
    ga, gx = [], []
    for n in range(RNN_BLOCKS):
        c_n = cbf[:, n * LANE:(n + 1) * LANE]
        ga.append(jnp.dot(c_n, wa_ref[n], preferred_element_type=F32))
        gx.append(jnp.dot(c_n, wx_ref[n], preferred_element_type=F32))
    gate_r = _sigmoid(jnp.concatenate(ga, axis=1) + ba)
    gate_i = _sigmoid(jnp.concatenate(gx, axis=1) + bx)
    sp = _softplus_neg(lam)
    log_a = -LRU_C * gate_r * sp
    a = jnp.exp(log_a)
    mult_raw = jnp.sqrt(_neg_expm1(2.0 * log_a))
    mult = jnp.where(first_row, 1.0, mult_raw)
    return cbf, gate_r, gate_i, sp, a, mult_raw, mult


def _rglru_fwd(p_a, conv_w, conv_b, wa, ba, wx, bx, lam):
    t = T_RNN
    n = S // t

    def body(xr_ref, g_ref, cw_ref, cb_ref, wa_ref, ba_ref, wx_ref, bx_ref, lam_ref,
             y_ref, h_ref, xp_s, hcar, a_s, b_s):
        i = pl.program_id(0)

        @pl.when(i == 0)
        def _():
            xp_s[0:8, :] = jnp.zeros((8, D_RNN), F32)
            hcar[...] = jnp.zeros_like(hcar)

        @pl.when(i > 0)
        def _():
            xp_s[0:8, :] = xp_s[t:t + 8, :]

        xp_s[8:8 + t, :] = xr_ref[...]
        conv = cb_ref[...]
        for k in range(CONV_W):
            conv = conv + cw_ref[k:k + 1, :] * xp_s[8 - k:8 - k + t, :]
        rows = i * t + lax.broadcasted_iota(jnp.int32, (t, 1), 0)
        _, _, gate_i, _, a, _, mult = _rnn_gates(
            conv, wa_ref, ba_ref[...], wx_ref, bx_ref[...], lam_ref[...], rows == 0)
        a_s[...] = a
        b_s[...] = mult * gate_i * conv

        def step(tt, h):
            h = a_s[pl.ds(tt, 1), :] * h + b_s[pl.ds(tt, 1), :]
            h_ref[pl.ds(tt, 1), :] = h
            return h

        hcar[...] = lax.fori_loop(0, t, step, hcar[...], unroll=8)
        g = g_ref[...]
        y_ref[...] = (h_ref[...] * (g * _sigmoid(g))).astype(BF16)

    blk = lambda c: pl.BlockSpec((t, D_RNN), lambda i: (i, c))
    full = lambda shape: pl.BlockSpec(shape, lambda i: (0,) * len(shape))
    return pl.pallas_call(
        body,
        name="rglru_fwd",
        grid=(n,),
        in_specs=[blk(0), blk(1), full((CONV_W, D_RNN)), full((1, D_RNN)),
                  full((RNN_BLOCKS, LANE, LANE)), full((1, D_RNN)),
                  full((RNN_BLOCKS, LANE, LANE)), full((1, D_RNN)), full((1, D_RNN))],
        out_specs=[blk(0), blk(0)],
        out_shape=[jax.ShapeDtypeStruct((S, D_RNN), BF16), jax.ShapeDtypeStruct((S, D_RNN), F32)],
        scratch_shapes=[pltpu.VMEM((t + 8, D_RNN), F32), pltpu.VMEM((1, D_RNN), F32),
                        pltpu.VMEM((t, D_RNN), F32), pltpu.VMEM((t, D_RNN), F32)],
        compiler_params=_params(("arbitrary",)),
    )(p_a, p_a, conv_w, conv_b, wa, ba, wx, bx, lam)


def _rglru_bwd(dy, p_a, hseq, conv_w, conv_b, wa, ba, wx, bx, lam):
    t = T_RNN
    n = S // t
    rb = t // 8

    def body(dy_ref, xr_ref, g_ref, h_ref, xrp_ref, hp_ref, cw_ref, cb_ref, wa_ref, ba_ref, wx_ref, bx_ref, lam_ref,
             dp_ref, dcw_ref, dcb_ref, dwa_ref, dba_ref, dwx_ref, dbx_ref, dlam_ref,
             xp_s, hp_s, dxp_s, lamcar, a_s, dh_s, lam_s):
        i = pl.program_id(0)
        r = n - 1 - i

        @pl.when(i == 0)
        def _():
            for ref in (dcw_ref, dcb_ref, dwa_ref, dba_ref, dwx_ref, dbx_ref, dlam_ref, lamcar):
                ref[...] = jnp.zeros_like(ref)
            dxp_s[t:t + 8, :] = jnp.zeros((8, D_RNN), F32)

        @pl.when(i > 0)
        def _():
            dxp_s[t:t + 8, :] = dxp_s[0:8, :]

        has_prev = r > 0
        xp_s[0:8, :] = jnp.where(has_prev, xrp_ref[...], 0.0)
        xp_s[8:8 + t, :] = xr_ref[...]
        hp_s[0:8, :] = jnp.where(has_prev, hp_ref[...], 0.0)
        hp_s[8:8 + t, :] = h_ref[...]
        xs = [xp_s[8 - k:8 - k + t, :] for k in range(CONV_W)]
        conv = cb_ref[...]
        for k in range(CONV_W):
            conv = conv + cw_ref[k:k + 1, :] * xs[k]
        rows = r * t + lax.broadcasted_iota(jnp.int32, (t, 1), 0)
        first = rows == 0
        lam_p = lam_ref[...]
        cbf, gate_r, gate_i, sp, a, mult_raw, mult = _rnn_gates(
            conv, wa_ref, ba_ref[...], wx_ref, bx_ref[...], lam_p, first)

        g = g_ref[...]
        sg = _sigmoid(g)
        dyv = dy_ref[...]
        a_s[...] = a
        dh_s[...] = dyv * (g * sg)
        dg = dyv * h_ref[...] * (sg * (1.0 + g * (1.0 - sg)))

        def step(jj, car):
            tt = t - 1 - jj
            lm = dh_s[pl.ds(tt, 1), :] + car
            lam_s[pl.ds(tt, 1), :] = lm
            return a_s[pl.ds(tt, 1), :] * lm

        lamcar[...] = lax.fori_loop(0, t, step, lamcar[...], unroll=8)
        db = lam_s[...]
        da = db * hp_s[7:7 + t, :]
        dmult = db * gate_i * conv
        dgate_i = db * mult * conv
        dconv = db * mult * gate_i
        dlog_a = da * a + jnp.where(first, 0.0, dmult * (-(a * a) / mult_raw))
        dgate_r = dlog_a * (-LRU_C * sp)
        dsp = jnp.sum(dlog_a * (-LRU_C * gate_r), axis=0, keepdims=True)
        dlam_ref[...] += dsp * (-_sigmoid(-lam_p))
        dga = dgate_r * gate_r * (1.0 - gate_r)
        dgx = dgate_i * gate_i * (1.0 - gate_i)
        dba_ref[...] += jnp.sum(dga, axis=0, keepdims=True)
        dbx_ref[...] += jnp.sum(dgx, axis=0, keepdims=True)
        dga16, dgx16 = dga.astype(BF16), dgx.astype(BF16)
        back = []
        for nb in range(RNN_BLOCKS):
            sl = slice(nb * LANE, (nb + 1) * LANE)
            dwa_ref[nb] += lax.dot_general(cbf[:, sl], dga16[:, sl], _DIMS["tn"], preferred_element_type=F32)
            dwx_ref[nb] += lax.dot_general(cbf[:, sl], dgx16[:, sl], _DIMS["tn"], preferred_element_type=F32)
            back.append(lax.dot_general(dga16[:, sl], wa_ref[nb], _DIMS["nt"], preferred_element_type=F32)
                        + lax.dot_general(dgx16[:, sl], wx_ref[nb], _DIMS["nt"], preferred_element_type=F32))
        dconv = dconv + jnp.concatenate(back, axis=1)
        dcb_ref[...] += jnp.sum(dconv, axis=0, keepdims=True)
        for k in range(CONV_W):
            dcw_ref[k:k + 1, :] += jnp.sum(dconv * xs[k], axis=0, keepdims=True)
        dxp_s[0:t, :] = dconv
        dxr = cw_ref[0:1, :] * dconv
        for k in range(1, CONV_W):
            dxr = dxr + cw_ref[k:k + 1, :] * dxp_s[k:k + t, :]
        dp_ref[:, 0:D_RNN] = dxr.astype(BF16)
        dp_ref[:, D_RNN:2 * D_RNN] = dg.astype(BF16)

    blk = lambda c: pl.BlockSpec((t, D_RNN), lambda i: (n - 1 - i, c))
    prev8 = pl.BlockSpec((8, D_RNN), lambda i: (jnp.maximum((n - 1 - i) * rb - 1, 0), 0))
    full = lambda shape: pl.BlockSpec(shape, lambda i: (0,) * len(shape))
    vec = full((1, D_RNN))
    mat = full((RNN_BLOCKS, LANE, LANE))
    return pl.pallas_call(
        body,
        name="rglru_bwd",
        grid=(n,),
        in_specs=[blk(0), blk(0), blk(1), blk(0), prev8, prev8,
                  full((CONV_W, D_RNN)), vec, mat, vec, mat, vec, vec],
        out_specs=[pl.BlockSpec((t, 2 * D_RNN), lambda i: (n - 1 - i, 0)),
                   full((CONV_W, D_RNN)), vec, mat, vec, mat, vec, vec],
        out_shape=[jax.ShapeDtypeStruct((S, 2 * D_RNN), BF16),
                   jax.ShapeDtypeStruct((CONV_W, D_RNN), F32), jax.ShapeDtypeStruct((1, D_RNN), F32),
                   jax.ShapeDtypeStruct((RNN_BLOCKS, LANE, LANE), F32), jax.ShapeDtypeStruct((1, D_RNN), F32),
                   jax.ShapeDtypeStruct((RNN_BLOCKS, LANE, LANE), F32), jax.ShapeDtypeStruct((1, D_RNN), F32),
                   jax.ShapeDtypeStruct((1, D_RNN), F32)],
        scratch_shapes=[pltpu.VMEM((t + 8, D_RNN), F32), pltpu.VMEM((t + 8, D_RNN), F32),
                        pltpu.VMEM((t + 8, D_RNN), F32), pltpu.VMEM((1, D_RNN), F32),
                        pltpu.VMEM((t, D_RNN), F32), pltpu.VMEM((t, D_RNN), F32), pltpu.VMEM((t, D_RNN), F32)],
        compiler_params=_params(("arbitrary",)),
    )(dy, p_a, p_a, hseq, p_a, hseq, conv_w, conv_b, wa, ba, wx, bx, lam)


QB = WINDOW
KB2 = 2 * WINDOW
N_QB = S // QB
N_PAIR = SWA_HEADS // 2


def _swa_keys(kvc_ref, kvp_ref):
    kk = jnp.concatenate([kvp_ref[:, 0:LANE], kvc_ref[:, 0:LANE]], axis=0)
    vv = jnp.concatenate([kvp_ref[:, LANE:2 * LANE], kvc_ref[:, LANE:2 * LANE]], axis=0)
    lo = lax.broadcasted_iota(jnp.int32, (1, LANE), 1) < SWA_HD
    kk_sw, vv_sw = pltpu.roll(kk, SWA_HD, 1), pltpu.roll(vv, SWA_HD, 1)
    kd = [jnp.where(lo, kk, kk_sw).astype(BF16), jnp.where(lo, kk_sw, kk).astype(BF16)]
    vd = [jnp.where(lo, vv, vv_sw).astype(BF16), jnp.where(lo, vv_sw, vv).astype(BF16)]
    return lo, kd, vd


GRP = SWA_HEADS // 2
STACK = GRP
GQ = STACK * QB


def _swa_valid(n, rows):
    qi = lax.broadcasted_iota(jnp.int32, (rows, KB2), 0) % QB
    kj = lax.broadcasted_iota(jnp.int32, (rows, KB2), 1)
    dist = qi + WINDOW - kj
    return (dist >= 0) & (dist < WINDOW) & ((n > 0) | (kj >= WINDOW))


def _swa_stack(tile_of, lo, h0, masked):
    parts = []
    for h in range(h0, h0 + STACK):
        t = tile_of(h // 2)
        if masked:
            t = jnp.where(lo if h % 2 == 0 else jnp.logical_not(lo), t, 0.0)
        parts.append(t)
    return jnp.concatenate(parts, axis=0)


def _swa_unstack(stacked, lo, pair):
    return jnp.where(lo, stacked[2 * pair * QB:(2 * pair + 1) * QB], stacked[(2 * pair + 1) * QB:(2 * pair + 2) * QB])


def _swa_softmax(lg, sink, valid):
    lg = jnp.where(valid, lg, NEG_INF)
    m = jnp.maximum(jnp.max(lg, axis=-1, keepdims=True), sink)
    p = jnp.exp(lg - m)
    es = jnp.exp(sink - m)
    den = jnp.sum(p, axis=-1, keepdims=True) + es
    return p / den, es / den


def _swa_probs_head(qh16, kd, bias, sink, valid):
    lg = lax.dot_general(qh16, kd, _DIMS["nt"], preferred_element_type=F32) * (SWA_HD ** -0.5) + bias
    return _swa_softmax(lg, sink, valid)[0]


def _swa_probs(q16, kd, bias_ref, sink_ref, h0, valid):
    bias = bias_ref[h0:h0 + STACK].reshape(GQ, KB2)
    sink = jnp.concatenate([jnp.full((QB, 1), sink_ref[h], F32) for h in range(h0, h0 + STACK)], axis=0)
    lg = lax.dot_general(q16, kd, _DIMS["nt"], preferred_element_type=F32) * (SWA_HD ** -0.5) + bias
    return _swa_softmax(lg, sink, valid)


def _swa_specs():
    q = pl.BlockSpec((QB, D_RNN), lambda n: (n, 0))
    g = pl.BlockSpec((QB, D_RNN), lambda n: (n, 1))
    kvc = pl.BlockSpec((QB, 2 * LANE), lambda n: (n, 8))
    kvp = pl.BlockSpec((QB, 2 * LANE), lambda n: (jnp.maximum(n - 1, 0), 8))
    bias = pl.BlockSpec((SWA_HEADS, QB, KB2), lambda n: (0, 0, 0))
    sinks = pl.BlockSpec(memory_space=pltpu.SMEM)
    return q, g, kvc, kvp, bias, sinks


def _swa_fwd(p_b, bias_t, sinks):
    def body(q_ref, g_ref, kvc_ref, kvp_ref, bias_ref, sink_ref, y_ref, o_ref):
        n = pl.program_id(0)
        lo, kd, vd = _swa_keys(kvc_ref, kvp_ref)
        valid = _swa_valid(n, QB)
        for hp in range(N_PAIR):
            sl = slice(hp * LANE, (hp + 1) * LANE)
            kvh = hp // (N_PAIR // 2)
            q = q_ref[:, sl]
            outs = []
            for j in range(2):
                qh16 = jnp.where(lo if j == 0 else jnp.logical_not(lo), q, 0.0).astype(BF16)
                probs = _swa_probs_head(qh16, kd[kvh], bias_ref[2 * hp + j], sink_ref[2 * hp + j], valid)
                outs.append(jnp.dot(probs.astype(BF16), vd[kvh], preferred_element_type=F32))
            o = jnp.where(lo, outs[0], outs[1])
            o_ref[:, sl] = o
            g = g_ref[:, sl]
            y_ref[:, sl] = (o * (g * _sigmoid(g))).astype(BF16)

    q, g, kvc, kvp, bias, sinks_spec = _swa_specs()
    out = pl.BlockSpec((QB, D_RNN), lambda n: (n, 0))
    return pl.pallas_call(
        body,
        name="swa_fwd",
        grid=(N_QB,),
        in_specs=[q, g, kvc, kvp, bias, sinks_spec],
        out_specs=[out, out],
        out_shape=[jax.ShapeDtypeStruct((S, D_RNN), BF16), jax.ShapeDtypeStruct((S, D_RNN), F32)],
        compiler_params=_params(("parallel",)),
    )(p_b, p_b, p_b, p_b, bias_t, sinks)


def _swa_bwd(dy, p_b, o_swa, bias_t, sinks, after=None):
    def body(dy_ref, q_ref, g_ref, kvc_ref, kvp_ref, o_ref, bias_ref, sink_ref, *rest):
        dp_ref, dk_ref, dv_ref, dbias_ref, dsink_ref, do_s = rest[-6:]
        n = pl.program_id(0)

        @pl.when(n == 0)
        def _():
            for ref in (dk_ref, dv_ref, dbias_ref, dsink_ref):
                ref[...] = jnp.zeros_like(ref)

        lo, kd, vd = _swa_keys(kvc_ref, kvp_ref)
        hi = jnp.logical_not(lo)
        valid = _swa_valid(n, GQ)
        tile = lambda ref: (lambda hp: ref[:, hp * LANE:(hp + 1) * LANE])
        for hp in range(N_PAIR):
            sl = slice(hp * LANE, (hp + 1) * LANE)
            g, dyv = g_ref[:, sl], dy_ref[:, sl]
            sg = _sigmoid(g)
            do_s[:, sl] = dyv * (g * sg)
            dp_ref[:, D_RNN + hp * LANE:D_RNN + (hp + 1) * LANE] = (
                dyv * o_ref[:, sl] * (sg * (1.0 + g * (1.0 - sg)))).astype(BF16)

        dk_blk = jnp.zeros((KB2, LANE), F32)
        dv_blk = jnp.zeros((KB2, LANE), F32)
        for h0 in range(0, SWA_HEADS, STACK):
            kvh = h0 // GRP
            q16 = _swa_stack(tile(q_ref), lo, h0, masked=True).astype(BF16)
            do8 = _swa_stack(tile(do_s), lo, h0, masked=True)
            do16 = do8.astype(BF16)
            delta = jnp.sum(do8 * _swa_stack(tile(o_ref), lo, h0, masked=False), axis=-1, keepdims=True)
            probs, psink = _swa_probs(q16, kd[kvh], bias_ref, sink_ref, h0, valid)
            dpr = lax.dot_general(do16, vd[kvh], _DIMS["nt"], preferred_element_type=F32)
            ds = probs * (dpr - delta)
            sink_term = psink * delta
            for g in range(STACK):
                h, rows = h0 + g, slice(g * QB, (g + 1) * QB)
                dbias_ref[h] += ds[rows]
                dsink_ref[h:h + 1, :] += jnp.zeros((1, LANE), F32) - jnp.sum(sink_term[rows])
            ds16 = (ds * (SWA_HD ** -0.5)).astype(BF16)
            dq_all = jnp.dot(ds16, kd[kvh], preferred_element_type=F32)
            for pair in range(STACK // 2):
                sl = slice((h0 // 2 + pair) * LANE, (h0 // 2 + pair + 1) * LANE)
                dp_ref[:, sl] = _swa_unstack(dq_all, lo, pair).astype(BF16)
            dk_pair = lax.dot_general(ds16, q16, _DIMS["tn"], preferred_element_type=F32)
            dv_pair = lax.dot_general(probs.astype(BF16), do16, _DIMS["tn"], preferred_element_type=F32)
            keep = lo if kvh == 0 else hi
            dk_blk = dk_blk + jnp.where(keep, dk_pair + pltpu.roll(dk_pair, SWA_HD, 1), 0.0)
            dv_blk = dv_blk + jnp.where(keep, dv_pair + pltpu.roll(dv_pair, SWA_HD, 1), 0.0)

        cur = pl.ds(pl.multiple_of(n * QB, QB), QB)
        dk_ref[cur, :] += dk_blk[QB:KB2]
        dv_ref[cur, :] += dv_blk[QB:KB2]

        @pl.when(n > 0)
        def _():
            prev = pl.ds(pl.multiple_of((n - 1) * QB, QB), QB)
            dk_ref[prev, :] += dk_blk[0:QB]
            dv_ref[prev, :] += dv_blk[0:QB]

    q, g, kvc, kvp, bias, sinks_spec = _swa_specs()
    row = pl.BlockSpec((QB, D_RNN), lambda n: (n, 0))
    acc = pl.BlockSpec((S, LANE), lambda n: (0, 0))
    return pl.pallas_call(
        body,
        name="swa_bwd",
        grid=(N_QB,),
        in_specs=[row, q, g, kvc, kvp, row, bias, sinks_spec] + ([ANY] if after is not None else []),
        out_specs=[pl.BlockSpec((QB, 2 * D_RNN), lambda n: (n, 0)), acc, acc, bias,
                   pl.BlockSpec((SWA_HEADS, LANE), lambda n: (0, 0))],
        out_shape=[jax.ShapeDtypeStruct((S, GROUP_TILES["B"] * LANE), BF16),
                   jax.ShapeDtypeStruct((S, LANE), F32), jax.ShapeDtypeStruct((S, LANE), F32),
                   jax.ShapeDtypeStruct((SWA_HEADS, QB, KB2), F32),
                   jax.ShapeDtypeStruct((SWA_HEADS, LANE), F32)],
        scratch_shapes=[pltpu.VMEM((QB, D_RNN), F32)],
        compiler_params=_params(("arbitrary",)),
    )(dy, p_b, p_b, p_b, p_b, o_swa, bias_t, sinks, *([after] if after is not None else []))


def _swa_pack(dp_b, dk, dv, ts=512):
    def body(_, dk_ref, dv_ref, o_ref):
        o_ref[:, 0:LANE] = dk_ref[...].astype(BF16)
        o_ref[:, LANE:2 * LANE] = dv_ref[...].astype(BF16)

    tile = pl.BlockSpec((ts, LANE), lambda i: (i, 0))
    return pl.pallas_call(
        body,
        name="swa_pack",
        grid=(S // ts,),
        in_specs=[pl.BlockSpec(memory_space=pl.ANY), tile, tile],
        out_specs=pl.BlockSpec((ts, 2 * LANE), lambda i: (i, 8)),
        out_shape=jax.ShapeDtypeStruct(dp_b.shape, dp_b.dtype),
        input_output_aliases={0: 0},
        compiler_params=_params(("parallel",)),
    )(dp_b, dk, dv)


def _split3(v):
    a = v.astype(BF16)
    r = v - a.astype(F32)
    b = r.astype(BF16)
    c = (r - b.astype(F32)).astype(BF16)
    return a, b, c


def _relbias_grad(dbias_flat, onehot_t):
    def body(d_ref, e_ref, o_ref):
        e = e_ref[...]
        acc = jnp.zeros((SWA_HEADS, REL_BUCKETS), F32)
        for term in _split3(d_ref[...]):
            acc = acc + lax.dot_general(term, e, _DIMS["nt"], preferred_element_type=F32)
        o_ref[...] = acc

    return pl.pallas_call(
        body,
        name="relbias_grad",
        out_shape=jax.ShapeDtypeStruct((SWA_HEADS, REL_BUCKETS), F32),
        compiler_params=_params(),
    )(dbias_flat, onehot_t)


TS_MEM = 512


def _mem_probs(q16, mk):
    lg = lax.dot_general(q16, mk, _DIMS["nt"], preferred_element_type=F32) * (MEM_HD ** -0.5)
    p = jnp.exp(lg - jnp.max(lg, axis=-1, keepdims=True))
    return p / jnp.sum(p, axis=-1, keepdims=True)


def _mem_fwd(p_c, mkv):
    def body(q_ref, g_ref, mkv_ref, y_ref, o_ref):
        for hm in range(MEM_HEADS):
            sl = slice(hm * MEM_HD, (hm + 1) * MEM_HD)
            probs = _mem_probs(q_ref[:, sl].astype(BF16), mkv_ref[:, sl])
            o = jnp.dot(probs.astype(BF16), mkv_ref[:, D_RNN + hm * MEM_HD:D_RNN + (hm + 1) * MEM_HD],
                        preferred_element_type=F32)
            o_ref[:, sl] = o
            g = g_ref[:, sl]
            y_ref[:, sl] = (o * (g * _sigmoid(g))).astype(BF16)

    blk = lambda c: pl.BlockSpec((TS_MEM, D_RNN), lambda i: (i, c))
    return pl.pallas_call(
        body,
        name="mem_fwd",
        grid=(S // TS_MEM,),
        in_specs=[blk(0), blk(1), pl.BlockSpec((MEM, 2 * D_RNN), lambda i: (0, 0))],
        out_specs=[blk(0), blk(0)],
        out_shape=[jax.ShapeDtypeStruct((S, D_RNN), BF16), jax.ShapeDtypeStruct((S, D_RNN), F32)],
        compiler_params=_params(("parallel",)),
    )(p_c, p_c, mkv)


def _mem_bwd(dy, p_c, o_mem, mkv):
    def body(dy_ref, q_ref, g_ref, o_ref, mkv_ref, dp_ref, dmkv_ref):
        @pl.when(pl.program_id(0) == 0)
        def _():
            dmkv_ref[...] = jnp.zeros_like(dmkv_ref)

        for hm in range(MEM_HEADS):
            sl = slice(hm * MEM_HD, (hm + 1) * MEM_HD)
            sv = slice(D_RNN + hm * MEM_HD, D_RNN + (hm + 1) * MEM_HD)
            q16 = q_ref[:, sl].astype(BF16)
            mk, mv = mkv_ref[:, sl], mkv_ref[:, sv]
            probs = _mem_probs(q16, mk)
            g, o, dyv = g_ref[:, sl], o_ref[:, sl], dy_ref[:, sl]
            sg = _sigmoid(g)
            do = dyv * (g * sg)
            dp_ref[:, sv] = (dyv * o * (sg * (1.0 + g * (1.0 - sg)))).astype(BF16)
            do16 = do.astype(BF16)
            delta = jnp.sum(do * o, axis=-1, keepdims=True)
            dpr = lax.dot_general(do16, mv, _DIMS["nt"], preferred_element_type=F32)
            ds16 = (probs * (dpr - delta) * (MEM_HD ** -0.5)).astype(BF16)
            dp_ref[:, sl] = jnp.dot(ds16, mk, preferred_element_type=F32).astype(BF16)
            dmkv_ref[:, sl] += lax.dot_general(ds16, q16, _DIMS["tn"], preferred_element_type=F32)
            dmkv_ref[:, sv] += lax.dot_general(probs.astype(BF16), do16, _DIMS["tn"], preferred_element_type=F32)

    blk = lambda c: pl.BlockSpec((TS_MEM, D_RNN), lambda i: (i, c))
    kv = pl.BlockSpec((MEM, 2 * D_RNN), lambda i: (0, 0))
    return pl.pallas_call(
        body,
        name="mem_bwd",
        grid=(S // TS_MEM,),
        in_specs=[blk(0), blk(0), blk(1), blk(0), kv],
        out_specs=[pl.BlockSpec((TS_MEM, 2 * D_RNN), lambda i: (i, 0)), kv],
        out_shape=[jax.ShapeDtypeStruct((S, 2 * D_RNN), BF16), jax.ShapeDtypeStruct((MEM, 2 * D_RNN), F32)],
        compiler_params=_params(("arbitrary",)),
    )(dy, p_c, p_c, o_mem, mkv)


TS_MRG = 512
TD_MRG = 1024
N_DBLK = D // TD_MRG


def _merge_fwd(z, p_d):
    def body(z0, z1, z2, g0, g1, g2, o_ref):
        term = lambda g, z: _sigmoid(g[...].astype(F32)) * z[...].astype(F32)
        o_ref[...] = (term(g0, z0) + term(g1, z1) + term(g2, z2)).astype(BF16)

    blk = pl.BlockSpec((TS_MRG, TD_MRG), lambda i, d: (i, d))
    gate = lambda b: pl.BlockSpec((TS_MRG, TD_MRG), lambda i, d: (i, b * N_DBLK + d))
    return pl.pallas_call(
        body,
        name="merge_fwd",
        grid=(S // TS_MRG, N_DBLK),
        in_specs=[blk, blk, blk, gate(0), gate(1), gate(2)],
        out_specs=blk,
        out_shape=jax.ShapeDtypeStruct((S, D), BF16),
        compiler_params=_params(("parallel", "parallel")),
    )(z[0], z[1], z[2], p_d, p_d, p_d)


TS_MRG_BWD = 128


def _merge_bwd(dmerged, z, p_d, after):
    def body(dm_ref, z0, z1, z2, g_ref, _, dz0, dz1, dz2, dg_ref):
        dm = dm_ref[...]
        for b, (z_ref, dz_ref) in enumerate(((z0, dz0), (z1, dz1), (z2, dz2))):
            cols = slice(b * D, (b + 1) * D)
            sg = _sigmoid(g_ref[:, cols].astype(F32))
            dz_ref[...] = (dm * sg).astype(BF16)
            dg_ref[:, cols] = (dm * z_ref[...].astype(F32) * sg * (1.0 - sg)).astype(BF16)

    row = pl.BlockSpec((TS_MRG_BWD, D), lambda i: (i, 0))
    wide = pl.BlockSpec((TS_MRG_BWD, 3 * D), lambda i: (i, 0))
    outs = pl.pallas_call(
        body,
        name="merge_bwd",
        grid=(S // TS_MRG_BWD,),
        in_specs=[row, row, row, row, wide, pl.BlockSpec(memory_space=pl.ANY)],
        out_specs=[row, row, row, wide],
        out_shape=[jax.ShapeDtypeStruct((S, D), BF16)] * 3 + [jax.ShapeDtypeStruct((S, 3 * D), BF16)],
        compiler_params=_params(("parallel",)),
    )(dmerged, z[0], z[1], z[2], p_d, after)
    return list(outs[:3]), outs[3]


def _bucket_table():
    import numpy as np
    qi = np.arange(QB)[:, None]
    kj = np.arange(KB2)[None, :]
    n = np.maximum(qi + WINDOW - kj, 0)
    max_exact = REL_BUCKETS // 2
    ratio = np.log(np.maximum(n, 1).astype(np.float32) / max_exact) / np.float32(math.log(REL_MAX_DIST / max_exact))
    large = np.minimum(max_exact + (ratio * (REL_BUCKETS - max_exact)).astype(np.int32), REL_BUCKETS - 1)
    bucket = np.where(n < max_exact, n, large).reshape(1, QB * KB2)
    return (bucket == np.arange(REL_BUCKETS)[:, None]).astype(np.float32)


def _bias_expand(rel_bias_t, onehot_t):
    def body(r_ref, e_ref, o_ref):
        e = e_ref[...]
        acc = jnp.zeros((SWA_HEADS, QB * KB2), F32)
        for term in _split3(r_ref[...]):
            acc = acc + jnp.dot(term, e, preferred_element_type=F32)
        o_ref[...] = acc

    return pl.pallas_call(
        body,
        name="bias_expand",
        out_shape=jax.ShapeDtypeStruct((SWA_HEADS, QB * KB2), F32),
        compiler_params=_params(),
    )(rel_bias_t, onehot_t)


PROJ_TN = {"A": 1024, "B": 1152, "C": 1024, "D": 1536}


def _do_first(arrays, token):
    def body(*refs):
        refs[-1][...] = jnp.zeros_like(refs[-1])

    return pl.pallas_call(
        body,
        name="do_first",
        in_specs=[pl.BlockSpec(memory_space=pl.ANY)] * (len(arrays) + 1),
        out_specs=pl.BlockSpec(memory_space=pltpu.VMEM),
        out_shape=jax.ShapeDtypeStruct((8, LANE), F32),
    )(*arrays, token)


def _local_step(x, h, mem, tgt, sp, early, fetch, prefetch, emit, advance):
    onehot_t = jnp.asarray(_bucket_table(), BF16)
    bias_t = _bias_expand(sp["rel_bias"].T, onehot_t).reshape(SWA_HEADS, QB, KB2)
    sinks = sp["swa_sinks"].reshape(SWA_HEADS)
    wa16, wx16 = sp["w_rg_a"].astype(BF16), sp["w_rg_x"].astype(BF16)
    rnn = (sp["conv_w"], sp["conv_b"], wa16, sp["b_rg_a"], wx16, sp["b_rg_x"], sp["lru_lambda"])

    memn = _rms_fwd(mem, sp["mem_norm_g"], "rms_mem", h)
    h_and_prep = _do_first([bias_t, memn, wa16, wx16, *early], h)
    w_grp, p = {}, {}

    def project(g, after, then=None):
        (w_grp[g],) = fetch((g,), after)
        tok = prefetch(then, w_grp[g]) if then is not None else None
        p[g] = _mm(h, w_grp[g], "nt", BF16 if g == "D" else F32, 1024, PROJ_TN[g], D, f"proj_{g}", after=tok)

    project("A", h_and_prep)
    y_rg, hseq = _rglru_fwd(p["A"], *rnn)
    project("B", y_rg)
    y_swa, o_swa = _swa_fwd(p["B"], bias_t, sinks)
    project("C", y_swa, then=("mk",))
    (wmk,) = fetch(("mk",), p["C"])
    tok = prefetch(("br0", "br1", "br2"), wmk)
    mkv = _mm(memn, wmk, "nn", BF16, MEM, 1024, D, "mkv", after=tok)
    y_mem, o_mem = _mem_fwd(p["C"], mkv)
    ys = (y_rg, y_swa, y_mem)
    wbr = fetch(("br0", "br1", "br2"), y_mem)
    tok = prefetch(("D",), wbr[2])
    z = []
    for b in range(3):
        z.append(_mm(ys[b], wbr[b], "nn", BF16, 1024, 1024, D_RNN, f"branch_out{b}", after=z[-1] if z else tok))
    project("D", z[2], then=("out",))
    merged = _merge_fwd(z, p["D"])
    (wout,) = fetch(("out",), merged)
    out = _mm(merged, wout, "nn", F32, 1024, 1024, D, "out_proj")
    sq, dy, dout, d_post = _post_loss(out, x, tgt, sp["post_norm_g"])

    tok = emit({"out": _mm(merged, dout, "tn", BF16, 1024, 1024, S, "d_wout")})
    dmerged = _mm(dout, wout, "nt", F32, 1024, 1024, D, "d_merged", after=tok)
    tok = advance(dmerged)
    dz, dp_d = _merge_bwd(dmerged, z, p["D"], tok)
    d_win = lambda g, dp_g, after=None: _mm(dp_g, h, "tn", BF16, PROJ_TN[g], 1024, S, f"d_win_{g}", after=after)
    tok = emit({f"br{b}": _mm(ys[b], dz[b], "tn", BF16, 1024, 1024, S, f"d_wbr{b}") for b in range(3)}, tok)
    d_w_d = d_win("D", dp_d, tok)
    tok = emit({"D": d_w_d}, advance(d_w_d))
    dy_mem = _mm(dz[2], wbr[2], "nt", F32, 1024, 1024, D, "d_branch2", after=tok)
    tok = advance(dy_mem)
    dp_c, dmkv = _mem_bwd(dy_mem, p["C"], o_mem, mkv)
    dmkv16 = dmkv.astype(BF16)
    tok = emit({"mk": _mm(memn, dmkv16, "tn", BF16, 1024, 1024, MEM, "d_wmk", after=tok), "C": d_win("C", dp_c)}, tok)
    dmemn = _mm(dmkv16, wmk, "nt", F32, MEM, 1024, D, "d_memn", after=tok)
    tok = advance(dmemn)
    d_memg = _memnorm_bwd(dmemn, mem)
    dy_rg = _mm(dz[0], wbr[0], "nt", F32, 1024, 1024, D, "d_branch0", after=tok)
    dp_a, d_cw, d_cb, d_wa, d_ba, d_wx, d_bx, d_lam = _rglru_bwd(dy_rg, p["A"], hseq, *rnn)
    tok = emit({"A": d_win("A", dp_a)}, tok)
    dy_swa = _mm(dz[1], wbr[1], "nt", F32, 1024, 1024, D, "d_branch1", after=tok)
    tok = advance(dy_swa)
    dp_b, dk, dv, d_bias, d_sink = _swa_bwd(dy_swa, p["B"], o_swa, bias_t, sinks, after=tok)
    dp_b = _swa_pack(dp_b, dk, dv)
    d_rel = _relbias_grad(d_bias.reshape(SWA_HEADS, QB * KB2), onehot_t).T
    dp = {"A": dp_a, "B": dp_b, "C": dp_c, "D": dp_d}
    tok = emit({"B": d_win("B", dp_b)}, tok)
    dh = None
    for g in GROUPS:
        dh = _mm(dp[g], w_grp[g], "nn", F32, 1024, 1024, 2304 if g == "B" else 2048, f"d_h_{g}", acc=dh,
                 after=tok if g in ("A", "B") else None)
        if g == "A":
            tok = advance(dh)
    grad_x, d_pre = _pre_bwd(dh, x, dy, sp["pre_norm_g"])

    d_small = {
        "pre_norm_g": d_pre, "post_norm_g": d_post, "mem_norm_g": d_memg, "conv_w": d_cw, "conv_b": d_cb,
        "w_rg_a": d_wa, "b_rg_a": d_ba, "w_rg_x": d_wx, "b_rg_x": d_bx, "lru_lambda": d_lam,
        "swa_sinks": d_sink[:, 0].reshape(1, SWA_HEADS), "rel_bias": d_rel,
    }
    return sq, grad_x, d_small


ANY = pl.BlockSpec(memory_space=pl.ANY)
SHARD_ROWS = D // N_CHIPS
GATHERED = {"A": (2048, D), "B": (2304, D), "C": (2048, D), "D": (6144, D), "mk": (D, D),
            "br0": (D_RNN, D), "br1": (D_RNN, D), "br2": (D_RNN, D), "out": (D, D)}
SHARD_SHAPES = {"win": (SHARD, D), "mk": (SHARD_ROWS, D), "br0": (D_RNN, SHARD_ROWS), "br1": (D_RNN, SHARD_ROWS),
                "br2": (D_RNN, SHARD_ROWS), "out": (SHARD_ROWS, D)}
SHARDS = tuple(SHARD_SHAPES)
HALF_AXIS = {"win": 1, "mk": 1, "br0": 0, "br1": 0, "br2": 0, "out": 1,
             "A": 1, "B": 1, "C": 1, "D": 1}


def _halved(shape, axis):
    return (shape[0] // 2, shape[1]) if axis == 0 else (shape[0], shape[1] // 2)


class Piece(NamedTuple):
    src: str
    dst: str
    rows: int
    sr0: int
    sc0: int
    dr0: int
    dc0: int
    ncols: int


def _pieces_of(jj):
    out = [Piece("win", g, n, r, 0, gr, 0, D) for r, n, g, gr in _shard_runs(jj)]
    out.append(Piece("mk", "mk", SHARD_ROWS, 0, 0, SHARD_ROWS * jj, 0, D))
    out += [Piece(f"br{b}", f"br{b}", D_RNN, 0, 0, 0, SHARD_ROWS * jj, SHARD_ROWS) for b in range(3)]
    out.append(Piece("out", "out", SHARD_ROWS, 0, 0, SHARD_ROWS * jj, 0, D))
    return out


def _half_rect(ref, p, side, which):
    r0, c0 = (p.sr0, p.sc0) if side == "src" else (p.dr0, p.dc0)
    if HALF_AXIS[p.src] == 1:
        return _rect(ref, r0, p.rows, c0 + which * (p.ncols // 2), p.ncols // 2)
    return _rect(ref, r0 + which * (p.rows // 2), p.rows // 2, c0, p.ncols)


def _rect_in_half(ref, p, side):
    r0, c0 = (p.sr0, p.sc0) if side == "src" else (p.dr0, p.dc0)
    if HALF_AXIS[p.src] == 1:
        return _rect(ref, r0, p.rows, 0, p.ncols // 2)
    return _rect(ref, 0, p.rows // 2, c0, p.ncols)


MAX_PIECES = max(len(_pieces_of(jj)) for jj in range(N_CHIPS))


def _rect(ref, r0, rows, c0, ncols):
    return ref.at[pl.ds(r0, rows), pl.ds(c0, ncols)]


def _position():
    x, y, c = lax.axis_index("x"), lax.axis_index("y"), lax.axis_index("c")
    return x, y, c, 2 * x + y


HBM = pl.BlockSpec(memory_space=pltpu.HBM)
SEM = pl.BlockSpec(memory_space=pltpu.SEMAPHORE)
EFFECT = pltpu.SideEffectType.DATAFLOW_SIDE_EFFECTING
N_SEM = MAX_PIECES * N_CHIPS
GATHER_STAGES = (("A",), ("B",), ("C",), ("mk",), ("br0", "br1", "br2"), ("D",), ("out",))


def _in_hbm(a):
    return pltpu.with_memory_space_constraint(a, pltpu.HBM)


def _stage_pieces(jj, stage):
    return [(i, p) for i, p in enumerate(_pieces_of(jj)) if p.dst in stage]


def _own_block_table(g):
    import numpy as np
    units = np.full((N_CHIPS, GATHERED[g][0] // HALF_TILE), -1, np.int64)
    for jj in range(N_CHIPS):
        for r, n, grp, gr in _shard_runs(jj):
            if grp == g:
                for k in range(n // HALF_TILE):
                    units[jj, gr // HALF_TILE + k] = r // HALF_TILE + k
    tbl = np.zeros((N_CHIPS, 2, GATHERED[g][0] // LANE), np.int32)
    for jj in range(N_CHIPS):
        for b in range(tbl.shape[2]):
            first, second = units[jj, 2 * b], units[jj, 2 * b + 1]
            if jj % 2 == 0:
                src = first if first >= 0 else second - 1
                if first >= 0 or second >= 0:
                    assert src % 2 == 0
                    tbl[jj, :, b] = src // 2
            else:
                if first >= 0:
                    assert first % 2 == 1
                    tbl[jj, 0, b] = first // 2
                if second >= 0:
                    assert second % 2 == 0
                    tbl[jj, 1, b] = second // 2
    return tbl


def _place_group(w_t, g, tables, odd_arr, after):
    nb = GATHERED[g][0] // LANE

    def body(t_ref, odd_ref, a_ref, b_ref, _, o_ref):
        odd = odd_ref[0] == 1
        o_ref[0:HALF_TILE, :] = jnp.where(odd, a_ref[HALF_TILE:LANE, :], a_ref[0:HALF_TILE, :]).astype(BF16)
        o_ref[HALF_TILE:LANE, :] = jnp.where(odd, b_ref[0:HALF_TILE, :], a_ref[HALF_TILE:LANE, :]).astype(BF16)

    return pl.pallas_call(
        body,
        name=f"place_{g}",
        grid_spec=pltpu.PrefetchScalarGridSpec(
            num_scalar_prefetch=2,
            grid=(nb,),
            in_specs=[pl.BlockSpec((LANE, D), lambda b, t, o: (t[0, b], 0)),
                      pl.BlockSpec((LANE, D), lambda b, t, o: (t[1, b], 0)), ANY],
            out_specs=pl.BlockSpec((LANE, D), lambda b, t, o: (b, 0)),
        ),
        out_shape=jax.ShapeDtypeStruct(GATHERED[g], BF16),
        compiler_params=_params(("parallel",)),
    )(tables, odd_arr, w_t, w_t, after)


def _place_shard(shard, name, after):
    rows, cols = shard.shape
    by_rows = HALF_AXIS[name] == 1

    def body(x_ref, _, o_ref):
        o_ref[...] = x_ref[...].astype(BF16)

    return pl.pallas_call(
        body,
        name=f"place_{name}",
        grid=(N_CHIPS,),
        in_specs=[pl.BlockSpec((rows, cols), lambda b: (0, 0)), ANY],
        out_specs=pl.BlockSpec((rows, cols), (lambda b: (b, 0)) if by_rows else (lambda b: (0, b))),
        out_shape=jax.ShapeDtypeStruct(GATHERED[name], BF16),
        compiler_params=_params(("parallel",)),
    )(shard, after)


def _gather_copy(arr, send_sems, recv_sems, c, jj, i, p, kk):
    rect = _half_rect(arr[p.dst], p, "dst", c)
    return pltpu.make_async_remote_copy(
        src_ref=rect, dst_ref=rect, send_sem=send_sems.at[i * N_CHIPS + kk],
        recv_sem=recv_sems.at[jj * MAX_PIECES + i], device_id=(kk // 2, kk % 2, c), device_id_type=MESH)


def _gather_start(arrays, after):
    stage = tuple(arrays)
    na = len(stage)

    def body(*refs):
        arr = dict(zip(stage, refs[:na]))
        send_sems, recv_sems = refs[na + 1], refs[na + 2]
        token = refs[-1]
        _, _, c, j = _position()
        for jj in range(N_CHIPS):
            @pl.when(j == jj)
            def _():
                for i, p in _stage_pieces(jj, stage):
                    for kk in range(N_CHIPS):
                        if kk != jj:
                            _gather_copy(arr, send_sems, recv_sems, c, jj, i, p, kk).start()
        token[...] = jnp.zeros_like(token)

    outs = pl.pallas_call(
        body,
        name=f"gather_start_{stage[0]}",
        in_specs=[HBM] * na + [ANY],
        out_specs=[SEM, SEM] + [HBM] * na + [pl.BlockSpec(memory_space=pltpu.VMEM)],
        out_shape=[pltpu.SemaphoreType.DMA((N_SEM,)), pltpu.SemaphoreType.DMA((N_SEM,))]
        + [pltpu.HBM(GATHERED[n], BF16) for n in stage] + [jax.ShapeDtypeStruct((8, LANE), F32)],
        input_output_aliases={k: 2 + k for k in range(na)},
        compiler_params=pltpu.CompilerParams(has_side_effects=EFFECT),
    )(*[_in_hbm(arrays[n]) for n in stage], after)
    return outs[0], outs[1], dict(zip(stage, outs[2:2 + na])), outs[-1]


def _gather_wait(send_sems, recv_sems, arrays, after):
    stage = tuple(arrays)
    na = len(stage)

    def body(*refs):
        arr = dict(zip(stage, refs[:na]))
        sems_s, sems_r = refs[na], refs[na + 1]
        _, _, c, j = _position()
        for jj in range(N_CHIPS):
            @pl.when(j != jj)
            def _():
                for i, p in _stage_pieces(jj, stage):
                    _gather_copy(arr, sems_s, sems_r, c, jj, i, p, jj).wait_recv()

            @pl.when(j == jj)
            def _():
                for i, p in _stage_pieces(jj, stage):
                    for kk in range(N_CHIPS):
                        if kk != jj:
                            _gather_copy(arr, sems_s, sems_r, c, jj, i, p, kk).wait_send()

    outs = pl.pallas_call(
        body,
        name=f"gather_wait_{stage[0]}",
        in_specs=[HBM] * na + [SEM, SEM, ANY],
        out_specs=[HBM] * na,
        out_shape=[pltpu.HBM(GATHERED[n], BF16) for n in stage],
        input_output_aliases={k: k for k in range(na)},
        compiler_params=pltpu.CompilerParams(has_side_effects=EFFECT),
    )(*[arrays[n] for n in stage], send_sems, recv_sems, after)
    return dict(zip(stage, outs))


def _gather_swap(arrays):
    stage = tuple(arrays)
    na = len(stage)

    def body(*refs):
        dst = dict(zip(stage, refs[na:2 * na]))
        send_sems, recv_sems = refs[2 * na:]
        x, y, c, j = _position()

        def fwd(jj, i, p, which):
            rect = _half_rect(dst[p.dst], p, "dst", which)
            return pltpu.make_async_remote_copy(
                src_ref=rect, dst_ref=rect, send_sem=send_sems.at[jj * MAX_PIECES + i],
                recv_sem=recv_sems.at[jj * MAX_PIECES + i], device_id=(x, y, 1 - c), device_id_type=MESH)

        for jj in range(N_CHIPS):
            @pl.when(j != jj)
            def _():
                for i, p in _stage_pieces(jj, stage):
                    fwd(jj, i, p, c).start()
        for jj in range(N_CHIPS):
            @pl.when(j != jj)
            def _():
                for i, p in _stage_pieces(jj, stage):
                    fwd(jj, i, p, 1 - c).wait_recv()
        for jj in range(N_CHIPS):
            @pl.when(j != jj)
            def _():
                for i, p in _stage_pieces(jj, stage):
                    fwd(jj, i, p, c).wait_send()

    outs = pl.pallas_call(
        body,
        name=f"gather_swap_{stage[0]}",
        in_specs=[ANY] * na,
        out_specs=[ANY] * na,
        out_shape=[jax.ShapeDtypeStruct(GATHERED[n], BF16) for n in stage],
        input_output_aliases={k: k for k in range(na)},
        scratch_shapes=[pltpu.SemaphoreType.DMA((N_SEM,)), pltpu.SemaphoreType.DMA((N_SEM,))],
        compiler_params=pltpu.CompilerParams(has_side_effects=True),
    )(*[arrays[n] for n in stage])
    return dict(zip(stage, outs))


def _pass_on_copy(arr, send_sems, recv_sems, x, y, c, jj, i, p, which):
    rect = _half_rect(arr[p.dst], p, "dst", which)
    return pltpu.make_async_remote_copy(
        src_ref=rect, dst_ref=rect, send_sem=send_sems.at[jj * MAX_PIECES + i],
        recv_sem=recv_sems.at[jj * MAX_PIECES + i], device_id=(x, y, 1 - c), device_id_type=MESH)


def _gather_pass_start(arrays, after):
    stage = tuple(arrays)
    na = len(stage)

    def body(*refs):
        arr = dict(zip(stage, refs[:na]))
        x, y, c, j = _position()
        for jj in range(N_CHIPS):
            @pl.when(j != jj)
            def _():
                for i, p in _stage_pieces(jj, stage):
                    _pass_on_copy(arr, refs[na + 1], refs[na + 2], x, y, c, jj, i, p, c).start()
        refs[-1][...] = jnp.zeros_like(refs[-1])

    outs = pl.pallas_call(
        body,
        name=f"gather_pass_start_{stage[0]}",
        in_specs=[HBM] * na + [ANY],
        out_specs=[SEM, SEM] + [HBM] * na + [pl.BlockSpec(memory_space=pltpu.VMEM)],
        out_shape=[pltpu.SemaphoreType.DMA((N_SEM,)), pltpu.SemaphoreType.DMA((N_SEM,))]
        + [pltpu.HBM(GATHERED[n], BF16) for n in stage] + [jax.ShapeDtypeStruct((8, LANE), F32)],
        input_output_aliases={k: 2 + k for k in range(na)},
        compiler_params=pltpu.CompilerParams(has_side_effects=EFFECT),
    )(*[arrays[n] for n in stage], after)
    return outs[0], outs[1], dict(zip(stage, outs[2:2 + na])), outs[-1]


def _gather_pass_wait(send_sems, recv_sems, arrays, after):
    stage = tuple(arrays)
    na = len(stage)

    def body(*refs):
        arr = dict(zip(stage, refs[:na]))
        x, y, c, j = _position()
        for jj in range(N_CHIPS):
            @pl.when(j != jj)
            def _():
                for i, p in _stage_pieces(jj, stage):
                    _pass_on_copy(arr, refs[na], refs[na + 1], x, y, c, jj, i, p, 1 - c).wait_recv()
                    _pass_on_copy(arr, refs[na], refs[na + 1], x, y, c, jj, i, p, c).wait_send()

    outs = pl.pallas_call(
        body,
        name=f"gather_pass_wait_{stage[0]}",
        in_specs=[HBM] * na + [SEM, SEM, ANY],
        out_specs=[HBM] * na,
        out_shape=[pltpu.HBM(GATHERED[n], BF16) for n in stage],
        input_output_aliases={k: k for k in range(na)},
        compiler_params=pltpu.CompilerParams(has_side_effects=EFFECT),
    )(*[arrays[n] for n in stage], send_sems, recv_sems, after)
    return dict(zip(stage, outs))


def _own_half(ref, shape, axis, which):
    if axis == 1:
        return ref.at[:, pl.ds(which * (shape[1] // 2), shape[1] // 2)]
    return ref.at[pl.ds(which * (shape[0] // 2), shape[0] // 2), :]


def _swap_copies(names, src, dst, send_sems, recv_sems):
    x, y, c, _ = _position()
    return [pltpu.make_async_remote_copy(
        src_ref=_own_half(src[n], GATHERED[n], HALF_AXIS[n], 1 - c), dst_ref=dst[n],
        send_sem=send_sems.at[k], recv_sem=recv_sems.at[k],
        device_id=(x, y, 1 - c), device_id_type=MESH) for k, n in enumerate(names)]


def _swap_start(grads, after):
    names = tuple(grads)
    n = len(names)

    def body(*refs):
        src, dst = dict(zip(names, refs[:n])), dict(zip(names, refs[n:2 * n]))
        for cp in _swap_copies(names, src, dst, refs[2 * n + 1], refs[2 * n + 2]):
            cp.start()
        refs[-1][...] = jnp.zeros_like(refs[-1])

    half_shape = lambda nm: _halved(GATHERED[nm], HALF_AXIS[nm])
    args = [_in_hbm(grads[nm]) for nm in names] + [_in_hbm(lax.empty(half_shape(nm), BF16)) for nm in names]
    if after is None:
        after = jnp.zeros((8, LANE), F32)
    outs = pl.pallas_call(
        body,
        name=f"swap_start_{names[0]}",
        in_specs=[HBM] * (2 * n) + [ANY],
        out_specs=[SEM, SEM] + [HBM] * (2 * n) + [pl.BlockSpec(memory_space=pltpu.VMEM)],
        out_shape=[pltpu.SemaphoreType.DMA((n,)), pltpu.SemaphoreType.DMA((n,))]
        + [pltpu.HBM(GATHERED[nm], BF16) for nm in names] + [pltpu.HBM(half_shape(nm), BF16) for nm in names]
        + [jax.ShapeDtypeStruct((8, LANE), F32)],
        input_output_aliases={k: 2 + k for k in range(2 * n)},
        compiler_params=pltpu.CompilerParams(has_side_effects=EFFECT),
    )(*args, after)
    return outs[0], outs[1], dict(zip(names, outs[2:2 + n])), dict(zip(names, outs[2 + n:2 + 2 * n])), outs[-1]


def _swap_wait(send_sems, recv_sems, grads, landing, after):
    names = tuple(grads)
    n = len(names)

    def body(*refs):
        src, dst = dict(zip(names, refs[:n])), dict(zip(names, refs[n:2 * n]))
        copies = _swap_copies(names, src, dst, refs[2 * n], refs[2 * n + 1])
        for cp in copies:
            cp.wait_recv()
        for cp in copies:
            cp.wait_send()

    half_shape = lambda nm: _halved(GATHERED[nm], HALF_AXIS[nm])
    outs = pl.pallas_call(
        body,
        name=f"swap_wait_{names[0]}",
        in_specs=[HBM] * (2 * n) + [SEM, SEM, ANY],
        out_specs=[HBM] * (2 * n),
        out_shape=[pltpu.HBM(GATHERED[nm], BF16) for nm in names] + [pltpu.HBM(half_shape(nm), BF16) for nm in names],
        input_output_aliases={k: k for k in range(2 * n)},
        compiler_params=pltpu.CompilerParams(has_side_effects=EFFECT),
    )(*[grads[nm] for nm in names], *[landing[nm] for nm in names], send_sems, recv_sems, after)
    return dict(zip(names, outs[:n])), dict(zip(names, outs[n:]))


ADD_ROWS = {"A": 1024, "B": 768, "C": 1024, "D": 1536, "mk": 1024, "br0": 512, "br1": 512, "br2": 512, "out": 1024}


def _add_half(full, recv, c_arr, name):
    rows, cols = recv.shape
    tr = ADD_ROWS[name]
    if HALF_AXIS[name] == 1:
        index = lambda i, c_ref: (i, c_ref[0])
    else:
        nb = rows // tr
        index = lambda i, c_ref: (nb * c_ref[0] + i, 0)

    def body(c_ref, a_ref, b_ref, o_ref):
        o_ref[...] = (a_ref[...].astype(F32) + b_ref[...].astype(F32)).astype(BF16)

    return pl.pallas_call(
        body,
        name=f"add_half_{name}",
        grid_spec=pltpu.PrefetchScalarGridSpec(
            num_scalar_prefetch=1,
            grid=(rows // tr,),
            in_specs=[pl.BlockSpec((tr, cols), index), pl.BlockSpec((tr, cols), lambda i, c_ref: (i, 0))],
            out_specs=pl.BlockSpec((tr, cols), lambda i, c_ref: (i, 0)),
        ),
        out_shape=jax.ShapeDtypeStruct((rows, cols), BF16),
        compiler_params=_params(("parallel",)),
    )(c_arr, full, recv)


SLOT_SHAPES = {n: _halved(SHARD_SHAPES[n], HALF_AXIS[n]) for n in SHARDS}


def _slot_shape(n):
    return (N_CHIPS,) + SLOT_SHAPES[n]


def _stage_shards(stage):
    pieces = [p for jj in range(N_CHIPS) for p in _pieces_of(jj)]
    return tuple(s for s in SHARDS if any(p.src == s and p.dst in stage for p in pieces))


def _scatter_copy(src, dst, send_sems, recv_sems, c, jj, kk, i, p):
    return pltpu.make_async_remote_copy(
        src_ref=_rect_in_half(src[p.dst], p, "dst"), dst_ref=_rect_in_half(dst[p.src].at[jj], p, "src"),
        send_sem=send_sems.at[kk * MAX_PIECES + i], recv_sem=recv_sems.at[jj * MAX_PIECES + i],
        device_id=(kk // 2, kk % 2, c), device_id_type=MESH)


def _scatter_start(halves, slots):
    stage, touched = tuple(halves), tuple(slots)
    nh, nt = len(stage), len(touched)

    def body(*refs):
        src = dict(zip(stage, refs[:nh]))
        dst = dict(zip(touched, refs[nh:nh + nt]))
        send_sems, recv_sems = refs[nh + nt], refs[nh + nt + 1]
        token = refs[-1]
        _, _, c, j = _position()
        for jj in range(N_CHIPS):
            @pl.when(j == jj)
            def _():
                for kk in range(N_CHIPS):
                    if kk != jj:
                        for i, p in _stage_pieces(kk, stage):
                            _scatter_copy(src, dst, send_sems, recv_sems, c, jj, kk, i, p).start()
        token[...] = jnp.zeros_like(token)

    outs = pl.pallas_call(
        body,
        name=f"scatter_start_{stage[0]}",
        in_specs=[HBM] * (nh + nt),
        out_specs=[SEM, SEM] + [HBM] * (nh + nt) + [pl.BlockSpec(memory_space=pltpu.VMEM)],
        out_shape=[pltpu.SemaphoreType.DMA((N_SEM,)), pltpu.SemaphoreType.DMA((N_SEM,))]
        + [pltpu.HBM(halves[n].shape, BF16) for n in stage] + [pltpu.HBM(_slot_shape(s), BF16) for s in touched]
        + [jax.ShapeDtypeStruct((8, LANE), F32)],
        input_output_aliases={k: 2 + k for k in range(nh + nt)},
        compiler_params=pltpu.CompilerParams(has_side_effects=EFFECT),
    )(*[_in_hbm(halves[n]) for n in stage], *[_in_hbm(slots[s]) for s in touched])
    return outs[0], outs[1], dict(zip(stage, outs[2:2 + nh])), dict(zip(touched, outs[2 + nh:2 + nh + nt])), outs[-1]


def _scatter_wait(send_sems, recv_sems, halves, slots, after):
    stage, touched = tuple(halves), tuple(slots)
    nh, nt = len(stage), len(touched)

    def body(*refs):
        src = dict(zip(stage, refs[:nh]))
        dst = dict(zip(touched, refs[nh:nh + nt]))
        sems_s, sems_r = refs[nh + nt], refs[nh + nt + 1]
        _, _, c, j = _position()
        for jj in range(N_CHIPS):
            @pl.when(j == jj)
            def _():
                for ss in range(N_CHIPS):
                    if ss != jj:
                        for i, p in _stage_pieces(jj, stage):
                            _scatter_copy(src, dst, sems_s, sems_r, c, ss, jj, i, p).wait_recv()
                for kk in range(N_CHIPS):
                    if kk != jj:
                        for i, p in _stage_pieces(kk, stage):
                            _scatter_copy(src, dst, sems_s, sems_r, c, jj, kk, i, p).wait_send()

    outs = pl.pallas_call(
        body,
        name=f"scatter_wait_{stage[0]}",
        in_specs=[HBM] * (nh + nt) + [SEM, SEM, ANY],
        out_specs=[HBM] * (nh + nt),
        out_shape=[pltpu.HBM(halves[n].shape, BF16) for n in stage] + [pltpu.HBM(_slot_shape(s), BF16) for s in touched],
        input_output_aliases={k: k for k in range(nh + nt)},
        compiler_params=pltpu.CompilerParams(has_side_effects=EFFECT),
    )(*[halves[n] for n in stage], *[slots[s] for s in touched], send_sems, recv_sems, after)
    return dict(zip(stage, outs[:nh])), dict(zip(touched, outs[nh:]))


SUM_ROWS = {"mk": 512, "br0": 512, "br1": 512, "br2": 512, "out": 512}


def _sum_in_chip_order(chip, own, s_ref):
    acc = None
    for k in range(N_CHIPS):
        term = jnp.where(chip == k, own, s_ref[k].astype(F32))
        acc = term if acc is None else acc + term
    return acc


def _sum_slots(slots, own_half, pos_arr, name):
    _, rows, cols = slots.shape
    tr = SUM_ROWS[name]
    nb = rows // tr
    if HALF_AXIS[name] == 1:
        own_index = lambda i, pos: (nb * pos[1] + i, 0)
        out_index = lambda i, pos: (i, pos[0])
    else:
        own_index = lambda i, pos: (i, pos[1])
        out_index = lambda i, pos: (nb * pos[0] + i, 0)

    def body(pos, s_ref, own_ref, o_ref):
        o_ref[...] = _sum_in_chip_order(pos[1], own_ref[...].astype(F32), s_ref)

    return pl.pallas_call(
        body,
        name=f"sum_slots_{name}",
        grid_spec=pltpu.PrefetchScalarGridSpec(
            num_scalar_prefetch=1,
            grid=(nb,),
            in_specs=[pl.BlockSpec((N_CHIPS, tr, cols), lambda i, pos: (0, i, 0)),
                      pl.BlockSpec((tr, cols), own_index)],
            out_specs=pl.BlockSpec((tr, cols), out_index),
        ),
        out_shape=jax.ShapeDtypeStruct(SHARD_SHAPES[name], F32),
        compiler_params=_params(("parallel",)),
    )(pos_arr, slots, own_half)


def _own_partial_tables():
    import numpy as np
    nb = SHARD // HALF_TILE
    grp, blk = np.zeros((N_CHIPS, nb), np.int32), np.zeros((N_CHIPS, nb), np.int32)
    for jj in range(N_CHIPS):
        for r, n, g, gr in _shard_runs(jj):
            for k in range(n // HALF_TILE):
                grp[jj, r // HALF_TILE + k] = GROUPS.index(g)
                blk[jj, r // HALF_TILE + k] = gr // HALF_TILE + k
    return grp, blk


def _sum_slots_win(slots, own_halves, pos_arr, grp_tbl, blk_tbl):
    nb = SHARD // HALF_TILE
    cols = D // 2

    def own_spec(gi):
        return pl.BlockSpec((HALF_TILE, cols), lambda b, pos, grp, blk: (jnp.where(grp[b] == gi, blk[b], 0), 0))

    def body(pos, grp, blk, s_ref, a_ref, b_ref, c_ref, d_ref, o_ref):
        g = grp[pl.program_id(0)]
        own = a_ref[...]
        for gi, ref in ((1, b_ref), (2, c_ref), (3, d_ref)):
            own = jnp.where(g == gi, ref[...], own)
        o_ref[...] = _sum_in_chip_order(pos[1], own.astype(F32), s_ref)

    return pl.pallas_call(
        body,
        name="sum_slots_win",
        grid_spec=pltpu.PrefetchScalarGridSpec(
            num_scalar_prefetch=3,
            grid=(nb,),
            in_specs=[pl.BlockSpec((N_CHIPS, HALF_TILE, cols), lambda b, pos, grp, blk: (0, b, 0))]
            + [own_spec(gi) for gi in range(len(GROUPS))],
            out_specs=pl.BlockSpec((HALF_TILE, cols), lambda b, pos, grp, blk: (b, pos[0])),
        ),
        out_shape=jax.ShapeDtypeStruct(SHARD_SHAPES["win"], F32),
        compiler_params=_params(("parallel",)),
    )(pos_arr, grp_tbl, blk_tbl, slots, *[own_halves[g] for g in GROUPS])


def _share_copy(buf, name, send_sems, recv_sems, k, which):
    x, y, c, _ = _position()
    half = _own_half(buf, SHARD_SHAPES[name], HALF_AXIS[name], which)
    return pltpu.make_async_remote_copy(src_ref=half, dst_ref=half, send_sem=send_sems.at[k], recv_sem=recv_sems.at[k],
                                        device_id=(x, y, 1 - c), device_id_type=MESH)


def _share_start(sums, after):
    names = tuple(sums)
    n = len(names)

    def body(*refs):
        _, _, c, _ = _position()
        for k, nm in enumerate(names):
            _share_copy(refs[k], nm, refs[n + 1], refs[n + 2], k, c).start()
        refs[-1][...] = jnp.zeros_like(refs[-1])

    outs = pl.pallas_call(
        body,
        name=f"share_start_{names[0]}",
        in_specs=[HBM] * n + [ANY],
        out_specs=[SEM, SEM] + [HBM] * n + [pl.BlockSpec(memory_space=pltpu.VMEM)],
        out_shape=[pltpu.SemaphoreType.DMA((n,)), pltpu.SemaphoreType.DMA((n,))]
        + [pltpu.HBM(SHARD_SHAPES[nm], F32) for nm in names] + [jax.ShapeDtypeStruct((8, LANE), F32)],
        input_output_aliases={k: 2 + k for k in range(n)},
        compiler_params=pltpu.CompilerParams(has_side_effects=EFFECT),
    )(*[_in_hbm(sums[nm]) for nm in names], after)
    return outs[0], outs[1], dict(zip(names, outs[2:2 + n])), outs[-1]


def _share_wait(send_sems, recv_sems, sums, after):
    names = tuple(sums)
    n = len(names)

    def body(*refs):
        _, _, c, _ = _position()
        for k, nm in enumerate(names):
            _share_copy(refs[k], nm, refs[n], refs[n + 1], k, 1 - c).wait_recv()
            _share_copy(refs[k], nm, refs[n], refs[n + 1], k, c).wait_send()

    outs = pl.pallas_call(
        body,
        name=f"share_wait_{names[0]}",
        in_specs=[HBM] * n + [SEM, SEM, ANY],
        out_specs=[HBM] * n,
        out_shape=[pltpu.HBM(SHARD_SHAPES[nm], F32) for nm in names],
        input_output_aliases={k: k for k in range(n)},
        compiler_params=pltpu.CompilerParams(has_side_effects=EFFECT),
    )(*[sums[nm] for nm in names], send_sems, recv_sems, after)
    return dict(zip(names, outs))


def _all_reduce_small(pack, name):
    rows = pack.shape[0]
    half = rows // 2

    def body(p_ref, o_ref, sib, land, sems):
        x, y, c, j = _position()
        sibling = (x, y, 1 - c)
        swap = pltpu.make_async_remote_copy(src_ref=p_ref, dst_ref=sib, send_sem=sems.at[0], recv_sem=sems.at[1],
                                            device_id=sibling, device_id_type=MESH)
        swap.start()
        swap.wait_recv()
        land[j] = p_ref[...] + sib[...]

        def mine(k, which):
            return land.at[k, pl.ds(which * half, half)]

        def ici(kk):
            return pltpu.make_async_remote_copy(
                src_ref=mine(j, c), dst_ref=mine(j, c), send_sem=sems.at[2 + kk], recv_sem=sems.at[6 + j],
                device_id=(kk // 2, kk % 2, c), device_id_type=MESH)

        def arrival(kk):
            return pltpu.make_async_remote_copy(
                src_ref=mine(kk, c), dst_ref=mine(kk, c), send_sem=sems.at[2 + kk], recv_sem=sems.at[6 + kk],
                device_id=(kk // 2, kk % 2, c), device_id_type=MESH)

        def passed_on(kk, which):
            return pltpu.make_async_remote_copy(
                src_ref=mine(kk, which), dst_ref=mine(kk, which), send_sem=sems.at[10 + kk],
                recv_sem=sems.at[14 + kk], device_id=sibling, device_id_type=MESH)

        for kk in range(N_CHIPS):
            @pl.when(j != kk)
            def _():
                ici(kk).start()
        for kk in range(N_CHIPS):
            @pl.when(j != kk)
            def _():
                arrival(kk).wait_recv()
                passed_on(kk, c).start()
        for kk in range(N_CHIPS):
            @pl.when(j != kk)
            def _():
                passed_on(kk, 1 - c).wait_recv()
        acc = land[0]
        for kk in range(1, N_CHIPS):
            acc = acc + land[kk]
        o_ref[...] = acc
        swap.wait_send()
        for kk in range(N_CHIPS):
            @pl.when(j != kk)
            def _():
                ici(kk).wait_send()
                passed_on(kk, c).wait_send()

    vmem = pl.BlockSpec(memory_space=pltpu.VMEM)
    return pl.pallas_call(
        body,
        name=name,
        in_specs=[vmem],
        out_specs=vmem,
        out_shape=jax.ShapeDtypeStruct((rows, LANE), F32),
        scratch_shapes=[pltpu.VMEM((rows, LANE), F32), pltpu.VMEM((N_CHIPS, rows, LANE), F32),
                        pltpu.SemaphoreType.DMA((18,))],
        compiler_params=pltpu.CompilerParams(has_side_effects=True, vmem_limit_bytes=VMEM_LIMIT),
    )(pack)


ADAM_ROWS = {"win": 224, "mk": 256, "br0": 512, "br1": 512, "br2": 512, "out": 256}


def _adamw(w, g, m, v, name, tr):
    rows, cols = w.shape
    tr = min(tr, rows)

    def body(w_ref, g_ref, m_ref, v_ref, go_ref, d_ref, nm_ref, nv_ref):
        gv = g_ref[...]
        go_ref[...] = gv
        nm = ADAM_B1 * m_ref[...] + (1.0 - ADAM_B1) * gv
        nv = ADAM_B2 * v_ref[...] + (1.0 - ADAM_B2) * (gv * gv)
        nm_ref[...] = nm
        nv_ref[...] = nv
        m_hat = nm / (1.0 - ADAM_B1 ** ADAM_STEP)
        v_hat = nv / (1.0 - ADAM_B2 ** ADAM_STEP)
        d_ref[...] = -ADAM_LR * (m_hat / (jnp.sqrt(v_hat) + ADAM_EPS) + ADAM_WD * w_ref[...])

    blk = pl.BlockSpec((tr, cols), lambda i: (i, 0))
    shape = jax.ShapeDtypeStruct((rows, cols), F32)
    return pl.pallas_call(
        body,
        name=f"adamw_{name}",
        grid=(rows // tr,),
        in_specs=[blk] * 4,
        out_specs=[blk] * 4,
        out_shape=[shape] * 4,
        compiler_params=_params(("parallel",)),
    )(w, g, m, v)


SMALL = (("pre_norm_g", (1, D)), ("post_norm_g", (1, D)), ("mem_norm_g", (1, D)), ("conv_w", (CONV_W, D_RNN)),
         ("conv_b", (1, D_RNN)), ("w_rg_a", (RNN_BLOCKS, LANE, LANE)), ("b_rg_a", (1, D_RNN)),
         ("w_rg_x", (RNN_BLOCKS, LANE, LANE)), ("b_rg_x", (1, D_RNN)), ("lru_lambda", (1, D_RNN)),
         ("swa_sinks", (1, SWA_HEADS)), ("rel_bias", (REL_BUCKETS, SWA_HEADS)))
PACK_ROWS = 2176


def _slot_len(shape):
    return -(-math.prod(shape) // LANE) * LANE


def _pack(values, last_row=None):
    parts = []
    for name, shape in SMALL:
        flat = values[name].reshape(-1).astype(F32)
        parts.append(jnp.pad(flat, (0, _slot_len(shape) - flat.shape[0])))
    flat = jnp.concatenate(parts)
    tail = jnp.zeros((LANE,), F32) if last_row is None else last_row
    return jnp.concatenate([jnp.pad(flat, (0, (PACK_ROWS - 1) * LANE - flat.shape[0])), tail]).reshape(PACK_ROWS, LANE)


def _unpack(pack):
    flat = pack.reshape(-1)
    out, off = {}, 0
    for name, shape in SMALL:
        out[name] = flat[off:off + math.prod(shape)].reshape(shape)
        off += _slot_len(shape)
    return out


TWIN_WEIGHTS = ("pre_norm_g", "post_norm_g", "mem_norm_g", "w_in", "conv_w", "conv_b", "w_rg_a", "b_rg_a", "w_rg_x",
                "b_rg_x", "lru_lambda", "swa_sinks", "rel_bias", "w_mem_kv", "w_br_rg", "w_br_swa", "w_br_mem", "w_out")
BIG = {"w_in": "win", "w_mem_kv": "mk", "w_br_rg": "br0", "w_br_swa": "br1", "w_br_mem": "br2", "w_out": "out"}


def kernel(x, mem, pre_norm_g, post_norm_g, mem_norm_g, w_in, conv_w, conv_b, w_rg_a, b_rg_a, w_rg_x, b_rg_x, lru_lambda, swa_sinks, rel_bias, w_mem_kv, w_br_rg, w_br_swa, w_br_mem, w_out, loss_target, m_pre_norm_g, m_post_norm_g, m_mem_norm_g, m_w_in, m_conv_w, m_conv_b, m_w_rg_a, m_b_rg_a, m_w_rg_x, m_b_rg_x, m_lru_lambda, m_swa_sinks, m_rel_bias, m_w_mem_kv, m_w_br_rg, m_w_br_swa, m_w_br_mem, m_w_out, v_pre_norm_g, v_post_norm_g, v_mem_norm_g, v_w_in, v_conv_w, v_conv_b, v_w_rg_a, v_b_rg_a, v_w_rg_x, v_b_rg_x, v_lru_lambda, v_swa_sinks, v_rel_bias, v_w_mem_kv, v_w_br_rg, v_w_br_swa, v_w_br_mem, v_w_out):
    args = dict(locals())
    out_shapes = {n: args[n].shape for n in TWIN_WEIGHTS}
    w = {n: (args[n] if n == "rel_bias" else args[n][0]) for n in TWIN_WEIGHTS}
    m = {n: (args["m_" + n] if n == "rel_bias" else args["m_" + n][0]) for n in TWIN_WEIGHTS}
    v = {n: (args["v_" + n] if n == "rel_bias" else args["v_" + n][0]) for n in TWIN_WEIGHTS}
    for d in (w, m, v):
        for n, shape in SMALL:
            if n != "conv_w":
                d[n] = d[n].reshape(shape)

    xi, yi, ci = lax.axis_index("x"), lax.axis_index("y"), lax.axis_index("c")
    chip = 2 * xi + yi
    c_arr = ci.astype(jnp.int32).reshape(1)
    zero = jnp.zeros((), jnp.int32)
    cw0 = (chip * (D_RNN // N_CHIPS)).astype(jnp.int32)

    placed = lax.dynamic_update_slice(jnp.zeros((CONV_W, D_RNN), F32), w["conv_w"], (zero, cw0))
    placed = jnp.where(ci == 0, placed, 0.0).reshape(CONV_W * D_RNN // LANE, LANE)
    conv_w_full = _all_reduce_small(placed, "gather_conv_w").reshape(CONV_W, D_RNN)

    for d in (w, m, v):
        d["w_in"] = d["w_in"].T
    chip_row = lambda tbl: lax.dynamic_slice(jnp.asarray(tbl), (chip.astype(jnp.int32), zero), (1, tbl.shape[1]))[0]
    chip_tables = lambda tbl: lax.dynamic_slice(jnp.asarray(tbl), (chip.astype(jnp.int32), zero, zero),
                                                (1,) + tbl.shape[1:])[0]
    odd_arr = yi.astype(jnp.int32).reshape(1)
    big_of = {s: n for n, s in BIG.items()}
    ag, token = {}, conv_w_full
    for stage in GATHER_STAGES:
        behind = c_arr if stage == GATHER_STAGES[0] else token
        placed = {n: (_place_group(w["w_in"], n, chip_tables(_own_block_table(n)), odd_arr, behind) if n in GROUPS
                      else _place_shard(w[big_of[n]], n, behind)) for n in stage}
        send, recv, in_flight, token = _gather_start(placed, token)
        ag[stage] = (send, recv, in_flight)
    h = _rms_fwd(x[0], w["pre_norm_g"], "rms_pre", token)

    def conv_w_in_place(d):
        return dict(d, conv_w=lax.dynamic_update_slice(jnp.zeros((CONV_W, D_RNN), F32), d["conv_w"], (zero, cw0)))

    small_packs = [_pack(conv_w_in_place(d)) for d in (w, m, v)]

    passing = {}

    def prefetch(names, after):
        send, recv, in_flight = ag[names]
        *passing[names], token = _gather_pass_start(_gather_wait(send, recv, in_flight, after), after)
        return token

    def fetch(names, after):
        if names in passing:
            ready = _gather_pass_wait(*passing.pop(names), after)
        else:
            send, recv, in_flight = ag[names]
            ready = _gather_swap(_gather_wait(send, recv, in_flight, after))
        return tuple(ready[n] for n in names)

    rs = {"slots": {}, "halves": {}, "pending": [], "swap": None}

    def emit(grads, after=None):
        assert rs["swap"] is None
        *rs["swap"], token = _swap_start(grads, after)
        return token

    def advance(after):
        grads, received = _swap_wait(*rs["swap"], after)
        rs["swap"] = None
        halves = {n: _add_half(grads[n], received[n], c_arr, n) for n in grads}
        landing = {s: rs["slots"][s] if s in rs["slots"] else lax.empty(_slot_shape(s), BF16)
                   for s in _stage_shards(tuple(grads))}
        send, recv, halves, landing, token = _scatter_start(halves, landing)
        rs["slots"].update(landing)
        rs["pending"].append((send, recv, halves, tuple(landing)))
        return token

    sp = {n: w[n] for n, _ in SMALL}
    sp["conv_w"] = conv_w_full
    sq, grad_x, d_small = _local_step(x[0], h, mem[0], loss_target[0], sp, small_packs, fetch, prefetch, emit, advance)
    small_total = _all_reduce_small(_pack(d_small, sq[0]), "all_reduce_small")
    loss = small_total[PACK_ROWS - 1, 0] * (0.5 / D)

    for send, recv, halves, touched in rs["pending"]:
        halves, landed = _scatter_wait(send, recv, halves, {s: rs["slots"][s] for s in touched}, small_total)
        rs["slots"].update(landed)
        rs["halves"].update(halves)
    pos_arr = jnp.stack([ci, chip]).astype(jnp.int32)
    grp_tbl, blk_tbl = (chip_row(t) for t in _own_partial_tables())
    rest = {s: _sum_slots(rs["slots"][s], rs["halves"][s], pos_arr, s) for s in SHARDS if s != "win"}
    *rest_share, tok = _share_start(rest, small_total)
    win_sum = _sum_slots_win(rs["slots"]["win"], rs["halves"], pos_arr, grp_tbl, blk_tbl)
    *win_share, tok = _share_start({"win": win_sum}, tok)
    sums = _share_wait(*rest_share, tok)

    grad, delta, new_m, new_v = {}, {}, {}, {}
    for n, s in BIG.items():
        if n == "w_in":
            continue
        grad[n], delta[n], new_m[n], new_v[n] = _adamw(w[n], sums[s], m[n], v[n], s, ADAM_ROWS[s])
    g_win = _share_wait(*win_share, delta["w_out"])["win"]
    n = "w_in"
    grad[n], delta[n], new_m[n], new_v[n] = _adamw(w[n], g_win, m[n], v[n], "win", ADAM_ROWS["win"])
    for group in (grad, delta, new_m, new_v):
        group["w_in"] = group["w_in"].T
    _, d_, m_, v_ = _adamw(small_packs[0], small_total, small_packs[1], small_packs[2], "small", PACK_ROWS)
    for group, pack in ((grad, small_total), (delta, d_), (new_m, m_), (new_v, v_)):
        group.update(_unpack(pack))
    for group in (grad, delta, new_m, new_v):
        group["conv_w"] = lax.dynamic_slice(group["conv_w"], (zero, cw0), (CONV_W, D_RNN // N_CHIPS))

    outs = [loss, grad_x.reshape(1, S, D)]
    for group in (grad, delta, new_m, new_v):
        outs += [group[n].reshape(out_shapes[n]) for n in TWIN_WEIGHTS]
    return tuple(outs)
```

```python
import math
from typing import NamedTuple

import jax
import jax.numpy as jnp
from jax import lax
from jax.experimental import pallas as pl
from jax.experimental.pallas import tpu as pltpu

F32 = jnp.float32
BF16 = jnp.bfloat16
MESH = pl.DeviceIdType.MESH

S = 2048
D = 2048
MEM = 256
D_RNN = 1024
RNN_BLOCKS = 8
CONV_W = 4
LRU_C = 8.0
SWA_HEADS = 16
SWA_HD = 64
WINDOW = 128
MEM_HEADS = 4
MEM_HD = 256
REL_BUCKETS = 32
REL_MAX_DIST = 128
EPS = 1e-6
NEG_INF = -1e30
LANE = 128
SHARD = 3136
HALF_TILE = 64
N_CHIPS = 4
VMEM_LIMIT = 56 * 1024 * 1024

ADAM_LR = 0.001
ADAM_B1 = 0.9
ADAM_B2 = 0.999
ADAM_EPS = 1e-08
ADAM_WD = 0.01
ADAM_STEP = 10

GROUP_TILES = {"A": 16, "B": 18, "C": 16, "D": 48}
GROUPS = ("A", "B", "C", "D")


def _params(sem=None):
    return pltpu.CompilerParams(dimension_semantics=sem, vmem_limit_bytes=VMEM_LIMIT)


def _sigmoid(v):
    return jax.nn.sigmoid(v)


def _tile_home(t):
    if t < 16:
        return "A", t
    if t < 24:
        return "B", t - 16
    if t < 26:
        return "B", t - 24 + 16
    if t < 34:
        return "B", t - 26 + 8
    if t < 50:
        return "C", t - 34
    return "D", t - 50


def _shard_runs(j):
    runs = []
    per_shard = SHARD // HALF_TILE
    for q in range(per_shard * j, per_shard * (j + 1)):
        g, gt = _tile_home(q // 2)
        row = gt * LANE + (q % 2) * HALF_TILE
        if runs and runs[-1][2] == g and runs[-1][3] + runs[-1][1] == row:
            runs[-1][1] += HALF_TILE
        else:
            runs.append([(q - per_shard * j) * HALF_TILE, HALF_TILE, g, row])
    return [tuple(r) for r in runs]


_DIMS = {
    "nn": (((1,), (0,)), ((), ())),
    "nt": (((1,), (1,)), ((), ())),
    "tn": (((0,), (0,)), ((), ())),
}


def _mm(a, b, mode, out_dtype, tm, tn, tk, name, acc=None, after=None):
    if mode == "nn":
        (m, k), n = a.shape, b.shape[1]
    elif mode == "nt":
        (m, k), n = a.shape, b.shape[0]
    else:
        (k, m), n = a.shape, b.shape[1]
    tm, tn, tk = min(tm, m), min(tn, n), min(tk, k)
    assert m % tm == 0 and n % tn == 0 and k % tk == 0, (name, m, n, k)
    nk = k // tk
    has_acc = acc is not None

    def body(*refs):
        a_ref, b_ref = refs[0], refs[1]
        o_ref = refs[3] if has_acc else refs[2]
        p = lax.dot_general(a_ref[...], b_ref[...], _DIMS[mode], preferred_element_type=F32)

        def finish(v):
            if has_acc:
                v = v + refs[2][...]
            o_ref[...] = v.astype(out_dtype)

        if nk == 1:
            finish(p)
        else:
            s_ref = refs[-1]
            kk = pl.program_id(2)

            @pl.when(kk == 0)
            def _():
                s_ref[...] = p

            @pl.when(kk > 0)
            def _():
                s_ref[...] += p

            @pl.when(kk == nk - 1)
            def _():
                finish(s_ref[...])

    if mode == "nn":
        a_spec = pl.BlockSpec((tm, tk), lambda i, j, kk: (i, kk))
        b_spec = pl.BlockSpec((tk, tn), lambda i, j, kk: (kk, j))
    elif mode == "nt":
        a_spec = pl.BlockSpec((tm, tk), lambda i, j, kk: (i, kk))
        b_spec = pl.BlockSpec((tn, tk), lambda i, j, kk: (j, kk))
    else:
        a_spec = pl.BlockSpec((tk, tm), lambda i, j, kk: (kk, i))
        b_spec = pl.BlockSpec((tk, tn), lambda i, j, kk: (kk, j))
    o_spec = pl.BlockSpec((tm, tn), lambda i, j, kk: (i, j))
    in_specs = [a_spec, b_spec] + ([o_spec] if has_acc else [])
    args = (a, b) + ((acc,) if has_acc else ())
    if after is not None:
        in_specs.append(pl.BlockSpec(memory_space=pl.ANY))
        args += (after,)
    n_in = len(args)
    kernel_body = body

    def body(*refs):
        kernel_body(*(refs[:n_in - (after is not None)] + refs[n_in:]))

    return pl.pallas_call(
        body,
        name=name,
        grid=(m // tm, n // tn, nk),
        in_specs=in_specs,
        out_specs=o_spec,
        out_shape=jax.ShapeDtypeStruct((m, n), out_dtype),
        scratch_shapes=[pltpu.VMEM((tm, tn), F32)] if nk > 1 else [],
        compiler_params=_params(("parallel", "parallel", "arbitrary")),
    )(*args)


def _rms_fwd(x, g, name, after, ts=256):
    r, d = x.shape

    def body(x_ref, g_ref, _, o_ref):
        xv = x_ref[...]
        inv = lax.rsqrt(jnp.mean(xv * xv, axis=-1, keepdims=True) + EPS)
        o_ref[...] = (xv * inv * g_ref[...]).astype(BF16)

    return pl.pallas_call(
        body,
        name=name,
        grid=(r // ts,),
        in_specs=[pl.BlockSpec((ts, d), lambda i: (i, 0)), pl.BlockSpec((1, d), lambda i: (0, 0)),
                  pl.BlockSpec(memory_space=pl.ANY)],
        out_specs=pl.BlockSpec((ts, d), lambda i: (i, 0)),
        out_shape=jax.ShapeDtypeStruct((r, d), BF16),
        compiler_params=_params(("parallel",)),
    )(x, g, after)


def _post_loss(out, x, tgt, g_post, ts=256):
    n = S // ts

    def body(o_ref, x_ref, t_ref, g_ref, sq_ref, dy_ref, do_ref, dg_ref):
        i = pl.program_id(0)

        @pl.when(i == 0)
        def _():
            sq_ref[...] = jnp.zeros_like(sq_ref)
            dg_ref[...] = jnp.zeros_like(dg_ref)

        ov = o_ref[...]
        g = g_ref[...]
        inv = lax.rsqrt(jnp.mean(ov * ov, axis=-1, keepdims=True) + EPS)
        on = ov * inv
        err = x_ref[...] + on * g - t_ref[...]
        sq_ref[...] += jnp.sum(err * err)
        dy = err * (1.0 / D)
        dy_ref[...] = dy.astype(BF16)
        dg_ref[...] += jnp.sum(dy * on, axis=0, keepdims=True)
        don = dy * g
        do_ref[...] = (inv * (don - on * jnp.mean(don * on, axis=-1, keepdims=True))).astype(BF16)

    row = pl.BlockSpec((ts, D), lambda i: (i, 0))
    vec = pl.BlockSpec((1, D), lambda i: (0, 0))
    return pl.pallas_call(
        body,
        name="post_loss",
        grid=(n,),
        in_specs=[row, row, row, vec],
        out_specs=[pl.BlockSpec((8, LANE), lambda i: (0, 0)), row, row, vec],
        out_shape=[
            jax.ShapeDtypeStruct((8, LANE), F32),
            jax.ShapeDtypeStruct((S, D), BF16),
            jax.ShapeDtypeStruct((S, D), BF16),
            jax.ShapeDtypeStruct((1, D), F32),
        ],
        compiler_params=_params(("arbitrary",)),
    )(out, x, tgt, g_post)


def _pre_bwd(dh, x, dy, g_pre, ts=256):
    n = S // ts

    def body(dh_ref, x_ref, dy_ref, g_ref, gx_ref, dg_ref):
        i = pl.program_id(0)

        @pl.when(i == 0)
        def _():
            dg_ref[...] = jnp.zeros_like(dg_ref)

        xv = x_ref[...]
        dhv = dh_ref[...]
        inv = lax.rsqrt(jnp.mean(xv * xv, axis=-1, keepdims=True) + EPS)
        xn = xv * inv
        dg_ref[...] += jnp.sum(dhv * xn, axis=0, keepdims=True)
        dxn = dhv * g_ref[...]
        gx_ref[...] = dy_ref[...].astype(F32) + inv * (dxn - xn * jnp.mean(dxn * xn, axis=-1, keepdims=True))

    row = pl.BlockSpec((ts, D), lambda i: (i, 0))
    vec = pl.BlockSpec((1, D), lambda i: (0, 0))
    return pl.pallas_call(
        body,
        name="pre_bwd",
        grid=(n,),
        in_specs=[row, row, row, vec],
        out_specs=[row, vec],
        out_shape=[jax.ShapeDtypeStruct((S, D), F32), jax.ShapeDtypeStruct((1, D), F32)],
        compiler_params=_params(("arbitrary",)),
    )(dh, x, dy, g_pre)


def _memnorm_bwd(dmemn, mem):
    def body(d_ref, m_ref, dg_ref):
        mv = m_ref[...]
        inv = lax.rsqrt(jnp.mean(mv * mv, axis=-1, keepdims=True) + EPS)
        dg_ref[...] = jnp.sum(d_ref[...] * mv * inv, axis=0, keepdims=True)

    return pl.pallas_call(
        body,
        name="memnorm_bwd",
        out_shape=jax.ShapeDtypeStruct((1, D), F32),
        compiler_params=_params(),
    )(dmemn, mem)


T_RNN = 256


def _neg_expm1(z):
    poly = -z * (1.0 + z * (0.5 + z * (1.0 / 6 + z * (1.0 / 24 + z * (1.0 / 120 + z * (1.0 / 720))))))
    return jnp.where(z > -0.1, poly, 1.0 - jnp.exp(z))


def _softplus_neg(lam):
    return jnp.maximum(-lam, 0.0) + jnp.log1p(jnp.exp(-jnp.abs(lam)))


def _rnn_gates(conv, wa_ref, ba, wx_ref, bx, lam, first_row):
    cbf = conv.astype(BF16)
    ga, gx = [], []
    for n in range(RNN_BLOCKS):
        c_n = cbf[:, n * LANE:(n + 1) * LANE]
        ga.append(jnp.dot(c_n, wa_ref[n], preferred_element_type=F32))
        gx.append(jnp.dot(c_n, wx_ref[n], preferred_element_type=F32))
    gate_r = _sigmoid(jnp.concatenate(ga, axis=1) + ba)
    gate_i = _sigmoid(jnp.concatenate(gx, axis=1) + bx)
    sp = _softplus_neg(lam)
    log_a = -LRU_C * gate_r * sp
    a = jnp.exp(log_a)
    mult_raw = jnp.sqrt(_neg_expm1(2.0 * log_a))
    mult = jnp.where(first_row, 1.0, mult_raw)
    return cbf, gate_r, gate_i, sp, a, mult_raw, mult


def _rglru_fwd(p_a, conv_w, conv_b, wa, ba, wx, bx, lam):
    t = T_RNN
    n = S // t

    def body(xr_ref, g_ref, cw_ref, cb_ref, wa_ref, ba_ref, wx_ref, bx_ref, lam_ref,
             y_ref, h_ref, xp_s, hcar, a_s, b_s):
        i = pl.program_id(0)

        @pl.when(i == 0)
        def _():
            xp_s[0:8, :] = jnp.zeros((8, D_RNN), F32)
            hcar[...] = jnp.zeros_like(hcar)

        @pl.when(i > 0)
        def _():
            xp_s[0:8, :] = xp_s[t:t + 8, :]

        xp_s[8:8 + t, :] = xr_ref[...]
        conv = cb_ref[...]
        for k in range(CONV_W):
            conv = conv + cw_ref[k:k + 1, :] * xp_s[8 - k:8 - k + t, :]
        rows = i * t + lax.broadcasted_iota(jnp.int32, (t, 1), 0)
        _, _, gate_i, _, a, _, mult = _rnn_gates(
            conv, wa_ref, ba_ref[...], wx_ref, bx_ref[...], lam_ref[...], rows == 0)
        a_s[...] = a
        b_s[...] = mult * gate_i * conv

        def step(tt, h):
            h = a_s[pl.ds(tt, 1), :] * h + b_s[pl.ds(tt, 1), :]
            h_ref[pl.ds(tt, 1), :] = h
            return h

        hcar[...] = lax.fori_loop(0, t, step, hcar[...], unroll=8)
        g = g_ref[...]
        y_ref[...] = (h_ref[...] * (g * _sigmoid(g))).astype(BF16)

    blk = lambda c: pl.BlockSpec((t, D_RNN), lambda i: (i, c))
    full = lambda shape: pl.BlockSpec(shape, lambda i: (0,) * len(shape))
    return pl.pallas_call(
        body,
        name="rglru_fwd",
        grid=(n,),
        in_specs=[blk(0), blk(1), full((CONV_W, D_RNN)), full((1, D_RNN)),
                  full((RNN_BLOCKS, LANE, LANE)), full((1, D_RNN)),
                  full((RNN_BLOCKS, LANE, LANE)), full((1, D_RNN)), full((1, D_RNN))],
        out_specs=[blk(0), blk(0)],
        out_shape=[jax.ShapeDtypeStruct((S, D_RNN), BF16), jax.ShapeDtypeStruct((S, D_RNN), F32)],
        scratch_shapes=[pltpu.VMEM((t + 8, D_RNN), F32), pltpu.VMEM((1, D_RNN), F32),
                        pltpu.VMEM((t, D_RNN), F32), pltpu.VMEM((t, D_RNN), F32)],
        compiler_params=_params(("arbitrary",)),
    )(p_a, p_a, conv_w, conv_b, wa, ba, wx, bx, lam)


def _rglru_bwd(dy, p_a, hseq, conv_w, conv_b, wa, ba, wx, bx, lam):
    t = T_RNN
    n = S // t
    rb = t // 8

    def body(dy_ref, xr_ref, g_ref, h_ref, xrp_ref, hp_ref, cw_ref, cb_ref, wa_ref, ba_ref, wx_ref, bx_ref, lam_ref,
             dp_ref, dcw_ref, dcb_ref, dwa_ref, dba_ref, dwx_ref, dbx_ref, dlam_ref,
             xp_s, hp_s, dxp_s, lamcar, a_s, dh_s, lam_s):
        i = pl.program_id(0)
        r = n - 1 - i

        @pl.when(i == 0)
        def _():
            for ref in (dcw_ref, dcb_ref, dwa_ref, dba_ref, dwx_ref, dbx_ref, dlam_ref, lamcar):
                ref[...] = jnp.zeros_like(ref)
            dxp_s[t:t + 8, :] = jnp.zeros((8, D_RNN), F32)

        @pl.when(i > 0)
        def _():
            dxp_s[t:t + 8, :] = dxp_s[0:8, :]

        has_prev = r > 0
        xp_s[0:8, :] = jnp.where(has_prev, xrp_ref[...], 0.0)
        xp_s[8:8 + t, :] = xr_ref[...]
        hp_s[0:8, :] = jnp.where(has_prev, hp_ref[...], 0.0)
        hp_s[8:8 + t, :] = h_ref[...]
        xs = [xp_s[8 - k:8 - k + t, :] for k in range(CONV_W)]
        conv = cb_ref[...]
        for k in range(CONV_W):
            conv = conv + cw_ref[k:k + 1, :] * xs[k]
        rows = r * t + lax.broadcasted_iota(jnp.int32, (t, 1), 0)
        first = rows == 0
        lam_p = lam_ref[...]
        cbf, gate_r, gate_i, sp, a, mult_raw, mult = _rnn_gates(
            conv, wa_ref, ba_ref[...], wx_ref, bx_ref[...], lam_p, first)

        g = g_ref[...]
        sg = _sigmoid(g)
        dyv = dy_ref[...]
        a_s[...] = a
        dh_s[...] = dyv * (g * sg)
        dg = dyv * h_ref[...] * (sg * (1.0 + g * (1.0 - sg)))

        def step(jj, car):
            tt = t - 1 - jj
            lm = dh_s[pl.ds(tt, 1), :] + car
            lam_s[pl.ds(tt, 1), :] = lm
            return a_s[pl.ds(tt, 1), :] * lm

        lamcar[...] = lax.fori_loop(0, t, step, lamcar[...], unroll=8)
        db = lam_s[...]
        da = db * hp_s[7:7 + t, :]
        dmult = db * gate_i * conv
        dgate_i = db * mult * conv
        dconv = db * mult * gate_i
        dlog_a = da * a + jnp.where(first, 0.0, dmult * (-(a * a) / mult_raw))
        dgate_r = dlog_a * (-LRU_C * sp)
        dsp = jnp.sum(dlog_a * (-LRU_C * gate_r), axis=0, keepdims=True)
        dlam_ref[...] += dsp * (-_sigmoid(-lam_p))
        dga = dgate_r * gate_r * (1.0 - gate_r)
        dgx = dgate_i * gate_i * (1.0 - gate_i)
        dba_ref[...] += jnp.sum(dga, axis=0, keepdims=True)
        dbx_ref[...] += jnp.sum(dgx, axis=0, keepdims=True)
        dga16, dgx16 = dga.astype(BF16), dgx.astype(BF16)
        back = []
        for nb in range(RNN_BLOCKS):
            sl = slice(nb * LANE, (nb + 1) * LANE)
            dwa_ref[nb] += lax.dot_general(cbf[:, sl], dga16[:, sl], _DIMS["tn"], preferred_element_type=F32)
            dwx_ref[nb] += lax.dot_general(cbf[:, sl], dgx16[:, sl], _DIMS["tn"], preferred_element_type=F32)
            back.append(lax.dot_general(dga16[:, sl], wa_ref[nb], _DIMS["nt"], preferred_element_type=F32)
                        + lax.dot_general(dgx16[:, sl], wx_ref[nb], _DIMS["nt"], preferred_element_type=F32))
        dconv = dconv + jnp.concatenate(back, axis=1)
        dcb_ref[...] += jnp.sum(dconv, axis=0, keepdims=True)
        for k in range(CONV_W):
            dcw_ref[k:k + 1, :] += jnp.sum(dconv * xs[k], axis=0, keepdims=True)
        dxp_s[0:t, :] = dconv
        dxr = cw_ref[0:1, :] * dconv
        for k in range(1, CONV_W):
            dxr = dxr + cw_ref[k:k + 1, :] * dxp_s[k:k + t, :]
        dp_ref[:, 0:D_RNN] = dxr.astype(BF16)
        dp_ref[:, D_RNN:2 * D_RNN] = dg.astype(BF16)

    blk = lambda c: pl.BlockSpec((t, D_RNN), lambda i: (n - 1 - i, c))
    prev8 = pl.BlockSpec((8, D_RNN), lambda i: (jnp.maximum((n - 1 - i) * rb - 1, 0), 0))
    full = lambda shape: pl.BlockSpec(shape, lambda i: (0,) * len(shape))
    vec = full((1, D_RNN))
    mat = full((RNN_BLOCKS, LANE, LANE))
    return pl.pallas_call(
        body,
        name="rglru_bwd",
        grid=(n,),
        in_specs=[blk(0), blk(0), blk(1), blk(0), prev8, prev8,
                  full((CONV_W, D_RNN)), vec, mat, vec, mat, vec, vec],
        out_specs=[pl.BlockSpec((t, 2 * D_RNN), lambda i: (n - 1 - i, 0)),
                   full((CONV_W, D_RNN)), vec, mat, vec, mat, vec, vec],
        out_shape=[jax.ShapeDtypeStruct((S, 2 * D_RNN), BF16),
                   jax.ShapeDtypeStruct((CONV_W, D_RNN), F32), jax.ShapeDtypeStruct((1, D_RNN), F32),
                   jax.ShapeDtypeStruct((RNN_BLOCKS, LANE, LANE), F32), jax.ShapeDtypeStruct((1, D_RNN), F32),
                   jax.ShapeDtypeStruct((RNN_BLOCKS, LANE, LANE), F32), jax.ShapeDtypeStruct((1, D_RNN), F32),
                   jax.ShapeDtypeStruct((1, D_RNN), F32)],
        scratch_shapes=[pltpu.VMEM((t + 8, D_RNN), F32), pltpu.VMEM((t + 8, D_RNN), F32),
                        pltpu.VMEM((t + 8, D_RNN), F32), pltpu.VMEM((1, D_RNN), F32),
                        pltpu.VMEM((t, D_RNN), F32), pltpu.VMEM((t, D_RNN), F32), pltpu.VMEM((t, D_RNN), F32)],
        compiler_params=_params(("arbitrary",)),
    )(dy, p_a, p_a, hseq, p_a, hseq, conv_w, conv_b, wa, ba, wx, bx, lam)


QB = WINDOW
KB2 = 2 * WINDOW
N_QB = S // QB
N_PAIR = SWA_HEADS // 2


def _swa_keys(kvc_ref, kvp_ref):
    kk = jnp.concatenate([kvp_ref[:, 0:LANE], kvc_ref[:, 0:LANE]], axis=0)
    vv = jnp.concatenate([kvp_ref[:, LANE:2 * LANE], kvc_ref[:, LANE:2 * LANE]], axis=0)
    lo = lax.broadcasted_iota(jnp.int32, (1, LANE), 1) < SWA_HD
    kk_sw, vv_sw = pltpu.roll(kk, SWA_HD, 1), pltpu.roll(vv, SWA_HD, 1)
    kd = [jnp.where(lo, kk, kk_sw).astype(BF16), jnp.where(lo, kk_sw, kk).astype(BF16)]
    vd = [jnp.where(lo, vv, vv_sw).astype(BF16), jnp.where(lo, vv_sw, vv).astype(BF16)]
    return lo, kd, vd


GRP = SWA_HEADS // 2
STACK = GRP
GQ = STACK * QB


def _swa_valid(n, rows):
    qi = lax.broadcasted_iota(jnp.int32, (rows, KB2), 0) % QB
    kj = lax.broadcasted_iota(jnp.int32, (rows, KB2), 1)
    dist = qi + WINDOW - kj
    return (dist >= 0) & (dist < WINDOW) & ((n > 0) | (kj >= WINDOW))


def _swa_stack(tile_of, lo, h0, masked):
    parts = []
    for h in range(h0, h0 + STACK):
        t = tile_of(h // 2)
        if masked:
            t = jnp.where(lo if h % 2 == 0 else jnp.logical_not(lo), t, 0.0)
        parts.append(t)
    return jnp.concatenate(parts, axis=0)


def _swa_unstack(stacked, lo, pair):
    return jnp.where(lo, stacked[2 * pair * QB:(2 * pair + 1) * QB], stacked[(2 * pair + 1) * QB:(2 * pair + 2) * QB])


def _swa_softmax(lg, sink, valid):
    lg = jnp.where(valid, lg, NEG_INF)
    m = jnp.maximum(jnp.max(lg, axis=-1, keepdims=True), sink)
    p = jnp.exp(lg - m)
    es = jnp.exp(sink - m)
    den = jnp.sum(p, axis=-1, keepdims=True) + es
    return p / den, es / den


def _swa_probs_head(qh16, kd, bias, sink, valid):
    lg = lax.dot_general(qh16, kd, _DIMS["nt"], preferred_element_type=F32) * (SWA_HD ** -0.5) + bias
    return _swa_softmax(lg, sink, valid)[0]


def _swa_probs(q16, kd, bias_ref, sink_ref, h0, valid):
    bias = bias_ref[h0:h0 + STACK].reshape(GQ, KB2)
    sink = jnp.concatenate([jnp.full((QB, 1), sink_ref[h], F32) for h in range(h0, h0 + STACK)], axis=0)
    lg = lax.dot_general(q16, kd, _DIMS["nt"], preferred_element_type=F32) * (SWA_HD ** -0.5) + bias
    return _swa_softmax(lg, sink, valid)


def _swa_specs():
    q = pl.BlockSpec((QB, D_RNN), lambda n: (n, 0))
    g = pl.BlockSpec((QB, D_RNN), lambda n: (n, 1))
    kvc = pl.BlockSpec((QB, 2 * LANE), lambda n: (n, 8))
    kvp = pl.BlockSpec((QB, 2 * LANE), lambda n: (jnp.maximum(n - 1, 0), 8))
    bias = pl.BlockSpec((SWA_HEADS, QB, KB2), lambda n: (0, 0, 0))
    sinks = pl.BlockSpec(memory_space=pltpu.SMEM)
    return q, g, kvc, kvp, bias, sinks


def _swa_fwd(p_b, bias_t, sinks):
    def body(q_ref, g_ref, kvc_ref, kvp_ref, bias_ref, sink_ref, y_ref, o_ref):
        n = pl.program_id(0)
        lo, kd, vd = _swa_keys(kvc_ref, kvp_ref)
        valid = _swa_valid(n, QB)
        for hp in range(N_PAIR):
            sl = slice(hp * LANE, (hp + 1) * LANE)
            kvh = hp // (N_PAIR // 2)
            q = q_ref[:, sl]
            outs = []
            for j in range(2):
                qh16 = jnp.where(lo if j == 0 else jnp.logical_not(lo), q, 0.0).astype(BF16)
                probs = _swa_probs_head(qh16, kd[kvh], bias_ref[2 * hp + j], sink_ref[2 * hp + j], valid)
                outs.append(jnp.dot(probs.astype(BF16), vd[kvh], preferred_element_type=F32))
            o = jnp.where(lo, outs[0], outs[1])
            o_ref[:, sl] = o
            g = g_ref[:, sl]
            y_ref[:, sl] = (o * (g * _sigmoid(g))).astype(BF16)

    q, g, kvc, kvp, bias, sinks_spec = _swa_specs()
    out = pl.BlockSpec((QB, D_RNN), lambda n: (n, 0))
    return pl.pallas_call(
        body,
        name="swa_fwd",
        grid=(N_QB,),
        in_specs=[q, g, kvc, kvp, bias, sinks_spec],
        out_specs=[out, out],
        out_shape=[jax.ShapeDtypeStruct((S, D_RNN), BF16), jax.ShapeDtypeStruct((S, D_RNN), F32)],
        compiler_params=_params(("parallel",)),
    )(p_b, p_b, p_b, p_b, bias_t, sinks)


def _swa_bwd(dy, p_b, o_swa, bias_t, sinks, after=None):
    def body(dy_ref, q_ref, g_ref, kvc_ref, kvp_ref, o_ref, bias_ref, sink_ref, *rest):
        dp_ref, dk_ref, dv_ref, dbias_ref, dsink_ref, do_s = rest[-6:]
        n = pl.program_id(0)

        @pl.when(n == 0)
        def _():
            for ref in (dk_ref, dv_ref, dbias_ref, dsink_ref):
                ref[...] = jnp.zeros_like(ref)

        lo, kd, vd = _swa_keys(kvc_ref, kvp_ref)
        hi = jnp.logical_not(lo)
        valid = _swa_valid(n, GQ)
        tile = lambda ref: (lambda hp: ref[:, hp * LANE:(hp + 1) * LANE])
        for hp in range(N_PAIR):
            sl = slice(hp * LANE, (hp + 1) * LANE)
            g, dyv = g_ref[:, sl], dy_ref[:, sl]
            sg = _sigmoid(g)
            do_s[:, sl] = dyv * (g * sg)
            dp_ref[:, D_RNN + hp * LANE:D_RNN + (hp + 1) * LANE] = (
                dyv * o_ref[:, sl] * (sg * (1.0 + g * (1.0 - sg)))).astype(BF16)

        dk_blk = jnp.zeros((KB2, LANE), F32)
        dv_blk = jnp.zeros((KB2, LANE), F32)
        for h0 in range(0, SWA_HEADS, STACK):
            kvh = h0 // GRP
            q16 = _swa_stack(tile(q_ref), lo, h0, masked=True).astype(BF16)
            do8 = _swa_stack(tile(do_s), lo, h0, masked=True)
            do16 = do8.astype(BF16)
            delta = jnp.sum(do8 * _swa_stack(tile(o_ref), lo, h0, masked=False), axis=-1, keepdims=True)
            probs, psink = _swa_probs(q16, kd[kvh], bias_ref, sink_ref, h0, valid)
            dpr = lax.dot_general(do16, vd[kvh], _DIMS["nt"], preferred_element_type=F32)
            ds = probs * (dpr - delta)
            sink_term = psink * delta
            for g in range(STACK):
                h, rows = h0 + g, slice(g * QB, (g + 1) * QB)
                dbias_ref[h] += ds[rows]
                dsink_ref[h:h + 1, :] += jnp.zeros((1, LANE), F32) - jnp.sum(sink_term[rows])
            ds16 = (ds * (SWA_HD ** -0.5)).astype(BF16)
            dq_all = jnp.dot(ds16, kd[kvh], preferred_element_type=F32)
            for pair in range(STACK // 2):
                sl = slice((h0 // 2 + pair) * LANE, (h0 // 2 + pair + 1) * LANE)
                dp_ref[:, sl] = _swa_unstack(dq_all, lo, pair).astype(BF16)
            dk_pair = lax.dot_general(ds16, q16, _DIMS["tn"], preferred_element_type=F32)
            dv_pair = lax.dot_general(probs.astype(BF16), do16, _DIMS["tn"], preferred_element_type=F32)
            keep = lo if kvh == 0 else hi
            dk_blk = dk_blk + jnp.where(keep, dk_pair + pltpu.roll(dk_pair, SWA_HD, 1), 0.0)
            dv_blk = dv_blk + jnp.where(keep, dv_pair + pltpu.roll(dv_pair, SWA_HD, 1), 0.0)

        cur = pl.ds(pl.multiple_of(n * QB, QB), QB)
        dk_ref[cur, :] += dk_blk[QB:KB2]
        dv_ref[cur, :] += dv_blk[QB:KB2]

        @pl.when(n > 0)
        def _():
            prev = pl.ds(pl.multiple_of((n - 1) * QB, QB), QB)
            dk_ref[prev, :] += dk_blk[0:QB]
            dv_ref[prev, :] += dv_blk[0:QB]

    q, g, kvc, kvp, bias, sinks_spec = _swa_specs()
    row = pl.BlockSpec((QB, D_RNN), lambda n: (n, 0))
    acc = pl.BlockSpec((S, LANE), lambda n: (0, 0))
    return pl.pallas_call(
        body,
        name="swa_bwd",
        grid=(N_QB,),
        in_specs=[row, q, g, kvc, kvp, row, bias, sinks_spec] + ([ANY] if after is not None else []),
        out_specs=[pl.BlockSpec((QB, 2 * D_RNN), lambda n: (n, 0)), acc, acc, bias,
                   pl.BlockSpec((SWA_HEADS, LANE), lambda n: (0, 0))],
        out_shape=[jax.ShapeDtypeStruct((S, GROUP_TILES["B"] * LANE), BF16),
                   jax.ShapeDtypeStruct((S, LANE), F32), jax.ShapeDtypeStruct((S, LANE), F32),
                   jax.ShapeDtypeStruct((SWA_HEADS, QB, KB2), F32),
                   jax.ShapeDtypeStruct((SWA_HEADS, LANE), F32)],
        scratch_shapes=[pltpu.VMEM((QB, D_RNN), F32)],
        compiler_params=_params(("arbitrary",)),
    )(dy, p_b, p_b, p_b, p_b, o_swa, bias_t, sinks, *([after] if after is not None else []))


def _swa_pack(dp_b, dk, dv, ts=512):
    def body(_, dk_ref, dv_ref, o_ref):
        o_ref[:, 0:LANE] = dk_ref[...].astype(BF16)
        o_ref[:, LANE:2 * LANE] = dv_ref[...].astype(BF16)

    tile = pl.BlockSpec((ts, LANE), lambda i: (i, 0))
    return pl.pallas_call(
        body,
        name="swa_pack",
        grid=(S // ts,),
        in_specs=[pl.BlockSpec(memory_space=pl.ANY), tile, tile],
        out_specs=pl.BlockSpec((ts, 2 * LANE), lambda i: (i, 8)),
        out_shape=jax.ShapeDtypeStruct(dp_b.shape, dp_b.dtype),
        input_output_aliases={0: 0},
        compiler_params=_params(("parallel",)),
    )(dp_b, dk, dv)


def _split3(v):
    a = v.astype(BF16)
    r = v - a.astype(F32)
    b = r.astype(BF16)
    c = (r - b.astype(F32)).astype(BF16)
    return a, b, c


def _relbias_grad(dbias_flat, onehot_t):
    def body(d_ref, e_ref, o_ref):
        e = e_ref[...]
        acc = jnp.zeros((SWA_HEADS, REL_BUCKETS), F32)
        for term in _split3(d_ref[...]):
            acc = acc + lax.dot_general(term, e, _DIMS["nt"], preferred_element_type=F32)
        o_ref[...] = acc

    return pl.pallas_call(
        body,
        name="relbias_grad",
        out_shape=jax.ShapeDtypeStruct((SWA_HEADS, REL_BUCKETS), F32),
        compiler_params=_params(),
    )(dbias_flat, onehot_t)


TS_MEM = 512


def _mem_probs(q16, mk):
    lg = lax.dot_general(q16, mk, _DIMS["nt"], preferred_element_type=F32) * (MEM_HD ** -0.5)
    p = jnp.exp(lg - jnp.max(lg, axis=-1, keepdims=True))
    return p / jnp.sum(p, axis=-1, keepdims=True)


def _mem_fwd(p_c, mkv):
    def body(q_ref, g_ref, mkv_ref, y_ref, o_ref):
        for hm in range(MEM_HEADS):
            sl = slice(hm * MEM_HD, (hm + 1) * MEM_HD)
            probs = _mem_probs(q_ref[:, sl].astype(BF16), mkv_ref[:, sl])
            o = jnp.dot(probs.astype(BF16), mkv_ref[:, D_RNN + hm * MEM_HD:D_RNN + (hm + 1) * MEM_HD],
                        preferred_element_type=F32)
            o_ref[:, sl] = o
            g = g_ref[:, sl]
            y_ref[:, sl] = (o * (g * _sigmoid(g))).astype(BF16)

    blk = lambda c: pl.BlockSpec((TS_MEM, D_RNN), lambda i: (i, c))
    return pl.pallas_call(
        body,
        name="mem_fwd",
        grid=(S // TS_MEM,),
        in_specs=[blk(0), blk(1), pl.BlockSpec((MEM, 2 * D_RNN), lambda i: (0, 0))],
        out_specs=[blk(0), blk(0)],
        out_shape=[jax.ShapeDtypeStruct((S, D_RNN), BF16), jax.ShapeDtypeStruct((S, D_RNN), F32)],
        compiler_params=_params(("parallel",)),
    )(p_c, p_c, mkv)


def _mem_bwd(dy, p_c, o_mem, mkv):
    def body(dy_ref, q_ref, g_ref, o_ref, mkv_ref, dp_ref, dmkv_ref):
        @pl.when(pl.program_id(0) == 0)
        def _():
            dmkv_ref[...] = jnp.zeros_like(dmkv_ref)

        for hm in range(MEM_HEADS):
            sl = slice(hm * MEM_HD, (hm + 1) * MEM_HD)
            sv = slice(D_RNN + hm * MEM_HD, D_RNN + (hm + 1) * MEM_HD)
            q16 = q_ref[:, sl].astype(BF16)
            mk, mv = mkv_ref[:, sl], mkv_ref[:, sv]
            probs = _mem_probs(q16, mk)
            g, o, dyv = g_ref[:, sl], o_ref[:, sl], dy_ref[:, sl]
            sg = _sigmoid(g)
            do = dyv * (g * sg)
            dp_ref[:, sv] = (dyv * o * (sg * (1.0 + g * (1.0 - sg)))).astype(BF16)
            do16 = do.astype(BF16)
            delta = jnp.sum(do * o, axis=-1, keepdims=True)
            dpr = lax.dot_general(do16, mv, _DIMS["nt"], preferred_element_type=F32)
            ds16 = (probs * (dpr - delta) * (MEM_HD ** -0.5)).astype(BF16)
            dp_ref[:, sl] = jnp.dot(ds16, mk, preferred_element_type=F32).astype(BF16)
            dmkv_ref[:, sl] += lax.dot_general(ds16, q16, _DIMS["tn"], preferred_element_type=F32)
            dmkv_ref[:, sv] += lax.dot_general(probs.astype(BF16), do16, _DIMS["tn"], preferred_element_type=F32)

    blk = lambda c: pl.BlockSpec((TS_MEM, D_RNN), lambda i: (i, c))
    kv = pl.BlockSpec((MEM, 2 * D_RNN), lambda i: (0, 0))
    return pl.pallas_call(
        body,
        name="mem_bwd",
        grid=(S // TS_MEM,),
        in_specs=[blk(0), blk(0), blk(1), blk(0), kv],
        out_specs=[pl.BlockSpec((TS_MEM, 2 * D_RNN), lambda i: (i, 0)), kv],
        out_shape=[jax.ShapeDtypeStruct((S, 2 * D_RNN), BF16), jax.ShapeDtypeStruct((MEM, 2 * D_RNN), F32)],
        compiler_params=_params(("arbitrary",)),
    )(dy, p_c, p_c, o_mem, mkv)


TS_MRG = 512
TD_MRG = 1024
N_DBLK = D // TD_MRG


def _merge_fwd(z, p_d):
    def body(z0, z1, z2, g0, g1, g2, o_ref):
        term = lambda g, z: _sigmoid(g[...].astype(F32)) * z[...].astype(F32)
        o_ref[...] = (term(g0, z0) + term(g1, z1) + term(g2, z2)).astype(BF16)

    blk = pl.BlockSpec((TS_MRG, TD_MRG), lambda i, d: (i, d))
    gate = lambda b: pl.BlockSpec((TS_MRG, TD_MRG), lambda i, d: (i, b * N_DBLK + d))
    return pl.pallas_call(
        body,
        name="merge_fwd",
        grid=(S // TS_MRG, N_DBLK),
        in_specs=[blk, blk, blk, gate(0), gate(1), gate(2)],
        out_specs=blk,
        out_shape=jax.ShapeDtypeStruct((S, D), BF16),
        compiler_params=_params(("parallel", "parallel")),
    )(z[0], z[1], z[2], p_d, p_d, p_d)


TS_MRG_BWD = 128


def _merge_bwd(dmerged, z, p_d, after):
    def body(dm_ref, z0, z1, z2, g_ref, _, dz0, dz1, dz2, dg_ref):
        dm = dm_ref[...].astype(F32)
        for b, (z_ref, dz_ref) in enumerate(((z0, dz0), (z1, dz1), (z2, dz2))):
            cols = slice(b * D, (b + 1) * D)
            sg = _sigmoid(g_ref[:, cols].astype(F32))
            dz_ref[...] = (dm * sg).astype(BF16)
            dg_ref[:, cols] = (dm * z_ref[...].astype(F32) * sg * (1.0 - sg)).astype(BF16)

    row = pl.BlockSpec((TS_MRG_BWD, D), lambda i: (i, 0))
    wide = pl.BlockSpec((TS_MRG_BWD, 3 * D), lambda i: (i, 0))
    outs = pl.pallas_call(
        body,
        name="merge_bwd",
        grid=(S // TS_MRG_BWD,),
        in_specs=[row, row, row, row, wide, pl.BlockSpec(memory_space=pl.ANY)],
        out_specs=[row, row, row, wide],
        out_shape=[jax.ShapeDtypeStruct((S, D), BF16)] * 3 + [jax.ShapeDtypeStruct((S, 3 * D), BF16)],
        compiler_params=_params(("parallel",)),
    )(dmerged, z[0], z[1], z[2], p_d, after)
    return list(outs[:3]), outs[3]


def _bucket_table():
    import numpy as np
    qi = np.arange(QB)[:, None]
    kj = np.arange(KB2)[None, :]
    n = np.maximum(qi + WINDOW - kj, 0)
    max_exact = REL_BUCKETS // 2
    ratio = np.log(np.maximum(n, 1).astype(np.float32) / max_exact) / np.float32(math.log(REL_MAX_DIST / max_exact))
    large = np.minimum(max_exact + (ratio * (REL_BUCKETS - max_exact)).astype(np.int32), REL_BUCKETS - 1)
    bucket = np.where(n < max_exact, n, large).reshape(1, QB * KB2)
    return (bucket == np.arange(REL_BUCKETS)[:, None]).astype(np.float32)


def _bias_expand(rel_bias_t, onehot_t):
    def body(r_ref, e_ref, o_ref):
        e = e_ref[...]
        acc = jnp.zeros((SWA_HEADS, QB * KB2), F32)
        for term in _split3(r_ref[...]):
            acc = acc + jnp.dot(term, e, preferred_element_type=F32)
        o_ref[...] = acc

    return pl.pallas_call(
        body,
        name="bias_expand",
        out_shape=jax.ShapeDtypeStruct((SWA_HEADS, QB * KB2), F32),
        compiler_params=_params(),
    )(rel_bias_t, onehot_t)


PROJ_TN = {"A": 1024, "B": 1152, "C": 1024, "D": 1536}


def _do_first(arrays, token):
    def body(*refs):
        refs[-1][...] = jnp.zeros_like(refs[-1])

    return pl.pallas_call(
        body,
        name="do_first",
        in_specs=[pl.BlockSpec(memory_space=pl.ANY)] * (len(arrays) + 1),
        out_specs=pl.BlockSpec(memory_space=pltpu.VMEM),
        out_shape=jax.ShapeDtypeStruct((8, LANE), F32),
    )(*arrays, token)


def _local_step(x, h, mem, tgt, sp, early, fetch, prefetch, emit, advance):
    onehot_t = jnp.asarray(_bucket_table(), BF16)
    bias_t = _bias_expand(sp["rel_bias"].T, onehot_t).reshape(SWA_HEADS, QB, KB2)
    sinks = sp["swa_sinks"].reshape(SWA_HEADS)
    wa16, wx16 = sp["w_rg_a"].astype(BF16), sp["w_rg_x"].astype(BF16)
    rnn = (sp["conv_w"], sp["conv_b"], wa16, sp["b_rg_a"], wx16, sp["b_rg_x"], sp["lru_lambda"])

    memn = _rms_fwd(mem, sp["mem_norm_g"], "rms_mem", h)
    h_and_prep = _do_first([bias_t, memn, wa16, wx16, *early], h)
    w_grp, p = {}, {}

    def project(g, after, then=None):
        (w_grp[g],) = fetch((g,), after)
        tok = prefetch(then, w_grp[g]) if then is not None else None
        p[g] = _mm(h, w_grp[g], "nt", BF16 if g == "D" else F32, 1024, PROJ_TN[g], D, f"proj_{g}", after=tok)

    project("A", h_and_prep)
    y_rg, hseq = _rglru_fwd(p["A"], *rnn)
    project("B", y_rg)
    y_swa, o_swa = _swa_fwd(p["B"], bias_t, sinks)
    project("C", y_swa, then=("mk",))
    (wmk,) = fetch(("mk",), p["C"])
    tok = prefetch(("br0", "br1", "br2"), wmk)
    mkv = _mm(memn, wmk, "nn", BF16, MEM, 1024, D, "mkv", after=tok)
    y_mem, o_mem = _mem_fwd(p["C"], mkv)
    ys = (y_rg, y_swa, y_mem)
    wbr = fetch(("br0", "br1", "br2"), y_mem)
    tok = prefetch(("D",), wbr[2])
    z = []
    for b in range(3):
        z.append(_mm(ys[b], wbr[b], "nn", BF16, 1024, 1024, D_RNN, f"branch_out{b}", after=z[-1] if z else tok))
    project("D", z[2], then=("out",))
    merged = _merge_fwd(z, p["D"])
    (wout,) = fetch(("out",), merged)
    out = _mm(merged, wout, "nn", F32, 1024, 1024, D, "out_proj")
    sq, dy, dout, d_post = _post_loss(out, x, tgt, sp["post_norm_g"])

    tok = emit({"out": _mm(merged, dout, "tn", BF16, 1024, 1024, S, "d_wout")})
    dmerged = _mm(dout, wout, "nt", BF16, 1024, 1024, D, "d_merged", after=tok)
    tok = advance(dmerged)
    dz, dp_d = _merge_bwd(dmerged, z, p["D"], tok)
    d_win = lambda g, dp_g, after=None: _mm(dp_g, h, "tn", BF16, PROJ_TN[g], 1024, S, f"d_win_{g}", after=after)
    tok = emit({f"br{b}": _mm(ys[b], dz[b], "tn", BF16, 1024, 1024, S, f"d_wbr{b}") for b in range(3)}, tok)
    d_w_d = d_win("D", dp_d, tok)
    tok = emit({"D": d_w_d}, advance(d_w_d))
    dy_mem = _mm(dz[2], wbr[2], "nt", F32, 1024, 1024, D, "d_branch2", after=tok)
    tok = advance(dy_mem)
    dp_c, dmkv = _mem_bwd(dy_mem, p["C"], o_mem, mkv)
    dmkv16 = dmkv.astype(BF16)
    tok = emit({"mk": _mm(memn, dmkv16, "tn", BF16, 1024, 1024, MEM, "d_wmk", after=tok), "C": d_win("C", dp_c)}, tok)
    dmemn = _mm(dmkv16, wmk, "nt", F32, MEM, 1024, D, "d_memn", after=tok)
    tok = advance(dmemn)
    d_memg = _memnorm_bwd(dmemn, mem)
    dy_rg = _mm(dz[0], wbr[0], "nt", F32, 1024, 1024, D, "d_branch0", after=tok)
    dp_a, d_cw, d_cb, d_wa, d_ba, d_wx, d_bx, d_lam = _rglru_bwd(dy_rg, p["A"], hseq, *rnn)
    tok = emit({"A": d_win("A", dp_a)}, tok)
    dy_swa = _mm(dz[1], wbr[1], "nt", F32, 1024, 1024, D, "d_branch1", after=tok)
    tok = advance(dy_swa)
    dp_b, dk, dv, d_bias, d_sink = _swa_bwd(dy_swa, p["B"], o_swa, bias_t, sinks, after=tok)
    dp_b = _swa_pack(dp_b, dk, dv)
    d_rel = _relbias_grad(d_bias.reshape(SWA_HEADS, QB * KB2), onehot_t).T
    dp = {"A": dp_a, "B": dp_b, "C": dp_c, "D": dp_d}
    tok = emit({"B": d_win("B", dp_b)}, tok)
    dh = None
    for g in GROUPS:
        dh = _mm(dp[g], w_grp[g], "nn", F32, 1024, 1024, 2304 if g == "B" else 2048, f"d_h_{g}", acc=dh,
                 after=tok if g in ("A", "B") else None)
        if g == "A":
            tok = advance(dh)
    grad_x, d_pre = _pre_bwd(dh, x, dy, sp["pre_norm_g"])

    d_small = {
        "pre_norm_g": d_pre, "post_norm_g": d_post, "mem_norm_g": d_memg, "conv_w": d_cw, "conv_b": d_cb,
        "w_rg_a": d_wa, "b_rg_a": d_ba, "w_rg_x": d_wx, "b_rg_x": d_bx, "lru_lambda": d_lam,
        "swa_sinks": d_sink[:, 0].reshape(1, SWA_HEADS), "rel_bias": d_rel,
    }
    return sq, grad_x, d_small


ANY = pl.BlockSpec(memory_space=pl.ANY)
SHARD_ROWS = D // N_CHIPS
GATHERED = {"A": (2048, D), "B": (2304, D), "C": (2048, D), "D": (6144, D), "mk": (D, D),
            "br0": (D_RNN, D), "br1": (D_RNN, D), "br2": (D_RNN, D), "out": (D, D)}
SHARD_SHAPES = {"win": (SHARD, D), "mk": (SHARD_ROWS, D), "br0": (D_RNN, SHARD_ROWS), "br1": (D_RNN, SHARD_ROWS),
                "br2": (D_RNN, SHARD_ROWS), "out": (SHARD_ROWS, D)}
SHARDS = tuple(SHARD_SHAPES)
HALF_AXIS = {"win": 1, "mk": 1, "br0": 0, "br1": 0, "br2": 0, "out": 1,
             "A": 1, "B": 1, "C": 1, "D": 1}


def _halved(shape, axis):
    return (shape[0] // 2, shape[1]) if axis == 0 else (shape[0], shape[1] // 2)


class Piece(NamedTuple):
    src: str
    dst: str
    rows: int
    sr0: int
    sc0: int
    dr0: int
    dc0: int
    ncols: int


def _pieces_of(jj):
    out = [Piece("win", g, n, r, 0, gr, 0, D) for r, n, g, gr in _shard_runs(jj)]
    out.append(Piece("mk", "mk", SHARD_ROWS, 0, 0, SHARD_ROWS * jj, 0, D))
    out += [Piece(f"br{b}", f"br{b}", D_RNN, 0, 0, 0, SHARD_ROWS * jj, SHARD_ROWS) for b in range(3)]
    out.append(Piece("out", "out", SHARD_ROWS, 0, 0, SHARD_ROWS * jj, 0, D))
    return out


def _half_rect(ref, p, side, which):
    r0, c0 = (p.sr0, p.sc0) if side == "src" else (p.dr0, p.dc0)
    if HALF_AXIS[p.src] == 1:
        return _rect(ref, r0, p.rows, c0 + which * (p.ncols // 2), p.ncols // 2)
    return _rect(ref, r0 + which * (p.rows // 2), p.rows // 2, c0, p.ncols)


def _rect_in_half(ref, p, side):
    r0, c0 = (p.sr0, p.sc0) if side == "src" else (p.dr0, p.dc0)
    if HALF_AXIS[p.src] == 1:
        return _rect(ref, r0, p.rows, 0, p.ncols // 2)
    return _rect(ref, 0, p.rows // 2, c0, p.ncols)


MAX_PIECES = max(len(_pieces_of(jj)) for jj in range(N_CHIPS))


def _rect(ref, r0, rows, c0, ncols):
    return ref.at[pl.ds(r0, rows), pl.ds(c0, ncols)]


def _position():
    x, y, c = lax.axis_index("x"), lax.axis_index("y"), lax.axis_index("c")
    return x, y, c, 2 * x + y


HBM = pl.BlockSpec(memory_space=pltpu.HBM)
SEM = pl.BlockSpec(memory_space=pltpu.SEMAPHORE)
EFFECT = pltpu.SideEffectType.DATAFLOW_SIDE_EFFECTING
N_SEM = MAX_PIECES * N_CHIPS
GATHER_STAGES = (("A",), ("B",), ("C",), ("mk",), ("br0", "br1", "br2"), ("D",), ("out",))


def _in_hbm(a):
    return pltpu.with_memory_space_constraint(a, pltpu.HBM)


def _stage_pieces(jj, stage):
    return [(i, p) for i, p in enumerate(_pieces_of(jj)) if p.dst in stage]


def _own_block_table(g):
    import numpy as np
    units = np.full((N_CHIPS, GATHERED[g][0] // HALF_TILE), -1, np.int64)
    for jj in range(N_CHIPS):
        for r, n, grp, gr in _shard_runs(jj):
            if grp == g:
                for k in range(n // HALF_TILE):
                    units[jj, gr // HALF_TILE + k] = r // HALF_TILE + k
    tbl = np.zeros((N_CHIPS, 2, GATHERED[g][0] // LANE), np.int32)
    for jj in range(N_CHIPS):
        for b in range(tbl.shape[2]):
            first, second = units[jj, 2 * b], units[jj, 2 * b + 1]
            if jj % 2 == 0:
                src = first if first >= 0 else second - 1
                if first >= 0 or second >= 0:
                    assert src % 2 == 0
                    tbl[jj, :, b] = src // 2
            else:
                if first >= 0:
                    assert first % 2 == 1
                    tbl[jj, 0, b] = first // 2
                if second >= 0:
                    assert second % 2 == 0
                    tbl[jj, 1, b] = second // 2
    return tbl


def _place_group(w_t, g, tables, odd_arr, after):
    nb = GATHERED[g][0] // LANE

    def body(t_ref, odd_ref, a_ref, b_ref, _, o_ref):
        odd = odd_ref[0] == 1
        o_ref[0:HALF_TILE, :] = jnp.where(odd, a_ref[HALF_TILE:LANE, :], a_ref[0:HALF_TILE, :]).astype(BF16)
        o_ref[HALF_TILE:LANE, :] = jnp.where(odd, b_ref[0:HALF_TILE, :], a_ref[HALF_TILE:LANE, :]).astype(BF16)

    return pl.pallas_call(
        body,
        name=f"place_{g}",
        grid_spec=pltpu.PrefetchScalarGridSpec(
            num_scalar_prefetch=2,
            grid=(nb,),
            in_specs=[pl.BlockSpec((LANE, D), lambda b, t, o: (t[0, b], 0)),
                      pl.BlockSpec((LANE, D), lambda b, t, o: (t[1, b], 0)), ANY],
            out_specs=pl.BlockSpec((LANE, D), lambda b, t, o: (b, 0)),
        ),
        out_shape=jax.ShapeDtypeStruct(GATHERED[g], BF16),
        compiler_params=_params(("parallel",)),
    )(tables, odd_arr, w_t, w_t, after)


def _place_shard(shard, name, after):
    rows, cols = shard.shape
    by_rows = HALF_AXIS[name] == 1

    def body(x_ref, _, o_ref):
        o_ref[...] = x_ref[...].astype(BF16)

    return pl.pallas_call(
        body,
        name=f"place_{name}",
        grid=(N_CHIPS,),
        in_specs=[pl.BlockSpec((rows, cols), lambda b: (0, 0)), ANY],
        out_specs=pl.BlockSpec((rows, cols), (lambda b: (b, 0)) if by_rows else (lambda b: (0, b))),
        out_shape=jax.ShapeDtypeStruct(GATHERED[name], BF16),
        compiler_params=_params(("parallel",)),
    )(shard, after)


def _gather_copy(arr, send_sems, recv_sems, c, jj, i, p, kk):
    rect = _half_rect(arr[p.dst], p, "dst", c)
    return pltpu.make_async_remote_copy(
        src_ref=rect, dst_ref=rect, send_sem=send_sems.at[i * N_CHIPS + kk],
        recv_sem=recv_sems.at[jj * MAX_PIECES + i], device_id=(kk // 2, kk % 2, c), device_id_type=MESH)


def _gather_start(arrays, after):
    stage = tuple(arrays)
    na = len(stage)

    def body(*refs):
        arr = dict(zip(stage, refs[:na]))
        send_sems, recv_sems = refs[na + 1], refs[na + 2]
        token = refs[-1]
        _, _, c, j = _position()
        for jj in range(N_CHIPS):
            @pl.when(j == jj)
            def _():
                for i, p in _stage_pieces(jj, stage):
                    for kk in range(N_CHIPS):
                        if kk != jj:
                            _gather_copy(arr, send_sems, recv_sems, c, jj, i, p, kk).start()
        token[...] = jnp.zeros_like(token)

    outs = pl.pallas_call(
        body,
        name=f"gather_start_{stage[0]}",
        in_specs=[HBM] * na + [ANY],
        out_specs=[SEM, SEM] + [HBM] * na + [pl.BlockSpec(memory_space=pltpu.VMEM)],
        out_shape=[pltpu.SemaphoreType.DMA((N_SEM,)), pltpu.SemaphoreType.DMA((N_SEM,))]
        + [pltpu.HBM(GATHERED[n], BF16) for n in stage] + [jax.ShapeDtypeStruct((8, LANE), F32)],
        input_output_aliases={k: 2 + k for k in range(na)},
        compiler_params=pltpu.CompilerParams(has_side_effects=EFFECT),
    )(*[_in_hbm(arrays[n]) for n in stage], after)
    return outs[0], outs[1], dict(zip(stage, outs[2:2 + na])), outs[-1]


def _gather_wait(send_sems, recv_sems, arrays, after):
    stage = tuple(arrays)
    na = len(stage)

    def body(*refs):
        arr = dict(zip(stage, refs[:na]))
        sems_s, sems_r = refs[na], refs[na + 1]
        _, _, c, j = _position()
        for jj in range(N_CHIPS):
            @pl.when(j != jj)
            def _():
                for i, p in _stage_pieces(jj, stage):
                    _gather_copy(arr, sems_s, sems_r, c, jj, i, p, jj).wait_recv()

            @pl.when(j == jj)
            def _():
                for i, p in _stage_pieces(jj, stage):
                    for kk in range(N_CHIPS):
                        if kk != jj:
                            _gather_copy(arr, sems_s, sems_r, c, jj, i, p, kk).wait_send()

    outs = pl.pallas_call(
        body,
        name=f"gather_wait_{stage[0]}",
        in_specs=[HBM] * na + [SEM, SEM, ANY],
        out_specs=[HBM] * na,
        out_shape=[pltpu.HBM(GATHERED[n], BF16) for n in stage],
        input_output_aliases={k: k for k in range(na)},
        compiler_params=pltpu.CompilerParams(has_side_effects=EFFECT),
    )(*[arrays[n] for n in stage], send_sems, recv_sems, after)
    return dict(zip(stage, outs))


def _gather_swap(arrays):
    stage = tuple(arrays)
    na = len(stage)

    def body(*refs):
        dst = dict(zip(stage, refs[na:2 * na]))
        send_sems, recv_sems = refs[2 * na:]
        x, y, c, j = _position()

        def fwd(jj, i, p, which):
            rect = _half_rect(dst[p.dst], p, "dst", which)
            return pltpu.make_async_remote_copy(
                src_ref=rect, dst_ref=rect, send_sem=send_sems.at[jj * MAX_PIECES + i],
                recv_sem=recv_sems.at[jj * MAX_PIECES + i], device_id=(x, y, 1 - c), device_id_type=MESH)

        for jj in range(N_CHIPS):
            @pl.when(j != jj)
            def _():
                for i, p in _stage_pieces(jj, stage):
                    fwd(jj, i, p, c).start()
        for jj in range(N_CHIPS):
            @pl.when(j != jj)
            def _():
                for i, p in _stage_pieces(jj, stage):
                    fwd(jj, i, p, 1 - c).wait_recv()
        for jj in range(N_CHIPS):
            @pl.when(j != jj)
            def _():
                for i, p in _stage_pieces(jj, stage):
                    fwd(jj, i, p, c).wait_send()

    outs = pl.pallas_call(
        body,
        name=f"gather_swap_{stage[0]}",
        in_specs=[ANY] * na,
        out_specs=[ANY] * na,
        out_shape=[jax.ShapeDtypeStruct(GATHERED[n], BF16) for n in stage],
        input_output_aliases={k: k for k in range(na)},
        scratch_shapes=[pltpu.SemaphoreType.DMA((N_SEM,)), pltpu.SemaphoreType.DMA((N_SEM,))],
        compiler_params=pltpu.CompilerParams(has_side_effects=True),
    )(*[arrays[n] for n in stage])
    return dict(zip(stage, outs))


def _pass_on_copy(arr, send_sems, recv_sems, x, y, c, jj, i, p, which):
    rect = _half_rect(arr[p.dst], p, "dst", which)
    return pltpu.make_async_remote_copy(
        src_ref=rect, dst_ref=rect, send_sem=send_sems.at[jj * MAX_PIECES + i],
        recv_sem=recv_sems.at[jj * MAX_PIECES + i], device_id=(x, y, 1 - c), device_id_type=MESH)


def _gather_pass_start(arrays, after):
    stage = tuple(arrays)
    na = len(stage)

    def body(*refs):
        arr = dict(zip(stage, refs[:na]))
        x, y, c, j = _position()
        for jj in range(N_CHIPS):
            @pl.when(j != jj)
            def _():
                for i, p in _stage_pieces(jj, stage):
                    _pass_on_copy(arr, refs[na + 1], refs[na + 2], x, y, c, jj, i, p, c).start()
        refs[-1][...] = jnp.zeros_like(refs[-1])

    outs = pl.pallas_call(
        body,
        name=f"gather_pass_start_{stage[0]}",
        in_specs=[HBM] * na + [ANY],
        out_specs=[SEM, SEM] + [HBM] * na + [pl.BlockSpec(memory_space=pltpu.VMEM)],
        out_shape=[pltpu.SemaphoreType.DMA((N_SEM,)), pltpu.SemaphoreType.DMA((N_SEM,))]
        + [pltpu.HBM(GATHERED[n], BF16) for n in stage] + [jax.ShapeDtypeStruct((8, LANE), F32)],
        input_output_aliases={k: 2 + k for k in range(na)},
        compiler_params=pltpu.CompilerParams(has_side_effects=EFFECT),
    )(*[arrays[n] for n in stage], after)
    return outs[0], outs[1], dict(zip(stage, outs[2:2 + na])), outs[-1]


def _gather_pass_wait(send_sems, recv_sems, arrays, after):
    stage = tuple(arrays)
    na = len(stage)

    def body(*refs):
        arr = dict(zip(stage, refs[:na]))
        x, y, c, j = _position()
        for jj in range(N_CHIPS):
            @pl.when(j != jj)
            def _():
                for i, p in _stage_pieces(jj, stage):
                    _pass_on_copy(arr, refs[na], refs[na + 1], x, y, c, jj, i, p, 1 - c).wait_recv()
                    _pass_on_copy(arr, refs[na], refs[na + 1], x, y, c, jj, i, p, c).wait_send()

    outs = pl.pallas_call(
        body,
        name=f"gather_pass_wait_{stage[0]}",
        in_specs=[HBM] * na + [SEM, SEM, ANY],
        out_specs=[HBM] * na,
        out_shape=[pltpu.HBM(GATHERED[n], BF16) for n in stage],
        input_output_aliases={k: k for k in range(na)},
        compiler_params=pltpu.CompilerParams(has_side_effects=EFFECT),
    )(*[arrays[n] for n in stage], send_sems, recv_sems, after)
    return dict(zip(stage, outs))


def _own_half(ref, shape, axis, which):
    if axis == 1:
        return ref.at[:, pl.ds(which * (shape[1] // 2), shape[1] // 2)]
    return ref.at[pl.ds(which * (shape[0] // 2), shape[0] // 2), :]


def _swap_copies(names, src, dst, send_sems, recv_sems):
    x, y, c, _ = _position()
    return [pltpu.make_async_remote_copy(
        src_ref=_own_half(src[n], GATHERED[n], HALF_AXIS[n], 1 - c), dst_ref=dst[n],
        send_sem=send_sems.at[k], recv_sem=recv_sems.at[k],
        device_id=(x, y, 1 - c), device_id_type=MESH) for k, n in enumerate(names)]


def _swap_start(grads, after):
    names = tuple(grads)
    n = len(names)

    def body(*refs):
        src, dst = dict(zip(names, refs[:n])), dict(zip(names, refs[n:2 * n]))
        for cp in _swap_copies(names, src, dst, refs[2 * n + 1], refs[2 * n + 2]):
            cp.start()
        refs[-1][...] = jnp.zeros_like(refs[-1])

    half_shape = lambda nm: _halved(GATHERED[nm], HALF_AXIS[nm])
    args = [_in_hbm(grads[nm]) for nm in names] + [_in_hbm(lax.empty(half_shape(nm), BF16)) for nm in names]
    if after is None:
        after = jnp.zeros((8, LANE), F32)
    outs = pl.pallas_call(
        body,
        name=f"swap_start_{names[0]}",
        in_specs=[HBM] * (2 * n) + [ANY],
        out_specs=[SEM, SEM] + [HBM] * (2 * n) + [pl.BlockSpec(memory_space=pltpu.VMEM)],
        out_shape=[pltpu.SemaphoreType.DMA((n,)), pltpu.SemaphoreType.DMA((n,))]
        + [pltpu.HBM(GATHERED[nm], BF16) for nm in names] + [pltpu.HBM(half_shape(nm), BF16) for nm in names]
        + [jax.ShapeDtypeStruct((8, LANE), F32)],
        input_output_aliases={k: 2 + k for k in range(2 * n)},
        compiler_params=pltpu.CompilerParams(has_side_effects=EFFECT),
    )(*args, after)
    return outs[0], outs[1], dict(zip(names, outs[2:2 + n])), dict(zip(names, outs[2 + n:2 + 2 * n])), outs[-1]


def _swap_wait(send_sems, recv_sems, grads, landing, after):
    names = tuple(grads)
    n = len(names)

    def body(*refs):
        src, dst = dict(zip(names, refs[:n])), dict(zip(names, refs[n:2 * n]))
        copies = _swap_copies(names, src, dst, refs[2 * n], refs[2 * n + 1])
        for cp in copies:
            cp.wait_recv()
        for cp in copies:
            cp.wait_send()

    half_shape = lambda nm: _halved(GATHERED[nm], HALF_AXIS[nm])
    outs = pl.pallas_call(
        body,
        name=f"swap_wait_{names[0]}",
        in_specs=[HBM] * (2 * n) + [SEM, SEM, ANY],
        out_specs=[HBM] * (2 * n),
        out_shape=[pltpu.HBM(GATHERED[nm], BF16) for nm in names] + [pltpu.HBM(half_shape(nm), BF16) for nm in names],
        input_output_aliases={k: k for k in range(2 * n)},
        compiler_params=pltpu.CompilerParams(has_side_effects=EFFECT),
    )(*[grads[nm] for nm in names], *[landing[nm] for nm in names], send_sems, recv_sems, after)
    return dict(zip(names, outs[:n])), dict(zip(names, outs[n:]))


ADD_ROWS = {"A": 1024, "B": 768, "C": 1024, "D": 1536, "mk": 1024, "br0": 512, "br1": 512, "br2": 512, "out": 1024}


def _add_half(full, recv, c_arr, name):
    rows, cols = recv.shape
    tr = ADD_ROWS[name]
    if HALF_AXIS[name] == 1:
        index = lambda i, c_ref: (i, c_ref[0])
    else:
        nb = rows // tr
        index = lambda i, c_ref: (nb * c_ref[0] + i, 0)

    def body(c_ref, a_ref, b_ref, o_ref):
        o_ref[...] = (a_ref[...].astype(F32) + b_ref[...].astype(F32)).astype(BF16)

    return pl.pallas_call(
        body,
        name=f"add_half_{name}",
        grid_spec=pltpu.PrefetchScalarGridSpec(
            num_scalar_prefetch=1,
            grid=(rows // tr,),
            in_specs=[pl.BlockSpec((tr, cols), index), pl.BlockSpec((tr, cols), lambda i, c_ref: (i, 0))],
            out_specs=pl.BlockSpec((tr, cols), lambda i, c_ref: (i, 0)),
        ),
        out_shape=jax.ShapeDtypeStruct((rows, cols), BF16),
        compiler_params=_params(("parallel",)),
    )(c_arr, full, recv)


SLOT_SHAPES = {n: _halved(SHARD_SHAPES[n], HALF_AXIS[n]) for n in SHARDS}


def _slot_shape(n):
    return (N_CHIPS,) + SLOT_SHAPES[n]


def _stage_shards(stage):
    pieces = [p for jj in range(N_CHIPS) for p in _pieces_of(jj)]
    return tuple(s for s in SHARDS if any(p.src == s and p.dst in stage for p in pieces))


def _scatter_copy(src, dst, send_sems, recv_sems, c, jj, kk, i, p):
    return pltpu.make_async_remote_copy(
        src_ref=_rect_in_half(src[p.dst], p, "dst"), dst_ref=_rect_in_half(dst[p.src].at[jj], p, "src"),
        send_sem=send_sems.at[kk * MAX_PIECES + i], recv_sem=recv_sems.at[jj * MAX_PIECES + i],
        device_id=(kk // 2, kk % 2, c), device_id_type=MESH)


def _scatter_start(halves, slots):
    stage, touched = tuple(halves), tuple(slots)
    nh, nt = len(stage), len(touched)

    def body(*refs):
        src = dict(zip(stage, refs[:nh]))
        dst = dict(zip(touched, refs[nh:nh + nt]))
        send_sems, recv_sems = refs[nh + nt], refs[nh + nt + 1]
        token = refs[-1]
        _, _, c, j = _position()
        for jj in range(N_CHIPS):
            @pl.when(j == jj)
            def _():
                for kk in range(N_CHIPS):
                    if kk != jj:
                        for i, p in _stage_pieces(kk, stage):
                            _scatter_copy(src, dst, send_sems, recv_sems, c, jj, kk, i, p).start()
        token[...] = jnp.zeros_like(token)

    outs = pl.pallas_call(
        body,
        name=f"scatter_start_{stage[0]}",
        in_specs=[HBM] * (nh + nt),
        out_specs=[SEM, SEM] + [HBM] * (nh + nt) + [pl.BlockSpec(memory_space=pltpu.VMEM)],
        out_shape=[pltpu.SemaphoreType.DMA((N_SEM,)), pltpu.SemaphoreType.DMA((N_SEM,))]
        + [pltpu.HBM(halves[n].shape, BF16) for n in stage] + [pltpu.HBM(_slot_shape(s), BF16) for s in touched]
        + [jax.ShapeDtypeStruct((8, LANE), F32)],
        input_output_aliases={k: 2 + k for k in range(nh + nt)},
        compiler_params=pltpu.CompilerParams(has_side_effects=EFFECT),
    )(*[_in_hbm(halves[n]) for n in stage], *[_in_hbm(slots[s]) for s in touched])
    return outs[0], outs[1], dict(zip(stage, outs[2:2 + nh])), dict(zip(touched, outs[2 + nh:2 + nh + nt])), outs[-1]


def _scatter_wait(send_sems, recv_sems, halves, slots, after):
    stage, touched = tuple(halves), tuple(slots)
    nh, nt = len(stage), len(touched)

    def body(*refs):
        src = dict(zip(stage, refs[:nh]))
        dst = dict(zip(touched, refs[nh:nh + nt]))
        sems_s, sems_r = refs[nh + nt], refs[nh + nt + 1]
        _, _, c, j = _position()
        for jj in range(N_CHIPS):
            @pl.when(j == jj)
            def _():
                for ss in range(N_CHIPS):
                    if ss != jj:
                        for i, p in _stage_pieces(jj, stage):
                            _scatter_copy(src, dst, sems_s, sems_r, c, ss, jj, i, p).wait_recv()
                for kk in range(N_CHIPS):
                    if kk != jj:
                        for i, p in _stage_pieces(kk, stage):
                            _scatter_copy(src, dst, sems_s, sems_r, c, jj, kk, i, p).wait_send()

    outs = pl.pallas_call(
        body,
        name=f"scatter_wait_{stage[0]}",
        in_specs=[HBM] * (nh + nt) + [SEM, SEM, ANY],
        out_specs=[HBM] * (nh + nt),
        out_shape=[pltpu.HBM(halves[n].shape, BF16) for n in stage] + [pltpu.HBM(_slot_shape(s), BF16) for s in touched],
        input_output_aliases={k: k for k in range(nh + nt)},
        compiler_params=pltpu.CompilerParams(has_side_effects=EFFECT),
    )(*[halves[n] for n in stage], *[slots[s] for s in touched], send_sems, recv_sems, after)
    return dict(zip(stage, outs[:nh])), dict(zip(touched, outs[nh:]))


SUM_ROWS = {"mk": 512, "br0": 512, "br1": 512, "br2": 512, "out": 512}


def _sum_in_chip_order(chip, own, s_ref):
    acc = None
    for k in range(N_CHIPS):
        term = jnp.where(chip == k, own, s_ref[k].astype(F32))
        acc = term if acc is None else acc + term
    return acc


def _sum_slots(slots, own_half, pos_arr, name):
    _, rows, cols = slots.shape
    tr = SUM_ROWS[name]
    nb = rows // tr
    if HALF_AXIS[name] == 1:
        own_index = lambda i, pos: (nb * pos[1] + i, 0)
        out_index = lambda i, pos: (i, pos[0])
    else:
        own_index = lambda i, pos: (i, pos[1])
        out_index = lambda i, pos: (nb * pos[0] + i, 0)

    def body(pos, s_ref, own_ref, o_ref):
        o_ref[...] = _sum_in_chip_order(pos[1], own_ref[...].astype(F32), s_ref)

    return pl.pallas_call(
        body,
        name=f"sum_slots_{name}",
        grid_spec=pltpu.PrefetchScalarGridSpec(
            num_scalar_prefetch=1,
            grid=(nb,),
            in_specs=[pl.BlockSpec((N_CHIPS, tr, cols), lambda i, pos: (0, i, 0)),
                      pl.BlockSpec((tr, cols), own_index)],
            out_specs=pl.BlockSpec((tr, cols), out_index),
        ),
        out_shape=jax.ShapeDtypeStruct(SHARD_SHAPES[name], F32),
        compiler_params=_params(("parallel",)),
    )(pos_arr, slots, own_half)


def _own_partial_tables():
    import numpy as np
    nb = SHARD // HALF_TILE
    grp, blk = np.zeros((N_CHIPS, nb), np.int32), np.zeros((N_CHIPS, nb), np.int32)
    for jj in range(N_CHIPS):
        for r, n, g, gr in _shard_runs(jj):
            for k in range(n // HALF_TILE):
                grp[jj, r // HALF_TILE + k] = GROUPS.index(g)
                blk[jj, r // HALF_TILE + k] = gr // HALF_TILE + k
    return grp, blk


def _sum_slots_win(slots, own_halves, pos_arr, grp_tbl, blk_tbl):
    nb = SHARD // HALF_TILE
    cols = D // 2

    def own_spec(gi):
        return pl.BlockSpec((HALF_TILE, cols), lambda b, pos, grp, blk: (jnp.where(grp[b] == gi, blk[b], 0), 0))

    def body(pos, grp, blk, s_ref, a_ref, b_ref, c_ref, d_ref, o_ref):
        g = grp[pl.program_id(0)]
        own = a_ref[...]
        for gi, ref in ((1, b_ref), (2, c_ref), (3, d_ref)):
            own = jnp.where(g == gi, ref[...], own)
        o_ref[...] = _sum_in_chip_order(pos[1], own.astype(F32), s_ref)

    return pl.pallas_call(
        body,
        name="sum_slots_win",
        grid_spec=pltpu.PrefetchScalarGridSpec(
            num_scalar_prefetch=3,
            grid=(nb,),
            in_specs=[pl.BlockSpec((N_CHIPS, HALF_TILE, cols), lambda b, pos, grp, blk: (0, b, 0))]
            + [own_spec(gi) for gi in range(len(GROUPS))],
            out_specs=pl.BlockSpec((HALF_TILE, cols), lambda b, pos, grp, blk: (b, pos[0])),
        ),
        out_shape=jax.ShapeDtypeStruct(SHARD_SHAPES["win"], F32),
        compiler_params=_params(("parallel",)),
    )(pos_arr, grp_tbl, blk_tbl, slots, *[own_halves[g] for g in GROUPS])


def _share_copy(buf, name, send_sems, recv_sems, k, which):
    x, y, c, _ = _position()
    half = _own_half(buf, SHARD_SHAPES[name], HALF_AXIS[name], which)
    return pltpu.make_async_remote_copy(src_ref=half, dst_ref=half, send_sem=send_sems.at[k], recv_sem=recv_sems.at[k],
                                        device_id=(x, y, 1 - c), device_id_type=MESH)


def _share_start(sums, after):
    names = tuple(sums)
    n = len(names)

    def body(*refs):
        _, _, c, _ = _position()
        for k, nm in enumerate(names):
            _share_copy(refs[k], nm, refs[n + 1], refs[n + 2], k, c).start()
        refs[-1][...] = jnp.zeros_like(refs[-1])

    outs = pl.pallas_call(
        body,
        name=f"share_start_{names[0]}",
        in_specs=[HBM] * n + [ANY],
        out_specs=[SEM, SEM] + [HBM] * n + [pl.BlockSpec(memory_space=pltpu.VMEM)],
        out_shape=[pltpu.SemaphoreType.DMA((n,)), pltpu.SemaphoreType.DMA((n,))]
        + [pltpu.HBM(SHARD_SHAPES[nm], F32) for nm in names] + [jax.ShapeDtypeStruct((8, LANE), F32)],
        input_output_aliases={k: 2 + k for k in range(n)},
        compiler_params=pltpu.CompilerParams(has_side_effects=EFFECT),
    )(*[_in_hbm(sums[nm]) for nm in names], after)
    return outs[0], outs[1], dict(zip(names, outs[2:2 + n])), outs[-1]


def _share_wait(send_sems, recv_sems, sums, after):
    names = tuple(sums)
    n = len(names)

    def body(*refs):
        _, _, c, _ = _position()
        for k, nm in enumerate(names):
            _share_copy(refs[k], nm, refs[n], refs[n + 1], k, 1 - c).wait_recv()
            _share_copy(refs[k], nm, refs[n], refs[n + 1], k, c).wait_send()

    outs = pl.pallas_call(
        body,
        name=f"share_wait_{names[0]}",
        in_specs=[HBM] * n + [SEM, SEM, ANY],
        out_specs=[HBM] * n,
        out_shape=[pltpu.HBM(SHARD_SHAPES[nm], F32) for nm in names],
        input_output_aliases={k: k for k in range(n)},
        compiler_params=pltpu.CompilerParams(has_side_effects=EFFECT),
    )(*[sums[nm] for nm in names], send_sems, recv_sems, after)
    return dict(zip(names, outs))


def _all_reduce_small(pack, name):
    rows = pack.shape[0]
    half = rows // 2

    def body(p_ref, o_ref, sib, land, sems):
        x, y, c, j = _position()
        sibling = (x, y, 1 - c)
        swap = pltpu.make_async_remote_copy(src_ref=p_ref, dst_ref=sib, send_sem=sems.at[0], recv_sem=sems.at[1],
                                            device_id=sibling, device_id_type=MESH)
        swap.start()
        swap.wait_recv()
        land[j] = p_ref[...] + sib[...]

        def mine(k, which):
            return land.at[k, pl.ds(which * half, half)]

        def ici(kk):
            return pltpu.make_async_remote_copy(
                src_ref=mine(j, c), dst_ref=mine(j, c), send_sem=sems.at[2 + kk], recv_sem=sems.at[6 + j],
                device_id=(kk // 2, kk % 2, c), device_id_type=MESH)

        def arrival(kk):
            return pltpu.make_async_remote_copy(
                src_ref=mine(kk, c), dst_ref=mine(kk, c), send_sem=sems.at[2 + kk], recv_sem=sems.at[6 + kk],
                device_id=(kk // 2, kk % 2, c), device_id_type=MESH)

        def passed_on(kk, which):
            return pltpu.make_async_remote_copy(
                src_ref=mine(kk, which), dst_ref=mine(kk, which), send_sem=sems.at[10 + kk],
                recv_sem=sems.at[14 + kk], device_id=sibling, device_id_type=MESH)

        for kk in range(N_CHIPS):
            @pl.when(j != kk)
            def _():
                ici(kk).start()
        for kk in range(N_CHIPS):
            @pl.when(j != kk)
            def _():
                arrival(kk).wait_recv()
                passed_on(kk, c).start()
        for kk in range(N_CHIPS):
            @pl.when(j != kk)
            def _():
                passed_on(kk, 1 - c).wait_recv()
        acc = land[0]
        for kk in range(1, N_CHIPS):
            acc = acc + land[kk]
        o_ref[...] = acc
        swap.wait_send()
        for kk in range(N_CHIPS):
            @pl.when(j != kk)
            def _():
                ici(kk).wait_send()
                passed_on(kk, c).wait_send()

    vmem = pl.BlockSpec(memory_space=pltpu.VMEM)
    return pl.pallas_call(
        body,
        name=name,
        in_specs=[vmem],
        out_specs=vmem,
        out_shape=jax.ShapeDtypeStruct((rows, LANE), F32),
        scratch_shapes=[pltpu.VMEM((rows, LANE), F32), pltpu.VMEM((N_CHIPS, rows, LANE), F32),
                        pltpu.SemaphoreType.DMA((18,))],
        compiler_params=pltpu.CompilerParams(has_side_effects=True, vmem_limit_bytes=VMEM_LIMIT),
    )(pack)


ADAM_ROWS = {"win": 224, "mk": 256, "br0": 512, "br1": 512, "br2": 512, "out": 256}


def _adamw(w, g, m, v, name, tr):
    rows, cols = w.shape
    tr = min(tr, rows)

    def body(w_ref, g_ref, m_ref, v_ref, go_ref, d_ref, nm_ref, nv_ref):
        gv = g_ref[...]
        go_ref[...] = gv
        nm = ADAM_B1 * m_ref[...] + (1.0 - ADAM_B1) * gv
        nv = ADAM_B2 * v_ref[...] + (1.0 - ADAM_B2) * (gv * gv)
        nm_ref[...] = nm
        nv_ref[...] = nv
        m_hat = nm / (1.0 - ADAM_B1 ** ADAM_STEP)
        v_hat = nv / (1.0 - ADAM_B2 ** ADAM_STEP)
        d_ref[...] = -ADAM_LR * (m_hat / (jnp.sqrt(v_hat) + ADAM_EPS) + ADAM_WD * w_ref[...])

    blk = pl.BlockSpec((tr, cols), lambda i: (i, 0))
    shape = jax.ShapeDtypeStruct((rows, cols), F32)
    return pl.pallas_call(
        body,
        name=f"adamw_{name}",
        grid=(rows // tr,),
        in_specs=[blk] * 4,
        out_specs=[blk] * 4,
        out_shape=[shape] * 4,
        compiler_params=_params(("parallel",)),
    )(w, g, m, v)


SMALL = (("pre_norm_g", (1, D)), ("post_norm_g", (1, D)), ("mem_norm_g", (1, D)), ("conv_w", (CONV_W, D_RNN)),
         ("conv_b", (1, D_RNN)), ("w_rg_a", (RNN_BLOCKS, LANE, LANE)), ("b_rg_a", (1, D_RNN)),
         ("w_rg_x", (RNN_BLOCKS, LANE, LANE)), ("b_rg_x", (1, D_RNN)), ("lru_lambda", (1, D_RNN)),
         ("swa_sinks", (1, SWA_HEADS)), ("rel_bias", (REL_BUCKETS, SWA_HEADS)))
PACK_ROWS = 2176


def _slot_len(shape):
    return -(-math.prod(shape) // LANE) * LANE


def _pack(values, last_row=None):
    parts = []
    for name, shape in SMALL:
        flat = values[name].reshape(-1).astype(F32)
        parts.append(jnp.pad(flat, (0, _slot_len(shape) - flat.shape[0])))
    flat = jnp.concatenate(parts)
    tail = jnp.zeros((LANE,), F32) if last_row is None else last_row
    return jnp.concatenate([jnp.pad(flat, (0, (PACK_ROWS - 1) * LANE - flat.shape[0])), tail]).reshape(PACK_ROWS, LANE)


def _unpack(pack):
    flat = pack.reshape(-1)
    out, off = {}, 0
    for name, shape in SMALL:
        out[name] = flat[off:off + math.prod(shape)].reshape(shape)
        off += _slot_len(shape)
    return out


TWIN_WEIGHTS = ("pre_norm_g", "post_norm_g", "mem_norm_g", "w_in", "conv_w", "conv_b", "w_rg_a", "b_rg_a", "w_rg_x",
                "b_rg_x", "lru_lambda", "swa_sinks", "rel_bias", "w_mem_kv", "w_br_rg", "w_br_swa", "w_br_mem", "w_out")
BIG = {"w_in": "win", "w_mem_kv": "mk", "w_br_rg": "br0", "w_br_swa": "br1", "w_br_mem": "br2", "w_out": "out"}


def kernel(x, mem, pre_norm_g, post_norm_g, mem_norm_g, w_in, conv_w, conv_b, w_rg_a, b_rg_a, w_rg_x, b_rg_x, lru_lambda, swa_sinks, rel_bias, w_mem_kv, w_br_rg, w_br_swa, w_br_mem, w_out, loss_target, m_pre_norm_g, m_post_norm_g, m_mem_norm_g, m_w_in, m_conv_w, m_conv_b, m_w_rg_a, m_b_rg_a, m_w_rg_x, m_b_rg_x, m_lru_lambda, m_swa_sinks, m_rel_bias, m_w_mem_kv, m_w_br_rg, m_w_br_swa, m_w_br_mem, m_w_out, v_pre_norm_g, v_post_norm_g, v_mem_norm_g, v_w_in, v_conv_w, v_conv_b, v_w_rg_a, v_b_rg_a, v_w_rg_x, v_b_rg_x, v_lru_lambda, v_swa_sinks, v_rel_bias, v_w_mem_kv, v_w_br_rg, v_w_br_swa, v_w_br_mem, v_w_out):
    args = dict(locals())
    out_shapes = {n: args[n].shape for n in TWIN_WEIGHTS}
    w = {n: (args[n] if n == "rel_bias" else args[n][0]) for n in TWIN_WEIGHTS}
    m = {n: (args["m_" + n] if n == "rel_bias" else args["m_" + n][0]) for n in TWIN_WEIGHTS}
    v = {n: (args["v_" + n] if n == "rel_bias" else args["v_" + n][0]) for n in TWIN_WEIGHTS}
    for d in (w, m, v):
        for n, shape in SMALL:
            if n != "conv_w":
                d[n] = d[n].reshape(shape)

    xi, yi, ci = lax.axis_index("x"), lax.axis_index("y"), lax.axis_index("c")
    chip = 2 * xi + yi
    c_arr = ci.astype(jnp.int32).reshape(1)
    zero = jnp.zeros((), jnp.int32)
    cw0 = (chip * (D_RNN // N_CHIPS)).astype(jnp.int32)

    placed = lax.dynamic_update_slice(jnp.zeros((CONV_W, D_RNN), F32), w["conv_w"], (zero, cw0))
    placed = jnp.where(ci == 0, placed, 0.0).reshape(CONV_W * D_RNN // LANE, LANE)
    conv_w_full = _all_reduce_small(placed, "gather_conv_w").reshape(CONV_W, D_RNN)

    for d in (w, m, v):
        d["w_in"] = d["w_in"].T
    chip_row = lambda tbl: lax.dynamic_slice(jnp.asarray(tbl), (chip.astype(jnp.int32), zero), (1, tbl.shape[1]))[0]
    chip_tables = lambda tbl: lax.dynamic_slice(jnp.asarray(tbl), (chip.astype(jnp.int32), zero, zero),
                                                (1,) + tbl.shape[1:])[0]
    odd_arr = yi.astype(jnp.int32).reshape(1)
    big_of = {s: n for n, s in BIG.items()}
    ag, token = {}, conv_w_full
    for stage in GATHER_STAGES:
        behind = c_arr if stage == GATHER_STAGES[0] else token
        placed = {n: (_place_group(w["w_in"], n, chip_tables(_own_block_table(n)), odd_arr, behind) if n in GROUPS
                      else _place_shard(w[big_of[n]], n, behind)) for n in stage}
        send, recv, in_flight, token = _gather_start(placed, token)
        ag[stage] = (send, recv, in_flight)
    h = _rms_fwd(x[0], w["pre_norm_g"], "rms_pre", token)

    def conv_w_in_place(d):
        return dict(d, conv_w=lax.dynamic_update_slice(jnp.zeros((CONV_W, D_RNN), F32), d["conv_w"], (zero, cw0)))

    small_packs = [_pack(conv_w_in_place(d)) for d in (w, m, v)]

    passing = {}

    def prefetch(names, after):
        send, recv, in_flight = ag[names]
        *passing[names], token = _gather_pass_start(_gather_wait(send, recv, in_flight, after), after)
        return token

    def fetch(names, after):
        if names in passing:
            ready = _gather_pass_wait(*passing.pop(names), after)
        else:
            send, recv, in_flight = ag[names]
            ready = _gather_swap(_gather_wait(send, recv, in_flight, after))
        return tuple(ready[n] for n in names)

    rs = {"slots": {}, "halves": {}, "pending": [], "swap": None}

    def emit(grads, after=None):
        assert rs["swap"] is None
        *rs["swap"], token = _swap_start(grads, after)
        return token

    def advance(after):
        grads, received = _swap_wait(*rs["swap"], after)
        rs["swap"] = None
        halves = {n: _add_half(grads[n], received[n], c_arr, n) for n in grads}
        landing = {s: rs["slots"][s] if s in rs["slots"] else lax.empty(_slot_shape(s), BF16)
                   for s in _stage_shards(tuple(grads))}
        send, recv, halves, landing, token = _scatter_start(halves, landing)
        rs["slots"].update(landing)
        rs["pending"].append((send, recv, halves, tuple(landing)))
        return token

    sp = {n: w[n] for n, _ in SMALL}
    sp["conv_w"] = conv_w_full
    sq, grad_x, d_small = _local_step(x[0], h, mem[0], loss_target[0], sp, small_packs, fetch, prefetch, emit, advance)
    small_total = _all_reduce_small(_pack(d_small, sq[0]), "all_reduce_small")
    loss = small_total[PACK_ROWS - 1, 0] * (0.5 / D)

    for send, recv, halves, touched in rs["pending"]:
        halves, landed = _scatter_wait(send, recv, halves, {s: rs["slots"][s] for s in touched}, small_total)
        rs["slots"].update(landed)
        rs["halves"].update(halves)
    pos_arr = jnp.stack([ci, chip]).astype(jnp.int32)
    grp_tbl, blk_tbl = (chip_row(t) for t in _own_partial_tables())
    rest = {s: _sum_slots(rs["slots"][s], rs["halves"][s], pos_arr, s) for s in SHARDS if s != "win"}
    *rest_share, tok = _share_start(rest, small_total)
    win_sum = _sum_slots_win(rs["slots"]["win"], rs["halves"], pos_arr, grp_tbl, blk_tbl)
    *win_share, tok = _share_start({"win": win_sum}, tok)
    sums = _share_wait(*rest_share, tok)

    grad, delta, new_m, new_v = {}, {}, {}, {}
    for n, s in BIG.items():
        if n == "w_in":
            continue
        grad[n], delta[n], new_m[n], new_v[n] = _adamw(w[n], sums[s], m[n], v[n], s, ADAM_ROWS[s])
    g_win = _share_wait(*win_share, delta["w_out"])["win"]
    n = "w_in"
    grad[n], delta[n], new_m[n], new_v[n] = _adamw(w[n], g_win, m[n], v[n], "win", ADAM_ROWS["win"])
    for group in (grad, delta, new_m, new_v):
        group["w_in"] = group["w_in"].T
    _, d_, m_, v_ = _adamw(small_packs[0], small_total, small_packs[1], small_packs[2], "small", PACK_ROWS)
    for group, pack in ((grad, small_total), (delta, d_), (new_m, m_), (new_v, v_)):
        group.update(_unpack(pack))
    for group in (grad, delta, new_m, new_v):
        group["conv_w"] = lax.dynamic_slice(group["conv_w"], (zero, cw0), (CONV_W, D_RNN // N_CHIPS))

    outs = [loss, grad_x.reshape(1, S, D)]
    for group in (grad, delta, new_m, new_v):
        outs += [group[n].reshape(out_shapes[n]) for n in TWIN_WEIGHTS]
    return tuple(outs)
```

```python
import math
from typing import NamedTuple

import jax
import jax.numpy as jnp
from jax import lax
from jax.experimental import pallas as pl
from jax.experimental.pallas import tpu as pltpu

F32 = jnp.float32
BF16 = jnp.bfloat16
MESH = pl.DeviceIdType.MESH

S = 2048
D = 2048
MEM = 256
D_RNN = 1024
RNN_BLOCKS = 8
CONV_W = 4
LRU_C = 8.0
SWA_HEADS = 16
SWA_HD = 64
WINDOW = 128
MEM_HEADS = 4
MEM_HD = 256
REL_BUCKETS = 32
REL_MAX_DIST = 128
EPS = 1e-6
NEG_INF = -1e30
LANE = 128
SHARD = 3136
HALF_TILE = 64
N_CHIPS = 4
VMEM_LIMIT = 56 * 1024 * 1024

ADAM_LR = 0.001
ADAM_B1 = 0.9
ADAM_B2 = 0.999
ADAM_EPS = 1e-08
ADAM_WD = 0.01
ADAM_STEP = 10

GROUP_TILES = {"A": 16, "B": 18, "C": 16, "D": 48}
GROUPS = ("A", "B", "C", "D")


def _params(sem=None):
    return pltpu.CompilerParams(dimension_semantics=sem, vmem_limit_bytes=VMEM_LIMIT)


def _sigmoid(v):
    return jax.nn.sigmoid(v)


def _tile_home(t):
    if t < 16:
        return "A", t
    if t < 24:
        return "B", t - 16
    if t < 26:
        return "B", t - 24 + 16
    if t < 34:
        return "B", t - 26 + 8
    if t < 50:
        return "C", t - 34
    return "D", t - 50


def _shard_runs(j):
    runs = []
    per_shard = SHARD // HALF_TILE
    for q in range(per_shard * j, per_shard * (j + 1)):
        g, gt = _tile_home(q // 2)
        row = gt * LANE + (q % 2) * HALF_TILE
        if runs and runs[-1][2] == g and runs[-1][3] + runs[-1][1] == row:
            runs[-1][1] += HALF_TILE
        else:
            runs.append([(q - per_shard * j) * HALF_TILE, HALF_TILE, g, row])
    return [tuple(r) for r in runs]


_DIMS = {
    "nn": (((1,), (0,)), ((), ())),
    "nt": (((1,), (1,)), ((), ())),
    "tn": (((0,), (0,)), ((), ())),
}


def _mm(a, b, mode, out_dtype, tm, tn, tk, name, acc=None, after=None):
    if mode == "nn":
        (m, k), n = a.shape, b.shape[1]
    elif mode == "nt":
        (m, k), n = a.shape, b.shape[0]
    else:
        (k, m), n = a.shape, b.shape[1]
    tm, tn, tk = min(tm, m), min(tn, n), min(tk, k)
    assert m % tm == 0 and n % tn == 0 and k % tk == 0, (name, m, n, k)
    nk = k // tk
    has_acc = acc is not None

    def body(*refs):
        a_ref, b_ref = refs[0], refs[1]
        o_ref = refs[3] if has_acc else refs[2]
        p = lax.dot_general(a_ref[...], b_ref[...], _DIMS[mode], preferred_element_type=F32)

        def finish(v):
            if has_acc:
                v = v + refs[2][...]
            o_ref[...] = v.astype(out_dtype)

        if nk == 1:
            finish(p)
        else:
            s_ref = refs[-1]
            kk = pl.program_id(2)

            @pl.when(kk == 0)
            def _():
                s_ref[...] = p

            @pl.when(kk > 0)
            def _():
                s_ref[...] += p

            @pl.when(kk == nk - 1)
            def _():
                finish(s_ref[...])

    if mode == "nn":
        a_spec = pl.BlockSpec((tm, tk), lambda i, j, kk: (i, kk))
        b_spec = pl.BlockSpec((tk, tn), lambda i, j, kk: (kk, j))
    elif mode == "nt":
        a_spec = pl.BlockSpec((tm, tk), lambda i, j, kk: (i, kk))
        b_spec = pl.BlockSpec((tn, tk), lambda i, j, kk: (j, kk))
    else:
        a_spec = pl.BlockSpec((tk, tm), lambda i, j, kk: (kk, i))
        b_spec = pl.BlockSpec((tk, tn), lambda i, j, kk: (kk, j))
    o_spec = pl.BlockSpec((tm, tn), lambda i, j, kk: (i, j))
    in_specs = [a_spec, b_spec] + ([o_spec] if has_acc else [])
    args = (a, b) + ((acc,) if has_acc else ())
    if after is not None:
        in_specs.append(pl.BlockSpec(memory_space=pl.ANY))
        args += (after,)
    n_in = len(args)
    kernel_body = body

    def body(*refs):
        kernel_body(*(refs[:n_in - (after is not None)] + refs[n_in:]))

    return pl.pallas_call(
        body,
        name=name,
        grid=(m // tm, n // tn, nk),
        in_specs=in_specs,
        out_specs=o_spec,
        out_shape=jax.ShapeDtypeStruct((m, n), out_dtype),
        scratch_shapes=[pltpu.VMEM((tm, tn), F32)] if nk > 1 else [],
        compiler_params=_params(("parallel", "parallel", "arbitrary")),
    )(*args)


def _rms_fwd(x, g, name, after, ts=256):
    r, d = x.shape

    def body(x_ref, g_ref, _, o_ref):
        xv = x_ref[...]
        inv = lax.rsqrt(jnp.mean(xv * xv, axis=-1, keepdims=True) + EPS)
        o_ref[...] = (xv * inv * g_ref[...]).astype(BF16)

    return pl.pallas_call(
        body,
        name=name,
        grid=(r // ts,),
        in_specs=[pl.BlockSpec((ts, d), lambda i: (i, 0)), pl.BlockSpec((1, d), lambda i: (0, 0)),
                  pl.BlockSpec(memory_space=pl.ANY)],
        out_specs=pl.BlockSpec((ts, d), lambda i: (i, 0)),
        out_shape=jax.ShapeDtypeStruct((r, d), BF16),
        compiler_params=_params(("parallel",)),
    )(x, g, after)


def _post_loss(out, x, tgt, g_post, ts=256):
    n = S // ts

    def body(o_ref, x_ref, t_ref, g_ref, sq_ref, dy_ref, do_ref, dg_ref):
        i = pl.program_id(0)

        @pl.when(i == 0)
        def _():
            sq_ref[...] = jnp.zeros_like(sq_ref)
            dg_ref[...] = jnp.zeros_like(dg_ref)

        ov = o_ref[...]
        g = g_ref[...]
        inv = lax.rsqrt(jnp.mean(ov * ov, axis=-1, keepdims=True) + EPS)
        on = ov * inv
        err = x_ref[...] + on * g - t_ref[...]
        sq_ref[...] += jnp.sum(err * err)
        dy = err * (1.0 / D)
        dy_ref[...] = dy.astype(BF16)
        dg_ref[...] += jnp.sum(dy * on, axis=0, keepdims=True)
        don = dy * g
        do_ref[...] = (inv * (don - on * jnp.mean(don * on, axis=-1, keepdims=True))).astype(BF16)

    row = pl.BlockSpec((ts, D), lambda i: (i, 0))
    vec = pl.BlockSpec((1, D), lambda i: (0, 0))
    return pl.pallas_call(
        body,
        name="post_loss",
        grid=(n,),
        in_specs=[row, row, row, vec],
        out_specs=[pl.BlockSpec((8, LANE), lambda i: (0, 0)), row, row, vec],
        out_shape=[
            jax.ShapeDtypeStruct((8, LANE), F32),
            jax.ShapeDtypeStruct((S, D), BF16),
            jax.ShapeDtypeStruct((S, D), BF16),
            jax.ShapeDtypeStruct((1, D), F32),
        ],
        compiler_params=_params(("arbitrary",)),
    )(out, x, tgt, g_post)


def _pre_bwd(dh, x, dy, g_pre, ts=256):
    n = S // ts

    def body(dh_ref, x_ref, dy_ref, g_ref, gx_ref, dg_ref):
        i = pl.program_id(0)

        @pl.when(i == 0)
        def _():
            dg_ref[...] = jnp.zeros_like(dg_ref)

        xv = x_ref[...]
        dhv = dh_ref[...]
        inv = lax.rsqrt(jnp.mean(xv * xv, axis=-1, keepdims=True) + EPS)
        xn = xv * inv
        dg_ref[...] += jnp.sum(dhv * xn, axis=0, keepdims=True)
        dxn = dhv * g_ref[...]
        gx_ref[...] = dy_ref[...].astype(F32) + inv * (dxn - xn * jnp.mean(dxn * xn, axis=-1, keepdims=True))

    row = pl.BlockSpec((ts, D), lambda i: (i, 0))
    vec = pl.BlockSpec((1, D), lambda i: (0, 0))
    return pl.pallas_call(
        body,
        name="pre_bwd",
        grid=(n,),
        in_specs=[row, row, row, vec],
        out_specs=[row, vec],
        out_shape=[jax.ShapeDtypeStruct((S, D), F32), jax.ShapeDtypeStruct((1, D), F32)],
        compiler_params=_params(("arbitrary",)),
    )(dh, x, dy, g_pre)


def _memnorm_bwd(dmemn, mem):
    def body(d_ref, m_ref, dg_ref):
        mv = m_ref[...]
        inv = lax.rsqrt(jnp.mean(mv * mv, axis=-1, keepdims=True) + EPS)
        dg_ref[...] = jnp.sum(d_ref[...] * mv * inv, axis=0, keepdims=True)

    return pl.pallas_call(
        body,
        name="memnorm_bwd",
        out_shape=jax.ShapeDtypeStruct((1, D), F32),
        compiler_params=_params(),
    )(dmemn, mem)


T_RNN = 256


def _neg_expm1(z):
    poly = -z * (1.0 + z * (0.5 + z * (1.0 / 6 + z * (1.0 / 24 + z * (1.0 / 120 + z * (1.0 / 720))))))
    return jnp.where(z > -0.1, poly, 1.0 - jnp.exp(z))


def _softplus_neg(lam):
    return jnp.maximum(-lam, 0.0) + jnp.log1p(jnp.exp(-jnp.abs(lam)))


def _rnn_gates(conv, wa_ref, ba, wx_ref, bx, lam, first_row):
    cbf = conv.astype(BF16)
    ga, gx = [], []
    for n in range(RNN_BLOCKS):
        c_n = cbf[:, n * LANE:(n + 1) * LANE]
        ga.append(jnp.dot(c_n, wa_ref[n], preferred_element_type=F32))
        gx.append(jnp.dot(c_n, wx_ref[n], preferred_element_type=F32))
    gate_r = _sigmoid(jnp.concatenate(ga, axis=1) + ba)
    gate_i = _sigmoid(jnp.concatenate(gx, axis=1) + bx)
    sp = _softplus_neg(lam)
    log_a = -LRU_C * gate_r * sp
    a = jnp.exp(log_a)
    mult_raw = jnp.sqrt(_neg_expm1(2.0 * log_a))
    mult = jnp.where(first_row, 1.0, mult_raw)
    return cbf, gate_r, gate_i, sp, a, mult_raw, mult


def _rglru_fwd(p_a, conv_w, conv_b, wa, ba, wx, bx, lam):
    t = T_RNN
    n = S // t

    def body(xr_ref, g_ref, cw_ref, cb_ref, wa_ref, ba_ref, wx_ref, bx_ref, lam_ref,
             y_ref, h_ref, xp_s, hcar, a_s, b_s):
        i = pl.program_id(0)

        @pl.when(i == 0)
        def _():
            xp_s[0:8, :] = jnp.zeros((8, D_RNN), F32)
            hcar[...] = jnp.zeros_like(hcar)

        @pl.when(i > 0)
        def _():
            xp_s[0:8, :] = xp_s[t:t + 8, :]

        xp_s[8:8 + t, :] = xr_ref[...]
        conv = cb_ref[...]
        for k in range(CONV_W):
            conv = conv + cw_ref[k:k + 1, :] * xp_s[8 - k:8 - k + t, :]
        rows = i * t + lax.broadcasted_iota(jnp.int32, (t, 1), 0)
        _, _, gate_i, _, a, _, mult = _rnn_gates(
            conv, wa_ref, ba_ref[...], wx_ref, bx_ref[...], lam_ref[...], rows == 0)
        a_s[...] = a
        b_s[...] = mult * gate_i * conv

        def step(tt, h):
            h = a_s[pl.ds(tt, 1), :] * h + b_s[pl.ds(tt, 1), :]
            h_ref[pl.ds(tt, 1), :] = h
            return h

        hcar[...] = lax.fori_loop(0, t, step, hcar[...], unroll=8)
        g = g_ref[...]
        y_ref[...] = (h_ref[...] * (g * _sigmoid(g))).astype(BF16)

    blk = lambda c: pl.BlockSpec((t, D_RNN), lambda i: (i, c))
    full = lambda shape: pl.BlockSpec(shape, lambda i: (0,) * len(shape))
    return pl.pallas_call(
        body,
        name="rglru_fwd",
        grid=(n,),
        in_specs=[blk(0), blk(1), full((CONV_W, D_RNN)), full((1, D_RNN)),
                  full((RNN_BLOCKS, LANE, LANE)), full((1, D_RNN)),
                  full((RNN_BLOCKS, LANE, LANE)), full((1, D_RNN)), full((1, D_RNN))],
        out_specs=[blk(0), blk(0)],
        out_shape=[jax.ShapeDtypeStruct((S, D_RNN), BF16), jax.ShapeDtypeStruct((S, D_RNN), F32)],
        scratch_shapes=[pltpu.VMEM((t + 8, D_RNN), F32), pltpu.VMEM((1, D_RNN), F32),
                        pltpu.VMEM((t, D_RNN), F32), pltpu.VMEM((t, D_RNN), F32)],
        compiler_params=_params(("arbitrary",)),
    )(p_a, p_a, conv_w, conv_b, wa, ba, wx, bx, lam)


def _rglru_bwd(dy, p_a, hseq, conv_w, conv_b, wa, ba, wx, bx, lam):
    t = T_RNN
    n = S // t
    rb = t // 8

    def body(dy_ref, xr_ref, g_ref, h_ref, xrp_ref, hp_ref, cw_ref, cb_ref, wa_ref, ba_ref, wx_ref, bx_ref, lam_ref,
             dp_ref, dcw_ref, dcb_ref, dwa_ref, dba_ref, dwx_ref, dbx_ref, dlam_ref,
             xp_s, hp_s, dxp_s, lamcar, a_s, dh_s, lam_s):
        i = pl.program_id(0)
        r = n - 1 - i

        @pl.when(i == 0)
        def _():
            for ref in (dcw_ref, dcb_ref, dwa_ref, dba_ref, dwx_ref, dbx_ref, dlam_ref, lamcar):
                ref[...] = jnp.zeros_like(ref)
            dxp_s[t:t + 8, :] = jnp.zeros((8, D_RNN), F32)

        @pl.when(i > 0)
        def _():
            dxp_s[t:t + 8, :] = dxp_s[0:8, :]

        has_prev = r > 0
        xp_s[0:8, :] = jnp.where(has_prev, xrp_ref[...], 0.0)
        xp_s[8:8 + t, :] = xr_ref[...]
        hp_s[0:8, :] = jnp.where(has_prev, hp_ref[...], 0.0)
        hp_s[8:8 + t, :] = h_ref[...]
        xs = [xp_s[8 - k:8 - k + t, :] for k in range(CONV_W)]
        conv = cb_ref[...]
        for k in range(CONV_W):
            conv = conv + cw_ref[k:k + 1, :] * xs[k]
        rows = r * t + lax.broadcasted_iota(jnp.int32, (t, 1), 0)
        first = rows == 0
        lam_p = lam_ref[...]
        cbf, gate_r, gate_i, sp, a, mult_raw, mult = _rnn_gates(
            conv, wa_ref, ba_ref[...], wx_ref, bx_ref[...], lam_p, first)

        g = g_ref[...]
        sg = _sigmoid(g)
        dyv = dy_ref[...]
        a_s[...] = a
        dh_s[...] = dyv * (g * sg)
        dg = dyv * h_ref[...] * (sg * (1.0 + g * (1.0 - sg)))

        def step(jj, car):
            tt = t - 1 - jj
            lm = dh_s[pl.ds(tt, 1), :] + car
            lam_s[pl.ds(tt, 1), :] = lm
            return a_s[pl.ds(tt, 1), :] * lm

        lamcar[...] = lax.fori_loop(0, t, step, lamcar[...], unroll=8)
        db = lam_s[...]
        da = db * hp_s[7:7 + t, :]
        dmult = db * gate_i * conv
        dgate_i = db * mult * conv
        dconv = db * mult * gate_i
        dlog_a = da * a + jnp.where(first, 0.0, dmult * (-(a * a) / mult_raw))
        dgate_r = dlog_a * (-LRU_C * sp)
        dsp = jnp.sum(dlog_a * (-LRU_C * gate_r), axis=0, keepdims=True)
        dlam_ref[...] += dsp * (-_sigmoid(-lam_p))
        dga = dgate_r * gate_r * (1.0 - gate_r)
        dgx = dgate_i * gate_i * (1.0 - gate_i)
        dba_ref[...] += jnp.sum(dga, axis=0, keepdims=True)
        dbx_ref[...] += jnp.sum(dgx, axis=0, keepdims=True)
        dga16, dgx16 = dga.astype(BF16), dgx.astype(BF16)
        back = []
        for nb in range(RNN_BLOCKS):
            sl = slice(nb * LANE, (nb + 1) * LANE)
            dwa_ref[nb] += lax.dot_general(cbf[:, sl], dga16[:, sl], _DIMS["tn"], preferred_element_type=F32)
            dwx_ref[nb] += lax.dot_general(cbf[:, sl], dgx16[:, sl], _DIMS["tn"], preferred_element_type=F32)
            back.append(lax.dot_general(dga16[:, sl], wa_ref[nb], _DIMS["nt"], preferred_element_type=F32)
                        + lax.dot_general(dgx16[:, sl], wx_ref[nb], _DIMS["nt"], preferred_element_type=F32))
        dconv = dconv + jnp.concatenate(back, axis=1)
        dcb_ref[...] += jnp.sum(dconv, axis=0, keepdims=True)
        for k in range(CONV_W):
            dcw_ref[k:k + 1, :] += jnp.sum(dconv * xs[k], axis=0, keepdims=True)
        dxp_s[0:t, :] = dconv
        dxr = cw_ref[0:1, :] * dconv
        for k in range(1, CONV_W):
            dxr = dxr + cw_ref[k:k + 1, :] * dxp_s[k:k + t, :]
        dp_ref[:, 0:D_RNN] = dxr.astype(BF16)
        dp_ref[:, D_RNN:2 * D_RNN] = dg.astype(BF16)

    blk = lambda c: pl.BlockSpec((t, D_RNN), lambda i: (n - 1 - i, c))
    prev8 = pl.BlockSpec((8, D_RNN), lambda i: (jnp.maximum((n - 1 - i) * rb - 1, 0), 0))
    full = lambda shape: pl.BlockSpec(shape, lambda i: (0,) * len(shape))
    vec = full((1, D_RNN))
    mat = full((RNN_BLOCKS, LANE, LANE))
    return pl.pallas_call(
        body,
        name="rglru_bwd",
        grid=(n,),
        in_specs=[blk(0), blk(0), blk(1), blk(0), prev8, prev8,
                  full((CONV_W, D_RNN)), vec, mat, vec, mat, vec, vec],
        out_specs=[pl.BlockSpec((t, 2 * D_RNN), lambda i: (n - 1 - i, 0)),
                   full((CONV_W, D_RNN)), vec, mat, vec, mat, vec, vec],
        out_shape=[jax.ShapeDtypeStruct((S, 2 * D_RNN), BF16),
                   jax.ShapeDtypeStruct((CONV_W, D_RNN), F32), jax.ShapeDtypeStruct((1, D_RNN), F32),
                   jax.ShapeDtypeStruct((RNN_BLOCKS, LANE, LANE), F32), jax.ShapeDtypeStruct((1, D_RNN), F32),
                   jax.ShapeDtypeStruct((RNN_BLOCKS, LANE, LANE), F32), jax.ShapeDtypeStruct((1, D_RNN), F32),
                   jax.ShapeDtypeStruct((1, D_RNN), F32)],
        scratch_shapes=[pltpu.VMEM((t + 8, D_RNN), F32), pltpu.VMEM((t + 8, D_RNN), F32),
                        pltpu.VMEM((t + 8, D_RNN), F32), pltpu.VMEM((1, D_RNN), F32),
                        pltpu.VMEM((t, D_RNN), F32), pltpu.VMEM((t, D_RNN), F32), pltpu.VMEM((t, D_RNN), F32)],
        compiler_params=_params(("arbitrary",)),
    )(dy, p_a, p_a, hseq, p_a, hseq, conv_w, conv_b, wa, ba, wx, bx, lam)


QB = WINDOW
KB2 = 2 * WINDOW
N_QB = S // QB
N_PAIR = SWA_HEADS // 2


def _swa_keys(kvc_ref, kvp_ref):
    kk = jnp.concatenate([kvp_ref[:, 0:LANE], kvc_ref[:, 0:LANE]], axis=0)
    vv = jnp.concatenate([kvp_ref[:, LANE:2 * LANE], kvc_ref[:, LANE:2 * LANE]], axis=0)
    lo = lax.broadcasted_iota(jnp.int32, (1, LANE), 1) < SWA_HD
    kk_sw, vv_sw = pltpu.roll(kk, SWA_HD, 1), pltpu.roll(vv, SWA_HD, 1)
    kd = [jnp.where(lo, kk, kk_sw).astype(BF16), jnp.where(lo, kk_sw, kk).astype(BF16)]
    vd = [jnp.where(lo, vv, vv_sw).astype(BF16), jnp.where(lo, vv_sw, vv).astype(BF16)]
    return lo, kd, vd


GRP = SWA_HEADS // 2
STACK = GRP
GQ = STACK * QB


def _swa_valid(n, rows):
    qi = lax.broadcasted_iota(jnp.int32, (rows, KB2), 0) % QB
    kj = lax.broadcasted_iota(jnp.int32, (rows, KB2), 1)
    dist = qi + WINDOW - kj
    return (dist >= 0) & (dist < WINDOW) & ((n > 0) | (kj >= WINDOW))


def _swa_stack(tile_of, lo, h0, masked):
    parts = []
    for h in range(h0, h0 + STACK):
        t = tile_of(h // 2)
        if masked:
            t = jnp.where(lo if h % 2 == 0 else jnp.logical_not(lo), t, 0.0)
        parts.append(t)
    return jnp.concatenate(parts, axis=0)


def _swa_unstack(stacked, lo, pair):
    return jnp.where(lo, stacked[2 * pair * QB:(2 * pair + 1) * QB], stacked[(2 * pair + 1) * QB:(2 * pair + 2) * QB])


def _swa_softmax(lg, sink, valid):
    lg = jnp.where(valid, lg, NEG_INF)
    m = jnp.maximum(jnp.max(lg, axis=-1, keepdims=True), sink)
    p = jnp.exp(lg - m)
    es = jnp.exp(sink - m)
    den = jnp.sum(p, axis=-1, keepdims=True) + es
    return p / den, es / den


def _swa_probs_head(qh16, kd, bias, sink, valid):
    lg = lax.dot_general(qh16, kd, _DIMS["nt"], preferred_element_type=F32) * (SWA_HD ** -0.5) + bias
    return _swa_softmax(lg, sink, valid)[0]


def _swa_probs(q16, kd, bias_ref, sink_ref, h0, valid):
    bias = bias_ref[h0:h0 + STACK].reshape(GQ, KB2)
    sink = jnp.concatenate([jnp.full((QB, 1), sink_ref[h], F32) for h in range(h0, h0 + STACK)], axis=0)
    lg = lax.dot_general(q16, kd, _DIMS["nt"], preferred_element_type=F32) * (SWA_HD ** -0.5) + bias
    return _swa_softmax(lg, sink, valid)


def _swa_specs():
    q = pl.BlockSpec((QB, D_RNN), lambda n: (n, 0))
    g = pl.BlockSpec((QB, D_RNN), lambda n: (n, 1))
    kvc = pl.BlockSpec((QB, 2 * LANE), lambda n: (n, 8))
    kvp = pl.BlockSpec((QB, 2 * LANE), lambda n: (jnp.maximum(n - 1, 0), 8))
    bias = pl.BlockSpec((SWA_HEADS, QB, KB2), lambda n: (0, 0, 0))
    sinks = pl.BlockSpec(memory_space=pltpu.SMEM)
    return q, g, kvc, kvp, bias, sinks


def _swa_fwd(p_b, bias_t, sinks):
    def body(q_ref, g_ref, kvc_ref, kvp_ref, bias_ref, sink_ref, y_ref, o_ref):
        n = pl.program_id(0)
        lo, kd, vd = _swa_keys(kvc_ref, kvp_ref)
        valid = _swa_valid(n, QB)
        for hp in range(N_PAIR):
            sl = slice(hp * LANE, (hp + 1) * LANE)
            kvh = hp // (N_PAIR // 2)
            q = q_ref[:, sl]
            outs = []
            for j in range(2):
                qh16 = jnp.where(lo if j == 0 else jnp.logical_not(lo), q, 0.0).astype(BF16)
                probs = _swa_probs_head(qh16, kd[kvh], bias_ref[2 * hp + j], sink_ref[2 * hp + j], valid)
                outs.append(jnp.dot(probs.astype(BF16), vd[kvh], preferred_element_type=F32))
            o = jnp.where(lo, outs[0], outs[1])
            o_ref[:, sl] = o
            g = g_ref[:, sl]
            y_ref[:, sl] = (o * (g * _sigmoid(g))).astype(BF16)

    q, g, kvc, kvp, bias, sinks_spec = _swa_specs()
    out = pl.BlockSpec((QB, D_RNN), lambda n: (n, 0))
    return pl.pallas_call(
        body,
        name="swa_fwd",
        grid=(N_QB,),
        in_specs=[q, g, kvc, kvp, bias, sinks_spec],
        out_specs=[out, out],
        out_shape=[jax.ShapeDtypeStruct((S, D_RNN), BF16), jax.ShapeDtypeStruct((S, D_RNN), F32)],
        compiler_params=_params(("parallel",)),
    )(p_b, p_b, p_b, p_b, bias_t, sinks)


def _swa_bwd(dy, p_b, o_swa, bias_t, sinks, after=None):
    def body(dy_ref, q_ref, g_ref, kvc_ref, kvp_ref, o_ref, bias_ref, sink_ref, *rest):
        dp_ref, dk_ref, dv_ref, dbias_ref, dsink_ref, do_s = rest[-6:]
        n = pl.program_id(0)

        @pl.when(n == 0)
        def _():
            for ref in (dk_ref, dv_ref, dbias_ref, dsink_ref):
                ref[...] = jnp.zeros_like(ref)

        lo, kd, vd = _swa_keys(kvc_ref, kvp_ref)
        hi = jnp.logical_not(lo)
        valid = _swa_valid(n, GQ)
        tile = lambda ref: (lambda hp: ref[:, hp * LANE:(hp + 1) * LANE])
        for hp in range(N_PAIR):
            sl = slice(hp * LANE, (hp + 1) * LANE)
            g, dyv = g_ref[:, sl], dy_ref[:, sl]
            sg = _sigmoid(g)
            do_s[:, sl] = dyv * (g * sg)
            dp_ref[:, D_RNN + hp * LANE:D_RNN + (hp + 1) * LANE] = (
                dyv * o_ref[:, sl] * (sg * (1.0 + g * (1.0 - sg)))).astype(BF16)

        dk_blk = jnp.zeros((KB2, LANE), F32)
        dv_blk = jnp.zeros((KB2, LANE), F32)
        for h0 in range(0, SWA_HEADS, STACK):
            kvh = h0 // GRP
            q16 = _swa_stack(tile(q_ref), lo, h0, masked=True).astype(BF16)
            do8 = _swa_stack(tile(do_s), lo, h0, masked=True)
            do16 = do8.astype(BF16)
            delta = jnp.sum(do8 * _swa_stack(tile(o_ref), lo, h0, masked=False), axis=-1, keepdims=True)
            probs, psink = _swa_probs(q16, kd[kvh], bias_ref, sink_ref, h0, valid)
            dpr = lax.dot_general(do16, vd[kvh], _DIMS["nt"], preferred_element_type=F32)
            ds = probs * (dpr - delta)
            sink_term = psink * delta
            for g in range(STACK):
                h, rows = h0 + g, slice(g * QB, (g + 1) * QB)
                dbias_ref[h] += ds[rows]
                dsink_ref[h:h + 1, :] += jnp.zeros((1, LANE), F32) - jnp.sum(sink_term[rows])
            ds16 = (ds * (SWA_HD ** -0.5)).astype(BF16)
            dq_all = jnp.dot(ds16, kd[kvh], preferred_element_type=F32)
            for pair in range(STACK // 2):
                sl = slice((h0 // 2 + pair) * LANE, (h0 // 2 + pair + 1) * LANE)
                dp_ref[:, sl] = _swa_unstack(dq_all, lo, pair).astype(BF16)
            dk_pair = lax.dot_general(ds16, q16, _DIMS["tn"], preferred_element_type=F32)
            dv_pair = lax.dot_general(probs.astype(BF16), do16, _DIMS["tn"], preferred_element_type=F32)
            keep = lo if kvh == 0 else hi
            dk_blk = dk_blk + jnp.where(keep, dk_pair + pltpu.roll(dk_pair, SWA_HD, 1), 0.0)
            dv_blk = dv_blk + jnp.where(keep, dv_pair + pltpu.roll(dv_pair, SWA_HD, 1), 0.0)

        cur = pl.ds(pl.multiple_of(n * QB, QB), QB)
        dk_ref[cur, :] += dk_blk[QB:KB2]
        dv_ref[cur, :] += dv_blk[QB:KB2]

        @pl.when(n > 0)
        def _():
            prev = pl.ds(pl.multiple_of((n - 1) * QB, QB), QB)
            dk_ref[prev, :] += dk_blk[0:QB]
            dv_ref[prev, :] += dv_blk[0:QB]

    q, g, kvc, kvp, bias, sinks_spec = _swa_specs()
    row = pl.BlockSpec((QB, D_RNN), lambda n: (n, 0))
    acc = pl.BlockSpec((S, LANE), lambda n: (0, 0))
    return pl.pallas_call(
        body,
        name="swa_bwd",
        grid=(N_QB,),
        in_specs=[row, q, g, kvc, kvp, row, bias, sinks_spec] + ([ANY] if after is not None else []),
        out_specs=[pl.BlockSpec((QB, 2 * D_RNN), lambda n: (n, 0)), acc, acc, bias,
                   pl.BlockSpec((SWA_HEADS, LANE), lambda n: (0, 0))],
        out_shape=[jax.ShapeDtypeStruct((S, GROUP_TILES["B"] * LANE), BF16),
                   jax.ShapeDtypeStruct((S, LANE), F32), jax.ShapeDtypeStruct((S, LANE), F32),
                   jax.ShapeDtypeStruct((SWA_HEADS, QB, KB2), F32),
                   jax.ShapeDtypeStruct((SWA_HEADS, LANE), F32)],
        scratch_shapes=[pltpu.VMEM((QB, D_RNN), F32)],
        compiler_params=_params(("arbitrary",)),
    )(dy, p_b, p_b, p_b, p_b, o_swa, bias_t, sinks, *([after] if after is not None else []))


def _swa_pack(dp_b, dk, dv, ts=512):
    def body(_, dk_ref, dv_ref, o_ref):
        o_ref[:, 0:LANE] = dk_ref[...].astype(BF16)
        o_ref[:, LANE:2 * LANE] = dv_ref[...].astype(BF16)

    tile = pl.BlockSpec((ts, LANE), lambda i: (i, 0))
    return pl.pallas_call(
        body,
        name="swa_pack",
        grid=(S // ts,),
        in_specs=[pl.BlockSpec(memory_space=pl.ANY), tile, tile],
        out_specs=pl.BlockSpec((ts, 2 * LANE), lambda i: (i, 8)),
        out_shape=jax.ShapeDtypeStruct(dp_b.shape, dp_b.dtype),
        input_output_aliases={0: 0},
        compiler_params=_params(("parallel",)),
    )(dp_b, dk, dv)


def _split3(v):
    a = v.astype(BF16)
    r = v - a.astype(F32)
    b = r.astype(BF16)
    c = (r - b.astype(F32)).astype(BF16)
    return a, b, c


def _relbias_grad(dbias_flat, onehot_t):
    def body(d_ref, e_ref, o_ref):
        e = e_ref[...]
        acc = jnp.zeros((SWA_HEADS, REL_BUCKETS), F32)
        for term in _split3(d_ref[...]):
            acc = acc + lax.dot_general(term, e, _DIMS["nt"], preferred_element_type=F32)
        o_ref[...] = acc

    return pl.pallas_call(
        body,
        name="relbias_grad",
        out_shape=jax.ShapeDtypeStruct((SWA_HEADS, REL_BUCKETS), F32),
        compiler_params=_params(),
    )(dbias_flat, onehot_t)


TS_MEM = 512


def _mem_probs(q16, mk):
    lg = lax.dot_general(q16, mk, _DIMS["nt"], preferred_element_type=F32) * (MEM_HD ** -0.5)
    p = jnp.exp(lg - jnp.max(lg, axis=-1, keepdims=True))
    return p / jnp.sum(p, axis=-1, keepdims=True)


def _mem_fwd(p_c, mkv):
    def body(q_ref, g_ref, mkv_ref, y_ref, o_ref):
        for hm in range(MEM_HEADS):
            sl = slice(hm * MEM_HD, (hm + 1) * MEM_HD)
            probs = _mem_probs(q_ref[:, sl].astype(BF16), mkv_ref[:, sl])
            o = jnp.dot(probs.astype(BF16), mkv_ref[:, D_RNN + hm * MEM_HD:D_RNN + (hm + 1) * MEM_HD],
                        preferred_element_type=F32)
            o_ref[:, sl] = o
            g = g_ref[:, sl]
            y_ref[:, sl] = (o * (g * _sigmoid(g))).astype(BF16)

    blk = lambda c: pl.BlockSpec((TS_MEM, D_RNN), lambda i: (i, c))
    return pl.pallas_call(
        body,
        name="mem_fwd",
        grid=(S // TS_MEM,),
        in_specs=[blk(0), blk(1), pl.BlockSpec((MEM, 2 * D_RNN), lambda i: (0, 0))],
        out_specs=[blk(0), blk(0)],
        out_shape=[jax.ShapeDtypeStruct((S, D_RNN), BF16), jax.ShapeDtypeStruct((S, D_RNN), F32)],
        compiler_params=_params(("parallel",)),
    )(p_c, p_c, mkv)


def _mem_bwd(dy, p_c, o_mem, mkv):
    def body(dy_ref, q_ref, g_ref, o_ref, mkv_ref, dp_ref, dmkv_ref):
        @pl.when(pl.program_id(0) == 0)
        def _():
            dmkv_ref[...] = jnp.zeros_like(dmkv_ref)

        for hm in range(MEM_HEADS):
            sl = slice(hm * MEM_HD, (hm + 1) * MEM_HD)
            sv = slice(D_RNN + hm * MEM_HD, D_RNN + (hm + 1) * MEM_HD)
            q16 = q_ref[:, sl].astype(BF16)
            mk, mv = mkv_ref[:, sl], mkv_ref[:, sv]
            probs = _mem_probs(q16, mk)
            g, o, dyv = g_ref[:, sl], o_ref[:, sl], dy_ref[:, sl]
            sg = _sigmoid(g)
            do = dyv * (g * sg)
            dp_ref[:, sv] = (dyv * o * (sg * (1.0 + g * (1.0 - sg)))).astype(BF16)
            do16 = do.astype(BF16)
            delta = jnp.sum(do * o, axis=-1, keepdims=True)
            dpr = lax.dot_general(do16, mv, _DIMS["nt"], preferred_element_type=F32)
            ds16 = (probs * (dpr - delta) * (MEM_HD ** -0.5)).astype(BF16)
            dp_ref[:, sl] = jnp.dot(ds16, mk, preferred_element_type=F32).astype(BF16)
            dmkv_ref[:, sl] += lax.dot_general(ds16, q16, _DIMS["tn"], preferred_element_type=F32)
            dmkv_ref[:, sv] += lax.dot_general(probs.astype(BF16), do16, _DIMS["tn"], preferred_element_type=F32)

    blk = lambda c: pl.BlockSpec((TS_MEM, D_RNN), lambda i: (i, c))
    kv = pl.BlockSpec((MEM, 2 * D_RNN), lambda i: (0, 0))
    return pl.pallas_call(
        body,
        name="mem_bwd",
        grid=(S // TS_MEM,),
        in_specs=[blk(0), blk(0), blk(1), blk(0), kv],
        out_specs=[pl.BlockSpec((TS_MEM, 2 * D_RNN), lambda i: (i, 0)), kv],
        out_shape=[jax.ShapeDtypeStruct((S, 2 * D_RNN), BF16), jax.ShapeDtypeStruct((MEM, 2 * D_RNN), F32)],
        compiler_params=_params(("arbitrary",)),
    )(dy, p_c, p_c, o_mem, mkv)


TS_MRG = 512
TD_MRG = 1024
N_DBLK = D // TD_MRG


def _merge_fwd(z, p_d):
    def body(z0, z1, z2, g0, g1, g2, o_ref):
        term = lambda g, z: _sigmoid(g[...].astype(F32)) * z[...].astype(F32)
        o_ref[...] = (term(g0, z0) + term(g1, z1) + term(g2, z2)).astype(BF16)

    blk = pl.BlockSpec((TS_MRG, TD_MRG), lambda i, d: (i, d))
    gate = lambda b: pl.BlockSpec((TS_MRG, TD_MRG), lambda i, d: (i, b * N_DBLK + d))
    return pl.pallas_call(
        body,
        name="merge_fwd",
        grid=(S // TS_MRG, N_DBLK),
        in_specs=[blk, blk, blk, gate(0), gate(1), gate(2)],
        out_specs=blk,
        out_shape=jax.ShapeDtypeStruct((S, D), BF16),
        compiler_params=_params(("parallel", "parallel")),
    )(z[0], z[1], z[2], p_d, p_d, p_d)


TS_MRG_BWD = 128


def _merge_bwd(dmerged, z, p_d, after):
    def body(dm_ref, z0, z1, z2, g_ref, _, dz0, dz1, dz2, dg_ref):
        dm = dm_ref[...].astype(F32)
        for b, (z_ref, dz_ref) in enumerate(((z0, dz0), (z1, dz1), (z2, dz2))):
            cols = slice(b * D, (b + 1) * D)
            sg = _sigmoid(g_ref[:, cols].astype(F32))
            dz_ref[...] = (dm * sg).astype(BF16)
            dg_ref[:, cols] = (dm * z_ref[...].astype(F32) * sg * (1.0 - sg)).astype(BF16)

    row = pl.BlockSpec((TS_MRG_BWD, D), lambda i: (i, 0))
    wide = pl.BlockSpec((TS_MRG_BWD, 3 * D), lambda i: (i, 0))
    outs = pl.pallas_call(
        body,
        name="merge_bwd",
        grid=(S // TS_MRG_BWD,),
        in_specs=[row, row, row, row, wide, pl.BlockSpec(memory_space=pl.ANY)],
        out_specs=[row, row, row, wide],
        out_shape=[jax.ShapeDtypeStruct((S, D), BF16)] * 3 + [jax.ShapeDtypeStruct((S, 3 * D), BF16)],
        compiler_params=_params(("parallel",)),
    )(dmerged, z[0], z[1], z[2], p_d, after)
    return list(outs[:3]), outs[3]


def _bucket_table():
    import numpy as np
    qi = np.arange(QB)[:, None]
    kj = np.arange(KB2)[None, :]
    n = np.maximum(qi + WINDOW - kj, 0)
    max_exact = REL_BUCKETS // 2
    ratio = np.log(np.maximum(n, 1).astype(np.float32) / max_exact) / np.float32(math.log(REL_MAX_DIST / max_exact))
    large = np.minimum(max_exact + (ratio * (REL_BUCKETS - max_exact)).astype(np.int32), REL_BUCKETS - 1)
    bucket = np.where(n < max_exact, n, large).reshape(1, QB * KB2)
    return (bucket == np.arange(REL_BUCKETS)[:, None]).astype(np.float32)


def _bias_expand(rel_bias_t, onehot_t):
    def body(r_ref, e_ref, o_ref):
        e = e_ref[...]
        acc = jnp.zeros((SWA_HEADS, QB * KB2), F32)
        for term in _split3(r_ref[...]):
            acc = acc + jnp.dot(term, e, preferred_element_type=F32)
        o_ref[...] = acc

    return pl.pallas_call(
        body,
        name="bias_expand",
        out_shape=jax.ShapeDtypeStruct((SWA_HEADS, QB * KB2), F32),
        compiler_params=_params(),
    )(rel_bias_t, onehot_t)


PROJ_TN = {"A": 1024, "B": 1152, "C": 1024, "D": 1536}


def _do_first(arrays, token):
    def body(*refs):
        refs[-1][...] = jnp.zeros_like(refs[-1])

    return pl.pallas_call(
        body,
        name="do_first",
        in_specs=[pl.BlockSpec(memory_space=pl.ANY)] * (len(arrays) + 1),
        out_specs=pl.BlockSpec(memory_space=pltpu.VMEM),
        out_shape=jax.ShapeDtypeStruct((8, LANE), F32),
    )(*arrays, token)


def _local_step(x, h, mem, tgt, sp, early, fetch, prefetch, emit, advance):
    onehot_t = jnp.asarray(_bucket_table(), BF16)
    bias_t = _bias_expand(sp["rel_bias"].T, onehot_t).reshape(SWA_HEADS, QB, KB2)
    sinks = sp["swa_sinks"].reshape(SWA_HEADS)
    wa16, wx16 = sp["w_rg_a"].astype(BF16), sp["w_rg_x"].astype(BF16)
    rnn = (sp["conv_w"], sp["conv_b"], wa16, sp["b_rg_a"], wx16, sp["b_rg_x"], sp["lru_lambda"])

    memn = _rms_fwd(mem, sp["mem_norm_g"], "rms_mem", h)
    h_and_prep = _do_first([bias_t, memn, wa16, wx16, *early], h)
    w_grp, p = {}, {}

    def project(g, after, then=None):
        (w_grp[g],) = fetch((g,), after)
        tok = prefetch(then, w_grp[g]) if then is not None else None
        p[g] = _mm(h, w_grp[g], "nt", BF16 if g == "D" else F32, 1024, PROJ_TN[g], D, f"proj_{g}", after=tok)

    project("A", h_and_prep)
    y_rg, hseq = _rglru_fwd(p["A"], *rnn)
    project("B", y_rg)
    y_swa, o_swa = _swa_fwd(p["B"], bias_t, sinks)
    project("C", y_swa, then=("mk",))
    (wmk,) = fetch(("mk",), p["C"])
    tok = prefetch(("br0", "br1", "br2"), wmk)
    mkv = _mm(memn, wmk, "nn", BF16, MEM, 1024, D, "mkv", after=tok)
    y_mem, o_mem = _mem_fwd(p["C"], mkv)
    ys = (y_rg, y_swa, y_mem)
    wbr = fetch(("br0", "br1", "br2"), y_mem)
    tok = prefetch(("D",), wbr[2])
    z = []
    for b in range(3):
        z.append(_mm(ys[b], wbr[b], "nn", BF16, 1024, 1024, D_RNN, f"branch_out{b}", after=z[-1] if z else tok))
    project("D", z[2], then=("out",))
    merged = _merge_fwd(z, p["D"])
    (wout,) = fetch(("out",), merged)
    out = _mm(merged, wout, "nn", F32, 1024, 1024, D, "out_proj")
    sq, dy, dout, d_post = _post_loss(out, x, tgt, sp["post_norm_g"])

    tok = emit({"out": _mm(merged, dout, "tn", BF16, 1024, 1024, S, "d_wout")})
    dmerged = _mm(dout, wout, "nt", BF16, 1024, 1024, D, "d_merged", after=tok)
    tok = advance(dmerged)
    dz, dp_d = _merge_bwd(dmerged, z, p["D"], tok)
    d_win = lambda g, dp_g, after=None: _mm(dp_g, h, "tn", BF16, PROJ_TN[g], 1024, S, f"d_win_{g}", after=after)
    tok = emit({f"br{b}": _mm(ys[b], dz[b], "tn", BF16, 1024, 1024, S, f"d_wbr{b}") for b in range(3)}, tok)
    d_w_d = d_win("D", dp_d, tok)
    tok = emit({"D": d_w_d}, advance(d_w_d))
    dy_mem = _mm(dz[2], wbr[2], "nt", F32, 1024, 1024, D, "d_branch2", after=tok)
    tok = advance(dy_mem)
    dp_c, dmkv = _mem_bwd(dy_mem, p["C"], o_mem, mkv)
    dmkv16 = dmkv.astype(BF16)
    tok = emit({"mk": _mm(memn, dmkv16, "tn", BF16, 1024, 1024, MEM, "d_wmk", after=tok), "C": d_win("C", dp_c)}, tok)
    dmemn = _mm(dmkv16, wmk, "nt", F32, MEM, 1024, D, "d_memn", after=tok)
    tok = advance(dmemn)
    d_memg = _memnorm_bwd(dmemn, mem)
    dy_rg = _mm(dz[0], wbr[0], "nt", F32, 1024, 1024, D, "d_branch0", after=tok)
    dp_a, d_cw, d_cb, d_wa, d_ba, d_wx, d_bx, d_lam = _rglru_bwd(dy_rg, p["A"], hseq, *rnn)
    tok = emit({"A": d_win("A", dp_a)}, tok)
    dy_swa = _mm(dz[1], wbr[1], "nt", F32, 1024, 1024, D, "d_branch1", after=tok)
    tok = advance(dy_swa)
    dp_b, dk, dv, d_bias, d_sink = _swa_bwd(dy_swa, p["B"], o_swa, bias_t, sinks, after=tok)
    dp_b = _swa_pack(dp_b, dk, dv)
    d_rel = _relbias_grad(d_bias.reshape(SWA_HEADS, QB * KB2), onehot_t).T
    dp = {"A": dp_a, "B": dp_b, "C": dp_c, "D": dp_d}
    tok = emit({"B": d_win("B", dp_b)}, tok)
    dh = None
    for g in GROUPS:
        dh = _mm(dp[g], w_grp[g], "nn", F32, 1024, 1024, 2304 if g == "B" else 2048, f"d_h_{g}", acc=dh,
                 after=tok if g in ("A", "B") else None)
        if g == "A":
            tok = advance(dh)
    grad_x, d_pre = _pre_bwd(dh, x, dy, sp["pre_norm_g"])

    d_small = {
        "pre_norm_g": d_pre, "post_norm_g": d_post, "mem_norm_g": d_memg, "conv_w": d_cw, "conv_b": d_cb,
        "w_rg_a": d_wa, "b_rg_a": d_ba, "w_rg_x": d_wx, "b_rg_x": d_bx, "lru_lambda": d_lam,
        "swa_sinks": d_sink[:, 0].reshape(1, SWA_HEADS), "rel_bias": d_rel,
    }
    return sq, grad_x, d_small


ANY = pl.BlockSpec(memory_space=pl.ANY)
SHARD_ROWS = D // N_CHIPS
GATHERED = {"A": (2048, D), "B": (2304, D), "C": (2048, D), "D": (6144, D), "mk": (D, D),
            "br0": (D_RNN, D), "br1": (D_RNN, D), "br2": (D_RNN, D), "out": (D, D)}
SHARD_SHAPES = {"win": (SHARD, D), "mk": (SHARD_ROWS, D), "br0": (D_RNN, SHARD_ROWS), "br1": (D_RNN, SHARD_ROWS),
                "br2": (D_RNN, SHARD_ROWS), "out": (SHARD_ROWS, D)}
SHARDS = tuple(SHARD_SHAPES)
HALF_AXIS = {"win": 1, "mk": 1, "br0": 0, "br1": 0, "br2": 0, "out": 1,
             "A": 1, "B": 1, "C": 1, "D": 1}


def _halved(shape, axis):
    return (shape[0] // 2, shape[1]) if axis == 0 else (shape[0], shape[1] // 2)


class Piece(NamedTuple):
    src: str
    dst: str
    rows: int
    sr0: int
    sc0: int
    dr0: int
    dc0: int
    ncols: int


def _pieces_of(jj):
    out = [Piece("win", g, n, r, 0, gr, 0, D) for r, n, g, gr in _shard_runs(jj)]
    out.append(Piece("mk", "mk", SHARD_ROWS, 0, 0, SHARD_ROWS * jj, 0, D))
    out += [Piece(f"br{b}", f"br{b}", D_RNN, 0, 0, 0, SHARD_ROWS * jj, SHARD_ROWS) for b in range(3)]
    out.append(Piece("out", "out", SHARD_ROWS, 0, 0, SHARD_ROWS * jj, 0, D))
    return out


def _half_rect(ref, p, side, which):
    r0, c0 = (p.sr0, p.sc0) if side == "src" else (p.dr0, p.dc0)
    if HALF_AXIS[p.src] == 1:
        return _rect(ref, r0, p.rows, c0 + which * (p.ncols // 2), p.ncols // 2)
    return _rect(ref, r0 + which * (p.rows // 2), p.rows // 2, c0, p.ncols)


def _rect_in_half(ref, p, side):
    r0, c0 = (p.sr0, p.sc0) if side == "src" else (p.dr0, p.dc0)
    if HALF_AXIS[p.src] == 1:
        return _rect(ref, r0, p.rows, 0, p.ncols // 2)
    return _rect(ref, 0, p.rows // 2, c0, p.ncols)


MAX_PIECES = max(len(_pieces_of(jj)) for jj in range(N_CHIPS))


def _rect(ref, r0, rows, c0, ncols):
    return ref.at[pl.ds(r0, rows), pl.ds(c0, ncols)]


def _position():
    x, y, c = lax.axis_index("x"), lax.axis_index("y"), lax.axis_index("c")
    return x, y, c, 2 * x + y


HBM = pl.BlockSpec(memory_space=pltpu.HBM)
SEM = pl.BlockSpec(memory_space=pltpu.SEMAPHORE)
EFFECT = pltpu.SideEffectType.DATAFLOW_SIDE_EFFECTING
N_SEM = MAX_PIECES * N_CHIPS
GATHER_STAGES = (("A",), ("B",), ("C",), ("mk",), ("br0", "br1", "br2"), ("D",), ("out",))


def _in_hbm(a):
    return pltpu.with_memory_space_constraint(a, pltpu.HBM)


def _stage_pieces(jj, stage):
    return [(i, p) for i, p in enumerate(_pieces_of(jj)) if p.dst in stage]


def _own_block_table(g):
    import numpy as np
    units = np.full((N_CHIPS, GATHERED[g][0] // HALF_TILE), -1, np.int64)
    for jj in range(N_CHIPS):
        for r, n, grp, gr in _shard_runs(jj):
            if grp == g:
                for k in range(n // HALF_TILE):
                    units[jj, gr // HALF_TILE + k] = r // HALF_TILE + k
    tbl = np.zeros((N_CHIPS, 2, GATHERED[g][0] // LANE), np.int32)
    for jj in range(N_CHIPS):
        for b in range(tbl.shape[2]):
            first, second = units[jj, 2 * b], units[jj, 2 * b + 1]
            if jj % 2 == 0:
                src = first if first >= 0 else second - 1
                if first >= 0 or second >= 0:
                    assert src % 2 == 0
                    tbl[jj, :, b] = src // 2
            else:
                if first >= 0:
                    assert first % 2 == 1
                    tbl[jj, 0, b] = first // 2
                if second >= 0:
                    assert second % 2 == 0
                    tbl[jj, 1, b] = second // 2
    return tbl


def _place_group(w_t, g, tables, odd_arr, after):
    nb = GATHERED[g][0] // LANE

    def body(t_ref, odd_ref, a_ref, b_ref, _, o_ref):
        odd = odd_ref[0] == 1
        o_ref[0:HALF_TILE, :] = jnp.where(odd, a_ref[HALF_TILE:LANE, :], a_ref[0:HALF_TILE, :]).astype(BF16)
        o_ref[HALF_TILE:LANE, :] = jnp.where(odd, b_ref[0:HALF_TILE, :], a_ref[HALF_TILE:LANE, :]).astype(BF16)

    return pl.pallas_call(
        body,
        name=f"place_{g}",
        grid_spec=pltpu.PrefetchScalarGridSpec(
            num_scalar_prefetch=2,
            grid=(nb,),
            in_specs=[pl.BlockSpec((LANE, D), lambda b, t, o: (t[0, b], 0)),
                      pl.BlockSpec((LANE, D), lambda b, t, o: (t[1, b], 0)), ANY],
            out_specs=pl.BlockSpec((LANE, D), lambda b, t, o: (b, 0)),
        ),
        out_shape=jax.ShapeDtypeStruct(GATHERED[g], BF16),
        compiler_params=_params(("parallel",)),
    )(tables, odd_arr, w_t, w_t, after)


def _place_shard(shard, name, after):
    rows, cols = shard.shape
    by_rows = HALF_AXIS[name] == 1

    def body(x_ref, _, o_ref):
        o_ref[...] = x_ref[...].astype(BF16)

    return pl.pallas_call(
        body,
        name=f"place_{name}",
        grid=(N_CHIPS,),
        in_specs=[pl.BlockSpec((rows, cols), lambda b: (0, 0)), ANY],
        out_specs=pl.BlockSpec((rows, cols), (lambda b: (b, 0)) if by_rows else (lambda b: (0, b))),
        out_shape=jax.ShapeDtypeStruct(GATHERED[name], BF16),
        compiler_params=_params(("parallel",)),
    )(shard, after)


def _gather_copy(arr, send_sems, recv_sems, c, jj, i, p, kk):
    rect = _half_rect(arr[p.dst], p, "dst", c)
    return pltpu.make_async_remote_copy(
        src_ref=rect, dst_ref=rect, send_sem=send_sems.at[i * N_CHIPS + kk],
        recv_sem=recv_sems.at[jj * MAX_PIECES + i], device_id=(kk // 2, kk % 2, c), device_id_type=MESH)


def _gather_start(arrays, after):
    stage = tuple(arrays)
    na = len(stage)

    def body(*refs):
        arr = dict(zip(stage, refs[:na]))
        send_sems, recv_sems = refs[na + 1], refs[na + 2]
        token = refs[-1]
        _, _, c, j = _position()
        for jj in range(N_CHIPS):
            @pl.when(j == jj)
            def _():
                for i, p in _stage_pieces(jj, stage):
                    for kk in range(N_CHIPS):
                        if kk != jj:
                            _gather_copy(arr, send_sems, recv_sems, c, jj, i, p, kk).start()
        token[...] = jnp.zeros_like(token)

    outs = pl.pallas_call(
        body,
        name=f"gather_start_{stage[0]}",
        in_specs=[HBM] * na + [ANY],
        out_specs=[SEM, SEM] + [HBM] * na + [pl.BlockSpec(memory_space=pltpu.VMEM)],
        out_shape=[pltpu.SemaphoreType.DMA((N_SEM,)), pltpu.SemaphoreType.DMA((N_SEM,))]
        + [pltpu.HBM(GATHERED[n], BF16) for n in stage] + [jax.ShapeDtypeStruct((8, LANE), F32)],
        input_output_aliases={k: 2 + k for k in range(na)},
        compiler_params=pltpu.CompilerParams(has_side_effects=EFFECT),
    )(*[_in_hbm(arrays[n]) for n in stage], after)
    return outs[0], outs[1], dict(zip(stage, outs[2:2 + na])), outs[-1]


def _gather_wait(send_sems, recv_sems, arrays, after):
    stage = tuple(arrays)
    na = len(stage)

    def body(*refs):
        arr = dict(zip(stage, refs[:na]))
        sems_s, sems_r = refs[na], refs[na + 1]
        _, _, c, j = _position()
        for jj in range(N_CHIPS):
            @pl.when(j != jj)
            def _():
                for i, p in _stage_pieces(jj, stage):
                    _gather_copy(arr, sems_s, sems_r, c, jj, i, p, jj).wait_recv()

            @pl.when(j == jj)
            def _():
                for i, p in _stage_pieces(jj, stage):
                    for kk in range(N_CHIPS):
                        if kk != jj:
                            _gather_copy(arr, sems_s, sems_r, c, jj, i, p, kk).wait_send()

    outs = pl.pallas_call(
        body,
        name=f"gather_wait_{stage[0]}",
        in_specs=[HBM] * na + [SEM, SEM, ANY],
        out_specs=[HBM] * na,
        out_shape=[pltpu.HBM(GATHERED[n], BF16) for n in stage],
        input_output_aliases={k: k for k in range(na)},
        compiler_params=pltpu.CompilerParams(has_side_effects=EFFECT),
    )(*[arrays[n] for n in stage], send_sems, recv_sems, after)
    return dict(zip(stage, outs))


def _gather_swap(arrays):
    stage = tuple(arrays)
    na = len(stage)

    def body(*refs):
        dst = dict(zip(stage, refs[na:2 * na]))
        send_sems, recv_sems = refs[2 * na:]
        x, y, c, j = _position()

        def fwd(jj, i, p, which):
            rect = _half_rect(dst[p.dst], p, "dst", which)
            return pltpu.make_async_remote_copy(
                src_ref=rect, dst_ref=rect, send_sem=send_sems.at[jj * MAX_PIECES + i],
                recv_sem=recv_sems.at[jj * MAX_PIECES + i], device_id=(x, y, 1 - c), device_id_type=MESH)

        for jj in range(N_CHIPS):
            @pl.when(j != jj)
            def _():
                for i, p in _stage_pieces(jj, stage):
                    fwd(jj, i, p, c).start()
        for jj in range(N_CHIPS):
            @pl.when(j != jj)
            def _():
                for i, p in _stage_pieces(jj, stage):
                    fwd(jj, i, p, 1 - c).wait_recv()
        for jj in range(N_CHIPS):
            @pl.when(j != jj)
            def _():
                for i, p in _stage_pieces(jj, stage):
                    fwd(jj, i, p, c).wait_send()

    outs = pl.pallas_call(
        body,
        name=f"gather_swap_{stage[0]}",
        in_specs=[ANY] * na,
        out_specs=[ANY] * na,
        out_shape=[jax.ShapeDtypeStruct(GATHERED[n], BF16) for n in stage],
        input_output_aliases={k: k for k in range(na)},
        scratch_shapes=[pltpu.SemaphoreType.DMA((N_SEM,)), pltpu.SemaphoreType.DMA((N_SEM,))],
        compiler_params=pltpu.CompilerParams(has_side_effects=True),
    )(*[arrays[n] for n in stage])
    return dict(zip(stage, outs))


def _pass_on_copy(arr, send_sems, recv_sems, x, y, c, jj, i, p, which):
    rect = _half_rect(arr[p.dst], p, "dst", which)
    return pltpu.make_async_remote_copy(
        src_ref=rect, dst_ref=rect, send_sem=send_sems.at[jj * MAX_PIECES + i],
        recv_sem=recv_sems.at[jj * MAX_PIECES + i], device_id=(x, y, 1 - c), device_id_type=MESH)


def _gather_pass_start(arrays, after):
    stage = tuple(arrays)
    na = len(stage)

    def body(*refs):
        arr = dict(zip(stage, refs[:na]))
        x, y, c, j = _position()
        for jj in range(N_CHIPS):
            @pl.when(j != jj)
            def _():
                for i, p in _stage_pieces(jj, stage):
                    _pass_on_copy(arr, refs[na + 1], refs[na + 2], x, y, c, jj, i, p, c).start()
        refs[-1][...] = jnp.zeros_like(refs[-1])

    outs = pl.pallas_call(
        body,
        name=f"gather_pass_start_{stage[0]}",
        in_specs=[HBM] * na + [ANY],
        out_specs=[SEM, SEM] + [HBM] * na + [pl.BlockSpec(memory_space=pltpu.VMEM)],
        out_shape=[pltpu.SemaphoreType.DMA((N_SEM,)), pltpu.SemaphoreType.DMA((N_SEM,))]
        + [pltpu.HBM(GATHERED[n], BF16) for n in stage] + [jax.ShapeDtypeStruct((8, LANE), F32)],
        input_output_aliases={k: 2 + k for k in range(na)},
        compiler_params=pltpu.CompilerParams(has_side_effects=EFFECT),
    )(*[arrays[n] for n in stage], after)
    return outs[0], outs[1], dict(zip(stage, outs[2:2 + na])), outs[-1]


def _gather_pass_wait(send_sems, recv_sems, arrays, after):
    stage = tuple(arrays)
    na = len(stage)

    def body(*refs):
        arr = dict(zip(stage, refs[:na]))
        x, y, c, j = _position()
        for jj in range(N_CHIPS):
            @pl.when(j != jj)
            def _():
                for i, p in _stage_pieces(jj, stage):
                    _pass_on_copy(arr, refs[na], refs[na + 1], x, y, c, jj, i, p, 1 - c).wait_recv()
                    _pass_on_copy(arr, refs[na], refs[na + 1], x, y, c, jj, i, p, c).wait_send()

    outs = pl.pallas_call(
        body,
        name=f"gather_pass_wait_{stage[0]}",
        in_specs=[HBM] * na + [SEM, SEM, ANY],
        out_specs=[HBM] * na,
        out_shape=[pltpu.HBM(GATHERED[n], BF16) for n in stage],
        input_output_aliases={k: k for k in range(na)},
        compiler_params=pltpu.CompilerParams(has_side_effects=EFFECT),
    )(*[arrays[n] for n in stage], send_sems, recv_sems, after)
    return dict(zip(stage, outs))


def _own_half(ref, shape, axis, which):
    if axis == 1:
        return ref.at[:, pl.ds(which * (shape[1] // 2), shape[1] // 2)]
    return ref.at[pl.ds(which * (shape[0] // 2), shape[0] // 2), :]


def _swap_copies(names, src, dst, send_sems, recv_sems):
    x, y, c, _ = _position()
    return [pltpu.make_async_remote_copy(
        src_ref=_own_half(src[n], GATHERED[n], HALF_AXIS[n], 1 - c), dst_ref=dst[n],
        send_sem=send_sems.at[k], recv_sem=recv_sems.at[k],
        device_id=(x, y, 1 - c), device_id_type=MESH) for k, n in enumerate(names)]


def _swap_start(grads, after):
    names = tuple(grads)
    n = len(names)

    def body(*refs):
        src, dst = dict(zip(names, refs[:n])), dict(zip(names, refs[n:2 * n]))
        for cp in _swap_copies(names, src, dst, refs[2 * n + 1], refs[2 * n + 2]):
            cp.start()
        refs[-1][...] = jnp.zeros_like(refs[-1])

    half_shape = lambda nm: _halved(GATHERED[nm], HALF_AXIS[nm])
    args = [_in_hbm(grads[nm]) for nm in names] + [_in_hbm(lax.empty(half_shape(nm), BF16)) for nm in names]
    if after is None:
        after = jnp.zeros((8, LANE), F32)
    outs = pl.pallas_call(
        body,
        name=f"swap_start_{names[0]}",
        in_specs=[HBM] * (2 * n) + [ANY],
        out_specs=[SEM, SEM] + [HBM] * (2 * n) + [pl.BlockSpec(memory_space=pltpu.VMEM)],
        out_shape=[pltpu.SemaphoreType.DMA((n,)), pltpu.SemaphoreType.DMA((n,))]
        + [pltpu.HBM(GATHERED[nm], BF16) for nm in names] + [pltpu.HBM(half_shape(nm), BF16) for nm in names]
        + [jax.ShapeDtypeStruct((8, LANE), F32)],
        input_output_aliases={k: 2 + k for k in range(2 * n)},
        compiler_params=pltpu.CompilerParams(has_side_effects=EFFECT),
    )(*args, after)
    return outs[0], outs[1], dict(zip(names, outs[2:2 + n])), dict(zip(names, outs[2 + n:2 + 2 * n])), outs[-1]


def _swap_wait(send_sems, recv_sems, grads, landing, after):
    names = tuple(grads)
    n = len(names)

    def body(*refs):
        src, dst = dict(zip(names, refs[:n])), dict(zip(names, refs[n:2 * n]))
        copies = _swap_copies(names, src, dst, refs[2 * n], refs[2 * n + 1])
        for cp in copies:
            cp.wait_recv()
        for cp in copies:
            cp.wait_send()

    half_shape = lambda nm: _halved(GATHERED[nm], HALF_AXIS[nm])
    outs = pl.pallas_call(
        body,
        name=f"swap_wait_{names[0]}",
        in_specs=[HBM] * (2 * n) + [SEM, SEM, ANY],
        out_specs=[HBM] * (2 * n),
        out_shape=[pltpu.HBM(GATHERED[nm], BF16) for nm in names] + [pltpu.HBM(half_shape(nm), BF16) for nm in names],
        input_output_aliases={k: k for k in range(2 * n)},
        compiler_params=pltpu.CompilerParams(has_side_effects=EFFECT),
    )(*[grads[nm] for nm in names], *[landing[nm] for nm in names], send_sems, recv_sems, after)
    return dict(zip(names, outs[:n])), dict(zip(names, outs[n:]))


ADD_ROWS = {"A": 1024, "B": 768, "C": 1024, "D": 1536, "mk": 1024, "br0": 512, "br1": 512, "br2": 512, "out": 1024}


def _add_half(full, recv, c_arr, name):
    rows, cols = recv.shape
    tr = ADD_ROWS[name]
    if HALF_AXIS[name] == 1:
        index = lambda i, c_ref: (i, c_ref[0])
    else:
        nb = rows // tr
        index = lambda i, c_ref: (nb * c_ref[0] + i, 0)

    def body(c_ref, a_ref, b_ref, o_ref):
        o_ref[...] = (a_ref[...].astype(F32) + b_ref[...].astype(F32)).astype(BF16)

    return pl.pallas_call(
        body,
        name=f"add_half_{name}",
        grid_spec=pltpu.PrefetchScalarGridSpec(
            num_scalar_prefetch=1,
            grid=(rows // tr,),
            in_specs=[pl.BlockSpec((tr, cols), index), pl.BlockSpec((tr, cols), lambda i, c_ref: (i, 0))],
            out_specs=pl.BlockSpec((tr, cols), lambda i, c_ref: (i, 0)),
        ),
        out_shape=jax.ShapeDtypeStruct((rows, cols), BF16),
        compiler_params=_params(("parallel",)),
    )(c_arr, full, recv)


SLOT_SHAPES = {n: _halved(SHARD_SHAPES[n], HALF_AXIS[n]) for n in SHARDS}


def _slot_shape(n):
    return (N_CHIPS,) + SLOT_SHAPES[n]


def _stage_shards(stage):
    pieces = [p for jj in range(N_CHIPS) for p in _pieces_of(jj)]
    return tuple(s for s in SHARDS if any(p.src == s and p.dst in stage for p in pieces))


def _scatter_copy(src, dst, send_sems, recv_sems, c, jj, kk, i, p):
    return pltpu.make_async_remote_copy(
        src_ref=_rect_in_half(src[p.dst], p, "dst"), dst_ref=_rect_in_half(dst[p.src].at[jj], p, "src"),
        send_sem=send_sems.at[kk * MAX_PIECES + i], recv_sem=recv_sems.at[jj * MAX_PIECES + i],
        device_id=(kk // 2, kk % 2, c), device_id_type=MESH)


def _scatter_start(halves, slots):
    stage, touched = tuple(halves), tuple(slots)
    nh, nt = len(stage), len(touched)

    def body(*refs):
        src = dict(zip(stage, refs[:nh]))
        dst = dict(zip(touched, refs[nh:nh + nt]))
        send_sems, recv_sems = refs[nh + nt], refs[nh + nt + 1]
        token = refs[-1]
        _, _, c, j = _position()
        for jj in range(N_CHIPS):
            @pl.when(j == jj)
            def _():
                for kk in range(N_CHIPS):
                    if kk != jj:
                        for i, p in _stage_pieces(kk, stage):
                            _scatter_copy(src, dst, send_sems, recv_sems, c, jj, kk, i, p).start()
        token[...] = jnp.zeros_like(token)

    outs = pl.pallas_call(
        body,
        name=f"scatter_start_{stage[0]}",
        in_specs=[HBM] * (nh + nt),
        out_specs=[SEM, SEM] + [HBM] * (nh + nt) + [pl.BlockSpec(memory_space=pltpu.VMEM)],
        out_shape=[pltpu.SemaphoreType.DMA((N_SEM,)), pltpu.SemaphoreType.DMA((N_SEM,))]
        + [pltpu.HBM(halves[n].shape, BF16) for n in stage] + [pltpu.HBM(_slot_shape(s), BF16) for s in touched]
        + [jax.ShapeDtypeStruct((8, LANE), F32)],
        input_output_aliases={k: 2 + k for k in range(nh + nt)},
        compiler_params=pltpu.CompilerParams(has_side_effects=EFFECT),
    )(*[_in_hbm(halves[n]) for n in stage], *[_in_hbm(slots[s]) for s in touched])
    return outs[0], outs[1], dict(zip(stage, outs[2:2 + nh])), dict(zip(touched, outs[2 + nh:2 + nh + nt])), outs[-1]


def _scatter_wait(send_sems, recv_sems, halves, slots, after):
    stage, touched = tuple(halves), tuple(slots)
    nh, nt = len(stage), len(touched)

    def body(*refs):
        src = dict(zip(stage, refs[:nh]))
        dst = dict(zip(touched, refs[nh:nh + nt]))
        sems_s, sems_r = refs[nh + nt], refs[nh + nt + 1]
        _, _, c, j = _position()
        for jj in range(N_CHIPS):
            @pl.when(j == jj)
            def _():
                for ss in range(N_CHIPS):
                    if ss != jj:
                        for i, p in _stage_pieces(jj, stage):
                            _scatter_copy(src, dst, sems_s, sems_r, c, ss, jj, i, p).wait_recv()
                for kk in range(N_CHIPS):
                    if kk != jj:
                        for i, p in _stage_pieces(kk, stage):
                            _scatter_copy(src, dst, sems_s, sems_r, c, jj, kk, i, p).wait_send()

    outs = pl.pallas_call(
        body,
        name=f"scatter_wait_{stage[0]}",
        in_specs=[HBM] * (nh + nt) + [SEM, SEM, ANY],
        out_specs=[HBM] * (nh + nt),
        out_shape=[pltpu.HBM(halves[n].shape, BF16) for n in stage] + [pltpu.HBM(_slot_shape(s), BF16) for s in touched],
        input_output_aliases={k: k for k in range(nh + nt)},
        compiler_params=pltpu.CompilerParams(has_side_effects=EFFECT),
    )(*[halves[n] for n in stage], *[slots[s] for s in touched], send_sems, recv_sems, after)
    return dict(zip(stage, outs[:nh])), dict(zip(touched, outs[nh:]))


SUM_ROWS = {"mk": 512, "br0": 512, "br1": 512, "br2": 512, "out": 512}


def _sum_in_chip_order(chip, own, s_ref):
    acc = None
    for k in range(N_CHIPS):
        term = jnp.where(chip == k, own, s_ref[k].astype(F32))
        acc = term if acc is None else acc + term
    return acc


def _sum_slots(slots, own_half, pos_arr, name):
    _, rows, cols = slots.shape
    tr = SUM_ROWS[name]
    nb = rows // tr
    if HALF_AXIS[name] == 1:
        own_index = lambda i, pos: (nb * pos[1] + i, 0)
        out_index = lambda i, pos: (i, pos[0])
    else:
        own_index = lambda i, pos: (i, pos[1])
        out_index = lambda i, pos: (nb * pos[0] + i, 0)

    def body(pos, s_ref, own_ref, o_ref):
        o_ref[...] = _sum_in_chip_order(pos[1], own_ref[...].astype(F32), s_ref)

    return pl.pallas_call(
        body,
        name=f"sum_slots_{name}",
        grid_spec=pltpu.PrefetchScalarGridSpec(
            num_scalar_prefetch=1,
            grid=(nb,),
            in_specs=[pl.BlockSpec((N_CHIPS, tr, cols), lambda i, pos: (0, i, 0)),
                      pl.BlockSpec((tr, cols), own_index)],
            out_specs=pl.BlockSpec((tr, cols), out_index),
        ),
        out_shape=jax.ShapeDtypeStruct(SHARD_SHAPES[name], F32),
        compiler_params=_params(("parallel",)),
    )(pos_arr, slots, own_half)


def _own_partial_tables():
    import numpy as np
    nb = SHARD // HALF_TILE
    grp, blk = np.zeros((N_CHIPS, nb), np.int32), np.zeros((N_CHIPS, nb), np.int32)
    for jj in range(N_CHIPS):
        for r, n, g, gr in _shard_runs(jj):
            for k in range(n // HALF_TILE):
                grp[jj, r // HALF_TILE + k] = GROUPS.index(g)
                blk[jj, r // HALF_TILE + k] = gr // HALF_TILE + k
    return grp, blk


SUM_READS = 4
SUM_WRITES = 2


def _sum_slots_win(slots, own_halves, pos_arr, grp_tbl, blk_tbl):
    nb = SHARD // HALF_TILE
    cols = D // 2

    def body(pos, grp, blk, s_hbm, a_hbm, b_hbm, c_hbm, d_hbm, o_hbm, sbuf, ownbuf, obuf, ssem, ownsem, osem):
        groups = (a_hbm, b_hbm, c_hbm, d_hbm)

        def rows_of(b):
            start = b * HALF_TILE
            return pl.ds(start if isinstance(b, int) else pl.multiple_of(start, HALF_TILE), HALF_TILE)

        def slots_read(b, slot):
            return pltpu.make_async_copy(s_hbm.at[:, rows_of(b)], sbuf.at[slot], ssem.at[slot])

        def own_read(b, slot, act):
            for gi, ref in enumerate(groups):
                @pl.when(grp[b] == gi)
                def _():
                    act(pltpu.make_async_copy(ref.at[rows_of(blk[b])], ownbuf.at[slot], ownsem.at[slot]))

        def write(b, slot):
            half = pl.ds(pl.multiple_of(pos[0] * cols, cols), cols)
            return pltpu.make_async_copy(obuf.at[slot], o_hbm.at[rows_of(b), half], osem.at[slot])

        for b in range(SUM_READS):
            slots_read(b, b).start()
            own_read(b, b, lambda copy: copy.start())

        def step(b, carry):
            slot = b % SUM_READS
            oslot = b % SUM_WRITES
            slots_read(b, slot).wait()
            own_read(b, slot, lambda copy: copy.wait())

            @pl.when(b >= SUM_WRITES)
            def _():
                write(b - SUM_WRITES, oslot).wait()

            obuf[oslot] = _sum_in_chip_order(pos[1], ownbuf[slot].astype(F32), sbuf.at[slot])
            write(b, oslot).start()

            @pl.when(b + SUM_READS < nb)
            def _():
                slots_read(b + SUM_READS, slot).start()
                own_read(b + SUM_READS, slot, lambda copy: copy.start())
            return carry

        lax.fori_loop(0, nb, step, 0)
        for b in range(nb - SUM_WRITES, nb):
            write(b, b % SUM_WRITES).wait()

    smem = pl.BlockSpec(memory_space=pltpu.SMEM)
    return pl.pallas_call(
        body,
        name="sum_slots_win",
        in_specs=[smem] * 3 + [ANY] * (1 + len(GROUPS)),
        out_specs=ANY,
        out_shape=jax.ShapeDtypeStruct(SHARD_SHAPES["win"], F32),
        scratch_shapes=[pltpu.VMEM((SUM_READS, N_CHIPS, HALF_TILE, cols), BF16), pltpu.VMEM((SUM_READS, HALF_TILE, cols), BF16),
                        pltpu.VMEM((SUM_WRITES, HALF_TILE, cols), F32), pltpu.SemaphoreType.DMA((SUM_READS,)),
                        pltpu.SemaphoreType.DMA((SUM_READS,)), pltpu.SemaphoreType.DMA((SUM_WRITES,))],
        compiler_params=pltpu.CompilerParams(vmem_limit_bytes=VMEM_LIMIT),
    )(pos_arr, grp_tbl, blk_tbl, slots, *[own_halves[g] for g in GROUPS])


def _share_copy(buf, name, send_sems, recv_sems, k, which):
    x, y, c, _ = _position()
    half = _own_half(buf, SHARD_SHAPES[name], HALF_AXIS[name], which)
    return pltpu.make_async_remote_copy(src_ref=half, dst_ref=half, send_sem=send_sems.at[k], recv_sem=recv_sems.at[k],
                                        device_id=(x, y, 1 - c), device_id_type=MESH)


def _share_start(sums, after):
    names = tuple(sums)
    n = len(names)

    def body(*refs):
        _, _, c, _ = _position()
        for k, nm in enumerate(names):
            _share_copy(refs[k], nm, refs[n + 1], refs[n + 2], k, c).start()
        refs[-1][...] = jnp.zeros_like(refs[-1])

    outs = pl.pallas_call(
        body,
        name=f"share_start_{names[0]}",
        in_specs=[HBM] * n + [ANY],
        out_specs=[SEM, SEM] + [HBM] * n + [pl.BlockSpec(memory_space=pltpu.VMEM)],
        out_shape=[pltpu.SemaphoreType.DMA((n,)), pltpu.SemaphoreType.DMA((n,))]
        + [pltpu.HBM(SHARD_SHAPES[nm], F32) for nm in names] + [jax.ShapeDtypeStruct((8, LANE), F32)],
        input_output_aliases={k: 2 + k for k in range(n)},
        compiler_params=pltpu.CompilerParams(has_side_effects=EFFECT),
    )(*[_in_hbm(sums[nm]) for nm in names], after)
    return outs[0], outs[1], dict(zip(names, outs[2:2 + n])), outs[-1]


def _share_wait(send_sems, recv_sems, sums, after):
    names = tuple(sums)
    n = len(names)

    def body(*refs):
        _, _, c, _ = _position()
        for k, nm in enumerate(names):
            _share_copy(refs[k], nm, refs[n], refs[n + 1], k, 1 - c).wait_recv()
            _share_copy(refs[k], nm, refs[n], refs[n + 1], k, c).wait_send()

    outs = pl.pallas_call(
        body,
        name=f"share_wait_{names[0]}",
        in_specs=[HBM] * n + [SEM, SEM, ANY],
        out_specs=[HBM] * n,
        out_shape=[pltpu.HBM(SHARD_SHAPES[nm], F32) for nm in names],
        input_output_aliases={k: k for k in range(n)},
        compiler_params=pltpu.CompilerParams(has_side_effects=EFFECT),
    )(*[sums[nm] for nm in names], send_sems, recv_sems, after)
    return dict(zip(names, outs))


def _all_reduce_small(pack, name):
    rows = pack.shape[0]
    half = rows // 2

    def body(p_ref, o_ref, sib, land, sems):
        x, y, c, j = _position()
        sibling = (x, y, 1 - c)
        swap = pltpu.make_async_remote_copy(src_ref=p_ref, dst_ref=sib, send_sem=sems.at[0], recv_sem=sems.at[1],
                                            device_id=sibling, device_id_type=MESH)
        swap.start()
        swap.wait_recv()
        land[j] = p_ref[...] + sib[...]

        def mine(k, which):
            return land.at[k, pl.ds(which * half, half)]

        def ici(kk):
            return pltpu.make_async_remote_copy(
                src_ref=mine(j, c), dst_ref=mine(j, c), send_sem=sems.at[2 + kk], recv_sem=sems.at[6 + j],
                device_id=(kk // 2, kk % 2, c), device_id_type=MESH)

        def arrival(kk):
            return pltpu.make_async_remote_copy(
                src_ref=mine(kk, c), dst_ref=mine(kk, c), send_sem=sems.at[2 + kk], recv_sem=sems.at[6 + kk],
                device_id=(kk // 2, kk % 2, c), device_id_type=MESH)

        def passed_on(kk, which):
            return pltpu.make_async_remote_copy(
                src_ref=mine(kk, which), dst_ref=mine(kk, which), send_sem=sems.at[10 + kk],
                recv_sem=sems.at[14 + kk], device_id=sibling, device_id_type=MESH)

        for kk in range(N_CHIPS):
            @pl.when(j != kk)
            def _():
                ici(kk).start()
        for kk in range(N_CHIPS):
            @pl.when(j != kk)
            def _():
                arrival(kk).wait_recv()
                passed_on(kk, c).start()
        for kk in range(N_CHIPS):
            @pl.when(j != kk)
            def _():
                passed_on(kk, 1 - c).wait_recv()
        acc = land[0]
        for kk in range(1, N_CHIPS):
            acc = acc + land[kk]
        o_ref[...] = acc
        swap.wait_send()
        for kk in range(N_CHIPS):
            @pl.when(j != kk)
            def _():
                ici(kk).wait_send()
                passed_on(kk, c).wait_send()

    vmem = pl.BlockSpec(memory_space=pltpu.VMEM)
    return pl.pallas_call(
        body,
        name=name,
        in_specs=[vmem],
        out_specs=vmem,
        out_shape=jax.ShapeDtypeStruct((rows, LANE), F32),
        scratch_shapes=[pltpu.VMEM((rows, LANE), F32), pltpu.VMEM((N_CHIPS, rows, LANE), F32),
                        pltpu.SemaphoreType.DMA((18,))],
        compiler_params=pltpu.CompilerParams(has_side_effects=True, vmem_limit_bytes=VMEM_LIMIT),
    )(pack)


ADAM_ROWS = {"mk": 256, "br0": 512, "br1": 512, "br2": 512, "out": 256}


def _adamw(w, g, m, v, name, tr):
    rows, cols = w.shape
    tr = min(tr, rows)

    def body(w_ref, g_ref, m_ref, v_ref, go_ref, d_ref, nm_ref, nv_ref):
        gv = g_ref[...]
        go_ref[...] = gv
        nm = ADAM_B1 * m_ref[...] + (1.0 - ADAM_B1) * gv
        nv = ADAM_B2 * v_ref[...] + (1.0 - ADAM_B2) * (gv * gv)
        nm_ref[...] = nm
        nv_ref[...] = nv
        m_hat = nm / (1.0 - ADAM_B1 ** ADAM_STEP)
        v_hat = nv / (1.0 - ADAM_B2 ** ADAM_STEP)
        d_ref[...] = -ADAM_LR * (m_hat / (jnp.sqrt(v_hat) + ADAM_EPS) + ADAM_WD * w_ref[...])

    blk = pl.BlockSpec((tr, cols), lambda i: (i, 0))
    shape = jax.ShapeDtypeStruct((rows, cols), F32)
    return pl.pallas_call(
        body,
        name=f"adamw_{name}",
        grid=(rows // tr,),
        in_specs=[blk] * 4,
        out_specs=[blk] * 4,
        out_shape=[shape] * 4,
        compiler_params=_params(("parallel",)),
    )(w, g, m, v)


RING_ROWS = 112
RING_READS = 3
RING_WRITES = 2


def _adamw_ring(w, g, m, v, name):
    rows, cols = w.shape
    n = rows // RING_ROWS
    assert n * RING_ROWS == rows and n >= RING_READS

    def body(w_hbm, g_hbm, m_hbm, v_hbm, go_hbm, d_hbm, nm_hbm, nv_hbm, ibuf, obuf, isem, osem):
        ins = (w_hbm, g_hbm, m_hbm, v_hbm)
        outs = (go_hbm, d_hbm, nm_hbm, nv_hbm)

        def read(k, i, slot):
            return pltpu.make_async_copy(ins[k].at[pl.ds(i * RING_ROWS, RING_ROWS)], ibuf.at[k, slot], isem.at[k, slot])

        def write(k, i, slot):
            return pltpu.make_async_copy(obuf.at[k, slot], outs[k].at[pl.ds(i * RING_ROWS, RING_ROWS)], osem.at[k, slot])

        for i in range(RING_READS):
            for k in range(4):
                read(k, i, i).start()

        def step(i, carry):
            slot = i % RING_READS
            oslot = i % RING_WRITES
            for k in range(4):
                read(k, i, slot).wait()

            @pl.when(i >= RING_WRITES)
            def _():
                for k in range(4):
                    write(k, i - RING_WRITES, oslot).wait()

            gv = ibuf[1, slot]
            obuf[0, oslot] = gv
            nm = ADAM_B1 * ibuf[2, slot] + (1.0 - ADAM_B1) * gv
            nv = ADAM_B2 * ibuf[3, slot] + (1.0 - ADAM_B2) * (gv * gv)
            obuf[2, oslot] = nm
            obuf[3, oslot] = nv
            m_hat = nm / (1.0 - ADAM_B1 ** ADAM_STEP)
            v_hat = nv / (1.0 - ADAM_B2 ** ADAM_STEP)
            obuf[1, oslot] = -ADAM_LR * (m_hat / (jnp.sqrt(v_hat) + ADAM_EPS) + ADAM_WD * ibuf[0, slot])
            for k in range(4):
                write(k, i, oslot).start()

            @pl.when(i + RING_READS < n)
            def _():
                for k in range(4):
                    read(k, i + RING_READS, slot).start()
            return carry

        lax.fori_loop(0, n, step, 0)
        for i in range(n - RING_WRITES, n):
            for k in range(4):
                write(k, i, i % RING_WRITES).wait()

    any_space = pl.BlockSpec(memory_space=pl.ANY)
    shape = jax.ShapeDtypeStruct((rows, cols), F32)
    return pl.pallas_call(
        body,
        name=f"adamw_{name}",
        in_specs=[any_space] * 4,
        out_specs=[any_space] * 4,
        out_shape=[shape] * 4,
        scratch_shapes=[pltpu.VMEM((4, RING_READS, RING_ROWS, cols), F32), pltpu.VMEM((4, RING_WRITES, RING_ROWS, cols), F32),
                        pltpu.SemaphoreType.DMA((4, RING_READS)), pltpu.SemaphoreType.DMA((4, RING_WRITES))],
        compiler_params=pltpu.CompilerParams(vmem_limit_bytes=VMEM_LIMIT),
    )(w, g, m, v)


SMALL = (("pre_norm_g", (1, D)), ("post_norm_g", (1, D)), ("mem_norm_g", (1, D)), ("conv_w", (CONV_W, D_RNN)),
         ("conv_b", (1, D_RNN)), ("w_rg_a", (RNN_BLOCKS, LANE, LANE)), ("b_rg_a", (1, D_RNN)),
         ("w_rg_x", (RNN_BLOCKS, LANE, LANE)), ("b_rg_x", (1, D_RNN)), ("lru_lambda", (1, D_RNN)),
         ("swa_sinks", (1, SWA_HEADS)), ("rel_bias", (REL_BUCKETS, SWA_HEADS)))
PACK_ROWS = 2176


def _slot_len(shape):
    return -(-math.prod(shape) // LANE) * LANE


def _pack(values, last_row=None):
    parts = []
    for name, shape in SMALL:
        flat = values[name].reshape(-1).astype(F32)
        parts.append(jnp.pad(flat, (0, _slot_len(shape) - flat.shape[0])))
    flat = jnp.concatenate(parts)
    tail = jnp.zeros((LANE,), F32) if last_row is None else last_row
    return jnp.concatenate([jnp.pad(flat, (0, (PACK_ROWS - 1) * LANE - flat.shape[0])), tail]).reshape(PACK_ROWS, LANE)


def _unpack(pack):
    flat = pack.reshape(-1)
    out, off = {}, 0
    for name, shape in SMALL:
        out[name] = flat[off:off + math.prod(shape)].reshape(shape)
        off += _slot_len(shape)
    return out


TWIN_WEIGHTS = ("pre_norm_g", "post_norm_g", "mem_norm_g", "w_in", "conv_w", "conv_b", "w_rg_a", "b_rg_a", "w_rg_x",
                "b_rg_x", "lru_lambda", "swa_sinks", "rel_bias", "w_mem_kv", "w_br_rg", "w_br_swa", "w_br_mem", "w_out")
BIG = {"w_in": "win", "w_mem_kv": "mk", "w_br_rg": "br0", "w_br_swa": "br1", "w_br_mem": "br2", "w_out": "out"}


def kernel(x, mem, pre_norm_g, post_norm_g, mem_norm_g, w_in, conv_w, conv_b, w_rg_a, b_rg_a, w_rg_x, b_rg_x, lru_lambda, swa_sinks, rel_bias, w_mem_kv, w_br_rg, w_br_swa, w_br_mem, w_out, loss_target, m_pre_norm_g, m_post_norm_g, m_mem_norm_g, m_w_in, m_conv_w, m_conv_b, m_w_rg_a, m_b_rg_a, m_w_rg_x, m_b_rg_x, m_lru_lambda, m_swa_sinks, m_rel_bias, m_w_mem_kv, m_w_br_rg, m_w_br_swa, m_w_br_mem, m_w_out, v_pre_norm_g, v_post_norm_g, v_mem_norm_g, v_w_in, v_conv_w, v_conv_b, v_w_rg_a, v_b_rg_a, v_w_rg_x, v_b_rg_x, v_lru_lambda, v_swa_sinks, v_rel_bias, v_w_mem_kv, v_w_br_rg, v_w_br_swa, v_w_br_mem, v_w_out):
    args = dict(locals())
    out_shapes = {n: args[n].shape for n in TWIN_WEIGHTS}
    w = {n: (args[n] if n == "rel_bias" else args[n][0]) for n in TWIN_WEIGHTS}
    m = {n: (args["m_" + n] if n == "rel_bias" else args["m_" + n][0]) for n in TWIN_WEIGHTS}
    v = {n: (args["v_" + n] if n == "rel_bias" else args["v_" + n][0]) for n in TWIN_WEIGHTS}
    for d in (w, m, v):
        for n, shape in SMALL:
            if n != "conv_w":
                d[n] = d[n].reshape(shape)

    xi, yi, ci = lax.axis_index("x"), lax.axis_index("y"), lax.axis_index("c")
    chip = 2 * xi + yi
    c_arr = ci.astype(jnp.int32).reshape(1)
    zero = jnp.zeros((), jnp.int32)
    cw0 = (chip * (D_RNN // N_CHIPS)).astype(jnp.int32)

    placed = lax.dynamic_update_slice(jnp.zeros((CONV_W, D_RNN), F32), w["conv_w"], (zero, cw0))
    placed = jnp.where(ci == 0, placed, 0.0).reshape(CONV_W * D_RNN // LANE, LANE)
    conv_w_full = _all_reduce_small(placed, "gather_conv_w").reshape(CONV_W, D_RNN)

    for d in (w, m, v):
        d["w_in"] = d["w_in"].T
    chip_row = lambda tbl: lax.dynamic_slice(jnp.asarray(tbl), (chip.astype(jnp.int32), zero), (1, tbl.shape[1]))[0]
    chip_tables = lambda tbl: lax.dynamic_slice(jnp.asarray(tbl), (chip.astype(jnp.int32), zero, zero),
                                                (1,) + tbl.shape[1:])[0]
    odd_arr = yi.astype(jnp.int32).reshape(1)
    big_of = {s: n for n, s in BIG.items()}
    ag, token = {}, conv_w_full
    for stage in GATHER_STAGES:
        behind = c_arr if stage == GATHER_STAGES[0] else token
        placed = {n: (_place_group(w["w_in"], n, chip_tables(_own_block_table(n)), odd_arr, behind) if n in GROUPS
                      else _place_shard(w[big_of[n]], n, behind)) for n in stage}
        send, recv, in_flight, token = _gather_start(placed, token)
        ag[stage] = (send, recv, in_flight)
    h = _rms_fwd(x[0], w["pre_norm_g"], "rms_pre", token)

    def conv_w_in_place(d):
        return dict(d, conv_w=lax.dynamic_update_slice(jnp.zeros((CONV_W, D_RNN), F32), d["conv_w"], (zero, cw0)))

    small_packs = [_pack(conv_w_in_place(d)) for d in (w, m, v)]

    passing = {}

    def prefetch(names, after):
        send, recv, in_flight = ag[names]
        *passing[names], token = _gather_pass_start(_gather_wait(send, recv, in_flight, after), after)
        return token

    def fetch(names, after):
        if names in passing:
            ready = _gather_pass_wait(*passing.pop(names), after)
        else:
            send, recv, in_flight = ag[names]
            ready = _gather_swap(_gather_wait(send, recv, in_flight, after))
        return tuple(ready[n] for n in names)

    rs = {"slots": {}, "halves": {}, "pending": [], "swap": None}

    def emit(grads, after=None):
        assert rs["swap"] is None
        *rs["swap"], token = _swap_start(grads, after)
        return token

    def advance(after):
        grads, received = _swap_wait(*rs["swap"], after)
        rs["swap"] = None
        halves = {n: _add_half(grads[n], received[n], c_arr, n) for n in grads}
        landing = {s: rs["slots"][s] if s in rs["slots"] else lax.empty(_slot_shape(s), BF16)
                   for s in _stage_shards(tuple(grads))}
        send, recv, halves, landing, token = _scatter_start(halves, landing)
        rs["slots"].update(landing)
        rs["pending"].append((send, recv, halves, tuple(landing)))
        return token

    sp = {n: w[n] for n, _ in SMALL}
    sp["conv_w"] = conv_w_full
    sq, grad_x, d_small = _local_step(x[0], h, mem[0], loss_target[0], sp, small_packs, fetch, prefetch, emit, advance)
    small_total = _all_reduce_small(_pack(d_small, sq[0]), "all_reduce_small")
    loss = small_total[PACK_ROWS - 1, 0] * (0.5 / D)

    for send, recv, halves, touched in rs["pending"]:
        halves, landed = _scatter_wait(send, recv, halves, {s: rs["slots"][s] for s in touched}, small_total)
        rs["slots"].update(landed)
        rs["halves"].update(halves)
    pos_arr = jnp.stack([ci, chip]).astype(jnp.int32)
    grp_tbl, blk_tbl = (chip_row(t) for t in _own_partial_tables())
    rest = {s: _sum_slots(rs["slots"][s], rs["halves"][s], pos_arr, s) for s in SHARDS if s != "win"}
    *rest_share, tok = _share_start(rest, small_total)
    win_sum = _sum_slots_win(rs["slots"]["win"], rs["halves"], pos_arr, grp_tbl, blk_tbl)
    *win_share, tok = _share_start({"win": win_sum}, tok)
    sums = _share_wait(*rest_share, tok)

    grad, delta, new_m, new_v = {}, {}, {}, {}
    for n, s in BIG.items():
        if n == "w_in":
            continue
        grad[n], delta[n], new_m[n], new_v[n] = _adamw(w[n], sums[s], m[n], v[n], s, ADAM_ROWS[s])
    g_win = _share_wait(*win_share, delta["w_out"])["win"]
    n = "w_in"
    grad[n], delta[n], new_m[n], new_v[n] = _adamw_ring(w[n], g_win, m[n], v[n], "win")
    for group in (grad, delta, new_m, new_v):
        group["w_in"] = group["w_in"].T
    _, d_, m_, v_ = _adamw(small_packs[0], small_total, small_packs[1], small_packs[2], "small", PACK_ROWS)
    for group, pack in ((grad, small_total), (delta, d_), (new_m, m_), (new_v, v_)):
        group.update(_unpack(pack))
    for group in (grad, delta, new_m, new_v):
        group["conv_w"] = lax.dynamic_slice(group["conv_w"], (zero, cw0), (CONV_W, D_RNN // N_CHIPS))

    outs = [loss, grad_x.reshape(1, S, D)]
    for group in (grad, delta, new_m, new_v):
        outs += [group[n].reshape(out_shapes[n]) for n in TWIN_WEIGHTS]
    return tuple(outs)
```

```python
import math
from typing import NamedTuple

import jax
import jax.numpy as jnp
from jax import lax
from jax.experimental import pallas as pl
from jax.experimental.pallas import tpu as pltpu

F32 = jnp.float32
BF16 = jnp.bfloat16
MESH = pl.DeviceIdType.MESH

S = 2048
D = 2048
MEM = 256
D_RNN = 1024
RNN_BLOCKS = 8
CONV_W = 4
LRU_C = 8.0
SWA_HEADS = 16
SWA_HD = 64
WINDOW = 128
MEM_HEADS = 4
MEM_HD = 256
REL_BUCKETS = 32
REL_MAX_DIST = 128
EPS = 1e-6
NEG_INF = -1e30
LANE = 128
SHARD = 3136
HALF_TILE = 64
N_CHIPS = 4
VMEM_LIMIT = 56 * 1024 * 1024

ADAM_LR = 0.001
ADAM_B1 = 0.9
ADAM_B2 = 0.999
ADAM_EPS = 1e-08
ADAM_WD = 0.01
ADAM_STEP = 10

GROUP_TILES = {"A": 16, "B": 18, "C": 16, "D": 48}
GROUPS = ("A", "B", "C", "D")


def _params(sem=None):
    return pltpu.CompilerParams(dimension_semantics=sem, vmem_limit_bytes=VMEM_LIMIT)


def _sigmoid(v):
    return jax.nn.sigmoid(v)


def _tile_home(t):
    if t < 16:
        return "A", t
    if t < 24:
        return "B", t - 16
    if t < 26:
        return "B", t - 24 + 16
    if t < 34:
        return "B", t - 26 + 8
    if t < 50:
        return "C", t - 34
    return "D", t - 50


def _shard_runs(j):
    runs = []
    per_shard = SHARD // HALF_TILE
    for q in range(per_shard * j, per_shard * (j + 1)):
        g, gt = _tile_home(q // 2)
        row = gt * LANE + (q % 2) * HALF_TILE
        if runs and runs[-1][2] == g and runs[-1][3] + runs[-1][1] == row:
            runs[-1][1] += HALF_TILE
        else:
            runs.append([(q - per_shard * j) * HALF_TILE, HALF_TILE, g, row])
    return [tuple(r) for r in runs]


_DIMS = {
    "nn": (((1,), (0,)), ((), ())),
    "nt": (((1,), (1,)), ((), ())),
    "tn": (((0,), (0,)), ((), ())),
}


def _mm(a, b, mode, out_dtype, tm, tn, tk, name, acc=None, after=None):
    if mode == "nn":
        (m, k), n = a.shape, b.shape[1]
    elif mode == "nt":
        (m, k), n = a.shape, b.shape[0]
    else:
        (k, m), n = a.shape, b.shape[1]
    tm, tn, tk = min(tm, m), min(tn, n), min(tk, k)
    assert m % tm == 0 and n % tn == 0 and k % tk == 0, (name, m, n, k)
    nk = k // tk
    has_acc = acc is not None

    def body(*refs):
        a_ref, b_ref = refs[0], refs[1]
        o_ref = refs[3] if has_acc else refs[2]
        p = lax.dot_general(a_ref[...], b_ref[...], _DIMS[mode], preferred_element_type=F32)

        def finish(v):
            if has_acc:
                v = v + refs[2][...]
            o_ref[...] = v.astype(out_dtype)

        if nk == 1:
            finish(p)
        else:
            s_ref = refs[-1]
            kk = pl.program_id(2)

            @pl.when(kk == 0)
            def _():
                s_ref[...] = p

            @pl.when(kk > 0)
            def _():
                s_ref[...] += p

            @pl.when(kk == nk - 1)
            def _():
                finish(s_ref[...])

    if mode == "nn":
        a_spec = pl.BlockSpec((tm, tk), lambda i, j, kk: (i, kk))
        b_spec = pl.BlockSpec((tk, tn), lambda i, j, kk: (kk, j))
    elif mode == "nt":
        a_spec = pl.BlockSpec((tm, tk), lambda i, j, kk: (i, kk))
        b_spec = pl.BlockSpec((tn, tk), lambda i, j, kk: (j, kk))
    else:
        a_spec = pl.BlockSpec((tk, tm), lambda i, j, kk: (kk, i))
        b_spec = pl.BlockSpec((tk, tn), lambda i, j, kk: (kk, j))
    o_spec = pl.BlockSpec((tm, tn), lambda i, j, kk: (i, j))
    in_specs = [a_spec, b_spec] + ([o_spec] if has_acc else [])
    args = (a, b) + ((acc,) if has_acc else ())
    if after is not None:
        in_specs.append(pl.BlockSpec(memory_space=pl.ANY))
        args += (after,)
    n_in = len(args)
    kernel_body = body

    def body(*refs):
        kernel_body(*(refs[:n_in - (after is not None)] + refs[n_in:]))

    return pl.pallas_call(
        body,
        name=name,
        grid=(m // tm, n // tn, nk),
        in_specs=in_specs,
        out_specs=o_spec,
        out_shape=jax.ShapeDtypeStruct((m, n), out_dtype),
        scratch_shapes=[pltpu.VMEM((tm, tn), F32)] if nk > 1 else [],
        compiler_params=_params(("parallel", "parallel", "arbitrary")),
    )(*args)


def _rms_fwd(x, g, name, after, ts=256):
    r, d = x.shape

    def body(x_ref, g_ref, _, o_ref):
        xv = x_ref[...]
        inv = lax.rsqrt(jnp.mean(xv * xv, axis=-1, keepdims=True) + EPS)
        o_ref[...] = (xv * inv * g_ref[...]).astype(BF16)

    return pl.pallas_call(
        body,
        name=name,
        grid=(r // ts,),
        in_specs=[pl.BlockSpec((ts, d), lambda i: (i, 0)), pl.BlockSpec((1, d), lambda i: (0, 0)),
                  pl.BlockSpec(memory_space=pl.ANY)],
        out_specs=pl.BlockSpec((ts, d), lambda i: (i, 0)),
        out_shape=jax.ShapeDtypeStruct((r, d), BF16),
        compiler_params=_params(("parallel",)),
    )(x, g, after)


def _post_loss(out, x, tgt, g_post, ts=256):
    n = S // ts

    def body(o_ref, x_ref, t_ref, g_ref, sq_ref, dy_ref, do_ref, dg_ref):
        i = pl.program_id(0)

        @pl.when(i == 0)
        def _():
            sq_ref[...] = jnp.zeros_like(sq_ref)
            dg_ref[...] = jnp.zeros_like(dg_ref)

        ov = o_ref[...]
        g = g_ref[...]
        inv = lax.rsqrt(jnp.mean(ov * ov, axis=-1, keepdims=True) + EPS)
        on = ov * inv
        err = x_ref[...] + on * g - t_ref[...]
        sq_ref[...] += jnp.sum(err * err)
        dy = err * (1.0 / D)
        dy_ref[...] = dy.astype(BF16)
        dg_ref[...] += jnp.sum(dy * on, axis=0, keepdims=True)
        don = dy * g
        do_ref[...] = (inv * (don - on * jnp.mean(don * on, axis=-1, keepdims=True))).astype(BF16)

    row = pl.BlockSpec((ts, D), lambda i: (i, 0))
    vec = pl.BlockSpec((1, D), lambda i: (0, 0))
    return pl.pallas_call(
        body,
        name="post_loss",
        grid=(n,),
        in_specs=[row, row, row, vec],
        out_specs=[pl.BlockSpec((8, LANE), lambda i: (0, 0)), row, row, vec],
        out_shape=[
            jax.ShapeDtypeStruct((8, LANE), F32),
            jax.ShapeDtypeStruct((S, D), BF16),
            jax.ShapeDtypeStruct((S, D), BF16),
            jax.ShapeDtypeStruct((1, D), F32),
        ],
        compiler_params=_params(("arbitrary",)),
    )(out, x, tgt, g_post)


def _pre_bwd(dh, x, dy, g_pre, ts=256):
    n = S // ts

    def body(dh_ref, x_ref, dy_ref, g_ref, gx_ref, dg_ref):
        i = pl.program_id(0)

        @pl.when(i == 0)
        def _():
            dg_ref[...] = jnp.zeros_like(dg_ref)

        xv = x_ref[...]
        dhv = dh_ref[...]
        inv = lax.rsqrt(jnp.mean(xv * xv, axis=-1, keepdims=True) + EPS)
        xn = xv * inv
        dg_ref[...] += jnp.sum(dhv * xn, axis=0, keepdims=True)
        dxn = dhv * g_ref[...]
        gx_ref[...] = dy_ref[...].astype(F32) + inv * (dxn - xn * jnp.mean(dxn * xn, axis=-1, keepdims=True))

    row = pl.BlockSpec((ts, D), lambda i: (i, 0))
    vec = pl.BlockSpec((1, D), lambda i: (0, 0))
    return pl.pallas_call(
        body,
        name="pre_bwd",
        grid=(n,),
        in_specs=[row, row, row, vec],
        out_specs=[row, vec],
        out_shape=[jax.ShapeDtypeStruct((S, D), F32), jax.ShapeDtypeStruct((1, D), F32)],
        compiler_params=_params(("arbitrary",)),
    )(dh, x, dy, g_pre)


def _memnorm_bwd(dmemn, mem):
    def body(d_ref, m_ref, dg_ref):
        mv = m_ref[...]
        inv = lax.rsqrt(jnp.mean(mv * mv, axis=-1, keepdims=True) + EPS)
        dg_ref[...] = jnp.sum(d_ref[...] * mv * inv, axis=0, keepdims=True)

    return pl.pallas_call(
        body,
        name="memnorm_bwd",
        out_shape=jax.ShapeDtypeStruct((1, D), F32),
        compiler_params=_params(),
    )(dmemn, mem)


T_RNN = 256


def _neg_expm1(z):
    poly = -z * (1.0 + z * (0.5 + z * (1.0 / 6 + z * (1.0 / 24 + z * (1.0 / 120 + z * (1.0 / 720))))))
    return jnp.where(z > -0.1, poly, 1.0 - jnp.exp(z))


def _softplus_neg(lam):
    return jnp.maximum(-lam, 0.0) + jnp.log1p(jnp.exp(-jnp.abs(lam)))


def _rnn_gates(conv, wa_ref, ba, wx_ref, bx, lam, first_row):
    cbf = conv.astype(BF16)
    ga, gx = [], []
    for n in range(RNN_BLOCKS):
        c_n = cbf[:, n * LANE:(n + 1) * LANE]
        ga.append(jnp.dot(c_n, wa_ref[n], preferred_element_type=F32))
        gx.append(jnp.dot(c_n, wx_ref[n], preferred_element_type=F32))
    gate_r = _sigmoid(jnp.concatenate(ga, axis=1) + ba)
    gate_i = _sigmoid(jnp.concatenate(gx, axis=1) + bx)
    sp = _softplus_neg(lam)
    log_a = -LRU_C * gate_r * sp
    a = jnp.exp(log_a)
    mult_raw = jnp.sqrt(_neg_expm1(2.0 * log_a))
    mult = jnp.where(first_row, 1.0, mult_raw)
    return cbf, gate_r, gate_i, sp, a, mult_raw, mult


def _rglru_fwd(p_a, conv_w, conv_b, wa, ba, wx, bx, lam):
    t = T_RNN
    n = S // t

    def body(xr_ref, g_ref, cw_ref, cb_ref, wa_ref, ba_ref, wx_ref, bx_ref, lam_ref,
             y_ref, h_ref, xp_s, hcar, a_s, b_s):
        i = pl.program_id(0)

        @pl.when(i == 0)
        def _():
            xp_s[0:8, :] = jnp.zeros((8, D_RNN), F32)
            hcar[...] = jnp.zeros_like(hcar)

        @pl.when(i > 0)
        def _():
            xp_s[0:8, :] = xp_s[t:t + 8, :]

        xp_s[8:8 + t, :] = xr_ref[...]
        conv = cb_ref[...]
        for k in range(CONV_W):
            conv = conv + cw_ref[k:k + 1, :] * xp_s[8 - k:8 - k + t, :]
        rows = i * t + lax.broadcasted_iota(jnp.int32, (t, 1), 0)
        _, _, gate_i, _, a, _, mult = _rnn_gates(
            conv, wa_ref, ba_ref[...], wx_ref, bx_ref[...], lam_ref[...], rows == 0)
        a_s[...] = a
        b_s[...] = mult * gate_i * conv

        def step(tt, h):
            h = a_s[pl.ds(tt, 1), :] * h + b_s[pl.ds(tt, 1), :]
            h_ref[pl.ds(tt, 1), :] = h
            return h

        hcar[...] = lax.fori_loop(0, t, step, hcar[...], unroll=8)
        g = g_ref[...]
        y_ref[...] = (h_ref[...] * (g * _sigmoid(g))).astype(BF16)

    blk = lambda c: pl.BlockSpec((t, D_RNN), lambda i: (i, c))
    full = lambda shape: pl.BlockSpec(shape, lambda i: (0,) * len(shape))
    return pl.pallas_call(
        body,
        name="rglru_fwd",
        grid=(n,),
        in_specs=[blk(0), blk(1), full((CONV_W, D_RNN)), full((1, D_RNN)),
                  full((RNN_BLOCKS, LANE, LANE)), full((1, D_RNN)),
                  full((RNN_BLOCKS, LANE, LANE)), full((1, D_RNN)), full((1, D_RNN))],
        out_specs=[blk(0), blk(0)],
        out_shape=[jax.ShapeDtypeStruct((S, D_RNN), BF16), jax.ShapeDtypeStruct((S, D_RNN), F32)],
        scratch_shapes=[pltpu.VMEM((t + 8, D_RNN), F32), pltpu.VMEM((1, D_RNN), F32),
                        pltpu.VMEM((t, D_RNN), F32), pltpu.VMEM((t, D_RNN), F32)],
        compiler_params=_params(("arbitrary",)),
    )(p_a, p_a, conv_w, conv_b, wa, ba, wx, bx, lam)


def _rglru_bwd(dy, p_a, hseq, conv_w, conv_b, wa, ba, wx, bx, lam):
    t = T_RNN
    n = S // t
    rb = t // 8

    def body(dy_ref, xr_ref, g_ref, h_ref, xrp_ref, hp_ref, cw_ref, cb_ref, wa_ref, ba_ref, wx_ref, bx_ref, lam_ref,
             dp_ref, dcw_ref, dcb_ref, dwa_ref, dba_ref, dwx_ref, dbx_ref, dlam_ref,
             xp_s, hp_s, dxp_s, lamcar, a_s, dh_s, lam_s):
        i = pl.program_id(0)
        r = n - 1 - i

        @pl.when(i == 0)
        def _():
            for ref in (dcw_ref, dcb_ref, dwa_ref, dba_ref, dwx_ref, dbx_ref, dlam_ref, lamcar):
                ref[...] = jnp.zeros_like(ref)
            dxp_s[t:t + 8, :] = jnp.zeros((8, D_RNN), F32)

        @pl.when(i > 0)
        def _():
            dxp_s[t:t + 8, :] = dxp_s[0:8, :]

        has_prev = r > 0
        xp_s[0:8, :] = jnp.where(has_prev, xrp_ref[...], 0.0)
        xp_s[8:8 + t, :] = xr_ref[...]
        hp_s[0:8, :] = jnp.where(has_prev, hp_ref[...], 0.0)
        hp_s[8:8 + t, :] = h_ref[...]
        xs = [xp_s[8 - k:8 - k + t, :] for k in range(CONV_W)]
        conv = cb_ref[...]
        for k in range(CONV_W):
            conv = conv + cw_ref[k:k + 1, :] * xs[k]
        rows = r * t + lax.broadcasted_iota(jnp.int32, (t, 1), 0)
        first = rows == 0
        lam_p = lam_ref[...]
        cbf, gate_r, gate_i, sp, a, mult_raw, mult = _rnn_gates(
            conv, wa_ref, ba_ref[...], wx_ref, bx_ref[...], lam_p, first)

        g = g_ref[...]
        sg = _sigmoid(g)
        dyv = dy_ref[...]
        a_s[...] = a
        dh_s[...] = dyv * (g * sg)
        dg = dyv * h_ref[...] * (sg * (1.0 + g * (1.0 - sg)))

        def step(jj, car):
            tt = t - 1 - jj
            lm = dh_s[pl.ds(tt, 1), :] + car
            lam_s[pl.ds(tt, 1), :] = lm
            return a_s[pl.ds(tt, 1), :] * lm

        lamcar[...] = lax.fori_loop(0, t, step, lamcar[...], unroll=8)
        db = lam_s[...]
        da = db * hp_s[7:7 + t, :]
        dmult = db * gate_i * conv
        dgate_i = db * mult * conv
        dconv = db * mult * gate_i
        dlog_a = da * a + jnp.where(first, 0.0, dmult * (-(a * a) / mult_raw))
        dgate_r = dlog_a * (-LRU_C * sp)
        dsp = jnp.sum(dlog_a * (-LRU_C * gate_r), axis=0, keepdims=True)
        dlam_ref[...] += dsp * (-_sigmoid(-lam_p))
        dga = dgate_r * gate_r * (1.0 - gate_r)
        dgx = dgate_i * gate_i * (1.0 - gate_i)
        dba_ref[...] += jnp.sum(dga, axis=0, keepdims=True)
        dbx_ref[...] += jnp.sum(dgx, axis=0, keepdims=True)
        dga16, dgx16 = dga.astype(BF16), dgx.astype(BF16)
        back = []
        for nb in range(RNN_BLOCKS):
            sl = slice(nb * LANE, (nb + 1) * LANE)
            dwa_ref[nb] += lax.dot_general(cbf[:, sl], dga16[:, sl], _DIMS["tn"], preferred_element_type=F32)
            dwx_ref[nb] += lax.dot_general(cbf[:, sl], dgx16[:, sl], _DIMS["tn"], preferred_element_type=F32)
            back.append(lax.dot_general(dga16[:, sl], wa_ref[nb], _DIMS["nt"], preferred_element_type=F32)
                        + lax.dot_general(dgx16[:, sl], wx_ref[nb], _DIMS["nt"], preferred_element_type=F32))
        dconv = dconv + jnp.concatenate(back, axis=1)
        dcb_ref[...] += jnp.sum(dconv, axis=0, keepdims=True)
        for k in range(CONV_W):
            dcw_ref[k:k + 1, :] += jnp.sum(dconv * xs[k], axis=0, keepdims=True)
        dxp_s[0:t, :] = dconv
        dxr = cw_ref[0:1, :] * dconv
        for k in range(1, CONV_W):
            dxr = dxr + cw_ref[k:k + 1, :] * dxp_s[k:k + t, :]
        dp_ref[:, 0:D_RNN] = dxr.astype(BF16)
        dp_ref[:, D_RNN:2 * D_RNN] = dg.astype(BF16)

    blk = lambda c: pl.BlockSpec((t, D_RNN), lambda i: (n - 1 - i, c))
    prev8 = pl.BlockSpec((8, D_RNN), lambda i: (jnp.maximum((n - 1 - i) * rb - 1, 0), 0))
    full = lambda shape: pl.BlockSpec(shape, lambda i: (0,) * len(shape))
    vec = full((1, D_RNN))
    mat = full((RNN_BLOCKS, LANE, LANE))
    return pl.pallas_call(
        body,
        name="rglru_bwd",
        grid=(n,),
        in_specs=[blk(0), blk(0), blk(1), blk(0), prev8, prev8,
                  full((CONV_W, D_RNN)), vec, mat, vec, mat, vec, vec],
        out_specs=[pl.BlockSpec((t, 2 * D_RNN), lambda i: (n - 1 - i, 0)),
                   full((CONV_W, D_RNN)), vec, mat, vec, mat, vec, vec],
        out_shape=[jax.ShapeDtypeStruct((S, 2 * D_RNN), BF16),
                   jax.ShapeDtypeStruct((CONV_W, D_RNN), F32), jax.ShapeDtypeStruct((1, D_RNN), F32),
                   jax.ShapeDtypeStruct((RNN_BLOCKS, LANE, LANE), F32), jax.ShapeDtypeStruct((1, D_RNN), F32),
                   jax.ShapeDtypeStruct((RNN_BLOCKS, LANE, LANE), F32), jax.ShapeDtypeStruct((1, D_RNN), F32),
                   jax.ShapeDtypeStruct((1, D_RNN), F32)],
        scratch_shapes=[pltpu.VMEM((t + 8, D_RNN), F32), pltpu.VMEM((t + 8, D_RNN), F32),
                        pltpu.VMEM((t + 8, D_RNN), F32), pltpu.VMEM((1, D_RNN), F32),
                        pltpu.VMEM((t, D_RNN), F32), pltpu.VMEM((t, D_RNN), F32), pltpu.VMEM((t, D_RNN), F32)],
        compiler_params=_params(("arbitrary",)),
    )(dy, p_a, p_a, hseq, p_a, hseq, conv_w, conv_b, wa, ba, wx, bx, lam)


QB = WINDOW
KB2 = 2 * WINDOW
N_QB = S // QB
N_PAIR = SWA_HEADS // 2


def _swa_keys(kvc_ref, kvp_ref):
    kk = jnp.concatenate([kvp_ref[:, 0:LANE], kvc_ref[:, 0:LANE]], axis=0)
    vv = jnp.concatenate([kvp_ref[:, LANE:2 * LANE], kvc_ref[:, LANE:2 * LANE]], axis=0)
    lo = lax.broadcasted_iota(jnp.int32, (1, LANE), 1) < SWA_HD
    kk_sw, vv_sw = pltpu.roll(kk, SWA_HD, 1), pltpu.roll(vv, SWA_HD, 1)
    kd = [jnp.where(lo, kk, kk_sw).astype(BF16), jnp.where(lo, kk_sw, kk).astype(BF16)]
    vd = [jnp.where(lo, vv, vv_sw).astype(BF16), jnp.where(lo, vv_sw, vv).astype(BF16)]
    return lo, kd, vd


GRP = SWA_HEADS // 2
STACK = GRP
GQ = STACK * QB


def _swa_valid(n, rows):
    qi = lax.broadcasted_iota(jnp.int32, (rows, KB2), 0) % QB
    kj = lax.broadcasted_iota(jnp.int32, (rows, KB2), 1)
    dist = qi + WINDOW - kj
    return (dist >= 0) & (dist < WINDOW) & ((n > 0) | (kj >= WINDOW))


def _swa_stack(tile_of, lo, h0, masked):
    parts = []
    for h in range(h0, h0 + STACK):
        t = tile_of(h // 2)
        if masked:
            t = jnp.where(lo if h % 2 == 0 else jnp.logical_not(lo), t, 0.0)
        parts.append(t)
    return jnp.concatenate(parts, axis=0)


def _swa_unstack(stacked, lo, pair):
    return jnp.where(lo, stacked[2 * pair * QB:(2 * pair + 1) * QB], stacked[(2 * pair + 1) * QB:(2 * pair + 2) * QB])


def _swa_softmax(lg, sink, valid):
    lg = jnp.where(valid, lg, NEG_INF)
    m = jnp.maximum(jnp.max(lg, axis=-1, keepdims=True), sink)
    p = jnp.exp(lg - m)
    es = jnp.exp(sink - m)
    den = jnp.sum(p, axis=-1, keepdims=True) + es
    return p / den, es / den


def _swa_probs_head(qh16, kd, bias, sink, valid):
    lg = lax.dot_general(qh16, kd, _DIMS["nt"], preferred_element_type=F32) * (SWA_HD ** -0.5) + bias
    return _swa_softmax(lg, sink, valid)[0]


def _swa_probs(q16, kd, bias_ref, sink_ref, h0, valid):
    bias = bias_ref[h0:h0 + STACK].reshape(GQ, KB2)
    sink = jnp.concatenate([jnp.full((QB, 1), sink_ref[h], F32) for h in range(h0, h0 + STACK)], axis=0)
    lg = lax.dot_general(q16, kd, _DIMS["nt"], preferred_element_type=F32) * (SWA_HD ** -0.5) + bias
    return _swa_softmax(lg, sink, valid)


def _swa_specs():
    q = pl.BlockSpec((QB, D_RNN), lambda n: (n, 0))
    g = pl.BlockSpec((QB, D_RNN), lambda n: (n, 1))
    kvc = pl.BlockSpec((QB, 2 * LANE), lambda n: (n, 8))
    kvp = pl.BlockSpec((QB, 2 * LANE), lambda n: (jnp.maximum(n - 1, 0), 8))
    bias = pl.BlockSpec((SWA_HEADS, QB, KB2), lambda n: (0, 0, 0))
    sinks = pl.BlockSpec(memory_space=pltpu.SMEM)
    return q, g, kvc, kvp, bias, sinks


def _swa_fwd(p_b, bias_t, sinks):
    def body(q_ref, g_ref, kvc_ref, kvp_ref, bias_ref, sink_ref, y_ref, o_ref):
        n = pl.program_id(0)
        lo, kd, vd = _swa_keys(kvc_ref, kvp_ref)
        valid = _swa_valid(n, QB)
        for hp in range(N_PAIR):
            sl = slice(hp * LANE, (hp + 1) * LANE)
            kvh = hp // (N_PAIR // 2)
            q = q_ref[:, sl]
            outs = []
            for j in range(2):
                qh16 = jnp.where(lo if j == 0 else jnp.logical_not(lo), q, 0.0).astype(BF16)
                probs = _swa_probs_head(qh16, kd[kvh], bias_ref[2 * hp + j], sink_ref[2 * hp + j], valid)
                outs.append(jnp.dot(probs.astype(BF16), vd[kvh], preferred_element_type=F32))
            o = jnp.where(lo, outs[0], outs[1])
            o_ref[:, sl] = o
            g = g_ref[:, sl]
            y_ref[:, sl] = (o * (g * _sigmoid(g))).astype(BF16)

    q, g, kvc, kvp, bias, sinks_spec = _swa_specs()
    out = pl.BlockSpec((QB, D_RNN), lambda n: (n, 0))
    return pl.pallas_call(
        body,
        name="swa_fwd",
        grid=(N_QB,),
        in_specs=[q, g, kvc, kvp, bias, sinks_spec],
        out_specs=[out, out],
        out_shape=[jax.ShapeDtypeStruct((S, D_RNN), BF16), jax.ShapeDtypeStruct((S, D_RNN), F32)],
        compiler_params=_params(("parallel",)),
    )(p_b, p_b, p_b, p_b, bias_t, sinks)


def _swa_bwd(dy, p_b, o_swa, bias_t, sinks, after=None):
    def body(dy_ref, q_ref, g_ref, kvc_ref, kvp_ref, o_ref, bias_ref, sink_ref, *rest):
        dp_ref, dk_ref, dv_ref, dbias_ref, dsink_ref, do_s = rest[-6:]
        n = pl.program_id(0)

        @pl.when(n == 0)
        def _():
            for ref in (dk_ref, dv_ref, dbias_ref, dsink_ref):
                ref[...] = jnp.zeros_like(ref)

        lo, kd, vd = _swa_keys(kvc_ref, kvp_ref)
        hi = jnp.logical_not(lo)
        valid = _swa_valid(n, GQ)
        tile = lambda ref: (lambda hp: ref[:, hp * LANE:(hp + 1) * LANE])
        for hp in range(N_PAIR):
            sl = slice(hp * LANE, (hp + 1) * LANE)
            g, dyv = g_ref[:, sl], dy_ref[:, sl]
            sg = _sigmoid(g)
            do_s[:, sl] = dyv * (g * sg)
            dp_ref[:, D_RNN + hp * LANE:D_RNN + (hp + 1) * LANE] = (
                dyv * o_ref[:, sl] * (sg * (1.0 + g * (1.0 - sg)))).astype(BF16)

        dk_blk = jnp.zeros((KB2, LANE), F32)
        dv_blk = jnp.zeros((KB2, LANE), F32)
        for h0 in range(0, SWA_HEADS, STACK):
            kvh = h0 // GRP
            q16 = _swa_stack(tile(q_ref), lo, h0, masked=True).astype(BF16)
            do8 = _swa_stack(tile(do_s), lo, h0, masked=True)
            do16 = do8.astype(BF16)
            delta = jnp.sum(do8 * _swa_stack(tile(o_ref), lo, h0, masked=False), axis=-1, keepdims=True)
            probs, psink = _swa_probs(q16, kd[kvh], bias_ref, sink_ref, h0, valid)
            dpr = lax.dot_general(do16, vd[kvh], _DIMS["nt"], preferred_element_type=F32)
            ds = probs * (dpr - delta)
            sink_term = psink * delta
            for g in range(STACK):
                h, rows = h0 + g, slice(g * QB, (g + 1) * QB)
                dbias_ref[h] += ds[rows]
                dsink_ref[h:h + 1, :] += jnp.zeros((1, LANE), F32) - jnp.sum(sink_term[rows])
            ds16 = (ds * (SWA_HD ** -0.5)).astype(BF16)
            dq_all = jnp.dot(ds16, kd[kvh], preferred_element_type=F32)
            for pair in range(STACK // 2):
                sl = slice((h0 // 2 + pair) * LANE, (h0 // 2 + pair + 1) * LANE)
                dp_ref[:, sl] = _swa_unstack(dq_all, lo, pair).astype(BF16)
            dk_pair = lax.dot_general(ds16, q16, _DIMS["tn"], preferred_element_type=F32)
            dv_pair = lax.dot_general(probs.astype(BF16), do16, _DIMS["tn"], preferred_element_type=F32)
            keep = lo if kvh == 0 else hi
            dk_blk = dk_blk + jnp.where(keep, dk_pair + pltpu.roll(dk_pair, SWA_HD, 1), 0.0)
            dv_blk = dv_blk + jnp.where(keep, dv_pair + pltpu.roll(dv_pair, SWA_HD, 1), 0.0)

        cur = pl.ds(pl.multiple_of(n * QB, QB), QB)
        dk_ref[cur, :] += dk_blk[QB:KB2]
        dv_ref[cur, :] += dv_blk[QB:KB2]

        @pl.when(n > 0)
        def _():
            prev = pl.ds(pl.multiple_of((n - 1) * QB, QB), QB)
            dk_ref[prev, :] += dk_blk[0:QB]
            dv_ref[prev, :] += dv_blk[0:QB]

    q, g, kvc, kvp, bias, sinks_spec = _swa_specs()
    row = pl.BlockSpec((QB, D_RNN), lambda n: (n, 0))
    acc = pl.BlockSpec((S, LANE), lambda n: (0, 0))
    return pl.pallas_call(
        body,
        name="swa_bwd",
        grid=(N_QB,),
        in_specs=[row, q, g, kvc, kvp, row, bias, sinks_spec] + ([ANY] if after is not None else []),
        out_specs=[pl.BlockSpec((QB, 2 * D_RNN), lambda n: (n, 0)), acc, acc, bias,
                   pl.BlockSpec((SWA_HEADS, LANE), lambda n: (0, 0))],
        out_shape=[jax.ShapeDtypeStruct((S, GROUP_TILES["B"] * LANE), BF16),
                   jax.ShapeDtypeStruct((S, LANE), F32), jax.ShapeDtypeStruct((S, LANE), F32),
                   jax.ShapeDtypeStruct((SWA_HEADS, QB, KB2), F32),
                   jax.ShapeDtypeStruct((SWA_HEADS, LANE), F32)],
        scratch_shapes=[pltpu.VMEM((QB, D_RNN), F32)],
        compiler_params=_params(("arbitrary",)),
    )(dy, p_b, p_b, p_b, p_b, o_swa, bias_t, sinks, *([after] if after is not None else []))


def _swa_pack(dp_b, dk, dv, ts=512):
    def body(_, dk_ref, dv_ref, o_ref):
        o_ref[:, 0:LANE] = dk_ref[...].astype(BF16)
        o_ref[:, LANE:2 * LANE] = dv_ref[...].astype(BF16)

    tile = pl.BlockSpec((ts, LANE), lambda i: (i, 0))
    return pl.pallas_call(
        body,
        name="swa_pack",
        grid=(S // ts,),
        in_specs=[pl.BlockSpec(memory_space=pl.ANY), tile, tile],
        out_specs=pl.BlockSpec((ts, 2 * LANE), lambda i: (i, 8)),
        out_shape=jax.ShapeDtypeStruct(dp_b.shape, dp_b.dtype),
        input_output_aliases={0: 0},
        compiler_params=_params(("parallel",)),
    )(dp_b, dk, dv)


def _split3(v):
    a = v.astype(BF16)
    r = v - a.astype(F32)
    b = r.astype(BF16)
    c = (r - b.astype(F32)).astype(BF16)
    return a, b, c


def _relbias_grad(dbias_flat, onehot_t):
    def body(d_ref, e_ref, o_ref):
        e = e_ref[...]
        acc = jnp.zeros((SWA_HEADS, REL_BUCKETS), F32)
        for term in _split3(d_ref[...]):
            acc = acc + lax.dot_general(term, e, _DIMS["nt"], preferred_element_type=F32)
        o_ref[...] = acc

    return pl.pallas_call(
        body,
        name="relbias_grad",
        out_shape=jax.ShapeDtypeStruct((SWA_HEADS, REL_BUCKETS), F32),
        compiler_params=_params(),
    )(dbias_flat, onehot_t)


TS_MEM = 512


def _mem_probs(q16, mk):
    lg = lax.dot_general(q16, mk, _DIMS["nt"], preferred_element_type=F32) * (MEM_HD ** -0.5)
    p = jnp.exp(lg - jnp.max(lg, axis=-1, keepdims=True))
    return p / jnp.sum(p, axis=-1, keepdims=True)


def _mem_fwd(p_c, mkv):
    def body(q_ref, g_ref, mkv_ref, y_ref, o_ref):
        for hm in range(MEM_HEADS):
            sl = slice(hm * MEM_HD, (hm + 1) * MEM_HD)
            probs = _mem_probs(q_ref[:, sl].astype(BF16), mkv_ref[:, sl])
            o = jnp.dot(probs.astype(BF16), mkv_ref[:, D_RNN + hm * MEM_HD:D_RNN + (hm + 1) * MEM_HD],
                        preferred_element_type=F32)
            o_ref[:, sl] = o
            g = g_ref[:, sl]
            y_ref[:, sl] = (o * (g * _sigmoid(g))).astype(BF16)

    blk = lambda c: pl.BlockSpec((TS_MEM, D_RNN), lambda i: (i, c))
    return pl.pallas_call(
        body,
        name="mem_fwd",
        grid=(S // TS_MEM,),
        in_specs=[blk(0), blk(1), pl.BlockSpec((MEM, 2 * D_RNN), lambda i: (0, 0))],
        out_specs=[blk(0), blk(0)],
        out_shape=[jax.ShapeDtypeStruct((S, D_RNN), BF16), jax.ShapeDtypeStruct((S, D_RNN), F32)],
        compiler_params=_params(("parallel",)),
    )(p_c, p_c, mkv)


def _mem_bwd(dy, p_c, o_mem, mkv):
    def body(dy_ref, q_ref, g_ref, o_ref, mkv_ref, dp_ref, dmkv_ref):
        @pl.when(pl.program_id(0) == 0)
        def _():
            dmkv_ref[...] = jnp.zeros_like(dmkv_ref)

        for hm in range(MEM_HEADS):
            sl = slice(hm * MEM_HD, (hm + 1) * MEM_HD)
            sv = slice(D_RNN + hm * MEM_HD, D_RNN + (hm + 1) * MEM_HD)
            q16 = q_ref[:, sl].astype(BF16)
            mk, mv = mkv_ref[:, sl], mkv_ref[:, sv]
            probs = _mem_probs(q16, mk)
            g, o, dyv = g_ref[:, sl], o_ref[:, sl], dy_ref[:, sl]
            sg = _sigmoid(g)
            do = dyv * (g * sg)
            dp_ref[:, sv] = (dyv * o * (sg * (1.0 + g * (1.0 - sg)))).astype(BF16)
            do16 = do.astype(BF16)
            delta = jnp.sum(do * o, axis=-1, keepdims=True)
            dpr = lax.dot_general(do16, mv, _DIMS["nt"], preferred_element_type=F32)
            ds16 = (probs * (dpr - delta) * (MEM_HD ** -0.5)).astype(BF16)
            dp_ref[:, sl] = jnp.dot(ds16, mk, preferred_element_type=F32).astype(BF16)
            dmkv_ref[:, sl] += lax.dot_general(ds16, q16, _DIMS["tn"], preferred_element_type=F32)
            dmkv_ref[:, sv] += lax.dot_general(probs.astype(BF16), do16, _DIMS["tn"], preferred_element_type=F32)

    blk = lambda c: pl.BlockSpec((TS_MEM, D_RNN), lambda i: (i, c))
    kv = pl.BlockSpec((MEM, 2 * D_RNN), lambda i: (0, 0))
    return pl.pallas_call(
        body,
        name="mem_bwd",
        grid=(S // TS_MEM,),
        in_specs=[blk(0), blk(0), blk(1), blk(0), kv],
        out_specs=[pl.BlockSpec((TS_MEM, 2 * D_RNN), lambda i: (i, 0)), kv],
        out_shape=[jax.ShapeDtypeStruct((S, 2 * D_RNN), BF16), jax.ShapeDtypeStruct((MEM, 2 * D_RNN), F32)],
        compiler_params=_params(("arbitrary",)),
    )(dy, p_c, p_c, o_mem, mkv)


TS_MRG = 512
TD_MRG = 1024
N_DBLK = D // TD_MRG


def _merge_fwd(z, p_d):
    def body(z0, z1, z2, g0, g1, g2, o_ref):
        term = lambda g, z: _sigmoid(g[...].astype(F32)) * z[...].astype(F32)
        o_ref[...] = (term(g0, z0) + term(g1, z1) + term(g2, z2)).astype(BF16)

    blk = pl.BlockSpec((TS_MRG, TD_MRG), lambda i, d: (i, d))
    gate = lambda b: pl.BlockSpec((TS_MRG, TD_MRG), lambda i, d: (i, b * N_DBLK + d))
    return pl.pallas_call(
        body,
        name="merge_fwd",
        grid=(S // TS_MRG, N_DBLK),
        in_specs=[blk, blk, blk, gate(0), gate(1), gate(2)],
        out_specs=blk,
        out_shape=jax.ShapeDtypeStruct((S, D), BF16),
        compiler_params=_params(("parallel", "parallel")),
    )(z[0], z[1], z[2], p_d, p_d, p_d)


TS_MRG_BWD = 128


def _merge_bwd(dmerged, z, p_d, after):
    def body(dm_ref, z0, z1, z2, g_ref, _, dz0, dz1, dz2, dg_ref):
        dm = dm_ref[...].astype(F32)
        for b, (z_ref, dz_ref) in enumerate(((z0, dz0), (z1, dz1), (z2, dz2))):
            cols = slice(b * D, (b + 1) * D)
            sg = _sigmoid(g_ref[:, cols].astype(F32))
            dz_ref[...] = (dm * sg).astype(BF16)
            dg_ref[:, cols] = (dm * z_ref[...].astype(F32) * sg * (1.0 - sg)).astype(BF16)

    row = pl.BlockSpec((TS_MRG_BWD, D), lambda i: (i, 0))
    wide = pl.BlockSpec((TS_MRG_BWD, 3 * D), lambda i: (i, 0))
    outs = pl.pallas_call(
        body,
        name="merge_bwd",
        grid=(S // TS_MRG_BWD,),
        in_specs=[row, row, row, row, wide, pl.BlockSpec(memory_space=pl.ANY)],
        out_specs=[row, row, row, wide],
        out_shape=[jax.ShapeDtypeStruct((S, D), BF16)] * 3 + [jax.ShapeDtypeStruct((S, 3 * D), BF16)],
        compiler_params=_params(("parallel",)),
    )(dmerged, z[0], z[1], z[2], p_d, after)
    return list(outs[:3]), outs[3]


def _bucket_table():
    import numpy as np
    qi = np.arange(QB)[:, None]
    kj = np.arange(KB2)[None, :]
    n = np.maximum(qi + WINDOW - kj, 0)
    max_exact = REL_BUCKETS // 2
    ratio = np.log(np.maximum(n, 1).astype(np.float32) / max_exact) / np.float32(math.log(REL_MAX_DIST / max_exact))
    large = np.minimum(max_exact + (ratio * (REL_BUCKETS - max_exact)).astype(np.int32), REL_BUCKETS - 1)
    bucket = np.where(n < max_exact, n, large).reshape(1, QB * KB2)
    return (bucket == np.arange(REL_BUCKETS)[:, None]).astype(np.float32)


def _bias_expand(rel_bias_t, onehot_t):
    def body(r_ref, e_ref, o_ref):
        e = e_ref[...]
        acc = jnp.zeros((SWA_HEADS, QB * KB2), F32)
        for term in _split3(r_ref[...]):
            acc = acc + jnp.dot(term, e, preferred_element_type=F32)
        o_ref[...] = acc

    return pl.pallas_call(
        body,
        name="bias_expand",
        out_shape=jax.ShapeDtypeStruct((SWA_HEADS, QB * KB2), F32),
        compiler_params=_params(),
    )(rel_bias_t, onehot_t)


PROJ_TN = {"A": 1024, "B": 1152, "C": 1024, "D": 1536}


def _do_first(arrays, token):
    def body(*refs):
        refs[-1][...] = jnp.zeros_like(refs[-1])

    return pl.pallas_call(
        body,
        name="do_first",
        in_specs=[pl.BlockSpec(memory_space=pl.ANY)] * (len(arrays) + 1),
        out_specs=pl.BlockSpec(memory_space=pltpu.VMEM),
        out_shape=jax.ShapeDtypeStruct((8, LANE), F32),
    )(*arrays, token)


def _local_step(x, h, mem, tgt, sp, early, fetch, prefetch, emit, advance):
    onehot_t = jnp.asarray(_bucket_table(), BF16)
    bias_t = _bias_expand(sp["rel_bias"].T, onehot_t).reshape(SWA_HEADS, QB, KB2)
    sinks = sp["swa_sinks"].reshape(SWA_HEADS)
    wa16, wx16 = sp["w_rg_a"].astype(BF16), sp["w_rg_x"].astype(BF16)
    rnn = (sp["conv_w"], sp["conv_b"], wa16, sp["b_rg_a"], wx16, sp["b_rg_x"], sp["lru_lambda"])

    memn = _rms_fwd(mem, sp["mem_norm_g"], "rms_mem", h)
    h_and_prep = _do_first([bias_t, memn, wa16, wx16, *early], h)
    w_grp, p = {}, {}

    def project(g, after, then=None):
        (w_grp[g],) = fetch((g,), after)
        tok = prefetch(then, w_grp[g]) if then is not None else None
        p[g] = _mm(h, w_grp[g], "nt", BF16 if g == "D" else F32, 1024, PROJ_TN[g], D, f"proj_{g}", after=tok)

    project("A", h_and_prep)
    y_rg, hseq = _rglru_fwd(p["A"], *rnn)
    project("B", y_rg)
    y_swa, o_swa = _swa_fwd(p["B"], bias_t, sinks)
    project("C", y_swa, then=("mk",))
    (wmk,) = fetch(("mk",), p["C"])
    tok = prefetch(("br0", "br1", "br2"), wmk)
    mkv = _mm(memn, wmk, "nn", BF16, MEM, 1024, D, "mkv", after=tok)
    y_mem, o_mem = _mem_fwd(p["C"], mkv)
    ys = (y_rg, y_swa, y_mem)
    wbr = fetch(("br0", "br1", "br2"), y_mem)
    tok = prefetch(("D",), wbr[2])
    z = []
    for b in range(3):
        z.append(_mm(ys[b], wbr[b], "nn", BF16, 1024, 1024, D_RNN, f"branch_out{b}", after=z[-1] if z else tok))
    project("D", z[2], then=("out",))
    merged = _merge_fwd(z, p["D"])
    (wout,) = fetch(("out",), merged)
    out = _mm(merged, wout, "nn", F32, 1024, 1024, D, "out_proj")
    sq, dy, dout, d_post = _post_loss(out, x, tgt, sp["post_norm_g"])

    tok = emit({"out": _mm(merged, dout, "tn", BF16, 1024, 1024, S, "d_wout")})
    dmerged = _mm(dout, wout, "nt", BF16, 1024, 1024, D, "d_merged", after=tok)
    tok = advance(dmerged)
    dz, dp_d = _merge_bwd(dmerged, z, p["D"], tok)
    d_win = lambda g, dp_g, after=None: _mm(dp_g, h, "tn", BF16, PROJ_TN[g], 1024, S, f"d_win_{g}", after=after)
    tok = emit({f"br{b}": _mm(ys[b], dz[b], "tn", BF16, 1024, 1024, S, f"d_wbr{b}") for b in range(3)}, tok)
    d_w_d = d_win("D", dp_d, tok)
    tok = emit({"D": d_w_d}, advance(d_w_d))
    dy_mem = _mm(dz[2], wbr[2], "nt", F32, 1024, 1024, D, "d_branch2", after=tok)
    tok = advance(dy_mem)
    dp_c, dmkv = _mem_bwd(dy_mem, p["C"], o_mem, mkv)
    dmkv16 = dmkv.astype(BF16)
    tok = emit({"mk": _mm(memn, dmkv16, "tn", BF16, 1024, 1024, MEM, "d_wmk", after=tok), "C": d_win("C", dp_c)}, tok)
    dmemn = _mm(dmkv16, wmk, "nt", F32, MEM, 1024, D, "d_memn", after=tok)
    tok = advance(dmemn)
    d_memg = _memnorm_bwd(dmemn, mem)
    dy_rg = _mm(dz[0], wbr[0], "nt", F32, 1024, 1024, D, "d_branch0", after=tok)
    dp_a, d_cw, d_cb, d_wa, d_ba, d_wx, d_bx, d_lam = _rglru_bwd(dy_rg, p["A"], hseq, *rnn)
    tok = emit({"A": d_win("A", dp_a)}, tok)
    dy_swa = _mm(dz[1], wbr[1], "nt", F32, 1024, 1024, D, "d_branch1", after=tok)
    tok = advance(dy_swa)
    dp_b, dk, dv, d_bias, d_sink = _swa_bwd(dy_swa, p["B"], o_swa, bias_t, sinks, after=tok)
    dp_b = _swa_pack(dp_b, dk, dv)
    d_rel = _relbias_grad(d_bias.reshape(SWA_HEADS, QB * KB2), onehot_t).T
    dp = {"A": dp_a, "B": dp_b, "C": dp_c, "D": dp_d}
    tok = emit({"B": d_win("B", dp_b)}, tok)
    dh = None
    for g in GROUPS:
        dh = _mm(dp[g], w_grp[g], "nn", F32, 1024, 1024, 2304 if g == "B" else 2048, f"d_h_{g}", acc=dh,
                 after=tok if g in ("A", "B") else None)
        if g == "A":
            tok = advance(dh)
    grad_x, d_pre = _pre_bwd(dh, x, dy, sp["pre_norm_g"])

    d_small = {
        "pre_norm_g": d_pre, "post_norm_g": d_post, "mem_norm_g": d_memg, "conv_w": d_cw, "conv_b": d_cb,
        "w_rg_a": d_wa, "b_rg_a": d_ba, "w_rg_x": d_wx, "b_rg_x": d_bx, "lru_lambda": d_lam,
        "swa_sinks": d_sink[:, 0].reshape(1, SWA_HEADS), "rel_bias": d_rel,
    }
    return sq, grad_x, d_small


ANY = pl.BlockSpec(memory_space=pl.ANY)
SHARD_ROWS = D // N_CHIPS
GATHERED = {"A": (2048, D), "B": (2304, D), "C": (2048, D), "D": (6144, D), "mk": (D, D),
            "br0": (D_RNN, D), "br1": (D_RNN, D), "br2": (D_RNN, D), "out": (D, D)}
SHARD_SHAPES = {"win": (SHARD, D), "mk": (SHARD_ROWS, D), "br0": (D_RNN, SHARD_ROWS), "br1": (D_RNN, SHARD_ROWS),
                "br2": (D_RNN, SHARD_ROWS), "out": (SHARD_ROWS, D)}
SHARDS = tuple(SHARD_SHAPES)
HALF_AXIS = {"win": 1, "mk": 1, "br0": 0, "br1": 0, "br2": 0, "out": 1,
             "A": 1, "B": 1, "C": 1, "D": 1}


def _halved(shape, axis):
    return (shape[0] // 2, shape[1]) if axis == 0 else (shape[0], shape[1] // 2)


class Piece(NamedTuple):
    src: str
    dst: str
    rows: int
    sr0: int
    sc0: int
    dr0: int
    dc0: int
    ncols: int


def _pieces_of(jj):
    out = [Piece("win", g, n, r, 0, gr, 0, D) for r, n, g, gr in _shard_runs(jj)]
    out.append(Piece("mk", "mk", SHARD_ROWS, 0, 0, SHARD_ROWS * jj, 0, D))
    out += [Piece(f"br{b}", f"br{b}", D_RNN, 0, 0, 0, SHARD_ROWS * jj, SHARD_ROWS) for b in range(3)]
    out.append(Piece("out", "out", SHARD_ROWS, 0, 0, SHARD_ROWS * jj, 0, D))
    return out


def _half_rect(ref, p, side, which):
    r0, c0 = (p.sr0, p.sc0) if side == "src" else (p.dr0, p.dc0)
    if HALF_AXIS[p.src] == 1:
        return _rect(ref, r0, p.rows, c0 + which * (p.ncols // 2), p.ncols // 2)
    return _rect(ref, r0 + which * (p.rows // 2), p.rows // 2, c0, p.ncols)


def _rect_in_half(ref, p, side):
    r0, c0 = (p.sr0, p.sc0) if side == "src" else (p.dr0, p.dc0)
    if HALF_AXIS[p.src] == 1:
        return _rect(ref, r0, p.rows, 0, p.ncols // 2)
    return _rect(ref, 0, p.rows // 2, c0, p.ncols)


MAX_PIECES = max(len(_pieces_of(jj)) for jj in range(N_CHIPS))


def _rect(ref, r0, rows, c0, ncols):
    return ref.at[pl.ds(r0, rows), pl.ds(c0, ncols)]


def _position():
    x, y, c = lax.axis_index("x"), lax.axis_index("y"), lax.axis_index("c")
    return x, y, c, 2 * x + y


HBM = pl.BlockSpec(memory_space=pltpu.HBM)
SEM = pl.BlockSpec(memory_space=pltpu.SEMAPHORE)
EFFECT = pltpu.SideEffectType.DATAFLOW_SIDE_EFFECTING
N_SEM = MAX_PIECES * N_CHIPS
GATHER_STAGES = (("A",), ("B",), ("C",), ("mk",), ("br0", "br1", "br2"), ("D",), ("out",))


def _in_hbm(a):
    return pltpu.with_memory_space_constraint(a, pltpu.HBM)


def _stage_pieces(jj, stage):
    return [(i, p) for i, p in enumerate(_pieces_of(jj)) if p.dst in stage]


def _own_block_table(g):
    import numpy as np
    units = np.full((N_CHIPS, GATHERED[g][0] // HALF_TILE), -1, np.int64)
    for jj in range(N_CHIPS):
        for r, n, grp, gr in _shard_runs(jj):
            if grp == g:
                for k in range(n // HALF_TILE):
                    units[jj, gr // HALF_TILE + k] = r // HALF_TILE + k
    tbl = np.zeros((N_CHIPS, 2, GATHERED[g][0] // LANE), np.int32)
    for jj in range(N_CHIPS):
        for b in range(tbl.shape[2]):
            first, second = units[jj, 2 * b], units[jj, 2 * b + 1]
            if jj % 2 == 0:
                src = first if first >= 0 else second - 1
                if first >= 0 or second >= 0:
                    assert src % 2 == 0
                    tbl[jj, :, b] = src // 2
            else:
                if first >= 0:
                    assert first % 2 == 1
                    tbl[jj, 0, b] = first // 2
                if second >= 0:
                    assert second % 2 == 0
                    tbl[jj, 1, b] = second // 2
    return tbl


def _place_group(w_t, g, tables, odd_arr, after):
    nb = GATHERED[g][0] // LANE

    def body(t_ref, odd_ref, a_ref, b_ref, _, o_ref):
        odd = odd_ref[0] == 1
        o_ref[0:HALF_TILE, :] = jnp.where(odd, a_ref[HALF_TILE:LANE, :], a_ref[0:HALF_TILE, :]).astype(BF16)
        o_ref[HALF_TILE:LANE, :] = jnp.where(odd, b_ref[0:HALF_TILE, :], a_ref[HALF_TILE:LANE, :]).astype(BF16)

    return pl.pallas_call(
        body,
        name=f"place_{g}",
        grid_spec=pltpu.PrefetchScalarGridSpec(
            num_scalar_prefetch=2,
            grid=(nb,),
            in_specs=[pl.BlockSpec((LANE, D), lambda b, t, o: (t[0, b], 0)),
                      pl.BlockSpec((LANE, D), lambda b, t, o: (t[1, b], 0)), ANY],
            out_specs=pl.BlockSpec((LANE, D), lambda b, t, o: (b, 0)),
        ),
        out_shape=jax.ShapeDtypeStruct(GATHERED[g], BF16),
        compiler_params=_params(("parallel",)),
    )(tables, odd_arr, w_t, w_t, after)


def _place_shard(shard, name, after):
    rows, cols = shard.shape
    by_rows = HALF_AXIS[name] == 1

    def body(x_ref, _, o_ref):
        o_ref[...] = x_ref[...].astype(BF16)

    return pl.pallas_call(
        body,
        name=f"place_{name}",
        grid=(N_CHIPS,),
        in_specs=[pl.BlockSpec((rows, cols), lambda b: (0, 0)), ANY],
        out_specs=pl.BlockSpec((rows, cols), (lambda b: (b, 0)) if by_rows else (lambda b: (0, b))),
        out_shape=jax.ShapeDtypeStruct(GATHERED[name], BF16),
        compiler_params=_params(("parallel",)),
    )(shard, after)


def _gather_copy(arr, send_sems, recv_sems, c, jj, i, p, kk):
    rect = _half_rect(arr[p.dst], p, "dst", c)
    return pltpu.make_async_remote_copy(
        src_ref=rect, dst_ref=rect, send_sem=send_sems.at[i * N_CHIPS + kk],
        recv_sem=recv_sems.at[jj * MAX_PIECES + i], device_id=(kk // 2, kk % 2, c), device_id_type=MESH)


def _gather_start(arrays, after):
    stage = tuple(arrays)
    na = len(stage)

    def body(*refs):
        arr = dict(zip(stage, refs[:na]))
        send_sems, recv_sems = refs[na + 1], refs[na + 2]
        token = refs[-1]
        _, _, c, j = _position()
        for jj in range(N_CHIPS):
            @pl.when(j == jj)
            def _():
                for i, p in _stage_pieces(jj, stage):
                    for kk in range(N_CHIPS):
                        if kk != jj:
                            _gather_copy(arr, send_sems, recv_sems, c, jj, i, p, kk).start()
        token[...] = jnp.zeros_like(token)

    outs = pl.pallas_call(
        body,
        name=f"gather_start_{stage[0]}",
        in_specs=[HBM] * na + [ANY],
        out_specs=[SEM, SEM] + [HBM] * na + [pl.BlockSpec(memory_space=pltpu.VMEM)],
        out_shape=[pltpu.SemaphoreType.DMA((N_SEM,)), pltpu.SemaphoreType.DMA((N_SEM,))]
        + [pltpu.HBM(GATHERED[n], BF16) for n in stage] + [jax.ShapeDtypeStruct((8, LANE), F32)],
        input_output_aliases={k: 2 + k for k in range(na)},
        compiler_params=pltpu.CompilerParams(has_side_effects=EFFECT),
    )(*[_in_hbm(arrays[n]) for n in stage], after)
    return outs[0], outs[1], dict(zip(stage, outs[2:2 + na])), outs[-1]


def _gather_wait(send_sems, recv_sems, arrays, after):
    stage = tuple(arrays)
    na = len(stage)

    def body(*refs):
        arr = dict(zip(stage, refs[:na]))
        sems_s, sems_r = refs[na], refs[na + 1]
        _, _, c, j = _position()
        for jj in range(N_CHIPS):
            @pl.when(j != jj)
            def _():
                for i, p in _stage_pieces(jj, stage):
                    _gather_copy(arr, sems_s, sems_r, c, jj, i, p, jj).wait_recv()

            @pl.when(j == jj)
            def _():
                for i, p in _stage_pieces(jj, stage):
                    for kk in range(N_CHIPS):
                        if kk != jj:
                            _gather_copy(arr, sems_s, sems_r, c, jj, i, p, kk).wait_send()

    outs = pl.pallas_call(
        body,
        name=f"gather_wait_{stage[0]}",
        in_specs=[HBM] * na + [SEM, SEM, ANY],
        out_specs=[HBM] * na,
        out_shape=[pltpu.HBM(GATHERED[n], BF16) for n in stage],
        input_output_aliases={k: k for k in range(na)},
        compiler_params=pltpu.CompilerParams(has_side_effects=EFFECT),
    )(*[arrays[n] for n in stage], send_sems, recv_sems, after)
    return dict(zip(stage, outs))


def _gather_swap(arrays):
    stage = tuple(arrays)
    na = len(stage)

    def body(*refs):
        dst = dict(zip(stage, refs[na:2 * na]))
        send_sems, recv_sems = refs[2 * na:]
        x, y, c, j = _position()

        def fwd(jj, i, p, which):
            rect = _half_rect(dst[p.dst], p, "dst", which)
            return pltpu.make_async_remote_copy(
                src_ref=rect, dst_ref=rect, send_sem=send_sems.at[jj * MAX_PIECES + i],
                recv_sem=recv_sems.at[jj * MAX_PIECES + i], device_id=(x, y, 1 - c), device_id_type=MESH)

        for jj in range(N_CHIPS):
            @pl.when(j != jj)
            def _():
                for i, p in _stage_pieces(jj, stage):
                    fwd(jj, i, p, c).start()
        for jj in range(N_CHIPS):
            @pl.when(j != jj)
            def _():
                for i, p in _stage_pieces(jj, stage):
                    fwd(jj, i, p, 1 - c).wait_recv()
        for jj in range(N_CHIPS):
            @pl.when(j != jj)
            def _():
                for i, p in _stage_pieces(jj, stage):
                    fwd(jj, i, p, c).wait_send()

    outs = pl.pallas_call(
        body,
        name=f"gather_swap_{stage[0]}",
        in_specs=[ANY] * na,
        out_specs=[ANY] * na,
        out_shape=[jax.ShapeDtypeStruct(GATHERED[n], BF16) for n in stage],
        input_output_aliases={k: k for k in range(na)},
        scratch_shapes=[pltpu.SemaphoreType.DMA((N_SEM,)), pltpu.SemaphoreType.DMA((N_SEM,))],
        compiler_params=pltpu.CompilerParams(has_side_effects=True),
    )(*[arrays[n] for n in stage])
    return dict(zip(stage, outs))


def _pass_on_copy(arr, send_sems, recv_sems, x, y, c, jj, i, p, which):
    rect = _half_rect(arr[p.dst], p, "dst", which)
    return pltpu.make_async_remote_copy(
        src_ref=rect, dst_ref=rect, send_sem=send_sems.at[jj * MAX_PIECES + i],
        recv_sem=recv_sems.at[jj * MAX_PIECES + i], device_id=(x, y, 1 - c), device_id_type=MESH)


def _gather_pass_start(arrays, after):
    stage = tuple(arrays)
    na = len(stage)

    def body(*refs):
        arr = dict(zip(stage, refs[:na]))
        x, y, c, j = _position()
        for jj in range(N_CHIPS):
            @pl.when(j != jj)
            def _():
                for i, p in _stage_pieces(jj, stage):
                    _pass_on_copy(arr, refs[na + 1], refs[na + 2], x, y, c, jj, i, p, c).start()
        refs[-1][...] = jnp.zeros_like(refs[-1])

    outs = pl.pallas_call(
        body,
        name=f"gather_pass_start_{stage[0]}",
        in_specs=[HBM] * na + [ANY],
        out_specs=[SEM, SEM] + [HBM] * na + [pl.BlockSpec(memory_space=pltpu.VMEM)],
        out_shape=[pltpu.SemaphoreType.DMA((N_SEM,)), pltpu.SemaphoreType.DMA((N_SEM,))]
        + [pltpu.HBM(GATHERED[n], BF16) for n in stage] + [jax.ShapeDtypeStruct((8, LANE), F32)],
        input_output_aliases={k: 2 + k for k in range(na)},
        compiler_params=pltpu.CompilerParams(has_side_effects=EFFECT),
    )(*[arrays[n] for n in stage], after)
    return outs[0], outs[1], dict(zip(stage, outs[2:2 + na])), outs[-1]


def _gather_pass_wait(send_sems, recv_sems, arrays, after):
    stage = tuple(arrays)
    na = len(stage)

    def body(*refs):
        arr = dict(zip(stage, refs[:na]))
        x, y, c, j = _position()
        for jj in range(N_CHIPS):
            @pl.when(j != jj)
            def _():
                for i, p in _stage_pieces(jj, stage):
                    _pass_on_copy(arr, refs[na], refs[na + 1], x, y, c, jj, i, p, 1 - c).wait_recv()
                    _pass_on_copy(arr, refs[na], refs[na + 1], x, y, c, jj, i, p, c).wait_send()

    outs = pl.pallas_call(
        body,
        name=f"gather_pass_wait_{stage[0]}",
        in_specs=[HBM] * na + [SEM, SEM, ANY],
        out_specs=[HBM] * na,
        out_shape=[pltpu.HBM(GATHERED[n], BF16) for n in stage],
        input_output_aliases={k: k for k in range(na)},
        compiler_params=pltpu.CompilerParams(has_side_effects=EFFECT),
    )(*[arrays[n] for n in stage], send_sems, recv_sems, after)
    return dict(zip(stage, outs))


def _own_half(ref, shape, axis, which):
    if axis == 1:
        return ref.at[:, pl.ds(which * (shape[1] // 2), shape[1] // 2)]
    return ref.at[pl.ds(which * (shape[0] // 2), shape[0] // 2), :]


def _swap_copies(names, src, dst, send_sems, recv_sems):
    x, y, c, _ = _position()
    return [pltpu.make_async_remote_copy(
        src_ref=_own_half(src[n], GATHERED[n], HALF_AXIS[n], 1 - c), dst_ref=dst[n],
        send_sem=send_sems.at[k], recv_sem=recv_sems.at[k],
        device_id=(x, y, 1 - c), device_id_type=MESH) for k, n in enumerate(names)]


def _swap_start(grads, after):
    names = tuple(grads)
    n = len(names)

    def body(*refs):
        src, dst = dict(zip(names, refs[:n])), dict(zip(names, refs[n:2 * n]))
        for cp in _swap_copies(names, src, dst, refs[2 * n + 1], refs[2 * n + 2]):
            cp.start()
        refs[-1][...] = jnp.zeros_like(refs[-1])

    half_shape = lambda nm: _halved(GATHERED[nm], HALF_AXIS[nm])
    args = [_in_hbm(grads[nm]) for nm in names] + [_in_hbm(lax.empty(half_shape(nm), BF16)) for nm in names]
    if after is None:
        after = jnp.zeros((8, LANE), F32)
    outs = pl.pallas_call(
        body,
        name=f"swap_start_{names[0]}",
        in_specs=[HBM] * (2 * n) + [ANY],
        out_specs=[SEM, SEM] + [HBM] * (2 * n) + [pl.BlockSpec(memory_space=pltpu.VMEM)],
        out_shape=[pltpu.SemaphoreType.DMA((n,)), pltpu.SemaphoreType.DMA((n,))]
        + [pltpu.HBM(GATHERED[nm], BF16) for nm in names] + [pltpu.HBM(half_shape(nm), BF16) for nm in names]
        + [jax.ShapeDtypeStruct((8, LANE), F32)],
        input_output_aliases={k: 2 + k for k in range(2 * n)},
        compiler_params=pltpu.CompilerParams(has_side_effects=EFFECT),
    )(*args, after)
    return outs[0], outs[1], dict(zip(names, outs[2:2 + n])), dict(zip(names, outs[2 + n:2 + 2 * n])), outs[-1]


def _swap_wait(send_sems, recv_sems, grads, landing, after):
    names = tuple(grads)
    n = len(names)

    def body(*refs):
        src, dst = dict(zip(names, refs[:n])), dict(zip(names, refs[n:2 * n]))
        copies = _swap_copies(names, src, dst, refs[2 * n], refs[2 * n + 1])
        for cp in copies:
            cp.wait_recv()
        for cp in copies:
            cp.wait_send()

    half_shape = lambda nm: _halved(GATHERED[nm], HALF_AXIS[nm])
    outs = pl.pallas_call(
        body,
        name=f"swap_wait_{names[0]}",
        in_specs=[HBM] * (2 * n) + [SEM, SEM, ANY],
        out_specs=[HBM] * (2 * n),
        out_shape=[pltpu.HBM(GATHERED[nm], BF16) for nm in names] + [pltpu.HBM(half_shape(nm), BF16) for nm in names],
        input_output_aliases={k: k for k in range(2 * n)},
        compiler_params=pltpu.CompilerParams(has_side_effects=EFFECT),
    )(*[grads[nm] for nm in names], *[landing[nm] for nm in names], send_sems, recv_sems, after)
    return dict(zip(names, outs[:n])), dict(zip(names, outs[n:]))


ADD_ROWS = {"A": 1024, "B": 768, "C": 1024, "D": 1536, "mk": 1024, "br0": 512, "br1": 512, "br2": 512, "out": 1024}


def _add_half(full, recv, c_arr, name):
    rows, cols = recv.shape
    tr = ADD_ROWS[name]
    if HALF_AXIS[name] == 1:
        index = lambda i, c_ref: (i, c_ref[0])
    else:
        nb = rows // tr
        index = lambda i, c_ref: (nb * c_ref[0] + i, 0)

    def body(c_ref, a_ref, b_ref, o_ref):
        o_ref[...] = (a_ref[...].astype(F32) + b_ref[...].astype(F32)).astype(BF16)

    return pl.pallas_call(
        body,
        name=f"add_half_{name}",
        grid_spec=pltpu.PrefetchScalarGridSpec(
            num_scalar_prefetch=1,
            grid=(rows // tr,),
            in_specs=[pl.BlockSpec((tr, cols), index), pl.BlockSpec((tr, cols), lambda i, c_ref: (i, 0))],
            out_specs=pl.BlockSpec((tr, cols), lambda i, c_ref: (i, 0)),
        ),
        out_shape=jax.ShapeDtypeStruct((rows, cols), BF16),
        compiler_params=_params(("parallel",)),
    )(c_arr, full, recv)


SLOT_SHAPES = {n: _halved(SHARD_SHAPES[n], HALF_AXIS[n]) for n in SHARDS}


def _slot_shape(n):
    return (N_CHIPS,) + SLOT_SHAPES[n]


def _stage_shards(stage):
    pieces = [p for jj in range(N_CHIPS) for p in _pieces_of(jj)]
    return tuple(s for s in SHARDS if any(p.src == s and p.dst in stage for p in pieces))


def _scatter_copy(src, dst, send_sems, recv_sems, c, jj, kk, i, p):
    return pltpu.make_async_remote_copy(
        src_ref=_rect_in_half(src[p.dst], p, "dst"), dst_ref=_rect_in_half(dst[p.src].at[jj], p, "src"),
        send_sem=send_sems.at[kk * MAX_PIECES + i], recv_sem=recv_sems.at[jj * MAX_PIECES + i],
        device_id=(kk // 2, kk % 2, c), device_id_type=MESH)


def _scatter_start(halves, slots):
    stage, touched = tuple(halves), tuple(slots)
    nh, nt = len(stage), len(touched)

    def body(*refs):
        src = dict(zip(stage, refs[:nh]))
        dst = dict(zip(touched, refs[nh:nh + nt]))
        send_sems, recv_sems = refs[nh + nt], refs[nh + nt + 1]
        token = refs[-1]
        _, _, c, j = _position()
        for jj in range(N_CHIPS):
            @pl.when(j == jj)
            def _():
                for kk in range(N_CHIPS):
                    if kk != jj:
                        for i, p in _stage_pieces(kk, stage):
                            _scatter_copy(src, dst, send_sems, recv_sems, c, jj, kk, i, p).start()
        token[...] = jnp.zeros_like(token)

    outs = pl.pallas_call(
        body,
        name=f"scatter_start_{stage[0]}",
        in_specs=[HBM] * (nh + nt),
        out_specs=[SEM, SEM] + [HBM] * (nh + nt) + [pl.BlockSpec(memory_space=pltpu.VMEM)],
        out_shape=[pltpu.SemaphoreType.DMA((N_SEM,)), pltpu.SemaphoreType.DMA((N_SEM,))]
        + [pltpu.HBM(halves[n].shape, BF16) for n in stage] + [pltpu.HBM(_slot_shape(s), BF16) for s in touched]
        + [jax.ShapeDtypeStruct((8, LANE), F32)],
        input_output_aliases={k: 2 + k for k in range(nh + nt)},
        compiler_params=pltpu.CompilerParams(has_side_effects=EFFECT),
    )(*[_in_hbm(halves[n]) for n in stage], *[_in_hbm(slots[s]) for s in touched])
    return outs[0], outs[1], dict(zip(stage, outs[2:2 + nh])), dict(zip(touched, outs[2 + nh:2 + nh + nt])), outs[-1]


def _scatter_wait(send_sems, recv_sems, halves, slots, after):
    stage, touched = tuple(halves), tuple(slots)
    nh, nt = len(stage), len(touched)

    def body(*refs):
        src = dict(zip(stage, refs[:nh]))
        dst = dict(zip(touched, refs[nh:nh + nt]))
        sems_s, sems_r = refs[nh + nt], refs[nh + nt + 1]
        _, _, c, j = _position()
        for jj in range(N_CHIPS):
            @pl.when(j == jj)
            def _():
                for ss in range(N_CHIPS):
                    if ss != jj:
                        for i, p in _stage_pieces(jj, stage):
                            _scatter_copy(src, dst, sems_s, sems_r, c, ss, jj, i, p).wait_recv()
                for kk in range(N_CHIPS):
                    if kk != jj:
                        for i, p in _stage_pieces(kk, stage):
                            _scatter_copy(src, dst, sems_s, sems_r, c, jj, kk, i, p).wait_send()

    outs = pl.pallas_call(
        body,
        name=f"scatter_wait_{stage[0]}",
        in_specs=[HBM] * (nh + nt) + [SEM, SEM, ANY],
        out_specs=[HBM] * (nh + nt),
        out_shape=[pltpu.HBM(halves[n].shape, BF16) for n in stage] + [pltpu.HBM(_slot_shape(s), BF16) for s in touched],
        input_output_aliases={k: k for k in range(nh + nt)},
        compiler_params=pltpu.CompilerParams(has_side_effects=EFFECT),
    )(*[halves[n] for n in stage], *[slots[s] for s in touched], send_sems, recv_sems, after)
    return dict(zip(stage, outs[:nh])), dict(zip(touched, outs[nh:]))


SUM_ROWS = {"mk": 512, "br0": 512, "br1": 512, "br2": 512, "out": 512}


def _sum_in_chip_order(chip, own, s_ref):
    acc = None
    for k in range(N_CHIPS):
        term = jnp.where(chip == k, own, s_ref[k].astype(F32))
        acc = term if acc is None else acc + term
    return acc


def _sum_slots(slots, own_half, pos_arr, name):
    _, rows, cols = slots.shape
    tr = SUM_ROWS[name]
    nb = rows // tr
    if HALF_AXIS[name] == 1:
        own_index = lambda i, pos: (nb * pos[1] + i, 0)
        out_index = lambda i, pos: (i, pos[0])
    else:
        own_index = lambda i, pos: (i, pos[1])
        out_index = lambda i, pos: (nb * pos[0] + i, 0)

    def body(pos, s_ref, own_ref, o_ref):
        o_ref[...] = _sum_in_chip_order(pos[1], own_ref[...].astype(F32), s_ref)

    return pl.pallas_call(
        body,
        name=f"sum_slots_{name}",
        grid_spec=pltpu.PrefetchScalarGridSpec(
            num_scalar_prefetch=1,
            grid=(nb,),
            in_specs=[pl.BlockSpec((N_CHIPS, tr, cols), lambda i, pos: (0, i, 0)),
                      pl.BlockSpec((tr, cols), own_index)],
            out_specs=pl.BlockSpec((tr, cols), out_index),
        ),
        out_shape=jax.ShapeDtypeStruct(SHARD_SHAPES[name], F32),
        compiler_params=_params(("parallel",)),
    )(pos_arr, slots, own_half)


def _own_partial_tables():
    import numpy as np
    nb = SHARD // HALF_TILE
    grp, blk = np.zeros((N_CHIPS, nb), np.int32), np.zeros((N_CHIPS, nb), np.int32)
    for jj in range(N_CHIPS):
        for r, n, g, gr in _shard_runs(jj):
            for k in range(n // HALF_TILE):
                grp[jj, r // HALF_TILE + k] = GROUPS.index(g)
                blk[jj, r // HALF_TILE + k] = gr // HALF_TILE + k
    return grp, blk


SUM_READS = 6
SUM_WRITES = 3


def _sum_slots_win(slots, own_halves, pos_arr, grp_tbl, blk_tbl):
    nb = SHARD // HALF_TILE
    cols = D // 2

    def body(pos, grp, blk, s_hbm, a_hbm, b_hbm, c_hbm, d_hbm, o_hbm, sbuf, ownbuf, obuf, ssem, ownsem, osem):
        groups = (a_hbm, b_hbm, c_hbm, d_hbm)

        def rows_of(b):
            start = b * HALF_TILE
            return pl.ds(start if isinstance(b, int) else pl.multiple_of(start, HALF_TILE), HALF_TILE)

        def slots_read(b, slot):
            return pltpu.make_async_copy(s_hbm.at[:, rows_of(b)], sbuf.at[slot], ssem.at[slot])

        def own_read(b, slot, act):
            for gi, ref in enumerate(groups):
                @pl.when(grp[b] == gi)
                def _():
                    act(pltpu.make_async_copy(ref.at[rows_of(blk[b])], ownbuf.at[slot], ownsem.at[slot]))

        def write(b, slot):
            half = pl.ds(pl.multiple_of(pos[0] * cols, cols), cols)
            return pltpu.make_async_copy(obuf.at[slot], o_hbm.at[rows_of(b), half], osem.at[slot])

        for b in range(SUM_READS):
            slots_read(b, b).start()
            own_read(b, b, lambda copy: copy.start())

        def step(b, carry):
            slot = b % SUM_READS
            oslot = b % SUM_WRITES
            slots_read(b, slot).wait()
            own_read(b, slot, lambda copy: copy.wait())

            @pl.when(b >= SUM_WRITES)
            def _():
                write(b - SUM_WRITES, oslot).wait()

            obuf[oslot] = _sum_in_chip_order(pos[1], ownbuf[slot].astype(F32), sbuf.at[slot])
            write(b, oslot).start()

            @pl.when(b + SUM_READS < nb)
            def _():
                slots_read(b + SUM_READS, slot).start()
                own_read(b + SUM_READS, slot, lambda copy: copy.start())
            return carry

        lax.fori_loop(0, nb, step, 0)
        for b in range(nb - SUM_WRITES, nb):
            write(b, b % SUM_WRITES).wait()

    smem = pl.BlockSpec(memory_space=pltpu.SMEM)
    return pl.pallas_call(
        body,
        name="sum_slots_win",
        in_specs=[smem] * 3 + [ANY] * (1 + len(GROUPS)),
        out_specs=ANY,
        out_shape=jax.ShapeDtypeStruct(SHARD_SHAPES["win"], F32),
        scratch_shapes=[pltpu.VMEM((SUM_READS, N_CHIPS, HALF_TILE, cols), BF16), pltpu.VMEM((SUM_READS, HALF_TILE, cols), BF16),
                        pltpu.VMEM((SUM_WRITES, HALF_TILE, cols), F32), pltpu.SemaphoreType.DMA((SUM_READS,)),
                        pltpu.SemaphoreType.DMA((SUM_READS,)), pltpu.SemaphoreType.DMA((SUM_WRITES,))],
        compiler_params=pltpu.CompilerParams(vmem_limit_bytes=VMEM_LIMIT),
    )(pos_arr, grp_tbl, blk_tbl, slots, *[own_halves[g] for g in GROUPS])


def _share_copy(buf, name, send_sems, recv_sems, k, which):
    x, y, c, _ = _position()
    half = _own_half(buf, SHARD_SHAPES[name], HALF_AXIS[name], which)
    return pltpu.make_async_remote_copy(src_ref=half, dst_ref=half, send_sem=send_sems.at[k], recv_sem=recv_sems.at[k],
                                        device_id=(x, y, 1 - c), device_id_type=MESH)


def _share_start(sums, after):
    names = tuple(sums)
    n = len(names)

    def body(*refs):
        _, _, c, _ = _position()
        for k, nm in enumerate(names):
            _share_copy(refs[k], nm, refs[n + 1], refs[n + 2], k, c).start()
        refs[-1][...] = jnp.zeros_like(refs[-1])

    outs = pl.pallas_call(
        body,
        name=f"share_start_{names[0]}",
        in_specs=[HBM] * n + [ANY],
        out_specs=[SEM, SEM] + [HBM] * n + [pl.BlockSpec(memory_space=pltpu.VMEM)],
        out_shape=[pltpu.SemaphoreType.DMA((n,)), pltpu.SemaphoreType.DMA((n,))]
        + [pltpu.HBM(SHARD_SHAPES[nm], F32) for nm in names] + [jax.ShapeDtypeStruct((8, LANE), F32)],
        input_output_aliases={k: 2 + k for k in range(n)},
        compiler_params=pltpu.CompilerParams(has_side_effects=EFFECT),
    )(*[_in_hbm(sums[nm]) for nm in names], after)
    return outs[0], outs[1], dict(zip(names, outs[2:2 + n])), outs[-1]


def _share_wait(send_sems, recv_sems, sums, after):
    names = tuple(sums)
    n = len(names)

    def body(*refs):
        _, _, c, _ = _position()
        for k, nm in enumerate(names):
            _share_copy(refs[k], nm, refs[n], refs[n + 1], k, 1 - c).wait_recv()
            _share_copy(refs[k], nm, refs[n], refs[n + 1], k, c).wait_send()

    outs = pl.pallas_call(
        body,
        name=f"share_wait_{names[0]}",
        in_specs=[HBM] * n + [SEM, SEM, ANY],
        out_specs=[HBM] * n,
        out_shape=[pltpu.HBM(SHARD_SHAPES[nm], F32) for nm in names],
        input_output_aliases={k: k for k in range(n)},
        compiler_params=pltpu.CompilerParams(has_side_effects=EFFECT),
    )(*[sums[nm] for nm in names], send_sems, recv_sems, after)
    return dict(zip(names, outs))


def _all_reduce_small(pack, name):
    rows = pack.shape[0]
    half = rows // 2

    def body(p_ref, o_ref, sib, land, sems):
        x, y, c, j = _position()
        sibling = (x, y, 1 - c)
        swap = pltpu.make_async_remote_copy(src_ref=p_ref, dst_ref=sib, send_sem=sems.at[0], recv_sem=sems.at[1],
                                            device_id=sibling, device_id_type=MESH)
        swap.start()
        swap.wait_recv()
        land[j] = p_ref[...] + sib[...]

        def mine(k, which):
            return land.at[k, pl.ds(which * half, half)]

        def ici(kk):
            return pltpu.make_async_remote_copy(
                src_ref=mine(j, c), dst_ref=mine(j, c), send_sem=sems.at[2 + kk], recv_sem=sems.at[6 + j],
                device_id=(kk // 2, kk % 2, c), device_id_type=MESH)

        def arrival(kk):
            return pltpu.make_async_remote_copy(
                src_ref=mine(kk, c), dst_ref=mine(kk, c), send_sem=sems.at[2 + kk], recv_sem=sems.at[6 + kk],
                device_id=(kk // 2, kk % 2, c), device_id_type=MESH)

        def passed_on(kk, which):
            return pltpu.make_async_remote_copy(
                src_ref=mine(kk, which), dst_ref=mine(kk, which), send_sem=sems.at[10 + kk],
                recv_sem=sems.at[14 + kk], device_id=sibling, device_id_type=MESH)

        for kk in range(N_CHIPS):
            @pl.when(j != kk)
            def _():
                ici(kk).start()
        for kk in range(N_CHIPS):
            @pl.when(j != kk)
            def _():
                arrival(kk).wait_recv()
                passed_on(kk, c).start()
        for kk in range(N_CHIPS):
            @pl.when(j != kk)
            def _():
                passed_on(kk, 1 - c).wait_recv()
        acc = land[0]
        for kk in range(1, N_CHIPS):
            acc = acc + land[kk]
        o_ref[...] = acc
        swap.wait_send()
        for kk in range(N_CHIPS):
            @pl.when(j != kk)
            def _():
                ici(kk).wait_send()
                passed_on(kk, c).wait_send()

    vmem = pl.BlockSpec(memory_space=pltpu.VMEM)
    return pl.pallas_call(
        body,
        name=name,
        in_specs=[vmem],
        out_specs=vmem,
        out_shape=jax.ShapeDtypeStruct((rows, LANE), F32),
        scratch_shapes=[pltpu.VMEM((rows, LANE), F32), pltpu.VMEM((N_CHIPS, rows, LANE), F32),
                        pltpu.SemaphoreType.DMA((18,))],
        compiler_params=pltpu.CompilerParams(has_side_effects=True, vmem_limit_bytes=VMEM_LIMIT),
    )(pack)


ADAM_ROWS = {"mk": 256, "br0": 512, "br1": 512, "br2": 512, "out": 256}


def _adamw(w, g, m, v, name, tr):
    rows, cols = w.shape
    tr = min(tr, rows)

    def body(w_ref, g_ref, m_ref, v_ref, go_ref, d_ref, nm_ref, nv_ref):
        gv = g_ref[...]
        go_ref[...] = gv
        nm = ADAM_B1 * m_ref[...] + (1.0 - ADAM_B1) * gv
        nv = ADAM_B2 * v_ref[...] + (1.0 - ADAM_B2) * (gv * gv)
        nm_ref[...] = nm
        nv_ref[...] = nv
        m_hat = nm / (1.0 - ADAM_B1 ** ADAM_STEP)
        v_hat = nv / (1.0 - ADAM_B2 ** ADAM_STEP)
        d_ref[...] = -ADAM_LR * (m_hat / (jnp.sqrt(v_hat) + ADAM_EPS) + ADAM_WD * w_ref[...])

    blk = pl.BlockSpec((tr, cols), lambda i: (i, 0))
    shape = jax.ShapeDtypeStruct((rows, cols), F32)
    return pl.pallas_call(
        body,
        name=f"adamw_{name}",
        grid=(rows // tr,),
        in_specs=[blk] * 4,
        out_specs=[blk] * 4,
        out_shape=[shape] * 4,
        compiler_params=_params(("parallel",)),
    )(w, g, m, v)


RING_ROWS = 112
RING_READS = 3
RING_WRITES = 2


def _adamw_ring(w, g, m, v, name):
    rows, cols = w.shape
    n = rows // RING_ROWS
    assert n * RING_ROWS == rows and n >= RING_READS

    def body(w_hbm, g_hbm, m_hbm, v_hbm, go_hbm, d_hbm, nm_hbm, nv_hbm, ibuf, obuf, isem, osem):
        ins = (w_hbm, g_hbm, m_hbm, v_hbm)
        outs = (go_hbm, d_hbm, nm_hbm, nv_hbm)

        def read(k, i, slot):
            return pltpu.make_async_copy(ins[k].at[pl.ds(i * RING_ROWS, RING_ROWS)], ibuf.at[k, slot], isem.at[k, slot])

        def write(k, i, slot):
            return pltpu.make_async_copy(obuf.at[k, slot], outs[k].at[pl.ds(i * RING_ROWS, RING_ROWS)], osem.at[k, slot])

        for i in range(RING_READS):
            for k in range(4):
                read(k, i, i).start()

        def step(i, carry):
            slot = i % RING_READS
            oslot = i % RING_WRITES
            for k in range(4):
                read(k, i, slot).wait()

            @pl.when(i >= RING_WRITES)
            def _():
                for k in range(4):
                    write(k, i - RING_WRITES, oslot).wait()

            gv = ibuf[1, slot]
            obuf[0, oslot] = gv
            nm = ADAM_B1 * ibuf[2, slot] + (1.0 - ADAM_B1) * gv
            nv = ADAM_B2 * ibuf[3, slot] + (1.0 - ADAM_B2) * (gv * gv)
            obuf[2, oslot] = nm
            obuf[3, oslot] = nv
            m_hat = nm / (1.0 - ADAM_B1 ** ADAM_STEP)
            v_hat = nv / (1.0 - ADAM_B2 ** ADAM_STEP)
            obuf[1, oslot] = -ADAM_LR * (m_hat / (jnp.sqrt(v_hat) + ADAM_EPS) + ADAM_WD * ibuf[0, slot])
            for k in range(4):
                write(k, i, oslot).start()

            @pl.when(i + RING_READS < n)
            def _():
                for k in range(4):
                    read(k, i + RING_READS, slot).start()
            return carry

        lax.fori_loop(0, n, step, 0)
        for i in range(n - RING_WRITES, n):
            for k in range(4):
                write(k, i, i % RING_WRITES).wait()

    any_space = pl.BlockSpec(memory_space=pl.ANY)
    shape = jax.ShapeDtypeStruct((rows, cols), F32)
    return pl.pallas_call(
        body,
        name=f"adamw_{name}",
        in_specs=[any_space] * 4,
        out_specs=[any_space] * 4,
        out_shape=[shape] * 4,
        scratch_shapes=[pltpu.VMEM((4, RING_READS, RING_ROWS, cols), F32), pltpu.VMEM((4, RING_WRITES, RING_ROWS, cols), F32),
                        pltpu.SemaphoreType.DMA((4, RING_READS)), pltpu.SemaphoreType.DMA((4, RING_WRITES))],
        compiler_params=pltpu.CompilerParams(vmem_limit_bytes=VMEM_LIMIT),
    )(w, g, m, v)


SMALL = (("pre_norm_g", (1, D)), ("post_norm_g", (1, D)), ("mem_norm_g", (1, D)), ("conv_w", (CONV_W, D_RNN)),
         ("conv_b", (1, D_RNN)), ("w_rg_a", (RNN_BLOCKS, LANE, LANE)), ("b_rg_a", (1, D_RNN)),
         ("w_rg_x", (RNN_BLOCKS, LANE, LANE)), ("b_rg_x", (1, D_RNN)), ("lru_lambda", (1, D_RNN)),
         ("swa_sinks", (1, SWA_HEADS)), ("rel_bias", (REL_BUCKETS, SWA_HEADS)))
PACK_ROWS = 2176


def _slot_len(shape):
    return -(-math.prod(shape) // LANE) * LANE


def _pack(values, last_row=None):
    parts = []
    for name, shape in SMALL:
        flat = values[name].reshape(-1).astype(F32)
        parts.append(jnp.pad(flat, (0, _slot_len(shape) - flat.shape[0])))
    flat = jnp.concatenate(parts)
    tail = jnp.zeros((LANE,), F32) if last_row is None else last_row
    return jnp.concatenate([jnp.pad(flat, (0, (PACK_ROWS - 1) * LANE - flat.shape[0])), tail]).reshape(PACK_ROWS, LANE)


def _unpack(pack):
    flat = pack.reshape(-1)
    out, off = {}, 0
    for name, shape in SMALL:
        out[name] = flat[off:off + math.prod(shape)].reshape(shape)
        off += _slot_len(shape)
    return out


TWIN_WEIGHTS = ("pre_norm_g", "post_norm_g", "mem_norm_g", "w_in", "conv_w", "conv_b", "w_rg_a", "b_rg_a", "w_rg_x",
                "b_rg_x", "lru_lambda", "swa_sinks", "rel_bias", "w_mem_kv", "w_br_rg", "w_br_swa", "w_br_mem", "w_out")
BIG = {"w_in": "win", "w_mem_kv": "mk", "w_br_rg": "br0", "w_br_swa": "br1", "w_br_mem": "br2", "w_out": "out"}


def kernel(x, mem, pre_norm_g, post_norm_g, mem_norm_g, w_in, conv_w, conv_b, w_rg_a, b_rg_a, w_rg_x, b_rg_x, lru_lambda, swa_sinks, rel_bias, w_mem_kv, w_br_rg, w_br_swa, w_br_mem, w_out, loss_target, m_pre_norm_g, m_post_norm_g, m_mem_norm_g, m_w_in, m_conv_w, m_conv_b, m_w_rg_a, m_b_rg_a, m_w_rg_x, m_b_rg_x, m_lru_lambda, m_swa_sinks, m_rel_bias, m_w_mem_kv, m_w_br_rg, m_w_br_swa, m_w_br_mem, m_w_out, v_pre_norm_g, v_post_norm_g, v_mem_norm_g, v_w_in, v_conv_w, v_conv_b, v_w_rg_a, v_b_rg_a, v_w_rg_x, v_b_rg_x, v_lru_lambda, v_swa_sinks, v_rel_bias, v_w_mem_kv, v_w_br_rg, v_w_br_swa, v_w_br_mem, v_w_out):
    args = dict(locals())
    out_shapes = {n: args[n].shape for n in TWIN_WEIGHTS}
    w = {n: (args[n] if n == "rel_bias" else args[n][0]) for n in TWIN_WEIGHTS}
    m = {n: (args["m_" + n] if n == "rel_bias" else args["m_" + n][0]) for n in TWIN_WEIGHTS}
    v = {n: (args["v_" + n] if n == "rel_bias" else args["v_" + n][0]) for n in TWIN_WEIGHTS}
    for d in (w, m, v):
        for n, shape in SMALL:
            if n != "conv_w":
                d[n] = d[n].reshape(shape)

    xi, yi, ci = lax.axis_index("x"), lax.axis_index("y"), lax.axis_index("c")
    chip = 2 * xi + yi
    c_arr = ci.astype(jnp.int32).reshape(1)
    zero = jnp.zeros((), jnp.int32)
    cw0 = (chip * (D_RNN // N_CHIPS)).astype(jnp.int32)

    placed = lax.dynamic_update_slice(jnp.zeros((CONV_W, D_RNN), F32), w["conv_w"], (zero, cw0))
    placed = jnp.where(ci == 0, placed, 0.0).reshape(CONV_W * D_RNN // LANE, LANE)
    conv_w_full = _all_reduce_small(placed, "gather_conv_w").reshape(CONV_W, D_RNN)

    for d in (w, m, v):
        d["w_in"] = d["w_in"].T
    chip_row = lambda tbl: lax.dynamic_slice(jnp.asarray(tbl), (chip.astype(jnp.int32), zero), (1, tbl.shape[1]))[0]
    chip_tables = lambda tbl: lax.dynamic_slice(jnp.asarray(tbl), (chip.astype(jnp.int32), zero, zero),
                                                (1,) + tbl.shape[1:])[0]
    odd_arr = yi.astype(jnp.int32).reshape(1)
    big_of = {s: n for n, s in BIG.items()}
    ag, token = {}, conv_w_full
    for stage in GATHER_STAGES:
        behind = c_arr if stage == GATHER_STAGES[0] else token
        placed = {n: (_place_group(w["w_in"], n, chip_tables(_own_block_table(n)), odd_arr, behind) if n in GROUPS
                      else _place_shard(w[big_of[n]], n, behind)) for n in stage}
        send, recv, in_flight, token = _gather_start(placed, token)
        ag[stage] = (send, recv, in_flight)
    h = _rms_fwd(x[0], w["pre_norm_g"], "rms_pre", token)

    def conv_w_in_place(d):
        return dict(d, conv_w=lax.dynamic_update_slice(jnp.zeros((CONV_W, D_RNN), F32), d["conv_w"], (zero, cw0)))

    small_packs = [_pack(conv_w_in_place(d)) for d in (w, m, v)]

    passing = {}

    def prefetch(names, after):
        send, recv, in_flight = ag[names]
        *passing[names], token = _gather_pass_start(_gather_wait(send, recv, in_flight, after), after)
        return token

    def fetch(names, after):
        if names in passing:
            ready = _gather_pass_wait(*passing.pop(names), after)
        else:
            send, recv, in_flight = ag[names]
            ready = _gather_swap(_gather_wait(send, recv, in_flight, after))
        return tuple(ready[n] for n in names)

    rs = {"slots": {}, "halves": {}, "pending": [], "swap": None}

    def emit(grads, after=None):
        assert rs["swap"] is None
        *rs["swap"], token = _swap_start(grads, after)
        return token

    def advance(after):
        grads, received = _swap_wait(*rs["swap"], after)
        rs["swap"] = None
        halves = {n: _add_half(grads[n], received[n], c_arr, n) for n in grads}
        landing = {s: rs["slots"][s] if s in rs["slots"] else lax.empty(_slot_shape(s), BF16)
                   for s in _stage_shards(tuple(grads))}
        send, recv, halves, landing, token = _scatter_start(halves, landing)
        rs["slots"].update(landing)
        rs["pending"].append((send, recv, halves, tuple(landing)))
        return token

    sp = {n: w[n] for n, _ in SMALL}
    sp["conv_w"] = conv_w_full
    sq, grad_x, d_small = _local_step(x[0], h, mem[0], loss_target[0], sp, small_packs, fetch, prefetch, emit, advance)
    small_total = _all_reduce_small(_pack(d_small, sq[0]), "all_reduce_small")
    loss = small_total[PACK_ROWS - 1, 0] * (0.5 / D)

    for send, recv, halves, touched in rs["pending"]:
        halves, landed = _scatter_wait(send, recv, halves, {s: rs["slots"][s] for s in touched}, small_total)
        rs["slots"].update(landed)
        rs["halves"].update(halves)
    pos_arr = jnp.stack([ci, chip]).astype(jnp.int32)
    grp_tbl, blk_tbl = (chip_row(t) for t in _own_partial_tables())
    rest = {s: _sum_slots(rs["slots"][s], rs["halves"][s], pos_arr, s) for s in SHARDS if s != "win"}
    *rest_share, tok = _share_start(rest, small_total)
    win_sum = _sum_slots_win(rs["slots"]["win"], rs["halves"], pos_arr, grp_tbl, blk_tbl)
    *win_share, tok = _share_start({"win": win_sum}, tok)
    sums = _share_wait(*rest_share, tok)

    grad, delta, new_m, new_v = {}, {}, {}, {}
    for n, s in BIG.items():
        if n == "w_in":
            continue
        grad[n], delta[n], new_m[n], new_v[n] = _adamw(w[n], sums[s], m[n], v[n], s, ADAM_ROWS[s])
    g_win = _share_wait(*win_share, delta["w_out"])["win"]
    n = "w_in"
    grad[n], delta[n], new_m[n], new_v[n] = _adamw_ring(w[n], g_win, m[n], v[n], "win")
    for group in (grad, delta, new_m, new_v):
        group["w_in"] = group["w_in"].T
    _, d_, m_, v_ = _adamw(small_packs[0], small_total, small_packs[1], small_packs[2], "small", PACK_ROWS)
    for group, pack in ((grad, small_total), (delta, d_), (new_m, m_), (new_v, v_)):
        group.update(_unpack(pack))
    for group in (grad, delta, new_m, new_v):
        group["conv_w"] = lax.dynamic_slice(group["conv_w"], (zero, cw0), (CONV_W, D_RNN // N_CHIPS))

    outs = [loss, grad_x.reshape(1, S, D)]
    for group in (grad, delta, new_m, new_v):
        outs += [group[n].reshape(out_shapes[n]) for n in TWIN_WEIGHTS]
    return tuple(outs)
```
